```python
import jax
import jax.numpy as jnp
from jax import lax
import numpy as np


D_MODEL = 1024
BATCH = 8
SEQ = 8192
DEPTH = 1

CHUNK = 64
D_MIX = D_MODEL
RET_WIDTH = D_MIX // 2
RET_HEADS = 4
RET_DK = RET_WIDTH // RET_HEADS
RET_DV = RET_WIDTH // RET_HEADS
ROPE_BASE = 10000.0
GLA_WIDTH = D_MIX - RET_WIDTH
GLA_HEADS = 4
GLA_KEY_WIDTH = GLA_WIDTH // 2
GLA_DK = GLA_KEY_WIDTH // GLA_HEADS
GLA_DV = GLA_WIDTH // GLA_HEADS
GLA_GATE_RANK = 16
GLA_GATE_TAU = 16.0
D_FF = 4 * D_MODEL
LN_EPS = 1e-5
DEEPNORM_ALPHA = (2.0 * DEPTH) ** 0.25
DEEPNORM_BETA = (8.0 * DEPTH) ** -0.25
IN_SPLITS = (RET_WIDTH, RET_WIDTH, RET_WIDTH, RET_WIDTH,
             GLA_KEY_WIDTH, GLA_KEY_WIDTH, GLA_WIDTH, GLA_WIDTH, GLA_GATE_RANK)
D_IN_PROJ = sum(IN_SPLITS)
VALUE_SLOTS = (2, 6)

kernel_name = 'hybrid_retention_gla_deepnorm_adaln'


def _layer_norm(x, w=None, b=None):
    xf = x.astype(jnp.float32)
    mu = jnp.mean(xf, axis=-1, keepdims=True)
    var = jnp.mean(jnp.square(xf - mu), axis=-1, keepdims=True)
    y = (xf - mu) * lax.rsqrt(var + LN_EPS)
    if w is not None:
        y = y * w.astype(jnp.float32) + b.astype(jnp.float32)
    return y.astype(x.dtype)


def _head_norm(o, w, center):
    if center:
        o = o - jnp.mean(o, axis=-1, keepdims=True)
    o = o * lax.rsqrt(jnp.mean(jnp.square(o), axis=-1, keepdims=True) + LN_EPS)
    B, S, H, d = o.shape
    return o.reshape(B, S, H * d) * w.astype(jnp.float32)


def _rotary(x, pos):
    half = x.shape[-1] // 2
    inv = 1.0 / (ROPE_BASE ** jnp.linspace(0.0, 1.0, half, dtype=jnp.float32))
    ang = pos[:, None] * inv[None, :]
    cos = jnp.cos(ang)[:, None, :]
    sin = jnp.sin(ang)[:, None, :]
    x1, x2 = x[..., :half], x[..., half:]
    return jnp.concatenate([x1 * cos - x2 * sin, x2 * cos + x1 * sin], axis=-1)


def _retention(q, k, v):
    B, S, H, dk = q.shape
    dv = v.shape[-1]
    N = S // CHUNK
    log_gamma = jnp.log(1.0 - 2.0 ** (-5.0 - jnp.arange(H, dtype=jnp.float32)))
    pos = jnp.arange(S, dtype=jnp.float32)
    q = _rotary(q, pos) * (dk ** -0.5)
    k = _rotary(k, pos)
    qc = q.reshape(B, N, CHUNK, H, dk)
    kc = k.reshape(B, N, CHUNK, H, dk)
    vc = v.reshape(B, N, CHUNK, H, dv)
    idx = jnp.arange(CHUNK, dtype=jnp.float32)
    dist = jnp.abs(idx[:, None] - idx[None, :])
    intra_decay = jnp.exp(log_gamma[:, None, None] * dist)
    scores = jnp.einsum('bnchd,bnshd->bnhcs', qc, kc) * intra_decay
    o = jnp.einsum('bnhcs,bnshe->bnche', scores, vc)
    k_decay = jnp.exp(log_gamma[None, :] * (CHUNK - 1.0 - idx)[:, None])
    q_decay = jnp.exp(log_gamma[None, :] * (idx + 1.0)[:, None])
    chunk_decay = jnp.exp(log_gamma * CHUNK)[None, :, None, None]
    U = jnp.einsum('bnshd,bnshe->nbhde', kc * k_decay[:, :, None], vc)

    def step(R, U_i):
        return chunk_decay * R + U_i, R

    _, R_prev = lax.scan(step, jnp.zeros((B, H, dk, dv), jnp.float32), U)
    o = o + jnp.einsum('bnchd,nbhde->bnche', qc * q_decay[:, :, None], R_prev)
    return o.reshape(B, S, H, dv)


def _gla(q, k, v, log_a):
    B, S, H, dk = q.shape
    dv = v.shape[-1]
    N = S // CHUNK
    q = q * (dk ** -0.5)

    def to_chunks(t):
        return t.reshape(B, N, CHUNK, H, t.shape[-1]).transpose(1, 0, 2, 3, 4)

    def step(state, inp):
        qi, ki, vi, lai = inp
        b = jnp.cumsum(lai, axis=1)
        decay = jnp.exp(-jnp.abs(b[:, :, None] - b[:, None, :]))
        A = jnp.einsum('bthd,bshd,btshd->bhts', qi, ki, decay)
        o = jnp.einsum('bhts,bshe->bthe', A, vi)
        o = o + jnp.einsum('bthd,bhde->bthe', qi * jnp.exp(b), state)
        b_last = b[:, -1]
        new_state = jnp.exp(b_last)[..., None] * state + jnp.einsum(
            'bshd,bshe->bhde', ki * jnp.exp(b_last[:, None] - b), vi)
        return new_state, o

    _, o = lax.scan(step, jnp.zeros((B, H, dk, dv), jnp.float32),
                    (to_chunks(q), to_chunks(k), to_chunks(v), to_chunks(log_a)))
    return o.transpose(1, 0, 2, 3, 4).reshape(B, S, H, dv)


def _token_mixer(u, w_in, ret_norm_w, gla_gate_w, gla_gate_b, gla_norm_w, w_out):
    B, S, _ = u.shape
    proj = (u @ w_in).astype(jnp.float32)
    cuts = [int(v) for v in np.cumsum(IN_SPLITS)[:-1]]
    rq, rk, rv, rg, gq, gk, gv, gg, glr = jnp.split(proj, cuts, axis=-1)
    ro = _retention(rq.reshape(B, S, RET_HEADS, RET_DK),
                    rk.reshape(B, S, RET_HEADS, RET_DK),
                    rv.reshape(B, S, RET_HEADS, RET_DV))
    ro = _head_norm(ro, ret_norm_w, center=True) * jax.nn.silu(rg)
    gate_logit = glr @ gla_gate_w.astype(jnp.float32) + gla_gate_b.astype(jnp.float32)
    log_a = jax.nn.log_sigmoid(gate_logit) / GLA_GATE_TAU
    go = _gla(gq.reshape(B, S, GLA_HEADS, GLA_DK),
              gk.reshape(B, S, GLA_HEADS, GLA_DK),
              gv.reshape(B, S, GLA_HEADS, GLA_DV),
              log_a.reshape(B, S, GLA_HEADS, GLA_DK))
    go = _head_norm(go, gla_norm_w, center=False) * jax.nn.silu(gg)
    mixed = jnp.concatenate([ro, go], axis=-1).astype(u.dtype)
    return mixed @ w_out


def _fwd_setup_inputs(seed: int = 0) -> dict:
    key = jax.random.key(seed)
    ks = jax.random.split(key, 16)
    nrm = jax.random.normal
    offs = np.concatenate([[0], np.cumsum(IN_SPLITS)])
    col_scale = np.ones((D_IN_PROJ,), np.float32)
    for slot in VALUE_SLOTS:
        col_scale[int(offs[slot]):int(offs[slot + 1])] = DEEPNORM_BETA
    return {
        'x': nrm(ks[0], (BATCH, SEQ, D_MODEL), jnp.float32),
        'c': nrm(ks[1], (BATCH, D_MODEL), jnp.float32),
        'w_ada': nrm(ks[2], (DEPTH, D_MODEL, 6 * D_MODEL), jnp.float32) * (0.5 * D_MODEL ** -0.5),
        'b_ada': 0.02 * nrm(ks[3], (DEPTH, 6 * D_MODEL), jnp.float32),
        'w_in': nrm(ks[4], (DEPTH, D_MODEL, D_IN_PROJ), jnp.float32) * (D_MODEL ** -0.5) * jnp.asarray(col_scale),
        'ret_norm_w': 1.0 + 0.02 * nrm(ks[5], (DEPTH, RET_WIDTH), jnp.float32),
        'gla_gate_w': nrm(ks[6], (DEPTH, GLA_GATE_RANK, GLA_KEY_WIDTH), jnp.float32) * (GLA_GATE_RANK ** -0.5),
        'gla_gate_b': 0.1 * nrm(ks[7], (DEPTH, GLA_KEY_WIDTH), jnp.float32),
        'gla_norm_w': 1.0 + 0.02 * nrm(ks[8], (DEPTH, GLA_WIDTH), jnp.float32),
        'w_out': nrm(ks[9], (DEPTH, D_MIX, D_MODEL), jnp.float32) * (D_MIX ** -0.5) * DEEPNORM_BETA,
        'ln1_w': 1.0 + 0.02 * nrm(ks[10], (DEPTH, D_MODEL), jnp.float32),
        'ln1_b': 0.02 * nrm(ks[11], (DEPTH, D_MODEL), jnp.float32),
        'w_ff1': nrm(ks[12], (DEPTH, D_MODEL, D_FF), jnp.float32) * (D_MODEL ** -0.5) * DEEPNORM_BETA,
        'w_ff2': nrm(ks[13], (DEPTH, D_FF, D_MODEL), jnp.float32) * (D_FF ** -0.5) * DEEPNORM_BETA,
        'ln2_w': 1.0 + 0.02 * nrm(ks[14], (DEPTH, D_MODEL), jnp.float32),
        'ln2_b': 0.02 * nrm(ks[15], (DEPTH, D_MODEL), jnp.float32),
    }


def _fwd_reference(x, c, w_ada, b_ada, w_in, ret_norm_w, gla_gate_w, gla_gate_b, gla_norm_w,
              w_out, ln1_w, ln1_b, w_ff1, w_ff2, ln2_w, ln2_b):
    for l in range(DEPTH):
        mod = jax.nn.silu(c) @ w_ada[l] + b_ada[l]
        shift1, scale1, gate1, shift2, scale2, gate2 = jnp.split(mod, 6, axis=-1)
        u = _layer_norm(x) * (1.0 + scale1[:, None, :]) + shift1[:, None, :]
        m = _token_mixer(u, w_in[l], ret_norm_w[l], gla_gate_w[l], gla_gate_b[l],
                         gla_norm_w[l], w_out[l])
        x = _layer_norm(DEEPNORM_ALPHA * x + gate1[:, None, :] * m, ln1_w[l], ln1_b[l])
        u2 = _layer_norm(x) * (1.0 + scale2[:, None, :]) + shift2[:, None, :]
        f = jnp.square(jax.nn.relu(u2 @ w_ff1[l])) @ w_ff2[l]
        x = _layer_norm(DEEPNORM_ALPHA * x + gate2[:, None, :] * f, ln2_w[l], ln2_b[l])
    return x


import jax as _jax
import jax.numpy as _jnp

TWIN_FORMAT = 'train_step'
FWD_PARAMS = ['x', 'c', 'w_ada', 'b_ada', 'w_in', 'ret_norm_w', 'gla_gate_w', 'gla_gate_b', 'gla_norm_w', 'w_out', 'ln1_w', 'ln1_b', 'w_ff1', 'w_ff2', 'ln2_w', 'ln2_b']
TWIN_WEIGHTS = ['w_ada', 'b_ada', 'w_in', 'ret_norm_w', 'gla_gate_w', 'gla_gate_b', 'gla_norm_w', 'w_out', 'ln1_w', 'ln1_b', 'w_ff1', 'w_ff2', 'ln2_w', 'ln2_b']
TWIN_DIFF_INPUT = 'x'
TWIN_INPUTS = ['x', 'c', 'w_ada', 'b_ada', 'w_in', 'ret_norm_w', 'gla_gate_w', 'gla_gate_b', 'gla_norm_w', 'w_out', 'ln1_w', 'ln1_b', 'w_ff1', 'w_ff2', 'ln2_w', 'ln2_b', 'loss_target', 'm_w_ada', 'm_b_ada', 'm_w_in', 'm_ret_norm_w', 'm_gla_gate_w', 'm_gla_gate_b', 'm_gla_norm_w', 'm_w_out', 'm_ln1_w', 'm_ln1_b', 'm_w_ff1', 'm_w_ff2', 'm_ln2_w', 'm_ln2_b', 'v_w_ada', 'v_b_ada', 'v_w_in', 'v_ret_norm_w', 'v_gla_gate_w', 'v_gla_gate_b', 'v_gla_norm_w', 'v_w_out', 'v_ln1_w', 'v_ln1_b', 'v_w_ff1', 'v_w_ff2', 'v_ln2_w', 'v_ln2_b']
TWIN_OUTPUTS = ['loss', 'grad_x', 'grad_w_ada', 'grad_b_ada', 'grad_w_in', 'grad_ret_norm_w', 'grad_gla_gate_w', 'grad_gla_gate_b', 'grad_gla_norm_w', 'grad_w_out', 'grad_ln1_w', 'grad_ln1_b', 'grad_w_ff1', 'grad_w_ff2', 'grad_ln2_w', 'grad_ln2_b', 'delta_w_ada', 'delta_b_ada', 'delta_w_in', 'delta_ret_norm_w', 'delta_gla_gate_w', 'delta_gla_gate_b', 'delta_gla_norm_w', 'delta_w_out', 'delta_ln1_w', 'delta_ln1_b', 'delta_w_ff1', 'delta_w_ff2', 'delta_ln2_w', 'delta_ln2_b', 'new_m_w_ada', 'new_m_b_ada', 'new_m_w_in', 'new_m_ret_norm_w', 'new_m_gla_gate_w', 'new_m_gla_gate_b', 'new_m_gla_norm_w', 'new_m_w_out', 'new_m_ln1_w', 'new_m_ln1_b', 'new_m_w_ff1', 'new_m_w_ff2', 'new_m_ln2_w', 'new_m_ln2_b', 'new_v_w_ada', 'new_v_b_ada', 'new_v_w_in', 'new_v_ret_norm_w', 'new_v_gla_gate_w', 'new_v_gla_gate_b', 'new_v_gla_norm_w', 'new_v_w_out', 'new_v_ln1_w', 'new_v_ln1_b', 'new_v_w_ff1', 'new_v_w_ff2', 'new_v_ln2_w', 'new_v_ln2_b']
TWIN_LEAF_KINDS = {'loss': 'loss', 'grad_x': 'grad_x', 'grad_w_ada': 'grad_w', 'grad_b_ada': 'grad_w', 'grad_w_in': 'grad_w', 'grad_ret_norm_w': 'grad_w', 'grad_gla_gate_w': 'grad_w', 'grad_gla_gate_b': 'grad_w', 'grad_gla_norm_w': 'grad_w', 'grad_w_out': 'grad_w', 'grad_ln1_w': 'grad_w', 'grad_ln1_b': 'grad_w', 'grad_w_ff1': 'grad_w', 'grad_w_ff2': 'grad_w', 'grad_ln2_w': 'grad_w', 'grad_ln2_b': 'grad_w', 'delta_w_ada': 'delta_w', 'delta_b_ada': 'delta_w', 'delta_w_in': 'delta_w', 'delta_ret_norm_w': 'delta_w', 'delta_gla_gate_w': 'delta_w', 'delta_gla_gate_b': 'delta_w', 'delta_gla_norm_w': 'delta_w', 'delta_w_out': 'delta_w', 'delta_ln1_w': 'delta_w', 'delta_ln1_b': 'delta_w', 'delta_w_ff1': 'delta_w', 'delta_w_ff2': 'delta_w', 'delta_ln2_w': 'delta_w', 'delta_ln2_b': 'delta_w', 'new_m_w_ada': 'new_m', 'new_m_b_ada': 'new_m', 'new_m_w_in': 'new_m', 'new_m_ret_norm_w': 'new_m', 'new_m_gla_gate_w': 'new_m', 'new_m_gla_gate_b': 'new_m', 'new_m_gla_norm_w': 'new_m', 'new_m_w_out': 'new_m', 'new_m_ln1_w': 'new_m', 'new_m_ln1_b': 'new_m', 'new_m_w_ff1': 'new_m', 'new_m_w_ff2': 'new_m', 'new_m_ln2_w': 'new_m', 'new_m_ln2_b': 'new_m', 'new_v_w_ada': 'new_v', 'new_v_b_ada': 'new_v', 'new_v_w_in': 'new_v', 'new_v_ret_norm_w': 'new_v', 'new_v_gla_gate_w': 'new_v', 'new_v_gla_gate_b': 'new_v', 'new_v_gla_norm_w': 'new_v', 'new_v_w_out': 'new_v', 'new_v_ln1_w': 'new_v', 'new_v_ln1_b': 'new_v', 'new_v_w_ff1': 'new_v', 'new_v_w_ff2': 'new_v', 'new_v_ln2_w': 'new_v', 'new_v_ln2_b': 'new_v'}


def _forward(args):
    return _fwd_reference(*[args[k] for k in FWD_PARAMS])


def _output_shape():
    def fwd():
        inp = _fwd_setup_inputs(0)
        return _fwd_reference(*[inp[k] for k in FWD_PARAMS])
    out = _jax.eval_shape(fwd)
    return out.shape, out.dtype

N_MICROBATCH = 1
ADAM_LR = 0.001
ADAM_B1 = 0.9
ADAM_B2 = 0.999
ADAM_EPS = 1e-08
ADAM_WD = 0.01
ADAM_STEP = 10
PER_EXAMPLE_BATCH_AXIS = {'x': 0, 'c': 0, 'loss_target': 0}
SHARED_INPUTS = []
_WEIGHT_DTYPES = {'w_ada': _jnp.float32, 'b_ada': _jnp.float32, 'w_in': _jnp.float32, 'ret_norm_w': _jnp.float32, 'gla_gate_w': _jnp.float32, 'gla_gate_b': _jnp.float32, 'gla_norm_w': _jnp.float32, 'w_out': _jnp.float32, 'ln1_w': _jnp.float32, 'ln1_b': _jnp.float32, 'w_ff1': _jnp.float32, 'w_ff2': _jnp.float32, 'ln2_w': _jnp.float32, 'ln2_b': _jnp.float32}
MOMENT_SCALE = {'w_ada': 3.627607e-02, 'b_ada': 6.806094e-02, 'w_in': 3.369626e-02, 'ret_norm_w': 2.725504e-02, 'gla_gate_w': 6.799675e-03, 'gla_gate_b': 2.057940e-02, 'gla_norm_w': 2.746855e-02, 'w_out': 4.217841e-02, 'ln1_w': 2.269103e+00, 'ln1_b': 1.039753e+00, 'w_ff1': 1.916779e-02, 'w_ff2': 3.778148e-02, 'ln2_w': 6.404462e+01, 'ln2_b': 1.667511e+00}


def _to_microbatches(a, axis):
    t = _jnp.moveaxis(a, axis, 0)
    t = t.reshape((N_MICROBATCH, t.shape[0] // N_MICROBATCH) + t.shape[1:])
    return _jnp.moveaxis(t, 1, axis + 1)


def setup_inputs(seed: int = 0) -> dict:
    inp = _fwd_setup_inputs(seed)
    key = _jax.random.fold_in(_jax.random.key(seed), 7919)
    shape, _ = _output_shape()
    out = dict(inp)
    out["loss_target"] = _jax.random.normal(_jax.random.fold_in(key, 0), shape, _jnp.float32)
    for i, name in enumerate(TWIN_WEIGHTS):
        w = inp[name].astype(_jnp.float32)
        if MOMENT_SCALE is None:
            s = _jnp.sqrt(_jnp.mean(_jnp.square(w)) + 1e-30)
        else:
            s = MOMENT_SCALE[name]
        km, kv = _jax.random.split(_jax.random.fold_in(key, i + 1))
        out[name] = w
        out["m_" + name] = s * _jax.random.normal(km, w.shape, _jnp.float32)
        out["v_" + name] = (s * s) * _jax.random.uniform(kv, w.shape, _jnp.float32, 0.5, 1.5)
    if N_MICROBATCH > 1:
        for name, axis in PER_EXAMPLE_BATCH_AXIS.items():
            out[name] = _to_microbatches(out[name], axis)
    return {'x': out['x'], 'c': out['c'], 'w_ada': out['w_ada'], 'b_ada': out['b_ada'], 'w_in': out['w_in'], 'ret_norm_w': out['ret_norm_w'], 'gla_gate_w': out['gla_gate_w'], 'gla_gate_b': out['gla_gate_b'], 'gla_norm_w': out['gla_norm_w'], 'w_out': out['w_out'], 'ln1_w': out['ln1_w'], 'ln1_b': out['ln1_b'], 'w_ff1': out['w_ff1'], 'w_ff2': out['w_ff2'], 'ln2_w': out['ln2_w'], 'ln2_b': out['ln2_b'], 'loss_target': out['loss_target'], 'm_w_ada': out['m_w_ada'], 'm_b_ada': out['m_b_ada'], 'm_w_in': out['m_w_in'], 'm_ret_norm_w': out['m_ret_norm_w'], 'm_gla_gate_w': out['m_gla_gate_w'], 'm_gla_gate_b': out['m_gla_gate_b'], 'm_gla_norm_w': out['m_gla_norm_w'], 'm_w_out': out['m_w_out'], 'm_ln1_w': out['m_ln1_w'], 'm_ln1_b': out['m_ln1_b'], 'm_w_ff1': out['m_w_ff1'], 'm_w_ff2': out['m_w_ff2'], 'm_ln2_w': out['m_ln2_w'], 'm_ln2_b': out['m_ln2_b'], 'v_w_ada': out['v_w_ada'], 'v_b_ada': out['v_b_ada'], 'v_w_in': out['v_w_in'], 'v_ret_norm_w': out['v_ret_norm_w'], 'v_gla_gate_w': out['v_gla_gate_w'], 'v_gla_gate_b': out['v_gla_gate_b'], 'v_gla_norm_w': out['v_gla_norm_w'], 'v_w_out': out['v_w_out'], 'v_ln1_w': out['v_ln1_w'], 'v_ln1_b': out['v_ln1_b'], 'v_w_ff1': out['v_w_ff1'], 'v_w_ff2': out['v_w_ff2'], 'v_ln2_w': out['v_ln2_w'], 'v_ln2_b': out['v_ln2_b']}


def _loss(weights, diff, rest, loss_target):
    with _jax.named_scope("forward"):
        args = {**rest, TWIN_DIFF_INPUT: diff, **{k: w.astype(_WEIGHT_DTYPES[k]) for k, w in weights.items()}}
        y = _forward(args)
    with _jax.named_scope("loss_head"):
        err = _jnp.square(y.astype(_jnp.float32) - loss_target)
        return 0.5 * _jnp.sum(_jnp.mean(err, axis=-1)) if err.ndim else 0.5 * err


def _adamw(w, g, m, v):
    m = ADAM_B1 * m + (1.0 - ADAM_B1) * g
    v = ADAM_B2 * v + (1.0 - ADAM_B2) * _jnp.square(g)
    m_hat = m / (1.0 - ADAM_B1 ** ADAM_STEP)
    v_hat = v / (1.0 - ADAM_B2 ** ADAM_STEP)
    delta = -ADAM_LR * (m_hat / (_jnp.sqrt(v_hat) + ADAM_EPS) + ADAM_WD * w)
    return delta, m, v


def reference(x, c, w_ada, b_ada, w_in, ret_norm_w, gla_gate_w, gla_gate_b, gla_norm_w, w_out, ln1_w, ln1_b, w_ff1, w_ff2, ln2_w, ln2_b, loss_target, m_w_ada, m_b_ada, m_w_in, m_ret_norm_w, m_gla_gate_w, m_gla_gate_b, m_gla_norm_w, m_w_out, m_ln1_w, m_ln1_b, m_w_ff1, m_w_ff2, m_ln2_w, m_ln2_b, v_w_ada, v_b_ada, v_w_in, v_ret_norm_w, v_gla_gate_w, v_gla_gate_b, v_gla_norm_w, v_w_out, v_ln1_w, v_ln1_b, v_w_ff1, v_w_ff2, v_ln2_w, v_ln2_b):
    given = dict(x=x, c=c, w_ada=w_ada, b_ada=b_ada, w_in=w_in, ret_norm_w=ret_norm_w, gla_gate_w=gla_gate_w, gla_gate_b=gla_gate_b, gla_norm_w=gla_norm_w, w_out=w_out, ln1_w=ln1_w, ln1_b=ln1_b, w_ff1=w_ff1, w_ff2=w_ff2, ln2_w=ln2_w, ln2_b=ln2_b, loss_target=loss_target, m_w_ada=m_w_ada, m_b_ada=m_b_ada, m_w_in=m_w_in, m_ret_norm_w=m_ret_norm_w, m_gla_gate_w=m_gla_gate_w, m_gla_gate_b=m_gla_gate_b, m_gla_norm_w=m_gla_norm_w, m_w_out=m_w_out, m_ln1_w=m_ln1_w, m_ln1_b=m_ln1_b, m_w_ff1=m_w_ff1, m_w_ff2=m_w_ff2, m_ln2_w=m_ln2_w, m_ln2_b=m_ln2_b, v_w_ada=v_w_ada, v_b_ada=v_b_ada, v_w_in=v_w_in, v_ret_norm_w=v_ret_norm_w, v_gla_gate_w=v_gla_gate_w, v_gla_gate_b=v_gla_gate_b, v_gla_norm_w=v_gla_norm_w, v_w_out=v_w_out, v_ln1_w=v_ln1_w, v_ln1_b=v_ln1_b, v_w_ff1=v_w_ff1, v_w_ff2=v_w_ff2, v_ln2_w=v_ln2_w, v_ln2_b=v_ln2_b)
    weights = {n: given[n] for n in TWIN_WEIGHTS}
    shared = {n: given[n] for n in SHARED_INPUTS}
    per_example = {n: given[n] for n in ['x', 'c']}
    grad_fn = _jax.value_and_grad(_loss, argnums=(0, 1))

    def one_microbatch(ex, loss_target):
        ex = dict(ex)
        diff = ex.pop(TWIN_DIFF_INPUT)
        return grad_fn(weights, diff, {**shared, **ex}, loss_target)

    if N_MICROBATCH == 1:
        loss, (grad_w, grad_x) = one_microbatch(per_example, given["loss_target"])
    else:
        def body(carry, xs):
            loss_sum, grad_sum = carry
            l_k, (gw_k, gx_k) = one_microbatch(xs[0], xs[1])
            with _jax.named_scope("update"):
                return (loss_sum + l_k, _jax.tree.map(_jnp.add, grad_sum, gw_k)), gx_k

        init = (_jnp.zeros((), _jnp.float32), _jax.tree.map(_jnp.zeros_like, weights))
        (loss, grad_w), grad_x = _jax.lax.scan(body, init, (per_example, given["loss_target"]))
    with _jax.named_scope("update"):
        delta_w, new_m, new_v = {}, {}, {}
        for n in TWIN_WEIGHTS:
            delta_w[n], new_m[n], new_v[n] = _adamw(weights[n], grad_w[n], given["m_" + n], given["v_" + n])
    return (loss, grad_x, *[grad_w[n] for n in TWIN_WEIGHTS], *[delta_w[n] for n in TWIN_WEIGHTS],
            *[new_m[n] for n in TWIN_WEIGHTS], *[new_v[n] for n in TWIN_WEIGHTS])
```

```python
import functools

import numpy as np
import jax
import jax.numpy as jnp
from jax import lax
from jax.experimental import pallas as pl
from jax.experimental.pallas import tpu as pltpu

F32 = jnp.float32
BF16 = jnp.bfloat16
MESH = pl.DeviceIdType.MESH
HIGHEST = lax.Precision.HIGHEST

N_DEV = 8
D_MODEL = 1024
CHUNK = 64
RET_HEADS = 4
RET_D = 128
GLA_HEADS = 4
GLA_DK = 64
GLA_DV = 128
GLA_KW = GLA_HEADS * GLA_DK
GATE_RANK = 16
GATE_TAU = 16.0
D_FF = 4096
LN_EPS = 1e-5
ALPHA = (2.0 * 1) ** 0.25
D_IN = 3600
D_IN_PAD = 3712
ADA_COLS = 6 * D_MODEL // N_DEV
IN_COLS = D_IN // N_DEV
FF_COLS = D_FF // N_DEV
OUT_ROWS = D_MODEL // N_DEV

OFF_RQ, OFF_RK, OFF_RV, OFF_RG = 0, 512, 1024, 1536
OFF_GQ, OFF_GK, OFF_GV, OFF_GG, OFF_GLR = 2048, 2304, 2560, 3072, 3584

ADAM_LR, ADAM_B1, ADAM_B2, ADAM_EPS, ADAM_WD, ADAM_STEP = 0.001, 0.9, 0.999, 1e-08, 0.01, 10

PK_IN, PK_OUT, PK_FF1, PK_FF2 = 0, 450, 578, 1090
PK_ROWS = 1664

SM_DMOD, SM_RNW, SM_GGB, SM_GNW, SM_LN1W, SM_LN1B, SM_LN2W, SM_LN2B, SM_GGW, SM_LOSS = 0, 48, 52, 54, 58, 66, 74, 82, 90, 122
SM_ROWS = 136

V7X_VMEM_LIMIT = 56 * 1024 * 1024

ROW_TILE = 256
MIX_TILE = 256


def _log_gamma(h):
    return float(np.log(np.float32(1.0) - np.float32(2.0) ** np.float32(-5.0 - h)))


def _my_coords():
    return lax.axis_index("x"), lax.axis_index("y"), lax.axis_index("c")


def _flip(v, bit):
    return 1 - v if bit else v


def _peer(k):
    x, y, c = _my_coords()
    px, py, pc = _flip(x, (k >> 2) & 1), _flip(y, (k >> 1) & 1), _flip(c, k & 1)
    return (px, py, pc), 4 * px + 2 * py + pc


def _dot(a, b, dims=(((1,), (0,)), ((), ())), precision=None):
    return lax.dot_general(a, b, dims, precision=precision, preferred_element_type=F32)


NN = (((1,), (0,)), ((), ()))
NT = (((1,), (1,)), ((), ()))
TN = (((0,), (0,)), ((), ()))


def _sigmoid(x):
    return 1.0 / (1.0 + jnp.exp(-x))


def _ln_stats(x):
    mu = jnp.mean(x, axis=-1, keepdims=True)
    xc = x - mu
    var = jnp.mean(xc * xc, axis=-1, keepdims=True)
    rstd = lax.rsqrt(var + LN_EPS)
    return xc * rstd, rstd


def _ln_bwd(dyh, xh, rstd):
    return rstd * (dyh - jnp.mean(dyh, axis=-1, keepdims=True) - xh * jnp.mean(dyh * xh, axis=-1, keepdims=True))


def _adaln_mod(c_ext, w_ada_l, b_l):
    width = c_ext.shape[1]

    def body(c_ref, w_ref, b_ref, call_ref, mod_ref, s1, r1, s2, r2):
        x, y, c = _my_coords()
        me = 4 * x + 2 * y + c
        call_ref[me] = c_ref[...]
        sends = []
        for k in range(1, N_DEV):
            peer, _ = _peer(k)
            cp = pltpu.make_async_remote_copy(c_ref, call_ref.at[me], s1.at[k - 1], r1.at[k - 1],
                                              device_id=peer, device_id_type=MESH)
            cp.start()
            sends.append(cp)
        for k in range(1, N_DEV):
            peer, pid = _peer(k)
            pltpu.make_async_remote_copy(c_ref, call_ref.at[pid], s1.at[k - 1], r1.at[k - 1],
                                         device_id=peer, device_id_type=MESH).wait_recv()
        for cp in sends:
            cp.wait_send()
        row = lax.broadcasted_iota(jnp.int32, (N_DEV, D_MODEL), 0)
        call = jnp.zeros((N_DEV, D_MODEL), F32)
        for j in range(N_DEV):
            call = jnp.where(row == j, jnp.broadcast_to(call_ref[j][:, :D_MODEL], (N_DEV, D_MODEL)), call)
        sc = call * _sigmoid(call)
        mod = _dot(sc, w_ref[...], NN, HIGHEST) + b_ref[...]
        mod_ref[me] = mod
        sends = []
        for k in range(1, N_DEV):
            peer, _ = _peer(k)
            cp = pltpu.make_async_remote_copy(mod_ref.at[me], mod_ref.at[me], s2.at[k - 1], r2.at[k - 1],
                                              device_id=peer, device_id_type=MESH)
            cp.start()
            sends.append(cp)
        for k in range(1, N_DEV):
            peer, pid = _peer(k)
            pltpu.make_async_remote_copy(mod_ref.at[pid], mod_ref.at[pid], s2.at[k - 1], r2.at[k - 1],
                                         device_id=peer, device_id_type=MESH).wait_recv()
        for cp in sends:
            cp.wait_send()

    vm = pl.BlockSpec(memory_space=pltpu.VMEM)
    return pl.pallas_call(
        body, name="adaln_mod",
        out_shape=(jax.ShapeDtypeStruct((N_DEV, 1, width), F32),
                   jax.ShapeDtypeStruct((N_DEV, N_DEV, ADA_COLS), F32)),
        in_specs=[vm, vm, vm], out_specs=(vm, vm),
        scratch_shapes=[pltpu.SemaphoreType.DMA((N_DEV - 1,))] * 4,
        compiler_params=pltpu.CompilerParams(vmem_limit_bytes=V7X_VMEM_LIMIT),
    )(c_ext, w_ada_l, b_l)


def _wgather(packed):
    rows, cols = packed.shape

    def body(x_ref, out_ref, send_sems, recv_sems, local_sem):
        x, y, c = _my_coords()
        me, sibling = (x, y, c), (x, y, 1 - c)
        chips = [(1 - x, y), (x, 1 - y), (1 - x, 1 - y)]

        def slab(px, py, pc):
            return out_ref.at[4 * px + 2 * py + pc]

        def copy(k, block, to, src=None):
            return pltpu.make_async_remote_copy(
                src_ref=slab(*block) if src is None else src, dst_ref=slab(*block),
                send_sem=send_sems.at[k], recv_sem=recv_sems.at[k], device_id=to, device_id_type=MESH)

        mine = pltpu.make_async_copy(x_ref, slab(*me), local_sem)
        mine.start()
        first = [copy(0, me, sibling, src=x_ref)]
        first += [copy(1 + j, me, (*chip, c), src=x_ref) for j, chip in enumerate(chips)]
        for cp in first:
            cp.start()
        passed = [copy(4 + j, (*chip, c), sibling) for j, chip in enumerate(chips)]
        for j, chip in enumerate(chips):
            copy(1 + j, (*chip, c), me).wait_recv()
            passed[j].start()
        copy(0, sibling, me).wait_recv()
        for j, chip in enumerate(chips):
            copy(4 + j, (*chip, 1 - c), me).wait_recv()
        for cp in first + passed:
            cp.wait_send()
        mine.wait()

    anyspace = pl.BlockSpec(memory_space=pl.ANY)
    return pl.pallas_call(
        body, name="wgather",
        out_shape=jax.ShapeDtypeStruct((N_DEV, rows, cols), packed.dtype),
        in_specs=[anyspace], out_specs=anyspace,
        scratch_shapes=[pltpu.SemaphoreType.DMA((7,)), pltpu.SemaphoreType.DMA((7,)), pltpu.SemaphoreType.DMA],
    )(packed)


def _small_gather(vec):
    rows, cols = vec.shape

    def body(v_ref, out_ref, s_sems, r_sems):
        x, y, c = _my_coords()
        me = 4 * x + 2 * y + c
        out_ref[me] = v_ref[...]
        sends = []
        for k in range(1, N_DEV):
            peer, _ = _peer(k)
            cp = pltpu.make_async_remote_copy(v_ref, out_ref.at[me], s_sems.at[k - 1], r_sems.at[k - 1],
                                              device_id=peer, device_id_type=MESH)
            cp.start()
            sends.append(cp)
        for k in range(1, N_DEV):
            peer, pid = _peer(k)
            pltpu.make_async_remote_copy(v_ref, out_ref.at[pid], s_sems.at[k - 1], r_sems.at[k - 1],
                                         device_id=peer, device_id_type=MESH).wait_recv()
        for cp in sends:
            cp.wait_send()

    vm = pl.BlockSpec(memory_space=pltpu.VMEM)
    return pl.pallas_call(
        body, name="small_gather",
        out_shape=jax.ShapeDtypeStruct((N_DEV, rows, cols), vec.dtype),
        in_specs=[vm], out_specs=vm,
        scratch_shapes=[pltpu.SemaphoreType.DMA((N_DEV - 1,))] * 2,
    )(vec)


def _grad_exchange(send):
    shape = send.shape

    def body(s_ref, r_ref, s_sems, r_sems, local_sem):
        x, y, c = _my_coords()
        me = 4 * x + 2 * y + c
        mine = pltpu.make_async_copy(s_ref.at[me], r_ref.at[me], local_sem)
        mine.start()
        sends = []
        for k in range(1, N_DEV):
            peer, pid = _peer(k)
            cp = pltpu.make_async_remote_copy(s_ref.at[pid], r_ref.at[me], s_sems.at[k - 1], r_sems.at[k - 1],
                                              device_id=peer, device_id_type=MESH)
            cp.start()
            sends.append(cp)
        for k in range(1, N_DEV):
            peer, pid = _peer(k)
            pltpu.make_async_remote_copy(s_ref.at[pid], r_ref.at[pid], s_sems.at[k - 1], r_sems.at[k - 1],
                                         device_id=peer, device_id_type=MESH).wait_recv()
        for cp in sends:
            cp.wait_send()
        mine.wait()

    anyspace = pl.BlockSpec(memory_space=pl.ANY)
    return pl.pallas_call(
        body, name="grad_exchange",
        out_shape=jax.ShapeDtypeStruct(shape, send.dtype),
        in_specs=[anyspace], out_specs=anyspace,
        scratch_shapes=[pltpu.SemaphoreType.DMA((N_DEV - 1,)), pltpu.SemaphoreType.DMA((N_DEV - 1,)),
                        pltpu.SemaphoreType.DMA],
    )(send)


def _load_resident(step_is_first, pairs, sem):
    @pl.when(step_is_first)
    def _():
        copies = [pltpu.make_async_copy(src, dst, sem.at[i]) for i, (src, dst) in enumerate(pairs)]
        for cp in copies:
            cp.start()
        for cp in copies:
            cp.wait()


def _inproj_fwd(x2d, sc1p, sh1, w_in_b):
    t = x2d.shape[0]
    tm = min(ROW_TILE, t)

    def body(x_ref, sc_ref, sh_ref, w_hbm, proj_ref, u_ref, w_vmem, sem):
        _load_resident(pl.program_id(0) == 0, [(w_hbm, w_vmem)], sem)
        xh, _ = _ln_stats(x_ref[...])
        ub = (xh * sc_ref[...] + sh_ref[...]).astype(BF16)
        u_ref[...] = ub
        proj_ref[...] = _dot(ub, w_vmem[...])

    row = lambda i: (i, 0)
    fix = lambda i: (0, 0)
    return pl.pallas_call(
        body, name="inproj_fwd", grid=(t // tm,),
        in_specs=[pl.BlockSpec((tm, D_MODEL), row), pl.BlockSpec((1, D_MODEL), fix), pl.BlockSpec((1, D_MODEL), fix),
                  pl.BlockSpec(memory_space=pl.ANY)],
        out_specs=(pl.BlockSpec((tm, D_IN_PAD), row), pl.BlockSpec((tm, D_MODEL), row)),
        out_shape=(jax.ShapeDtypeStruct((t, D_IN_PAD), F32), jax.ShapeDtypeStruct((t, D_MODEL), BF16)),
        scratch_shapes=[pltpu.VMEM((D_MODEL, D_IN_PAD), BF16), pltpu.SemaphoreType.DMA((1,))],
        compiler_params=pltpu.CompilerParams(dimension_semantics=("arbitrary",), vmem_limit_bytes=V7X_VMEM_LIMIT),
    )(x2d, sc1p, sh1, w_in_b)


def _mixer_consts():
    r64 = lax.broadcasted_iota(jnp.int32, (CHUNK, CHUNK), 0)
    c64 = lax.broadcasted_iota(jnp.int32, (CHUNK, CHUNK), 1)
    dist = jnp.abs(r64 - c64).astype(F32)
    rowf = lax.broadcasted_iota(jnp.int32, (CHUNK, RET_D), 0).astype(F32)
    dec, qd, kd, g64 = [], [], [], []
    for h in range(RET_HEADS):
        lg = _log_gamma(h)
        dec.append(jnp.exp(lg * dist))
        qd.append(jnp.exp(lg * (rowf + 1.0)))
        kd.append(jnp.exp(lg * (CHUNK - 1.0 - rowf)))
        g64.append(float(np.exp(np.float32(lg) * np.float32(CHUNK))))
    ltri = (c64 <= r64).astype(F32)
    utri = (c64 >= r64).astype(F32)
    lane = lax.broadcasted_iota(jnp.int32, (1, GLA_KW), 1)
    hmask = [((lane >= h * GLA_DK) & (lane < (h + 1) * GLA_DK)).astype(F32) for h in range(GLA_HEADS)]
    rs = lax.broadcasted_iota(jnp.int32, (GLA_HEADS * CHUNK, CHUNK), 0) & (CHUNK - 1)
    cs = lax.broadcasted_iota(jnp.int32, (GLA_HEADS * CHUNK, CHUNK), 1)
    lower = cs <= rs
    return dict(dec=dec, qd=qd, kd=kd, g64=g64, ltri=ltri, utri=utri, hmask=hmask, lower=lower)


def _rotate(v, cosv, sinv):
    return v * cosv + pltpu.roll(v, RET_D // 2, 1) * sinv


def _rotate_t(d, cosv, sinv):
    return d * cosv + pltpu.roll(d * sinv, RET_D // 2, 1)


def _stack_heads(v, hmask):
    return jnp.concatenate([v * hmask[h] for h in range(GLA_HEADS)], axis=0)


def _gla_gates(glr, gw, gb, ltri):
    z = _dot(glr, gw, NN, HIGHEST) + gb
    la = (jnp.minimum(z, 0.0) - jnp.log(1.0 + jnp.exp(-jnp.abs(z)))) * (1.0 / GATE_TAU)
    b = _dot(ltri, la, NN, HIGHEST)
    level = b[CHUNK // 2 - 1:CHUNK // 2, :]
    ep = jnp.exp(jnp.clip(b - level, -80.0, 80.0))
    em = jnp.exp(jnp.clip(level - b, -80.0, 80.0))
    bl = b[CHUNK - 1:CHUNK, :]
    return z, b, bl, ep, em


def _mixer_fwd(proj, cos_t, sin_t, gw_pad, gb, rnw, gnw):
    t = proj.shape[0]
    tc = min(MIX_TILE, t)
    cps = tc // CHUNK
    nch = t // CHUNK
    scale_r = RET_D ** -0.5
    scale_g = GLA_DK ** -0.5

    def body(rq_ref, rk_ref, rv_ref, rg_ref, gq_ref, gk_ref, gv_ref, gg_ref, glr_ref, cos_ref, sin_ref,
             gw_ref, gb_ref, rnw_ref, gnw_ref, mix_ref, oraw_ref, rst_ref, sst_ref, r_scr, s_scr):
        @pl.when(pl.program_id(0) == 0)
        def _():
            r_scr[...] = jnp.zeros_like(r_scr)
            s_scr[...] = jnp.zeros_like(s_scr)

        k = _mixer_consts()

        def chunk(j, carry):
            rows = pl.ds(pl.multiple_of(j * CHUNK, CHUNK), CHUNK)
            cosv, sinv = cos_ref[rows, :], sin_ref[rows, :]
            for h in range(RET_HEADS):
                cols = slice(h * RET_D, (h + 1) * RET_D)
                qr = _rotate(rq_ref[rows, cols], cosv, sinv) * scale_r
                kr = _rotate(rk_ref[rows, cols], cosv, sinv)
                vb = rv_ref[rows, cols].astype(BF16)
                qb, kb = qr.astype(BF16), kr.astype(BF16)
                p = _dot(qb, kb, NT) * k["dec"][h]
                rp = r_scr[cols, :]
                o = _dot(p.astype(BF16), vb) + _dot((qr * k["qd"][h]).astype(BF16), rp.astype(BF16))
                rst_ref[j, cols, :] = rp
                r_scr[cols, :] = k["g64"][h] * rp + _dot((kr * k["kd"][h]).astype(BF16), vb, TN)
                oraw_ref[rows, cols] = o
                oc = o - jnp.mean(o, axis=-1, keepdims=True)
                n = oc * lax.rsqrt(jnp.mean(oc * oc, axis=-1, keepdims=True) + LN_EPS)
                g = rg_ref[rows, cols]
                mix_ref[rows, cols] = (n * rnw_ref[:, cols] * (g * _sigmoid(g))).astype(BF16)

            _, b, bl, ep, em = _gla_gates(glr_ref[rows, :], gw_ref[...], gb_ref[...], k["ltri"])
            qs = gq_ref[rows, :] * scale_g
            kk = gk_ref[rows, :]
            x_all = _dot(_stack_heads(qs * ep, k["hmask"]).astype(BF16), (kk * em).astype(BF16), NT)
            y_all = _dot(_stack_heads(qs * em, k["hmask"]).astype(BF16), (kk * ep).astype(BF16), NT)
            a_all = jnp.where(k["lower"], x_all, y_all).astype(BF16)
            st = s_scr[...]
            oq = _dot(_stack_heads(qs * jnp.exp(b), k["hmask"]).astype(BF16), st.astype(BF16), NT)
            kg = kk * jnp.exp(bl - b)
            sst_ref[j] = st
            st_new = st * jnp.exp(bl)
            for h in range(GLA_HEADS):
                cols = slice(h * GLA_DV, (h + 1) * GLA_DV)
                hr = slice(h * CHUNK, (h + 1) * CHUNK)
                vb = gv_ref[rows, cols].astype(BF16)
                o = _dot(a_all[hr, :], vb) + oq[hr, :]
                st_new = st_new + _dot(vb, (kg * k["hmask"][h]).astype(BF16), TN)
                ocols = slice(RET_HEADS * RET_D + h * GLA_DV, RET_HEADS * RET_D + (h + 1) * GLA_DV)
                oraw_ref[rows, ocols] = o
                n = o * lax.rsqrt(jnp.mean(o * o, axis=-1, keepdims=True) + LN_EPS)
                g = gg_ref[rows, cols]
                mix_ref[rows, ocols] = (n * gnw_ref[:, cols] * (g * _sigmoid(g))).astype(BF16)
            s_scr[...] = st_new
            return carry

        lax.fori_loop(0, cps, chunk, 0)

    def col(width, off):
        return pl.BlockSpec((tc, width), lambda i, o=off // width: (i, o))

    fix = lambda i: (0, 0)
    in_specs = [col(512, OFF_RQ), col(512, OFF_RK), col(512, OFF_RV), col(512, OFF_RG),
                col(256, OFF_GQ), col(256, OFF_GK), col(512, OFF_GV), col(512, OFF_GG), col(128, OFF_GLR),
                pl.BlockSpec((tc, RET_D), lambda i: (i, 0)), pl.BlockSpec((tc, RET_D), lambda i: (i, 0)),
                pl.BlockSpec((128, GLA_KW), fix), pl.BlockSpec((1, GLA_KW), fix),
                pl.BlockSpec((1, 512), fix), pl.BlockSpec((1, 512), fix)]
    out_specs = (pl.BlockSpec((tc, D_MODEL), lambda i: (i, 0)), pl.BlockSpec((tc, D_MODEL), lambda i: (i, 0)),
                 pl.BlockSpec((cps, RET_HEADS * RET_D, RET_D), lambda i: (i, 0, 0)),
                 pl.BlockSpec((cps, GLA_DV, GLA_KW), lambda i: (i, 0, 0)))
    out_shape = (jax.ShapeDtypeStruct((t, D_MODEL), BF16), jax.ShapeDtypeStruct((t, D_MODEL), F32),
                 jax.ShapeDtypeStruct((nch, RET_HEADS * RET_D, RET_D), F32),
                 jax.ShapeDtypeStruct((nch, GLA_DV, GLA_KW), F32))
    return pl.pallas_call(
        body, name="mixer_fwd", grid=(t // tc,), in_specs=in_specs, out_specs=out_specs, out_shape=out_shape,
        scratch_shapes=[pltpu.VMEM((RET_HEADS * RET_D, RET_D), F32), pltpu.VMEM((GLA_DV, GLA_KW), F32)],
        compiler_params=pltpu.CompilerParams(dimension_semantics=("arbitrary",), vmem_limit_bytes=V7X_VMEM_LIMIT),
    )(*([proj] * 9), cos_t, sin_t, gw_pad, gb, rnw, gnw)


def _mid_fwd(mixed, x2d, target, vecs, w_out_b, w1_b, w2_b):
    t = x2d.shape[0]
    tm = min(ROW_TILE, t)

    def body(mix_ref, x_ref, tgt_ref, v_ref, wo_hbm, w1_hbm, w2_hbm,
             m_ref, x1n_ref, rstd_ref, u2_ref, a_ref, df_ref, dh2_ref, acc_ref, wo, w1, w2, sem):
        first = pl.program_id(0) == 0
        _load_resident(first, [(wo_hbm, wo), (w1_hbm, w1), (w2_hbm, w2)], sem)

        @pl.when(first)
        def _():
            acc_ref[...] = jnp.zeros_like(acc_ref)

        gate1, sc2p, sh2, gate2 = v_ref[0:1, :], v_ref[1:2, :], v_ref[2:3, :], v_ref[3:4, :]
        l1w, l1b, l2w, l2b = v_ref[4:5, :], v_ref[5:6, :], v_ref[6:7, :], v_ref[7:8, :]
        m = _dot(mix_ref[...], wo[...])
        m_ref[...] = m.astype(BF16)
        x1n, rstd1 = _ln_stats(ALPHA * x_ref[...] + gate1 * m)
        x1n_ref[...] = x1n
        rstd_ref[...] = rstd1
        x1 = x1n * l1w + l1b
        xh1, _ = _ln_stats(x1)
        u2 = (xh1 * sc2p + sh2).astype(BF16)
        u2_ref[...] = u2
        a = _dot(u2, w1[...])
        a_ref[...] = a.astype(BF16)
        r = jnp.maximum(a, 0.0)
        f = _dot((r * r).astype(BF16), w2[...])
        yh, rstd2 = _ln_stats(ALPHA * x1 + gate2 * f)
        e = yh * l2w + l2b - tgt_ref[...]
        dy = e * (1.0 / D_MODEL)
        dh2 = _ln_bwd(dy * l2w, yh, rstd2)
        dh2_ref[...] = dh2
        df_ref[...] = (dh2 * gate2).astype(BF16)
        acc_ref[0:1, :] += jnp.sum(dy * yh, axis=0, keepdims=True)
        acc_ref[1:2, :] += jnp.sum(dy, axis=0, keepdims=True)
        acc_ref[2:3, :] += jnp.sum(dh2 * f, axis=0, keepdims=True)
        acc_ref[3:4, :] += jnp.sum(e * e, axis=0, keepdims=True) * (0.5 / D_MODEL)

    row = lambda i: (i, 0)
    fix = lambda i: (0, 0)
    hbm = pl.BlockSpec(memory_space=pl.ANY)
    return pl.pallas_call(
        body, name="mid_fwd", grid=(t // tm,),
        in_specs=[pl.BlockSpec((tm, D_MODEL), row), pl.BlockSpec((tm, D_MODEL), row), pl.BlockSpec((tm, D_MODEL), row),
                  pl.BlockSpec((8, D_MODEL), fix), hbm, hbm, hbm],
        out_specs=(pl.BlockSpec((tm, D_MODEL), row), pl.BlockSpec((tm, D_MODEL), row), pl.BlockSpec((tm, 1), row),
                   pl.BlockSpec((tm, D_MODEL), row), pl.BlockSpec((tm, D_FF), row), pl.BlockSpec((tm, D_MODEL), row),
                   pl.BlockSpec((tm, D_MODEL), row), pl.BlockSpec((8, D_MODEL), fix)),
        out_shape=(jax.ShapeDtypeStruct((t, D_MODEL), BF16), jax.ShapeDtypeStruct((t, D_MODEL), F32),
                   jax.ShapeDtypeStruct((t, 1), F32), jax.ShapeDtypeStruct((t, D_MODEL), BF16),
                   jax.ShapeDtypeStruct((t, D_FF), BF16), jax.ShapeDtypeStruct((t, D_MODEL), BF16),
                   jax.ShapeDtypeStruct((t, D_MODEL), F32), jax.ShapeDtypeStruct((8, D_MODEL), F32)),
        scratch_shapes=[pltpu.VMEM((D_MODEL, D_MODEL), BF16), pltpu.VMEM((D_MODEL, D_FF), BF16),
                        pltpu.VMEM((D_FF, D_MODEL), BF16), pltpu.SemaphoreType.DMA((3,))],
        compiler_params=pltpu.CompilerParams(dimension_semantics=("arbitrary",), vmem_limit_bytes=V7X_VMEM_LIMIT),
    )(mixed, x2d, target, vecs, w_out_b, w1_b, w2_b)


def _ffn_bwd(df, a, dh2, x1n, rstd1, m, vecs, w_out_b, w1_b, w2_b):
    t = x1n.shape[0]
    tm = min(ROW_TILE, t)

    def body(df_ref, a_ref, dh2_ref, x1n_ref, rstd_ref, m_ref, v_ref, wo_hbm, w1_hbm, w2_hbm,
             da_ref, dm_ref, dmix_ref, dxa_ref, acc_ref, wo, w1, w2, sem):
        first = pl.program_id(0) == 0
        _load_resident(first, [(wo_hbm, wo), (w1_hbm, w1), (w2_hbm, w2)], sem)

        @pl.when(first)
        def _():
            acc_ref[...] = jnp.zeros_like(acc_ref)

        gate1, sc2p, l1w, l1b = v_ref[0:1, :], v_ref[1:2, :], v_ref[2:3, :], v_ref[3:4, :]
        dr2 = _dot(df_ref[...], w2[...], NT)
        da = (dr2 * (2.0 * jnp.maximum(a_ref[...].astype(F32), 0.0))).astype(BF16)
        da_ref[...] = da
        du2 = _dot(da, w1[...], NT)
        x1n = x1n_ref[...]
        xh1, rstd0 = _ln_stats(x1n * l1w + l1b)
        dx1 = ALPHA * dh2_ref[...] + _ln_bwd(du2 * sc2p, xh1, rstd0)
        dh1 = _ln_bwd(dx1 * l1w, x1n, rstd_ref[...])
        dxa_ref[...] = ALPHA * dh1
        dm = (dh1 * gate1).astype(BF16)
        dm_ref[...] = dm
        dmix_ref[...] = _dot(dm, wo[...], NT)
        acc_ref[0:1, :] += jnp.sum(du2 * xh1, axis=0, keepdims=True)
        acc_ref[1:2, :] += jnp.sum(du2, axis=0, keepdims=True)
        acc_ref[2:3, :] += jnp.sum(dx1 * x1n, axis=0, keepdims=True)
        acc_ref[3:4, :] += jnp.sum(dx1, axis=0, keepdims=True)
        acc_ref[4:5, :] += jnp.sum(dh1 * m_ref[...].astype(F32), axis=0, keepdims=True)

    row = lambda i: (i, 0)
    fix = lambda i: (0, 0)
    hbm = pl.BlockSpec(memory_space=pl.ANY)
    return pl.pallas_call(
        body, name="ffn_bwd", grid=(t // tm,),
        in_specs=[pl.BlockSpec((tm, D_MODEL), row), pl.BlockSpec((tm, D_FF), row), pl.BlockSpec((tm, D_MODEL), row),
                  pl.BlockSpec((tm, D_MODEL), row), pl.BlockSpec((tm, 1), row), pl.BlockSpec((tm, D_MODEL), row),
                  pl.BlockSpec((8, D_MODEL), fix), hbm, hbm, hbm],
        out_specs=(pl.BlockSpec((tm, D_FF), row), pl.BlockSpec((tm, D_MODEL), row), pl.BlockSpec((tm, D_MODEL), row),
                   pl.BlockSpec((tm, D_MODEL), row), pl.BlockSpec((8, D_MODEL), fix)),
        out_shape=(jax.ShapeDtypeStruct((t, D_FF), BF16), jax.ShapeDtypeStruct((t, D_MODEL), BF16),
                   jax.ShapeDtypeStruct((t, D_MODEL), F32), jax.ShapeDtypeStruct((t, D_MODEL), F32),
                   jax.ShapeDtypeStruct((8, D_MODEL), F32)),
        scratch_shapes=[pltpu.VMEM((D_MODEL, D_MODEL), BF16), pltpu.VMEM((D_MODEL, D_FF), BF16),
                        pltpu.VMEM((D_FF, D_MODEL), BF16), pltpu.SemaphoreType.DMA((3,))],
        compiler_params=pltpu.CompilerParams(dimension_semantics=("arbitrary",), vmem_limit_bytes=V7X_VMEM_LIMIT),
    )(df, a, dh2, x1n, rstd1, m, vecs, w_out_b, w1_b, w2_b)


def _matmul_tn(lhs, rhs, tmm, tn, name, relu_sq=False):
    t, mm = lhs.shape
    nn = rhs.shape[1]
    tk = min(512, t)
    nk = t // tk

    def body(l_ref, r_ref, o_ref, acc):
        kk = pl.program_id(2)

        @pl.when(kk == 0)
        def _():
            acc[...] = jnp.zeros_like(acc)

        l = l_ref[...]
        if relu_sq:
            lf = jnp.maximum(l.astype(F32), 0.0)
            l = (lf * lf).astype(BF16)
        acc[...] += _dot(l, r_ref[...], TN)

        @pl.when(kk == nk - 1)
        def _():
            o_ref[...] = acc[...].astype(o_ref.dtype)

    return pl.pallas_call(
        body, name=name, grid=(mm // tmm, nn // tn, nk),
        in_specs=[pl.BlockSpec((tk, tmm), lambda i, j, k: (k, i)), pl.BlockSpec((tk, tn), lambda i, j, k: (k, j))],
        out_specs=pl.BlockSpec((tmm, tn), lambda i, j, k: (i, j)),
        out_shape=jax.ShapeDtypeStruct((mm, nn), BF16),
        scratch_shapes=[pltpu.VMEM((tmm, tn), F32)],
        compiler_params=pltpu.CompilerParams(dimension_semantics=("arbitrary", "arbitrary", "arbitrary"),
                                             vmem_limit_bytes=V7X_VMEM_LIMIT),
    )(lhs, rhs)


def _mixer_bwd(dmix, proj, oraw, cos_t, sin_t, rst, sst, gw_pad, gb, rnw, gnw):
    t = proj.shape[0]
    tc = min(MIX_TILE, t)
    cps = tc // CHUNK
    nsteps = t // tc
    scale_r = RET_D ** -0.5
    scale_g = GLA_DK ** -0.5

    def body(dmix_ref, rq_ref, rk_ref, rv_ref, rg_ref, gq_ref, gk_ref, gv_ref, gg_ref, glr_ref, oraw_ref,
             cos_ref, sin_ref, rst_ref, sst_ref, gw_ref, gb_ref, rnw_ref, gnw_ref,
             dproj_ref, dgw_ref, dvec_ref, dr_scr, ds_scr):
        @pl.when(pl.program_id(0) == 0)
        def _():
            dr_scr[...] = jnp.zeros_like(dr_scr)
            ds_scr[...] = jnp.zeros_like(ds_scr)
            dgw_ref[...] = jnp.zeros_like(dgw_ref)
            dvec_ref[...] = jnp.zeros_like(dvec_ref)

        k = _mixer_consts()
        last_row = lax.broadcasted_iota(jnp.int32, (CHUNK, GLA_KW), 0) == CHUNK - 1

        def chunk(jj, carry):
            j = cps - 1 - jj
            rows = pl.ds(pl.multiple_of(j * CHUNK, CHUNK), CHUNK)
            cosv, sinv = cos_ref[rows, :], sin_ref[rows, :]
            for h in range(RET_HEADS):
                cols = slice(h * RET_D, (h + 1) * RET_D)
                o = oraw_ref[rows, cols]
                g = rg_ref[rows, cols]
                w = rnw_ref[:, cols]
                dout = dmix_ref[rows, cols]
                oc = o - jnp.mean(o, axis=-1, keepdims=True)
                inv = lax.rsqrt(jnp.mean(oc * oc, axis=-1, keepdims=True) + LN_EPS)
                n = oc * inv
                sg = _sigmoid(g)
                sil = g * sg
                dn = dout * w * sil
                dvec_ref[0:1, cols] += jnp.sum(dout * n * sil, axis=0, keepdims=True)
                dproj_ref[rows, OFF_RG + h * RET_D:OFF_RG + (h + 1) * RET_D] = (
                    dout * n * w * (sg * (1.0 + g * (1.0 - sg)))).astype(BF16)
                doc = inv * (dn - n * jnp.mean(dn * n, axis=-1, keepdims=True))
                do = doc - jnp.mean(doc, axis=-1, keepdims=True)

                qr = _rotate(rq_ref[rows, cols], cosv, sinv) * scale_r
                kr = _rotate(rk_ref[rows, cols], cosv, sinv)
                vb = rv_ref[rows, cols].astype(BF16)
                qb, kb, dob = qr.astype(BF16), kr.astype(BF16), do.astype(BF16)
                p = _dot(qb, kb, NT) * k["dec"][h]
                rp = rst_ref[j, cols, :].astype(BF16)
                dr = dr_scr[cols, :]
                drb = dr.astype(BF16)
                dpb = (_dot(dob, vb, NT) * k["dec"][h]).astype(BF16)
                dqr = _dot(dpb, kb) + _dot(dob, rp, NT) * k["qd"][h]
                dkr = _dot(dpb, qb, TN) + _dot(vb, drb, NT) * k["kd"][h]
                dv = _dot(p.astype(BF16), dob, TN) + _dot((kr * k["kd"][h]).astype(BF16), drb)
                dr_scr[cols, :] = k["g64"][h] * dr + _dot((qr * k["qd"][h]).astype(BF16), dob, TN)
                dproj_ref[rows, OFF_RQ + h * RET_D:OFF_RQ + (h + 1) * RET_D] = (
                    _rotate_t(dqr, cosv, sinv) * scale_r).astype(BF16)
                dproj_ref[rows, OFF_RK + h * RET_D:OFF_RK + (h + 1) * RET_D] = _rotate_t(dkr, cosv, sinv).astype(BF16)
                dproj_ref[rows, OFF_RV + h * RET_D:OFF_RV + (h + 1) * RET_D] = dv.astype(BF16)

            glr = glr_ref[rows, :]
            z, b, bl, ep, em = _gla_gates(glr, gw_ref[...], gb_ref[...], k["ltri"])
            qs = gq_ref[rows, :] * scale_g
            kk = gk_ref[rows, :]
            eb = jnp.exp(b)
            ekb = jnp.exp(bl - b)
            ebl = jnp.exp(bl)
            ql, qu, kl, ku = qs * ep, qs * em, kk * em, kk * ep
            qg, kg = qs * eb, kk * ekb
            qlm = _stack_heads(ql, k["hmask"]).astype(BF16)
            qum = _stack_heads(qu, k["hmask"]).astype(BF16)
            klb, kub = kl.astype(BF16), ku.astype(BF16)
            a_all = jnp.where(k["lower"], _dot(qlm, klb, NT), _dot(qum, kub, NT)).astype(BF16)
            st = sst_ref[j]
            stb = st.astype(BF16)
            ds = ds_scr[...]
            dsb = ds.astype(BF16)
            ds_new = ds * ebl
            da_parts = []
            dqg = jnp.zeros((CHUNK, GLA_KW), F32)
            dkg = jnp.zeros((CHUNK, GLA_KW), F32)
            for h in range(GLA_HEADS):
                cols = slice(h * GLA_DV, (h + 1) * GLA_DV)
                hr = slice(h * CHUNK, (h + 1) * CHUNK)
                ocols = slice(RET_HEADS * RET_D + h * GLA_DV, RET_HEADS * RET_D + (h + 1) * GLA_DV)
                o = oraw_ref[rows, ocols]
                g = gg_ref[rows, cols]
                w = gnw_ref[:, cols]
                dout = dmix_ref[rows, ocols]
                inv = lax.rsqrt(jnp.mean(o * o, axis=-1, keepdims=True) + LN_EPS)
                n = o * inv
                sg = _sigmoid(g)
                sil = g * sg
                dn = dout * w * sil
                dvec_ref[1:2, cols] += jnp.sum(dout * n * sil, axis=0, keepdims=True)
                dproj_ref[rows, OFF_GG + h * GLA_DV:OFF_GG + (h + 1) * GLA_DV] = (
                    dout * n * w * (sg * (1.0 + g * (1.0 - sg)))).astype(BF16)
                dob = (inv * (dn - n * jnp.mean(dn * n, axis=-1, keepdims=True))).astype(BF16)
                vb = gv_ref[rows, cols].astype(BF16)
                mh = k["hmask"][h]
                da_parts.append(_dot(dob, vb, NT))
                dv = _dot(a_all[hr, :], dob, TN) + _dot((kg * mh).astype(BF16), dsb, NT)
                dproj_ref[rows, OFF_GV + h * GLA_DV:OFF_GV + (h + 1) * GLA_DV] = dv.astype(BF16)
                dkg = dkg + mh * _dot(vb, dsb)
                dqg = dqg + mh * _dot(dob, stb)
                ds_new = ds_new + _dot(dob, (qg * mh).astype(BF16), TN)
            da_all = jnp.concatenate(da_parts, axis=0)
            dal = jnp.where(k["lower"], da_all, 0.0).astype(BF16)
            dau = jnp.where(k["lower"], 0.0, da_all).astype(BF16)
            dqlm = _dot(dal, klb)
            dqum = _dot(dau, kub)
            dql = jnp.zeros((CHUNK, GLA_KW), F32)
            dqu = jnp.zeros((CHUNK, GLA_KW), F32)
            for h in range(GLA_HEADS):
                hr = slice(h * CHUNK, (h + 1) * CHUNK)
                dql = dql + k["hmask"][h] * dqlm[hr, :]
                dqu = dqu + k["hmask"][h] * dqum[hr, :]
            dkl = _dot(dal, qlm, TN)
            dku = _dot(dau, qum, TN)
            dbl = (jnp.sum(dkg * kg, axis=0, keepdims=True)
                   + jnp.sum(ds * st, axis=0, keepdims=True) * ebl)
            ds_scr[...] = ds_new
            dqs = dql * ep + dqu * em + dqg * eb
            dk = dkl * em + dku * ep + dkg * ekb
            db = dql * ql - dkl * kl - dqu * qu + dku * ku + dqg * qg - dkg * kg
            db = db + jnp.where(last_row, dbl, 0.0)
            dla = _dot(k["utri"], db, NN, HIGHEST)
            dz = dla * (1.0 / GATE_TAU) * _sigmoid(-z)
            dvec_ref[2:3, 0:GLA_KW] += jnp.sum(dz, axis=0, keepdims=True)
            dgw_ref[...] += _dot(glr, dz, TN, HIGHEST)
            dproj_ref[rows, OFF_GLR:OFF_GLR + 128] = _dot(dz, gw_ref[...], NT, HIGHEST).astype(BF16)
            dproj_ref[rows, OFF_GQ:OFF_GQ + GLA_KW] = (dqs * scale_g).astype(BF16)
            dproj_ref[rows, OFF_GK:OFF_GK + GLA_KW] = dk.astype(BF16)
            return carry

        lax.fori_loop(0, cps, chunk, 0)

    rev = lambda i: (nsteps - 1 - i, 0)

    def col(width, off):
        return pl.BlockSpec((tc, width), lambda i, o=off // width: (nsteps - 1 - i, o))

    fix = lambda i: (0, 0)
    in_specs = [pl.BlockSpec((tc, D_MODEL), rev),
                col(512, OFF_RQ), col(512, OFF_RK), col(512, OFF_RV), col(512, OFF_RG),
                col(256, OFF_GQ), col(256, OFF_GK), col(512, OFF_GV), col(512, OFF_GG), col(128, OFF_GLR),
                pl.BlockSpec((tc, D_MODEL), rev), pl.BlockSpec((tc, RET_D), rev), pl.BlockSpec((tc, RET_D), rev),
                pl.BlockSpec((cps, RET_HEADS * RET_D, RET_D), lambda i: (nsteps - 1 - i, 0, 0)),
                pl.BlockSpec((cps, GLA_DV, GLA_KW), lambda i: (nsteps - 1 - i, 0, 0)),
                pl.BlockSpec((128, GLA_KW), fix), pl.BlockSpec((1, GLA_KW), fix),
                pl.BlockSpec((1, 512), fix), pl.BlockSpec((1, 512), fix)]
    out_specs = (pl.BlockSpec((tc, D_IN_PAD), rev), pl.BlockSpec((128, GLA_KW), fix), pl.BlockSpec((8, 512), fix))
    out_shape = (jax.ShapeDtypeStruct((t, D_IN_PAD), BF16), jax.ShapeDtypeStruct((128, GLA_KW), F32),
                 jax.ShapeDtypeStruct((8, 512), F32))
    return pl.pallas_call(
        body, name="mixer_bwd", grid=(nsteps,), in_specs=in_specs, out_specs=out_specs, out_shape=out_shape,
        scratch_shapes=[pltpu.VMEM((RET_HEADS * RET_D, RET_D), F32), pltpu.VMEM((GLA_DV, GLA_KW), F32)],
        compiler_params=pltpu.CompilerParams(dimension_semantics=("arbitrary",), vmem_limit_bytes=V7X_VMEM_LIMIT),
    )(dmix, *([proj] * 9), oraw, cos_t, sin_t, rst, sst, gw_pad, gb, rnw, gnw)


def _inproj_bwd(dproj, x2d, dxa, sc1p, w_in_b):
    t = x2d.shape[0]
    tm = min(ROW_TILE, t)

    def body(dp_ref, x_ref, dxa_ref, sc_ref, w_hbm, gx_ref, acc_ref, w_vmem, sem):
        first = pl.program_id(0) == 0
        _load_resident(first, [(w_hbm, w_vmem)], sem)

        @pl.when(first)
        def _():
            acc_ref[...] = jnp.zeros_like(acc_ref)

        du = _dot(dp_ref[...], w_vmem[...], NT)
        xh, rstd = _ln_stats(x_ref[...])
        gx_ref[...] = dxa_ref[...] + _ln_bwd(du * sc_ref[...], xh, rstd)
        acc_ref[0:1, :] += jnp.sum(du * xh, axis=0, keepdims=True)
        acc_ref[1:2, :] += jnp.sum(du, axis=0, keepdims=True)

    row = lambda i: (i, 0)
    fix = lambda i: (0, 0)
    return pl.pallas_call(
        body, name="inproj_bwd", grid=(t // tm,),
        in_specs=[pl.BlockSpec((tm, D_IN_PAD), row), pl.BlockSpec((tm, D_MODEL), row), pl.BlockSpec((tm, D_MODEL), row),
                  pl.BlockSpec((1, D_MODEL), fix), pl.BlockSpec(memory_space=pl.ANY)],
        out_specs=(pl.BlockSpec((tm, D_MODEL), row), pl.BlockSpec((8, D_MODEL), fix)),
        out_shape=(jax.ShapeDtypeStruct((t, D_MODEL), F32), jax.ShapeDtypeStruct((8, D_MODEL), F32)),
        scratch_shapes=[pltpu.VMEM((D_MODEL, D_IN_PAD), BF16), pltpu.SemaphoreType.DMA((1,))],
        compiler_params=pltpu.CompilerParams(dimension_semantics=("arbitrary",), vmem_limit_bytes=V7X_VMEM_LIMIT),
    )(dproj, x2d, dxa, sc1p, w_in_b)


def _adam_math(w, g, m, v):
    m = ADAM_B1 * m + (1.0 - ADAM_B1) * g
    v = ADAM_B2 * v + (1.0 - ADAM_B2) * (g * g)
    m_hat = m / (1.0 - ADAM_B1 ** ADAM_STEP)
    v_hat = v / (1.0 - ADAM_B2 ** ADAM_STEP)
    delta = -ADAM_LR * (m_hat / (jnp.sqrt(v_hat) + ADAM_EPS) + ADAM_WD * w)
    return delta, m, v


def _adamw(w, gparts, m, v, name):
    nparts, rows, cols = gparts.shape
    tr = rows
    for cand in (512, 256, 128, 64, 32, 16, 8):
        if rows % cand == 0:
            tr = cand
            break

    def body(w_ref, g_ref, m_ref, v_ref, go_ref, d_ref, mo_ref, vo_ref):
        g = g_ref[0].astype(F32)
        for p in range(1, nparts):
            g = g + g_ref[p].astype(F32)
        delta, mn, vn = _adam_math(w_ref[...], g, m_ref[...], v_ref[...])
        go_ref[...] = g
        d_ref[...] = delta
        mo_ref[...] = mn
        vo_ref[...] = vn

    blk = pl.BlockSpec((tr, cols), lambda i: (i, 0))
    shp = jax.ShapeDtypeStruct((rows, cols), F32)
    return pl.pallas_call(
        body, name=name, grid=(rows // tr,),
        in_specs=[blk, pl.BlockSpec((nparts, tr, cols), lambda i: (0, i, 0)), blk, blk],
        out_specs=(blk, blk, blk, blk), out_shape=(shp, shp, shp, shp),
        compiler_params=pltpu.CompilerParams(dimension_semantics=("arbitrary",), vmem_limit_bytes=V7X_VMEM_LIMIT),
    )(w, gparts, m, v)


def _small_reduce(gathered, c_all, dmod_cols):
    def body(g_ref, c_ref, dm_ref, sum_ref, gwa_ref):
        s = g_ref[0]
        for p in range(1, N_DEV):
            s = s + g_ref[p]
        sum_ref[...] = s
        cc = c_ref[...]
        gwa_ref[...] = _dot(cc * _sigmoid(cc), dm_ref[...], TN, HIGHEST)

    vm = pl.BlockSpec(memory_space=pltpu.VMEM)
    return pl.pallas_call(
        body, name="small_reduce",
        out_shape=(jax.ShapeDtypeStruct(gathered.shape[1:], F32), jax.ShapeDtypeStruct((D_MODEL, ADA_COLS), F32)),
        in_specs=[vm, vm, vm], out_specs=(vm, vm),
        compiler_params=pltpu.CompilerParams(vmem_limit_bytes=V7X_VMEM_LIMIT),
    )(gathered, c_all, dmod_cols)


def _rows128(v):
    return v.reshape(-1, 128)


def kernel(x, c, w_ada, b_ada, w_in, ret_norm_w, gla_gate_w, gla_gate_b, gla_norm_w, w_out, ln1_w, ln1_b, w_ff1, w_ff2, ln2_w, ln2_b, loss_target, m_w_ada, m_b_ada, m_w_in, m_ret_norm_w, m_gla_gate_w, m_gla_gate_b, m_gla_norm_w, m_w_out, m_ln1_w, m_ln1_b, m_w_ff1, m_w_ff2, m_ln2_w, m_ln2_b, v_w_ada, v_b_ada, v_w_in, v_ret_norm_w, v_gla_gate_w, v_gla_gate_b, v_gla_norm_w, v_w_out, v_ln1_w, v_ln1_b, v_w_ff1, v_w_ff2, v_ln2_w, v_ln2_b):
    t = x.shape[1]
    xi, yi, ci = _my_coords()
    me = 4 * xi + 2 * yi + ci
    x2d = x[0]
    tgt = loss_target[0]

    c_ext = jnp.concatenate([c, gla_gate_w[0].reshape(1, GATE_RANK * GLA_KW // N_DEV)], axis=1)
    b_l = lax.dynamic_slice(b_ada, (0, me * ADA_COLS), (1, ADA_COLS))
    c_all3, mod_all = _adaln_mod(c_ext, w_ada[0], b_l)
    c_all = c_all3[:, 0, :D_MODEL]
    gate_w = c_all3[:, 0, D_MODEL:].reshape(N_DEV, GATE_RANK, GLA_KW // N_DEV)
    gate_w = gate_w.transpose(1, 0, 2).reshape(GATE_RANK, GLA_KW)
    gw_pad = jnp.zeros((128, GLA_KW), F32).at[:GATE_RANK].set(gate_w)
    mod = lax.dynamic_slice(mod_all, (0, me, 0), (N_DEV, 1, ADA_COLS)).reshape(6, D_MODEL)
    shift1, scale1, gate1, shift2, scale2, gate2 = [mod[i:i + 1] for i in range(6)]

    packed = jnp.concatenate([
        w_in[0].astype(BF16).reshape(IN_COLS, D_MODEL),
        w_out[0].astype(BF16),
        w_ff1[0].astype(BF16).reshape(FF_COLS, D_MODEL),
        w_ff2[0].astype(BF16),
        jnp.zeros((PK_ROWS - 1602, D_MODEL), BF16)], axis=0)
    gathered = _wgather(packed)
    w_in_b = gathered[:, PK_IN:PK_OUT].reshape(N_DEV, D_MODEL, IN_COLS).transpose(1, 0, 2).reshape(D_MODEL, D_IN)
    w_in_b = jnp.pad(w_in_b, ((0, 0), (0, D_IN_PAD - D_IN)))
    w_out_b = gathered[:, PK_OUT:PK_FF1].reshape(D_MODEL, D_MODEL)
    w1_b = gathered[:, PK_FF1:PK_FF2].reshape(N_DEV, D_MODEL, FF_COLS).transpose(1, 0, 2).reshape(D_MODEL, D_FF)
    w2_b = gathered[:, PK_FF2:PK_FF2 + FF_COLS].reshape(D_FF, D_MODEL)

    pos = jnp.arange(t, dtype=F32)
    inv = 1.0 / (10000.0 ** jnp.linspace(0.0, 1.0, RET_D // 2, dtype=F32))
    ang = pos[:, None] * inv[None, :]
    cos_t = jnp.concatenate([jnp.cos(ang), jnp.cos(ang)], axis=1)
    sin_t = jnp.concatenate([-jnp.sin(ang), jnp.sin(ang)], axis=1)

    sc1p = 1.0 + scale1
    proj, u = _inproj_fwd(x2d, sc1p, shift1, w_in_b)
    mixed, oraw, rst, sst = _mixer_fwd(proj, cos_t, sin_t, gw_pad, gla_gate_b, ret_norm_w, gla_norm_w)
    vec_f = jnp.concatenate([gate1, 1.0 + scale2, shift2, gate2, ln1_w, ln1_b, ln2_w, ln2_b], axis=0)
    m, x1n, rstd1, u2, a, df, dh2, acc_f = _mid_fwd(mixed, x2d, tgt, vec_f, w_out_b, w1_b, w2_b)

    vec_b = jnp.concatenate([gate1, 1.0 + scale2, ln1_w, ln1_b, jnp.zeros((4, D_MODEL), F32)], axis=0)
    da, dm, dmix, dxa, acc_b = _ffn_bwd(df, a, dh2, x1n, rstd1, m, vec_b, w_out_b, w1_b, w2_b)
    dw2 = _matmul_tn(a, df, 2048, 1024, "tn_dw2", relu_sq=True)
    dw1 = _matmul_tn(u2, da, 1024, 2048, "tn_dw1")
    dwo = _matmul_tn(mixed, dm, 1024, 1024, "tn_dwout")
    dproj, dgw, dvec = _mixer_bwd(dmix, proj, oraw, cos_t, sin_t, rst, sst, gw_pad, gla_gate_b, ret_norm_w, gla_norm_w)
    grad_x, acc_i = _inproj_bwd(dproj, x2d, dxa, sc1p, w_in_b)
    dwi = _matmul_tn(u, dproj, 1024, D_IN_PAD, "tn_dwin")

    dmod = jnp.concatenate([acc_i[1:2], acc_i[0:1], acc_b[4:5], acc_b[1:2], acc_b[0:1], acc_f[2:3]], axis=0)
    loss_part = jnp.sum(acc_f[3])
    small = jnp.concatenate([
        _rows128(dmod), _rows128(dvec[0]), _rows128(dvec[2, :GLA_KW]), _rows128(dvec[1]),
        _rows128(acc_b[2]), _rows128(acc_b[3]), _rows128(acc_f[0]), _rows128(acc_f[1]),
        _rows128(dgw[:GATE_RANK]), jnp.full((1, 128), loss_part, F32),
        jnp.zeros((SM_ROWS - SM_LOSS - 1, 128), F32)], axis=0)
    small_all = _small_gather(small)
    dmod_all = small_all[:, SM_DMOD:SM_RNW].reshape(N_DEV, 6 * D_MODEL)
    dmod_cols = lax.dynamic_slice(dmod_all, (0, me * ADA_COLS), (N_DEV, ADA_COLS))
    ssum, g_w_ada = _small_reduce(small_all, c_all, dmod_cols)

    def seg(lo, hi, shape):
        return ssum[lo:hi].reshape(shape)

    loss = ssum[SM_LOSS, 0]
    g_b_ada = seg(SM_DMOD, SM_RNW, (1, 6 * D_MODEL))
    g_rnw = seg(SM_RNW, SM_GGB, (1, 512))
    g_ggb = seg(SM_GGB, SM_GNW, (1, GLA_KW))
    g_gnw = seg(SM_GNW, SM_LN1W, (1, 512))
    g_l1w = seg(SM_LN1W, SM_LN1B, (1, D_MODEL))
    g_l1b = seg(SM_LN1B, SM_LN2W, (1, D_MODEL))
    g_l2w = seg(SM_LN2W, SM_LN2B, (1, D_MODEL))
    g_l2b = seg(SM_LN2B, SM_GGW, (1, D_MODEL))
    g_ggw_full = seg(SM_GGW, SM_LOSS, (GATE_RANK, GLA_KW))
    g_ggw = lax.dynamic_slice(g_ggw_full, (0, me * (GLA_KW // N_DEV)), (GATE_RANK, GLA_KW // N_DEV))[None]

    small_w = [b_ada, ret_norm_w, gla_gate_b, gla_norm_w, ln1_w, ln1_b, ln2_w, ln2_b, gla_gate_w]
    small_g = [g_b_ada, g_rnw, g_ggb, g_gnw, g_l1w, g_l1b, g_l2w, g_l2b, g_ggw]
    small_m = [m_b_ada, m_ret_norm_w, m_gla_gate_b, m_gla_norm_w, m_ln1_w, m_ln1_b, m_ln2_w, m_ln2_b, m_gla_gate_w]
    small_v = [v_b_ada, v_ret_norm_w, v_gla_gate_b, v_gla_norm_w, v_ln1_w, v_ln1_b, v_ln2_w, v_ln2_b, v_gla_gate_w]

    def pack_small(parts):
        flat = jnp.concatenate([p.reshape(-1) for p in parts])
        return jnp.pad(flat, (0, 96 * 128 - flat.shape[0])).reshape(96, 128)

    _, sd, smn, svn = _adamw(pack_small(small_w), pack_small(small_g)[None], pack_small(small_m), pack_small(small_v),
                             "adamw_small")

    def unpack_small(packed_arr):
        flat = packed_arr.reshape(-1)
        out, off = [], 0
        for p in small_w:
            out.append(flat[off:off + p.size].reshape(p.shape))
            off += p.size
        return out

    d_small, m_small, v_small = unpack_small(sd), unpack_small(smn), unpack_small(svn)

    _, d_w_ada, nm_w_ada, nv_w_ada = _adamw(w_ada[0], g_w_ada[None], m_w_ada[0], v_w_ada[0], "adamw_ada")

    send = jnp.concatenate([
        dwi[:, :D_IN].reshape(D_MODEL, N_DEV, IN_COLS).transpose(1, 0, 2).reshape(N_DEV, IN_COLS, D_MODEL),
        dwo.reshape(N_DEV, OUT_ROWS, D_MODEL),
        dw1.reshape(D_MODEL, N_DEV, FF_COLS).transpose(1, 0, 2).reshape(N_DEV, FF_COLS, D_MODEL),
        dw2.reshape(N_DEV, FF_COLS, D_MODEL),
        jnp.zeros((N_DEV, PK_ROWS - 1602, D_MODEL), BF16)], axis=1)
    recv = _grad_exchange(send)

    def pack_big(wi, wo, w1, w2):
        return jnp.concatenate([wi[0].reshape(IN_COLS, D_MODEL), wo[0], w1[0].reshape(FF_COLS, D_MODEL), w2[0],
                                jnp.zeros((PK_ROWS - 1602, D_MODEL), F32)], axis=0)

    bg, bd, bm, bv = _adamw(pack_big(w_in, w_out, w_ff1, w_ff2), recv, pack_big(m_w_in, m_w_out, m_w_ff1, m_w_ff2),
                            pack_big(v_w_in, v_w_out, v_w_ff1, v_w_ff2), "adamw_big")

    def unpack_big(p):
        return (p[PK_IN:PK_OUT].reshape(1, D_MODEL, IN_COLS), p[PK_OUT:PK_FF1].reshape(1, OUT_ROWS, D_MODEL),
                p[PK_FF1:PK_FF2].reshape(1, D_MODEL, FF_COLS), p[PK_FF2:PK_FF2 + FF_COLS].reshape(1, FF_COLS, D_MODEL))

    g_big, d_big, m_big, v_big = unpack_big(bg), unpack_big(bd), unpack_big(bm), unpack_big(bv)

    def ordered(w_ada_v, small_vals, big_vals):
        b_ada_v, rnw_v, ggb_v, gnw_v, l1w_v, l1b_v, l2w_v, l2b_v, ggw_v = small_vals
        wi_v, wo_v, w1_v, w2_v = big_vals
        return [w_ada_v, b_ada_v, wi_v, rnw_v, ggw_v, ggb_v, gnw_v, wo_v, l1w_v, l1b_v, w1_v, w2_v, l2w_v, l2b_v]

    grads = ordered(g_w_ada[None], small_g, g_big)
    deltas = ordered(d_w_ada[None], d_small, d_big)
    new_m = ordered(nm_w_ada[None], m_small, m_big)
    new_v = ordered(nv_w_ada[None], v_small, v_big)
    return (loss, grad_x[None], *grads, *deltas, *new_m, *new_v)
```

```python
import functools

import numpy as np
import jax
import jax.numpy as jnp
from jax import lax
from jax.experimental import pallas as pl
from jax.experimental.pallas import tpu as pltpu

F32 = jnp.float32
BF16 = jnp.bfloat16
MESH = pl.DeviceIdType.MESH
HIGHEST = lax.Precision.HIGHEST

N_DEV = 8
D_MODEL = 1024
CHUNK = 64
RET_HEADS = 4
RET_D = 128
GLA_HEADS = 4
GLA_DK = 64
GLA_DV = 128
GLA_KW = GLA_HEADS * GLA_DK
GATE_RANK = 16
GATE_TAU = 16.0
D_FF = 4096
LN_EPS = 1e-5
ALPHA = (2.0 * 1) ** 0.25
D_IN = 3600
D_IN_PAD = 3712
ADA_COLS = 6 * D_MODEL // N_DEV
IN_COLS = D_IN // N_DEV
FF_COLS = D_FF // N_DEV
OUT_ROWS = D_MODEL // N_DEV

OFF_RQ, OFF_RK, OFF_RV, OFF_RG = 0, 512, 1024, 1536
OFF_GQ, OFF_GK, OFF_GV, OFF_GG, OFF_GLR = 2048, 2304, 2560, 3072, 3584

ADAM_LR, ADAM_B1, ADAM_B2, ADAM_EPS, ADAM_WD, ADAM_STEP = 0.001, 0.9, 0.999, 1e-08, 0.01, 10

V7X_VMEM_LIMIT = 56 * 1024 * 1024

ROW_TILE = 256
MIX_TILE = 256


def _log_gamma(h):
    return float(np.log(np.float32(1.0) - np.float32(2.0) ** np.float32(-5.0 - h)))


def _my_coords():
    return lax.axis_index("x"), lax.axis_index("y"), lax.axis_index("c")


def _flip(v, bit):
    return 1 - v if bit else v


def _peer(k):
    x, y, c = _my_coords()
    px, py, pc = _flip(x, (k >> 2) & 1), _flip(y, (k >> 1) & 1), _flip(c, k & 1)
    return (px, py, pc), 4 * px + 2 * py + pc


def _dot(a, b, dims=(((1,), (0,)), ((), ())), precision=None):
    return lax.dot_general(a, b, dims, precision=precision, preferred_element_type=F32)


NN = (((1,), (0,)), ((), ()))
NT = (((1,), (1,)), ((), ()))
TN = (((0,), (0,)), ((), ()))


def _sigmoid(x):
    return 1.0 / (1.0 + jnp.exp(-x))


def _ln_stats(x):
    mu = jnp.mean(x, axis=-1, keepdims=True)
    xc = x - mu
    var = jnp.mean(xc * xc, axis=-1, keepdims=True)
    rstd = lax.rsqrt(var + LN_EPS)
    return xc * rstd, rstd


def _ln_bwd(dyh, xh, rstd):
    return rstd * (dyh - jnp.mean(dyh, axis=-1, keepdims=True) - xh * jnp.mean(dyh * xh, axis=-1, keepdims=True))


def _adaln_mod(c_ext, w_ada_l, b_l):
    width = c_ext.shape[1]

    def body(c_ref, w_ref, b_ref, call_ref, mod_ref, s1, r1, s2, r2):
        x, y, c = _my_coords()
        me = 4 * x + 2 * y + c
        call_ref[me] = c_ref[...]
        sends = []
        for k in range(1, N_DEV):
            peer, _ = _peer(k)
            cp = pltpu.make_async_remote_copy(c_ref, call_ref.at[me], s1.at[k - 1], r1.at[k - 1],
                                              device_id=peer, device_id_type=MESH)
            cp.start()
            sends.append(cp)
        for k in range(1, N_DEV):
            peer, pid = _peer(k)
            pltpu.make_async_remote_copy(c_ref, call_ref.at[pid], s1.at[k - 1], r1.at[k - 1],
                                         device_id=peer, device_id_type=MESH).wait_recv()
        for cp in sends:
            cp.wait_send()
        row = lax.broadcasted_iota(jnp.int32, (N_DEV, D_MODEL), 0)
        call = jnp.zeros((N_DEV, D_MODEL), F32)
        for j in range(N_DEV):
            call = jnp.where(row == j, jnp.broadcast_to(call_ref[j][:, :D_MODEL], (N_DEV, D_MODEL)), call)
        sc = call * _sigmoid(call)
        mod = _dot(sc, w_ref[...], NN, HIGHEST) + b_ref[...]
        mod_ref[me] = mod
        sends = []
        for k in range(1, N_DEV):
            peer, _ = _peer(k)
            cp = pltpu.make_async_remote_copy(mod_ref.at[me], mod_ref.at[me], s2.at[k - 1], r2.at[k - 1],
                                              device_id=peer, device_id_type=MESH)
            cp.start()
            sends.append(cp)
        for k in range(1, N_DEV):
            peer, pid = _peer(k)
            pltpu.make_async_remote_copy(mod_ref.at[pid], mod_ref.at[pid], s2.at[k - 1], r2.at[k - 1],
                                         device_id=peer, device_id_type=MESH).wait_recv()
        for cp in sends:
            cp.wait_send()

    vm = pl.BlockSpec(memory_space=pltpu.VMEM)
    return pl.pallas_call(
        body, name="adaln_mod",
        out_shape=(jax.ShapeDtypeStruct((N_DEV, 1, width), F32),
                   jax.ShapeDtypeStruct((N_DEV, N_DEV, ADA_COLS), F32)),
        in_specs=[vm, vm, vm], out_specs=(vm, vm),
        scratch_shapes=[pltpu.SemaphoreType.DMA((N_DEV - 1,))] * 4,
        compiler_params=pltpu.CompilerParams(vmem_limit_bytes=V7X_VMEM_LIMIT),
    )(c_ext, w_ada_l, b_l)


class _TwoLevelGather:
    def __init__(self, x_refs, out_refs, send_sems, recv_sems, local_sems):
        self.x_refs, self.out_refs = x_refs, out_refs
        self.send_sems, self.recv_sems, self.local_sems = send_sems, recv_sems, local_sems
        x, y, c = _my_coords()
        self.c = c
        self.me, self.sibling = (x, y, c), (x, y, 1 - c)
        self.chips = [(1 - x, y), (x, 1 - y), (1 - x, 1 - y)]

    def _copy(self, a, k, block, to, src=None):
        px, py, pc = block
        slab = self.out_refs[a].at[4 * px + 2 * py + pc]
        return pltpu.make_async_remote_copy(
            src_ref=slab if src is None else src, dst_ref=slab,
            send_sem=self.send_sems.at[7 * a + k], recv_sem=self.recv_sems.at[7 * a + k],
            device_id=to, device_id_type=MESH)

    def _mine(self, a):
        px, py, pc = self.me
        return pltpu.make_async_copy(self.x_refs[a], self.out_refs[a].at[4 * px + 2 * py + pc], self.local_sems.at[a])

    def _first(self, a):
        cps = [self._copy(a, 0, self.me, self.sibling, src=self.x_refs[a])]
        cps += [self._copy(a, 1 + j, self.me, (*chip, self.c), src=self.x_refs[a]) for j, chip in enumerate(self.chips)]
        return cps

    def _passed(self, a):
        return [self._copy(a, 4 + j, (*chip, self.c), self.sibling) for j, chip in enumerate(self.chips)]

    def start(self):
        for a in range(len(self.x_refs)):
            self._mine(a).start()
            for cp in self._first(a):
                cp.start()

    def forward(self):
        for a in range(len(self.x_refs)):
            passed = self._passed(a)
            for j, chip in enumerate(self.chips):
                self._copy(a, 1 + j, (*chip, self.c), self.me).wait_recv()
                passed[j].start()

    def finish(self):
        for a in range(len(self.x_refs)):
            self._copy(a, 0, self.sibling, self.me).wait_recv()
            for j, chip in enumerate(self.chips):
                self._copy(a, 4 + j, (*chip, 1 - self.c), self.me).wait_recv()
            for cp in self._first(a) + self._passed(a):
                cp.wait_send()
            self._mine(a).wait()


class _DirectExchange:
    def __init__(self, s_refs, r_refs, send_sems, recv_sems, local_sems):
        self.s_refs, self.r_refs = s_refs, r_refs
        self.send_sems, self.recv_sems, self.local_sems = send_sems, recv_sems, local_sems
        x, y, c = _my_coords()
        self.me = 4 * x + 2 * y + c

    def _mine(self, a):
        return pltpu.make_async_copy(self.s_refs[a].at[self.me], self.r_refs[a].at[self.me], self.local_sems.at[a])

    def _send(self, a, k):
        peer, pid = _peer(k)
        return pltpu.make_async_remote_copy(self.s_refs[a].at[pid], self.r_refs[a].at[self.me],
                                            self.send_sems.at[7 * a + k - 1], self.recv_sems.at[7 * a + k - 1],
                                            device_id=peer, device_id_type=MESH)

    def _recv(self, a, k):
        peer, pid = _peer(k)
        return pltpu.make_async_remote_copy(self.s_refs[a].at[pid], self.r_refs[a].at[pid],
                                            self.send_sems.at[7 * a + k - 1], self.recv_sems.at[7 * a + k - 1],
                                            device_id=peer, device_id_type=MESH)

    def start(self):
        for a in range(len(self.s_refs)):
            self._mine(a).start()
            for k in range(1, N_DEV):
                self._send(a, k).start()

    def finish(self):
        for a in range(len(self.s_refs)):
            for k in range(1, N_DEV):
                self._recv(a, k).wait_recv()
            for k in range(1, N_DEV):
                self._send(a, k).wait_send()
            self._mine(a).wait()


def _wgather(shards):
    n = len(shards)

    def body(*refs):
        g = _TwoLevelGather(refs[:n], refs[n:2 * n], *refs[2 * n:])
        g.start()
        g.forward()
        g.finish()

    anyspace = pl.BlockSpec(memory_space=pl.ANY)
    return pl.pallas_call(
        body, name="wgather",
        out_shape=tuple(jax.ShapeDtypeStruct((N_DEV, *s.shape), s.dtype) for s in shards),
        in_specs=[anyspace] * n, out_specs=tuple([anyspace] * n),
        scratch_shapes=[pltpu.SemaphoreType.DMA((7 * n,)), pltpu.SemaphoreType.DMA((7 * n,)),
                        pltpu.SemaphoreType.DMA((n,))],
    )(*shards)


def _small_gather(vecs):
    n = len(vecs)

    def body(*refs):
        v_refs, out_refs, s_sems, r_sems = refs[:n], refs[n:2 * n], refs[2 * n], refs[2 * n + 1]
        x, y, c = _my_coords()
        me = 4 * x + 2 * y + c
        sends = []
        for a in range(n):
            out_refs[a][me] = v_refs[a][...]
            for k in range(1, N_DEV):
                peer, _ = _peer(k)
                cp = pltpu.make_async_remote_copy(v_refs[a], out_refs[a].at[me], s_sems.at[7 * a + k - 1],
                                                  r_sems.at[7 * a + k - 1], device_id=peer, device_id_type=MESH)
                cp.start()
                sends.append(cp)
        for a in range(n):
            for k in range(1, N_DEV):
                peer, pid = _peer(k)
                pltpu.make_async_remote_copy(v_refs[a], out_refs[a].at[pid], s_sems.at[7 * a + k - 1],
                                             r_sems.at[7 * a + k - 1], device_id=peer, device_id_type=MESH).wait_recv()
        for cp in sends:
            cp.wait_send()

    vm = pl.BlockSpec(memory_space=pltpu.VMEM)
    return pl.pallas_call(
        body, name="small_gather",
        out_shape=tuple(jax.ShapeDtypeStruct((N_DEV, *v.shape), v.dtype) for v in vecs),
        in_specs=[vm] * n, out_specs=tuple([vm] * n),
        scratch_shapes=[pltpu.SemaphoreType.DMA((7 * n,))] * 2,
    )(*vecs)


def _grad_exchange(sends, name):
    n = len(sends)

    def body(*refs):
        ex = _DirectExchange(refs[:n], refs[n:2 * n], *refs[2 * n:])
        ex.start()
        ex.finish()

    anyspace = pl.BlockSpec(memory_space=pl.ANY)
    return pl.pallas_call(
        body, name=name,
        out_shape=tuple(jax.ShapeDtypeStruct(s.shape, s.dtype) for s in sends),
        in_specs=[anyspace] * n, out_specs=tuple([anyspace] * n),
        scratch_shapes=[pltpu.SemaphoreType.DMA((7 * n,)), pltpu.SemaphoreType.DMA((7 * n,)),
                        pltpu.SemaphoreType.DMA((n,))],
    )(*sends)


def _load_resident(step_is_first, pairs, sem):
    @pl.when(step_is_first)
    def _():
        copies = [pltpu.make_async_copy(src, dst, sem.at[i]) for i, (src, dst) in enumerate(pairs)]
        for cp in copies:
            cp.start()
        for cp in copies:
            cp.wait()


def _inproj_fwd(x2d, sc1p, sh1, w_in_b):
    t = x2d.shape[0]
    tm = min(ROW_TILE, t)

    def body(x_ref, sc_ref, sh_ref, w_hbm, proj_ref, u_ref, w_vmem, sem):
        _load_resident(pl.program_id(0) == 0, [(w_hbm, w_vmem)], sem)
        xh, _ = _ln_stats(x_ref[...])
        ub = (xh * sc_ref[...] + sh_ref[...]).astype(BF16)
        u_ref[...] = ub
        proj_ref[...] = _dot(ub, w_vmem[...])

    row = lambda i: (i, 0)
    fix = lambda i: (0, 0)
    return pl.pallas_call(
        body, name="inproj_fwd", grid=(t // tm,),
        in_specs=[pl.BlockSpec((tm, D_MODEL), row), pl.BlockSpec((1, D_MODEL), fix), pl.BlockSpec((1, D_MODEL), fix),
                  pl.BlockSpec(memory_space=pl.ANY)],
        out_specs=(pl.BlockSpec((tm, D_IN_PAD), row), pl.BlockSpec((tm, D_MODEL), row)),
        out_shape=(jax.ShapeDtypeStruct((t, D_IN_PAD), F32), jax.ShapeDtypeStruct((t, D_MODEL), BF16)),
        scratch_shapes=[pltpu.VMEM((D_MODEL, D_IN_PAD), BF16), pltpu.SemaphoreType.DMA((1,))],
        compiler_params=pltpu.CompilerParams(dimension_semantics=("arbitrary",), vmem_limit_bytes=V7X_VMEM_LIMIT),
    )(x2d, sc1p, sh1, w_in_b)


def _mixer_consts():
    r64 = lax.broadcasted_iota(jnp.int32, (CHUNK, CHUNK), 0)
    c64 = lax.broadcasted_iota(jnp.int32, (CHUNK, CHUNK), 1)
    dist = jnp.abs(r64 - c64).astype(F32)
    rowf = lax.broadcasted_iota(jnp.int32, (CHUNK, RET_D), 0).astype(F32)
    dec, qd, kd, g64 = [], [], [], []
    for h in range(RET_HEADS):
        lg = _log_gamma(h)
        dec.append(jnp.exp(lg * dist))
        qd.append(jnp.exp(lg * (rowf + 1.0)))
        kd.append(jnp.exp(lg * (CHUNK - 1.0 - rowf)))
        g64.append(float(np.exp(np.float32(lg) * np.float32(CHUNK))))
    ltri = (c64 <= r64).astype(F32)
    utri = (c64 >= r64).astype(F32)
    lane = lax.broadcasted_iota(jnp.int32, (1, GLA_KW), 1)
    hmask = [((lane >= h * GLA_DK) & (lane < (h + 1) * GLA_DK)).astype(F32) for h in range(GLA_HEADS)]
    rs = lax.broadcasted_iota(jnp.int32, (GLA_HEADS * CHUNK, CHUNK), 0) & (CHUNK - 1)
    cs = lax.broadcasted_iota(jnp.int32, (GLA_HEADS * CHUNK, CHUNK), 1)
    lower = cs <= rs
    return dict(dec=dec, qd=qd, kd=kd, g64=g64, ltri=ltri, utri=utri, hmask=hmask, lower=lower)


def _rotate(v, cosv, sinv):
    return v * cosv + pltpu.roll(v, RET_D // 2, 1) * sinv


def _rotate_t(d, cosv, sinv):
    return d * cosv + pltpu.roll(d * sinv, RET_D // 2, 1)


def _stack_heads(v, hmask):
    return jnp.concatenate([v * hmask[h] for h in range(GLA_HEADS)], axis=0)


def _gla_gates(glr, gw, gb, ltri):
    z = _dot(glr, gw, NN, HIGHEST) + gb
    la = (jnp.minimum(z, 0.0) - jnp.log(1.0 + jnp.exp(-jnp.abs(z)))) * (1.0 / GATE_TAU)
    b = _dot(ltri, la, NN, HIGHEST)
    level = b[CHUNK // 2 - 1:CHUNK // 2, :]
    ep = jnp.exp(jnp.clip(b - level, -80.0, 80.0))
    em = jnp.exp(jnp.clip(level - b, -80.0, 80.0))
    bl = b[CHUNK - 1:CHUNK, :]
    return z, b, bl, ep, em


def _mixer_fwd(proj, cos_t, sin_t, gw_pad, gb, rnw, gnw):
    t = proj.shape[0]
    tc = min(MIX_TILE, t)
    cps = tc // CHUNK
    nch = t // CHUNK
    scale_r = RET_D ** -0.5
    scale_g = GLA_DK ** -0.5

    def body(rq_ref, rk_ref, rv_ref, rg_ref, gq_ref, gk_ref, gv_ref, gg_ref, glr_ref, cos_ref, sin_ref,
             gw_ref, gb_ref, rnw_ref, gnw_ref, mix_ref, oraw_ref, rst_ref, sst_ref, r_scr, s_scr):
        @pl.when(pl.program_id(0) == 0)
        def _():
            r_scr[...] = jnp.zeros_like(r_scr)
            s_scr[...] = jnp.zeros_like(s_scr)

        k = _mixer_consts()

        def chunk(j, carry):
            rows = pl.ds(pl.multiple_of(j * CHUNK, CHUNK), CHUNK)
            cosv, sinv = cos_ref[rows, :], sin_ref[rows, :]
            for h in range(RET_HEADS):
                cols = slice(h * RET_D, (h + 1) * RET_D)
                qr = _rotate(rq_ref[rows, cols], cosv, sinv) * scale_r
                kr = _rotate(rk_ref[rows, cols], cosv, sinv)
                vb = rv_ref[rows, cols].astype(BF16)
                qb, kb = qr.astype(BF16), kr.astype(BF16)
                p = _dot(qb, kb, NT) * k["dec"][h]
                rp = r_scr[cols, :]
                o = _dot(p.astype(BF16), vb) + _dot((qr * k["qd"][h]).astype(BF16), rp.astype(BF16))
                rst_ref[j, cols, :] = rp
                r_scr[cols, :] = k["g64"][h] * rp + _dot((kr * k["kd"][h]).astype(BF16), vb, TN)
                oraw_ref[rows, cols] = o
                oc = o - jnp.mean(o, axis=-1, keepdims=True)
                n = oc * lax.rsqrt(jnp.mean(oc * oc, axis=-1, keepdims=True) + LN_EPS)
                g = rg_ref[rows, cols]
                mix_ref[rows, cols] = (n * rnw_ref[:, cols] * (g * _sigmoid(g))).astype(BF16)

            _, b, bl, ep, em = _gla_gates(glr_ref[rows, :], gw_ref[...], gb_ref[...], k["ltri"])
            qs = gq_ref[rows, :] * scale_g
            kk = gk_ref[rows, :]
            x_all = _dot(_stack_heads(qs * ep, k["hmask"]).astype(BF16), (kk * em).astype(BF16), NT)
            y_all = _dot(_stack_heads(qs * em, k["hmask"]).astype(BF16), (kk * ep).astype(BF16), NT)
            a_all = jnp.where(k["lower"], x_all, y_all).astype(BF16)
            st = s_scr[...]
            oq = _dot(_stack_heads(qs * jnp.exp(b), k["hmask"]).astype(BF16), st.astype(BF16), NT)
            kg = kk * jnp.exp(bl - b)
            sst_ref[j] = st
            st_new = st * jnp.exp(bl)
            for h in range(GLA_HEADS):
                cols = slice(h * GLA_DV, (h + 1) * GLA_DV)
                hr = slice(h * CHUNK, (h + 1) * CHUNK)
                vb = gv_ref[rows, cols].astype(BF16)
                o = _dot(a_all[hr, :], vb) + oq[hr, :]
                st_new = st_new + _dot(vb, (kg * k["hmask"][h]).astype(BF16), TN)
                ocols = slice(RET_HEADS * RET_D + h * GLA_DV, RET_HEADS * RET_D + (h + 1) * GLA_DV)
                oraw_ref[rows, ocols] = o
                n = o * lax.rsqrt(jnp.mean(o * o, axis=-1, keepdims=True) + LN_EPS)
                g = gg_ref[rows, cols]
                mix_ref[rows, ocols] = (n * gnw_ref[:, cols] * (g * _sigmoid(g))).astype(BF16)
            s_scr[...] = st_new
            return carry

        lax.fori_loop(0, cps, chunk, 0)

    def col(width, off):
        return pl.BlockSpec((tc, width), lambda i, o=off // width: (i, o))

    fix = lambda i: (0, 0)
    in_specs = [col(512, OFF_RQ), col(512, OFF_RK), col(512, OFF_RV), col(512, OFF_RG),
                col(256, OFF_GQ), col(256, OFF_GK), col(512, OFF_GV), col(512, OFF_GG), col(128, OFF_GLR),
                pl.BlockSpec((tc, RET_D), lambda i: (i, 0)), pl.BlockSpec((tc, RET_D), lambda i: (i, 0)),
                pl.BlockSpec((128, GLA_KW), fix), pl.BlockSpec((1, GLA_KW), fix),
                pl.BlockSpec((1, 512), fix), pl.BlockSpec((1, 512), fix)]
    out_specs = (pl.BlockSpec((tc, D_MODEL), lambda i: (i, 0)), pl.BlockSpec((tc, D_MODEL), lambda i: (i, 0)),
                 pl.BlockSpec((cps, RET_HEADS * RET_D, RET_D), lambda i: (i, 0, 0)),
                 pl.BlockSpec((cps, GLA_DV, GLA_KW), lambda i: (i, 0, 0)))
    out_shape = (jax.ShapeDtypeStruct((t, D_MODEL), BF16), jax.ShapeDtypeStruct((t, D_MODEL), F32),
                 jax.ShapeDtypeStruct((nch, RET_HEADS * RET_D, RET_D), F32),
                 jax.ShapeDtypeStruct((nch, GLA_DV, GLA_KW), F32))
    return pl.pallas_call(
        body, name="mixer_fwd", grid=(t // tc,), in_specs=in_specs, out_specs=out_specs, out_shape=out_shape,
        scratch_shapes=[pltpu.VMEM((RET_HEADS * RET_D, RET_D), F32), pltpu.VMEM((GLA_DV, GLA_KW), F32)],
        compiler_params=pltpu.CompilerParams(dimension_semantics=("arbitrary",), vmem_limit_bytes=V7X_VMEM_LIMIT),
    )(*([proj] * 9), cos_t, sin_t, gw_pad, gb, rnw, gnw)


def _mid_fwd(mixed, x2d, target, vecs, w_out_b, w1_b, w2_b):
    t = x2d.shape[0]
    tm = min(ROW_TILE, t)

    def body(mix_ref, x_ref, tgt_ref, v_ref, wo_hbm, w1_hbm, w2_hbm,
             m_ref, x1n_ref, rstd_ref, u2_ref, a_ref, df_ref, dh2_ref, acc_ref, wo, w1, w2, sem):
        first = pl.program_id(0) == 0
        _load_resident(first, [(wo_hbm, wo), (w1_hbm, w1), (w2_hbm, w2)], sem)

        @pl.when(first)
        def _():
            acc_ref[...] = jnp.zeros_like(acc_ref)

        gate1, sc2p, sh2, gate2 = v_ref[0:1, :], v_ref[1:2, :], v_ref[2:3, :], v_ref[3:4, :]
        l1w, l1b, l2w, l2b = v_ref[4:5, :], v_ref[5:6, :], v_ref[6:7, :], v_ref[7:8, :]
        m = _dot(mix_ref[...], wo[...])
        m_ref[...] = m.astype(BF16)
        x1n, rstd1 = _ln_stats(ALPHA * x_ref[...] + gate1 * m)
        x1n_ref[...] = x1n
        rstd_ref[...] = rstd1
        x1 = x1n * l1w + l1b
        xh1, _ = _ln_stats(x1)
        u2 = (xh1 * sc2p + sh2).astype(BF16)
        u2_ref[...] = u2
        f = jnp.zeros((tm, D_MODEL), F32)
        for j in range(N_DEV):
            cols = slice(j * FF_COLS, (j + 1) * FF_COLS)
            a = _dot(u2, w1[j])
            a_ref[:, cols] = a.astype(BF16)
            r = jnp.maximum(a, 0.0)
            f = f + _dot((r * r).astype(BF16), w2[cols, :])
        yh, rstd2 = _ln_stats(ALPHA * x1 + gate2 * f)
        e = yh * l2w + l2b - tgt_ref[...]
        dy = e * (1.0 / D_MODEL)
        dh2 = _ln_bwd(dy * l2w, yh, rstd2)
        dh2_ref[...] = dh2
        df_ref[...] = (dh2 * gate2).astype(BF16)
        acc_ref[0:1, :] += jnp.sum(dy * yh, axis=0, keepdims=True)
        acc_ref[1:2, :] += jnp.sum(dy, axis=0, keepdims=True)
        acc_ref[2:3, :] += jnp.sum(dh2 * f, axis=0, keepdims=True)
        acc_ref[3:4, :] += jnp.sum(e * e, axis=0, keepdims=True) * (0.5 / D_MODEL)

    row = lambda i: (i, 0)
    fix = lambda i: (0, 0)
    hbm = pl.BlockSpec(memory_space=pl.ANY)
    return pl.pallas_call(
        body, name="mid_fwd", grid=(t // tm,),
        in_specs=[pl.BlockSpec((tm, D_MODEL), row), pl.BlockSpec((tm, D_MODEL), row), pl.BlockSpec((tm, D_MODEL), row),
                  pl.BlockSpec((8, D_MODEL), fix), hbm, hbm, hbm],
        out_specs=(pl.BlockSpec((tm, D_MODEL), row), pl.BlockSpec((tm, D_MODEL), row), pl.BlockSpec((tm, 1), row),
                   pl.BlockSpec((tm, D_MODEL), row), pl.BlockSpec((tm, D_FF), row), pl.BlockSpec((tm, D_MODEL), row),
                   pl.BlockSpec((tm, D_MODEL), row), pl.BlockSpec((8, D_MODEL), fix)),
        out_shape=(jax.ShapeDtypeStruct((t, D_MODEL), BF16), jax.ShapeDtypeStruct((t, D_MODEL), F32),
                   jax.ShapeDtypeStruct((t, 1), F32), jax.ShapeDtypeStruct((t, D_MODEL), BF16),
                   jax.ShapeDtypeStruct((t, D_FF), BF16), jax.ShapeDtypeStruct((t, D_MODEL), BF16),
                   jax.ShapeDtypeStruct((t, D_MODEL), F32), jax.ShapeDtypeStruct((8, D_MODEL), F32)),
        scratch_shapes=[pltpu.VMEM((D_MODEL, D_MODEL), BF16), pltpu.VMEM((N_DEV, D_MODEL, FF_COLS), BF16),
                        pltpu.VMEM((D_FF, D_MODEL), BF16), pltpu.SemaphoreType.DMA((3,))],
        compiler_params=pltpu.CompilerParams(dimension_semantics=("arbitrary",), vmem_limit_bytes=V7X_VMEM_LIMIT),
    )(mixed, x2d, target, vecs, w_out_b, w1_b, w2_b)


def _ffn_bwd(df, a, dh2, x1n, rstd1, m, vecs, w_out_b, w1_b, w2_b):
    t = x1n.shape[0]
    tm = min(ROW_TILE, t)

    def body(df_ref, a_ref, dh2_ref, x1n_ref, rstd_ref, m_ref, v_ref, wo_hbm, w1_hbm, w2_hbm,
             da_ref, dm_ref, dmix_ref, dxa_ref, acc_ref, wo, w1, w2, sem):
        first = pl.program_id(0) == 0
        _load_resident(first, [(wo_hbm, wo), (w1_hbm, w1), (w2_hbm, w2)], sem)

        @pl.when(first)
        def _():
            acc_ref[...] = jnp.zeros_like(acc_ref)

        gate1, sc2p, l1w, l1b = v_ref[0:1, :], v_ref[1:2, :], v_ref[2:3, :], v_ref[3:4, :]
        df = df_ref[...]
        du2 = jnp.zeros((tm, D_MODEL), F32)
        for j in range(N_DEV):
            cols = slice(j * FF_COLS, (j + 1) * FF_COLS)
            dr2 = _dot(df, w2[cols, :], NT)
            da = (dr2 * (2.0 * jnp.maximum(a_ref[:, cols].astype(F32), 0.0))).astype(BF16)
            da_ref[:, cols] = da
            du2 = du2 + _dot(da, w1[j], NT)
        x1n = x1n_ref[...]
        xh1, rstd0 = _ln_stats(x1n * l1w + l1b)
        dx1 = ALPHA * dh2_ref[...] + _ln_bwd(du2 * sc2p, xh1, rstd0)
        dh1 = _ln_bwd(dx1 * l1w, x1n, rstd_ref[...])
        dxa_ref[...] = ALPHA * dh1
        dm = (dh1 * gate1).astype(BF16)
        dm_ref[...] = dm
        dmix_ref[...] = _dot(dm, wo[...], NT)
        acc_ref[0:1, :] += jnp.sum(du2 * xh1, axis=0, keepdims=True)
        acc_ref[1:2, :] += jnp.sum(du2, axis=0, keepdims=True)
        acc_ref[2:3, :] += jnp.sum(dx1 * x1n, axis=0, keepdims=True)
        acc_ref[3:4, :] += jnp.sum(dx1, axis=0, keepdims=True)
        acc_ref[4:5, :] += jnp.sum(dh1 * m_ref[...].astype(F32), axis=0, keepdims=True)

    row = lambda i: (i, 0)
    fix = lambda i: (0, 0)
    hbm = pl.BlockSpec(memory_space=pl.ANY)
    return pl.pallas_call(
        body, name="ffn_bwd", grid=(t // tm,),
        in_specs=[pl.BlockSpec((tm, D_MODEL), row), pl.BlockSpec((tm, D_FF), row), pl.BlockSpec((tm, D_MODEL), row),
                  pl.BlockSpec((tm, D_MODEL), row), pl.BlockSpec((tm, 1), row), pl.BlockSpec((tm, D_MODEL), row),
                  pl.BlockSpec((8, D_MODEL), fix), hbm, hbm, hbm],
        out_specs=(pl.BlockSpec((tm, D_FF), row), pl.BlockSpec((tm, D_MODEL), row), pl.BlockSpec((tm, D_MODEL), row),
                   pl.BlockSpec((tm, D_MODEL), row), pl.BlockSpec((8, D_MODEL), fix)),
        out_shape=(jax.ShapeDtypeStruct((t, D_FF), BF16), jax.ShapeDtypeStruct((t, D_MODEL), BF16),
                   jax.ShapeDtypeStruct((t, D_MODEL), F32), jax.ShapeDtypeStruct((t, D_MODEL), F32),
                   jax.ShapeDtypeStruct((8, D_MODEL), F32)),
        scratch_shapes=[pltpu.VMEM((D_MODEL, D_MODEL), BF16), pltpu.VMEM((N_DEV, D_MODEL, FF_COLS), BF16),
                        pltpu.VMEM((D_FF, D_MODEL), BF16), pltpu.SemaphoreType.DMA((3,))],
        compiler_params=pltpu.CompilerParams(dimension_semantics=("arbitrary",), vmem_limit_bytes=V7X_VMEM_LIMIT),
    )(df, a, dh2, x1n, rstd1, m, vecs, w_out_b, w1_b, w2_b)


def _matmul_tn(lhs, rhs, tmm, tn, name, relu_sq=False, col_slab=None):
    t, mm = lhs.shape
    nn = rhs.shape[1]
    tk = min(512, t)
    nk = t // tk

    def body(l_ref, r_ref, o_ref, acc):
        kk = pl.program_id(2)

        @pl.when(kk == 0)
        def _():
            acc[...] = jnp.zeros_like(acc)

        l = l_ref[...]
        if relu_sq:
            lf = jnp.maximum(l.astype(F32), 0.0)
            l = (lf * lf).astype(BF16)
        acc[...] += _dot(l, r_ref[...], TN)

        @pl.when(kk == nk - 1)
        def _():
            if col_slab is None:
                o_ref[...] = acc[...].astype(o_ref.dtype)
            else:
                for s in range(tn // col_slab):
                    o_ref[s] = acc[:, s * col_slab:(s + 1) * col_slab].astype(o_ref.dtype)

    if col_slab is None:
        out_spec = pl.BlockSpec((tmm, tn), lambda i, j, k: (i, j))
        out_shape = jax.ShapeDtypeStruct((mm, nn), BF16)
    else:
        out_spec = pl.BlockSpec((tn // col_slab, tmm, col_slab), lambda i, j, k: (j, i, 0))
        out_shape = jax.ShapeDtypeStruct((nn // col_slab, mm, col_slab), BF16)
    return pl.pallas_call(
        body, name=name, grid=(mm // tmm, nn // tn, nk),
        in_specs=[pl.BlockSpec((tk, tmm), lambda i, j, k: (k, i)), pl.BlockSpec((tk, tn), lambda i, j, k: (k, j))],
        out_specs=out_spec,
        out_shape=out_shape,
        scratch_shapes=[pltpu.VMEM((tmm, tn), F32)],
        compiler_params=pltpu.CompilerParams(dimension_semantics=("arbitrary", "arbitrary", "arbitrary"),
                                             vmem_limit_bytes=V7X_VMEM_LIMIT),
    )(lhs, rhs)


def _mixer_bwd(dmix, proj, oraw, cos_t, sin_t, rst, sst, gw_pad, gb, rnw, gnw):
    t = proj.shape[0]
    tc = min(MIX_TILE, t)
    cps = tc // CHUNK
    nsteps = t // tc
    scale_r = RET_D ** -0.5
    scale_g = GLA_DK ** -0.5

    def body(dmix_ref, rq_ref, rk_ref, rv_ref, rg_ref, gq_ref, gk_ref, gv_ref, gg_ref, glr_ref, oraw_ref,
             cos_ref, sin_ref, rst_ref, sst_ref, gw_ref, gb_ref, rnw_ref, gnw_ref,
             dproj_ref, dgw_ref, dvec_ref, dr_scr, ds_scr):
        @pl.when(pl.program_id(0) == 0)
        def _():
            dr_scr[...] = jnp.zeros_like(dr_scr)
            ds_scr[...] = jnp.zeros_like(ds_scr)
            dgw_ref[...] = jnp.zeros_like(dgw_ref)
            dvec_ref[...] = jnp.zeros_like(dvec_ref)

        k = _mixer_consts()
        last_row = lax.broadcasted_iota(jnp.int32, (CHUNK, GLA_KW), 0) == CHUNK - 1

        def chunk(jj, carry):
            j = cps - 1 - jj
            rows = pl.ds(pl.multiple_of(j * CHUNK, CHUNK), CHUNK)
            cosv, sinv = cos_ref[rows, :], sin_ref[rows, :]
            for h in range(RET_HEADS):
                cols = slice(h * RET_D, (h + 1) * RET_D)
                o = oraw_ref[rows, cols]
                g = rg_ref[rows, cols]
                w = rnw_ref[:, cols]
                dout = dmix_ref[rows, cols]
                oc = o - jnp.mean(o, axis=-1, keepdims=True)
                inv = lax.rsqrt(jnp.mean(oc * oc, axis=-1, keepdims=True) + LN_EPS)
                n = oc * inv
                sg = _sigmoid(g)
                sil = g * sg
                dn = dout * w * sil
                dvec_ref[0:1, cols] += jnp.sum(dout * n * sil, axis=0, keepdims=True)
                dproj_ref[rows, OFF_RG + h * RET_D:OFF_RG + (h + 1) * RET_D] = (
                    dout * n * w * (sg * (1.0 + g * (1.0 - sg)))).astype(BF16)
                doc = inv * (dn - n * jnp.mean(dn * n, axis=-1, keepdims=True))
                do = doc - jnp.mean(doc, axis=-1, keepdims=True)

                qr = _rotate(rq_ref[rows, cols], cosv, sinv) * scale_r
                kr = _rotate(rk_ref[rows, cols], cosv, sinv)
                vb = rv_ref[rows, cols].astype(BF16)
                qb, kb, dob = qr.astype(BF16), kr.astype(BF16), do.astype(BF16)
                p = _dot(qb, kb, NT) * k["dec"][h]
                rp = rst_ref[j, cols, :].astype(BF16)
                dr = dr_scr[cols, :]
                drb = dr.astype(BF16)
                dpb = (_dot(dob, vb, NT) * k["dec"][h]).astype(BF16)
                dqr = _dot(dpb, kb) + _dot(dob, rp, NT) * k["qd"][h]
                dkr = _dot(dpb, qb, TN) + _dot(vb, drb, NT) * k["kd"][h]
                dv = _dot(p.astype(BF16), dob, TN) + _dot((kr * k["kd"][h]).astype(BF16), drb)
                dr_scr[cols, :] = k["g64"][h] * dr + _dot((qr * k["qd"][h]).astype(BF16), dob, TN)
                dproj_ref[rows, OFF_RQ + h * RET_D:OFF_RQ + (h + 1) * RET_D] = (
                    _rotate_t(dqr, cosv, sinv) * scale_r).astype(BF16)
                dproj_ref[rows, OFF_RK + h * RET_D:OFF_RK + (h + 1) * RET_D] = _rotate_t(dkr, cosv, sinv).astype(BF16)
                dproj_ref[rows, OFF_RV + h * RET_D:OFF_RV + (h + 1) * RET_D] = dv.astype(BF16)

            glr = glr_ref[rows, :]
            z, b, bl, ep, em = _gla_gates(glr, gw_ref[...], gb_ref[...], k["ltri"])
            qs = gq_ref[rows, :] * scale_g
            kk = gk_ref[rows, :]
            eb = jnp.exp(b)
            ekb = jnp.exp(bl - b)
            ebl = jnp.exp(bl)
            ql, qu, kl, ku = qs * ep, qs * em, kk * em, kk * ep
            qg, kg = qs * eb, kk * ekb
            qlm = _stack_heads(ql, k["hmask"]).astype(BF16)
            qum = _stack_heads(qu, k["hmask"]).astype(BF16)
            klb, kub = kl.astype(BF16), ku.astype(BF16)
            a_all = jnp.where(k["lower"], _dot(qlm, klb, NT), _dot(qum, kub, NT)).astype(BF16)
            st = sst_ref[j]
            stb = st.astype(BF16)
            ds = ds_scr[...]
            dsb = ds.astype(BF16)
            ds_new = ds * ebl
            da_parts = []
            dqg = jnp.zeros((CHUNK, GLA_KW), F32)
            dkg = jnp.zeros((CHUNK, GLA_KW), F32)
            for h in range(GLA_HEADS):
                cols = slice(h * GLA_DV, (h + 1) * GLA_DV)
                hr = slice(h * CHUNK, (h + 1) * CHUNK)
                ocols = slice(RET_HEADS * RET_D + h * GLA_DV, RET_HEADS * RET_D + (h + 1) * GLA_DV)
                o = oraw_ref[rows, ocols]
                g = gg_ref[rows, cols]
                w = gnw_ref[:, cols]
                dout = dmix_ref[rows, ocols]
                inv = lax.rsqrt(jnp.mean(o * o, axis=-1, keepdims=True) + LN_EPS)
                n = o * inv
                sg = _sigmoid(g)
                sil = g * sg
                dn = dout * w * sil
                dvec_ref[1:2, cols] += jnp.sum(dout * n * sil, axis=0, keepdims=True)
                dproj_ref[rows, OFF_GG + h * GLA_DV:OFF_GG + (h + 1) * GLA_DV] = (
                    dout * n * w * (sg * (1.0 + g * (1.0 - sg)))).astype(BF16)
                dob = (inv * (dn - n * jnp.mean(dn * n, axis=-1, keepdims=True))).astype(BF16)
                vb = gv_ref[rows, cols].astype(BF16)
                mh = k["hmask"][h]
                da_parts.append(_dot(dob, vb, NT))
                dv = _dot(a_all[hr, :], dob, TN) + _dot((kg * mh).astype(BF16), dsb, NT)
                dproj_ref[rows, OFF_GV + h * GLA_DV:OFF_GV + (h + 1) * GLA_DV] = dv.astype(BF16)
                dkg = dkg + mh * _dot(vb, dsb)
                dqg = dqg + mh * _dot(dob, stb)
                ds_new = ds_new + _dot(dob, (qg * mh).astype(BF16), TN)
            da_all = jnp.concatenate(da_parts, axis=0)
            dal = jnp.where(k["lower"], da_all, 0.0).astype(BF16)
            dau = jnp.where(k["lower"], 0.0, da_all).astype(BF16)
            dqlm = _dot(dal, klb)
            dqum = _dot(dau, kub)
            dql = jnp.zeros((CHUNK, GLA_KW), F32)
            dqu = jnp.zeros((CHUNK, GLA_KW), F32)
            for h in range(GLA_HEADS):
                hr = slice(h * CHUNK, (h + 1) * CHUNK)
                dql = dql + k["hmask"][h] * dqlm[hr, :]
                dqu = dqu + k["hmask"][h] * dqum[hr, :]
            dkl = _dot(dal, qlm, TN)
            dku = _dot(dau, qum, TN)
            dbl = (jnp.sum(dkg * kg, axis=0, keepdims=True)
                   + jnp.sum(ds * st, axis=0, keepdims=True) * ebl)
            ds_scr[...] = ds_new
            dqs = dql * ep + dqu * em + dqg * eb
            dk = dkl * em + dku * ep + dkg * ekb
            db = dql * ql - dkl * kl - dqu * qu + dku * ku + dqg * qg - dkg * kg
            db = db + jnp.where(last_row, dbl, 0.0)
            dla = _dot(k["utri"], db, NN, HIGHEST)
            dz = dla * (1.0 / GATE_TAU) * _sigmoid(-z)
            dvec_ref[2:3, 0:GLA_KW] += jnp.sum(dz, axis=0, keepdims=True)
            dgw_ref[...] += _dot(glr, dz, TN, HIGHEST)
            dproj_ref[rows, OFF_GLR:OFF_GLR + 128] = _dot(dz, gw_ref[...], NT, HIGHEST).astype(BF16)
            dproj_ref[rows, OFF_GQ:OFF_GQ + GLA_KW] = (dqs * scale_g).astype(BF16)
            dproj_ref[rows, OFF_GK:OFF_GK + GLA_KW] = dk.astype(BF16)
            return carry

        lax.fori_loop(0, cps, chunk, 0)

    rev = lambda i: (nsteps - 1 - i, 0)

    def col(width, off):
        return pl.BlockSpec((tc, width), lambda i, o=off // width: (nsteps - 1 - i, o))

    fix = lambda i: (0, 0)
    in_specs = [pl.BlockSpec((tc, D_MODEL), rev),
                col(512, OFF_RQ), col(512, OFF_RK), col(512, OFF_RV), col(512, OFF_RG),
                col(256, OFF_GQ), col(256, OFF_GK), col(512, OFF_GV), col(512, OFF_GG), col(128, OFF_GLR),
                pl.BlockSpec((tc, D_MODEL), rev), pl.BlockSpec((tc, RET_D), rev), pl.BlockSpec((tc, RET_D), rev),
                pl.BlockSpec((cps, RET_HEADS * RET_D, RET_D), lambda i: (nsteps - 1 - i, 0, 0)),
                pl.BlockSpec((cps, GLA_DV, GLA_KW), lambda i: (nsteps - 1 - i, 0, 0)),
                pl.BlockSpec((128, GLA_KW), fix), pl.BlockSpec((1, GLA_KW), fix),
                pl.BlockSpec((1, 512), fix), pl.BlockSpec((1, 512), fix)]
    out_specs = (pl.BlockSpec((tc, D_IN_PAD), rev), pl.BlockSpec((128, GLA_KW), fix), pl.BlockSpec((8, 512), fix))
    out_shape = (jax.ShapeDtypeStruct((t, D_IN_PAD), BF16), jax.ShapeDtypeStruct((128, GLA_KW), F32),
                 jax.ShapeDtypeStruct((8, 512), F32))
    return pl.pallas_call(
        body, name="mixer_bwd", grid=(nsteps,), in_specs=in_specs, out_specs=out_specs, out_shape=out_shape,
        scratch_shapes=[pltpu.VMEM((RET_HEADS * RET_D, RET_D), F32), pltpu.VMEM((GLA_DV, GLA_KW), F32)],
        compiler_params=pltpu.CompilerParams(dimension_semantics=("arbitrary",), vmem_limit_bytes=V7X_VMEM_LIMIT),
    )(dmix, *([proj] * 9), oraw, cos_t, sin_t, rst, sst, gw_pad, gb, rnw, gnw)


def _inproj_bwd(dproj, x2d, dxa, sc1p, w_in_b):
    t = x2d.shape[0]
    tm = min(ROW_TILE, t)

    def body(dp_ref, x_ref, dxa_ref, sc_ref, w_hbm, gx_ref, acc_ref, w_vmem, sem):
        first = pl.program_id(0) == 0
        _load_resident(first, [(w_hbm, w_vmem)], sem)

        @pl.when(first)
        def _():
            acc_ref[...] = jnp.zeros_like(acc_ref)

        du = _dot(dp_ref[...], w_vmem[...], NT)
        xh, rstd = _ln_stats(x_ref[...])
        gx_ref[...] = dxa_ref[...] + _ln_bwd(du * sc_ref[...], xh, rstd)
        acc_ref[0:1, :] += jnp.sum(du * xh, axis=0, keepdims=True)
        acc_ref[1:2, :] += jnp.sum(du, axis=0, keepdims=True)

    row = lambda i: (i, 0)
    fix = lambda i: (0, 0)
    return pl.pallas_call(
        body, name="inproj_bwd", grid=(t // tm,),
        in_specs=[pl.BlockSpec((tm, D_IN_PAD), row), pl.BlockSpec((tm, D_MODEL), row), pl.BlockSpec((tm, D_MODEL), row),
                  pl.BlockSpec((1, D_MODEL), fix), pl.BlockSpec(memory_space=pl.ANY)],
        out_specs=(pl.BlockSpec((tm, D_MODEL), row), pl.BlockSpec((8, D_MODEL), fix)),
        out_shape=(jax.ShapeDtypeStruct((t, D_MODEL), F32), jax.ShapeDtypeStruct((8, D_MODEL), F32)),
        scratch_shapes=[pltpu.VMEM((D_MODEL, D_IN_PAD), BF16), pltpu.SemaphoreType.DMA((1,))],
        compiler_params=pltpu.CompilerParams(dimension_semantics=("arbitrary",), vmem_limit_bytes=V7X_VMEM_LIMIT),
    )(dproj, x2d, dxa, sc1p, w_in_b)


def _adam_math(w, g, m, v):
    m = ADAM_B1 * m + (1.0 - ADAM_B1) * g
    v = ADAM_B2 * v + (1.0 - ADAM_B2) * (g * g)
    m_hat = m / (1.0 - ADAM_B1 ** ADAM_STEP)
    v_hat = v / (1.0 - ADAM_B2 ** ADAM_STEP)
    delta = -ADAM_LR * (m_hat / (jnp.sqrt(v_hat) + ADAM_EPS) + ADAM_WD * w)
    return delta, m, v


def _adamw(w, gparts, m, v, name):
    nparts, rows, cols = gparts.shape
    tr = rows
    for cand in (512, 256, 128, 64, 32, 16, 8):
        if rows % cand == 0:
            tr = cand
            break

    def body(w_ref, g_ref, m_ref, v_ref, go_ref, d_ref, mo_ref, vo_ref):
        g = g_ref[0].astype(F32)
        for p in range(1, nparts):
            g = g + g_ref[p].astype(F32)
        delta, mn, vn = _adam_math(w_ref[...], g, m_ref[...], v_ref[...])
        go_ref[...] = g
        d_ref[...] = delta
        mo_ref[...] = mn
        vo_ref[...] = vn

    blk = pl.BlockSpec((tr, cols), lambda i: (i, 0))
    shp = jax.ShapeDtypeStruct((rows, cols), F32)
    return pl.pallas_call(
        body, name=name, grid=(rows // tr,),
        in_specs=[blk, pl.BlockSpec((nparts, tr, cols), lambda i: (0, i, 0)), blk, blk],
        out_specs=(blk, blk, blk, blk), out_shape=(shp, shp, shp, shp),
        compiler_params=pltpu.CompilerParams(dimension_semantics=("arbitrary",), vmem_limit_bytes=V7X_VMEM_LIMIT),
    )(w, gparts, m, v)


def _small_reduce(gathered, gathered_gw, c_all, dmod_cols):
    def body(g_ref, gw_ref, c_ref, dm_ref, sum_ref, gwsum_ref, gb_ref, gwa_ref):
        s = g_ref[0]
        sw = gw_ref[0]
        for p in range(1, N_DEV):
            s = s + g_ref[p]
            sw = sw + gw_ref[p]
        sum_ref[...] = s
        gwsum_ref[...] = sw
        for i in range(6):
            gb_ref[:, i * D_MODEL:(i + 1) * D_MODEL] = s[i:i + 1, :]
        cc = c_ref[...]
        gwa_ref[...] = _dot(cc * _sigmoid(cc), dm_ref[...], TN, HIGHEST)

    vm = pl.BlockSpec(memory_space=pltpu.VMEM)
    return pl.pallas_call(
        body, name="small_reduce",
        out_shape=(jax.ShapeDtypeStruct(gathered.shape[1:], F32), jax.ShapeDtypeStruct(gathered_gw.shape[1:], F32),
                   jax.ShapeDtypeStruct((1, 6 * D_MODEL), F32), jax.ShapeDtypeStruct((D_MODEL, ADA_COLS), F32)),
        in_specs=[vm] * 4, out_specs=(vm, vm, vm, vm),
        compiler_params=pltpu.CompilerParams(vmem_limit_bytes=V7X_VMEM_LIMIT),
    )(gathered, gathered_gw, c_all, dmod_cols)


SMR_LN1W, SMR_LN1B, SMR_LN2W, SMR_LN2B, SMR_NORMS, SMR_MISC = 6, 7, 8, 9, 10, 11


def _adamw_small(gsum, g_b_ada, g_ggw, params, moms, vels):
    n = len(params)

    def body(*refs):
        gsum_ref, gb_ref, gw_ref = refs[:3]
        w_refs, m_refs, v_refs = refs[3:3 + n], refs[3 + n:3 + 2 * n], refs[3 + 2 * n:3 + 3 * n]
        outs = refs[3 + 3 * n:]
        g_refs, d_refs, mo_refs, vo_refs = outs[:n - 1], outs[n - 1:2 * n - 1], outs[2 * n - 1:3 * n - 1], outs[3 * n - 1:]
        grads = [gb_ref[...],
                 gsum_ref[SMR_NORMS:SMR_NORMS + 1, 0:512],
                 gsum_ref[SMR_MISC:SMR_MISC + 1, 0:GLA_KW],
                 gsum_ref[SMR_NORMS:SMR_NORMS + 1, 512:1024],
                 gsum_ref[SMR_LN1W:SMR_LN1W + 1, :], gsum_ref[SMR_LN1B:SMR_LN1B + 1, :],
                 gsum_ref[SMR_LN2W:SMR_LN2W + 1, :], gsum_ref[SMR_LN2B:SMR_LN2B + 1, :],
                 gw_ref[...]]
        for i in range(n):
            delta, mn, vn = _adam_math(w_refs[i][...], grads[i], m_refs[i][...], v_refs[i][...])
            if i < n - 1:
                g_refs[i][...] = grads[i]
            d_refs[i][...] = delta
            mo_refs[i][...] = mn
            vo_refs[i][...] = vn

    vm = pl.BlockSpec(memory_space=pltpu.VMEM)
    shapes = [jax.ShapeDtypeStruct(p.shape, F32) for p in params]
    n_in = 3 + 3 * n
    out_shape = tuple(shapes[:n - 1] + shapes * 3)
    return pl.pallas_call(
        body, name="adamw_small", out_shape=out_shape,
        in_specs=[vm] * n_in, out_specs=tuple([vm] * len(out_shape)),
        compiler_params=pltpu.CompilerParams(vmem_limit_bytes=V7X_VMEM_LIMIT),
    )(gsum, g_b_ada, g_ggw, *params, *moms, *vels)


def kernel(x, c, w_ada, b_ada, w_in, ret_norm_w, gla_gate_w, gla_gate_b, gla_norm_w, w_out, ln1_w, ln1_b, w_ff1, w_ff2, ln2_w, ln2_b, loss_target, m_w_ada, m_b_ada, m_w_in, m_ret_norm_w, m_gla_gate_w, m_gla_gate_b, m_gla_norm_w, m_w_out, m_ln1_w, m_ln1_b, m_w_ff1, m_w_ff2, m_ln2_w, m_ln2_b, v_w_ada, v_b_ada, v_w_in, v_ret_norm_w, v_gla_gate_w, v_gla_gate_b, v_gla_norm_w, v_w_out, v_ln1_w, v_ln1_b, v_w_ff1, v_w_ff2, v_ln2_w, v_ln2_b):
    t = x.shape[1]
    xi, yi, ci = _my_coords()
    me = 4 * xi + 2 * yi + ci
    x2d = x[0]
    tgt = loss_target[0]

    c_ext = jnp.concatenate([c, gla_gate_w[0].reshape(1, GATE_RANK * GLA_KW // N_DEV)], axis=1)
    b_l = lax.dynamic_slice(b_ada, (0, me * ADA_COLS), (1, ADA_COLS))
    c_all3, mod_all = _adaln_mod(c_ext, w_ada[0], b_l)
    c_all = c_all3[:, 0, :D_MODEL]
    gate_w = c_all3[:, 0, D_MODEL:].reshape(N_DEV, GATE_RANK, GLA_KW // N_DEV)
    gate_w = gate_w.transpose(1, 0, 2).reshape(GATE_RANK, GLA_KW)
    gw_pad = jnp.zeros((128, GLA_KW), F32).at[:GATE_RANK].set(gate_w)
    mod = lax.dynamic_slice(mod_all, (0, me, 0), (N_DEV, 1, ADA_COLS)).reshape(6, D_MODEL)
    shift1, scale1, gate1, shift2, scale2, gate2 = [mod[i:i + 1] for i in range(6)]

    wi_g, wo_g, w1_b, w2_g = _wgather([w_in[0].astype(BF16), w_out[0].astype(BF16),
                                       w_ff1[0].astype(BF16), w_ff2[0].astype(BF16)])
    w_in_b = jnp.pad(wi_g.transpose(1, 0, 2).reshape(D_MODEL, D_IN), ((0, 0), (0, D_IN_PAD - D_IN)))
    w_out_b = wo_g.reshape(D_MODEL, D_MODEL)
    w2_b = w2_g.reshape(D_FF, D_MODEL)

    pos = jnp.arange(t, dtype=F32)
    inv = 1.0 / (10000.0 ** jnp.linspace(0.0, 1.0, RET_D // 2, dtype=F32))
    ang = pos[:, None] * inv[None, :]
    cos_t = jnp.concatenate([jnp.cos(ang), jnp.cos(ang)], axis=1)
    sin_t = jnp.concatenate([-jnp.sin(ang), jnp.sin(ang)], axis=1)

    sc1p = 1.0 + scale1
    proj, u = _inproj_fwd(x2d, sc1p, shift1, w_in_b)
    mixed, oraw, rst, sst = _mixer_fwd(proj, cos_t, sin_t, gw_pad, gla_gate_b, ret_norm_w, gla_norm_w)
    vec_f = jnp.concatenate([gate1, 1.0 + scale2, shift2, gate2, ln1_w, ln1_b, ln2_w, ln2_b], axis=0)
    m, x1n, rstd1, u2, a, df, dh2, acc_f = _mid_fwd(mixed, x2d, tgt, vec_f, w_out_b, w1_b, w2_b)

    vec_b = jnp.concatenate([gate1, 1.0 + scale2, ln1_w, ln1_b, jnp.zeros((4, D_MODEL), F32)], axis=0)
    da, dm, dmix, dxa, acc_b = _ffn_bwd(df, a, dh2, x1n, rstd1, m, vec_b, w_out_b, w1_b, w2_b)
    dw2 = _matmul_tn(a, df, 2048, 1024, "tn_dw2", relu_sq=True)
    dw1 = _matmul_tn(u2, da, 1024, 2048, "tn_dw1", col_slab=FF_COLS)
    dwo = _matmul_tn(mixed, dm, 1024, 1024, "tn_dwout")
    dproj, dgw, dvec = _mixer_bwd(dmix, proj, oraw, cos_t, sin_t, rst, sst, gw_pad, gla_gate_b, ret_norm_w, gla_norm_w)
    grad_x, acc_i = _inproj_bwd(dproj, x2d, dxa, sc1p, w_in_b)
    dwi = _matmul_tn(u, dproj, 1024, D_IN_PAD, "tn_dwin")

    loss_part = jnp.sum(acc_f[3])
    small = jnp.concatenate([
        acc_i[1:2], acc_i[0:1], acc_b[4:5], acc_b[1:2], acc_b[0:1], acc_f[2:3],
        acc_b[2:3], acc_b[3:4], acc_f[0:1], acc_f[1:2],
        jnp.concatenate([dvec[0:1], dvec[1:2]], axis=1),
        jnp.concatenate([dvec[2:3, :GLA_KW], jnp.full((1, 128), loss_part, F32),
                         jnp.zeros((1, D_MODEL - GLA_KW - 128), F32)], axis=1),
        jnp.zeros((4, D_MODEL), F32)], axis=0)
    small_all, gw_all = _small_gather([small, dgw[:GATE_RANK]])
    dmod_all = small_all[:, :6].reshape(N_DEV, 6 * D_MODEL)
    dmod_cols = lax.dynamic_slice(dmod_all, (0, me * ADA_COLS), (N_DEV, ADA_COLS))
    ssum, gw_sum, g_b_ada, g_w_ada = _small_reduce(small_all, gw_all, c_all, dmod_cols)
    loss = ssum[SMR_MISC, GLA_KW]
    g_ggw = lax.dynamic_slice(gw_sum, (0, me * (GLA_KW // N_DEV)), (GATE_RANK, GLA_KW // N_DEV))[None]

    small_w = [b_ada, ret_norm_w, gla_gate_b, gla_norm_w, ln1_w, ln1_b, ln2_w, ln2_b, gla_gate_w]
    small_m = [m_b_ada, m_ret_norm_w, m_gla_gate_b, m_gla_norm_w, m_ln1_w, m_ln1_b, m_ln2_w, m_ln2_b, m_gla_gate_w]
    small_v = [v_b_ada, v_ret_norm_w, v_gla_gate_b, v_gla_norm_w, v_ln1_w, v_ln1_b, v_ln2_w, v_ln2_b, v_gla_gate_w]
    res = _adamw_small(ssum, g_b_ada, g_ggw, small_w, small_m, small_v)
    small_g = list(res[:8]) + [g_ggw]
    d_small, m_small, v_small = list(res[8:17]), list(res[17:26]), list(res[26:35])

    _, d_w_ada, nm_w_ada, nv_w_ada = _adamw(w_ada[0], g_w_ada[None], m_w_ada[0], v_w_ada[0], "adamw_ada")

    dwi_s = dwi[:, :D_IN].reshape(D_MODEL, N_DEV, IN_COLS).transpose(1, 0, 2)
    r_wi, r_wo, r_w1, r_w2 = _grad_exchange(
        [dwi_s, dwo.reshape(N_DEV, OUT_ROWS, D_MODEL), dw1, dw2.reshape(N_DEV, FF_COLS, D_MODEL)], "grad_exchange")
    big = [_adamw(w[0], r, m_[0], v_[0], nm) for w, r, m_, v_, nm in (
        (w_in, r_wi, m_w_in, v_w_in, "adamw_in"), (w_out, r_wo, m_w_out, v_w_out, "adamw_out"),
        (w_ff1, r_w1, m_w_ff1, v_w_ff1, "adamw_ff1"), (w_ff2, r_w2, m_w_ff2, v_w_ff2, "adamw_ff2"))]
    g_big, d_big, m_big, v_big = [[b[i][None] for b in big] for i in range(4)]

    def ordered(w_ada_v, small_vals, big_vals):
        b_ada_v, rnw_v, ggb_v, gnw_v, l1w_v, l1b_v, l2w_v, l2b_v, ggw_v = small_vals
        wi_v, wo_v, w1_v, w2_v = big_vals
        return [w_ada_v, b_ada_v, wi_v, rnw_v, ggw_v, ggb_v, gnw_v, wo_v, l1w_v, l1b_v, w1_v, w2_v, l2w_v, l2b_v]

    grads = ordered(g_w_ada[None], small_g, g_big)
    deltas = ordered(d_w_ada[None], d_small, d_big)
    new_m = ordered(nm_w_ada[None], m_small, m_big)
    new_v = ordered(nv_w_ada[None], v_small, v_big)
    return (loss, grad_x[None], *grads, *deltas, *new_m, *new_v)
```

```python
import functools

import numpy as np
import jax
import jax.numpy as jnp
from jax import lax
from jax.experimental import pallas as pl
from jax.experimental.pallas import tpu as pltpu

F32 = jnp.float32
BF16 = jnp.bfloat16
MESH = pl.DeviceIdType.MESH
HIGHEST = lax.Precision.HIGHEST

N_DEV = 8
D_MODEL = 1024
CHUNK = 64
RET_HEADS = 4
RET_D = 128
GLA_HEADS = 4
GLA_DK = 64
GLA_DV = 128
GLA_KW = GLA_HEADS * GLA_DK
GATE_RANK = 16
GATE_TAU = 16.0
D_FF = 4096
LN_EPS = 1e-5
ALPHA = (2.0 * 1) ** 0.25
D_IN = 3600
D_IN_PAD = 3712
ADA_COLS = 6 * D_MODEL // N_DEV
IN_COLS = D_IN // N_DEV
FF_COLS = D_FF // N_DEV
OUT_ROWS = D_MODEL // N_DEV

OFF_RQ, OFF_RK, OFF_RV, OFF_RG = 0, 512, 1024, 1536
OFF_GQ, OFF_GK, OFF_GV, OFF_GG, OFF_GLR = 2048, 2304, 2560, 3072, 3584

ADAM_LR, ADAM_B1, ADAM_B2, ADAM_EPS, ADAM_WD, ADAM_STEP = 0.001, 0.9, 0.999, 1e-08, 0.01, 10

V7X_VMEM_LIMIT = 56 * 1024 * 1024

ROW_TILE = 256
MIX_TILE = 256


def _log_gamma(h):
    return float(np.log(np.float32(1.0) - np.float32(2.0) ** np.float32(-5.0 - h)))


def _my_coords():
    return lax.axis_index("x"), lax.axis_index("y"), lax.axis_index("c")


def _flip(v, bit):
    return 1 - v if bit else v


def _peer(k):
    x, y, c = _my_coords()
    px, py, pc = _flip(x, (k >> 2) & 1), _flip(y, (k >> 1) & 1), _flip(c, k & 1)
    return (px, py, pc), 4 * px + 2 * py + pc


def _dot(a, b, dims=(((1,), (0,)), ((), ())), precision=None):
    return lax.dot_general(a, b, dims, precision=precision, preferred_element_type=F32)


NN = (((1,), (0,)), ((), ()))
NT = (((1,), (1,)), ((), ()))
TN = (((0,), (0,)), ((), ()))


def _sigmoid(x):
    return 1.0 / (1.0 + jnp.exp(-x))


def _ln_stats(x):
    mu = jnp.mean(x, axis=-1, keepdims=True)
    xc = x - mu
    var = jnp.mean(xc * xc, axis=-1, keepdims=True)
    rstd = lax.rsqrt(var + LN_EPS)
    return xc * rstd, rstd


def _ln_bwd(dyh, xh, rstd):
    return rstd * (dyh - jnp.mean(dyh, axis=-1, keepdims=True) - xh * jnp.mean(dyh * xh, axis=-1, keepdims=True))


def _adaln_mod(c_ext, w_ada_l, b_l, w_in_l):
    width = c_ext.shape[1]

    def body(c_ref, w_ref, b_ref, wi_ref, call_ref, mod_ref, wig_ref, s1, r1, s2, r2, gs, gr, gl):
        gather = _TwoLevelGather([wi_ref], [wig_ref], gs, gr, gl)
        gather.start()
        x, y, c = _my_coords()
        me = 4 * x + 2 * y + c
        call_ref[me] = c_ref[...]
        sends = []
        for k in range(1, N_DEV):
            peer, _ = _peer(k)
            cp = pltpu.make_async_remote_copy(c_ref, call_ref.at[me], s1.at[k - 1], r1.at[k - 1],
                                              device_id=peer, device_id_type=MESH)
            cp.start()
            sends.append(cp)
        for k in range(1, N_DEV):
            peer, pid = _peer(k)
            pltpu.make_async_remote_copy(c_ref, call_ref.at[pid], s1.at[k - 1], r1.at[k - 1],
                                         device_id=peer, device_id_type=MESH).wait_recv()
        for cp in sends:
            cp.wait_send()
        row = lax.broadcasted_iota(jnp.int32, (N_DEV, D_MODEL), 0)
        call = jnp.zeros((N_DEV, D_MODEL), F32)
        for j in range(N_DEV):
            call = jnp.where(row == j, jnp.broadcast_to(call_ref[j][:, :D_MODEL], (N_DEV, D_MODEL)), call)
        sc = call * _sigmoid(call)
        mod = _dot(sc, w_ref[...], NN, HIGHEST) + b_ref[...]
        mod_ref[me] = mod
        sends = []
        for k in range(1, N_DEV):
            peer, _ = _peer(k)
            cp = pltpu.make_async_remote_copy(mod_ref.at[me], mod_ref.at[me], s2.at[k - 1], r2.at[k - 1],
                                              device_id=peer, device_id_type=MESH)
            cp.start()
            sends.append(cp)
        for k in range(1, N_DEV):
            peer, pid = _peer(k)
            pltpu.make_async_remote_copy(mod_ref.at[pid], mod_ref.at[pid], s2.at[k - 1], r2.at[k - 1],
                                         device_id=peer, device_id_type=MESH).wait_recv()
        for cp in sends:
            cp.wait_send()
        gather.forward()
        gather.finish()

    vm = pl.BlockSpec(memory_space=pltpu.VMEM)
    hbm = pl.BlockSpec(memory_space=pl.ANY)
    return pl.pallas_call(
        body, name="adaln_mod",
        out_shape=(jax.ShapeDtypeStruct((N_DEV, 1, width), F32),
                   jax.ShapeDtypeStruct((N_DEV, N_DEV, ADA_COLS), F32),
                   jax.ShapeDtypeStruct((N_DEV, *w_in_l.shape), w_in_l.dtype)),
        in_specs=[vm, vm, vm, hbm], out_specs=(vm, vm, hbm),
        scratch_shapes=[pltpu.SemaphoreType.DMA((N_DEV - 1,))] * 4
        + [pltpu.SemaphoreType.DMA((7,)), pltpu.SemaphoreType.DMA((7,)), pltpu.SemaphoreType.DMA((1,))],
        compiler_params=pltpu.CompilerParams(vmem_limit_bytes=V7X_VMEM_LIMIT),
    )(c_ext, w_ada_l, b_l, w_in_l)


class _TwoLevelGather:
    def __init__(self, x_refs, out_refs, send_sems, recv_sems, local_sems):
        self.x_refs, self.out_refs = x_refs, out_refs
        self.send_sems, self.recv_sems, self.local_sems = send_sems, recv_sems, local_sems
        x, y, c = _my_coords()
        self.c = c
        self.me, self.sibling = (x, y, c), (x, y, 1 - c)
        self.chips = [(1 - x, y), (x, 1 - y), (1 - x, 1 - y)]

    def _copy(self, a, k, block, to, src=None):
        px, py, pc = block
        slab = self.out_refs[a].at[4 * px + 2 * py + pc]
        return pltpu.make_async_remote_copy(
            src_ref=slab if src is None else src, dst_ref=slab,
            send_sem=self.send_sems.at[7 * a + k], recv_sem=self.recv_sems.at[7 * a + k],
            device_id=to, device_id_type=MESH)

    def _mine(self, a):
        px, py, pc = self.me
        return pltpu.make_async_copy(self.x_refs[a], self.out_refs[a].at[4 * px + 2 * py + pc], self.local_sems.at[a])

    def _first(self, a):
        cps = [self._copy(a, 0, self.me, self.sibling, src=self.x_refs[a])]
        cps += [self._copy(a, 1 + j, self.me, (*chip, self.c), src=self.x_refs[a]) for j, chip in enumerate(self.chips)]
        return cps

    def _passed(self, a):
        return [self._copy(a, 4 + j, (*chip, self.c), self.sibling) for j, chip in enumerate(self.chips)]

    def start(self):
        for a in range(len(self.x_refs)):
            self._mine(a).start()
            for cp in self._first(a):
                cp.start()

    def forward(self):
        for a in range(len(self.x_refs)):
            passed = self._passed(a)
            for j, chip in enumerate(self.chips):
                self._copy(a, 1 + j, (*chip, self.c), self.me).wait_recv()
                passed[j].start()

    def finish(self):
        for a in range(len(self.x_refs)):
            self._copy(a, 0, self.sibling, self.me).wait_recv()
            for j, chip in enumerate(self.chips):
                self._copy(a, 4 + j, (*chip, 1 - self.c), self.me).wait_recv()
            for cp in self._first(a) + self._passed(a):
                cp.wait_send()
            self._mine(a).wait()


class _DirectExchange:
    def __init__(self, s_refs, r_refs, send_sems, recv_sems, local_sems):
        self.s_refs, self.r_refs = s_refs, r_refs
        self.send_sems, self.recv_sems, self.local_sems = send_sems, recv_sems, local_sems
        x, y, c = _my_coords()
        self.me = 4 * x + 2 * y + c

    def _mine(self, a):
        return pltpu.make_async_copy(self.s_refs[a].at[self.me], self.r_refs[a].at[self.me], self.local_sems.at[a])

    def _send(self, a, k):
        peer, pid = _peer(k)
        return pltpu.make_async_remote_copy(self.s_refs[a].at[pid], self.r_refs[a].at[self.me],
                                            self.send_sems.at[7 * a + k - 1], self.recv_sems.at[7 * a + k - 1],
                                            device_id=peer, device_id_type=MESH)

    def _recv(self, a, k):
        peer, pid = _peer(k)
        return pltpu.make_async_remote_copy(self.s_refs[a].at[pid], self.r_refs[a].at[pid],
                                            self.send_sems.at[7 * a + k - 1], self.recv_sems.at[7 * a + k - 1],
                                            device_id=peer, device_id_type=MESH)

    def start(self):
        for a in range(len(self.s_refs)):
            self._mine(a).start()
            for k in range(1, N_DEV):
                self._send(a, k).start()

    def finish(self):
        for a in range(len(self.s_refs)):
            for k in range(1, N_DEV):
                self._recv(a, k).wait_recv()
            for k in range(1, N_DEV):
                self._send(a, k).wait_send()
            self._mine(a).wait()


def _small_gather(vecs):
    n = len(vecs)

    def body(*refs):
        v_refs, out_refs, s_sems, r_sems = refs[:n], refs[n:2 * n], refs[2 * n], refs[2 * n + 1]
        x, y, c = _my_coords()
        me = 4 * x + 2 * y + c
        sends = []
        for a in range(n):
            out_refs[a][me] = v_refs[a][...]
            for k in range(1, N_DEV):
                peer, _ = _peer(k)
                cp = pltpu.make_async_remote_copy(v_refs[a], out_refs[a].at[me], s_sems.at[7 * a + k - 1],
                                                  r_sems.at[7 * a + k - 1], device_id=peer, device_id_type=MESH)
                cp.start()
                sends.append(cp)
        for a in range(n):
            for k in range(1, N_DEV):
                peer, pid = _peer(k)
                pltpu.make_async_remote_copy(v_refs[a], out_refs[a].at[pid], s_sems.at[7 * a + k - 1],
                                             r_sems.at[7 * a + k - 1], device_id=peer, device_id_type=MESH).wait_recv()
        for cp in sends:
            cp.wait_send()

    vm = pl.BlockSpec(memory_space=pltpu.VMEM)
    return pl.pallas_call(
        body, name="small_gather",
        out_shape=tuple(jax.ShapeDtypeStruct((N_DEV, *v.shape), v.dtype) for v in vecs),
        in_specs=[vm] * n, out_specs=tuple([vm] * n),
        scratch_shapes=[pltpu.SemaphoreType.DMA((7 * n,))] * 2,
    )(*vecs)


def _load_resident(step_is_first, pairs, sem):
    @pl.when(step_is_first)
    def _():
        copies = [pltpu.make_async_copy(src, dst, sem.at[i]) for i, (src, dst) in enumerate(pairs)]
        for cp in copies:
            cp.start()
        for cp in copies:
            cp.wait()


def _inproj_fwd(x2d, sc1p, sh1, w_in_b):
    t = x2d.shape[0]
    tm = min(ROW_TILE, t)

    def body(x_ref, sc_ref, sh_ref, w_hbm, proj_ref, u_ref, w_vmem, sem):
        _load_resident(pl.program_id(0) == 0, [(w_hbm, w_vmem)], sem)
        xh, _ = _ln_stats(x_ref[...])
        ub = (xh * sc_ref[...] + sh_ref[...]).astype(BF16)
        u_ref[...] = ub
        proj_ref[...] = _dot(ub, w_vmem[...])

    row = lambda i: (i, 0)
    fix = lambda i: (0, 0)
    return pl.pallas_call(
        body, name="inproj_fwd", grid=(t // tm,),
        in_specs=[pl.BlockSpec((tm, D_MODEL), row), pl.BlockSpec((1, D_MODEL), fix), pl.BlockSpec((1, D_MODEL), fix),
                  pl.BlockSpec(memory_space=pl.ANY)],
        out_specs=(pl.BlockSpec((tm, D_IN_PAD), row), pl.BlockSpec((tm, D_MODEL), row)),
        out_shape=(jax.ShapeDtypeStruct((t, D_IN_PAD), F32), jax.ShapeDtypeStruct((t, D_MODEL), BF16)),
        scratch_shapes=[pltpu.VMEM((D_MODEL, D_IN_PAD), BF16), pltpu.SemaphoreType.DMA((1,))],
        compiler_params=pltpu.CompilerParams(dimension_semantics=("arbitrary",), vmem_limit_bytes=V7X_VMEM_LIMIT),
    )(x2d, sc1p, sh1, w_in_b)


def _mixer_consts():
    r64 = lax.broadcasted_iota(jnp.int32, (CHUNK, CHUNK), 0)
    c64 = lax.broadcasted_iota(jnp.int32, (CHUNK, CHUNK), 1)
    dist = jnp.abs(r64 - c64).astype(F32)
    rowf = lax.broadcasted_iota(jnp.int32, (CHUNK, RET_D), 0).astype(F32)
    dec, qd, kd, g64 = [], [], [], []
    for h in range(RET_HEADS):
        lg = _log_gamma(h)
        dec.append(jnp.exp(lg * dist))
        qd.append(jnp.exp(lg * (rowf + 1.0)))
        kd.append(jnp.exp(lg * (CHUNK - 1.0 - rowf)))
        g64.append(float(np.exp(np.float32(lg) * np.float32(CHUNK))))
    ltri = (c64 <= r64).astype(F32)
    utri = (c64 >= r64).astype(F32)
    lane = lax.broadcasted_iota(jnp.int32, (1, GLA_KW), 1)
    hmask = [((lane >= h * GLA_DK) & (lane < (h + 1) * GLA_DK)).astype(F32) for h in range(GLA_HEADS)]
    rs = lax.broadcasted_iota(jnp.int32, (GLA_HEADS * CHUNK, CHUNK), 0) & (CHUNK - 1)
    cs = lax.broadcasted_iota(jnp.int32, (GLA_HEADS * CHUNK, CHUNK), 1)
    lower = cs <= rs
    return dict(dec=dec, qd=qd, kd=kd, g64=g64, ltri=ltri, utri=utri, hmask=hmask, lower=lower)


def _rotate(v, cosv, sinv):
    return v * cosv + pltpu.roll(v, RET_D // 2, 1) * sinv


def _rotate_t(d, cosv, sinv):
    return d * cosv + pltpu.roll(d * sinv, RET_D // 2, 1)


def _stack_heads(v, hmask):
    return jnp.concatenate([v * hmask[h] for h in range(GLA_HEADS)], axis=0)


def _gla_gates(glr, gw, gb, ltri):
    z = _dot(glr, gw, NN, HIGHEST) + gb
    la = (jnp.minimum(z, 0.0) - jnp.log(1.0 + jnp.exp(-jnp.abs(z)))) * (1.0 / GATE_TAU)
    b = _dot(ltri, la, NN, HIGHEST)
    level = b[CHUNK // 2 - 1:CHUNK // 2, :]
    ep = jnp.exp(jnp.clip(b - level, -80.0, 80.0))
    em = jnp.exp(jnp.clip(level - b, -80.0, 80.0))
    bl = b[CHUNK - 1:CHUNK, :]
    return z, b, bl, ep, em


def _mixer_fwd(proj, cos_t, sin_t, gw_pad, gb, rnw, gnw, shards):
    t = proj.shape[0]
    tc = min(MIX_TILE, t)
    cps = tc // CHUNK
    nch = t // CHUNK
    nsteps = t // tc
    fwd_step = (3 * nsteps) // 4
    ns = len(shards)
    scale_r = RET_D ** -0.5
    scale_g = GLA_DK ** -0.5

    def body(rq_ref, rk_ref, rv_ref, rg_ref, gq_ref, gk_ref, gv_ref, gg_ref, glr_ref, cos_ref, sin_ref,
             gw_ref, gb_ref, rnw_ref, gnw_ref, *rest):
        shard_refs, rest = rest[:ns], rest[ns:]
        mix_ref, oraw_ref, rst_ref, sst_ref = rest[:4]
        gathered_refs, rest = rest[4:4 + ns], rest[4 + ns:]
        r_scr, s_scr, gs, gr, gl = rest
        step = pl.program_id(0)

        @pl.when(step == 0)
        def _():
            r_scr[...] = jnp.zeros_like(r_scr)
            s_scr[...] = jnp.zeros_like(s_scr)
            _TwoLevelGather(shard_refs, gathered_refs, gs, gr, gl).start()

        if fwd_step != nsteps - 1:
            @pl.when(step == fwd_step)
            def _():
                _TwoLevelGather(shard_refs, gathered_refs, gs, gr, gl).forward()

        k = _mixer_consts()

        def chunk(j, carry):
            rows = pl.ds(pl.multiple_of(j * CHUNK, CHUNK), CHUNK)
            cosv, sinv = cos_ref[rows, :], sin_ref[rows, :]
            for h in range(RET_HEADS):
                cols = slice(h * RET_D, (h + 1) * RET_D)
                qr = _rotate(rq_ref[rows, cols], cosv, sinv) * scale_r
                kr = _rotate(rk_ref[rows, cols], cosv, sinv)
                vb = rv_ref[rows, cols].astype(BF16)
                qb, kb = qr.astype(BF16), kr.astype(BF16)
                p = _dot(qb, kb, NT) * k["dec"][h]
                rp = r_scr[cols, :]
                o = _dot(p.astype(BF16), vb) + _dot((qr * k["qd"][h]).astype(BF16), rp.astype(BF16))
                rst_ref[j, cols, :] = rp
                r_scr[cols, :] = k["g64"][h] * rp + _dot((kr * k["kd"][h]).astype(BF16), vb, TN)
                oraw_ref[rows, cols] = o
                oc = o - jnp.mean(o, axis=-1, keepdims=True)
                n = oc * lax.rsqrt(jnp.mean(oc * oc, axis=-1, keepdims=True) + LN_EPS)
                g = rg_ref[rows, cols]
                mix_ref[rows, cols] = (n * rnw_ref[:, cols] * (g * _sigmoid(g))).astype(BF16)

            _, b, bl, ep, em = _gla_gates(glr_ref[rows, :], gw_ref[...], gb_ref[...], k["ltri"])
            qs = gq_ref[rows, :] * scale_g
            kk = gk_ref[rows, :]
            x_all = _dot(_stack_heads(qs * ep, k["hmask"]).astype(BF16), (kk * em).astype(BF16), NT)
            y_all = _dot(_stack_heads(qs * em, k["hmask"]).astype(BF16), (kk * ep).astype(BF16), NT)
            a_all = jnp.where(k["lower"], x_all, y_all).astype(BF16)
            st = s_scr[...]
            oq = _dot(_stack_heads(qs * jnp.exp(b), k["hmask"]).astype(BF16), st.astype(BF16), NT)
            kg = kk * jnp.exp(bl - b)
            sst_ref[j] = st
            st_new = st * jnp.exp(bl)
            for h in range(GLA_HEADS):
                cols = slice(h * GLA_DV, (h + 1) * GLA_DV)
                hr = slice(h * CHUNK, (h + 1) * CHUNK)
                vb = gv_ref[rows, cols].astype(BF16)
                o = _dot(a_all[hr, :], vb) + oq[hr, :]
                st_new = st_new + _dot(vb, (kg * k["hmask"][h]).astype(BF16), TN)
                ocols = slice(RET_HEADS * RET_D + h * GLA_DV, RET_HEADS * RET_D + (h + 1) * GLA_DV)
                oraw_ref[rows, ocols] = o
                n = o * lax.rsqrt(jnp.mean(o * o, axis=-1, keepdims=True) + LN_EPS)
                g = gg_ref[rows, cols]
                mix_ref[rows, ocols] = (n * gnw_ref[:, cols] * (g * _sigmoid(g))).astype(BF16)
            s_scr[...] = st_new
            return carry

        lax.fori_loop(0, cps, chunk, 0)

        @pl.when(step == nsteps - 1)
        def _():
            gather = _TwoLevelGather(shard_refs, gathered_refs, gs, gr, gl)
            if fwd_step == nsteps - 1:
                gather.forward()
            gather.finish()

    def col(width, off):
        return pl.BlockSpec((tc, width), lambda i, o=off // width: (i, o))

    fix = lambda i: (0, 0)
    hbm = pl.BlockSpec(memory_space=pl.ANY)
    in_specs = [col(512, OFF_RQ), col(512, OFF_RK), col(512, OFF_RV), col(512, OFF_RG),
                col(256, OFF_GQ), col(256, OFF_GK), col(512, OFF_GV), col(512, OFF_GG), col(128, OFF_GLR),
                pl.BlockSpec((tc, RET_D), lambda i: (i, 0)), pl.BlockSpec((tc, RET_D), lambda i: (i, 0)),
                pl.BlockSpec((128, GLA_KW), fix), pl.BlockSpec((1, GLA_KW), fix),
                pl.BlockSpec((1, 512), fix), pl.BlockSpec((1, 512), fix)] + [hbm] * ns
    out_specs = (pl.BlockSpec((tc, D_MODEL), lambda i: (i, 0)), pl.BlockSpec((tc, D_MODEL), lambda i: (i, 0)),
                 pl.BlockSpec((cps, RET_HEADS * RET_D, RET_D), lambda i: (i, 0, 0)),
                 pl.BlockSpec((cps, GLA_DV, GLA_KW), lambda i: (i, 0, 0))) + tuple([hbm] * ns)
    out_shape = (jax.ShapeDtypeStruct((t, D_MODEL), BF16), jax.ShapeDtypeStruct((t, D_MODEL), F32),
                 jax.ShapeDtypeStruct((nch, RET_HEADS * RET_D, RET_D), F32),
                 jax.ShapeDtypeStruct((nch, GLA_DV, GLA_KW), F32)) + tuple(
                     jax.ShapeDtypeStruct((N_DEV, *s.shape), s.dtype) for s in shards)
    return pl.pallas_call(
        body, name="mixer_fwd", grid=(nsteps,), in_specs=in_specs, out_specs=out_specs, out_shape=out_shape,
        scratch_shapes=[pltpu.VMEM((RET_HEADS * RET_D, RET_D), F32), pltpu.VMEM((GLA_DV, GLA_KW), F32),
                        pltpu.SemaphoreType.DMA((7 * ns,)), pltpu.SemaphoreType.DMA((7 * ns,)),
                        pltpu.SemaphoreType.DMA((ns,))],
        compiler_params=pltpu.CompilerParams(dimension_semantics=("arbitrary",), vmem_limit_bytes=V7X_VMEM_LIMIT),
    )(*([proj] * 9), cos_t, sin_t, gw_pad, gb, rnw, gnw, *shards)


def _mid_fwd(mixed, x2d, target, vecs, w_out_b, w1_b, w2_b):
    t = x2d.shape[0]
    tm = min(ROW_TILE, t)

    def body(mix_ref, x_ref, tgt_ref, v_ref, wo_hbm, w1_hbm, w2_hbm,
             m_ref, x1n_ref, rstd_ref, u2_ref, a_ref, df_ref, dh2_ref, acc_ref, wo, w1, w2, sem):
        first = pl.program_id(0) == 0
        _load_resident(first, [(wo_hbm, wo), (w1_hbm, w1), (w2_hbm, w2)], sem)

        @pl.when(first)
        def _():
            acc_ref[...] = jnp.zeros_like(acc_ref)

        gate1, sc2p, sh2, gate2 = v_ref[0:1, :], v_ref[1:2, :], v_ref[2:3, :], v_ref[3:4, :]
        l1w, l1b, l2w, l2b = v_ref[4:5, :], v_ref[5:6, :], v_ref[6:7, :], v_ref[7:8, :]
        m = _dot(mix_ref[...], wo[...])
        m_ref[...] = m.astype(BF16)
        x1n, rstd1 = _ln_stats(ALPHA * x_ref[...] + gate1 * m)
        x1n_ref[...] = x1n
        rstd_ref[...] = rstd1
        x1 = x1n * l1w + l1b
        xh1, _ = _ln_stats(x1)
        u2 = (xh1 * sc2p + sh2).astype(BF16)
        u2_ref[...] = u2
        f = jnp.zeros((tm, D_MODEL), F32)
        for j in range(N_DEV):
            cols = slice(j * FF_COLS, (j + 1) * FF_COLS)
            a = _dot(u2, w1[j])
            a_ref[:, cols] = a.astype(BF16)
            r = jnp.maximum(a, 0.0)
            f = f + _dot((r * r).astype(BF16), w2[cols, :])
        yh, rstd2 = _ln_stats(ALPHA * x1 + gate2 * f)
        e = yh * l2w + l2b - tgt_ref[...]
        dy = e * (1.0 / D_MODEL)
        dh2 = _ln_bwd(dy * l2w, yh, rstd2)
        dh2_ref[...] = dh2
        df_ref[...] = (dh2 * gate2).astype(BF16)
        acc_ref[0:1, :] += jnp.sum(dy * yh, axis=0, keepdims=True)
        acc_ref[1:2, :] += jnp.sum(dy, axis=0, keepdims=True)
        acc_ref[2:3, :] += jnp.sum(dh2 * f, axis=0, keepdims=True)
        acc_ref[3:4, :] += jnp.sum(e * e, axis=0, keepdims=True) * (0.5 / D_MODEL)

    row = lambda i: (i, 0)
    fix = lambda i: (0, 0)
    hbm = pl.BlockSpec(memory_space=pl.ANY)
    return pl.pallas_call(
        body, name="mid_fwd", grid=(t // tm,),
        in_specs=[pl.BlockSpec((tm, D_MODEL), row), pl.BlockSpec((tm, D_MODEL), row), pl.BlockSpec((tm, D_MODEL), row),
                  pl.BlockSpec((8, D_MODEL), fix), hbm, hbm, hbm],
        out_specs=(pl.BlockSpec((tm, D_MODEL), row), pl.BlockSpec((tm, D_MODEL), row), pl.BlockSpec((tm, 1), row),
                   pl.BlockSpec((tm, D_MODEL), row), pl.BlockSpec((tm, D_FF), row), pl.BlockSpec((tm, D_MODEL), row),
                   pl.BlockSpec((tm, D_MODEL), row), pl.BlockSpec((8, D_MODEL), fix)),
        out_shape=(jax.ShapeDtypeStruct((t, D_MODEL), BF16), jax.ShapeDtypeStruct((t, D_MODEL), F32),
                   jax.ShapeDtypeStruct((t, 1), F32), jax.ShapeDtypeStruct((t, D_MODEL), BF16),
                   jax.ShapeDtypeStruct((t, D_FF), BF16), jax.ShapeDtypeStruct((t, D_MODEL), BF16),
                   jax.ShapeDtypeStruct((t, D_MODEL), F32), jax.ShapeDtypeStruct((8, D_MODEL), F32)),
        scratch_shapes=[pltpu.VMEM((D_MODEL, D_MODEL), BF16), pltpu.VMEM((N_DEV, D_MODEL, FF_COLS), BF16),
                        pltpu.VMEM((D_FF, D_MODEL), BF16), pltpu.SemaphoreType.DMA((3,))],
        compiler_params=pltpu.CompilerParams(dimension_semantics=("arbitrary",), vmem_limit_bytes=V7X_VMEM_LIMIT),
    )(mixed, x2d, target, vecs, w_out_b, w1_b, w2_b)


def _ffn_bwd(df, a, dh2, x1n, rstd1, m, vecs, w_out_b, w1_b, w2_b):
    t = x1n.shape[0]
    tm = min(ROW_TILE, t)

    def body(df_ref, a_ref, dh2_ref, x1n_ref, rstd_ref, m_ref, v_ref, wo_hbm, w1_hbm, w2_hbm,
             da_ref, dm_ref, dmix_ref, dxa_ref, acc_ref, wo, w1, w2, sem):
        first = pl.program_id(0) == 0
        _load_resident(first, [(wo_hbm, wo), (w1_hbm, w1), (w2_hbm, w2)], sem)

        @pl.when(first)
        def _():
            acc_ref[...] = jnp.zeros_like(acc_ref)

        gate1, sc2p, l1w, l1b = v_ref[0:1, :], v_ref[1:2, :], v_ref[2:3, :], v_ref[3:4, :]
        df = df_ref[...]
        du2 = jnp.zeros((tm, D_MODEL), F32)
        for j in range(N_DEV):
            cols = slice(j * FF_COLS, (j + 1) * FF_COLS)
            dr2 = _dot(df, w2[cols, :], NT)
            da = (dr2 * (2.0 * jnp.maximum(a_ref[:, cols].astype(F32), 0.0))).astype(BF16)
            da_ref[:, cols] = da
            du2 = du2 + _dot(da, w1[j], NT)
        x1n = x1n_ref[...]
        xh1, rstd0 = _ln_stats(x1n * l1w + l1b)
        dx1 = ALPHA * dh2_ref[...] + _ln_bwd(du2 * sc2p, xh1, rstd0)
        dh1 = _ln_bwd(dx1 * l1w, x1n, rstd_ref[...])
        dxa_ref[...] = ALPHA * dh1
        dm = (dh1 * gate1).astype(BF16)
        dm_ref[...] = dm
        dmix_ref[...] = _dot(dm, wo[...], NT)
        acc_ref[0:1, :] += jnp.sum(du2 * xh1, axis=0, keepdims=True)
        acc_ref[1:2, :] += jnp.sum(du2, axis=0, keepdims=True)
        acc_ref[2:3, :] += jnp.sum(dx1 * x1n, axis=0, keepdims=True)
        acc_ref[3:4, :] += jnp.sum(dx1, axis=0, keepdims=True)
        acc_ref[4:5, :] += jnp.sum(dh1 * m_ref[...].astype(F32), axis=0, keepdims=True)

    row = lambda i: (i, 0)
    fix = lambda i: (0, 0)
    hbm = pl.BlockSpec(memory_space=pl.ANY)
    return pl.pallas_call(
        body, name="ffn_bwd", grid=(t // tm,),
        in_specs=[pl.BlockSpec((tm, D_MODEL), row), pl.BlockSpec((tm, D_FF), row), pl.BlockSpec((tm, D_MODEL), row),
                  pl.BlockSpec((tm, D_MODEL), row), pl.BlockSpec((tm, 1), row), pl.BlockSpec((tm, D_MODEL), row),
                  pl.BlockSpec((8, D_MODEL), fix), hbm, hbm, hbm],
        out_specs=(pl.BlockSpec((tm, D_FF), row), pl.BlockSpec((tm, D_MODEL), row), pl.BlockSpec((tm, D_MODEL), row),
                   pl.BlockSpec((tm, D_MODEL), row), pl.BlockSpec((8, D_MODEL), fix)),
        out_shape=(jax.ShapeDtypeStruct((t, D_FF), BF16), jax.ShapeDtypeStruct((t, D_MODEL), BF16),
                   jax.ShapeDtypeStruct((t, D_MODEL), F32), jax.ShapeDtypeStruct((t, D_MODEL), F32),
                   jax.ShapeDtypeStruct((8, D_MODEL), F32)),
        scratch_shapes=[pltpu.VMEM((D_MODEL, D_MODEL), BF16), pltpu.VMEM((N_DEV, D_MODEL, FF_COLS), BF16),
                        pltpu.VMEM((D_FF, D_MODEL), BF16), pltpu.SemaphoreType.DMA((3,))],
        compiler_params=pltpu.CompilerParams(dimension_semantics=("arbitrary",), vmem_limit_bytes=V7X_VMEM_LIMIT),
    )(df, a, dh2, x1n, rstd1, m, vecs, w_out_b, w1_b, w2_b)


def _matmul_tn(lhs, rhs, tmm, tn, name, relu_sq=False, col_slab=None):
    t, mm = lhs.shape
    nn = rhs.shape[1]
    tk = min(512, t)
    nk = t // tk

    def body(l_ref, r_ref, o_ref, acc):
        kk = pl.program_id(2)

        @pl.when(kk == 0)
        def _():
            acc[...] = jnp.zeros_like(acc)

        l = l_ref[...]
        if relu_sq:
            lf = jnp.maximum(l.astype(F32), 0.0)
            l = (lf * lf).astype(BF16)
        acc[...] += _dot(l, r_ref[...], TN)

        @pl.when(kk == nk - 1)
        def _():
            if col_slab is None:
                o_ref[...] = acc[...].astype(o_ref.dtype)
            else:
                for s in range(tn // col_slab):
                    o_ref[s] = acc[:, s * col_slab:(s + 1) * col_slab].astype(o_ref.dtype)

    if col_slab is None:
        out_spec = pl.BlockSpec((tmm, tn), lambda i, j, k: (i, j))
        out_shape = jax.ShapeDtypeStruct((mm, nn), BF16)
    else:
        out_spec = pl.BlockSpec((tn // col_slab, tmm, col_slab), lambda i, j, k: (j, i, 0))
        out_shape = jax.ShapeDtypeStruct((nn // col_slab, mm, col_slab), BF16)
    return pl.pallas_call(
        body, name=name, grid=(mm // tmm, nn // tn, nk),
        in_specs=[pl.BlockSpec((tk, tmm), lambda i, j, k: (k, i)), pl.BlockSpec((tk, tn), lambda i, j, k: (k, j))],
        out_specs=out_spec,
        out_shape=out_shape,
        scratch_shapes=[pltpu.VMEM((tmm, tn), F32)],
        compiler_params=pltpu.CompilerParams(dimension_semantics=("arbitrary", "arbitrary", "arbitrary"),
                                             vmem_limit_bytes=V7X_VMEM_LIMIT),
    )(lhs, rhs)


def _mixer_bwd(dmix, proj, oraw, cos_t, sin_t, rst, sst, gw_pad, gb, rnw, gnw, slabs):
    t = proj.shape[0]
    tc = min(MIX_TILE, t)
    cps = tc // CHUNK
    nsteps = t // tc
    ns = len(slabs)
    scale_r = RET_D ** -0.5
    scale_g = GLA_DK ** -0.5

    def body(dmix_ref, rq_ref, rk_ref, rv_ref, rg_ref, gq_ref, gk_ref, gv_ref, gg_ref, glr_ref, oraw_ref,
             cos_ref, sin_ref, rst_ref, sst_ref, gw_ref, gb_ref, rnw_ref, gnw_ref, *rest):
        send_refs, rest = rest[:ns], rest[ns:]
        dproj_ref, dgw_ref, dvec_ref = rest[:3]
        recv_refs, rest = rest[3:3 + ns], rest[3 + ns:]
        dr_scr, ds_scr, xs, xr, xl = rest
        step = pl.program_id(0)

        @pl.when(step == 0)
        def _():
            dr_scr[...] = jnp.zeros_like(dr_scr)
            ds_scr[...] = jnp.zeros_like(ds_scr)
            dgw_ref[...] = jnp.zeros_like(dgw_ref)
            dvec_ref[...] = jnp.zeros_like(dvec_ref)
            _DirectExchange(send_refs, recv_refs, xs, xr, xl).start()

        k = _mixer_consts()
        last_row = lax.broadcasted_iota(jnp.int32, (CHUNK, GLA_KW), 0) == CHUNK - 1

        def chunk(jj, carry):
            j = cps - 1 - jj
            rows = pl.ds(pl.multiple_of(j * CHUNK, CHUNK), CHUNK)
            cosv, sinv = cos_ref[rows, :], sin_ref[rows, :]
            for h in range(RET_HEADS):
                cols = slice(h * RET_D, (h + 1) * RET_D)
                o = oraw_ref[rows, cols]
                g = rg_ref[rows, cols]
                w = rnw_ref[:, cols]
                dout = dmix_ref[rows, cols]
                oc = o - jnp.mean(o, axis=-1, keepdims=True)
                inv = lax.rsqrt(jnp.mean(oc * oc, axis=-1, keepdims=True) + LN_EPS)
                n = oc * inv
                sg = _sigmoid(g)
                sil = g * sg
                dn = dout * w * sil
                dvec_ref[0:1, cols] += jnp.sum(dout * n * sil, axis=0, keepdims=True)
                dproj_ref[rows, OFF_RG + h * RET_D:OFF_RG + (h + 1) * RET_D] = (
                    dout * n * w * (sg * (1.0 + g * (1.0 - sg)))).astype(BF16)
                doc = inv * (dn - n * jnp.mean(dn * n, axis=-1, keepdims=True))
                do = doc - jnp.mean(doc, axis=-1, keepdims=True)

                qr = _rotate(rq_ref[rows, cols], cosv, sinv) * scale_r
                kr = _rotate(rk_ref[rows, cols], cosv, sinv)
                vb = rv_ref[rows, cols].astype(BF16)
                qb, kb, dob = qr.astype(BF16), kr.astype(BF16), do.astype(BF16)
                p = _dot(qb, kb, NT) * k["dec"][h]
                rp = rst_ref[j, cols, :].astype(BF16)
                dr = dr_scr[cols, :]
                drb = dr.astype(BF16)
                dpb = (_dot(dob, vb, NT) * k["dec"][h]).astype(BF16)
                dqr = _dot(dpb, kb) + _dot(dob, rp, NT) * k["qd"][h]
                dkr = _dot(dpb, qb, TN) + _dot(vb, drb, NT) * k["kd"][h]
                dv = _dot(p.astype(BF16), dob, TN) + _dot((kr * k["kd"][h]).astype(BF16), drb)
                dr_scr[cols, :] = k["g64"][h] * dr + _dot((qr * k["qd"][h]).astype(BF16), dob, TN)
                dproj_ref[rows, OFF_RQ + h * RET_D:OFF_RQ + (h + 1) * RET_D] = (
                    _rotate_t(dqr, cosv, sinv) * scale_r).astype(BF16)
                dproj_ref[rows, OFF_RK + h * RET_D:OFF_RK + (h + 1) * RET_D] = _rotate_t(dkr, cosv, sinv).astype(BF16)
                dproj_ref[rows, OFF_RV + h * RET_D:OFF_RV + (h + 1) * RET_D] = dv.astype(BF16)

            glr = glr_ref[rows, :]
            z, b, bl, ep, em = _gla_gates(glr, gw_ref[...], gb_ref[...], k["ltri"])
            qs = gq_ref[rows, :] * scale_g
            kk = gk_ref[rows, :]
            eb = jnp.exp(b)
            ekb = jnp.exp(bl - b)
            ebl = jnp.exp(bl)
            ql, qu, kl, ku = qs * ep, qs * em, kk * em, kk * ep
            qg, kg = qs * eb, kk * ekb
            qlm = _stack_heads(ql, k["hmask"]).astype(BF16)
            qum = _stack_heads(qu, k["hmask"]).astype(BF16)
            klb, kub = kl.astype(BF16), ku.astype(BF16)
            a_all = jnp.where(k["lower"], _dot(qlm, klb, NT), _dot(qum, kub, NT)).astype(BF16)
            st = sst_ref[j]
            stb = st.astype(BF16)
            ds = ds_scr[...]
            dsb = ds.astype(BF16)
            ds_new = ds * ebl
            da_parts = []
            dqg = jnp.zeros((CHUNK, GLA_KW), F32)
            dkg = jnp.zeros((CHUNK, GLA_KW), F32)
            for h in range(GLA_HEADS):
                cols = slice(h * GLA_DV, (h + 1) * GLA_DV)
                hr = slice(h * CHUNK, (h + 1) * CHUNK)
                ocols = slice(RET_HEADS * RET_D + h * GLA_DV, RET_HEADS * RET_D + (h + 1) * GLA_DV)
                o = oraw_ref[rows, ocols]
                g = gg_ref[rows, cols]
                w = gnw_ref[:, cols]
                dout = dmix_ref[rows, ocols]
                inv = lax.rsqrt(jnp.mean(o * o, axis=-1, keepdims=True) + LN_EPS)
                n = o * inv
                sg = _sigmoid(g)
                sil = g * sg
                dn = dout * w * sil
                dvec_ref[1:2, cols] += jnp.sum(dout * n * sil, axis=0, keepdims=True)
                dproj_ref[rows, OFF_GG + h * GLA_DV:OFF_GG + (h + 1) * GLA_DV] = (
                    dout * n * w * (sg * (1.0 + g * (1.0 - sg)))).astype(BF16)
                dob = (inv * (dn - n * jnp.mean(dn * n, axis=-1, keepdims=True))).astype(BF16)
                vb = gv_ref[rows, cols].astype(BF16)
                mh = k["hmask"][h]
                da_parts.append(_dot(dob, vb, NT))
                dv = _dot(a_all[hr, :], dob, TN) + _dot((kg * mh).astype(BF16), dsb, NT)
                dproj_ref[rows, OFF_GV + h * GLA_DV:OFF_GV + (h + 1) * GLA_DV] = dv.astype(BF16)
                dkg = dkg + mh * _dot(vb, dsb)
                dqg = dqg + mh * _dot(dob, stb)
                ds_new = ds_new + _dot(dob, (qg * mh).astype(BF16), TN)
            da_all = jnp.concatenate(da_parts, axis=0)
            dal = jnp.where(k["lower"], da_all, 0.0).astype(BF16)
            dau = jnp.where(k["lower"], 0.0, da_all).astype(BF16)
            dqlm = _dot(dal, klb)
            dqum = _dot(dau, kub)
            dql = jnp.zeros((CHUNK, GLA_KW), F32)
            dqu = jnp.zeros((CHUNK, GLA_KW), F32)
            for h in range(GLA_HEADS):
                hr = slice(h * CHUNK, (h + 1) * CHUNK)
                dql = dql + k["hmask"][h] * dqlm[hr, :]
                dqu = dqu + k["hmask"][h] * dqum[hr, :]
            dkl = _dot(dal, qlm, TN)
            dku = _dot(dau, qum, TN)
            dbl = (jnp.sum(dkg * kg, axis=0, keepdims=True)
                   + jnp.sum(ds * st, axis=0, keepdims=True) * ebl)
            ds_scr[...] = ds_new
            dqs = dql * ep + dqu * em + dqg * eb
            dk = dkl * em + dku * ep + dkg * ekb
            db = dql * ql - dkl * kl - dqu * qu + dku * ku + dqg * qg - dkg * kg
            db = db + jnp.where(last_row, dbl, 0.0)
            dla = _dot(k["utri"], db, NN, HIGHEST)
            dz = dla * (1.0 / GATE_TAU) * _sigmoid(-z)
            dvec_ref[2:3, 0:GLA_KW] += jnp.sum(dz, axis=0, keepdims=True)
            dgw_ref[...] += _dot(glr, dz, TN, HIGHEST)
            dproj_ref[rows, OFF_GLR:OFF_GLR + 128] = _dot(dz, gw_ref[...], NT, HIGHEST).astype(BF16)
            dproj_ref[rows, OFF_GQ:OFF_GQ + GLA_KW] = (dqs * scale_g).astype(BF16)
            dproj_ref[rows, OFF_GK:OFF_GK + GLA_KW] = dk.astype(BF16)
            return carry

        lax.fori_loop(0, cps, chunk, 0)

        @pl.when(step == nsteps - 1)
        def _():
            _DirectExchange(send_refs, recv_refs, xs, xr, xl).finish()

    rev = lambda i: (nsteps - 1 - i, 0)

    def col(width, off):
        return pl.BlockSpec((tc, width), lambda i, o=off // width: (nsteps - 1 - i, o))

    fix = lambda i: (0, 0)
    hbm = pl.BlockSpec(memory_space=pl.ANY)
    in_specs = [pl.BlockSpec((tc, D_MODEL), rev),
                col(512, OFF_RQ), col(512, OFF_RK), col(512, OFF_RV), col(512, OFF_RG),
                col(256, OFF_GQ), col(256, OFF_GK), col(512, OFF_GV), col(512, OFF_GG), col(128, OFF_GLR),
                pl.BlockSpec((tc, D_MODEL), rev), pl.BlockSpec((tc, RET_D), rev), pl.BlockSpec((tc, RET_D), rev),
                pl.BlockSpec((cps, RET_HEADS * RET_D, RET_D), lambda i: (nsteps - 1 - i, 0, 0)),
                pl.BlockSpec((cps, GLA_DV, GLA_KW), lambda i: (nsteps - 1 - i, 0, 0)),
                pl.BlockSpec((128, GLA_KW), fix), pl.BlockSpec((1, GLA_KW), fix),
                pl.BlockSpec((1, 512), fix), pl.BlockSpec((1, 512), fix)] + [hbm] * ns
    out_specs = (pl.BlockSpec((tc, D_IN_PAD), rev), pl.BlockSpec((128, GLA_KW), fix),
                 pl.BlockSpec((8, 512), fix)) + tuple([hbm] * ns)
    out_shape = (jax.ShapeDtypeStruct((t, D_IN_PAD), BF16), jax.ShapeDtypeStruct((128, GLA_KW), F32),
                 jax.ShapeDtypeStruct((8, 512), F32)) + tuple(jax.ShapeDtypeStruct(s.shape, s.dtype) for s in slabs)
    return pl.pallas_call(
        body, name="mixer_bwd", grid=(nsteps,), in_specs=in_specs, out_specs=out_specs, out_shape=out_shape,
        scratch_shapes=[pltpu.VMEM((RET_HEADS * RET_D, RET_D), F32), pltpu.VMEM((GLA_DV, GLA_KW), F32),
                        pltpu.SemaphoreType.DMA((7 * ns,)), pltpu.SemaphoreType.DMA((7 * ns,)),
                        pltpu.SemaphoreType.DMA((ns,))],
        compiler_params=pltpu.CompilerParams(dimension_semantics=("arbitrary",), vmem_limit_bytes=V7X_VMEM_LIMIT),
    )(dmix, *([proj] * 9), oraw, cos_t, sin_t, rst, sst, gw_pad, gb, rnw, gnw, *slabs)


def _inproj_bwd(dproj, x2d, dxa, sc1p, w_in_b, slab):
    t = x2d.shape[0]
    tm = min(ROW_TILE, t)
    nsteps = t // tm

    def body(dp_ref, x_ref, dxa_ref, sc_ref, w_hbm, send_ref, gx_ref, acc_ref, recv_ref, w_vmem, sem, xs, xr, xl):
        step = pl.program_id(0)
        first = step == 0
        _load_resident(first, [(w_hbm, w_vmem)], sem)

        @pl.when(first)
        def _():
            acc_ref[...] = jnp.zeros_like(acc_ref)
            _DirectExchange([send_ref], [recv_ref], xs, xr, xl).start()

        du = _dot(dp_ref[...], w_vmem[...], NT)
        xh, rstd = _ln_stats(x_ref[...])
        gx_ref[...] = dxa_ref[...] + _ln_bwd(du * sc_ref[...], xh, rstd)
        acc_ref[0:1, :] += jnp.sum(du * xh, axis=0, keepdims=True)
        acc_ref[1:2, :] += jnp.sum(du, axis=0, keepdims=True)

        @pl.when(step == nsteps - 1)
        def _():
            _DirectExchange([send_ref], [recv_ref], xs, xr, xl).finish()

    row = lambda i: (i, 0)
    fix = lambda i: (0, 0)
    hbm = pl.BlockSpec(memory_space=pl.ANY)
    return pl.pallas_call(
        body, name="inproj_bwd", grid=(nsteps,),
        in_specs=[pl.BlockSpec((tm, D_IN_PAD), row), pl.BlockSpec((tm, D_MODEL), row), pl.BlockSpec((tm, D_MODEL), row),
                  pl.BlockSpec((1, D_MODEL), fix), hbm, hbm],
        out_specs=(pl.BlockSpec((tm, D_MODEL), row), pl.BlockSpec((8, D_MODEL), fix), hbm),
        out_shape=(jax.ShapeDtypeStruct((t, D_MODEL), F32), jax.ShapeDtypeStruct((8, D_MODEL), F32),
                   jax.ShapeDtypeStruct(slab.shape, slab.dtype)),
        scratch_shapes=[pltpu.VMEM((D_MODEL, D_IN_PAD), BF16), pltpu.SemaphoreType.DMA((1,)),
                        pltpu.SemaphoreType.DMA((7,)), pltpu.SemaphoreType.DMA((7,)), pltpu.SemaphoreType.DMA((1,))],
        compiler_params=pltpu.CompilerParams(dimension_semantics=("arbitrary",), vmem_limit_bytes=V7X_VMEM_LIMIT),
    )(dproj, x2d, dxa, sc1p, w_in_b, slab)


def _adam_math(w, g, m, v):
    m = ADAM_B1 * m + (1.0 - ADAM_B1) * g
    v = ADAM_B2 * v + (1.0 - ADAM_B2) * (g * g)
    m_hat = m / (1.0 - ADAM_B1 ** ADAM_STEP)
    v_hat = v / (1.0 - ADAM_B2 ** ADAM_STEP)
    delta = -ADAM_LR * (m_hat / (jnp.sqrt(v_hat) + ADAM_EPS) + ADAM_WD * w)
    return delta, m, v


def _adamw(w, gparts, m, v, name):
    nparts, rows, cols = gparts.shape
    tr = rows
    for cand in (512, 256, 128, 64, 32, 16, 8):
        if rows % cand == 0:
            tr = cand
            break

    def body(w_ref, g_ref, m_ref, v_ref, go_ref, d_ref, mo_ref, vo_ref):
        g = g_ref[0].astype(F32)
        for p in range(1, nparts):
            g = g + g_ref[p].astype(F32)
        delta, mn, vn = _adam_math(w_ref[...], g, m_ref[...], v_ref[...])
        go_ref[...] = g
        d_ref[...] = delta
        mo_ref[...] = mn
        vo_ref[...] = vn

    blk = pl.BlockSpec((tr, cols), lambda i: (i, 0))
    shp = jax.ShapeDtypeStruct((rows, cols), F32)
    return pl.pallas_call(
        body, name=name, grid=(rows // tr,),
        in_specs=[blk, pl.BlockSpec((nparts, tr, cols), lambda i: (0, i, 0)), blk, blk],
        out_specs=(blk, blk, blk, blk), out_shape=(shp, shp, shp, shp),
        compiler_params=pltpu.CompilerParams(dimension_semantics=("arbitrary",), vmem_limit_bytes=V7X_VMEM_LIMIT),
    )(w, gparts, m, v)


def _small_reduce(gathered, gathered_gw, c_all, dmod_cols):
    def body(g_ref, gw_ref, c_ref, dm_ref, sum_ref, gwsum_ref, gb_ref, gwa_ref):
        s = g_ref[0]
        sw = gw_ref[0]
        for p in range(1, N_DEV):
            s = s + g_ref[p]
            sw = sw + gw_ref[p]
        sum_ref[...] = s
        gwsum_ref[...] = sw
        for i in range(6):
            gb_ref[:, i * D_MODEL:(i + 1) * D_MODEL] = s[i:i + 1, :]
        cc = c_ref[...]
        gwa_ref[...] = _dot(cc * _sigmoid(cc), dm_ref[...], TN, HIGHEST)

    vm = pl.BlockSpec(memory_space=pltpu.VMEM)
    return pl.pallas_call(
        body, name="small_reduce",
        out_shape=(jax.ShapeDtypeStruct(gathered.shape[1:], F32), jax.ShapeDtypeStruct(gathered_gw.shape[1:], F32),
                   jax.ShapeDtypeStruct((1, 6 * D_MODEL), F32), jax.ShapeDtypeStruct((D_MODEL, ADA_COLS), F32)),
        in_specs=[vm] * 4, out_specs=(vm, vm, vm, vm),
        compiler_params=pltpu.CompilerParams(vmem_limit_bytes=V7X_VMEM_LIMIT),
    )(gathered, gathered_gw, c_all, dmod_cols)


SMR_LN1W, SMR_LN1B, SMR_LN2W, SMR_LN2B, SMR_NORMS, SMR_MISC = 6, 7, 8, 9, 10, 11


def _adamw_small(gsum, g_b_ada, g_ggw, params, moms, vels):
    n = len(params)

    def body(*refs):
        gsum_ref, gb_ref, gw_ref = refs[:3]
        w_refs, m_refs, v_refs = refs[3:3 + n], refs[3 + n:3 + 2 * n], refs[3 + 2 * n:3 + 3 * n]
        outs = refs[3 + 3 * n:]
        g_refs, d_refs, mo_refs, vo_refs = outs[:n - 1], outs[n - 1:2 * n - 1], outs[2 * n - 1:3 * n - 1], outs[3 * n - 1:]
        grads = [gb_ref[...],
                 gsum_ref[SMR_NORMS:SMR_NORMS + 1, 0:512],
                 gsum_ref[SMR_MISC:SMR_MISC + 1, 0:GLA_KW],
                 gsum_ref[SMR_NORMS:SMR_NORMS + 1, 512:1024],
                 gsum_ref[SMR_LN1W:SMR_LN1W + 1, :], gsum_ref[SMR_LN1B:SMR_LN1B + 1, :],
                 gsum_ref[SMR_LN2W:SMR_LN2W + 1, :], gsum_ref[SMR_LN2B:SMR_LN2B + 1, :],
                 gw_ref[...]]
        for i in range(n):
            delta, mn, vn = _adam_math(w_refs[i][...], grads[i], m_refs[i][...], v_refs[i][...])
            if i < n - 1:
                g_refs[i][...] = grads[i]
            d_refs[i][...] = delta
            mo_refs[i][...] = mn
            vo_refs[i][...] = vn

    vm = pl.BlockSpec(memory_space=pltpu.VMEM)
    shapes = [jax.ShapeDtypeStruct(p.shape, F32) for p in params]
    n_in = 3 + 3 * n
    out_shape = tuple(shapes[:n - 1] + shapes * 3)
    return pl.pallas_call(
        body, name="adamw_small", out_shape=out_shape,
        in_specs=[vm] * n_in, out_specs=tuple([vm] * len(out_shape)),
        compiler_params=pltpu.CompilerParams(vmem_limit_bytes=V7X_VMEM_LIMIT),
    )(gsum, g_b_ada, g_ggw, *params, *moms, *vels)


def kernel(x, c, w_ada, b_ada, w_in, ret_norm_w, gla_gate_w, gla_gate_b, gla_norm_w, w_out, ln1_w, ln1_b, w_ff1, w_ff2, ln2_w, ln2_b, loss_target, m_w_ada, m_b_ada, m_w_in, m_ret_norm_w, m_gla_gate_w, m_gla_gate_b, m_gla_norm_w, m_w_out, m_ln1_w, m_ln1_b, m_w_ff1, m_w_ff2, m_ln2_w, m_ln2_b, v_w_ada, v_b_ada, v_w_in, v_ret_norm_w, v_gla_gate_w, v_gla_gate_b, v_gla_norm_w, v_w_out, v_ln1_w, v_ln1_b, v_w_ff1, v_w_ff2, v_ln2_w, v_ln2_b):
    t = x.shape[1]
    xi, yi, ci = _my_coords()
    me = 4 * xi + 2 * yi + ci
    x2d = x[0]
    tgt = loss_target[0]

    c_ext = jnp.concatenate([c, gla_gate_w[0].reshape(1, GATE_RANK * GLA_KW // N_DEV)], axis=1)
    b_l = lax.dynamic_slice(b_ada, (0, me * ADA_COLS), (1, ADA_COLS))
    c_all3, mod_all, wi_g = _adaln_mod(c_ext, w_ada[0], b_l, w_in[0].astype(BF16))
    c_all = c_all3[:, 0, :D_MODEL]
    gate_w = c_all3[:, 0, D_MODEL:].reshape(N_DEV, GATE_RANK, GLA_KW // N_DEV)
    gate_w = gate_w.transpose(1, 0, 2).reshape(GATE_RANK, GLA_KW)
    gw_pad = jnp.zeros((128, GLA_KW), F32).at[:GATE_RANK].set(gate_w)
    mod = lax.dynamic_slice(mod_all, (0, me, 0), (N_DEV, 1, ADA_COLS)).reshape(6, D_MODEL)
    shift1, scale1, gate1, shift2, scale2, gate2 = [mod[i:i + 1] for i in range(6)]

    w_in_b = jnp.pad(wi_g.transpose(1, 0, 2).reshape(D_MODEL, D_IN), ((0, 0), (0, D_IN_PAD - D_IN)))

    pos = jnp.arange(t, dtype=F32)
    inv = 1.0 / (10000.0 ** jnp.linspace(0.0, 1.0, RET_D // 2, dtype=F32))
    ang = pos[:, None] * inv[None, :]
    cos_t = jnp.concatenate([jnp.cos(ang), jnp.cos(ang)], axis=1)
    sin_t = jnp.concatenate([-jnp.sin(ang), jnp.sin(ang)], axis=1)

    sc1p = 1.0 + scale1
    proj, u = _inproj_fwd(x2d, sc1p, shift1, w_in_b)
    mixed, oraw, rst, sst, wo_g, w1_b, w2_g = _mixer_fwd(
        proj, cos_t, sin_t, gw_pad, gla_gate_b, ret_norm_w, gla_norm_w,
        [w_out[0].astype(BF16), w_ff1[0].astype(BF16), w_ff2[0].astype(BF16)])
    w_out_b = wo_g.reshape(D_MODEL, D_MODEL)
    w2_b = w2_g.reshape(D_FF, D_MODEL)
    vec_f = jnp.concatenate([gate1, 1.0 + scale2, shift2, gate2, ln1_w, ln1_b, ln2_w, ln2_b], axis=0)
    m, x1n, rstd1, u2, a, df, dh2, acc_f = _mid_fwd(mixed, x2d, tgt, vec_f, w_out_b, w1_b, w2_b)

    vec_b = jnp.concatenate([gate1, 1.0 + scale2, ln1_w, ln1_b, jnp.zeros((4, D_MODEL), F32)], axis=0)
    da, dm, dmix, dxa, acc_b = _ffn_bwd(df, a, dh2, x1n, rstd1, m, vec_b, w_out_b, w1_b, w2_b)
    dw2 = _matmul_tn(a, df, 2048, 1024, "tn_dw2", relu_sq=True)
    dw1 = _matmul_tn(u2, da, 1024, 2048, "tn_dw1", col_slab=FF_COLS)
    dwo = _matmul_tn(mixed, dm, 1024, 1024, "tn_dwout")
    dproj, dgw, dvec, r_wo, r_w1, r_w2 = _mixer_bwd(
        dmix, proj, oraw, cos_t, sin_t, rst, sst, gw_pad, gla_gate_b, ret_norm_w, gla_norm_w,
        [dwo.reshape(N_DEV, OUT_ROWS, D_MODEL), dw1, dw2.reshape(N_DEV, FF_COLS, D_MODEL)])
    dwi = _matmul_tn(u, dproj, 1024, D_IN_PAD, "tn_dwin")
    dwi_s = dwi[:, :D_IN].reshape(D_MODEL, N_DEV, IN_COLS).transpose(1, 0, 2)
    grad_x, acc_i, r_wi = _inproj_bwd(dproj, x2d, dxa, sc1p, w_in_b, dwi_s)

    loss_part = jnp.sum(acc_f[3])
    small = jnp.concatenate([
        acc_i[1:2], acc_i[0:1], acc_b[4:5], acc_b[1:2], acc_b[0:1], acc_f[2:3],
        acc_b[2:3], acc_b[3:4], acc_f[0:1], acc_f[1:2],
        jnp.concatenate([dvec[0:1], dvec[1:2]], axis=1),
        jnp.concatenate([dvec[2:3, :GLA_KW], jnp.full((1, 128), loss_part, F32),
                         jnp.zeros((1, D_MODEL - GLA_KW - 128), F32)], axis=1),
        jnp.zeros((4, D_MODEL), F32)], axis=0)
    small_all, gw_all = _small_gather([small, dgw[:GATE_RANK]])
    dmod_all = small_all[:, :6].reshape(N_DEV, 6 * D_MODEL)
    dmod_cols = lax.dynamic_slice(dmod_all, (0, me * ADA_COLS), (N_DEV, ADA_COLS))
    ssum, gw_sum, g_b_ada, g_w_ada = _small_reduce(small_all, gw_all, c_all, dmod_cols)
    loss = ssum[SMR_MISC, GLA_KW]
    g_ggw = lax.dynamic_slice(gw_sum, (0, me * (GLA_KW // N_DEV)), (GATE_RANK, GLA_KW // N_DEV))[None]

    small_w = [b_ada, ret_norm_w, gla_gate_b, gla_norm_w, ln1_w, ln1_b, ln2_w, ln2_b, gla_gate_w]
    small_m = [m_b_ada, m_ret_norm_w, m_gla_gate_b, m_gla_norm_w, m_ln1_w, m_ln1_b, m_ln2_w, m_ln2_b, m_gla_gate_w]
    small_v = [v_b_ada, v_ret_norm_w, v_gla_gate_b, v_gla_norm_w, v_ln1_w, v_ln1_b, v_ln2_w, v_ln2_b, v_gla_gate_w]
    res = _adamw_small(ssum, g_b_ada, g_ggw, small_w, small_m, small_v)
    small_g = list(res[:8]) + [g_ggw]
    d_small, m_small, v_small = list(res[8:17]), list(res[17:26]), list(res[26:35])

    _, d_w_ada, nm_w_ada, nv_w_ada = _adamw(w_ada[0], g_w_ada[None], m_w_ada[0], v_w_ada[0], "adamw_ada")

    big =[_adamw(w[0], r, m_[0], v_[0], nm) for w, r, m_, v_, nm in (
        (w_in, r_wi, m_w_in, v_w_in, "adamw_in"), (w_out, r_wo, m_w_out, v_w_out, "adamw_out"),
        (w_ff1, r_w1, m_w_ff1, v_w_ff1, "adamw_ff1"), (w_ff2, r_w2, m_w_ff2, v_w_ff2, "adamw_ff2"))]
    g_big, d_big, m_big, v_big = [[b[i][None] for b in big] for i in range(4)]

    def ordered(w_ada_v, small_vals, big_vals):
        b_ada_v, rnw_v, ggb_v, gnw_v, l1w_v, l1b_v, l2w_v, l2b_v, ggw_v = small_vals
        wi_v, wo_v, w1_v, w2_v = big_vals
        return [w_ada_v, b_ada_v, wi_v, rnw_v, ggw_v, ggb_v, gnw_v, wo_v, l1w_v, l1b_v, w1_v, w2_v, l2w_v, l2b_v]

    grads = ordered(g_w_ada[None], small_g, g_big)
    deltas = ordered(d_w_ada[None], d_small, d_big)
    new_m = ordered(nm_w_ada[None], m_small, m_big)
    new_v = ordered(nv_w_ada[None], v_small, v_big)
    return (loss, grad_x[None], *grads, *deltas, *new_m, *new_v)
```

```python
import functools

import numpy as np
import jax
import jax.numpy as jnp
from jax import lax
from jax.experimental import pallas as pl
from jax.experimental.pallas import tpu as pltpu

F32 = jnp.float32
BF16 = jnp.bfloat16
MESH = pl.DeviceIdType.MESH
HIGHEST = lax.Precision.HIGHEST

N_DEV = 8
D_MODEL = 1024
CHUNK = 64
RET_HEADS = 4
RET_D = 128
GLA_HEADS = 4
GLA_DK = 64
GLA_DV = 128
GLA_KW = GLA_HEADS * GLA_DK
GATE_RANK = 16
GATE_TAU = 16.0
D_FF = 4096
LN_EPS = 1e-5
ALPHA = (2.0 * 1) ** 0.25
D_IN = 3600
D_IN_PAD = 3712
ADA_COLS = 6 * D_MODEL // N_DEV
IN_COLS = D_IN // N_DEV
FF_COLS = D_FF // N_DEV
OUT_ROWS = D_MODEL // N_DEV

OFF_RQ, OFF_RK, OFF_RV, OFF_RG = 0, 512, 1024, 1536
OFF_GQ, OFF_GK, OFF_GV, OFF_GG, OFF_GLR = 2048, 2304, 2560, 3072, 3584

ADAM_LR, ADAM_B1, ADAM_B2, ADAM_EPS, ADAM_WD, ADAM_STEP = 0.001, 0.9, 0.999, 1e-08, 0.01, 10

V7X_VMEM_LIMIT = 56 * 1024 * 1024

ROW_TILE = 256
MIX_TILE = 256
RET_SUB = 256
GLA_SUB = 128


def _log_gamma(h):
    return float(np.log(np.float32(1.0) - np.float32(2.0) ** np.float32(-5.0 - h)))


def _my_coords():
    return lax.axis_index("x"), lax.axis_index("y"), lax.axis_index("c")


def _flip(v, bit):
    return 1 - v if bit else v


def _peer(k):
    x, y, c = _my_coords()
    px, py, pc = _flip(x, (k >> 2) & 1), _flip(y, (k >> 1) & 1), _flip(c, k & 1)
    return (px, py, pc), 4 * px + 2 * py + pc


def _dot(a, b, dims=(((1,), (0,)), ((), ())), precision=None):
    return lax.dot_general(a, b, dims, precision=precision, preferred_element_type=F32)


NN = (((1,), (0,)), ((), ()))
NT = (((1,), (1,)), ((), ()))
TN = (((0,), (0,)), ((), ()))


def _sigmoid(x):
    return 1.0 / (1.0 + jnp.exp(-x))


def _ln_stats(x):
    mu = jnp.mean(x, axis=-1, keepdims=True)
    xc = x - mu
    var = jnp.mean(xc * xc, axis=-1, keepdims=True)
    rstd = lax.rsqrt(var + LN_EPS)
    return xc * rstd, rstd


def _ln_bwd(dyh, xh, rstd):
    return rstd * (dyh - jnp.mean(dyh, axis=-1, keepdims=True) - xh * jnp.mean(dyh * xh, axis=-1, keepdims=True))


def _adaln_mod(c_ext, w_ada_l, b_l, w_in_l):
    width = c_ext.shape[1]

    def body(c_ref, w_ref, b_ref, wi_ref, call_ref, mod_ref, wig_ref, s1, r1, s2, r2, gs, gr, gl):
        gather = _TwoLevelGather([wi_ref], [wig_ref], gs, gr, gl)
        gather.start()
        x, y, c = _my_coords()
        me = 4 * x + 2 * y + c
        call_ref[me] = c_ref[...]
        sends = []
        for k in range(1, N_DEV):
            peer, _ = _peer(k)
            cp = pltpu.make_async_remote_copy(c_ref, call_ref.at[me], s1.at[k - 1], r1.at[k - 1],
                                              device_id=peer, device_id_type=MESH)
            cp.start()
            sends.append(cp)
        for k in range(1, N_DEV):
            peer, pid = _peer(k)
            pltpu.make_async_remote_copy(c_ref, call_ref.at[pid], s1.at[k - 1], r1.at[k - 1],
                                         device_id=peer, device_id_type=MESH).wait_recv()
        for cp in sends:
            cp.wait_send()
        row = lax.broadcasted_iota(jnp.int32, (N_DEV, D_MODEL), 0)
        call = jnp.zeros((N_DEV, D_MODEL), F32)
        for j in range(N_DEV):
            call = jnp.where(row == j, jnp.broadcast_to(call_ref[j][:, :D_MODEL], (N_DEV, D_MODEL)), call)
        sc = call * _sigmoid(call)
        mod = _dot(sc, w_ref[...], NN, HIGHEST) + b_ref[...]
        mod_ref[me] = mod
        sends = []
        for k in range(1, N_DEV):
            peer, _ = _peer(k)
            cp = pltpu.make_async_remote_copy(mod_ref.at[me], mod_ref.at[me], s2.at[k - 1], r2.at[k - 1],
                                              device_id=peer, device_id_type=MESH)
            cp.start()
            sends.append(cp)
        for k in range(1, N_DEV):
            peer, pid = _peer(k)
            pltpu.make_async_remote_copy(mod_ref.at[pid], mod_ref.at[pid], s2.at[k - 1], r2.at[k - 1],
                                         device_id=peer, device_id_type=MESH).wait_recv()
        for cp in sends:
            cp.wait_send()
        gather.forward()
        gather.finish()

    vm = pl.BlockSpec(memory_space=pltpu.VMEM)
    hbm = pl.BlockSpec(memory_space=pl.ANY)
    return pl.pallas_call(
        body, name="adaln_mod",
        out_shape=(jax.ShapeDtypeStruct((N_DEV, 1, width), F32),
                   jax.ShapeDtypeStruct((N_DEV, N_DEV, ADA_COLS), F32),
                   jax.ShapeDtypeStruct((N_DEV, *w_in_l.shape), w_in_l.dtype)),
        in_specs=[vm, vm, vm, hbm], out_specs=(vm, vm, hbm),
        scratch_shapes=[pltpu.SemaphoreType.DMA((N_DEV - 1,))] * 4
        + [pltpu.SemaphoreType.DMA((7,)), pltpu.SemaphoreType.DMA((7,)), pltpu.SemaphoreType.DMA((1,))],
        compiler_params=pltpu.CompilerParams(vmem_limit_bytes=V7X_VMEM_LIMIT),
    )(c_ext, w_ada_l, b_l, w_in_l)


class _TwoLevelGather:
    def __init__(self, x_refs, out_refs, send_sems, recv_sems, local_sems):
        self.x_refs, self.out_refs = x_refs, out_refs
        self.send_sems, self.recv_sems, self.local_sems = send_sems, recv_sems, local_sems
        x, y, c = _my_coords()
        self.c = c
        self.me, self.sibling = (x, y, c), (x, y, 1 - c)
        self.chips = [(1 - x, y), (x, 1 - y), (1 - x, 1 - y)]

    def _copy(self, a, k, block, to, src=None):
        px, py, pc = block
        slab = self.out_refs[a].at[4 * px + 2 * py + pc]
        return pltpu.make_async_remote_copy(
            src_ref=slab if src is None else src, dst_ref=slab,
            send_sem=self.send_sems.at[7 * a + k], recv_sem=self.recv_sems.at[7 * a + k],
            device_id=to, device_id_type=MESH)

    def _mine(self, a):
        px, py, pc = self.me
        return pltpu.make_async_copy(self.x_refs[a], self.out_refs[a].at[4 * px + 2 * py + pc], self.local_sems.at[a])

    def _first(self, a):
        cps = [self._copy(a, 0, self.me, self.sibling, src=self.x_refs[a])]
        cps += [self._copy(a, 1 + j, self.me, (*chip, self.c), src=self.x_refs[a]) for j, chip in enumerate(self.chips)]
        return cps

    def _passed(self, a):
        return [self._copy(a, 4 + j, (*chip, self.c), self.sibling) for j, chip in enumerate(self.chips)]

    def start(self):
        for a in range(len(self.x_refs)):
            self._mine(a).start()
            for cp in self._first(a):
                cp.start()

    def forward(self):
        for a in range(len(self.x_refs)):
            passed = self._passed(a)
            for j, chip in enumerate(self.chips):
                self._copy(a, 1 + j, (*chip, self.c), self.me).wait_recv()
                passed[j].start()

    def finish(self):
        for a in range(len(self.x_refs)):
            self._copy(a, 0, self.sibling, self.me).wait_recv()
            for j, chip in enumerate(self.chips):
                self._copy(a, 4 + j, (*chip, 1 - self.c), self.me).wait_recv()
            for cp in self._first(a) + self._passed(a):
                cp.wait_send()
            self._mine(a).wait()


class _DirectExchange:
    def __init__(self, s_refs, r_refs, send_sems, recv_sems, local_sems):
        self.s_refs, self.r_refs = s_refs, r_refs
        self.send_sems, self.recv_sems, self.local_sems = send_sems, recv_sems, local_sems
        x, y, c = _my_coords()
        self.me = 4 * x + 2 * y + c

    def _mine(self, a):
        return pltpu.make_async_copy(self.s_refs[a].at[self.me], self.r_refs[a].at[self.me], self.local_sems.at[a])

    def _send(self, a, k):
        peer, pid = _peer(k)
        return pltpu.make_async_remote_copy(self.s_refs[a].at[pid], self.r_refs[a].at[self.me],
                                            self.send_sems.at[7 * a + k - 1], self.recv_sems.at[7 * a + k - 1],
                                            device_id=peer, device_id_type=MESH)

    def _recv(self, a, k):
        peer, pid = _peer(k)
        return pltpu.make_async_remote_copy(self.s_refs[a].at[pid], self.r_refs[a].at[pid],
                                            self.send_sems.at[7 * a + k - 1], self.recv_sems.at[7 * a + k - 1],
                                            device_id=peer, device_id_type=MESH)

    def start(self):
        for a in range(len(self.s_refs)):
            self._mine(a).start()
            for k in range(1, N_DEV):
                self._send(a, k).start()

    def finish(self):
        for a in range(len(self.s_refs)):
            for k in range(1, N_DEV):
                self._recv(a, k).wait_recv()
            for k in range(1, N_DEV):
                self._send(a, k).wait_send()
            self._mine(a).wait()


def _small_gather(vecs):
    n = len(vecs)

    def body(*refs):
        v_refs, out_refs, s_sems, r_sems = refs[:n], refs[n:2 * n], refs[2 * n], refs[2 * n + 1]
        x, y, c = _my_coords()
        me = 4 * x + 2 * y + c
        sends = []
        for a in range(n):
            out_refs[a][me] = v_refs[a][...]
            for k in range(1, N_DEV):
                peer, _ = _peer(k)
                cp = pltpu.make_async_remote_copy(v_refs[a], out_refs[a].at[me], s_sems.at[7 * a + k - 1],
                                                  r_sems.at[7 * a + k - 1], device_id=peer, device_id_type=MESH)
                cp.start()
                sends.append(cp)
        for a in range(n):
            for k in range(1, N_DEV):
                peer, pid = _peer(k)
                pltpu.make_async_remote_copy(v_refs[a], out_refs[a].at[pid], s_sems.at[7 * a + k - 1],
                                             r_sems.at[7 * a + k - 1], device_id=peer, device_id_type=MESH).wait_recv()
        for cp in sends:
            cp.wait_send()

    vm = pl.BlockSpec(memory_space=pltpu.VMEM)
    return pl.pallas_call(
        body, name="small_gather",
        out_shape=tuple(jax.ShapeDtypeStruct((N_DEV, *v.shape), v.dtype) for v in vecs),
        in_specs=[vm] * n, out_specs=tuple([vm] * n),
        scratch_shapes=[pltpu.SemaphoreType.DMA((7 * n,))] * 2,
    )(*vecs)


def _load_resident(step_is_first, pairs, sem):
    @pl.when(step_is_first)
    def _():
        copies = [pltpu.make_async_copy(src, dst, sem.at[i]) for i, (src, dst) in enumerate(pairs)]
        for cp in copies:
            cp.start()
        for cp in copies:
            cp.wait()


def _inproj_fwd(x2d, sc1p, sh1, w_in_b):
    t = x2d.shape[0]
    tm = min(ROW_TILE, t)

    def body(x_ref, sc_ref, sh_ref, w_hbm, proj_ref, u_ref, w_vmem, sem):
        _load_resident(pl.program_id(0) == 0, [(w_hbm, w_vmem)], sem)
        xh, _ = _ln_stats(x_ref[...])
        ub = (xh * sc_ref[...] + sh_ref[...]).astype(BF16)
        u_ref[...] = ub
        proj_ref[...] = _dot(ub, w_vmem[...])

    row = lambda i: (i, 0)
    fix = lambda i: (0, 0)
    return pl.pallas_call(
        body, name="inproj_fwd", grid=(t // tm,),
        in_specs=[pl.BlockSpec((tm, D_MODEL), row), pl.BlockSpec((1, D_MODEL), fix), pl.BlockSpec((1, D_MODEL), fix),
                  pl.BlockSpec(memory_space=pl.ANY)],
        out_specs=(pl.BlockSpec((tm, D_IN_PAD), row), pl.BlockSpec((tm, D_MODEL), row)),
        out_shape=(jax.ShapeDtypeStruct((t, D_IN_PAD), F32), jax.ShapeDtypeStruct((t, D_MODEL), BF16)),
        scratch_shapes=[pltpu.VMEM((D_MODEL, D_IN_PAD), BF16), pltpu.SemaphoreType.DMA((1,))],
        compiler_params=pltpu.CompilerParams(dimension_semantics=("arbitrary",), vmem_limit_bytes=V7X_VMEM_LIMIT),
    )(x2d, sc1p, sh1, w_in_b)


CHUNK_SHIFT = 6


def _ret_consts(tl):
    r = lax.broadcasted_iota(jnp.int32, (tl, tl), 0)
    c = lax.broadcasted_iota(jnp.int32, (tl, tl), 1)
    allowed = jnp.right_shift(c, CHUNK_SHIFT) <= jnp.right_shift(r, CHUNK_SHIFT)
    dist = jnp.abs(r - c).astype(F32)
    rowf = lax.broadcasted_iota(jnp.int32, (tl, RET_D), 0).astype(F32)
    dec, qd, kd, gl = [], [], [], []
    for h in range(RET_HEADS):
        lg = _log_gamma(h)
        dec.append(jnp.where(allowed, jnp.exp(lg * dist), 0.0))
        qd.append(jnp.exp(lg * (rowf + 1.0)))
        kd.append(jnp.exp(lg * (tl - 1.0 - rowf)))
        gl.append(float(np.exp(np.float32(lg) * np.float32(tl))))
    return dict(dec=dec, qd=qd, kd=kd, gl=gl)


def _gla_consts(tl):
    r = lax.broadcasted_iota(jnp.int32, (tl, tl), 0)
    c = lax.broadcasted_iota(jnp.int32, (tl, tl), 1)
    ltri = (c <= r).astype(F32)
    utri = (c >= r).astype(F32)
    lane = lax.broadcasted_iota(jnp.int32, (1, GLA_KW), 1)
    hmask = [((lane >= h * GLA_DK) & (lane < (h + 1) * GLA_DK)).astype(F32) for h in range(GLA_HEADS)]
    rs = lax.broadcasted_iota(jnp.int32, (GLA_HEADS * tl, tl), 0) & (tl - 1)
    cs = lax.broadcasted_iota(jnp.int32, (GLA_HEADS * tl, tl), 1)
    lower = cs <= rs
    same = jnp.right_shift(cs, CHUNK_SHIFT) == jnp.right_shift(rs, CHUNK_SHIFT)
    upper = jnp.logical_and(jnp.logical_not(lower), same)
    return dict(ltri=ltri, utri=utri, hmask=hmask, lower=lower, upper=upper)


def _tile_rows(j, tl):
    return pl.ds(j * tl, tl) if isinstance(j, int) else pl.ds(pl.multiple_of(j * tl, tl), tl)


def _for_tiles(cps, fn):
    if cps == 1:
        fn(0, 0)
    else:
        lax.fori_loop(0, cps, fn, 0)


def _rotate(v, cosv, sinv):
    return v * cosv + pltpu.roll(v, RET_D // 2, 1) * sinv


def _rotate_t(d, cosv, sinv):
    return d * cosv + pltpu.roll(d * sinv, RET_D // 2, 1)


def _stack_heads(v, hmask):
    return jnp.concatenate([v * hmask[h] for h in range(GLA_HEADS)], axis=0)


def _gla_gates(glr, gw, gb, ltri, tl):
    z = _dot(glr, gw, NN, HIGHEST) + gb
    la = (jnp.minimum(z, 0.0) - jnp.log(1.0 + jnp.exp(-jnp.abs(z)))) * (1.0 / GATE_TAU)
    b = _dot(ltri, la, NN, HIGHEST)
    level = b[tl // 2 - 1:tl // 2, :]
    ep = jnp.exp(jnp.clip(b - level, -80.0, 80.0))
    em = jnp.exp(jnp.clip(level - b, -80.0, 80.0))
    bl = b[tl - 1:tl, :]
    return z, b, bl, ep, em


def _mixer_fwd(proj, cos_t, sin_t, gw_pad, gb, rnw, gnw, shards):
    t = proj.shape[0]
    tc = min(MIX_TILE, t)
    tr, tg = min(RET_SUB, tc), min(GLA_SUB, tc)
    nsteps = t // tc
    fwd_step = (3 * nsteps) // 4
    ns = len(shards)
    scale_r = RET_D ** -0.5
    scale_g = GLA_DK ** -0.5

    def body(rq_ref, rk_ref, rv_ref, rg_ref, gq_ref, gk_ref, gv_ref, gg_ref, glr_ref, cos_ref, sin_ref,
             gw_ref, gb_ref, rnw_ref, gnw_ref, *rest):
        shard_refs, rest = rest[:ns], rest[ns:]
        mix_ref, oraw_ref, rst_ref, sst_ref = rest[:4]
        gathered_refs, rest = rest[4:4 + ns], rest[4 + ns:]
        r_scr, s_scr, gs, gr, gl = rest
        step = pl.program_id(0)

        @pl.when(step == 0)
        def _():
            r_scr[...] = jnp.zeros_like(r_scr)
            s_scr[...] = jnp.zeros_like(s_scr)
            _TwoLevelGather(shard_refs, gathered_refs, gs, gr, gl).start()

        if fwd_step != nsteps - 1:
            @pl.when(step == fwd_step)
            def _():
                _TwoLevelGather(shard_refs, gathered_refs, gs, gr, gl).forward()

        ret_k, gla_k = _ret_consts(tr), _gla_consts(tg)

        def ret_tile(j, carry):
            k = ret_k
            rows = _tile_rows(j, tr)
            cosv, sinv = cos_ref[rows, :], sin_ref[rows, :]
            for h in range(RET_HEADS):
                cols = slice(h * RET_D, (h + 1) * RET_D)
                qr = _rotate(rq_ref[rows, cols], cosv, sinv) * scale_r
                kr = _rotate(rk_ref[rows, cols], cosv, sinv)
                vb = rv_ref[rows, cols].astype(BF16)
                qb, kb = qr.astype(BF16), kr.astype(BF16)
                p = _dot(qb, kb, NT) * k["dec"][h]
                rp = r_scr[cols, :]
                o = _dot(p.astype(BF16), vb) + _dot((qr * k["qd"][h]).astype(BF16), rp.astype(BF16))
                rst_ref[j, cols, :] = rp
                r_scr[cols, :] = k["gl"][h] * rp + _dot((kr * k["kd"][h]).astype(BF16), vb, TN)
                oraw_ref[rows, cols] = o
                oc = o - jnp.mean(o, axis=-1, keepdims=True)
                n = oc * lax.rsqrt(jnp.mean(oc * oc, axis=-1, keepdims=True) + LN_EPS)
                g = rg_ref[rows, cols]
                mix_ref[rows, cols] = (n * rnw_ref[:, cols] * (g * _sigmoid(g))).astype(BF16)
            return carry

        def gla_tile(j, carry):
            k = gla_k
            tl = tg
            rows = _tile_rows(j, tg)
            _, b, bl, ep, em = _gla_gates(glr_ref[rows, :], gw_ref[...], gb_ref[...], k["ltri"], tl)
            qs = gq_ref[rows, :] * scale_g
            kk = gk_ref[rows, :]
            x_all = _dot(_stack_heads(qs * ep, k["hmask"]).astype(BF16), (kk * em).astype(BF16), NT)
            y_all = _dot(_stack_heads(qs * em, k["hmask"]).astype(BF16), (kk * ep).astype(BF16), NT)
            a_all = jnp.where(k["lower"], x_all, jnp.where(k["upper"], y_all, 0.0)).astype(BF16)
            st = s_scr[...]
            oq = _dot(_stack_heads(qs * jnp.exp(b), k["hmask"]).astype(BF16), st.astype(BF16), NT)
            kg = kk * jnp.exp(bl - b)
            sst_ref[j] = st
            st_new = st * jnp.exp(bl)
            for h in range(GLA_HEADS):
                cols = slice(h * GLA_DV, (h + 1) * GLA_DV)
                hr = slice(h * tl, (h + 1) * tl)
                vb = gv_ref[rows, cols].astype(BF16)
                o = _dot(a_all[hr, :], vb) + oq[hr, :]
                st_new = st_new + _dot(vb, (kg * k["hmask"][h]).astype(BF16), TN)
                ocols = slice(RET_HEADS * RET_D + h * GLA_DV, RET_HEADS * RET_D + (h + 1) * GLA_DV)
                oraw_ref[rows, ocols] = o
                n = o * lax.rsqrt(jnp.mean(o * o, axis=-1, keepdims=True) + LN_EPS)
                g = gg_ref[rows, cols]
                mix_ref[rows, ocols] = (n * gnw_ref[:, cols] * (g * _sigmoid(g))).astype(BF16)
            s_scr[...] = st_new
            return carry

        _for_tiles(tc // tr, ret_tile)
        _for_tiles(tc // tg, gla_tile)

        @pl.when(step == nsteps - 1)
        def _():
            gather = _TwoLevelGather(shard_refs, gathered_refs, gs, gr, gl)
            if fwd_step == nsteps - 1:
                gather.forward()
            gather.finish()

    def col(width, off):
        return pl.BlockSpec((tc, width), lambda i, o=off // width: (i, o))

    fix = lambda i: (0, 0)
    hbm = pl.BlockSpec(memory_space=pl.ANY)
    in_specs = [col(512, OFF_RQ), col(512, OFF_RK), col(512, OFF_RV), col(512, OFF_RG),
                col(256, OFF_GQ), col(256, OFF_GK), col(512, OFF_GV), col(512, OFF_GG), col(128, OFF_GLR),
                pl.BlockSpec((tc, RET_D), lambda i: (i, 0)), pl.BlockSpec((tc, RET_D), lambda i: (i, 0)),
                pl.BlockSpec((128, GLA_KW), fix), pl.BlockSpec((1, GLA_KW), fix),
                pl.BlockSpec((1, 512), fix), pl.BlockSpec((1, 512), fix)] + [hbm] * ns
    out_specs = (pl.BlockSpec((tc, D_MODEL), lambda i: (i, 0)), pl.BlockSpec((tc, D_MODEL), lambda i: (i, 0)),
                 pl.BlockSpec((tc // tr, RET_HEADS * RET_D, RET_D), lambda i: (i, 0, 0)),
                 pl.BlockSpec((tc // tg, GLA_DV, GLA_KW), lambda i: (i, 0, 0))) + tuple([hbm] * ns)
    out_shape = (jax.ShapeDtypeStruct((t, D_MODEL), BF16), jax.ShapeDtypeStruct((t, D_MODEL), F32),
                 jax.ShapeDtypeStruct((t // tr, RET_HEADS * RET_D, RET_D), F32),
                 jax.ShapeDtypeStruct((t // tg, GLA_DV, GLA_KW), F32)) + tuple(
                     jax.ShapeDtypeStruct((N_DEV, *s.shape), s.dtype) for s in shards)
    return pl.pallas_call(
        body, name="mixer_fwd", grid=(nsteps,), in_specs=in_specs, out_specs=out_specs, out_shape=out_shape,
        scratch_shapes=[pltpu.VMEM((RET_HEADS * RET_D, RET_D), F32), pltpu.VMEM((GLA_DV, GLA_KW), F32),
                        pltpu.SemaphoreType.DMA((7 * ns,)), pltpu.SemaphoreType.DMA((7 * ns,)),
                        pltpu.SemaphoreType.DMA((ns,))],
        compiler_params=pltpu.CompilerParams(dimension_semantics=("arbitrary",), vmem_limit_bytes=V7X_VMEM_LIMIT),
    )(*([proj] * 9), cos_t, sin_t, gw_pad, gb, rnw, gnw, *shards)


def _mid_fwd(mixed, x2d, target, vecs, w_out_b, w1_b, w2_b):
    t = x2d.shape[0]
    tm = min(ROW_TILE, t)

    def body(mix_ref, x_ref, tgt_ref, v_ref, wo_hbm, w1_hbm, w2_hbm,
             m_ref, x1n_ref, rstd_ref, u2_ref, a_ref, df_ref, dh2_ref, acc_ref, wo, w1, w2, sem):
        first = pl.program_id(0) == 0
        _load_resident(first, [(wo_hbm, wo), (w1_hbm, w1), (w2_hbm, w2)], sem)

        @pl.when(first)
        def _():
            acc_ref[...] = jnp.zeros_like(acc_ref)

        gate1, sc2p, sh2, gate2 = v_ref[0:1, :], v_ref[1:2, :], v_ref[2:3, :], v_ref[3:4, :]
        l1w, l1b, l2w, l2b = v_ref[4:5, :], v_ref[5:6, :], v_ref[6:7, :], v_ref[7:8, :]
        m = _dot(mix_ref[...], wo[...])
        m_ref[...] = m.astype(BF16)
        x1n, rstd1 = _ln_stats(ALPHA * x_ref[...] + gate1 * m)
        x1n_ref[...] = x1n
        rstd_ref[...] = rstd1
        x1 = x1n * l1w + l1b
        xh1, _ = _ln_stats(x1)
        u2 = (xh1 * sc2p + sh2).astype(BF16)
        u2_ref[...] = u2
        f = jnp.zeros((tm, D_MODEL), F32)
        for j in range(N_DEV):
            cols = slice(j * FF_COLS, (j + 1) * FF_COLS)
            a = _dot(u2, w1[j])
            a_ref[:, cols] = a.astype(BF16)
            r = jnp.maximum(a, 0.0)
            f = f + _dot((r * r).astype(BF16), w2[cols, :])
        yh, rstd2 = _ln_stats(ALPHA * x1 + gate2 * f)
        e = yh * l2w + l2b - tgt_ref[...]
        dy = e * (1.0 / D_MODEL)
        dh2 = _ln_bwd(dy * l2w, yh, rstd2)
        dh2_ref[...] = dh2
        df_ref[...] = (dh2 * gate2).astype(BF16)
        acc_ref[0:1, :] += jnp.sum(dy * yh, axis=0, keepdims=True)
        acc_ref[1:2, :] += jnp.sum(dy, axis=0, keepdims=True)
        acc_ref[2:3, :] += jnp.sum(dh2 * f, axis=0, keepdims=True)
        acc_ref[3:4, :] += jnp.sum(e * e, axis=0, keepdims=True) * (0.5 / D_MODEL)

    row = lambda i: (i, 0)
    fix = lambda i: (0, 0)
    hbm = pl.BlockSpec(memory_space=pl.ANY)
    return pl.pallas_call(
        body, name="mid_fwd", grid=(t // tm,),
        in_specs=[pl.BlockSpec((tm, D_MODEL), row), pl.BlockSpec((tm, D_MODEL), row), pl.BlockSpec((tm, D_MODEL), row),
                  pl.BlockSpec((8, D_MODEL), fix), hbm, hbm, hbm],
        out_specs=(pl.BlockSpec((tm, D_MODEL), row), pl.BlockSpec((tm, D_MODEL), row), pl.BlockSpec((tm, 1), row),
                   pl.BlockSpec((tm, D_MODEL), row), pl.BlockSpec((tm, D_FF), row), pl.BlockSpec((tm, D_MODEL), row),
                   pl.BlockSpec((tm, D_MODEL), row), pl.BlockSpec((8, D_MODEL), fix)),
        out_shape=(jax.ShapeDtypeStruct((t, D_MODEL), BF16), jax.ShapeDtypeStruct((t, D_MODEL), F32),
                   jax.ShapeDtypeStruct((t, 1), F32), jax.ShapeDtypeStruct((t, D_MODEL), BF16),
                   jax.ShapeDtypeStruct((t, D_FF), BF16), jax.ShapeDtypeStruct((t, D_MODEL), BF16),
                   jax.ShapeDtypeStruct((t, D_MODEL), F32), jax.ShapeDtypeStruct((8, D_MODEL), F32)),
        scratch_shapes=[pltpu.VMEM((D_MODEL, D_MODEL), BF16), pltpu.VMEM((N_DEV, D_MODEL, FF_COLS), BF16),
                        pltpu.VMEM((D_FF, D_MODEL), BF16), pltpu.SemaphoreType.DMA((3,))],
        compiler_params=pltpu.CompilerParams(dimension_semantics=("arbitrary",), vmem_limit_bytes=V7X_VMEM_LIMIT),
    )(mixed, x2d, target, vecs, w_out_b, w1_b, w2_b)


def _ffn_bwd(df, a, dh2, x1n, rstd1, m, vecs, w_out_b, w1_b, w2_b):
    t = x1n.shape[0]
    tm = min(ROW_TILE, t)

    def body(df_ref, a_ref, dh2_ref, x1n_ref, rstd_ref, m_ref, v_ref, wo_hbm, w1_hbm, w2_hbm,
             da_ref, dm_ref, dmix_ref, dxa_ref, acc_ref, wo, w1, w2, sem):
        first = pl.program_id(0) == 0
        _load_resident(first, [(wo_hbm, wo), (w1_hbm, w1), (w2_hbm, w2)], sem)

        @pl.when(first)
        def _():
            acc_ref[...] = jnp.zeros_like(acc_ref)

        gate1, sc2p, l1w, l1b = v_ref[0:1, :], v_ref[1:2, :], v_ref[2:3, :], v_ref[3:4, :]
        df = df_ref[...]
        du2 = jnp.zeros((tm, D_MODEL), F32)
        for j in range(N_DEV):
            cols = slice(j * FF_COLS, (j + 1) * FF_COLS)
            dr2 = _dot(df, w2[cols, :], NT)
            da = (dr2 * (2.0 * jnp.maximum(a_ref[:, cols].astype(F32), 0.0))).astype(BF16)
            da_ref[:, cols] = da
            du2 = du2 + _dot(da, w1[j], NT)
        x1n = x1n_ref[...]
        xh1, rstd0 = _ln_stats(x1n * l1w + l1b)
        dx1 = ALPHA * dh2_ref[...] + _ln_bwd(du2 * sc2p, xh1, rstd0)
        dh1 = _ln_bwd(dx1 * l1w, x1n, rstd_ref[...])
        dxa_ref[...] = ALPHA * dh1
        dm = (dh1 * gate1).astype(BF16)
        dm_ref[...] = dm
        dmix_ref[...] = _dot(dm, wo[...], NT)
        acc_ref[0:1, :] += jnp.sum(du2 * xh1, axis=0, keepdims=True)
        acc_ref[1:2, :] += jnp.sum(du2, axis=0, keepdims=True)
        acc_ref[2:3, :] += jnp.sum(dx1 * x1n, axis=0, keepdims=True)
        acc_ref[3:4, :] += jnp.sum(dx1, axis=0, keepdims=True)
        acc_ref[4:5, :] += jnp.sum(dh1 * m_ref[...].astype(F32), axis=0, keepdims=True)

    row = lambda i: (i, 0)
    fix = lambda i: (0, 0)
    hbm = pl.BlockSpec(memory_space=pl.ANY)
    return pl.pallas_call(
        body, name="ffn_bwd", grid=(t // tm,),
        in_specs=[pl.BlockSpec((tm, D_MODEL), row), pl.BlockSpec((tm, D_FF), row), pl.BlockSpec((tm, D_MODEL), row),
                  pl.BlockSpec((tm, D_MODEL), row), pl.BlockSpec((tm, 1), row), pl.BlockSpec((tm, D_MODEL), row),
                  pl.BlockSpec((8, D_MODEL), fix), hbm, hbm, hbm],
        out_specs=(pl.BlockSpec((tm, D_FF), row), pl.BlockSpec((tm, D_MODEL), row), pl.BlockSpec((tm, D_MODEL), row),
                   pl.BlockSpec((tm, D_MODEL), row), pl.BlockSpec((8, D_MODEL), fix)),
        out_shape=(jax.ShapeDtypeStruct((t, D_FF), BF16), jax.ShapeDtypeStruct((t, D_MODEL), BF16),
                   jax.ShapeDtypeStruct((t, D_MODEL), F32), jax.ShapeDtypeStruct((t, D_MODEL), F32),
                   jax.ShapeDtypeStruct((8, D_MODEL), F32)),
        scratch_shapes=[pltpu.VMEM((D_MODEL, D_MODEL), BF16), pltpu.VMEM((N_DEV, D_MODEL, FF_COLS), BF16),
                        pltpu.VMEM((D_FF, D_MODEL), BF16), pltpu.SemaphoreType.DMA((3,))],
        compiler_params=pltpu.CompilerParams(dimension_semantics=("arbitrary",), vmem_limit_bytes=V7X_VMEM_LIMIT),
    )(df, a, dh2, x1n, rstd1, m, vecs, w_out_b, w1_b, w2_b)


def _matmul_tn(lhs, rhs, tmm, tn, name, relu_sq=False, col_slab=None):
    t, mm = lhs.shape
    nn = rhs.shape[1]
    tk = min(512, t)
    nk = t // tk

    def body(l_ref, r_ref, o_ref, acc):
        kk = pl.program_id(2)

        @pl.when(kk == 0)
        def _():
            acc[...] = jnp.zeros_like(acc)

        l = l_ref[...]
        if relu_sq:
            lf = jnp.maximum(l.astype(F32), 0.0)
            l = (lf * lf).astype(BF16)
        acc[...] += _dot(l, r_ref[...], TN)

        @pl.when(kk == nk - 1)
        def _():
            if col_slab is None:
                o_ref[...] = acc[...].astype(o_ref.dtype)
            else:
                for s in range(tn // col_slab):
                    o_ref[s] = acc[:, s * col_slab:(s + 1) * col_slab].astype(o_ref.dtype)

    if col_slab is None:
        out_spec = pl.BlockSpec((tmm, tn), lambda i, j, k: (i, j))
        out_shape = jax.ShapeDtypeStruct((mm, nn), BF16)
    else:
        out_spec = pl.BlockSpec((tn // col_slab, tmm, col_slab), lambda i, j, k: (j, i, 0))
        out_shape = jax.ShapeDtypeStruct((nn // col_slab, mm, col_slab), BF16)
    return pl.pallas_call(
        body, name=name, grid=(mm // tmm, nn // tn, nk),
        in_specs=[pl.BlockSpec((tk, tmm), lambda i, j, k: (k, i)), pl.BlockSpec((tk, tn), lambda i, j, k: (k, j))],
        out_specs=out_spec,
        out_shape=out_shape,
        scratch_shapes=[pltpu.VMEM((tmm, tn), F32)],
        compiler_params=pltpu.CompilerParams(dimension_semantics=("arbitrary", "arbitrary", "arbitrary"),
                                             vmem_limit_bytes=V7X_VMEM_LIMIT),
    )(lhs, rhs)


def _mixer_bwd(dmix, proj, oraw, cos_t, sin_t, rst, sst, gw_pad, gb, rnw, gnw, slabs):
    t = proj.shape[0]
    tc = min(MIX_TILE, t)
    tr, tg = min(RET_SUB, tc), min(GLA_SUB, tc)
    nsteps = t // tc
    ns = len(slabs)
    scale_r = RET_D ** -0.5
    scale_g = GLA_DK ** -0.5

    def body(dmix_ref, rq_ref, rk_ref, rv_ref, rg_ref, gq_ref, gk_ref, gv_ref, gg_ref, glr_ref, oraw_ref,
             cos_ref, sin_ref, rst_ref, sst_ref, gw_ref, gb_ref, rnw_ref, gnw_ref, *rest):
        send_refs, rest = rest[:ns], rest[ns:]
        dproj_ref, dgw_ref, dvec_ref = rest[:3]
        recv_refs, rest = rest[3:3 + ns], rest[3 + ns:]
        dr_scr, ds_scr, xs, xr, xl = rest
        step = pl.program_id(0)

        @pl.when(step == 0)
        def _():
            dr_scr[...] = jnp.zeros_like(dr_scr)
            ds_scr[...] = jnp.zeros_like(ds_scr)
            dgw_ref[...] = jnp.zeros_like(dgw_ref)
            dvec_ref[...] = jnp.zeros_like(dvec_ref)
            _DirectExchange(send_refs, recv_refs, xs, xr, xl).start()

        ret_k, gla_k = _ret_consts(tr), _gla_consts(tg)
        last_row = lax.broadcasted_iota(jnp.int32, (tg, GLA_KW), 0) == tg - 1

        def ret_tile(jj, carry):
            k = ret_k
            j = tc // tr - 1 - jj
            rows = _tile_rows(j, tr)
            cosv, sinv = cos_ref[rows, :], sin_ref[rows, :]
            for h in range(RET_HEADS):
                cols = slice(h * RET_D, (h + 1) * RET_D)
                o = oraw_ref[rows, cols]
                g = rg_ref[rows, cols]
                w = rnw_ref[:, cols]
                dout = dmix_ref[rows, cols]
                oc = o - jnp.mean(o, axis=-1, keepdims=True)
                inv = lax.rsqrt(jnp.mean(oc * oc, axis=-1, keepdims=True) + LN_EPS)
                n = oc * inv
                sg = _sigmoid(g)
                sil = g * sg
                dn = dout * w * sil
                dvec_ref[0:1, cols] += jnp.sum(dout * n * sil, axis=0, keepdims=True)
                dproj_ref[rows, OFF_RG + h * RET_D:OFF_RG + (h + 1) * RET_D] = (
                    dout * n * w * (sg * (1.0 + g * (1.0 - sg)))).astype(BF16)
                doc = inv * (dn - n * jnp.mean(dn * n, axis=-1, keepdims=True))
                do = doc - jnp.mean(doc, axis=-1, keepdims=True)

                qr = _rotate(rq_ref[rows, cols], cosv, sinv) * scale_r
                kr = _rotate(rk_ref[rows, cols], cosv, sinv)
                vb = rv_ref[rows, cols].astype(BF16)
                qb, kb, dob = qr.astype(BF16), kr.astype(BF16), do.astype(BF16)
                p = _dot(qb, kb, NT) * k["dec"][h]
                rp = rst_ref[j, cols, :].astype(BF16)
                dr = dr_scr[cols, :]
                drb = dr.astype(BF16)
                dpb = (_dot(dob, vb, NT) * k["dec"][h]).astype(BF16)
                dqr = _dot(dpb, kb) + _dot(dob, rp, NT) * k["qd"][h]
                dkr = _dot(dpb, qb, TN) + _dot(vb, drb, NT) * k["kd"][h]
                dv = _dot(p.astype(BF16), dob, TN) + _dot((kr * k["kd"][h]).astype(BF16), drb)
                dr_scr[cols, :] = k["gl"][h] * dr + _dot((qr * k["qd"][h]).astype(BF16), dob, TN)
                dproj_ref[rows, OFF_RQ + h * RET_D:OFF_RQ + (h + 1) * RET_D] = (
                    _rotate_t(dqr, cosv, sinv) * scale_r).astype(BF16)
                dproj_ref[rows, OFF_RK + h * RET_D:OFF_RK + (h + 1) * RET_D] = _rotate_t(dkr, cosv, sinv).astype(BF16)
                dproj_ref[rows, OFF_RV + h * RET_D:OFF_RV + (h + 1) * RET_D] = dv.astype(BF16)
            return carry

        def gla_tile(jj, carry):
            k = gla_k
            tl = tg
            j = tc // tg - 1 - jj
            rows = _tile_rows(j, tg)
            glr = glr_ref[rows, :]
            z, b, bl, ep, em = _gla_gates(glr, gw_ref[...], gb_ref[...], k["ltri"], tl)
            qs = gq_ref[rows, :] * scale_g
            kk = gk_ref[rows, :]
            eb = jnp.exp(b)
            ekb = jnp.exp(bl - b)
            ebl = jnp.exp(bl)
            ql, qu, kl, ku = qs * ep, qs * em, kk * em, kk * ep
            qg, kg = qs * eb, kk * ekb
            qlm = _stack_heads(ql, k["hmask"]).astype(BF16)
            qum = _stack_heads(qu, k["hmask"]).astype(BF16)
            klb, kub = kl.astype(BF16), ku.astype(BF16)
            a_all = jnp.where(k["lower"], _dot(qlm, klb, NT),
                              jnp.where(k["upper"], _dot(qum, kub, NT), 0.0)).astype(BF16)
            st = sst_ref[j]
            stb = st.astype(BF16)
            ds = ds_scr[...]
            dsb = ds.astype(BF16)
            ds_new = ds * ebl
            da_parts = []
            dqg = jnp.zeros((tl, GLA_KW), F32)
            dkg = jnp.zeros((tl, GLA_KW), F32)
            for h in range(GLA_HEADS):
                cols = slice(h * GLA_DV, (h + 1) * GLA_DV)
                hr = slice(h * tl, (h + 1) * tl)
                ocols = slice(RET_HEADS * RET_D + h * GLA_DV, RET_HEADS * RET_D + (h + 1) * GLA_DV)
                o = oraw_ref[rows, ocols]
                g = gg_ref[rows, cols]
                w = gnw_ref[:, cols]
                dout = dmix_ref[rows, ocols]
                inv = lax.rsqrt(jnp.mean(o * o, axis=-1, keepdims=True) + LN_EPS)
                n = o * inv
                sg = _sigmoid(g)
                sil = g * sg
                dn = dout * w * sil
                dvec_ref[1:2, cols] += jnp.sum(dout * n * sil, axis=0, keepdims=True)
                dproj_ref[rows, OFF_GG + h * GLA_DV:OFF_GG + (h + 1) * GLA_DV] = (
                    dout * n * w * (sg * (1.0 + g * (1.0 - sg)))).astype(BF16)
                dob = (inv * (dn - n * jnp.mean(dn * n, axis=-1, keepdims=True))).astype(BF16)
                vb = gv_ref[rows, cols].astype(BF16)
                mh = k["hmask"][h]
                da_parts.append(_dot(dob, vb, NT))
                dv = _dot(a_all[hr, :], dob, TN) + _dot((kg * mh).astype(BF16), dsb, NT)
                dproj_ref[rows, OFF_GV + h * GLA_DV:OFF_GV + (h + 1) * GLA_DV] = dv.astype(BF16)
                dkg = dkg + mh * _dot(vb, dsb)
                dqg = dqg + mh * _dot(dob, stb)
                ds_new = ds_new + _dot(dob, (qg * mh).astype(BF16), TN)
            da_all = jnp.concatenate(da_parts, axis=0)
            dal = jnp.where(k["lower"], da_all, 0.0).astype(BF16)
            dau = jnp.where(k["upper"], da_all, 0.0).astype(BF16)
            dqlm = _dot(dal, klb)
            dqum = _dot(dau, kub)
            dql = jnp.zeros((tl, GLA_KW), F32)
            dqu = jnp.zeros((tl, GLA_KW), F32)
            for h in range(GLA_HEADS):
                hr = slice(h * tl, (h + 1) * tl)
                dql = dql + k["hmask"][h] * dqlm[hr, :]
                dqu = dqu + k["hmask"][h] * dqum[hr, :]
            dkl = _dot(dal, qlm, TN)
            dku = _dot(dau, qum, TN)
            dbl = (jnp.sum(dkg * kg, axis=0, keepdims=True)
                   + jnp.sum(ds * st, axis=0, keepdims=True) * ebl)
            ds_scr[...] = ds_new
            dqs = dql * ep + dqu * em + dqg * eb
            dk = dkl * em + dku * ep + dkg * ekb
            db = dql * ql - dkl * kl - dqu * qu + dku * ku + dqg * qg - dkg * kg
            db = db + jnp.where(last_row, dbl, 0.0)
            dla = _dot(k["utri"], db, NN, HIGHEST)
            dz = dla * (1.0 / GATE_TAU) * _sigmoid(-z)
            dvec_ref[2:3, 0:GLA_KW] += jnp.sum(dz, axis=0, keepdims=True)
            dgw_ref[...] += _dot(glr, dz, TN, HIGHEST)
            dproj_ref[rows, OFF_GLR:OFF_GLR + 128] = _dot(dz, gw_ref[...], NT, HIGHEST).astype(BF16)
            dproj_ref[rows, OFF_GQ:OFF_GQ + GLA_KW] = (dqs * scale_g).astype(BF16)
            dproj_ref[rows, OFF_GK:OFF_GK + GLA_KW] = dk.astype(BF16)
            return carry

        _for_tiles(tc // tr, ret_tile)
        _for_tiles(tc // tg, gla_tile)

        @pl.when(step == nsteps - 1)
        def _():
            _DirectExchange(send_refs, recv_refs, xs, xr, xl).finish()

    rev = lambda i: (nsteps - 1 - i, 0)

    def col(width, off):
        return pl.BlockSpec((tc, width), lambda i, o=off // width: (nsteps - 1 - i, o))

    fix = lambda i: (0, 0)
    hbm = pl.BlockSpec(memory_space=pl.ANY)
    in_specs = [pl.BlockSpec((tc, D_MODEL), rev),
                col(512, OFF_RQ), col(512, OFF_RK), col(512, OFF_RV), col(512, OFF_RG),
                col(256, OFF_GQ), col(256, OFF_GK), col(512, OFF_GV), col(512, OFF_GG), col(128, OFF_GLR),
                pl.BlockSpec((tc, D_MODEL), rev), pl.BlockSpec((tc, RET_D), rev), pl.BlockSpec((tc, RET_D), rev),
                pl.BlockSpec((tc // tr, RET_HEADS * RET_D, RET_D), lambda i: (nsteps - 1 - i, 0, 0)),
                pl.BlockSpec((tc // tg, GLA_DV, GLA_KW), lambda i: (nsteps - 1 - i, 0, 0)),
                pl.BlockSpec((128, GLA_KW), fix), pl.BlockSpec((1, GLA_KW), fix),
                pl.BlockSpec((1, 512), fix), pl.BlockSpec((1, 512), fix)] + [hbm] * ns
    out_specs = (pl.BlockSpec((tc, D_IN_PAD), rev), pl.BlockSpec((128, GLA_KW), fix),
                 pl.BlockSpec((8, 512), fix)) + tuple([hbm] * ns)
    out_shape = (jax.ShapeDtypeStruct((t, D_IN_PAD), BF16), jax.ShapeDtypeStruct((128, GLA_KW), F32),
                 jax.ShapeDtypeStruct((8, 512), F32)) + tuple(jax.ShapeDtypeStruct(s.shape, s.dtype) for s in slabs)
    return pl.pallas_call(
        body, name="mixer_bwd", grid=(nsteps,), in_specs=in_specs, out_specs=out_specs, out_shape=out_shape,
        scratch_shapes=[pltpu.VMEM((RET_HEADS * RET_D, RET_D), F32), pltpu.VMEM((GLA_DV, GLA_KW), F32),
                        pltpu.SemaphoreType.DMA((7 * ns,)), pltpu.SemaphoreType.DMA((7 * ns,)),
                        pltpu.SemaphoreType.DMA((ns,))],
        compiler_params=pltpu.CompilerParams(dimension_semantics=("arbitrary",), vmem_limit_bytes=V7X_VMEM_LIMIT),
    )(dmix, *([proj] * 9), oraw, cos_t, sin_t, rst, sst, gw_pad, gb, rnw, gnw, *slabs)


def _inproj_bwd(dproj, x2d, dxa, sc1p, w_in_b, slab):
    t = x2d.shape[0]
    tm = min(ROW_TILE, t)
    nsteps = t // tm

    def body(dp_ref, x_ref, dxa_ref, sc_ref, w_hbm, send_ref, gx_ref, acc_ref, recv_ref, w_vmem, sem, xs, xr, xl):
        step = pl.program_id(0)
        first = step == 0
        _load_resident(first, [(w_hbm, w_vmem)], sem)

        @pl.when(first)
        def _():
            acc_ref[...] = jnp.zeros_like(acc_ref)
            _DirectExchange([send_ref], [recv_ref], xs, xr, xl).start()

        du = _dot(dp_ref[...], w_vmem[...], NT)
        xh, rstd = _ln_stats(x_ref[...])
        gx_ref[...] = dxa_ref[...] + _ln_bwd(du * sc_ref[...], xh, rstd)
        acc_ref[0:1, :] += jnp.sum(du * xh, axis=0, keepdims=True)
        acc_ref[1:2, :] += jnp.sum(du, axis=0, keepdims=True)

        @pl.when(step == nsteps - 1)
        def _():
            _DirectExchange([send_ref], [recv_ref], xs, xr, xl).finish()

    row = lambda i: (i, 0)
    fix = lambda i: (0, 0)
    hbm = pl.BlockSpec(memory_space=pl.ANY)
    return pl.pallas_call(
        body, name="inproj_bwd", grid=(nsteps,),
        in_specs=[pl.BlockSpec((tm, D_IN_PAD), row), pl.BlockSpec((tm, D_MODEL), row), pl.BlockSpec((tm, D_MODEL), row),
                  pl.BlockSpec((1, D_MODEL), fix), hbm, hbm],
        out_specs=(pl.BlockSpec((tm, D_MODEL), row), pl.BlockSpec((8, D_MODEL), fix), hbm),
        out_shape=(jax.ShapeDtypeStruct((t, D_MODEL), F32), jax.ShapeDtypeStruct((8, D_MODEL), F32),
                   jax.ShapeDtypeStruct(slab.shape, slab.dtype)),
        scratch_shapes=[pltpu.VMEM((D_MODEL, D_IN_PAD), BF16), pltpu.SemaphoreType.DMA((1,)),
                        pltpu.SemaphoreType.DMA((7,)), pltpu.SemaphoreType.DMA((7,)), pltpu.SemaphoreType.DMA((1,))],
        compiler_params=pltpu.CompilerParams(dimension_semantics=("arbitrary",), vmem_limit_bytes=V7X_VMEM_LIMIT),
    )(dproj, x2d, dxa, sc1p, w_in_b, slab)


def _adam_math(w, g, m, v):
    m = ADAM_B1 * m + (1.0 - ADAM_B1) * g
    v = ADAM_B2 * v + (1.0 - ADAM_B2) * (g * g)
    m_hat = m / (1.0 - ADAM_B1 ** ADAM_STEP)
    v_hat = v / (1.0 - ADAM_B2 ** ADAM_STEP)
    delta = -ADAM_LR * (m_hat / (jnp.sqrt(v_hat) + ADAM_EPS) + ADAM_WD * w)
    return delta, m, v


def _adamw(w, gparts, m, v, name):
    nparts, rows, cols = gparts.shape
    tr = rows
    for cand in (512, 256, 128, 64, 32, 16, 8):
        if rows % cand == 0:
            tr = cand
            break

    def body(w_ref, g_ref, m_ref, v_ref, go_ref, d_ref, mo_ref, vo_ref):
        g = g_ref[0].astype(F32)
        for p in range(1, nparts):
            g = g + g_ref[p].astype(F32)
        delta, mn, vn = _adam_math(w_ref[...], g, m_ref[...], v_ref[...])
        go_ref[...] = g
        d_ref[...] = delta
        mo_ref[...] = mn
        vo_ref[...] = vn

    blk = pl.BlockSpec((tr, cols), lambda i: (i, 0))
    shp = jax.ShapeDtypeStruct((rows, cols), F32)
    return pl.pallas_call(
        body, name=name, grid=(rows // tr,),
        in_specs=[blk, pl.BlockSpec((nparts, tr, cols), lambda i: (0, i, 0)), blk, blk],
        out_specs=(blk, blk, blk, blk), out_shape=(shp, shp, shp, shp),
        compiler_params=pltpu.CompilerParams(dimension_semantics=("arbitrary",), vmem_limit_bytes=V7X_VMEM_LIMIT),
    )(w, gparts, m, v)


def _small_reduce(gathered, gathered_gw, c_all, dmod_cols):
    def body(g_ref, gw_ref, c_ref, dm_ref, sum_ref, gwsum_ref, gb_ref, gwa_ref):
        s = g_ref[0]
        sw = gw_ref[0]
        for p in range(1, N_DEV):
            s = s + g_ref[p]
            sw = sw + gw_ref[p]
        sum_ref[...] = s
        gwsum_ref[...] = sw
        for i in range(6):
            gb_ref[:, i * D_MODEL:(i + 1) * D_MODEL] = s[i:i + 1, :]
        cc = c_ref[...]
        gwa_ref[...] = _dot(cc * _sigmoid(cc), dm_ref[...], TN, HIGHEST)

    vm = pl.BlockSpec(memory_space=pltpu.VMEM)
    return pl.pallas_call(
        body, name="small_reduce",
        out_shape=(jax.ShapeDtypeStruct(gathered.shape[1:], F32), jax.ShapeDtypeStruct(gathered_gw.shape[1:], F32),
                   jax.ShapeDtypeStruct((1, 6 * D_MODEL), F32), jax.ShapeDtypeStruct((D_MODEL, ADA_COLS), F32)),
        in_specs=[vm] * 4, out_specs=(vm, vm, vm, vm),
        compiler_params=pltpu.CompilerParams(vmem_limit_bytes=V7X_VMEM_LIMIT),
    )(gathered, gathered_gw, c_all, dmod_cols)


SMR_LN1W, SMR_LN1B, SMR_LN2W, SMR_LN2B, SMR_NORMS, SMR_MISC = 6, 7, 8, 9, 10, 11


def _adamw_small(gsum, g_b_ada, g_ggw, params, moms, vels):
    n = len(params)

    def body(*refs):
        gsum_ref, gb_ref, gw_ref = refs[:3]
        w_refs, m_refs, v_refs = refs[3:3 + n], refs[3 + n:3 + 2 * n], refs[3 + 2 * n:3 + 3 * n]
        outs = refs[3 + 3 * n:]
        g_refs, d_refs, mo_refs, vo_refs = outs[:n - 1], outs[n - 1:2 * n - 1], outs[2 * n - 1:3 * n - 1], outs[3 * n - 1:]
        grads = [gb_ref[...],
                 gsum_ref[SMR_NORMS:SMR_NORMS + 1, 0:512],
                 gsum_ref[SMR_MISC:SMR_MISC + 1, 0:GLA_KW],
                 gsum_ref[SMR_NORMS:SMR_NORMS + 1, 512:1024],
                 gsum_ref[SMR_LN1W:SMR_LN1W + 1, :], gsum_ref[SMR_LN1B:SMR_LN1B + 1, :],
                 gsum_ref[SMR_LN2W:SMR_LN2W + 1, :], gsum_ref[SMR_LN2B:SMR_LN2B + 1, :],
                 gw_ref[...]]
        for i in range(n):
            delta, mn, vn = _adam_math(w_refs[i][...], grads[i], m_refs[i][...], v_refs[i][...])
            if i < n - 1:
                g_refs[i][...] = grads[i]
            d_refs[i][...] = delta
            mo_refs[i][...] = mn
            vo_refs[i][...] = vn

    vm = pl.BlockSpec(memory_space=pltpu.VMEM)
    shapes = [jax.ShapeDtypeStruct(p.shape, F32) for p in params]
    n_in = 3 + 3 * n
    out_shape = tuple(shapes[:n - 1] + shapes * 3)
    return pl.pallas_call(
        body, name="adamw_small", out_shape=out_shape,
        in_specs=[vm] * n_in, out_specs=tuple([vm] * len(out_shape)),
        compiler_params=pltpu.CompilerParams(vmem_limit_bytes=V7X_VMEM_LIMIT),
    )(gsum, g_b_ada, g_ggw, *params, *moms, *vels)


def kernel(x, c, w_ada, b_ada, w_in, ret_norm_w, gla_gate_w, gla_gate_b, gla_norm_w, w_out, ln1_w, ln1_b, w_ff1, w_ff2, ln2_w, ln2_b, loss_target, m_w_ada, m_b_ada, m_w_in, m_ret_norm_w, m_gla_gate_w, m_gla_gate_b, m_gla_norm_w, m_w_out, m_ln1_w, m_ln1_b, m_w_ff1, m_w_ff2, m_ln2_w, m_ln2_b, v_w_ada, v_b_ada, v_w_in, v_ret_norm_w, v_gla_gate_w, v_gla_gate_b, v_gla_norm_w, v_w_out, v_ln1_w, v_ln1_b, v_w_ff1, v_w_ff2, v_ln2_w, v_ln2_b):
    t = x.shape[1]
    xi, yi, ci = _my_coords()
    me = 4 * xi + 2 * yi + ci
    x2d = x[0]
    tgt = loss_target[0]

    c_ext = jnp.concatenate([c, gla_gate_w[0].reshape(1, GATE_RANK * GLA_KW // N_DEV)], axis=1)
    b_l = lax.dynamic_slice(b_ada, (0, me * ADA_COLS), (1, ADA_COLS))
    c_all3, mod_all, wi_g = _adaln_mod(c_ext, w_ada[0], b_l, w_in[0].astype(BF16))
    c_all = c_all3[:, 0, :D_MODEL]
    gate_w = c_all3[:, 0, D_MODEL:].reshape(N_DEV, GATE_RANK, GLA_KW // N_DEV)
    gate_w = gate_w.transpose(1, 0, 2).reshape(GATE_RANK, GLA_KW)
    gw_pad = jnp.zeros((128, GLA_KW), F32).at[:GATE_RANK].set(gate_w)
    mod = lax.dynamic_slice(mod_all, (0, me, 0), (N_DEV, 1, ADA_COLS)).reshape(6, D_MODEL)
    shift1, scale1, gate1, shift2, scale2, gate2 = [mod[i:i + 1] for i in range(6)]

    w_in_b = jnp.pad(wi_g.transpose(1, 0, 2).reshape(D_MODEL, D_IN), ((0, 0), (0, D_IN_PAD - D_IN)))

    pos = jnp.arange(t, dtype=F32)
    inv = 1.0 / (10000.0 ** jnp.linspace(0.0, 1.0, RET_D // 2, dtype=F32))
    ang = pos[:, None] * inv[None, :]
    cos_t = jnp.concatenate([jnp.cos(ang), jnp.cos(ang)], axis=1)
    sin_t = jnp.concatenate([-jnp.sin(ang), jnp.sin(ang)], axis=1)

    sc1p = 1.0 + scale1
    proj, u = _inproj_fwd(x2d, sc1p, shift1, w_in_b)
    mixed, oraw, rst, sst, wo_g, w1_b, w2_g = _mixer_fwd(
        proj, cos_t, sin_t, gw_pad, gla_gate_b, ret_norm_w, gla_norm_w,
        [w_out[0].astype(BF16), w_ff1[0].astype(BF16), w_ff2[0].astype(BF16)])
    w_out_b = wo_g.reshape(D_MODEL, D_MODEL)
    w2_b = w2_g.reshape(D_FF, D_MODEL)
    vec_f = jnp.concatenate([gate1, 1.0 + scale2, shift2, gate2, ln1_w, ln1_b, ln2_w, ln2_b], axis=0)
    m, x1n, rstd1, u2, a, df, dh2, acc_f = _mid_fwd(mixed, x2d, tgt, vec_f, w_out_b, w1_b, w2_b)

    vec_b = jnp.concatenate([gate1, 1.0 + scale2, ln1_w, ln1_b, jnp.zeros((4, D_MODEL), F32)], axis=0)
    da, dm, dmix, dxa, acc_b = _ffn_bwd(df, a, dh2, x1n, rstd1, m, vec_b, w_out_b, w1_b, w2_b)
    dw2 = _matmul_tn(a, df, 2048, 1024, "tn_dw2", relu_sq=True)
    dw1 = _matmul_tn(u2, da, 1024, 2048, "tn_dw1", col_slab=FF_COLS)
    dwo = _matmul_tn(mixed, dm, 1024, 1024, "tn_dwout")
    dproj, dgw, dvec, r_wo, r_w1, r_w2 = _mixer_bwd(
        dmix, proj, oraw, cos_t, sin_t, rst, sst, gw_pad, gla_gate_b, ret_norm_w, gla_norm_w,
        [dwo.reshape(N_DEV, OUT_ROWS, D_MODEL), dw1, dw2.reshape(N_DEV, FF_COLS, D_MODEL)])
    dwi = _matmul_tn(u, dproj, 1024, D_IN_PAD, "tn_dwin")
    dwi_s = dwi[:, :D_IN].reshape(D_MODEL, N_DEV, IN_COLS).transpose(1, 0, 2)
    grad_x, acc_i, r_wi = _inproj_bwd(dproj, x2d, dxa, sc1p, w_in_b, dwi_s)

    loss_part = jnp.sum(acc_f[3])
    small = jnp.concatenate([
        acc_i[1:2], acc_i[0:1], acc_b[4:5], acc_b[1:2], acc_b[0:1], acc_f[2:3],
        acc_b[2:3], acc_b[3:4], acc_f[0:1], acc_f[1:2],
        jnp.concatenate([dvec[0:1], dvec[1:2]], axis=1),
        jnp.concatenate([dvec[2:3, :GLA_KW], jnp.full((1, 128), loss_part, F32),
                         jnp.zeros((1, D_MODEL - GLA_KW - 128), F32)], axis=1),
        jnp.zeros((4, D_MODEL), F32)], axis=0)
    small_all, gw_all = _small_gather([small, dgw[:GATE_RANK]])
    dmod_all = small_all[:, :6].reshape(N_DEV, 6 * D_MODEL)
    dmod_cols = lax.dynamic_slice(dmod_all, (0, me * ADA_COLS), (N_DEV, ADA_COLS))
    ssum, gw_sum, g_b_ada, g_w_ada = _small_reduce(small_all, gw_all, c_all, dmod_cols)
    loss = ssum[SMR_MISC, GLA_KW]
    g_ggw = lax.dynamic_slice(gw_sum, (0, me * (GLA_KW // N_DEV)), (GATE_RANK, GLA_KW // N_DEV))[None]

    small_w = [b_ada, ret_norm_w, gla_gate_b, gla_norm_w, ln1_w, ln1_b, ln2_w, ln2_b, gla_gate_w]
    small_m = [m_b_ada, m_ret_norm_w, m_gla_gate_b, m_gla_norm_w, m_ln1_w, m_ln1_b, m_ln2_w, m_ln2_b, m_gla_gate_w]
    small_v = [v_b_ada, v_ret_norm_w, v_gla_gate_b, v_gla_norm_w, v_ln1_w, v_ln1_b, v_ln2_w, v_ln2_b, v_gla_gate_w]
    res = _adamw_small(ssum, g_b_ada, g_ggw, small_w, small_m, small_v)
    small_g = list(res[:8]) + [g_ggw]
    d_small, m_small, v_small = list(res[8:17]), list(res[17:26]), list(res[26:35])

    _, d_w_ada, nm_w_ada, nv_w_ada = _adamw(w_ada[0], g_w_ada[None], m_w_ada[0], v_w_ada[0], "adamw_ada")

    big =[_adamw(w[0], r, m_[0], v_[0], nm) for w, r, m_, v_, nm in (
        (w_in, r_wi, m_w_in, v_w_in, "adamw_in"), (w_out, r_wo, m_w_out, v_w_out, "adamw_out"),
        (w_ff1, r_w1, m_w_ff1, v_w_ff1, "adamw_ff1"), (w_ff2, r_w2, m_w_ff2, v_w_ff2, "adamw_ff2"))]
    g_big, d_big, m_big, v_big = [[b[i][None] for b in big] for i in range(4)]

    def ordered(w_ada_v, small_vals, big_vals):
        b_ada_v, rnw_v, ggb_v, gnw_v, l1w_v, l1b_v, l2w_v, l2b_v, ggw_v = small_vals
        wi_v, wo_v, w1_v, w2_v = big_vals
        return [w_ada_v, b_ada_v, wi_v, rnw_v, ggw_v, ggb_v, gnw_v, wo_v, l1w_v, l1b_v, w1_v, w2_v, l2w_v, l2b_v]

    grads = ordered(g_w_ada[None], small_g, g_big)
    deltas = ordered(d_w_ada[None], d_small, d_big)
    new_m = ordered(nm_w_ada[None], m_small, m_big)
    new_v = ordered(nv_w_ada[None], v_small, v_big)
    return (loss, grad_x[None], *grads, *deltas, *new_m, *new_v)
```

```python
import functools

import numpy as np
import jax
import jax.numpy as jnp
from jax import lax
from jax.experimental import pallas as pl
from jax.experimental.pallas import tpu as pltpu

F32 = jnp.float32
BF16 = jnp.bfloat16
MESH = pl.DeviceIdType.MESH
HIGHEST = lax.Precision.HIGHEST

N_DEV = 8
D_MODEL = 1024
CHUNK = 64
RET_HEADS = 4
RET_D = 128
GLA_HEADS = 4
GLA_DK = 64
GLA_DV = 128
GLA_KW = GLA_HEADS * GLA_DK
GATE_RANK = 16
GATE_TAU = 16.0
D_FF = 4096
LN_EPS = 1e-5
ALPHA = (2.0 * 1) ** 0.25
D_IN = 3600
D_IN_PAD = 3712
ADA_COLS = 6 * D_MODEL // N_DEV
IN_COLS = D_IN // N_DEV
FF_COLS = D_FF // N_DEV
OUT_ROWS = D_MODEL // N_DEV

OFF_RQ, OFF_RK, OFF_RV, OFF_RG = 0, 512, 1024, 1536
OFF_GQ, OFF_GK, OFF_GV, OFF_GG, OFF_GLR = 2048, 2304, 2560, 3072, 3584

ADAM_LR, ADAM_B1, ADAM_B2, ADAM_EPS, ADAM_WD, ADAM_STEP = 0.001, 0.9, 0.999, 1e-08, 0.01, 10

V7X_VMEM_LIMIT = 56 * 1024 * 1024

ROW_TILE = 256
PROJ_TILE = 512
MIX_TILE = 256
RET_SUB = 256
GLA_SUB = 128


def _log_gamma(h):
    return float(np.log(np.float32(1.0) - np.float32(2.0) ** np.float32(-5.0 - h)))


def _my_coords():
    return lax.axis_index("x"), lax.axis_index("y"), lax.axis_index("c")


def _flip(v, bit):
    return 1 - v if bit else v


def _peer(k):
    x, y, c = _my_coords()
    px, py, pc = _flip(x, (k >> 2) & 1), _flip(y, (k >> 1) & 1), _flip(c, k & 1)
    return (px, py, pc), 4 * px + 2 * py + pc


def _dot(a, b, dims=(((1,), (0,)), ((), ())), precision=None):
    return lax.dot_general(a, b, dims, precision=precision, preferred_element_type=F32)


NN = (((1,), (0,)), ((), ()))
NT = (((1,), (1,)), ((), ()))
TN = (((0,), (0,)), ((), ()))


def _split_bf16(v, parts):
    out = []
    for _ in range(parts):
        p = v.astype(BF16)
        out.append(p)
        v = v - p.astype(F32)
    return out


def _dot_split(a, b, dims, a_exact=False):
    if a_exact:
        ab = a.astype(BF16)
        return sum(_dot(ab, p, dims) for p in _split_bf16(b, 3))
    a_hi, a_lo = _split_bf16(a, 2)
    b_hi, b_lo = _split_bf16(b, 2)
    return _dot(a_hi, b_hi, dims) + _dot(a_hi, b_lo, dims) + _dot(a_lo, b_hi, dims)


def _sigmoid(x):
    return 1.0 / (1.0 + jnp.exp(-x))


def _ln_stats(x):
    mu = jnp.mean(x, axis=-1, keepdims=True)
    xc = x - mu
    var = jnp.mean(xc * xc, axis=-1, keepdims=True)
    rstd = lax.rsqrt(var + LN_EPS)
    return xc * rstd, rstd


def _ln_bwd(dyh, xh, rstd):
    return rstd * (dyh - jnp.mean(dyh, axis=-1, keepdims=True) - xh * jnp.mean(dyh * xh, axis=-1, keepdims=True))


def _adaln_mod(c_ext, w_ada_l, b_l, w_in_l):
    width = c_ext.shape[1]

    def body(c_ref, w_ref, b_ref, wi_ref, call_ref, mod_ref, wig_ref, s1, r1, s2, r2, gs, gr, gl):
        gather = _TwoLevelGather([wi_ref], [wig_ref], gs, gr, gl)
        gather.start()
        x, y, c = _my_coords()
        me = 4 * x + 2 * y + c
        call_ref[me] = c_ref[...]
        sends = []
        for k in range(1, N_DEV):
            peer, _ = _peer(k)
            cp = pltpu.make_async_remote_copy(c_ref, call_ref.at[me], s1.at[k - 1], r1.at[k - 1],
                                              device_id=peer, device_id_type=MESH)
            cp.start()
            sends.append(cp)
        for k in range(1, N_DEV):
            peer, pid = _peer(k)
            pltpu.make_async_remote_copy(c_ref, call_ref.at[pid], s1.at[k - 1], r1.at[k - 1],
                                         device_id=peer, device_id_type=MESH).wait_recv()
        for cp in sends:
            cp.wait_send()
        row = lax.broadcasted_iota(jnp.int32, (N_DEV, D_MODEL), 0)
        call = jnp.zeros((N_DEV, D_MODEL), F32)
        for j in range(N_DEV):
            call = jnp.where(row == j, jnp.broadcast_to(call_ref[j][:, :D_MODEL], (N_DEV, D_MODEL)), call)
        sc = call * _sigmoid(call)
        mod = _dot(sc, w_ref[...], NN, HIGHEST) + b_ref[...]
        mod_ref[me] = mod
        sends = []
        for k in range(1, N_DEV):
            peer, _ = _peer(k)
            cp = pltpu.make_async_remote_copy(mod_ref.at[me], mod_ref.at[me], s2.at[k - 1], r2.at[k - 1],
                                              device_id=peer, device_id_type=MESH)
            cp.start()
            sends.append(cp)
        for k in range(1, N_DEV):
            peer, pid = _peer(k)
            pltpu.make_async_remote_copy(mod_ref.at[pid], mod_ref.at[pid], s2.at[k - 1], r2.at[k - 1],
                                         device_id=peer, device_id_type=MESH).wait_recv()
        for cp in sends:
            cp.wait_send()
        gather.forward()
        gather.finish()

    vm = pl.BlockSpec(memory_space=pltpu.VMEM)
    hbm = pl.BlockSpec(memory_space=pl.ANY)
    return pl.pallas_call(
        body, name="adaln_mod",
        out_shape=(jax.ShapeDtypeStruct((N_DEV, 1, width), F32),
                   jax.ShapeDtypeStruct((N_DEV, N_DEV, ADA_COLS), F32),
                   jax.ShapeDtypeStruct((N_DEV, *w_in_l.shape), w_in_l.dtype)),
        in_specs=[vm, vm, vm, hbm], out_specs=(vm, vm, hbm),
        scratch_shapes=[pltpu.SemaphoreType.DMA((N_DEV - 1,))] * 4
        + [pltpu.SemaphoreType.DMA((7,)), pltpu.SemaphoreType.DMA((7,)), pltpu.SemaphoreType.DMA((1,))],
        compiler_params=pltpu.CompilerParams(vmem_limit_bytes=V7X_VMEM_LIMIT),
    )(c_ext, w_ada_l, b_l, w_in_l)


class _TwoLevelGather:
    def __init__(self, x_refs, out_refs, send_sems, recv_sems, local_sems):
        self.x_refs, self.out_refs = x_refs, out_refs
        self.send_sems, self.recv_sems, self.local_sems = send_sems, recv_sems, local_sems
        x, y, c = _my_coords()
        self.c = c
        self.me, self.sibling = (x, y, c), (x, y, 1 - c)
        self.chips = [(1 - x, y), (x, 1 - y), (1 - x, 1 - y)]

    def _copy(self, a, k, block, to, src=None):
        px, py, pc = block
        slab = self.out_refs[a].at[4 * px + 2 * py + pc]
        return pltpu.make_async_remote_copy(
            src_ref=slab if src is None else src, dst_ref=slab,
            send_sem=self.send_sems.at[7 * a + k], recv_sem=self.recv_sems.at[7 * a + k],
            device_id=to, device_id_type=MESH)

    def _mine(self, a):
        px, py, pc = self.me
        return pltpu.make_async_copy(self.x_refs[a], self.out_refs[a].at[4 * px + 2 * py + pc], self.local_sems.at[a])

    def _first(self, a):
        cps = [self._copy(a, 0, self.me, self.sibling, src=self.x_refs[a])]
        cps += [self._copy(a, 1 + j, self.me, (*chip, self.c), src=self.x_refs[a]) for j, chip in enumerate(self.chips)]
        return cps

    def _passed(self, a):
        return [self._copy(a, 4 + j, (*chip, self.c), self.sibling) for j, chip in enumerate(self.chips)]

    def start(self):
        for a in range(len(self.x_refs)):
            self._mine(a).start()
            for cp in self._first(a):
                cp.start()

    def forward(self):
        for a in range(len(self.x_refs)):
            passed = self._passed(a)
            for j, chip in enumerate(self.chips):
                self._copy(a, 1 + j, (*chip, self.c), self.me).wait_recv()
                passed[j].start()

    def finish(self):
        for a in range(len(self.x_refs)):
            self._copy(a, 0, self.sibling, self.me).wait_recv()
            for j, chip in enumerate(self.chips):
                self._copy(a, 4 + j, (*chip, 1 - self.c), self.me).wait_recv()
            for cp in self._first(a) + self._passed(a):
                cp.wait_send()
            self._mine(a).wait()


class _DirectExchange:
    def __init__(self, s_refs, r_refs, send_sems, recv_sems, local_sems):
        self.s_refs, self.r_refs = s_refs, r_refs
        self.send_sems, self.recv_sems, self.local_sems = send_sems, recv_sems, local_sems
        x, y, c = _my_coords()
        self.me = 4 * x + 2 * y + c

    def _mine(self, a):
        return pltpu.make_async_copy(self.s_refs[a].at[self.me], self.r_refs[a].at[self.me], self.local_sems.at[a])

    def _send(self, a, k):
        peer, pid = _peer(k)
        return pltpu.make_async_remote_copy(self.s_refs[a].at[pid], self.r_refs[a].at[self.me],
                                            self.send_sems.at[7 * a + k - 1], self.recv_sems.at[7 * a + k - 1],
                                            device_id=peer, device_id_type=MESH)

    def _recv(self, a, k):
        peer, pid = _peer(k)
        return pltpu.make_async_remote_copy(self.s_refs[a].at[pid], self.r_refs[a].at[pid],
                                            self.send_sems.at[7 * a + k - 1], self.recv_sems.at[7 * a + k - 1],
                                            device_id=peer, device_id_type=MESH)

    def start(self):
        for a in range(len(self.s_refs)):
            self._mine(a).start()
            for k in range(1, N_DEV):
                self._send(a, k).start()

    def finish(self):
        for a in range(len(self.s_refs)):
            for k in range(1, N_DEV):
                self._recv(a, k).wait_recv()
            for k in range(1, N_DEV):
                self._send(a, k).wait_send()
            self._mine(a).wait()


def _small_gather(vecs):
    n = len(vecs)

    def body(*refs):
        v_refs, out_refs, s_sems, r_sems = refs[:n], refs[n:2 * n], refs[2 * n], refs[2 * n + 1]
        x, y, c = _my_coords()
        me = 4 * x + 2 * y + c
        sends = []
        for a in range(n):
            out_refs[a][me] = v_refs[a][...]
            for k in range(1, N_DEV):
                peer, _ = _peer(k)
                cp = pltpu.make_async_remote_copy(v_refs[a], out_refs[a].at[me], s_sems.at[7 * a + k - 1],
                                                  r_sems.at[7 * a + k - 1], device_id=peer, device_id_type=MESH)
                cp.start()
                sends.append(cp)
        for a in range(n):
            for k in range(1, N_DEV):
                peer, pid = _peer(k)
                pltpu.make_async_remote_copy(v_refs[a], out_refs[a].at[pid], s_sems.at[7 * a + k - 1],
                                             r_sems.at[7 * a + k - 1], device_id=peer, device_id_type=MESH).wait_recv()
        for cp in sends:
            cp.wait_send()

    vm = pl.BlockSpec(memory_space=pltpu.VMEM)
    return pl.pallas_call(
        body, name="small_gather",
        out_shape=tuple(jax.ShapeDtypeStruct((N_DEV, *v.shape), v.dtype) for v in vecs),
        in_specs=[vm] * n, out_specs=tuple([vm] * n),
        scratch_shapes=[pltpu.SemaphoreType.DMA((7 * n,))] * 2,
    )(*vecs)


def _load_resident(step_is_first, pairs, sem):
    @pl.when(step_is_first)
    def _():
        copies = [pltpu.make_async_copy(src, dst, sem.at[i]) for i, (src, dst) in enumerate(pairs)]
        for cp in copies:
            cp.start()
        for cp in copies:
            cp.wait()


def _load_w_in_t(step_is_first, w_hbm, w_vmem, sem):
    @pl.when(step_is_first)
    def _():
        w_vmem[D_IN:, :] = jnp.zeros((D_IN_PAD - D_IN, D_MODEL), BF16)
    _load_resident(step_is_first, [(w_hbm, w_vmem.at[pl.ds(0, D_IN)])], sem)


def _inproj_fwd(x2d, sc1p, sh1, w_in_t):
    t = x2d.shape[0]
    tm = min(PROJ_TILE, t)

    def body(x_ref, sc_ref, sh_ref, w_hbm, proj_ref, u_ref, w_vmem, sem):
        _load_w_in_t(pl.program_id(0) == 0, w_hbm, w_vmem, sem)
        xh, _ = _ln_stats(x_ref[...])
        ub = (xh * sc_ref[...] + sh_ref[...]).astype(BF16)
        u_ref[...] = ub
        proj_ref[...] = _dot(ub, w_vmem[...], NT)

    row = lambda i: (i, 0)
    fix = lambda i: (0, 0)
    return pl.pallas_call(
        body, name="inproj_fwd", grid=(t // tm,),
        in_specs=[pl.BlockSpec((tm, D_MODEL), row), pl.BlockSpec((1, D_MODEL), fix), pl.BlockSpec((1, D_MODEL), fix),
                  pl.BlockSpec(memory_space=pl.ANY)],
        out_specs=(pl.BlockSpec((tm, D_IN_PAD), row), pl.BlockSpec((tm, D_MODEL), row)),
        out_shape=(jax.ShapeDtypeStruct((t, D_IN_PAD), F32), jax.ShapeDtypeStruct((t, D_MODEL), BF16)),
        scratch_shapes=[pltpu.VMEM((D_IN_PAD, D_MODEL), BF16), pltpu.SemaphoreType.DMA((1,))],
        compiler_params=pltpu.CompilerParams(dimension_semantics=("arbitrary",), vmem_limit_bytes=V7X_VMEM_LIMIT),
    )(x2d, sc1p, sh1, w_in_t)


CHUNK_SHIFT = 6


def _ret_consts(tl):
    r = lax.broadcasted_iota(jnp.int32, (tl, tl), 0)
    c = lax.broadcasted_iota(jnp.int32, (tl, tl), 1)
    allowed = jnp.right_shift(c, CHUNK_SHIFT) <= jnp.right_shift(r, CHUNK_SHIFT)
    dist = jnp.abs(r - c).astype(F32)
    rowf = lax.broadcasted_iota(jnp.int32, (tl, RET_D), 0).astype(F32)
    dec, qd, kd, gl = [], [], [], []
    for h in range(RET_HEADS):
        lg = _log_gamma(h)
        dec.append(jnp.where(allowed, jnp.exp(lg * dist), 0.0))
        qd.append(jnp.exp(lg * (rowf + 1.0)))
        kd.append(jnp.exp(lg * (tl - 1.0 - rowf)))
        gl.append(float(np.exp(np.float32(lg) * np.float32(tl))))
    return dict(dec=dec, qd=qd, kd=kd, gl=gl)


def _gla_consts(tl):
    r = lax.broadcasted_iota(jnp.int32, (tl, tl), 0)
    c = lax.broadcasted_iota(jnp.int32, (tl, tl), 1)
    ltri = (c <= r).astype(F32)
    utri = (c >= r).astype(F32)
    lane = lax.broadcasted_iota(jnp.int32, (1, GLA_KW), 1)
    hmask = [((lane >= h * GLA_DK) & (lane < (h + 1) * GLA_DK)).astype(F32) for h in range(GLA_HEADS)]
    rs = lax.broadcasted_iota(jnp.int32, (GLA_HEADS * tl, tl), 0) & (tl - 1)
    cs = lax.broadcasted_iota(jnp.int32, (GLA_HEADS * tl, tl), 1)
    lower = cs <= rs
    same = jnp.right_shift(cs, CHUNK_SHIFT) == jnp.right_shift(rs, CHUNK_SHIFT)
    upper = jnp.logical_and(jnp.logical_not(lower), same)
    return dict(ltri=ltri, utri=utri, hmask=hmask, lower=lower, upper=upper)


def _tile_rows(j, tl):
    return pl.ds(j * tl, tl) if isinstance(j, int) else pl.ds(pl.multiple_of(j * tl, tl), tl)


def _for_tiles(cps, fn):
    if cps == 1:
        fn(0, 0)
    else:
        lax.fori_loop(0, cps, fn, 0)


def _rotate(v, cosv, sinv):
    return v * cosv + pltpu.roll(v, RET_D // 2, 1) * sinv


def _rotate_t(d, cosv, sinv):
    return d * cosv + pltpu.roll(d * sinv, RET_D // 2, 1)


def _stack_heads(v, hmask):
    return jnp.concatenate([v * hmask[h] for h in range(GLA_HEADS)], axis=0)


def _gla_gates(glr, gw, gb, ltri, tl):
    z = _dot_split(glr, gw, NN) + gb
    la = (jnp.minimum(z, 0.0) - jnp.log(1.0 + jnp.exp(-jnp.abs(z)))) * (1.0 / GATE_TAU)
    b = _dot_split(ltri, la, NN, a_exact=True)
    level = b[tl // 2 - 1:tl // 2, :]
    ep = jnp.exp(jnp.clip(b - level, -80.0, 80.0))
    em = jnp.exp(jnp.clip(level - b, -80.0, 80.0))
    bl = b[tl - 1:tl, :]
    return z, b, bl, ep, em


def _mixer_fwd(proj, cos_t, sin_t, gw_pad, gb, rnw, gnw, shards):
    t = proj.shape[0]
    tc = min(MIX_TILE, t)
    tr, tg = min(RET_SUB, tc), min(GLA_SUB, tc)
    nsteps = t // tc
    fwd_step = (3 * nsteps) // 4
    ns = len(shards)
    scale_r = RET_D ** -0.5
    scale_g = GLA_DK ** -0.5

    def body(rq_ref, rk_ref, rv_ref, rg_ref, gq_ref, gk_ref, gv_ref, gg_ref, glr_ref, cos_ref, sin_ref,
             gw_ref, gb_ref, rnw_ref, gnw_ref, *rest):
        shard_refs, rest = rest[:ns], rest[ns:]
        mix_ref, oraw_ref, rst_ref, sst_ref = rest[:4]
        gathered_refs, rest = rest[4:4 + ns], rest[4 + ns:]
        r_scr, s_scr, gs, gr, gl = rest
        step = pl.program_id(0)

        @pl.when(step == 0)
        def _():
            r_scr[...] = jnp.zeros_like(r_scr)
            s_scr[...] = jnp.zeros_like(s_scr)
            _TwoLevelGather(shard_refs, gathered_refs, gs, gr, gl).start()

        if fwd_step != nsteps - 1:
            @pl.when(step == fwd_step)
            def _():
                _TwoLevelGather(shard_refs, gathered_refs, gs, gr, gl).forward()

        ret_k, gla_k = _ret_consts(tr), _gla_consts(tg)

        def ret_tile(j, carry):
            k = ret_k
            rows = _tile_rows(j, tr)
            cosv, sinv = cos_ref[rows, :], sin_ref[rows, :]
            for h in range(RET_HEADS):
                cols = slice(h * RET_D, (h + 1) * RET_D)
                qr = _rotate(rq_ref[rows, cols], cosv, sinv) * scale_r
                kr = _rotate(rk_ref[rows, cols], cosv, sinv)
                vb = rv_ref[rows, cols].astype(BF16)
                qb, kb = qr.astype(BF16), kr.astype(BF16)
                p = _dot(qb, kb, NT) * k["dec"][h]
                rp = r_scr[cols, :]
                o = _dot(p.astype(BF16), vb) + _dot((qr * k["qd"][h]).astype(BF16), rp.astype(BF16))
                rst_ref[j, cols, :] = rp
                r_scr[cols, :] = k["gl"][h] * rp + _dot((kr * k["kd"][h]).astype(BF16), vb, TN)
                oraw_ref[rows, cols] = o
                oc = o - jnp.mean(o, axis=-1, keepdims=True)
                n = oc * lax.rsqrt(jnp.mean(oc * oc, axis=-1, keepdims=True) + LN_EPS)
                g = rg_ref[rows, cols]
                mix_ref[rows, cols] = (n * rnw_ref[:, cols] * (g * _sigmoid(g))).astype(BF16)
            return carry

        def gla_tile(j, carry):
            k = gla_k
            tl = tg
            rows = _tile_rows(j, tg)
            _, b, bl, ep, em = _gla_gates(glr_ref[rows, :], gw_ref[...], gb_ref[...], k["ltri"], tl)
            qs = gq_ref[rows, :] * scale_g
            kk = gk_ref[rows, :]
            x_all = _dot(_stack_heads(qs * ep, k["hmask"]).astype(BF16), (kk * em).astype(BF16), NT)
            y_all = _dot(_stack_heads(qs * em, k["hmask"]).astype(BF16), (kk * ep).astype(BF16), NT)
            a_all = jnp.where(k["lower"], x_all, jnp.where(k["upper"], y_all, 0.0)).astype(BF16)
            st = s_scr[...]
            oq = _dot(_stack_heads(qs * jnp.exp(b), k["hmask"]).astype(BF16), st.astype(BF16), NT)
            kg = kk * jnp.exp(bl - b)
            sst_ref[j] = st
            st_new = st * jnp.exp(bl)
            for h in range(GLA_HEADS):
                cols = slice(h * GLA_DV, (h + 1) * GLA_DV)
                hr = slice(h * tl, (h + 1) * tl)
                vb = gv_ref[rows, cols].astype(BF16)
                o = _dot(a_all[hr, :], vb) + oq[hr, :]
                st_new = st_new + _dot(vb, (kg * k["hmask"][h]).astype(BF16), TN)
                ocols = slice(RET_HEADS * RET_D + h * GLA_DV, RET_HEADS * RET_D + (h + 1) * GLA_DV)
                oraw_ref[rows, ocols] = o
                n = o * lax.rsqrt(jnp.mean(o * o, axis=-1, keepdims=True) + LN_EPS)
                g = gg_ref[rows, cols]
                mix_ref[rows, ocols] = (n * gnw_ref[:, cols] * (g * _sigmoid(g))).astype(BF16)
            s_scr[...] = st_new
            return carry

        _for_tiles(tc // tr, ret_tile)
        _for_tiles(tc // tg, gla_tile)

        @pl.when(step == nsteps - 1)
        def _():
            gather = _TwoLevelGather(shard_refs, gathered_refs, gs, gr, gl)
            if fwd_step == nsteps - 1:
                gather.forward()
            gather.finish()

    def col(width, off):
        return pl.BlockSpec((tc, width), lambda i, o=off // width: (i, o))

    fix = lambda i: (0, 0)
    hbm = pl.BlockSpec(memory_space=pl.ANY)
    in_specs = [col(512, OFF_RQ), col(512, OFF_RK), col(512, OFF_RV), col(512, OFF_RG),
                col(256, OFF_GQ), col(256, OFF_GK), col(512, OFF_GV), col(512, OFF_GG), col(128, OFF_GLR),
                pl.BlockSpec((tc, RET_D), lambda i: (i, 0)), pl.BlockSpec((tc, RET_D), lambda i: (i, 0)),
                pl.BlockSpec((128, GLA_KW), fix), pl.BlockSpec((1, GLA_KW), fix),
                pl.BlockSpec((1, 512), fix), pl.BlockSpec((1, 512), fix)] + [hbm] * ns
    out_specs = (pl.BlockSpec((tc, D_MODEL), lambda i: (i, 0)), pl.BlockSpec((tc, D_MODEL), lambda i: (i, 0)),
                 pl.BlockSpec((tc // tr, RET_HEADS * RET_D, RET_D), lambda i: (i, 0, 0)),
                 pl.BlockSpec((tc // tg, GLA_DV, GLA_KW), lambda i: (i, 0, 0))) + tuple([hbm] * ns)
    out_shape = (jax.ShapeDtypeStruct((t, D_MODEL), BF16), jax.ShapeDtypeStruct((t, D_MODEL), F32),
                 jax.ShapeDtypeStruct((t // tr, RET_HEADS * RET_D, RET_D), F32),
                 jax.ShapeDtypeStruct((t // tg, GLA_DV, GLA_KW), F32)) + tuple(
                     jax.ShapeDtypeStruct((N_DEV, *s.shape), s.dtype) for s in shards)
    return pl.pallas_call(
        body, name="mixer_fwd", grid=(nsteps,), in_specs=in_specs, out_specs=out_specs, out_shape=out_shape,
        scratch_shapes=[pltpu.VMEM((RET_HEADS * RET_D, RET_D), F32), pltpu.VMEM((GLA_DV, GLA_KW), F32),
                        pltpu.SemaphoreType.DMA((7 * ns,)), pltpu.SemaphoreType.DMA((7 * ns,)),
                        pltpu.SemaphoreType.DMA((ns,))],
        compiler_params=pltpu.CompilerParams(dimension_semantics=("arbitrary",), vmem_limit_bytes=V7X_VMEM_LIMIT),
    )(*([proj] * 9), cos_t, sin_t, gw_pad, gb, rnw, gnw, *shards)


def _mid_fwd(mixed, x2d, target, vecs, w_out_b, w1_b, w2_b):
    t = x2d.shape[0]
    tm = min(ROW_TILE, t)

    def body(mix_ref, x_ref, tgt_ref, v_ref, wo_hbm, w1_hbm, w2_hbm,
             m_ref, x1n_ref, rstd_ref, u2_ref, a_ref, df_ref, dh2_ref, acc_ref, wo, w1, w2, sem):
        first = pl.program_id(0) == 0
        _load_resident(first, [(wo_hbm, wo), (w1_hbm, w1), (w2_hbm, w2)], sem)

        @pl.when(first)
        def _():
            acc_ref[...] = jnp.zeros_like(acc_ref)

        gate1, sc2p, sh2, gate2 = v_ref[0:1, :], v_ref[1:2, :], v_ref[2:3, :], v_ref[3:4, :]
        l1w, l1b, l2w, l2b = v_ref[4:5, :], v_ref[5:6, :], v_ref[6:7, :], v_ref[7:8, :]
        m = _dot(mix_ref[...], wo[...])
        m_ref[...] = m.astype(BF16)
        x1n, rstd1 = _ln_stats(ALPHA * x_ref[...] + gate1 * m)
        x1n_ref[...] = x1n
        rstd_ref[...] = rstd1
        x1 = x1n * l1w + l1b
        xh1, _ = _ln_stats(x1)
        u2 = (xh1 * sc2p + sh2).astype(BF16)
        u2_ref[...] = u2
        f = jnp.zeros((tm, D_MODEL), F32)
        for j in range(N_DEV):
            cols = slice(j * FF_COLS, (j + 1) * FF_COLS)
            a = _dot(u2, w1[j])
            a_ref[:, cols] = a.astype(BF16)
            r = jnp.maximum(a, 0.0)
            f = f + _dot((r * r).astype(BF16), w2[cols, :])
        yh, rstd2 = _ln_stats(ALPHA * x1 + gate2 * f)
        e = yh * l2w + l2b - tgt_ref[...]
        dy = e * (1.0 / D_MODEL)
        dh2 = _ln_bwd(dy * l2w, yh, rstd2)
        dh2_ref[...] = dh2
        df_ref[...] = (dh2 * gate2).astype(BF16)
        acc_ref[0:1, :] += jnp.sum(dy * yh, axis=0, keepdims=True)
        acc_ref[1:2, :] += jnp.sum(dy, axis=0, keepdims=True)
        acc_ref[2:3, :] += jnp.sum(dh2 * f, axis=0, keepdims=True)
        acc_ref[3:4, :] += jnp.sum(e * e, axis=0, keepdims=True) * (0.5 / D_MODEL)

    row = lambda i: (i, 0)
    fix = lambda i: (0, 0)
    hbm = pl.BlockSpec(memory_space=pl.ANY)
    return pl.pallas_call(
        body, name="mid_fwd", grid=(t // tm,),
        in_specs=[pl.BlockSpec((tm, D_MODEL), row), pl.BlockSpec((tm, D_MODEL), row), pl.BlockSpec((tm, D_MODEL), row),
                  pl.BlockSpec((8, D_MODEL), fix), hbm, hbm, hbm],
        out_specs=(pl.BlockSpec((tm, D_MODEL), row), pl.BlockSpec((tm, D_MODEL), row), pl.BlockSpec((tm, 1), row),
                   pl.BlockSpec((tm, D_MODEL), row), pl.BlockSpec((tm, D_FF), row), pl.BlockSpec((tm, D_MODEL), row),
                   pl.BlockSpec((tm, D_MODEL), row), pl.BlockSpec((8, D_MODEL), fix)),
        out_shape=(jax.ShapeDtypeStruct((t, D_MODEL), BF16), jax.ShapeDtypeStruct((t, D_MODEL), F32),
                   jax.ShapeDtypeStruct((t, 1), F32), jax.ShapeDtypeStruct((t, D_MODEL), BF16),
                   jax.ShapeDtypeStruct((t, D_FF), BF16), jax.ShapeDtypeStruct((t, D_MODEL), BF16),
                   jax.ShapeDtypeStruct((t, D_MODEL), F32), jax.ShapeDtypeStruct((8, D_MODEL), F32)),
        scratch_shapes=[pltpu.VMEM((D_MODEL, D_MODEL), BF16), pltpu.VMEM((N_DEV, D_MODEL, FF_COLS), BF16),
                        pltpu.VMEM((D_FF, D_MODEL), BF16), pltpu.SemaphoreType.DMA((3,))],
        compiler_params=pltpu.CompilerParams(dimension_semantics=("arbitrary",), vmem_limit_bytes=V7X_VMEM_LIMIT),
    )(mixed, x2d, target, vecs, w_out_b, w1_b, w2_b)


def _ffn_bwd(df, a, dh2, x1n, rstd1, m, vecs, w_out_b, w1_b, w2_b):
    t = x1n.shape[0]
    tm = min(ROW_TILE, t)

    def body(df_ref, a_ref, dh2_ref, x1n_ref, rstd_ref, m_ref, v_ref, wo_hbm, w1_hbm, w2_hbm,
             da_ref, dm_ref, dmix_ref, dxa_ref, acc_ref, wo, w1, w2, sem):
        first = pl.program_id(0) == 0
        _load_resident(first, [(wo_hbm, wo), (w1_hbm, w1), (w2_hbm, w2)], sem)

        @pl.when(first)
        def _():
            acc_ref[...] = jnp.zeros_like(acc_ref)

        gate1, sc2p, l1w, l1b = v_ref[0:1, :], v_ref[1:2, :], v_ref[2:3, :], v_ref[3:4, :]
        df = df_ref[...]
        du2 = jnp.zeros((tm, D_MODEL), F32)
        for j in range(N_DEV):
            cols = slice(j * FF_COLS, (j + 1) * FF_COLS)
            dr2 = _dot(df, w2[cols, :], NT)
            da = (dr2 * (2.0 * jnp.maximum(a_ref[:, cols].astype(F32), 0.0))).astype(BF16)
            da_ref[:, cols] = da
            du2 = du2 + _dot(da, w1[j], NT)
        x1n = x1n_ref[...]
        xh1, rstd0 = _ln_stats(x1n * l1w + l1b)
        dx1 = ALPHA * dh2_ref[...] + _ln_bwd(du2 * sc2p, xh1, rstd0)
        dh1 = _ln_bwd(dx1 * l1w, x1n, rstd_ref[...])
        dxa_ref[...] = ALPHA * dh1
        dm = (dh1 * gate1).astype(BF16)
        dm_ref[...] = dm
        dmix_ref[...] = _dot(dm, wo[...], NT)
        acc_ref[0:1, :] += jnp.sum(du2 * xh1, axis=0, keepdims=True)
        acc_ref[1:2, :] += jnp.sum(du2, axis=0, keepdims=True)
        acc_ref[2:3, :] += jnp.sum(dx1 * x1n, axis=0, keepdims=True)
        acc_ref[3:4, :] += jnp.sum(dx1, axis=0, keepdims=True)
        acc_ref[4:5, :] += jnp.sum(dh1 * m_ref[...].astype(F32), axis=0, keepdims=True)

    row = lambda i: (i, 0)
    fix = lambda i: (0, 0)
    hbm = pl.BlockSpec(memory_space=pl.ANY)
    return pl.pallas_call(
        body, name="ffn_bwd", grid=(t // tm,),
        in_specs=[pl.BlockSpec((tm, D_MODEL), row), pl.BlockSpec((tm, D_FF), row), pl.BlockSpec((tm, D_MODEL), row),
                  pl.BlockSpec((tm, D_MODEL), row), pl.BlockSpec((tm, 1), row), pl.BlockSpec((tm, D_MODEL), row),
                  pl.BlockSpec((8, D_MODEL), fix), hbm, hbm, hbm],
        out_specs=(pl.BlockSpec((tm, D_FF), row), pl.BlockSpec((tm, D_MODEL), row), pl.BlockSpec((tm, D_MODEL), row),
                   pl.BlockSpec((tm, D_MODEL), row), pl.BlockSpec((8, D_MODEL), fix)),
        out_shape=(jax.ShapeDtypeStruct((t, D_FF), BF16), jax.ShapeDtypeStruct((t, D_MODEL), BF16),
                   jax.ShapeDtypeStruct((t, D_MODEL), F32), jax.ShapeDtypeStruct((t, D_MODEL), F32),
                   jax.ShapeDtypeStruct((8, D_MODEL), F32)),
        scratch_shapes=[pltpu.VMEM((D_MODEL, D_MODEL), BF16), pltpu.VMEM((N_DEV, D_MODEL, FF_COLS), BF16),
                        pltpu.VMEM((D_FF, D_MODEL), BF16), pltpu.SemaphoreType.DMA((3,))],
        compiler_params=pltpu.CompilerParams(dimension_semantics=("arbitrary",), vmem_limit_bytes=V7X_VMEM_LIMIT),
    )(df, a, dh2, x1n, rstd1, m, vecs, w_out_b, w1_b, w2_b)


def _matmul_tn(lhs, rhs, tmm, tn, tk, name, relu_sq=False, col_slab=None):
    t, mm = lhs.shape
    nn = rhs.shape[1]
    tk = min(tk, t)
    nk = t // tk

    def body(l_ref, r_ref, o_ref, acc):
        kk = pl.program_id(2)

        @pl.when(kk == 0)
        def _():
            acc[...] = jnp.zeros_like(acc)

        l = l_ref[...]
        if relu_sq:
            lf = jnp.maximum(l.astype(F32), 0.0)
            l = (lf * lf).astype(BF16)
        acc[...] += _dot(l, r_ref[...], TN)

        @pl.when(kk == nk - 1)
        def _():
            if col_slab is None:
                o_ref[...] = acc[...].astype(o_ref.dtype)
            else:
                for s in range(tn // col_slab):
                    o_ref[s] = acc[:, s * col_slab:(s + 1) * col_slab].astype(o_ref.dtype)

    if col_slab is None:
        out_spec = pl.BlockSpec((tmm, tn), lambda i, j, k: (i, j))
        out_shape = jax.ShapeDtypeStruct((mm, nn), BF16)
    else:
        out_spec = pl.BlockSpec((tn // col_slab, tmm, col_slab), lambda i, j, k: (j, i, 0))
        out_shape = jax.ShapeDtypeStruct((nn // col_slab, mm, col_slab), BF16)
    return pl.pallas_call(
        body, name=name, grid=(mm // tmm, nn // tn, nk),
        in_specs=[pl.BlockSpec((tk, tmm), lambda i, j, k: (k, i)), pl.BlockSpec((tk, tn), lambda i, j, k: (k, j))],
        out_specs=out_spec,
        out_shape=out_shape,
        scratch_shapes=[pltpu.VMEM((tmm, tn), F32)],
        compiler_params=pltpu.CompilerParams(dimension_semantics=("arbitrary", "arbitrary", "arbitrary"),
                                             vmem_limit_bytes=V7X_VMEM_LIMIT),
    )(lhs, rhs)


def _mixer_bwd(dmix, proj, oraw, cos_t, sin_t, rst, sst, gw_pad, gb, rnw, gnw, slabs):
    t = proj.shape[0]
    tc = min(MIX_TILE, t)
    tr, tg = min(RET_SUB, tc), min(GLA_SUB, tc)
    nsteps = t // tc
    ns = len(slabs)
    scale_r = RET_D ** -0.5
    scale_g = GLA_DK ** -0.5

    def body(dmix_ref, rq_ref, rk_ref, rv_ref, rg_ref, gq_ref, gk_ref, gv_ref, gg_ref, glr_ref, oraw_ref,
             cos_ref, sin_ref, rst_ref, sst_ref, gw_ref, gb_ref, rnw_ref, gnw_ref, *rest):
        send_refs, rest = rest[:ns], rest[ns:]
        dproj_ref, dgw_ref, dvec_ref = rest[:3]
        recv_refs, rest = rest[3:3 + ns], rest[3 + ns:]
        dr_scr, ds_scr, xs, xr, xl = rest
        step = pl.program_id(0)

        @pl.when(step == 0)
        def _():
            dr_scr[...] = jnp.zeros_like(dr_scr)
            ds_scr[...] = jnp.zeros_like(ds_scr)
            dgw_ref[...] = jnp.zeros_like(dgw_ref)
            dvec_ref[...] = jnp.zeros_like(dvec_ref)
            _DirectExchange(send_refs, recv_refs, xs, xr, xl).start()

        ret_k, gla_k = _ret_consts(tr), _gla_consts(tg)
        last_row = lax.broadcasted_iota(jnp.int32, (tg, GLA_KW), 0) == tg - 1

        def ret_tile(jj, carry):
            k = ret_k
            j = tc // tr - 1 - jj
            rows = _tile_rows(j, tr)
            cosv, sinv = cos_ref[rows, :], sin_ref[rows, :]
            for h in range(RET_HEADS):
                cols = slice(h * RET_D, (h + 1) * RET_D)
                o = oraw_ref[rows, cols]
                g = rg_ref[rows, cols]
                w = rnw_ref[:, cols]
                dout = dmix_ref[rows, cols]
                oc = o - jnp.mean(o, axis=-1, keepdims=True)
                inv = lax.rsqrt(jnp.mean(oc * oc, axis=-1, keepdims=True) + LN_EPS)
                n = oc * inv
                sg = _sigmoid(g)
                sil = g * sg
                dn = dout * w * sil
                dvec_ref[0:1, cols] += jnp.sum(dout * n * sil, axis=0, keepdims=True)
                dproj_ref[rows, OFF_RG + h * RET_D:OFF_RG + (h + 1) * RET_D] = (
                    dout * n * w * (sg * (1.0 + g * (1.0 - sg)))).astype(BF16)
                doc = inv * (dn - n * jnp.mean(dn * n, axis=-1, keepdims=True))
                do = doc - jnp.mean(doc, axis=-1, keepdims=True)

                qr = _rotate(rq_ref[rows, cols], cosv, sinv) * scale_r
                kr = _rotate(rk_ref[rows, cols], cosv, sinv)
                vb = rv_ref[rows, cols].astype(BF16)
                qb, kb, dob = qr.astype(BF16), kr.astype(BF16), do.astype(BF16)
                p = _dot(qb, kb, NT) * k["dec"][h]
                rp = rst_ref[j, cols, :].astype(BF16)
                dr = dr_scr[cols, :]
                drb = dr.astype(BF16)
                dpb = (_dot(dob, vb, NT) * k["dec"][h]).astype(BF16)
                dqr = _dot(dpb, kb) + _dot(dob, rp, NT) * k["qd"][h]
                dkr = _dot(dpb, qb, TN) + _dot(vb, drb, NT) * k["kd"][h]
                dv = _dot(p.astype(BF16), dob, TN) + _dot((kr * k["kd"][h]).astype(BF16), drb)
                dr_scr[cols, :] = k["gl"][h] * dr + _dot((qr * k["qd"][h]).astype(BF16), dob, TN)
                dproj_ref[rows, OFF_RQ + h * RET_D:OFF_RQ + (h + 1) * RET_D] = (
                    _rotate_t(dqr, cosv, sinv) * scale_r).astype(BF16)
                dproj_ref[rows, OFF_RK + h * RET_D:OFF_RK + (h + 1) * RET_D] = _rotate_t(dkr, cosv, sinv).astype(BF16)
                dproj_ref[rows, OFF_RV + h * RET_D:OFF_RV + (h + 1) * RET_D] = dv.astype(BF16)
            return carry

        def gla_tile(jj, carry):
            k = gla_k
            tl = tg
            j = tc // tg - 1 - jj
            rows = _tile_rows(j, tg)
            glr = glr_ref[rows, :]
            z, b, bl, ep, em = _gla_gates(glr, gw_ref[...], gb_ref[...], k["ltri"], tl)
            qs = gq_ref[rows, :] * scale_g
            kk = gk_ref[rows, :]
            eb = jnp.exp(b)
            ekb = jnp.exp(bl - b)
            ebl = jnp.exp(bl)
            ql, qu, kl, ku = qs * ep, qs * em, kk * em, kk * ep
            qg, kg = qs * eb, kk * ekb
            qlm = _stack_heads(ql, k["hmask"]).astype(BF16)
            qum = _stack_heads(qu, k["hmask"]).astype(BF16)
            klb, kub = kl.astype(BF16), ku.astype(BF16)
            a_all = jnp.where(k["lower"], _dot(qlm, klb, NT),
                              jnp.where(k["upper"], _dot(qum, kub, NT), 0.0)).astype(BF16)
            st = sst_ref[j]
            stb = st.astype(BF16)
            ds = ds_scr[...]
            dsb = ds.astype(BF16)
            ds_new = ds * ebl
            da_parts = []
            dqg = jnp.zeros((tl, GLA_KW), F32)
            dkg = jnp.zeros((tl, GLA_KW), F32)
            for h in range(GLA_HEADS):
                cols = slice(h * GLA_DV, (h + 1) * GLA_DV)
                hr = slice(h * tl, (h + 1) * tl)
                ocols = slice(RET_HEADS * RET_D + h * GLA_DV, RET_HEADS * RET_D + (h + 1) * GLA_DV)
                o = oraw_ref[rows, ocols]
                g = gg_ref[rows, cols]
                w = gnw_ref[:, cols]
                dout = dmix_ref[rows, ocols]
                inv = lax.rsqrt(jnp.mean(o * o, axis=-1, keepdims=True) + LN_EPS)
                n = o * inv
                sg = _sigmoid(g)
                sil = g * sg
                dn = dout * w * sil
                dvec_ref[1:2, cols] += jnp.sum(dout * n * sil, axis=0, keepdims=True)
                dproj_ref[rows, OFF_GG + h * GLA_DV:OFF_GG + (h + 1) * GLA_DV] = (
                    dout * n * w * (sg * (1.0 + g * (1.0 - sg)))).astype(BF16)
                dob = (inv * (dn - n * jnp.mean(dn * n, axis=-1, keepdims=True))).astype(BF16)
                vb = gv_ref[rows, cols].astype(BF16)
                mh = k["hmask"][h]
                da_parts.append(_dot(dob, vb, NT))
                dv = _dot(a_all[hr, :], dob, TN) + _dot((kg * mh).astype(BF16), dsb, NT)
                dproj_ref[rows, OFF_GV + h * GLA_DV:OFF_GV + (h + 1) * GLA_DV] = dv.astype(BF16)
                dkg = dkg + mh * _dot(vb, dsb)
                dqg = dqg + mh * _dot(dob, stb)
                ds_new = ds_new + _dot(dob, (qg * mh).astype(BF16), TN)
            da_all = jnp.concatenate(da_parts, axis=0)
            dal = jnp.where(k["lower"], da_all, 0.0).astype(BF16)
            dau = jnp.where(k["upper"], da_all, 0.0).astype(BF16)
            dqlm = _dot(dal, klb)
            dqum = _dot(dau, kub)
            dql = jnp.zeros((tl, GLA_KW), F32)
            dqu = jnp.zeros((tl, GLA_KW), F32)
            for h in range(GLA_HEADS):
                hr = slice(h * tl, (h + 1) * tl)
                dql = dql + k["hmask"][h] * dqlm[hr, :]
                dqu = dqu + k["hmask"][h] * dqum[hr, :]
            dkl = _dot(dal, qlm, TN)
            dku = _dot(dau, qum, TN)
            dbl = (jnp.sum(dkg * kg, axis=0, keepdims=True)
                   + jnp.sum(ds * st, axis=0, keepdims=True) * ebl)
            ds_scr[...] = ds_new
            dqs = dql * ep + dqu * em + dqg * eb
            dk = dkl * em + dku * ep + dkg * ekb
            db = dql * ql - dkl * kl - dqu * qu + dku * ku + dqg * qg - dkg * kg
            db = db + jnp.where(last_row, dbl, 0.0)
            dla = _dot_split(k["utri"], db, NN, a_exact=True)
            dz = dla * (1.0 / GATE_TAU) * _sigmoid(-z)
            dvec_ref[2:3, 0:GLA_KW] += jnp.sum(dz, axis=0, keepdims=True)
            dgw_ref[...] += _dot_split(glr, dz, TN)
            dproj_ref[rows, OFF_GLR:OFF_GLR + 128] = _dot(dz.astype(BF16), gw_ref[...].astype(BF16), NT).astype(BF16)
            dproj_ref[rows, OFF_GQ:OFF_GQ + GLA_KW] = (dqs * scale_g).astype(BF16)
            dproj_ref[rows, OFF_GK:OFF_GK + GLA_KW] = dk.astype(BF16)
            return carry

        _for_tiles(tc // tr, ret_tile)
        _for_tiles(tc // tg, gla_tile)

        @pl.when(step == nsteps - 1)
        def _():
            _DirectExchange(send_refs, recv_refs, xs, xr, xl).finish()

    rev = lambda i: (nsteps - 1 - i, 0)

    def col(width, off):
        return pl.BlockSpec((tc, width), lambda i, o=off // width: (nsteps - 1 - i, o))

    fix = lambda i: (0, 0)
    hbm = pl.BlockSpec(memory_space=pl.ANY)
    in_specs = [pl.BlockSpec((tc, D_MODEL), rev),
                col(512, OFF_RQ), col(512, OFF_RK), col(512, OFF_RV), col(512, OFF_RG),
                col(256, OFF_GQ), col(256, OFF_GK), col(512, OFF_GV), col(512, OFF_GG), col(128, OFF_GLR),
                pl.BlockSpec((tc, D_MODEL), rev), pl.BlockSpec((tc, RET_D), rev), pl.BlockSpec((tc, RET_D), rev),
                pl.BlockSpec((tc // tr, RET_HEADS * RET_D, RET_D), lambda i: (nsteps - 1 - i, 0, 0)),
                pl.BlockSpec((tc // tg, GLA_DV, GLA_KW), lambda i: (nsteps - 1 - i, 0, 0)),
                pl.BlockSpec((128, GLA_KW), fix), pl.BlockSpec((1, GLA_KW), fix),
                pl.BlockSpec((1, 512), fix), pl.BlockSpec((1, 512), fix)] + [hbm] * ns
    out_specs = (pl.BlockSpec((tc, D_IN_PAD), rev), pl.BlockSpec((128, GLA_KW), fix),
                 pl.BlockSpec((8, 512), fix)) + tuple([hbm] * ns)
    out_shape = (jax.ShapeDtypeStruct((t, D_IN_PAD), BF16), jax.ShapeDtypeStruct((128, GLA_KW), F32),
                 jax.ShapeDtypeStruct((8, 512), F32)) + tuple(jax.ShapeDtypeStruct(s.shape, s.dtype) for s in slabs)
    return pl.pallas_call(
        body, name="mixer_bwd", grid=(nsteps,), in_specs=in_specs, out_specs=out_specs, out_shape=out_shape,
        scratch_shapes=[pltpu.VMEM((RET_HEADS * RET_D, RET_D), F32), pltpu.VMEM((GLA_DV, GLA_KW), F32),
                        pltpu.SemaphoreType.DMA((7 * ns,)), pltpu.SemaphoreType.DMA((7 * ns,)),
                        pltpu.SemaphoreType.DMA((ns,))],
        compiler_params=pltpu.CompilerParams(dimension_semantics=("arbitrary",), vmem_limit_bytes=V7X_VMEM_LIMIT),
    )(dmix, *([proj] * 9), oraw, cos_t, sin_t, rst, sst, gw_pad, gb, rnw, gnw, *slabs)


def _inproj_bwd(dproj, x2d, dxa, sc1p, w_in_t, slab):
    t = x2d.shape[0]
    tm = min(PROJ_TILE, t)
    nsteps = t // tm

    def body(dp_ref, x_ref, dxa_ref, sc_ref, w_hbm, send_ref, gx_ref, acc_ref, recv_ref, w_vmem, sem, xs, xr, xl):
        step = pl.program_id(0)
        first = step == 0
        _load_w_in_t(first, w_hbm, w_vmem, sem)

        @pl.when(first)
        def _():
            acc_ref[...] = jnp.zeros_like(acc_ref)
            _DirectExchange([send_ref], [recv_ref], xs, xr, xl).start()

        du = _dot(dp_ref[...], w_vmem[...])
        xh, rstd = _ln_stats(x_ref[...])
        gx_ref[...] = dxa_ref[...] + _ln_bwd(du * sc_ref[...], xh, rstd)
        acc_ref[0:1, :] += jnp.sum(du * xh, axis=0, keepdims=True)
        acc_ref[1:2, :] += jnp.sum(du, axis=0, keepdims=True)

        @pl.when(step == nsteps - 1)
        def _():
            _DirectExchange([send_ref], [recv_ref], xs, xr, xl).finish()

    row = lambda i: (i, 0)
    fix = lambda i: (0, 0)
    hbm = pl.BlockSpec(memory_space=pl.ANY)
    return pl.pallas_call(
        body, name="inproj_bwd", grid=(nsteps,),
        in_specs=[pl.BlockSpec((tm, D_IN_PAD), row), pl.BlockSpec((tm, D_MODEL), row), pl.BlockSpec((tm, D_MODEL), row),
                  pl.BlockSpec((1, D_MODEL), fix), hbm, hbm],
        out_specs=(pl.BlockSpec((tm, D_MODEL), row), pl.BlockSpec((8, D_MODEL), fix), hbm),
        out_shape=(jax.ShapeDtypeStruct((t, D_MODEL), F32), jax.ShapeDtypeStruct((8, D_MODEL), F32),
                   jax.ShapeDtypeStruct(slab.shape, slab.dtype)),
        scratch_shapes=[pltpu.VMEM((D_IN_PAD, D_MODEL), BF16), pltpu.SemaphoreType.DMA((1,)),
                        pltpu.SemaphoreType.DMA((7,)), pltpu.SemaphoreType.DMA((7,)), pltpu.SemaphoreType.DMA((1,))],
        compiler_params=pltpu.CompilerParams(dimension_semantics=("arbitrary",), vmem_limit_bytes=V7X_VMEM_LIMIT),
    )(dproj, x2d, dxa, sc1p, w_in_t, slab)


def _adam_math(w, g, m, v):
    m = ADAM_B1 * m + (1.0 - ADAM_B1) * g
    v = ADAM_B2 * v + (1.0 - ADAM_B2) * (g * g)
    m_hat = m / (1.0 - ADAM_B1 ** ADAM_STEP)
    v_hat = v / (1.0 - ADAM_B2 ** ADAM_STEP)
    delta = -ADAM_LR * (m_hat / (jnp.sqrt(v_hat) + ADAM_EPS) + ADAM_WD * w)
    return delta, m, v


def _adamw(w, gparts, m, v, name):
    nparts, rows, cols = gparts.shape
    tr = rows
    for cand in (512, 256, 128, 64, 32, 16, 8):
        if rows % cand == 0:
            tr = cand
            break

    def body(w_ref, g_ref, m_ref, v_ref, go_ref, d_ref, mo_ref, vo_ref):
        g = g_ref[0].astype(F32)
        for p in range(1, nparts):
            g = g + g_ref[p].astype(F32)
        delta, mn, vn = _adam_math(w_ref[...], g, m_ref[...], v_ref[...])
        go_ref[...] = g
        d_ref[...] = delta
        mo_ref[...] = mn
        vo_ref[...] = vn

    blk = pl.BlockSpec((tr, cols), lambda i: (i, 0))
    shp = jax.ShapeDtypeStruct((rows, cols), F32)
    return pl.pallas_call(
        body, name=name, grid=(rows // tr,),
        in_specs=[blk, pl.BlockSpec((nparts, tr, cols), lambda i: (0, i, 0)), blk, blk],
        out_specs=(blk, blk, blk, blk), out_shape=(shp, shp, shp, shp),
        compiler_params=pltpu.CompilerParams(dimension_semantics=("arbitrary",), vmem_limit_bytes=V7X_VMEM_LIMIT),
    )(w, gparts, m, v)


def _small_reduce(gathered, gathered_gw, c_all, dmod_cols):
    def body(g_ref, gw_ref, c_ref, dm_ref, sum_ref, gwsum_ref, gb_ref, gwa_ref):
        s = g_ref[0]
        sw = gw_ref[0]
        for p in range(1, N_DEV):
            s = s + g_ref[p]
            sw = sw + gw_ref[p]
        sum_ref[...] = s
        gwsum_ref[...] = sw
        for i in range(6):
            gb_ref[:, i * D_MODEL:(i + 1) * D_MODEL] = s[i:i + 1, :]
        cc = c_ref[...]
        gwa_ref[...] = _dot(cc * _sigmoid(cc), dm_ref[...], TN, HIGHEST)

    vm = pl.BlockSpec(memory_space=pltpu.VMEM)
    return pl.pallas_call(
        body, name="small_reduce",
        out_shape=(jax.ShapeDtypeStruct(gathered.shape[1:], F32), jax.ShapeDtypeStruct(gathered_gw.shape[1:], F32),
                   jax.ShapeDtypeStruct((1, 6 * D_MODEL), F32), jax.ShapeDtypeStruct((D_MODEL, ADA_COLS), F32)),
        in_specs=[vm] * 4, out_specs=(vm, vm, vm, vm),
        compiler_params=pltpu.CompilerParams(vmem_limit_bytes=V7X_VMEM_LIMIT),
    )(gathered, gathered_gw, c_all, dmod_cols)


SMR_LN1W, SMR_LN1B, SMR_LN2W, SMR_LN2B, SMR_NORMS, SMR_MISC = 6, 7, 8, 9, 10, 11


def _adamw_small(gsum, g_b_ada, g_ggw, params, moms, vels):
    n = len(params)

    def body(*refs):
        gsum_ref, gb_ref, gw_ref = refs[:3]
        w_refs, m_refs, v_refs = refs[3:3 + n], refs[3 + n:3 + 2 * n], refs[3 + 2 * n:3 + 3 * n]
        outs = refs[3 + 3 * n:]
        g_refs, d_refs, mo_refs, vo_refs = outs[:n - 1], outs[n - 1:2 * n - 1], outs[2 * n - 1:3 * n - 1], outs[3 * n - 1:]
        grads = [gb_ref[...],
                 gsum_ref[SMR_NORMS:SMR_NORMS + 1, 0:512],
                 gsum_ref[SMR_MISC:SMR_MISC + 1, 0:GLA_KW],
                 gsum_ref[SMR_NORMS:SMR_NORMS + 1, 512:1024],
                 gsum_ref[SMR_LN1W:SMR_LN1W + 1, :], gsum_ref[SMR_LN1B:SMR_LN1B + 1, :],
                 gsum_ref[SMR_LN2W:SMR_LN2W + 1, :], gsum_ref[SMR_LN2B:SMR_LN2B + 1, :],
                 gw_ref[...]]
        for i in range(n):
            delta, mn, vn = _adam_math(w_refs[i][...], grads[i], m_refs[i][...], v_refs[i][...])
            if i < n - 1:
                g_refs[i][...] = grads[i]
            d_refs[i][...] = delta
            mo_refs[i][...] = mn
            vo_refs[i][...] = vn

    vm = pl.BlockSpec(memory_space=pltpu.VMEM)
    shapes = [jax.ShapeDtypeStruct(p.shape, F32) for p in params]
    n_in = 3 + 3 * n
    out_shape = tuple(shapes[:n - 1] + shapes * 3)
    return pl.pallas_call(
        body, name="adamw_small", out_shape=out_shape,
        in_specs=[vm] * n_in, out_specs=tuple([vm] * len(out_shape)),
        compiler_params=pltpu.CompilerParams(vmem_limit_bytes=V7X_VMEM_LIMIT),
    )(gsum, g_b_ada, g_ggw, *params, *moms, *vels)


def kernel(x, c, w_ada, b_ada, w_in, ret_norm_w, gla_gate_w, gla_gate_b, gla_norm_w, w_out, ln1_w, ln1_b, w_ff1, w_ff2, ln2_w, ln2_b, loss_target, m_w_ada, m_b_ada, m_w_in, m_ret_norm_w, m_gla_gate_w, m_gla_gate_b, m_gla_norm_w, m_w_out, m_ln1_w, m_ln1_b, m_w_ff1, m_w_ff2, m_ln2_w, m_ln2_b, v_w_ada, v_b_ada, v_w_in, v_ret_norm_w, v_gla_gate_w, v_gla_gate_b, v_gla_norm_w, v_w_out, v_ln1_w, v_ln1_b, v_w_ff1, v_w_ff2, v_ln2_w, v_ln2_b):
    t = x.shape[1]
    xi, yi, ci = _my_coords()
    me = 4 * xi + 2 * yi + ci
    x2d = x[0]
    tgt = loss_target[0]

    c_ext = jnp.concatenate([c, gla_gate_w[0].reshape(1, GATE_RANK * GLA_KW // N_DEV)], axis=1)
    b_l = lax.dynamic_slice(b_ada, (0, me * ADA_COLS), (1, ADA_COLS))
    c_all3, mod_all, wi_g = _adaln_mod(c_ext, w_ada[0], b_l, w_in[0].T.astype(BF16))
    c_all = c_all3[:, 0, :D_MODEL]
    gate_w = c_all3[:, 0, D_MODEL:].reshape(N_DEV, GATE_RANK, GLA_KW // N_DEV)
    gate_w = gate_w.transpose(1, 0, 2).reshape(GATE_RANK, GLA_KW)
    gw_pad = jnp.zeros((128, GLA_KW), F32).at[:GATE_RANK].set(gate_w)
    mod = lax.dynamic_slice(mod_all, (0, me, 0), (N_DEV, 1, ADA_COLS)).reshape(6, D_MODEL)
    shift1, scale1, gate1, shift2, scale2, gate2 = [mod[i:i + 1] for i in range(6)]

    w_in_t = wi_g.reshape(D_IN, D_MODEL)

    pos = jnp.arange(t, dtype=F32)
    inv = 1.0 / (10000.0 ** jnp.linspace(0.0, 1.0, RET_D // 2, dtype=F32))
    ang = pos[:, None] * inv[None, :]
    cos_t = jnp.concatenate([jnp.cos(ang), jnp.cos(ang)], axis=1)
    sin_t = jnp.concatenate([-jnp.sin(ang), jnp.sin(ang)], axis=1)

    sc1p = 1.0 + scale1
    proj, u = _inproj_fwd(x2d, sc1p, shift1, w_in_t)
    mixed, oraw, rst, sst, wo_g, w1_b, w2_g = _mixer_fwd(
        proj, cos_t, sin_t, gw_pad, gla_gate_b, ret_norm_w, gla_norm_w,
        [w_out[0].astype(BF16), w_ff1[0].astype(BF16), w_ff2[0].astype(BF16)])
    w_out_b = wo_g.reshape(D_MODEL, D_MODEL)
    w2_b = w2_g.reshape(D_FF, D_MODEL)
    vec_f = jnp.concatenate([gate1, 1.0 + scale2, shift2, gate2, ln1_w, ln1_b, ln2_w, ln2_b], axis=0)
    m, x1n, rstd1, u2, a, df, dh2, acc_f = _mid_fwd(mixed, x2d, tgt, vec_f, w_out_b, w1_b, w2_b)

    vec_b = jnp.concatenate([gate1, 1.0 + scale2, ln1_w, ln1_b, jnp.zeros((4, D_MODEL), F32)], axis=0)
    da, dm, dmix, dxa, acc_b = _ffn_bwd(df, a, dh2, x1n, rstd1, m, vec_b, w_out_b, w1_b, w2_b)
    dw2 = _matmul_tn(a, df, 2048, 1024, 1024, "tn_dw2", relu_sq=True)
    dw1 = _matmul_tn(u2, da, 1024, 2048, 1024, "tn_dw1", col_slab=FF_COLS)
    dwo = _matmul_tn(mixed, dm, 1024, 1024, 1024, "tn_dwout")
    dproj, dgw, dvec, r_wo, r_w1, r_w2 = _mixer_bwd(
        dmix, proj, oraw, cos_t, sin_t, rst, sst, gw_pad, gla_gate_b, ret_norm_w, gla_norm_w,
        [dwo.reshape(N_DEV, OUT_ROWS, D_MODEL), dw1, dw2.reshape(N_DEV, FF_COLS, D_MODEL)])
    dwi = _matmul_tn(u, dproj, 1024, D_IN_PAD, 512, "tn_dwin")
    dwi_s = dwi[:, :D_IN].reshape(D_MODEL, N_DEV, IN_COLS).transpose(1, 0, 2)
    grad_x, acc_i, r_wi = _inproj_bwd(dproj, x2d, dxa, sc1p, w_in_t, dwi_s)

    loss_part = jnp.sum(acc_f[3])
    small = jnp.concatenate([
        acc_i[1:2], acc_i[0:1], acc_b[4:5], acc_b[1:2], acc_b[0:1], acc_f[2:3],
        acc_b[2:3], acc_b[3:4], acc_f[0:1], acc_f[1:2],
        jnp.concatenate([dvec[0:1], dvec[1:2]], axis=1),
        jnp.concatenate([dvec[2:3, :GLA_KW], jnp.full((1, 128), loss_part, F32),
                         jnp.zeros((1, D_MODEL - GLA_KW - 128), F32)], axis=1),
        jnp.zeros((4, D_MODEL), F32)], axis=0)
    small_all, gw_all = _small_gather([small, dgw[:GATE_RANK]])
    dmod_all = small_all[:, :6].reshape(N_DEV, 6 * D_MODEL)
    dmod_cols = lax.dynamic_slice(dmod_all, (0, me * ADA_COLS), (N_DEV, ADA_COLS))
    ssum, gw_sum, g_b_ada, g_w_ada = _small_reduce(small_all, gw_all, c_all, dmod_cols)
    loss = ssum[SMR_MISC, GLA_KW]
    g_ggw = lax.dynamic_slice(gw_sum, (0, me * (GLA_KW // N_DEV)), (GATE_RANK, GLA_KW // N_DEV))[None]

    small_w = [b_ada, ret_norm_w, gla_gate_b, gla_norm_w, ln1_w, ln1_b, ln2_w, ln2_b, gla_gate_w]
    small_m = [m_b_ada, m_ret_norm_w, m_gla_gate_b, m_gla_norm_w, m_ln1_w, m_ln1_b, m_ln2_w, m_ln2_b, m_gla_gate_w]
    small_v = [v_b_ada, v_ret_norm_w, v_gla_gate_b, v_gla_norm_w, v_ln1_w, v_ln1_b, v_ln2_w, v_ln2_b, v_gla_gate_w]
    res = _adamw_small(ssum, g_b_ada, g_ggw, small_w, small_m, small_v)
    small_g = list(res[:8]) + [g_ggw]
    d_small, m_small, v_small = list(res[8:17]), list(res[17:26]), list(res[26:35])

    _, d_w_ada, nm_w_ada, nv_w_ada = _adamw(w_ada[0], g_w_ada[None], m_w_ada[0], v_w_ada[0], "adamw_ada")

    big =[_adamw(w[0], r, m_[0], v_[0], nm) for w, r, m_, v_, nm in (
        (w_in, r_wi, m_w_in, v_w_in, "adamw_in"), (w_out, r_wo, m_w_out, v_w_out, "adamw_out"),
        (w_ff1, r_w1, m_w_ff1, v_w_ff1, "adamw_ff1"), (w_ff2, r_w2, m_w_ff2, v_w_ff2, "adamw_ff2"))]
    g_big, d_big, m_big, v_big = [[b[i][None] for b in big] for i in range(4)]

    def ordered(w_ada_v, small_vals, big_vals):
        b_ada_v, rnw_v, ggb_v, gnw_v, l1w_v, l1b_v, l2w_v, l2b_v, ggw_v = small_vals
        wi_v, wo_v, w1_v, w2_v = big_vals
        return [w_ada_v, b_ada_v, wi_v, rnw_v, ggw_v, ggb_v, gnw_v, wo_v, l1w_v, l1b_v, w1_v, w2_v, l2w_v, l2b_v]

    grads = ordered(g_w_ada[None], small_g, g_big)
    deltas = ordered(d_w_ada[None], d_small, d_big)
    new_m = ordered(nm_w_ada[None], m_small, m_big)
    new_v = ordered(nv_w_ada[None], v_small, v_big)
    return (loss, grad_x[None], *grads, *deltas, *new_m, *new_v)
```

```python
import functools

import numpy as np
import jax
import jax.numpy as jnp
from jax import lax
from jax.experimental import pallas as pl
from jax.experimental.pallas import tpu as pltpu

F32 = jnp.float32
BF16 = jnp.bfloat16
MESH = pl.DeviceIdType.MESH
HIGHEST = lax.Precision.HIGHEST

N_DEV = 8
D_MODEL = 1024
CHUNK = 64
RET_HEADS = 4
RET_D = 128
GLA_HEADS = 4
GLA_DK = 64
GLA_DV = 128
GLA_KW = GLA_HEADS * GLA_DK
GATE_RANK = 16
GATE_TAU = 16.0
D_FF = 4096
LN_EPS = 1e-5
ALPHA = (2.0 * 1) ** 0.25
D_IN = 3600
D_IN_PAD = 3712
ADA_COLS = 6 * D_MODEL // N_DEV
IN_COLS = D_IN // N_DEV
FF_COLS = D_FF // N_DEV
OUT_ROWS = D_MODEL // N_DEV

OFF_RQ, OFF_RK, OFF_RV, OFF_RG = 0, 512, 1024, 1536
OFF_GQ, OFF_GK, OFF_GV, OFF_GG, OFF_GLR = 2048, 2304, 2560, 3072, 3584

ADAM_LR, ADAM_B1, ADAM_B2, ADAM_EPS, ADAM_WD, ADAM_STEP = 0.001, 0.9, 0.999, 1e-08, 0.01, 10

V7X_VMEM_LIMIT = 56 * 1024 * 1024

ROW_TILE = 256
PROJ_TILE = 512
MIX_TILE = 256
GLA_SUB = 128


def _log_gamma(h):
    return float(np.log(np.float32(1.0) - np.float32(2.0) ** np.float32(-5.0 - h)))


def _my_coords():
    return lax.axis_index("x"), lax.axis_index("y"), lax.axis_index("c")


def _flip(v, bit):
    return 1 - v if bit else v


def _peer(k):
    x, y, c = _my_coords()
    px, py, pc = _flip(x, (k >> 2) & 1), _flip(y, (k >> 1) & 1), _flip(c, k & 1)
    return (px, py, pc), 4 * px + 2 * py + pc


def _dot(a, b, dims=(((1,), (0,)), ((), ())), precision=None):
    return lax.dot_general(a, b, dims, precision=precision, preferred_element_type=F32)


NN = (((1,), (0,)), ((), ()))
NT = (((1,), (1,)), ((), ()))
TN = (((0,), (0,)), ((), ()))


def _split_bf16(v, parts):
    out = []
    for _ in range(parts):
        p = v.astype(BF16)
        out.append(p)
        v = v - p.astype(F32)
    return out


def _dot_split(a, b, dims, a_exact=False):
    if a_exact:
        ab = a.astype(BF16)
        return sum(_dot(ab, p, dims) for p in _split_bf16(b, 3))
    a_hi, a_lo = _split_bf16(a, 2)
    b_hi, b_lo = _split_bf16(b, 2)
    return _dot(a_hi, b_hi, dims) + _dot(a_hi, b_lo, dims) + _dot(a_lo, b_hi, dims)


def _sigmoid(x):
    return 1.0 / (1.0 + jnp.exp(-x))


def _ln_stats(x):
    mu = jnp.mean(x, axis=-1, keepdims=True)
    xc = x - mu
    var = jnp.mean(xc * xc, axis=-1, keepdims=True)
    rstd = lax.rsqrt(var + LN_EPS)
    return xc * rstd, rstd


def _ln_bwd(dyh, xh, rstd):
    return rstd * (dyh - jnp.mean(dyh, axis=-1, keepdims=True) - xh * jnp.mean(dyh * xh, axis=-1, keepdims=True))


def _adaln_mod(c_ext, w_ada_l, b_l, w_in_l):
    width = c_ext.shape[1]

    def body(c_ref, w_ref, b_ref, wi_ref, call_ref, mod_ref, wig_ref, s1, r1, s2, r2, gs, gr, gl):
        gather = _TwoLevelGather([wi_ref], [wig_ref], gs, gr, gl)
        gather.start()
        x, y, c = _my_coords()
        me = 4 * x + 2 * y + c
        call_ref[me] = c_ref[...]
        sends = []
        for k in range(1, N_DEV):
            peer, _ = _peer(k)
            cp = pltpu.make_async_remote_copy(c_ref, call_ref.at[me], s1.at[k - 1], r1.at[k - 1],
                                              device_id=peer, device_id_type=MESH)
            cp.start()
            sends.append(cp)
        for k in range(1, N_DEV):
            peer, pid = _peer(k)
            pltpu.make_async_remote_copy(c_ref, call_ref.at[pid], s1.at[k - 1], r1.at[k - 1],
                                         device_id=peer, device_id_type=MESH).wait_recv()
        for cp in sends:
            cp.wait_send()
        row = lax.broadcasted_iota(jnp.int32, (N_DEV, D_MODEL), 0)
        call = jnp.zeros((N_DEV, D_MODEL), F32)
        for j in range(N_DEV):
            call = jnp.where(row == j, jnp.broadcast_to(call_ref[j][:, :D_MODEL], (N_DEV, D_MODEL)), call)
        sc = call * _sigmoid(call)
        mod = _dot(sc, w_ref[...], NN, HIGHEST) + b_ref[...]
        mod_ref[me] = mod
        sends = []
        for k in range(1, N_DEV):
            peer, _ = _peer(k)
            cp = pltpu.make_async_remote_copy(mod_ref.at[me], mod_ref.at[me], s2.at[k - 1], r2.at[k - 1],
                                              device_id=peer, device_id_type=MESH)
            cp.start()
            sends.append(cp)
        for k in range(1, N_DEV):
            peer, pid = _peer(k)
            pltpu.make_async_remote_copy(mod_ref.at[pid], mod_ref.at[pid], s2.at[k - 1], r2.at[k - 1],
                                         device_id=peer, device_id_type=MESH).wait_recv()
        for cp in sends:
            cp.wait_send()
        gather.forward()
        gather.finish()

    vm = pl.BlockSpec(memory_space=pltpu.VMEM)
    hbm = pl.BlockSpec(memory_space=pl.ANY)
    return pl.pallas_call(
        body, name="adaln_mod",
        out_shape=(jax.ShapeDtypeStruct((N_DEV, 1, width), F32),
                   jax.ShapeDtypeStruct((N_DEV, N_DEV, ADA_COLS), F32),
                   jax.ShapeDtypeStruct((N_DEV, *w_in_l.shape), w_in_l.dtype)),
        in_specs=[vm, vm, vm, hbm], out_specs=(vm, vm, hbm),
        scratch_shapes=[pltpu.SemaphoreType.DMA((N_DEV - 1,))] * 4
        + [pltpu.SemaphoreType.DMA((7,)), pltpu.SemaphoreType.DMA((7,)), pltpu.SemaphoreType.DMA((1,))],
        compiler_params=pltpu.CompilerParams(vmem_limit_bytes=V7X_VMEM_LIMIT),
    )(c_ext, w_ada_l, b_l, w_in_l)


class _TwoLevelGather:
    def __init__(self, x_refs, out_refs, send_sems, recv_sems, local_sems):
        self.x_refs, self.out_refs = x_refs, out_refs
        self.send_sems, self.recv_sems, self.local_sems = send_sems, recv_sems, local_sems
        x, y, c = _my_coords()
        self.c = c
        self.me, self.sibling = (x, y, c), (x, y, 1 - c)
        self.chips = [(1 - x, y), (x, 1 - y), (1 - x, 1 - y)]

    def _copy(self, a, k, block, to, src=None):
        px, py, pc = block
        slab = self.out_refs[a].at[4 * px + 2 * py + pc]
        return pltpu.make_async_remote_copy(
            src_ref=slab if src is None else src, dst_ref=slab,
            send_sem=self.send_sems.at[7 * a + k], recv_sem=self.recv_sems.at[7 * a + k],
            device_id=to, device_id_type=MESH)

    def _mine(self, a):
        px, py, pc = self.me
        return pltpu.make_async_copy(self.x_refs[a], self.out_refs[a].at[4 * px + 2 * py + pc], self.local_sems.at[a])

    def _first(self, a):
        cps = [self._copy(a, 0, self.me, self.sibling, src=self.x_refs[a])]
        cps += [self._copy(a, 1 + j, self.me, (*chip, self.c), src=self.x_refs[a]) for j, chip in enumerate(self.chips)]
        return cps

    def _passed(self, a):
        return [self._copy(a, 4 + j, (*chip, self.c), self.sibling) for j, chip in enumerate(self.chips)]

    def start(self):
        for a in range(len(self.x_refs)):
            self._mine(a).start()
            for cp in self._first(a):
                cp.start()

    def forward(self):
        for a in range(len(self.x_refs)):
            passed = self._passed(a)
            for j, chip in enumerate(self.chips):
                self._copy(a, 1 + j, (*chip, self.c), self.me).wait_recv()
                passed[j].start()

    def finish(self):
        for a in range(len(self.x_refs)):
            self._copy(a, 0, self.sibling, self.me).wait_recv()
            for j, chip in enumerate(self.chips):
                self._copy(a, 4 + j, (*chip, 1 - self.c), self.me).wait_recv()
            for cp in self._first(a) + self._passed(a):
                cp.wait_send()
            self._mine(a).wait()


class _DirectExchange:
    def __init__(self, s_refs, r_refs, send_sems, recv_sems, local_sems):
        self.s_refs, self.r_refs = s_refs, r_refs
        self.send_sems, self.recv_sems, self.local_sems = send_sems, recv_sems, local_sems
        x, y, c = _my_coords()
        self.me = 4 * x + 2 * y + c

    def _mine(self, a):
        return pltpu.make_async_copy(self.s_refs[a].at[self.me], self.r_refs[a].at[self.me], self.local_sems.at[a])

    def _send(self, a, k):
        peer, pid = _peer(k)
        return pltpu.make_async_remote_copy(self.s_refs[a].at[pid], self.r_refs[a].at[self.me],
                                            self.send_sems.at[7 * a + k - 1], self.recv_sems.at[7 * a + k - 1],
                                            device_id=peer, device_id_type=MESH)

    def _recv(self, a, k):
        peer, pid = _peer(k)
        return pltpu.make_async_remote_copy(self.s_refs[a].at[pid], self.r_refs[a].at[pid],
                                            self.send_sems.at[7 * a + k - 1], self.recv_sems.at[7 * a + k - 1],
                                            device_id=peer, device_id_type=MESH)

    def start(self):
        for a in range(len(self.s_refs)):
            self._mine(a).start()
            for k in range(1, N_DEV):
                self._send(a, k).start()

    def finish(self):
        for a in range(len(self.s_refs)):
            for k in range(1, N_DEV):
                self._recv(a, k).wait_recv()
            for k in range(1, N_DEV):
                self._send(a, k).wait_send()
            self._mine(a).wait()


def _small_gather(vecs):
    n = len(vecs)

    def body(*refs):
        v_refs, out_refs, s_sems, r_sems = refs[:n], refs[n:2 * n], refs[2 * n], refs[2 * n + 1]
        x, y, c = _my_coords()
        me = 4 * x + 2 * y + c
        sends = []
        for a in range(n):
            out_refs[a][me] = v_refs[a][...]
            for k in range(1, N_DEV):
                peer, _ = _peer(k)
                cp = pltpu.make_async_remote_copy(v_refs[a], out_refs[a].at[me], s_sems.at[7 * a + k - 1],
                                                  r_sems.at[7 * a + k - 1], device_id=peer, device_id_type=MESH)
                cp.start()
                sends.append(cp)
        for a in range(n):
            for k in range(1, N_DEV):
                peer, pid = _peer(k)
                pltpu.make_async_remote_copy(v_refs[a], out_refs[a].at[pid], s_sems.at[7 * a + k - 1],
                                             r_sems.at[7 * a + k - 1], device_id=peer, device_id_type=MESH).wait_recv()
        for cp in sends:
            cp.wait_send()

    vm = pl.BlockSpec(memory_space=pltpu.VMEM)
    return pl.pallas_call(
        body, name="small_gather",
        out_shape=tuple(jax.ShapeDtypeStruct((N_DEV, *v.shape), v.dtype) for v in vecs),
        in_specs=[vm] * n, out_specs=tuple([vm] * n),
        scratch_shapes=[pltpu.SemaphoreType.DMA((7 * n,))] * 2,
    )(*vecs)


def _load_resident(step_is_first, pairs, sem):
    @pl.when(step_is_first)
    def _():
        copies = [pltpu.make_async_copy(src, dst, sem.at[i]) for i, (src, dst) in enumerate(pairs)]
        for cp in copies:
            cp.start()
        for cp in copies:
            cp.wait()


def _load_w_in_t(step_is_first, w_hbm, w_vmem, sem):
    @pl.when(step_is_first)
    def _():
        w_vmem[D_IN:, :] = jnp.zeros((D_IN_PAD - D_IN, D_MODEL), BF16)
    _load_resident(step_is_first, [(w_hbm, w_vmem.at[pl.ds(0, D_IN)])], sem)


def _inproj_fwd(x2d, sc1p, sh1, w_in_t):
    t = x2d.shape[0]
    tm = min(PROJ_TILE, t)

    def body(x_ref, sc_ref, sh_ref, w_hbm, proj_ref, u_ref, w_vmem, sem):
        _load_w_in_t(pl.program_id(0) == 0, w_hbm, w_vmem, sem)
        xh, _ = _ln_stats(x_ref[...])
        ub = (xh * sc_ref[...] + sh_ref[...]).astype(BF16)
        u_ref[...] = ub
        proj_ref[...] = _dot(ub, w_vmem[...], NT)

    row = lambda i: (i, 0)
    fix = lambda i: (0, 0)
    return pl.pallas_call(
        body, name="inproj_fwd", grid=(t // tm,),
        in_specs=[pl.BlockSpec((tm, D_MODEL), row), pl.BlockSpec((1, D_MODEL), fix), pl.BlockSpec((1, D_MODEL), fix),
                  pl.BlockSpec(memory_space=pl.ANY)],
        out_specs=(pl.BlockSpec((tm, D_IN_PAD), row), pl.BlockSpec((tm, D_MODEL), row)),
        out_shape=(jax.ShapeDtypeStruct((t, D_IN_PAD), F32), jax.ShapeDtypeStruct((t, D_MODEL), BF16)),
        scratch_shapes=[pltpu.VMEM((D_IN_PAD, D_MODEL), BF16), pltpu.SemaphoreType.DMA((1,))],
        compiler_params=pltpu.CompilerParams(dimension_semantics=("arbitrary",), vmem_limit_bytes=V7X_VMEM_LIMIT),
    )(x2d, sc1p, sh1, w_in_t)


CHUNK_SHIFT = 6


def _ret_tables(t, tl):
    r = lax.broadcasted_iota(jnp.int32, (tl, tl), 0)
    c = lax.broadcasted_iota(jnp.int32, (tl, tl), 1)
    allowed = jnp.right_shift(c, CHUNK_SHIFT) <= jnp.right_shift(r, CHUNK_SHIFT)
    dist = jnp.abs(r - c).astype(F32)
    rowf = lax.broadcasted_iota(jnp.int32, (tl, RET_D), 0).astype(F32)
    lgs = [_log_gamma(h) for h in range(RET_HEADS)]
    dec = jnp.stack([jnp.where(allowed, jnp.exp(lg * dist), 0.0) for lg in lgs])
    qkd = jnp.stack([jnp.exp(lg * (rowf + 1.0)) for lg in lgs] + [jnp.exp(lg * (tl - 1.0 - rowf)) for lg in lgs])
    inv = 1.0 / (10000.0 ** jnp.linspace(0.0, 1.0, RET_D // 2, dtype=F32))
    off = jnp.arange(tl, dtype=F32)[:, None] * inv[None, :]
    start = (jnp.arange(t // tl, dtype=F32) * tl)[:, None] * inv[None, :]
    co, so = jnp.cos(off), jnp.sin(off)
    rot_in = jnp.stack([jnp.concatenate([co, co], 1), jnp.concatenate([so, so], 1),
                        jnp.concatenate([-co, co], 1), jnp.concatenate([-so, so], 1)])
    cs, ss = jnp.cos(start), jnp.sin(start)
    rot_tile = jnp.concatenate([cs, cs, ss, ss], axis=1)
    rot_tile = jnp.broadcast_to(rot_tile[:, None, :], (t // tl, 8, 2 * RET_D))
    return dec, qkd, rot_in, rot_tile


def _tile_gammas(tl):
    return [float(np.exp(np.float32(_log_gamma(h)) * np.float32(tl))) for h in range(RET_HEADS)]


def _tile_rotary(rot_in_ref, rot_tile_ref):
    ca, sa = rot_tile_ref[0, 0:1, 0:RET_D], rot_tile_ref[0, 0:1, RET_D:2 * RET_D]
    cosv = ca * rot_in_ref[0] - sa * rot_in_ref[1]
    sinv = sa * rot_in_ref[2] + ca * rot_in_ref[3]
    return cosv, sinv


def _gla_consts(tl):
    r = lax.broadcasted_iota(jnp.int32, (tl, tl), 0)
    c = lax.broadcasted_iota(jnp.int32, (tl, tl), 1)
    ltri = (c <= r).astype(F32)
    utri = (c >= r).astype(F32)
    lane = lax.broadcasted_iota(jnp.int32, (1, GLA_KW), 1)
    hmask = [((lane >= h * GLA_DK) & (lane < (h + 1) * GLA_DK)).astype(F32) for h in range(GLA_HEADS)]
    rs = lax.broadcasted_iota(jnp.int32, (GLA_HEADS * tl, tl), 0) & (tl - 1)
    cs = lax.broadcasted_iota(jnp.int32, (GLA_HEADS * tl, tl), 1)
    lower = cs <= rs
    same = jnp.right_shift(cs, CHUNK_SHIFT) == jnp.right_shift(rs, CHUNK_SHIFT)
    upper = jnp.logical_and(jnp.logical_not(lower), same)
    return dict(ltri=ltri, utri=utri, hmask=hmask, lower=lower, upper=upper)


def _tile_rows(j, tl):
    return pl.ds(j * tl, tl) if isinstance(j, int) else pl.ds(pl.multiple_of(j * tl, tl), tl)


def _for_tiles(cps, fn):
    if cps == 1:
        fn(0, 0)
    else:
        lax.fori_loop(0, cps, fn, 0)


def _rotate(v, cosv, sinv):
    return v * cosv + pltpu.roll(v, RET_D // 2, 1) * sinv


def _rotate_t(d, cosv, sinv):
    return d * cosv + pltpu.roll(d * sinv, RET_D // 2, 1)


def _stack_heads(v, hmask):
    return jnp.concatenate([v * hmask[h] for h in range(GLA_HEADS)], axis=0)


def _gla_gates(glr, gw, gb, ltri, tl):
    z = _dot_split(glr, gw, NN) + gb
    la = (jnp.minimum(z, 0.0) - jnp.log(1.0 + jnp.exp(-jnp.abs(z)))) * (1.0 / GATE_TAU)
    b = _dot_split(ltri, la, NN, a_exact=True)
    level = b[tl // 2 - 1:tl // 2, :]
    ep = jnp.exp(jnp.clip(b - level, -80.0, 80.0))
    em = jnp.exp(jnp.clip(level - b, -80.0, 80.0))
    bl = b[tl - 1:tl, :]
    return z, b, bl, ep, em


def _mixer_fwd(proj, tables, gw_pad, gb, rnw, gnw, shards):
    t = proj.shape[0]
    tc = min(MIX_TILE, t)
    tr, tg = tc, min(GLA_SUB, tc)
    nsteps = t // tc
    fwd_step = (3 * nsteps) // 4
    ns = len(shards)
    scale_r = RET_D ** -0.5
    scale_g = GLA_DK ** -0.5
    gammas = _tile_gammas(tr)

    def body(rq_ref, rk_ref, rv_ref, rg_ref, gq_ref, gk_ref, gv_ref, gg_ref, glr_ref,
             dec_ref, qkd_ref, rot_in_ref, rot_tile_ref, gw_ref, gb_ref, rnw_ref, gnw_ref, *rest):
        shard_refs, rest = rest[:ns], rest[ns:]
        mix_ref, oraw_ref, qrb_ref, krb_ref, rst_ref, sst_ref = rest[:6]
        gathered_refs, rest = rest[6:6 + ns], rest[6 + ns:]
        r_scr, s_scr, gs, gr, gl = rest
        step = pl.program_id(0)

        @pl.when(step == 0)
        def _():
            r_scr[...] = jnp.zeros_like(r_scr)
            s_scr[...] = jnp.zeros_like(s_scr)
            _TwoLevelGather(shard_refs, gathered_refs, gs, gr, gl).start()

        if fwd_step != nsteps - 1:
            @pl.when(step == fwd_step)
            def _():
                _TwoLevelGather(shard_refs, gathered_refs, gs, gr, gl).forward()

        gla_k = _gla_consts(tg)

        def ret_tile(j, carry):
            rows = _tile_rows(j, tr)
            cosv, sinv = _tile_rotary(rot_in_ref, rot_tile_ref)
            for h in range(RET_HEADS):
                cols = slice(h * RET_D, (h + 1) * RET_D)
                qr = _rotate(rq_ref[rows, cols], cosv, sinv) * scale_r
                kr = _rotate(rk_ref[rows, cols], cosv, sinv)
                vb = rv_ref[rows, cols].astype(BF16)
                qb, kb = qr.astype(BF16), kr.astype(BF16)
                qrb_ref[rows, cols] = qb
                krb_ref[rows, cols] = kb
                p = _dot(qb, kb, NT) * dec_ref[h]
                rp = r_scr[cols, :]
                o = _dot(p.astype(BF16), vb) + _dot((qr * qkd_ref[h]).astype(BF16), rp.astype(BF16))
                rst_ref[j, cols, :] = rp
                r_scr[cols, :] = gammas[h] * rp + _dot((kr * qkd_ref[RET_HEADS + h]).astype(BF16), vb, TN)
                oraw_ref[rows, cols] = o
                oc = o - jnp.mean(o, axis=-1, keepdims=True)
                n = oc * lax.rsqrt(jnp.mean(oc * oc, axis=-1, keepdims=True) + LN_EPS)
                g = rg_ref[rows, cols]
                mix_ref[rows, cols] = (n * rnw_ref[:, cols] * (g * _sigmoid(g))).astype(BF16)
            return carry

        def gla_tile(j, carry):
            k = gla_k
            tl = tg
            rows = _tile_rows(j, tg)
            _, b, bl, ep, em = _gla_gates(glr_ref[rows, :], gw_ref[...], gb_ref[...], k["ltri"], tl)
            qs = gq_ref[rows, :] * scale_g
            kk = gk_ref[rows, :]
            x_all = _dot(_stack_heads(qs * ep, k["hmask"]).astype(BF16), (kk * em).astype(BF16), NT)
            y_all = _dot(_stack_heads(qs * em, k["hmask"]).astype(BF16), (kk * ep).astype(BF16), NT)
            a_all = jnp.where(k["lower"], x_all, jnp.where(k["upper"], y_all, 0.0)).astype(BF16)
            st = s_scr[...]
            oq = _dot(_stack_heads(qs * jnp.exp(b), k["hmask"]).astype(BF16), st.astype(BF16), NT)
            kg = kk * jnp.exp(bl - b)
            sst_ref[j] = st
            st_new = st * jnp.exp(bl)
            for h in range(GLA_HEADS):
                cols = slice(h * GLA_DV, (h + 1) * GLA_DV)
                hr = slice(h * tl, (h + 1) * tl)
                vb = gv_ref[rows, cols].astype(BF16)
                o = _dot(a_all[hr, :], vb) + oq[hr, :]
                st_new = st_new + _dot(vb, (kg * k["hmask"][h]).astype(BF16), TN)
                ocols = slice(RET_HEADS * RET_D + h * GLA_DV, RET_HEADS * RET_D + (h + 1) * GLA_DV)
                oraw_ref[rows, ocols] = o
                n = o * lax.rsqrt(jnp.mean(o * o, axis=-1, keepdims=True) + LN_EPS)
                g = gg_ref[rows, cols]
                mix_ref[rows, ocols] = (n * gnw_ref[:, cols] * (g * _sigmoid(g))).astype(BF16)
            s_scr[...] = st_new
            return carry

        _for_tiles(tc // tr, ret_tile)
        _for_tiles(tc // tg, gla_tile)

        @pl.when(step == nsteps - 1)
        def _():
            gather = _TwoLevelGather(shard_refs, gathered_refs, gs, gr, gl)
            if fwd_step == nsteps - 1:
                gather.forward()
            gather.finish()

    def col(width, off):
        return pl.BlockSpec((tc, width), lambda i, o=off // width: (i, o))

    fix = lambda i: (0, 0)
    fix3 = lambda i: (0, 0, 0)
    hbm = pl.BlockSpec(memory_space=pl.ANY)
    dec, qkd, rot_in, rot_tile = tables
    in_specs = [col(512, OFF_RQ), col(512, OFF_RK), col(512, OFF_RV), col(512, OFF_RG),
                col(256, OFF_GQ), col(256, OFF_GK), col(512, OFF_GV), col(512, OFF_GG), col(128, OFF_GLR),
                pl.BlockSpec(dec.shape, fix3), pl.BlockSpec(qkd.shape, fix3), pl.BlockSpec(rot_in.shape, fix3),
                pl.BlockSpec((1, 8, 2 * RET_D), lambda i: (i, 0, 0)),
                pl.BlockSpec((128, GLA_KW), fix), pl.BlockSpec((1, GLA_KW), fix),
                pl.BlockSpec((1, 512), fix), pl.BlockSpec((1, 512), fix)] + [hbm] * ns
    half = pl.BlockSpec((tc, RET_HEADS * RET_D), lambda i: (i, 0))
    out_specs = (pl.BlockSpec((tc, D_MODEL), lambda i: (i, 0)), pl.BlockSpec((tc, D_MODEL), lambda i: (i, 0)),
                 half, half,
                 pl.BlockSpec((tc // tr, RET_HEADS * RET_D, RET_D), lambda i: (i, 0, 0)),
                 pl.BlockSpec((tc // tg, GLA_DV, GLA_KW), lambda i: (i, 0, 0))) + tuple([hbm] * ns)
    out_shape = (jax.ShapeDtypeStruct((t, D_MODEL), BF16), jax.ShapeDtypeStruct((t, D_MODEL), F32),
                 jax.ShapeDtypeStruct((t, RET_HEADS * RET_D), BF16), jax.ShapeDtypeStruct((t, RET_HEADS * RET_D), BF16),
                 jax.ShapeDtypeStruct((t // tr, RET_HEADS * RET_D, RET_D), F32),
                 jax.ShapeDtypeStruct((t // tg, GLA_DV, GLA_KW), F32)) + tuple(
                     jax.ShapeDtypeStruct((N_DEV, *s.shape), s.dtype) for s in shards)
    return pl.pallas_call(
        body, name="mixer_fwd", grid=(nsteps,), in_specs=in_specs, out_specs=out_specs, out_shape=out_shape,
        scratch_shapes=[pltpu.VMEM((RET_HEADS * RET_D, RET_D), F32), pltpu.VMEM((GLA_DV, GLA_KW), F32),
                        pltpu.SemaphoreType.DMA((7 * ns,)), pltpu.SemaphoreType.DMA((7 * ns,)),
                        pltpu.SemaphoreType.DMA((ns,))],
        compiler_params=pltpu.CompilerParams(dimension_semantics=("arbitrary",), vmem_limit_bytes=V7X_VMEM_LIMIT),
    )(*([proj] * 9), dec, qkd, rot_in, rot_tile, gw_pad, gb, rnw, gnw, *shards)


def _mid_fwd(mixed, x2d, target, vecs, w_out_b, w1_b, w2_b):
    t = x2d.shape[0]
    tm = min(ROW_TILE, t)

    def body(mix_ref, x_ref, tgt_ref, v_ref, wo_hbm, w1_hbm, w2_hbm,
             m_ref, x1n_ref, rstd_ref, u2_ref, a_ref, df_ref, dh2_ref, acc_ref, wo, w1, w2, sem):
        first = pl.program_id(0) == 0
        _load_resident(first, [(wo_hbm, wo), (w1_hbm, w1), (w2_hbm, w2)], sem)

        @pl.when(first)
        def _():
            acc_ref[...] = jnp.zeros_like(acc_ref)

        gate1, sc2p, sh2, gate2 = v_ref[0:1, :], v_ref[1:2, :], v_ref[2:3, :], v_ref[3:4, :]
        l1w, l1b, l2w, l2b = v_ref[4:5, :], v_ref[5:6, :], v_ref[6:7, :], v_ref[7:8, :]
        m = _dot(mix_ref[...], wo[...])
        m_ref[...] = m.astype(BF16)
        x1n, rstd1 = _ln_stats(ALPHA * x_ref[...] + gate1 * m)
        x1n_ref[...] = x1n
        rstd_ref[...] = rstd1
        x1 = x1n * l1w + l1b
        xh1, _ = _ln_stats(x1)
        u2 = (xh1 * sc2p + sh2).astype(BF16)
        u2_ref[...] = u2
        f = jnp.zeros((tm, D_MODEL), F32)
        for j in range(N_DEV):
            cols = slice(j * FF_COLS, (j + 1) * FF_COLS)
            a = _dot(u2, w1[j])
            a_ref[:, cols] = a.astype(BF16)
            r = jnp.maximum(a, 0.0)
            f = f + _dot((r * r).astype(BF16), w2[cols, :])
        yh, rstd2 = _ln_stats(ALPHA * x1 + gate2 * f)
        e = yh * l2w + l2b - tgt_ref[...]
        dy = e * (1.0 / D_MODEL)
        dh2 = _ln_bwd(dy * l2w, yh, rstd2)
        dh2_ref[...] = dh2
        df_ref[...] = (dh2 * gate2).astype(BF16)
        acc_ref[0:1, :] += jnp.sum(dy * yh, axis=0, keepdims=True)
        acc_ref[1:2, :] += jnp.sum(dy, axis=0, keepdims=True)
        acc_ref[2:3, :] += jnp.sum(dh2 * f, axis=0, keepdims=True)
        acc_ref[3:4, :] += jnp.sum(e * e, axis=0, keepdims=True) * (0.5 / D_MODEL)

    row = lambda i: (i, 0)
    fix = lambda i: (0, 0)
    hbm = pl.BlockSpec(memory_space=pl.ANY)
    return pl.pallas_call(
        body, name="mid_fwd", grid=(t // tm,),
        in_specs=[pl.BlockSpec((tm, D_MODEL), row), pl.BlockSpec((tm, D_MODEL), row), pl.BlockSpec((tm, D_MODEL), row),
                  pl.BlockSpec((8, D_MODEL), fix), hbm, hbm, hbm],
        out_specs=(pl.BlockSpec((tm, D_MODEL), row), pl.BlockSpec((tm, D_MODEL), row), pl.BlockSpec((tm, 1), row),
                   pl.BlockSpec((tm, D_MODEL), row), pl.BlockSpec((tm, D_FF), row), pl.BlockSpec((tm, D_MODEL), row),
                   pl.BlockSpec((tm, D_MODEL), row), pl.BlockSpec((8, D_MODEL), fix)),
        out_shape=(jax.ShapeDtypeStruct((t, D_MODEL), BF16), jax.ShapeDtypeStruct((t, D_MODEL), F32),
                   jax.ShapeDtypeStruct((t, 1), F32), jax.ShapeDtypeStruct((t, D_MODEL), BF16),
                   jax.ShapeDtypeStruct((t, D_FF), BF16), jax.ShapeDtypeStruct((t, D_MODEL), BF16),
                   jax.ShapeDtypeStruct((t, D_MODEL), F32), jax.ShapeDtypeStruct((8, D_MODEL), F32)),
        scratch_shapes=[pltpu.VMEM((D_MODEL, D_MODEL), BF16), pltpu.VMEM((N_DEV, D_MODEL, FF_COLS), BF16),
                        pltpu.VMEM((D_FF, D_MODEL), BF16), pltpu.SemaphoreType.DMA((3,))],
        compiler_params=pltpu.CompilerParams(dimension_semantics=("arbitrary",), vmem_limit_bytes=V7X_VMEM_LIMIT),
    )(mixed, x2d, target, vecs, w_out_b, w1_b, w2_b)


def _ffn_bwd(df, a, dh2, x1n, rstd1, m, vecs, w_out_b, w1_b, w2_b):
    t = x1n.shape[0]
    tm = min(ROW_TILE, t)

    def body(df_ref, a_ref, dh2_ref, x1n_ref, rstd_ref, m_ref, v_ref, wo_hbm, w1_hbm, w2_hbm,
             da_ref, dm_ref, dmix_ref, dxa_ref, acc_ref, wo, w1, w2, sem):
        first = pl.program_id(0) == 0
        _load_resident(first, [(wo_hbm, wo), (w1_hbm, w1), (w2_hbm, w2)], sem)

        @pl.when(first)
        def _():
            acc_ref[...] = jnp.zeros_like(acc_ref)

        gate1, sc2p, l1w, l1b = v_ref[0:1, :], v_ref[1:2, :], v_ref[2:3, :], v_ref[3:4, :]
        df = df_ref[...]
        du2 = jnp.zeros((tm, D_MODEL), F32)
        for j in range(N_DEV):
            cols = slice(j * FF_COLS, (j + 1) * FF_COLS)
            dr2 = _dot(df, w2[cols, :], NT)
            da = (dr2 * (2.0 * jnp.maximum(a_ref[:, cols].astype(F32), 0.0))).astype(BF16)
            da_ref[:, cols] = da
            du2 = du2 + _dot(da, w1[j], NT)
        x1n = x1n_ref[...]
        xh1, rstd0 = _ln_stats(x1n * l1w + l1b)
        dx1 = ALPHA * dh2_ref[...] + _ln_bwd(du2 * sc2p, xh1, rstd0)
        dh1 = _ln_bwd(dx1 * l1w, x1n, rstd_ref[...])
        dxa_ref[...] = ALPHA * dh1
        dm = (dh1 * gate1).astype(BF16)
        dm_ref[...] = dm
        dmix_ref[...] = _dot(dm, wo[...], NT)
        acc_ref[0:1, :] += jnp.sum(du2 * xh1, axis=0, keepdims=True)
        acc_ref[1:2, :] += jnp.sum(du2, axis=0, keepdims=True)
        acc_ref[2:3, :] += jnp.sum(dx1 * x1n, axis=0, keepdims=True)
        acc_ref[3:4, :] += jnp.sum(dx1, axis=0, keepdims=True)
        acc_ref[4:5, :] += jnp.sum(dh1 * m_ref[...].astype(F32), axis=0, keepdims=True)

    row = lambda i: (i, 0)
    fix = lambda i: (0, 0)
    hbm = pl.BlockSpec(memory_space=pl.ANY)
    return pl.pallas_call(
        body, name="ffn_bwd", grid=(t // tm,),
        in_specs=[pl.BlockSpec((tm, D_MODEL), row), pl.BlockSpec((tm, D_FF), row), pl.BlockSpec((tm, D_MODEL), row),
                  pl.BlockSpec((tm, D_MODEL), row), pl.BlockSpec((tm, 1), row), pl.BlockSpec((tm, D_MODEL), row),
                  pl.BlockSpec((8, D_MODEL), fix), hbm, hbm, hbm],
        out_specs=(pl.BlockSpec((tm, D_FF), row), pl.BlockSpec((tm, D_MODEL), row), pl.BlockSpec((tm, D_MODEL), row),
                   pl.BlockSpec((tm, D_MODEL), row), pl.BlockSpec((8, D_MODEL), fix)),
        out_shape=(jax.ShapeDtypeStruct((t, D_FF), BF16), jax.ShapeDtypeStruct((t, D_MODEL), BF16),
                   jax.ShapeDtypeStruct((t, D_MODEL), F32), jax.ShapeDtypeStruct((t, D_MODEL), F32),
                   jax.ShapeDtypeStruct((8, D_MODEL), F32)),
        scratch_shapes=[pltpu.VMEM((D_MODEL, D_MODEL), BF16), pltpu.VMEM((N_DEV, D_MODEL, FF_COLS), BF16),
                        pltpu.VMEM((D_FF, D_MODEL), BF16), pltpu.SemaphoreType.DMA((3,))],
        compiler_params=pltpu.CompilerParams(dimension_semantics=("arbitrary",), vmem_limit_bytes=V7X_VMEM_LIMIT),
    )(df, a, dh2, x1n, rstd1, m, vecs, w_out_b, w1_b, w2_b)


def _matmul_tn(lhs, rhs, tmm, tn, tk, name, relu_sq=False, col_slab=None):
    t, mm = lhs.shape
    nn = rhs.shape[1]
    tk = min(tk, t)
    nk = t // tk

    def body(l_ref, r_ref, o_ref, acc):
        kk = pl.program_id(2)

        @pl.when(kk == 0)
        def _():
            acc[...] = jnp.zeros_like(acc)

        l = l_ref[...]
        if relu_sq:
            lf = jnp.maximum(l.astype(F32), 0.0)
            l = (lf * lf).astype(BF16)
        acc[...] += _dot(l, r_ref[...], TN)

        @pl.when(kk == nk - 1)
        def _():
            if col_slab is None:
                o_ref[...] = acc[...].astype(o_ref.dtype)
            else:
                for s in range(tn // col_slab):
                    o_ref[s] = acc[:, s * col_slab:(s + 1) * col_slab].astype(o_ref.dtype)

    if col_slab is None:
        out_spec = pl.BlockSpec((tmm, tn), lambda i, j, k: (i, j))
        out_shape = jax.ShapeDtypeStruct((mm, nn), BF16)
    else:
        out_spec = pl.BlockSpec((tn // col_slab, tmm, col_slab), lambda i, j, k: (j, i, 0))
        out_shape = jax.ShapeDtypeStruct((nn // col_slab, mm, col_slab), BF16)
    return pl.pallas_call(
        body, name=name, grid=(mm // tmm, nn // tn, nk),
        in_specs=[pl.BlockSpec((tk, tmm), lambda i, j, k: (k, i)), pl.BlockSpec((tk, tn), lambda i, j, k: (k, j))],
        out_specs=out_spec,
        out_shape=out_shape,
        scratch_shapes=[pltpu.VMEM((tmm, tn), F32)],
        compiler_params=pltpu.CompilerParams(dimension_semantics=("arbitrary", "arbitrary", "arbitrary"),
                                             vmem_limit_bytes=V7X_VMEM_LIMIT),
    )(lhs, rhs)


def _mixer_bwd(dmix, proj, qrb, krb, oraw, tables, rst, sst, gw_pad, gb, rnw, gnw, slabs):
    t = proj.shape[0]
    tc = min(MIX_TILE, t)
    tr, tg = tc, min(GLA_SUB, tc)
    nsteps = t // tc
    ns = len(slabs)
    scale_r = RET_D ** -0.5
    scale_g = GLA_DK ** -0.5
    gammas = _tile_gammas(tr)

    def body(dmix_ref, qrb_ref, krb_ref, rv_ref, rg_ref, gq_ref, gk_ref, gv_ref, gg_ref, glr_ref, oraw_ref,
             dec_ref, qkd_ref, rot_in_ref, rot_tile_ref, rst_ref, sst_ref, gw_ref, gb_ref, rnw_ref, gnw_ref, *rest):
        send_refs, rest = rest[:ns], rest[ns:]
        dproj_ref, dgw_ref, dvec_ref = rest[:3]
        recv_refs, rest = rest[3:3 + ns], rest[3 + ns:]
        dr_scr, ds_scr, xs, xr, xl = rest
        step = pl.program_id(0)

        @pl.when(step == 0)
        def _():
            dr_scr[...] = jnp.zeros_like(dr_scr)
            ds_scr[...] = jnp.zeros_like(ds_scr)
            dgw_ref[...] = jnp.zeros_like(dgw_ref)
            dvec_ref[...] = jnp.zeros_like(dvec_ref)
            _DirectExchange(send_refs, recv_refs, xs, xr, xl).start()

        gla_k = _gla_consts(tg)
        last_row = lax.broadcasted_iota(jnp.int32, (tg, GLA_KW), 0) == tg - 1

        def ret_tile(jj, carry):
            j = tc // tr - 1 - jj
            rows = _tile_rows(j, tr)
            cosv, sinv = _tile_rotary(rot_in_ref, rot_tile_ref)
            for h in range(RET_HEADS):
                cols = slice(h * RET_D, (h + 1) * RET_D)
                o = oraw_ref[rows, cols]
                g = rg_ref[rows, cols]
                w = rnw_ref[:, cols]
                dout = dmix_ref[rows, cols]
                oc = o - jnp.mean(o, axis=-1, keepdims=True)
                inv = lax.rsqrt(jnp.mean(oc * oc, axis=-1, keepdims=True) + LN_EPS)
                n = oc * inv
                sg = _sigmoid(g)
                sil = g * sg
                dn = dout * w * sil
                dvec_ref[0:1, cols] += jnp.sum(dout * n * sil, axis=0, keepdims=True)
                dproj_ref[rows, OFF_RG + h * RET_D:OFF_RG + (h + 1) * RET_D] = (
                    dout * n * w * (sg * (1.0 + g * (1.0 - sg)))).astype(BF16)
                doc = inv * (dn - n * jnp.mean(dn * n, axis=-1, keepdims=True))
                do = doc - jnp.mean(doc, axis=-1, keepdims=True)

                qb, kb = qrb_ref[rows, cols], krb_ref[rows, cols]
                qr, kr = qb.astype(F32), kb.astype(F32)
                vb = rv_ref[rows, cols].astype(BF16)
                dob = do.astype(BF16)
                qd, kd = qkd_ref[h], qkd_ref[RET_HEADS + h]
                p = _dot(qb, kb, NT) * dec_ref[h]
                rp = rst_ref[j, cols, :].astype(BF16)
                dr = dr_scr[cols, :]
                drb = dr.astype(BF16)
                dpb = (_dot(dob, vb, NT) * dec_ref[h]).astype(BF16)
                dqr = _dot(dpb, kb) + _dot(dob, rp, NT) * qd
                dkr = _dot(dpb, qb, TN) + _dot(vb, drb, NT) * kd
                dv = _dot(p.astype(BF16), dob, TN) + _dot((kr * kd).astype(BF16), drb)
                dr_scr[cols, :] = gammas[h] * dr + _dot((qr * qd).astype(BF16), dob, TN)
                dproj_ref[rows, OFF_RQ + h * RET_D:OFF_RQ + (h + 1) * RET_D] = (
                    _rotate_t(dqr, cosv, sinv) * scale_r).astype(BF16)
                dproj_ref[rows, OFF_RK + h * RET_D:OFF_RK + (h + 1) * RET_D] = _rotate_t(dkr, cosv, sinv).astype(BF16)
                dproj_ref[rows, OFF_RV + h * RET_D:OFF_RV + (h + 1) * RET_D] = dv.astype(BF16)
            return carry

        def gla_tile(jj, carry):
            k = gla_k
            tl = tg
            j = tc // tg - 1 - jj
            rows = _tile_rows(j, tg)
            glr = glr_ref[rows, :]
            z, b, bl, ep, em = _gla_gates(glr, gw_ref[...], gb_ref[...], k["ltri"], tl)
            qs = gq_ref[rows, :] * scale_g
            kk = gk_ref[rows, :]
            eb = jnp.exp(b)
            ekb = jnp.exp(bl - b)
            ebl = jnp.exp(bl)
            ql, qu, kl, ku = qs * ep, qs * em, kk * em, kk * ep
            qg, kg = qs * eb, kk * ekb
            qlm = _stack_heads(ql, k["hmask"]).astype(BF16)
            qum = _stack_heads(qu, k["hmask"]).astype(BF16)
            klb, kub = kl.astype(BF16), ku.astype(BF16)
            a_all = jnp.where(k["lower"], _dot(qlm, klb, NT),
                              jnp.where(k["upper"], _dot(qum, kub, NT), 0.0)).astype(BF16)
            st = sst_ref[j]
            stb = st.astype(BF16)
            ds = ds_scr[...]
            dsb = ds.astype(BF16)
            ds_new = ds * ebl
            da_parts = []
            dqg = jnp.zeros((tl, GLA_KW), F32)
            dkg = jnp.zeros((tl, GLA_KW), F32)
            for h in range(GLA_HEADS):
                cols = slice(h * GLA_DV, (h + 1) * GLA_DV)
                hr = slice(h * tl, (h + 1) * tl)
                ocols = slice(RET_HEADS * RET_D + h * GLA_DV, RET_HEADS * RET_D + (h + 1) * GLA_DV)
                o = oraw_ref[rows, ocols]
                g = gg_ref[rows, cols]
                w = gnw_ref[:, cols]
                dout = dmix_ref[rows, ocols]
                inv = lax.rsqrt(jnp.mean(o * o, axis=-1, keepdims=True) + LN_EPS)
                n = o * inv
                sg = _sigmoid(g)
                sil = g * sg
                dn = dout * w * sil
                dvec_ref[1:2, cols] += jnp.sum(dout * n * sil, axis=0, keepdims=True)
                dproj_ref[rows, OFF_GG + h * GLA_DV:OFF_GG + (h + 1) * GLA_DV] = (
                    dout * n * w * (sg * (1.0 + g * (1.0 - sg)))).astype(BF16)
                dob = (inv * (dn - n * jnp.mean(dn * n, axis=-1, keepdims=True))).astype(BF16)
                vb = gv_ref[rows, cols].astype(BF16)
                mh = k["hmask"][h]
                da_parts.append(_dot(dob, vb, NT))
                dv = _dot(a_all[hr, :], dob, TN) + _dot((kg * mh).astype(BF16), dsb, NT)
                dproj_ref[rows, OFF_GV + h * GLA_DV:OFF_GV + (h + 1) * GLA_DV] = dv.astype(BF16)
                dkg = dkg + mh * _dot(vb, dsb)
                dqg = dqg + mh * _dot(dob, stb)
                ds_new = ds_new + _dot(dob, (qg * mh).astype(BF16), TN)
            da_all = jnp.concatenate(da_parts, axis=0)
            dal = jnp.where(k["lower"], da_all, 0.0).astype(BF16)
            dau = jnp.where(k["upper"], da_all, 0.0).astype(BF16)
            dqlm = _dot(dal, klb)
            dqum = _dot(dau, kub)
            dql = jnp.zeros((tl, GLA_KW), F32)
            dqu = jnp.zeros((tl, GLA_KW), F32)
            for h in range(GLA_HEADS):
                hr = slice(h * tl, (h + 1) * tl)
                dql = dql + k["hmask"][h] * dqlm[hr, :]
                dqu = dqu + k["hmask"][h] * dqum[hr, :]
            dkl = _dot(dal, qlm, TN)
            dku = _dot(dau, qum, TN)
            dbl = (jnp.sum(dkg * kg, axis=0, keepdims=True)
                   + jnp.sum(ds * st, axis=0, keepdims=True) * ebl)
            ds_scr[...] = ds_new
            dqs = dql * ep + dqu * em + dqg * eb
            dk = dkl * em + dku * ep + dkg * ekb
            db = dql * ql - dkl * kl - dqu * qu + dku * ku + dqg * qg - dkg * kg
            db = db + jnp.where(last_row, dbl, 0.0)
            dla = _dot_split(k["utri"], db, NN, a_exact=True)
            dz = dla * (1.0 / GATE_TAU) * _sigmoid(-z)
            dvec_ref[2:3, 0:GLA_KW] += jnp.sum(dz, axis=0, keepdims=True)
            dgw_ref[...] += _dot_split(glr, dz, TN)
            dproj_ref[rows, OFF_GLR:OFF_GLR + 128] = _dot(dz.astype(BF16), gw_ref[...].astype(BF16), NT).astype(BF16)
            dproj_ref[rows, OFF_GQ:OFF_GQ + GLA_KW] = (dqs * scale_g).astype(BF16)
            dproj_ref[rows, OFF_GK:OFF_GK + GLA_KW] = dk.astype(BF16)
            return carry

        _for_tiles(tc // tr, ret_tile)
        _for_tiles(tc // tg, gla_tile)

        @pl.when(step == nsteps - 1)
        def _():
            _DirectExchange(send_refs, recv_refs, xs, xr, xl).finish()

    rev = lambda i: (nsteps - 1 - i, 0)

    def col(width, off):
        return pl.BlockSpec((tc, width), lambda i, o=off // width: (nsteps - 1 - i, o))

    fix = lambda i: (0, 0)
    fix3 = lambda i: (0, 0, 0)
    hbm = pl.BlockSpec(memory_space=pl.ANY)
    dec, qkd, rot_in, rot_tile = tables
    half = pl.BlockSpec((tc, RET_HEADS * RET_D), rev)
    in_specs = [pl.BlockSpec((tc, D_MODEL), rev), half, half, col(512, OFF_RV), col(512, OFF_RG),
                col(256, OFF_GQ), col(256, OFF_GK), col(512, OFF_GV), col(512, OFF_GG), col(128, OFF_GLR),
                pl.BlockSpec((tc, D_MODEL), rev),
                pl.BlockSpec(dec.shape, fix3), pl.BlockSpec(qkd.shape, fix3), pl.BlockSpec(rot_in.shape, fix3),
                pl.BlockSpec((1, 8, 2 * RET_D), lambda i: (nsteps - 1 - i, 0, 0)),
                pl.BlockSpec((tc // tr, RET_HEADS * RET_D, RET_D), lambda i: (nsteps - 1 - i, 0, 0)),
                pl.BlockSpec((tc // tg, GLA_DV, GLA_KW), lambda i: (nsteps - 1 - i, 0, 0)),
                pl.BlockSpec((128, GLA_KW), fix), pl.BlockSpec((1, GLA_KW), fix),
                pl.BlockSpec((1, 512), fix), pl.BlockSpec((1, 512), fix)] + [hbm] * ns
    out_specs = (pl.BlockSpec((tc, D_IN_PAD), rev), pl.BlockSpec((128, GLA_KW), fix),
                 pl.BlockSpec((8, 512), fix)) + tuple([hbm] * ns)
    out_shape = (jax.ShapeDtypeStruct((t, D_IN_PAD), BF16), jax.ShapeDtypeStruct((128, GLA_KW), F32),
                 jax.ShapeDtypeStruct((8, 512), F32)) + tuple(jax.ShapeDtypeStruct(s.shape, s.dtype) for s in slabs)
    return pl.pallas_call(
        body, name="mixer_bwd", grid=(nsteps,), in_specs=in_specs, out_specs=out_specs, out_shape=out_shape,
        scratch_shapes=[pltpu.VMEM((RET_HEADS * RET_D, RET_D), F32), pltpu.VMEM((GLA_DV, GLA_KW), F32),
                        pltpu.SemaphoreType.DMA((7 * ns,)), pltpu.SemaphoreType.DMA((7 * ns,)),
                        pltpu.SemaphoreType.DMA((ns,))],
        compiler_params=pltpu.CompilerParams(dimension_semantics=("arbitrary",), vmem_limit_bytes=V7X_VMEM_LIMIT),
    )(dmix, qrb, krb, *([proj] * 7), oraw, dec, qkd, rot_in, rot_tile, rst, sst, gw_pad, gb, rnw, gnw, *slabs)


def _inproj_bwd(dproj, x2d, dxa, sc1p, w_in_t, slab):
    t = x2d.shape[0]
    tm = min(PROJ_TILE, t)
    nsteps = t // tm

    def body(dp_ref, x_ref, dxa_ref, sc_ref, w_hbm, send_ref, gx_ref, acc_ref, recv_ref, w_vmem, sem, xs, xr, xl):
        step = pl.program_id(0)
        first = step == 0
        _load_w_in_t(first, w_hbm, w_vmem, sem)

        @pl.when(first)
        def _():
            acc_ref[...] = jnp.zeros_like(acc_ref)
            _DirectExchange([send_ref], [recv_ref], xs, xr, xl).start()

        du = _dot(dp_ref[...], w_vmem[...])
        xh, rstd = _ln_stats(x_ref[...])
        gx_ref[...] = dxa_ref[...] + _ln_bwd(du * sc_ref[...], xh, rstd)
        acc_ref[0:1, :] += jnp.sum(du * xh, axis=0, keepdims=True)
        acc_ref[1:2, :] += jnp.sum(du, axis=0, keepdims=True)

        @pl.when(step == nsteps - 1)
        def _():
            _DirectExchange([send_ref], [recv_ref], xs, xr, xl).finish()

    row = lambda i: (i, 0)
    fix = lambda i: (0, 0)
    hbm = pl.BlockSpec(memory_space=pl.ANY)
    return pl.pallas_call(
        body, name="inproj_bwd", grid=(nsteps,),
        in_specs=[pl.BlockSpec((tm, D_IN_PAD), row), pl.BlockSpec((tm, D_MODEL), row), pl.BlockSpec((tm, D_MODEL), row),
                  pl.BlockSpec((1, D_MODEL), fix), hbm, hbm],
        out_specs=(pl.BlockSpec((tm, D_MODEL), row), pl.BlockSpec((8, D_MODEL), fix), hbm),
        out_shape=(jax.ShapeDtypeStruct((t, D_MODEL), F32), jax.ShapeDtypeStruct((8, D_MODEL), F32),
                   jax.ShapeDtypeStruct(slab.shape, slab.dtype)),
        scratch_shapes=[pltpu.VMEM((D_IN_PAD, D_MODEL), BF16), pltpu.SemaphoreType.DMA((1,)),
                        pltpu.SemaphoreType.DMA((7,)), pltpu.SemaphoreType.DMA((7,)), pltpu.SemaphoreType.DMA((1,))],
        compiler_params=pltpu.CompilerParams(dimension_semantics=("arbitrary",), vmem_limit_bytes=V7X_VMEM_LIMIT),
    )(dproj, x2d, dxa, sc1p, w_in_t, slab)


def _adam_math(w, g, m, v):
    m = ADAM_B1 * m + (1.0 - ADAM_B1) * g
    v = ADAM_B2 * v + (1.0 - ADAM_B2) * (g * g)
    m_hat = m / (1.0 - ADAM_B1 ** ADAM_STEP)
    v_hat = v / (1.0 - ADAM_B2 ** ADAM_STEP)
    delta = -ADAM_LR * (m_hat / (jnp.sqrt(v_hat) + ADAM_EPS) + ADAM_WD * w)
    return delta, m, v


def _adamw(w, gparts, m, v, name):
    nparts, rows, cols = gparts.shape
    tr = rows
    for cand in (512, 256, 128, 64, 32, 16, 8):
        if rows % cand == 0:
            tr = cand
            break

    def body(w_ref, g_ref, m_ref, v_ref, go_ref, d_ref, mo_ref, vo_ref):
        g = g_ref[0].astype(F32)
        for p in range(1, nparts):
            g = g + g_ref[p].astype(F32)
        delta, mn, vn = _adam_math(w_ref[...], g, m_ref[...], v_ref[...])
        go_ref[...] = g
        d_ref[...] = delta
        mo_ref[...] = mn
        vo_ref[...] = vn

    blk = pl.BlockSpec((tr, cols), lambda i: (i, 0))
    shp = jax.ShapeDtypeStruct((rows, cols), F32)
    return pl.pallas_call(
        body, name=name, grid=(rows // tr,),
        in_specs=[blk, pl.BlockSpec((nparts, tr, cols), lambda i: (0, i, 0)), blk, blk],
        out_specs=(blk, blk, blk, blk), out_shape=(shp, shp, shp, shp),
        compiler_params=pltpu.CompilerParams(dimension_semantics=("arbitrary",), vmem_limit_bytes=V7X_VMEM_LIMIT),
    )(w, gparts, m, v)


def _small_reduce(gathered, gathered_gw, c_all, dmod_cols):
    def body(g_ref, gw_ref, c_ref, dm_ref, sum_ref, gwsum_ref, gb_ref, gwa_ref):
        s = g_ref[0]
        sw = gw_ref[0]
        for p in range(1, N_DEV):
            s = s + g_ref[p]
            sw = sw + gw_ref[p]
        sum_ref[...] = s
        gwsum_ref[...] = sw
        for i in range(6):
            gb_ref[:, i * D_MODEL:(i + 1) * D_MODEL] = s[i:i + 1, :]
        cc = c_ref[...]
        gwa_ref[...] = _dot(cc * _sigmoid(cc), dm_ref[...], TN, HIGHEST)

    vm = pl.BlockSpec(memory_space=pltpu.VMEM)
    return pl.pallas_call(
        body, name="small_reduce",
        out_shape=(jax.ShapeDtypeStruct(gathered.shape[1:], F32), jax.ShapeDtypeStruct(gathered_gw.shape[1:], F32),
                   jax.ShapeDtypeStruct((1, 6 * D_MODEL), F32), jax.ShapeDtypeStruct((D_MODEL, ADA_COLS), F32)),
        in_specs=[vm] * 4, out_specs=(vm, vm, vm, vm),
        compiler_params=pltpu.CompilerParams(vmem_limit_bytes=V7X_VMEM_LIMIT),
    )(gathered, gathered_gw, c_all, dmod_cols)


SMR_LN1W, SMR_LN1B, SMR_LN2W, SMR_LN2B, SMR_NORMS, SMR_MISC = 6, 7, 8, 9, 10, 11


def _adamw_small(gsum, g_b_ada, g_ggw, params, moms, vels):
    n = len(params)

    def body(*refs):
        gsum_ref, gb_ref, gw_ref = refs[:3]
        w_refs, m_refs, v_refs = refs[3:3 + n], refs[3 + n:3 + 2 * n], refs[3 + 2 * n:3 + 3 * n]
        outs = refs[3 + 3 * n:]
        g_refs, d_refs, mo_refs, vo_refs = outs[:n - 1], outs[n - 1:2 * n - 1], outs[2 * n - 1:3 * n - 1], outs[3 * n - 1:]
        grads = [gb_ref[...],
                 gsum_ref[SMR_NORMS:SMR_NORMS + 1, 0:512],
                 gsum_ref[SMR_MISC:SMR_MISC + 1, 0:GLA_KW],
                 gsum_ref[SMR_NORMS:SMR_NORMS + 1, 512:1024],
                 gsum_ref[SMR_LN1W:SMR_LN1W + 1, :], gsum_ref[SMR_LN1B:SMR_LN1B + 1, :],
                 gsum_ref[SMR_LN2W:SMR_LN2W + 1, :], gsum_ref[SMR_LN2B:SMR_LN2B + 1, :],
                 gw_ref[...]]
        for i in range(n):
            delta, mn, vn = _adam_math(w_refs[i][...], grads[i], m_refs[i][...], v_refs[i][...])
            if i < n - 1:
                g_refs[i][...] = grads[i]
            d_refs[i][...] = delta
            mo_refs[i][...] = mn
            vo_refs[i][...] = vn

    vm = pl.BlockSpec(memory_space=pltpu.VMEM)
    shapes = [jax.ShapeDtypeStruct(p.shape, F32) for p in params]
    n_in = 3 + 3 * n
    out_shape = tuple(shapes[:n - 1] + shapes * 3)
    return pl.pallas_call(
        body, name="adamw_small", out_shape=out_shape,
        in_specs=[vm] * n_in, out_specs=tuple([vm] * len(out_shape)),
        compiler_params=pltpu.CompilerParams(vmem_limit_bytes=V7X_VMEM_LIMIT),
    )(gsum, g_b_ada, g_ggw, *params, *moms, *vels)


def kernel(x, c, w_ada, b_ada, w_in, ret_norm_w, gla_gate_w, gla_gate_b, gla_norm_w, w_out, ln1_w, ln1_b, w_ff1, w_ff2, ln2_w, ln2_b, loss_target, m_w_ada, m_b_ada, m_w_in, m_ret_norm_w, m_gla_gate_w, m_gla_gate_b, m_gla_norm_w, m_w_out, m_ln1_w, m_ln1_b, m_w_ff1, m_w_ff2, m_ln2_w, m_ln2_b, v_w_ada, v_b_ada, v_w_in, v_ret_norm_w, v_gla_gate_w, v_gla_gate_b, v_gla_norm_w, v_w_out, v_ln1_w, v_ln1_b, v_w_ff1, v_w_ff2, v_ln2_w, v_ln2_b):
    t = x.shape[1]
    xi, yi, ci = _my_coords()
    me = 4 * xi + 2 * yi + ci
    x2d = x[0]
    tgt = loss_target[0]

    c_ext = jnp.concatenate([c, gla_gate_w[0].reshape(1, GATE_RANK * GLA_KW // N_DEV)], axis=1)
    b_l = lax.dynamic_slice(b_ada, (0, me * ADA_COLS), (1, ADA_COLS))
    c_all3, mod_all, wi_g = _adaln_mod(c_ext, w_ada[0], b_l, w_in[0].T.astype(BF16))
    c_all = c_all3[:, 0, :D_MODEL]
    gate_w = c_all3[:, 0, D_MODEL:].reshape(N_DEV, GATE_RANK, GLA_KW // N_DEV)
    gate_w = gate_w.transpose(1, 0, 2).reshape(GATE_RANK, GLA_KW)
    gw_pad = jnp.zeros((128, GLA_KW), F32).at[:GATE_RANK].set(gate_w)
    mod = lax.dynamic_slice(mod_all, (0, me, 0), (N_DEV, 1, ADA_COLS)).reshape(6, D_MODEL)
    shift1, scale1, gate1, shift2, scale2, gate2 = [mod[i:i + 1] for i in range(6)]

    w_in_t = wi_g.reshape(D_IN, D_MODEL)

    tables = _ret_tables(t, min(MIX_TILE, t))

    sc1p = 1.0 + scale1
    proj, u = _inproj_fwd(x2d, sc1p, shift1, w_in_t)
    mixed, oraw, qrb, krb, rst, sst, wo_g, w1_b, w2_g = _mixer_fwd(
        proj, tables, gw_pad, gla_gate_b, ret_norm_w, gla_norm_w,
        [w_out[0].astype(BF16), w_ff1[0].astype(BF16), w_ff2[0].astype(BF16)])
    w_out_b = wo_g.reshape(D_MODEL, D_MODEL)
    w2_b = w2_g.reshape(D_FF, D_MODEL)
    vec_f = jnp.concatenate([gate1, 1.0 + scale2, shift2, gate2, ln1_w, ln1_b, ln2_w, ln2_b], axis=0)
    m, x1n, rstd1, u2, a, df, dh2, acc_f = _mid_fwd(mixed, x2d, tgt, vec_f, w_out_b, w1_b, w2_b)

    vec_b = jnp.concatenate([gate1, 1.0 + scale2, ln1_w, ln1_b, jnp.zeros((4, D_MODEL), F32)], axis=0)
    da, dm, dmix, dxa, acc_b = _ffn_bwd(df, a, dh2, x1n, rstd1, m, vec_b, w_out_b, w1_b, w2_b)
    dw2 = _matmul_tn(a, df, 2048, 1024, 1024, "tn_dw2", relu_sq=True)
    dw1 = _matmul_tn(u2, da, 1024, 2048, 1024, "tn_dw1", col_slab=FF_COLS)
    dwo = _matmul_tn(mixed, dm, 1024, 1024, 1024, "tn_dwout")
    dproj, dgw, dvec, r_wo, r_w1, r_w2 = _mixer_bwd(
        dmix, proj, qrb, krb, oraw, tables, rst, sst, gw_pad, gla_gate_b, ret_norm_w, gla_norm_w,
        [dwo.reshape(N_DEV, OUT_ROWS, D_MODEL), dw1, dw2.reshape(N_DEV, FF_COLS, D_MODEL)])
    dwi = _matmul_tn(u, dproj, 1024, D_IN_PAD, 512, "tn_dwin")
    dwi_s = dwi[:, :D_IN].reshape(D_MODEL, N_DEV, IN_COLS).transpose(1, 0, 2)
    grad_x, acc_i, r_wi = _inproj_bwd(dproj, x2d, dxa, sc1p, w_in_t, dwi_s)

    loss_part = jnp.sum(acc_f[3])
    small = jnp.concatenate([
        acc_i[1:2], acc_i[0:1], acc_b[4:5], acc_b[1:2], acc_b[0:1], acc_f[2:3],
        acc_b[2:3], acc_b[3:4], acc_f[0:1], acc_f[1:2],
        jnp.concatenate([dvec[0:1], dvec[1:2]], axis=1),
        jnp.concatenate([dvec[2:3, :GLA_KW], jnp.full((1, 128), loss_part, F32),
                         jnp.zeros((1, D_MODEL - GLA_KW - 128), F32)], axis=1),
        jnp.zeros((4, D_MODEL), F32)], axis=0)
    small_all, gw_all = _small_gather([small, dgw[:GATE_RANK]])
    dmod_all = small_all[:, :6].reshape(N_DEV, 6 * D_MODEL)
    dmod_cols = lax.dynamic_slice(dmod_all, (0, me * ADA_COLS), (N_DEV, ADA_COLS))
    ssum, gw_sum, g_b_ada, g_w_ada = _small_reduce(small_all, gw_all, c_all, dmod_cols)
    loss = ssum[SMR_MISC, GLA_KW]
    g_ggw = lax.dynamic_slice(gw_sum, (0, me * (GLA_KW // N_DEV)), (GATE_RANK, GLA_KW // N_DEV))[None]

    small_w = [b_ada, ret_norm_w, gla_gate_b, gla_norm_w, ln1_w, ln1_b, ln2_w, ln2_b, gla_gate_w]
    small_m = [m_b_ada, m_ret_norm_w, m_gla_gate_b, m_gla_norm_w, m_ln1_w, m_ln1_b, m_ln2_w, m_ln2_b, m_gla_gate_w]
    small_v = [v_b_ada, v_ret_norm_w, v_gla_gate_b, v_gla_norm_w, v_ln1_w, v_ln1_b, v_ln2_w, v_ln2_b, v_gla_gate_w]
    res = _adamw_small(ssum, g_b_ada, g_ggw, small_w, small_m, small_v)
    small_g = list(res[:8]) + [g_ggw]
    d_small, m_small, v_small = list(res[8:17]), list(res[17:26]), list(res[26:35])

    _, d_w_ada, nm_w_ada, nv_w_ada = _adamw(w_ada[0], g_w_ada[None], m_w_ada[0], v_w_ada[0], "adamw_ada")

    big =[_adamw(w[0], r, m_[0], v_[0], nm) for w, r, m_, v_, nm in (
        (w_in, r_wi, m_w_in, v_w_in, "adamw_in"), (w_out, r_wo, m_w_out, v_w_out, "adamw_out"),
        (w_ff1, r_w1, m_w_ff1, v_w_ff1, "adamw_ff1"), (w_ff2, r_w2, m_w_ff2, v_w_ff2, "adamw_ff2"))]
    g_big, d_big, m_big, v_big = [[b[i][None] for b in big] for i in range(4)]

    def ordered(w_ada_v, small_vals, big_vals):
        b_ada_v, rnw_v, ggb_v, gnw_v, l1w_v, l1b_v, l2w_v, l2b_v, ggw_v = small_vals
        wi_v, wo_v, w1_v, w2_v = big_vals
        return [w_ada_v, b_ada_v, wi_v, rnw_v, ggw_v, ggb_v, gnw_v, wo_v, l1w_v, l1b_v, w1_v, w2_v, l2w_v, l2b_v]

    grads = ordered(g_w_ada[None], small_g, g_big)
    deltas = ordered(d_w_ada[None], d_small, d_big)
    new_m = ordered(nm_w_ada[None], m_small, m_big)
    new_v = ordered(nv_w_ada[None], v_small, v_big)
    return (loss, grad_x[None], *grads, *deltas, *new_m, *new_v)
```

```python
import functools

import numpy as np
import jax
import jax.numpy as jnp
from jax import lax
from jax.experimental import pallas as pl
from jax.experimental.pallas import tpu as pltpu

F32 = jnp.float32
BF16 = jnp.bfloat16
MESH = pl.DeviceIdType.MESH
HIGHEST = lax.Precision.HIGHEST

N_DEV = 8
D_MODEL = 1024
CHUNK = 64
RET_HEADS = 4
RET_D = 128
GLA_HEADS = 4
GLA_DK = 64
GLA_DV = 128
GLA_KW = GLA_HEADS * GLA_DK
GATE_RANK = 16
GATE_TAU = 16.0
D_FF = 4096
LN_EPS = 1e-5
ALPHA = (2.0 * 1) ** 0.25
D_IN = 3600
D_IN_PAD = 3712
ADA_COLS = 6 * D_MODEL // N_DEV
IN_COLS = D_IN // N_DEV
FF_COLS = D_FF // N_DEV
OUT_ROWS = D_MODEL // N_DEV

OFF_RQ, OFF_RK, OFF_RV, OFF_RG = 0, 512, 1024, 1536
OFF_GQ, OFF_GK, OFF_GV, OFF_GG, OFF_GLR = 2048, 2304, 2560, 3072, 3584

ADAM_LR, ADAM_B1, ADAM_B2, ADAM_EPS, ADAM_WD, ADAM_STEP = 0.001, 0.9, 0.999, 1e-08, 0.01, 10

V7X_VMEM_LIMIT = 56 * 1024 * 1024

ROW_TILE = 256
PROJ_TILE = 512
MIX_TILE = 256
GLA_SUB = 128


def _log_gamma(h):
    return float(np.log(np.float32(1.0) - np.float32(2.0) ** np.float32(-5.0 - h)))


def _my_coords():
    return lax.axis_index("x"), lax.axis_index("y"), lax.axis_index("c")


def _flip(v, bit):
    return 1 - v if bit else v


def _peer(k):
    x, y, c = _my_coords()
    px, py, pc = _flip(x, (k >> 2) & 1), _flip(y, (k >> 1) & 1), _flip(c, k & 1)
    return (px, py, pc), 4 * px + 2 * py + pc


def _dot(a, b, dims=(((1,), (0,)), ((), ())), precision=None):
    return lax.dot_general(a, b, dims, precision=precision, preferred_element_type=F32)


NN = (((1,), (0,)), ((), ()))
NT = (((1,), (1,)), ((), ()))
TN = (((0,), (0,)), ((), ()))


def _split_bf16(v, parts):
    out = []
    for _ in range(parts):
        p = v.astype(BF16)
        out.append(p)
        v = v - p.astype(F32)
    return out


def _dot_split(a, b, dims, a_exact=False):
    if a_exact:
        ab = a.astype(BF16)
        return sum(_dot(ab, p, dims) for p in _split_bf16(b, 3))
    a_hi, a_lo = _split_bf16(a, 2)
    b_hi, b_lo = _split_bf16(b, 2)
    return _dot(a_hi, b_hi, dims) + _dot(a_hi, b_lo, dims) + _dot(a_lo, b_hi, dims)


def _sigmoid(x):
    return 1.0 / (1.0 + jnp.exp(-x))


def _ln_stats(x):
    mu = jnp.mean(x, axis=-1, keepdims=True)
    xc = x - mu
    var = jnp.mean(xc * xc, axis=-1, keepdims=True)
    rstd = lax.rsqrt(var + LN_EPS)
    return xc * rstd, rstd


def _ln_bwd(dyh, xh, rstd):
    return rstd * (dyh - jnp.mean(dyh, axis=-1, keepdims=True) - xh * jnp.mean(dyh * xh, axis=-1, keepdims=True))


def _adaln_mod(c_ext, w_ada_l, b_l, w_in_l):
    width = c_ext.shape[1]

    def body(c_ref, w_ref, b_ref, wi_ref, call_ref, mod_ref, wig_ref, s1, r1, s2, r2, gs, gr, gl):
        gather = _TwoLevelGather([wi_ref], [wig_ref], gs, gr, gl)
        gather.start()
        x, y, c = _my_coords()
        me = 4 * x + 2 * y + c
        call_ref[me] = c_ref[...]
        sends = []
        for k in range(1, N_DEV):
            peer, _ = _peer(k)
            cp = pltpu.make_async_remote_copy(c_ref, call_ref.at[me], s1.at[k - 1], r1.at[k - 1],
                                              device_id=peer, device_id_type=MESH)
            cp.start()
            sends.append(cp)
        for k in range(1, N_DEV):
            peer, pid = _peer(k)
            pltpu.make_async_remote_copy(c_ref, call_ref.at[pid], s1.at[k - 1], r1.at[k - 1],
                                         device_id=peer, device_id_type=MESH).wait_recv()
        for cp in sends:
            cp.wait_send()
        row = lax.broadcasted_iota(jnp.int32, (N_DEV, D_MODEL), 0)
        call = jnp.zeros((N_DEV, D_MODEL), F32)
        for j in range(N_DEV):
            call = jnp.where(row == j, jnp.broadcast_to(call_ref[j][:, :D_MODEL], (N_DEV, D_MODEL)), call)
        sc = call * _sigmoid(call)
        mod = _dot(sc, w_ref[...], NN, HIGHEST) + b_ref[...]
        mod_ref[me] = mod
        sends = []
        for k in range(1, N_DEV):
            peer, _ = _peer(k)
            cp = pltpu.make_async_remote_copy(mod_ref.at[me], mod_ref.at[me], s2.at[k - 1], r2.at[k - 1],
                                              device_id=peer, device_id_type=MESH)
            cp.start()
            sends.append(cp)
        for k in range(1, N_DEV):
            peer, pid = _peer(k)
            pltpu.make_async_remote_copy(mod_ref.at[pid], mod_ref.at[pid], s2.at[k - 1], r2.at[k - 1],
                                         device_id=peer, device_id_type=MESH).wait_recv()
        for cp in sends:
            cp.wait_send()
        gather.forward()
        gather.finish()

    vm = pl.BlockSpec(memory_space=pltpu.VMEM)
    hbm = pl.BlockSpec(memory_space=pl.ANY)
    return pl.pallas_call(
        body, name="adaln_mod",
        out_shape=(jax.ShapeDtypeStruct((N_DEV, 1, width), F32),
                   jax.ShapeDtypeStruct((N_DEV, N_DEV, ADA_COLS), F32),
                   jax.ShapeDtypeStruct((N_DEV, *w_in_l.shape), w_in_l.dtype)),
        in_specs=[vm, vm, vm, hbm], out_specs=(vm, vm, hbm),
        scratch_shapes=[pltpu.SemaphoreType.DMA((N_DEV - 1,))] * 4
        + [pltpu.SemaphoreType.DMA((7,)), pltpu.SemaphoreType.DMA((7,)), pltpu.SemaphoreType.DMA((1,))],
        compiler_params=pltpu.CompilerParams(vmem_limit_bytes=V7X_VMEM_LIMIT),
    )(c_ext, w_ada_l, b_l, w_in_l)


class _TwoLevelGather:
    def __init__(self, x_refs, out_refs, send_sems, recv_sems, local_sems):
        self.x_refs, self.out_refs = x_refs, out_refs
        self.send_sems, self.recv_sems, self.local_sems = send_sems, recv_sems, local_sems
        x, y, c = _my_coords()
        self.c = c
        self.me, self.sibling = (x, y, c), (x, y, 1 - c)
        self.chips = [(1 - x, y), (x, 1 - y), (1 - x, 1 - y)]

    def _copy(self, a, k, block, to, src=None):
        px, py, pc = block
        slab = self.out_refs[a].at[4 * px + 2 * py + pc]
        return pltpu.make_async_remote_copy(
            src_ref=slab if src is None else src, dst_ref=slab,
            send_sem=self.send_sems.at[7 * a + k], recv_sem=self.recv_sems.at[7 * a + k],
            device_id=to, device_id_type=MESH)

    def _mine(self, a):
        px, py, pc = self.me
        return pltpu.make_async_copy(self.x_refs[a], self.out_refs[a].at[4 * px + 2 * py + pc], self.local_sems.at[a])

    def _first(self, a):
        cps = [self._copy(a, 0, self.me, self.sibling, src=self.x_refs[a])]
        cps += [self._copy(a, 1 + j, self.me, (*chip, self.c), src=self.x_refs[a]) for j, chip in enumerate(self.chips)]
        return cps

    def _passed(self, a):
        return [self._copy(a, 4 + j, (*chip, self.c), self.sibling) for j, chip in enumerate(self.chips)]

    def start(self):
        for a in range(len(self.x_refs)):
            self._mine(a).start()
            for cp in self._first(a):
                cp.start()

    def forward(self):
        for a in range(len(self.x_refs)):
            passed = self._passed(a)
            for j, chip in enumerate(self.chips):
                self._copy(a, 1 + j, (*chip, self.c), self.me).wait_recv()
                passed[j].start()

    def finish(self):
        for a in range(len(self.x_refs)):
            self._copy(a, 0, self.sibling, self.me).wait_recv()
            for j, chip in enumerate(self.chips):
                self._copy(a, 4 + j, (*chip, 1 - self.c), self.me).wait_recv()
            for cp in self._first(a) + self._passed(a):
                cp.wait_send()
            self._mine(a).wait()


def _exchange_copy(src_refs, land_refs, send_sems, recv_sems, a, k, gather, receiving):
    x, y, c = _my_coords()
    me = 4 * x + 2 * y + c
    peer, pid = _peer(k)
    src = src_refs[a] if gather else src_refs[a].at[pid]
    dst = land_refs[a].at[pid if receiving else me]
    return pltpu.make_async_remote_copy(src, dst, send_sems.at[7 * a + k - 1], recv_sems.at[7 * a + k - 1],
                                        device_id=peer, device_id_type=MESH)


def _exchange_start(srcs, gather, name):
    n = len(srcs)
    xi, yi, ci = _my_coords()
    me = 4 * xi + 2 * yi + ci
    lands = []
    for s in srcs:
        own = s[None] if gather else lax.dynamic_slice_in_dim(s, me, 1, axis=0)
        lands.append(lax.dynamic_update_slice_in_dim(lax.empty((N_DEV, *own.shape[1:]), s.dtype), own, me, axis=0))

    def body(*refs):
        src_refs, land_refs, send_sems, recv_sems, token = refs[:n], refs[n:2 * n], refs[2 * n], refs[2 * n + 1], refs[-1]
        for a in range(n):
            for k in range(1, N_DEV):
                _exchange_copy(src_refs, land_refs, send_sems, recv_sems, a, k, gather, receiving=False).start()
        token[...] = jnp.zeros_like(token)

    hbm = pl.BlockSpec(memory_space=pltpu.HBM)
    sem = pl.BlockSpec(memory_space=pltpu.SEMAPHORE)
    res = pl.pallas_call(
        body, name=name,
        out_shape=(pltpu.SemaphoreType.DMA((7 * n,)), pltpu.SemaphoreType.DMA((7 * n,)),
                   *[pltpu.HBM(v.shape, v.dtype) for v in srcs + lands], jax.ShapeDtypeStruct((8, 128), F32)),
        in_specs=[hbm] * (2 * n),
        out_specs=(sem, sem, *([hbm] * (2 * n)), pl.BlockSpec(memory_space=pltpu.VMEM)),
        input_output_aliases={i: 2 + i for i in range(2 * n)},
        compiler_params=pltpu.CompilerParams(has_side_effects=pltpu.SideEffectType.DATAFLOW_SIDE_EFFECTING),
    )(*[pltpu.with_memory_space_constraint(v, pltpu.HBM) for v in srcs + lands])
    return res[0], res[1], list(res[2:2 + n]), list(res[2 + n:2 + 2 * n]), res[-1]


def _exchange_wait(send_sems, recv_sems, srcs, lands, after, gather, name):
    n = len(srcs)

    def body(*refs):
        src_refs, land_refs, s_sems, r_sems = refs[:n], refs[n:2 * n], refs[2 * n], refs[2 * n + 1]
        for a in range(n):
            for k in range(1, N_DEV):
                _exchange_copy(src_refs, land_refs, s_sems, r_sems, a, k, gather, receiving=False).wait_send()
                _exchange_copy(src_refs, land_refs, s_sems, r_sems, a, k, gather, receiving=True).wait_recv()

    hbm = pl.BlockSpec(memory_space=pltpu.HBM)
    sem = pl.BlockSpec(memory_space=pltpu.SEMAPHORE)
    res = pl.pallas_call(
        body, name=name,
        out_shape=tuple(pltpu.HBM(v.shape, v.dtype) for v in srcs + lands),
        in_specs=[hbm] * (2 * n) + [sem, sem, pl.BlockSpec(memory_space=pl.ANY)],
        out_specs=tuple([hbm] * (2 * n)),
        input_output_aliases={i: i for i in range(2 * n)},
        compiler_params=pltpu.CompilerParams(has_side_effects=pltpu.SideEffectType.DATAFLOW_SIDE_EFFECTING),
    )(*srcs, *lands, send_sems, recv_sems, after)
    return list(res[n:])


def _small_gather(vecs):
    n = len(vecs)

    def body(*refs):
        v_refs, out_refs, s_sems, r_sems = refs[:n], refs[n:2 * n], refs[2 * n], refs[2 * n + 1]
        x, y, c = _my_coords()
        me = 4 * x + 2 * y + c
        sends = []
        for a in range(n):
            out_refs[a][me] = v_refs[a][...]
            for k in range(1, N_DEV):
                peer, _ = _peer(k)
                cp = pltpu.make_async_remote_copy(v_refs[a], out_refs[a].at[me], s_sems.at[7 * a + k - 1],
                                                  r_sems.at[7 * a + k - 1], device_id=peer, device_id_type=MESH)
                cp.start()
                sends.append(cp)
        for a in range(n):
            for k in range(1, N_DEV):
                peer, pid = _peer(k)
                pltpu.make_async_remote_copy(v_refs[a], out_refs[a].at[pid], s_sems.at[7 * a + k - 1],
                                             r_sems.at[7 * a + k - 1], device_id=peer, device_id_type=MESH).wait_recv()
        for cp in sends:
            cp.wait_send()

    vm = pl.BlockSpec(memory_space=pltpu.VMEM)
    return pl.pallas_call(
        body, name="small_gather",
        out_shape=tuple(jax.ShapeDtypeStruct((N_DEV, *v.shape), v.dtype) for v in vecs),
        in_specs=[vm] * n, out_specs=tuple([vm] * n),
        scratch_shapes=[pltpu.SemaphoreType.DMA((7 * n,))] * 2,
    )(*vecs)


def _load_resident(step_is_first, pairs, sem):
    @pl.when(step_is_first)
    def _():
        copies = [pltpu.make_async_copy(src, dst, sem.at[i]) for i, (src, dst) in enumerate(pairs)]
        for cp in copies:
            cp.start()
        for cp in copies:
            cp.wait()


def _load_w_in_t(step_is_first, w_hbm, w_vmem, sem):
    @pl.when(step_is_first)
    def _():
        w_vmem[D_IN:, :] = jnp.zeros((D_IN_PAD - D_IN, D_MODEL), BF16)
    _load_resident(step_is_first, [(w_hbm, w_vmem.at[pl.ds(0, D_IN)])], sem)


def _inproj_fwd(x2d, sc1p, sh1, w_in_t):
    t = x2d.shape[0]
    tm = min(PROJ_TILE, t)

    def body(x_ref, sc_ref, sh_ref, w_hbm, proj_ref, u_ref, w_vmem, sem):
        _load_w_in_t(pl.program_id(0) == 0, w_hbm, w_vmem, sem)
        xh, _ = _ln_stats(x_ref[...])
        ub = (xh * sc_ref[...] + sh_ref[...]).astype(BF16)
        u_ref[...] = ub
        proj_ref[...] = _dot(ub, w_vmem[...], NT)

    row = lambda i: (i, 0)
    fix = lambda i: (0, 0)
    return pl.pallas_call(
        body, name="inproj_fwd", grid=(t // tm,),
        in_specs=[pl.BlockSpec((tm, D_MODEL), row), pl.BlockSpec((1, D_MODEL), fix), pl.BlockSpec((1, D_MODEL), fix),
                  pl.BlockSpec(memory_space=pl.ANY)],
        out_specs=(pl.BlockSpec((tm, D_IN_PAD), row), pl.BlockSpec((tm, D_MODEL), row)),
        out_shape=(jax.ShapeDtypeStruct((t, D_IN_PAD), F32), jax.ShapeDtypeStruct((t, D_MODEL), BF16)),
        scratch_shapes=[pltpu.VMEM((D_IN_PAD, D_MODEL), BF16), pltpu.SemaphoreType.DMA((1,))],
        compiler_params=pltpu.CompilerParams(dimension_semantics=("arbitrary",), vmem_limit_bytes=V7X_VMEM_LIMIT),
    )(x2d, sc1p, sh1, w_in_t)


CHUNK_SHIFT = 6


def _ret_tables(t, tl):
    r = lax.broadcasted_iota(jnp.int32, (tl, tl), 0)
    c = lax.broadcasted_iota(jnp.int32, (tl, tl), 1)
    allowed = jnp.right_shift(c, CHUNK_SHIFT) <= jnp.right_shift(r, CHUNK_SHIFT)
    dist = jnp.abs(r - c).astype(F32)
    rowf = lax.broadcasted_iota(jnp.int32, (tl, RET_D), 0).astype(F32)
    lgs = [_log_gamma(h) for h in range(RET_HEADS)]
    dec = jnp.stack([jnp.where(allowed, jnp.exp(lg * dist), 0.0) for lg in lgs])
    qkd = jnp.stack([jnp.exp(lg * (rowf + 1.0)) for lg in lgs] + [jnp.exp(lg * (tl - 1.0 - rowf)) for lg in lgs])
    inv = 1.0 / (10000.0 ** jnp.linspace(0.0, 1.0, RET_D // 2, dtype=F32))
    off = jnp.arange(tl, dtype=F32)[:, None] * inv[None, :]
    start = (jnp.arange(t // tl, dtype=F32) * tl)[:, None] * inv[None, :]
    co, so = jnp.cos(off), jnp.sin(off)
    rot_in = jnp.stack([jnp.concatenate([co, co], 1), jnp.concatenate([so, so], 1),
                        jnp.concatenate([-co, co], 1), jnp.concatenate([-so, so], 1)])
    cs, ss = jnp.cos(start), jnp.sin(start)
    rot_tile = jnp.concatenate([cs, cs, ss, ss], axis=1)
    rot_tile = jnp.broadcast_to(rot_tile[:, None, :], (t // tl, 8, 2 * RET_D))
    return dec, qkd, rot_in, rot_tile


def _tile_gammas(tl):
    return [float(np.exp(np.float32(_log_gamma(h)) * np.float32(tl))) for h in range(RET_HEADS)]


def _tile_rotary(rot_in_ref, rot_tile_ref):
    ca, sa = rot_tile_ref[0, 0:1, 0:RET_D], rot_tile_ref[0, 0:1, RET_D:2 * RET_D]
    cosv = ca * rot_in_ref[0] - sa * rot_in_ref[1]
    sinv = sa * rot_in_ref[2] + ca * rot_in_ref[3]
    return cosv, sinv


def _gla_consts(tl):
    r = lax.broadcasted_iota(jnp.int32, (tl, tl), 0)
    c = lax.broadcasted_iota(jnp.int32, (tl, tl), 1)
    ltri = (c <= r).astype(F32)
    utri = (c >= r).astype(F32)
    lane = lax.broadcasted_iota(jnp.int32, (1, GLA_KW), 1)
    hmask = [((lane >= h * GLA_DK) & (lane < (h + 1) * GLA_DK)).astype(F32) for h in range(GLA_HEADS)]
    rs = lax.broadcasted_iota(jnp.int32, (GLA_HEADS * tl, tl), 0) & (tl - 1)
    cs = lax.broadcasted_iota(jnp.int32, (GLA_HEADS * tl, tl), 1)
    lower = cs <= rs
    same = jnp.right_shift(cs, CHUNK_SHIFT) == jnp.right_shift(rs, CHUNK_SHIFT)
    upper = jnp.logical_and(jnp.logical_not(lower), same)
    return dict(ltri=ltri, utri=utri, hmask=hmask, lower=lower, upper=upper)


def _tile_rows(j, tl):
    return pl.ds(j * tl, tl) if isinstance(j, int) else pl.ds(pl.multiple_of(j * tl, tl), tl)


def _for_tiles(cps, fn):
    if cps == 1:
        fn(0, 0)
    else:
        lax.fori_loop(0, cps, fn, 0)


def _rotate(v, cosv, sinv):
    return v * cosv + pltpu.roll(v, RET_D // 2, 1) * sinv


def _rotate_t(d, cosv, sinv):
    return d * cosv + pltpu.roll(d * sinv, RET_D // 2, 1)


def _stack_heads(v, hmask):
    return jnp.concatenate([v * hmask[h] for h in range(GLA_HEADS)], axis=0)


def _gla_gates(glr, gw, gb, ltri, tl):
    z = _dot_split(glr, gw, NN) + gb
    la = (jnp.minimum(z, 0.0) - jnp.log(1.0 + jnp.exp(-jnp.abs(z)))) * (1.0 / GATE_TAU)
    b = _dot_split(ltri, la, NN, a_exact=True)
    level = b[tl // 2 - 1:tl // 2, :]
    ep = jnp.exp(jnp.clip(b - level, -80.0, 80.0))
    em = jnp.exp(jnp.clip(level - b, -80.0, 80.0))
    bl = b[tl - 1:tl, :]
    return z, b, bl, ep, em


def _mixer_fwd(proj, tables, gw_pad, gb, rnw, gnw):
    t = proj.shape[0]
    tc = min(MIX_TILE, t)
    tr, tg = tc, min(GLA_SUB, tc)
    nsteps = t // tc
    scale_r = RET_D ** -0.5
    scale_g = GLA_DK ** -0.5
    gammas = _tile_gammas(tr)

    def body(rq_ref, rk_ref, rv_ref, rg_ref, gq_ref, gk_ref, gv_ref, gg_ref, glr_ref,
             dec_ref, qkd_ref, rot_in_ref, rot_tile_ref, gw_ref, gb_ref, rnw_ref, gnw_ref,
             mix_ref, oraw_ref, qrb_ref, krb_ref, rst_ref, sst_ref, r_scr, s_scr):
        @pl.when(pl.program_id(0) == 0)
        def _():
            r_scr[...] = jnp.zeros_like(r_scr)
            s_scr[...] = jnp.zeros_like(s_scr)

        gla_k = _gla_consts(tg)

        def ret_tile(j, carry):
            rows = _tile_rows(j, tr)
            cosv, sinv = _tile_rotary(rot_in_ref, rot_tile_ref)
            for h in range(RET_HEADS):
                cols = slice(h * RET_D, (h + 1) * RET_D)
                qr = _rotate(rq_ref[rows, cols], cosv, sinv) * scale_r
                kr = _rotate(rk_ref[rows, cols], cosv, sinv)
                vb = rv_ref[rows, cols].astype(BF16)
                qb, kb = qr.astype(BF16), kr.astype(BF16)
                qrb_ref[rows, cols] = qb
                krb_ref[rows, cols] = kb
                p = _dot(qb, kb, NT) * dec_ref[h]
                rp = r_scr[cols, :]
                o = _dot(p.astype(BF16), vb) + _dot((qr * qkd_ref[h]).astype(BF16), rp.astype(BF16))
                rst_ref[j, cols, :] = rp
                r_scr[cols, :] = gammas[h] * rp + _dot((kr * qkd_ref[RET_HEADS + h]).astype(BF16), vb, TN)
                oraw_ref[rows, cols] = o
                oc = o - jnp.mean(o, axis=-1, keepdims=True)
                n = oc * lax.rsqrt(jnp.mean(oc * oc, axis=-1, keepdims=True) + LN_EPS)
                g = rg_ref[rows, cols]
                mix_ref[rows, cols] = (n * rnw_ref[:, cols] * (g * _sigmoid(g))).astype(BF16)
            return carry

        def gla_tile(j, carry):
            k = gla_k
            tl = tg
            rows = _tile_rows(j, tg)
            _, b, bl, ep, em = _gla_gates(glr_ref[rows, :], gw_ref[...], gb_ref[...], k["ltri"], tl)
            qs = gq_ref[rows, :] * scale_g
            kk = gk_ref[rows, :]
            x_all = _dot(_stack_heads(qs * ep, k["hmask"]).astype(BF16), (kk * em).astype(BF16), NT)
            y_all = _dot(_stack_heads(qs * em, k["hmask"]).astype(BF16), (kk * ep).astype(BF16), NT)
            a_all = jnp.where(k["lower"], x_all, jnp.where(k["upper"], y_all, 0.0)).astype(BF16)
            st = s_scr[...]
            oq = _dot(_stack_heads(qs * jnp.exp(b), k["hmask"]).astype(BF16), st.astype(BF16), NT)
            kg = kk * jnp.exp(bl - b)
            sst_ref[j] = st
            st_new = st * jnp.exp(bl)
            for h in range(GLA_HEADS):
                cols = slice(h * GLA_DV, (h + 1) * GLA_DV)
                hr = slice(h * tl, (h + 1) * tl)
                vb = gv_ref[rows, cols].astype(BF16)
                o = _dot(a_all[hr, :], vb) + oq[hr, :]
                st_new = st_new + _dot(vb, (kg * k["hmask"][h]).astype(BF16), TN)
                ocols = slice(RET_HEADS * RET_D + h * GLA_DV, RET_HEADS * RET_D + (h + 1) * GLA_DV)
                oraw_ref[rows, ocols] = o
                n = o * lax.rsqrt(jnp.mean(o * o, axis=-1, keepdims=True) + LN_EPS)
                g = gg_ref[rows, cols]
                mix_ref[rows, ocols] = (n * gnw_ref[:, cols] * (g * _sigmoid(g))).astype(BF16)
            s_scr[...] = st_new
            return carry

        _for_tiles(tc // tr, ret_tile)
        _for_tiles(tc // tg, gla_tile)

    def col(width, off):
        return pl.BlockSpec((tc, width), lambda i, o=off // width: (i, o))

    fix = lambda i: (0, 0)
    fix3 = lambda i: (0, 0, 0)
    dec, qkd, rot_in, rot_tile = tables
    in_specs = [col(512, OFF_RQ), col(512, OFF_RK), col(512, OFF_RV), col(512, OFF_RG),
                col(256, OFF_GQ), col(256, OFF_GK), col(512, OFF_GV), col(512, OFF_GG), col(128, OFF_GLR),
                pl.BlockSpec(dec.shape, fix3), pl.BlockSpec(qkd.shape, fix3), pl.BlockSpec(rot_in.shape, fix3),
                pl.BlockSpec((1, 8, 2 * RET_D), lambda i: (i, 0, 0)),
                pl.BlockSpec((128, GLA_KW), fix), pl.BlockSpec((1, GLA_KW), fix),
                pl.BlockSpec((1, 512), fix), pl.BlockSpec((1, 512), fix)]
    half = pl.BlockSpec((tc, RET_HEADS * RET_D), lambda i: (i, 0))
    out_specs = (pl.BlockSpec((tc, D_MODEL), lambda i: (i, 0)), pl.BlockSpec((tc, D_MODEL), lambda i: (i, 0)),
                 half, half,
                 pl.BlockSpec((tc // tr, RET_HEADS * RET_D, RET_D), lambda i: (i, 0, 0)),
                 pl.BlockSpec((tc // tg, GLA_DV, GLA_KW), lambda i: (i, 0, 0)))
    out_shape = (jax.ShapeDtypeStruct((t, D_MODEL), BF16), jax.ShapeDtypeStruct((t, D_MODEL), F32),
                 jax.ShapeDtypeStruct((t, RET_HEADS * RET_D), BF16), jax.ShapeDtypeStruct((t, RET_HEADS * RET_D), BF16),
                 jax.ShapeDtypeStruct((t // tr, RET_HEADS * RET_D, RET_D), F32),
                 jax.ShapeDtypeStruct((t // tg, GLA_DV, GLA_KW), F32))
    return pl.pallas_call(
        body, name="mixer_fwd", grid=(nsteps,), in_specs=in_specs, out_specs=out_specs, out_shape=out_shape,
        scratch_shapes=[pltpu.VMEM((RET_HEADS * RET_D, RET_D), F32), pltpu.VMEM((GLA_DV, GLA_KW), F32)],
        compiler_params=pltpu.CompilerParams(dimension_semantics=("arbitrary",), vmem_limit_bytes=V7X_VMEM_LIMIT),
    )(*([proj] * 9), dec, qkd, rot_in, rot_tile, gw_pad, gb, rnw, gnw)


def _mid_fwd(mixed, x2d, target, vecs, w_out_b, w1_b, w2_b):
    t = x2d.shape[0]
    tm = min(ROW_TILE, t)

    def body(mix_ref, x_ref, tgt_ref, v_ref, wo_hbm, w1_hbm, w2_hbm,
             m_ref, x1n_ref, rstd_ref, u2_ref, a_ref, df_ref, dh2_ref, acc_ref, wo, w1, w2, sem):
        first = pl.program_id(0) == 0
        _load_resident(first, [(wo_hbm, wo), (w1_hbm, w1), (w2_hbm, w2)], sem)

        @pl.when(first)
        def _():
            acc_ref[...] = jnp.zeros_like(acc_ref)

        gate1, sc2p, sh2, gate2 = v_ref[0:1, :], v_ref[1:2, :], v_ref[2:3, :], v_ref[3:4, :]
        l1w, l1b, l2w, l2b = v_ref[4:5, :], v_ref[5:6, :], v_ref[6:7, :], v_ref[7:8, :]
        m = _dot(mix_ref[...], wo[...])
        m_ref[...] = m.astype(BF16)
        x1n, rstd1 = _ln_stats(ALPHA * x_ref[...] + gate1 * m)
        x1n_ref[...] = x1n
        rstd_ref[...] = rstd1
        x1 = x1n * l1w + l1b
        xh1, _ = _ln_stats(x1)
        u2 = (xh1 * sc2p + sh2).astype(BF16)
        u2_ref[...] = u2
        f = jnp.zeros((tm, D_MODEL), F32)
        for j in range(N_DEV):
            cols = slice(j * FF_COLS, (j + 1) * FF_COLS)
            a = _dot(u2, w1[j])
            a_ref[:, cols] = a.astype(BF16)
            r = jnp.maximum(a, 0.0)
            f = f + _dot((r * r).astype(BF16), w2[cols, :])
        yh, rstd2 = _ln_stats(ALPHA * x1 + gate2 * f)
        e = yh * l2w + l2b - tgt_ref[...]
        dy = e * (1.0 / D_MODEL)
        dh2 = _ln_bwd(dy * l2w, yh, rstd2)
        dh2_ref[...] = dh2
        df_ref[...] = (dh2 * gate2).astype(BF16)
        acc_ref[0:1, :] += jnp.sum(dy * yh, axis=0, keepdims=True)
        acc_ref[1:2, :] += jnp.sum(dy, axis=0, keepdims=True)
        acc_ref[2:3, :] += jnp.sum(dh2 * f, axis=0, keepdims=True)
        acc_ref[3:4, :] += jnp.sum(e * e, axis=0, keepdims=True) * (0.5 / D_MODEL)

    row = lambda i: (i, 0)
    fix = lambda i: (0, 0)
    hbm = pl.BlockSpec(memory_space=pl.ANY)
    return pl.pallas_call(
        body, name="mid_fwd", grid=(t // tm,),
        in_specs=[pl.BlockSpec((tm, D_MODEL), row), pl.BlockSpec((tm, D_MODEL), row), pl.BlockSpec((tm, D_MODEL), row),
                  pl.BlockSpec((8, D_MODEL), fix), hbm, hbm, hbm],
        out_specs=(pl.BlockSpec((tm, D_MODEL), row), pl.BlockSpec((tm, D_MODEL), row), pl.BlockSpec((tm, 1), row),
                   pl.BlockSpec((tm, D_MODEL), row), pl.BlockSpec((tm, D_FF), row), pl.BlockSpec((tm, D_MODEL), row),
                   pl.BlockSpec((tm, D_MODEL), row), pl.BlockSpec((8, D_MODEL), fix)),
        out_shape=(jax.ShapeDtypeStruct((t, D_MODEL), BF16), jax.ShapeDtypeStruct((t, D_MODEL), F32),
                   jax.ShapeDtypeStruct((t, 1), F32), jax.ShapeDtypeStruct((t, D_MODEL), BF16),
                   jax.ShapeDtypeStruct((t, D_FF), BF16), jax.ShapeDtypeStruct((t, D_MODEL), BF16),
                   jax.ShapeDtypeStruct((t, D_MODEL), F32), jax.ShapeDtypeStruct((8, D_MODEL), F32)),
        scratch_shapes=[pltpu.VMEM((D_MODEL, D_MODEL), BF16), pltpu.VMEM((N_DEV, D_MODEL, FF_COLS), BF16),
                        pltpu.VMEM((D_FF, D_MODEL), BF16), pltpu.SemaphoreType.DMA((3,))],
        compiler_params=pltpu.CompilerParams(dimension_semantics=("arbitrary",), vmem_limit_bytes=V7X_VMEM_LIMIT),
    )(mixed, x2d, target, vecs, w_out_b, w1_b, w2_b)


def _ffn_bwd(df, a, dh2, x1n, rstd1, m, vecs, w_out_b, w1_b, w2_b):
    t = x1n.shape[0]
    tm = min(ROW_TILE, t)

    def body(df_ref, a_ref, dh2_ref, x1n_ref, rstd_ref, m_ref, v_ref, wo_hbm, w1_hbm, w2_hbm,
             da_ref, dm_ref, dmix_ref, dxa_ref, acc_ref, wo, w1, w2, sem):
        first = pl.program_id(0) == 0
        _load_resident(first, [(wo_hbm, wo), (w1_hbm, w1), (w2_hbm, w2)], sem)

        @pl.when(first)
        def _():
            acc_ref[...] = jnp.zeros_like(acc_ref)

        gate1, sc2p, l1w, l1b = v_ref[0:1, :], v_ref[1:2, :], v_ref[2:3, :], v_ref[3:4, :]
        df = df_ref[...]
        du2 = jnp.zeros((tm, D_MODEL), F32)
        for j in range(N_DEV):
            cols = slice(j * FF_COLS, (j + 1) * FF_COLS)
            dr2 = _dot(df, w2[cols, :], NT)
            da = (dr2 * (2.0 * jnp.maximum(a_ref[:, cols].astype(F32), 0.0))).astype(BF16)
            da_ref[:, cols] = da
            du2 = du2 + _dot(da, w1[j], NT)
        x1n = x1n_ref[...]
        xh1, rstd0 = _ln_stats(x1n * l1w + l1b)
        dx1 = ALPHA * dh2_ref[...] + _ln_bwd(du2 * sc2p, xh1, rstd0)
        dh1 = _ln_bwd(dx1 * l1w, x1n, rstd_ref[...])
        dxa_ref[...] = ALPHA * dh1
        dm = (dh1 * gate1).astype(BF16)
        dm_ref[...] = dm
        dmix_ref[...] = _dot(dm, wo[...], NT)
        acc_ref[0:1, :] += jnp.sum(du2 * xh1, axis=0, keepdims=True)
        acc_ref[1:2, :] += jnp.sum(du2, axis=0, keepdims=True)
        acc_ref[2:3, :] += jnp.sum(dx1 * x1n, axis=0, keepdims=True)
        acc_ref[3:4, :] += jnp.sum(dx1, axis=0, keepdims=True)
        acc_ref[4:5, :] += jnp.sum(dh1 * m_ref[...].astype(F32), axis=0, keepdims=True)

    row = lambda i: (i, 0)
    fix = lambda i: (0, 0)
    hbm = pl.BlockSpec(memory_space=pl.ANY)
    return pl.pallas_call(
        body, name="ffn_bwd", grid=(t // tm,),
        in_specs=[pl.BlockSpec((tm, D_MODEL), row), pl.BlockSpec((tm, D_FF), row), pl.BlockSpec((tm, D_MODEL), row),
                  pl.BlockSpec((tm, D_MODEL), row), pl.BlockSpec((tm, 1), row), pl.BlockSpec((tm, D_MODEL), row),
                  pl.BlockSpec((8, D_MODEL), fix), hbm, hbm, hbm],
        out_specs=(pl.BlockSpec((tm, D_FF), row), pl.BlockSpec((tm, D_MODEL), row), pl.BlockSpec((tm, D_MODEL), row),
                   pl.BlockSpec((tm, D_MODEL), row), pl.BlockSpec((8, D_MODEL), fix)),
        out_shape=(jax.ShapeDtypeStruct((t, D_FF), BF16), jax.ShapeDtypeStruct((t, D_MODEL), BF16),
                   jax.ShapeDtypeStruct((t, D_MODEL), F32), jax.ShapeDtypeStruct((t, D_MODEL), F32),
                   jax.ShapeDtypeStruct((8, D_MODEL), F32)),
        scratch_shapes=[pltpu.VMEM((D_MODEL, D_MODEL), BF16), pltpu.VMEM((N_DEV, D_MODEL, FF_COLS), BF16),
                        pltpu.VMEM((D_FF, D_MODEL), BF16), pltpu.SemaphoreType.DMA((3,))],
        compiler_params=pltpu.CompilerParams(dimension_semantics=("arbitrary",), vmem_limit_bytes=V7X_VMEM_LIMIT),
    )(df, a, dh2, x1n, rstd1, m, vecs, w_out_b, w1_b, w2_b)


def _matmul_tn(lhs, rhs, tmm, tn, tk, name, relu_sq=False, col_slab=None):
    t, mm = lhs.shape
    nn = rhs.shape[1]
    tk = min(tk, t)
    nk = t // tk

    def body(l_ref, r_ref, o_ref, acc):
        kk = pl.program_id(2)

        @pl.when(kk == 0)
        def _():
            acc[...] = jnp.zeros_like(acc)

        l = l_ref[...]
        if relu_sq:
            lf = jnp.maximum(l.astype(F32), 0.0)
            l = (lf * lf).astype(BF16)
        acc[...] += _dot(l, r_ref[...], TN)

        @pl.when(kk == nk - 1)
        def _():
            if col_slab is None:
                o_ref[...] = acc[...].astype(o_ref.dtype)
            else:
                for s in range(tn // col_slab):
                    o_ref[s] = acc[:, s * col_slab:(s + 1) * col_slab].astype(o_ref.dtype)

    if col_slab is None:
        out_spec = pl.BlockSpec((tmm, tn), lambda i, j, k: (i, j))
        out_shape = jax.ShapeDtypeStruct((mm, nn), BF16)
    else:
        out_spec = pl.BlockSpec((tn // col_slab, tmm, col_slab), lambda i, j, k: (j, i, 0))
        out_shape = jax.ShapeDtypeStruct((nn // col_slab, mm, col_slab), BF16)
    return pl.pallas_call(
        body, name=name, grid=(mm // tmm, nn // tn, nk),
        in_specs=[pl.BlockSpec((tk, tmm), lambda i, j, k: (k, i)), pl.BlockSpec((tk, tn), lambda i, j, k: (k, j))],
        out_specs=out_spec,
        out_shape=out_shape,
        scratch_shapes=[pltpu.VMEM((tmm, tn), F32)],
        compiler_params=pltpu.CompilerParams(dimension_semantics=("arbitrary", "arbitrary", "arbitrary"),
                                             vmem_limit_bytes=V7X_VMEM_LIMIT),
    )(lhs, rhs)


def _mixer_bwd(dmix, proj, qrb, krb, oraw, tables, rst, sst, gw_pad, gb, rnw, gnw):
    t = proj.shape[0]
    tc = min(MIX_TILE, t)
    tr, tg = tc, min(GLA_SUB, tc)
    nsteps = t // tc
    scale_r = RET_D ** -0.5
    scale_g = GLA_DK ** -0.5
    gammas = _tile_gammas(tr)

    def body(dmix_ref, qrb_ref, krb_ref, rv_ref, rg_ref, gq_ref, gk_ref, gv_ref, gg_ref, glr_ref, oraw_ref,
             dec_ref, qkd_ref, rot_in_ref, rot_tile_ref, rst_ref, sst_ref, gw_ref, gb_ref, rnw_ref, gnw_ref,
             dproj_ref, dgw_ref, dvec_ref, dr_scr, ds_scr):
        @pl.when(pl.program_id(0) == 0)
        def _():
            dr_scr[...] = jnp.zeros_like(dr_scr)
            ds_scr[...] = jnp.zeros_like(ds_scr)
            dgw_ref[...] = jnp.zeros_like(dgw_ref)
            dvec_ref[...] = jnp.zeros_like(dvec_ref)

        gla_k = _gla_consts(tg)
        last_row = lax.broadcasted_iota(jnp.int32, (tg, GLA_KW), 0) == tg - 1

        def ret_tile(jj, carry):
            j = tc // tr - 1 - jj
            rows = _tile_rows(j, tr)
            cosv, sinv = _tile_rotary(rot_in_ref, rot_tile_ref)
            for h in range(RET_HEADS):
                cols = slice(h * RET_D, (h + 1) * RET_D)
                o = oraw_ref[rows, cols]
                g = rg_ref[rows, cols]
                w = rnw_ref[:, cols]
                dout = dmix_ref[rows, cols]
                oc = o - jnp.mean(o, axis=-1, keepdims=True)
                inv = lax.rsqrt(jnp.mean(oc * oc, axis=-1, keepdims=True) + LN_EPS)
                n = oc * inv
                sg = _sigmoid(g)
                sil = g * sg
                dn = dout * w * sil
                dvec_ref[0:1, cols] += jnp.sum(dout * n * sil, axis=0, keepdims=True)
                dproj_ref[rows, OFF_RG + h * RET_D:OFF_RG + (h + 1) * RET_D] = (
                    dout * n * w * (sg * (1.0 + g * (1.0 - sg)))).astype(BF16)
                doc = inv * (dn - n * jnp.mean(dn * n, axis=-1, keepdims=True))
                do = doc - jnp.mean(doc, axis=-1, keepdims=True)

                qb, kb = qrb_ref[rows, cols], krb_ref[rows, cols]
                qr, kr = qb.astype(F32), kb.astype(F32)
                vb = rv_ref[rows, cols].astype(BF16)
                dob = do.astype(BF16)
                qd, kd = qkd_ref[h], qkd_ref[RET_HEADS + h]
                p = _dot(qb, kb, NT) * dec_ref[h]
                rp = rst_ref[j, cols, :].astype(BF16)
                dr = dr_scr[cols, :]
                drb = dr.astype(BF16)
                dpb = (_dot(dob, vb, NT) * dec_ref[h]).astype(BF16)
                dqr = _dot(dpb, kb) + _dot(dob, rp, NT) * qd
                dkr = _dot(dpb, qb, TN) + _dot(vb, drb, NT) * kd
                dv = _dot(p.astype(BF16), dob, TN) + _dot((kr * kd).astype(BF16), drb)
                dr_scr[cols, :] = gammas[h] * dr + _dot((qr * qd).astype(BF16), dob, TN)
                dproj_ref[rows, OFF_RQ + h * RET_D:OFF_RQ + (h + 1) * RET_D] = (
                    _rotate_t(dqr, cosv, sinv) * scale_r).astype(BF16)
                dproj_ref[rows, OFF_RK + h * RET_D:OFF_RK + (h + 1) * RET_D] = _rotate_t(dkr, cosv, sinv).astype(BF16)
                dproj_ref[rows, OFF_RV + h * RET_D:OFF_RV + (h + 1) * RET_D] = dv.astype(BF16)
            return carry

        def gla_tile(jj, carry):
            k = gla_k
            tl = tg
            j = tc // tg - 1 - jj
            rows = _tile_rows(j, tg)
            glr = glr_ref[rows, :]
            z, b, bl, ep, em = _gla_gates(glr, gw_ref[...], gb_ref[...], k["ltri"], tl)
            qs = gq_ref[rows, :] * scale_g
            kk = gk_ref[rows, :]
            eb = jnp.exp(b)
            ekb = jnp.exp(bl - b)
            ebl = jnp.exp(bl)
            ql, qu, kl, ku = qs * ep, qs * em, kk * em, kk * ep
            qg, kg = qs * eb, kk * ekb
            qlm = _stack_heads(ql, k["hmask"]).astype(BF16)
            qum = _stack_heads(qu, k["hmask"]).astype(BF16)
            klb, kub = kl.astype(BF16), ku.astype(BF16)
            a_all = jnp.where(k["lower"], _dot(qlm, klb, NT),
                              jnp.where(k["upper"], _dot(qum, kub, NT), 0.0)).astype(BF16)
            st = sst_ref[j]
            stb = st.astype(BF16)
            ds = ds_scr[...]
            dsb = ds.astype(BF16)
            ds_new = ds * ebl
            da_parts = []
            dqg = jnp.zeros((tl, GLA_KW), F32)
            dkg = jnp.zeros((tl, GLA_KW), F32)
            for h in range(GLA_HEADS):
                cols = slice(h * GLA_DV, (h + 1) * GLA_DV)
                hr = slice(h * tl, (h + 1) * tl)
                ocols = slice(RET_HEADS * RET_D + h * GLA_DV, RET_HEADS * RET_D + (h + 1) * GLA_DV)
                o = oraw_ref[rows, ocols]
                g = gg_ref[rows, cols]
                w = gnw_ref[:, cols]
                dout = dmix_ref[rows, ocols]
                inv = lax.rsqrt(jnp.mean(o * o, axis=-1, keepdims=True) + LN_EPS)
                n = o * inv
                sg = _sigmoid(g)
                sil = g * sg
                dn = dout * w * sil
                dvec_ref[1:2, cols] += jnp.sum(dout * n * sil, axis=0, keepdims=True)
                dproj_ref[rows, OFF_GG + h * GLA_DV:OFF_GG + (h + 1) * GLA_DV] = (
                    dout * n * w * (sg * (1.0 + g * (1.0 - sg)))).astype(BF16)
                dob = (inv * (dn - n * jnp.mean(dn * n, axis=-1, keepdims=True))).astype(BF16)
                vb = gv_ref[rows, cols].astype(BF16)
                mh = k["hmask"][h]
                da_parts.append(_dot(dob, vb, NT))
                dv = _dot(a_all[hr, :], dob, TN) + _dot((kg * mh).astype(BF16), dsb, NT)
                dproj_ref[rows, OFF_GV + h * GLA_DV:OFF_GV + (h + 1) * GLA_DV] = dv.astype(BF16)
                dkg = dkg + mh * _dot(vb, dsb)
                dqg = dqg + mh * _dot(dob, stb)
                ds_new = ds_new + _dot(dob, (qg * mh).astype(BF16), TN)
            da_all = jnp.concatenate(da_parts, axis=0)
            dal = jnp.where(k["lower"], da_all, 0.0).astype(BF16)
            dau = jnp.where(k["upper"], da_all, 0.0).astype(BF16)
            dqlm = _dot(dal, klb)
            dqum = _dot(dau, kub)
            dql = jnp.zeros((tl, GLA_KW), F32)
            dqu = jnp.zeros((tl, GLA_KW), F32)
            for h in range(GLA_HEADS):
                hr = slice(h * tl, (h + 1) * tl)
                dql = dql + k["hmask"][h] * dqlm[hr, :]
                dqu = dqu + k["hmask"][h] * dqum[hr, :]
            dkl = _dot(dal, qlm, TN)
            dku = _dot(dau, qum, TN)
            dbl = (jnp.sum(dkg * kg, axis=0, keepdims=True)
                   + jnp.sum(ds * st, axis=0, keepdims=True) * ebl)
            ds_scr[...] = ds_new
            dqs = dql * ep + dqu * em + dqg * eb
            dk = dkl * em + dku * ep + dkg * ekb
            db = dql * ql - dkl * kl - dqu * qu + dku * ku + dqg * qg - dkg * kg
            db = db + jnp.where(last_row, dbl, 0.0)
            dla = _dot_split(k["utri"], db, NN, a_exact=True)
            dz = dla * (1.0 / GATE_TAU) * _sigmoid(-z)
            dvec_ref[2:3, 0:GLA_KW] += jnp.sum(dz, axis=0, keepdims=True)
            dgw_ref[...] += _dot_split(glr, dz, TN)
            dproj_ref[rows, OFF_GLR:OFF_GLR + 128] = _dot(dz.astype(BF16), gw_ref[...].astype(BF16), NT).astype(BF16)
            dproj_ref[rows, OFF_GQ:OFF_GQ + GLA_KW] = (dqs * scale_g).astype(BF16)
            dproj_ref[rows, OFF_GK:OFF_GK + GLA_KW] = dk.astype(BF16)
            return carry

        _for_tiles(tc // tr, ret_tile)
        _for_tiles(tc // tg, gla_tile)

    rev = lambda i: (nsteps - 1 - i, 0)

    def col(width, off):
        return pl.BlockSpec((tc, width), lambda i, o=off // width: (nsteps - 1 - i, o))

    fix = lambda i: (0, 0)
    fix3 = lambda i: (0, 0, 0)
    dec, qkd, rot_in, rot_tile = tables
    half = pl.BlockSpec((tc, RET_HEADS * RET_D), rev)
    in_specs = [pl.BlockSpec((tc, D_MODEL), rev), half, half, col(512, OFF_RV), col(512, OFF_RG),
                col(256, OFF_GQ), col(256, OFF_GK), col(512, OFF_GV), col(512, OFF_GG), col(128, OFF_GLR),
                pl.BlockSpec((tc, D_MODEL), rev),
                pl.BlockSpec(dec.shape, fix3), pl.BlockSpec(qkd.shape, fix3), pl.BlockSpec(rot_in.shape, fix3),
                pl.BlockSpec((1, 8, 2 * RET_D), lambda i: (nsteps - 1 - i, 0, 0)),
                pl.BlockSpec((tc // tr, RET_HEADS * RET_D, RET_D), lambda i: (nsteps - 1 - i, 0, 0)),
                pl.BlockSpec((tc // tg, GLA_DV, GLA_KW), lambda i: (nsteps - 1 - i, 0, 0)),
                pl.BlockSpec((128, GLA_KW), fix), pl.BlockSpec((1, GLA_KW), fix),
                pl.BlockSpec((1, 512), fix), pl.BlockSpec((1, 512), fix)]
    out_specs = (pl.BlockSpec((tc, D_IN_PAD), rev), pl.BlockSpec((128, GLA_KW), fix), pl.BlockSpec((8, 512), fix))
    out_shape = (jax.ShapeDtypeStruct((t, D_IN_PAD), BF16), jax.ShapeDtypeStruct((128, GLA_KW), F32),
                 jax.ShapeDtypeStruct((8, 512), F32))
    return pl.pallas_call(
        body, name="mixer_bwd", grid=(nsteps,), in_specs=in_specs, out_specs=out_specs, out_shape=out_shape,
        scratch_shapes=[pltpu.VMEM((RET_HEADS * RET_D, RET_D), F32), pltpu.VMEM((GLA_DV, GLA_KW), F32)],
        compiler_params=pltpu.CompilerParams(dimension_semantics=("arbitrary",), vmem_limit_bytes=V7X_VMEM_LIMIT),
    )(dmix, qrb, krb, *([proj] * 7), oraw, dec, qkd, rot_in, rot_tile, rst, sst, gw_pad, gb, rnw, gnw)


def _inproj_bwd(dproj, x2d, dxa, sc1p, w_in_t):
    t = x2d.shape[0]
    tm = min(PROJ_TILE, t)

    def body(dp_ref, x_ref, dxa_ref, sc_ref, w_hbm, gx_ref, acc_ref, w_vmem, sem):
        first = pl.program_id(0) == 0
        _load_w_in_t(first, w_hbm, w_vmem, sem)

        @pl.when(first)
        def _():
            acc_ref[...] = jnp.zeros_like(acc_ref)

        du = _dot(dp_ref[...], w_vmem[...])
        xh, rstd = _ln_stats(x_ref[...])
        gx_ref[...] = dxa_ref[...] + _ln_bwd(du * sc_ref[...], xh, rstd)
        acc_ref[0:1, :] += jnp.sum(du * xh, axis=0, keepdims=True)
        acc_ref[1:2, :] += jnp.sum(du, axis=0, keepdims=True)

    row = lambda i: (i, 0)
    fix = lambda i: (0, 0)
    return pl.pallas_call(
        body, name="inproj_bwd", grid=(t // tm,),
        in_specs=[pl.BlockSpec((tm, D_IN_PAD), row), pl.BlockSpec((tm, D_MODEL), row), pl.BlockSpec((tm, D_MODEL), row),
                  pl.BlockSpec((1, D_MODEL), fix), pl.BlockSpec(memory_space=pl.ANY)],
        out_specs=(pl.BlockSpec((tm, D_MODEL), row), pl.BlockSpec((8, D_MODEL), fix)),
        out_shape=(jax.ShapeDtypeStruct((t, D_MODEL), F32), jax.ShapeDtypeStruct((8, D_MODEL), F32)),
        scratch_shapes=[pltpu.VMEM((D_IN_PAD, D_MODEL), BF16), pltpu.SemaphoreType.DMA((1,))],
        compiler_params=pltpu.CompilerParams(dimension_semantics=("arbitrary",), vmem_limit_bytes=V7X_VMEM_LIMIT),
    )(dproj, x2d, dxa, sc1p, w_in_t)


def _adam_math(w, g, m, v):
    m = ADAM_B1 * m + (1.0 - ADAM_B1) * g
    v = ADAM_B2 * v + (1.0 - ADAM_B2) * (g * g)
    m_hat = m / (1.0 - ADAM_B1 ** ADAM_STEP)
    v_hat = v / (1.0 - ADAM_B2 ** ADAM_STEP)
    delta = -ADAM_LR * (m_hat / (jnp.sqrt(v_hat) + ADAM_EPS) + ADAM_WD * w)
    return delta, m, v


def _adamw(w, gparts, m, v, name):
    nparts, rows, cols = gparts.shape
    tr = rows
    for cand in (512, 256, 128, 64, 32, 16, 8):
        if rows % cand == 0:
            tr = cand
            break

    def body(w_ref, g_ref, m_ref, v_ref, go_ref, d_ref, mo_ref, vo_ref):
        g = g_ref[0].astype(F32)
        for p in range(1, nparts):
            g = g + g_ref[p].astype(F32)
        delta, mn, vn = _adam_math(w_ref[...], g, m_ref[...], v_ref[...])
        go_ref[...] = g
        d_ref[...] = delta
        mo_ref[...] = mn
        vo_ref[...] = vn

    blk = pl.BlockSpec((tr, cols), lambda i: (i, 0))
    shp = jax.ShapeDtypeStruct((rows, cols), F32)
    return pl.pallas_call(
        body, name=name, grid=(rows // tr,),
        in_specs=[blk, pl.BlockSpec((nparts, tr, cols), lambda i: (0, i, 0)), blk, blk],
        out_specs=(blk, blk, blk, blk), out_shape=(shp, shp, shp, shp),
        compiler_params=pltpu.CompilerParams(dimension_semantics=("arbitrary",), vmem_limit_bytes=V7X_VMEM_LIMIT),
    )(w, gparts, m, v)


def _small_reduce(gathered, gathered_gw, c_all, dmod_cols):
    def body(g_ref, gw_ref, c_ref, dm_ref, sum_ref, gwsum_ref, gb_ref, gwa_ref):
        s = g_ref[0]
        sw = gw_ref[0]
        for p in range(1, N_DEV):
            s = s + g_ref[p]
            sw = sw + gw_ref[p]
        sum_ref[...] = s
        gwsum_ref[...] = sw
        for i in range(6):
            gb_ref[:, i * D_MODEL:(i + 1) * D_MODEL] = s[i:i + 1, :]
        cc = c_ref[...]
        gwa_ref[...] = _dot(cc * _sigmoid(cc), dm_ref[...], TN, HIGHEST)

    vm = pl.BlockSpec(memory_space=pltpu.VMEM)
    return pl.pallas_call(
        body, name="small_reduce",
        out_shape=(jax.ShapeDtypeStruct(gathered.shape[1:], F32), jax.ShapeDtypeStruct(gathered_gw.shape[1:], F32),
                   jax.ShapeDtypeStruct((1, 6 * D_MODEL), F32), jax.ShapeDtypeStruct((D_MODEL, ADA_COLS), F32)),
        in_specs=[vm] * 4, out_specs=(vm, vm, vm, vm),
        compiler_params=pltpu.CompilerParams(vmem_limit_bytes=V7X_VMEM_LIMIT),
    )(gathered, gathered_gw, c_all, dmod_cols)


SMR_LN1W, SMR_LN1B, SMR_LN2W, SMR_LN2B, SMR_NORMS, SMR_MISC = 6, 7, 8, 9, 10, 11


def _adamw_small(gsum, g_b_ada, g_ggw, params, moms, vels):
    n = len(params)

    def body(*refs):
        gsum_ref, gb_ref, gw_ref = refs[:3]
        w_refs, m_refs, v_refs = refs[3:3 + n], refs[3 + n:3 + 2 * n], refs[3 + 2 * n:3 + 3 * n]
        outs = refs[3 + 3 * n:]
        g_refs, d_refs, mo_refs, vo_refs = outs[:n - 1], outs[n - 1:2 * n - 1], outs[2 * n - 1:3 * n - 1], outs[3 * n - 1:]
        grads = [gb_ref[...],
                 gsum_ref[SMR_NORMS:SMR_NORMS + 1, 0:512],
                 gsum_ref[SMR_MISC:SMR_MISC + 1, 0:GLA_KW],
                 gsum_ref[SMR_NORMS:SMR_NORMS + 1, 512:1024],
                 gsum_ref[SMR_LN1W:SMR_LN1W + 1, :], gsum_ref[SMR_LN1B:SMR_LN1B + 1, :],
                 gsum_ref[SMR_LN2W:SMR_LN2W + 1, :], gsum_ref[SMR_LN2B:SMR_LN2B + 1, :],
                 gw_ref[...]]
        for i in range(n):
            delta, mn, vn = _adam_math(w_refs[i][...], grads[i], m_refs[i][...], v_refs[i][...])
            if i < n - 1:
                g_refs[i][...] = grads[i]
            d_refs[i][...] = delta
            mo_refs[i][...] = mn
            vo_refs[i][...] = vn

    vm = pl.BlockSpec(memory_space=pltpu.VMEM)
    shapes = [jax.ShapeDtypeStruct(p.shape, F32) for p in params]
    n_in = 3 + 3 * n
    out_shape = tuple(shapes[:n - 1] + shapes * 3)
    return pl.pallas_call(
        body, name="adamw_small", out_shape=out_shape,
        in_specs=[vm] * n_in, out_specs=tuple([vm] * len(out_shape)),
        compiler_params=pltpu.CompilerParams(vmem_limit_bytes=V7X_VMEM_LIMIT),
    )(gsum, g_b_ada, g_ggw, *params, *moms, *vels)


def kernel(x, c, w_ada, b_ada, w_in, ret_norm_w, gla_gate_w, gla_gate_b, gla_norm_w, w_out, ln1_w, ln1_b, w_ff1, w_ff2, ln2_w, ln2_b, loss_target, m_w_ada, m_b_ada, m_w_in, m_ret_norm_w, m_gla_gate_w, m_gla_gate_b, m_gla_norm_w, m_w_out, m_ln1_w, m_ln1_b, m_w_ff1, m_w_ff2, m_ln2_w, m_ln2_b, v_w_ada, v_b_ada, v_w_in, v_ret_norm_w, v_gla_gate_w, v_gla_gate_b, v_gla_norm_w, v_w_out, v_ln1_w, v_ln1_b, v_w_ff1, v_w_ff2, v_ln2_w, v_ln2_b):
    t = x.shape[1]
    xi, yi, ci = _my_coords()
    me = 4 * xi + 2 * yi + ci
    x2d = x[0]
    tgt = loss_target[0]

    wg = _exchange_start([w_out[0].astype(BF16), w_ff1[0].astype(BF16), w_ff2[0].astype(BF16)], True, "wgather_start")

    c_ext = jnp.concatenate([c, gla_gate_w[0].reshape(1, GATE_RANK * GLA_KW // N_DEV)], axis=1)
    b_l = lax.dynamic_slice(b_ada, (0, me * ADA_COLS), (1, ADA_COLS)) + wg[4][0, 0]
    c_all3, mod_all, wi_g = _adaln_mod(c_ext, w_ada[0], b_l, w_in[0].T.astype(BF16))
    c_all = c_all3[:, 0, :D_MODEL]
    gate_w = c_all3[:, 0, D_MODEL:].reshape(N_DEV, GATE_RANK, GLA_KW // N_DEV)
    gate_w = gate_w.transpose(1, 0, 2).reshape(GATE_RANK, GLA_KW)
    gw_pad = jnp.zeros((128, GLA_KW), F32).at[:GATE_RANK].set(gate_w)
    mod = lax.dynamic_slice(mod_all, (0, me, 0), (N_DEV, 1, ADA_COLS)).reshape(6, D_MODEL)
    shift1, scale1, gate1, shift2, scale2, gate2 = [mod[i:i + 1] for i in range(6)]

    w_in_t = wi_g.reshape(D_IN, D_MODEL)

    tables = _ret_tables(t, min(MIX_TILE, t))

    sc1p = 1.0 + scale1
    proj, u = _inproj_fwd(x2d, sc1p, shift1, w_in_t)
    mixed, oraw, qrb, krb, rst, sst = _mixer_fwd(proj, tables, gw_pad, gla_gate_b, ret_norm_w, gla_norm_w)
    wo_g, w1_b, w2_g = _exchange_wait(*wg[:4], mixed, True, "wgather_wait")
    w_out_b = wo_g.reshape(D_MODEL, D_MODEL)
    w2_b = w2_g.reshape(D_FF, D_MODEL)
    vec_f = jnp.concatenate([gate1, 1.0 + scale2, shift2, gate2, ln1_w, ln1_b, ln2_w, ln2_b], axis=0)
    m, x1n, rstd1, u2, a, df, dh2, acc_f = _mid_fwd(mixed, x2d, tgt, vec_f, w_out_b, w1_b, w2_b)

    vec_b = jnp.concatenate([gate1, 1.0 + scale2, ln1_w, ln1_b, jnp.zeros((4, D_MODEL), F32)], axis=0)
    da, dm, dmix, dxa, acc_b = _ffn_bwd(df, a, dh2, x1n, rstd1, m, vec_b, w_out_b, w1_b, w2_b)
    dw2 = _matmul_tn(a, df, 2048, 1024, 1024, "tn_dw2", relu_sq=True)
    dw1 = _matmul_tn(u2, da, 1024, 2048, 1024, "tn_dw1", col_slab=FF_COLS)
    dwo = _matmul_tn(mixed, dm, 1024, 1024, 1024, "tn_dwout")
    gx = _exchange_start([dwo.reshape(N_DEV, OUT_ROWS, D_MODEL), dw1, dw2.reshape(N_DEV, FF_COLS, D_MODEL)], False,
                         "gradx_start")
    dproj, dgw, dvec = _mixer_bwd(dmix, proj, qrb, krb, oraw, tables, rst, sst, gw_pad,
                                  gla_gate_b + gx[4][0, 0], ret_norm_w, gla_norm_w)
    dwi = _matmul_tn(u, dproj, 1024, D_IN_PAD, 512, "tn_dwin")
    dwi_s = dwi[:, :D_IN].reshape(D_MODEL, N_DEV, IN_COLS).transpose(1, 0, 2)
    gi = _exchange_start([dwi_s], False, "gradin_start")
    grad_x, acc_i = _inproj_bwd(dproj, x2d, dxa, sc1p + gi[4][0, 0], w_in_t)
    r_wo, r_w1, r_w2 = _exchange_wait(*gx[:4], acc_i, False, "gradx_wait")
    r_wi, = _exchange_wait(*gi[:4], acc_i, False, "gradin_wait")

    loss_part = jnp.sum(acc_f[3])
    small = jnp.concatenate([
        acc_i[1:2], acc_i[0:1], acc_b[4:5], acc_b[1:2], acc_b[0:1], acc_f[2:3],
        acc_b[2:3], acc_b[3:4], acc_f[0:1], acc_f[1:2],
        jnp.concatenate([dvec[0:1], dvec[1:2]], axis=1),
        jnp.concatenate([dvec[2:3, :GLA_KW], jnp.full((1, 128), loss_part, F32),
                         jnp.zeros((1, D_MODEL - GLA_KW - 128), F32)], axis=1),
        jnp.zeros((4, D_MODEL), F32)], axis=0)
    small_all, gw_all = _small_gather([small, dgw[:GATE_RANK]])
    dmod_all = small_all[:, :6].reshape(N_DEV, 6 * D_MODEL)
    dmod_cols = lax.dynamic_slice(dmod_all, (0, me * ADA_COLS), (N_DEV, ADA_COLS))
    ssum, gw_sum, g_b_ada, g_w_ada = _small_reduce(small_all, gw_all, c_all, dmod_cols)
    loss = ssum[SMR_MISC, GLA_KW]
    g_ggw = lax.dynamic_slice(gw_sum, (0, me * (GLA_KW // N_DEV)), (GATE_RANK, GLA_KW // N_DEV))[None]

    small_w = [b_ada, ret_norm_w, gla_gate_b, gla_norm_w, ln1_w, ln1_b, ln2_w, ln2_b, gla_gate_w]
    small_m = [m_b_ada, m_ret_norm_w, m_gla_gate_b, m_gla_norm_w, m_ln1_w, m_ln1_b, m_ln2_w, m_ln2_b, m_gla_gate_w]
    small_v = [v_b_ada, v_ret_norm_w, v_gla_gate_b, v_gla_norm_w, v_ln1_w, v_ln1_b, v_ln2_w, v_ln2_b, v_gla_gate_w]
    res = _adamw_small(ssum, g_b_ada, g_ggw, small_w, small_m, small_v)
    small_g = list(res[:8]) + [g_ggw]
    d_small, m_small, v_small = list(res[8:17]), list(res[17:26]), list(res[26:35])

    _, d_w_ada, nm_w_ada, nv_w_ada = _adamw(w_ada[0], g_w_ada[None], m_w_ada[0], v_w_ada[0], "adamw_ada")

    big =[_adamw(w[0], r, m_[0], v_[0], nm) for w, r, m_, v_, nm in (
        (w_in, r_wi, m_w_in, v_w_in, "adamw_in"), (w_out, r_wo, m_w_out, v_w_out, "adamw_out"),
        (w_ff1, r_w1, m_w_ff1, v_w_ff1, "adamw_ff1"), (w_ff2, r_w2, m_w_ff2, v_w_ff2, "adamw_ff2"))]
    g_big, d_big, m_big, v_big = [[b[i][None] for b in big] for i in range(4)]

    def ordered(w_ada_v, small_vals, big_vals):
        b_ada_v, rnw_v, ggb_v, gnw_v, l1w_v, l1b_v, l2w_v, l2b_v, ggw_v = small_vals
        wi_v, wo_v, w1_v, w2_v = big_vals
        return [w_ada_v, b_ada_v, wi_v, rnw_v, ggw_v, ggb_v, gnw_v, wo_v, l1w_v, l1b_v, w1_v, w2_v, l2w_v, l2b_v]

    grads = ordered(g_w_ada[None], small_g, g_big)
    deltas = ordered(d_w_ada[None], d_small, d_big)
    new_m = ordered(nm_w_ada[None], m_small, m_big)
    new_v = ordered(nv_w_ada[None], v_small, v_big)
    return (loss, grad_x[None], *grads, *deltas, *new_m, *new_v)
```

```python
import functools

import numpy as np
import jax
import jax.numpy as jnp
from jax import lax
from jax.experimental import pallas as pl
from jax.experimental.pallas import tpu as pltpu

F32 = jnp.float32
BF16 = jnp.bfloat16
MESH = pl.DeviceIdType.MESH
HIGHEST = lax.Precision.HIGHEST

N_DEV = 8
D_MODEL = 1024
CHUNK = 64
RET_HEADS = 4
RET_D = 128
GLA_HEADS = 4
GLA_DK = 64
GLA_DV = 128
GLA_KW = GLA_HEADS * GLA_DK
GATE_RANK = 16
GATE_TAU = 16.0
D_FF = 4096
LN_EPS = 1e-5
ALPHA = (2.0 * 1) ** 0.25
D_IN = 3600
D_IN_PAD = 3712
ADA_COLS = 6 * D_MODEL // N_DEV
IN_COLS = D_IN // N_DEV
FF_COLS = D_FF // N_DEV
OUT_ROWS = D_MODEL // N_DEV

OFF_RQ, OFF_RK, OFF_RV, OFF_RG = 0, 512, 1024, 1536
OFF_GQ, OFF_GK, OFF_GV, OFF_GG, OFF_GLR = 2048, 2304, 2560, 3072, 3584

ADAM_LR, ADAM_B1, ADAM_B2, ADAM_EPS, ADAM_WD, ADAM_STEP = 0.001, 0.9, 0.999, 1e-08, 0.01, 10

V7X_VMEM_LIMIT = 56 * 1024 * 1024

ROW_TILE = 256
PROJ_TILE = 512
MIX_TILE = 256
GLA_SUB = 128


def _log_gamma(h):
    return float(np.log(np.float32(1.0) - np.float32(2.0) ** np.float32(-5.0 - h)))


def _my_coords():
    return lax.axis_index("x"), lax.axis_index("y"), lax.axis_index("c")


def _flip(v, bit):
    return 1 - v if bit else v


def _peer(k):
    x, y, c = _my_coords()
    px, py, pc = _flip(x, (k >> 2) & 1), _flip(y, (k >> 1) & 1), _flip(c, k & 1)
    return (px, py, pc), 4 * px + 2 * py + pc


def _dot(a, b, dims=(((1,), (0,)), ((), ())), precision=None):
    return lax.dot_general(a, b, dims, precision=precision, preferred_element_type=F32)


NN = (((1,), (0,)), ((), ()))
NT = (((1,), (1,)), ((), ()))
TN = (((0,), (0,)), ((), ()))


def _split_bf16(v, parts):
    out = []
    for _ in range(parts):
        p = v.astype(BF16)
        out.append(p)
        v = v - p.astype(F32)
    return out


def _dot_split(a, b, dims, a_exact=False):
    if a_exact:
        ab = a.astype(BF16)
        return sum(_dot(ab, p, dims) for p in _split_bf16(b, 3))
    a_hi, a_lo = _split_bf16(a, 2)
    b_hi, b_lo = _split_bf16(b, 2)
    return _dot(a_hi, b_hi, dims) + _dot(a_hi, b_lo, dims) + _dot(a_lo, b_hi, dims)


def _sigmoid(x):
    return 1.0 / (1.0 + jnp.exp(-x))


def _ln_stats(x):
    mu = jnp.mean(x, axis=-1, keepdims=True)
    xc = x - mu
    var = jnp.mean(xc * xc, axis=-1, keepdims=True)
    rstd = lax.rsqrt(var + LN_EPS)
    return xc * rstd, rstd


def _ln_bwd(dyh, xh, rstd):
    return rstd * (dyh - jnp.mean(dyh, axis=-1, keepdims=True) - xh * jnp.mean(dyh * xh, axis=-1, keepdims=True))


def _adaln_mod(c_ext, w_ada_l, b_l, w_in_l):
    width = c_ext.shape[1]

    def body(c_ref, w_ref, b_ref, wi_ref, call_ref, mod_ref, wig_ref, token_ref, s1, r1, s2, r2, gs, gr, gl):
        gather = _TwoLevelGather([wi_ref], [wig_ref], gs, gr, gl)
        gather.start()
        token_ref[...] = jnp.zeros_like(token_ref)
        x, y, c = _my_coords()
        me = 4 * x + 2 * y + c
        call_ref[me] = c_ref[...]
        sends = []
        for k in range(1, N_DEV):
            peer, _ = _peer(k)
            cp = pltpu.make_async_remote_copy(c_ref, call_ref.at[me], s1.at[k - 1], r1.at[k - 1],
                                              device_id=peer, device_id_type=MESH)
            cp.start()
            sends.append(cp)
        for k in range(1, N_DEV):
            peer, pid = _peer(k)
            pltpu.make_async_remote_copy(c_ref, call_ref.at[pid], s1.at[k - 1], r1.at[k - 1],
                                         device_id=peer, device_id_type=MESH).wait_recv()
        for cp in sends:
            cp.wait_send()
        row = lax.broadcasted_iota(jnp.int32, (N_DEV, D_MODEL), 0)
        call = jnp.zeros((N_DEV, D_MODEL), F32)
        for j in range(N_DEV):
            call = jnp.where(row == j, jnp.broadcast_to(call_ref[j][:, :D_MODEL], (N_DEV, D_MODEL)), call)
        sc = call * _sigmoid(call)
        mod = _dot(sc, w_ref[...], NN, HIGHEST) + b_ref[...]
        mod_ref[me] = mod
        sends = []
        for k in range(1, N_DEV):
            peer, _ = _peer(k)
            cp = pltpu.make_async_remote_copy(mod_ref.at[me], mod_ref.at[me], s2.at[k - 1], r2.at[k - 1],
                                              device_id=peer, device_id_type=MESH)
            cp.start()
            sends.append(cp)
        for k in range(1, N_DEV):
            peer, pid = _peer(k)
            pltpu.make_async_remote_copy(mod_ref.at[pid], mod_ref.at[pid], s2.at[k - 1], r2.at[k - 1],
                                         device_id=peer, device_id_type=MESH).wait_recv()
        for cp in sends:
            cp.wait_send()
        gather.forward()
        gather.finish()

    vm = pl.BlockSpec(memory_space=pltpu.VMEM)
    hbm = pl.BlockSpec(memory_space=pl.ANY)
    return pl.pallas_call(
        body, name="adaln_mod",
        out_shape=(jax.ShapeDtypeStruct((N_DEV, 1, width), F32),
                   jax.ShapeDtypeStruct((N_DEV, N_DEV, ADA_COLS), F32),
                   jax.ShapeDtypeStruct((N_DEV, *w_in_l.shape), w_in_l.dtype),
                   jax.ShapeDtypeStruct((8, 128), F32)),
        in_specs=[vm, vm, vm, hbm], out_specs=(vm, vm, hbm, vm),
        scratch_shapes=[pltpu.SemaphoreType.DMA((N_DEV - 1,))] * 4
        + [pltpu.SemaphoreType.DMA((7,)), pltpu.SemaphoreType.DMA((7,)), pltpu.SemaphoreType.DMA((1,))],
        compiler_params=pltpu.CompilerParams(vmem_limit_bytes=V7X_VMEM_LIMIT),
    )(c_ext, w_ada_l, b_l, w_in_l)


class _TwoLevelGather:
    def __init__(self, x_refs, out_refs, send_sems, recv_sems, local_sems):
        self.x_refs, self.out_refs = x_refs, out_refs
        self.send_sems, self.recv_sems, self.local_sems = send_sems, recv_sems, local_sems
        x, y, c = _my_coords()
        self.c = c
        self.me, self.sibling = (x, y, c), (x, y, 1 - c)
        self.chips = [(1 - x, y), (x, 1 - y), (1 - x, 1 - y)]

    def _copy(self, a, k, block, to, src=None):
        px, py, pc = block
        slab = self.out_refs[a].at[4 * px + 2 * py + pc]
        return pltpu.make_async_remote_copy(
            src_ref=slab if src is None else src, dst_ref=slab,
            send_sem=self.send_sems.at[7 * a + k], recv_sem=self.recv_sems.at[7 * a + k],
            device_id=to, device_id_type=MESH)

    def _mine(self, a):
        px, py, pc = self.me
        return pltpu.make_async_copy(self.x_refs[a], self.out_refs[a].at[4 * px + 2 * py + pc], self.local_sems.at[a])

    def _first(self, a):
        cps = [self._copy(a, 0, self.me, self.sibling, src=self.x_refs[a])]
        cps += [self._copy(a, 1 + j, self.me, (*chip, self.c), src=self.x_refs[a]) for j, chip in enumerate(self.chips)]
        return cps

    def _passed(self, a):
        return [self._copy(a, 4 + j, (*chip, self.c), self.sibling) for j, chip in enumerate(self.chips)]

    def start(self):
        for a in range(len(self.x_refs)):
            self._mine(a).start()
            for cp in self._first(a):
                cp.start()

    def forward(self):
        for a in range(len(self.x_refs)):
            passed = self._passed(a)
            for j, chip in enumerate(self.chips):
                self._copy(a, 1 + j, (*chip, self.c), self.me).wait_recv()
                passed[j].start()

    def finish(self):
        for a in range(len(self.x_refs)):
            self._copy(a, 0, self.sibling, self.me).wait_recv()
            for j, chip in enumerate(self.chips):
                self._copy(a, 4 + j, (*chip, 1 - self.c), self.me).wait_recv()
            for cp in self._first(a) + self._passed(a):
                cp.wait_send()
            self._mine(a).wait()


def _exchange_copy(src_refs, land_refs, send_sems, recv_sems, a, k, gather, receiving):
    x, y, c = _my_coords()
    me = 4 * x + 2 * y + c
    peer, pid = _peer(k)
    src = src_refs[a] if gather else src_refs[a].at[pid]
    dst = land_refs[a].at[pid if receiving else me]
    return pltpu.make_async_remote_copy(src, dst, send_sems.at[7 * a + k - 1], recv_sems.at[7 * a + k - 1],
                                        device_id=peer, device_id_type=MESH)


def _exchange_start(srcs, gather, name):
    n = len(srcs)
    xi, yi, ci = _my_coords()
    me = 4 * xi + 2 * yi + ci
    lands = []
    for s in srcs:
        own = s[None] if gather else lax.dynamic_slice_in_dim(s, me, 1, axis=0)
        lands.append(lax.dynamic_update_slice_in_dim(lax.empty((N_DEV, *own.shape[1:]), s.dtype), own, me, axis=0))

    def body(*refs):
        src_refs, land_refs, send_sems, recv_sems, token = refs[:n], refs[n:2 * n], refs[2 * n], refs[2 * n + 1], refs[-1]
        for a in range(n):
            for k in range(1, N_DEV):
                _exchange_copy(src_refs, land_refs, send_sems, recv_sems, a, k, gather, receiving=False).start()
        token[...] = jnp.zeros_like(token)

    hbm = pl.BlockSpec(memory_space=pltpu.HBM)
    sem = pl.BlockSpec(memory_space=pltpu.SEMAPHORE)
    res = pl.pallas_call(
        body, name=name,
        out_shape=(pltpu.SemaphoreType.DMA((7 * n,)), pltpu.SemaphoreType.DMA((7 * n,)),
                   *[pltpu.HBM(v.shape, v.dtype) for v in srcs + lands], jax.ShapeDtypeStruct((8, 128), F32)),
        in_specs=[hbm] * (2 * n),
        out_specs=(sem, sem, *([hbm] * (2 * n)), pl.BlockSpec(memory_space=pltpu.VMEM)),
        input_output_aliases={i: 2 + i for i in range(2 * n)},
        compiler_params=pltpu.CompilerParams(has_side_effects=pltpu.SideEffectType.DATAFLOW_SIDE_EFFECTING),
    )(*[pltpu.with_memory_space_constraint(v, pltpu.HBM) for v in srcs + lands])
    return res[0], res[1], list(res[2:2 + n]), list(res[2 + n:2 + 2 * n]), res[-1]


def _exchange_wait(send_sems, recv_sems, srcs, lands, after, gather, name):
    n = len(srcs)

    def body(*refs):
        src_refs, land_refs, s_sems, r_sems = refs[:n], refs[n:2 * n], refs[2 * n], refs[2 * n + 1]
        for a in range(n):
            for k in range(1, N_DEV):
                _exchange_copy(src_refs, land_refs, s_sems, r_sems, a, k, gather, receiving=False).wait_send()
                _exchange_copy(src_refs, land_refs, s_sems, r_sems, a, k, gather, receiving=True).wait_recv()

    hbm = pl.BlockSpec(memory_space=pltpu.HBM)
    sem = pl.BlockSpec(memory_space=pltpu.SEMAPHORE)
    res = pl.pallas_call(
        body, name=name,
        out_shape=tuple(pltpu.HBM(v.shape, v.dtype) for v in srcs + lands),
        in_specs=[hbm] * (2 * n) + [sem, sem, pl.BlockSpec(memory_space=pl.ANY)],
        out_specs=tuple([hbm] * (2 * n)),
        input_output_aliases={i: i for i in range(2 * n)},
        compiler_params=pltpu.CompilerParams(has_side_effects=pltpu.SideEffectType.DATAFLOW_SIDE_EFFECTING),
    )(*srcs, *lands, send_sems, recv_sems, after)
    return list(res[n:])


def _small_gather(vecs):
    n = len(vecs)

    def body(*refs):
        v_refs, out_refs, s_sems, r_sems = refs[:n], refs[n:2 * n], refs[2 * n], refs[2 * n + 1]
        x, y, c = _my_coords()
        me = 4 * x + 2 * y + c
        sends = []
        for a in range(n):
            out_refs[a][me] = v_refs[a][...]
            for k in range(1, N_DEV):
                peer, _ = _peer(k)
                cp = pltpu.make_async_remote_copy(v_refs[a], out_refs[a].at[me], s_sems.at[7 * a + k - 1],
                                                  r_sems.at[7 * a + k - 1], device_id=peer, device_id_type=MESH)
                cp.start()
                sends.append(cp)
        for a in range(n):
            for k in range(1, N_DEV):
                peer, pid = _peer(k)
                pltpu.make_async_remote_copy(v_refs[a], out_refs[a].at[pid], s_sems.at[7 * a + k - 1],
                                             r_sems.at[7 * a + k - 1], device_id=peer, device_id_type=MESH).wait_recv()
        for cp in sends:
            cp.wait_send()

    vm = pl.BlockSpec(memory_space=pltpu.VMEM)
    return pl.pallas_call(
        body, name="small_gather",
        out_shape=tuple(jax.ShapeDtypeStruct((N_DEV, *v.shape), v.dtype) for v in vecs),
        in_specs=[vm] * n, out_specs=tuple([vm] * n),
        scratch_shapes=[pltpu.SemaphoreType.DMA((7 * n,))] * 2,
    )(*vecs)


def _load_resident(step_is_first, pairs, sem):
    @pl.when(step_is_first)
    def _():
        copies = [pltpu.make_async_copy(src, dst, sem.at[i]) for i, (src, dst) in enumerate(pairs)]
        for cp in copies:
            cp.start()
        for cp in copies:
            cp.wait()


def _load_w_in_t(step_is_first, w_hbm, w_vmem, sem):
    @pl.when(step_is_first)
    def _():
        w_vmem[D_IN:, :] = jnp.zeros((D_IN_PAD - D_IN, D_MODEL), BF16)
    _load_resident(step_is_first, [(w_hbm, w_vmem.at[pl.ds(0, D_IN)])], sem)


def _inproj_fwd(x2d, sc1p, sh1, w_in_t):
    t = x2d.shape[0]
    tm = min(PROJ_TILE, t)

    def body(x_ref, sc_ref, sh_ref, w_hbm, proj_ref, u_ref, w_vmem, sem):
        _load_w_in_t(pl.program_id(0) == 0, w_hbm, w_vmem, sem)
        xh, _ = _ln_stats(x_ref[...])
        ub = (xh * sc_ref[...] + sh_ref[...]).astype(BF16)
        u_ref[...] = ub
        proj_ref[...] = _dot(ub, w_vmem[...], NT)

    row = lambda i: (i, 0)
    fix = lambda i: (0, 0)
    return pl.pallas_call(
        body, name="inproj_fwd", grid=(t // tm,),
        in_specs=[pl.BlockSpec((tm, D_MODEL), row), pl.BlockSpec((1, D_MODEL), fix), pl.BlockSpec((1, D_MODEL), fix),
                  pl.BlockSpec(memory_space=pl.ANY)],
        out_specs=(pl.BlockSpec((tm, D_IN_PAD), row), pl.BlockSpec((tm, D_MODEL), row)),
        out_shape=(jax.ShapeDtypeStruct((t, D_IN_PAD), F32), jax.ShapeDtypeStruct((t, D_MODEL), BF16)),
        scratch_shapes=[pltpu.VMEM((D_IN_PAD, D_MODEL), BF16), pltpu.SemaphoreType.DMA((1,))],
        compiler_params=pltpu.CompilerParams(dimension_semantics=("arbitrary",), vmem_limit_bytes=V7X_VMEM_LIMIT),
    )(x2d, sc1p, sh1, w_in_t)


CHUNK_SHIFT = 6


def _ret_tables(t, tl):
    r = lax.broadcasted_iota(jnp.int32, (tl, tl), 0)
    c = lax.broadcasted_iota(jnp.int32, (tl, tl), 1)
    allowed = jnp.right_shift(c, CHUNK_SHIFT) <= jnp.right_shift(r, CHUNK_SHIFT)
    dist = jnp.abs(r - c).astype(F32)
    rowf = lax.broadcasted_iota(jnp.int32, (tl, RET_D), 0).astype(F32)
    lgs = [_log_gamma(h) for h in range(RET_HEADS)]
    dec = jnp.stack([jnp.where(allowed, jnp.exp(lg * dist), 0.0) for lg in lgs])
    qkd = jnp.stack([jnp.exp(lg * (rowf + 1.0)) for lg in lgs] + [jnp.exp(lg * (tl - 1.0 - rowf)) for lg in lgs])
    inv = 1.0 / (10000.0 ** jnp.linspace(0.0, 1.0, RET_D // 2, dtype=F32))
    off = jnp.arange(tl, dtype=F32)[:, None] * inv[None, :]
    start = (jnp.arange(t // tl, dtype=F32) * tl)[:, None] * inv[None, :]
    co, so = jnp.cos(off), jnp.sin(off)
    rot_in = jnp.stack([jnp.concatenate([co, co], 1), jnp.concatenate([so, so], 1),
                        jnp.concatenate([-co, co], 1), jnp.concatenate([-so, so], 1)])
    cs, ss = jnp.cos(start), jnp.sin(start)
    rot_tile = jnp.concatenate([cs, cs, ss, ss], axis=1)
    rot_tile = jnp.broadcast_to(rot_tile[:, None, :], (t // tl, 8, 2 * RET_D))
    return dec, qkd, rot_in, rot_tile


def _tile_gammas(tl):
    return [float(np.exp(np.float32(_log_gamma(h)) * np.float32(tl))) for h in range(RET_HEADS)]


def _tile_rotary(rot_in_ref, rot_tile_ref):
    ca, sa = rot_tile_ref[0, 0:1, 0:RET_D], rot_tile_ref[0, 0:1, RET_D:2 * RET_D]
    cosv = ca * rot_in_ref[0] - sa * rot_in_ref[1]
    sinv = sa * rot_in_ref[2] + ca * rot_in_ref[3]
    return cosv, sinv


def _gla_consts(tl):
    r = lax.broadcasted_iota(jnp.int32, (tl, tl), 0)
    c = lax.broadcasted_iota(jnp.int32, (tl, tl), 1)
    ltri = (c <= r).astype(F32)
    utri = (c >= r).astype(F32)
    lane = lax.broadcasted_iota(jnp.int32, (1, GLA_KW), 1)
    hmask = [((lane >= h * GLA_DK) & (lane < (h + 1) * GLA_DK)).astype(F32) for h in range(GLA_HEADS)]
    rs = lax.broadcasted_iota(jnp.int32, (GLA_HEADS * tl, tl), 0) & (tl - 1)
    cs = lax.broadcasted_iota(jnp.int32, (GLA_HEADS * tl, tl), 1)
    lower = cs <= rs
    same = jnp.right_shift(cs, CHUNK_SHIFT) == jnp.right_shift(rs, CHUNK_SHIFT)
    upper = jnp.logical_and(jnp.logical_not(lower), same)
    return dict(ltri=ltri, utri=utri, hmask=hmask, lower=lower, upper=upper)


def _tile_rows(j, tl):
    return pl.ds(j * tl, tl) if isinstance(j, int) else pl.ds(pl.multiple_of(j * tl, tl), tl)


def _for_tiles(cps, fn):
    if cps == 1:
        fn(0, 0)
    else:
        lax.fori_loop(0, cps, fn, 0)


def _rotate(v, cosv, sinv):
    return v * cosv + pltpu.roll(v, RET_D // 2, 1) * sinv


def _rotate_t(d, cosv, sinv):
    return d * cosv + pltpu.roll(d * sinv, RET_D // 2, 1)


def _stack_heads(v, hmask):
    return jnp.concatenate([v * hmask[h] for h in range(GLA_HEADS)], axis=0)


def _gla_gates(glr, gw, gb, ltri, tl):
    z = _dot_split(glr, gw, NN) + gb
    la = (jnp.minimum(z, 0.0) - jnp.log(1.0 + jnp.exp(-jnp.abs(z)))) * (1.0 / GATE_TAU)
    b = _dot_split(ltri, la, NN, a_exact=True)
    level = b[tl // 2 - 1:tl // 2, :]
    ep = jnp.exp(jnp.clip(b - level, -80.0, 80.0))
    em = jnp.exp(jnp.clip(level - b, -80.0, 80.0))
    bl = b[tl - 1:tl, :]
    return z, b, bl, ep, em


def _mixer_fwd(proj, tables, gw_pad, gb, rnw, gnw):
    t = proj.shape[0]
    tc = min(MIX_TILE, t)
    tr, tg = tc, min(GLA_SUB, tc)
    nsteps = t // tc
    scale_r = RET_D ** -0.5
    scale_g = GLA_DK ** -0.5
    gammas = _tile_gammas(tr)

    def body(rq_ref, rk_ref, rv_ref, rg_ref, gq_ref, gk_ref, gv_ref, gg_ref, glr_ref,
             dec_ref, qkd_ref, rot_in_ref, rot_tile_ref, gw_ref, gb_ref, rnw_ref, gnw_ref,
             mix_ref, oraw_ref, qrb_ref, krb_ref, rst_ref, sst_ref, r_scr, s_scr):
        @pl.when(pl.program_id(0) == 0)
        def _():
            r_scr[...] = jnp.zeros_like(r_scr)
            s_scr[...] = jnp.zeros_like(s_scr)

        gla_k = _gla_consts(tg)

        def ret_tile(j, carry):
            rows = _tile_rows(j, tr)
            cosv, sinv = _tile_rotary(rot_in_ref, rot_tile_ref)
            for h in range(RET_HEADS):
                cols = slice(h * RET_D, (h + 1) * RET_D)
                qr = _rotate(rq_ref[rows, cols], cosv, sinv) * scale_r
                kr = _rotate(rk_ref[rows, cols], cosv, sinv)
                vb = rv_ref[rows, cols].astype(BF16)
                qb, kb = qr.astype(BF16), kr.astype(BF16)
                qrb_ref[rows, cols] = qb
                krb_ref[rows, cols] = kb
                p = _dot(qb, kb, NT) * dec_ref[h]
                rp = r_scr[cols, :]
                o = _dot(p.astype(BF16), vb) + _dot((qr * qkd_ref[h]).astype(BF16), rp.astype(BF16))
                rst_ref[j, cols, :] = rp
                r_scr[cols, :] = gammas[h] * rp + _dot((kr * qkd_ref[RET_HEADS + h]).astype(BF16), vb, TN)
                oraw_ref[rows, cols] = o
                oc = o - jnp.mean(o, axis=-1, keepdims=True)
                n = oc * lax.rsqrt(jnp.mean(oc * oc, axis=-1, keepdims=True) + LN_EPS)
                g = rg_ref[rows, cols]
                mix_ref[rows, cols] = (n * rnw_ref[:, cols] * (g * _sigmoid(g))).astype(BF16)
            return carry

        def gla_tile(j, carry):
            k = gla_k
            tl = tg
            rows = _tile_rows(j, tg)
            _, b, bl, ep, em = _gla_gates(glr_ref[rows, :], gw_ref[...], gb_ref[...], k["ltri"], tl)
            qs = gq_ref[rows, :] * scale_g
            kk = gk_ref[rows, :]
            x_all = _dot(_stack_heads(qs * ep, k["hmask"]).astype(BF16), (kk * em).astype(BF16), NT)
            y_all = _dot(_stack_heads(qs * em, k["hmask"]).astype(BF16), (kk * ep).astype(BF16), NT)
            a_all = jnp.where(k["lower"], x_all, jnp.where(k["upper"], y_all, 0.0)).astype(BF16)
            st = s_scr[...]
            oq = _dot(_stack_heads(qs * jnp.exp(b), k["hmask"]).astype(BF16), st.astype(BF16), NT)
            kg = kk * jnp.exp(bl - b)
            sst_ref[j] = st
            st_new = st * jnp.exp(bl)
            for h in range(GLA_HEADS):
                cols = slice(h * GLA_DV, (h + 1) * GLA_DV)
                hr = slice(h * tl, (h + 1) * tl)
                vb = gv_ref[rows, cols].astype(BF16)
                o = _dot(a_all[hr, :], vb) + oq[hr, :]
                st_new = st_new + _dot(vb, (kg * k["hmask"][h]).astype(BF16), TN)
                ocols = slice(RET_HEADS * RET_D + h * GLA_DV, RET_HEADS * RET_D + (h + 1) * GLA_DV)
                oraw_ref[rows, ocols] = o
                n = o * lax.rsqrt(jnp.mean(o * o, axis=-1, keepdims=True) + LN_EPS)
                g = gg_ref[rows, cols]
                mix_ref[rows, ocols] = (n * gnw_ref[:, cols] * (g * _sigmoid(g))).astype(BF16)
            s_scr[...] = st_new
            return carry

        _for_tiles(tc // tr, ret_tile)
        _for_tiles(tc // tg, gla_tile)

    def col(width, off):
        return pl.BlockSpec((tc, width), lambda i, o=off // width: (i, o))

    fix = lambda i: (0, 0)
    fix3 = lambda i: (0, 0, 0)
    dec, qkd, rot_in, rot_tile = tables
    in_specs = [col(512, OFF_RQ), col(512, OFF_RK), col(512, OFF_RV), col(512, OFF_RG),
                col(256, OFF_GQ), col(256, OFF_GK), col(512, OFF_GV), col(512, OFF_GG), col(128, OFF_GLR),
                pl.BlockSpec(dec.shape, fix3), pl.BlockSpec(qkd.shape, fix3), pl.BlockSpec(rot_in.shape, fix3),
                pl.BlockSpec((1, 8, 2 * RET_D), lambda i: (i, 0, 0)),
                pl.BlockSpec((128, GLA_KW), fix), pl.BlockSpec((1, GLA_KW), fix),
                pl.BlockSpec((1, 512), fix), pl.BlockSpec((1, 512), fix)]
    half = pl.BlockSpec((tc, RET_HEADS * RET_D), lambda i: (i, 0))
    out_specs = (pl.BlockSpec((tc, D_MODEL), lambda i: (i, 0)), pl.BlockSpec((tc, D_MODEL), lambda i: (i, 0)),
                 half, half,
                 pl.BlockSpec((tc // tr, RET_HEADS * RET_D, RET_D), lambda i: (i, 0, 0)),
                 pl.BlockSpec((tc // tg, GLA_DV, GLA_KW), lambda i: (i, 0, 0)))
    out_shape = (jax.ShapeDtypeStruct((t, D_MODEL), BF16), jax.ShapeDtypeStruct((t, D_MODEL), F32),
                 jax.ShapeDtypeStruct((t, RET_HEADS * RET_D), BF16), jax.ShapeDtypeStruct((t, RET_HEADS * RET_D), BF16),
                 jax.ShapeDtypeStruct((t // tr, RET_HEADS * RET_D, RET_D), F32),
                 jax.ShapeDtypeStruct((t // tg, GLA_DV, GLA_KW), F32))
    return pl.pallas_call(
        body, name="mixer_fwd", grid=(nsteps,), in_specs=in_specs, out_specs=out_specs, out_shape=out_shape,
        scratch_shapes=[pltpu.VMEM((RET_HEADS * RET_D, RET_D), F32), pltpu.VMEM((GLA_DV, GLA_KW), F32)],
        compiler_params=pltpu.CompilerParams(dimension_semantics=("arbitrary",), vmem_limit_bytes=V7X_VMEM_LIMIT),
    )(*([proj] * 9), dec, qkd, rot_in, rot_tile, gw_pad, gb, rnw, gnw)


def _mid_fwd(mixed, x2d, target, vecs, w_out_b, w1_b, w2_b):
    t = x2d.shape[0]
    tm = min(ROW_TILE, t)

    def body(mix_ref, x_ref, tgt_ref, v_ref, wo_hbm, w1_hbm, w2_hbm,
             m_ref, x1n_ref, rstd_ref, u2_ref, a_ref, df_ref, dh2_ref, acc_ref, wo, w1, w2, sem):
        first = pl.program_id(0) == 0
        _load_resident(first, [(wo_hbm, wo), (w1_hbm, w1), (w2_hbm, w2)], sem)

        @pl.when(first)
        def _():
            acc_ref[...] = jnp.zeros_like(acc_ref)

        gate1, sc2p, sh2, gate2 = v_ref[0:1, :], v_ref[1:2, :], v_ref[2:3, :], v_ref[3:4, :]
        l1w, l1b, l2w, l2b = v_ref[4:5, :], v_ref[5:6, :], v_ref[6:7, :], v_ref[7:8, :]
        m = _dot(mix_ref[...], wo[...])
        m_ref[...] = m.astype(BF16)
        x1n, rstd1 = _ln_stats(ALPHA * x_ref[...] + gate1 * m)
        x1n_ref[...] = x1n
        rstd_ref[...] = rstd1
        x1 = x1n * l1w + l1b
        xh1, _ = _ln_stats(x1)
        u2 = (xh1 * sc2p + sh2).astype(BF16)
        u2_ref[...] = u2
        f = jnp.zeros((tm, D_MODEL), F32)
        for j in range(N_DEV):
            cols = slice(j * FF_COLS, (j + 1) * FF_COLS)
            a = _dot(u2, w1[j])
            a_ref[:, cols] = a.astype(BF16)
            r = jnp.maximum(a, 0.0)
            f = f + _dot((r * r).astype(BF16), w2[cols, :])
        yh, rstd2 = _ln_stats(ALPHA * x1 + gate2 * f)
        e = yh * l2w + l2b - tgt_ref[...]
        dy = e * (1.0 / D_MODEL)
        dh2 = _ln_bwd(dy * l2w, yh, rstd2)
        dh2_ref[...] = dh2
        df_ref[...] = (dh2 * gate2).astype(BF16)
        acc_ref[0:1, :] += jnp.sum(dy * yh, axis=0, keepdims=True)
        acc_ref[1:2, :] += jnp.sum(dy, axis=0, keepdims=True)
        acc_ref[2:3, :] += jnp.sum(dh2 * f, axis=0, keepdims=True)
        acc_ref[3:4, :] += jnp.sum(e * e, axis=0, keepdims=True) * (0.5 / D_MODEL)

    row = lambda i: (i, 0)
    fix = lambda i: (0, 0)
    hbm = pl.BlockSpec(memory_space=pl.ANY)
    return pl.pallas_call(
        body, name="mid_fwd", grid=(t // tm,),
        in_specs=[pl.BlockSpec((tm, D_MODEL), row), pl.BlockSpec((tm, D_MODEL), row), pl.BlockSpec((tm, D_MODEL), row),
                  pl.BlockSpec((8, D_MODEL), fix), hbm, hbm, hbm],
        out_specs=(pl.BlockSpec((tm, D_MODEL), row), pl.BlockSpec((tm, D_MODEL), row), pl.BlockSpec((tm, 1), row),
                   pl.BlockSpec((tm, D_MODEL), row), pl.BlockSpec((tm, D_FF), row), pl.BlockSpec((tm, D_MODEL), row),
                   pl.BlockSpec((tm, D_MODEL), row), pl.BlockSpec((8, D_MODEL), fix)),
        out_shape=(jax.ShapeDtypeStruct((t, D_MODEL), BF16), jax.ShapeDtypeStruct((t, D_MODEL), F32),
                   jax.ShapeDtypeStruct((t, 1), F32), jax.ShapeDtypeStruct((t, D_MODEL), BF16),
                   jax.ShapeDtypeStruct((t, D_FF), BF16), jax.ShapeDtypeStruct((t, D_MODEL), BF16),
                   jax.ShapeDtypeStruct((t, D_MODEL), F32), jax.ShapeDtypeStruct((8, D_MODEL), F32)),
        scratch_shapes=[pltpu.VMEM((D_MODEL, D_MODEL), BF16), pltpu.VMEM((N_DEV, D_MODEL, FF_COLS), BF16),
                        pltpu.VMEM((D_FF, D_MODEL), BF16), pltpu.SemaphoreType.DMA((3,))],
        compiler_params=pltpu.CompilerParams(dimension_semantics=("arbitrary",), vmem_limit_bytes=V7X_VMEM_LIMIT),
    )(mixed, x2d, target, vecs, w_out_b, w1_b, w2_b)


def _ffn_bwd(df, a, dh2, x1n, rstd1, m, vecs, w_out_b, w1_b, w2_b):
    t = x1n.shape[0]
    tm = min(ROW_TILE, t)

    def body(df_ref, a_ref, dh2_ref, x1n_ref, rstd_ref, m_ref, v_ref, wo_hbm, w1_hbm, w2_hbm,
             da_ref, dm_ref, dmix_ref, dxa_ref, acc_ref, wo, w1, w2, sem):
        first = pl.program_id(0) == 0
        _load_resident(first, [(wo_hbm, wo), (w1_hbm, w1), (w2_hbm, w2)], sem)

        @pl.when(first)
        def _():
            acc_ref[...] = jnp.zeros_like(acc_ref)

        gate1, sc2p, l1w, l1b = v_ref[0:1, :], v_ref[1:2, :], v_ref[2:3, :], v_ref[3:4, :]
        df = df_ref[...]
        du2 = jnp.zeros((tm, D_MODEL), F32)
        for j in range(N_DEV):
            cols = slice(j * FF_COLS, (j + 1) * FF_COLS)
            dr2 = _dot(df, w2[cols, :], NT)
            da = (dr2 * (2.0 * jnp.maximum(a_ref[:, cols].astype(F32), 0.0))).astype(BF16)
            da_ref[:, cols] = da
            du2 = du2 + _dot(da, w1[j], NT)
        x1n = x1n_ref[...]
        xh1, rstd0 = _ln_stats(x1n * l1w + l1b)
        dx1 = ALPHA * dh2_ref[...] + _ln_bwd(du2 * sc2p, xh1, rstd0)
        dh1 = _ln_bwd(dx1 * l1w, x1n, rstd_ref[...])
        dxa_ref[...] = ALPHA * dh1
        dm = (dh1 * gate1).astype(BF16)
        dm_ref[...] = dm
        dmix_ref[...] = _dot(dm, wo[...], NT)
        acc_ref[0:1, :] += jnp.sum(du2 * xh1, axis=0, keepdims=True)
        acc_ref[1:2, :] += jnp.sum(du2, axis=0, keepdims=True)
        acc_ref[2:3, :] += jnp.sum(dx1 * x1n, axis=0, keepdims=True)
        acc_ref[3:4, :] += jnp.sum(dx1, axis=0, keepdims=True)
        acc_ref[4:5, :] += jnp.sum(dh1 * m_ref[...].astype(F32), axis=0, keepdims=True)

    row = lambda i: (i, 0)
    fix = lambda i: (0, 0)
    hbm = pl.BlockSpec(memory_space=pl.ANY)
    return pl.pallas_call(
        body, name="ffn_bwd", grid=(t // tm,),
        in_specs=[pl.BlockSpec((tm, D_MODEL), row), pl.BlockSpec((tm, D_FF), row), pl.BlockSpec((tm, D_MODEL), row),
                  pl.BlockSpec((tm, D_MODEL), row), pl.BlockSpec((tm, 1), row), pl.BlockSpec((tm, D_MODEL), row),
                  pl.BlockSpec((8, D_MODEL), fix), hbm, hbm, hbm],
        out_specs=(pl.BlockSpec((tm, D_FF), row), pl.BlockSpec((tm, D_MODEL), row), pl.BlockSpec((tm, D_MODEL), row),
                   pl.BlockSpec((tm, D_MODEL), row), pl.BlockSpec((8, D_MODEL), fix)),
        out_shape=(jax.ShapeDtypeStruct((t, D_FF), BF16), jax.ShapeDtypeStruct((t, D_MODEL), BF16),
                   jax.ShapeDtypeStruct((t, D_MODEL), F32), jax.ShapeDtypeStruct((t, D_MODEL), F32),
                   jax.ShapeDtypeStruct((8, D_MODEL), F32)),
        scratch_shapes=[pltpu.VMEM((D_MODEL, D_MODEL), BF16), pltpu.VMEM((N_DEV, D_MODEL, FF_COLS), BF16),
                        pltpu.VMEM((D_FF, D_MODEL), BF16), pltpu.SemaphoreType.DMA((3,))],
        compiler_params=pltpu.CompilerParams(dimension_semantics=("arbitrary",), vmem_limit_bytes=V7X_VMEM_LIMIT),
    )(df, a, dh2, x1n, rstd1, m, vecs, w_out_b, w1_b, w2_b)


def _matmul_tn(lhs, rhs, tmm, tn, tk, name, relu_sq=False, col_slab=None):
    t, mm = lhs.shape
    nn = rhs.shape[1]
    tk = min(tk, t)
    nk = t // tk

    def body(l_ref, r_ref, o_ref, acc):
        kk = pl.program_id(2)

        @pl.when(kk == 0)
        def _():
            acc[...] = jnp.zeros_like(acc)

        l = l_ref[...]
        if relu_sq:
            lf = jnp.maximum(l.astype(F32), 0.0)
            l = (lf * lf).astype(BF16)
        acc[...] += _dot(l, r_ref[...], TN)

        @pl.when(kk == nk - 1)
        def _():
            if col_slab is None:
                o_ref[...] = acc[...].astype(o_ref.dtype)
            else:
                for s in range(tn // col_slab):
                    o_ref[s] = acc[:, s * col_slab:(s + 1) * col_slab].astype(o_ref.dtype)

    if col_slab is None:
        out_spec = pl.BlockSpec((tmm, tn), lambda i, j, k: (i, j))
        out_shape = jax.ShapeDtypeStruct((mm, nn), BF16)
    else:
        out_spec = pl.BlockSpec((tn // col_slab, tmm, col_slab), lambda i, j, k: (j, i, 0))
        out_shape = jax.ShapeDtypeStruct((nn // col_slab, mm, col_slab), BF16)
    return pl.pallas_call(
        body, name=name, grid=(mm // tmm, nn // tn, nk),
        in_specs=[pl.BlockSpec((tk, tmm), lambda i, j, k: (k, i)), pl.BlockSpec((tk, tn), lambda i, j, k: (k, j))],
        out_specs=out_spec,
        out_shape=out_shape,
        scratch_shapes=[pltpu.VMEM((tmm, tn), F32)],
        compiler_params=pltpu.CompilerParams(dimension_semantics=("arbitrary", "arbitrary", "arbitrary"),
                                             vmem_limit_bytes=V7X_VMEM_LIMIT),
    )(lhs, rhs)


def _mixer_bwd(dmix, proj, qrb, krb, oraw, tables, rst, sst, gw_pad, gb, rnw, gnw):
    t = proj.shape[0]
    tc = min(MIX_TILE, t)
    tr, tg = tc, min(GLA_SUB, tc)
    nsteps = t // tc
    scale_r = RET_D ** -0.5
    scale_g = GLA_DK ** -0.5
    gammas = _tile_gammas(tr)

    def body(dmix_ref, qrb_ref, krb_ref, rv_ref, rg_ref, gq_ref, gk_ref, gv_ref, gg_ref, glr_ref, oraw_ref,
             dec_ref, qkd_ref, rot_in_ref, rot_tile_ref, rst_ref, sst_ref, gw_ref, gb_ref, rnw_ref, gnw_ref,
             dproj_ref, dgw_ref, dvec_ref, dr_scr, ds_scr):
        @pl.when(pl.program_id(0) == 0)
        def _():
            dr_scr[...] = jnp.zeros_like(dr_scr)
            ds_scr[...] = jnp.zeros_like(ds_scr)
            dgw_ref[...] = jnp.zeros_like(dgw_ref)
            dvec_ref[...] = jnp.zeros_like(dvec_ref)

        gla_k = _gla_consts(tg)
        last_row = lax.broadcasted_iota(jnp.int32, (tg, GLA_KW), 0) == tg - 1

        def ret_tile(jj, carry):
            j = tc // tr - 1 - jj
            rows = _tile_rows(j, tr)
            cosv, sinv = _tile_rotary(rot_in_ref, rot_tile_ref)
            for h in range(RET_HEADS):
                cols = slice(h * RET_D, (h + 1) * RET_D)
                o = oraw_ref[rows, cols]
                g = rg_ref[rows, cols]
                w = rnw_ref[:, cols]
                dout = dmix_ref[rows, cols]
                oc = o - jnp.mean(o, axis=-1, keepdims=True)
                inv = lax.rsqrt(jnp.mean(oc * oc, axis=-1, keepdims=True) + LN_EPS)
                n = oc * inv
                sg = _sigmoid(g)
                sil = g * sg
                dn = dout * w * sil
                dvec_ref[0:1, cols] += jnp.sum(dout * n * sil, axis=0, keepdims=True)
                dproj_ref[rows, OFF_RG + h * RET_D:OFF_RG + (h + 1) * RET_D] = (
                    dout * n * w * (sg * (1.0 + g * (1.0 - sg)))).astype(BF16)
                doc = inv * (dn - n * jnp.mean(dn * n, axis=-1, keepdims=True))
                do = doc - jnp.mean(doc, axis=-1, keepdims=True)

                qb, kb = qrb_ref[rows, cols], krb_ref[rows, cols]
                qr, kr = qb.astype(F32), kb.astype(F32)
                vb = rv_ref[rows, cols].astype(BF16)
                dob = do.astype(BF16)
                qd, kd = qkd_ref[h], qkd_ref[RET_HEADS + h]
                p = _dot(qb, kb, NT) * dec_ref[h]
                rp = rst_ref[j, cols, :].astype(BF16)
                dr = dr_scr[cols, :]
                drb = dr.astype(BF16)
                dpb = (_dot(dob, vb, NT) * dec_ref[h]).astype(BF16)
                dqr = _dot(dpb, kb) + _dot(dob, rp, NT) * qd
                dkr = _dot(dpb, qb, TN) + _dot(vb, drb, NT) * kd
                dv = _dot(p.astype(BF16), dob, TN) + _dot((kr * kd).astype(BF16), drb)
                dr_scr[cols, :] = gammas[h] * dr + _dot((qr * qd).astype(BF16), dob, TN)
                dproj_ref[rows, OFF_RQ + h * RET_D:OFF_RQ + (h + 1) * RET_D] = (
                    _rotate_t(dqr, cosv, sinv) * scale_r).astype(BF16)
                dproj_ref[rows, OFF_RK + h * RET_D:OFF_RK + (h + 1) * RET_D] = _rotate_t(dkr, cosv, sinv).astype(BF16)
                dproj_ref[rows, OFF_RV + h * RET_D:OFF_RV + (h + 1) * RET_D] = dv.astype(BF16)
            return carry

        def gla_tile(jj, carry):
            k = gla_k
            tl = tg
            j = tc // tg - 1 - jj
            rows = _tile_rows(j, tg)
            glr = glr_ref[rows, :]
            z, b, bl, ep, em = _gla_gates(glr, gw_ref[...], gb_ref[...], k["ltri"], tl)
            qs = gq_ref[rows, :] * scale_g
            kk = gk_ref[rows, :]
            eb = jnp.exp(b)
            ekb = jnp.exp(bl - b)
            ebl = jnp.exp(bl)
            ql, qu, kl, ku = qs * ep, qs * em, kk * em, kk * ep
            qg, kg = qs * eb, kk * ekb
            qlm = _stack_heads(ql, k["hmask"]).astype(BF16)
            qum = _stack_heads(qu, k["hmask"]).astype(BF16)
            klb, kub = kl.astype(BF16), ku.astype(BF16)
            a_all = jnp.where(k["lower"], _dot(qlm, klb, NT),
                              jnp.where(k["upper"], _dot(qum, kub, NT), 0.0)).astype(BF16)
            st = sst_ref[j]
            stb = st.astype(BF16)
            ds = ds_scr[...]
            dsb = ds.astype(BF16)
            ds_new = ds * ebl
            da_parts = []
            dqg = jnp.zeros((tl, GLA_KW), F32)
            dkg = jnp.zeros((tl, GLA_KW), F32)
            for h in range(GLA_HEADS):
                cols = slice(h * GLA_DV, (h + 1) * GLA_DV)
                hr = slice(h * tl, (h + 1) * tl)
                ocols = slice(RET_HEADS * RET_D + h * GLA_DV, RET_HEADS * RET_D + (h + 1) * GLA_DV)
                o = oraw_ref[rows, ocols]
                g = gg_ref[rows, cols]
                w = gnw_ref[:, cols]
                dout = dmix_ref[rows, ocols]
                inv = lax.rsqrt(jnp.mean(o * o, axis=-1, keepdims=True) + LN_EPS)
                n = o * inv
                sg = _sigmoid(g)
                sil = g * sg
                dn = dout * w * sil
                dvec_ref[1:2, cols] += jnp.sum(dout * n * sil, axis=0, keepdims=True)
                dproj_ref[rows, OFF_GG + h * GLA_DV:OFF_GG + (h + 1) * GLA_DV] = (
                    dout * n * w * (sg * (1.0 + g * (1.0 - sg)))).astype(BF16)
                dob = (inv * (dn - n * jnp.mean(dn * n, axis=-1, keepdims=True))).astype(BF16)
                vb = gv_ref[rows, cols].astype(BF16)
                mh = k["hmask"][h]
                da_parts.append(_dot(dob, vb, NT))
                dv = _dot(a_all[hr, :], dob, TN) + _dot((kg * mh).astype(BF16), dsb, NT)
                dproj_ref[rows, OFF_GV + h * GLA_DV:OFF_GV + (h + 1) * GLA_DV] = dv.astype(BF16)
                dkg = dkg + mh * _dot(vb, dsb)
                dqg = dqg + mh * _dot(dob, stb)
                ds_new = ds_new + _dot(dob, (qg * mh).astype(BF16), TN)
            da_all = jnp.concatenate(da_parts, axis=0)
            dal = jnp.where(k["lower"], da_all, 0.0).astype(BF16)
            dau = jnp.where(k["upper"], da_all, 0.0).astype(BF16)
            dqlm = _dot(dal, klb)
            dqum = _dot(dau, kub)
            dql = jnp.zeros((tl, GLA_KW), F32)
            dqu = jnp.zeros((tl, GLA_KW), F32)
            for h in range(GLA_HEADS):
                hr = slice(h * tl, (h + 1) * tl)
                dql = dql + k["hmask"][h] * dqlm[hr, :]
                dqu = dqu + k["hmask"][h] * dqum[hr, :]
            dkl = _dot(dal, qlm, TN)
            dku = _dot(dau, qum, TN)
            dbl = (jnp.sum(dkg * kg, axis=0, keepdims=True)
                   + jnp.sum(ds * st, axis=0, keepdims=True) * ebl)
            ds_scr[...] = ds_new
            dqs = dql * ep + dqu * em + dqg * eb
            dk = dkl * em + dku * ep + dkg * ekb
            db = dql * ql - dkl * kl - dqu * qu + dku * ku + dqg * qg - dkg * kg
            db = db + jnp.where(last_row, dbl, 0.0)
            dla = _dot_split(k["utri"], db, NN, a_exact=True)
            dz = dla * (1.0 / GATE_TAU) * _sigmoid(-z)
            dvec_ref[2:3, 0:GLA_KW] += jnp.sum(dz, axis=0, keepdims=True)
            dgw_ref[...] += _dot_split(glr, dz, TN)
            dproj_ref[rows, OFF_GLR:OFF_GLR + 128] = _dot(dz.astype(BF16), gw_ref[...].astype(BF16), NT).astype(BF16)
            dproj_ref[rows, OFF_GQ:OFF_GQ + GLA_KW] = (dqs * scale_g).astype(BF16)
            dproj_ref[rows, OFF_GK:OFF_GK + GLA_KW] = dk.astype(BF16)
            return carry

        _for_tiles(tc // tr, ret_tile)
        _for_tiles(tc // tg, gla_tile)

    rev = lambda i: (nsteps - 1 - i, 0)

    def col(width, off):
        return pl.BlockSpec((tc, width), lambda i, o=off // width: (nsteps - 1 - i, o))

    fix = lambda i: (0, 0)
    fix3 = lambda i: (0, 0, 0)
    dec, qkd, rot_in, rot_tile = tables
    half = pl.BlockSpec((tc, RET_HEADS * RET_D), rev)
    in_specs = [pl.BlockSpec((tc, D_MODEL), rev), half, half, col(512, OFF_RV), col(512, OFF_RG),
                col(256, OFF_GQ), col(256, OFF_GK), col(512, OFF_GV), col(512, OFF_GG), col(128, OFF_GLR),
                pl.BlockSpec((tc, D_MODEL), rev),
                pl.BlockSpec(dec.shape, fix3), pl.BlockSpec(qkd.shape, fix3), pl.BlockSpec(rot_in.shape, fix3),
                pl.BlockSpec((1, 8, 2 * RET_D), lambda i: (nsteps - 1 - i, 0, 0)),
                pl.BlockSpec((tc // tr, RET_HEADS * RET_D, RET_D), lambda i: (nsteps - 1 - i, 0, 0)),
                pl.BlockSpec((tc // tg, GLA_DV, GLA_KW), lambda i: (nsteps - 1 - i, 0, 0)),
                pl.BlockSpec((128, GLA_KW), fix), pl.BlockSpec((1, GLA_KW), fix),
                pl.BlockSpec((1, 512), fix), pl.BlockSpec((1, 512), fix)]
    out_specs = (pl.BlockSpec((tc, D_IN_PAD), rev), pl.BlockSpec((128, GLA_KW), fix), pl.BlockSpec((8, 512), fix))
    out_shape = (jax.ShapeDtypeStruct((t, D_IN_PAD), BF16), jax.ShapeDtypeStruct((128, GLA_KW), F32),
                 jax.ShapeDtypeStruct((8, 512), F32))
    return pl.pallas_call(
        body, name="mixer_bwd", grid=(nsteps,), in_specs=in_specs, out_specs=out_specs, out_shape=out_shape,
        scratch_shapes=[pltpu.VMEM((RET_HEADS * RET_D, RET_D), F32), pltpu.VMEM((GLA_DV, GLA_KW), F32)],
        compiler_params=pltpu.CompilerParams(dimension_semantics=("arbitrary",), vmem_limit_bytes=V7X_VMEM_LIMIT),
    )(dmix, qrb, krb, *([proj] * 7), oraw, dec, qkd, rot_in, rot_tile, rst, sst, gw_pad, gb, rnw, gnw)


def _inproj_bwd(dproj, x2d, dxa, sc1p, w_in_t):
    t = x2d.shape[0]
    tm = min(PROJ_TILE, t)

    def body(dp_ref, x_ref, dxa_ref, sc_ref, w_hbm, gx_ref, acc_ref, w_vmem, sem):
        first = pl.program_id(0) == 0
        _load_w_in_t(first, w_hbm, w_vmem, sem)

        @pl.when(first)
        def _():
            acc_ref[...] = jnp.zeros_like(acc_ref)

        du = _dot(dp_ref[...], w_vmem[...])
        xh, rstd = _ln_stats(x_ref[...])
        gx_ref[...] = dxa_ref[...] + _ln_bwd(du * sc_ref[...], xh, rstd)
        acc_ref[0:1, :] += jnp.sum(du * xh, axis=0, keepdims=True)
        acc_ref[1:2, :] += jnp.sum(du, axis=0, keepdims=True)

    row = lambda i: (i, 0)
    fix = lambda i: (0, 0)
    return pl.pallas_call(
        body, name="inproj_bwd", grid=(t // tm,),
        in_specs=[pl.BlockSpec((tm, D_IN_PAD), row), pl.BlockSpec((tm, D_MODEL), row), pl.BlockSpec((tm, D_MODEL), row),
                  pl.BlockSpec((1, D_MODEL), fix), pl.BlockSpec(memory_space=pl.ANY)],
        out_specs=(pl.BlockSpec((tm, D_MODEL), row), pl.BlockSpec((8, D_MODEL), fix)),
        out_shape=(jax.ShapeDtypeStruct((t, D_MODEL), F32), jax.ShapeDtypeStruct((8, D_MODEL), F32)),
        scratch_shapes=[pltpu.VMEM((D_IN_PAD, D_MODEL), BF16), pltpu.SemaphoreType.DMA((1,))],
        compiler_params=pltpu.CompilerParams(dimension_semantics=("arbitrary",), vmem_limit_bytes=V7X_VMEM_LIMIT),
    )(dproj, x2d, dxa, sc1p, w_in_t)


def _adam_math(w, g, m, v):
    m = ADAM_B1 * m + (1.0 - ADAM_B1) * g
    v = ADAM_B2 * v + (1.0 - ADAM_B2) * (g * g)
    m_hat = m / (1.0 - ADAM_B1 ** ADAM_STEP)
    v_hat = v / (1.0 - ADAM_B2 ** ADAM_STEP)
    delta = -ADAM_LR * (m_hat / (jnp.sqrt(v_hat) + ADAM_EPS) + ADAM_WD * w)
    return delta, m, v


def _adamw(w, gparts, m, v, name):
    nparts, rows, cols = gparts.shape
    tr = rows
    for cand in (512, 256, 128, 64, 32, 16, 8):
        if rows % cand == 0:
            tr = cand
            break

    def body(w_ref, g_ref, m_ref, v_ref, go_ref, d_ref, mo_ref, vo_ref):
        g = g_ref[0].astype(F32)
        for p in range(1, nparts):
            g = g + g_ref[p].astype(F32)
        delta, mn, vn = _adam_math(w_ref[...], g, m_ref[...], v_ref[...])
        go_ref[...] = g
        d_ref[...] = delta
        mo_ref[...] = mn
        vo_ref[...] = vn

    blk = pl.BlockSpec((tr, cols), lambda i: (i, 0))
    shp = jax.ShapeDtypeStruct((rows, cols), F32)
    return pl.pallas_call(
        body, name=name, grid=(rows // tr,),
        in_specs=[blk, pl.BlockSpec((nparts, tr, cols), lambda i: (0, i, 0)), blk, blk],
        out_specs=(blk, blk, blk, blk), out_shape=(shp, shp, shp, shp),
        compiler_params=pltpu.CompilerParams(dimension_semantics=("arbitrary",), vmem_limit_bytes=V7X_VMEM_LIMIT),
    )(w, gparts, m, v)


def _small_reduce(gathered, gathered_gw, c_all, dmod_cols):
    def body(g_ref, gw_ref, c_ref, dm_ref, sum_ref, gwsum_ref, gb_ref, gwa_ref):
        s = g_ref[0]
        sw = gw_ref[0]
        for p in range(1, N_DEV):
            s = s + g_ref[p]
            sw = sw + gw_ref[p]
        sum_ref[...] = s
        gwsum_ref[...] = sw
        for i in range(6):
            gb_ref[:, i * D_MODEL:(i + 1) * D_MODEL] = s[i:i + 1, :]
        cc = c_ref[...]
        gwa_ref[...] = _dot(cc * _sigmoid(cc), dm_ref[...], TN, HIGHEST)

    vm = pl.BlockSpec(memory_space=pltpu.VMEM)
    return pl.pallas_call(
        body, name="small_reduce",
        out_shape=(jax.ShapeDtypeStruct(gathered.shape[1:], F32), jax.ShapeDtypeStruct(gathered_gw.shape[1:], F32),
                   jax.ShapeDtypeStruct((1, 6 * D_MODEL), F32), jax.ShapeDtypeStruct((D_MODEL, ADA_COLS), F32)),
        in_specs=[vm] * 4, out_specs=(vm, vm, vm, vm),
        compiler_params=pltpu.CompilerParams(vmem_limit_bytes=V7X_VMEM_LIMIT),
    )(gathered, gathered_gw, c_all, dmod_cols)


SMR_LN1W, SMR_LN1B, SMR_LN2W, SMR_LN2B, SMR_NORMS, SMR_MISC = 6, 7, 8, 9, 10, 11


def _adamw_small(gsum, g_b_ada, g_ggw, params, moms, vels):
    n = len(params)

    def body(*refs):
        gsum_ref, gb_ref, gw_ref = refs[:3]
        w_refs, m_refs, v_refs = refs[3:3 + n], refs[3 + n:3 + 2 * n], refs[3 + 2 * n:3 + 3 * n]
        outs = refs[3 + 3 * n:]
        g_refs, d_refs, mo_refs, vo_refs = outs[:n - 1], outs[n - 1:2 * n - 1], outs[2 * n - 1:3 * n - 1], outs[3 * n - 1:]
        grads = [gb_ref[...],
                 gsum_ref[SMR_NORMS:SMR_NORMS + 1, 0:512],
                 gsum_ref[SMR_MISC:SMR_MISC + 1, 0:GLA_KW],
                 gsum_ref[SMR_NORMS:SMR_NORMS + 1, 512:1024],
                 gsum_ref[SMR_LN1W:SMR_LN1W + 1, :], gsum_ref[SMR_LN1B:SMR_LN1B + 1, :],
                 gsum_ref[SMR_LN2W:SMR_LN2W + 1, :], gsum_ref[SMR_LN2B:SMR_LN2B + 1, :],
                 gw_ref[...]]
        for i in range(n):
            delta, mn, vn = _adam_math(w_refs[i][...], grads[i], m_refs[i][...], v_refs[i][...])
            if i < n - 1:
                g_refs[i][...] = grads[i]
            d_refs[i][...] = delta
            mo_refs[i][...] = mn
            vo_refs[i][...] = vn

    vm = pl.BlockSpec(memory_space=pltpu.VMEM)
    shapes = [jax.ShapeDtypeStruct(p.shape, F32) for p in params]
    n_in = 3 + 3 * n
    out_shape = tuple(shapes[:n - 1] + shapes * 3)
    return pl.pallas_call(
        body, name="adamw_small", out_shape=out_shape,
        in_specs=[vm] * n_in, out_specs=tuple([vm] * len(out_shape)),
        compiler_params=pltpu.CompilerParams(vmem_limit_bytes=V7X_VMEM_LIMIT),
    )(gsum, g_b_ada, g_ggw, *params, *moms, *vels)


def kernel(x, c, w_ada, b_ada, w_in, ret_norm_w, gla_gate_w, gla_gate_b, gla_norm_w, w_out, ln1_w, ln1_b, w_ff1, w_ff2, ln2_w, ln2_b, loss_target, m_w_ada, m_b_ada, m_w_in, m_ret_norm_w, m_gla_gate_w, m_gla_gate_b, m_gla_norm_w, m_w_out, m_ln1_w, m_ln1_b, m_w_ff1, m_w_ff2, m_ln2_w, m_ln2_b, v_w_ada, v_b_ada, v_w_in, v_ret_norm_w, v_gla_gate_w, v_gla_gate_b, v_gla_norm_w, v_w_out, v_ln1_w, v_ln1_b, v_w_ff1, v_w_ff2, v_ln2_w, v_ln2_b):
    t = x.shape[1]
    xi, yi, ci = _my_coords()
    me = 4 * xi + 2 * yi + ci
    x2d = x[0]
    tgt = loss_target[0]

    c_ext = jnp.concatenate([c, gla_gate_w[0].reshape(1, GATE_RANK * GLA_KW // N_DEV)], axis=1)
    b_l = lax.dynamic_slice(b_ada, (0, me * ADA_COLS), (1, ADA_COLS))
    c_all3, mod_all, wi_g, ada_token = _adaln_mod(c_ext, w_ada[0], b_l, w_in[0].T.astype(BF16))

    wg = _exchange_start([(w_out[0] + ada_token[0, 0]).astype(BF16), w_ff1[0].astype(BF16), w_ff2[0].astype(BF16)],
                         True, "wgather_start")

    c_all = c_all3[:, 0, :D_MODEL]
    gate_w = c_all3[:, 0, D_MODEL:].reshape(N_DEV, GATE_RANK, GLA_KW // N_DEV)
    gate_w = gate_w.transpose(1, 0, 2).reshape(GATE_RANK, GLA_KW)
    gw_pad = jnp.zeros((128, GLA_KW), F32).at[:GATE_RANK].set(gate_w)
    mod = lax.dynamic_slice(mod_all, (0, me, 0), (N_DEV, 1, ADA_COLS)).reshape(6, D_MODEL)
    shift1, scale1, gate1, shift2, scale2, gate2 = [mod[i:i + 1] for i in range(6)]

    w_in_t = wi_g.reshape(D_IN, D_MODEL)

    tables = _ret_tables(t, min(MIX_TILE, t))

    sc1p = 1.0 + scale1
    proj, u = _inproj_fwd(x2d, sc1p, shift1 + wg[4][0, 0], w_in_t)
    mixed, oraw, qrb, krb, rst, sst = _mixer_fwd(proj, tables, gw_pad, gla_gate_b, ret_norm_w, gla_norm_w)
    wo_g, w1_b, w2_g = _exchange_wait(*wg[:4], mixed, True, "wgather_wait")
    w_out_b = wo_g.reshape(D_MODEL, D_MODEL)
    w2_b = w2_g.reshape(D_FF, D_MODEL)
    vec_f = jnp.concatenate([gate1, 1.0 + scale2, shift2, gate2, ln1_w, ln1_b, ln2_w, ln2_b], axis=0)
    m, x1n, rstd1, u2, a, df, dh2, acc_f = _mid_fwd(mixed, x2d, tgt, vec_f, w_out_b, w1_b, w2_b)

    vec_b = jnp.concatenate([gate1, 1.0 + scale2, ln1_w, ln1_b, jnp.zeros((4, D_MODEL), F32)], axis=0)
    da, dm, dmix, dxa, acc_b = _ffn_bwd(df, a, dh2, x1n, rstd1, m, vec_b, w_out_b, w1_b, w2_b)
    dw2 = _matmul_tn(a, df, 2048, 1024, 1024, "tn_dw2", relu_sq=True)
    dw1 = _matmul_tn(u2, da, 1024, 2048, 1024, "tn_dw1", col_slab=FF_COLS)
    dwo = _matmul_tn(mixed, dm, 1024, 1024, 1024, "tn_dwout")
    gx = _exchange_start([dwo.reshape(N_DEV, OUT_ROWS, D_MODEL), dw1, dw2.reshape(N_DEV, FF_COLS, D_MODEL)], False,
                         "gradx_start")
    dproj, dgw, dvec = _mixer_bwd(dmix, proj, qrb, krb, oraw, tables, rst, sst, gw_pad,
                                  gla_gate_b + gx[4][0, 0], ret_norm_w, gla_norm_w)
    dwi = _matmul_tn(u, dproj, 1024, D_IN_PAD, 512, "tn_dwin")
    dwi_s = dwi[:, :D_IN].reshape(D_MODEL, N_DEV, IN_COLS).transpose(1, 0, 2)
    gi = _exchange_start([dwi_s], False, "gradin_start")
    grad_x, acc_i = _inproj_bwd(dproj, x2d, dxa, sc1p + gi[4][0, 0], w_in_t)
    r_wo, r_w1, r_w2 = _exchange_wait(*gx[:4], acc_i, False, "gradx_wait")
    r_wi, = _exchange_wait(*gi[:4], acc_i, False, "gradin_wait")

    loss_part = jnp.sum(acc_f[3])
    small = jnp.concatenate([
        acc_i[1:2], acc_i[0:1], acc_b[4:5], acc_b[1:2], acc_b[0:1], acc_f[2:3],
        acc_b[2:3], acc_b[3:4], acc_f[0:1], acc_f[1:2],
        jnp.concatenate([dvec[0:1], dvec[1:2]], axis=1),
        jnp.concatenate([dvec[2:3, :GLA_KW], jnp.full((1, 128), loss_part, F32),
                         jnp.zeros((1, D_MODEL - GLA_KW - 128), F32)], axis=1),
        jnp.zeros((4, D_MODEL), F32)], axis=0)
    small_all, gw_all = _small_gather([small, dgw[:GATE_RANK]])
    dmod_all = small_all[:, :6].reshape(N_DEV, 6 * D_MODEL)
    dmod_cols = lax.dynamic_slice(dmod_all, (0, me * ADA_COLS), (N_DEV, ADA_COLS))
    ssum, gw_sum, g_b_ada, g_w_ada = _small_reduce(small_all, gw_all, c_all, dmod_cols)
    loss = ssum[SMR_MISC, GLA_KW]
    g_ggw = lax.dynamic_slice(gw_sum, (0, me * (GLA_KW // N_DEV)), (GATE_RANK, GLA_KW // N_DEV))[None]

    small_w = [b_ada, ret_norm_w, gla_gate_b, gla_norm_w, ln1_w, ln1_b, ln2_w, ln2_b, gla_gate_w]
    small_m = [m_b_ada, m_ret_norm_w, m_gla_gate_b, m_gla_norm_w, m_ln1_w, m_ln1_b, m_ln2_w, m_ln2_b, m_gla_gate_w]
    small_v = [v_b_ada, v_ret_norm_w, v_gla_gate_b, v_gla_norm_w, v_ln1_w, v_ln1_b, v_ln2_w, v_ln2_b, v_gla_gate_w]
    res = _adamw_small(ssum, g_b_ada, g_ggw, small_w, small_m, small_v)
    small_g = list(res[:8]) + [g_ggw]
    d_small, m_small, v_small = list(res[8:17]), list(res[17:26]), list(res[26:35])

    _, d_w_ada, nm_w_ada, nv_w_ada = _adamw(w_ada[0], g_w_ada[None], m_w_ada[0], v_w_ada[0], "adamw_ada")

    big =[_adamw(w[0], r, m_[0], v_[0], nm) for w, r, m_, v_, nm in (
        (w_in, r_wi, m_w_in, v_w_in, "adamw_in"), (w_out, r_wo, m_w_out, v_w_out, "adamw_out"),
        (w_ff1, r_w1, m_w_ff1, v_w_ff1, "adamw_ff1"), (w_ff2, r_w2, m_w_ff2, v_w_ff2, "adamw_ff2"))]
    g_big, d_big, m_big, v_big = [[b[i][None] for b in big] for i in range(4)]

    def ordered(w_ada_v, small_vals, big_vals):
        b_ada_v, rnw_v, ggb_v, gnw_v, l1w_v, l1b_v, l2w_v, l2b_v, ggw_v = small_vals
        wi_v, wo_v, w1_v, w2_v = big_vals
        return [w_ada_v, b_ada_v, wi_v, rnw_v, ggw_v, ggb_v, gnw_v, wo_v, l1w_v, l1b_v, w1_v, w2_v, l2w_v, l2b_v]

    grads = ordered(g_w_ada[None], small_g, g_big)
    deltas = ordered(d_w_ada[None], d_small, d_big)
    new_m = ordered(nm_w_ada[None], m_small, m_big)
    new_v = ordered(nv_w_ada[None], v_small, v_big)
    return (loss, grad_x[None], *grads, *deltas, *new_m, *new_v)
```

```python
import functools

import numpy as np
import jax
import jax.numpy as jnp
from jax import lax
from jax.experimental import pallas as pl
from jax.experimental.pallas import tpu as pltpu

F32 = jnp.float32
BF16 = jnp.bfloat16
MESH = pl.DeviceIdType.MESH
HIGHEST = lax.Precision.HIGHEST

N_DEV = 8
D_MODEL = 1024
CHUNK = 64
RET_HEADS = 4
RET_D = 128
GLA_HEADS = 4
GLA_DK = 64
GLA_DV = 128
GLA_KW = GLA_HEADS * GLA_DK
GATE_RANK = 16
GATE_TAU = 16.0
D_FF = 4096
LN_EPS = 1e-5
ALPHA = (2.0 * 1) ** 0.25
D_IN = 3600
D_IN_PAD = 3712
ADA_COLS = 6 * D_MODEL // N_DEV
IN_COLS = D_IN // N_DEV
FF_COLS = D_FF // N_DEV
OUT_ROWS = D_MODEL // N_DEV

OFF_RQ, OFF_RK, OFF_RV, OFF_RG = 0, 512, 1024, 1536
OFF_GQ, OFF_GK, OFF_GV, OFF_GG, OFF_GLR = 2048, 2304, 2560, 3072, 3584

ADAM_LR, ADAM_B1, ADAM_B2, ADAM_EPS, ADAM_WD, ADAM_STEP = 0.001, 0.9, 0.999, 1e-08, 0.01, 10

V7X_VMEM_LIMIT = 62 * 1024 * 1024

ROW_TILE = 512
PROJ_TILE = 512
MIX_TILE = 256
GLA_SUB = 128


def _log_gamma(h):
    return float(np.log(np.float32(1.0) - np.float32(2.0) ** np.float32(-5.0 - h)))


def _my_coords():
    return lax.axis_index("x"), lax.axis_index("y"), lax.axis_index("c")


def _flip(v, bit):
    return 1 - v if bit else v


def _peer(k):
    x, y, c = _my_coords()
    px, py, pc = _flip(x, (k >> 2) & 1), _flip(y, (k >> 1) & 1), _flip(c, k & 1)
    return (px, py, pc), 4 * px + 2 * py + pc


def _dot(a, b, dims=(((1,), (0,)), ((), ())), precision=None):
    return lax.dot_general(a, b, dims, precision=precision, preferred_element_type=F32)


NN = (((1,), (0,)), ((), ()))
NT = (((1,), (1,)), ((), ()))
TN = (((0,), (0,)), ((), ()))


def _split_bf16(v, parts):
    out = []
    for _ in range(parts):
        p = v.astype(BF16)
        out.append(p)
        v = v - p.astype(F32)
    return out


def _dot_split(a, b, dims, a_exact=False):
    if a_exact:
        ab = a.astype(BF16)
        return sum(_dot(ab, p, dims) for p in _split_bf16(b, 3))
    a_hi, a_lo = _split_bf16(a, 2)
    b_hi, b_lo = _split_bf16(b, 2)
    return _dot(a_hi, b_hi, dims) + _dot(a_hi, b_lo, dims) + _dot(a_lo, b_hi, dims)


def _sigmoid(x):
    return 1.0 / (1.0 + jnp.exp(-x))


def _ln_stats(x):
    mu = jnp.mean(x, axis=-1, keepdims=True)
    xc = x - mu
    var = jnp.mean(xc * xc, axis=-1, keepdims=True)
    rstd = lax.rsqrt(var + LN_EPS)
    return xc * rstd, rstd


def _ln_bwd(dyh, xh, rstd):
    return rstd * (dyh - jnp.mean(dyh, axis=-1, keepdims=True) - xh * jnp.mean(dyh * xh, axis=-1, keepdims=True))


def _adaln_mod(c_ext, w_ada_l, b_l, w_in_l):
    width = c_ext.shape[1]

    def body(c_ref, w_ref, b_ref, wi_ref, call_ref, mod_ref, wig_ref, token_ref, s1, r1, s2, r2, gs, gr, gl):
        gather = _TwoLevelGather([wi_ref], [wig_ref], gs, gr, gl)
        gather.start()
        token_ref[...] = jnp.zeros_like(token_ref)
        x, y, c = _my_coords()
        me = 4 * x + 2 * y + c
        call_ref[me] = c_ref[...]
        sends = []
        for k in range(1, N_DEV):
            peer, _ = _peer(k)
            cp = pltpu.make_async_remote_copy(c_ref, call_ref.at[me], s1.at[k - 1], r1.at[k - 1],
                                              device_id=peer, device_id_type=MESH)
            cp.start()
            sends.append(cp)
        for k in range(1, N_DEV):
            peer, pid = _peer(k)
            pltpu.make_async_remote_copy(c_ref, call_ref.at[pid], s1.at[k - 1], r1.at[k - 1],
                                         device_id=peer, device_id_type=MESH).wait_recv()
        for cp in sends:
            cp.wait_send()
        row = lax.broadcasted_iota(jnp.int32, (N_DEV, D_MODEL), 0)
        call = jnp.zeros((N_DEV, D_MODEL), F32)
        for j in range(N_DEV):
            call = jnp.where(row == j, jnp.broadcast_to(call_ref[j][:, :D_MODEL], (N_DEV, D_MODEL)), call)
        sc = call * _sigmoid(call)
        mod = _dot(sc, w_ref[...], NN, HIGHEST) + b_ref[...]
        mod_ref[me] = mod
        sends = []
        for k in range(1, N_DEV):
            peer, _ = _peer(k)
            cp = pltpu.make_async_remote_copy(mod_ref.at[me], mod_ref.at[me], s2.at[k - 1], r2.at[k - 1],
                                              device_id=peer, device_id_type=MESH)
            cp.start()
            sends.append(cp)
        for k in range(1, N_DEV):
            peer, pid = _peer(k)
            pltpu.make_async_remote_copy(mod_ref.at[pid], mod_ref.at[pid], s2.at[k - 1], r2.at[k - 1],
                                         device_id=peer, device_id_type=MESH).wait_recv()
        for cp in sends:
            cp.wait_send()
        gather.forward()
        gather.finish()

    vm = pl.BlockSpec(memory_space=pltpu.VMEM)
    hbm = pl.BlockSpec(memory_space=pl.ANY)
    return pl.pallas_call(
        body, name="adaln_mod",
        out_shape=(jax.ShapeDtypeStruct((N_DEV, 1, width), F32),
                   jax.ShapeDtypeStruct((N_DEV, N_DEV, ADA_COLS), F32),
                   jax.ShapeDtypeStruct((N_DEV, *w_in_l.shape), w_in_l.dtype),
                   jax.ShapeDtypeStruct((8, 128), F32)),
        in_specs=[vm, vm, vm, hbm], out_specs=(vm, vm, hbm, vm),
        scratch_shapes=[pltpu.SemaphoreType.DMA((N_DEV - 1,))] * 4
        + [pltpu.SemaphoreType.DMA((7,)), pltpu.SemaphoreType.DMA((7,)), pltpu.SemaphoreType.DMA((1,))],
        compiler_params=pltpu.CompilerParams(vmem_limit_bytes=V7X_VMEM_LIMIT),
    )(c_ext, w_ada_l, b_l, w_in_l)


class _TwoLevelGather:
    def __init__(self, x_refs, out_refs, send_sems, recv_sems, local_sems):
        self.x_refs, self.out_refs = x_refs, out_refs
        self.send_sems, self.recv_sems, self.local_sems = send_sems, recv_sems, local_sems
        x, y, c = _my_coords()
        self.c = c
        self.me, self.sibling = (x, y, c), (x, y, 1 - c)
        self.chips = [(1 - x, y), (x, 1 - y), (1 - x, 1 - y)]

    def _copy(self, a, k, block, to, src=None):
        px, py, pc = block
        slab = self.out_refs[a].at[4 * px + 2 * py + pc]
        return pltpu.make_async_remote_copy(
            src_ref=slab if src is None else src, dst_ref=slab,
            send_sem=self.send_sems.at[7 * a + k], recv_sem=self.recv_sems.at[7 * a + k],
            device_id=to, device_id_type=MESH)

    def _mine(self, a):
        px, py, pc = self.me
        return pltpu.make_async_copy(self.x_refs[a], self.out_refs[a].at[4 * px + 2 * py + pc], self.local_sems.at[a])

    def _first(self, a):
        cps = [self._copy(a, 0, self.me, self.sibling, src=self.x_refs[a])]
        cps += [self._copy(a, 1 + j, self.me, (*chip, self.c), src=self.x_refs[a]) for j, chip in enumerate(self.chips)]
        return cps

    def _passed(self, a):
        return [self._copy(a, 4 + j, (*chip, self.c), self.sibling) for j, chip in enumerate(self.chips)]

    def start(self):
        for a in range(len(self.x_refs)):
            self._mine(a).start()
            for cp in self._first(a):
                cp.start()

    def forward(self):
        for a in range(len(self.x_refs)):
            passed = self._passed(a)
            for j, chip in enumerate(self.chips):
                self._copy(a, 1 + j, (*chip, self.c), self.me).wait_recv()
                passed[j].start()

    def finish(self):
        for a in range(len(self.x_refs)):
            self._copy(a, 0, self.sibling, self.me).wait_recv()
            for j, chip in enumerate(self.chips):
                self._copy(a, 4 + j, (*chip, 1 - self.c), self.me).wait_recv()
            for cp in self._first(a) + self._passed(a):
                cp.wait_send()
            self._mine(a).wait()


def _exchange_copy(src_refs, land_refs, send_sems, recv_sems, a, k, gather, receiving):
    x, y, c = _my_coords()
    me = 4 * x + 2 * y + c
    peer, pid = _peer(k)
    src = src_refs[a] if gather else src_refs[a].at[pid]
    dst = land_refs[a].at[pid if receiving else me]
    return pltpu.make_async_remote_copy(src, dst, send_sems.at[7 * a + k - 1], recv_sems.at[7 * a + k - 1],
                                        device_id=peer, device_id_type=MESH)


def _exchange_start(srcs, gather, name):
    n = len(srcs)
    xi, yi, ci = _my_coords()
    me = 4 * xi + 2 * yi + ci
    lands = []
    for s in srcs:
        own = s[None] if gather else lax.dynamic_slice_in_dim(s, me, 1, axis=0)
        lands.append(lax.dynamic_update_slice_in_dim(lax.empty((N_DEV, *own.shape[1:]), s.dtype), own, me, axis=0))

    def body(*refs):
        src_refs, land_refs, send_sems, recv_sems, token = refs[:n], refs[n:2 * n], refs[2 * n], refs[2 * n + 1], refs[-1]
        for a in range(n):
            for k in range(1, N_DEV):
                _exchange_copy(src_refs, land_refs, send_sems, recv_sems, a, k, gather, receiving=False).start()
        token[...] = jnp.zeros_like(token)

    hbm = pl.BlockSpec(memory_space=pltpu.HBM)
    sem = pl.BlockSpec(memory_space=pltpu.SEMAPHORE)
    res = pl.pallas_call(
        body, name=name,
        out_shape=(pltpu.SemaphoreType.DMA((7 * n,)), pltpu.SemaphoreType.DMA((7 * n,)),
                   *[pltpu.HBM(v.shape, v.dtype) for v in srcs + lands], jax.ShapeDtypeStruct((8, 128), F32)),
        in_specs=[hbm] * (2 * n),
        out_specs=(sem, sem, *([hbm] * (2 * n)), pl.BlockSpec(memory_space=pltpu.VMEM)),
        input_output_aliases={i: 2 + i for i in range(2 * n)},
        compiler_params=pltpu.CompilerParams(has_side_effects=pltpu.SideEffectType.DATAFLOW_SIDE_EFFECTING),
    )(*[pltpu.with_memory_space_constraint(v, pltpu.HBM) for v in srcs + lands])
    return res[0], res[1], list(res[2:2 + n]), list(res[2 + n:2 + 2 * n]), res[-1]


def _exchange_wait(send_sems, recv_sems, srcs, lands, after, gather, name):
    n = len(srcs)

    def body(*refs):
        src_refs, land_refs, s_sems, r_sems = refs[:n], refs[n:2 * n], refs[2 * n], refs[2 * n + 1]
        for a in range(n):
            for k in range(1, N_DEV):
                _exchange_copy(src_refs, land_refs, s_sems, r_sems, a, k, gather, receiving=False).wait_send()
                _exchange_copy(src_refs, land_refs, s_sems, r_sems, a, k, gather, receiving=True).wait_recv()

    hbm = pl.BlockSpec(memory_space=pltpu.HBM)
    sem = pl.BlockSpec(memory_space=pltpu.SEMAPHORE)
    res = pl.pallas_call(
        body, name=name,
        out_shape=tuple(pltpu.HBM(v.shape, v.dtype) for v in srcs + lands),
        in_specs=[hbm] * (2 * n) + [sem, sem, pl.BlockSpec(memory_space=pl.ANY)],
        out_specs=tuple([hbm] * (2 * n)),
        input_output_aliases={i: i for i in range(2 * n)},
        compiler_params=pltpu.CompilerParams(has_side_effects=pltpu.SideEffectType.DATAFLOW_SIDE_EFFECTING),
    )(*srcs, *lands, send_sems, recv_sems, after)
    return list(res[n:])


def _small_gather(vecs):
    n = len(vecs)

    def body(*refs):
        v_refs, out_refs, s_sems, r_sems = refs[:n], refs[n:2 * n], refs[2 * n], refs[2 * n + 1]
        x, y, c = _my_coords()
        me = 4 * x + 2 * y + c
        sends = []
        for a in range(n):
            out_refs[a][me] = v_refs[a][...]
            for k in range(1, N_DEV):
                peer, _ = _peer(k)
                cp = pltpu.make_async_remote_copy(v_refs[a], out_refs[a].at[me], s_sems.at[7 * a + k - 1],
                                                  r_sems.at[7 * a + k - 1], device_id=peer, device_id_type=MESH)
                cp.start()
                sends.append(cp)
        for a in range(n):
            for k in range(1, N_DEV):
                peer, pid = _peer(k)
                pltpu.make_async_remote_copy(v_refs[a], out_refs[a].at[pid], s_sems.at[7 * a + k - 1],
                                             r_sems.at[7 * a + k - 1], device_id=peer, device_id_type=MESH).wait_recv()
        for cp in sends:
            cp.wait_send()

    vm = pl.BlockSpec(memory_space=pltpu.VMEM)
    return pl.pallas_call(
        body, name="small_gather",
        out_shape=tuple(jax.ShapeDtypeStruct((N_DEV, *v.shape), v.dtype) for v in vecs),
        in_specs=[vm] * n, out_specs=tuple([vm] * n),
        scratch_shapes=[pltpu.SemaphoreType.DMA((7 * n,))] * 2,
    )(*vecs)


def _load_resident(step_is_first, pairs, sem):
    @pl.when(step_is_first)
    def _():
        copies = [pltpu.make_async_copy(src, dst, sem.at[i]) for i, (src, dst) in enumerate(pairs)]
        for cp in copies:
            cp.start()
        for cp in copies:
            cp.wait()


def _load_w_in_t(step_is_first, w_hbm, w_vmem, sem):
    @pl.when(step_is_first)
    def _():
        w_vmem[D_IN:, :] = jnp.zeros((D_IN_PAD - D_IN, D_MODEL), BF16)
    _load_resident(step_is_first, [(w_hbm, w_vmem.at[pl.ds(0, D_IN)])], sem)


def _inproj_fwd(x2d, sc1p, sh1, w_in_t):
    t = x2d.shape[0]
    tm = min(PROJ_TILE, t)

    def body(x_ref, sc_ref, sh_ref, w_hbm, proj_ref, u_ref, w_vmem, sem):
        _load_w_in_t(pl.program_id(0) == 0, w_hbm, w_vmem, sem)
        xh, _ = _ln_stats(x_ref[...])
        ub = (xh * sc_ref[...] + sh_ref[...]).astype(BF16)
        u_ref[...] = ub
        proj_ref[...] = _dot(ub, w_vmem[...], NT)

    row = lambda i: (i, 0)
    fix = lambda i: (0, 0)
    return pl.pallas_call(
        body, name="inproj_fwd", grid=(t // tm,),
        in_specs=[pl.BlockSpec((tm, D_MODEL), row), pl.BlockSpec((1, D_MODEL), fix), pl.BlockSpec((1, D_MODEL), fix),
                  pl.BlockSpec(memory_space=pl.ANY)],
        out_specs=(pl.BlockSpec((tm, D_IN_PAD), row), pl.BlockSpec((tm, D_MODEL), row)),
        out_shape=(jax.ShapeDtypeStruct((t, D_IN_PAD), F32), jax.ShapeDtypeStruct((t, D_MODEL), BF16)),
        scratch_shapes=[pltpu.VMEM((D_IN_PAD, D_MODEL), BF16), pltpu.SemaphoreType.DMA((1,))],
        compiler_params=pltpu.CompilerParams(dimension_semantics=("arbitrary",), vmem_limit_bytes=V7X_VMEM_LIMIT),
    )(x2d, sc1p, sh1, w_in_t)


CHUNK_SHIFT = 6


def _ret_tables(t, tl):
    r = lax.broadcasted_iota(jnp.int32, (tl, tl), 0)
    c = lax.broadcasted_iota(jnp.int32, (tl, tl), 1)
    allowed = jnp.right_shift(c, CHUNK_SHIFT) <= jnp.right_shift(r, CHUNK_SHIFT)
    dist = jnp.abs(r - c).astype(F32)
    rowf = lax.broadcasted_iota(jnp.int32, (tl, RET_D), 0).astype(F32)
    lgs = [_log_gamma(h) for h in range(RET_HEADS)]
    dec = jnp.stack([jnp.where(allowed, jnp.exp(lg * dist), 0.0) for lg in lgs])
    qkd = jnp.stack([jnp.exp(lg * (rowf + 1.0)) for lg in lgs] + [jnp.exp(lg * (tl - 1.0 - rowf)) for lg in lgs])
    inv = 1.0 / (10000.0 ** jnp.linspace(0.0, 1.0, RET_D // 2, dtype=F32))
    off = jnp.arange(tl, dtype=F32)[:, None] * inv[None, :]
    start = (jnp.arange(t // tl, dtype=F32) * tl)[:, None] * inv[None, :]
    co, so = jnp.cos(off), jnp.sin(off)
    rot_in = jnp.stack([jnp.concatenate([co, co], 1), jnp.concatenate([so, so], 1),
                        jnp.concatenate([-co, co], 1), jnp.concatenate([-so, so], 1)])
    cs, ss = jnp.cos(start), jnp.sin(start)
    rot_tile = jnp.concatenate([cs, cs, ss, ss], axis=1)
    rot_tile = jnp.broadcast_to(rot_tile[:, None, :], (t // tl, 8, 2 * RET_D))
    return dec, qkd, rot_in, rot_tile


def _tile_gammas(tl):
    return [float(np.exp(np.float32(_log_gamma(h)) * np.float32(tl))) for h in range(RET_HEADS)]


def _tile_rotary(rot_in_ref, rot_tile_ref):
    ca, sa = rot_tile_ref[0, 0:1, 0:RET_D], rot_tile_ref[0, 0:1, RET_D:2 * RET_D]
    cosv = ca * rot_in_ref[0] - sa * rot_in_ref[1]
    sinv = sa * rot_in_ref[2] + ca * rot_in_ref[3]
    return cosv, sinv


def _gla_consts(tl):
    r = lax.broadcasted_iota(jnp.int32, (tl, tl), 0)
    c = lax.broadcasted_iota(jnp.int32, (tl, tl), 1)
    ltri = (c <= r).astype(F32)
    utri = (c >= r).astype(F32)
    lane = lax.broadcasted_iota(jnp.int32, (1, GLA_KW), 1)
    hmask = [((lane >= h * GLA_DK) & (lane < (h + 1) * GLA_DK)).astype(F32) for h in range(GLA_HEADS)]
    rs = lax.broadcasted_iota(jnp.int32, (GLA_HEADS * tl, tl), 0) & (tl - 1)
    cs = lax.broadcasted_iota(jnp.int32, (GLA_HEADS * tl, tl), 1)
    lower = cs <= rs
    same = jnp.right_shift(cs, CHUNK_SHIFT) == jnp.right_shift(rs, CHUNK_SHIFT)
    upper = jnp.logical_and(jnp.logical_not(lower), same)
    return dict(ltri=ltri, utri=utri, hmask=hmask, lower=lower, upper=upper)


def _tile_rows(j, tl):
    return pl.ds(j * tl, tl) if isinstance(j, int) else pl.ds(pl.multiple_of(j * tl, tl), tl)


def _for_tiles(cps, fn):
    if cps == 1:
        fn(0, 0)
    else:
        lax.fori_loop(0, cps, fn, 0)


def _rotate(v, cosv, sinv):
    return v * cosv + pltpu.roll(v, RET_D // 2, 1) * sinv


def _rotate_t(d, cosv, sinv):
    return d * cosv + pltpu.roll(d * sinv, RET_D // 2, 1)


def _stack_heads(v, hmask):
    return jnp.concatenate([v * hmask[h] for h in range(GLA_HEADS)], axis=0)


def _gla_gates(glr, gw, gb, ltri, tl):
    z = _dot_split(glr, gw, NN) + gb
    la = (jnp.minimum(z, 0.0) - jnp.log(1.0 + jnp.exp(-jnp.abs(z)))) * (1.0 / GATE_TAU)
    b = _dot_split(ltri, la, NN, a_exact=True)
    level = b[tl // 2 - 1:tl // 2, :]
    ep = jnp.exp(jnp.clip(b - level, -80.0, 80.0))
    em = jnp.exp(jnp.clip(level - b, -80.0, 80.0))
    bl = b[tl - 1:tl, :]
    return z, b, bl, ep, em


def _mixer_fwd(proj, tables, gw_pad, gb, rnw, gnw):
    t = proj.shape[0]
    tc = min(MIX_TILE, t)
    tr, tg = tc, min(GLA_SUB, tc)
    nsteps = t // tc
    scale_r = RET_D ** -0.5
    scale_g = GLA_DK ** -0.5
    gammas = _tile_gammas(tr)

    def body(rq_ref, rk_ref, rv_ref, rg_ref, gq_ref, gk_ref, gv_ref, gg_ref, glr_ref,
             dec_ref, qkd_ref, rot_in_ref, rot_tile_ref, gw_ref, gb_ref, rnw_ref, gnw_ref,
             mix_ref, oraw_ref, qrb_ref, krb_ref, rst_ref, sst_ref, r_scr, s_scr):
        @pl.when(pl.program_id(0) == 0)
        def _():
            r_scr[...] = jnp.zeros_like(r_scr)
            s_scr[...] = jnp.zeros_like(s_scr)

        gla_k = _gla_consts(tg)

        def ret_tile(j, carry):
            rows = _tile_rows(j, tr)
            cosv, sinv = _tile_rotary(rot_in_ref, rot_tile_ref)
            for h in range(RET_HEADS):
                cols = slice(h * RET_D, (h + 1) * RET_D)
                qr = _rotate(rq_ref[rows, cols], cosv, sinv) * scale_r
                kr = _rotate(rk_ref[rows, cols], cosv, sinv)
                vb = rv_ref[rows, cols].astype(BF16)
                qb, kb = qr.astype(BF16), kr.astype(BF16)
                qrb_ref[rows, cols] = qb
                krb_ref[rows, cols] = kb
                p = _dot(qb, kb, NT) * dec_ref[h]
                rp = r_scr[cols, :]
                o = _dot(p.astype(BF16), vb) + _dot((qr * qkd_ref[h]).astype(BF16), rp.astype(BF16))
                rst_ref[j, cols, :] = rp
                r_scr[cols, :] = gammas[h] * rp + _dot((kr * qkd_ref[RET_HEADS + h]).astype(BF16), vb, TN)
                oraw_ref[rows, cols] = o
                oc = o - jnp.mean(o, axis=-1, keepdims=True)
                n = oc * lax.rsqrt(jnp.mean(oc * oc, axis=-1, keepdims=True) + LN_EPS)
                g = rg_ref[rows, cols]
                mix_ref[rows, cols] = (n * rnw_ref[:, cols] * (g * _sigmoid(g))).astype(BF16)
            return carry

        def gla_tile(j, carry):
            k = gla_k
            tl = tg
            rows = _tile_rows(j, tg)
            _, b, bl, ep, em = _gla_gates(glr_ref[rows, :], gw_ref[...], gb_ref[...], k["ltri"], tl)
            qs = gq_ref[rows, :] * scale_g
            kk = gk_ref[rows, :]
            x_all = _dot(_stack_heads(qs * ep, k["hmask"]).astype(BF16), (kk * em).astype(BF16), NT)
            y_all = _dot(_stack_heads(qs * em, k["hmask"]).astype(BF16), (kk * ep).astype(BF16), NT)
            a_all = jnp.where(k["lower"], x_all, jnp.where(k["upper"], y_all, 0.0)).astype(BF16)
            st = s_scr[...]
            oq = _dot(_stack_heads(qs * jnp.exp(b), k["hmask"]).astype(BF16), st.astype(BF16), NT)
            kg = kk * jnp.exp(bl - b)
            sst_ref[j] = st
            st_new = st * jnp.exp(bl)
            for h in range(GLA_HEADS):
                cols = slice(h * GLA_DV, (h + 1) * GLA_DV)
                hr = slice(h * tl, (h + 1) * tl)
                vb = gv_ref[rows, cols].astype(BF16)
                o = _dot(a_all[hr, :], vb) + oq[hr, :]
                st_new = st_new + _dot(vb, (kg * k["hmask"][h]).astype(BF16), TN)
                ocols = slice(RET_HEADS * RET_D + h * GLA_DV, RET_HEADS * RET_D + (h + 1) * GLA_DV)
                oraw_ref[rows, ocols] = o
                n = o * lax.rsqrt(jnp.mean(o * o, axis=-1, keepdims=True) + LN_EPS)
                g = gg_ref[rows, cols]
                mix_ref[rows, ocols] = (n * gnw_ref[:, cols] * (g * _sigmoid(g))).astype(BF16)
            s_scr[...] = st_new
            return carry

        _for_tiles(tc // tr, ret_tile)
        _for_tiles(tc // tg, gla_tile)

    def col(width, off):
        return pl.BlockSpec((tc, width), lambda i, o=off // width: (i, o))

    fix = lambda i: (0, 0)
    fix3 = lambda i: (0, 0, 0)
    dec, qkd, rot_in, rot_tile = tables
    in_specs = [col(512, OFF_RQ), col(512, OFF_RK), col(512, OFF_RV), col(512, OFF_RG),
                col(256, OFF_GQ), col(256, OFF_GK), col(512, OFF_GV), col(512, OFF_GG), col(128, OFF_GLR),
                pl.BlockSpec(dec.shape, fix3), pl.BlockSpec(qkd.shape, fix3), pl.BlockSpec(rot_in.shape, fix3),
                pl.BlockSpec((1, 8, 2 * RET_D), lambda i: (i, 0, 0)),
                pl.BlockSpec((128, GLA_KW), fix), pl.BlockSpec((1, GLA_KW), fix),
                pl.BlockSpec((1, 512), fix), pl.BlockSpec((1, 512), fix)]
    half = pl.BlockSpec((tc, RET_HEADS * RET_D), lambda i: (i, 0))
    out_specs = (pl.BlockSpec((tc, D_MODEL), lambda i: (i, 0)), pl.BlockSpec((tc, D_MODEL), lambda i: (i, 0)),
                 half, half,
                 pl.BlockSpec((tc // tr, RET_HEADS * RET_D, RET_D), lambda i: (i, 0, 0)),
                 pl.BlockSpec((tc // tg, GLA_DV, GLA_KW), lambda i: (i, 0, 0)))
    out_shape = (jax.ShapeDtypeStruct((t, D_MODEL), BF16), jax.ShapeDtypeStruct((t, D_MODEL), F32),
                 jax.ShapeDtypeStruct((t, RET_HEADS * RET_D), BF16), jax.ShapeDtypeStruct((t, RET_HEADS * RET_D), BF16),
                 jax.ShapeDtypeStruct((t // tr, RET_HEADS * RET_D, RET_D), F32),
                 jax.ShapeDtypeStruct((t // tg, GLA_DV, GLA_KW), F32))
    return pl.pallas_call(
        body, name="mixer_fwd", grid=(nsteps,), in_specs=in_specs, out_specs=out_specs, out_shape=out_shape,
        scratch_shapes=[pltpu.VMEM((RET_HEADS * RET_D, RET_D), F32), pltpu.VMEM((GLA_DV, GLA_KW), F32)],
        compiler_params=pltpu.CompilerParams(dimension_semantics=("arbitrary",), vmem_limit_bytes=V7X_VMEM_LIMIT),
    )(*([proj] * 9), dec, qkd, rot_in, rot_tile, gw_pad, gb, rnw, gnw)


def _mid_fwd(mixed, x2d, target, vecs, w_out_b, w1_b, w2_b):
    t = x2d.shape[0]
    tm = min(ROW_TILE, t)

    def body(mix_ref, x_ref, tgt_ref, v_ref, wo_hbm, w1_hbm, w2_hbm,
             m_ref, x1n_ref, rstd_ref, u2_ref, a_ref, df_ref, dh2_ref, acc_ref, wo, w1, w2, sem):
        first = pl.program_id(0) == 0
        _load_resident(first, [(wo_hbm, wo), (w1_hbm, w1), (w2_hbm, w2)], sem)

        @pl.when(first)
        def _():
            acc_ref[...] = jnp.zeros_like(acc_ref)

        gate1, sc2p, sh2, gate2 = v_ref[0:1, :], v_ref[1:2, :], v_ref[2:3, :], v_ref[3:4, :]
        l1w, l1b, l2w, l2b = v_ref[4:5, :], v_ref[5:6, :], v_ref[6:7, :], v_ref[7:8, :]
        m = _dot(mix_ref[...], wo[...])
        m_ref[...] = m.astype(BF16)
        x1n, rstd1 = _ln_stats(ALPHA * x_ref[...] + gate1 * m)
        x1n_ref[...] = x1n
        rstd_ref[...] = rstd1
        x1 = x1n * l1w + l1b
        xh1, _ = _ln_stats(x1)
        u2 = (xh1 * sc2p + sh2).astype(BF16)
        u2_ref[...] = u2
        f = jnp.zeros((tm, D_MODEL), F32)
        for j in range(N_DEV):
            cols = slice(j * FF_COLS, (j + 1) * FF_COLS)
            a = _dot(u2, w1[j])
            a_ref[:, cols] = a.astype(BF16)
            r = jnp.maximum(a, 0.0)
            f = f + _dot((r * r).astype(BF16), w2[cols, :])
        yh, rstd2 = _ln_stats(ALPHA * x1 + gate2 * f)
        e = yh * l2w + l2b - tgt_ref[...]
        dy = e * (1.0 / D_MODEL)
        dh2 = _ln_bwd(dy * l2w, yh, rstd2)
        dh2_ref[...] = dh2
        df_ref[...] = (dh2 * gate2).astype(BF16)
        acc_ref[0:1, :] += jnp.sum(dy * yh, axis=0, keepdims=True)
        acc_ref[1:2, :] += jnp.sum(dy, axis=0, keepdims=True)
        acc_ref[2:3, :] += jnp.sum(dh2 * f, axis=0, keepdims=True)
        acc_ref[3:4, :] += jnp.sum(e * e, axis=0, keepdims=True) * (0.5 / D_MODEL)

    row = lambda i: (i, 0)
    fix = lambda i: (0, 0)
    hbm = pl.BlockSpec(memory_space=pl.ANY)
    return pl.pallas_call(
        body, name="mid_fwd", grid=(t // tm,),
        in_specs=[pl.BlockSpec((tm, D_MODEL), row), pl.BlockSpec((tm, D_MODEL), row), pl.BlockSpec((tm, D_MODEL), row),
                  pl.BlockSpec((8, D_MODEL), fix), hbm, hbm, hbm],
        out_specs=(pl.BlockSpec((tm, D_MODEL), row), pl.BlockSpec((tm, D_MODEL), row), pl.BlockSpec((tm, 1), row),
                   pl.BlockSpec((tm, D_MODEL), row), pl.BlockSpec((tm, D_FF), row), pl.BlockSpec((tm, D_MODEL), row),
                   pl.BlockSpec((tm, D_MODEL), row), pl.BlockSpec((8, D_MODEL), fix)),
        out_shape=(jax.ShapeDtypeStruct((t, D_MODEL), BF16), jax.ShapeDtypeStruct((t, D_MODEL), F32),
                   jax.ShapeDtypeStruct((t, 1), F32), jax.ShapeDtypeStruct((t, D_MODEL), BF16),
                   jax.ShapeDtypeStruct((t, D_FF), BF16), jax.ShapeDtypeStruct((t, D_MODEL), BF16),
                   jax.ShapeDtypeStruct((t, D_MODEL), F32), jax.ShapeDtypeStruct((8, D_MODEL), F32)),
        scratch_shapes=[pltpu.VMEM((D_MODEL, D_MODEL), BF16), pltpu.VMEM((N_DEV, D_MODEL, FF_COLS), BF16),
                        pltpu.VMEM((D_FF, D_MODEL), BF16), pltpu.SemaphoreType.DMA((3,))],
        compiler_params=pltpu.CompilerParams(dimension_semantics=("arbitrary",), vmem_limit_bytes=V7X_VMEM_LIMIT),
    )(mixed, x2d, target, vecs, w_out_b, w1_b, w2_b)


def _ffn_bwd(df, a, dh2, x1n, rstd1, m, vecs, w_out_b, w1_b, w2_b):
    t = x1n.shape[0]
    tm = min(ROW_TILE, t)

    def body(df_ref, a_ref, dh2_ref, x1n_ref, rstd_ref, m_ref, v_ref, wo_hbm, w1_hbm, w2_hbm,
             da_ref, dm_ref, dmix_ref, dxa_ref, acc_ref, wo, w1, w2, sem):
        first = pl.program_id(0) == 0
        _load_resident(first, [(wo_hbm, wo), (w1_hbm, w1), (w2_hbm, w2)], sem)

        @pl.when(first)
        def _():
            acc_ref[...] = jnp.zeros_like(acc_ref)

        gate1, sc2p, l1w, l1b = v_ref[0:1, :], v_ref[1:2, :], v_ref[2:3, :], v_ref[3:4, :]
        df = df_ref[...]
        du2 = jnp.zeros((tm, D_MODEL), F32)
        for j in range(N_DEV):
            cols = slice(j * FF_COLS, (j + 1) * FF_COLS)
            dr2 = _dot(df, w2[cols, :], NT)
            da = (dr2 * (2.0 * jnp.maximum(a_ref[:, cols].astype(F32), 0.0))).astype(BF16)
            da_ref[:, cols] = da
            du2 = du2 + _dot(da, w1[j], NT)
        x1n = x1n_ref[...]
        xh1, rstd0 = _ln_stats(x1n * l1w + l1b)
        dx1 = ALPHA * dh2_ref[...] + _ln_bwd(du2 * sc2p, xh1, rstd0)
        dh1 = _ln_bwd(dx1 * l1w, x1n, rstd_ref[...])
        dxa_ref[...] = ALPHA * dh1
        dm = (dh1 * gate1).astype(BF16)
        dm_ref[...] = dm
        dmix_ref[...] = _dot(dm, wo[...], NT)
        acc_ref[0:1, :] += jnp.sum(du2 * xh1, axis=0, keepdims=True)
        acc_ref[1:2, :] += jnp.sum(du2, axis=0, keepdims=True)
        acc_ref[2:3, :] += jnp.sum(dx1 * x1n, axis=0, keepdims=True)
        acc_ref[3:4, :] += jnp.sum(dx1, axis=0, keepdims=True)
        acc_ref[4:5, :] += jnp.sum(dh1 * m_ref[...].astype(F32), axis=0, keepdims=True)

    row = lambda i: (i, 0)
    fix = lambda i: (0, 0)
    hbm = pl.BlockSpec(memory_space=pl.ANY)
    return pl.pallas_call(
        body, name="ffn_bwd", grid=(t // tm,),
        in_specs=[pl.BlockSpec((tm, D_MODEL), row), pl.BlockSpec((tm, D_FF), row), pl.BlockSpec((tm, D_MODEL), row),
                  pl.BlockSpec((tm, D_MODEL), row), pl.BlockSpec((tm, 1), row), pl.BlockSpec((tm, D_MODEL), row),
                  pl.BlockSpec((8, D_MODEL), fix), hbm, hbm, hbm],
        out_specs=(pl.BlockSpec((tm, D_FF), row), pl.BlockSpec((tm, D_MODEL), row), pl.BlockSpec((tm, D_MODEL), row),
                   pl.BlockSpec((tm, D_MODEL), row), pl.BlockSpec((8, D_MODEL), fix)),
        out_shape=(jax.ShapeDtypeStruct((t, D_FF), BF16), jax.ShapeDtypeStruct((t, D_MODEL), BF16),
                   jax.ShapeDtypeStruct((t, D_MODEL), F32), jax.ShapeDtypeStruct((t, D_MODEL), F32),
                   jax.ShapeDtypeStruct((8, D_MODEL), F32)),
        scratch_shapes=[pltpu.VMEM((D_MODEL, D_MODEL), BF16), pltpu.VMEM((N_DEV, D_MODEL, FF_COLS), BF16),
                        pltpu.VMEM((D_FF, D_MODEL), BF16), pltpu.SemaphoreType.DMA((3,))],
        compiler_params=pltpu.CompilerParams(dimension_semantics=("arbitrary",), vmem_limit_bytes=V7X_VMEM_LIMIT),
    )(df, a, dh2, x1n, rstd1, m, vecs, w_out_b, w1_b, w2_b)


def _matmul_tn(lhs, rhs, tmm, tn, tk, name, relu_sq=False, col_slab=None):
    t, mm = lhs.shape
    nn = rhs.shape[1]
    tk = min(tk, t)
    nk = t // tk

    def body(l_ref, r_ref, o_ref, acc):
        kk = pl.program_id(2)

        @pl.when(kk == 0)
        def _():
            acc[...] = jnp.zeros_like(acc)

        l = l_ref[...]
        if relu_sq:
            lf = jnp.maximum(l.astype(F32), 0.0)
            l = (lf * lf).astype(BF16)
        acc[...] += _dot(l, r_ref[...], TN)

        @pl.when(kk == nk - 1)
        def _():
            if col_slab is None:
                o_ref[...] = acc[...].astype(o_ref.dtype)
            else:
                for s in range(tn // col_slab):
                    o_ref[s] = acc[:, s * col_slab:(s + 1) * col_slab].astype(o_ref.dtype)

    if col_slab is None:
        out_spec = pl.BlockSpec((tmm, tn), lambda i, j, k: (i, j))
        out_shape = jax.ShapeDtypeStruct((mm, nn), BF16)
    else:
        out_spec = pl.BlockSpec((tn // col_slab, tmm, col_slab), lambda i, j, k: (j, i, 0))
        out_shape = jax.ShapeDtypeStruct((nn // col_slab, mm, col_slab), BF16)
    return pl.pallas_call(
        body, name=name, grid=(mm // tmm, nn // tn, nk),
        in_specs=[pl.BlockSpec((tk, tmm), lambda i, j, k: (k, i)), pl.BlockSpec((tk, tn), lambda i, j, k: (k, j))],
        out_specs=out_spec,
        out_shape=out_shape,
        scratch_shapes=[pltpu.VMEM((tmm, tn), F32)],
        compiler_params=pltpu.CompilerParams(dimension_semantics=("arbitrary", "arbitrary", "arbitrary"),
                                             vmem_limit_bytes=V7X_VMEM_LIMIT),
    )(lhs, rhs)


def _mixer_bwd(dmix, proj, qrb, krb, oraw, tables, rst, sst, gw_pad, gb, rnw, gnw):
    t = proj.shape[0]
    tc = min(MIX_TILE, t)
    tr, tg = tc, min(GLA_SUB, tc)
    nsteps = t // tc
    scale_r = RET_D ** -0.5
    scale_g = GLA_DK ** -0.5
    gammas = _tile_gammas(tr)

    def body(dmix_ref, qrb_ref, krb_ref, rv_ref, rg_ref, gq_ref, gk_ref, gv_ref, gg_ref, glr_ref, oraw_ref,
             dec_ref, qkd_ref, rot_in_ref, rot_tile_ref, rst_ref, sst_ref, gw_ref, gb_ref, rnw_ref, gnw_ref,
             dproj_ref, dgw_ref, dvec_ref, dr_scr, ds_scr):
        @pl.when(pl.program_id(0) == 0)
        def _():
            dr_scr[...] = jnp.zeros_like(dr_scr)
            ds_scr[...] = jnp.zeros_like(ds_scr)
            dgw_ref[...] = jnp.zeros_like(dgw_ref)
            dvec_ref[...] = jnp.zeros_like(dvec_ref)

        gla_k = _gla_consts(tg)
        last_row = lax.broadcasted_iota(jnp.int32, (tg, GLA_KW), 0) == tg - 1

        def ret_tile(jj, carry):
            j = tc // tr - 1 - jj
            rows = _tile_rows(j, tr)
            cosv, sinv = _tile_rotary(rot_in_ref, rot_tile_ref)
            for h in range(RET_HEADS):
                cols = slice(h * RET_D, (h + 1) * RET_D)
                o = oraw_ref[rows, cols]
                g = rg_ref[rows, cols]
                w = rnw_ref[:, cols]
                dout = dmix_ref[rows, cols]
                oc = o - jnp.mean(o, axis=-1, keepdims=True)
                inv = lax.rsqrt(jnp.mean(oc * oc, axis=-1, keepdims=True) + LN_EPS)
                n = oc * inv
                sg = _sigmoid(g)
                sil = g * sg
                dn = dout * w * sil
                dvec_ref[0:1, cols] += jnp.sum(dout * n * sil, axis=0, keepdims=True)
                dproj_ref[rows, OFF_RG + h * RET_D:OFF_RG + (h + 1) * RET_D] = (
                    dout * n * w * (sg * (1.0 + g * (1.0 - sg)))).astype(BF16)
                doc = inv * (dn - n * jnp.mean(dn * n, axis=-1, keepdims=True))
                do = doc - jnp.mean(doc, axis=-1, keepdims=True)

                qb, kb = qrb_ref[rows, cols], krb_ref[rows, cols]
                qr, kr = qb.astype(F32), kb.astype(F32)
                vb = rv_ref[rows, cols].astype(BF16)
                dob = do.astype(BF16)
                qd, kd = qkd_ref[h], qkd_ref[RET_HEADS + h]
                p = _dot(qb, kb, NT) * dec_ref[h]
                rp = rst_ref[j, cols, :].astype(BF16)
                dr = dr_scr[cols, :]
                drb = dr.astype(BF16)
                dpb = (_dot(dob, vb, NT) * dec_ref[h]).astype(BF16)
                dqr = _dot(dpb, kb) + _dot(dob, rp, NT) * qd
                dkr = _dot(dpb, qb, TN) + _dot(vb, drb, NT) * kd
                dv = _dot(p.astype(BF16), dob, TN) + _dot((kr * kd).astype(BF16), drb)
                dr_scr[cols, :] = gammas[h] * dr + _dot((qr * qd).astype(BF16), dob, TN)
                dproj_ref[rows, OFF_RQ + h * RET_D:OFF_RQ + (h + 1) * RET_D] = (
                    _rotate_t(dqr, cosv, sinv) * scale_r).astype(BF16)
                dproj_ref[rows, OFF_RK + h * RET_D:OFF_RK + (h + 1) * RET_D] = _rotate_t(dkr, cosv, sinv).astype(BF16)
                dproj_ref[rows, OFF_RV + h * RET_D:OFF_RV + (h + 1) * RET_D] = dv.astype(BF16)
            return carry

        def gla_tile(jj, carry):
            k = gla_k
            tl = tg
            j = tc // tg - 1 - jj
            rows = _tile_rows(j, tg)
            glr = glr_ref[rows, :]
            z, b, bl, ep, em = _gla_gates(glr, gw_ref[...], gb_ref[...], k["ltri"], tl)
            qs = gq_ref[rows, :] * scale_g
            kk = gk_ref[rows, :]
            eb = jnp.exp(b)
            ekb = jnp.exp(bl - b)
            ebl = jnp.exp(bl)
            ql, qu, kl, ku = qs * ep, qs * em, kk * em, kk * ep
            qg, kg = qs * eb, kk * ekb
            qlm = _stack_heads(ql, k["hmask"]).astype(BF16)
            qum = _stack_heads(qu, k["hmask"]).astype(BF16)
            klb, kub = kl.astype(BF16), ku.astype(BF16)
            a_all = jnp.where(k["lower"], _dot(qlm, klb, NT),
                              jnp.where(k["upper"], _dot(qum, kub, NT), 0.0)).astype(BF16)
            st = sst_ref[j]
            stb = st.astype(BF16)
            ds = ds_scr[...]
            dsb = ds.astype(BF16)
            ds_new = ds * ebl
            da_parts = []
            dqg = jnp.zeros((tl, GLA_KW), F32)
            dkg = jnp.zeros((tl, GLA_KW), F32)
            for h in range(GLA_HEADS):
                cols = slice(h * GLA_DV, (h + 1) * GLA_DV)
                hr = slice(h * tl, (h + 1) * tl)
                ocols = slice(RET_HEADS * RET_D + h * GLA_DV, RET_HEADS * RET_D + (h + 1) * GLA_DV)
                o = oraw_ref[rows, ocols]
                g = gg_ref[rows, cols]
                w = gnw_ref[:, cols]
                dout = dmix_ref[rows, ocols]
                inv = lax.rsqrt(jnp.mean(o * o, axis=-1, keepdims=True) + LN_EPS)
                n = o * inv
                sg = _sigmoid(g)
                sil = g * sg
                dn = dout * w * sil
                dvec_ref[1:2, cols] += jnp.sum(dout * n * sil, axis=0, keepdims=True)
                dproj_ref[rows, OFF_GG + h * GLA_DV:OFF_GG + (h + 1) * GLA_DV] = (
                    dout * n * w * (sg * (1.0 + g * (1.0 - sg)))).astype(BF16)
                dob = (inv * (dn - n * jnp.mean(dn * n, axis=-1, keepdims=True))).astype(BF16)
                vb = gv_ref[rows, cols].astype(BF16)
                mh = k["hmask"][h]
                da_parts.append(_dot(dob, vb, NT))
                dv = _dot(a_all[hr, :], dob, TN) + _dot((kg * mh).astype(BF16), dsb, NT)
                dproj_ref[rows, OFF_GV + h * GLA_DV:OFF_GV + (h + 1) * GLA_DV] = dv.astype(BF16)
                dkg = dkg + mh * _dot(vb, dsb)
                dqg = dqg + mh * _dot(dob, stb)
                ds_new = ds_new + _dot(dob, (qg * mh).astype(BF16), TN)
            da_all = jnp.concatenate(da_parts, axis=0)
            dal = jnp.where(k["lower"], da_all, 0.0).astype(BF16)
            dau = jnp.where(k["upper"], da_all, 0.0).astype(BF16)
            dqlm = _dot(dal, klb)
            dqum = _dot(dau, kub)
            dql = jnp.zeros((tl, GLA_KW), F32)
            dqu = jnp.zeros((tl, GLA_KW), F32)
            for h in range(GLA_HEADS):
                hr = slice(h * tl, (h + 1) * tl)
                dql = dql + k["hmask"][h] * dqlm[hr, :]
                dqu = dqu + k["hmask"][h] * dqum[hr, :]
            dkl = _dot(dal, qlm, TN)
            dku = _dot(dau, qum, TN)
            dbl = (jnp.sum(dkg * kg, axis=0, keepdims=True)
                   + jnp.sum(ds * st, axis=0, keepdims=True) * ebl)
            ds_scr[...] = ds_new
            dqs = dql * ep + dqu * em + dqg * eb
            dk = dkl * em + dku * ep + dkg * ekb
            db = dql * ql - dkl * kl - dqu * qu + dku * ku + dqg * qg - dkg * kg
            db = db + jnp.where(last_row, dbl, 0.0)
            dla = _dot_split(k["utri"], db, NN, a_exact=True)
            dz = dla * (1.0 / GATE_TAU) * _sigmoid(-z)
            dvec_ref[2:3, 0:GLA_KW] += jnp.sum(dz, axis=0, keepdims=True)
            dgw_ref[...] += _dot_split(glr, dz, TN)
            dproj_ref[rows, OFF_GLR:OFF_GLR + 128] = _dot(dz.astype(BF16), gw_ref[...].astype(BF16), NT).astype(BF16)
            dproj_ref[rows, OFF_GQ:OFF_GQ + GLA_KW] = (dqs * scale_g).astype(BF16)
            dproj_ref[rows, OFF_GK:OFF_GK + GLA_KW] = dk.astype(BF16)
            return carry

        _for_tiles(tc // tr, ret_tile)
        _for_tiles(tc // tg, gla_tile)

    rev = lambda i: (nsteps - 1 - i, 0)

    def col(width, off):
        return pl.BlockSpec((tc, width), lambda i, o=off // width: (nsteps - 1 - i, o))

    fix = lambda i: (0, 0)
    fix3 = lambda i: (0, 0, 0)
    dec, qkd, rot_in, rot_tile = tables
    half = pl.BlockSpec((tc, RET_HEADS * RET_D), rev)
    in_specs = [pl.BlockSpec((tc, D_MODEL), rev), half, half, col(512, OFF_RV), col(512, OFF_RG),
                col(256, OFF_GQ), col(256, OFF_GK), col(512, OFF_GV), col(512, OFF_GG), col(128, OFF_GLR),
                pl.BlockSpec((tc, D_MODEL), rev),
                pl.BlockSpec(dec.shape, fix3), pl.BlockSpec(qkd.shape, fix3), pl.BlockSpec(rot_in.shape, fix3),
                pl.BlockSpec((1, 8, 2 * RET_D), lambda i: (nsteps - 1 - i, 0, 0)),
                pl.BlockSpec((tc // tr, RET_HEADS * RET_D, RET_D), lambda i: (nsteps - 1 - i, 0, 0)),
                pl.BlockSpec((tc // tg, GLA_DV, GLA_KW), lambda i: (nsteps - 1 - i, 0, 0)),
                pl.BlockSpec((128, GLA_KW), fix), pl.BlockSpec((1, GLA_KW), fix),
                pl.BlockSpec((1, 512), fix), pl.BlockSpec((1, 512), fix)]
    out_specs = (pl.BlockSpec((tc, D_IN_PAD), rev), pl.BlockSpec((128, GLA_KW), fix), pl.BlockSpec((8, 512), fix))
    out_shape = (jax.ShapeDtypeStruct((t, D_IN_PAD), BF16), jax.ShapeDtypeStruct((128, GLA_KW), F32),
                 jax.ShapeDtypeStruct((8, 512), F32))
    return pl.pallas_call(
        body, name="mixer_bwd", grid=(nsteps,), in_specs=in_specs, out_specs=out_specs, out_shape=out_shape,
        scratch_shapes=[pltpu.VMEM((RET_HEADS * RET_D, RET_D), F32), pltpu.VMEM((GLA_DV, GLA_KW), F32)],
        compiler_params=pltpu.CompilerParams(dimension_semantics=("arbitrary",), vmem_limit_bytes=V7X_VMEM_LIMIT),
    )(dmix, qrb, krb, *([proj] * 7), oraw, dec, qkd, rot_in, rot_tile, rst, sst, gw_pad, gb, rnw, gnw)


def _inproj_bwd(dproj, x2d, dxa, sc1p, w_in_t):
    t = x2d.shape[0]
    tm = min(PROJ_TILE, t)

    def body(dp_ref, x_ref, dxa_ref, sc_ref, w_hbm, gx_ref, acc_ref, w_vmem, sem):
        first = pl.program_id(0) == 0
        _load_w_in_t(first, w_hbm, w_vmem, sem)

        @pl.when(first)
        def _():
            acc_ref[...] = jnp.zeros_like(acc_ref)

        du = _dot(dp_ref[...], w_vmem[...])
        xh, rstd = _ln_stats(x_ref[...])
        gx_ref[...] = dxa_ref[...] + _ln_bwd(du * sc_ref[...], xh, rstd)
        acc_ref[0:1, :] += jnp.sum(du * xh, axis=0, keepdims=True)
        acc_ref[1:2, :] += jnp.sum(du, axis=0, keepdims=True)

    row = lambda i: (i, 0)
    fix = lambda i: (0, 0)
    return pl.pallas_call(
        body, name="inproj_bwd", grid=(t // tm,),
        in_specs=[pl.BlockSpec((tm, D_IN_PAD), row), pl.BlockSpec((tm, D_MODEL), row), pl.BlockSpec((tm, D_MODEL), row),
                  pl.BlockSpec((1, D_MODEL), fix), pl.BlockSpec(memory_space=pl.ANY)],
        out_specs=(pl.BlockSpec((tm, D_MODEL), row), pl.BlockSpec((8, D_MODEL), fix)),
        out_shape=(jax.ShapeDtypeStruct((t, D_MODEL), F32), jax.ShapeDtypeStruct((8, D_MODEL), F32)),
        scratch_shapes=[pltpu.VMEM((D_IN_PAD, D_MODEL), BF16), pltpu.SemaphoreType.DMA((1,))],
        compiler_params=pltpu.CompilerParams(dimension_semantics=("arbitrary",), vmem_limit_bytes=V7X_VMEM_LIMIT),
    )(dproj, x2d, dxa, sc1p, w_in_t)


def _adam_math(w, g, m, v):
    m = ADAM_B1 * m + (1.0 - ADAM_B1) * g
    v = ADAM_B2 * v + (1.0 - ADAM_B2) * (g * g)
    m_hat = m / (1.0 - ADAM_B1 ** ADAM_STEP)
    v_hat = v / (1.0 - ADAM_B2 ** ADAM_STEP)
    delta = -ADAM_LR * (m_hat / (jnp.sqrt(v_hat) + ADAM_EPS) + ADAM_WD * w)
    return delta, m, v


def _adamw(w, gparts, m, v, name):
    nparts, rows, cols = gparts.shape
    tr = rows
    for cand in (512, 256, 128, 64, 32, 16, 8):
        if rows % cand == 0:
            tr = cand
            break

    def body(w_ref, g_ref, m_ref, v_ref, go_ref, d_ref, mo_ref, vo_ref):
        g = g_ref[0].astype(F32)
        for p in range(1, nparts):
            g = g + g_ref[p].astype(F32)
        delta, mn, vn = _adam_math(w_ref[...], g, m_ref[...], v_ref[...])
        go_ref[...] = g
        d_ref[...] = delta
        mo_ref[...] = mn
        vo_ref[...] = vn

    blk = pl.BlockSpec((tr, cols), lambda i: (i, 0))
    shp = jax.ShapeDtypeStruct((rows, cols), F32)
    return pl.pallas_call(
        body, name=name, grid=(rows // tr,),
        in_specs=[blk, pl.BlockSpec((nparts, tr, cols), lambda i: (0, i, 0)), blk, blk],
        out_specs=(blk, blk, blk, blk), out_shape=(shp, shp, shp, shp),
        compiler_params=pltpu.CompilerParams(dimension_semantics=("arbitrary",), vmem_limit_bytes=V7X_VMEM_LIMIT),
    )(w, gparts, m, v)


def _small_reduce(gathered, gathered_gw, c_all, dmod_cols):
    def body(g_ref, gw_ref, c_ref, dm_ref, sum_ref, gwsum_ref, gb_ref, gwa_ref):
        s = g_ref[0]
        sw = gw_ref[0]
        for p in range(1, N_DEV):
            s = s + g_ref[p]
            sw = sw + gw_ref[p]
        sum_ref[...] = s
        gwsum_ref[...] = sw
        for i in range(6):
            gb_ref[:, i * D_MODEL:(i + 1) * D_MODEL] = s[i:i + 1, :]
        cc = c_ref[...]
        gwa_ref[...] = _dot(cc * _sigmoid(cc), dm_ref[...], TN, HIGHEST)

    vm = pl.BlockSpec(memory_space=pltpu.VMEM)
    return pl.pallas_call(
        body, name="small_reduce",
        out_shape=(jax.ShapeDtypeStruct(gathered.shape[1:], F32), jax.ShapeDtypeStruct(gathered_gw.shape[1:], F32),
                   jax.ShapeDtypeStruct((1, 6 * D_MODEL), F32), jax.ShapeDtypeStruct((D_MODEL, ADA_COLS), F32)),
        in_specs=[vm] * 4, out_specs=(vm, vm, vm, vm),
        compiler_params=pltpu.CompilerParams(vmem_limit_bytes=V7X_VMEM_LIMIT),
    )(gathered, gathered_gw, c_all, dmod_cols)


SMR_LN1W, SMR_LN1B, SMR_LN2W, SMR_LN2B, SMR_NORMS, SMR_MISC = 6, 7, 8, 9, 10, 11


def _adamw_small(gsum, g_b_ada, g_ggw, params, moms, vels):
    n = len(params)

    def body(*refs):
        gsum_ref, gb_ref, gw_ref = refs[:3]
        w_refs, m_refs, v_refs = refs[3:3 + n], refs[3 + n:3 + 2 * n], refs[3 + 2 * n:3 + 3 * n]
        outs = refs[3 + 3 * n:]
        g_refs, d_refs, mo_refs, vo_refs = outs[:n - 1], outs[n - 1:2 * n - 1], outs[2 * n - 1:3 * n - 1], outs[3 * n - 1:]
        grads = [gb_ref[...],
                 gsum_ref[SMR_NORMS:SMR_NORMS + 1, 0:512],
                 gsum_ref[SMR_MISC:SMR_MISC + 1, 0:GLA_KW],
                 gsum_ref[SMR_NORMS:SMR_NORMS + 1, 512:1024],
                 gsum_ref[SMR_LN1W:SMR_LN1W + 1, :], gsum_ref[SMR_LN1B:SMR_LN1B + 1, :],
                 gsum_ref[SMR_LN2W:SMR_LN2W + 1, :], gsum_ref[SMR_LN2B:SMR_LN2B + 1, :],
                 gw_ref[...]]
        for i in range(n):
            delta, mn, vn = _adam_math(w_refs[i][...], grads[i], m_refs[i][...], v_refs[i][...])
            if i < n - 1:
                g_refs[i][...] = grads[i]
            d_refs[i][...] = delta
            mo_refs[i][...] = mn
            vo_refs[i][...] = vn

    vm = pl.BlockSpec(memory_space=pltpu.VMEM)
    shapes = [jax.ShapeDtypeStruct(p.shape, F32) for p in params]
    n_in = 3 + 3 * n
    out_shape = tuple(shapes[:n - 1] + shapes * 3)
    return pl.pallas_call(
        body, name="adamw_small", out_shape=out_shape,
        in_specs=[vm] * n_in, out_specs=tuple([vm] * len(out_shape)),
        compiler_params=pltpu.CompilerParams(vmem_limit_bytes=V7X_VMEM_LIMIT),
    )(gsum, g_b_ada, g_ggw, *params, *moms, *vels)


def kernel(x, c, w_ada, b_ada, w_in, ret_norm_w, gla_gate_w, gla_gate_b, gla_norm_w, w_out, ln1_w, ln1_b, w_ff1, w_ff2, ln2_w, ln2_b, loss_target, m_w_ada, m_b_ada, m_w_in, m_ret_norm_w, m_gla_gate_w, m_gla_gate_b, m_gla_norm_w, m_w_out, m_ln1_w, m_ln1_b, m_w_ff1, m_w_ff2, m_ln2_w, m_ln2_b, v_w_ada, v_b_ada, v_w_in, v_ret_norm_w, v_gla_gate_w, v_gla_gate_b, v_gla_norm_w, v_w_out, v_ln1_w, v_ln1_b, v_w_ff1, v_w_ff2, v_ln2_w, v_ln2_b):
    t = x.shape[1]
    xi, yi, ci = _my_coords()
    me = 4 * xi + 2 * yi + ci
    x2d = x[0]
    tgt = loss_target[0]

    c_ext = jnp.concatenate([c, gla_gate_w[0].reshape(1, GATE_RANK * GLA_KW // N_DEV)], axis=1)
    b_l = lax.dynamic_slice(b_ada, (0, me * ADA_COLS), (1, ADA_COLS))
    c_all3, mod_all, wi_g, ada_token = _adaln_mod(c_ext, w_ada[0], b_l, w_in[0].T.astype(BF16))

    wg = _exchange_start([(w_out[0] + ada_token[0, 0]).astype(BF16), w_ff1[0].astype(BF16), w_ff2[0].astype(BF16)],
                         True, "wgather_start")

    c_all = c_all3[:, 0, :D_MODEL]
    gate_w = c_all3[:, 0, D_MODEL:].reshape(N_DEV, GATE_RANK, GLA_KW // N_DEV)
    gate_w = gate_w.transpose(1, 0, 2).reshape(GATE_RANK, GLA_KW)
    gw_pad = jnp.zeros((128, GLA_KW), F32).at[:GATE_RANK].set(gate_w)
    mod = lax.dynamic_slice(mod_all, (0, me, 0), (N_DEV, 1, ADA_COLS)).reshape(6, D_MODEL)
    shift1, scale1, gate1, shift2, scale2, gate2 = [mod[i:i + 1] for i in range(6)]

    w_in_t = wi_g.reshape(D_IN, D_MODEL)

    tables = _ret_tables(t, min(MIX_TILE, t))

    sc1p = 1.0 + scale1
    proj, u = _inproj_fwd(x2d, sc1p, shift1 + wg[4][0, 0], w_in_t)
    mixed, oraw, qrb, krb, rst, sst = _mixer_fwd(proj, tables, gw_pad, gla_gate_b, ret_norm_w, gla_norm_w)
    wo_g, w1_b, w2_g = _exchange_wait(*wg[:4], mixed, True, "wgather_wait")
    w_out_b = wo_g.reshape(D_MODEL, D_MODEL)
    w2_b = w2_g.reshape(D_FF, D_MODEL)
    vec_f = jnp.concatenate([gate1, 1.0 + scale2, shift2, gate2, ln1_w, ln1_b, ln2_w, ln2_b], axis=0)
    m, x1n, rstd1, u2, a, df, dh2, acc_f = _mid_fwd(mixed, x2d, tgt, vec_f, w_out_b, w1_b, w2_b)

    vec_b = jnp.concatenate([gate1, 1.0 + scale2, ln1_w, ln1_b, jnp.zeros((4, D_MODEL), F32)], axis=0)
    da, dm, dmix, dxa, acc_b = _ffn_bwd(df, a, dh2, x1n, rstd1, m, vec_b, w_out_b, w1_b, w2_b)
    dw2 = _matmul_tn(a, df, 2048, 1024, 1024, "tn_dw2", relu_sq=True)
    dw1 = _matmul_tn(u2, da, 1024, 2048, 1024, "tn_dw1", col_slab=FF_COLS)
    dwo = _matmul_tn(mixed, dm, 1024, 1024, 1024, "tn_dwout")
    gx = _exchange_start([dwo.reshape(N_DEV, OUT_ROWS, D_MODEL), dw1, dw2.reshape(N_DEV, FF_COLS, D_MODEL)], False,
                         "gradx_start")
    dproj, dgw, dvec = _mixer_bwd(dmix, proj, qrb, krb, oraw, tables, rst, sst, gw_pad,
                                  gla_gate_b + gx[4][0, 0], ret_norm_w, gla_norm_w)
    dwi = _matmul_tn(u, dproj, 1024, D_IN_PAD, 512, "tn_dwin")
    dwi_s = dwi[:, :D_IN].reshape(D_MODEL, N_DEV, IN_COLS).transpose(1, 0, 2)
    gi = _exchange_start([dwi_s], False, "gradin_start")
    grad_x, acc_i = _inproj_bwd(dproj, x2d, dxa, sc1p + gi[4][0, 0], w_in_t)
    r_wo, r_w1, r_w2 = _exchange_wait(*gx[:4], acc_i, False, "gradx_wait")
    r_wi, = _exchange_wait(*gi[:4], acc_i, False, "gradin_wait")

    loss_part = jnp.sum(acc_f[3])
    small = jnp.concatenate([
        acc_i[1:2], acc_i[0:1], acc_b[4:5], acc_b[1:2], acc_b[0:1], acc_f[2:3],
        acc_b[2:3], acc_b[3:4], acc_f[0:1], acc_f[1:2],
        jnp.concatenate([dvec[0:1], dvec[1:2]], axis=1),
        jnp.concatenate([dvec[2:3, :GLA_KW], jnp.full((1, 128), loss_part, F32),
                         jnp.zeros((1, D_MODEL - GLA_KW - 128), F32)], axis=1),
        jnp.zeros((4, D_MODEL), F32)], axis=0)
    small_all, gw_all = _small_gather([small, dgw[:GATE_RANK]])
    dmod_all = small_all[:, :6].reshape(N_DEV, 6 * D_MODEL)
    dmod_cols = lax.dynamic_slice(dmod_all, (0, me * ADA_COLS), (N_DEV, ADA_COLS))
    ssum, gw_sum, g_b_ada, g_w_ada = _small_reduce(small_all, gw_all, c_all, dmod_cols)
    loss = ssum[SMR_MISC, GLA_KW]
    g_ggw = lax.dynamic_slice(gw_sum, (0, me * (GLA_KW // N_DEV)), (GATE_RANK, GLA_KW // N_DEV))[None]

    small_w = [b_ada, ret_norm_w, gla_gate_b, gla_norm_w, ln1_w, ln1_b, ln2_w, ln2_b, gla_gate_w]
    small_m = [m_b_ada, m_ret_norm_w, m_gla_gate_b, m_gla_norm_w, m_ln1_w, m_ln1_b, m_ln2_w, m_ln2_b, m_gla_gate_w]
    small_v = [v_b_ada, v_ret_norm_w, v_gla_gate_b, v_gla_norm_w, v_ln1_w, v_ln1_b, v_ln2_w, v_ln2_b, v_gla_gate_w]
    res = _adamw_small(ssum, g_b_ada, g_ggw, small_w, small_m, small_v)
    small_g = list(res[:8]) + [g_ggw]
    d_small, m_small, v_small = list(res[8:17]), list(res[17:26]), list(res[26:35])

    _, d_w_ada, nm_w_ada, nv_w_ada = _adamw(w_ada[0], g_w_ada[None], m_w_ada[0], v_w_ada[0], "adamw_ada")

    big =[_adamw(w[0], r, m_[0], v_[0], nm) for w, r, m_, v_, nm in (
        (w_in, r_wi, m_w_in, v_w_in, "adamw_in"), (w_out, r_wo, m_w_out, v_w_out, "adamw_out"),
        (w_ff1, r_w1, m_w_ff1, v_w_ff1, "adamw_ff1"), (w_ff2, r_w2, m_w_ff2, v_w_ff2, "adamw_ff2"))]
    g_big, d_big, m_big, v_big = [[b[i][None] for b in big] for i in range(4)]

    def ordered(w_ada_v, small_vals, big_vals):
        b_ada_v, rnw_v, ggb_v, gnw_v, l1w_v, l1b_v, l2w_v, l2b_v, ggw_v = small_vals
        wi_v, wo_v, w1_v, w2_v = big_vals
        return [w_ada_v, b_ada_v, wi_v, rnw_v, ggw_v, ggb_v, gnw_v, wo_v, l1w_v, l1b_v, w1_v, w2_v, l2w_v, l2b_v]

    grads = ordered(g_w_ada[None], small_g, g_big)
    deltas = ordered(d_w_ada[None], d_small, d_big)
    new_m = ordered(nm_w_ada[None], m_small, m_big)
    new_v = ordered(nv_w_ada[None], v_small, v_big)
    return (loss, grad_x[None], *grads, *deltas, *new_m, *new_v)
```

```python
import functools

import numpy as np
import jax
import jax.numpy as jnp
from jax import lax
from jax.experimental import pallas as pl
from jax.experimental.pallas import tpu as pltpu

F32 = jnp.float32
BF16 = jnp.bfloat16
MESH = pl.DeviceIdType.MESH
HIGHEST = lax.Precision.HIGHEST

N_DEV = 8
D_MODEL = 1024
CHUNK = 64
RET_HEADS = 4
RET_D = 128
GLA_HEADS = 4
GLA_DK = 64
GLA_DV = 128
GLA_KW = GLA_HEADS * GLA_DK
GATE_RANK = 16
GATE_TAU = 16.0
D_FF = 4096
LN_EPS = 1e-5
ALPHA = (2.0 * 1) ** 0.25
D_IN = 3600
D_IN_PAD = 3712
ADA_COLS = 6 * D_MODEL // N_DEV
IN_COLS = D_IN // N_DEV
FF_COLS = D_FF // N_DEV
OUT_ROWS = D_MODEL // N_DEV

OFF_RQ, OFF_RK, OFF_RV, OFF_RG = 0, 512, 1024, 1536
OFF_GQ, OFF_GK, OFF_GV, OFF_GG, OFF_GLR = 2048, 2304, 2560, 3072, 3584

ADAM_LR, ADAM_B1, ADAM_B2, ADAM_EPS, ADAM_WD, ADAM_STEP = 0.001, 0.9, 0.999, 1e-08, 0.01, 10

V7X_VMEM_LIMIT = 62 * 1024 * 1024

ROW_TILE = 512
PROJ_TILE = 512
MIX_TILE = 256
GLA_SUB = 128


def _log_gamma(h):
    return float(np.log(np.float32(1.0) - np.float32(2.0) ** np.float32(-5.0 - h)))


def _my_coords():
    return lax.axis_index("x"), lax.axis_index("y"), lax.axis_index("c")


def _flip(v, bit):
    return 1 - v if bit else v


def _peer(k):
    x, y, c = _my_coords()
    px, py, pc = _flip(x, (k >> 2) & 1), _flip(y, (k >> 1) & 1), _flip(c, k & 1)
    return (px, py, pc), 4 * px + 2 * py + pc


def _dot(a, b, dims=(((1,), (0,)), ((), ())), precision=None):
    return lax.dot_general(a, b, dims, precision=precision, preferred_element_type=F32)


NN = (((1,), (0,)), ((), ()))
NT = (((1,), (1,)), ((), ()))
TN = (((0,), (0,)), ((), ()))


def _split_bf16(v, parts):
    out = []
    for _ in range(parts):
        p = v.astype(BF16)
        out.append(p)
        v = v - p.astype(F32)
    return out


def _dot_split(a, b, dims, a_exact=False):
    if a_exact:
        ab = a.astype(BF16)
        return sum(_dot(ab, p, dims) for p in _split_bf16(b, 3))
    a_hi, a_lo = _split_bf16(a, 2)
    b_hi, b_lo = _split_bf16(b, 2)
    return _dot(a_hi, b_hi, dims) + _dot(a_hi, b_lo, dims) + _dot(a_lo, b_hi, dims)


def _sigmoid(x):
    return 1.0 / (1.0 + jnp.exp(-x))


def _ln_stats(x):
    mu = jnp.mean(x, axis=-1, keepdims=True)
    xc = x - mu
    var = jnp.mean(xc * xc, axis=-1, keepdims=True)
    rstd = lax.rsqrt(var + LN_EPS)
    return xc * rstd, rstd


def _ln_bwd(dyh, xh, rstd):
    return rstd * (dyh - jnp.mean(dyh, axis=-1, keepdims=True) - xh * jnp.mean(dyh * xh, axis=-1, keepdims=True))


def _adaln_mod(c_ext, w_ada_l, b_l, w_in_l):
    width = c_ext.shape[1]

    def body(c_ref, w_ref, b_ref, wi_ref, call_ref, mod_ref, wig_ref, token_ref, s1, r1, s2, r2, gs, gr, gl):
        gather = _TwoLevelGather([wi_ref], [wig_ref], gs, gr, gl)
        gather.start()
        token_ref[...] = jnp.zeros_like(token_ref)
        x, y, c = _my_coords()
        me = 4 * x + 2 * y + c
        call_ref[me] = c_ref[...]
        sends = []
        for k in range(1, N_DEV):
            peer, _ = _peer(k)
            cp = pltpu.make_async_remote_copy(c_ref, call_ref.at[me], s1.at[k - 1], r1.at[k - 1],
                                              device_id=peer, device_id_type=MESH)
            cp.start()
            sends.append(cp)
        for k in range(1, N_DEV):
            peer, pid = _peer(k)
            pltpu.make_async_remote_copy(c_ref, call_ref.at[pid], s1.at[k - 1], r1.at[k - 1],
                                         device_id=peer, device_id_type=MESH).wait_recv()
        for cp in sends:
            cp.wait_send()
        row = lax.broadcasted_iota(jnp.int32, (N_DEV, D_MODEL), 0)
        call = jnp.zeros((N_DEV, D_MODEL), F32)
        for j in range(N_DEV):
            call = jnp.where(row == j, jnp.broadcast_to(call_ref[j][:, :D_MODEL], (N_DEV, D_MODEL)), call)
        sc = call * _sigmoid(call)
        mod = _dot(sc, w_ref[...], NN, HIGHEST) + b_ref[...]
        mod_ref[me] = mod
        sends = []
        for k in range(1, N_DEV):
            peer, _ = _peer(k)
            cp = pltpu.make_async_remote_copy(mod_ref.at[me], mod_ref.at[me], s2.at[k - 1], r2.at[k - 1],
                                              device_id=peer, device_id_type=MESH)
            cp.start()
            sends.append(cp)
        for k in range(1, N_DEV):
            peer, pid = _peer(k)
            pltpu.make_async_remote_copy(mod_ref.at[pid], mod_ref.at[pid], s2.at[k - 1], r2.at[k - 1],
                                         device_id=peer, device_id_type=MESH).wait_recv()
        for cp in sends:
            cp.wait_send()
        gather.forward()
        gather.finish()

    vm = pl.BlockSpec(memory_space=pltpu.VMEM)
    hbm = pl.BlockSpec(memory_space=pl.ANY)
    return pl.pallas_call(
        body, name="adaln_mod",
        out_shape=(jax.ShapeDtypeStruct((N_DEV, 1, width), F32),
                   jax.ShapeDtypeStruct((N_DEV, N_DEV, ADA_COLS), F32),
                   jax.ShapeDtypeStruct((N_DEV, *w_in_l.shape), w_in_l.dtype),
                   jax.ShapeDtypeStruct((8, 128), F32)),
        in_specs=[vm, vm, vm, hbm], out_specs=(vm, vm, hbm, vm),
        scratch_shapes=[pltpu.SemaphoreType.DMA((N_DEV - 1,))] * 4
        + [pltpu.SemaphoreType.DMA((7,)), pltpu.SemaphoreType.DMA((7,)), pltpu.SemaphoreType.DMA((1,))],
        compiler_params=pltpu.CompilerParams(vmem_limit_bytes=V7X_VMEM_LIMIT),
    )(c_ext, w_ada_l, b_l, w_in_l)


class _TwoLevelGather:
    def __init__(self, x_refs, out_refs, send_sems, recv_sems, local_sems):
        self.x_refs, self.out_refs = x_refs, out_refs
        self.send_sems, self.recv_sems, self.local_sems = send_sems, recv_sems, local_sems
        x, y, c = _my_coords()
        self.c = c
        self.me, self.sibling = (x, y, c), (x, y, 1 - c)
        self.chips = [(1 - x, y), (x, 1 - y), (1 - x, 1 - y)]

    def _copy(self, a, k, block, to, src=None):
        px, py, pc = block
        slab = self.out_refs[a].at[4 * px + 2 * py + pc]
        return pltpu.make_async_remote_copy(
            src_ref=slab if src is None else src, dst_ref=slab,
            send_sem=self.send_sems.at[7 * a + k], recv_sem=self.recv_sems.at[7 * a + k],
            device_id=to, device_id_type=MESH)

    def _mine(self, a):
        px, py, pc = self.me
        return pltpu.make_async_copy(self.x_refs[a], self.out_refs[a].at[4 * px + 2 * py + pc], self.local_sems.at[a])

    def _first(self, a):
        cps = [self._copy(a, 0, self.me, self.sibling, src=self.x_refs[a])]
        cps += [self._copy(a, 1 + j, self.me, (*chip, self.c), src=self.x_refs[a]) for j, chip in enumerate(self.chips)]
        return cps

    def _passed(self, a):
        return [self._copy(a, 4 + j, (*chip, self.c), self.sibling) for j, chip in enumerate(self.chips)]

    def start(self):
        for a in range(len(self.x_refs)):
            self._mine(a).start()
            for cp in self._first(a):
                cp.start()

    def forward(self):
        for a in range(len(self.x_refs)):
            passed = self._passed(a)
            for j, chip in enumerate(self.chips):
                self._copy(a, 1 + j, (*chip, self.c), self.me).wait_recv()
                passed[j].start()

    def finish(self):
        for a in range(len(self.x_refs)):
            self._copy(a, 0, self.sibling, self.me).wait_recv()
            for j, chip in enumerate(self.chips):
                self._copy(a, 4 + j, (*chip, 1 - self.c), self.me).wait_recv()
            for cp in self._first(a) + self._passed(a):
                cp.wait_send()
            self._mine(a).wait()


def _exchange_copy(src_refs, land_refs, send_sems, recv_sems, a, k, gather, receiving):
    x, y, c = _my_coords()
    me = 4 * x + 2 * y + c
    peer, pid = _peer(k)
    src = src_refs[a] if gather else src_refs[a].at[pid]
    dst = land_refs[a].at[pid if receiving else me]
    return pltpu.make_async_remote_copy(src, dst, send_sems.at[7 * a + k - 1], recv_sems.at[7 * a + k - 1],
                                        device_id=peer, device_id_type=MESH)


def _exchange_start(srcs, gather, name):
    n = len(srcs)
    xi, yi, ci = _my_coords()
    me = 4 * xi + 2 * yi + ci
    lands = []
    for s in srcs:
        own = s[None] if gather else lax.dynamic_slice_in_dim(s, me, 1, axis=0)
        lands.append(lax.dynamic_update_slice_in_dim(lax.empty((N_DEV, *own.shape[1:]), s.dtype), own, me, axis=0))

    def body(*refs):
        src_refs, land_refs, send_sems, recv_sems, token = refs[:n], refs[n:2 * n], refs[2 * n], refs[2 * n + 1], refs[-1]
        for a in range(n):
            for k in range(1, N_DEV):
                _exchange_copy(src_refs, land_refs, send_sems, recv_sems, a, k, gather, receiving=False).start()
        token[...] = jnp.zeros_like(token)

    hbm = pl.BlockSpec(memory_space=pltpu.HBM)
    sem = pl.BlockSpec(memory_space=pltpu.SEMAPHORE)
    res = pl.pallas_call(
        body, name=name,
        out_shape=(pltpu.SemaphoreType.DMA((7 * n,)), pltpu.SemaphoreType.DMA((7 * n,)),
                   *[pltpu.HBM(v.shape, v.dtype) for v in srcs + lands], jax.ShapeDtypeStruct((8, 128), F32)),
        in_specs=[hbm] * (2 * n),
        out_specs=(sem, sem, *([hbm] * (2 * n)), pl.BlockSpec(memory_space=pltpu.VMEM)),
        input_output_aliases={i: 2 + i for i in range(2 * n)},
        compiler_params=pltpu.CompilerParams(has_side_effects=pltpu.SideEffectType.DATAFLOW_SIDE_EFFECTING),
    )(*[pltpu.with_memory_space_constraint(v, pltpu.HBM) for v in srcs + lands])
    return res[0], res[1], list(res[2:2 + n]), list(res[2 + n:2 + 2 * n]), res[-1]


def _exchange_wait(send_sems, recv_sems, srcs, lands, after, gather, name):
    n = len(srcs)

    def body(*refs):
        src_refs, land_refs, s_sems, r_sems = refs[:n], refs[n:2 * n], refs[2 * n], refs[2 * n + 1]
        for a in range(n):
            for k in range(1, N_DEV):
                _exchange_copy(src_refs, land_refs, s_sems, r_sems, a, k, gather, receiving=False).wait_send()
                _exchange_copy(src_refs, land_refs, s_sems, r_sems, a, k, gather, receiving=True).wait_recv()

    hbm = pl.BlockSpec(memory_space=pltpu.HBM)
    sem = pl.BlockSpec(memory_space=pltpu.SEMAPHORE)
    res = pl.pallas_call(
        body, name=name,
        out_shape=tuple(pltpu.HBM(v.shape, v.dtype) for v in srcs + lands),
        in_specs=[hbm] * (2 * n) + [sem, sem, pl.BlockSpec(memory_space=pl.ANY)],
        out_specs=tuple([hbm] * (2 * n)),
        input_output_aliases={i: i for i in range(2 * n)},
        compiler_params=pltpu.CompilerParams(has_side_effects=pltpu.SideEffectType.DATAFLOW_SIDE_EFFECTING),
    )(*srcs, *lands, send_sems, recv_sems, after)
    return list(res[n:])


def _small_gather(vecs):
    n = len(vecs)

    def body(*refs):
        v_refs, out_refs, s_sems, r_sems = refs[:n], refs[n:2 * n], refs[2 * n], refs[2 * n + 1]
        x, y, c = _my_coords()
        me = 4 * x + 2 * y + c
        sends = []
        for a in range(n):
            out_refs[a][me] = v_refs[a][...]
            for k in range(1, N_DEV):
                peer, _ = _peer(k)
                cp = pltpu.make_async_remote_copy(v_refs[a], out_refs[a].at[me], s_sems.at[7 * a + k - 1],
                                                  r_sems.at[7 * a + k - 1], device_id=peer, device_id_type=MESH)
                cp.start()
                sends.append(cp)
        for a in range(n):
            for k in range(1, N_DEV):
                peer, pid = _peer(k)
                pltpu.make_async_remote_copy(v_refs[a], out_refs[a].at[pid], s_sems.at[7 * a + k - 1],
                                             r_sems.at[7 * a + k - 1], device_id=peer, device_id_type=MESH).wait_recv()
        for cp in sends:
            cp.wait_send()

    vm = pl.BlockSpec(memory_space=pltpu.VMEM)
    return pl.pallas_call(
        body, name="small_gather",
        out_shape=tuple(jax.ShapeDtypeStruct((N_DEV, *v.shape), v.dtype) for v in vecs),
        in_specs=[vm] * n, out_specs=tuple([vm] * n),
        scratch_shapes=[pltpu.SemaphoreType.DMA((7 * n,))] * 2,
    )(*vecs)


def _load_resident(step_is_first, pairs, sem):
    @pl.when(step_is_first)
    def _():
        copies = [pltpu.make_async_copy(src, dst, sem.at[i]) for i, (src, dst) in enumerate(pairs)]
        for cp in copies:
            cp.start()
        for cp in copies:
            cp.wait()


def _load_w_in_t(step_is_first, w_hbm, w_vmem, sem):
    @pl.when(step_is_first)
    def _():
        w_vmem[D_IN:, :] = jnp.zeros((D_IN_PAD - D_IN, D_MODEL), BF16)
    _load_resident(step_is_first, [(w_hbm, w_vmem.at[pl.ds(0, D_IN)])], sem)


def _inproj_fwd(x2d, sc1p, sh1, w_in_t):
    t = x2d.shape[0]
    tm = min(PROJ_TILE, t)

    def body(x_ref, sc_ref, sh_ref, w_hbm, proj_ref, u_ref, w_vmem, sem):
        _load_w_in_t(pl.program_id(0) == 0, w_hbm, w_vmem, sem)
        xh, _ = _ln_stats(x_ref[...])
        ub = (xh * sc_ref[...] + sh_ref[...]).astype(BF16)
        u_ref[...] = ub
        proj_ref[...] = _dot(ub, w_vmem[...], NT)

    row = lambda i: (i, 0)
    fix = lambda i: (0, 0)
    return pl.pallas_call(
        body, name="inproj_fwd", grid=(t // tm,),
        in_specs=[pl.BlockSpec((tm, D_MODEL), row), pl.BlockSpec((1, D_MODEL), fix), pl.BlockSpec((1, D_MODEL), fix),
                  pl.BlockSpec(memory_space=pl.ANY)],
        out_specs=(pl.BlockSpec((tm, D_IN_PAD), row), pl.BlockSpec((tm, D_MODEL), row)),
        out_shape=(jax.ShapeDtypeStruct((t, D_IN_PAD), F32), jax.ShapeDtypeStruct((t, D_MODEL), BF16)),
        scratch_shapes=[pltpu.VMEM((D_IN_PAD, D_MODEL), BF16), pltpu.SemaphoreType.DMA((1,))],
        compiler_params=pltpu.CompilerParams(dimension_semantics=("arbitrary",), vmem_limit_bytes=V7X_VMEM_LIMIT),
    )(x2d, sc1p, sh1, w_in_t)


CHUNK_SHIFT = 6


def _ret_tables(t, tl):
    r = lax.broadcasted_iota(jnp.int32, (tl, tl), 0)
    c = lax.broadcasted_iota(jnp.int32, (tl, tl), 1)
    allowed = jnp.right_shift(c, CHUNK_SHIFT) <= jnp.right_shift(r, CHUNK_SHIFT)
    dist = jnp.abs(r - c).astype(F32)
    rowf = lax.broadcasted_iota(jnp.int32, (tl, RET_D), 0).astype(F32)
    lgs = [_log_gamma(h) for h in range(RET_HEADS)]
    dec = jnp.stack([jnp.where(allowed, jnp.exp(lg * dist), 0.0) for lg in lgs])
    qkd = jnp.stack([jnp.exp(lg * (rowf + 1.0)) for lg in lgs] + [jnp.exp(lg * (tl - 1.0 - rowf)) for lg in lgs])
    inv = 1.0 / (10000.0 ** jnp.linspace(0.0, 1.0, RET_D // 2, dtype=F32))
    off = jnp.arange(tl, dtype=F32)[:, None] * inv[None, :]
    start = (jnp.arange(t // tl, dtype=F32) * tl)[:, None] * inv[None, :]
    co, so = jnp.cos(off), jnp.sin(off)
    rot_in = jnp.stack([jnp.concatenate([co, co], 1), jnp.concatenate([so, so], 1),
                        jnp.concatenate([-co, co], 1), jnp.concatenate([-so, so], 1)])
    cs, ss = jnp.cos(start), jnp.sin(start)
    rot_tile = jnp.concatenate([cs, cs, ss, ss], axis=1)
    rot_tile = jnp.broadcast_to(rot_tile[:, None, :], (t // tl, 8, 2 * RET_D))
    return dec, qkd, rot_in, rot_tile


def _tile_gammas(tl):
    return [float(np.exp(np.float32(_log_gamma(h)) * np.float32(tl))) for h in range(RET_HEADS)]


def _tile_rotary(rot_in_ref, rot_tile_ref):
    ca, sa = rot_tile_ref[0, 0:1, 0:RET_D], rot_tile_ref[0, 0:1, RET_D:2 * RET_D]
    cosv = ca * rot_in_ref[0] - sa * rot_in_ref[1]
    sinv = sa * rot_in_ref[2] + ca * rot_in_ref[3]
    return cosv, sinv


def _gla_consts(tl):
    r = lax.broadcasted_iota(jnp.int32, (tl, tl), 0)
    c = lax.broadcasted_iota(jnp.int32, (tl, tl), 1)
    ltri = (c <= r).astype(F32)
    utri = (c >= r).astype(F32)
    lane = lax.broadcasted_iota(jnp.int32, (1, GLA_KW), 1)
    hmask = [((lane >= h * GLA_DK) & (lane < (h + 1) * GLA_DK)).astype(F32) for h in range(GLA_HEADS)]
    rs = lax.broadcasted_iota(jnp.int32, (GLA_HEADS * tl, tl), 0) & (tl - 1)
    cs = lax.broadcasted_iota(jnp.int32, (GLA_HEADS * tl, tl), 1)
    lower = cs <= rs
    same = jnp.right_shift(cs, CHUNK_SHIFT) == jnp.right_shift(rs, CHUNK_SHIFT)
    upper = jnp.logical_and(jnp.logical_not(lower), same)
    return dict(ltri=ltri, utri=utri, hmask=hmask, lower=lower, upper=upper)


def _tile_rows(j, tl):
    return pl.ds(j * tl, tl) if isinstance(j, int) else pl.ds(pl.multiple_of(j * tl, tl), tl)


def _for_tiles(cps, fn):
    if cps == 1:
        fn(0, 0)
    else:
        lax.fori_loop(0, cps, fn, 0)


def _rotate(v, cosv, sinv):
    return v * cosv + pltpu.roll(v, RET_D // 2, 1) * sinv


def _rotate_t(d, cosv, sinv):
    return d * cosv + pltpu.roll(d * sinv, RET_D // 2, 1)


def _stack_heads(v, hmask):
    return jnp.concatenate([v * hmask[h] for h in range(GLA_HEADS)], axis=0)


def _gla_gates(glr, gw, gb, ltri, tl):
    z = _dot_split(glr, gw, NN) + gb
    la = (jnp.minimum(z, 0.0) - jnp.log(1.0 + jnp.exp(-jnp.abs(z)))) * (1.0 / GATE_TAU)
    b = _dot_split(ltri, la, NN, a_exact=True)
    level = b[tl // 2 - 1:tl // 2, :]
    ep = jnp.exp(jnp.clip(b - level, -80.0, 80.0))
    em = jnp.exp(jnp.clip(level - b, -80.0, 80.0))
    bl = b[tl - 1:tl, :]
    return z, b, bl, ep, em


def _mixer_fwd(proj, tables, gw_pad, gb, rnw, gnw):
    t = proj.shape[0]
    tc = min(MIX_TILE, t)
    tr, tg = tc, min(GLA_SUB, tc)
    nsteps = t // tc
    scale_r = RET_D ** -0.5
    scale_g = GLA_DK ** -0.5
    gammas = _tile_gammas(tr)

    def body(rq_ref, rk_ref, rv_ref, rg_ref, gq_ref, gk_ref, gv_ref, gg_ref, glr_ref,
             dec_ref, qkd_ref, rot_in_ref, rot_tile_ref, gw_ref, gb_ref, rnw_ref, gnw_ref,
             mix_ref, oraw_ref, qrb_ref, krb_ref, rst_ref, sst_ref, r_scr, s_scr):
        @pl.when(pl.program_id(0) == 0)
        def _():
            r_scr[...] = jnp.zeros_like(r_scr)
            s_scr[...] = jnp.zeros_like(s_scr)

        gla_k = _gla_consts(tg)

        def ret_tile(j, carry):
            rows = _tile_rows(j, tr)
            cosv, sinv = _tile_rotary(rot_in_ref, rot_tile_ref)
            for h in range(RET_HEADS):
                cols = slice(h * RET_D, (h + 1) * RET_D)
                qr = _rotate(rq_ref[rows, cols], cosv, sinv) * scale_r
                kr = _rotate(rk_ref[rows, cols], cosv, sinv)
                vb = rv_ref[rows, cols].astype(BF16)
                qb, kb = qr.astype(BF16), kr.astype(BF16)
                qrb_ref[rows, cols] = qb
                krb_ref[rows, cols] = kb
                p = _dot(qb, kb, NT) * dec_ref[h]
                rp = r_scr[cols, :]
                o = _dot(p.astype(BF16), vb) + _dot((qr * qkd_ref[h]).astype(BF16), rp.astype(BF16))
                rst_ref[j, cols, :] = rp
                r_scr[cols, :] = gammas[h] * rp + _dot((kr * qkd_ref[RET_HEADS + h]).astype(BF16), vb, TN)
                oraw_ref[rows, cols] = o
                oc = o - jnp.mean(o, axis=-1, keepdims=True)
                n = oc * lax.rsqrt(jnp.mean(oc * oc, axis=-1, keepdims=True) + LN_EPS)
                g = rg_ref[rows, cols]
                mix_ref[rows, cols] = (n * rnw_ref[:, cols] * (g * _sigmoid(g))).astype(BF16)
            return carry

        def gla_tile(j, carry):
            k = gla_k
            tl = tg
            rows = _tile_rows(j, tg)
            _, b, bl, ep, em = _gla_gates(glr_ref[rows, :], gw_ref[...], gb_ref[...], k["ltri"], tl)
            qs = gq_ref[rows, :] * scale_g
            kk = gk_ref[rows, :]
            x_all = _dot(_stack_heads(qs * ep, k["hmask"]).astype(BF16), (kk * em).astype(BF16), NT)
            y_all = _dot(_stack_heads(qs * em, k["hmask"]).astype(BF16), (kk * ep).astype(BF16), NT)
            a_all = jnp.where(k["lower"], x_all, jnp.where(k["upper"], y_all, 0.0)).astype(BF16)
            st = s_scr[...]
            oq = _dot(_stack_heads(qs * jnp.exp(b), k["hmask"]).astype(BF16), st.astype(BF16), NT)
            kg = kk * jnp.exp(bl - b)
            sst_ref[j] = st
            st_new = st * jnp.exp(bl)
            for h in range(GLA_HEADS):
                cols = slice(h * GLA_DV, (h + 1) * GLA_DV)
                hr = slice(h * tl, (h + 1) * tl)
                vb = gv_ref[rows, cols].astype(BF16)
                o = _dot(a_all[hr, :], vb) + oq[hr, :]
                st_new = st_new + _dot(vb, (kg * k["hmask"][h]).astype(BF16), TN)
                ocols = slice(RET_HEADS * RET_D + h * GLA_DV, RET_HEADS * RET_D + (h + 1) * GLA_DV)
                oraw_ref[rows, ocols] = o
                n = o * lax.rsqrt(jnp.mean(o * o, axis=-1, keepdims=True) + LN_EPS)
                g = gg_ref[rows, cols]
                mix_ref[rows, ocols] = (n * gnw_ref[:, cols] * (g * _sigmoid(g))).astype(BF16)
            s_scr[...] = st_new
            return carry

        _for_tiles(tc // tr, ret_tile)
        _for_tiles(tc // tg, gla_tile)

    def col(width, off):
        return pl.BlockSpec((tc, width), lambda i, o=off // width: (i, o))

    fix = lambda i: (0, 0)
    fix3 = lambda i: (0, 0, 0)
    dec, qkd, rot_in, rot_tile = tables
    in_specs = [col(512, OFF_RQ), col(512, OFF_RK), col(512, OFF_RV), col(512, OFF_RG),
                col(256, OFF_GQ), col(256, OFF_GK), col(512, OFF_GV), col(512, OFF_GG), col(128, OFF_GLR),
                pl.BlockSpec(dec.shape, fix3), pl.BlockSpec(qkd.shape, fix3), pl.BlockSpec(rot_in.shape, fix3),
                pl.BlockSpec((1, 8, 2 * RET_D), lambda i: (i, 0, 0)),
                pl.BlockSpec((128, GLA_KW), fix), pl.BlockSpec((1, GLA_KW), fix),
                pl.BlockSpec((1, 512), fix), pl.BlockSpec((1, 512), fix)]
    half = pl.BlockSpec((tc, RET_HEADS * RET_D), lambda i: (i, 0))
    out_specs = (pl.BlockSpec((tc, D_MODEL), lambda i: (i, 0)), pl.BlockSpec((tc, D_MODEL), lambda i: (i, 0)),
                 half, half,
                 pl.BlockSpec((tc // tr, RET_HEADS * RET_D, RET_D), lambda i: (i, 0, 0)),
                 pl.BlockSpec((tc // tg, GLA_DV, GLA_KW), lambda i: (i, 0, 0)))
    out_shape = (jax.ShapeDtypeStruct((t, D_MODEL), BF16), jax.ShapeDtypeStruct((t, D_MODEL), F32),
                 jax.ShapeDtypeStruct((t, RET_HEADS * RET_D), BF16), jax.ShapeDtypeStruct((t, RET_HEADS * RET_D), BF16),
                 jax.ShapeDtypeStruct((t // tr, RET_HEADS * RET_D, RET_D), F32),
                 jax.ShapeDtypeStruct((t // tg, GLA_DV, GLA_KW), F32))
    return pl.pallas_call(
        body, name="mixer_fwd", grid=(nsteps,), in_specs=in_specs, out_specs=out_specs, out_shape=out_shape,
        scratch_shapes=[pltpu.VMEM((RET_HEADS * RET_D, RET_D), F32), pltpu.VMEM((GLA_DV, GLA_KW), F32)],
        compiler_params=pltpu.CompilerParams(dimension_semantics=("arbitrary",), vmem_limit_bytes=V7X_VMEM_LIMIT),
    )(*([proj] * 9), dec, qkd, rot_in, rot_tile, gw_pad, gb, rnw, gnw)


def _mid_fwd(mixed, x2d, target, vecs, w_out_b, w1_b, w2_b):
    t = x2d.shape[0]
    tm = min(ROW_TILE, t)

    def body(mix_ref, x_ref, tgt_ref, v_ref, wo_hbm, w1_hbm, w2_hbm,
             m_ref, x1n_ref, rstd_ref, u2_ref, a_ref, df_ref, dh2_ref, acc_ref, wo, w1, w2, sem):
        first = pl.program_id(0) == 0
        _load_resident(first, [(wo_hbm, wo), (w1_hbm, w1), (w2_hbm, w2)], sem)

        @pl.when(first)
        def _():
            acc_ref[...] = jnp.zeros_like(acc_ref)

        gate1, sc2p, sh2, gate2 = v_ref[0:1, :], v_ref[1:2, :], v_ref[2:3, :], v_ref[3:4, :]
        l1w, l1b, l2w, l2b = v_ref[4:5, :], v_ref[5:6, :], v_ref[6:7, :], v_ref[7:8, :]
        m = _dot(mix_ref[...], wo[...])
        m_ref[...] = m.astype(BF16)
        x1n, rstd1 = _ln_stats(ALPHA * x_ref[...] + gate1 * m)
        x1n_ref[...] = x1n
        rstd_ref[...] = rstd1
        x1 = x1n * l1w + l1b
        xh1, _ = _ln_stats(x1)
        u2 = (xh1 * sc2p + sh2).astype(BF16)
        u2_ref[...] = u2
        f = jnp.zeros((tm, D_MODEL), F32)
        for j in range(N_DEV):
            cols = slice(j * FF_COLS, (j + 1) * FF_COLS)
            a = _dot(u2, w1[j])
            a_ref[:, cols] = a.astype(BF16)
            r = jnp.maximum(a, 0.0)
            f = f + _dot((r * r).astype(BF16), w2[cols, :])
        yh, rstd2 = _ln_stats(ALPHA * x1 + gate2 * f)
        e = yh * l2w + l2b - tgt_ref[...]
        dy = e * (1.0 / D_MODEL)
        dh2 = _ln_bwd(dy * l2w, yh, rstd2)
        dh2_ref[...] = dh2
        df_ref[...] = (dh2 * gate2).astype(BF16)
        acc_ref[0:1, :] += jnp.sum(dy * yh, axis=0, keepdims=True)
        acc_ref[1:2, :] += jnp.sum(dy, axis=0, keepdims=True)
        acc_ref[2:3, :] += jnp.sum(dh2 * f, axis=0, keepdims=True)
        acc_ref[3:4, :] += jnp.sum(e * e, axis=0, keepdims=True) * (0.5 / D_MODEL)

    row = lambda i: (i, 0)
    fix = lambda i: (0, 0)
    hbm = pl.BlockSpec(memory_space=pl.ANY)
    return pl.pallas_call(
        body, name="mid_fwd", grid=(t // tm,),
        in_specs=[pl.BlockSpec((tm, D_MODEL), row), pl.BlockSpec((tm, D_MODEL), row), pl.BlockSpec((tm, D_MODEL), row),
                  pl.BlockSpec((8, D_MODEL), fix), hbm, hbm, hbm],
        out_specs=(pl.BlockSpec((tm, D_MODEL), row), pl.BlockSpec((tm, D_MODEL), row), pl.BlockSpec((tm, 1), row),
                   pl.BlockSpec((tm, D_MODEL), row), pl.BlockSpec((tm, D_FF), row), pl.BlockSpec((tm, D_MODEL), row),
                   pl.BlockSpec((tm, D_MODEL), row), pl.BlockSpec((8, D_MODEL), fix)),
        out_shape=(jax.ShapeDtypeStruct((t, D_MODEL), BF16), jax.ShapeDtypeStruct((t, D_MODEL), F32),
                   jax.ShapeDtypeStruct((t, 1), F32), jax.ShapeDtypeStruct((t, D_MODEL), BF16),
                   jax.ShapeDtypeStruct((t, D_FF), BF16), jax.ShapeDtypeStruct((t, D_MODEL), BF16),
                   jax.ShapeDtypeStruct((t, D_MODEL), F32), jax.ShapeDtypeStruct((8, D_MODEL), F32)),
        scratch_shapes=[pltpu.VMEM((D_MODEL, D_MODEL), BF16), pltpu.VMEM((N_DEV, D_MODEL, FF_COLS), BF16),
                        pltpu.VMEM((D_FF, D_MODEL), BF16), pltpu.SemaphoreType.DMA((3,))],
        compiler_params=pltpu.CompilerParams(dimension_semantics=("arbitrary",), vmem_limit_bytes=V7X_VMEM_LIMIT),
    )(mixed, x2d, target, vecs, w_out_b, w1_b, w2_b)


def _ffn_bwd(df, a, dh2, x1n, rstd1, m, vecs, w_out_b, w1_b, w2_b):
    t = x1n.shape[0]
    tm = min(ROW_TILE, t)

    def body(df_ref, a_ref, dh2_ref, x1n_ref, rstd_ref, m_ref, v_ref, wo_hbm, w1_hbm, w2_hbm,
             da_ref, dm_ref, dmix_ref, dxa_ref, acc_ref, wo, w1, w2, sem):
        first = pl.program_id(0) == 0
        _load_resident(first, [(wo_hbm, wo), (w1_hbm, w1), (w2_hbm, w2)], sem)

        @pl.when(first)
        def _():
            acc_ref[...] = jnp.zeros_like(acc_ref)

        gate1, sc2p, l1w, l1b = v_ref[0:1, :], v_ref[1:2, :], v_ref[2:3, :], v_ref[3:4, :]
        df = df_ref[...]
        du2 = jnp.zeros((tm, D_MODEL), F32)
        for j in range(N_DEV):
            cols = slice(j * FF_COLS, (j + 1) * FF_COLS)
            dr2 = _dot(df, w2[cols, :], NT)
            da = (dr2 * (2.0 * jnp.maximum(a_ref[:, cols].astype(F32), 0.0))).astype(BF16)
            da_ref[:, cols] = da
            du2 = du2 + _dot(da, w1[j], NT)
        x1n = x1n_ref[...]
        xh1, rstd0 = _ln_stats(x1n * l1w + l1b)
        dx1 = ALPHA * dh2_ref[...] + _ln_bwd(du2 * sc2p, xh1, rstd0)
        dh1 = _ln_bwd(dx1 * l1w, x1n, rstd_ref[...])
        dxa_ref[...] = ALPHA * dh1
        dm = (dh1 * gate1).astype(BF16)
        dm_ref[...] = dm
        dmix_ref[...] = _dot(dm, wo[...], NT)
        acc_ref[0:1, :] += jnp.sum(du2 * xh1, axis=0, keepdims=True)
        acc_ref[1:2, :] += jnp.sum(du2, axis=0, keepdims=True)
        acc_ref[2:3, :] += jnp.sum(dx1 * x1n, axis=0, keepdims=True)
        acc_ref[3:4, :] += jnp.sum(dx1, axis=0, keepdims=True)
        acc_ref[4:5, :] += jnp.sum(dh1 * m_ref[...].astype(F32), axis=0, keepdims=True)

    row = lambda i: (i, 0)
    fix = lambda i: (0, 0)
    hbm = pl.BlockSpec(memory_space=pl.ANY)
    return pl.pallas_call(
        body, name="ffn_bwd", grid=(t // tm,),
        in_specs=[pl.BlockSpec((tm, D_MODEL), row), pl.BlockSpec((tm, D_FF), row), pl.BlockSpec((tm, D_MODEL), row),
                  pl.BlockSpec((tm, D_MODEL), row), pl.BlockSpec((tm, 1), row), pl.BlockSpec((tm, D_MODEL), row),
                  pl.BlockSpec((8, D_MODEL), fix), hbm, hbm, hbm],
        out_specs=(pl.BlockSpec((tm, D_FF), row), pl.BlockSpec((tm, D_MODEL), row), pl.BlockSpec((tm, D_MODEL), row),
                   pl.BlockSpec((tm, D_MODEL), row), pl.BlockSpec((8, D_MODEL), fix)),
        out_shape=(jax.ShapeDtypeStruct((t, D_FF), BF16), jax.ShapeDtypeStruct((t, D_MODEL), BF16),
                   jax.ShapeDtypeStruct((t, D_MODEL), F32), jax.ShapeDtypeStruct((t, D_MODEL), F32),
                   jax.ShapeDtypeStruct((8, D_MODEL), F32)),
        scratch_shapes=[pltpu.VMEM((D_MODEL, D_MODEL), BF16), pltpu.VMEM((N_DEV, D_MODEL, FF_COLS), BF16),
                        pltpu.VMEM((D_FF, D_MODEL), BF16), pltpu.SemaphoreType.DMA((3,))],
        compiler_params=pltpu.CompilerParams(dimension_semantics=("arbitrary",), vmem_limit_bytes=V7X_VMEM_LIMIT),
    )(df, a, dh2, x1n, rstd1, m, vecs, w_out_b, w1_b, w2_b)


def _matmul_tn(lhs, rhs, tmm, tn, tk, name, relu_sq=False, col_slab=None):
    t, mm = lhs.shape
    nn = rhs.shape[1]
    tk = min(tk, t)
    nk = t // tk

    def body(l_ref, r_ref, o_ref, acc):
        kk = pl.program_id(2)

        @pl.when(kk == 0)
        def _():
            acc[...] = jnp.zeros_like(acc)

        l = l_ref[...]
        if relu_sq:
            lf = jnp.maximum(l.astype(F32), 0.0)
            l = (lf * lf).astype(BF16)
        acc[...] += _dot(l, r_ref[...], TN)

        @pl.when(kk == nk - 1)
        def _():
            if col_slab is None:
                o_ref[...] = acc[...].astype(o_ref.dtype)
            else:
                for s in range(tn // col_slab):
                    o_ref[s] = acc[:, s * col_slab:(s + 1) * col_slab].astype(o_ref.dtype)

    if col_slab is None:
        out_spec = pl.BlockSpec((tmm, tn), lambda i, j, k: (i, j))
        out_shape = jax.ShapeDtypeStruct((mm, nn), BF16)
    else:
        out_spec = pl.BlockSpec((tn // col_slab, tmm, col_slab), lambda i, j, k: (j, i, 0))
        out_shape = jax.ShapeDtypeStruct((nn // col_slab, mm, col_slab), BF16)
    return pl.pallas_call(
        body, name=name, grid=(mm // tmm, nn // tn, nk),
        in_specs=[pl.BlockSpec((tk, tmm), lambda i, j, k: (k, i)), pl.BlockSpec((tk, tn), lambda i, j, k: (k, j))],
        out_specs=out_spec,
        out_shape=out_shape,
        scratch_shapes=[pltpu.VMEM((tmm, tn), F32)],
        compiler_params=pltpu.CompilerParams(dimension_semantics=("arbitrary", "arbitrary", "arbitrary"),
                                             vmem_limit_bytes=V7X_VMEM_LIMIT),
    )(lhs, rhs)


def _mixer_bwd(dmix, proj, qrb, krb, oraw, tables, rst, sst, gw_pad, gb, rnw, gnw):
    t = proj.shape[0]
    tc = min(MIX_TILE, t)
    tr, tg = tc, min(GLA_SUB, tc)
    nsteps = t // tc
    scale_r = RET_D ** -0.5
    scale_g = GLA_DK ** -0.5
    gammas = _tile_gammas(tr)

    def body(dmix_ref, qrb_ref, krb_ref, rv_ref, rg_ref, gq_ref, gk_ref, gv_ref, gg_ref, glr_ref, oraw_ref,
             dec_ref, qkd_ref, rot_in_ref, rot_tile_ref, rst_ref, sst_ref, gw_ref, gb_ref, rnw_ref, gnw_ref,
             dproj_ref, dgw_ref, dvec_ref, dr_scr, ds_scr):
        @pl.when(pl.program_id(0) == 0)
        def _():
            dr_scr[...] = jnp.zeros_like(dr_scr)
            ds_scr[...] = jnp.zeros_like(ds_scr)
            dgw_ref[...] = jnp.zeros_like(dgw_ref)
            dvec_ref[...] = jnp.zeros_like(dvec_ref)

        gla_k = _gla_consts(tg)
        last_row = lax.broadcasted_iota(jnp.int32, (tg, GLA_KW), 0) == tg - 1

        def ret_tile(jj, carry):
            j = tc // tr - 1 - jj
            rows = _tile_rows(j, tr)
            cosv, sinv = _tile_rotary(rot_in_ref, rot_tile_ref)
            for h in range(RET_HEADS):
                cols = slice(h * RET_D, (h + 1) * RET_D)
                o = oraw_ref[rows, cols]
                g = rg_ref[rows, cols]
                w = rnw_ref[:, cols]
                dout = dmix_ref[rows, cols]
                oc = o - jnp.mean(o, axis=-1, keepdims=True)
                inv = lax.rsqrt(jnp.mean(oc * oc, axis=-1, keepdims=True) + LN_EPS)
                n = oc * inv
                sg = _sigmoid(g)
                sil = g * sg
                dn = dout * w * sil
                dvec_ref[0:1, cols] += jnp.sum(dout * n * sil, axis=0, keepdims=True)
                dproj_ref[rows, OFF_RG + h * RET_D:OFF_RG + (h + 1) * RET_D] = (
                    dout * n * w * (sg * (1.0 + g * (1.0 - sg)))).astype(BF16)
                doc = inv * (dn - n * jnp.mean(dn * n, axis=-1, keepdims=True))
                do = doc - jnp.mean(doc, axis=-1, keepdims=True)

                qb, kb = qrb_ref[rows, cols], krb_ref[rows, cols]
                qr, kr = qb.astype(F32), kb.astype(F32)
                vb = rv_ref[rows, cols].astype(BF16)
                dob = do.astype(BF16)
                qd, kd = qkd_ref[h], qkd_ref[RET_HEADS + h]
                p = _dot(qb, kb, NT) * dec_ref[h]
                rp = rst_ref[j, cols, :].astype(BF16)
                dr = dr_scr[cols, :]
                drb = dr.astype(BF16)
                dpb = (_dot(dob, vb, NT) * dec_ref[h]).astype(BF16)
                dqr = _dot(dpb, kb) + _dot(dob, rp, NT) * qd
                dkr = _dot(dpb, qb, TN) + _dot(vb, drb, NT) * kd
                dv = _dot(p.astype(BF16), dob, TN) + _dot((kr * kd).astype(BF16), drb)
                dr_scr[cols, :] = gammas[h] * dr + _dot((qr * qd).astype(BF16), dob, TN)
                dproj_ref[rows, OFF_RQ + h * RET_D:OFF_RQ + (h + 1) * RET_D] = (
                    _rotate_t(dqr, cosv, sinv) * scale_r).astype(BF16)
                dproj_ref[rows, OFF_RK + h * RET_D:OFF_RK + (h + 1) * RET_D] = _rotate_t(dkr, cosv, sinv).astype(BF16)
                dproj_ref[rows, OFF_RV + h * RET_D:OFF_RV + (h + 1) * RET_D] = dv.astype(BF16)
            return carry

        def gla_tile(jj, carry):
            k = gla_k
            tl = tg
            j = tc // tg - 1 - jj
            rows = _tile_rows(j, tg)
            glr = glr_ref[rows, :]
            z, b, bl, ep, em = _gla_gates(glr, gw_ref[...], gb_ref[...], k["ltri"], tl)
            qs = gq_ref[rows, :] * scale_g
            kk = gk_ref[rows, :]
            eb = jnp.exp(b)
            ekb = jnp.exp(bl - b)
            ebl = jnp.exp(bl)
            ql, qu, kl, ku = qs * ep, qs * em, kk * em, kk * ep
            qg, kg = qs * eb, kk * ekb
            qlm = _stack_heads(ql, k["hmask"]).astype(BF16)
            qum = _stack_heads(qu, k["hmask"]).astype(BF16)
            klb, kub = kl.astype(BF16), ku.astype(BF16)
            a_all = jnp.where(k["lower"], _dot(qlm, klb, NT),
                              jnp.where(k["upper"], _dot(qum, kub, NT), 0.0)).astype(BF16)
            st = sst_ref[j]
            stb = st.astype(BF16)
            ds = ds_scr[...]
            dsb = ds.astype(BF16)
            ds_new = ds * ebl
            da_parts = []
            dqg = jnp.zeros((tl, GLA_KW), F32)
            dkg = jnp.zeros((tl, GLA_KW), F32)
            for h in range(GLA_HEADS):
                cols = slice(h * GLA_DV, (h + 1) * GLA_DV)
                hr = slice(h * tl, (h + 1) * tl)
                ocols = slice(RET_HEADS * RET_D + h * GLA_DV, RET_HEADS * RET_D + (h + 1) * GLA_DV)
                o = oraw_ref[rows, ocols]
                g = gg_ref[rows, cols]
                w = gnw_ref[:, cols]
                dout = dmix_ref[rows, ocols]
                inv = lax.rsqrt(jnp.mean(o * o, axis=-1, keepdims=True) + LN_EPS)
                n = o * inv
                sg = _sigmoid(g)
                sil = g * sg
                dn = dout * w * sil
                dvec_ref[1:2, cols] += jnp.sum(dout * n * sil, axis=0, keepdims=True)
                dproj_ref[rows, OFF_GG + h * GLA_DV:OFF_GG + (h + 1) * GLA_DV] = (
                    dout * n * w * (sg * (1.0 + g * (1.0 - sg)))).astype(BF16)
                dob = (inv * (dn - n * jnp.mean(dn * n, axis=-1, keepdims=True))).astype(BF16)
                vb = gv_ref[rows, cols].astype(BF16)
                mh = k["hmask"][h]
                da_parts.append(_dot(dob, vb, NT))
                dv = _dot(a_all[hr, :], dob, TN) + _dot((kg * mh).astype(BF16), dsb, NT)
                dproj_ref[rows, OFF_GV + h * GLA_DV:OFF_GV + (h + 1) * GLA_DV] = dv.astype(BF16)
                dkg = dkg + mh * _dot(vb, dsb)
                dqg = dqg + mh * _dot(dob, stb)
                ds_new = ds_new + _dot(dob, (qg * mh).astype(BF16), TN)
            da_all = jnp.concatenate(da_parts, axis=0)
            dal = jnp.where(k["lower"], da_all, 0.0).astype(BF16)
            dau = jnp.where(k["upper"], da_all, 0.0).astype(BF16)
            dqlm = _dot(dal, klb)
            dqum = _dot(dau, kub)
            dql = jnp.zeros((tl, GLA_KW), F32)
            dqu = jnp.zeros((tl, GLA_KW), F32)
            for h in range(GLA_HEADS):
                hr = slice(h * tl, (h + 1) * tl)
                dql = dql + k["hmask"][h] * dqlm[hr, :]
                dqu = dqu + k["hmask"][h] * dqum[hr, :]
            dkl = _dot(dal, qlm, TN)
            dku = _dot(dau, qum, TN)
            dbl = (jnp.sum(dkg * kg, axis=0, keepdims=True)
                   + jnp.sum(ds * st, axis=0, keepdims=True) * ebl)
            ds_scr[...] = ds_new
            dqs = dql * ep + dqu * em + dqg * eb
            dk = dkl * em + dku * ep + dkg * ekb
            db = dql * ql - dkl * kl - dqu * qu + dku * ku + dqg * qg - dkg * kg
            db = db + jnp.where(last_row, dbl, 0.0)
            dla = _dot_split(k["utri"], db, NN, a_exact=True)
            dz = dla * (1.0 / GATE_TAU) * _sigmoid(-z)
            dvec_ref[2:3, 0:GLA_KW] += jnp.sum(dz, axis=0, keepdims=True)
            dgw_ref[...] += _dot_split(glr, dz, TN)
            dproj_ref[rows, OFF_GLR:OFF_GLR + 128] = _dot(dz.astype(BF16), gw_ref[...].astype(BF16), NT).astype(BF16)
            dproj_ref[rows, OFF_GQ:OFF_GQ + GLA_KW] = (dqs * scale_g).astype(BF16)
            dproj_ref[rows, OFF_GK:OFF_GK + GLA_KW] = dk.astype(BF16)
            return carry

        _for_tiles(tc // tr, ret_tile)
        _for_tiles(tc // tg, gla_tile)

    rev = lambda i: (nsteps - 1 - i, 0)

    def col(width, off):
        return pl.BlockSpec((tc, width), lambda i, o=off // width: (nsteps - 1 - i, o))

    fix = lambda i: (0, 0)
    fix3 = lambda i: (0, 0, 0)
    dec, qkd, rot_in, rot_tile = tables
    half = pl.BlockSpec((tc, RET_HEADS * RET_D), rev)
    in_specs = [pl.BlockSpec((tc, D_MODEL), rev), half, half, col(512, OFF_RV), col(512, OFF_RG),
                col(256, OFF_GQ), col(256, OFF_GK), col(512, OFF_GV), col(512, OFF_GG), col(128, OFF_GLR),
                pl.BlockSpec((tc, D_MODEL), rev),
                pl.BlockSpec(dec.shape, fix3), pl.BlockSpec(qkd.shape, fix3), pl.BlockSpec(rot_in.shape, fix3),
                pl.BlockSpec((1, 8, 2 * RET_D), lambda i: (nsteps - 1 - i, 0, 0)),
                pl.BlockSpec((tc // tr, RET_HEADS * RET_D, RET_D), lambda i: (nsteps - 1 - i, 0, 0)),
                pl.BlockSpec((tc // tg, GLA_DV, GLA_KW), lambda i: (nsteps - 1 - i, 0, 0)),
                pl.BlockSpec((128, GLA_KW), fix), pl.BlockSpec((1, GLA_KW), fix),
                pl.BlockSpec((1, 512), fix), pl.BlockSpec((1, 512), fix)]
    out_specs = (pl.BlockSpec((tc, D_IN_PAD), rev), pl.BlockSpec((128, GLA_KW), fix), pl.BlockSpec((8, 512), fix))
    out_shape = (jax.ShapeDtypeStruct((t, D_IN_PAD), BF16), jax.ShapeDtypeStruct((128, GLA_KW), F32),
                 jax.ShapeDtypeStruct((8, 512), F32))
    return pl.pallas_call(
        body, name="mixer_bwd", grid=(nsteps,), in_specs=in_specs, out_specs=out_specs, out_shape=out_shape,
        scratch_shapes=[pltpu.VMEM((RET_HEADS * RET_D, RET_D), F32), pltpu.VMEM((GLA_DV, GLA_KW), F32)],
        compiler_params=pltpu.CompilerParams(dimension_semantics=("arbitrary",), vmem_limit_bytes=V7X_VMEM_LIMIT),
    )(dmix, qrb, krb, *([proj] * 7), oraw, dec, qkd, rot_in, rot_tile, rst, sst, gw_pad, gb, rnw, gnw)


def _inproj_bwd(dproj, x2d, dxa, sc1p, w_in_t):
    t = x2d.shape[0]
    tm = min(PROJ_TILE, t)

    def body(dp_ref, x_ref, dxa_ref, sc_ref, w_hbm, gx_ref, acc_ref, w_vmem, sem):
        first = pl.program_id(0) == 0
        _load_w_in_t(first, w_hbm, w_vmem, sem)

        @pl.when(first)
        def _():
            acc_ref[...] = jnp.zeros_like(acc_ref)

        du = _dot(dp_ref[...], w_vmem[...])
        xh, rstd = _ln_stats(x_ref[...])
        gx_ref[...] = dxa_ref[...] + _ln_bwd(du * sc_ref[...], xh, rstd)
        acc_ref[0:1, :] += jnp.sum(du * xh, axis=0, keepdims=True)
        acc_ref[1:2, :] += jnp.sum(du, axis=0, keepdims=True)

    row = lambda i: (i, 0)
    fix = lambda i: (0, 0)
    return pl.pallas_call(
        body, name="inproj_bwd", grid=(t // tm,),
        in_specs=[pl.BlockSpec((tm, D_IN_PAD), row), pl.BlockSpec((tm, D_MODEL), row), pl.BlockSpec((tm, D_MODEL), row),
                  pl.BlockSpec((1, D_MODEL), fix), pl.BlockSpec(memory_space=pl.ANY)],
        out_specs=(pl.BlockSpec((tm, D_MODEL), row), pl.BlockSpec((8, D_MODEL), fix)),
        out_shape=(jax.ShapeDtypeStruct((t, D_MODEL), F32), jax.ShapeDtypeStruct((8, D_MODEL), F32)),
        scratch_shapes=[pltpu.VMEM((D_IN_PAD, D_MODEL), BF16), pltpu.SemaphoreType.DMA((1,))],
        compiler_params=pltpu.CompilerParams(dimension_semantics=("arbitrary",), vmem_limit_bytes=V7X_VMEM_LIMIT),
    )(dproj, x2d, dxa, sc1p, w_in_t)


def _adam_math(w, g, m, v):
    m = ADAM_B1 * m + (1.0 - ADAM_B1) * g
    v = ADAM_B2 * v + (1.0 - ADAM_B2) * (g * g)
    m_hat = m / (1.0 - ADAM_B1 ** ADAM_STEP)
    v_hat = v / (1.0 - ADAM_B2 ** ADAM_STEP)
    delta = -ADAM_LR * (m_hat / (jnp.sqrt(v_hat) + ADAM_EPS) + ADAM_WD * w)
    return delta, m, v


def _adamw(w, gparts, m, v, name):
    nparts, rows, cols = gparts.shape
    tr = rows
    for cand in (512, 256, 128, 64, 32, 16, 8):
        if rows % cand == 0:
            tr = cand
            break

    def body(w_ref, g_ref, m_ref, v_ref, go_ref, d_ref, mo_ref, vo_ref):
        g = g_ref[0].astype(F32)
        for p in range(1, nparts):
            g = g + g_ref[p].astype(F32)
        delta, mn, vn = _adam_math(w_ref[...], g, m_ref[...], v_ref[...])
        go_ref[...] = g
        d_ref[...] = delta
        mo_ref[...] = mn
        vo_ref[...] = vn

    blk = pl.BlockSpec((tr, cols), lambda i: (i, 0))
    shp = jax.ShapeDtypeStruct((rows, cols), F32)
    return pl.pallas_call(
        body, name=name, grid=(rows // tr,),
        in_specs=[blk, pl.BlockSpec((nparts, tr, cols), lambda i: (0, i, 0)), blk, blk],
        out_specs=(blk, blk, blk, blk), out_shape=(shp, shp, shp, shp),
        compiler_params=pltpu.CompilerParams(dimension_semantics=("arbitrary",), vmem_limit_bytes=V7X_VMEM_LIMIT),
    )(w, gparts, m, v)


def _small_reduce(gathered, gathered_gw, c_all, dmod_cols):
    def body(g_ref, gw_ref, c_ref, dm_ref, sum_ref, gwsum_ref, gb_ref, gwa_ref):
        s = g_ref[0]
        sw = gw_ref[0]
        for p in range(1, N_DEV):
            s = s + g_ref[p]
            sw = sw + gw_ref[p]
        sum_ref[...] = s
        gwsum_ref[...] = sw
        for i in range(6):
            gb_ref[:, i * D_MODEL:(i + 1) * D_MODEL] = s[i:i + 1, :]
        cc = c_ref[...]
        gwa_ref[...] = _dot(cc * _sigmoid(cc), dm_ref[...], TN, HIGHEST)

    vm = pl.BlockSpec(memory_space=pltpu.VMEM)
    return pl.pallas_call(
        body, name="small_reduce",
        out_shape=(jax.ShapeDtypeStruct(gathered.shape[1:], F32), jax.ShapeDtypeStruct(gathered_gw.shape[1:], F32),
                   jax.ShapeDtypeStruct((1, 6 * D_MODEL), F32), jax.ShapeDtypeStruct((D_MODEL, ADA_COLS), F32)),
        in_specs=[vm] * 4, out_specs=(vm, vm, vm, vm),
        compiler_params=pltpu.CompilerParams(vmem_limit_bytes=V7X_VMEM_LIMIT),
    )(gathered, gathered_gw, c_all, dmod_cols)


SMR_LN1W, SMR_LN1B, SMR_LN2W, SMR_LN2B, SMR_NORMS, SMR_MISC = 6, 7, 8, 9, 10, 11


def _adamw_small(gsum, g_b_ada, g_ggw, params, moms, vels):
    n = len(params)

    def body(*refs):
        gsum_ref, gb_ref, gw_ref = refs[:3]
        w_refs, m_refs, v_refs = refs[3:3 + n], refs[3 + n:3 + 2 * n], refs[3 + 2 * n:3 + 3 * n]
        outs = refs[3 + 3 * n:]
        g_refs, d_refs, mo_refs, vo_refs = outs[:n - 1], outs[n - 1:2 * n - 1], outs[2 * n - 1:3 * n - 1], outs[3 * n - 1:]
        grads = [gb_ref[...],
                 gsum_ref[SMR_NORMS:SMR_NORMS + 1, 0:512],
                 gsum_ref[SMR_MISC:SMR_MISC + 1, 0:GLA_KW],
                 gsum_ref[SMR_NORMS:SMR_NORMS + 1, 512:1024],
                 gsum_ref[SMR_LN1W:SMR_LN1W + 1, :], gsum_ref[SMR_LN1B:SMR_LN1B + 1, :],
                 gsum_ref[SMR_LN2W:SMR_LN2W + 1, :], gsum_ref[SMR_LN2B:SMR_LN2B + 1, :],
                 gw_ref[...]]
        for i in range(n):
            delta, mn, vn = _adam_math(w_refs[i][...], grads[i], m_refs[i][...], v_refs[i][...])
            if i < n - 1:
                g_refs[i][...] = grads[i]
            d_refs[i][...] = delta
            mo_refs[i][...] = mn
            vo_refs[i][...] = vn

    vm = pl.BlockSpec(memory_space=pltpu.VMEM)
    shapes = [jax.ShapeDtypeStruct(p.shape, F32) for p in params]
    n_in = 3 + 3 * n
    out_shape = tuple(shapes[:n - 1] + shapes * 3)
    return pl.pallas_call(
        body, name="adamw_small", out_shape=out_shape,
        in_specs=[vm] * n_in, out_specs=tuple([vm] * len(out_shape)),
        compiler_params=pltpu.CompilerParams(vmem_limit_bytes=V7X_VMEM_LIMIT),
    )(gsum, g_b_ada, g_ggw, *params, *moms, *vels)


def kernel(x, c, w_ada, b_ada, w_in, ret_norm_w, gla_gate_w, gla_gate_b, gla_norm_w, w_out, ln1_w, ln1_b, w_ff1, w_ff2, ln2_w, ln2_b, loss_target, m_w_ada, m_b_ada, m_w_in, m_ret_norm_w, m_gla_gate_w, m_gla_gate_b, m_gla_norm_w, m_w_out, m_ln1_w, m_ln1_b, m_w_ff1, m_w_ff2, m_ln2_w, m_ln2_b, v_w_ada, v_b_ada, v_w_in, v_ret_norm_w, v_gla_gate_w, v_gla_gate_b, v_gla_norm_w, v_w_out, v_ln1_w, v_ln1_b, v_w_ff1, v_w_ff2, v_ln2_w, v_ln2_b):
    t = x.shape[1]
    xi, yi, ci = _my_coords()
    me = 4 * xi + 2 * yi + ci
    x2d = x[0]
    tgt = loss_target[0]

    c_ext = jnp.concatenate([c, gla_gate_w[0].reshape(1, GATE_RANK * GLA_KW // N_DEV)], axis=1)
    b_l = lax.dynamic_slice(b_ada, (0, me * ADA_COLS), (1, ADA_COLS))
    c_all3, mod_all, wi_g, ada_token = _adaln_mod(c_ext, w_ada[0], b_l, w_in[0].T.astype(BF16))

    wg = _exchange_start([(w_out[0] + ada_token[0, 0]).astype(BF16), w_ff1[0].astype(BF16), w_ff2[0].astype(BF16)],
                         True, "wgather_start")

    c_all = c_all3[:, 0, :D_MODEL]
    gate_w = c_all3[:, 0, D_MODEL:].reshape(N_DEV, GATE_RANK, GLA_KW // N_DEV)
    gate_w = gate_w.transpose(1, 0, 2).reshape(GATE_RANK, GLA_KW)
    gw_pad = jnp.zeros((128, GLA_KW), F32).at[:GATE_RANK].set(gate_w)
    mod = lax.dynamic_slice(mod_all, (0, me, 0), (N_DEV, 1, ADA_COLS)).reshape(6, D_MODEL)
    shift1, scale1, gate1, shift2, scale2, gate2 = [mod[i:i + 1] for i in range(6)]

    w_in_t = wi_g.reshape(D_IN, D_MODEL)

    tables = _ret_tables(t, min(MIX_TILE, t))

    sc1p = 1.0 + scale1
    proj, u = _inproj_fwd(x2d, sc1p, shift1 + wg[4][0, 0], w_in_t)
    mixed, oraw, qrb, krb, rst, sst = _mixer_fwd(proj, tables, gw_pad, gla_gate_b, ret_norm_w, gla_norm_w)
    wo_g, w1_b, w2_g = _exchange_wait(*wg[:4], mixed, True, "wgather_wait")
    w_out_b = wo_g.reshape(D_MODEL, D_MODEL)
    w2_b = w2_g.reshape(D_FF, D_MODEL)
    vec_f = jnp.concatenate([gate1, 1.0 + scale2, shift2, gate2, ln1_w, ln1_b, ln2_w, ln2_b], axis=0)
    m, x1n, rstd1, u2, a, df, dh2, acc_f = _mid_fwd(mixed, x2d, tgt, vec_f, w_out_b, w1_b, w2_b)

    vec_b = jnp.concatenate([gate1, 1.0 + scale2, ln1_w, ln1_b, jnp.zeros((4, D_MODEL), F32)], axis=0)
    da, dm, dmix, dxa, acc_b = _ffn_bwd(df, a, dh2, x1n, rstd1, m, vec_b, w_out_b, w1_b, w2_b)
    dw2 = _matmul_tn(a, df, 2048, 1024, 2048, "tn_dw2", relu_sq=True)
    dw1 = _matmul_tn(u2, da, 1024, 2048, 2048, "tn_dw1", col_slab=FF_COLS)
    dwo = _matmul_tn(mixed, dm, 1024, 1024, 2048, "tn_dwout")
    gx = _exchange_start([dwo.reshape(N_DEV, OUT_ROWS, D_MODEL), dw1, dw2.reshape(N_DEV, FF_COLS, D_MODEL)], False,
                         "gradx_start")
    dproj, dgw, dvec = _mixer_bwd(dmix, proj, qrb, krb, oraw, tables, rst, sst, gw_pad,
                                  gla_gate_b + gx[4][0, 0], ret_norm_w, gla_norm_w)
    dwi_t = _matmul_tn(dproj, u, D_IN_PAD, 1024, 1024, "tn_dwin")
    dwi_s = dwi_t[:D_IN].reshape(N_DEV, IN_COLS, D_MODEL)
    gi = _exchange_start([dwi_s], False, "gradin_start")
    grad_x, acc_i = _inproj_bwd(dproj, x2d, dxa, sc1p + gi[4][0, 0], w_in_t)
    r_wo, r_w1, r_w2 = _exchange_wait(*gx[:4], acc_i, False, "gradx_wait")
    r_wi, = _exchange_wait(*gi[:4], acc_i, False, "gradin_wait")

    loss_part = jnp.sum(acc_f[3])
    small = jnp.concatenate([
        acc_i[1:2], acc_i[0:1], acc_b[4:5], acc_b[1:2], acc_b[0:1], acc_f[2:3],
        acc_b[2:3], acc_b[3:4], acc_f[0:1], acc_f[1:2],
        jnp.concatenate([dvec[0:1], dvec[1:2]], axis=1),
        jnp.concatenate([dvec[2:3, :GLA_KW], jnp.full((1, 128), loss_part, F32),
                         jnp.zeros((1, D_MODEL - GLA_KW - 128), F32)], axis=1),
        jnp.zeros((4, D_MODEL), F32)], axis=0)
    small_all, gw_all = _small_gather([small, dgw[:GATE_RANK]])
    dmod_all = small_all[:, :6].reshape(N_DEV, 6 * D_MODEL)
    dmod_cols = lax.dynamic_slice(dmod_all, (0, me * ADA_COLS), (N_DEV, ADA_COLS))
    ssum, gw_sum, g_b_ada, g_w_ada = _small_reduce(small_all, gw_all, c_all, dmod_cols)
    loss = ssum[SMR_MISC, GLA_KW]
    g_ggw = lax.dynamic_slice(gw_sum, (0, me * (GLA_KW // N_DEV)), (GATE_RANK, GLA_KW // N_DEV))[None]

    small_w = [b_ada, ret_norm_w, gla_gate_b, gla_norm_w, ln1_w, ln1_b, ln2_w, ln2_b, gla_gate_w]
    small_m = [m_b_ada, m_ret_norm_w, m_gla_gate_b, m_gla_norm_w, m_ln1_w, m_ln1_b, m_ln2_w, m_ln2_b, m_gla_gate_w]
    small_v = [v_b_ada, v_ret_norm_w, v_gla_gate_b, v_gla_norm_w, v_ln1_w, v_ln1_b, v_ln2_w, v_ln2_b, v_gla_gate_w]
    res = _adamw_small(ssum, g_b_ada, g_ggw, small_w, small_m, small_v)
    small_g = list(res[:8]) + [g_ggw]
    d_small, m_small, v_small = list(res[8:17]), list(res[17:26]), list(res[26:35])

    _, d_w_ada, nm_w_ada, nv_w_ada = _adamw(w_ada[0], g_w_ada[None], m_w_ada[0], v_w_ada[0], "adamw_ada")

    big = [_adamw(w[0], r, m_[0], v_[0], nm) for w, r, m_, v_, nm in (
        (w_out, r_wo, m_w_out, v_w_out, "adamw_out"),
        (w_ff1, r_w1, m_w_ff1, v_w_ff1, "adamw_ff1"), (w_ff2, r_w2, m_w_ff2, v_w_ff2, "adamw_ff2"))]
    big_in = _adamw(w_in[0].T, r_wi, m_w_in[0].T, v_w_in[0].T, "adamw_in")
    big = [tuple(b.T for b in big_in)] + big
    g_big, d_big, m_big, v_big = [[b[i][None] for b in big] for i in range(4)]

    def ordered(w_ada_v, small_vals, big_vals):
        b_ada_v, rnw_v, ggb_v, gnw_v, l1w_v, l1b_v, l2w_v, l2b_v, ggw_v = small_vals
        wi_v, wo_v, w1_v, w2_v = big_vals
        return [w_ada_v, b_ada_v, wi_v, rnw_v, ggw_v, ggb_v, gnw_v, wo_v, l1w_v, l1b_v, w1_v, w2_v, l2w_v, l2b_v]

    grads = ordered(g_w_ada[None], small_g, g_big)
    deltas = ordered(d_w_ada[None], d_small, d_big)
    new_m = ordered(nm_w_ada[None], m_small, m_big)
    new_v = ordered(nv_w_ada[None], v_small, v_big)
    return (loss, grad_x[None], *grads, *deltas, *new_m, *new_v)
```

```python
import functools

import numpy as np
import jax
import jax.numpy as jnp
from jax import lax
from jax.experimental import pallas as pl
from jax.experimental.pallas import tpu as pltpu

F32 = jnp.float32
BF16 = jnp.bfloat16
MESH = pl.DeviceIdType.MESH
HIGHEST = lax.Precision.HIGHEST

N_DEV = 8
D_MODEL = 1024
CHUNK = 64
RET_HEADS = 4
RET_D = 128
GLA_HEADS = 4
GLA_DK = 64
GLA_DV = 128
GLA_KW = GLA_HEADS * GLA_DK
GATE_RANK = 16
GATE_TAU = 16.0
D_FF = 4096
LN_EPS = 1e-5
ALPHA = (2.0 * 1) ** 0.25
D_IN = 3600
D_IN_PAD = 3712
ADA_COLS = 6 * D_MODEL // N_DEV
IN_COLS = D_IN // N_DEV
FF_COLS = D_FF // N_DEV
OUT_ROWS = D_MODEL // N_DEV

OFF_RQ, OFF_RK, OFF_RV, OFF_RG = 0, 512, 1024, 1536
OFF_GQ, OFF_GK, OFF_GV, OFF_GG, OFF_GLR = 2048, 2304, 2560, 3072, 3584

ADAM_LR, ADAM_B1, ADAM_B2, ADAM_EPS, ADAM_WD, ADAM_STEP = 0.001, 0.9, 0.999, 1e-08, 0.01, 10

V7X_VMEM_LIMIT = 62 * 1024 * 1024

ROW_TILE = 512
PROJ_TILE = 512
MIX_TILE = 256
GLA_SUB = 128


def _log_gamma(h):
    return float(np.log(np.float32(1.0) - np.float32(2.0) ** np.float32(-5.0 - h)))


def _my_coords():
    return lax.axis_index("x"), lax.axis_index("y"), lax.axis_index("c")


def _flip(v, bit):
    return 1 - v if bit else v


def _peer(k):
    x, y, c = _my_coords()
    px, py, pc = _flip(x, (k >> 2) & 1), _flip(y, (k >> 1) & 1), _flip(c, k & 1)
    return (px, py, pc), 4 * px + 2 * py + pc


def _dot(a, b, dims=(((1,), (0,)), ((), ())), precision=None):
    return lax.dot_general(a, b, dims, precision=precision, preferred_element_type=F32)


NN = (((1,), (0,)), ((), ()))
NT = (((1,), (1,)), ((), ()))
TN = (((0,), (0,)), ((), ()))


def _split_bf16(v, parts):
    out = []
    for _ in range(parts):
        p = v.astype(BF16)
        out.append(p)
        v = v - p.astype(F32)
    return out


def _dot_split(a, b, dims, a_exact=False):
    if a_exact:
        ab = a.astype(BF16)
        return sum(_dot(ab, p, dims) for p in _split_bf16(b, 3))
    a_hi, a_lo = _split_bf16(a, 2)
    b_hi, b_lo = _split_bf16(b, 2)
    return _dot(a_hi, b_hi, dims) + _dot(a_hi, b_lo, dims) + _dot(a_lo, b_hi, dims)


def _sigmoid(x):
    return 1.0 / (1.0 + jnp.exp(-x))


def _ln_stats(x):
    mu = jnp.mean(x, axis=-1, keepdims=True)
    xc = x - mu
    var = jnp.mean(xc * xc, axis=-1, keepdims=True)
    rstd = lax.rsqrt(var + LN_EPS)
    return xc * rstd, rstd


def _ln_bwd(dyh, xh, rstd):
    return rstd * (dyh - jnp.mean(dyh, axis=-1, keepdims=True) - xh * jnp.mean(dyh * xh, axis=-1, keepdims=True))


def _adaln_mod(c_ext, w_ada_l, b_l, w_in_l):
    width = c_ext.shape[1]

    def body(c_ref, w_ref, b_ref, wi_ref, call_ref, mod_ref, wig_ref, token_ref, s1, r1, s2, r2, gs, gr, gl):
        gather = _TwoLevelGather([wi_ref], [wig_ref], gs, gr, gl)
        gather.start()
        token_ref[...] = jnp.zeros_like(token_ref)
        x, y, c = _my_coords()
        me = 4 * x + 2 * y + c
        call_ref[me] = c_ref[...]
        sends = []
        for k in range(1, N_DEV):
            peer, _ = _peer(k)
            cp = pltpu.make_async_remote_copy(c_ref, call_ref.at[me], s1.at[k - 1], r1.at[k - 1],
                                              device_id=peer, device_id_type=MESH)
            cp.start()
            sends.append(cp)
        for k in range(1, N_DEV):
            peer, pid = _peer(k)
            pltpu.make_async_remote_copy(c_ref, call_ref.at[pid], s1.at[k - 1], r1.at[k - 1],
                                         device_id=peer, device_id_type=MESH).wait_recv()
        for cp in sends:
            cp.wait_send()
        row = lax.broadcasted_iota(jnp.int32, (N_DEV, D_MODEL), 0)
        call = jnp.zeros((N_DEV, D_MODEL), F32)
        for j in range(N_DEV):
            call = jnp.where(row == j, jnp.broadcast_to(call_ref[j][:, :D_MODEL], (N_DEV, D_MODEL)), call)
        sc = call * _sigmoid(call)
        mod = _dot(sc, w_ref[...], NN, HIGHEST) + b_ref[...]
        mod_ref[me] = mod
        sends = []
        for k in range(1, N_DEV):
            peer, _ = _peer(k)
            cp = pltpu.make_async_remote_copy(mod_ref.at[me], mod_ref.at[me], s2.at[k - 1], r2.at[k - 1],
                                              device_id=peer, device_id_type=MESH)
            cp.start()
            sends.append(cp)
        for k in range(1, N_DEV):
            peer, pid = _peer(k)
            pltpu.make_async_remote_copy(mod_ref.at[pid], mod_ref.at[pid], s2.at[k - 1], r2.at[k - 1],
                                         device_id=peer, device_id_type=MESH).wait_recv()
        for cp in sends:
            cp.wait_send()
        gather.forward()
        gather.finish()

    vm = pl.BlockSpec(memory_space=pltpu.VMEM)
    hbm = pl.BlockSpec(memory_space=pl.ANY)
    return pl.pallas_call(
        body, name="adaln_mod",
        out_shape=(jax.ShapeDtypeStruct((N_DEV, 1, width), F32),
                   jax.ShapeDtypeStruct((N_DEV, N_DEV, ADA_COLS), F32),
                   jax.ShapeDtypeStruct((N_DEV, *w_in_l.shape), w_in_l.dtype),
                   jax.ShapeDtypeStruct((8, 128), F32)),
        in_specs=[vm, vm, vm, hbm], out_specs=(vm, vm, hbm, vm),
        scratch_shapes=[pltpu.SemaphoreType.DMA((N_DEV - 1,))] * 4
        + [pltpu.SemaphoreType.DMA((7,)), pltpu.SemaphoreType.DMA((7,)), pltpu.SemaphoreType.DMA((1,))],
        compiler_params=pltpu.CompilerParams(vmem_limit_bytes=V7X_VMEM_LIMIT),
    )(c_ext, w_ada_l, b_l, w_in_l)


class _TwoLevelGather:
    def __init__(self, x_refs, out_refs, send_sems, recv_sems, local_sems):
        self.x_refs, self.out_refs = x_refs, out_refs
        self.send_sems, self.recv_sems, self.local_sems = send_sems, recv_sems, local_sems
        x, y, c = _my_coords()
        self.c = c
        self.me, self.sibling = (x, y, c), (x, y, 1 - c)
        self.chips = [(1 - x, y), (x, 1 - y), (1 - x, 1 - y)]

    def _copy(self, a, k, block, to, src=None):
        px, py, pc = block
        slab = self.out_refs[a].at[4 * px + 2 * py + pc]
        return pltpu.make_async_remote_copy(
            src_ref=slab if src is None else src, dst_ref=slab,
            send_sem=self.send_sems.at[7 * a + k], recv_sem=self.recv_sems.at[7 * a + k],
            device_id=to, device_id_type=MESH)

    def _mine(self, a):
        px, py, pc = self.me
        return pltpu.make_async_copy(self.x_refs[a], self.out_refs[a].at[4 * px + 2 * py + pc], self.local_sems.at[a])

    def _first(self, a):
        cps = [self._copy(a, 0, self.me, self.sibling, src=self.x_refs[a])]
        cps += [self._copy(a, 1 + j, self.me, (*chip, self.c), src=self.x_refs[a]) for j, chip in enumerate(self.chips)]
        return cps

    def _passed(self, a):
        return [self._copy(a, 4 + j, (*chip, self.c), self.sibling) for j, chip in enumerate(self.chips)]

    def start(self):
        for a in range(len(self.x_refs)):
            self._mine(a).start()
            for cp in self._first(a):
                cp.start()

    def forward(self):
        for a in range(len(self.x_refs)):
            passed = self._passed(a)
            for j, chip in enumerate(self.chips):
                self._copy(a, 1 + j, (*chip, self.c), self.me).wait_recv()
                passed[j].start()

    def finish(self):
        for a in range(len(self.x_refs)):
            self._copy(a, 0, self.sibling, self.me).wait_recv()
            for j, chip in enumerate(self.chips):
                self._copy(a, 4 + j, (*chip, 1 - self.c), self.me).wait_recv()
            for cp in self._first(a) + self._passed(a):
                cp.wait_send()
            self._mine(a).wait()


def _exchange_copy(src_refs, land_refs, send_sems, recv_sems, a, k, gather, receiving):
    x, y, c = _my_coords()
    me = 4 * x + 2 * y + c
    peer, pid = _peer(k)
    src = src_refs[a] if gather else src_refs[a].at[pid]
    dst = land_refs[a].at[pid if receiving else me]
    return pltpu.make_async_remote_copy(src, dst, send_sems.at[7 * a + k - 1], recv_sems.at[7 * a + k - 1],
                                        device_id=peer, device_id_type=MESH)


def _exchange_start(srcs, gather, name):
    n = len(srcs)
    xi, yi, ci = _my_coords()
    me = 4 * xi + 2 * yi + ci
    lands = []
    for s in srcs:
        own = s[None] if gather else lax.dynamic_slice_in_dim(s, me, 1, axis=0)
        lands.append(lax.dynamic_update_slice_in_dim(lax.empty((N_DEV, *own.shape[1:]), s.dtype), own, me, axis=0))

    def body(*refs):
        src_refs, land_refs, send_sems, recv_sems, token = refs[:n], refs[n:2 * n], refs[2 * n], refs[2 * n + 1], refs[-1]
        for a in range(n):
            for k in range(1, N_DEV):
                _exchange_copy(src_refs, land_refs, send_sems, recv_sems, a, k, gather, receiving=False).start()
        token[...] = jnp.zeros_like(token)

    hbm = pl.BlockSpec(memory_space=pltpu.HBM)
    sem = pl.BlockSpec(memory_space=pltpu.SEMAPHORE)
    res = pl.pallas_call(
        body, name=name,
        out_shape=(pltpu.SemaphoreType.DMA((7 * n,)), pltpu.SemaphoreType.DMA((7 * n,)),
                   *[pltpu.HBM(v.shape, v.dtype) for v in srcs + lands], jax.ShapeDtypeStruct((8, 128), F32)),
        in_specs=[hbm] * (2 * n),
        out_specs=(sem, sem, *([hbm] * (2 * n)), pl.BlockSpec(memory_space=pltpu.VMEM)),
        input_output_aliases={i: 2 + i for i in range(2 * n)},
        compiler_params=pltpu.CompilerParams(has_side_effects=pltpu.SideEffectType.DATAFLOW_SIDE_EFFECTING),
    )(*[pltpu.with_memory_space_constraint(v, pltpu.HBM) for v in srcs + lands])
    return res[0], res[1], list(res[2:2 + n]), list(res[2 + n:2 + 2 * n]), res[-1]


def _exchange_wait(send_sems, recv_sems, srcs, lands, after, gather, name):
    n = len(srcs)

    def body(*refs):
        src_refs, land_refs, s_sems, r_sems = refs[:n], refs[n:2 * n], refs[2 * n], refs[2 * n + 1]
        for a in range(n):
            for k in range(1, N_DEV):
                _exchange_copy(src_refs, land_refs, s_sems, r_sems, a, k, gather, receiving=False).wait_send()
                _exchange_copy(src_refs, land_refs, s_sems, r_sems, a, k, gather, receiving=True).wait_recv()

    hbm = pl.BlockSpec(memory_space=pltpu.HBM)
    sem = pl.BlockSpec(memory_space=pltpu.SEMAPHORE)
    res = pl.pallas_call(
        body, name=name,
        out_shape=tuple(pltpu.HBM(v.shape, v.dtype) for v in srcs + lands),
        in_specs=[hbm] * (2 * n) + [sem, sem, pl.BlockSpec(memory_space=pl.ANY)],
        out_specs=tuple([hbm] * (2 * n)),
        input_output_aliases={i: i for i in range(2 * n)},
        compiler_params=pltpu.CompilerParams(has_side_effects=pltpu.SideEffectType.DATAFLOW_SIDE_EFFECTING),
    )(*srcs, *lands, send_sems, recv_sems, after)
    return list(res[n:])


def _load_resident(step_is_first, pairs, sem):
    @pl.when(step_is_first)
    def _():
        copies = [pltpu.make_async_copy(src, dst, sem.at[i]) for i, (src, dst) in enumerate(pairs)]
        for cp in copies:
            cp.start()
        for cp in copies:
            cp.wait()


def _load_w_in_t(step_is_first, w_hbm, w_vmem, sem):
    @pl.when(step_is_first)
    def _():
        w_vmem[D_IN:, :] = jnp.zeros((D_IN_PAD - D_IN, D_MODEL), BF16)
    _load_resident(step_is_first, [(w_hbm, w_vmem.at[pl.ds(0, D_IN)])], sem)


def _inproj_fwd(x2d, sc1p, sh1, w_in_t):
    t = x2d.shape[0]
    tm = min(PROJ_TILE, t)

    def body(x_ref, sc_ref, sh_ref, w_hbm, proj_ref, u_ref, w_vmem, sem):
        _load_w_in_t(pl.program_id(0) == 0, w_hbm, w_vmem, sem)
        xh, _ = _ln_stats(x_ref[...])
        ub = (xh * sc_ref[...] + sh_ref[...]).astype(BF16)
        u_ref[...] = ub
        proj_ref[...] = _dot(ub, w_vmem[...], NT)

    row = lambda i: (i, 0)
    fix = lambda i: (0, 0)
    return pl.pallas_call(
        body, name="inproj_fwd", grid=(t // tm,),
        in_specs=[pl.BlockSpec((tm, D_MODEL), row), pl.BlockSpec((1, D_MODEL), fix), pl.BlockSpec((1, D_MODEL), fix),
                  pl.BlockSpec(memory_space=pl.ANY)],
        out_specs=(pl.BlockSpec((tm, D_IN_PAD), row), pl.BlockSpec((tm, D_MODEL), row)),
        out_shape=(jax.ShapeDtypeStruct((t, D_IN_PAD), F32), jax.ShapeDtypeStruct((t, D_MODEL), BF16)),
        scratch_shapes=[pltpu.VMEM((D_IN_PAD, D_MODEL), BF16), pltpu.SemaphoreType.DMA((1,))],
        compiler_params=pltpu.CompilerParams(dimension_semantics=("arbitrary",), vmem_limit_bytes=V7X_VMEM_LIMIT),
    )(x2d, sc1p, sh1, w_in_t)


CHUNK_SHIFT = 6


def _ret_tables(t, tl):
    r = lax.broadcasted_iota(jnp.int32, (tl, tl), 0)
    c = lax.broadcasted_iota(jnp.int32, (tl, tl), 1)
    allowed = jnp.right_shift(c, CHUNK_SHIFT) <= jnp.right_shift(r, CHUNK_SHIFT)
    dist = jnp.abs(r - c).astype(F32)
    rowf = lax.broadcasted_iota(jnp.int32, (tl, RET_D), 0).astype(F32)
    lgs = [_log_gamma(h) for h in range(RET_HEADS)]
    dec = jnp.stack([jnp.where(allowed, jnp.exp(lg * dist), 0.0) for lg in lgs])
    qkd = jnp.stack([jnp.exp(lg * (rowf + 1.0)) for lg in lgs] + [jnp.exp(lg * (tl - 1.0 - rowf)) for lg in lgs])
    inv = 1.0 / (10000.0 ** jnp.linspace(0.0, 1.0, RET_D // 2, dtype=F32))
    off = jnp.arange(tl, dtype=F32)[:, None] * inv[None, :]
    start = (jnp.arange(t // tl, dtype=F32) * tl)[:, None] * inv[None, :]
    co, so = jnp.cos(off), jnp.sin(off)
    rot_in = jnp.stack([jnp.concatenate([co, co], 1), jnp.concatenate([so, so], 1),
                        jnp.concatenate([-co, co], 1), jnp.concatenate([-so, so], 1)])
    cs, ss = jnp.cos(start), jnp.sin(start)
    rot_tile = jnp.concatenate([cs, cs, ss, ss], axis=1)
    rot_tile = jnp.broadcast_to(rot_tile[:, None, :], (t // tl, 8, 2 * RET_D))
    return dec, qkd, rot_in, rot_tile


def _tile_gammas(tl):
    return [float(np.exp(np.float32(_log_gamma(h)) * np.float32(tl))) for h in range(RET_HEADS)]


def _tile_rotary(rot_in_ref, rot_tile_ref):
    ca, sa = rot_tile_ref[0, 0:1, 0:RET_D], rot_tile_ref[0, 0:1, RET_D:2 * RET_D]
    cosv = ca * rot_in_ref[0] - sa * rot_in_ref[1]
    sinv = sa * rot_in_ref[2] + ca * rot_in_ref[3]
    return cosv, sinv


def _gla_consts(tl):
    r = lax.broadcasted_iota(jnp.int32, (tl, tl), 0)
    c = lax.broadcasted_iota(jnp.int32, (tl, tl), 1)
    ltri = (c <= r).astype(F32)
    utri = (c >= r).astype(F32)
    lane = lax.broadcasted_iota(jnp.int32, (1, GLA_KW), 1)
    hmask = [((lane >= h * GLA_DK) & (lane < (h + 1) * GLA_DK)).astype(F32) for h in range(GLA_HEADS)]
    rs = lax.broadcasted_iota(jnp.int32, (GLA_HEADS * tl, tl), 0) & (tl - 1)
    cs = lax.broadcasted_iota(jnp.int32, (GLA_HEADS * tl, tl), 1)
    lower = cs <= rs
    same = jnp.right_shift(cs, CHUNK_SHIFT) == jnp.right_shift(rs, CHUNK_SHIFT)
    upper = jnp.logical_and(jnp.logical_not(lower), same)
    return dict(ltri=ltri, utri=utri, hmask=hmask, lower=lower, upper=upper)


def _tile_rows(j, tl):
    return pl.ds(j * tl, tl) if isinstance(j, int) else pl.ds(pl.multiple_of(j * tl, tl), tl)


def _for_tiles(cps, fn):
    if cps == 1:
        fn(0, 0)
    else:
        lax.fori_loop(0, cps, fn, 0)


def _rotate(v, cosv, sinv):
    return v * cosv + pltpu.roll(v, RET_D // 2, 1) * sinv


def _rotate_t(d, cosv, sinv):
    return d * cosv + pltpu.roll(d * sinv, RET_D // 2, 1)


def _stack_heads(v, hmask):
    return jnp.concatenate([v * hmask[h] for h in range(GLA_HEADS)], axis=0)


def _gla_gates(glr, gw, gb, ltri, tl):
    z = _dot_split(glr, gw, NN) + gb
    la = (jnp.minimum(z, 0.0) - jnp.log(1.0 + jnp.exp(-jnp.abs(z)))) * (1.0 / GATE_TAU)
    b = _dot_split(ltri, la, NN, a_exact=True)
    level = b[tl // 2 - 1:tl // 2, :]
    ep = jnp.exp(jnp.clip(b - level, -80.0, 80.0))
    em = jnp.exp(jnp.clip(level - b, -80.0, 80.0))
    bl = b[tl - 1:tl, :]
    return z, b, bl, ep, em


def _mixer_fwd(proj, tables, gw_pad, gb, rnw, gnw):
    t = proj.shape[0]
    tc = min(MIX_TILE, t)
    tr, tg = tc, min(GLA_SUB, tc)
    nsteps = t // tc
    scale_r = RET_D ** -0.5
    scale_g = GLA_DK ** -0.5
    gammas = _tile_gammas(tr)

    def body(rq_ref, rk_ref, rv_ref, rg_ref, gq_ref, gk_ref, gv_ref, gg_ref, glr_ref,
             dec_ref, qkd_ref, rot_in_ref, rot_tile_ref, gw_ref, gb_ref, rnw_ref, gnw_ref,
             mix_ref, oraw_ref, qrb_ref, krb_ref, rst_ref, sst_ref, r_scr, s_scr):
        @pl.when(pl.program_id(0) == 0)
        def _():
            r_scr[...] = jnp.zeros_like(r_scr)
            s_scr[...] = jnp.zeros_like(s_scr)

        gla_k = _gla_consts(tg)

        def ret_tile(j, carry):
            rows = _tile_rows(j, tr)
            cosv, sinv = _tile_rotary(rot_in_ref, rot_tile_ref)
            for h in range(RET_HEADS):
                cols = slice(h * RET_D, (h + 1) * RET_D)
                qr = _rotate(rq_ref[rows, cols], cosv, sinv) * scale_r
                kr = _rotate(rk_ref[rows, cols], cosv, sinv)
                vb = rv_ref[rows, cols].astype(BF16)
                qb, kb = qr.astype(BF16), kr.astype(BF16)
                qrb_ref[rows, cols] = qb
                krb_ref[rows, cols] = kb
                p = _dot(qb, kb, NT) * dec_ref[h]
                rp = r_scr[cols, :]
                o = _dot(p.astype(BF16), vb) + _dot((qr * qkd_ref[h]).astype(BF16), rp.astype(BF16))
                rst_ref[j, cols, :] = rp
                r_scr[cols, :] = gammas[h] * rp + _dot((kr * qkd_ref[RET_HEADS + h]).astype(BF16), vb, TN)
                oraw_ref[rows, cols] = o
                oc = o - jnp.mean(o, axis=-1, keepdims=True)
                n = oc * lax.rsqrt(jnp.mean(oc * oc, axis=-1, keepdims=True) + LN_EPS)
                g = rg_ref[rows, cols]
                mix_ref[rows, cols] = (n * rnw_ref[:, cols] * (g * _sigmoid(g))).astype(BF16)
            return carry

        def gla_tile(j, carry):
            k = gla_k
            tl = tg
            rows = _tile_rows(j, tg)
            _, b, bl, ep, em = _gla_gates(glr_ref[rows, :], gw_ref[...], gb_ref[...], k["ltri"], tl)
            qs = gq_ref[rows, :] * scale_g
            kk = gk_ref[rows, :]
            x_all = _dot(_stack_heads(qs * ep, k["hmask"]).astype(BF16), (kk * em).astype(BF16), NT)
            y_all = _dot(_stack_heads(qs * em, k["hmask"]).astype(BF16), (kk * ep).astype(BF16), NT)
            a_all = jnp.where(k["lower"], x_all, jnp.where(k["upper"], y_all, 0.0)).astype(BF16)
            st = s_scr[...]
            oq = _dot(_stack_heads(qs * jnp.exp(b), k["hmask"]).astype(BF16), st.astype(BF16), NT)
            kg = kk * jnp.exp(bl - b)
            sst_ref[j] = st
            st_new = st * jnp.exp(bl)
            for h in range(GLA_HEADS):
                cols = slice(h * GLA_DV, (h + 1) * GLA_DV)
                hr = slice(h * tl, (h + 1) * tl)
                vb = gv_ref[rows, cols].astype(BF16)
                o = _dot(a_all[hr, :], vb) + oq[hr, :]
                st_new = st_new + _dot(vb, (kg * k["hmask"][h]).astype(BF16), TN)
                ocols = slice(RET_HEADS * RET_D + h * GLA_DV, RET_HEADS * RET_D + (h + 1) * GLA_DV)
                oraw_ref[rows, ocols] = o
                n = o * lax.rsqrt(jnp.mean(o * o, axis=-1, keepdims=True) + LN_EPS)
                g = gg_ref[rows, cols]
                mix_ref[rows, ocols] = (n * gnw_ref[:, cols] * (g * _sigmoid(g))).astype(BF16)
            s_scr[...] = st_new
            return carry

        _for_tiles(tc // tr, ret_tile)
        _for_tiles(tc // tg, gla_tile)

    def col(width, off):
        return pl.BlockSpec((tc, width), lambda i, o=off // width: (i, o))

    fix = lambda i: (0, 0)
    fix3 = lambda i: (0, 0, 0)
    dec, qkd, rot_in, rot_tile = tables
    in_specs = [col(512, OFF_RQ), col(512, OFF_RK), col(512, OFF_RV), col(512, OFF_RG),
                col(256, OFF_GQ), col(256, OFF_GK), col(512, OFF_GV), col(512, OFF_GG), col(128, OFF_GLR),
                pl.BlockSpec(dec.shape, fix3), pl.BlockSpec(qkd.shape, fix3), pl.BlockSpec(rot_in.shape, fix3),
                pl.BlockSpec((1, 8, 2 * RET_D), lambda i: (i, 0, 0)),
                pl.BlockSpec((128, GLA_KW), fix), pl.BlockSpec((1, GLA_KW), fix),
                pl.BlockSpec((1, 512), fix), pl.BlockSpec((1, 512), fix)]
    half = pl.BlockSpec((tc, RET_HEADS * RET_D), lambda i: (i, 0))
    out_specs = (pl.BlockSpec((tc, D_MODEL), lambda i: (i, 0)), pl.BlockSpec((tc, D_MODEL), lambda i: (i, 0)),
                 half, half,
                 pl.BlockSpec((tc // tr, RET_HEADS * RET_D, RET_D), lambda i: (i, 0, 0)),
                 pl.BlockSpec((tc // tg, GLA_DV, GLA_KW), lambda i: (i, 0, 0)))
    out_shape = (jax.ShapeDtypeStruct((t, D_MODEL), BF16), jax.ShapeDtypeStruct((t, D_MODEL), F32),
                 jax.ShapeDtypeStruct((t, RET_HEADS * RET_D), BF16), jax.ShapeDtypeStruct((t, RET_HEADS * RET_D), BF16),
                 jax.ShapeDtypeStruct((t // tr, RET_HEADS * RET_D, RET_D), F32),
                 jax.ShapeDtypeStruct((t // tg, GLA_DV, GLA_KW), F32))
    return pl.pallas_call(
        body, name="mixer_fwd", grid=(nsteps,), in_specs=in_specs, out_specs=out_specs, out_shape=out_shape,
        scratch_shapes=[pltpu.VMEM((RET_HEADS * RET_D, RET_D), F32), pltpu.VMEM((GLA_DV, GLA_KW), F32)],
        compiler_params=pltpu.CompilerParams(dimension_semantics=("arbitrary",), vmem_limit_bytes=V7X_VMEM_LIMIT),
    )(*([proj] * 9), dec, qkd, rot_in, rot_tile, gw_pad, gb, rnw, gnw)


def _mid_fwd(mixed, x2d, target, vecs, w_out_b, w1_b, w2_b):
    t = x2d.shape[0]
    tm = min(ROW_TILE, t)

    def body(mix_ref, x_ref, tgt_ref, v_ref, wo_hbm, w1_hbm, w2_hbm,
             m_ref, x1n_ref, rstd_ref, u2_ref, a_ref, df_ref, dh2_ref, acc_ref, wo, w1, w2, sem):
        first = pl.program_id(0) == 0
        _load_resident(first, [(wo_hbm, wo), (w1_hbm, w1), (w2_hbm, w2)], sem)

        @pl.when(first)
        def _():
            acc_ref[...] = jnp.zeros_like(acc_ref)

        gate1, sc2p, sh2, gate2 = v_ref[0:1, :], v_ref[1:2, :], v_ref[2:3, :], v_ref[3:4, :]
        l1w, l1b, l2w, l2b = v_ref[4:5, :], v_ref[5:6, :], v_ref[6:7, :], v_ref[7:8, :]
        m = _dot(mix_ref[...], wo[...])
        m_ref[...] = m.astype(BF16)
        x1n, rstd1 = _ln_stats(ALPHA * x_ref[...] + gate1 * m)
        x1n_ref[...] = x1n
        rstd_ref[...] = rstd1
        x1 = x1n * l1w + l1b
        xh1, _ = _ln_stats(x1)
        u2 = (xh1 * sc2p + sh2).astype(BF16)
        u2_ref[...] = u2
        f = jnp.zeros((tm, D_MODEL), F32)
        for j in range(N_DEV):
            cols = slice(j * FF_COLS, (j + 1) * FF_COLS)
            a = _dot(u2, w1[j])
            a_ref[:, cols] = a.astype(BF16)
            r = jnp.maximum(a, 0.0)
            f = f + _dot((r * r).astype(BF16), w2[cols, :])
        yh, rstd2 = _ln_stats(ALPHA * x1 + gate2 * f)
        e = yh * l2w + l2b - tgt_ref[...]
        dy = e * (1.0 / D_MODEL)
        dh2 = _ln_bwd(dy * l2w, yh, rstd2)
        dh2_ref[...] = dh2
        df_ref[...] = (dh2 * gate2).astype(BF16)
        acc_ref[0:1, :] += jnp.sum(dy * yh, axis=0, keepdims=True)
        acc_ref[1:2, :] += jnp.sum(dy, axis=0, keepdims=True)
        acc_ref[2:3, :] += jnp.sum(dh2 * f, axis=0, keepdims=True)
        acc_ref[3:4, :] += jnp.sum(e * e, axis=0, keepdims=True) * (0.5 / D_MODEL)

    row = lambda i: (i, 0)
    fix = lambda i: (0, 0)
    hbm = pl.BlockSpec(memory_space=pl.ANY)
    return pl.pallas_call(
        body, name="mid_fwd", grid=(t // tm,),
        in_specs=[pl.BlockSpec((tm, D_MODEL), row), pl.BlockSpec((tm, D_MODEL), row), pl.BlockSpec((tm, D_MODEL), row),
                  pl.BlockSpec((8, D_MODEL), fix), hbm, hbm, hbm],
        out_specs=(pl.BlockSpec((tm, D_MODEL), row), pl.BlockSpec((tm, D_MODEL), row), pl.BlockSpec((tm, 1), row),
                   pl.BlockSpec((tm, D_MODEL), row), pl.BlockSpec((tm, D_FF), row), pl.BlockSpec((tm, D_MODEL), row),
                   pl.BlockSpec((tm, D_MODEL), row), pl.BlockSpec((8, D_MODEL), fix)),
        out_shape=(jax.ShapeDtypeStruct((t, D_MODEL), BF16), jax.ShapeDtypeStruct((t, D_MODEL), F32),
                   jax.ShapeDtypeStruct((t, 1), F32), jax.ShapeDtypeStruct((t, D_MODEL), BF16),
                   jax.ShapeDtypeStruct((t, D_FF), BF16), jax.ShapeDtypeStruct((t, D_MODEL), BF16),
                   jax.ShapeDtypeStruct((t, D_MODEL), F32), jax.ShapeDtypeStruct((8, D_MODEL), F32)),
        scratch_shapes=[pltpu.VMEM((D_MODEL, D_MODEL), BF16), pltpu.VMEM((N_DEV, D_MODEL, FF_COLS), BF16),
                        pltpu.VMEM((D_FF, D_MODEL), BF16), pltpu.SemaphoreType.DMA((3,))],
        compiler_params=pltpu.CompilerParams(dimension_semantics=("arbitrary",), vmem_limit_bytes=V7X_VMEM_LIMIT),
    )(mixed, x2d, target, vecs, w_out_b, w1_b, w2_b)


def _ffn_bwd(df, a, dh2, x1n, rstd1, m, vecs, w_out_b, w1_b, w2_b):
    t = x1n.shape[0]
    tm = min(ROW_TILE, t)

    def body(df_ref, a_ref, dh2_ref, x1n_ref, rstd_ref, m_ref, v_ref, wo_hbm, w1_hbm, w2_hbm,
             da_ref, dm_ref, dmix_ref, dxa_ref, acc_ref, wo, w1, w2, sem):
        first = pl.program_id(0) == 0
        _load_resident(first, [(wo_hbm, wo), (w1_hbm, w1), (w2_hbm, w2)], sem)

        @pl.when(first)
        def _():
            acc_ref[...] = jnp.zeros_like(acc_ref)

        gate1, sc2p, l1w, l1b = v_ref[0:1, :], v_ref[1:2, :], v_ref[2:3, :], v_ref[3:4, :]
        df = df_ref[...]
        du2 = jnp.zeros((tm, D_MODEL), F32)
        for j in range(N_DEV):
            cols = slice(j * FF_COLS, (j + 1) * FF_COLS)
            dr2 = _dot(df, w2[cols, :], NT)
            da = (dr2 * (2.0 * jnp.maximum(a_ref[:, cols].astype(F32), 0.0))).astype(BF16)
            da_ref[:, cols] = da
            du2 = du2 + _dot(da, w1[j], NT)
        x1n = x1n_ref[...]
        xh1, rstd0 = _ln_stats(x1n * l1w + l1b)
        dx1 = ALPHA * dh2_ref[...] + _ln_bwd(du2 * sc2p, xh1, rstd0)
        dh1 = _ln_bwd(dx1 * l1w, x1n, rstd_ref[...])
        dxa_ref[...] = ALPHA * dh1
        dm = (dh1 * gate1).astype(BF16)
        dm_ref[...] = dm
        dmix_ref[...] = _dot(dm, wo[...], NT)
        acc_ref[0:1, :] += jnp.sum(du2 * xh1, axis=0, keepdims=True)
        acc_ref[1:2, :] += jnp.sum(du2, axis=0, keepdims=True)
        acc_ref[2:3, :] += jnp.sum(dx1 * x1n, axis=0, keepdims=True)
        acc_ref[3:4, :] += jnp.sum(dx1, axis=0, keepdims=True)
        acc_ref[4:5, :] += jnp.sum(dh1 * m_ref[...].astype(F32), axis=0, keepdims=True)

    row = lambda i: (i, 0)
    fix = lambda i: (0, 0)
    hbm = pl.BlockSpec(memory_space=pl.ANY)
    return pl.pallas_call(
        body, name="ffn_bwd", grid=(t // tm,),
        in_specs=[pl.BlockSpec((tm, D_MODEL), row), pl.BlockSpec((tm, D_FF), row), pl.BlockSpec((tm, D_MODEL), row),
                  pl.BlockSpec((tm, D_MODEL), row), pl.BlockSpec((tm, 1), row), pl.BlockSpec((tm, D_MODEL), row),
                  pl.BlockSpec((8, D_MODEL), fix), hbm, hbm, hbm],
        out_specs=(pl.BlockSpec((tm, D_FF), row), pl.BlockSpec((tm, D_MODEL), row), pl.BlockSpec((tm, D_MODEL), row),
                   pl.BlockSpec((tm, D_MODEL), row), pl.BlockSpec((8, D_MODEL), fix)),
        out_shape=(jax.ShapeDtypeStruct((t, D_FF), BF16), jax.ShapeDtypeStruct((t, D_MODEL), BF16),
                   jax.ShapeDtypeStruct((t, D_MODEL), F32), jax.ShapeDtypeStruct((t, D_MODEL), F32),
                   jax.ShapeDtypeStruct((8, D_MODEL), F32)),
        scratch_shapes=[pltpu.VMEM((D_MODEL, D_MODEL), BF16), pltpu.VMEM((N_DEV, D_MODEL, FF_COLS), BF16),
                        pltpu.VMEM((D_FF, D_MODEL), BF16), pltpu.SemaphoreType.DMA((3,))],
        compiler_params=pltpu.CompilerParams(dimension_semantics=("arbitrary",), vmem_limit_bytes=V7X_VMEM_LIMIT),
    )(df, a, dh2, x1n, rstd1, m, vecs, w_out_b, w1_b, w2_b)


def _matmul_tn(lhs, rhs, tmm, tn, tk, name, relu_sq=False, col_slab=None):
    t, mm = lhs.shape
    nn = rhs.shape[1]
    tk = min(tk, t)
    nk = t // tk

    def body(l_ref, r_ref, o_ref, acc):
        kk = pl.program_id(2)

        @pl.when(kk == 0)
        def _():
            acc[...] = jnp.zeros_like(acc)

        l = l_ref[...]
        if relu_sq:
            lf = jnp.maximum(l.astype(F32), 0.0)
            l = (lf * lf).astype(BF16)
        acc[...] += _dot(l, r_ref[...], TN)

        @pl.when(kk == nk - 1)
        def _():
            if col_slab is None:
                o_ref[...] = acc[...].astype(o_ref.dtype)
            else:
                for s in range(tn // col_slab):
                    o_ref[s] = acc[:, s * col_slab:(s + 1) * col_slab].astype(o_ref.dtype)

    if col_slab is None:
        out_spec = pl.BlockSpec((tmm, tn), lambda i, j, k: (i, j))
        out_shape = jax.ShapeDtypeStruct((mm, nn), BF16)
    else:
        out_spec = pl.BlockSpec((tn // col_slab, tmm, col_slab), lambda i, j, k: (j, i, 0))
        out_shape = jax.ShapeDtypeStruct((nn // col_slab, mm, col_slab), BF16)
    return pl.pallas_call(
        body, name=name, grid=(mm // tmm, nn // tn, nk),
        in_specs=[pl.BlockSpec((tk, tmm), lambda i, j, k: (k, i)), pl.BlockSpec((tk, tn), lambda i, j, k: (k, j))],
        out_specs=out_spec,
        out_shape=out_shape,
        scratch_shapes=[pltpu.VMEM((tmm, tn), F32)],
        compiler_params=pltpu.CompilerParams(dimension_semantics=("arbitrary", "arbitrary", "arbitrary"),
                                             vmem_limit_bytes=V7X_VMEM_LIMIT),
    )(lhs, rhs)


def _mixer_bwd(dmix, proj, qrb, krb, oraw, tables, rst, sst, gw_pad, gb, rnw, gnw):
    t = proj.shape[0]
    tc = min(MIX_TILE, t)
    tr, tg = tc, min(GLA_SUB, tc)
    nsteps = t // tc
    scale_r = RET_D ** -0.5
    scale_g = GLA_DK ** -0.5
    gammas = _tile_gammas(tr)

    def body(dmix_ref, qrb_ref, krb_ref, rv_ref, rg_ref, gq_ref, gk_ref, gv_ref, gg_ref, glr_ref, oraw_ref,
             dec_ref, qkd_ref, rot_in_ref, rot_tile_ref, rst_ref, sst_ref, gw_ref, gb_ref, rnw_ref, gnw_ref,
             dproj_ref, dgw_ref, dvec_ref, dr_scr, ds_scr):
        @pl.when(pl.program_id(0) == 0)
        def _():
            dr_scr[...] = jnp.zeros_like(dr_scr)
            ds_scr[...] = jnp.zeros_like(ds_scr)
            dgw_ref[...] = jnp.zeros_like(dgw_ref)
            dvec_ref[...] = jnp.zeros_like(dvec_ref)

        gla_k = _gla_consts(tg)
        last_row = lax.broadcasted_iota(jnp.int32, (tg, GLA_KW), 0) == tg - 1

        def ret_tile(jj, carry):
            j = tc // tr - 1 - jj
            rows = _tile_rows(j, tr)
            cosv, sinv = _tile_rotary(rot_in_ref, rot_tile_ref)
            for h in range(RET_HEADS):
                cols = slice(h * RET_D, (h + 1) * RET_D)
                o = oraw_ref[rows, cols]
                g = rg_ref[rows, cols]
                w = rnw_ref[:, cols]
                dout = dmix_ref[rows, cols]
                oc = o - jnp.mean(o, axis=-1, keepdims=True)
                inv = lax.rsqrt(jnp.mean(oc * oc, axis=-1, keepdims=True) + LN_EPS)
                n = oc * inv
                sg = _sigmoid(g)
                sil = g * sg
                dn = dout * w * sil
                dvec_ref[0:1, cols] += jnp.sum(dout * n * sil, axis=0, keepdims=True)
                dproj_ref[rows, OFF_RG + h * RET_D:OFF_RG + (h + 1) * RET_D] = (
                    dout * n * w * (sg * (1.0 + g * (1.0 - sg)))).astype(BF16)
                doc = inv * (dn - n * jnp.mean(dn * n, axis=-1, keepdims=True))
                do = doc - jnp.mean(doc, axis=-1, keepdims=True)

                qb, kb = qrb_ref[rows, cols], krb_ref[rows, cols]
                qr, kr = qb.astype(F32), kb.astype(F32)
                vb = rv_ref[rows, cols].astype(BF16)
                dob = do.astype(BF16)
                qd, kd = qkd_ref[h], qkd_ref[RET_HEADS + h]
                p = _dot(qb, kb, NT) * dec_ref[h]
                rp = rst_ref[j, cols, :].astype(BF16)
                dr = dr_scr[cols, :]
                drb = dr.astype(BF16)
                dpb = (_dot(dob, vb, NT) * dec_ref[h]).astype(BF16)
                dqr = _dot(dpb, kb) + _dot(dob, rp, NT) * qd
                dkr = _dot(dpb, qb, TN) + _dot(vb, drb, NT) * kd
                dv = _dot(p.astype(BF16), dob, TN) + _dot((kr * kd).astype(BF16), drb)
                dr_scr[cols, :] = gammas[h] * dr + _dot((qr * qd).astype(BF16), dob, TN)
                dproj_ref[rows, OFF_RQ + h * RET_D:OFF_RQ + (h + 1) * RET_D] = (
                    _rotate_t(dqr, cosv, sinv) * scale_r).astype(BF16)
                dproj_ref[rows, OFF_RK + h * RET_D:OFF_RK + (h + 1) * RET_D] = _rotate_t(dkr, cosv, sinv).astype(BF16)
                dproj_ref[rows, OFF_RV + h * RET_D:OFF_RV + (h + 1) * RET_D] = dv.astype(BF16)
            return carry

        def gla_tile(jj, carry):
            k = gla_k
            tl = tg
            j = tc // tg - 1 - jj
            rows = _tile_rows(j, tg)
            glr = glr_ref[rows, :]
            z, b, bl, ep, em = _gla_gates(glr, gw_ref[...], gb_ref[...], k["ltri"], tl)
            qs = gq_ref[rows, :] * scale_g
            kk = gk_ref[rows, :]
            eb = jnp.exp(b)
            ekb = jnp.exp(bl - b)
            ebl = jnp.exp(bl)
            ql, qu, kl, ku = qs * ep, qs * em, kk * em, kk * ep
            qg, kg = qs * eb, kk * ekb
            qlm = _stack_heads(ql, k["hmask"]).astype(BF16)
            qum = _stack_heads(qu, k["hmask"]).astype(BF16)
            klb, kub = kl.astype(BF16), ku.astype(BF16)
            a_all = jnp.where(k["lower"], _dot(qlm, klb, NT),
                              jnp.where(k["upper"], _dot(qum, kub, NT), 0.0)).astype(BF16)
            st = sst_ref[j]
            stb = st.astype(BF16)
            ds = ds_scr[...]
            dsb = ds.astype(BF16)
            ds_new = ds * ebl
            da_parts = []
            dqg = jnp.zeros((tl, GLA_KW), F32)
            dkg = jnp.zeros((tl, GLA_KW), F32)
            for h in range(GLA_HEADS):
                cols = slice(h * GLA_DV, (h + 1) * GLA_DV)
                hr = slice(h * tl, (h + 1) * tl)
                ocols = slice(RET_HEADS * RET_D + h * GLA_DV, RET_HEADS * RET_D + (h + 1) * GLA_DV)
                o = oraw_ref[rows, ocols]
                g = gg_ref[rows, cols]
                w = gnw_ref[:, cols]
                dout = dmix_ref[rows, ocols]
                inv = lax.rsqrt(jnp.mean(o * o, axis=-1, keepdims=True) + LN_EPS)
                n = o * inv
                sg = _sigmoid(g)
                sil = g * sg
                dn = dout * w * sil
                dvec_ref[1:2, cols] += jnp.sum(dout * n * sil, axis=0, keepdims=True)
                dproj_ref[rows, OFF_GG + h * GLA_DV:OFF_GG + (h + 1) * GLA_DV] = (
                    dout * n * w * (sg * (1.0 + g * (1.0 - sg)))).astype(BF16)
                dob = (inv * (dn - n * jnp.mean(dn * n, axis=-1, keepdims=True))).astype(BF16)
                vb = gv_ref[rows, cols].astype(BF16)
                mh = k["hmask"][h]
                da_parts.append(_dot(dob, vb, NT))
                dv = _dot(a_all[hr, :], dob, TN) + _dot((kg * mh).astype(BF16), dsb, NT)
                dproj_ref[rows, OFF_GV + h * GLA_DV:OFF_GV + (h + 1) * GLA_DV] = dv.astype(BF16)
                dkg = dkg + mh * _dot(vb, dsb)
                dqg = dqg + mh * _dot(dob, stb)
                ds_new = ds_new + _dot(dob, (qg * mh).astype(BF16), TN)
            da_all = jnp.concatenate(da_parts, axis=0)
            dal = jnp.where(k["lower"], da_all, 0.0).astype(BF16)
            dau = jnp.where(k["upper"], da_all, 0.0).astype(BF16)
            dqlm = _dot(dal, klb)
            dqum = _dot(dau, kub)
            dql = jnp.zeros((tl, GLA_KW), F32)
            dqu = jnp.zeros((tl, GLA_KW), F32)
            for h in range(GLA_HEADS):
                hr = slice(h * tl, (h + 1) * tl)
                dql = dql + k["hmask"][h] * dqlm[hr, :]
                dqu = dqu + k["hmask"][h] * dqum[hr, :]
            dkl = _dot(dal, qlm, TN)
            dku = _dot(dau, qum, TN)
            dbl = (jnp.sum(dkg * kg, axis=0, keepdims=True)
                   + jnp.sum(ds * st, axis=0, keepdims=True) * ebl)
            ds_scr[...] = ds_new
            dqs = dql * ep + dqu * em + dqg * eb
            dk = dkl * em + dku * ep + dkg * ekb
            db = dql * ql - dkl * kl - dqu * qu + dku * ku + dqg * qg - dkg * kg
            db = db + jnp.where(last_row, dbl, 0.0)
            dla = _dot_split(k["utri"], db, NN, a_exact=True)
            dz = dla * (1.0 / GATE_TAU) * _sigmoid(-z)
            dvec_ref[2:3, 0:GLA_KW] += jnp.sum(dz, axis=0, keepdims=True)
            dgw_ref[...] += _dot_split(glr, dz, TN)
            dproj_ref[rows, OFF_GLR:OFF_GLR + 128] = _dot(dz.astype(BF16), gw_ref[...].astype(BF16), NT).astype(BF16)
            dproj_ref[rows, OFF_GQ:OFF_GQ + GLA_KW] = (dqs * scale_g).astype(BF16)
            dproj_ref[rows, OFF_GK:OFF_GK + GLA_KW] = dk.astype(BF16)
            return carry

        _for_tiles(tc // tr, ret_tile)
        _for_tiles(tc // tg, gla_tile)

    rev = lambda i: (nsteps - 1 - i, 0)

    def col(width, off):
        return pl.BlockSpec((tc, width), lambda i, o=off // width: (nsteps - 1 - i, o))

    fix = lambda i: (0, 0)
    fix3 = lambda i: (0, 0, 0)
    dec, qkd, rot_in, rot_tile = tables
    half = pl.BlockSpec((tc, RET_HEADS * RET_D), rev)
    in_specs = [pl.BlockSpec((tc, D_MODEL), rev), half, half, col(512, OFF_RV), col(512, OFF_RG),
                col(256, OFF_GQ), col(256, OFF_GK), col(512, OFF_GV), col(512, OFF_GG), col(128, OFF_GLR),
                pl.BlockSpec((tc, D_MODEL), rev),
                pl.BlockSpec(dec.shape, fix3), pl.BlockSpec(qkd.shape, fix3), pl.BlockSpec(rot_in.shape, fix3),
                pl.BlockSpec((1, 8, 2 * RET_D), lambda i: (nsteps - 1 - i, 0, 0)),
                pl.BlockSpec((tc // tr, RET_HEADS * RET_D, RET_D), lambda i: (nsteps - 1 - i, 0, 0)),
                pl.BlockSpec((tc // tg, GLA_DV, GLA_KW), lambda i: (nsteps - 1 - i, 0, 0)),
                pl.BlockSpec((128, GLA_KW), fix), pl.BlockSpec((1, GLA_KW), fix),
                pl.BlockSpec((1, 512), fix), pl.BlockSpec((1, 512), fix)]
    out_specs = (pl.BlockSpec((tc, D_IN_PAD), rev), pl.BlockSpec((128, GLA_KW), fix), pl.BlockSpec((8, 512), fix))
    out_shape = (jax.ShapeDtypeStruct((t, D_IN_PAD), BF16), jax.ShapeDtypeStruct((128, GLA_KW), F32),
                 jax.ShapeDtypeStruct((8, 512), F32))
    return pl.pallas_call(
        body, name="mixer_bwd", grid=(nsteps,), in_specs=in_specs, out_specs=out_specs, out_shape=out_shape,
        scratch_shapes=[pltpu.VMEM((RET_HEADS * RET_D, RET_D), F32), pltpu.VMEM((GLA_DV, GLA_KW), F32)],
        compiler_params=pltpu.CompilerParams(dimension_semantics=("arbitrary",), vmem_limit_bytes=V7X_VMEM_LIMIT),
    )(dmix, qrb, krb, *([proj] * 7), oraw, dec, qkd, rot_in, rot_tile, rst, sst, gw_pad, gb, rnw, gnw)


def _inproj_bwd(dproj, x2d, dxa, sc1p, w_in_t):
    t = x2d.shape[0]
    tm = min(PROJ_TILE, t)

    def body(dp_ref, x_ref, dxa_ref, sc_ref, w_hbm, gx_ref, acc_ref, w_vmem, sem):
        first = pl.program_id(0) == 0
        _load_w_in_t(first, w_hbm, w_vmem, sem)

        @pl.when(first)
        def _():
            acc_ref[...] = jnp.zeros_like(acc_ref)

        du = _dot(dp_ref[...], w_vmem[...])
        xh, rstd = _ln_stats(x_ref[...])
        gx_ref[...] = dxa_ref[...] + _ln_bwd(du * sc_ref[...], xh, rstd)
        acc_ref[0:1, :] += jnp.sum(du * xh, axis=0, keepdims=True)
        acc_ref[1:2, :] += jnp.sum(du, axis=0, keepdims=True)

    row = lambda i: (i, 0)
    fix = lambda i: (0, 0)
    return pl.pallas_call(
        body, name="inproj_bwd", grid=(t // tm,),
        in_specs=[pl.BlockSpec((tm, D_IN_PAD), row), pl.BlockSpec((tm, D_MODEL), row), pl.BlockSpec((tm, D_MODEL), row),
                  pl.BlockSpec((1, D_MODEL), fix), pl.BlockSpec(memory_space=pl.ANY)],
        out_specs=(pl.BlockSpec((tm, D_MODEL), row), pl.BlockSpec((8, D_MODEL), fix)),
        out_shape=(jax.ShapeDtypeStruct((t, D_MODEL), F32), jax.ShapeDtypeStruct((8, D_MODEL), F32)),
        scratch_shapes=[pltpu.VMEM((D_IN_PAD, D_MODEL), BF16), pltpu.SemaphoreType.DMA((1,))],
        compiler_params=pltpu.CompilerParams(dimension_semantics=("arbitrary",), vmem_limit_bytes=V7X_VMEM_LIMIT),
    )(dproj, x2d, dxa, sc1p, w_in_t)


def _adam_math(w, g, m, v):
    m = ADAM_B1 * m + (1.0 - ADAM_B1) * g
    v = ADAM_B2 * v + (1.0 - ADAM_B2) * (g * g)
    m_hat = m / (1.0 - ADAM_B1 ** ADAM_STEP)
    v_hat = v / (1.0 - ADAM_B2 ** ADAM_STEP)
    delta = -ADAM_LR * (m_hat / (jnp.sqrt(v_hat) + ADAM_EPS) + ADAM_WD * w)
    return delta, m, v


def _adamw(w, gparts, m, v, name):
    nparts, rows, cols = gparts.shape
    tr = rows
    for cand in (512, 256, 128, 64, 32, 16, 8):
        if rows % cand == 0:
            tr = cand
            break

    def body(w_ref, g_ref, m_ref, v_ref, go_ref, d_ref, mo_ref, vo_ref):
        g = g_ref[0].astype(F32)
        for p in range(1, nparts):
            g = g + g_ref[p].astype(F32)
        delta, mn, vn = _adam_math(w_ref[...], g, m_ref[...], v_ref[...])
        go_ref[...] = g
        d_ref[...] = delta
        mo_ref[...] = mn
        vo_ref[...] = vn

    blk = pl.BlockSpec((tr, cols), lambda i: (i, 0))
    shp = jax.ShapeDtypeStruct((rows, cols), F32)
    return pl.pallas_call(
        body, name=name, grid=(rows // tr,),
        in_specs=[blk, pl.BlockSpec((nparts, tr, cols), lambda i: (0, i, 0)), blk, blk],
        out_specs=(blk, blk, blk, blk), out_shape=(shp, shp, shp, shp),
        compiler_params=pltpu.CompilerParams(dimension_semantics=("arbitrary",), vmem_limit_bytes=V7X_VMEM_LIMIT),
    )(w, gparts, m, v)


def _small_reduce(gathered, gathered_gw, c_all, dmod_cols):
    def body(g_ref, gw_ref, c_ref, dm_ref, sum_ref, gwsum_ref, gb_ref, gwa_ref):
        s = g_ref[0]
        sw = gw_ref[0]
        for p in range(1, N_DEV):
            s = s + g_ref[p]
            sw = sw + gw_ref[p]
        sum_ref[...] = s
        gwsum_ref[...] = sw
        for i in range(6):
            gb_ref[:, i * D_MODEL:(i + 1) * D_MODEL] = s[i:i + 1, :]
        cc = c_ref[...]
        gwa_ref[...] = _dot(cc * _sigmoid(cc), dm_ref[...], TN, HIGHEST)

    vm = pl.BlockSpec(memory_space=pltpu.VMEM)
    return pl.pallas_call(
        body, name="small_reduce",
        out_shape=(jax.ShapeDtypeStruct(gathered.shape[1:], F32), jax.ShapeDtypeStruct(gathered_gw.shape[1:], F32),
                   jax.ShapeDtypeStruct((1, 6 * D_MODEL), F32), jax.ShapeDtypeStruct((D_MODEL, ADA_COLS), F32)),
        in_specs=[vm] * 4, out_specs=(vm, vm, vm, vm),
        compiler_params=pltpu.CompilerParams(vmem_limit_bytes=V7X_VMEM_LIMIT),
    )(gathered, gathered_gw, c_all, dmod_cols)


SMR_LN1W, SMR_LN1B, SMR_LN2W, SMR_LN2B, SMR_NORMS, SMR_MISC = 6, 7, 8, 9, 10, 11


def _adamw_small(gsum, g_b_ada, g_ggw, params, moms, vels):
    n = len(params)

    def body(*refs):
        gsum_ref, gb_ref, gw_ref = refs[:3]
        w_refs, m_refs, v_refs = refs[3:3 + n], refs[3 + n:3 + 2 * n], refs[3 + 2 * n:3 + 3 * n]
        outs = refs[3 + 3 * n:]
        g_refs, d_refs, mo_refs, vo_refs = outs[:n - 1], outs[n - 1:2 * n - 1], outs[2 * n - 1:3 * n - 1], outs[3 * n - 1:]
        grads = [gb_ref[...],
                 gsum_ref[SMR_NORMS:SMR_NORMS + 1, 0:512],
                 gsum_ref[SMR_MISC:SMR_MISC + 1, 0:GLA_KW],
                 gsum_ref[SMR_NORMS:SMR_NORMS + 1, 512:1024],
                 gsum_ref[SMR_LN1W:SMR_LN1W + 1, :], gsum_ref[SMR_LN1B:SMR_LN1B + 1, :],
                 gsum_ref[SMR_LN2W:SMR_LN2W + 1, :], gsum_ref[SMR_LN2B:SMR_LN2B + 1, :],
                 gw_ref[...]]
        for i in range(n):
            delta, mn, vn = _adam_math(w_refs[i][...], grads[i], m_refs[i][...], v_refs[i][...])
            if i < n - 1:
                g_refs[i][...] = grads[i]
            d_refs[i][...] = delta
            mo_refs[i][...] = mn
            vo_refs[i][...] = vn

    vm = pl.BlockSpec(memory_space=pltpu.VMEM)
    shapes = [jax.ShapeDtypeStruct(p.shape, F32) for p in params]
    n_in = 3 + 3 * n
    out_shape = tuple(shapes[:n - 1] + shapes * 3)
    return pl.pallas_call(
        body, name="adamw_small", out_shape=out_shape,
        in_specs=[vm] * n_in, out_specs=tuple([vm] * len(out_shape)),
        compiler_params=pltpu.CompilerParams(vmem_limit_bytes=V7X_VMEM_LIMIT),
    )(gsum, g_b_ada, g_ggw, *params, *moms, *vels)


def kernel(x, c, w_ada, b_ada, w_in, ret_norm_w, gla_gate_w, gla_gate_b, gla_norm_w, w_out, ln1_w, ln1_b, w_ff1, w_ff2, ln2_w, ln2_b, loss_target, m_w_ada, m_b_ada, m_w_in, m_ret_norm_w, m_gla_gate_w, m_gla_gate_b, m_gla_norm_w, m_w_out, m_ln1_w, m_ln1_b, m_w_ff1, m_w_ff2, m_ln2_w, m_ln2_b, v_w_ada, v_b_ada, v_w_in, v_ret_norm_w, v_gla_gate_w, v_gla_gate_b, v_gla_norm_w, v_w_out, v_ln1_w, v_ln1_b, v_w_ff1, v_w_ff2, v_ln2_w, v_ln2_b):
    t = x.shape[1]
    xi, yi, ci = _my_coords()
    me = 4 * xi + 2 * yi + ci
    x2d = x[0]
    tgt = loss_target[0]

    c_ext = jnp.concatenate([c, gla_gate_w[0].reshape(1, GATE_RANK * GLA_KW // N_DEV)], axis=1)
    b_l = lax.dynamic_slice(b_ada, (0, me * ADA_COLS), (1, ADA_COLS))
    c_all3, mod_all, wi_g, ada_token = _adaln_mod(c_ext, w_ada[0], b_l, w_in[0].T.astype(BF16))

    wg = _exchange_start([(w_out[0] + ada_token[0, 0]).astype(BF16), w_ff1[0].astype(BF16), w_ff2[0].astype(BF16)],
                         True, "wgather_start")

    c_all = c_all3[:, 0, :D_MODEL]
    gate_w = c_all3[:, 0, D_MODEL:].reshape(N_DEV, GATE_RANK, GLA_KW // N_DEV)
    gate_w = gate_w.transpose(1, 0, 2).reshape(GATE_RANK, GLA_KW)
    gw_pad = jnp.zeros((128, GLA_KW), F32).at[:GATE_RANK].set(gate_w)
    mod = lax.dynamic_slice(mod_all, (0, me, 0), (N_DEV, 1, ADA_COLS)).reshape(6, D_MODEL)
    shift1, scale1, gate1, shift2, scale2, gate2 = [mod[i:i + 1] for i in range(6)]

    w_in_t = wi_g.reshape(D_IN, D_MODEL)

    tables = _ret_tables(t, min(MIX_TILE, t))

    sc1p = 1.0 + scale1
    proj, u = _inproj_fwd(x2d, sc1p, shift1 + wg[4][0, 0], w_in_t)
    mixed, oraw, qrb, krb, rst, sst = _mixer_fwd(proj, tables, gw_pad, gla_gate_b, ret_norm_w, gla_norm_w)
    wo_g, w1_b, w2_g = _exchange_wait(*wg[:4], mixed, True, "wgather_wait")
    w_out_b = wo_g.reshape(D_MODEL, D_MODEL)
    w2_b = w2_g.reshape(D_FF, D_MODEL)
    vec_f = jnp.concatenate([gate1, 1.0 + scale2, shift2, gate2, ln1_w, ln1_b, ln2_w, ln2_b], axis=0)
    m, x1n, rstd1, u2, a, df, dh2, acc_f = _mid_fwd(mixed, x2d, tgt, vec_f, w_out_b, w1_b, w2_b)

    vec_b = jnp.concatenate([gate1, 1.0 + scale2, ln1_w, ln1_b, jnp.zeros((4, D_MODEL), F32)], axis=0)
    da, dm, dmix, dxa, acc_b = _ffn_bwd(df, a, dh2, x1n, rstd1, m, vec_b, w_out_b, w1_b, w2_b)
    dw2 = _matmul_tn(a, df, 2048, 1024, 2048, "tn_dw2", relu_sq=True)
    dw1 = _matmul_tn(u2, da, 1024, 2048, 2048, "tn_dw1", col_slab=FF_COLS)
    dwo = _matmul_tn(mixed, dm, 1024, 1024, 2048, "tn_dwout")
    gx = _exchange_start([dwo.reshape(N_DEV, OUT_ROWS, D_MODEL), dw1, dw2.reshape(N_DEV, FF_COLS, D_MODEL)], False,
                         "gradx_start")
    dproj, dgw, dvec = _mixer_bwd(dmix, proj, qrb, krb, oraw, tables, rst, sst, gw_pad,
                                  gla_gate_b + gx[4][0, 0], ret_norm_w, gla_norm_w)
    dwi_t = _matmul_tn(dproj, u, D_IN_PAD, 1024, 1024, "tn_dwin")
    dwi_s = dwi_t[:D_IN].reshape(N_DEV, IN_COLS, D_MODEL)
    gi = _exchange_start([dwi_s], False, "gradin_start")
    grad_x, acc_i = _inproj_bwd(dproj, x2d, dxa, sc1p + gi[4][0, 0], w_in_t)

    loss_part = jnp.sum(acc_f[3])
    small = jnp.concatenate([
        acc_i[1:2], acc_i[0:1], acc_b[4:5], acc_b[1:2], acc_b[0:1], acc_f[2:3],
        acc_b[2:3], acc_b[3:4], acc_f[0:1], acc_f[1:2],
        jnp.concatenate([dvec[0:1], dvec[1:2]], axis=1),
        jnp.concatenate([dvec[2:3, :GLA_KW], jnp.full((1, 128), loss_part, F32),
                         jnp.zeros((1, D_MODEL - GLA_KW - 128), F32)], axis=1),
        jnp.zeros((4, D_MODEL), F32)], axis=0)
    sg = _exchange_start([small, dgw[:GATE_RANK]], True, "small_start")

    r_wo, r_w1, r_w2 = _exchange_wait(*gx[:4], sg[4], False, "gradx_wait")
    r_wi, = _exchange_wait(*gi[:4], sg[4], False, "gradin_wait")
    big = [_adamw(w[0], r, m_[0], v_[0], nm) for w, r, m_, v_, nm in (
        (w_out, r_wo, m_w_out, v_w_out, "adamw_out"),
        (w_ff1, r_w1, m_w_ff1, v_w_ff1, "adamw_ff1"), (w_ff2, r_w2, m_w_ff2, v_w_ff2, "adamw_ff2"))]
    big_in = _adamw(w_in[0].T, r_wi, m_w_in[0].T, v_w_in[0].T, "adamw_in")
    big = [tuple(b.T for b in big_in)] + big
    g_big, d_big, m_big, v_big = [[b[i][None] for b in big] for i in range(4)]

    small_all, gw_all = _exchange_wait(*sg[:4], big_in[1], True, "small_wait")
    dmod_all = small_all[:, :6].reshape(N_DEV, 6 * D_MODEL)
    dmod_cols = lax.dynamic_slice(dmod_all, (0, me * ADA_COLS), (N_DEV, ADA_COLS))
    ssum, gw_sum, g_b_ada, g_w_ada = _small_reduce(small_all, gw_all, c_all, dmod_cols)
    loss = ssum[SMR_MISC, GLA_KW]
    g_ggw = lax.dynamic_slice(gw_sum, (0, me * (GLA_KW // N_DEV)), (GATE_RANK, GLA_KW // N_DEV))[None]

    small_w = [b_ada, ret_norm_w, gla_gate_b, gla_norm_w, ln1_w, ln1_b, ln2_w, ln2_b, gla_gate_w]
    small_m = [m_b_ada, m_ret_norm_w, m_gla_gate_b, m_gla_norm_w, m_ln1_w, m_ln1_b, m_ln2_w, m_ln2_b, m_gla_gate_w]
    small_v = [v_b_ada, v_ret_norm_w, v_gla_gate_b, v_gla_norm_w, v_ln1_w, v_ln1_b, v_ln2_w, v_ln2_b, v_gla_gate_w]
    res = _adamw_small(ssum, g_b_ada, g_ggw, small_w, small_m, small_v)
    small_g = list(res[:8]) + [g_ggw]
    d_small, m_small, v_small = list(res[8:17]), list(res[17:26]), list(res[26:35])

    _, d_w_ada, nm_w_ada, nv_w_ada = _adamw(w_ada[0], g_w_ada[None], m_w_ada[0], v_w_ada[0], "adamw_ada")

    def ordered(w_ada_v, small_vals, big_vals):
        b_ada_v, rnw_v, ggb_v, gnw_v, l1w_v, l1b_v, l2w_v, l2b_v, ggw_v = small_vals
        wi_v, wo_v, w1_v, w2_v = big_vals
        return [w_ada_v, b_ada_v, wi_v, rnw_v, ggw_v, ggb_v, gnw_v, wo_v, l1w_v, l1b_v, w1_v, w2_v, l2w_v, l2b_v]

    grads = ordered(g_w_ada[None], small_g, g_big)
    deltas = ordered(d_w_ada[None], d_small, d_big)
    new_m = ordered(nm_w_ada[None], m_small, m_big)
    new_v = ordered(nv_w_ada[None], v_small, v_big)
    return (loss, grad_x[None], *grads, *deltas, *new_m, *new_v)
```

```python
import functools

import numpy as np
import jax
import jax.numpy as jnp
from jax import lax
from jax.experimental import pallas as pl
from jax.experimental.pallas import tpu as pltpu

F32 = jnp.float32
BF16 = jnp.bfloat16
MESH = pl.DeviceIdType.MESH
HIGHEST = lax.Precision.HIGHEST

N_DEV = 8
D_MODEL = 1024
CHUNK = 64
RET_HEADS = 4
RET_D = 128
GLA_HEADS = 4
GLA_DK = 64
GLA_DV = 128
GLA_KW = GLA_HEADS * GLA_DK
GATE_RANK = 16
GATE_TAU = 16.0
D_FF = 4096
LN_EPS = 1e-5
ALPHA = (2.0 * 1) ** 0.25
D_IN = 3600
D_IN_PAD = 3712
ADA_COLS = 6 * D_MODEL // N_DEV
IN_COLS = D_IN // N_DEV
FF_COLS = D_FF // N_DEV
OUT_ROWS = D_MODEL // N_DEV

OFF_RQ, OFF_RK, OFF_RV, OFF_RG = 0, 512, 1024, 1536
OFF_GQ, OFF_GK, OFF_GV, OFF_GG, OFF_GLR = 2048, 2304, 2560, 3072, 3584

ADAM_LR, ADAM_B1, ADAM_B2, ADAM_EPS, ADAM_WD, ADAM_STEP = 0.001, 0.9, 0.999, 1e-08, 0.01, 10

V7X_VMEM_LIMIT = 62 * 1024 * 1024

ROW_TILE = 512
PROJ_TILE = 512
MIX_TILE = 256
GLA_SUB = 128


def _log_gamma(h):
    return float(np.log(np.float32(1.0) - np.float32(2.0) ** np.float32(-5.0 - h)))


def _my_coords():
    return lax.axis_index("x"), lax.axis_index("y"), lax.axis_index("c")


def _flip(v, bit):
    return 1 - v if bit else v


def _peer(k):
    x, y, c = _my_coords()
    px, py, pc = _flip(x, (k >> 2) & 1), _flip(y, (k >> 1) & 1), _flip(c, k & 1)
    return (px, py, pc), 4 * px + 2 * py + pc


def _dot(a, b, dims=(((1,), (0,)), ((), ())), precision=None):
    return lax.dot_general(a, b, dims, precision=precision, preferred_element_type=F32)


NN = (((1,), (0,)), ((), ()))
NT = (((1,), (1,)), ((), ()))
TN = (((0,), (0,)), ((), ()))


def _split_bf16(v, parts):
    out = []
    for _ in range(parts):
        p = v.astype(BF16)
        out.append(p)
        v = v - p.astype(F32)
    return out


def _dot_split(a, b, dims, a_exact=False):
    if a_exact:
        ab = a.astype(BF16)
        return sum(_dot(ab, p, dims) for p in _split_bf16(b, 3))
    a_hi, a_lo = _split_bf16(a, 2)
    b_hi, b_lo = _split_bf16(b, 2)
    return _dot(a_hi, b_hi, dims) + _dot(a_hi, b_lo, dims) + _dot(a_lo, b_hi, dims)


def _sigmoid(x):
    return 1.0 / (1.0 + jnp.exp(-x))


def _ln_stats(x):
    mu = jnp.mean(x, axis=-1, keepdims=True)
    xc = x - mu
    var = jnp.mean(xc * xc, axis=-1, keepdims=True)
    rstd = lax.rsqrt(var + LN_EPS)
    return xc * rstd, rstd


def _ln_bwd(dyh, xh, rstd):
    return rstd * (dyh - jnp.mean(dyh, axis=-1, keepdims=True) - xh * jnp.mean(dyh * xh, axis=-1, keepdims=True))


def _adaln_mod(c_ext, w_ada_l, b_l, w_in_l):
    width = c_ext.shape[1]

    def body(c_ref, w_ref, b_ref, wi_ref, call_ref, mod_ref, wig_ref, token_ref, s1, r1, s2, r2, gs, gr, gl):
        gather = _TwoLevelGather([wi_ref], [wig_ref], gs, gr, gl)
        gather.start()
        token_ref[...] = jnp.zeros_like(token_ref)
        x, y, c = _my_coords()
        me = 4 * x + 2 * y + c
        call_ref[me] = c_ref[...]
        sends = []
        for k in range(1, N_DEV):
            peer, _ = _peer(k)
            cp = pltpu.make_async_remote_copy(c_ref, call_ref.at[me], s1.at[k - 1], r1.at[k - 1],
                                              device_id=peer, device_id_type=MESH)
            cp.start()
            sends.append(cp)
        for k in range(1, N_DEV):
            peer, pid = _peer(k)
            pltpu.make_async_remote_copy(c_ref, call_ref.at[pid], s1.at[k - 1], r1.at[k - 1],
                                         device_id=peer, device_id_type=MESH).wait_recv()
        for cp in sends:
            cp.wait_send()
        row = lax.broadcasted_iota(jnp.int32, (N_DEV, D_MODEL), 0)
        call = jnp.zeros((N_DEV, D_MODEL), F32)
        for j in range(N_DEV):
            call = jnp.where(row == j, jnp.broadcast_to(call_ref[j][:, :D_MODEL], (N_DEV, D_MODEL)), call)
        sc = call * _sigmoid(call)
        mod = _dot(sc, w_ref[...], NN, HIGHEST) + b_ref[...]
        mod_ref[me] = mod
        sends = []
        for k in range(1, N_DEV):
            peer, _ = _peer(k)
            cp = pltpu.make_async_remote_copy(mod_ref.at[me], mod_ref.at[me], s2.at[k - 1], r2.at[k - 1],
                                              device_id=peer, device_id_type=MESH)
            cp.start()
            sends.append(cp)
        for k in range(1, N_DEV):
            peer, pid = _peer(k)
            pltpu.make_async_remote_copy(mod_ref.at[pid], mod_ref.at[pid], s2.at[k - 1], r2.at[k - 1],
                                         device_id=peer, device_id_type=MESH).wait_recv()
        for cp in sends:
            cp.wait_send()
        gather.forward()
        gather.finish()

    vm = pl.BlockSpec(memory_space=pltpu.VMEM)
    hbm = pl.BlockSpec(memory_space=pl.ANY)
    return pl.pallas_call(
        body, name="adaln_mod",
        out_shape=(jax.ShapeDtypeStruct((N_DEV, 1, width), F32),
                   jax.ShapeDtypeStruct((N_DEV, N_DEV, ADA_COLS), F32),
                   jax.ShapeDtypeStruct((N_DEV, *w_in_l.shape), w_in_l.dtype),
                   jax.ShapeDtypeStruct((8, 128), F32)),
        in_specs=[vm, vm, vm, hbm], out_specs=(vm, vm, hbm, vm),
        scratch_shapes=[pltpu.SemaphoreType.DMA((N_DEV - 1,))] * 4
        + [pltpu.SemaphoreType.DMA((7,)), pltpu.SemaphoreType.DMA((7,)), pltpu.SemaphoreType.DMA((1,))],
        compiler_params=pltpu.CompilerParams(vmem_limit_bytes=V7X_VMEM_LIMIT),
    )(c_ext, w_ada_l, b_l, w_in_l)


class _TwoLevelGather:
    def __init__(self, x_refs, out_refs, send_sems, recv_sems, local_sems):
        self.x_refs, self.out_refs = x_refs, out_refs
        self.send_sems, self.recv_sems, self.local_sems = send_sems, recv_sems, local_sems
        x, y, c = _my_coords()
        self.c = c
        self.me, self.sibling = (x, y, c), (x, y, 1 - c)
        self.chips = [(1 - x, y), (x, 1 - y), (1 - x, 1 - y)]

    def _copy(self, a, k, block, to, src=None):
        px, py, pc = block
        slab = self.out_refs[a].at[4 * px + 2 * py + pc]
        return pltpu.make_async_remote_copy(
            src_ref=slab if src is None else src, dst_ref=slab,
            send_sem=self.send_sems.at[7 * a + k], recv_sem=self.recv_sems.at[7 * a + k],
            device_id=to, device_id_type=MESH)

    def _mine(self, a):
        px, py, pc = self.me
        return pltpu.make_async_copy(self.x_refs[a], self.out_refs[a].at[4 * px + 2 * py + pc], self.local_sems.at[a])

    def _first(self, a):
        cps = [self._copy(a, 0, self.me, self.sibling, src=self.x_refs[a])]
        cps += [self._copy(a, 1 + j, self.me, (*chip, self.c), src=self.x_refs[a]) for j, chip in enumerate(self.chips)]
        return cps

    def _passed(self, a):
        return [self._copy(a, 4 + j, (*chip, self.c), self.sibling) for j, chip in enumerate(self.chips)]

    def start(self):
        for a in range(len(self.x_refs)):
            self._mine(a).start()
            for cp in self._first(a):
                cp.start()

    def forward(self):
        for a in range(len(self.x_refs)):
            passed = self._passed(a)
            for j, chip in enumerate(self.chips):
                self._copy(a, 1 + j, (*chip, self.c), self.me).wait_recv()
                passed[j].start()

    def finish(self):
        for a in range(len(self.x_refs)):
            self._copy(a, 0, self.sibling, self.me).wait_recv()
            for j, chip in enumerate(self.chips):
                self._copy(a, 4 + j, (*chip, 1 - self.c), self.me).wait_recv()
            for cp in self._first(a) + self._passed(a):
                cp.wait_send()
            self._mine(a).wait()


def _exchange_copy(src_refs, land_refs, send_sems, recv_sems, a, k, gather, receiving):
    x, y, c = _my_coords()
    me = 4 * x + 2 * y + c
    peer, pid = _peer(k)
    src = src_refs[a] if gather else src_refs[a].at[pid]
    dst = land_refs[a].at[pid if receiving else me]
    return pltpu.make_async_remote_copy(src, dst, send_sems.at[7 * a + k - 1], recv_sems.at[7 * a + k - 1],
                                        device_id=peer, device_id_type=MESH)


def _exchange_start(srcs, gather, name):
    n = len(srcs)
    xi, yi, ci = _my_coords()
    me = 4 * xi + 2 * yi + ci
    lands = []
    for s in srcs:
        own = s[None] if gather else lax.dynamic_slice_in_dim(s, me, 1, axis=0)
        lands.append(lax.dynamic_update_slice_in_dim(lax.empty((N_DEV, *own.shape[1:]), s.dtype), own, me, axis=0))

    def body(*refs):
        src_refs, land_refs, send_sems, recv_sems, token = refs[:n], refs[n:2 * n], refs[2 * n], refs[2 * n + 1], refs[-1]
        for a in range(n):
            for k in range(1, N_DEV):
                _exchange_copy(src_refs, land_refs, send_sems, recv_sems, a, k, gather, receiving=False).start()
        token[...] = jnp.zeros_like(token)

    hbm = pl.BlockSpec(memory_space=pltpu.HBM)
    sem = pl.BlockSpec(memory_space=pltpu.SEMAPHORE)
    res = pl.pallas_call(
        body, name=name,
        out_shape=(pltpu.SemaphoreType.DMA((7 * n,)), pltpu.SemaphoreType.DMA((7 * n,)),
                   *[pltpu.HBM(v.shape, v.dtype) for v in srcs + lands], jax.ShapeDtypeStruct((8, 128), F32)),
        in_specs=[hbm] * (2 * n),
        out_specs=(sem, sem, *([hbm] * (2 * n)), pl.BlockSpec(memory_space=pltpu.VMEM)),
        input_output_aliases={i: 2 + i for i in range(2 * n)},
        compiler_params=pltpu.CompilerParams(has_side_effects=pltpu.SideEffectType.DATAFLOW_SIDE_EFFECTING),
    )(*[pltpu.with_memory_space_constraint(v, pltpu.HBM) for v in srcs + lands])
    return res[0], res[1], list(res[2:2 + n]), list(res[2 + n:2 + 2 * n]), res[-1]


def _exchange_wait(send_sems, recv_sems, srcs, lands, after, gather, name):
    n = len(srcs)

    def body(*refs):
        src_refs, land_refs, s_sems, r_sems = refs[:n], refs[n:2 * n], refs[2 * n], refs[2 * n + 1]
        for a in range(n):
            for k in range(1, N_DEV):
                _exchange_copy(src_refs, land_refs, s_sems, r_sems, a, k, gather, receiving=False).wait_send()
                _exchange_copy(src_refs, land_refs, s_sems, r_sems, a, k, gather, receiving=True).wait_recv()

    hbm = pl.BlockSpec(memory_space=pltpu.HBM)
    sem = pl.BlockSpec(memory_space=pltpu.SEMAPHORE)
    res = pl.pallas_call(
        body, name=name,
        out_shape=tuple(pltpu.HBM(v.shape, v.dtype) for v in srcs + lands),
        in_specs=[hbm] * (2 * n) + [sem, sem, pl.BlockSpec(memory_space=pl.ANY)],
        out_specs=tuple([hbm] * (2 * n)),
        input_output_aliases={i: i for i in range(2 * n)},
        compiler_params=pltpu.CompilerParams(has_side_effects=pltpu.SideEffectType.DATAFLOW_SIDE_EFFECTING),
    )(*srcs, *lands, send_sems, recv_sems, after)
    return list(res[n:])


def _load_resident(step_is_first, pairs, sem):
    @pl.when(step_is_first)
    def _():
        copies = [pltpu.make_async_copy(src, dst, sem.at[i]) for i, (src, dst) in enumerate(pairs)]
        for cp in copies:
            cp.start()
        for cp in copies:
            cp.wait()


def _load_w_in_t(step_is_first, w_hbm, w_vmem, sem):
    @pl.when(step_is_first)
    def _():
        w_vmem[D_IN:, :] = jnp.zeros((D_IN_PAD - D_IN, D_MODEL), BF16)
    _load_resident(step_is_first, [(w_hbm, w_vmem.at[pl.ds(0, D_IN)])], sem)


def _inproj_fwd(x2d, sc1p, sh1, w_in_t):
    t = x2d.shape[0]
    tm = min(PROJ_TILE, t)

    def body(x_ref, sc_ref, sh_ref, w_hbm, proj_ref, u_ref, w_vmem, sem):
        _load_w_in_t(pl.program_id(0) == 0, w_hbm, w_vmem, sem)
        xh, _ = _ln_stats(x_ref[...])
        ub = (xh * sc_ref[...] + sh_ref[...]).astype(BF16)
        u_ref[...] = ub
        proj_ref[...] = _dot(ub, w_vmem[...], NT)

    row = lambda i: (i, 0)
    fix = lambda i: (0, 0)
    return pl.pallas_call(
        body, name="inproj_fwd", grid=(t // tm,),
        in_specs=[pl.BlockSpec((tm, D_MODEL), row), pl.BlockSpec((1, D_MODEL), fix), pl.BlockSpec((1, D_MODEL), fix),
                  pl.BlockSpec(memory_space=pl.ANY)],
        out_specs=(pl.BlockSpec((tm, D_IN_PAD), row), pl.BlockSpec((tm, D_MODEL), row)),
        out_shape=(jax.ShapeDtypeStruct((t, D_IN_PAD), F32), jax.ShapeDtypeStruct((t, D_MODEL), BF16)),
        scratch_shapes=[pltpu.VMEM((D_IN_PAD, D_MODEL), BF16), pltpu.SemaphoreType.DMA((1,))],
        compiler_params=pltpu.CompilerParams(dimension_semantics=("arbitrary",), vmem_limit_bytes=V7X_VMEM_LIMIT),
    )(x2d, sc1p, sh1, w_in_t)


CHUNK_SHIFT = 6


def _ret_tables(t, tl):
    r = lax.broadcasted_iota(jnp.int32, (tl, tl), 0)
    c = lax.broadcasted_iota(jnp.int32, (tl, tl), 1)
    allowed = jnp.right_shift(c, CHUNK_SHIFT) <= jnp.right_shift(r, CHUNK_SHIFT)
    dist = jnp.abs(r - c).astype(F32)
    rowf = lax.broadcasted_iota(jnp.int32, (tl, RET_D), 0).astype(F32)
    lgs = [_log_gamma(h) for h in range(RET_HEADS)]
    dec = jnp.stack([jnp.where(allowed, jnp.exp(lg * dist), 0.0) for lg in lgs])
    qkd = jnp.stack([jnp.exp(lg * (rowf + 1.0)) for lg in lgs] + [jnp.exp(lg * (tl - 1.0 - rowf)) for lg in lgs])
    inv = 1.0 / (10000.0 ** jnp.linspace(0.0, 1.0, RET_D // 2, dtype=F32))
    off = jnp.arange(tl, dtype=F32)[:, None] * inv[None, :]
    start = (jnp.arange(t // tl, dtype=F32) * tl)[:, None] * inv[None, :]
    co, so = jnp.cos(off), jnp.sin(off)
    rot_in = jnp.stack([jnp.concatenate([co, co], 1), jnp.concatenate([so, so], 1),
                        jnp.concatenate([-co, co], 1), jnp.concatenate([-so, so], 1)])
    cs, ss = jnp.cos(start), jnp.sin(start)
    rot_tile = jnp.concatenate([cs, cs, ss, ss], axis=1)
    rot_tile = jnp.broadcast_to(rot_tile[:, None, :], (t // tl, 8, 2 * RET_D))
    return dec, qkd, rot_in, rot_tile


def _tile_gammas(tl):
    return [float(np.exp(np.float32(_log_gamma(h)) * np.float32(tl))) for h in range(RET_HEADS)]


def _tile_rotary(rot_in_ref, rot_tile_ref):
    ca, sa = rot_tile_ref[0, 0:1, 0:RET_D], rot_tile_ref[0, 0:1, RET_D:2 * RET_D]
    cosv = ca * rot_in_ref[0] - sa * rot_in_ref[1]
    sinv = sa * rot_in_ref[2] + ca * rot_in_ref[3]
    return cosv, sinv


def _gla_consts(tl):
    r = lax.broadcasted_iota(jnp.int32, (tl, tl), 0)
    c = lax.broadcasted_iota(jnp.int32, (tl, tl), 1)
    ltri = (c <= r).astype(F32)
    utri = (c >= r).astype(F32)
    lane = lax.broadcasted_iota(jnp.int32, (1, GLA_KW), 1)
    hmask = [((lane >= h * GLA_DK) & (lane < (h + 1) * GLA_DK)).astype(F32) for h in range(GLA_HEADS)]
    rs = lax.broadcasted_iota(jnp.int32, (GLA_HEADS * tl, tl), 0) & (tl - 1)
    cs = lax.broadcasted_iota(jnp.int32, (GLA_HEADS * tl, tl), 1)
    lower = cs <= rs
    same = jnp.right_shift(cs, CHUNK_SHIFT) == jnp.right_shift(rs, CHUNK_SHIFT)
    upper = jnp.logical_and(jnp.logical_not(lower), same)
    return dict(ltri=ltri, utri=utri, hmask=hmask, lower=lower, upper=upper)


def _tile_rows(j, tl):
    return pl.ds(j * tl, tl) if isinstance(j, int) else pl.ds(pl.multiple_of(j * tl, tl), tl)


def _for_tiles(cps, fn):
    if cps == 1:
        fn(0, 0)
    else:
        lax.fori_loop(0, cps, fn, 0)


def _rotate(v, cosv, sinv):
    return v * cosv + pltpu.roll(v, RET_D // 2, 1) * sinv


def _rotate_t(d, cosv, sinv):
    return d * cosv + pltpu.roll(d * sinv, RET_D // 2, 1)


def _stack_heads(v, hmask):
    return jnp.concatenate([v * hmask[h] for h in range(GLA_HEADS)], axis=0)


def _gla_gates(glr, gw, gb, ltri, tl):
    z = _dot_split(glr, gw, NN) + gb
    la = (jnp.minimum(z, 0.0) - jnp.log(1.0 + jnp.exp(-jnp.abs(z)))) * (1.0 / GATE_TAU)
    b = _dot_split(ltri, la, NN, a_exact=True)
    level = b[tl // 2 - 1:tl // 2, :]
    ep = jnp.exp(jnp.clip(b - level, -80.0, 80.0))
    em = jnp.exp(jnp.clip(level - b, -80.0, 80.0))
    bl = b[tl - 1:tl, :]
    return z, b, bl, ep, em


def _mixer_fwd(proj, tables, gw_pad, gb, rnw, gnw):
    t = proj.shape[0]
    tc = min(MIX_TILE, t)
    tr, tg = tc, min(GLA_SUB, tc)
    nsteps = t // tc
    scale_r = RET_D ** -0.5
    scale_g = GLA_DK ** -0.5
    gammas = _tile_gammas(tr)

    def body(rq_ref, rk_ref, rv_ref, rg_ref, gq_ref, gk_ref, gv_ref, gg_ref, glr_ref,
             dec_ref, qkd_ref, rot_in_ref, rot_tile_ref, gw_ref, gb_ref, rnw_ref, gnw_ref,
             mix_ref, oraw_ref, qrb_ref, krb_ref, rst_ref, sst_ref, r_scr, s_scr):
        @pl.when(pl.program_id(0) == 0)
        def _():
            r_scr[...] = jnp.zeros_like(r_scr)
            s_scr[...] = jnp.zeros_like(s_scr)

        gla_k = _gla_consts(tg)

        def ret_tile(j, carry):
            rows = _tile_rows(j, tr)
            cosv, sinv = _tile_rotary(rot_in_ref, rot_tile_ref)
            for h in range(RET_HEADS):
                cols = slice(h * RET_D, (h + 1) * RET_D)
                qr = _rotate(rq_ref[rows, cols], cosv, sinv) * scale_r
                kr = _rotate(rk_ref[rows, cols], cosv, sinv)
                vb = rv_ref[rows, cols].astype(BF16)
                qb, kb = qr.astype(BF16), kr.astype(BF16)
                qrb_ref[rows, cols] = qb
                krb_ref[rows, cols] = kb
                p = _dot(qb, kb, NT) * dec_ref[h]
                rp = r_scr[cols, :]
                o = _dot(p.astype(BF16), vb) + _dot((qr * qkd_ref[h]).astype(BF16), rp.astype(BF16))
                rst_ref[j, cols, :] = rp
                r_scr[cols, :] = gammas[h] * rp + _dot((kr * qkd_ref[RET_HEADS + h]).astype(BF16), vb, TN)
                oraw_ref[rows, cols] = o
                oc = o - jnp.mean(o, axis=-1, keepdims=True)
                n = oc * lax.rsqrt(jnp.mean(oc * oc, axis=-1, keepdims=True) + LN_EPS)
                g = rg_ref[rows, cols]
                mix_ref[rows, cols] = (n * rnw_ref[:, cols] * (g * _sigmoid(g))).astype(BF16)
            return carry

        def gla_tile(j, carry):
            k = gla_k
            tl = tg
            rows = _tile_rows(j, tg)
            _, b, bl, ep, em = _gla_gates(glr_ref[rows, :], gw_ref[...], gb_ref[...], k["ltri"], tl)
            qs = gq_ref[rows, :] * scale_g
            kk = gk_ref[rows, :]
            x_all = _dot(_stack_heads(qs * ep, k["hmask"]).astype(BF16), (kk * em).astype(BF16), NT)
            y_all = _dot(_stack_heads(qs * em, k["hmask"]).astype(BF16), (kk * ep).astype(BF16), NT)
            a_all = jnp.where(k["lower"], x_all, jnp.where(k["upper"], y_all, 0.0)).astype(BF16)
            st = s_scr[...]
            oq = _dot(_stack_heads(qs * jnp.exp(b), k["hmask"]).astype(BF16), st.astype(BF16), NT)
            kg = kk * jnp.exp(bl - b)
            sst_ref[j] = st
            st_new = st * jnp.exp(bl)
            for h in range(GLA_HEADS):
                cols = slice(h * GLA_DV, (h + 1) * GLA_DV)
                hr = slice(h * tl, (h + 1) * tl)
                vb = gv_ref[rows, cols].astype(BF16)
                o = _dot(a_all[hr, :], vb) + oq[hr, :]
                st_new = st_new + _dot(vb, (kg * k["hmask"][h]).astype(BF16), TN)
                ocols = slice(RET_HEADS * RET_D + h * GLA_DV, RET_HEADS * RET_D + (h + 1) * GLA_DV)
                oraw_ref[rows, ocols] = o
                n = o * lax.rsqrt(jnp.mean(o * o, axis=-1, keepdims=True) + LN_EPS)
                g = gg_ref[rows, cols]
                mix_ref[rows, ocols] = (n * gnw_ref[:, cols] * (g * _sigmoid(g))).astype(BF16)
            s_scr[...] = st_new
            return carry

        _for_tiles(tc // tr, ret_tile)
        _for_tiles(tc // tg, gla_tile)

    def col(width, off):
        return pl.BlockSpec((tc, width), lambda i, o=off // width: (i, o))

    fix = lambda i: (0, 0)
    fix3 = lambda i: (0, 0, 0)
    dec, qkd, rot_in, rot_tile = tables
    in_specs = [col(512, OFF_RQ), col(512, OFF_RK), col(512, OFF_RV), col(512, OFF_RG),
                col(256, OFF_GQ), col(256, OFF_GK), col(512, OFF_GV), col(512, OFF_GG), col(128, OFF_GLR),
                pl.BlockSpec(dec.shape, fix3), pl.BlockSpec(qkd.shape, fix3), pl.BlockSpec(rot_in.shape, fix3),
                pl.BlockSpec((1, 8, 2 * RET_D), lambda i: (i, 0, 0)),
                pl.BlockSpec((128, GLA_KW), fix), pl.BlockSpec((1, GLA_KW), fix),
                pl.BlockSpec((1, 512), fix), pl.BlockSpec((1, 512), fix)]
    half = pl.BlockSpec((tc, RET_HEADS * RET_D), lambda i: (i, 0))
    out_specs = (pl.BlockSpec((tc, D_MODEL), lambda i: (i, 0)), pl.BlockSpec((tc, D_MODEL), lambda i: (i, 0)),
                 half, half,
                 pl.BlockSpec((tc // tr, RET_HEADS * RET_D, RET_D), lambda i: (i, 0, 0)),
                 pl.BlockSpec((tc // tg, GLA_DV, GLA_KW), lambda i: (i, 0, 0)))
    out_shape = (jax.ShapeDtypeStruct((t, D_MODEL), BF16), jax.ShapeDtypeStruct((t, D_MODEL), F32),
                 jax.ShapeDtypeStruct((t, RET_HEADS * RET_D), BF16), jax.ShapeDtypeStruct((t, RET_HEADS * RET_D), BF16),
                 jax.ShapeDtypeStruct((t // tr, RET_HEADS * RET_D, RET_D), F32),
                 jax.ShapeDtypeStruct((t // tg, GLA_DV, GLA_KW), F32))
    return pl.pallas_call(
        body, name="mixer_fwd", grid=(nsteps,), in_specs=in_specs, out_specs=out_specs, out_shape=out_shape,
        scratch_shapes=[pltpu.VMEM((RET_HEADS * RET_D, RET_D), F32), pltpu.VMEM((GLA_DV, GLA_KW), F32)],
        compiler_params=pltpu.CompilerParams(dimension_semantics=("arbitrary",), vmem_limit_bytes=V7X_VMEM_LIMIT),
    )(*([proj] * 9), dec, qkd, rot_in, rot_tile, gw_pad, gb, rnw, gnw)


def _mid_fwd(mixed, x2d, target, vecs, w_out_b, w1_b, w2_b):
    t = x2d.shape[0]
    tm = min(ROW_TILE, t)

    def body(mix_ref, x_ref, tgt_ref, v_ref, wo_hbm, w1_hbm, w2_hbm,
             m_ref, x1n_ref, rstd_ref, u2_ref, a_ref, df_ref, dh2_ref, acc_ref, wo, w1, w2, sem):
        first = pl.program_id(0) == 0
        _load_resident(first, [(wo_hbm, wo), (w1_hbm, w1), (w2_hbm, w2)], sem)

        @pl.when(first)
        def _():
            acc_ref[...] = jnp.zeros_like(acc_ref)

        gate1, sc2p, sh2, gate2 = v_ref[0:1, :], v_ref[1:2, :], v_ref[2:3, :], v_ref[3:4, :]
        l1w, l1b, l2w, l2b = v_ref[4:5, :], v_ref[5:6, :], v_ref[6:7, :], v_ref[7:8, :]
        m = _dot(mix_ref[...], wo[...])
        m_ref[...] = m.astype(BF16)
        x1n, rstd1 = _ln_stats(ALPHA * x_ref[...] + gate1 * m)
        x1n_ref[...] = x1n
        rstd_ref[...] = rstd1
        x1 = x1n * l1w + l1b
        xh1, _ = _ln_stats(x1)
        u2 = (xh1 * sc2p + sh2).astype(BF16)
        u2_ref[...] = u2
        f = jnp.zeros((tm, D_MODEL), F32)
        for j in range(N_DEV):
            cols = slice(j * FF_COLS, (j + 1) * FF_COLS)
            a = _dot(u2, w1[j])
            a_ref[:, cols] = a.astype(BF16)
            r = jnp.maximum(a, 0.0)
            f = f + _dot((r * r).astype(BF16), w2[cols, :])
        yh, rstd2 = _ln_stats(ALPHA * x1 + gate2 * f)
        e = yh * l2w + l2b - tgt_ref[...]
        dy = e * (1.0 / D_MODEL)
        dh2 = _ln_bwd(dy * l2w, yh, rstd2)
        dh2_ref[...] = dh2
        df_ref[...] = (dh2 * gate2).astype(BF16)
        acc_ref[0:1, :] += jnp.sum(dy * yh, axis=0, keepdims=True)
        acc_ref[1:2, :] += jnp.sum(dy, axis=0, keepdims=True)
        acc_ref[2:3, :] += jnp.sum(dh2 * f, axis=0, keepdims=True)
        acc_ref[3:4, :] += jnp.sum(e * e, axis=0, keepdims=True) * (0.5 / D_MODEL)

    row = lambda i: (i, 0)
    fix = lambda i: (0, 0)
    hbm = pl.BlockSpec(memory_space=pl.ANY)
    return pl.pallas_call(
        body, name="mid_fwd", grid=(t // tm,),
        in_specs=[pl.BlockSpec((tm, D_MODEL), row), pl.BlockSpec((tm, D_MODEL), row), pl.BlockSpec((tm, D_MODEL), row),
                  pl.BlockSpec((8, D_MODEL), fix), hbm, hbm, hbm],
        out_specs=(pl.BlockSpec((tm, D_MODEL), row), pl.BlockSpec((tm, D_MODEL), row), pl.BlockSpec((tm, 1), row),
                   pl.BlockSpec((tm, D_MODEL), row), pl.BlockSpec((tm, D_FF), row), pl.BlockSpec((tm, D_MODEL), row),
                   pl.BlockSpec((tm, D_MODEL), row), pl.BlockSpec((8, D_MODEL), fix)),
        out_shape=(jax.ShapeDtypeStruct((t, D_MODEL), BF16), jax.ShapeDtypeStruct((t, D_MODEL), F32),
                   jax.ShapeDtypeStruct((t, 1), F32), jax.ShapeDtypeStruct((t, D_MODEL), BF16),
                   jax.ShapeDtypeStruct((t, D_FF), BF16), jax.ShapeDtypeStruct((t, D_MODEL), BF16),
                   jax.ShapeDtypeStruct((t, D_MODEL), F32), jax.ShapeDtypeStruct((8, D_MODEL), F32)),
        scratch_shapes=[pltpu.VMEM((D_MODEL, D_MODEL), BF16), pltpu.VMEM((N_DEV, D_MODEL, FF_COLS), BF16),
                        pltpu.VMEM((D_FF, D_MODEL), BF16), pltpu.SemaphoreType.DMA((3,))],
        compiler_params=pltpu.CompilerParams(dimension_semantics=("arbitrary",), vmem_limit_bytes=V7X_VMEM_LIMIT),
    )(mixed, x2d, target, vecs, w_out_b, w1_b, w2_b)


def _ffn_bwd(df, a, dh2, x1n, rstd1, m, vecs, w_out_b, w1_b, w2_b):
    t = x1n.shape[0]
    tm = min(ROW_TILE, t)

    def body(df_ref, a_ref, dh2_ref, x1n_ref, rstd_ref, m_ref, v_ref, wo_hbm, w1_hbm, w2_hbm,
             da_ref, dm_ref, dmix_ref, dxa_ref, acc_ref, wo, w1, w2, sem):
        first = pl.program_id(0) == 0
        _load_resident(first, [(wo_hbm, wo), (w1_hbm, w1), (w2_hbm, w2)], sem)

        @pl.when(first)
        def _():
            acc_ref[...] = jnp.zeros_like(acc_ref)

        gate1, sc2p, l1w, l1b = v_ref[0:1, :], v_ref[1:2, :], v_ref[2:3, :], v_ref[3:4, :]
        df = df_ref[...]
        du2 = jnp.zeros((tm, D_MODEL), F32)
        for j in range(N_DEV):
            cols = slice(j * FF_COLS, (j + 1) * FF_COLS)
            dr2 = _dot(df, w2[cols, :], NT)
            da = (dr2 * (2.0 * jnp.maximum(a_ref[:, cols].astype(F32), 0.0))).astype(BF16)
            da_ref[:, cols] = da
            du2 = du2 + _dot(da, w1[j], NT)
        x1n = x1n_ref[...]
        xh1, rstd0 = _ln_stats(x1n * l1w + l1b)
        dx1 = ALPHA * dh2_ref[...] + _ln_bwd(du2 * sc2p, xh1, rstd0)
        dh1 = _ln_bwd(dx1 * l1w, x1n, rstd_ref[...])
        dxa_ref[...] = ALPHA * dh1
        dm = (dh1 * gate1).astype(BF16)
        dm_ref[...] = dm
        dmix_ref[...] = _dot(dm, wo[...], NT)
        acc_ref[0:1, :] += jnp.sum(du2 * xh1, axis=0, keepdims=True)
        acc_ref[1:2, :] += jnp.sum(du2, axis=0, keepdims=True)
        acc_ref[2:3, :] += jnp.sum(dx1 * x1n, axis=0, keepdims=True)
        acc_ref[3:4, :] += jnp.sum(dx1, axis=0, keepdims=True)
        acc_ref[4:5, :] += jnp.sum(dh1 * m_ref[...].astype(F32), axis=0, keepdims=True)

    row = lambda i: (i, 0)
    fix = lambda i: (0, 0)
    hbm = pl.BlockSpec(memory_space=pl.ANY)
    return pl.pallas_call(
        body, name="ffn_bwd", grid=(t // tm,),
        in_specs=[pl.BlockSpec((tm, D_MODEL), row), pl.BlockSpec((tm, D_FF), row), pl.BlockSpec((tm, D_MODEL), row),
                  pl.BlockSpec((tm, D_MODEL), row), pl.BlockSpec((tm, 1), row), pl.BlockSpec((tm, D_MODEL), row),
                  pl.BlockSpec((8, D_MODEL), fix), hbm, hbm, hbm],
        out_specs=(pl.BlockSpec((tm, D_FF), row), pl.BlockSpec((tm, D_MODEL), row), pl.BlockSpec((tm, D_MODEL), row),
                   pl.BlockSpec((tm, D_MODEL), row), pl.BlockSpec((8, D_MODEL), fix)),
        out_shape=(jax.ShapeDtypeStruct((t, D_FF), BF16), jax.ShapeDtypeStruct((t, D_MODEL), BF16),
                   jax.ShapeDtypeStruct((t, D_MODEL), F32), jax.ShapeDtypeStruct((t, D_MODEL), F32),
                   jax.ShapeDtypeStruct((8, D_MODEL), F32)),
        scratch_shapes=[pltpu.VMEM((D_MODEL, D_MODEL), BF16), pltpu.VMEM((N_DEV, D_MODEL, FF_COLS), BF16),
                        pltpu.VMEM((D_FF, D_MODEL), BF16), pltpu.SemaphoreType.DMA((3,))],
        compiler_params=pltpu.CompilerParams(dimension_semantics=("arbitrary",), vmem_limit_bytes=V7X_VMEM_LIMIT),
    )(df, a, dh2, x1n, rstd1, m, vecs, w_out_b, w1_b, w2_b)


def _matmul_tn(lhs, rhs, tmm, tn, tk, name, relu_sq=False, col_slab=None, out_rows=None):
    t, mm = lhs.shape
    assert out_rows is None or (col_slab is None and tmm == mm)
    nn = rhs.shape[1]
    tk = min(tk, t)
    nk = t // tk

    def body(l_ref, r_ref, o_ref, acc):
        kk = pl.program_id(2)

        @pl.when(kk == 0)
        def _():
            acc[...] = jnp.zeros_like(acc)

        l = l_ref[...]
        if relu_sq:
            lf = jnp.maximum(l.astype(F32), 0.0)
            l = (lf * lf).astype(BF16)
        acc[...] += _dot(l, r_ref[...], TN)

        @pl.when(kk == nk - 1)
        def _():
            if out_rows is not None:
                for s in range(N_DEV):
                    o_ref[s] = acc[s * out_rows:(s + 1) * out_rows, :].astype(o_ref.dtype)
            elif col_slab is None:
                o_ref[...] = acc[...].astype(o_ref.dtype)
            else:
                for s in range(tn // col_slab):
                    o_ref[s] = acc[:, s * col_slab:(s + 1) * col_slab].astype(o_ref.dtype)

    if out_rows is not None:
        out_spec = pl.BlockSpec((N_DEV, out_rows, tn), lambda i, j, k: (0, 0, j))
        out_shape = jax.ShapeDtypeStruct((N_DEV, out_rows, nn), BF16)
    elif col_slab is None:
        out_spec = pl.BlockSpec((tmm, tn), lambda i, j, k: (i, j))
        out_shape = jax.ShapeDtypeStruct((mm, nn), BF16)
    else:
        out_spec = pl.BlockSpec((tn // col_slab, tmm, col_slab), lambda i, j, k: (j, i, 0))
        out_shape = jax.ShapeDtypeStruct((nn // col_slab, mm, col_slab), BF16)
    return pl.pallas_call(
        body, name=name, grid=(mm // tmm, nn // tn, nk),
        in_specs=[pl.BlockSpec((tk, tmm), lambda i, j, k: (k, i)), pl.BlockSpec((tk, tn), lambda i, j, k: (k, j))],
        out_specs=out_spec,
        out_shape=out_shape,
        scratch_shapes=[pltpu.VMEM((tmm, tn), F32)],
        compiler_params=pltpu.CompilerParams(dimension_semantics=("arbitrary", "arbitrary", "arbitrary"),
                                             vmem_limit_bytes=V7X_VMEM_LIMIT),
    )(lhs, rhs)


def _mixer_bwd(dmix, proj, qrb, krb, oraw, tables, rst, sst, gw_pad, gb, rnw, gnw):
    t = proj.shape[0]
    tc = min(MIX_TILE, t)
    tr, tg = tc, min(GLA_SUB, tc)
    nsteps = t // tc
    scale_r = RET_D ** -0.5
    scale_g = GLA_DK ** -0.5
    gammas = _tile_gammas(tr)

    def body(dmix_ref, qrb_ref, krb_ref, rv_ref, rg_ref, gq_ref, gk_ref, gv_ref, gg_ref, glr_ref, oraw_ref,
             dec_ref, qkd_ref, rot_in_ref, rot_tile_ref, rst_ref, sst_ref, gw_ref, gb_ref, rnw_ref, gnw_ref,
             dproj_ref, dgw_ref, dvec_ref, dr_scr, ds_scr):
        @pl.when(pl.program_id(0) == 0)
        def _():
            dr_scr[...] = jnp.zeros_like(dr_scr)
            ds_scr[...] = jnp.zeros_like(ds_scr)
            dgw_ref[...] = jnp.zeros_like(dgw_ref)
            dvec_ref[...] = jnp.zeros_like(dvec_ref)

        gla_k = _gla_consts(tg)
        last_row = lax.broadcasted_iota(jnp.int32, (tg, GLA_KW), 0) == tg - 1

        def ret_tile(jj, carry):
            j = tc // tr - 1 - jj
            rows = _tile_rows(j, tr)
            cosv, sinv = _tile_rotary(rot_in_ref, rot_tile_ref)
            for h in range(RET_HEADS):
                cols = slice(h * RET_D, (h + 1) * RET_D)
                o = oraw_ref[rows, cols]
                g = rg_ref[rows, cols]
                w = rnw_ref[:, cols]
                dout = dmix_ref[rows, cols]
                oc = o - jnp.mean(o, axis=-1, keepdims=True)
                inv = lax.rsqrt(jnp.mean(oc * oc, axis=-1, keepdims=True) + LN_EPS)
                n = oc * inv
                sg = _sigmoid(g)
                sil = g * sg
                dn = dout * w * sil
                dvec_ref[0:1, cols] += jnp.sum(dout * n * sil, axis=0, keepdims=True)
                dproj_ref[rows, OFF_RG + h * RET_D:OFF_RG + (h + 1) * RET_D] = (
                    dout * n * w * (sg * (1.0 + g * (1.0 - sg)))).astype(BF16)
                doc = inv * (dn - n * jnp.mean(dn * n, axis=-1, keepdims=True))
                do = doc - jnp.mean(doc, axis=-1, keepdims=True)

                qb, kb = qrb_ref[rows, cols], krb_ref[rows, cols]
                qr, kr = qb.astype(F32), kb.astype(F32)
                vb = rv_ref[rows, cols].astype(BF16)
                dob = do.astype(BF16)
                qd, kd = qkd_ref[h], qkd_ref[RET_HEADS + h]
                p = _dot(qb, kb, NT) * dec_ref[h]
                rp = rst_ref[j, cols, :].astype(BF16)
                dr = dr_scr[cols, :]
                drb = dr.astype(BF16)
                dpb = (_dot(dob, vb, NT) * dec_ref[h]).astype(BF16)
                dqr = _dot(dpb, kb) + _dot(dob, rp, NT) * qd
                dkr = _dot(dpb, qb, TN) + _dot(vb, drb, NT) * kd
                dv = _dot(p.astype(BF16), dob, TN) + _dot((kr * kd).astype(BF16), drb)
                dr_scr[cols, :] = gammas[h] * dr + _dot((qr * qd).astype(BF16), dob, TN)
                dproj_ref[rows, OFF_RQ + h * RET_D:OFF_RQ + (h + 1) * RET_D] = (
                    _rotate_t(dqr, cosv, sinv) * scale_r).astype(BF16)
                dproj_ref[rows, OFF_RK + h * RET_D:OFF_RK + (h + 1) * RET_D] = _rotate_t(dkr, cosv, sinv).astype(BF16)
                dproj_ref[rows, OFF_RV + h * RET_D:OFF_RV + (h + 1) * RET_D] = dv.astype(BF16)
            return carry

        def gla_tile(jj, carry):
            k = gla_k
            tl = tg
            j = tc // tg - 1 - jj
            rows = _tile_rows(j, tg)
            glr = glr_ref[rows, :]
            z, b, bl, ep, em = _gla_gates(glr, gw_ref[...], gb_ref[...], k["ltri"], tl)
            qs = gq_ref[rows, :] * scale_g
            kk = gk_ref[rows, :]
            eb = jnp.exp(b)
            ekb = jnp.exp(bl - b)
            ebl = jnp.exp(bl)
            ql, qu, kl, ku = qs * ep, qs * em, kk * em, kk * ep
            qg, kg = qs * eb, kk * ekb
            qlm = _stack_heads(ql, k["hmask"]).astype(BF16)
            qum = _stack_heads(qu, k["hmask"]).astype(BF16)
            klb, kub = kl.astype(BF16), ku.astype(BF16)
            a_all = jnp.where(k["lower"], _dot(qlm, klb, NT),
                              jnp.where(k["upper"], _dot(qum, kub, NT), 0.0)).astype(BF16)
            st = sst_ref[j]
            stb = st.astype(BF16)
            ds = ds_scr[...]
            dsb = ds.astype(BF16)
            ds_new = ds * ebl
            da_parts = []
            dqg = jnp.zeros((tl, GLA_KW), F32)
            dkg = jnp.zeros((tl, GLA_KW), F32)
            for h in range(GLA_HEADS):
                cols = slice(h * GLA_DV, (h + 1) * GLA_DV)
                hr = slice(h * tl, (h + 1) * tl)
                ocols = slice(RET_HEADS * RET_D + h * GLA_DV, RET_HEADS * RET_D + (h + 1) * GLA_DV)
                o = oraw_ref[rows, ocols]
                g = gg_ref[rows, cols]
                w = gnw_ref[:, cols]
                dout = dmix_ref[rows, ocols]
                inv = lax.rsqrt(jnp.mean(o * o, axis=-1, keepdims=True) + LN_EPS)
                n = o * inv
                sg = _sigmoid(g)
                sil = g * sg
                dn = dout * w * sil
                dvec_ref[1:2, cols] += jnp.sum(dout * n * sil, axis=0, keepdims=True)
                dproj_ref[rows, OFF_GG + h * GLA_DV:OFF_GG + (h + 1) * GLA_DV] = (
                    dout * n * w * (sg * (1.0 + g * (1.0 - sg)))).astype(BF16)
                dob = (inv * (dn - n * jnp.mean(dn * n, axis=-1, keepdims=True))).astype(BF16)
                vb = gv_ref[rows, cols].astype(BF16)
                mh = k["hmask"][h]
                da_parts.append(_dot(dob, vb, NT))
                dv = _dot(a_all[hr, :], dob, TN) + _dot((kg * mh).astype(BF16), dsb, NT)
                dproj_ref[rows, OFF_GV + h * GLA_DV:OFF_GV + (h + 1) * GLA_DV] = dv.astype(BF16)
                dkg = dkg + mh * _dot(vb, dsb)
                dqg = dqg + mh * _dot(dob, stb)
                ds_new = ds_new + _dot(dob, (qg * mh).astype(BF16), TN)
            da_all = jnp.concatenate(da_parts, axis=0)
            dal = jnp.where(k["lower"], da_all, 0.0).astype(BF16)
            dau = jnp.where(k["upper"], da_all, 0.0).astype(BF16)
            dqlm = _dot(dal, klb)
            dqum = _dot(dau, kub)
            dql = jnp.zeros((tl, GLA_KW), F32)
            dqu = jnp.zeros((tl, GLA_KW), F32)
            for h in range(GLA_HEADS):
                hr = slice(h * tl, (h + 1) * tl)
                dql = dql + k["hmask"][h] * dqlm[hr, :]
                dqu = dqu + k["hmask"][h] * dqum[hr, :]
            dkl = _dot(dal, qlm, TN)
            dku = _dot(dau, qum, TN)
            dbl = (jnp.sum(dkg * kg, axis=0, keepdims=True)
                   + jnp.sum(ds * st, axis=0, keepdims=True) * ebl)
            ds_scr[...] = ds_new
            dqs = dql * ep + dqu * em + dqg * eb
            dk = dkl * em + dku * ep + dkg * ekb
            db = dql * ql - dkl * kl - dqu * qu + dku * ku + dqg * qg - dkg * kg
            db = db + jnp.where(last_row, dbl, 0.0)
            dla = _dot_split(k["utri"], db, NN, a_exact=True)
            dz = dla * (1.0 / GATE_TAU) * _sigmoid(-z)
            dvec_ref[2:3, 0:GLA_KW] += jnp.sum(dz, axis=0, keepdims=True)
            dgw_ref[...] += _dot_split(glr, dz, TN)
            dproj_ref[rows, OFF_GLR:OFF_GLR + 128] = _dot(dz.astype(BF16), gw_ref[...].astype(BF16), NT).astype(BF16)
            dproj_ref[rows, OFF_GQ:OFF_GQ + GLA_KW] = (dqs * scale_g).astype(BF16)
            dproj_ref[rows, OFF_GK:OFF_GK + GLA_KW] = dk.astype(BF16)
            return carry

        _for_tiles(tc // tr, ret_tile)
        _for_tiles(tc // tg, gla_tile)

    rev = lambda i: (nsteps - 1 - i, 0)

    def col(width, off):
        return pl.BlockSpec((tc, width), lambda i, o=off // width: (nsteps - 1 - i, o))

    fix = lambda i: (0, 0)
    fix3 = lambda i: (0, 0, 0)
    dec, qkd, rot_in, rot_tile = tables
    half = pl.BlockSpec((tc, RET_HEADS * RET_D), rev)
    in_specs = [pl.BlockSpec((tc, D_MODEL), rev), half, half, col(512, OFF_RV), col(512, OFF_RG),
                col(256, OFF_GQ), col(256, OFF_GK), col(512, OFF_GV), col(512, OFF_GG), col(128, OFF_GLR),
                pl.BlockSpec((tc, D_MODEL), rev),
                pl.BlockSpec(dec.shape, fix3), pl.BlockSpec(qkd.shape, fix3), pl.BlockSpec(rot_in.shape, fix3),
                pl.BlockSpec((1, 8, 2 * RET_D), lambda i: (nsteps - 1 - i, 0, 0)),
                pl.BlockSpec((tc // tr, RET_HEADS * RET_D, RET_D), lambda i: (nsteps - 1 - i, 0, 0)),
                pl.BlockSpec((tc // tg, GLA_DV, GLA_KW), lambda i: (nsteps - 1 - i, 0, 0)),
                pl.BlockSpec((128, GLA_KW), fix), pl.BlockSpec((1, GLA_KW), fix),
                pl.BlockSpec((1, 512), fix), pl.BlockSpec((1, 512), fix)]
    out_specs = (pl.BlockSpec((tc, D_IN_PAD), rev), pl.BlockSpec((128, GLA_KW), fix), pl.BlockSpec((8, 512), fix))
    out_shape = (jax.ShapeDtypeStruct((t, D_IN_PAD), BF16), jax.ShapeDtypeStruct((128, GLA_KW), F32),
                 jax.ShapeDtypeStruct((8, 512), F32))
    return pl.pallas_call(
        body, name="mixer_bwd", grid=(nsteps,), in_specs=in_specs, out_specs=out_specs, out_shape=out_shape,
        scratch_shapes=[pltpu.VMEM((RET_HEADS * RET_D, RET_D), F32), pltpu.VMEM((GLA_DV, GLA_KW), F32)],
        compiler_params=pltpu.CompilerParams(dimension_semantics=("arbitrary",), vmem_limit_bytes=V7X_VMEM_LIMIT),
    )(dmix, qrb, krb, *([proj] * 7), oraw, dec, qkd, rot_in, rot_tile, rst, sst, gw_pad, gb, rnw, gnw)


def _inproj_bwd(dproj, x2d, dxa, sc1p, w_in_t):
    t = x2d.shape[0]
    tm = min(PROJ_TILE, t)

    def body(dp_ref, x_ref, dxa_ref, sc_ref, w_hbm, gx_ref, acc_ref, w_vmem, sem):
        first = pl.program_id(0) == 0
        _load_w_in_t(first, w_hbm, w_vmem, sem)

        @pl.when(first)
        def _():
            acc_ref[...] = jnp.zeros_like(acc_ref)

        du = _dot(dp_ref[...], w_vmem[...])
        xh, rstd = _ln_stats(x_ref[...])
        gx_ref[...] = dxa_ref[...] + _ln_bwd(du * sc_ref[...], xh, rstd)
        acc_ref[0:1, :] += jnp.sum(du * xh, axis=0, keepdims=True)
        acc_ref[1:2, :] += jnp.sum(du, axis=0, keepdims=True)

    row = lambda i: (i, 0)
    fix = lambda i: (0, 0)
    return pl.pallas_call(
        body, name="inproj_bwd", grid=(t // tm,),
        in_specs=[pl.BlockSpec((tm, D_IN_PAD), row), pl.BlockSpec((tm, D_MODEL), row), pl.BlockSpec((tm, D_MODEL), row),
                  pl.BlockSpec((1, D_MODEL), fix), pl.BlockSpec(memory_space=pl.ANY)],
        out_specs=(pl.BlockSpec((tm, D_MODEL), row), pl.BlockSpec((8, D_MODEL), fix)),
        out_shape=(jax.ShapeDtypeStruct((t, D_MODEL), F32), jax.ShapeDtypeStruct((8, D_MODEL), F32)),
        scratch_shapes=[pltpu.VMEM((D_IN_PAD, D_MODEL), BF16), pltpu.SemaphoreType.DMA((1,))],
        compiler_params=pltpu.CompilerParams(dimension_semantics=("arbitrary",), vmem_limit_bytes=V7X_VMEM_LIMIT),
    )(dproj, x2d, dxa, sc1p, w_in_t)


def _adam_math(w, g, m, v):
    m = ADAM_B1 * m + (1.0 - ADAM_B1) * g
    v = ADAM_B2 * v + (1.0 - ADAM_B2) * (g * g)
    m_hat = m / (1.0 - ADAM_B1 ** ADAM_STEP)
    v_hat = v / (1.0 - ADAM_B2 ** ADAM_STEP)
    delta = -ADAM_LR * (m_hat / (jnp.sqrt(v_hat) + ADAM_EPS) + ADAM_WD * w)
    return delta, m, v


def _adamw(w, gparts, m, v, name):
    nparts, rows, cols = gparts.shape
    tr = rows
    for cand in (512, 256, 128, 64, 32, 16, 8):
        if rows % cand == 0:
            tr = cand
            break

    def body(w_ref, g_ref, m_ref, v_ref, go_ref, d_ref, mo_ref, vo_ref):
        g = g_ref[0].astype(F32)
        for p in range(1, nparts):
            g = g + g_ref[p].astype(F32)
        delta, mn, vn = _adam_math(w_ref[...], g, m_ref[...], v_ref[...])
        go_ref[...] = g
        d_ref[...] = delta
        mo_ref[...] = mn
        vo_ref[...] = vn

    blk = pl.BlockSpec((tr, cols), lambda i: (i, 0))
    shp = jax.ShapeDtypeStruct((rows, cols), F32)
    return pl.pallas_call(
        body, name=name, grid=(rows // tr,),
        in_specs=[blk, pl.BlockSpec((nparts, tr, cols), lambda i: (0, i, 0)), blk, blk],
        out_specs=(blk, blk, blk, blk), out_shape=(shp, shp, shp, shp),
        compiler_params=pltpu.CompilerParams(dimension_semantics=("arbitrary",), vmem_limit_bytes=V7X_VMEM_LIMIT),
    )(w, gparts, m, v)


def _small_reduce(gathered, gathered_gw, c_all, dmod_cols):
    def body(g_ref, gw_ref, c_ref, dm_ref, sum_ref, gwsum_ref, gb_ref, gwa_ref):
        s = g_ref[0]
        sw = gw_ref[0]
        for p in range(1, N_DEV):
            s = s + g_ref[p]
            sw = sw + gw_ref[p]
        sum_ref[...] = s
        gwsum_ref[...] = sw
        for i in range(6):
            gb_ref[:, i * D_MODEL:(i + 1) * D_MODEL] = s[i:i + 1, :]
        cc = c_ref[...]
        gwa_ref[...] = _dot(cc * _sigmoid(cc), dm_ref[...], TN, HIGHEST)

    vm = pl.BlockSpec(memory_space=pltpu.VMEM)
    return pl.pallas_call(
        body, name="small_reduce",
        out_shape=(jax.ShapeDtypeStruct(gathered.shape[1:], F32), jax.ShapeDtypeStruct(gathered_gw.shape[1:], F32),
                   jax.ShapeDtypeStruct((1, 6 * D_MODEL), F32), jax.ShapeDtypeStruct((D_MODEL, ADA_COLS), F32)),
        in_specs=[vm] * 4, out_specs=(vm, vm, vm, vm),
        compiler_params=pltpu.CompilerParams(vmem_limit_bytes=V7X_VMEM_LIMIT),
    )(gathered, gathered_gw, c_all, dmod_cols)


SMR_LN1W, SMR_LN1B, SMR_LN2W, SMR_LN2B, SMR_NORMS, SMR_MISC = 6, 7, 8, 9, 10, 11


def _adamw_small(gsum, g_b_ada, g_ggw, params, moms, vels):
    n = len(params)

    def body(*refs):
        gsum_ref, gb_ref, gw_ref = refs[:3]
        w_refs, m_refs, v_refs = refs[3:3 + n], refs[3 + n:3 + 2 * n], refs[3 + 2 * n:3 + 3 * n]
        outs = refs[3 + 3 * n:]
        g_refs, d_refs, mo_refs, vo_refs = outs[:n - 1], outs[n - 1:2 * n - 1], outs[2 * n - 1:3 * n - 1], outs[3 * n - 1:]
        grads = [gb_ref[...],
                 gsum_ref[SMR_NORMS:SMR_NORMS + 1, 0:512],
                 gsum_ref[SMR_MISC:SMR_MISC + 1, 0:GLA_KW],
                 gsum_ref[SMR_NORMS:SMR_NORMS + 1, 512:1024],
                 gsum_ref[SMR_LN1W:SMR_LN1W + 1, :], gsum_ref[SMR_LN1B:SMR_LN1B + 1, :],
                 gsum_ref[SMR_LN2W:SMR_LN2W + 1, :], gsum_ref[SMR_LN2B:SMR_LN2B + 1, :],
                 gw_ref[...]]
        for i in range(n):
            delta, mn, vn = _adam_math(w_refs[i][...], grads[i], m_refs[i][...], v_refs[i][...])
            if i < n - 1:
                g_refs[i][...] = grads[i]
            d_refs[i][...] = delta
            mo_refs[i][...] = mn
            vo_refs[i][...] = vn

    vm = pl.BlockSpec(memory_space=pltpu.VMEM)
    shapes = [jax.ShapeDtypeStruct(p.shape, F32) for p in params]
    n_in = 3 + 3 * n
    out_shape = tuple(shapes[:n - 1] + shapes * 3)
    return pl.pallas_call(
        body, name="adamw_small", out_shape=out_shape,
        in_specs=[vm] * n_in, out_specs=tuple([vm] * len(out_shape)),
        compiler_params=pltpu.CompilerParams(vmem_limit_bytes=V7X_VMEM_LIMIT),
    )(gsum, g_b_ada, g_ggw, *params, *moms, *vels)


def kernel(x, c, w_ada, b_ada, w_in, ret_norm_w, gla_gate_w, gla_gate_b, gla_norm_w, w_out, ln1_w, ln1_b, w_ff1, w_ff2, ln2_w, ln2_b, loss_target, m_w_ada, m_b_ada, m_w_in, m_ret_norm_w, m_gla_gate_w, m_gla_gate_b, m_gla_norm_w, m_w_out, m_ln1_w, m_ln1_b, m_w_ff1, m_w_ff2, m_ln2_w, m_ln2_b, v_w_ada, v_b_ada, v_w_in, v_ret_norm_w, v_gla_gate_w, v_gla_gate_b, v_gla_norm_w, v_w_out, v_ln1_w, v_ln1_b, v_w_ff1, v_w_ff2, v_ln2_w, v_ln2_b):
    t = x.shape[1]
    xi, yi, ci = _my_coords()
    me = 4 * xi + 2 * yi + ci
    x2d = x[0]
    tgt = loss_target[0]

    c_ext = jnp.concatenate([c, gla_gate_w[0].reshape(1, GATE_RANK * GLA_KW // N_DEV)], axis=1)
    b_l = lax.dynamic_slice(b_ada, (0, me * ADA_COLS), (1, ADA_COLS))
    c_all3, mod_all, wi_g, ada_token = _adaln_mod(c_ext, w_ada[0], b_l, w_in[0].T.astype(BF16))

    wg = _exchange_start([(w_out[0] + ada_token[0, 0]).astype(BF16), w_ff1[0].astype(BF16), w_ff2[0].astype(BF16)],
                         True, "wgather_start")

    c_all = c_all3[:, 0, :D_MODEL]
    gate_w = c_all3[:, 0, D_MODEL:].reshape(N_DEV, GATE_RANK, GLA_KW // N_DEV)
    gate_w = gate_w.transpose(1, 0, 2).reshape(GATE_RANK, GLA_KW)
    gw_pad = jnp.zeros((128, GLA_KW), F32).at[:GATE_RANK].set(gate_w)
    mod = lax.dynamic_slice(mod_all, (0, me, 0), (N_DEV, 1, ADA_COLS)).reshape(6, D_MODEL)
    shift1, scale1, gate1, shift2, scale2, gate2 = [mod[i:i + 1] for i in range(6)]

    w_in_t = wi_g.reshape(D_IN, D_MODEL)

    tables = _ret_tables(t, min(MIX_TILE, t))

    sc1p = 1.0 + scale1
    proj, u = _inproj_fwd(x2d, sc1p, shift1 + wg[4][0, 0], w_in_t)
    mixed, oraw, qrb, krb, rst, sst = _mixer_fwd(proj, tables, gw_pad, gla_gate_b, ret_norm_w, gla_norm_w)
    wo_g, w1_b, w2_g = _exchange_wait(*wg[:4], mixed, True, "wgather_wait")
    w_out_b = wo_g.reshape(D_MODEL, D_MODEL)
    w2_b = w2_g.reshape(D_FF, D_MODEL)
    vec_f = jnp.concatenate([gate1, 1.0 + scale2, shift2, gate2, ln1_w, ln1_b, ln2_w, ln2_b], axis=0)
    m, x1n, rstd1, u2, a, df, dh2, acc_f = _mid_fwd(mixed, x2d, tgt, vec_f, w_out_b, w1_b, w2_b)

    vec_b = jnp.concatenate([gate1, 1.0 + scale2, ln1_w, ln1_b, jnp.zeros((4, D_MODEL), F32)], axis=0)
    da, dm, dmix, dxa, acc_b = _ffn_bwd(df, a, dh2, x1n, rstd1, m, vec_b, w_out_b, w1_b, w2_b)
    dw2 = _matmul_tn(a, df, 2048, 1024, 2048, "tn_dw2", relu_sq=True)
    dw1 = _matmul_tn(u2, da, 1024, 2048, 2048, "tn_dw1", col_slab=FF_COLS)
    dwo = _matmul_tn(mixed, dm, 1024, 1024, 2048, "tn_dwout")
    gx = _exchange_start([dwo.reshape(N_DEV, OUT_ROWS, D_MODEL), dw1, dw2.reshape(N_DEV, FF_COLS, D_MODEL)], False,
                         "gradx_start")
    dproj, dgw, dvec = _mixer_bwd(dmix, proj, qrb, krb, oraw, tables, rst, sst, gw_pad,
                                  gla_gate_b + gx[4][0, 0], ret_norm_w, gla_norm_w)
    dwi_s = _matmul_tn(dproj, u, D_IN_PAD, 1024, 1024, "tn_dwin", out_rows=IN_COLS)
    gi = _exchange_start([dwi_s], False, "gradin_start")
    grad_x, acc_i = _inproj_bwd(dproj, x2d, dxa, sc1p + gi[4][0, 0], w_in_t)

    loss_part = jnp.sum(acc_f[3])
    small = jnp.concatenate([
        acc_i[1:2], acc_i[0:1], acc_b[4:5], acc_b[1:2], acc_b[0:1], acc_f[2:3],
        acc_b[2:3], acc_b[3:4], acc_f[0:1], acc_f[1:2],
        jnp.concatenate([dvec[0:1], dvec[1:2]], axis=1),
        jnp.concatenate([dvec[2:3, :GLA_KW], jnp.full((1, 128), loss_part, F32),
                         jnp.zeros((1, D_MODEL - GLA_KW - 128), F32)], axis=1),
        jnp.zeros((4, D_MODEL), F32)], axis=0)
    sg = _exchange_start([small, dgw[:GATE_RANK]], True, "small_start")

    r_wo, r_w1, r_w2 = _exchange_wait(*gx[:4], sg[4], False, "gradx_wait")
    r_wi, = _exchange_wait(*gi[:4], sg[4], False, "gradin_wait")
    big = [_adamw(w[0], r, m_[0], v_[0], nm) for w, r, m_, v_, nm in (
        (w_out, r_wo, m_w_out, v_w_out, "adamw_out"),
        (w_ff1, r_w1, m_w_ff1, v_w_ff1, "adamw_ff1"), (w_ff2, r_w2, m_w_ff2, v_w_ff2, "adamw_ff2"))]
    big_in = _adamw(w_in[0].T, r_wi, m_w_in[0].T, v_w_in[0].T, "adamw_in")
    big = [tuple(b.T for b in big_in)] + big
    g_big, d_big, m_big, v_big = [[b[i][None] for b in big] for i in range(4)]

    small_all, gw_all = _exchange_wait(*sg[:4], big_in[1], True, "small_wait")
    dmod_all = small_all[:, :6].reshape(N_DEV, 6 * D_MODEL)
    dmod_cols = lax.dynamic_slice(dmod_all, (0, me * ADA_COLS), (N_DEV, ADA_COLS))
    ssum, gw_sum, g_b_ada, g_w_ada = _small_reduce(small_all, gw_all, c_all, dmod_cols)
    loss = ssum[SMR_MISC, GLA_KW]
    g_ggw = lax.dynamic_slice(gw_sum, (0, me * (GLA_KW // N_DEV)), (GATE_RANK, GLA_KW // N_DEV))[None]

    small_w = [b_ada, ret_norm_w, gla_gate_b, gla_norm_w, ln1_w, ln1_b, ln2_w, ln2_b, gla_gate_w]
    small_m = [m_b_ada, m_ret_norm_w, m_gla_gate_b, m_gla_norm_w, m_ln1_w, m_ln1_b, m_ln2_w, m_ln2_b, m_gla_gate_w]
    small_v = [v_b_ada, v_ret_norm_w, v_gla_gate_b, v_gla_norm_w, v_ln1_w, v_ln1_b, v_ln2_w, v_ln2_b, v_gla_gate_w]
    res = _adamw_small(ssum, g_b_ada, g_ggw, small_w, small_m, small_v)
    small_g = list(res[:8]) + [g_ggw]
    d_small, m_small, v_small = list(res[8:17]), list(res[17:26]), list(res[26:35])

    _, d_w_ada, nm_w_ada, nv_w_ada = _adamw(w_ada[0], g_w_ada[None], m_w_ada[0], v_w_ada[0], "adamw_ada")

    def ordered(w_ada_v, small_vals, big_vals):
        b_ada_v, rnw_v, ggb_v, gnw_v, l1w_v, l1b_v, l2w_v, l2b_v, ggw_v = small_vals
        wi_v, wo_v, w1_v, w2_v = big_vals
        return [w_ada_v, b_ada_v, wi_v, rnw_v, ggw_v, ggb_v, gnw_v, wo_v, l1w_v, l1b_v, w1_v, w2_v, l2w_v, l2b_v]

    grads = ordered(g_w_ada[None], small_g, g_big)
    deltas = ordered(d_w_ada[None], d_small, d_big)
    new_m = ordered(nm_w_ada[None], m_small, m_big)
    new_v = ordered(nv_w_ada[None], v_small, v_big)
    return (loss, grad_x[None], *grads, *deltas, *new_m, *new_v)
```

```python
import numpy as np
import jax
import jax.numpy as jnp
from jax import lax
from jax.experimental import pallas as pl
from jax.experimental.pallas import tpu as pltpu

F32 = jnp.float32
BF16 = jnp.bfloat16
MESH = pl.DeviceIdType.MESH
HIGHEST = lax.Precision.HIGHEST

N_DEV = 8
D_MODEL = 1024
CHUNK = 64
RET_HEADS = 4
RET_D = 128
GLA_HEADS = 4
GLA_DK = 64
GLA_DV = 128
GLA_KW = GLA_HEADS * GLA_DK
RET_W = RET_HEADS * RET_D
GLA_VW = GLA_HEADS * GLA_DV
V7X_LANES = 128
GATE_RANK = 16
GATE_TAU = 16.0
D_FF = 4096
LN_EPS = 1e-5
ALPHA = (2.0 * 1) ** 0.25
D_IN = 3600
D_IN_PAD = 3712
ADA_COLS = 6 * D_MODEL // N_DEV
IN_COLS = D_IN // N_DEV
FF_COLS = D_FF // N_DEV
OUT_ROWS = D_MODEL // N_DEV

OFF_RQ, OFF_RK, OFF_RV, OFF_RG = 0, RET_W, 2 * RET_W, 3 * RET_W
OFF_GQ = 4 * RET_W
OFF_GK = OFF_GQ + GLA_KW
OFF_GV = OFF_GK + GLA_KW
OFF_GG = OFF_GV + GLA_VW
OFF_GLR = OFF_GG + GLA_VW

ADAM_LR, ADAM_B1, ADAM_B2, ADAM_EPS, ADAM_WD, ADAM_STEP = 0.001, 0.9, 0.999, 1e-08, 0.01, 10

V7X_VMEM_LIMIT = 62 * 1024 * 1024

ROW_TILE = 512
PROJ_TILE = 512
MIX_TILE = 256
GLA_SUB = 128


def _log_gamma(h):
    return float(np.log(np.float32(1.0) - np.float32(2.0) ** np.float32(-5.0 - h)))


def _my_coords():
    return lax.axis_index("x"), lax.axis_index("y"), lax.axis_index("c")


def _flip(v, bit):
    return 1 - v if bit else v


def _peer(k):
    x, y, c = _my_coords()
    px, py, pc = _flip(x, (k >> 2) & 1), _flip(y, (k >> 1) & 1), _flip(c, k & 1)
    return (px, py, pc), 4 * px + 2 * py + pc


def _dot(a, b, dims=(((1,), (0,)), ((), ())), precision=None):
    return lax.dot_general(a, b, dims, precision=precision, preferred_element_type=F32)


NN = (((1,), (0,)), ((), ()))
NT = (((1,), (1,)), ((), ()))
TN = (((0,), (0,)), ((), ()))


def _split_bf16(v, parts):
    out = []
    for _ in range(parts):
        p = v.astype(BF16)
        out.append(p)
        v = v - p.astype(F32)
    return out


def _dot_split(a, b, dims, a_exact=False):
    if a_exact:
        ab = a.astype(BF16)
        return sum(_dot(ab, p, dims) for p in _split_bf16(b, 3))
    a_hi, a_lo = _split_bf16(a, 2)
    b_hi, b_lo = _split_bf16(b, 2)
    return _dot(a_hi, b_hi, dims) + _dot(a_hi, b_lo, dims) + _dot(a_lo, b_hi, dims)


def _sigmoid(x):
    return 1.0 / (1.0 + jnp.exp(-x))


def _ln_stats(x):
    mu = jnp.mean(x, axis=-1, keepdims=True)
    xc = x - mu
    var = jnp.mean(xc * xc, axis=-1, keepdims=True)
    rstd = lax.rsqrt(var + LN_EPS)
    return xc * rstd, rstd


def _ln_bwd(dyh, xh, rstd):
    return rstd * (dyh - jnp.mean(dyh, axis=-1, keepdims=True) - xh * jnp.mean(dyh * xh, axis=-1, keepdims=True))


def _adaln_mod(c_ext, w_ada_l, b_l, w_in_l):
    width = c_ext.shape[1]

    def body(c_ref, w_ref, b_ref, wi_ref, call_ref, mod_ref, wig_ref, token_ref, s1, r1, s2, r2, gs, gr, gl):
        gather = _TwoLevelGather([wi_ref], [wig_ref], gs, gr, gl)
        gather.start()
        token_ref[...] = jnp.zeros_like(token_ref)
        x, y, c = _my_coords()
        me = 4 * x + 2 * y + c
        call_ref[me] = c_ref[...]
        sends = []
        for k in range(1, N_DEV):
            peer, _ = _peer(k)
            cp = pltpu.make_async_remote_copy(c_ref, call_ref.at[me], s1.at[k - 1], r1.at[k - 1],
                                              device_id=peer, device_id_type=MESH)
            cp.start()
            sends.append(cp)
        for k in range(1, N_DEV):
            peer, pid = _peer(k)
            pltpu.make_async_remote_copy(c_ref, call_ref.at[pid], s1.at[k - 1], r1.at[k - 1],
                                         device_id=peer, device_id_type=MESH).wait_recv()
        for cp in sends:
            cp.wait_send()
        row = lax.broadcasted_iota(jnp.int32, (N_DEV, D_MODEL), 0)
        call = jnp.zeros((N_DEV, D_MODEL), F32)
        for j in range(N_DEV):
            call = jnp.where(row == j, jnp.broadcast_to(call_ref[j][:, :D_MODEL], (N_DEV, D_MODEL)), call)
        sc = call * _sigmoid(call)
        mod = _dot(sc, w_ref[...], NN, HIGHEST) + b_ref[...]
        mod_ref[me] = mod
        sends = []
        for k in range(1, N_DEV):
            peer, _ = _peer(k)
            cp = pltpu.make_async_remote_copy(mod_ref.at[me], mod_ref.at[me], s2.at[k - 1], r2.at[k - 1],
                                              device_id=peer, device_id_type=MESH)
            cp.start()
            sends.append(cp)
        for k in range(1, N_DEV):
            peer, pid = _peer(k)
            pltpu.make_async_remote_copy(mod_ref.at[pid], mod_ref.at[pid], s2.at[k - 1], r2.at[k - 1],
                                         device_id=peer, device_id_type=MESH).wait_recv()
        for cp in sends:
            cp.wait_send()
        gather.forward()
        gather.finish()

    vm = pl.BlockSpec(memory_space=pltpu.VMEM)
    hbm = pl.BlockSpec(memory_space=pl.ANY)
    return pl.pallas_call(
        body, name="adaln_mod",
        out_shape=(jax.ShapeDtypeStruct((N_DEV, 1, width), F32),
                   jax.ShapeDtypeStruct((N_DEV, N_DEV, ADA_COLS), F32),
                   jax.ShapeDtypeStruct((N_DEV, *w_in_l.shape), w_in_l.dtype),
                   jax.ShapeDtypeStruct((8, 128), F32)),
        in_specs=[vm, vm, vm, hbm], out_specs=(vm, vm, hbm, vm),
        scratch_shapes=[pltpu.SemaphoreType.DMA((N_DEV - 1,))] * 4
        + [pltpu.SemaphoreType.DMA((7,)), pltpu.SemaphoreType.DMA((7,)), pltpu.SemaphoreType.DMA((1,))],
        compiler_params=pltpu.CompilerParams(vmem_limit_bytes=V7X_VMEM_LIMIT),
    )(c_ext, w_ada_l, b_l, w_in_l)


class _TwoLevelGather:
    def __init__(self, x_refs, out_refs, send_sems, recv_sems, local_sems):
        self.x_refs, self.out_refs = x_refs, out_refs
        self.send_sems, self.recv_sems, self.local_sems = send_sems, recv_sems, local_sems
        x, y, c = _my_coords()
        self.c = c
        self.me, self.sibling = (x, y, c), (x, y, 1 - c)
        self.chips = [(1 - x, y), (x, 1 - y), (1 - x, 1 - y)]

    def _copy(self, a, k, block, to, src=None):
        px, py, pc = block
        slab = self.out_refs[a].at[4 * px + 2 * py + pc]
        return pltpu.make_async_remote_copy(
            src_ref=slab if src is None else src, dst_ref=slab,
            send_sem=self.send_sems.at[7 * a + k], recv_sem=self.recv_sems.at[7 * a + k],
            device_id=to, device_id_type=MESH)

    def _mine(self, a):
        px, py, pc = self.me
        return pltpu.make_async_copy(self.x_refs[a], self.out_refs[a].at[4 * px + 2 * py + pc], self.local_sems.at[a])

    def _first(self, a):
        cps = [self._copy(a, 0, self.me, self.sibling, src=self.x_refs[a])]
        cps += [self._copy(a, 1 + j, self.me, (*chip, self.c), src=self.x_refs[a]) for j, chip in enumerate(self.chips)]
        return cps

    def _passed(self, a):
        return [self._copy(a, 4 + j, (*chip, self.c), self.sibling) for j, chip in enumerate(self.chips)]

    def start(self):
        for a in range(len(self.x_refs)):
            self._mine(a).start()
            for cp in self._first(a):
                cp.start()

    def forward(self):
        for a in range(len(self.x_refs)):
            passed = self._passed(a)
            for j, chip in enumerate(self.chips):
                self._copy(a, 1 + j, (*chip, self.c), self.me).wait_recv()
                passed[j].start()

    def finish(self):
        for a in range(len(self.x_refs)):
            self._copy(a, 0, self.sibling, self.me).wait_recv()
            for j, chip in enumerate(self.chips):
                self._copy(a, 4 + j, (*chip, 1 - self.c), self.me).wait_recv()
            for cp in self._first(a) + self._passed(a):
                cp.wait_send()
            self._mine(a).wait()


def _exchange_copy(src_refs, land_refs, send_sems, recv_sems, a, k, gather, receiving):
    x, y, c = _my_coords()
    me = 4 * x + 2 * y + c
    peer, pid = _peer(k)
    src = src_refs[a] if gather else src_refs[a].at[pid]
    dst = land_refs[a].at[pid if receiving else me]
    return pltpu.make_async_remote_copy(src, dst, send_sems.at[7 * a + k - 1], recv_sems.at[7 * a + k - 1],
                                        device_id=peer, device_id_type=MESH)


def _exchange_start(srcs, gather, name):
    n = len(srcs)
    xi, yi, ci = _my_coords()
    me = 4 * xi + 2 * yi + ci
    lands = []
    for s in srcs:
        own = s[None] if gather else lax.dynamic_slice_in_dim(s, me, 1, axis=0)
        lands.append(lax.dynamic_update_slice_in_dim(lax.empty((N_DEV, *own.shape[1:]), s.dtype), own, me, axis=0))

    def body(*refs):
        src_refs, land_refs, send_sems, recv_sems, token = refs[:n], refs[n:2 * n], refs[2 * n], refs[2 * n + 1], refs[-1]
        for a in range(n):
            for k in range(1, N_DEV):
                _exchange_copy(src_refs, land_refs, send_sems, recv_sems, a, k, gather, receiving=False).start()
        token[...] = jnp.zeros_like(token)

    hbm = pl.BlockSpec(memory_space=pltpu.HBM)
    sem = pl.BlockSpec(memory_space=pltpu.SEMAPHORE)
    res = pl.pallas_call(
        body, name=name,
        out_shape=(pltpu.SemaphoreType.DMA((7 * n,)), pltpu.SemaphoreType.DMA((7 * n,)),
                   *[pltpu.HBM(v.shape, v.dtype) for v in srcs + lands], jax.ShapeDtypeStruct((8, 128), F32)),
        in_specs=[hbm] * (2 * n),
        out_specs=(sem, sem, *([hbm] * (2 * n)), pl.BlockSpec(memory_space=pltpu.VMEM)),
        input_output_aliases={i: 2 + i for i in range(2 * n)},
        compiler_params=pltpu.CompilerParams(has_side_effects=pltpu.SideEffectType.DATAFLOW_SIDE_EFFECTING),
    )(*[pltpu.with_memory_space_constraint(v, pltpu.HBM) for v in srcs + lands])
    return res[0], res[1], list(res[2:2 + n]), list(res[2 + n:2 + 2 * n]), res[-1]


def _exchange_wait(send_sems, recv_sems, srcs, lands, after, gather, name):
    n = len(srcs)

    def body(*refs):
        src_refs, land_refs, s_sems, r_sems = refs[:n], refs[n:2 * n], refs[2 * n], refs[2 * n + 1]
        for a in range(n):
            for k in range(1, N_DEV):
                _exchange_copy(src_refs, land_refs, s_sems, r_sems, a, k, gather, receiving=False).wait_send()
                _exchange_copy(src_refs, land_refs, s_sems, r_sems, a, k, gather, receiving=True).wait_recv()

    hbm = pl.BlockSpec(memory_space=pltpu.HBM)
    sem = pl.BlockSpec(memory_space=pltpu.SEMAPHORE)
    res = pl.pallas_call(
        body, name=name,
        out_shape=tuple(pltpu.HBM(v.shape, v.dtype) for v in srcs + lands),
        in_specs=[hbm] * (2 * n) + [sem, sem, pl.BlockSpec(memory_space=pl.ANY)],
        out_specs=tuple([hbm] * (2 * n)),
        input_output_aliases={i: i for i in range(2 * n)},
        compiler_params=pltpu.CompilerParams(has_side_effects=pltpu.SideEffectType.DATAFLOW_SIDE_EFFECTING),
    )(*srcs, *lands, send_sems, recv_sems, after)
    return list(res[n:])


def _load_resident(step_is_first, pairs, sem):
    @pl.when(step_is_first)
    def _():
        copies = [pltpu.make_async_copy(src, dst, sem.at[i]) for i, (src, dst) in enumerate(pairs)]
        for cp in copies:
            cp.start()
        for cp in copies:
            cp.wait()


def _load_w_in_t(step_is_first, w_hbm, w_vmem, sem):
    @pl.when(step_is_first)
    def _():
        w_vmem[D_IN:, :] = jnp.zeros((D_IN_PAD - D_IN, D_MODEL), BF16)
    _load_resident(step_is_first, [(w_hbm, w_vmem.at[pl.ds(0, D_IN)])], sem)


def _inproj_fwd(x2d, sc1p, sh1, w_in_t):
    t = x2d.shape[0]
    tm = min(PROJ_TILE, t)

    def body(x_ref, sc_ref, sh_ref, w_hbm, proj_ref, u_ref, w_vmem, sem):
        _load_w_in_t(pl.program_id(0) == 0, w_hbm, w_vmem, sem)
        xh, _ = _ln_stats(x_ref[...])
        ub = (xh * sc_ref[...] + sh_ref[...]).astype(BF16)
        u_ref[...] = ub
        proj_ref[...] = _dot(ub, w_vmem[...], NT)

    row = lambda i: (i, 0)
    fix = lambda i: (0, 0)
    return pl.pallas_call(
        body, name="inproj_fwd", grid=(t // tm,),
        in_specs=[pl.BlockSpec((tm, D_MODEL), row), pl.BlockSpec((1, D_MODEL), fix), pl.BlockSpec((1, D_MODEL), fix),
                  pl.BlockSpec(memory_space=pl.ANY)],
        out_specs=(pl.BlockSpec((tm, D_IN_PAD), row), pl.BlockSpec((tm, D_MODEL), row)),
        out_shape=(jax.ShapeDtypeStruct((t, D_IN_PAD), F32), jax.ShapeDtypeStruct((t, D_MODEL), BF16)),
        scratch_shapes=[pltpu.VMEM((D_IN_PAD, D_MODEL), BF16), pltpu.SemaphoreType.DMA((1,))],
        compiler_params=pltpu.CompilerParams(dimension_semantics=("arbitrary",), vmem_limit_bytes=V7X_VMEM_LIMIT),
    )(x2d, sc1p, sh1, w_in_t)


CHUNK_SHIFT = CHUNK.bit_length() - 1


def _ret_tables(t, tl):
    r = lax.broadcasted_iota(jnp.int32, (tl, tl), 0)
    c = lax.broadcasted_iota(jnp.int32, (tl, tl), 1)
    allowed = jnp.right_shift(c, CHUNK_SHIFT) <= jnp.right_shift(r, CHUNK_SHIFT)
    dist = jnp.abs(r - c).astype(F32)
    rowf = lax.broadcasted_iota(jnp.int32, (tl, RET_D), 0).astype(F32)
    lgs = [_log_gamma(h) for h in range(RET_HEADS)]
    dec = jnp.stack([jnp.where(allowed, jnp.exp(lg * dist), 0.0) for lg in lgs])
    qkd = jnp.stack([jnp.exp(lg * (rowf + 1.0)) for lg in lgs] + [jnp.exp(lg * (tl - 1.0 - rowf)) for lg in lgs])
    inv = 1.0 / (10000.0 ** jnp.linspace(0.0, 1.0, RET_D // 2, dtype=F32))
    off = jnp.arange(tl, dtype=F32)[:, None] * inv[None, :]
    start = (jnp.arange(t // tl, dtype=F32) * tl)[:, None] * inv[None, :]
    co, so = jnp.cos(off), jnp.sin(off)
    rot_in = jnp.stack([jnp.concatenate([co, co], 1), jnp.concatenate([so, so], 1),
                        jnp.concatenate([-co, co], 1), jnp.concatenate([-so, so], 1)])
    cs, ss = jnp.cos(start), jnp.sin(start)
    rot_tile = jnp.concatenate([cs, cs, ss, ss], axis=1)
    rot_tile = jnp.broadcast_to(rot_tile[:, None, :], (t // tl, 8, 2 * RET_D))
    return dec, qkd, rot_in, rot_tile


def _tile_gammas(tl):
    return [float(np.exp(np.float32(_log_gamma(h)) * np.float32(tl))) for h in range(RET_HEADS)]


def _tile_rotary(rot_in_ref, rot_tile_ref):
    ca, sa = rot_tile_ref[0, 0:1, 0:RET_D], rot_tile_ref[0, 0:1, RET_D:2 * RET_D]
    cosv = ca * rot_in_ref[0] - sa * rot_in_ref[1]
    sinv = sa * rot_in_ref[2] + ca * rot_in_ref[3]
    return cosv, sinv


def _gla_consts(tl):
    r = lax.broadcasted_iota(jnp.int32, (tl, tl), 0)
    c = lax.broadcasted_iota(jnp.int32, (tl, tl), 1)
    ltri = (c <= r).astype(F32)
    utri = (c >= r).astype(F32)
    lane = lax.broadcasted_iota(jnp.int32, (1, GLA_KW), 1)
    hmask = [((lane >= h * GLA_DK) & (lane < (h + 1) * GLA_DK)).astype(F32) for h in range(GLA_HEADS)]
    rs = lax.broadcasted_iota(jnp.int32, (GLA_HEADS * tl, tl), 0) & (tl - 1)
    cs = lax.broadcasted_iota(jnp.int32, (GLA_HEADS * tl, tl), 1)
    lower = cs <= rs
    same = jnp.right_shift(cs, CHUNK_SHIFT) == jnp.right_shift(rs, CHUNK_SHIFT)
    upper = jnp.logical_and(jnp.logical_not(lower), same)
    return dict(ltri=ltri, utri=utri, hmask=hmask, lower=lower, upper=upper)


def _tile_rows(j, tl):
    return pl.ds(j * tl, tl) if isinstance(j, int) else pl.ds(pl.multiple_of(j * tl, tl), tl)


def _for_tiles(cps, fn):
    for j in range(cps):
        fn(j, 0)


def _rotate(v, cosv, sinv):
    return v * cosv + pltpu.roll(v, RET_D // 2, 1) * sinv


def _rotate_t(d, cosv, sinv):
    return d * cosv + pltpu.roll(d * sinv, RET_D // 2, 1)


def _stack_heads(v, hmask):
    return jnp.concatenate([v * hmask[h] for h in range(GLA_HEADS)], axis=0)


def _gla_gates(glr, gw, gb, ltri, tl):
    z = _dot_split(glr, gw, NN) + gb
    la = (jnp.minimum(z, 0.0) - jnp.log(1.0 + jnp.exp(-jnp.abs(z)))) * (1.0 / GATE_TAU)
    b = _dot_split(ltri, la, NN, a_exact=True)
    level = b[tl // 2 - 1:tl // 2, :]
    ep = jnp.exp(jnp.clip(b - level, -80.0, 80.0))
    em = jnp.exp(jnp.clip(level - b, -80.0, 80.0))
    bl = b[tl - 1:tl, :]
    return z, b, bl, ep, em


def _mixer_fwd(proj, tables, gw_pad, gb, rnw, gnw):
    t = proj.shape[0]
    tc = min(MIX_TILE, t)
    tr, tg = tc, min(GLA_SUB, tc)
    nsteps = t // tc
    scale_r = RET_D ** -0.5
    scale_g = GLA_DK ** -0.5
    gammas = _tile_gammas(tr)

    def body(rq_ref, rk_ref, rv_ref, rg_ref, gq_ref, gk_ref, gv_ref, gg_ref, glr_ref,
             dec_ref, qkd_ref, rot_in_ref, rot_tile_ref, gw_ref, gb_ref, rnw_ref, gnw_ref,
             mix_ref, oraw_ref, qrb_ref, krb_ref, rst_ref, sst_ref, r_scr, s_scr):
        @pl.when(pl.program_id(0) == 0)
        def _():
            r_scr[...] = jnp.zeros_like(r_scr)
            s_scr[...] = jnp.zeros_like(s_scr)

        gla_k = _gla_consts(tg)

        def ret_tile(j, carry):
            rows = _tile_rows(j, tr)
            cosv, sinv = _tile_rotary(rot_in_ref, rot_tile_ref)
            for h in range(RET_HEADS):
                cols = slice(h * RET_D, (h + 1) * RET_D)
                qr = _rotate(rq_ref[rows, cols], cosv, sinv) * scale_r
                kr = _rotate(rk_ref[rows, cols], cosv, sinv)
                vb = rv_ref[rows, cols].astype(BF16)
                qb, kb = qr.astype(BF16), kr.astype(BF16)
                qrb_ref[rows, cols] = qb
                krb_ref[rows, cols] = kb
                p = _dot(qb, kb, NT) * dec_ref[h]
                rp = r_scr[cols, :]
                o = _dot(p.astype(BF16), vb) + _dot((qr * qkd_ref[h]).astype(BF16), rp.astype(BF16))
                rst_ref[j, cols, :] = rp
                r_scr[cols, :] = gammas[h] * rp + _dot((kr * qkd_ref[RET_HEADS + h]).astype(BF16), vb, TN)
                oraw_ref[rows, cols] = o
                oc = o - jnp.mean(o, axis=-1, keepdims=True)
                n = oc * lax.rsqrt(jnp.mean(oc * oc, axis=-1, keepdims=True) + LN_EPS)
                g = rg_ref[rows, cols]
                mix_ref[rows, cols] = (n * rnw_ref[:, cols] * (g * _sigmoid(g))).astype(BF16)
            return carry

        def gla_tile(j, carry):
            k = gla_k
            tl = tg
            rows = _tile_rows(j, tg)
            _, b, bl, ep, em = _gla_gates(glr_ref[rows, :], gw_ref[...], gb_ref[...], k["ltri"], tl)
            qs = gq_ref[rows, :] * scale_g
            kk = gk_ref[rows, :]
            x_all = _dot(_stack_heads(qs * ep, k["hmask"]).astype(BF16), (kk * em).astype(BF16), NT)
            y_all = _dot(_stack_heads(qs * em, k["hmask"]).astype(BF16), (kk * ep).astype(BF16), NT)
            a_all = jnp.where(k["lower"], x_all, jnp.where(k["upper"], y_all, 0.0)).astype(BF16)
            st = s_scr[...]
            oq = _dot(_stack_heads(qs * jnp.exp(b), k["hmask"]).astype(BF16), st.astype(BF16), NT)
            kg = kk * jnp.exp(bl - b)
            sst_ref[j] = st
            st_new = st * jnp.exp(bl)
            for h in range(GLA_HEADS):
                cols = slice(h * GLA_DV, (h + 1) * GLA_DV)
                hr = slice(h * tl, (h + 1) * tl)
                vb = gv_ref[rows, cols].astype(BF16)
                o = _dot(a_all[hr, :], vb) + oq[hr, :]
                st_new = st_new + _dot(vb, (kg * k["hmask"][h]).astype(BF16), TN)
                ocols = slice(RET_W + h * GLA_DV, RET_W + (h + 1) * GLA_DV)
                oraw_ref[rows, ocols] = o
                n = o * lax.rsqrt(jnp.mean(o * o, axis=-1, keepdims=True) + LN_EPS)
                g = gg_ref[rows, cols]
                mix_ref[rows, ocols] = (n * gnw_ref[:, cols] * (g * _sigmoid(g))).astype(BF16)
            s_scr[...] = st_new
            return carry

        _for_tiles(tc // tr, ret_tile)
        _for_tiles(tc // tg, gla_tile)

    def col(width, off):
        return pl.BlockSpec((tc, width), lambda i, o=off // width: (i, o))

    fix = lambda i: (0, 0)
    fix3 = lambda i: (0, 0, 0)
    dec, qkd, rot_in, rot_tile = tables
    in_specs = [col(RET_W, OFF_RQ), col(RET_W, OFF_RK), col(RET_W, OFF_RV), col(RET_W, OFF_RG),
                col(GLA_KW, OFF_GQ), col(GLA_KW, OFF_GK), col(GLA_VW, OFF_GV), col(GLA_VW, OFF_GG),
                col(V7X_LANES, OFF_GLR),
                pl.BlockSpec(dec.shape, fix3), pl.BlockSpec(qkd.shape, fix3), pl.BlockSpec(rot_in.shape, fix3),
                pl.BlockSpec((1, 8, 2 * RET_D), lambda i: (i, 0, 0)),
                pl.BlockSpec((V7X_LANES, GLA_KW), fix), pl.BlockSpec((1, GLA_KW), fix),
                pl.BlockSpec((1, RET_W), fix), pl.BlockSpec((1, GLA_VW), fix)]
    half = pl.BlockSpec((tc, RET_W), lambda i: (i, 0))
    out_specs = (pl.BlockSpec((tc, D_MODEL), lambda i: (i, 0)), pl.BlockSpec((tc, D_MODEL), lambda i: (i, 0)),
                 half, half,
                 pl.BlockSpec((tc // tr, RET_W, RET_D), lambda i: (i, 0, 0)),
                 pl.BlockSpec((tc // tg, GLA_DV, GLA_KW), lambda i: (i, 0, 0)))
    out_shape = (jax.ShapeDtypeStruct((t, D_MODEL), BF16), jax.ShapeDtypeStruct((t, D_MODEL), F32),
                 jax.ShapeDtypeStruct((t, RET_W), BF16), jax.ShapeDtypeStruct((t, RET_W), BF16),
                 jax.ShapeDtypeStruct((t // tr, RET_W, RET_D), F32),
                 jax.ShapeDtypeStruct((t // tg, GLA_DV, GLA_KW), F32))
    return pl.pallas_call(
        body, name="mixer_fwd", grid=(nsteps,), in_specs=in_specs, out_specs=out_specs, out_shape=out_shape,
        scratch_shapes=[pltpu.VMEM((RET_W, RET_D), F32), pltpu.VMEM((GLA_DV, GLA_KW), F32)],
        compiler_params=pltpu.CompilerParams(dimension_semantics=("arbitrary",), vmem_limit_bytes=V7X_VMEM_LIMIT),
    )(*([proj] * 9), dec, qkd, rot_in, rot_tile, gw_pad, gb, rnw, gnw)


def _mid_fwd(mixed, x2d, target, vecs, w_out_b, w1_b, w2_b):
    t = x2d.shape[0]
    tm = min(ROW_TILE, t)

    def body(mix_ref, x_ref, tgt_ref, v_ref, wo_hbm, w1_hbm, w2_hbm,
             m_ref, x1n_ref, rstd_ref, u2_ref, a_ref, df_ref, dh2_ref, acc_ref, wo, w1, w2, sem):
        first = pl.program_id(0) == 0
        _load_resident(first, [(wo_hbm, wo), (w1_hbm, w1), (w2_hbm, w2)], sem)

        @pl.when(first)
        def _():
            acc_ref[...] = jnp.zeros_like(acc_ref)

        gate1, sc2p, sh2, gate2 = v_ref[0:1, :], v_ref[1:2, :], v_ref[2:3, :], v_ref[3:4, :]
        l1w, l1b, l2w, l2b = v_ref[4:5, :], v_ref[5:6, :], v_ref[6:7, :], v_ref[7:8, :]
        m = _dot(mix_ref[...], wo[...])
        m_ref[...] = m.astype(BF16)
        x1n, rstd1 = _ln_stats(ALPHA * x_ref[...] + gate1 * m)
        x1n_ref[...] = x1n
        rstd_ref[...] = rstd1
        x1 = x1n * l1w + l1b
        xh1, _ = _ln_stats(x1)
        u2 = (xh1 * sc2p + sh2).astype(BF16)
        u2_ref[...] = u2
        f = jnp.zeros((tm, D_MODEL), F32)
        for j in range(N_DEV):
            cols = slice(j * FF_COLS, (j + 1) * FF_COLS)
            a = _dot(u2, w1[j])
            a_ref[:, cols] = a.astype(BF16)
            r = jnp.maximum(a, 0.0)
            f = f + _dot((r * r).astype(BF16), w2[cols, :])
        yh, rstd2 = _ln_stats(ALPHA * x1 + gate2 * f)
        e = yh * l2w + l2b - tgt_ref[...]
        dy = e * (1.0 / D_MODEL)
        dh2 = _ln_bwd(dy * l2w, yh, rstd2)
        dh2_ref[...] = dh2
        df_ref[...] = (dh2 * gate2).astype(BF16)
        acc_ref[0:1, :] += jnp.sum(dy * yh, axis=0, keepdims=True)
        acc_ref[1:2, :] += jnp.sum(dy, axis=0, keepdims=True)
        acc_ref[2:3, :] += jnp.sum(dh2 * f, axis=0, keepdims=True)
        acc_ref[3:4, :] += jnp.sum(e * e, axis=0, keepdims=True) * (0.5 / D_MODEL)

    row = lambda i: (i, 0)
    fix = lambda i: (0, 0)
    hbm = pl.BlockSpec(memory_space=pl.ANY)
    return pl.pallas_call(
        body, name="mid_fwd", grid=(t // tm,),
        in_specs=[pl.BlockSpec((tm, D_MODEL), row), pl.BlockSpec((tm, D_MODEL), row), pl.BlockSpec((tm, D_MODEL), row),
                  pl.BlockSpec((8, D_MODEL), fix), hbm, hbm, hbm],
        out_specs=(pl.BlockSpec((tm, D_MODEL), row), pl.BlockSpec((tm, D_MODEL), row), pl.BlockSpec((tm, 1), row),
                   pl.BlockSpec((tm, D_MODEL), row), pl.BlockSpec((tm, D_FF), row), pl.BlockSpec((tm, D_MODEL), row),
                   pl.BlockSpec((tm, D_MODEL), row), pl.BlockSpec((8, D_MODEL), fix)),
        out_shape=(jax.ShapeDtypeStruct((t, D_MODEL), BF16), jax.ShapeDtypeStruct((t, D_MODEL), F32),
                   jax.ShapeDtypeStruct((t, 1), F32), jax.ShapeDtypeStruct((t, D_MODEL), BF16),
                   jax.ShapeDtypeStruct((t, D_FF), BF16), jax.ShapeDtypeStruct((t, D_MODEL), BF16),
                   jax.ShapeDtypeStruct((t, D_MODEL), F32), jax.ShapeDtypeStruct((8, D_MODEL), F32)),
        scratch_shapes=[pltpu.VMEM((D_MODEL, D_MODEL), BF16), pltpu.VMEM((N_DEV, D_MODEL, FF_COLS), BF16),
                        pltpu.VMEM((D_FF, D_MODEL), BF16), pltpu.SemaphoreType.DMA((3,))],
        compiler_params=pltpu.CompilerParams(dimension_semantics=("arbitrary",), vmem_limit_bytes=V7X_VMEM_LIMIT),
    )(mixed, x2d, target, vecs, w_out_b, w1_b, w2_b)


def _ffn_bwd(df, a, dh2, x1n, rstd1, m, vecs, w_out_b, w1_b, w2_b):
    t = x1n.shape[0]
    tm = min(ROW_TILE, t)

    def body(df_ref, a_ref, dh2_ref, x1n_ref, rstd_ref, m_ref, v_ref, wo_hbm, w1_hbm, w2_hbm,
             da_ref, dm_ref, dmix_ref, dxa_ref, acc_ref, wo, w1, w2, sem):
        first = pl.program_id(0) == 0
        _load_resident(first, [(wo_hbm, wo), (w1_hbm, w1), (w2_hbm, w2)], sem)

        @pl.when(first)
        def _():
            acc_ref[...] = jnp.zeros_like(acc_ref)

        gate1, sc2p, l1w, l1b = v_ref[0:1, :], v_ref[1:2, :], v_ref[2:3, :], v_ref[3:4, :]
        df = df_ref[...]
        du2 = jnp.zeros((tm, D_MODEL), F32)
        for j in range(N_DEV):
            cols = slice(j * FF_COLS, (j + 1) * FF_COLS)
            dr2 = _dot(df, w2[cols, :], NT)
            da = (dr2 * (2.0 * jnp.maximum(a_ref[:, cols].astype(F32), 0.0))).astype(BF16)
            da_ref[:, cols] = da
            du2 = du2 + _dot(da, w1[j], NT)
        x1n = x1n_ref[...]
        xh1, rstd0 = _ln_stats(x1n * l1w + l1b)
        dx1 = ALPHA * dh2_ref[...] + _ln_bwd(du2 * sc2p, xh1, rstd0)
        dh1 = _ln_bwd(dx1 * l1w, x1n, rstd_ref[...])
        dxa_ref[...] = ALPHA * dh1
        dm = (dh1 * gate1).astype(BF16)
        dm_ref[...] = dm
        dmix_ref[...] = _dot(dm, wo[...], NT)
        acc_ref[0:1, :] += jnp.sum(du2 * xh1, axis=0, keepdims=True)
        acc_ref[1:2, :] += jnp.sum(du2, axis=0, keepdims=True)
        acc_ref[2:3, :] += jnp.sum(dx1 * x1n, axis=0, keepdims=True)
        acc_ref[3:4, :] += jnp.sum(dx1, axis=0, keepdims=True)
        acc_ref[4:5, :] += jnp.sum(dh1 * m_ref[...].astype(F32), axis=0, keepdims=True)

    row = lambda i: (i, 0)
    fix = lambda i: (0, 0)
    hbm = pl.BlockSpec(memory_space=pl.ANY)
    return pl.pallas_call(
        body, name="ffn_bwd", grid=(t // tm,),
        in_specs=[pl.BlockSpec((tm, D_MODEL), row), pl.BlockSpec((tm, D_FF), row), pl.BlockSpec((tm, D_MODEL), row),
                  pl.BlockSpec((tm, D_MODEL), row), pl.BlockSpec((tm, 1), row), pl.BlockSpec((tm, D_MODEL), row),
                  pl.BlockSpec((8, D_MODEL), fix), hbm, hbm, hbm],
        out_specs=(pl.BlockSpec((tm, D_FF), row), pl.BlockSpec((tm, D_MODEL), row), pl.BlockSpec((tm, D_MODEL), row),
                   pl.BlockSpec((tm, D_MODEL), row), pl.BlockSpec((8, D_MODEL), fix)),
        out_shape=(jax.ShapeDtypeStruct((t, D_FF), BF16), jax.ShapeDtypeStruct((t, D_MODEL), BF16),
                   jax.ShapeDtypeStruct((t, D_MODEL), F32), jax.ShapeDtypeStruct((t, D_MODEL), F32),
                   jax.ShapeDtypeStruct((8, D_MODEL), F32)),
        scratch_shapes=[pltpu.VMEM((D_MODEL, D_MODEL), BF16), pltpu.VMEM((N_DEV, D_MODEL, FF_COLS), BF16),
                        pltpu.VMEM((D_FF, D_MODEL), BF16), pltpu.SemaphoreType.DMA((3,))],
        compiler_params=pltpu.CompilerParams(dimension_semantics=("arbitrary",), vmem_limit_bytes=V7X_VMEM_LIMIT),
    )(df, a, dh2, x1n, rstd1, m, vecs, w_out_b, w1_b, w2_b)


def _matmul_tn(lhs, rhs, tmm, tn, tk, name, relu_sq=False, col_slab=None, out_rows=None):
    t, mm = lhs.shape
    assert out_rows is None or (col_slab is None and tmm == mm)
    nn = rhs.shape[1]
    tk = min(tk, t)
    nk = t // tk

    def body(l_ref, r_ref, o_ref, acc):
        kk = pl.program_id(2)

        @pl.when(kk == 0)
        def _():
            acc[...] = jnp.zeros_like(acc)

        l = l_ref[...]
        if relu_sq:
            lf = jnp.maximum(l.astype(F32), 0.0)
            l = (lf * lf).astype(BF16)
        acc[...] += _dot(l, r_ref[...], TN)

        @pl.when(kk == nk - 1)
        def _():
            if out_rows is not None:
                for s in range(N_DEV):
                    o_ref[s] = acc[s * out_rows:(s + 1) * out_rows, :].astype(o_ref.dtype)
            elif col_slab is None:
                o_ref[...] = acc[...].astype(o_ref.dtype)
            else:
                for s in range(tn // col_slab):
                    o_ref[s] = acc[:, s * col_slab:(s + 1) * col_slab].astype(o_ref.dtype)

    if out_rows is not None:
        out_spec = pl.BlockSpec((N_DEV, out_rows, tn), lambda i, j, k: (0, 0, j))
        out_shape = jax.ShapeDtypeStruct((N_DEV, out_rows, nn), BF16)
    elif col_slab is None:
        out_spec = pl.BlockSpec((tmm, tn), lambda i, j, k: (i, j))
        out_shape = jax.ShapeDtypeStruct((mm, nn), BF16)
    else:
        out_spec = pl.BlockSpec((tn // col_slab, tmm, col_slab), lambda i, j, k: (j, i, 0))
        out_shape = jax.ShapeDtypeStruct((nn // col_slab, mm, col_slab), BF16)
    return pl.pallas_call(
        body, name=name, grid=(mm // tmm, nn // tn, nk),
        in_specs=[pl.BlockSpec((tk, tmm), lambda i, j, k: (k, i)), pl.BlockSpec((tk, tn), lambda i, j, k: (k, j))],
        out_specs=out_spec,
        out_shape=out_shape,
        scratch_shapes=[pltpu.VMEM((tmm, tn), F32)],
        compiler_params=pltpu.CompilerParams(dimension_semantics=("arbitrary", "arbitrary", "arbitrary"),
                                             vmem_limit_bytes=V7X_VMEM_LIMIT),
    )(lhs, rhs)


def _mixer_bwd(dmix, proj, qrb, krb, oraw, tables, rst, sst, gw_pad, gb, rnw, gnw):
    t = proj.shape[0]
    tc = min(MIX_TILE, t)
    tr, tg = tc, min(GLA_SUB, tc)
    nsteps = t // tc
    scale_r = RET_D ** -0.5
    scale_g = GLA_DK ** -0.5
    gammas = _tile_gammas(tr)

    def body(dmix_ref, qrb_ref, krb_ref, rv_ref, rg_ref, gq_ref, gk_ref, gv_ref, gg_ref, glr_ref, oraw_ref,
             dec_ref, qkd_ref, rot_in_ref, rot_tile_ref, rst_ref, sst_ref, gw_ref, gb_ref, rnw_ref, gnw_ref,
             dproj_ref, dgw_ref, dvec_ref, dr_scr, ds_scr):
        @pl.when(pl.program_id(0) == 0)
        def _():
            dr_scr[...] = jnp.zeros_like(dr_scr)
            ds_scr[...] = jnp.zeros_like(ds_scr)
            dgw_ref[...] = jnp.zeros_like(dgw_ref)
            dvec_ref[...] = jnp.zeros_like(dvec_ref)

        gla_k = _gla_consts(tg)
        last_row = lax.broadcasted_iota(jnp.int32, (tg, GLA_KW), 0) == tg - 1

        def ret_tile(jj, carry):
            j = tc // tr - 1 - jj
            rows = _tile_rows(j, tr)
            cosv, sinv = _tile_rotary(rot_in_ref, rot_tile_ref)
            for h in range(RET_HEADS):
                cols = slice(h * RET_D, (h + 1) * RET_D)
                o = oraw_ref[rows, cols]
                g = rg_ref[rows, cols]
                w = rnw_ref[:, cols]
                dout = dmix_ref[rows, cols]
                oc = o - jnp.mean(o, axis=-1, keepdims=True)
                inv = lax.rsqrt(jnp.mean(oc * oc, axis=-1, keepdims=True) + LN_EPS)
                n = oc * inv
                sg = _sigmoid(g)
                sil = g * sg
                dn = dout * w * sil
                dvec_ref[0:1, cols] += jnp.sum(dout * n * sil, axis=0, keepdims=True)
                dproj_ref[rows, OFF_RG + h * RET_D:OFF_RG + (h + 1) * RET_D] = (
                    dout * n * w * (sg * (1.0 + g * (1.0 - sg)))).astype(BF16)
                doc = inv * (dn - n * jnp.mean(dn * n, axis=-1, keepdims=True))
                do = doc - jnp.mean(doc, axis=-1, keepdims=True)

                qb, kb = qrb_ref[rows, cols], krb_ref[rows, cols]
                qr, kr = qb.astype(F32), kb.astype(F32)
                vb = rv_ref[rows, cols].astype(BF16)
                dob = do.astype(BF16)
                qd, kd = qkd_ref[h], qkd_ref[RET_HEADS + h]
                p = _dot(qb, kb, NT) * dec_ref[h]
                rp = rst_ref[j, cols, :].astype(BF16)
                dr = dr_scr[cols, :]
                drb = dr.astype(BF16)
                dpb = (_dot(dob, vb, NT) * dec_ref[h]).astype(BF16)
                dqr = _dot(dpb, kb) + _dot(dob, rp, NT) * qd
                dkr = _dot(dpb, qb, TN) + _dot(vb, drb, NT) * kd
                dv = _dot(p.astype(BF16), dob, TN) + _dot((kr * kd).astype(BF16), drb)
                dr_scr[cols, :] = gammas[h] * dr + _dot((qr * qd).astype(BF16), dob, TN)
                dproj_ref[rows, OFF_RQ + h * RET_D:OFF_RQ + (h + 1) * RET_D] = (
                    _rotate_t(dqr, cosv, sinv) * scale_r).astype(BF16)
                dproj_ref[rows, OFF_RK + h * RET_D:OFF_RK + (h + 1) * RET_D] = _rotate_t(dkr, cosv, sinv).astype(BF16)
                dproj_ref[rows, OFF_RV + h * RET_D:OFF_RV + (h + 1) * RET_D] = dv.astype(BF16)
            return carry

        def gla_tile(jj, carry):
            k = gla_k
            tl = tg
            j = tc // tg - 1 - jj
            rows = _tile_rows(j, tg)
            glr = glr_ref[rows, :]
            z, b, bl, ep, em = _gla_gates(glr, gw_ref[...], gb_ref[...], k["ltri"], tl)
            qs = gq_ref[rows, :] * scale_g
            kk = gk_ref[rows, :]
            eb = jnp.exp(b)
            ekb = jnp.exp(bl - b)
            ebl = jnp.exp(bl)
            ql, qu, kl, ku = qs * ep, qs * em, kk * em, kk * ep
            qg, kg = qs * eb, kk * ekb
            qlm = _stack_heads(ql, k["hmask"]).astype(BF16)
            qum = _stack_heads(qu, k["hmask"]).astype(BF16)
            klb, kub = kl.astype(BF16), ku.astype(BF16)
            a_all = jnp.where(k["lower"], _dot(qlm, klb, NT),
                              jnp.where(k["upper"], _dot(qum, kub, NT), 0.0)).astype(BF16)
            st = sst_ref[j]
            stb = st.astype(BF16)
            ds = ds_scr[...]
            dsb = ds.astype(BF16)
            ds_new = ds * ebl
            da_parts = []
            dqg = jnp.zeros((tl, GLA_KW), F32)
            dkg = jnp.zeros((tl, GLA_KW), F32)
            for h in range(GLA_HEADS):
                cols = slice(h * GLA_DV, (h + 1) * GLA_DV)
                hr = slice(h * tl, (h + 1) * tl)
                ocols = slice(RET_W + h * GLA_DV, RET_W + (h + 1) * GLA_DV)
                o = oraw_ref[rows, ocols]
                g = gg_ref[rows, cols]
                w = gnw_ref[:, cols]
                dout = dmix_ref[rows, ocols]
                inv = lax.rsqrt(jnp.mean(o * o, axis=-1, keepdims=True) + LN_EPS)
                n = o * inv
                sg = _sigmoid(g)
                sil = g * sg
                dn = dout * w * sil
                dvec_ref[1:2, cols] += jnp.sum(dout * n * sil, axis=0, keepdims=True)
                dproj_ref[rows, OFF_GG + h * GLA_DV:OFF_GG + (h + 1) * GLA_DV] = (
                    dout * n * w * (sg * (1.0 + g * (1.0 - sg)))).astype(BF16)
                dob = (inv * (dn - n * jnp.mean(dn * n, axis=-1, keepdims=True))).astype(BF16)
                vb = gv_ref[rows, cols].astype(BF16)
                mh = k["hmask"][h]
                da_parts.append(_dot(dob, vb, NT))
                dv = _dot(a_all[hr, :], dob, TN) + _dot((kg * mh).astype(BF16), dsb, NT)
                dproj_ref[rows, OFF_GV + h * GLA_DV:OFF_GV + (h + 1) * GLA_DV] = dv.astype(BF16)
                dkg = dkg + mh * _dot(vb, dsb)
                dqg = dqg + mh * _dot(dob, stb)
                ds_new = ds_new + _dot(dob, (qg * mh).astype(BF16), TN)
            da_all = jnp.concatenate(da_parts, axis=0)
            dal = jnp.where(k["lower"], da_all, 0.0).astype(BF16)
            dau = jnp.where(k["upper"], da_all, 0.0).astype(BF16)
            dqlm = _dot(dal, klb)
            dqum = _dot(dau, kub)
            dql = jnp.zeros((tl, GLA_KW), F32)
            dqu = jnp.zeros((tl, GLA_KW), F32)
            for h in range(GLA_HEADS):
                hr = slice(h * tl, (h + 1) * tl)
                dql = dql + k["hmask"][h] * dqlm[hr, :]
                dqu = dqu + k["hmask"][h] * dqum[hr, :]
            dkl = _dot(dal, qlm, TN)
            dku = _dot(dau, qum, TN)
            dbl = (jnp.sum(dkg * kg, axis=0, keepdims=True)
                   + jnp.sum(ds * st, axis=0, keepdims=True) * ebl)
            ds_scr[...] = ds_new
            dqs = dql * ep + dqu * em + dqg * eb
            dk = dkl * em + dku * ep + dkg * ekb
            db = dql * ql - dkl * kl - dqu * qu + dku * ku + dqg * qg - dkg * kg
            db = db + jnp.where(last_row, dbl, 0.0)
            dla = _dot_split(k["utri"], db, NN, a_exact=True)
            dz = dla * (1.0 / GATE_TAU) * _sigmoid(-z)
            dvec_ref[2:3, 0:GLA_KW] += jnp.sum(dz, axis=0, keepdims=True)
            dgw_ref[...] += _dot_split(glr, dz, TN)
            dproj_ref[rows, OFF_GLR:D_IN_PAD] = _dot(dz.astype(BF16), gw_ref[...].astype(BF16), NT).astype(BF16)
            dproj_ref[rows, OFF_GQ:OFF_GQ + GLA_KW] = (dqs * scale_g).astype(BF16)
            dproj_ref[rows, OFF_GK:OFF_GK + GLA_KW] = dk.astype(BF16)
            return carry

        _for_tiles(tc // tr, ret_tile)
        _for_tiles(tc // tg, gla_tile)

    rev = lambda i: (nsteps - 1 - i, 0)

    def col(width, off):
        return pl.BlockSpec((tc, width), lambda i, o=off // width: (nsteps - 1 - i, o))

    fix = lambda i: (0, 0)
    fix3 = lambda i: (0, 0, 0)
    dec, qkd, rot_in, rot_tile = tables
    half = pl.BlockSpec((tc, RET_W), rev)
    in_specs = [pl.BlockSpec((tc, D_MODEL), rev), half, half, col(RET_W, OFF_RV), col(RET_W, OFF_RG),
                col(GLA_KW, OFF_GQ), col(GLA_KW, OFF_GK), col(GLA_VW, OFF_GV), col(GLA_VW, OFF_GG),
                col(V7X_LANES, OFF_GLR),
                pl.BlockSpec((tc, D_MODEL), rev),
                pl.BlockSpec(dec.shape, fix3), pl.BlockSpec(qkd.shape, fix3), pl.BlockSpec(rot_in.shape, fix3),
                pl.BlockSpec((1, 8, 2 * RET_D), lambda i: (nsteps - 1 - i, 0, 0)),
                pl.BlockSpec((tc // tr, RET_W, RET_D), lambda i: (nsteps - 1 - i, 0, 0)),
                pl.BlockSpec((tc // tg, GLA_DV, GLA_KW), lambda i: (nsteps - 1 - i, 0, 0)),
                pl.BlockSpec((V7X_LANES, GLA_KW), fix), pl.BlockSpec((1, GLA_KW), fix),
                pl.BlockSpec((1, RET_W), fix), pl.BlockSpec((1, GLA_VW), fix)]
    out_specs = (pl.BlockSpec((tc, D_IN_PAD), rev), pl.BlockSpec((V7X_LANES, GLA_KW), fix),
                 pl.BlockSpec((8, RET_W), fix))
    out_shape = (jax.ShapeDtypeStruct((t, D_IN_PAD), BF16), jax.ShapeDtypeStruct((V7X_LANES, GLA_KW), F32),
                 jax.ShapeDtypeStruct((8, RET_W), F32))
    return pl.pallas_call(
        body, name="mixer_bwd", grid=(nsteps,), in_specs=in_specs, out_specs=out_specs, out_shape=out_shape,
        scratch_shapes=[pltpu.VMEM((RET_W, RET_D), F32), pltpu.VMEM((GLA_DV, GLA_KW), F32)],
        compiler_params=pltpu.CompilerParams(dimension_semantics=("arbitrary",), vmem_limit_bytes=V7X_VMEM_LIMIT),
    )(dmix, qrb, krb, *([proj] * 7), oraw, dec, qkd, rot_in, rot_tile, rst, sst, gw_pad, gb, rnw, gnw)


def _inproj_bwd(dproj, x2d, dxa, sc1p, w_in_t):
    t = x2d.shape[0]
    tm = min(PROJ_TILE, t)

    def body(dp_ref, x_ref, dxa_ref, sc_ref, w_hbm, gx_ref, acc_ref, w_vmem, sem):
        first = pl.program_id(0) == 0
        _load_w_in_t(first, w_hbm, w_vmem, sem)

        @pl.when(first)
        def _():
            acc_ref[...] = jnp.zeros_like(acc_ref)

        du = _dot(dp_ref[...], w_vmem[...])
        xh, rstd = _ln_stats(x_ref[...])
        gx_ref[...] = dxa_ref[...] + _ln_bwd(du * sc_ref[...], xh, rstd)
        acc_ref[0:1, :] += jnp.sum(du * xh, axis=0, keepdims=True)
        acc_ref[1:2, :] += jnp.sum(du, axis=0, keepdims=True)

    row = lambda i: (i, 0)
    fix = lambda i: (0, 0)
    return pl.pallas_call(
        body, name="inproj_bwd", grid=(t // tm,),
        in_specs=[pl.BlockSpec((tm, D_IN_PAD), row), pl.BlockSpec((tm, D_MODEL), row), pl.BlockSpec((tm, D_MODEL), row),
                  pl.BlockSpec((1, D_MODEL), fix), pl.BlockSpec(memory_space=pl.ANY)],
        out_specs=(pl.BlockSpec((tm, D_MODEL), row), pl.BlockSpec((8, D_MODEL), fix)),
        out_shape=(jax.ShapeDtypeStruct((t, D_MODEL), F32), jax.ShapeDtypeStruct((8, D_MODEL), F32)),
        scratch_shapes=[pltpu.VMEM((D_IN_PAD, D_MODEL), BF16), pltpu.SemaphoreType.DMA((1,))],
        compiler_params=pltpu.CompilerParams(dimension_semantics=("arbitrary",), vmem_limit_bytes=V7X_VMEM_LIMIT),
    )(dproj, x2d, dxa, sc1p, w_in_t)


def _adam_math(w, g, m, v):
    m = ADAM_B1 * m + (1.0 - ADAM_B1) * g
    v = ADAM_B2 * v + (1.0 - ADAM_B2) * (g * g)
    m_hat = m / (1.0 - ADAM_B1 ** ADAM_STEP)
    v_hat = v / (1.0 - ADAM_B2 ** ADAM_STEP)
    delta = -ADAM_LR * (m_hat / (jnp.sqrt(v_hat) + ADAM_EPS) + ADAM_WD * w)
    return delta, m, v


def _adamw(w, gparts, m, v, name):
    nparts, rows, cols = gparts.shape
    tr = rows
    for cand in (512, 256, 128, 64, 32, 16, 8):
        if rows % cand == 0:
            tr = cand
            break

    def body(w_ref, g_ref, m_ref, v_ref, go_ref, d_ref, mo_ref, vo_ref):
        g = g_ref[0].astype(F32)
        for p in range(1, nparts):
            g = g + g_ref[p].astype(F32)
        delta, mn, vn = _adam_math(w_ref[...], g, m_ref[...], v_ref[...])
        go_ref[...] = g
        d_ref[...] = delta
        mo_ref[...] = mn
        vo_ref[...] = vn

    blk = pl.BlockSpec((tr, cols), lambda i: (i, 0))
    shp = jax.ShapeDtypeStruct((rows, cols), F32)
    return pl.pallas_call(
        body, name=name, grid=(rows // tr,),
        in_specs=[blk, pl.BlockSpec((nparts, tr, cols), lambda i: (0, i, 0)), blk, blk],
        out_specs=(blk, blk, blk, blk), out_shape=(shp, shp, shp, shp),
        compiler_params=pltpu.CompilerParams(dimension_semantics=("arbitrary",), vmem_limit_bytes=V7X_VMEM_LIMIT),
    )(w, gparts, m, v)


def _small_reduce(gathered, gathered_gw, c_all, dmod_cols):
    def body(g_ref, gw_ref, c_ref, dm_ref, sum_ref, gwsum_ref, gb_ref, gwa_ref):
        s = g_ref[0]
        sw = gw_ref[0]
        for p in range(1, N_DEV):
            s = s + g_ref[p]
            sw = sw + gw_ref[p]
        sum_ref[...] = s
        gwsum_ref[...] = sw
        for i in range(6):
            gb_ref[:, i * D_MODEL:(i + 1) * D_MODEL] = s[i:i + 1, :]
        cc = c_ref[...]
        gwa_ref[...] = _dot(cc * _sigmoid(cc), dm_ref[...], TN, HIGHEST)

    vm = pl.BlockSpec(memory_space=pltpu.VMEM)
    return pl.pallas_call(
        body, name="small_reduce",
        out_shape=(jax.ShapeDtypeStruct(gathered.shape[1:], F32), jax.ShapeDtypeStruct(gathered_gw.shape[1:], F32),
                   jax.ShapeDtypeStruct((1, 6 * D_MODEL), F32), jax.ShapeDtypeStruct((D_MODEL, ADA_COLS), F32)),
        in_specs=[vm] * 4, out_specs=(vm, vm, vm, vm),
        compiler_params=pltpu.CompilerParams(vmem_limit_bytes=V7X_VMEM_LIMIT),
    )(gathered, gathered_gw, c_all, dmod_cols)


SMR_LN1W, SMR_LN1B, SMR_LN2W, SMR_LN2B, SMR_NORMS, SMR_MISC = 6, 7, 8, 9, 10, 11


def _adamw_small(gsum, g_b_ada, g_ggw, params, moms, vels):
    n = len(params)

    def body(*refs):
        gsum_ref, gb_ref, gw_ref = refs[:3]
        w_refs, m_refs, v_refs = refs[3:3 + n], refs[3 + n:3 + 2 * n], refs[3 + 2 * n:3 + 3 * n]
        outs = refs[3 + 3 * n:]
        g_refs, d_refs, mo_refs, vo_refs = outs[:n - 1], outs[n - 1:2 * n - 1], outs[2 * n - 1:3 * n - 1], outs[3 * n - 1:]
        grads = [gb_ref[...],
                 gsum_ref[SMR_NORMS:SMR_NORMS + 1, 0:RET_W],
                 gsum_ref[SMR_MISC:SMR_MISC + 1, 0:GLA_KW],
                 gsum_ref[SMR_NORMS:SMR_NORMS + 1, RET_W:RET_W + GLA_VW],
                 gsum_ref[SMR_LN1W:SMR_LN1W + 1, :], gsum_ref[SMR_LN1B:SMR_LN1B + 1, :],
                 gsum_ref[SMR_LN2W:SMR_LN2W + 1, :], gsum_ref[SMR_LN2B:SMR_LN2B + 1, :],
                 gw_ref[...]]
        for i in range(n):
            delta, mn, vn = _adam_math(w_refs[i][...], grads[i], m_refs[i][...], v_refs[i][...])
            if i < n - 1:
                g_refs[i][...] = grads[i]
            d_refs[i][...] = delta
            mo_refs[i][...] = mn
            vo_refs[i][...] = vn

    vm = pl.BlockSpec(memory_space=pltpu.VMEM)
    shapes = [jax.ShapeDtypeStruct(p.shape, F32) for p in params]
    n_in = 3 + 3 * n
    out_shape = tuple(shapes[:n - 1] + shapes * 3)
    return pl.pallas_call(
        body, name="adamw_small", out_shape=out_shape,
        in_specs=[vm] * n_in, out_specs=tuple([vm] * len(out_shape)),
        compiler_params=pltpu.CompilerParams(vmem_limit_bytes=V7X_VMEM_LIMIT),
    )(gsum, g_b_ada, g_ggw, *params, *moms, *vels)


def kernel(x, c, w_ada, b_ada, w_in, ret_norm_w, gla_gate_w, gla_gate_b, gla_norm_w, w_out, ln1_w, ln1_b, w_ff1, w_ff2, ln2_w, ln2_b, loss_target, m_w_ada, m_b_ada, m_w_in, m_ret_norm_w, m_gla_gate_w, m_gla_gate_b, m_gla_norm_w, m_w_out, m_ln1_w, m_ln1_b, m_w_ff1, m_w_ff2, m_ln2_w, m_ln2_b, v_w_ada, v_b_ada, v_w_in, v_ret_norm_w, v_gla_gate_w, v_gla_gate_b, v_gla_norm_w, v_w_out, v_ln1_w, v_ln1_b, v_w_ff1, v_w_ff2, v_ln2_w, v_ln2_b):
    t = x.shape[1]
    xi, yi, ci = _my_coords()
    me = 4 * xi + 2 * yi + ci
    x2d = x[0]
    tgt = loss_target[0]

    c_ext = jnp.concatenate([c, gla_gate_w[0].reshape(1, GATE_RANK * GLA_KW // N_DEV)], axis=1)
    b_l = lax.dynamic_slice(b_ada, (0, me * ADA_COLS), (1, ADA_COLS))
    c_all3, mod_all, wi_g, ada_token = _adaln_mod(c_ext, w_ada[0], b_l, w_in[0].T.astype(BF16))

    wg = _exchange_start([(w_out[0] + ada_token[0, 0]).astype(BF16), w_ff1[0].astype(BF16), w_ff2[0].astype(BF16)],
                         True, "wgather_start")

    c_all = c_all3[:, 0, :D_MODEL]
    gate_w = c_all3[:, 0, D_MODEL:].reshape(N_DEV, GATE_RANK, GLA_KW // N_DEV)
    gate_w = gate_w.transpose(1, 0, 2).reshape(GATE_RANK, GLA_KW)
    gw_pad = jnp.zeros((V7X_LANES, GLA_KW), F32).at[:GATE_RANK].set(gate_w)
    mod = lax.dynamic_slice(mod_all, (0, me, 0), (N_DEV, 1, ADA_COLS)).reshape(6, D_MODEL)
    shift1, scale1, gate1, shift2, scale2, gate2 = [mod[i:i + 1] for i in range(6)]

    w_in_t = wi_g.reshape(D_IN, D_MODEL)

    tables = _ret_tables(t, min(MIX_TILE, t))

    sc1p = 1.0 + scale1
    proj, u = _inproj_fwd(x2d, sc1p, shift1 + wg[4][0, 0], w_in_t)
    mixed, oraw, qrb, krb, rst, sst = _mixer_fwd(proj, tables, gw_pad, gla_gate_b, ret_norm_w, gla_norm_w)
    wo_g, w1_b, w2_g = _exchange_wait(*wg[:4], mixed, True, "wgather_wait")
    w_out_b = wo_g.reshape(D_MODEL, D_MODEL)
    w2_b = w2_g.reshape(D_FF, D_MODEL)
    vec_f = jnp.concatenate([gate1, 1.0 + scale2, shift2, gate2, ln1_w, ln1_b, ln2_w, ln2_b], axis=0)
    m, x1n, rstd1, u2, a, df, dh2, acc_f = _mid_fwd(mixed, x2d, tgt, vec_f, w_out_b, w1_b, w2_b)

    vec_b = jnp.concatenate([gate1, 1.0 + scale2, ln1_w, ln1_b, jnp.zeros((4, D_MODEL), F32)], axis=0)
    da, dm, dmix, dxa, acc_b = _ffn_bwd(df, a, dh2, x1n, rstd1, m, vec_b, w_out_b, w1_b, w2_b)
    dw2 = _matmul_tn(a, df, 2048, 1024, 2048, "tn_dw2", relu_sq=True)
    dw1 = _matmul_tn(u2, da, 1024, 2048, 2048, "tn_dw1", col_slab=FF_COLS)
    dwo = _matmul_tn(mixed, dm, 1024, 1024, 2048, "tn_dwout")
    gx = _exchange_start([dwo.reshape(N_DEV, OUT_ROWS, D_MODEL), dw1, dw2.reshape(N_DEV, FF_COLS, D_MODEL)], False,
                         "gradx_start")
    dproj, dgw, dvec = _mixer_bwd(dmix, proj, qrb, krb, oraw, tables, rst, sst, gw_pad,
                                  gla_gate_b + gx[4][0, 0], ret_norm_w, gla_norm_w)
    dwi_s = _matmul_tn(dproj, u, D_IN_PAD, 1024, 1024, "tn_dwin", out_rows=IN_COLS)
    gi = _exchange_start([dwi_s], False, "gradin_start")
    grad_x, acc_i = _inproj_bwd(dproj, x2d, dxa, sc1p + gi[4][0, 0], w_in_t)

    loss_part = jnp.sum(acc_f[3])
    small = jnp.concatenate([
        acc_i[1:2], acc_i[0:1], acc_b[4:5], acc_b[1:2], acc_b[0:1], acc_f[2:3],
        acc_b[2:3], acc_b[3:4], acc_f[0:1], acc_f[1:2],
        jnp.concatenate([dvec[0:1], dvec[1:2]], axis=1),
        jnp.concatenate([dvec[2:3, :GLA_KW], jnp.full((1, 128), loss_part, F32),
                         jnp.zeros((1, D_MODEL - GLA_KW - 128), F32)], axis=1),
        jnp.zeros((4, D_MODEL), F32)], axis=0)
    sg = _exchange_start([small, dgw[:GATE_RANK]], True, "small_start")

    r_wo, r_w1, r_w2 = _exchange_wait(*gx[:4], sg[4], False, "gradx_wait")
    r_wi, = _exchange_wait(*gi[:4], sg[4], False, "gradin_wait")
    big = [_adamw(w[0], r, m_[0], v_[0], nm) for w, r, m_, v_, nm in (
        (w_out, r_wo, m_w_out, v_w_out, "adamw_out"),
        (w_ff1, r_w1, m_w_ff1, v_w_ff1, "adamw_ff1"), (w_ff2, r_w2, m_w_ff2, v_w_ff2, "adamw_ff2"))]
    big_in = _adamw(w_in[0].T, r_wi, m_w_in[0].T, v_w_in[0].T, "adamw_in")
    big = [tuple(b.T for b in big_in)] + big
    g_big, d_big, m_big, v_big = [[b[i][None] for b in big] for i in range(4)]

    small_all, gw_all = _exchange_wait(*sg[:4], big_in[1], True, "small_wait")
    dmod_all = small_all[:, :6].reshape(N_DEV, 6 * D_MODEL)
    dmod_cols = lax.dynamic_slice(dmod_all, (0, me * ADA_COLS), (N_DEV, ADA_COLS))
    ssum, gw_sum, g_b_ada, g_w_ada = _small_reduce(small_all, gw_all, c_all, dmod_cols)
    loss = ssum[SMR_MISC, GLA_KW]
    g_ggw = lax.dynamic_slice(gw_sum, (0, me * (GLA_KW // N_DEV)), (GATE_RANK, GLA_KW // N_DEV))[None]

    small_w = [b_ada, ret_norm_w, gla_gate_b, gla_norm_w, ln1_w, ln1_b, ln2_w, ln2_b, gla_gate_w]
    small_m = [m_b_ada, m_ret_norm_w, m_gla_gate_b, m_gla_norm_w, m_ln1_w, m_ln1_b, m_ln2_w, m_ln2_b, m_gla_gate_w]
    small_v = [v_b_ada, v_ret_norm_w, v_gla_gate_b, v_gla_norm_w, v_ln1_w, v_ln1_b, v_ln2_w, v_ln2_b, v_gla_gate_w]
    res = _adamw_small(ssum, g_b_ada, g_ggw, small_w, small_m, small_v)
    small_g = list(res[:8]) + [g_ggw]
    d_small, m_small, v_small = list(res[8:17]), list(res[17:26]), list(res[26:35])

    _, d_w_ada, nm_w_ada, nv_w_ada = _adamw(w_ada[0], g_w_ada[None], m_w_ada[0], v_w_ada[0], "adamw_ada")

    def ordered(w_ada_v, small_vals, big_vals):
        b_ada_v, rnw_v, ggb_v, gnw_v, l1w_v, l1b_v, l2w_v, l2b_v, ggw_v = small_vals
        wi_v, wo_v, w1_v, w2_v = big_vals
        return [w_ada_v, b_ada_v, wi_v, rnw_v, ggw_v, ggb_v, gnw_v, wo_v, l1w_v, l1b_v, w1_v, w2_v, l2w_v, l2b_v]

    grads = ordered(g_w_ada[None], small_g, g_big)
    deltas = ordered(d_w_ada[None], d_small, d_big)
    new_m = ordered(nm_w_ada[None], m_small, m_big)
    new_v = ordered(nv_w_ada[None], v_small, v_big)
    return (loss, grad_x[None], *grads, *deltas, *new_m, *new_v)
```

```python
import numpy as np
import jax
import jax.numpy as jnp
from jax import lax
from jax.experimental import pallas as pl
from jax.experimental.pallas import tpu as pltpu

F32 = jnp.float32
BF16 = jnp.bfloat16
MESH = pl.DeviceIdType.MESH
HIGHEST = lax.Precision.HIGHEST

N_DEV = 8
D_MODEL = 1024
CHUNK = 64
RET_HEADS = 4
RET_D = 128
GLA_HEADS = 4
GLA_DK = 64
GLA_DV = 128
GLA_KW = GLA_HEADS * GLA_DK
RET_W = RET_HEADS * RET_D
GLA_VW = GLA_HEADS * GLA_DV
V7X_LANES = 128
GATE_RANK = 16
GATE_TAU = 16.0
D_FF = 4096
LN_EPS = 1e-5
ALPHA = (2.0 * 1) ** 0.25
D_IN = 3600
D_IN_PAD = 3712
ADA_COLS = 6 * D_MODEL // N_DEV
IN_COLS = D_IN // N_DEV
FF_COLS = D_FF // N_DEV
OUT_ROWS = D_MODEL // N_DEV

OFF_RQ, OFF_RK, OFF_RV, OFF_RG = 0, RET_W, 2 * RET_W, 3 * RET_W
OFF_GQ = 4 * RET_W
OFF_GK = OFF_GQ + GLA_KW
OFF_GV = OFF_GK + GLA_KW
OFF_GG = OFF_GV + GLA_VW
OFF_GLR = OFF_GG + GLA_VW

ADAM_LR, ADAM_B1, ADAM_B2, ADAM_EPS, ADAM_WD, ADAM_STEP = 0.001, 0.9, 0.999, 1e-08, 0.01, 10

V7X_VMEM_LIMIT = 62 * 1024 * 1024

ROW_TILE = 512
PROJ_TILE = 512
MIX_TILE = 512
RET_SUB = 256
GLA_SUB = 128


def _log_gamma(h):
    return float(np.log(np.float32(1.0) - np.float32(2.0) ** np.float32(-5.0 - h)))


def _my_coords():
    return lax.axis_index("x"), lax.axis_index("y"), lax.axis_index("c")


def _flip(v, bit):
    return 1 - v if bit else v


def _peer(k):
    x, y, c = _my_coords()
    px, py, pc = _flip(x, (k >> 2) & 1), _flip(y, (k >> 1) & 1), _flip(c, k & 1)
    return (px, py, pc), 4 * px + 2 * py + pc


def _dot(a, b, dims=(((1,), (0,)), ((), ())), precision=None):
    return lax.dot_general(a, b, dims, precision=precision, preferred_element_type=F32)


NN = (((1,), (0,)), ((), ()))
NT = (((1,), (1,)), ((), ()))
TN = (((0,), (0,)), ((), ()))


def _split_bf16(v, parts):
    out = []
    for _ in range(parts):
        p = v.astype(BF16)
        out.append(p)
        v = v - p.astype(F32)
    return out


def _dot_split(a, b, dims, a_exact=False):
    if a_exact:
        ab = a.astype(BF16)
        return sum(_dot(ab, p, dims) for p in _split_bf16(b, 3))
    a_hi, a_lo = _split_bf16(a, 2)
    b_hi, b_lo = _split_bf16(b, 2)
    return _dot(a_hi, b_hi, dims) + _dot(a_hi, b_lo, dims) + _dot(a_lo, b_hi, dims)


def _sigmoid(x):
    return 1.0 / (1.0 + jnp.exp(-x))


def _ln_stats(x):
    mu = jnp.mean(x, axis=-1, keepdims=True)
    xc = x - mu
    var = jnp.mean(xc * xc, axis=-1, keepdims=True)
    rstd = lax.rsqrt(var + LN_EPS)
    return xc * rstd, rstd


def _ln_bwd(dyh, xh, rstd):
    return rstd * (dyh - jnp.mean(dyh, axis=-1, keepdims=True) - xh * jnp.mean(dyh * xh, axis=-1, keepdims=True))


def _adaln_mod(c_ext, w_ada_l, b_l, w_in_l):
    width = c_ext.shape[1]

    def body(c_ref, w_ref, b_ref, wi_ref, call_ref, mod_ref, wig_ref, token_ref, s1, r1, s2, r2, gs, gr, gl):
        gather = _TwoLevelGather([wi_ref], [wig_ref], gs, gr, gl)
        gather.start()
        token_ref[...] = jnp.zeros_like(token_ref)
        x, y, c = _my_coords()
        me = 4 * x + 2 * y + c
        call_ref[me] = c_ref[...]
        sends = []
        for k in range(1, N_DEV):
            peer, _ = _peer(k)
            cp = pltpu.make_async_remote_copy(c_ref, call_ref.at[me], s1.at[k - 1], r1.at[k - 1],
                                              device_id=peer, device_id_type=MESH)
            cp.start()
            sends.append(cp)
        for k in range(1, N_DEV):
            peer, pid = _peer(k)
            pltpu.make_async_remote_copy(c_ref, call_ref.at[pid], s1.at[k - 1], r1.at[k - 1],
                                         device_id=peer, device_id_type=MESH).wait_recv()
        for cp in sends:
            cp.wait_send()
        row = lax.broadcasted_iota(jnp.int32, (N_DEV, D_MODEL), 0)
        call = jnp.zeros((N_DEV, D_MODEL), F32)
        for j in range(N_DEV):
            call = jnp.where(row == j, jnp.broadcast_to(call_ref[j][:, :D_MODEL], (N_DEV, D_MODEL)), call)
        sc = call * _sigmoid(call)
        mod = _dot(sc, w_ref[...], NN, HIGHEST) + b_ref[...]
        mod_ref[me] = mod
        sends = []
        for k in range(1, N_DEV):
            peer, _ = _peer(k)
            cp = pltpu.make_async_remote_copy(mod_ref.at[me], mod_ref.at[me], s2.at[k - 1], r2.at[k - 1],
                                              device_id=peer, device_id_type=MESH)
            cp.start()
            sends.append(cp)
        for k in range(1, N_DEV):
            peer, pid = _peer(k)
            pltpu.make_async_remote_copy(mod_ref.at[pid], mod_ref.at[pid], s2.at[k - 1], r2.at[k - 1],
                                         device_id=peer, device_id_type=MESH).wait_recv()
        for cp in sends:
            cp.wait_send()
        gather.forward()
        gather.finish()

    vm = pl.BlockSpec(memory_space=pltpu.VMEM)
    hbm = pl.BlockSpec(memory_space=pl.ANY)
    return pl.pallas_call(
        body, name="adaln_mod",
        out_shape=(jax.ShapeDtypeStruct((N_DEV, 1, width), F32),
                   jax.ShapeDtypeStruct((N_DEV, N_DEV, ADA_COLS), F32),
                   jax.ShapeDtypeStruct((N_DEV, *w_in_l.shape), w_in_l.dtype),
                   jax.ShapeDtypeStruct((8, 128), F32)),
        in_specs=[vm, vm, vm, hbm], out_specs=(vm, vm, hbm, vm),
        scratch_shapes=[pltpu.SemaphoreType.DMA((N_DEV - 1,))] * 4
        + [pltpu.SemaphoreType.DMA((7,)), pltpu.SemaphoreType.DMA((7,)), pltpu.SemaphoreType.DMA((1,))],
        compiler_params=pltpu.CompilerParams(vmem_limit_bytes=V7X_VMEM_LIMIT),
    )(c_ext, w_ada_l, b_l, w_in_l)


class _TwoLevelGather:
    def __init__(self, x_refs, out_refs, send_sems, recv_sems, local_sems):
        self.x_refs, self.out_refs = x_refs, out_refs
        self.send_sems, self.recv_sems, self.local_sems = send_sems, recv_sems, local_sems
        x, y, c = _my_coords()
        self.c = c
        self.me, self.sibling = (x, y, c), (x, y, 1 - c)
        self.chips = [(1 - x, y), (x, 1 - y), (1 - x, 1 - y)]

    def _copy(self, a, k, block, to, src=None):
        px, py, pc = block
        slab = self.out_refs[a].at[4 * px + 2 * py + pc]
        return pltpu.make_async_remote_copy(
            src_ref=slab if src is None else src, dst_ref=slab,
            send_sem=self.send_sems.at[7 * a + k], recv_sem=self.recv_sems.at[7 * a + k],
            device_id=to, device_id_type=MESH)

    def _mine(self, a):
        px, py, pc = self.me
        return pltpu.make_async_copy(self.x_refs[a], self.out_refs[a].at[4 * px + 2 * py + pc], self.local_sems.at[a])

    def _first(self, a):
        cps = [self._copy(a, 0, self.me, self.sibling, src=self.x_refs[a])]
        cps += [self._copy(a, 1 + j, self.me, (*chip, self.c), src=self.x_refs[a]) for j, chip in enumerate(self.chips)]
        return cps

    def _passed(self, a):
        return [self._copy(a, 4 + j, (*chip, self.c), self.sibling) for j, chip in enumerate(self.chips)]

    def start(self):
        for a in range(len(self.x_refs)):
            self._mine(a).start()
            for cp in self._first(a):
                cp.start()

    def forward(self):
        for a in range(len(self.x_refs)):
            passed = self._passed(a)
            for j, chip in enumerate(self.chips):
                self._copy(a, 1 + j, (*chip, self.c), self.me).wait_recv()
                passed[j].start()

    def finish(self):
        for a in range(len(self.x_refs)):
            self._copy(a, 0, self.sibling, self.me).wait_recv()
            for j, chip in enumerate(self.chips):
                self._copy(a, 4 + j, (*chip, 1 - self.c), self.me).wait_recv()
            for cp in self._first(a) + self._passed(a):
                cp.wait_send()
            self._mine(a).wait()


def _exchange_copy(src_refs, land_refs, send_sems, recv_sems, a, k, gather, receiving):
    x, y, c = _my_coords()
    me = 4 * x + 2 * y + c
    peer, pid = _peer(k)
    src = src_refs[a] if gather else src_refs[a].at[pid]
    dst = land_refs[a].at[pid if receiving else me]
    return pltpu.make_async_remote_copy(src, dst, send_sems.at[7 * a + k - 1], recv_sems.at[7 * a + k - 1],
                                        device_id=peer, device_id_type=MESH)


def _exchange_start(srcs, gather, name):
    n = len(srcs)
    xi, yi, ci = _my_coords()
    me = 4 * xi + 2 * yi + ci
    lands = []
    for s in srcs:
        own = s[None] if gather else lax.dynamic_slice_in_dim(s, me, 1, axis=0)
        lands.append(lax.dynamic_update_slice_in_dim(lax.empty((N_DEV, *own.shape[1:]), s.dtype), own, me, axis=0))

    def body(*refs):
        src_refs, land_refs, send_sems, recv_sems, token = refs[:n], refs[n:2 * n], refs[2 * n], refs[2 * n + 1], refs[-1]
        for a in range(n):
            for k in range(1, N_DEV):
                _exchange_copy(src_refs, land_refs, send_sems, recv_sems, a, k, gather, receiving=False).start()
        token[...] = jnp.zeros_like(token)

    hbm = pl.BlockSpec(memory_space=pltpu.HBM)
    sem = pl.BlockSpec(memory_space=pltpu.SEMAPHORE)
    res = pl.pallas_call(
        body, name=name,
        out_shape=(pltpu.SemaphoreType.DMA((7 * n,)), pltpu.SemaphoreType.DMA((7 * n,)),
                   *[pltpu.HBM(v.shape, v.dtype) for v in srcs + lands], jax.ShapeDtypeStruct((8, 128), F32)),
        in_specs=[hbm] * (2 * n),
        out_specs=(sem, sem, *([hbm] * (2 * n)), pl.BlockSpec(memory_space=pltpu.VMEM)),
        input_output_aliases={i: 2 + i for i in range(2 * n)},
        compiler_params=pltpu.CompilerParams(has_side_effects=pltpu.SideEffectType.DATAFLOW_SIDE_EFFECTING),
    )(*[pltpu.with_memory_space_constraint(v, pltpu.HBM) for v in srcs + lands])
    return res[0], res[1], list(res[2:2 + n]), list(res[2 + n:2 + 2 * n]), res[-1]


def _exchange_wait(send_sems, recv_sems, srcs, lands, after, gather, name):
    n = len(srcs)

    def body(*refs):
        src_refs, land_refs, s_sems, r_sems = refs[:n], refs[n:2 * n], refs[2 * n], refs[2 * n + 1]
        for a in range(n):
            for k in range(1, N_DEV):
                _exchange_copy(src_refs, land_refs, s_sems, r_sems, a, k, gather, receiving=False).wait_send()
                _exchange_copy(src_refs, land_refs, s_sems, r_sems, a, k, gather, receiving=True).wait_recv()

    hbm = pl.BlockSpec(memory_space=pltpu.HBM)
    sem = pl.BlockSpec(memory_space=pltpu.SEMAPHORE)
    res = pl.pallas_call(
        body, name=name,
        out_shape=tuple(pltpu.HBM(v.shape, v.dtype) for v in srcs + lands),
        in_specs=[hbm] * (2 * n) + [sem, sem, pl.BlockSpec(memory_space=pl.ANY)],
        out_specs=tuple([hbm] * (2 * n)),
        input_output_aliases={i: i for i in range(2 * n)},
        compiler_params=pltpu.CompilerParams(has_side_effects=pltpu.SideEffectType.DATAFLOW_SIDE_EFFECTING),
    )(*srcs, *lands, send_sems, recv_sems, after)
    return list(res[n:])


def _load_resident(step_is_first, pairs, sem):
    @pl.when(step_is_first)
    def _():
        copies = [pltpu.make_async_copy(src, dst, sem.at[i]) for i, (src, dst) in enumerate(pairs)]
        for cp in copies:
            cp.start()
        for cp in copies:
            cp.wait()


def _load_w_in_t(step_is_first, w_hbm, w_vmem, sem):
    @pl.when(step_is_first)
    def _():
        w_vmem[D_IN:, :] = jnp.zeros((D_IN_PAD - D_IN, D_MODEL), BF16)
    _load_resident(step_is_first, [(w_hbm, w_vmem.at[pl.ds(0, D_IN)])], sem)


def _inproj_fwd(x2d, sc1p, sh1, w_in_t):
    t = x2d.shape[0]
    tm = min(PROJ_TILE, t)

    def body(x_ref, sc_ref, sh_ref, w_hbm, proj_ref, u_ref, w_vmem, sem):
        _load_w_in_t(pl.program_id(0) == 0, w_hbm, w_vmem, sem)
        xh, _ = _ln_stats(x_ref[...])
        ub = (xh * sc_ref[...] + sh_ref[...]).astype(BF16)
        u_ref[...] = ub
        proj_ref[...] = _dot(ub, w_vmem[...], NT)

    row = lambda i: (i, 0)
    fix = lambda i: (0, 0)
    return pl.pallas_call(
        body, name="inproj_fwd", grid=(t // tm,),
        in_specs=[pl.BlockSpec((tm, D_MODEL), row), pl.BlockSpec((1, D_MODEL), fix), pl.BlockSpec((1, D_MODEL), fix),
                  pl.BlockSpec(memory_space=pl.ANY)],
        out_specs=(pl.BlockSpec((tm, D_IN_PAD), row), pl.BlockSpec((tm, D_MODEL), row)),
        out_shape=(jax.ShapeDtypeStruct((t, D_IN_PAD), F32), jax.ShapeDtypeStruct((t, D_MODEL), BF16)),
        scratch_shapes=[pltpu.VMEM((D_IN_PAD, D_MODEL), BF16), pltpu.SemaphoreType.DMA((1,))],
        compiler_params=pltpu.CompilerParams(dimension_semantics=("arbitrary",), vmem_limit_bytes=V7X_VMEM_LIMIT),
    )(x2d, sc1p, sh1, w_in_t)


CHUNK_SHIFT = CHUNK.bit_length() - 1


def _ret_tables(t, tl):
    r = lax.broadcasted_iota(jnp.int32, (tl, tl), 0)
    c = lax.broadcasted_iota(jnp.int32, (tl, tl), 1)
    allowed = jnp.right_shift(c, CHUNK_SHIFT) <= jnp.right_shift(r, CHUNK_SHIFT)
    dist = jnp.abs(r - c).astype(F32)
    rowf = lax.broadcasted_iota(jnp.int32, (tl, RET_D), 0).astype(F32)
    lgs = [_log_gamma(h) for h in range(RET_HEADS)]
    dec = jnp.stack([jnp.where(allowed, jnp.exp(lg * dist), 0.0) for lg in lgs])
    qkd = jnp.stack([jnp.exp(lg * (rowf + 1.0)) for lg in lgs] + [jnp.exp(lg * (tl - 1.0 - rowf)) for lg in lgs])
    inv = 1.0 / (10000.0 ** jnp.linspace(0.0, 1.0, RET_D // 2, dtype=F32))
    off = jnp.arange(tl, dtype=F32)[:, None] * inv[None, :]
    start = (jnp.arange(t // tl, dtype=F32) * tl)[:, None] * inv[None, :]
    co, so = jnp.cos(off), jnp.sin(off)
    rot_in = jnp.stack([jnp.concatenate([co, co], 1), jnp.concatenate([so, so], 1),
                        jnp.concatenate([-co, co], 1), jnp.concatenate([-so, so], 1)])
    cs, ss = jnp.cos(start), jnp.sin(start)
    rot_tile = jnp.concatenate([cs, cs, ss, ss], axis=1)
    rot_tile = jnp.broadcast_to(rot_tile[:, None, :], (t // tl, 8, 2 * RET_D))
    return dec, qkd, rot_in, rot_tile


def _tile_gammas(tl):
    return [float(np.exp(np.float32(_log_gamma(h)) * np.float32(tl))) for h in range(RET_HEADS)]


def _tile_rotary(rot_in_ref, rot_tile_ref, j):
    ca, sa = rot_tile_ref[j, 0:1, 0:RET_D], rot_tile_ref[j, 0:1, RET_D:2 * RET_D]
    cosv = ca * rot_in_ref[0] - sa * rot_in_ref[1]
    sinv = sa * rot_in_ref[2] + ca * rot_in_ref[3]
    return cosv, sinv


def _gla_consts(tl):
    r = lax.broadcasted_iota(jnp.int32, (tl, tl), 0)
    c = lax.broadcasted_iota(jnp.int32, (tl, tl), 1)
    ltri = (c <= r).astype(F32)
    utri = (c >= r).astype(F32)
    lane = lax.broadcasted_iota(jnp.int32, (1, GLA_KW), 1)
    hmask = [((lane >= h * GLA_DK) & (lane < (h + 1) * GLA_DK)).astype(F32) for h in range(GLA_HEADS)]
    rs = lax.broadcasted_iota(jnp.int32, (GLA_HEADS * tl, tl), 0) & (tl - 1)
    cs = lax.broadcasted_iota(jnp.int32, (GLA_HEADS * tl, tl), 1)
    lower = cs <= rs
    same = jnp.right_shift(cs, CHUNK_SHIFT) == jnp.right_shift(rs, CHUNK_SHIFT)
    upper = jnp.logical_and(jnp.logical_not(lower), same)
    return dict(ltri=ltri, utri=utri, hmask=hmask, lower=lower, upper=upper)


def _tile_rows(j, tl):
    return pl.ds(j * tl, tl) if isinstance(j, int) else pl.ds(pl.multiple_of(j * tl, tl), tl)


def _for_tiles(cps, fn):
    for j in range(cps):
        fn(j, 0)


def _rotate(v, cosv, sinv):
    return v * cosv + pltpu.roll(v, RET_D // 2, 1) * sinv


def _rotate_t(d, cosv, sinv):
    return d * cosv + pltpu.roll(d * sinv, RET_D // 2, 1)


def _stack_heads(v, hmask):
    return jnp.concatenate([v * hmask[h] for h in range(GLA_HEADS)], axis=0)


def _gla_gates(glr, gw, gb, ltri, tl):
    z = _dot_split(glr, gw, NN) + gb
    la = (jnp.minimum(z, 0.0) - jnp.log(1.0 + jnp.exp(-jnp.abs(z)))) * (1.0 / GATE_TAU)
    b = _dot_split(ltri, la, NN, a_exact=True)
    level = b[tl // 2 - 1:tl // 2, :]
    ep = jnp.exp(jnp.clip(b - level, -80.0, 80.0))
    em = jnp.exp(jnp.clip(level - b, -80.0, 80.0))
    bl = b[tl - 1:tl, :]
    return z, b, bl, ep, em


def _mixer_fwd(proj, tables, gw_pad, gb, rnw, gnw):
    t = proj.shape[0]
    tc = min(MIX_TILE, t)
    tr, tg = min(RET_SUB, tc), min(GLA_SUB, tc)
    nsteps = t // tc
    scale_r = RET_D ** -0.5
    scale_g = GLA_DK ** -0.5
    gammas = _tile_gammas(tr)

    def body(rq_ref, rk_ref, rv_ref, rg_ref, gq_ref, gk_ref, gv_ref, gg_ref, glr_ref,
             dec_ref, qkd_ref, rot_in_ref, rot_tile_ref, gw_ref, gb_ref, rnw_ref, gnw_ref,
             mix_ref, oraw_ref, qrb_ref, krb_ref, rst_ref, sst_ref, r_scr, s_scr):
        @pl.when(pl.program_id(0) == 0)
        def _():
            r_scr[...] = jnp.zeros_like(r_scr)
            s_scr[...] = jnp.zeros_like(s_scr)

        gla_k = _gla_consts(tg)

        def ret_tile(j, carry):
            rows = _tile_rows(j, tr)
            cosv, sinv = _tile_rotary(rot_in_ref, rot_tile_ref, j)
            for h in range(RET_HEADS):
                cols = slice(h * RET_D, (h + 1) * RET_D)
                qr = _rotate(rq_ref[rows, cols], cosv, sinv) * scale_r
                kr = _rotate(rk_ref[rows, cols], cosv, sinv)
                vb = rv_ref[rows, cols].astype(BF16)
                qb, kb = qr.astype(BF16), kr.astype(BF16)
                qrb_ref[rows, cols] = qb
                krb_ref[rows, cols] = kb
                p = _dot(qb, kb, NT) * dec_ref[h]
                rp = r_scr[cols, :]
                o = _dot(p.astype(BF16), vb) + _dot((qr * qkd_ref[h]).astype(BF16), rp.astype(BF16))
                rst_ref[j, cols, :] = rp
                r_scr[cols, :] = gammas[h] * rp + _dot((kr * qkd_ref[RET_HEADS + h]).astype(BF16), vb, TN)
                oraw_ref[rows, cols] = o
                oc = o - jnp.mean(o, axis=-1, keepdims=True)
                n = oc * lax.rsqrt(jnp.mean(oc * oc, axis=-1, keepdims=True) + LN_EPS)
                g = rg_ref[rows, cols]
                mix_ref[rows, cols] = (n * rnw_ref[:, cols] * (g * _sigmoid(g))).astype(BF16)
            return carry

        def gla_tile(j, carry):
            k = gla_k
            tl = tg
            rows = _tile_rows(j, tg)
            _, b, bl, ep, em = _gla_gates(glr_ref[rows, :], gw_ref[...], gb_ref[...], k["ltri"], tl)
            qs = gq_ref[rows, :] * scale_g
            kk = gk_ref[rows, :]
            x_all = _dot(_stack_heads(qs * ep, k["hmask"]).astype(BF16), (kk * em).astype(BF16), NT)
            y_all = _dot(_stack_heads(qs * em, k["hmask"]).astype(BF16), (kk * ep).astype(BF16), NT)
            a_all = jnp.where(k["lower"], x_all, jnp.where(k["upper"], y_all, 0.0)).astype(BF16)
            st = s_scr[...]
            oq = _dot(_stack_heads(qs * jnp.exp(b), k["hmask"]).astype(BF16), st.astype(BF16), NT)
            kg = kk * jnp.exp(bl - b)
            sst_ref[j] = st
            st_new = st * jnp.exp(bl)
            for h in range(GLA_HEADS):
                cols = slice(h * GLA_DV, (h + 1) * GLA_DV)
                hr = slice(h * tl, (h + 1) * tl)
                vb = gv_ref[rows, cols].astype(BF16)
                o = _dot(a_all[hr, :], vb) + oq[hr, :]
                st_new = st_new + _dot(vb, (kg * k["hmask"][h]).astype(BF16), TN)
                ocols = slice(RET_W + h * GLA_DV, RET_W + (h + 1) * GLA_DV)
                oraw_ref[rows, ocols] = o
                n = o * lax.rsqrt(jnp.mean(o * o, axis=-1, keepdims=True) + LN_EPS)
                g = gg_ref[rows, cols]
                mix_ref[rows, ocols] = (n * gnw_ref[:, cols] * (g * _sigmoid(g))).astype(BF16)
            s_scr[...] = st_new
            return carry

        _for_tiles(tc // tr, ret_tile)
        _for_tiles(tc // tg, gla_tile)

    def col(width, off):
        return pl.BlockSpec((tc, width), lambda i, o=off // width: (i, o))

    fix = lambda i: (0, 0)
    fix3 = lambda i: (0, 0, 0)
    dec, qkd, rot_in, rot_tile = tables
    in_specs = [col(RET_W, OFF_RQ), col(RET_W, OFF_RK), col(RET_W, OFF_RV), col(RET_W, OFF_RG),
                col(GLA_KW, OFF_GQ), col(GLA_KW, OFF_GK), col(GLA_VW, OFF_GV), col(GLA_VW, OFF_GG),
                col(V7X_LANES, OFF_GLR),
                pl.BlockSpec(dec.shape, fix3), pl.BlockSpec(qkd.shape, fix3), pl.BlockSpec(rot_in.shape, fix3),
                pl.BlockSpec((tc // tr, 8, 2 * RET_D), lambda i: (i, 0, 0)),
                pl.BlockSpec((V7X_LANES, GLA_KW), fix), pl.BlockSpec((1, GLA_KW), fix),
                pl.BlockSpec((1, RET_W), fix), pl.BlockSpec((1, GLA_VW), fix)]
    half = pl.BlockSpec((tc, RET_W), lambda i: (i, 0))
    out_specs = (pl.BlockSpec((tc, D_MODEL), lambda i: (i, 0)), pl.BlockSpec((tc, D_MODEL), lambda i: (i, 0)),
                 half, half,
                 pl.BlockSpec((tc // tr, RET_W, RET_D), lambda i: (i, 0, 0)),
                 pl.BlockSpec((tc // tg, GLA_DV, GLA_KW), lambda i: (i, 0, 0)))
    out_shape = (jax.ShapeDtypeStruct((t, D_MODEL), BF16), jax.ShapeDtypeStruct((t, D_MODEL), F32),
                 jax.ShapeDtypeStruct((t, RET_W), BF16), jax.ShapeDtypeStruct((t, RET_W), BF16),
                 jax.ShapeDtypeStruct((t // tr, RET_W, RET_D), F32),
                 jax.ShapeDtypeStruct((t // tg, GLA_DV, GLA_KW), F32))
    return pl.pallas_call(
        body, name="mixer_fwd", grid=(nsteps,), in_specs=in_specs, out_specs=out_specs, out_shape=out_shape,
        scratch_shapes=[pltpu.VMEM((RET_W, RET_D), F32), pltpu.VMEM((GLA_DV, GLA_KW), F32)],
        compiler_params=pltpu.CompilerParams(dimension_semantics=("arbitrary",), vmem_limit_bytes=V7X_VMEM_LIMIT),
    )(*([proj] * 9), dec, qkd, rot_in, rot_tile, gw_pad, gb, rnw, gnw)


def _mid_fwd(mixed, x2d, target, vecs, w_out_b, w1_b, w2_b):
    t = x2d.shape[0]
    tm = min(ROW_TILE, t)

    def body(mix_ref, x_ref, tgt_ref, v_ref, wo_hbm, w1_hbm, w2_hbm,
             m_ref, x1n_ref, rstd_ref, u2_ref, a_ref, df_ref, dh2_ref, acc_ref, wo, w1, w2, sem):
        first = pl.program_id(0) == 0
        _load_resident(first, [(wo_hbm, wo), (w1_hbm, w1), (w2_hbm, w2)], sem)

        @pl.when(first)
        def _():
            acc_ref[...] = jnp.zeros_like(acc_ref)

        gate1, sc2p, sh2, gate2 = v_ref[0:1, :], v_ref[1:2, :], v_ref[2:3, :], v_ref[3:4, :]
        l1w, l1b, l2w, l2b = v_ref[4:5, :], v_ref[5:6, :], v_ref[6:7, :], v_ref[7:8, :]
        m = _dot(mix_ref[...], wo[...])
        m_ref[...] = m.astype(BF16)
        x1n, rstd1 = _ln_stats(ALPHA * x_ref[...] + gate1 * m)
        x1n_ref[...] = x1n
        rstd_ref[...] = rstd1
        x1 = x1n * l1w + l1b
        xh1, _ = _ln_stats(x1)
        u2 = (xh1 * sc2p + sh2).astype(BF16)
        u2_ref[...] = u2
        f = jnp.zeros((tm, D_MODEL), F32)
        for j in range(N_DEV):
            cols = slice(j * FF_COLS, (j + 1) * FF_COLS)
            a = _dot(u2, w1[j])
            a_ref[:, cols] = a.astype(BF16)
            r = jnp.maximum(a, 0.0)
            f = f + _dot((r * r).astype(BF16), w2[cols, :])
        yh, rstd2 = _ln_stats(ALPHA * x1 + gate2 * f)
        e = yh * l2w + l2b - tgt_ref[...]
        dy = e * (1.0 / D_MODEL)
        dh2 = _ln_bwd(dy * l2w, yh, rstd2)
        dh2_ref[...] = dh2
        df_ref[...] = (dh2 * gate2).astype(BF16)
        acc_ref[0:1, :] += jnp.sum(dy * yh, axis=0, keepdims=True)
        acc_ref[1:2, :] += jnp.sum(dy, axis=0, keepdims=True)
        acc_ref[2:3, :] += jnp.sum(dh2 * f, axis=0, keepdims=True)
        acc_ref[3:4, :] += jnp.sum(e * e, axis=0, keepdims=True) * (0.5 / D_MODEL)

    row = lambda i: (i, 0)
    fix = lambda i: (0, 0)
    hbm = pl.BlockSpec(memory_space=pl.ANY)
    return pl.pallas_call(
        body, name="mid_fwd", grid=(t // tm,),
        in_specs=[pl.BlockSpec((tm, D_MODEL), row), pl.BlockSpec((tm, D_MODEL), row), pl.BlockSpec((tm, D_MODEL), row),
                  pl.BlockSpec((8, D_MODEL), fix), hbm, hbm, hbm],
        out_specs=(pl.BlockSpec((tm, D_MODEL), row), pl.BlockSpec((tm, D_MODEL), row), pl.BlockSpec((tm, 1), row),
                   pl.BlockSpec((tm, D_MODEL), row), pl.BlockSpec((tm, D_FF), row), pl.BlockSpec((tm, D_MODEL), row),
                   pl.BlockSpec((tm, D_MODEL), row), pl.BlockSpec((8, D_MODEL), fix)),
        out_shape=(jax.ShapeDtypeStruct((t, D_MODEL), BF16), jax.ShapeDtypeStruct((t, D_MODEL), F32),
                   jax.ShapeDtypeStruct((t, 1), F32), jax.ShapeDtypeStruct((t, D_MODEL), BF16),
                   jax.ShapeDtypeStruct((t, D_FF), BF16), jax.ShapeDtypeStruct((t, D_MODEL), BF16),
                   jax.ShapeDtypeStruct((t, D_MODEL), F32), jax.ShapeDtypeStruct((8, D_MODEL), F32)),
        scratch_shapes=[pltpu.VMEM((D_MODEL, D_MODEL), BF16), pltpu.VMEM((N_DEV, D_MODEL, FF_COLS), BF16),
                        pltpu.VMEM((D_FF, D_MODEL), BF16), pltpu.SemaphoreType.DMA((3,))],
        compiler_params=pltpu.CompilerParams(dimension_semantics=("arbitrary",), vmem_limit_bytes=V7X_VMEM_LIMIT),
    )(mixed, x2d, target, vecs, w_out_b, w1_b, w2_b)


def _ffn_bwd(df, a, dh2, x1n, rstd1, m, vecs, w_out_b, w1_b, w2_b):
    t = x1n.shape[0]
    tm = min(ROW_TILE, t)

    def body(df_ref, a_ref, dh2_ref, x1n_ref, rstd_ref, m_ref, v_ref, wo_hbm, w1_hbm, w2_hbm,
             da_ref, dm_ref, dmix_ref, dxa_ref, acc_ref, wo, w1, w2, sem):
        first = pl.program_id(0) == 0
        _load_resident(first, [(wo_hbm, wo), (w1_hbm, w1), (w2_hbm, w2)], sem)

        @pl.when(first)
        def _():
            acc_ref[...] = jnp.zeros_like(acc_ref)

        gate1, sc2p, l1w, l1b = v_ref[0:1, :], v_ref[1:2, :], v_ref[2:3, :], v_ref[3:4, :]
        df = df_ref[...]
        du2 = jnp.zeros((tm, D_MODEL), F32)
        for j in range(N_DEV):
            cols = slice(j * FF_COLS, (j + 1) * FF_COLS)
            dr2 = _dot(df, w2[cols, :], NT)
            da = (dr2 * (2.0 * jnp.maximum(a_ref[:, cols].astype(F32), 0.0))).astype(BF16)
            da_ref[:, cols] = da
            du2 = du2 + _dot(da, w1[j], NT)
        x1n = x1n_ref[...]
        xh1, rstd0 = _ln_stats(x1n * l1w + l1b)
        dx1 = ALPHA * dh2_ref[...] + _ln_bwd(du2 * sc2p, xh1, rstd0)
        dh1 = _ln_bwd(dx1 * l1w, x1n, rstd_ref[...])
        dxa_ref[...] = ALPHA * dh1
        dm = (dh1 * gate1).astype(BF16)
        dm_ref[...] = dm
        dmix_ref[...] = _dot(dm, wo[...], NT)
        acc_ref[0:1, :] += jnp.sum(du2 * xh1, axis=0, keepdims=True)
        acc_ref[1:2, :] += jnp.sum(du2, axis=0, keepdims=True)
        acc_ref[2:3, :] += jnp.sum(dx1 * x1n, axis=0, keepdims=True)
        acc_ref[3:4, :] += jnp.sum(dx1, axis=0, keepdims=True)
        acc_ref[4:5, :] += jnp.sum(dh1 * m_ref[...].astype(F32), axis=0, keepdims=True)

    row = lambda i: (i, 0)
    fix = lambda i: (0, 0)
    hbm = pl.BlockSpec(memory_space=pl.ANY)
    return pl.pallas_call(
        body, name="ffn_bwd", grid=(t // tm,),
        in_specs=[pl.BlockSpec((tm, D_MODEL), row), pl.BlockSpec((tm, D_FF), row), pl.BlockSpec((tm, D_MODEL), row),
                  pl.BlockSpec((tm, D_MODEL), row), pl.BlockSpec((tm, 1), row), pl.BlockSpec((tm, D_MODEL), row),
                  pl.BlockSpec((8, D_MODEL), fix), hbm, hbm, hbm],
        out_specs=(pl.BlockSpec((tm, D_FF), row), pl.BlockSpec((tm, D_MODEL), row), pl.BlockSpec((tm, D_MODEL), row),
                   pl.BlockSpec((tm, D_MODEL), row), pl.BlockSpec((8, D_MODEL), fix)),
        out_shape=(jax.ShapeDtypeStruct((t, D_FF), BF16), jax.ShapeDtypeStruct((t, D_MODEL), BF16),
                   jax.ShapeDtypeStruct((t, D_MODEL), F32), jax.ShapeDtypeStruct((t, D_MODEL), F32),
                   jax.ShapeDtypeStruct((8, D_MODEL), F32)),
        scratch_shapes=[pltpu.VMEM((D_MODEL, D_MODEL), BF16), pltpu.VMEM((N_DEV, D_MODEL, FF_COLS), BF16),
                        pltpu.VMEM((D_FF, D_MODEL), BF16), pltpu.SemaphoreType.DMA((3,))],
        compiler_params=pltpu.CompilerParams(dimension_semantics=("arbitrary",), vmem_limit_bytes=V7X_VMEM_LIMIT),
    )(df, a, dh2, x1n, rstd1, m, vecs, w_out_b, w1_b, w2_b)


def _matmul_tn(lhs, rhs, tmm, tn, tk, name, relu_sq=False, col_slab=None, out_rows=None):
    t, mm = lhs.shape
    assert out_rows is None or (col_slab is None and tmm == mm)
    nn = rhs.shape[1]
    tk = min(tk, t)
    nk = t // tk

    def body(l_ref, r_ref, o_ref, acc):
        kk = pl.program_id(2)

        @pl.when(kk == 0)
        def _():
            acc[...] = jnp.zeros_like(acc)

        l = l_ref[...]
        if relu_sq:
            lf = jnp.maximum(l.astype(F32), 0.0)
            l = (lf * lf).astype(BF16)
        acc[...] += _dot(l, r_ref[...], TN)

        @pl.when(kk == nk - 1)
        def _():
            if out_rows is not None:
                for s in range(N_DEV):
                    o_ref[s] = acc[s * out_rows:(s + 1) * out_rows, :].astype(o_ref.dtype)
            elif col_slab is None:
                o_ref[...] = acc[...].astype(o_ref.dtype)
            else:
                for s in range(tn // col_slab):
                    o_ref[s] = acc[:, s * col_slab:(s + 1) * col_slab].astype(o_ref.dtype)

    if out_rows is not None:
        out_spec = pl.BlockSpec((N_DEV, out_rows, tn), lambda i, j, k: (0, 0, j))
        out_shape = jax.ShapeDtypeStruct((N_DEV, out_rows, nn), BF16)
    elif col_slab is None:
        out_spec = pl.BlockSpec((tmm, tn), lambda i, j, k: (i, j))
        out_shape = jax.ShapeDtypeStruct((mm, nn), BF16)
    else:
        out_spec = pl.BlockSpec((tn // col_slab, tmm, col_slab), lambda i, j, k: (j, i, 0))
        out_shape = jax.ShapeDtypeStruct((nn // col_slab, mm, col_slab), BF16)
    return pl.pallas_call(
        body, name=name, grid=(mm // tmm, nn // tn, nk),
        in_specs=[pl.BlockSpec((tk, tmm), lambda i, j, k: (k, i)), pl.BlockSpec((tk, tn), lambda i, j, k: (k, j))],
        out_specs=out_spec,
        out_shape=out_shape,
        scratch_shapes=[pltpu.VMEM((tmm, tn), F32)],
        compiler_params=pltpu.CompilerParams(dimension_semantics=("arbitrary", "arbitrary", "arbitrary"),
                                             vmem_limit_bytes=V7X_VMEM_LIMIT),
    )(lhs, rhs)


def _mixer_bwd(dmix, proj, qrb, krb, oraw, tables, rst, sst, gw_pad, gb, rnw, gnw):
    t = proj.shape[0]
    tc = min(MIX_TILE, t)
    tr, tg = min(RET_SUB, tc), min(GLA_SUB, tc)
    nsteps = t // tc
    scale_r = RET_D ** -0.5
    scale_g = GLA_DK ** -0.5
    gammas = _tile_gammas(tr)

    def body(dmix_ref, qrb_ref, krb_ref, rv_ref, rg_ref, gq_ref, gk_ref, gv_ref, gg_ref, glr_ref, oraw_ref,
             dec_ref, qkd_ref, rot_in_ref, rot_tile_ref, rst_ref, sst_ref, gw_ref, gb_ref, rnw_ref, gnw_ref,
             dproj_ref, dgw_ref, dvec_ref, dr_scr, ds_scr):
        @pl.when(pl.program_id(0) == 0)
        def _():
            dr_scr[...] = jnp.zeros_like(dr_scr)
            ds_scr[...] = jnp.zeros_like(ds_scr)
            dgw_ref[...] = jnp.zeros_like(dgw_ref)
            dvec_ref[...] = jnp.zeros_like(dvec_ref)

        gla_k = _gla_consts(tg)
        last_row = lax.broadcasted_iota(jnp.int32, (tg, GLA_KW), 0) == tg - 1

        def ret_tile(jj, carry):
            j = tc // tr - 1 - jj
            rows = _tile_rows(j, tr)
            cosv, sinv = _tile_rotary(rot_in_ref, rot_tile_ref, j)
            for h in range(RET_HEADS):
                cols = slice(h * RET_D, (h + 1) * RET_D)
                o = oraw_ref[rows, cols]
                g = rg_ref[rows, cols]
                w = rnw_ref[:, cols]
                dout = dmix_ref[rows, cols]
                oc = o - jnp.mean(o, axis=-1, keepdims=True)
                inv = lax.rsqrt(jnp.mean(oc * oc, axis=-1, keepdims=True) + LN_EPS)
                n = oc * inv
                sg = _sigmoid(g)
                sil = g * sg
                dn = dout * w * sil
                dvec_ref[0:1, cols] += jnp.sum(dout * n * sil, axis=0, keepdims=True)
                dproj_ref[rows, OFF_RG + h * RET_D:OFF_RG + (h + 1) * RET_D] = (
                    dout * n * w * (sg * (1.0 + g * (1.0 - sg)))).astype(BF16)
                doc = inv * (dn - n * jnp.mean(dn * n, axis=-1, keepdims=True))
                do = doc - jnp.mean(doc, axis=-1, keepdims=True)

                qb, kb = qrb_ref[rows, cols], krb_ref[rows, cols]
                qr, kr = qb.astype(F32), kb.astype(F32)
                vb = rv_ref[rows, cols].astype(BF16)
                dob = do.astype(BF16)
                qd, kd = qkd_ref[h], qkd_ref[RET_HEADS + h]
                p = _dot(qb, kb, NT) * dec_ref[h]
                rp = rst_ref[j, cols, :].astype(BF16)
                dr = dr_scr[cols, :]
                drb = dr.astype(BF16)
                dpb = (_dot(dob, vb, NT) * dec_ref[h]).astype(BF16)
                dqr = _dot(dpb, kb) + _dot(dob, rp, NT) * qd
                dkr = _dot(dpb, qb, TN) + _dot(vb, drb, NT) * kd
                dv = _dot(p.astype(BF16), dob, TN) + _dot((kr * kd).astype(BF16), drb)
                dr_scr[cols, :] = gammas[h] * dr + _dot((qr * qd).astype(BF16), dob, TN)
                dproj_ref[rows, OFF_RQ + h * RET_D:OFF_RQ + (h + 1) * RET_D] = (
                    _rotate_t(dqr, cosv, sinv) * scale_r).astype(BF16)
                dproj_ref[rows, OFF_RK + h * RET_D:OFF_RK + (h + 1) * RET_D] = _rotate_t(dkr, cosv, sinv).astype(BF16)
                dproj_ref[rows, OFF_RV + h * RET_D:OFF_RV + (h + 1) * RET_D] = dv.astype(BF16)
            return carry

        def gla_tile(jj, carry):
            k = gla_k
            tl = tg
            j = tc // tg - 1 - jj
            rows = _tile_rows(j, tg)
            glr = glr_ref[rows, :]
            z, b, bl, ep, em = _gla_gates(glr, gw_ref[...], gb_ref[...], k["ltri"], tl)
            qs = gq_ref[rows, :] * scale_g
            kk = gk_ref[rows, :]
            eb = jnp.exp(b)
            ekb = jnp.exp(bl - b)
            ebl = jnp.exp(bl)
            ql, qu, kl, ku = qs * ep, qs * em, kk * em, kk * ep
            qg, kg = qs * eb, kk * ekb
            qlm = _stack_heads(ql, k["hmask"]).astype(BF16)
            qum = _stack_heads(qu, k["hmask"]).astype(BF16)
            klb, kub = kl.astype(BF16), ku.astype(BF16)
            a_all = jnp.where(k["lower"], _dot(qlm, klb, NT),
                              jnp.where(k["upper"], _dot(qum, kub, NT), 0.0)).astype(BF16)
            st = sst_ref[j]
            stb = st.astype(BF16)
            ds = ds_scr[...]
            dsb = ds.astype(BF16)
            ds_new = ds * ebl
            da_parts = []
            dqg = jnp.zeros((tl, GLA_KW), F32)
            dkg = jnp.zeros((tl, GLA_KW), F32)
            for h in range(GLA_HEADS):
                cols = slice(h * GLA_DV, (h + 1) * GLA_DV)
                hr = slice(h * tl, (h + 1) * tl)
                ocols = slice(RET_W + h * GLA_DV, RET_W + (h + 1) * GLA_DV)
                o = oraw_ref[rows, ocols]
                g = gg_ref[rows, cols]
                w = gnw_ref[:, cols]
                dout = dmix_ref[rows, ocols]
                inv = lax.rsqrt(jnp.mean(o * o, axis=-1, keepdims=True) + LN_EPS)
                n = o * inv
                sg = _sigmoid(g)
                sil = g * sg
                dn = dout * w * sil
                dvec_ref[1:2, cols] += jnp.sum(dout * n * sil, axis=0, keepdims=True)
                dproj_ref[rows, OFF_GG + h * GLA_DV:OFF_GG + (h + 1) * GLA_DV] = (
                    dout * n * w * (sg * (1.0 + g * (1.0 - sg)))).astype(BF16)
                dob = (inv * (dn - n * jnp.mean(dn * n, axis=-1, keepdims=True))).astype(BF16)
                vb = gv_ref[rows, cols].astype(BF16)
                mh = k["hmask"][h]
                da_parts.append(_dot(dob, vb, NT))
                dv = _dot(a_all[hr, :], dob, TN) + _dot((kg * mh).astype(BF16), dsb, NT)
                dproj_ref[rows, OFF_GV + h * GLA_DV:OFF_GV + (h + 1) * GLA_DV] = dv.astype(BF16)
                dkg = dkg + mh * _dot(vb, dsb)
                dqg = dqg + mh * _dot(dob, stb)
                ds_new = ds_new + _dot(dob, (qg * mh).astype(BF16), TN)
            da_all = jnp.concatenate(da_parts, axis=0)
            dal = jnp.where(k["lower"], da_all, 0.0).astype(BF16)
            dau = jnp.where(k["upper"], da_all, 0.0).astype(BF16)
            dqlm = _dot(dal, klb)
            dqum = _dot(dau, kub)
            dql = jnp.zeros((tl, GLA_KW), F32)
            dqu = jnp.zeros((tl, GLA_KW), F32)
            for h in range(GLA_HEADS):
                hr = slice(h * tl, (h + 1) * tl)
                dql = dql + k["hmask"][h] * dqlm[hr, :]
                dqu = dqu + k["hmask"][h] * dqum[hr, :]
            dkl = _dot(dal, qlm, TN)
            dku = _dot(dau, qum, TN)
            dbl = (jnp.sum(dkg * kg, axis=0, keepdims=True)
                   + jnp.sum(ds * st, axis=0, keepdims=True) * ebl)
            ds_scr[...] = ds_new
            dqs = dql * ep + dqu * em + dqg * eb
            dk = dkl * em + dku * ep + dkg * ekb
            db = dql * ql - dkl * kl - dqu * qu + dku * ku + dqg * qg - dkg * kg
            db = db + jnp.where(last_row, dbl, 0.0)
            dla = _dot_split(k["utri"], db, NN, a_exact=True)
            dz = dla * (1.0 / GATE_TAU) * _sigmoid(-z)
            dvec_ref[2:3, 0:GLA_KW] += jnp.sum(dz, axis=0, keepdims=True)
            dgw_ref[...] += _dot_split(glr, dz, TN)
            dproj_ref[rows, OFF_GLR:D_IN_PAD] = _dot(dz.astype(BF16), gw_ref[...].astype(BF16), NT).astype(BF16)
            dproj_ref[rows, OFF_GQ:OFF_GQ + GLA_KW] = (dqs * scale_g).astype(BF16)
            dproj_ref[rows, OFF_GK:OFF_GK + GLA_KW] = dk.astype(BF16)
            return carry

        _for_tiles(tc // tr, ret_tile)
        _for_tiles(tc // tg, gla_tile)

    rev = lambda i: (nsteps - 1 - i, 0)

    def col(width, off):
        return pl.BlockSpec((tc, width), lambda i, o=off // width: (nsteps - 1 - i, o))

    fix = lambda i: (0, 0)
    fix3 = lambda i: (0, 0, 0)
    dec, qkd, rot_in, rot_tile = tables
    half = pl.BlockSpec((tc, RET_W), rev)
    in_specs = [pl.BlockSpec((tc, D_MODEL), rev), half, half, col(RET_W, OFF_RV), col(RET_W, OFF_RG),
                col(GLA_KW, OFF_GQ), col(GLA_KW, OFF_GK), col(GLA_VW, OFF_GV), col(GLA_VW, OFF_GG),
                col(V7X_LANES, OFF_GLR),
                pl.BlockSpec((tc, D_MODEL), rev),
                pl.BlockSpec(dec.shape, fix3), pl.BlockSpec(qkd.shape, fix3), pl.BlockSpec(rot_in.shape, fix3),
                pl.BlockSpec((tc // tr, 8, 2 * RET_D), lambda i: (nsteps - 1 - i, 0, 0)),
                pl.BlockSpec((tc // tr, RET_W, RET_D), lambda i: (nsteps - 1 - i, 0, 0)),
                pl.BlockSpec((tc // tg, GLA_DV, GLA_KW), lambda i: (nsteps - 1 - i, 0, 0)),
                pl.BlockSpec((V7X_LANES, GLA_KW), fix), pl.BlockSpec((1, GLA_KW), fix),
                pl.BlockSpec((1, RET_W), fix), pl.BlockSpec((1, GLA_VW), fix)]
    out_specs = (pl.BlockSpec((tc, D_IN_PAD), rev), pl.BlockSpec((V7X_LANES, GLA_KW), fix),
                 pl.BlockSpec((8, RET_W), fix))
    out_shape = (jax.ShapeDtypeStruct((t, D_IN_PAD), BF16), jax.ShapeDtypeStruct((V7X_LANES, GLA_KW), F32),
                 jax.ShapeDtypeStruct((8, RET_W), F32))
    return pl.pallas_call(
        body, name="mixer_bwd", grid=(nsteps,), in_specs=in_specs, out_specs=out_specs, out_shape=out_shape,
        scratch_shapes=[pltpu.VMEM((RET_W, RET_D), F32), pltpu.VMEM((GLA_DV, GLA_KW), F32)],
        compiler_params=pltpu.CompilerParams(dimension_semantics=("arbitrary",), vmem_limit_bytes=V7X_VMEM_LIMIT),
    )(dmix, qrb, krb, *([proj] * 7), oraw, dec, qkd, rot_in, rot_tile, rst, sst, gw_pad, gb, rnw, gnw)


def _inproj_bwd(dproj, x2d, dxa, sc1p, w_in_t):
    t = x2d.shape[0]
    tm = min(PROJ_TILE, t)

    def body(dp_ref, x_ref, dxa_ref, sc_ref, w_hbm, gx_ref, acc_ref, w_vmem, sem):
        first = pl.program_id(0) == 0
        _load_w_in_t(first, w_hbm, w_vmem, sem)

        @pl.when(first)
        def _():
            acc_ref[...] = jnp.zeros_like(acc_ref)

        du = _dot(dp_ref[...], w_vmem[...])
        xh, rstd = _ln_stats(x_ref[...])
        gx_ref[...] = dxa_ref[...] + _ln_bwd(du * sc_ref[...], xh, rstd)
        acc_ref[0:1, :] += jnp.sum(du * xh, axis=0, keepdims=True)
        acc_ref[1:2, :] += jnp.sum(du, axis=0, keepdims=True)

    row = lambda i: (i, 0)
    fix = lambda i: (0, 0)
    return pl.pallas_call(
        body, name="inproj_bwd", grid=(t // tm,),
        in_specs=[pl.BlockSpec((tm, D_IN_PAD), row), pl.BlockSpec((tm, D_MODEL), row), pl.BlockSpec((tm, D_MODEL), row),
                  pl.BlockSpec((1, D_MODEL), fix), pl.BlockSpec(memory_space=pl.ANY)],
        out_specs=(pl.BlockSpec((tm, D_MODEL), row), pl.BlockSpec((8, D_MODEL), fix)),
        out_shape=(jax.ShapeDtypeStruct((t, D_MODEL), F32), jax.ShapeDtypeStruct((8, D_MODEL), F32)),
        scratch_shapes=[pltpu.VMEM((D_IN_PAD, D_MODEL), BF16), pltpu.SemaphoreType.DMA((1,))],
        compiler_params=pltpu.CompilerParams(dimension_semantics=("arbitrary",), vmem_limit_bytes=V7X_VMEM_LIMIT),
    )(dproj, x2d, dxa, sc1p, w_in_t)


def _adam_math(w, g, m, v):
    m = ADAM_B1 * m + (1.0 - ADAM_B1) * g
    v = ADAM_B2 * v + (1.0 - ADAM_B2) * (g * g)
    m_hat = m / (1.0 - ADAM_B1 ** ADAM_STEP)
    v_hat = v / (1.0 - ADAM_B2 ** ADAM_STEP)
    delta = -ADAM_LR * (m_hat / (jnp.sqrt(v_hat) + ADAM_EPS) + ADAM_WD * w)
    return delta, m, v


def _adamw(w, gparts, m, v, name):
    nparts, rows, cols = gparts.shape
    tr = rows
    for cand in (512, 256, 128, 64, 32, 16, 8):
        if rows % cand == 0:
            tr = cand
            break

    def body(w_ref, g_ref, m_ref, v_ref, go_ref, d_ref, mo_ref, vo_ref):
        g = g_ref[0].astype(F32)
        for p in range(1, nparts):
            g = g + g_ref[p].astype(F32)
        delta, mn, vn = _adam_math(w_ref[...], g, m_ref[...], v_ref[...])
        go_ref[...] = g
        d_ref[...] = delta
        mo_ref[...] = mn
        vo_ref[...] = vn

    blk = pl.BlockSpec((tr, cols), lambda i: (i, 0))
    shp = jax.ShapeDtypeStruct((rows, cols), F32)
    return pl.pallas_call(
        body, name=name, grid=(rows // tr,),
        in_specs=[blk, pl.BlockSpec((nparts, tr, cols), lambda i: (0, i, 0)), blk, blk],
        out_specs=(blk, blk, blk, blk), out_shape=(shp, shp, shp, shp),
        compiler_params=pltpu.CompilerParams(dimension_semantics=("arbitrary",), vmem_limit_bytes=V7X_VMEM_LIMIT),
    )(w, gparts, m, v)


def _small_reduce(gathered, gathered_gw, c_all, dmod_cols):
    def body(g_ref, gw_ref, c_ref, dm_ref, sum_ref, gwsum_ref, gb_ref, gwa_ref):
        s = g_ref[0]
        sw = gw_ref[0]
        for p in range(1, N_DEV):
            s = s + g_ref[p]
            sw = sw + gw_ref[p]
        sum_ref[...] = s
        gwsum_ref[...] = sw
        for i in range(6):
            gb_ref[:, i * D_MODEL:(i + 1) * D_MODEL] = s[i:i + 1, :]
        cc = c_ref[...]
        gwa_ref[...] = _dot(cc * _sigmoid(cc), dm_ref[...], TN, HIGHEST)

    vm = pl.BlockSpec(memory_space=pltpu.VMEM)
    return pl.pallas_call(
        body, name="small_reduce",
        out_shape=(jax.ShapeDtypeStruct(gathered.shape[1:], F32), jax.ShapeDtypeStruct(gathered_gw.shape[1:], F32),
                   jax.ShapeDtypeStruct((1, 6 * D_MODEL), F32), jax.ShapeDtypeStruct((D_MODEL, ADA_COLS), F32)),
        in_specs=[vm] * 4, out_specs=(vm, vm, vm, vm),
        compiler_params=pltpu.CompilerParams(vmem_limit_bytes=V7X_VMEM_LIMIT),
    )(gathered, gathered_gw, c_all, dmod_cols)


SMR_LN1W, SMR_LN1B, SMR_LN2W, SMR_LN2B, SMR_NORMS, SMR_MISC = 6, 7, 8, 9, 10, 11


def _adamw_small(gsum, g_b_ada, g_ggw, params, moms, vels):
    n = len(params)

    def body(*refs):
        gsum_ref, gb_ref, gw_ref = refs[:3]
        w_refs, m_refs, v_refs = refs[3:3 + n], refs[3 + n:3 + 2 * n], refs[3 + 2 * n:3 + 3 * n]
        outs = refs[3 + 3 * n:]
        g_refs, d_refs, mo_refs, vo_refs = outs[:n - 1], outs[n - 1:2 * n - 1], outs[2 * n - 1:3 * n - 1], outs[3 * n - 1:]
        grads = [gb_ref[...],
                 gsum_ref[SMR_NORMS:SMR_NORMS + 1, 0:RET_W],
                 gsum_ref[SMR_MISC:SMR_MISC + 1, 0:GLA_KW],
                 gsum_ref[SMR_NORMS:SMR_NORMS + 1, RET_W:RET_W + GLA_VW],
                 gsum_ref[SMR_LN1W:SMR_LN1W + 1, :], gsum_ref[SMR_LN1B:SMR_LN1B + 1, :],
                 gsum_ref[SMR_LN2W:SMR_LN2W + 1, :], gsum_ref[SMR_LN2B:SMR_LN2B + 1, :],
                 gw_ref[...]]
        for i in range(n):
            delta, mn, vn = _adam_math(w_refs[i][...], grads[i], m_refs[i][...], v_refs[i][...])
            if i < n - 1:
                g_refs[i][...] = grads[i]
            d_refs[i][...] = delta
            mo_refs[i][...] = mn
            vo_refs[i][...] = vn

    vm = pl.BlockSpec(memory_space=pltpu.VMEM)
    shapes = [jax.ShapeDtypeStruct(p.shape, F32) for p in params]
    n_in = 3 + 3 * n
    out_shape = tuple(shapes[:n - 1] + shapes * 3)
    return pl.pallas_call(
        body, name="adamw_small", out_shape=out_shape,
        in_specs=[vm] * n_in, out_specs=tuple([vm] * len(out_shape)),
        compiler_params=pltpu.CompilerParams(vmem_limit_bytes=V7X_VMEM_LIMIT),
    )(gsum, g_b_ada, g_ggw, *params, *moms, *vels)


def kernel(x, c, w_ada, b_ada, w_in, ret_norm_w, gla_gate_w, gla_gate_b, gla_norm_w, w_out, ln1_w, ln1_b, w_ff1, w_ff2, ln2_w, ln2_b, loss_target, m_w_ada, m_b_ada, m_w_in, m_ret_norm_w, m_gla_gate_w, m_gla_gate_b, m_gla_norm_w, m_w_out, m_ln1_w, m_ln1_b, m_w_ff1, m_w_ff2, m_ln2_w, m_ln2_b, v_w_ada, v_b_ada, v_w_in, v_ret_norm_w, v_gla_gate_w, v_gla_gate_b, v_gla_norm_w, v_w_out, v_ln1_w, v_ln1_b, v_w_ff1, v_w_ff2, v_ln2_w, v_ln2_b):
    t = x.shape[1]
    xi, yi, ci = _my_coords()
    me = 4 * xi + 2 * yi + ci
    x2d = x[0]
    tgt = loss_target[0]

    c_ext = jnp.concatenate([c, gla_gate_w[0].reshape(1, GATE_RANK * GLA_KW // N_DEV)], axis=1)
    b_l = lax.dynamic_slice(b_ada, (0, me * ADA_COLS), (1, ADA_COLS))
    c_all3, mod_all, wi_g, ada_token = _adaln_mod(c_ext, w_ada[0], b_l, w_in[0].T.astype(BF16))

    wg = _exchange_start([(w_out[0] + ada_token[0, 0]).astype(BF16), w_ff1[0].astype(BF16), w_ff2[0].astype(BF16)],
                         True, "wgather_start")

    c_all = c_all3[:, 0, :D_MODEL]
    gate_w = c_all3[:, 0, D_MODEL:].reshape(N_DEV, GATE_RANK, GLA_KW // N_DEV)
    gate_w = gate_w.transpose(1, 0, 2).reshape(GATE_RANK, GLA_KW)
    gw_pad = jnp.zeros((V7X_LANES, GLA_KW), F32).at[:GATE_RANK].set(gate_w)
    mod = lax.dynamic_slice(mod_all, (0, me, 0), (N_DEV, 1, ADA_COLS)).reshape(6, D_MODEL)
    shift1, scale1, gate1, shift2, scale2, gate2 = [mod[i:i + 1] for i in range(6)]

    w_in_t = wi_g.reshape(D_IN, D_MODEL)

    tables = _ret_tables(t, min(RET_SUB, t))

    sc1p = 1.0 + scale1
    proj, u = _inproj_fwd(x2d, sc1p, shift1 + wg[4][0, 0], w_in_t)
    mixed, oraw, qrb, krb, rst, sst = _mixer_fwd(proj, tables, gw_pad, gla_gate_b, ret_norm_w, gla_norm_w)
    wo_g, w1_b, w2_g = _exchange_wait(*wg[:4], mixed, True, "wgather_wait")
    w_out_b = wo_g.reshape(D_MODEL, D_MODEL)
    w2_b = w2_g.reshape(D_FF, D_MODEL)
    vec_f = jnp.concatenate([gate1, 1.0 + scale2, shift2, gate2, ln1_w, ln1_b, ln2_w, ln2_b], axis=0)
    m, x1n, rstd1, u2, a, df, dh2, acc_f = _mid_fwd(mixed, x2d, tgt, vec_f, w_out_b, w1_b, w2_b)

    vec_b = jnp.concatenate([gate1, 1.0 + scale2, ln1_w, ln1_b, jnp.zeros((4, D_MODEL), F32)], axis=0)
    da, dm, dmix, dxa, acc_b = _ffn_bwd(df, a, dh2, x1n, rstd1, m, vec_b, w_out_b, w1_b, w2_b)
    dw2 = _matmul_tn(a, df, 2048, 1024, 2048, "tn_dw2", relu_sq=True)
    dw1 = _matmul_tn(u2, da, 1024, 2048, 2048, "tn_dw1", col_slab=FF_COLS)
    dwo = _matmul_tn(mixed, dm, 1024, 1024, 2048, "tn_dwout")
    gx = _exchange_start([dwo.reshape(N_DEV, OUT_ROWS, D_MODEL), dw1, dw2.reshape(N_DEV, FF_COLS, D_MODEL)], False,
                         "gradx_start")
    dproj, dgw, dvec = _mixer_bwd(dmix, proj, qrb, krb, oraw, tables, rst, sst, gw_pad,
                                  gla_gate_b + gx[4][0, 0], ret_norm_w, gla_norm_w)
    dwi_s = _matmul_tn(dproj, u, D_IN_PAD, 1024, 1024, "tn_dwin", out_rows=IN_COLS)
    gi = _exchange_start([dwi_s], False, "gradin_start")
    grad_x, acc_i = _inproj_bwd(dproj, x2d, dxa, sc1p + gi[4][0, 0], w_in_t)

    loss_part = jnp.sum(acc_f[3])
    small = jnp.concatenate([
        acc_i[1:2], acc_i[0:1], acc_b[4:5], acc_b[1:2], acc_b[0:1], acc_f[2:3],
        acc_b[2:3], acc_b[3:4], acc_f[0:1], acc_f[1:2],
        jnp.concatenate([dvec[0:1], dvec[1:2]], axis=1),
        jnp.concatenate([dvec[2:3, :GLA_KW], jnp.full((1, 128), loss_part, F32),
                         jnp.zeros((1, D_MODEL - GLA_KW - 128), F32)], axis=1),
        jnp.zeros((4, D_MODEL), F32)], axis=0)
    sg = _exchange_start([small, dgw[:GATE_RANK]], True, "small_start")

    r_wo, r_w1, r_w2 = _exchange_wait(*gx[:4], sg[4], False, "gradx_wait")
    r_wi, = _exchange_wait(*gi[:4], sg[4], False, "gradin_wait")
    big = [_adamw(w[0], r, m_[0], v_[0], nm) for w, r, m_, v_, nm in (
        (w_out, r_wo, m_w_out, v_w_out, "adamw_out"),
        (w_ff1, r_w1, m_w_ff1, v_w_ff1, "adamw_ff1"), (w_ff2, r_w2, m_w_ff2, v_w_ff2, "adamw_ff2"))]
    big_in = _adamw(w_in[0].T, r_wi, m_w_in[0].T, v_w_in[0].T, "adamw_in")
    big = [tuple(b.T for b in big_in)] + big
    g_big, d_big, m_big, v_big = [[b[i][None] for b in big] for i in range(4)]

    small_all, gw_all = _exchange_wait(*sg[:4], big_in[1], True, "small_wait")
    dmod_all = small_all[:, :6].reshape(N_DEV, 6 * D_MODEL)
    dmod_cols = lax.dynamic_slice(dmod_all, (0, me * ADA_COLS), (N_DEV, ADA_COLS))
    ssum, gw_sum, g_b_ada, g_w_ada = _small_reduce(small_all, gw_all, c_all, dmod_cols)
    loss = ssum[SMR_MISC, GLA_KW]
    g_ggw = lax.dynamic_slice(gw_sum, (0, me * (GLA_KW // N_DEV)), (GATE_RANK, GLA_KW // N_DEV))[None]

    small_w = [b_ada, ret_norm_w, gla_gate_b, gla_norm_w, ln1_w, ln1_b, ln2_w, ln2_b, gla_gate_w]
    small_m = [m_b_ada, m_ret_norm_w, m_gla_gate_b, m_gla_norm_w, m_ln1_w, m_ln1_b, m_ln2_w, m_ln2_b, m_gla_gate_w]
    small_v = [v_b_ada, v_ret_norm_w, v_gla_gate_b, v_gla_norm_w, v_ln1_w, v_ln1_b, v_ln2_w, v_ln2_b, v_gla_gate_w]
    res = _adamw_small(ssum, g_b_ada, g_ggw, small_w, small_m, small_v)
    small_g = list(res[:8]) + [g_ggw]
    d_small, m_small, v_small = list(res[8:17]), list(res[17:26]), list(res[26:35])

    _, d_w_ada, nm_w_ada, nv_w_ada = _adamw(w_ada[0], g_w_ada[None], m_w_ada[0], v_w_ada[0], "adamw_ada")

    def ordered(w_ada_v, small_vals, big_vals):
        b_ada_v, rnw_v, ggb_v, gnw_v, l1w_v, l1b_v, l2w_v, l2b_v, ggw_v = small_vals
        wi_v, wo_v, w1_v, w2_v = big_vals
        return [w_ada_v, b_ada_v, wi_v, rnw_v, ggw_v, ggb_v, gnw_v, wo_v, l1w_v, l1b_v, w1_v, w2_v, l2w_v, l2b_v]

    grads = ordered(g_w_ada[None], small_g, g_big)
    deltas = ordered(d_w_ada[None], d_small, d_big)
    new_m = ordered(nm_w_ada[None], m_small, m_big)
    new_v = ordered(nv_w_ada[None], v_small, v_big)
    return (loss, grad_x[None], *grads, *deltas, *new_m, *new_v)
```

```python
import numpy as np
import jax
import jax.numpy as jnp
from jax import lax
from jax.experimental import pallas as pl
from jax.experimental.pallas import tpu as pltpu

F32 = jnp.float32
BF16 = jnp.bfloat16
MESH = pl.DeviceIdType.MESH
HIGHEST = lax.Precision.HIGHEST

N_DEV = 8
D_MODEL = 1024
CHUNK = 64
RET_HEADS = 4
RET_D = 128
GLA_HEADS = 4
GLA_DK = 64
GLA_DV = 128
GLA_KW = GLA_HEADS * GLA_DK
RET_W = RET_HEADS * RET_D
GLA_VW = GLA_HEADS * GLA_DV
V7X_LANES = 128
GATE_RANK = 16
GATE_TAU = 16.0
D_FF = 4096
LN_EPS = 1e-5
ALPHA = (2.0 * 1) ** 0.25
D_IN = 3600
D_IN_PAD = 3712
ADA_COLS = 6 * D_MODEL // N_DEV
IN_COLS = D_IN // N_DEV
FF_COLS = D_FF // N_DEV
OUT_ROWS = D_MODEL // N_DEV

OFF_RQ, OFF_RK, OFF_RV, OFF_RG = 0, RET_W, 2 * RET_W, 3 * RET_W
OFF_GQ = 4 * RET_W
OFF_GK = OFF_GQ + GLA_KW
OFF_GV = OFF_GK + GLA_KW
OFF_GG = OFF_GV + GLA_VW
OFF_GLR = OFF_GG + GLA_VW

ADAM_LR, ADAM_B1, ADAM_B2, ADAM_EPS, ADAM_WD, ADAM_STEP = 0.001, 0.9, 0.999, 1e-08, 0.01, 10

V7X_VMEM_LIMIT = 62 * 1024 * 1024

ROW_TILE = 512
PROJ_TILE = 512
MIX_TILE = 512
RET_SUB = 256
GLA_SUB = 128


def _log_gamma(h):
    return float(np.log(np.float32(1.0) - np.float32(2.0) ** np.float32(-5.0 - h)))


def _my_coords():
    return lax.axis_index("x"), lax.axis_index("y"), lax.axis_index("c")


def _flip(v, bit):
    return 1 - v if bit else v


def _peer(k):
    x, y, c = _my_coords()
    px, py, pc = _flip(x, (k >> 2) & 1), _flip(y, (k >> 1) & 1), _flip(c, k & 1)
    return (px, py, pc), 4 * px + 2 * py + pc


def _dot(a, b, dims=(((1,), (0,)), ((), ())), precision=None):
    return lax.dot_general(a, b, dims, precision=precision, preferred_element_type=F32)


NN = (((1,), (0,)), ((), ()))
NT = (((1,), (1,)), ((), ()))
TN = (((0,), (0,)), ((), ()))


def _split_bf16(v, parts):
    out = []
    for _ in range(parts):
        p = v.astype(BF16)
        out.append(p)
        v = v - p.astype(F32)
    return out


def _dot_split(a, b, dims, a_exact=False):
    if a_exact:
        ab = a.astype(BF16)
        return sum(_dot(ab, p, dims) for p in _split_bf16(b, 2))
    a_hi, a_lo = _split_bf16(a, 2)
    b_hi, b_lo = _split_bf16(b, 2)
    return _dot(a_hi, b_hi, dims) + _dot(a_hi, b_lo, dims) + _dot(a_lo, b_hi, dims)


def _sigmoid(x):
    return 1.0 / (1.0 + jnp.exp(-x))


def _ln_stats(x):
    mu = jnp.mean(x, axis=-1, keepdims=True)
    xc = x - mu
    var = jnp.mean(xc * xc, axis=-1, keepdims=True)
    rstd = lax.rsqrt(var + LN_EPS)
    return xc * rstd, rstd


def _ln_bwd(dyh, xh, rstd):
    return rstd * (dyh - jnp.mean(dyh, axis=-1, keepdims=True) - xh * jnp.mean(dyh * xh, axis=-1, keepdims=True))


def _adaln_mod(c_ext, w_ada_l, b_l, w_in_l):
    width = c_ext.shape[1]

    def body(c_ref, w_ref, b_ref, wi_ref, call_ref, mod_ref, wig_ref, token_ref, s1, r1, s2, r2, gs, gr, gl):
        gather = _TwoLevelGather([wi_ref], [wig_ref], gs, gr, gl)
        gather.start()
        token_ref[...] = jnp.zeros_like(token_ref)
        x, y, c = _my_coords()
        me = 4 * x + 2 * y + c
        call_ref[me] = c_ref[...]
        sends = []
        for k in range(1, N_DEV):
            peer, _ = _peer(k)
            cp = pltpu.make_async_remote_copy(c_ref, call_ref.at[me], s1.at[k - 1], r1.at[k - 1],
                                              device_id=peer, device_id_type=MESH)
            cp.start()
            sends.append(cp)
        for k in range(1, N_DEV):
            peer, pid = _peer(k)
            pltpu.make_async_remote_copy(c_ref, call_ref.at[pid], s1.at[k - 1], r1.at[k - 1],
                                         device_id=peer, device_id_type=MESH).wait_recv()
        for cp in sends:
            cp.wait_send()
        row = lax.broadcasted_iota(jnp.int32, (N_DEV, D_MODEL), 0)
        call = jnp.zeros((N_DEV, D_MODEL), F32)
        for j in range(N_DEV):
            call = jnp.where(row == j, jnp.broadcast_to(call_ref[j][:, :D_MODEL], (N_DEV, D_MODEL)), call)
        sc = call * _sigmoid(call)
        mod = _dot(sc, w_ref[...], NN, HIGHEST) + b_ref[...]
        mod_ref[me] = mod
        sends = []
        for k in range(1, N_DEV):
            peer, _ = _peer(k)
            cp = pltpu.make_async_remote_copy(mod_ref.at[me], mod_ref.at[me], s2.at[k - 1], r2.at[k - 1],
                                              device_id=peer, device_id_type=MESH)
            cp.start()
            sends.append(cp)
        for k in range(1, N_DEV):
            peer, pid = _peer(k)
            pltpu.make_async_remote_copy(mod_ref.at[pid], mod_ref.at[pid], s2.at[k - 1], r2.at[k - 1],
                                         device_id=peer, device_id_type=MESH).wait_recv()
        for cp in sends:
            cp.wait_send()
        gather.forward()
        gather.finish()

    vm = pl.BlockSpec(memory_space=pltpu.VMEM)
    hbm = pl.BlockSpec(memory_space=pl.ANY)
    return pl.pallas_call(
        body, name="adaln_mod",
        out_shape=(jax.ShapeDtypeStruct((N_DEV, 1, width), F32),
                   jax.ShapeDtypeStruct((N_DEV, N_DEV, ADA_COLS), F32),
                   jax.ShapeDtypeStruct((N_DEV, *w_in_l.shape), w_in_l.dtype),
                   jax.ShapeDtypeStruct((8, 128), F32)),
        in_specs=[vm, vm, vm, hbm], out_specs=(vm, vm, hbm, vm),
        scratch_shapes=[pltpu.SemaphoreType.DMA((N_DEV - 1,))] * 4
        + [pltpu.SemaphoreType.DMA((7,)), pltpu.SemaphoreType.DMA((7,)), pltpu.SemaphoreType.DMA((1,))],
        compiler_params=pltpu.CompilerParams(vmem_limit_bytes=V7X_VMEM_LIMIT),
    )(c_ext, w_ada_l, b_l, w_in_l)


class _TwoLevelGather:
    def __init__(self, x_refs, out_refs, send_sems, recv_sems, local_sems):
        self.x_refs, self.out_refs = x_refs, out_refs
        self.send_sems, self.recv_sems, self.local_sems = send_sems, recv_sems, local_sems
        x, y, c = _my_coords()
        self.c = c
        self.me, self.sibling = (x, y, c), (x, y, 1 - c)
        self.chips = [(1 - x, y), (x, 1 - y), (1 - x, 1 - y)]

    def _copy(self, a, k, block, to, src=None):
        px, py, pc = block
        slab = self.out_refs[a].at[4 * px + 2 * py + pc]
        return pltpu.make_async_remote_copy(
            src_ref=slab if src is None else src, dst_ref=slab,
            send_sem=self.send_sems.at[7 * a + k], recv_sem=self.recv_sems.at[7 * a + k],
            device_id=to, device_id_type=MESH)

    def _mine(self, a):
        px, py, pc = self.me
        return pltpu.make_async_copy(self.x_refs[a], self.out_refs[a].at[4 * px + 2 * py + pc], self.local_sems.at[a])

    def _first(self, a):
        cps = [self._copy(a, 0, self.me, self.sibling, src=self.x_refs[a])]
        cps += [self._copy(a, 1 + j, self.me, (*chip, self.c), src=self.x_refs[a]) for j, chip in enumerate(self.chips)]
        return cps

    def _passed(self, a):
        return [self._copy(a, 4 + j, (*chip, self.c), self.sibling) for j, chip in enumerate(self.chips)]

    def start(self):
        for a in range(len(self.x_refs)):
            self._mine(a).start()
            for cp in self._first(a):
                cp.start()

    def forward(self):
        for a in range(len(self.x_refs)):
            passed = self._passed(a)
            for j, chip in enumerate(self.chips):
                self._copy(a, 1 + j, (*chip, self.c), self.me).wait_recv()
                passed[j].start()

    def finish(self):
        for a in range(len(self.x_refs)):
            self._copy(a, 0, self.sibling, self.me).wait_recv()
            for j, chip in enumerate(self.chips):
                self._copy(a, 4 + j, (*chip, 1 - self.c), self.me).wait_recv()
            for cp in self._first(a) + self._passed(a):
                cp.wait_send()
            self._mine(a).wait()


def _exchange_copy(src_refs, land_refs, send_sems, recv_sems, a, k, gather, receiving):
    x, y, c = _my_coords()
    me = 4 * x + 2 * y + c
    peer, pid = _peer(k)
    src = src_refs[a] if gather else src_refs[a].at[pid]
    dst = land_refs[a].at[pid if receiving else me]
    return pltpu.make_async_remote_copy(src, dst, send_sems.at[7 * a + k - 1], recv_sems.at[7 * a + k - 1],
                                        device_id=peer, device_id_type=MESH)


def _exchange_start(srcs, gather, name):
    n = len(srcs)
    xi, yi, ci = _my_coords()
    me = 4 * xi + 2 * yi + ci
    lands = []
    for s in srcs:
        own = s[None] if gather else lax.dynamic_slice_in_dim(s, me, 1, axis=0)
        lands.append(lax.dynamic_update_slice_in_dim(lax.empty((N_DEV, *own.shape[1:]), s.dtype), own, me, axis=0))

    def body(*refs):
        src_refs, land_refs, send_sems, recv_sems, token = refs[:n], refs[n:2 * n], refs[2 * n], refs[2 * n + 1], refs[-1]
        for a in range(n):
            for k in range(1, N_DEV):
                _exchange_copy(src_refs, land_refs, send_sems, recv_sems, a, k, gather, receiving=False).start()
        token[...] = jnp.zeros_like(token)

    hbm = pl.BlockSpec(memory_space=pltpu.HBM)
    sem = pl.BlockSpec(memory_space=pltpu.SEMAPHORE)
    res = pl.pallas_call(
        body, name=name,
        out_shape=(pltpu.SemaphoreType.DMA((7 * n,)), pltpu.SemaphoreType.DMA((7 * n,)),
                   *[pltpu.HBM(v.shape, v.dtype) for v in srcs + lands], jax.ShapeDtypeStruct((8, 128), F32)),
        in_specs=[hbm] * (2 * n),
        out_specs=(sem, sem, *([hbm] * (2 * n)), pl.BlockSpec(memory_space=pltpu.VMEM)),
        input_output_aliases={i: 2 + i for i in range(2 * n)},
        compiler_params=pltpu.CompilerParams(has_side_effects=pltpu.SideEffectType.DATAFLOW_SIDE_EFFECTING),
    )(*[pltpu.with_memory_space_constraint(v, pltpu.HBM) for v in srcs + lands])
    return res[0], res[1], list(res[2:2 + n]), list(res[2 + n:2 + 2 * n]), res[-1]


def _exchange_wait(send_sems, recv_sems, srcs, lands, after, gather, name):
    n = len(srcs)

    def body(*refs):
        src_refs, land_refs, s_sems, r_sems = refs[:n], refs[n:2 * n], refs[2 * n], refs[2 * n + 1]
        for a in range(n):
            for k in range(1, N_DEV):
                _exchange_copy(src_refs, land_refs, s_sems, r_sems, a, k, gather, receiving=False).wait_send()
                _exchange_copy(src_refs, land_refs, s_sems, r_sems, a, k, gather, receiving=True).wait_recv()

    hbm = pl.BlockSpec(memory_space=pltpu.HBM)
    sem = pl.BlockSpec(memory_space=pltpu.SEMAPHORE)
    res = pl.pallas_call(
        body, name=name,
        out_shape=tuple(pltpu.HBM(v.shape, v.dtype) for v in srcs + lands),
        in_specs=[hbm] * (2 * n) + [sem, sem, pl.BlockSpec(memory_space=pl.ANY)],
        out_specs=tuple([hbm] * (2 * n)),
        input_output_aliases={i: i for i in range(2 * n)},
        compiler_params=pltpu.CompilerParams(has_side_effects=pltpu.SideEffectType.DATAFLOW_SIDE_EFFECTING),
    )(*srcs, *lands, send_sems, recv_sems, after)
    return list(res[n:])


def _load_resident(step_is_first, pairs, sem):
    @pl.when(step_is_first)
    def _():
        copies = [pltpu.make_async_copy(src, dst, sem.at[i]) for i, (src, dst) in enumerate(pairs)]
        for cp in copies:
            cp.start()
        for cp in copies:
            cp.wait()


def _load_w_in_t(step_is_first, w_hbm, w_vmem, sem):
    @pl.when(step_is_first)
    def _():
        w_vmem[D_IN:, :] = jnp.zeros((D_IN_PAD - D_IN, D_MODEL), BF16)
    _load_resident(step_is_first, [(w_hbm, w_vmem.at[pl.ds(0, D_IN)])], sem)


def _inproj_fwd(x2d, sc1p, sh1, w_in_t):
    t = x2d.shape[0]
    tm = min(PROJ_TILE, t)

    def body(x_ref, sc_ref, sh_ref, w_hbm, proj_ref, u_ref, w_vmem, sem):
        _load_w_in_t(pl.program_id(0) == 0, w_hbm, w_vmem, sem)
        xh, _ = _ln_stats(x_ref[...])
        ub = (xh * sc_ref[...] + sh_ref[...]).astype(BF16)
        u_ref[...] = ub
        proj_ref[...] = _dot(ub, w_vmem[...], NT)

    row = lambda i: (i, 0)
    fix = lambda i: (0, 0)
    return pl.pallas_call(
        body, name="inproj_fwd", grid=(t // tm,),
        in_specs=[pl.BlockSpec((tm, D_MODEL), row), pl.BlockSpec((1, D_MODEL), fix), pl.BlockSpec((1, D_MODEL), fix),
                  pl.BlockSpec(memory_space=pl.ANY)],
        out_specs=(pl.BlockSpec((tm, D_IN_PAD), row), pl.BlockSpec((tm, D_MODEL), row)),
        out_shape=(jax.ShapeDtypeStruct((t, D_IN_PAD), F32), jax.ShapeDtypeStruct((t, D_MODEL), BF16)),
        scratch_shapes=[pltpu.VMEM((D_IN_PAD, D_MODEL), BF16), pltpu.SemaphoreType.DMA((1,))],
        compiler_params=pltpu.CompilerParams(dimension_semantics=("arbitrary",), vmem_limit_bytes=V7X_VMEM_LIMIT),
    )(x2d, sc1p, sh1, w_in_t)


CHUNK_SHIFT = CHUNK.bit_length() - 1


def _ret_tables(t, tl):
    r = lax.broadcasted_iota(jnp.int32, (tl, tl), 0)
    c = lax.broadcasted_iota(jnp.int32, (tl, tl), 1)
    allowed = jnp.right_shift(c, CHUNK_SHIFT) <= jnp.right_shift(r, CHUNK_SHIFT)
    dist = jnp.abs(r - c).astype(F32)
    rowf = lax.broadcasted_iota(jnp.int32, (tl, RET_D), 0).astype(F32)
    lgs = [_log_gamma(h) for h in range(RET_HEADS)]
    dec = jnp.stack([jnp.where(allowed, jnp.exp(lg * dist), 0.0) for lg in lgs])
    qkd = jnp.stack([jnp.exp(lg * (rowf + 1.0)) for lg in lgs] + [jnp.exp(lg * (tl - 1.0 - rowf)) for lg in lgs])
    inv = 1.0 / (10000.0 ** jnp.linspace(0.0, 1.0, RET_D // 2, dtype=F32))
    off = jnp.arange(tl, dtype=F32)[:, None] * inv[None, :]
    start = (jnp.arange(t // tl, dtype=F32) * tl)[:, None] * inv[None, :]
    co, so = jnp.cos(off), jnp.sin(off)
    rot_in = jnp.stack([jnp.concatenate([co, co], 1), jnp.concatenate([so, so], 1),
                        jnp.concatenate([-co, co], 1), jnp.concatenate([-so, so], 1)])
    cs, ss = jnp.cos(start), jnp.sin(start)
    rot_tile = jnp.concatenate([cs, cs, ss, ss], axis=1)
    rot_tile = jnp.broadcast_to(rot_tile[:, None, :], (t // tl, 8, 2 * RET_D))
    return dec, qkd, rot_in, rot_tile


def _tile_gammas(tl):
    return [float(np.exp(np.float32(_log_gamma(h)) * np.float32(tl))) for h in range(RET_HEADS)]


def _tile_rotary(rot_in_ref, rot_tile_ref, j):
    ca, sa = rot_tile_ref[j, 0:1, 0:RET_D], rot_tile_ref[j, 0:1, RET_D:2 * RET_D]
    cosv = ca * rot_in_ref[0] - sa * rot_in_ref[1]
    sinv = sa * rot_in_ref[2] + ca * rot_in_ref[3]
    return cosv, sinv


def _gla_consts(tl):
    r = lax.broadcasted_iota(jnp.int32, (tl, tl), 0)
    c = lax.broadcasted_iota(jnp.int32, (tl, tl), 1)
    ltri = (c <= r).astype(F32)
    utri = (c >= r).astype(F32)
    lane = lax.broadcasted_iota(jnp.int32, (1, GLA_KW), 1)
    hmask = [((lane >= h * GLA_DK) & (lane < (h + 1) * GLA_DK)).astype(F32) for h in range(GLA_HEADS)]
    rs = lax.broadcasted_iota(jnp.int32, (GLA_HEADS * tl, tl), 0) & (tl - 1)
    cs = lax.broadcasted_iota(jnp.int32, (GLA_HEADS * tl, tl), 1)
    lower = cs <= rs
    same = jnp.right_shift(cs, CHUNK_SHIFT) == jnp.right_shift(rs, CHUNK_SHIFT)
    upper = jnp.logical_and(jnp.logical_not(lower), same)
    return dict(ltri=ltri, utri=utri, hmask=hmask, lower=lower, upper=upper)


def _tile_rows(j, tl):
    return pl.ds(j * tl, tl) if isinstance(j, int) else pl.ds(pl.multiple_of(j * tl, tl), tl)


def _for_tiles(cps, fn):
    for j in range(cps):
        fn(j, 0)


def _rotate(v, cosv, sinv):
    return v * cosv + pltpu.roll(v, RET_D // 2, 1) * sinv


def _rotate_t(d, cosv, sinv):
    return d * cosv + pltpu.roll(d * sinv, RET_D // 2, 1)


def _stack_heads(v, hmask):
    return jnp.concatenate([v * hmask[h] for h in range(GLA_HEADS)], axis=0)


def _gla_gates(glr, gw, gb, ltri, tl):
    z = _dot_split(glr, gw, NN) + gb
    la = (jnp.minimum(z, 0.0) - jnp.log(1.0 + jnp.exp(-jnp.abs(z)))) * (1.0 / GATE_TAU)
    b = _dot_split(ltri, la, NN, a_exact=True)
    level = b[tl // 2 - 1:tl // 2, :]
    ep = jnp.exp(jnp.clip(b - level, -80.0, 80.0))
    em = jnp.exp(jnp.clip(level - b, -80.0, 80.0))
    bl = b[tl - 1:tl, :]
    return z, b, bl, ep, em


def _mixer_fwd(proj, tables, gw_pad, gb, rnw, gnw):
    t = proj.shape[0]
    tc = min(MIX_TILE, t)
    tr, tg = min(RET_SUB, tc), min(GLA_SUB, tc)
    nsteps = t // tc
    scale_r = RET_D ** -0.5
    scale_g = GLA_DK ** -0.5
    gammas = _tile_gammas(tr)

    def body(rq_ref, rk_ref, rv_ref, rg_ref, gq_ref, gk_ref, gv_ref, gg_ref, glr_ref,
             dec_ref, qkd_ref, rot_in_ref, rot_tile_ref, gw_ref, gb_ref, rnw_ref, gnw_ref,
             mix_ref, oraw_ref, qrb_ref, krb_ref, rst_ref, sst_ref, r_scr, s_scr):
        @pl.when(pl.program_id(0) == 0)
        def _():
            r_scr[...] = jnp.zeros_like(r_scr)
            s_scr[...] = jnp.zeros_like(s_scr)

        gla_k = _gla_consts(tg)

        def ret_tile(j, carry):
            rows = _tile_rows(j, tr)
            cosv, sinv = _tile_rotary(rot_in_ref, rot_tile_ref, j)
            for h in range(RET_HEADS):
                cols = slice(h * RET_D, (h + 1) * RET_D)
                qr = _rotate(rq_ref[rows, cols], cosv, sinv) * scale_r
                kr = _rotate(rk_ref[rows, cols], cosv, sinv)
                vb = rv_ref[rows, cols].astype(BF16)
                qb, kb = qr.astype(BF16), kr.astype(BF16)
                qrb_ref[rows, cols] = qb
                krb_ref[rows, cols] = kb
                p = _dot(qb, kb, NT) * dec_ref[h]
                rp = r_scr[cols, :]
                o = _dot(p.astype(BF16), vb) + _dot((qr * qkd_ref[h]).astype(BF16), rp.astype(BF16))
                rst_ref[j, cols, :] = rp
                r_scr[cols, :] = gammas[h] * rp + _dot((kr * qkd_ref[RET_HEADS + h]).astype(BF16), vb, TN)
                oraw_ref[rows, cols] = o
                oc = o - jnp.mean(o, axis=-1, keepdims=True)
                n = oc * lax.rsqrt(jnp.mean(oc * oc, axis=-1, keepdims=True) + LN_EPS)
                g = rg_ref[rows, cols]
                mix_ref[rows, cols] = (n * rnw_ref[:, cols] * (g * _sigmoid(g))).astype(BF16)
            return carry

        def gla_tile(j, carry):
            k = gla_k
            tl = tg
            rows = _tile_rows(j, tg)
            _, b, bl, ep, em = _gla_gates(glr_ref[rows, :], gw_ref[...], gb_ref[...], k["ltri"], tl)
            qs = gq_ref[rows, :] * scale_g
            kk = gk_ref[rows, :]
            x_all = _dot(_stack_heads(qs * ep, k["hmask"]).astype(BF16), (kk * em).astype(BF16), NT)
            y_all = _dot(_stack_heads(qs * em, k["hmask"]).astype(BF16), (kk * ep).astype(BF16), NT)
            a_all = jnp.where(k["lower"], x_all, jnp.where(k["upper"], y_all, 0.0)).astype(BF16)
            st = s_scr[...]
            oq = _dot(_stack_heads(qs * jnp.exp(b), k["hmask"]).astype(BF16), st.astype(BF16), NT)
            kg = kk * jnp.exp(bl - b)
            sst_ref[j] = st
            st_new = st * jnp.exp(bl)
            for h in range(GLA_HEADS):
                cols = slice(h * GLA_DV, (h + 1) * GLA_DV)
                hr = slice(h * tl, (h + 1) * tl)
                vb = gv_ref[rows, cols].astype(BF16)
                o = _dot(a_all[hr, :], vb) + oq[hr, :]
                st_new = st_new + _dot(vb, (kg * k["hmask"][h]).astype(BF16), TN)
                ocols = slice(RET_W + h * GLA_DV, RET_W + (h + 1) * GLA_DV)
                oraw_ref[rows, ocols] = o
                n = o * lax.rsqrt(jnp.mean(o * o, axis=-1, keepdims=True) + LN_EPS)
                g = gg_ref[rows, cols]
                mix_ref[rows, ocols] = (n * gnw_ref[:, cols] * (g * _sigmoid(g))).astype(BF16)
            s_scr[...] = st_new
            return carry

        _for_tiles(tc // tr, ret_tile)
        _for_tiles(tc // tg, gla_tile)

    def col(width, off):
        return pl.BlockSpec((tc, width), lambda i, o=off // width: (i, o))

    fix = lambda i: (0, 0)
    fix3 = lambda i: (0, 0, 0)
    dec, qkd, rot_in, rot_tile = tables
    in_specs = [col(RET_W, OFF_RQ), col(RET_W, OFF_RK), col(RET_W, OFF_RV), col(RET_W, OFF_RG),
                col(GLA_KW, OFF_GQ), col(GLA_KW, OFF_GK), col(GLA_VW, OFF_GV), col(GLA_VW, OFF_GG),
                col(V7X_LANES, OFF_GLR),
                pl.BlockSpec(dec.shape, fix3), pl.BlockSpec(qkd.shape, fix3), pl.BlockSpec(rot_in.shape, fix3),
                pl.BlockSpec((tc // tr, 8, 2 * RET_D), lambda i: (i, 0, 0)),
                pl.BlockSpec((V7X_LANES, GLA_KW), fix), pl.BlockSpec((1, GLA_KW), fix),
                pl.BlockSpec((1, RET_W), fix), pl.BlockSpec((1, GLA_VW), fix)]
    half = pl.BlockSpec((tc, RET_W), lambda i: (i, 0))
    out_specs = (pl.BlockSpec((tc, D_MODEL), lambda i: (i, 0)), pl.BlockSpec((tc, D_MODEL), lambda i: (i, 0)),
                 half, half,
                 pl.BlockSpec((tc // tr, RET_W, RET_D), lambda i: (i, 0, 0)),
                 pl.BlockSpec((tc // tg, GLA_DV, GLA_KW), lambda i: (i, 0, 0)))
    out_shape = (jax.ShapeDtypeStruct((t, D_MODEL), BF16), jax.ShapeDtypeStruct((t, D_MODEL), F32),
                 jax.ShapeDtypeStruct((t, RET_W), BF16), jax.ShapeDtypeStruct((t, RET_W), BF16),
                 jax.ShapeDtypeStruct((t // tr, RET_W, RET_D), F32),
                 jax.ShapeDtypeStruct((t // tg, GLA_DV, GLA_KW), F32))
    return pl.pallas_call(
        body, name="mixer_fwd", grid=(nsteps,), in_specs=in_specs, out_specs=out_specs, out_shape=out_shape,
        scratch_shapes=[pltpu.VMEM((RET_W, RET_D), F32), pltpu.VMEM((GLA_DV, GLA_KW), F32)],
        compiler_params=pltpu.CompilerParams(dimension_semantics=("arbitrary",), vmem_limit_bytes=V7X_VMEM_LIMIT),
    )(*([proj] * 9), dec, qkd, rot_in, rot_tile, gw_pad, gb, rnw, gnw)


def _mid_fwd(mixed, x2d, target, vecs, w_out_b, w1_b, w2_b):
    t = x2d.shape[0]
    tm = min(ROW_TILE, t)

    def body(mix_ref, x_ref, tgt_ref, v_ref, wo_hbm, w1_hbm, w2_hbm,
             m_ref, x1n_ref, rstd_ref, u2_ref, a_ref, df_ref, dh2_ref, acc_ref, wo, w1, w2, sem):
        first = pl.program_id(0) == 0
        _load_resident(first, [(wo_hbm, wo), (w1_hbm, w1), (w2_hbm, w2)], sem)

        @pl.when(first)
        def _():
            acc_ref[...] = jnp.zeros_like(acc_ref)

        gate1, sc2p, sh2, gate2 = v_ref[0:1, :], v_ref[1:2, :], v_ref[2:3, :], v_ref[3:4, :]
        l1w, l1b, l2w, l2b = v_ref[4:5, :], v_ref[5:6, :], v_ref[6:7, :], v_ref[7:8, :]
        m = _dot(mix_ref[...], wo[...])
        m_ref[...] = m.astype(BF16)
        x1n, rstd1 = _ln_stats(ALPHA * x_ref[...] + gate1 * m)
        x1n_ref[...] = x1n
        rstd_ref[...] = rstd1
        x1 = x1n * l1w + l1b
        xh1, _ = _ln_stats(x1)
        u2 = (xh1 * sc2p + sh2).astype(BF16)
        u2_ref[...] = u2
        f = jnp.zeros((tm, D_MODEL), F32)
        for j in range(N_DEV):
            cols = slice(j * FF_COLS, (j + 1) * FF_COLS)
            a = _dot(u2, w1[j])
            a_ref[:, cols] = a.astype(BF16)
            r = jnp.maximum(a, 0.0)
            f = f + _dot((r * r).astype(BF16), w2[cols, :])
        yh, rstd2 = _ln_stats(ALPHA * x1 + gate2 * f)
        e = yh * l2w + l2b - tgt_ref[...]
        dy = e * (1.0 / D_MODEL)
        dh2 = _ln_bwd(dy * l2w, yh, rstd2)
        dh2_ref[...] = dh2
        df_ref[...] = (dh2 * gate2).astype(BF16)
        acc_ref[0:1, :] += jnp.sum(dy * yh, axis=0, keepdims=True)
        acc_ref[1:2, :] += jnp.sum(dy, axis=0, keepdims=True)
        acc_ref[2:3, :] += jnp.sum(dh2 * f, axis=0, keepdims=True)
        acc_ref[3:4, :] += jnp.sum(e * e, axis=0, keepdims=True) * (0.5 / D_MODEL)

    row = lambda i: (i, 0)
    fix = lambda i: (0, 0)
    hbm = pl.BlockSpec(memory_space=pl.ANY)
    return pl.pallas_call(
        body, name="mid_fwd", grid=(t // tm,),
        in_specs=[pl.BlockSpec((tm, D_MODEL), row), pl.BlockSpec((tm, D_MODEL), row), pl.BlockSpec((tm, D_MODEL), row),
                  pl.BlockSpec((8, D_MODEL), fix), hbm, hbm, hbm],
        out_specs=(pl.BlockSpec((tm, D_MODEL), row), pl.BlockSpec((tm, D_MODEL), row), pl.BlockSpec((tm, 1), row),
                   pl.BlockSpec((tm, D_MODEL), row), pl.BlockSpec((tm, D_FF), row), pl.BlockSpec((tm, D_MODEL), row),
                   pl.BlockSpec((tm, D_MODEL), row), pl.BlockSpec((8, D_MODEL), fix)),
        out_shape=(jax.ShapeDtypeStruct((t, D_MODEL), BF16), jax.ShapeDtypeStruct((t, D_MODEL), F32),
                   jax.ShapeDtypeStruct((t, 1), F32), jax.ShapeDtypeStruct((t, D_MODEL), BF16),
                   jax.ShapeDtypeStruct((t, D_FF), BF16), jax.ShapeDtypeStruct((t, D_MODEL), BF16),
                   jax.ShapeDtypeStruct((t, D_MODEL), F32), jax.ShapeDtypeStruct((8, D_MODEL), F32)),
        scratch_shapes=[pltpu.VMEM((D_MODEL, D_MODEL), BF16), pltpu.VMEM((N_DEV, D_MODEL, FF_COLS), BF16),
                        pltpu.VMEM((D_FF, D_MODEL), BF16), pltpu.SemaphoreType.DMA((3,))],
        compiler_params=pltpu.CompilerParams(dimension_semantics=("arbitrary",), vmem_limit_bytes=V7X_VMEM_LIMIT),
    )(mixed, x2d, target, vecs, w_out_b, w1_b, w2_b)


def _ffn_bwd(df, a, dh2, x1n, rstd1, m, vecs, w_out_b, w1_b, w2_b):
    t = x1n.shape[0]
    tm = min(ROW_TILE, t)

    def body(df_ref, a_ref, dh2_ref, x1n_ref, rstd_ref, m_ref, v_ref, wo_hbm, w1_hbm, w2_hbm,
             da_ref, dm_ref, dmix_ref, dxa_ref, acc_ref, wo, w1, w2, sem):
        first = pl.program_id(0) == 0
        _load_resident(first, [(wo_hbm, wo), (w1_hbm, w1), (w2_hbm, w2)], sem)

        @pl.when(first)
        def _():
            acc_ref[...] = jnp.zeros_like(acc_ref)

        gate1, sc2p, l1w, l1b = v_ref[0:1, :], v_ref[1:2, :], v_ref[2:3, :], v_ref[3:4, :]
        df = df_ref[...]
        du2 = jnp.zeros((tm, D_MODEL), F32)
        for j in range(N_DEV):
            cols = slice(j * FF_COLS, (j + 1) * FF_COLS)
            dr2 = _dot(df, w2[cols, :], NT)
            da = (dr2 * (2.0 * jnp.maximum(a_ref[:, cols].astype(F32), 0.0))).astype(BF16)
            da_ref[:, cols] = da
            du2 = du2 + _dot(da, w1[j], NT)
        x1n = x1n_ref[...]
        xh1, rstd0 = _ln_stats(x1n * l1w + l1b)
        dx1 = ALPHA * dh2_ref[...] + _ln_bwd(du2 * sc2p, xh1, rstd0)
        dh1 = _ln_bwd(dx1 * l1w, x1n, rstd_ref[...])
        dxa_ref[...] = ALPHA * dh1
        dm = (dh1 * gate1).astype(BF16)
        dm_ref[...] = dm
        dmix_ref[...] = _dot(dm, wo[...], NT)
        acc_ref[0:1, :] += jnp.sum(du2 * xh1, axis=0, keepdims=True)
        acc_ref[1:2, :] += jnp.sum(du2, axis=0, keepdims=True)
        acc_ref[2:3, :] += jnp.sum(dx1 * x1n, axis=0, keepdims=True)
        acc_ref[3:4, :] += jnp.sum(dx1, axis=0, keepdims=True)
        acc_ref[4:5, :] += jnp.sum(dh1 * m_ref[...].astype(F32), axis=0, keepdims=True)

    row = lambda i: (i, 0)
    fix = lambda i: (0, 0)
    hbm = pl.BlockSpec(memory_space=pl.ANY)
    return pl.pallas_call(
        body, name="ffn_bwd", grid=(t // tm,),
        in_specs=[pl.BlockSpec((tm, D_MODEL), row), pl.BlockSpec((tm, D_FF), row), pl.BlockSpec((tm, D_MODEL), row),
                  pl.BlockSpec((tm, D_MODEL), row), pl.BlockSpec((tm, 1), row), pl.BlockSpec((tm, D_MODEL), row),
                  pl.BlockSpec((8, D_MODEL), fix), hbm, hbm, hbm],
        out_specs=(pl.BlockSpec((tm, D_FF), row), pl.BlockSpec((tm, D_MODEL), row), pl.BlockSpec((tm, D_MODEL), row),
                   pl.BlockSpec((tm, D_MODEL), row), pl.BlockSpec((8, D_MODEL), fix)),
        out_shape=(jax.ShapeDtypeStruct((t, D_FF), BF16), jax.ShapeDtypeStruct((t, D_MODEL), BF16),
                   jax.ShapeDtypeStruct((t, D_MODEL), F32), jax.ShapeDtypeStruct((t, D_MODEL), F32),
                   jax.ShapeDtypeStruct((8, D_MODEL), F32)),
        scratch_shapes=[pltpu.VMEM((D_MODEL, D_MODEL), BF16), pltpu.VMEM((N_DEV, D_MODEL, FF_COLS), BF16),
                        pltpu.VMEM((D_FF, D_MODEL), BF16), pltpu.SemaphoreType.DMA((3,))],
        compiler_params=pltpu.CompilerParams(dimension_semantics=("arbitrary",), vmem_limit_bytes=V7X_VMEM_LIMIT),
    )(df, a, dh2, x1n, rstd1, m, vecs, w_out_b, w1_b, w2_b)


def _matmul_tn(lhs, rhs, tmm, tn, tk, name, relu_sq=False, col_slab=None, out_rows=None):
    t, mm = lhs.shape
    assert out_rows is None or (col_slab is None and tmm == mm)
    nn = rhs.shape[1]
    tk = min(tk, t)
    nk = t // tk

    def body(l_ref, r_ref, o_ref, acc):
        kk = pl.program_id(2)

        @pl.when(kk == 0)
        def _():
            acc[...] = jnp.zeros_like(acc)

        l = l_ref[...]
        if relu_sq:
            lf = jnp.maximum(l.astype(F32), 0.0)
            l = (lf * lf).astype(BF16)
        acc[...] += _dot(l, r_ref[...], TN)

        @pl.when(kk == nk - 1)
        def _():
            if out_rows is not None:
                for s in range(N_DEV):
                    o_ref[s] = acc[s * out_rows:(s + 1) * out_rows, :].astype(o_ref.dtype)
            elif col_slab is None:
                o_ref[...] = acc[...].astype(o_ref.dtype)
            else:
                for s in range(tn // col_slab):
                    o_ref[s] = acc[:, s * col_slab:(s + 1) * col_slab].astype(o_ref.dtype)

    if out_rows is not None:
        out_spec = pl.BlockSpec((N_DEV, out_rows, tn), lambda i, j, k: (0, 0, j))
        out_shape = jax.ShapeDtypeStruct((N_DEV, out_rows, nn), BF16)
    elif col_slab is None:
        out_spec = pl.BlockSpec((tmm, tn), lambda i, j, k: (i, j))
        out_shape = jax.ShapeDtypeStruct((mm, nn), BF16)
    else:
        out_spec = pl.BlockSpec((tn // col_slab, tmm, col_slab), lambda i, j, k: (j, i, 0))
        out_shape = jax.ShapeDtypeStruct((nn // col_slab, mm, col_slab), BF16)
    return pl.pallas_call(
        body, name=name, grid=(mm // tmm, nn // tn, nk),
        in_specs=[pl.BlockSpec((tk, tmm), lambda i, j, k: (k, i)), pl.BlockSpec((tk, tn), lambda i, j, k: (k, j))],
        out_specs=out_spec,
        out_shape=out_shape,
        scratch_shapes=[pltpu.VMEM((tmm, tn), F32)],
        compiler_params=pltpu.CompilerParams(dimension_semantics=("arbitrary", "arbitrary", "arbitrary"),
                                             vmem_limit_bytes=V7X_VMEM_LIMIT),
    )(lhs, rhs)


def _mixer_bwd(dmix, proj, qrb, krb, oraw, tables, rst, sst, gw_pad, gb, rnw, gnw):
    t = proj.shape[0]
    tc = min(MIX_TILE, t)
    tr, tg = min(RET_SUB, tc), min(GLA_SUB, tc)
    nsteps = t // tc
    scale_r = RET_D ** -0.5
    scale_g = GLA_DK ** -0.5
    gammas = _tile_gammas(tr)

    def body(dmix_ref, qrb_ref, krb_ref, rv_ref, rg_ref, gq_ref, gk_ref, gv_ref, gg_ref, glr_ref, oraw_ref,
             dec_ref, qkd_ref, rot_in_ref, rot_tile_ref, rst_ref, sst_ref, gw_ref, gb_ref, rnw_ref, gnw_ref,
             dproj_ref, dgw_ref, dvec_ref, dr_scr, ds_scr):
        @pl.when(pl.program_id(0) == 0)
        def _():
            dr_scr[...] = jnp.zeros_like(dr_scr)
            ds_scr[...] = jnp.zeros_like(ds_scr)
            dgw_ref[...] = jnp.zeros_like(dgw_ref)
            dvec_ref[...] = jnp.zeros_like(dvec_ref)

        gla_k = _gla_consts(tg)
        last_row = lax.broadcasted_iota(jnp.int32, (tg, GLA_KW), 0) == tg - 1

        def ret_tile(jj, carry):
            j = tc // tr - 1 - jj
            rows = _tile_rows(j, tr)
            cosv, sinv = _tile_rotary(rot_in_ref, rot_tile_ref, j)
            for h in range(RET_HEADS):
                cols = slice(h * RET_D, (h + 1) * RET_D)
                o = oraw_ref[rows, cols]
                g = rg_ref[rows, cols]
                w = rnw_ref[:, cols]
                dout = dmix_ref[rows, cols]
                oc = o - jnp.mean(o, axis=-1, keepdims=True)
                inv = lax.rsqrt(jnp.mean(oc * oc, axis=-1, keepdims=True) + LN_EPS)
                n = oc * inv
                sg = _sigmoid(g)
                sil = g * sg
                dn = dout * w * sil
                dvec_ref[0:1, cols] += jnp.sum(dout * n * sil, axis=0, keepdims=True)
                dproj_ref[rows, OFF_RG + h * RET_D:OFF_RG + (h + 1) * RET_D] = (
                    dout * n * w * (sg * (1.0 + g * (1.0 - sg)))).astype(BF16)
                doc = inv * (dn - n * jnp.mean(dn * n, axis=-1, keepdims=True))
                do = doc - jnp.mean(doc, axis=-1, keepdims=True)

                qb, kb = qrb_ref[rows, cols], krb_ref[rows, cols]
                qr, kr = qb.astype(F32), kb.astype(F32)
                vb = rv_ref[rows, cols].astype(BF16)
                dob = do.astype(BF16)
                qd, kd = qkd_ref[h], qkd_ref[RET_HEADS + h]
                p = _dot(qb, kb, NT) * dec_ref[h]
                rp = rst_ref[j, cols, :].astype(BF16)
                dr = dr_scr[cols, :]
                drb = dr.astype(BF16)
                dpb = (_dot(dob, vb, NT) * dec_ref[h]).astype(BF16)
                dqr = _dot(dpb, kb) + _dot(dob, rp, NT) * qd
                dkr = _dot(dpb, qb, TN) + _dot(vb, drb, NT) * kd
                dv = _dot(p.astype(BF16), dob, TN) + _dot((kr * kd).astype(BF16), drb)
                dr_scr[cols, :] = gammas[h] * dr + _dot((qr * qd).astype(BF16), dob, TN)
                dproj_ref[rows, OFF_RQ + h * RET_D:OFF_RQ + (h + 1) * RET_D] = (
                    _rotate_t(dqr, cosv, sinv) * scale_r).astype(BF16)
                dproj_ref[rows, OFF_RK + h * RET_D:OFF_RK + (h + 1) * RET_D] = _rotate_t(dkr, cosv, sinv).astype(BF16)
                dproj_ref[rows, OFF_RV + h * RET_D:OFF_RV + (h + 1) * RET_D] = dv.astype(BF16)
            return carry

        def gla_tile(jj, carry):
            k = gla_k
            tl = tg
            j = tc // tg - 1 - jj
            rows = _tile_rows(j, tg)
            glr = glr_ref[rows, :]
            z, b, bl, ep, em = _gla_gates(glr, gw_ref[...], gb_ref[...], k["ltri"], tl)
            qs = gq_ref[rows, :] * scale_g
            kk = gk_ref[rows, :]
            eb = jnp.exp(b)
            ekb = jnp.exp(bl - b)
            ebl = jnp.exp(bl)
            ql, qu, kl, ku = qs * ep, qs * em, kk * em, kk * ep
            qg, kg = qs * eb, kk * ekb
            qlm = _stack_heads(ql, k["hmask"]).astype(BF16)
            qum = _stack_heads(qu, k["hmask"]).astype(BF16)
            klb, kub = kl.astype(BF16), ku.astype(BF16)
            a_all = jnp.where(k["lower"], _dot(qlm, klb, NT),
                              jnp.where(k["upper"], _dot(qum, kub, NT), 0.0)).astype(BF16)
            st = sst_ref[j]
            stb = st.astype(BF16)
            ds = ds_scr[...]
            dsb = ds.astype(BF16)
            ds_new = ds * ebl
            da_parts = []
            dqg = jnp.zeros((tl, GLA_KW), F32)
            dkg = jnp.zeros((tl, GLA_KW), F32)
            for h in range(GLA_HEADS):
                cols = slice(h * GLA_DV, (h + 1) * GLA_DV)
                hr = slice(h * tl, (h + 1) * tl)
                ocols = slice(RET_W + h * GLA_DV, RET_W + (h + 1) * GLA_DV)
                o = oraw_ref[rows, ocols]
                g = gg_ref[rows, cols]
                w = gnw_ref[:, cols]
                dout = dmix_ref[rows, ocols]
                inv = lax.rsqrt(jnp.mean(o * o, axis=-1, keepdims=True) + LN_EPS)
                n = o * inv
                sg = _sigmoid(g)
                sil = g * sg
                dn = dout * w * sil
                dvec_ref[1:2, cols] += jnp.sum(dout * n * sil, axis=0, keepdims=True)
                dproj_ref[rows, OFF_GG + h * GLA_DV:OFF_GG + (h + 1) * GLA_DV] = (
                    dout * n * w * (sg * (1.0 + g * (1.0 - sg)))).astype(BF16)
                dob = (inv * (dn - n * jnp.mean(dn * n, axis=-1, keepdims=True))).astype(BF16)
                vb = gv_ref[rows, cols].astype(BF16)
                mh = k["hmask"][h]
                da_parts.append(_dot(dob, vb, NT))
                dv = _dot(a_all[hr, :], dob, TN) + _dot((kg * mh).astype(BF16), dsb, NT)
                dproj_ref[rows, OFF_GV + h * GLA_DV:OFF_GV + (h + 1) * GLA_DV] = dv.astype(BF16)
                dkg = dkg + mh * _dot(vb, dsb)
                dqg = dqg + mh * _dot(dob, stb)
                ds_new = ds_new + _dot(dob, (qg * mh).astype(BF16), TN)
            da_all = jnp.concatenate(da_parts, axis=0)
            dal = jnp.where(k["lower"], da_all, 0.0).astype(BF16)
            dau = jnp.where(k["upper"], da_all, 0.0).astype(BF16)
            dqlm = _dot(dal, klb)
            dqum = _dot(dau, kub)
            dql = jnp.zeros((tl, GLA_KW), F32)
            dqu = jnp.zeros((tl, GLA_KW), F32)
            for h in range(GLA_HEADS):
                hr = slice(h * tl, (h + 1) * tl)
                dql = dql + k["hmask"][h] * dqlm[hr, :]
                dqu = dqu + k["hmask"][h] * dqum[hr, :]
            dkl = _dot(dal, qlm, TN)
            dku = _dot(dau, qum, TN)
            dbl = (jnp.sum(dkg * kg, axis=0, keepdims=True)
                   + jnp.sum(ds * st, axis=0, keepdims=True) * ebl)
            ds_scr[...] = ds_new
            dqs = dql * ep + dqu * em + dqg * eb
            dk = dkl * em + dku * ep + dkg * ekb
            db = dql * ql - dkl * kl - dqu * qu + dku * ku + dqg * qg - dkg * kg
            db = db + jnp.where(last_row, dbl, 0.0)
            dla = _dot_split(k["utri"], db, NN, a_exact=True)
            dz = dla * (1.0 / GATE_TAU) * _sigmoid(-z)
            dvec_ref[2:3, 0:GLA_KW] += jnp.sum(dz, axis=0, keepdims=True)
            dgw_ref[...] += _dot_split(glr, dz, TN)
            dproj_ref[rows, OFF_GLR:D_IN_PAD] = _dot(dz.astype(BF16), gw_ref[...].astype(BF16), NT).astype(BF16)
            dproj_ref[rows, OFF_GQ:OFF_GQ + GLA_KW] = (dqs * scale_g).astype(BF16)
            dproj_ref[rows, OFF_GK:OFF_GK + GLA_KW] = dk.astype(BF16)
            return carry

        _for_tiles(tc // tr, ret_tile)
        _for_tiles(tc // tg, gla_tile)

    rev = lambda i: (nsteps - 1 - i, 0)

    def col(width, off):
        return pl.BlockSpec((tc, width), lambda i, o=off // width: (nsteps - 1 - i, o))

    fix = lambda i: (0, 0)
    fix3 = lambda i: (0, 0, 0)
    dec, qkd, rot_in, rot_tile = tables
    half = pl.BlockSpec((tc, RET_W), rev)
    in_specs = [pl.BlockSpec((tc, D_MODEL), rev), half, half, col(RET_W, OFF_RV), col(RET_W, OFF_RG),
                col(GLA_KW, OFF_GQ), col(GLA_KW, OFF_GK), col(GLA_VW, OFF_GV), col(GLA_VW, OFF_GG),
                col(V7X_LANES, OFF_GLR),
                pl.BlockSpec((tc, D_MODEL), rev),
                pl.BlockSpec(dec.shape, fix3), pl.BlockSpec(qkd.shape, fix3), pl.BlockSpec(rot_in.shape, fix3),
                pl.BlockSpec((tc // tr, 8, 2 * RET_D), lambda i: (nsteps - 1 - i, 0, 0)),
                pl.BlockSpec((tc // tr, RET_W, RET_D), lambda i: (nsteps - 1 - i, 0, 0)),
                pl.BlockSpec((tc // tg, GLA_DV, GLA_KW), lambda i: (nsteps - 1 - i, 0, 0)),
                pl.BlockSpec((V7X_LANES, GLA_KW), fix), pl.BlockSpec((1, GLA_KW), fix),
                pl.BlockSpec((1, RET_W), fix), pl.BlockSpec((1, GLA_VW), fix)]
    out_specs = (pl.BlockSpec((tc, D_IN_PAD), rev), pl.BlockSpec((V7X_LANES, GLA_KW), fix),
                 pl.BlockSpec((8, RET_W), fix))
    out_shape = (jax.ShapeDtypeStruct((t, D_IN_PAD), BF16), jax.ShapeDtypeStruct((V7X_LANES, GLA_KW), F32),
                 jax.ShapeDtypeStruct((8, RET_W), F32))
    return pl.pallas_call(
        body, name="mixer_bwd", grid=(nsteps,), in_specs=in_specs, out_specs=out_specs, out_shape=out_shape,
        scratch_shapes=[pltpu.VMEM((RET_W, RET_D), F32), pltpu.VMEM((GLA_DV, GLA_KW), F32)],
        compiler_params=pltpu.CompilerParams(dimension_semantics=("arbitrary",), vmem_limit_bytes=V7X_VMEM_LIMIT),
    )(dmix, qrb, krb, *([proj] * 7), oraw, dec, qkd, rot_in, rot_tile, rst, sst, gw_pad, gb, rnw, gnw)


def _inproj_bwd(dproj, x2d, dxa, sc1p, w_in_t):
    t = x2d.shape[0]
    tm = min(2 * PROJ_TILE, t)

    def body(dp_ref, x_ref, dxa_ref, sc_ref, w_hbm, gx_ref, acc_ref, w_vmem, sem):
        first = pl.program_id(0) == 0
        _load_w_in_t(first, w_hbm, w_vmem, sem)

        @pl.when(first)
        def _():
            acc_ref[...] = jnp.zeros_like(acc_ref)

        du = _dot(dp_ref[...], w_vmem[...])
        xh, rstd = _ln_stats(x_ref[...])
        gx_ref[...] = dxa_ref[...] + _ln_bwd(du * sc_ref[...], xh, rstd)
        acc_ref[0:1, :] += jnp.sum(du * xh, axis=0, keepdims=True)
        acc_ref[1:2, :] += jnp.sum(du, axis=0, keepdims=True)

    row = lambda i: (i, 0)
    fix = lambda i: (0, 0)
    return pl.pallas_call(
        body, name="inproj_bwd", grid=(t // tm,),
        in_specs=[pl.BlockSpec((tm, D_IN_PAD), row), pl.BlockSpec((tm, D_MODEL), row), pl.BlockSpec((tm, D_MODEL), row),
                  pl.BlockSpec((1, D_MODEL), fix), pl.BlockSpec(memory_space=pl.ANY)],
        out_specs=(pl.BlockSpec((tm, D_MODEL), row), pl.BlockSpec((8, D_MODEL), fix)),
        out_shape=(jax.ShapeDtypeStruct((t, D_MODEL), F32), jax.ShapeDtypeStruct((8, D_MODEL), F32)),
        scratch_shapes=[pltpu.VMEM((D_IN_PAD, D_MODEL), BF16), pltpu.SemaphoreType.DMA((1,))],
        compiler_params=pltpu.CompilerParams(dimension_semantics=("arbitrary",), vmem_limit_bytes=V7X_VMEM_LIMIT),
    )(dproj, x2d, dxa, sc1p, w_in_t)


def _adam_math(w, g, m, v):
    m = ADAM_B1 * m + (1.0 - ADAM_B1) * g
    v = ADAM_B2 * v + (1.0 - ADAM_B2) * (g * g)
    m_hat = m / (1.0 - ADAM_B1 ** ADAM_STEP)
    v_hat = v / (1.0 - ADAM_B2 ** ADAM_STEP)
    delta = -ADAM_LR * (m_hat / (jnp.sqrt(v_hat) + ADAM_EPS) + ADAM_WD * w)
    return delta, m, v


def _adamw(w, gparts, m, v, name):
    nparts, rows, cols = gparts.shape
    tr = rows
    for cand in (512, 256, 128, 64, 32, 16, 8):
        if rows % cand == 0:
            tr = cand
            break

    def body(w_ref, g_ref, m_ref, v_ref, go_ref, d_ref, mo_ref, vo_ref):
        g = g_ref[0].astype(F32)
        for p in range(1, nparts):
            g = g + g_ref[p].astype(F32)
        delta, mn, vn = _adam_math(w_ref[...], g, m_ref[...], v_ref[...])
        go_ref[...] = g
        d_ref[...] = delta
        mo_ref[...] = mn
        vo_ref[...] = vn

    blk = pl.BlockSpec((tr, cols), lambda i: (i, 0))
    shp = jax.ShapeDtypeStruct((rows, cols), F32)
    return pl.pallas_call(
        body, name=name, grid=(rows // tr,),
        in_specs=[blk, pl.BlockSpec((nparts, tr, cols), lambda i: (0, i, 0)), blk, blk],
        out_specs=(blk, blk, blk, blk), out_shape=(shp, shp, shp, shp),
        compiler_params=pltpu.CompilerParams(dimension_semantics=("arbitrary",), vmem_limit_bytes=V7X_VMEM_LIMIT),
    )(w, gparts, m, v)


def _small_reduce(gathered, gathered_gw, c_all, dmod_cols):
    def body(g_ref, gw_ref, c_ref, dm_ref, sum_ref, gwsum_ref, gb_ref, gwa_ref):
        s = g_ref[0]
        sw = gw_ref[0]
        for p in range(1, N_DEV):
            s = s + g_ref[p]
            sw = sw + gw_ref[p]
        sum_ref[...] = s
        gwsum_ref[...] = sw
        for i in range(6):
            gb_ref[:, i * D_MODEL:(i + 1) * D_MODEL] = s[i:i + 1, :]
        cc = c_ref[...]
        gwa_ref[...] = _dot(cc * _sigmoid(cc), dm_ref[...], TN, HIGHEST)

    vm = pl.BlockSpec(memory_space=pltpu.VMEM)
    return pl.pallas_call(
        body, name="small_reduce",
        out_shape=(jax.ShapeDtypeStruct(gathered.shape[1:], F32), jax.ShapeDtypeStruct(gathered_gw.shape[1:], F32),
                   jax.ShapeDtypeStruct((1, 6 * D_MODEL), F32), jax.ShapeDtypeStruct((D_MODEL, ADA_COLS), F32)),
        in_specs=[vm] * 4, out_specs=(vm, vm, vm, vm),
        compiler_params=pltpu.CompilerParams(vmem_limit_bytes=V7X_VMEM_LIMIT),
    )(gathered, gathered_gw, c_all, dmod_cols)


SMR_LN1W, SMR_LN1B, SMR_LN2W, SMR_LN2B, SMR_NORMS, SMR_MISC = 6, 7, 8, 9, 10, 11


def _adamw_small(gsum, g_b_ada, g_ggw, params, moms, vels):
    n = len(params)

    def body(*refs):
        gsum_ref, gb_ref, gw_ref = refs[:3]
        w_refs, m_refs, v_refs = refs[3:3 + n], refs[3 + n:3 + 2 * n], refs[3 + 2 * n:3 + 3 * n]
        outs = refs[3 + 3 * n:]
        g_refs, d_refs, mo_refs, vo_refs = outs[:n - 1], outs[n - 1:2 * n - 1], outs[2 * n - 1:3 * n - 1], outs[3 * n - 1:]
        grads = [gb_ref[...],
                 gsum_ref[SMR_NORMS:SMR_NORMS + 1, 0:RET_W],
                 gsum_ref[SMR_MISC:SMR_MISC + 1, 0:GLA_KW],
                 gsum_ref[SMR_NORMS:SMR_NORMS + 1, RET_W:RET_W + GLA_VW],
                 gsum_ref[SMR_LN1W:SMR_LN1W + 1, :], gsum_ref[SMR_LN1B:SMR_LN1B + 1, :],
                 gsum_ref[SMR_LN2W:SMR_LN2W + 1, :], gsum_ref[SMR_LN2B:SMR_LN2B + 1, :],
                 gw_ref[...]]
        for i in range(n):
            delta, mn, vn = _adam_math(w_refs[i][...], grads[i], m_refs[i][...], v_refs[i][...])
            if i < n - 1:
                g_refs[i][...] = grads[i]
            d_refs[i][...] = delta
            mo_refs[i][...] = mn
            vo_refs[i][...] = vn

    vm = pl.BlockSpec(memory_space=pltpu.VMEM)
    shapes = [jax.ShapeDtypeStruct(p.shape, F32) for p in params]
    n_in = 3 + 3 * n
    out_shape = tuple(shapes[:n - 1] + shapes * 3)
    return pl.pallas_call(
        body, name="adamw_small", out_shape=out_shape,
        in_specs=[vm] * n_in, out_specs=tuple([vm] * len(out_shape)),
        compiler_params=pltpu.CompilerParams(vmem_limit_bytes=V7X_VMEM_LIMIT),
    )(gsum, g_b_ada, g_ggw, *params, *moms, *vels)


def kernel(x, c, w_ada, b_ada, w_in, ret_norm_w, gla_gate_w, gla_gate_b, gla_norm_w, w_out, ln1_w, ln1_b, w_ff1, w_ff2, ln2_w, ln2_b, loss_target, m_w_ada, m_b_ada, m_w_in, m_ret_norm_w, m_gla_gate_w, m_gla_gate_b, m_gla_norm_w, m_w_out, m_ln1_w, m_ln1_b, m_w_ff1, m_w_ff2, m_ln2_w, m_ln2_b, v_w_ada, v_b_ada, v_w_in, v_ret_norm_w, v_gla_gate_w, v_gla_gate_b, v_gla_norm_w, v_w_out, v_ln1_w, v_ln1_b, v_w_ff1, v_w_ff2, v_ln2_w, v_ln2_b):
    t = x.shape[1]
    xi, yi, ci = _my_coords()
    me = 4 * xi + 2 * yi + ci
    x2d = x[0]
    tgt = loss_target[0]

    c_ext = jnp.concatenate([c, gla_gate_w[0].reshape(1, GATE_RANK * GLA_KW // N_DEV)], axis=1)
    b_l = lax.dynamic_slice(b_ada, (0, me * ADA_COLS), (1, ADA_COLS))
    c_all3, mod_all, wi_g, ada_token = _adaln_mod(c_ext, w_ada[0], b_l, w_in[0].T.astype(BF16))

    wg = _exchange_start([(w_out[0] + ada_token[0, 0]).astype(BF16), w_ff1[0].astype(BF16), w_ff2[0].astype(BF16)],
                         True, "wgather_start")

    c_all = c_all3[:, 0, :D_MODEL]
    gate_w = c_all3[:, 0, D_MODEL:].reshape(N_DEV, GATE_RANK, GLA_KW // N_DEV)
    gate_w = gate_w.transpose(1, 0, 2).reshape(GATE_RANK, GLA_KW)
    gw_pad = jnp.zeros((V7X_LANES, GLA_KW), F32).at[:GATE_RANK].set(gate_w)
    mod = lax.dynamic_slice(mod_all, (0, me, 0), (N_DEV, 1, ADA_COLS)).reshape(6, D_MODEL)
    shift1, scale1, gate1, shift2, scale2, gate2 = [mod[i:i + 1] for i in range(6)]

    w_in_t = wi_g.reshape(D_IN, D_MODEL)

    tables = _ret_tables(t, min(RET_SUB, t))

    sc1p = 1.0 + scale1
    proj, u = _inproj_fwd(x2d, sc1p, shift1 + wg[4][0, 0], w_in_t)
    mixed, oraw, qrb, krb, rst, sst = _mixer_fwd(proj, tables, gw_pad, gla_gate_b, ret_norm_w, gla_norm_w)
    wo_g, w1_b, w2_g = _exchange_wait(*wg[:4], mixed, True, "wgather_wait")
    w_out_b = wo_g.reshape(D_MODEL, D_MODEL)
    w2_b = w2_g.reshape(D_FF, D_MODEL)
    vec_f = jnp.concatenate([gate1, 1.0 + scale2, shift2, gate2, ln1_w, ln1_b, ln2_w, ln2_b], axis=0)
    m, x1n, rstd1, u2, a, df, dh2, acc_f = _mid_fwd(mixed, x2d, tgt, vec_f, w_out_b, w1_b, w2_b)

    vec_b = jnp.concatenate([gate1, 1.0 + scale2, ln1_w, ln1_b, jnp.zeros((4, D_MODEL), F32)], axis=0)
    da, dm, dmix, dxa, acc_b = _ffn_bwd(df, a, dh2, x1n, rstd1, m, vec_b, w_out_b, w1_b, w2_b)
    dw2 = _matmul_tn(a, df, 2048, 1024, 2048, "tn_dw2", relu_sq=True)
    dw1 = _matmul_tn(u2, da, 1024, 2048, 2048, "tn_dw1", col_slab=FF_COLS)
    dwo = _matmul_tn(mixed, dm, 1024, 1024, 2048, "tn_dwout")
    gx = _exchange_start([dwo.reshape(N_DEV, OUT_ROWS, D_MODEL), dw1, dw2.reshape(N_DEV, FF_COLS, D_MODEL)], False,
                         "gradx_start")
    dproj, dgw, dvec = _mixer_bwd(dmix, proj, qrb, krb, oraw, tables, rst, sst, gw_pad,
                                  gla_gate_b + gx[4][0, 0], ret_norm_w, gla_norm_w)
    dwi_s = _matmul_tn(dproj, u, D_IN_PAD, 1024, 1024, "tn_dwin", out_rows=IN_COLS)
    gi = _exchange_start([dwi_s], False, "gradin_start")
    grad_x, acc_i = _inproj_bwd(dproj, x2d, dxa, sc1p + gi[4][0, 0], w_in_t)

    loss_part = jnp.sum(acc_f[3])
    small = jnp.concatenate([
        acc_i[1:2], acc_i[0:1], acc_b[4:5], acc_b[1:2], acc_b[0:1], acc_f[2:3],
        acc_b[2:3], acc_b[3:4], acc_f[0:1], acc_f[1:2],
        jnp.concatenate([dvec[0:1], dvec[1:2]], axis=1),
        jnp.concatenate([dvec[2:3, :GLA_KW], jnp.full((1, 128), loss_part, F32),
                         jnp.zeros((1, D_MODEL - GLA_KW - 128), F32)], axis=1),
        jnp.zeros((4, D_MODEL), F32)], axis=0)
    sg = _exchange_start([small, dgw[:GATE_RANK]], True, "small_start")

    r_wo, r_w1, r_w2 = _exchange_wait(*gx[:4], sg[4], False, "gradx_wait")
    r_wi, = _exchange_wait(*gi[:4], sg[4], False, "gradin_wait")
    big = [_adamw(w[0], r, m_[0], v_[0], nm) for w, r, m_, v_, nm in (
        (w_out, r_wo, m_w_out, v_w_out, "adamw_out"),
        (w_ff1, r_w1, m_w_ff1, v_w_ff1, "adamw_ff1"), (w_ff2, r_w2, m_w_ff2, v_w_ff2, "adamw_ff2"))]
    big_in = _adamw(w_in[0].T, r_wi, m_w_in[0].T, v_w_in[0].T, "adamw_in")
    big = [tuple(b.T for b in big_in)] + big
    g_big, d_big, m_big, v_big = [[b[i][None] for b in big] for i in range(4)]

    small_all, gw_all = _exchange_wait(*sg[:4], big_in[1], True, "small_wait")
    dmod_all = small_all[:, :6].reshape(N_DEV, 6 * D_MODEL)
    dmod_cols = lax.dynamic_slice(dmod_all, (0, me * ADA_COLS), (N_DEV, ADA_COLS))
    ssum, gw_sum, g_b_ada, g_w_ada = _small_reduce(small_all, gw_all, c_all, dmod_cols)
    loss = ssum[SMR_MISC, GLA_KW]
    g_ggw = lax.dynamic_slice(gw_sum, (0, me * (GLA_KW // N_DEV)), (GATE_RANK, GLA_KW // N_DEV))[None]

    small_w = [b_ada, ret_norm_w, gla_gate_b, gla_norm_w, ln1_w, ln1_b, ln2_w, ln2_b, gla_gate_w]
    small_m = [m_b_ada, m_ret_norm_w, m_gla_gate_b, m_gla_norm_w, m_ln1_w, m_ln1_b, m_ln2_w, m_ln2_b, m_gla_gate_w]
    small_v = [v_b_ada, v_ret_norm_w, v_gla_gate_b, v_gla_norm_w, v_ln1_w, v_ln1_b, v_ln2_w, v_ln2_b, v_gla_gate_w]
    res = _adamw_small(ssum, g_b_ada, g_ggw, small_w, small_m, small_v)
    small_g = list(res[:8]) + [g_ggw]
    d_small, m_small, v_small = list(res[8:17]), list(res[17:26]), list(res[26:35])

    _, d_w_ada, nm_w_ada, nv_w_ada = _adamw(w_ada[0], g_w_ada[None], m_w_ada[0], v_w_ada[0], "adamw_ada")

    def ordered(w_ada_v, small_vals, big_vals):
        b_ada_v, rnw_v, ggb_v, gnw_v, l1w_v, l1b_v, l2w_v, l2b_v, ggw_v = small_vals
        wi_v, wo_v, w1_v, w2_v = big_vals
        return [w_ada_v, b_ada_v, wi_v, rnw_v, ggw_v, ggb_v, gnw_v, wo_v, l1w_v, l1b_v, w1_v, w2_v, l2w_v, l2b_v]

    grads = ordered(g_w_ada[None], small_g, g_big)
    deltas = ordered(d_w_ada[None], d_small, d_big)
    new_m = ordered(nm_w_ada[None], m_small, m_big)
    new_v = ordered(nv_w_ada[None], v_small, v_big)
    return (loss, grad_x[None], *grads, *deltas, *new_m, *new_v)
```

```python
import numpy as np
import jax
import jax.numpy as jnp
from jax import lax
from jax.experimental import pallas as pl
from jax.experimental.pallas import tpu as pltpu

F32 = jnp.float32
BF16 = jnp.bfloat16
MESH = pl.DeviceIdType.MESH
HIGHEST = lax.Precision.HIGHEST

N_DEV = 8
D_MODEL = 1024
CHUNK = 64
RET_HEADS = 4
RET_D = 128
GLA_HEADS = 4
GLA_DK = 64
GLA_DV = 128
GLA_KW = GLA_HEADS * GLA_DK
RET_W = RET_HEADS * RET_D
GLA_VW = GLA_HEADS * GLA_DV
V7X_LANES = 128
GATE_RANK = 16
GATE_TAU = 16.0
D_FF = 4096
LN_EPS = 1e-5
ALPHA = (2.0 * 1) ** 0.25
D_IN = 3600
D_IN_PAD = 3712
ADA_COLS = 6 * D_MODEL // N_DEV
IN_COLS = D_IN // N_DEV
FF_COLS = D_FF // N_DEV
OUT_ROWS = D_MODEL // N_DEV

OFF_RQ, OFF_RK, OFF_RV, OFF_RG = 0, RET_W, 2 * RET_W, 3 * RET_W
OFF_GQ = 4 * RET_W
OFF_GK = OFF_GQ + GLA_KW
OFF_GV = OFF_GK + GLA_KW
OFF_GG = OFF_GV + GLA_VW
OFF_GLR = OFF_GG + GLA_VW

ADAM_LR, ADAM_B1, ADAM_B2, ADAM_EPS, ADAM_WD, ADAM_STEP = 0.001, 0.9, 0.999, 1e-08, 0.01, 10

V7X_VMEM_LIMIT = 62 * 1024 * 1024

ROW_TILE = 512
PROJ_TILE = 512
MIX_TILE = 512
RET_SUB = 256
GLA_SUB = 128


def _log_gamma(h):
    return float(np.log(np.float32(1.0) - np.float32(2.0) ** np.float32(-5.0 - h)))


def _my_coords():
    return lax.axis_index("x"), lax.axis_index("y"), lax.axis_index("c")


def _flip(v, bit):
    return 1 - v if bit else v


def _peer(k):
    x, y, c = _my_coords()
    px, py, pc = _flip(x, (k >> 2) & 1), _flip(y, (k >> 1) & 1), _flip(c, k & 1)
    return (px, py, pc), 4 * px + 2 * py + pc


def _dot(a, b, dims=(((1,), (0,)), ((), ())), precision=None):
    return lax.dot_general(a, b, dims, precision=precision, preferred_element_type=F32)


NN = (((1,), (0,)), ((), ()))
NT = (((1,), (1,)), ((), ()))
TN = (((0,), (0,)), ((), ()))


def _split_bf16(v, parts):
    out = []
    for _ in range(parts):
        p = v.astype(BF16)
        out.append(p)
        v = v - p.astype(F32)
    return out


def _dot_split(a, b, dims, a_exact=False):
    if a_exact:
        ab = a.astype(BF16)
        return sum(_dot(ab, p, dims) for p in _split_bf16(b, 2))
    a_hi, a_lo = _split_bf16(a, 2)
    b_hi, b_lo = _split_bf16(b, 2)
    return _dot(a_hi, b_hi, dims) + _dot(a_hi, b_lo, dims) + _dot(a_lo, b_hi, dims)


def _sigmoid(x):
    return 1.0 / (1.0 + jnp.exp(-x))


def _ln_stats(x):
    mu = jnp.mean(x, axis=-1, keepdims=True)
    xc = x - mu
    var = jnp.mean(xc * xc, axis=-1, keepdims=True)
    rstd = lax.rsqrt(var + LN_EPS)
    return xc * rstd, rstd


def _ln_bwd(dyh, xh, rstd):
    return rstd * (dyh - jnp.mean(dyh, axis=-1, keepdims=True) - xh * jnp.mean(dyh * xh, axis=-1, keepdims=True))


def _adaln_mod(c_ext, w_ada_l, b_l, w_in_l):
    width = c_ext.shape[1]

    def body(c_ref, w_ref, b_ref, wi_ref, call_ref, mod_ref, wig_ref, token_ref, s1, r1, s2, r2, gs, gr, gl):
        gather = _TwoLevelGather([wi_ref], [wig_ref], gs, gr, gl)
        gather.start()
        token_ref[...] = jnp.zeros_like(token_ref)
        x, y, c = _my_coords()
        me = 4 * x + 2 * y + c
        call_ref[me] = c_ref[...]
        sends = []
        for k in range(1, N_DEV):
            peer, _ = _peer(k)
            cp = pltpu.make_async_remote_copy(c_ref, call_ref.at[me], s1.at[k - 1], r1.at[k - 1],
                                              device_id=peer, device_id_type=MESH)
            cp.start()
            sends.append(cp)
        for k in range(1, N_DEV):
            peer, pid = _peer(k)
            pltpu.make_async_remote_copy(c_ref, call_ref.at[pid], s1.at[k - 1], r1.at[k - 1],
                                         device_id=peer, device_id_type=MESH).wait_recv()
        for cp in sends:
            cp.wait_send()
        row = lax.broadcasted_iota(jnp.int32, (N_DEV, D_MODEL), 0)
        call = jnp.zeros((N_DEV, D_MODEL), F32)
        for j in range(N_DEV):
            call = jnp.where(row == j, jnp.broadcast_to(call_ref[j][:, :D_MODEL], (N_DEV, D_MODEL)), call)
        sc = call * _sigmoid(call)
        mod = _dot(sc, w_ref[...], NN, HIGHEST) + b_ref[...]
        mod_ref[me] = mod
        sends = []
        for k in range(1, N_DEV):
            peer, _ = _peer(k)
            cp = pltpu.make_async_remote_copy(mod_ref.at[me], mod_ref.at[me], s2.at[k - 1], r2.at[k - 1],
                                              device_id=peer, device_id_type=MESH)
            cp.start()
            sends.append(cp)
        for k in range(1, N_DEV):
            peer, pid = _peer(k)
            pltpu.make_async_remote_copy(mod_ref.at[pid], mod_ref.at[pid], s2.at[k - 1], r2.at[k - 1],
                                         device_id=peer, device_id_type=MESH).wait_recv()
        for cp in sends:
            cp.wait_send()
        gather.forward()
        gather.finish()

    vm = pl.BlockSpec(memory_space=pltpu.VMEM)
    hbm = pl.BlockSpec(memory_space=pl.ANY)
    return pl.pallas_call(
        body, name="adaln_mod",
        out_shape=(jax.ShapeDtypeStruct((N_DEV, 1, width), F32),
                   jax.ShapeDtypeStruct((N_DEV, N_DEV, ADA_COLS), F32),
                   jax.ShapeDtypeStruct((N_DEV, *w_in_l.shape), w_in_l.dtype),
                   jax.ShapeDtypeStruct((8, 128), F32)),
        in_specs=[vm, vm, vm, hbm], out_specs=(vm, vm, hbm, vm),
        scratch_shapes=[pltpu.SemaphoreType.DMA((N_DEV - 1,))] * 4
        + [pltpu.SemaphoreType.DMA((7,)), pltpu.SemaphoreType.DMA((7,)), pltpu.SemaphoreType.DMA((1,))],
        compiler_params=pltpu.CompilerParams(vmem_limit_bytes=V7X_VMEM_LIMIT),
    )(c_ext, w_ada_l, b_l, w_in_l)


class _TwoLevelGather:
    def __init__(self, x_refs, out_refs, send_sems, recv_sems, local_sems):
        self.x_refs, self.out_refs = x_refs, out_refs
        self.send_sems, self.recv_sems, self.local_sems = send_sems, recv_sems, local_sems
        x, y, c = _my_coords()
        self.c = c
        self.me, self.sibling = (x, y, c), (x, y, 1 - c)
        self.chips = [(1 - x, y), (x, 1 - y), (1 - x, 1 - y)]

    def _copy(self, a, k, block, to, src=None):
        px, py, pc = block
        slab = self.out_refs[a].at[4 * px + 2 * py + pc]
        return pltpu.make_async_remote_copy(
            src_ref=slab if src is None else src, dst_ref=slab,
            send_sem=self.send_sems.at[7 * a + k], recv_sem=self.recv_sems.at[7 * a + k],
            device_id=to, device_id_type=MESH)

    def _mine(self, a):
        px, py, pc = self.me
        return pltpu.make_async_copy(self.x_refs[a], self.out_refs[a].at[4 * px + 2 * py + pc], self.local_sems.at[a])

    def _first(self, a):
        cps = [self._copy(a, 0, self.me, self.sibling, src=self.x_refs[a])]
        cps += [self._copy(a, 1 + j, self.me, (*chip, self.c), src=self.x_refs[a]) for j, chip in enumerate(self.chips)]
        return cps

    def _passed(self, a):
        return [self._copy(a, 4 + j, (*chip, self.c), self.sibling) for j, chip in enumerate(self.chips)]

    def start(self):
        for a in range(len(self.x_refs)):
            self._mine(a).start()
            for cp in self._first(a):
                cp.start()

    def forward(self):
        for a in range(len(self.x_refs)):
            passed = self._passed(a)
            for j, chip in enumerate(self.chips):
                self._copy(a, 1 + j, (*chip, self.c), self.me).wait_recv()
                passed[j].start()

    def finish(self):
        for a in range(len(self.x_refs)):
            self._copy(a, 0, self.sibling, self.me).wait_recv()
            for j, chip in enumerate(self.chips):
                self._copy(a, 4 + j, (*chip, 1 - self.c), self.me).wait_recv()
            for cp in self._first(a) + self._passed(a):
                cp.wait_send()
            self._mine(a).wait()


def _exchange_copy(src_refs, land_refs, send_sems, recv_sems, a, k, gather, receiving):
    x, y, c = _my_coords()
    me = 4 * x + 2 * y + c
    peer, pid = _peer(k)
    src = src_refs[a] if gather else src_refs[a].at[pid]
    dst = land_refs[a].at[pid if receiving else me]
    return pltpu.make_async_remote_copy(src, dst, send_sems.at[7 * a + k - 1], recv_sems.at[7 * a + k - 1],
                                        device_id=peer, device_id_type=MESH)


def _exchange_start(srcs, gather, name):
    n = len(srcs)
    xi, yi, ci = _my_coords()
    me = 4 * xi + 2 * yi + ci
    lands = []
    for s in srcs:
        own = s[None] if gather else lax.dynamic_slice_in_dim(s, me, 1, axis=0)
        lands.append(lax.dynamic_update_slice_in_dim(lax.empty((N_DEV, *own.shape[1:]), s.dtype), own, me, axis=0))

    def body(*refs):
        src_refs, land_refs, send_sems, recv_sems, token = refs[:n], refs[n:2 * n], refs[2 * n], refs[2 * n + 1], refs[-1]
        for a in range(n):
            for k in range(1, N_DEV):
                _exchange_copy(src_refs, land_refs, send_sems, recv_sems, a, k, gather, receiving=False).start()
        token[...] = jnp.zeros_like(token)

    hbm = pl.BlockSpec(memory_space=pltpu.HBM)
    sem = pl.BlockSpec(memory_space=pltpu.SEMAPHORE)
    res = pl.pallas_call(
        body, name=name,
        out_shape=(pltpu.SemaphoreType.DMA((7 * n,)), pltpu.SemaphoreType.DMA((7 * n,)),
                   *[pltpu.HBM(v.shape, v.dtype) for v in srcs + lands], jax.ShapeDtypeStruct((8, 128), F32)),
        in_specs=[hbm] * (2 * n),
        out_specs=(sem, sem, *([hbm] * (2 * n)), pl.BlockSpec(memory_space=pltpu.VMEM)),
        input_output_aliases={i: 2 + i for i in range(2 * n)},
        compiler_params=pltpu.CompilerParams(has_side_effects=pltpu.SideEffectType.DATAFLOW_SIDE_EFFECTING),
    )(*[pltpu.with_memory_space_constraint(v, pltpu.HBM) for v in srcs + lands])
    return res[0], res[1], list(res[2:2 + n]), list(res[2 + n:2 + 2 * n]), res[-1]


def _exchange_wait(send_sems, recv_sems, srcs, lands, after, gather, name):
    n = len(srcs)

    def body(*refs):
        src_refs, land_refs, s_sems, r_sems = refs[:n], refs[n:2 * n], refs[2 * n], refs[2 * n + 1]
        for a in range(n):
            for k in range(1, N_DEV):
                _exchange_copy(src_refs, land_refs, s_sems, r_sems, a, k, gather, receiving=False).wait_send()
                _exchange_copy(src_refs, land_refs, s_sems, r_sems, a, k, gather, receiving=True).wait_recv()

    hbm = pl.BlockSpec(memory_space=pltpu.HBM)
    sem = pl.BlockSpec(memory_space=pltpu.SEMAPHORE)
    res = pl.pallas_call(
        body, name=name,
        out_shape=tuple(pltpu.HBM(v.shape, v.dtype) for v in srcs + lands),
        in_specs=[hbm] * (2 * n) + [sem, sem, pl.BlockSpec(memory_space=pl.ANY)],
        out_specs=tuple([hbm] * (2 * n)),
        input_output_aliases={i: i for i in range(2 * n)},
        compiler_params=pltpu.CompilerParams(has_side_effects=pltpu.SideEffectType.DATAFLOW_SIDE_EFFECTING),
    )(*srcs, *lands, send_sems, recv_sems, after)
    return list(res[n:])


def _load_resident(step_is_first, pairs, sem):
    @pl.when(step_is_first)
    def _():
        copies = [pltpu.make_async_copy(src, dst, sem.at[i]) for i, (src, dst) in enumerate(pairs)]
        for cp in copies:
            cp.start()
        for cp in copies:
            cp.wait()


def _load_w_in_t(step_is_first, w_hbm, w_vmem, sem):
    @pl.when(step_is_first)
    def _():
        w_vmem[D_IN:, :] = jnp.zeros((D_IN_PAD - D_IN, D_MODEL), BF16)
    _load_resident(step_is_first, [(w_hbm, w_vmem.at[pl.ds(0, D_IN)])], sem)


def _inproj_fwd(x2d, sc1p, sh1, w_in_t):
    t = x2d.shape[0]
    tm = min(PROJ_TILE, t)

    def body(x_ref, sc_ref, sh_ref, w_hbm, proj_ref, u_ref, w_vmem, sem):
        _load_w_in_t(pl.program_id(0) == 0, w_hbm, w_vmem, sem)
        xh, _ = _ln_stats(x_ref[...])
        ub = (xh * sc_ref[...] + sh_ref[...]).astype(BF16)
        u_ref[...] = ub
        proj_ref[...] = _dot(ub, w_vmem[...], NT)

    row = lambda i: (i, 0)
    fix = lambda i: (0, 0)
    return pl.pallas_call(
        body, name="inproj_fwd", grid=(t // tm,),
        in_specs=[pl.BlockSpec((tm, D_MODEL), row), pl.BlockSpec((1, D_MODEL), fix), pl.BlockSpec((1, D_MODEL), fix),
                  pl.BlockSpec(memory_space=pl.ANY)],
        out_specs=(pl.BlockSpec((tm, D_IN_PAD), row), pl.BlockSpec((tm, D_MODEL), row)),
        out_shape=(jax.ShapeDtypeStruct((t, D_IN_PAD), F32), jax.ShapeDtypeStruct((t, D_MODEL), BF16)),
        scratch_shapes=[pltpu.VMEM((D_IN_PAD, D_MODEL), BF16), pltpu.SemaphoreType.DMA((1,))],
        compiler_params=pltpu.CompilerParams(dimension_semantics=("arbitrary",), vmem_limit_bytes=V7X_VMEM_LIMIT),
    )(x2d, sc1p, sh1, w_in_t)


CHUNK_SHIFT = CHUNK.bit_length() - 1


def _ret_tables(t, tl):
    r = lax.broadcasted_iota(jnp.int32, (tl, tl), 0)
    c = lax.broadcasted_iota(jnp.int32, (tl, tl), 1)
    allowed = jnp.right_shift(c, CHUNK_SHIFT) <= jnp.right_shift(r, CHUNK_SHIFT)
    dist = jnp.abs(r - c).astype(F32)
    rowf = lax.broadcasted_iota(jnp.int32, (tl, RET_D), 0).astype(F32)
    lgs = [_log_gamma(h) for h in range(RET_HEADS)]
    dec = jnp.stack([jnp.where(allowed, jnp.exp(lg * dist), 0.0) for lg in lgs])
    qkd = jnp.stack([jnp.exp(lg * (rowf + 1.0)) for lg in lgs] + [jnp.exp(lg * (tl - 1.0 - rowf)) for lg in lgs])
    inv = 1.0 / (10000.0 ** jnp.linspace(0.0, 1.0, RET_D // 2, dtype=F32))
    off = jnp.arange(tl, dtype=F32)[:, None] * inv[None, :]
    start = (jnp.arange(t // tl, dtype=F32) * tl)[:, None] * inv[None, :]
    co, so = jnp.cos(off), jnp.sin(off)
    rot_in = jnp.stack([jnp.concatenate([co, co], 1), jnp.concatenate([so, so], 1),
                        jnp.concatenate([-co, co], 1), jnp.concatenate([-so, so], 1)])
    cs, ss = jnp.cos(start), jnp.sin(start)
    rot_tile = jnp.concatenate([cs, cs, ss, ss], axis=1)
    rot_tile = jnp.broadcast_to(rot_tile[:, None, :], (t // tl, 8, 2 * RET_D))
    return dec, qkd, rot_in, rot_tile


def _tile_gammas(tl):
    return [float(np.exp(np.float32(_log_gamma(h)) * np.float32(tl))) for h in range(RET_HEADS)]


def _tile_rotary(rot_in_ref, rot_tile_ref, j):
    ca, sa = rot_tile_ref[j, 0:1, 0:RET_D], rot_tile_ref[j, 0:1, RET_D:2 * RET_D]
    cosv = ca * rot_in_ref[0] - sa * rot_in_ref[1]
    sinv = sa * rot_in_ref[2] + ca * rot_in_ref[3]
    return cosv, sinv


def _gla_consts(tl):
    r = lax.broadcasted_iota(jnp.int32, (tl, tl), 0)
    c = lax.broadcasted_iota(jnp.int32, (tl, tl), 1)
    ltri = (c <= r).astype(F32)
    utri = (c >= r).astype(F32)
    lane = lax.broadcasted_iota(jnp.int32, (1, GLA_KW), 1)
    hmask = [((lane >= h * GLA_DK) & (lane < (h + 1) * GLA_DK)).astype(F32) for h in range(GLA_HEADS)]
    rs = lax.broadcasted_iota(jnp.int32, (GLA_HEADS * tl, tl), 0) & (tl - 1)
    cs = lax.broadcasted_iota(jnp.int32, (GLA_HEADS * tl, tl), 1)
    lower = cs <= rs
    same = jnp.right_shift(cs, CHUNK_SHIFT) == jnp.right_shift(rs, CHUNK_SHIFT)
    upper = jnp.logical_and(jnp.logical_not(lower), same)
    return dict(ltri=ltri, utri=utri, hmask=hmask, lower=lower, upper=upper)


def _tile_rows(j, tl):
    return pl.ds(j * tl, tl) if isinstance(j, int) else pl.ds(pl.multiple_of(j * tl, tl), tl)


def _for_tiles(cps, fn):
    for j in range(cps):
        fn(j, 0)


def _proj_groups(proj_ref):
    cuts = [(OFF_RQ, RET_W), (OFF_RK, RET_W), (OFF_RV, RET_W), (OFF_RG, RET_W), (OFF_GQ, GLA_KW), (OFF_GK, GLA_KW),
            (OFF_GV, GLA_VW), (OFF_GG, GLA_VW), (OFF_GLR, V7X_LANES)]
    return [proj_ref.at[:, pl.ds(off, width)] for off, width in cuts]


def _rotate(v, cosv, sinv):
    return v * cosv + pltpu.roll(v, RET_D // 2, 1) * sinv


def _rotate_t(d, cosv, sinv):
    return d * cosv + pltpu.roll(d * sinv, RET_D // 2, 1)


def _stack_heads(v, hmask):
    return jnp.concatenate([v * hmask[h] for h in range(GLA_HEADS)], axis=0)


def _gla_gates(glr, gw, gb, ltri, tl):
    z = _dot_split(glr, gw, NN) + gb
    la = (jnp.minimum(z, 0.0) - jnp.log(1.0 + jnp.exp(-jnp.abs(z)))) * (1.0 / GATE_TAU)
    b = _dot_split(ltri, la, NN, a_exact=True)
    level = b[tl // 2 - 1:tl // 2, :]
    ep = jnp.exp(jnp.clip(b - level, -80.0, 80.0))
    em = jnp.exp(jnp.clip(level - b, -80.0, 80.0))
    bl = b[tl - 1:tl, :]
    return z, b, bl, ep, em


def _mixer_fwd(proj, tables, gw_pad, gb, rnw, gnw):
    t = proj.shape[0]
    tc = min(MIX_TILE, t)
    tr, tg = min(RET_SUB, tc), min(GLA_SUB, tc)
    nsteps = t // tc
    scale_r = RET_D ** -0.5
    scale_g = GLA_DK ** -0.5
    gammas = _tile_gammas(tr)

    def body(proj_ref, dec_ref, qkd_ref, rot_in_ref, rot_tile_ref, gw_ref, gb_ref, rnw_ref, gnw_ref,
             mix_ref, oraw_ref, qrb_ref, krb_ref, rst_ref, sst_ref, r_scr, s_scr):
        rq_ref, rk_ref, rv_ref, rg_ref, gq_ref, gk_ref, gv_ref, gg_ref, glr_ref = _proj_groups(proj_ref)

        @pl.when(pl.program_id(0) == 0)
        def _():
            r_scr[...] = jnp.zeros_like(r_scr)
            s_scr[...] = jnp.zeros_like(s_scr)

        gla_k = _gla_consts(tg)

        def ret_tile(j, carry):
            rows = _tile_rows(j, tr)
            cosv, sinv = _tile_rotary(rot_in_ref, rot_tile_ref, j)
            for h in range(RET_HEADS):
                cols = slice(h * RET_D, (h + 1) * RET_D)
                qr = _rotate(rq_ref[rows, cols], cosv, sinv) * scale_r
                kr = _rotate(rk_ref[rows, cols], cosv, sinv)
                vb = rv_ref[rows, cols].astype(BF16)
                qb, kb = qr.astype(BF16), kr.astype(BF16)
                qrb_ref[rows, cols] = qb
                krb_ref[rows, cols] = kb
                p = _dot(qb, kb, NT) * dec_ref[h]
                rp = r_scr[cols, :]
                o = _dot(p.astype(BF16), vb) + _dot((qr * qkd_ref[h]).astype(BF16), rp.astype(BF16))
                rst_ref[j, cols, :] = rp
                r_scr[cols, :] = gammas[h] * rp + _dot((kr * qkd_ref[RET_HEADS + h]).astype(BF16), vb, TN)
                oraw_ref[rows, cols] = o
                oc = o - jnp.mean(o, axis=-1, keepdims=True)
                n = oc * lax.rsqrt(jnp.mean(oc * oc, axis=-1, keepdims=True) + LN_EPS)
                g = rg_ref[rows, cols]
                mix_ref[rows, cols] = (n * rnw_ref[:, cols] * (g * _sigmoid(g))).astype(BF16)
            return carry

        def gla_tile(j, carry):
            k = gla_k
            tl = tg
            rows = _tile_rows(j, tg)
            _, b, bl, ep, em = _gla_gates(glr_ref[rows, :], gw_ref[...], gb_ref[...], k["ltri"], tl)
            qs = gq_ref[rows, :] * scale_g
            kk = gk_ref[rows, :]
            x_all = _dot(_stack_heads(qs * ep, k["hmask"]).astype(BF16), (kk * em).astype(BF16), NT)
            y_all = _dot(_stack_heads(qs * em, k["hmask"]).astype(BF16), (kk * ep).astype(BF16), NT)
            a_all = jnp.where(k["lower"], x_all, jnp.where(k["upper"], y_all, 0.0)).astype(BF16)
            st = s_scr[...]
            oq = _dot(_stack_heads(qs * jnp.exp(b), k["hmask"]).astype(BF16), st.astype(BF16), NT)
            kg = kk * jnp.exp(bl - b)
            sst_ref[j] = st
            st_new = st * jnp.exp(bl)
            for h in range(GLA_HEADS):
                cols = slice(h * GLA_DV, (h + 1) * GLA_DV)
                hr = slice(h * tl, (h + 1) * tl)
                vb = gv_ref[rows, cols].astype(BF16)
                o = _dot(a_all[hr, :], vb) + oq[hr, :]
                st_new = st_new + _dot(vb, (kg * k["hmask"][h]).astype(BF16), TN)
                ocols = slice(RET_W + h * GLA_DV, RET_W + (h + 1) * GLA_DV)
                oraw_ref[rows, ocols] = o
                n = o * lax.rsqrt(jnp.mean(o * o, axis=-1, keepdims=True) + LN_EPS)
                g = gg_ref[rows, cols]
                mix_ref[rows, ocols] = (n * gnw_ref[:, cols] * (g * _sigmoid(g))).astype(BF16)
            s_scr[...] = st_new
            return carry

        _for_tiles(tc // tr, ret_tile)
        _for_tiles(tc // tg, gla_tile)

    fix = lambda i: (0, 0)
    fix3 = lambda i: (0, 0, 0)
    dec, qkd, rot_in, rot_tile = tables
    in_specs = [pl.BlockSpec((tc, D_IN_PAD), lambda i: (i, 0)),
                pl.BlockSpec(dec.shape, fix3), pl.BlockSpec(qkd.shape, fix3), pl.BlockSpec(rot_in.shape, fix3),
                pl.BlockSpec((tc // tr, 8, 2 * RET_D), lambda i: (i, 0, 0)),
                pl.BlockSpec((V7X_LANES, GLA_KW), fix), pl.BlockSpec((1, GLA_KW), fix),
                pl.BlockSpec((1, RET_W), fix), pl.BlockSpec((1, GLA_VW), fix)]
    half = pl.BlockSpec((tc, RET_W), lambda i: (i, 0))
    out_specs = (pl.BlockSpec((tc, D_MODEL), lambda i: (i, 0)), pl.BlockSpec((tc, D_MODEL), lambda i: (i, 0)),
                 half, half,
                 pl.BlockSpec((tc // tr, RET_W, RET_D), lambda i: (i, 0, 0)),
                 pl.BlockSpec((tc // tg, GLA_DV, GLA_KW), lambda i: (i, 0, 0)))
    out_shape = (jax.ShapeDtypeStruct((t, D_MODEL), BF16), jax.ShapeDtypeStruct((t, D_MODEL), F32),
                 jax.ShapeDtypeStruct((t, RET_W), BF16), jax.ShapeDtypeStruct((t, RET_W), BF16),
                 jax.ShapeDtypeStruct((t // tr, RET_W, RET_D), F32),
                 jax.ShapeDtypeStruct((t // tg, GLA_DV, GLA_KW), F32))
    return pl.pallas_call(
        body, name="mixer_fwd", grid=(nsteps,), in_specs=in_specs, out_specs=out_specs, out_shape=out_shape,
        scratch_shapes=[pltpu.VMEM((RET_W, RET_D), F32), pltpu.VMEM((GLA_DV, GLA_KW), F32)],
        compiler_params=pltpu.CompilerParams(dimension_semantics=("arbitrary",), vmem_limit_bytes=V7X_VMEM_LIMIT),
    )(proj, dec, qkd, rot_in, rot_tile, gw_pad, gb, rnw, gnw)


def _mid_fwd(mixed, x2d, target, vecs, w_out_b, w1_b, w2_b):
    t = x2d.shape[0]
    tm = min(ROW_TILE, t)

    def body(mix_ref, x_ref, tgt_ref, v_ref, wo_hbm, w1_hbm, w2_hbm,
             m_ref, x1n_ref, rstd_ref, u2_ref, a_ref, df_ref, dh2_ref, acc_ref, wo, w1, w2, sem):
        first = pl.program_id(0) == 0
        _load_resident(first, [(wo_hbm, wo), (w1_hbm, w1), (w2_hbm, w2)], sem)

        @pl.when(first)
        def _():
            acc_ref[...] = jnp.zeros_like(acc_ref)

        gate1, sc2p, sh2, gate2 = v_ref[0:1, :], v_ref[1:2, :], v_ref[2:3, :], v_ref[3:4, :]
        l1w, l1b, l2w, l2b = v_ref[4:5, :], v_ref[5:6, :], v_ref[6:7, :], v_ref[7:8, :]
        m = _dot(mix_ref[...], wo[...])
        m_ref[...] = m.astype(BF16)
        x1n, rstd1 = _ln_stats(ALPHA * x_ref[...] + gate1 * m)
        x1n_ref[...] = x1n
        rstd_ref[...] = rstd1
        x1 = x1n * l1w + l1b
        xh1, _ = _ln_stats(x1)
        u2 = (xh1 * sc2p + sh2).astype(BF16)
        u2_ref[...] = u2
        f = jnp.zeros((tm, D_MODEL), F32)
        for j in range(N_DEV):
            cols = slice(j * FF_COLS, (j + 1) * FF_COLS)
            a = _dot(u2, w1[j])
            a_ref[:, cols] = a.astype(BF16)
            r = jnp.maximum(a, 0.0)
            f = f + _dot((r * r).astype(BF16), w2[cols, :])
        yh, rstd2 = _ln_stats(ALPHA * x1 + gate2 * f)
        e = yh * l2w + l2b - tgt_ref[...]
        dy = e * (1.0 / D_MODEL)
        dh2 = _ln_bwd(dy * l2w, yh, rstd2)
        dh2_ref[...] = dh2
        df_ref[...] = (dh2 * gate2).astype(BF16)
        acc_ref[0:1, :] += jnp.sum(dy * yh, axis=0, keepdims=True)
        acc_ref[1:2, :] += jnp.sum(dy, axis=0, keepdims=True)
        acc_ref[2:3, :] += jnp.sum(dh2 * f, axis=0, keepdims=True)
        acc_ref[3:4, :] += jnp.sum(e * e, axis=0, keepdims=True) * (0.5 / D_MODEL)

    row = lambda i: (i, 0)
    fix = lambda i: (0, 0)
    hbm = pl.BlockSpec(memory_space=pl.ANY)
    return pl.pallas_call(
        body, name="mid_fwd", grid=(t // tm,),
        in_specs=[pl.BlockSpec((tm, D_MODEL), row), pl.BlockSpec((tm, D_MODEL), row), pl.BlockSpec((tm, D_MODEL), row),
                  pl.BlockSpec((8, D_MODEL), fix), hbm, hbm, hbm],
        out_specs=(pl.BlockSpec((tm, D_MODEL), row), pl.BlockSpec((tm, D_MODEL), row), pl.BlockSpec((tm, 1), row),
                   pl.BlockSpec((tm, D_MODEL), row), pl.BlockSpec((tm, D_FF), row), pl.BlockSpec((tm, D_MODEL), row),
                   pl.BlockSpec((tm, D_MODEL), row), pl.BlockSpec((8, D_MODEL), fix)),
        out_shape=(jax.ShapeDtypeStruct((t, D_MODEL), BF16), jax.ShapeDtypeStruct((t, D_MODEL), F32),
                   jax.ShapeDtypeStruct((t, 1), F32), jax.ShapeDtypeStruct((t, D_MODEL), BF16),
                   jax.ShapeDtypeStruct((t, D_FF), BF16), jax.ShapeDtypeStruct((t, D_MODEL), BF16),
                   jax.ShapeDtypeStruct((t, D_MODEL), F32), jax.ShapeDtypeStruct((8, D_MODEL), F32)),
        scratch_shapes=[pltpu.VMEM((D_MODEL, D_MODEL), BF16), pltpu.VMEM((N_DEV, D_MODEL, FF_COLS), BF16),
                        pltpu.VMEM((D_FF, D_MODEL), BF16), pltpu.SemaphoreType.DMA((3,))],
        compiler_params=pltpu.CompilerParams(dimension_semantics=("arbitrary",), vmem_limit_bytes=V7X_VMEM_LIMIT),
    )(mixed, x2d, target, vecs, w_out_b, w1_b, w2_b)


def _ffn_bwd(df, a, dh2, x1n, rstd1, m, vecs, w_out_b, w1_b, w2_b):
    t = x1n.shape[0]
    tm = min(ROW_TILE, t)

    def body(df_ref, a_ref, dh2_ref, x1n_ref, rstd_ref, m_ref, v_ref, wo_hbm, w1_hbm, w2_hbm,
             da_ref, dm_ref, dmix_ref, dxa_ref, acc_ref, wo, w1, w2, sem):
        first = pl.program_id(0) == 0
        _load_resident(first, [(wo_hbm, wo), (w1_hbm, w1), (w2_hbm, w2)], sem)

        @pl.when(first)
        def _():
            acc_ref[...] = jnp.zeros_like(acc_ref)

        gate1, sc2p, l1w, l1b = v_ref[0:1, :], v_ref[1:2, :], v_ref[2:3, :], v_ref[3:4, :]
        df = df_ref[...]
        du2 = jnp.zeros((tm, D_MODEL), F32)
        for j in range(N_DEV):
            cols = slice(j * FF_COLS, (j + 1) * FF_COLS)
            dr2 = _dot(df, w2[cols, :], NT)
            da = (dr2 * (2.0 * jnp.maximum(a_ref[:, cols].astype(F32), 0.0))).astype(BF16)
            da_ref[:, cols] = da
            du2 = du2 + _dot(da, w1[j], NT)
        x1n = x1n_ref[...]
        xh1, rstd0 = _ln_stats(x1n * l1w + l1b)
        dx1 = ALPHA * dh2_ref[...] + _ln_bwd(du2 * sc2p, xh1, rstd0)
        dh1 = _ln_bwd(dx1 * l1w, x1n, rstd_ref[...])
        dxa_ref[...] = ALPHA * dh1
        dm = (dh1 * gate1).astype(BF16)
        dm_ref[...] = dm
        dmix_ref[...] = _dot(dm, wo[...], NT)
        acc_ref[0:1, :] += jnp.sum(du2 * xh1, axis=0, keepdims=True)
        acc_ref[1:2, :] += jnp.sum(du2, axis=0, keepdims=True)
        acc_ref[2:3, :] += jnp.sum(dx1 * x1n, axis=0, keepdims=True)
        acc_ref[3:4, :] += jnp.sum(dx1, axis=0, keepdims=True)
        acc_ref[4:5, :] += jnp.sum(dh1 * m_ref[...].astype(F32), axis=0, keepdims=True)

    row = lambda i: (i, 0)
    fix = lambda i: (0, 0)
    hbm = pl.BlockSpec(memory_space=pl.ANY)
    return pl.pallas_call(
        body, name="ffn_bwd", grid=(t // tm,),
        in_specs=[pl.BlockSpec((tm, D_MODEL), row), pl.BlockSpec((tm, D_FF), row), pl.BlockSpec((tm, D_MODEL), row),
                  pl.BlockSpec((tm, D_MODEL), row), pl.BlockSpec((tm, 1), row), pl.BlockSpec((tm, D_MODEL), row),
                  pl.BlockSpec((8, D_MODEL), fix), hbm, hbm, hbm],
        out_specs=(pl.BlockSpec((tm, D_FF), row), pl.BlockSpec((tm, D_MODEL), row), pl.BlockSpec((tm, D_MODEL), row),
                   pl.BlockSpec((tm, D_MODEL), row), pl.BlockSpec((8, D_MODEL), fix)),
        out_shape=(jax.ShapeDtypeStruct((t, D_FF), BF16), jax.ShapeDtypeStruct((t, D_MODEL), BF16),
                   jax.ShapeDtypeStruct((t, D_MODEL), F32), jax.ShapeDtypeStruct((t, D_MODEL), F32),
                   jax.ShapeDtypeStruct((8, D_MODEL), F32)),
        scratch_shapes=[pltpu.VMEM((D_MODEL, D_MODEL), BF16), pltpu.VMEM((N_DEV, D_MODEL, FF_COLS), BF16),
                        pltpu.VMEM((D_FF, D_MODEL), BF16), pltpu.SemaphoreType.DMA((3,))],
        compiler_params=pltpu.CompilerParams(dimension_semantics=("arbitrary",), vmem_limit_bytes=V7X_VMEM_LIMIT),
    )(df, a, dh2, x1n, rstd1, m, vecs, w_out_b, w1_b, w2_b)


def _matmul_tn(lhs, rhs, tmm, tn, tk, name, relu_sq=False, col_slab=None, out_rows=None):
    t, mm = lhs.shape
    assert out_rows is None or (col_slab is None and tmm == mm)
    nn = rhs.shape[1]
    tk = min(tk, t)
    nk = t // tk

    def body(l_ref, r_ref, o_ref, acc):
        kk = pl.program_id(2)

        @pl.when(kk == 0)
        def _():
            acc[...] = jnp.zeros_like(acc)

        l = l_ref[...]
        if relu_sq:
            lf = jnp.maximum(l.astype(F32), 0.0)
            l = (lf * lf).astype(BF16)
        acc[...] += _dot(l, r_ref[...], TN)

        @pl.when(kk == nk - 1)
        def _():
            if out_rows is not None:
                for s in range(N_DEV):
                    o_ref[s] = acc[s * out_rows:(s + 1) * out_rows, :].astype(o_ref.dtype)
            elif col_slab is None:
                o_ref[...] = acc[...].astype(o_ref.dtype)
            else:
                for s in range(tn // col_slab):
                    o_ref[s] = acc[:, s * col_slab:(s + 1) * col_slab].astype(o_ref.dtype)

    if out_rows is not None:
        out_spec = pl.BlockSpec((N_DEV, out_rows, tn), lambda i, j, k: (0, 0, j))
        out_shape = jax.ShapeDtypeStruct((N_DEV, out_rows, nn), BF16)
    elif col_slab is None:
        out_spec = pl.BlockSpec((tmm, tn), lambda i, j, k: (i, j))
        out_shape = jax.ShapeDtypeStruct((mm, nn), BF16)
    else:
        out_spec = pl.BlockSpec((tn // col_slab, tmm, col_slab), lambda i, j, k: (j, i, 0))
        out_shape = jax.ShapeDtypeStruct((nn // col_slab, mm, col_slab), BF16)
    return pl.pallas_call(
        body, name=name, grid=(mm // tmm, nn // tn, nk),
        in_specs=[pl.BlockSpec((tk, tmm), lambda i, j, k: (k, i)), pl.BlockSpec((tk, tn), lambda i, j, k: (k, j))],
        out_specs=out_spec,
        out_shape=out_shape,
        scratch_shapes=[pltpu.VMEM((tmm, tn), F32)],
        compiler_params=pltpu.CompilerParams(dimension_semantics=("arbitrary", "arbitrary", "arbitrary"),
                                             vmem_limit_bytes=V7X_VMEM_LIMIT),
    )(lhs, rhs)


def _mixer_bwd(dmix, proj, qrb, krb, oraw, tables, rst, sst, gw_pad, gb, rnw, gnw):
    t = proj.shape[0]
    tc = min(MIX_TILE, t)
    tr, tg = min(RET_SUB, tc), min(GLA_SUB, tc)
    nsteps = t // tc
    scale_r = RET_D ** -0.5
    scale_g = GLA_DK ** -0.5
    gammas = _tile_gammas(tr)

    def body(dmix_ref, qrb_ref, krb_ref, proj_ref, oraw_ref,
             dec_ref, qkd_ref, rot_in_ref, rot_tile_ref, rst_ref, sst_ref, gw_ref, gb_ref, rnw_ref, gnw_ref,
             dproj_ref, dgw_ref, dvec_ref, dr_scr, ds_scr):
        _, _, rv_ref, rg_ref, gq_ref, gk_ref, gv_ref, gg_ref, glr_ref = _proj_groups(proj_ref)

        @pl.when(pl.program_id(0) == 0)
        def _():
            dr_scr[...] = jnp.zeros_like(dr_scr)
            ds_scr[...] = jnp.zeros_like(ds_scr)
            dgw_ref[...] = jnp.zeros_like(dgw_ref)
            dvec_ref[...] = jnp.zeros_like(dvec_ref)

        gla_k = _gla_consts(tg)
        last_row = lax.broadcasted_iota(jnp.int32, (tg, GLA_KW), 0) == tg - 1

        def ret_tile(jj, carry):
            j = tc // tr - 1 - jj
            rows = _tile_rows(j, tr)
            cosv, sinv = _tile_rotary(rot_in_ref, rot_tile_ref, j)
            for h in range(RET_HEADS):
                cols = slice(h * RET_D, (h + 1) * RET_D)
                o = oraw_ref[rows, cols]
                g = rg_ref[rows, cols]
                w = rnw_ref[:, cols]
                dout = dmix_ref[rows, cols]
                oc = o - jnp.mean(o, axis=-1, keepdims=True)
                inv = lax.rsqrt(jnp.mean(oc * oc, axis=-1, keepdims=True) + LN_EPS)
                n = oc * inv
                sg = _sigmoid(g)
                sil = g * sg
                dn = dout * w * sil
                dvec_ref[0:1, cols] += jnp.sum(dout * n * sil, axis=0, keepdims=True)
                dproj_ref[rows, OFF_RG + h * RET_D:OFF_RG + (h + 1) * RET_D] = (
                    dout * n * w * (sg * (1.0 + g * (1.0 - sg)))).astype(BF16)
                doc = inv * (dn - n * jnp.mean(dn * n, axis=-1, keepdims=True))
                do = doc - jnp.mean(doc, axis=-1, keepdims=True)

                qb, kb = qrb_ref[rows, cols], krb_ref[rows, cols]
                qr, kr = qb.astype(F32), kb.astype(F32)
                vb = rv_ref[rows, cols].astype(BF16)
                dob = do.astype(BF16)
                qd, kd = qkd_ref[h], qkd_ref[RET_HEADS + h]
                p = _dot(qb, kb, NT) * dec_ref[h]
                rp = rst_ref[j, cols, :].astype(BF16)
                dr = dr_scr[cols, :]
                drb = dr.astype(BF16)
                dpb = (_dot(dob, vb, NT) * dec_ref[h]).astype(BF16)
                dqr = _dot(dpb, kb) + _dot(dob, rp, NT) * qd
                dkr = _dot(dpb, qb, TN) + _dot(vb, drb, NT) * kd
                dv = _dot(p.astype(BF16), dob, TN) + _dot((kr * kd).astype(BF16), drb)
                dr_scr[cols, :] = gammas[h] * dr + _dot((qr * qd).astype(BF16), dob, TN)
                dproj_ref[rows, OFF_RQ + h * RET_D:OFF_RQ + (h + 1) * RET_D] = (
                    _rotate_t(dqr, cosv, sinv) * scale_r).astype(BF16)
                dproj_ref[rows, OFF_RK + h * RET_D:OFF_RK + (h + 1) * RET_D] = _rotate_t(dkr, cosv, sinv).astype(BF16)
                dproj_ref[rows, OFF_RV + h * RET_D:OFF_RV + (h + 1) * RET_D] = dv.astype(BF16)
            return carry

        def gla_tile(jj, carry):
            k = gla_k
            tl = tg
            j = tc // tg - 1 - jj
            rows = _tile_rows(j, tg)
            glr = glr_ref[rows, :]
            z, b, bl, ep, em = _gla_gates(glr, gw_ref[...], gb_ref[...], k["ltri"], tl)
            qs = gq_ref[rows, :] * scale_g
            kk = gk_ref[rows, :]
            eb = jnp.exp(b)
            ekb = jnp.exp(bl - b)
            ebl = jnp.exp(bl)
            ql, qu, kl, ku = qs * ep, qs * em, kk * em, kk * ep
            qg, kg = qs * eb, kk * ekb
            qlm = _stack_heads(ql, k["hmask"]).astype(BF16)
            qum = _stack_heads(qu, k["hmask"]).astype(BF16)
            klb, kub = kl.astype(BF16), ku.astype(BF16)
            a_all = jnp.where(k["lower"], _dot(qlm, klb, NT),
                              jnp.where(k["upper"], _dot(qum, kub, NT), 0.0)).astype(BF16)
            st = sst_ref[j]
            stb = st.astype(BF16)
            ds = ds_scr[...]
            dsb = ds.astype(BF16)
            ds_new = ds * ebl
            da_parts = []
            dqg = jnp.zeros((tl, GLA_KW), F32)
            dkg = jnp.zeros((tl, GLA_KW), F32)
            for h in range(GLA_HEADS):
                cols = slice(h * GLA_DV, (h + 1) * GLA_DV)
                hr = slice(h * tl, (h + 1) * tl)
                ocols = slice(RET_W + h * GLA_DV, RET_W + (h + 1) * GLA_DV)
                o = oraw_ref[rows, ocols]
                g = gg_ref[rows, cols]
                w = gnw_ref[:, cols]
                dout = dmix_ref[rows, ocols]
                inv = lax.rsqrt(jnp.mean(o * o, axis=-1, keepdims=True) + LN_EPS)
                n = o * inv
                sg = _sigmoid(g)
                sil = g * sg
                dn = dout * w * sil
                dvec_ref[1:2, cols] += jnp.sum(dout * n * sil, axis=0, keepdims=True)
                dproj_ref[rows, OFF_GG + h * GLA_DV:OFF_GG + (h + 1) * GLA_DV] = (
                    dout * n * w * (sg * (1.0 + g * (1.0 - sg)))).astype(BF16)
                dob = (inv * (dn - n * jnp.mean(dn * n, axis=-1, keepdims=True))).astype(BF16)
                vb = gv_ref[rows, cols].astype(BF16)
                mh = k["hmask"][h]
                da_parts.append(_dot(dob, vb, NT))
                dv = _dot(a_all[hr, :], dob, TN) + _dot((kg * mh).astype(BF16), dsb, NT)
                dproj_ref[rows, OFF_GV + h * GLA_DV:OFF_GV + (h + 1) * GLA_DV] = dv.astype(BF16)
                dkg = dkg + mh * _dot(vb, dsb)
                dqg = dqg + mh * _dot(dob, stb)
                ds_new = ds_new + _dot(dob, (qg * mh).astype(BF16), TN)
            da_all = jnp.concatenate(da_parts, axis=0)
            dal = jnp.where(k["lower"], da_all, 0.0).astype(BF16)
            dau = jnp.where(k["upper"], da_all, 0.0).astype(BF16)
            dqlm = _dot(dal, klb)
            dqum = _dot(dau, kub)
            dql = jnp.zeros((tl, GLA_KW), F32)
            dqu = jnp.zeros((tl, GLA_KW), F32)
            for h in range(GLA_HEADS):
                hr = slice(h * tl, (h + 1) * tl)
                dql = dql + k["hmask"][h] * dqlm[hr, :]
                dqu = dqu + k["hmask"][h] * dqum[hr, :]
            dkl = _dot(dal, qlm, TN)
            dku = _dot(dau, qum, TN)
            dbl = (jnp.sum(dkg * kg, axis=0, keepdims=True)
                   + jnp.sum(ds * st, axis=0, keepdims=True) * ebl)
            ds_scr[...] = ds_new
            dqs = dql * ep + dqu * em + dqg * eb
            dk = dkl * em + dku * ep + dkg * ekb
            db = dql * ql - dkl * kl - dqu * qu + dku * ku + dqg * qg - dkg * kg
            db = db + jnp.where(last_row, dbl, 0.0)
            dla = _dot_split(k["utri"], db, NN, a_exact=True)
            dz = dla * (1.0 / GATE_TAU) * _sigmoid(-z)
            dvec_ref[2:3, 0:GLA_KW] += jnp.sum(dz, axis=0, keepdims=True)
            dgw_ref[...] += _dot_split(glr, dz, TN)
            dproj_ref[rows, OFF_GLR:D_IN_PAD] = _dot(dz.astype(BF16), gw_ref[...].astype(BF16), NT).astype(BF16)
            dproj_ref[rows, OFF_GQ:OFF_GQ + GLA_KW] = (dqs * scale_g).astype(BF16)
            dproj_ref[rows, OFF_GK:OFF_GK + GLA_KW] = dk.astype(BF16)
            return carry

        _for_tiles(tc // tr, ret_tile)
        _for_tiles(tc // tg, gla_tile)

    rev = lambda i: (nsteps - 1 - i, 0)

    fix = lambda i: (0, 0)
    fix3 = lambda i: (0, 0, 0)
    dec, qkd, rot_in, rot_tile = tables
    half = pl.BlockSpec((tc, RET_W), rev)
    in_specs = [pl.BlockSpec((tc, D_MODEL), rev), half, half, pl.BlockSpec((tc, D_IN_PAD), rev),
                pl.BlockSpec((tc, D_MODEL), rev),
                pl.BlockSpec(dec.shape, fix3), pl.BlockSpec(qkd.shape, fix3), pl.BlockSpec(rot_in.shape, fix3),
                pl.BlockSpec((tc // tr, 8, 2 * RET_D), lambda i: (nsteps - 1 - i, 0, 0)),
                pl.BlockSpec((tc // tr, RET_W, RET_D), lambda i: (nsteps - 1 - i, 0, 0)),
                pl.BlockSpec((tc // tg, GLA_DV, GLA_KW), lambda i: (nsteps - 1 - i, 0, 0)),
                pl.BlockSpec((V7X_LANES, GLA_KW), fix), pl.BlockSpec((1, GLA_KW), fix),
                pl.BlockSpec((1, RET_W), fix), pl.BlockSpec((1, GLA_VW), fix)]
    out_specs = (pl.BlockSpec((tc, D_IN_PAD), rev), pl.BlockSpec((V7X_LANES, GLA_KW), fix),
                 pl.BlockSpec((8, RET_W), fix))
    out_shape = (jax.ShapeDtypeStruct((t, D_IN_PAD), BF16), jax.ShapeDtypeStruct((V7X_LANES, GLA_KW), F32),
                 jax.ShapeDtypeStruct((8, RET_W), F32))
    return pl.pallas_call(
        body, name="mixer_bwd", grid=(nsteps,), in_specs=in_specs, out_specs=out_specs, out_shape=out_shape,
        scratch_shapes=[pltpu.VMEM((RET_W, RET_D), F32), pltpu.VMEM((GLA_DV, GLA_KW), F32)],
        compiler_params=pltpu.CompilerParams(dimension_semantics=("arbitrary",), vmem_limit_bytes=V7X_VMEM_LIMIT),
    )(dmix, qrb, krb, proj, oraw, dec, qkd, rot_in, rot_tile, rst, sst, gw_pad, gb, rnw, gnw)


def _inproj_bwd(dproj, x2d, dxa, sc1p, w_in_t):
    t = x2d.shape[0]
    tm = min(2 * PROJ_TILE, t)

    def body(dp_ref, x_ref, dxa_ref, sc_ref, w_hbm, gx_ref, acc_ref, w_vmem, sem):
        first = pl.program_id(0) == 0
        _load_w_in_t(first, w_hbm, w_vmem, sem)

        @pl.when(first)
        def _():
            acc_ref[...] = jnp.zeros_like(acc_ref)

        du = _dot(dp_ref[...], w_vmem[...])
        xh, rstd = _ln_stats(x_ref[...])
        gx_ref[...] = dxa_ref[...] + _ln_bwd(du * sc_ref[...], xh, rstd)
        acc_ref[0:1, :] += jnp.sum(du * xh, axis=0, keepdims=True)
        acc_ref[1:2, :] += jnp.sum(du, axis=0, keepdims=True)

    row = lambda i: (i, 0)
    fix = lambda i: (0, 0)
    return pl.pallas_call(
        body, name="inproj_bwd", grid=(t // tm,),
        in_specs=[pl.BlockSpec((tm, D_IN_PAD), row), pl.BlockSpec((tm, D_MODEL), row), pl.BlockSpec((tm, D_MODEL), row),
                  pl.BlockSpec((1, D_MODEL), fix), pl.BlockSpec(memory_space=pl.ANY)],
        out_specs=(pl.BlockSpec((tm, D_MODEL), row), pl.BlockSpec((8, D_MODEL), fix)),
        out_shape=(jax.ShapeDtypeStruct((t, D_MODEL), F32), jax.ShapeDtypeStruct((8, D_MODEL), F32)),
        scratch_shapes=[pltpu.VMEM((D_IN_PAD, D_MODEL), BF16), pltpu.SemaphoreType.DMA((1,))],
        compiler_params=pltpu.CompilerParams(dimension_semantics=("arbitrary",), vmem_limit_bytes=V7X_VMEM_LIMIT),
    )(dproj, x2d, dxa, sc1p, w_in_t)


def _adam_math(w, g, m, v):
    m = ADAM_B1 * m + (1.0 - ADAM_B1) * g
    v = ADAM_B2 * v + (1.0 - ADAM_B2) * (g * g)
    m_hat = m / (1.0 - ADAM_B1 ** ADAM_STEP)
    v_hat = v / (1.0 - ADAM_B2 ** ADAM_STEP)
    delta = -ADAM_LR * (m_hat / (jnp.sqrt(v_hat) + ADAM_EPS) + ADAM_WD * w)
    return delta, m, v


def _adamw(w, gparts, m, v, name):
    nparts, rows, cols = gparts.shape
    tr = rows
    for cand in (512, 256, 128, 64, 32, 16, 8):
        if rows % cand == 0:
            tr = cand
            break

    def body(w_ref, g_ref, m_ref, v_ref, go_ref, d_ref, mo_ref, vo_ref):
        g = g_ref[0].astype(F32)
        for p in range(1, nparts):
            g = g + g_ref[p].astype(F32)
        delta, mn, vn = _adam_math(w_ref[...], g, m_ref[...], v_ref[...])
        go_ref[...] = g
        d_ref[...] = delta
        mo_ref[...] = mn
        vo_ref[...] = vn

    blk = pl.BlockSpec((tr, cols), lambda i: (i, 0))
    shp = jax.ShapeDtypeStruct((rows, cols), F32)
    return pl.pallas_call(
        body, name=name, grid=(rows // tr,),
        in_specs=[blk, pl.BlockSpec((nparts, tr, cols), lambda i: (0, i, 0)), blk, blk],
        out_specs=(blk, blk, blk, blk), out_shape=(shp, shp, shp, shp),
        compiler_params=pltpu.CompilerParams(dimension_semantics=("arbitrary",), vmem_limit_bytes=V7X_VMEM_LIMIT),
    )(w, gparts, m, v)


def _small_reduce(gathered, gathered_gw, c_all, dmod_cols):
    def body(g_ref, gw_ref, c_ref, dm_ref, sum_ref, gwsum_ref, gb_ref, gwa_ref):
        s = g_ref[0]
        sw = gw_ref[0]
        for p in range(1, N_DEV):
            s = s + g_ref[p]
            sw = sw + gw_ref[p]
        sum_ref[...] = s
        gwsum_ref[...] = sw
        for i in range(6):
            gb_ref[:, i * D_MODEL:(i + 1) * D_MODEL] = s[i:i + 1, :]
        cc = c_ref[...]
        gwa_ref[...] = _dot(cc * _sigmoid(cc), dm_ref[...], TN, HIGHEST)

    vm = pl.BlockSpec(memory_space=pltpu.VMEM)
    return pl.pallas_call(
        body, name="small_reduce",
        out_shape=(jax.ShapeDtypeStruct(gathered.shape[1:], F32), jax.ShapeDtypeStruct(gathered_gw.shape[1:], F32),
                   jax.ShapeDtypeStruct((1, 6 * D_MODEL), F32), jax.ShapeDtypeStruct((D_MODEL, ADA_COLS), F32)),
        in_specs=[vm] * 4, out_specs=(vm, vm, vm, vm),
        compiler_params=pltpu.CompilerParams(vmem_limit_bytes=V7X_VMEM_LIMIT),
    )(gathered, gathered_gw, c_all, dmod_cols)


SMR_LN1W, SMR_LN1B, SMR_LN2W, SMR_LN2B, SMR_NORMS, SMR_MISC = 6, 7, 8, 9, 10, 11


def _adamw_small(gsum, g_b_ada, g_ggw, params, moms, vels):
    n = len(params)

    def body(*refs):
        gsum_ref, gb_ref, gw_ref = refs[:3]
        w_refs, m_refs, v_refs = refs[3:3 + n], refs[3 + n:3 + 2 * n], refs[3 + 2 * n:3 + 3 * n]
        outs = refs[3 + 3 * n:]
        g_refs, d_refs, mo_refs, vo_refs = outs[:n - 1], outs[n - 1:2 * n - 1], outs[2 * n - 1:3 * n - 1], outs[3 * n - 1:]
        grads = [gb_ref[...],
                 gsum_ref[SMR_NORMS:SMR_NORMS + 1, 0:RET_W],
                 gsum_ref[SMR_MISC:SMR_MISC + 1, 0:GLA_KW],
                 gsum_ref[SMR_NORMS:SMR_NORMS + 1, RET_W:RET_W + GLA_VW],
                 gsum_ref[SMR_LN1W:SMR_LN1W + 1, :], gsum_ref[SMR_LN1B:SMR_LN1B + 1, :],
                 gsum_ref[SMR_LN2W:SMR_LN2W + 1, :], gsum_ref[SMR_LN2B:SMR_LN2B + 1, :],
                 gw_ref[...]]
        for i in range(n):
            delta, mn, vn = _adam_math(w_refs[i][...], grads[i], m_refs[i][...], v_refs[i][...])
            if i < n - 1:
                g_refs[i][...] = grads[i]
            d_refs[i][...] = delta
            mo_refs[i][...] = mn
            vo_refs[i][...] = vn

    vm = pl.BlockSpec(memory_space=pltpu.VMEM)
    shapes = [jax.ShapeDtypeStruct(p.shape, F32) for p in params]
    n_in = 3 + 3 * n
    out_shape = tuple(shapes[:n - 1] + shapes * 3)
    return pl.pallas_call(
        body, name="adamw_small", out_shape=out_shape,
        in_specs=[vm] * n_in, out_specs=tuple([vm] * len(out_shape)),
        compiler_params=pltpu.CompilerParams(vmem_limit_bytes=V7X_VMEM_LIMIT),
    )(gsum, g_b_ada, g_ggw, *params, *moms, *vels)


def kernel(x, c, w_ada, b_ada, w_in, ret_norm_w, gla_gate_w, gla_gate_b, gla_norm_w, w_out, ln1_w, ln1_b, w_ff1, w_ff2, ln2_w, ln2_b, loss_target, m_w_ada, m_b_ada, m_w_in, m_ret_norm_w, m_gla_gate_w, m_gla_gate_b, m_gla_norm_w, m_w_out, m_ln1_w, m_ln1_b, m_w_ff1, m_w_ff2, m_ln2_w, m_ln2_b, v_w_ada, v_b_ada, v_w_in, v_ret_norm_w, v_gla_gate_w, v_gla_gate_b, v_gla_norm_w, v_w_out, v_ln1_w, v_ln1_b, v_w_ff1, v_w_ff2, v_ln2_w, v_ln2_b):
    t = x.shape[1]
    xi, yi, ci = _my_coords()
    me = 4 * xi + 2 * yi + ci
    x2d = x[0]
    tgt = loss_target[0]

    c_ext = jnp.concatenate([c, gla_gate_w[0].reshape(1, GATE_RANK * GLA_KW // N_DEV)], axis=1)
    b_l = lax.dynamic_slice(b_ada, (0, me * ADA_COLS), (1, ADA_COLS))
    c_all3, mod_all, wi_g, ada_token = _adaln_mod(c_ext, w_ada[0], b_l, w_in[0].T.astype(BF16))

    wg = _exchange_start([(w_out[0] + ada_token[0, 0]).astype(BF16), w_ff1[0].astype(BF16), w_ff2[0].astype(BF16)],
                         True, "wgather_start")

    c_all = c_all3[:, 0, :D_MODEL]
    gate_w = c_all3[:, 0, D_MODEL:].reshape(N_DEV, GATE_RANK, GLA_KW // N_DEV)
    gate_w = gate_w.transpose(1, 0, 2).reshape(GATE_RANK, GLA_KW)
    gw_pad = jnp.zeros((V7X_LANES, GLA_KW), F32).at[:GATE_RANK].set(gate_w)
    mod = lax.dynamic_slice(mod_all, (0, me, 0), (N_DEV, 1, ADA_COLS)).reshape(6, D_MODEL)
    shift1, scale1, gate1, shift2, scale2, gate2 = [mod[i:i + 1] for i in range(6)]

    w_in_t = wi_g.reshape(D_IN, D_MODEL)

    tables = _ret_tables(t, min(RET_SUB, t))

    sc1p = 1.0 + scale1
    proj, u = _inproj_fwd(x2d, sc1p, shift1 + wg[4][0, 0], w_in_t)
    mixed, oraw, qrb, krb, rst, sst = _mixer_fwd(proj, tables, gw_pad, gla_gate_b, ret_norm_w, gla_norm_w)
    wo_g, w1_b, w2_g = _exchange_wait(*wg[:4], mixed, True, "wgather_wait")
    w_out_b = wo_g.reshape(D_MODEL, D_MODEL)
    w2_b = w2_g.reshape(D_FF, D_MODEL)
    vec_f = jnp.concatenate([gate1, 1.0 + scale2, shift2, gate2, ln1_w, ln1_b, ln2_w, ln2_b], axis=0)
    m, x1n, rstd1, u2, a, df, dh2, acc_f = _mid_fwd(mixed, x2d, tgt, vec_f, w_out_b, w1_b, w2_b)

    vec_b = jnp.concatenate([gate1, 1.0 + scale2, ln1_w, ln1_b, jnp.zeros((4, D_MODEL), F32)], axis=0)
    da, dm, dmix, dxa, acc_b = _ffn_bwd(df, a, dh2, x1n, rstd1, m, vec_b, w_out_b, w1_b, w2_b)
    dw2 = _matmul_tn(a, df, 2048, 1024, 2048, "tn_dw2", relu_sq=True)
    dw1 = _matmul_tn(u2, da, 1024, 2048, 2048, "tn_dw1", col_slab=FF_COLS)
    dwo = _matmul_tn(mixed, dm, 1024, 1024, 2048, "tn_dwout")
    gx = _exchange_start([dwo.reshape(N_DEV, OUT_ROWS, D_MODEL), dw1, dw2.reshape(N_DEV, FF_COLS, D_MODEL)], False,
                         "gradx_start")
    dproj, dgw, dvec = _mixer_bwd(dmix, proj, qrb, krb, oraw, tables, rst, sst, gw_pad,
                                  gla_gate_b + gx[4][0, 0], ret_norm_w, gla_norm_w)
    dwi_s = _matmul_tn(dproj, u, D_IN_PAD, 1024, 1024, "tn_dwin", out_rows=IN_COLS)
    gi = _exchange_start([dwi_s], False, "gradin_start")
    grad_x, acc_i = _inproj_bwd(dproj, x2d, dxa, sc1p + gi[4][0, 0], w_in_t)

    loss_part = jnp.sum(acc_f[3])
    small = jnp.concatenate([
        acc_i[1:2], acc_i[0:1], acc_b[4:5], acc_b[1:2], acc_b[0:1], acc_f[2:3],
        acc_b[2:3], acc_b[3:4], acc_f[0:1], acc_f[1:2],
        jnp.concatenate([dvec[0:1], dvec[1:2]], axis=1),
        jnp.concatenate([dvec[2:3, :GLA_KW], jnp.full((1, 128), loss_part, F32),
                         jnp.zeros((1, D_MODEL - GLA_KW - 128), F32)], axis=1),
        jnp.zeros((4, D_MODEL), F32)], axis=0)
    sg = _exchange_start([small, dgw[:GATE_RANK]], True, "small_start")

    r_wo, r_w1, r_w2 = _exchange_wait(*gx[:4], sg[4], False, "gradx_wait")
    r_wi, = _exchange_wait(*gi[:4], sg[4], False, "gradin_wait")
    big = [_adamw(w[0], r, m_[0], v_[0], nm) for w, r, m_, v_, nm in (
        (w_out, r_wo, m_w_out, v_w_out, "adamw_out"),
        (w_ff1, r_w1, m_w_ff1, v_w_ff1, "adamw_ff1"), (w_ff2, r_w2, m_w_ff2, v_w_ff2, "adamw_ff2"))]
    big_in = _adamw(w_in[0].T, r_wi, m_w_in[0].T, v_w_in[0].T, "adamw_in")
    big = [tuple(b.T for b in big_in)] + big
    g_big, d_big, m_big, v_big = [[b[i][None] for b in big] for i in range(4)]

    small_all, gw_all = _exchange_wait(*sg[:4], big_in[1], True, "small_wait")
    dmod_all = small_all[:, :6].reshape(N_DEV, 6 * D_MODEL)
    dmod_cols = lax.dynamic_slice(dmod_all, (0, me * ADA_COLS), (N_DEV, ADA_COLS))
    ssum, gw_sum, g_b_ada, g_w_ada = _small_reduce(small_all, gw_all, c_all, dmod_cols)
    loss = ssum[SMR_MISC, GLA_KW]
    g_ggw = lax.dynamic_slice(gw_sum, (0, me * (GLA_KW // N_DEV)), (GATE_RANK, GLA_KW // N_DEV))[None]

    small_w = [b_ada, ret_norm_w, gla_gate_b, gla_norm_w, ln1_w, ln1_b, ln2_w, ln2_b, gla_gate_w]
    small_m = [m_b_ada, m_ret_norm_w, m_gla_gate_b, m_gla_norm_w, m_ln1_w, m_ln1_b, m_ln2_w, m_ln2_b, m_gla_gate_w]
    small_v = [v_b_ada, v_ret_norm_w, v_gla_gate_b, v_gla_norm_w, v_ln1_w, v_ln1_b, v_ln2_w, v_ln2_b, v_gla_gate_w]
    res = _adamw_small(ssum, g_b_ada, g_ggw, small_w, small_m, small_v)
    small_g = list(res[:8]) + [g_ggw]
    d_small, m_small, v_small = list(res[8:17]), list(res[17:26]), list(res[26:35])

    _, d_w_ada, nm_w_ada, nv_w_ada = _adamw(w_ada[0], g_w_ada[None], m_w_ada[0], v_w_ada[0], "adamw_ada")

    def ordered(w_ada_v, small_vals, big_vals):
        b_ada_v, rnw_v, ggb_v, gnw_v, l1w_v, l1b_v, l2w_v, l2b_v, ggw_v = small_vals
        wi_v, wo_v, w1_v, w2_v = big_vals
        return [w_ada_v, b_ada_v, wi_v, rnw_v, ggw_v, ggb_v, gnw_v, wo_v, l1w_v, l1b_v, w1_v, w2_v, l2w_v, l2b_v]

    grads = ordered(g_w_ada[None], small_g, g_big)
    deltas = ordered(d_w_ada[None], d_small, d_big)
    new_m = ordered(nm_w_ada[None], m_small, m_big)
    new_v = ordered(nv_w_ada[None], v_small, v_big)
    return (loss, grad_x[None], *grads, *deltas, *new_m, *new_v)
```

```python
import numpy as np
import jax
import jax.numpy as jnp
from jax import lax
from jax.experimental import pallas as pl
from jax.experimental.pallas import tpu as pltpu

F32 = jnp.float32
BF16 = jnp.bfloat16
MESH = pl.DeviceIdType.MESH
HIGHEST = lax.Precision.HIGHEST

N_DEV = 8
D_MODEL = 1024
CHUNK = 64
RET_HEADS = 4
RET_D = 128
GLA_HEADS = 4
GLA_DK = 64
GLA_DV = 128
GLA_KW = GLA_HEADS * GLA_DK
RET_W = RET_HEADS * RET_D
GLA_VW = GLA_HEADS * GLA_DV
V7X_LANES = 128
V7X_SUBLANES = 8
GATE_RANK = 16
GATE_TAU = 16.0
D_FF = 4096
LN_EPS = 1e-5
ALPHA = (2.0 * 1) ** 0.25
D_IN = 3600
D_IN_PAD = 3712
ADA_COLS = 6 * D_MODEL // N_DEV
IN_COLS = D_IN // N_DEV
FF_COLS = D_FF // N_DEV
OUT_ROWS = D_MODEL // N_DEV

OFF_RQ, OFF_RK, OFF_RV, OFF_RG = 0, RET_W, 2 * RET_W, 3 * RET_W
OFF_GQ = 4 * RET_W
OFF_GK = OFF_GQ + GLA_KW
OFF_GV = OFF_GK + GLA_KW
OFF_GG = OFF_GV + GLA_VW
OFF_GLR = OFF_GG + GLA_VW

ADAM_LR, ADAM_B1, ADAM_B2, ADAM_EPS, ADAM_WD, ADAM_STEP = 0.001, 0.9, 0.999, 1e-08, 0.01, 10

V7X_VMEM_LIMIT = 62 * 1024 * 1024

ROW_TILE = 512
PROJ_TILE = 512
MIX_TILE = 512
RET_SUB = 256
GLA_SUB = 128


def _log_gamma(h):
    return float(np.log(np.float32(1.0) - np.float32(2.0) ** np.float32(-5.0 - h)))


def _my_coords():
    return lax.axis_index("x"), lax.axis_index("y"), lax.axis_index("c")


def _flip(v, bit):
    return 1 - v if bit else v


def _peer(k):
    x, y, c = _my_coords()
    px, py, pc = _flip(x, (k >> 2) & 1), _flip(y, (k >> 1) & 1), _flip(c, k & 1)
    return (px, py, pc), 4 * px + 2 * py + pc


def _dot(a, b, dims=(((1,), (0,)), ((), ())), precision=None):
    return lax.dot_general(a, b, dims, precision=precision, preferred_element_type=F32)


NN = (((1,), (0,)), ((), ()))
NT = (((1,), (1,)), ((), ()))
TN = (((0,), (0,)), ((), ()))


def _split_bf16(v, parts):
    out = []
    for _ in range(parts):
        p = v.astype(BF16)
        out.append(p)
        v = v - p.astype(F32)
    return out


def _dot_split(a, b, dims, a_exact=False):
    if a_exact:
        ab = a.astype(BF16)
        return sum(_dot(ab, p, dims) for p in _split_bf16(b, 2))
    a_hi, a_lo = _split_bf16(a, 2)
    b_hi, b_lo = _split_bf16(b, 2)
    return _dot(a_hi, b_hi, dims) + _dot(a_hi, b_lo, dims) + _dot(a_lo, b_hi, dims)


def _sigmoid(x):
    return 1.0 / (1.0 + jnp.exp(-x))


def _ln_stats(x):
    mu = jnp.mean(x, axis=-1, keepdims=True)
    xc = x - mu
    var = jnp.mean(xc * xc, axis=-1, keepdims=True)
    rstd = lax.rsqrt(var + LN_EPS)
    return xc * rstd, rstd


def _ln_bwd(dyh, xh, rstd):
    return rstd * (dyh - jnp.mean(dyh, axis=-1, keepdims=True) - xh * jnp.mean(dyh * xh, axis=-1, keepdims=True))


def _adaln_mod(c_ext, w_ada_l, b_l, w_in_l):
    width = c_ext.shape[1]

    def body(c_ref, w_ref, b_ref, wi_ref, call_ref, mod_ref, wig_ref, token_ref, s1, r1, s2, r2, gs, gr, gl):
        gather = _TwoLevelGather([wi_ref], [wig_ref], gs, gr, gl)
        gather.start()
        token_ref[...] = jnp.zeros_like(token_ref)
        x, y, c = _my_coords()
        me = 4 * x + 2 * y + c
        call_ref[me] = c_ref[...]
        sends = []
        for k in range(1, N_DEV):
            peer, _ = _peer(k)
            cp = pltpu.make_async_remote_copy(c_ref, call_ref.at[me], s1.at[k - 1], r1.at[k - 1],
                                              device_id=peer, device_id_type=MESH)
            cp.start()
            sends.append(cp)
        for k in range(1, N_DEV):
            peer, pid = _peer(k)
            pltpu.make_async_remote_copy(c_ref, call_ref.at[pid], s1.at[k - 1], r1.at[k - 1],
                                         device_id=peer, device_id_type=MESH).wait_recv()
        for cp in sends:
            cp.wait_send()
        row = lax.broadcasted_iota(jnp.int32, (N_DEV, D_MODEL), 0)
        call = jnp.zeros((N_DEV, D_MODEL), F32)
        for j in range(N_DEV):
            call = jnp.where(row == j, jnp.broadcast_to(call_ref[j][:, :D_MODEL], (N_DEV, D_MODEL)), call)
        sc = call * _sigmoid(call)
        mod = _dot(sc, w_ref[...], NN, HIGHEST) + b_ref[...]
        mod_ref[me] = mod
        sends = []
        for k in range(1, N_DEV):
            peer, _ = _peer(k)
            cp = pltpu.make_async_remote_copy(mod_ref.at[me], mod_ref.at[me], s2.at[k - 1], r2.at[k - 1],
                                              device_id=peer, device_id_type=MESH)
            cp.start()
            sends.append(cp)
        for k in range(1, N_DEV):
            peer, pid = _peer(k)
            pltpu.make_async_remote_copy(mod_ref.at[pid], mod_ref.at[pid], s2.at[k - 1], r2.at[k - 1],
                                         device_id=peer, device_id_type=MESH).wait_recv()
        for cp in sends:
            cp.wait_send()
        gather.forward()
        gather.finish()

    vm = pl.BlockSpec(memory_space=pltpu.VMEM)
    hbm = pl.BlockSpec(memory_space=pl.ANY)
    return pl.pallas_call(
        body, name="adaln_mod",
        out_shape=(jax.ShapeDtypeStruct((N_DEV, 1, width), F32),
                   jax.ShapeDtypeStruct((N_DEV, N_DEV, ADA_COLS), F32),
                   jax.ShapeDtypeStruct((N_DEV, *w_in_l.shape), w_in_l.dtype),
                   jax.ShapeDtypeStruct((8, 128), F32)),
        in_specs=[vm, vm, vm, hbm], out_specs=(vm, vm, hbm, vm),
        scratch_shapes=[pltpu.SemaphoreType.DMA((N_DEV - 1,))] * 4
        + [pltpu.SemaphoreType.DMA((7,)), pltpu.SemaphoreType.DMA((7,)), pltpu.SemaphoreType.DMA((1,))],
        compiler_params=pltpu.CompilerParams(vmem_limit_bytes=V7X_VMEM_LIMIT),
    )(c_ext, w_ada_l, b_l, w_in_l)


class _TwoLevelGather:
    def __init__(self, x_refs, out_refs, send_sems, recv_sems, local_sems):
        self.x_refs, self.out_refs = x_refs, out_refs
        self.send_sems, self.recv_sems, self.local_sems = send_sems, recv_sems, local_sems
        x, y, c = _my_coords()
        self.c = c
        self.me, self.sibling = (x, y, c), (x, y, 1 - c)
        self.chips = [(1 - x, y), (x, 1 - y), (1 - x, 1 - y)]

    def _copy(self, a, k, block, to, src=None):
        px, py, pc = block
        slab = self.out_refs[a].at[4 * px + 2 * py + pc]
        return pltpu.make_async_remote_copy(
            src_ref=slab if src is None else src, dst_ref=slab,
            send_sem=self.send_sems.at[7 * a + k], recv_sem=self.recv_sems.at[7 * a + k],
            device_id=to, device_id_type=MESH)

    def _mine(self, a):
        px, py, pc = self.me
        return pltpu.make_async_copy(self.x_refs[a], self.out_refs[a].at[4 * px + 2 * py + pc], self.local_sems.at[a])

    def _first(self, a):
        cps = [self._copy(a, 0, self.me, self.sibling, src=self.x_refs[a])]
        cps += [self._copy(a, 1 + j, self.me, (*chip, self.c), src=self.x_refs[a]) for j, chip in enumerate(self.chips)]
        return cps

    def _passed(self, a):
        return [self._copy(a, 4 + j, (*chip, self.c), self.sibling) for j, chip in enumerate(self.chips)]

    def start(self):
        for a in range(len(self.x_refs)):
            self._mine(a).start()
            for cp in self._first(a):
                cp.start()

    def forward(self):
        for a in range(len(self.x_refs)):
            passed = self._passed(a)
            for j, chip in enumerate(self.chips):
                self._copy(a, 1 + j, (*chip, self.c), self.me).wait_recv()
                passed[j].start()

    def finish(self):
        for a in range(len(self.x_refs)):
            self._copy(a, 0, self.sibling, self.me).wait_recv()
            for j, chip in enumerate(self.chips):
                self._copy(a, 4 + j, (*chip, 1 - self.c), self.me).wait_recv()
            for cp in self._first(a) + self._passed(a):
                cp.wait_send()
            self._mine(a).wait()


def _exchange_copy(src_refs, land_refs, send_sems, recv_sems, a, k, gather, receiving):
    x, y, c = _my_coords()
    me = 4 * x + 2 * y + c
    peer, pid = _peer(k)
    src = src_refs[a] if gather else src_refs[a].at[pid]
    dst = land_refs[a].at[pid if receiving else me]
    return pltpu.make_async_remote_copy(src, dst, send_sems.at[7 * a + k - 1], recv_sems.at[7 * a + k - 1],
                                        device_id=peer, device_id_type=MESH)


def _exchange_start(srcs, gather, name):
    n = len(srcs)
    xi, yi, ci = _my_coords()
    me = 4 * xi + 2 * yi + ci
    lands = []
    for s in srcs:
        own = s[None] if gather else lax.dynamic_slice_in_dim(s, me, 1, axis=0)
        lands.append(lax.dynamic_update_slice_in_dim(lax.empty((N_DEV, *own.shape[1:]), s.dtype), own, me, axis=0))

    def body(*refs):
        src_refs, land_refs, send_sems, recv_sems, token = refs[:n], refs[n:2 * n], refs[2 * n], refs[2 * n + 1], refs[-1]
        for a in range(n):
            for k in range(1, N_DEV):
                _exchange_copy(src_refs, land_refs, send_sems, recv_sems, a, k, gather, receiving=False).start()
        token[...] = jnp.zeros_like(token)

    hbm = pl.BlockSpec(memory_space=pltpu.HBM)
    sem = pl.BlockSpec(memory_space=pltpu.SEMAPHORE)
    res = pl.pallas_call(
        body, name=name,
        out_shape=(pltpu.SemaphoreType.DMA((7 * n,)), pltpu.SemaphoreType.DMA((7 * n,)),
                   *[pltpu.HBM(v.shape, v.dtype) for v in srcs + lands], jax.ShapeDtypeStruct((8, 128), F32)),
        in_specs=[hbm] * (2 * n),
        out_specs=(sem, sem, *([hbm] * (2 * n)), pl.BlockSpec(memory_space=pltpu.VMEM)),
        input_output_aliases={i: 2 + i for i in range(2 * n)},
        compiler_params=pltpu.CompilerParams(has_side_effects=pltpu.SideEffectType.DATAFLOW_SIDE_EFFECTING),
    )(*[pltpu.with_memory_space_constraint(v, pltpu.HBM) for v in srcs + lands])
    return res[0], res[1], list(res[2:2 + n]), list(res[2 + n:2 + 2 * n]), res[-1]


def _exchange_wait(send_sems, recv_sems, srcs, lands, after, gather, name):
    n = len(srcs)

    def body(*refs):
        src_refs, land_refs, s_sems, r_sems = refs[:n], refs[n:2 * n], refs[2 * n], refs[2 * n + 1]
        for a in range(n):
            for k in range(1, N_DEV):
                _exchange_copy(src_refs, land_refs, s_sems, r_sems, a, k, gather, receiving=False).wait_send()
                _exchange_copy(src_refs, land_refs, s_sems, r_sems, a, k, gather, receiving=True).wait_recv()

    hbm = pl.BlockSpec(memory_space=pltpu.HBM)
    sem = pl.BlockSpec(memory_space=pltpu.SEMAPHORE)
    res = pl.pallas_call(
        body, name=name,
        out_shape=tuple(pltpu.HBM(v.shape, v.dtype) for v in srcs + lands),
        in_specs=[hbm] * (2 * n) + [sem, sem, pl.BlockSpec(memory_space=pl.ANY)],
        out_specs=tuple([hbm] * (2 * n)),
        input_output_aliases={i: i for i in range(2 * n)},
        compiler_params=pltpu.CompilerParams(has_side_effects=pltpu.SideEffectType.DATAFLOW_SIDE_EFFECTING),
    )(*srcs, *lands, send_sems, recv_sems, after)
    return list(res[n:])


def _load_resident(step_is_first, pairs, sem):
    @pl.when(step_is_first)
    def _():
        copies = [pltpu.make_async_copy(src, dst, sem.at[i]) for i, (src, dst) in enumerate(pairs)]
        for cp in copies:
            cp.start()
        for cp in copies:
            cp.wait()


def _load_w_in_t(step_is_first, w_hbm, w_vmem, sem):
    @pl.when(step_is_first)
    def _():
        w_vmem[D_IN:, :] = jnp.zeros((D_IN_PAD - D_IN, D_MODEL), BF16)
    _load_resident(step_is_first, [(w_hbm, w_vmem.at[pl.ds(0, D_IN)])], sem)


def _inproj_fwd(x2d, sc1p, sh1, w_in_t):
    t = x2d.shape[0]
    tm = min(PROJ_TILE, t)

    def body(x_ref, sc_ref, sh_ref, w_hbm, proj_ref, u_ref, w_vmem, sem):
        _load_w_in_t(pl.program_id(0) == 0, w_hbm, w_vmem, sem)
        xh, _ = _ln_stats(x_ref[...])
        ub = (xh * sc_ref[...] + sh_ref[...]).astype(BF16)
        u_ref[...] = ub
        proj_ref[...] = _dot(ub, w_vmem[...], NT)

    row = lambda i: (i, 0)
    fix = lambda i: (0, 0)
    return pl.pallas_call(
        body, name="inproj_fwd", grid=(t // tm,),
        in_specs=[pl.BlockSpec((tm, D_MODEL), row), pl.BlockSpec((1, D_MODEL), fix), pl.BlockSpec((1, D_MODEL), fix),
                  pl.BlockSpec(memory_space=pl.ANY)],
        out_specs=(pl.BlockSpec((tm, D_IN_PAD), row), pl.BlockSpec((tm, D_MODEL), row)),
        out_shape=(jax.ShapeDtypeStruct((t, D_IN_PAD), F32), jax.ShapeDtypeStruct((t, D_MODEL), BF16)),
        scratch_shapes=[pltpu.VMEM((D_IN_PAD, D_MODEL), BF16), pltpu.SemaphoreType.DMA((1,))],
        compiler_params=pltpu.CompilerParams(dimension_semantics=("arbitrary",), vmem_limit_bytes=V7X_VMEM_LIMIT),
    )(x2d, sc1p, sh1, w_in_t)


CHUNK_SHIFT = CHUNK.bit_length() - 1


def _ret_tables(t, tl):
    r = lax.broadcasted_iota(jnp.int32, (tl, tl), 0)
    c = lax.broadcasted_iota(jnp.int32, (tl, tl), 1)
    allowed = jnp.right_shift(c, CHUNK_SHIFT) <= jnp.right_shift(r, CHUNK_SHIFT)
    dist = jnp.abs(r - c).astype(F32)
    rowf = lax.broadcasted_iota(jnp.int32, (tl, RET_D), 0).astype(F32)
    lgs = [_log_gamma(h) for h in range(RET_HEADS)]
    dec = jnp.stack([jnp.where(allowed, jnp.exp(lg * dist), 0.0) for lg in lgs])
    qkd = jnp.stack([jnp.exp(lg * (rowf + 1.0)) for lg in lgs] + [jnp.exp(lg * (tl - 1.0 - rowf)) for lg in lgs])
    inv = 1.0 / (10000.0 ** jnp.linspace(0.0, 1.0, RET_D // 2, dtype=F32))
    off = jnp.arange(tl, dtype=F32)[:, None] * inv[None, :]
    start = (jnp.arange(t // tl, dtype=F32) * tl)[:, None] * inv[None, :]
    co, so = jnp.cos(off), jnp.sin(off)
    rot_in = jnp.stack([jnp.concatenate([co, co], 1), jnp.concatenate([so, so], 1),
                        jnp.concatenate([-co, co], 1), jnp.concatenate([-so, so], 1)])
    cs, ss = jnp.cos(start), jnp.sin(start)
    rot_tile = jnp.concatenate([cs, cs, ss, ss], axis=1)
    rot_tile = jnp.broadcast_to(rot_tile[:, None, :], (t // tl, 8, 2 * RET_D))
    return dec, qkd, rot_in, rot_tile


def _tile_gammas(tl):
    return [float(np.exp(np.float32(_log_gamma(h)) * np.float32(tl))) for h in range(RET_HEADS)]


def _tile_rotary(rot_in_ref, rot_tile_ref, j):
    ca, sa = rot_tile_ref[j, 0:1, 0:RET_D], rot_tile_ref[j, 0:1, RET_D:2 * RET_D]
    cosv = ca * rot_in_ref[0] - sa * rot_in_ref[1]
    sinv = sa * rot_in_ref[2] + ca * rot_in_ref[3]
    return cosv, sinv


def _gla_consts(tl):
    r = lax.broadcasted_iota(jnp.int32, (tl, tl), 0)
    c = lax.broadcasted_iota(jnp.int32, (tl, tl), 1)
    ltri = (c <= r).astype(F32)
    utri = (c >= r).astype(F32)
    lane = lax.broadcasted_iota(jnp.int32, (1, GLA_KW), 1)
    hmask = [((lane >= h * GLA_DK) & (lane < (h + 1) * GLA_DK)).astype(F32) for h in range(GLA_HEADS)]
    rs = lax.broadcasted_iota(jnp.int32, (GLA_HEADS * tl, tl), 0) & (tl - 1)
    cs = lax.broadcasted_iota(jnp.int32, (GLA_HEADS * tl, tl), 1)
    lower = cs <= rs
    same = jnp.right_shift(cs, CHUNK_SHIFT) == jnp.right_shift(rs, CHUNK_SHIFT)
    upper = jnp.logical_and(jnp.logical_not(lower), same)
    return dict(ltri=ltri, utri=utri, hmask=hmask, lower=lower, upper=upper)


def _tile_rows(j, tl):
    return pl.ds(j * tl, tl) if isinstance(j, int) else pl.ds(pl.multiple_of(j * tl, tl), tl)


def _for_tiles(cps, fn):
    for j in range(cps):
        fn(j, 0)


def _rotate(v, cosv, sinv):
    return v * cosv + pltpu.roll(v, RET_D // 2, 1) * sinv


def _rotate_t(d, cosv, sinv):
    return d * cosv + pltpu.roll(d * sinv, RET_D // 2, 1)


def _stack_heads(v, hmask):
    return jnp.concatenate([v * hmask[h] for h in range(GLA_HEADS)], axis=0)


def _gla_gates(glr, gw, gb, ltri, tl):
    z = _dot_split(glr, gw, NN) + gb
    la = (jnp.minimum(z, 0.0) - jnp.log(1.0 + jnp.exp(-jnp.abs(z)))) * (1.0 / GATE_TAU)
    b = _dot_split(ltri, la, NN, a_exact=True)
    level = b[tl // 2 - 1:tl // 2, :]
    ep = jnp.exp(jnp.clip(b - level, -80.0, 80.0))
    em = jnp.exp(jnp.clip(level - b, -80.0, 80.0))
    bl = b[tl - 1:tl, :]
    return z, b, bl, ep, em


def _mixer_fwd(proj, tables, gw_pad, gb, rnw, gnw):
    t = proj.shape[0]
    tc = min(MIX_TILE, t)
    tr, tg = min(RET_SUB, tc), min(GLA_SUB, tc)
    nsteps = t // tc
    scale_r = RET_D ** -0.5
    scale_g = GLA_DK ** -0.5
    gammas = _tile_gammas(tr)

    def body(rq_ref, rk_ref, rv_ref, rg_ref, gq_ref, gk_ref, gv_ref, gg_ref, glr_ref,
             dec_ref, qkd_ref, rot_in_ref, rot_tile_ref, gw_ref, gb_ref, rnw_ref, gnw_ref,
             mix_ref, oraw_ref, qrb_ref, krb_ref, rst_ref, sst_ref, r_scr, s_scr):
        @pl.when(pl.program_id(0) == 0)
        def _():
            r_scr[...] = jnp.zeros_like(r_scr)
            s_scr[...] = jnp.zeros_like(s_scr)

        gla_k = _gla_consts(tg)

        def ret_tile(j, carry):
            rows = _tile_rows(j, tr)
            cosv, sinv = _tile_rotary(rot_in_ref, rot_tile_ref, j)
            for h in range(RET_HEADS):
                cols = slice(h * RET_D, (h + 1) * RET_D)
                qr = _rotate(rq_ref[rows, cols], cosv, sinv) * scale_r
                kr = _rotate(rk_ref[rows, cols], cosv, sinv)
                vb = rv_ref[rows, cols].astype(BF16)
                qb, kb = qr.astype(BF16), kr.astype(BF16)
                qrb_ref[rows, cols] = qb
                krb_ref[rows, cols] = kb
                p = _dot(qb, kb, NT) * dec_ref[h]
                rp = r_scr[cols, :]
                o = _dot(p.astype(BF16), vb) + _dot((qr * qkd_ref[h]).astype(BF16), rp.astype(BF16))
                rst_ref[j, cols, :] = rp
                r_scr[cols, :] = gammas[h] * rp + _dot((kr * qkd_ref[RET_HEADS + h]).astype(BF16), vb, TN)
                oraw_ref[rows, cols] = o
                oc = o - jnp.mean(o, axis=-1, keepdims=True)
                n = oc * lax.rsqrt(jnp.mean(oc * oc, axis=-1, keepdims=True) + LN_EPS)
                g = rg_ref[rows, cols]
                mix_ref[rows, cols] = (n * rnw_ref[:, cols] * (g * _sigmoid(g))).astype(BF16)
            return carry

        def gla_tile(j, carry):
            k = gla_k
            tl = tg
            rows = _tile_rows(j, tg)
            _, b, bl, ep, em = _gla_gates(glr_ref[rows, :], gw_ref[...], gb_ref[...], k["ltri"], tl)
            qs = gq_ref[rows, :] * scale_g
            kk = gk_ref[rows, :]
            x_all = _dot(_stack_heads(qs * ep, k["hmask"]).astype(BF16), (kk * em).astype(BF16), NT)
            y_all = _dot(_stack_heads(qs * em, k["hmask"]).astype(BF16), (kk * ep).astype(BF16), NT)
            a_all = jnp.where(k["lower"], x_all, jnp.where(k["upper"], y_all, 0.0)).astype(BF16)
            st = s_scr[...]
            oq = _dot(_stack_heads(qs * jnp.exp(b), k["hmask"]).astype(BF16), st.astype(BF16), NT)
            kg = kk * jnp.exp(bl - b)
            sst_ref[j] = st
            st_new = st * jnp.exp(bl)
            for h in range(GLA_HEADS):
                cols = slice(h * GLA_DV, (h + 1) * GLA_DV)
                hr = slice(h * tl, (h + 1) * tl)
                vb = gv_ref[rows, cols].astype(BF16)
                o = _dot(a_all[hr, :], vb) + oq[hr, :]
                st_new = st_new + _dot(vb, (kg * k["hmask"][h]).astype(BF16), TN)
                ocols = slice(RET_W + h * GLA_DV, RET_W + (h + 1) * GLA_DV)
                oraw_ref[rows, ocols] = o
                n = o * lax.rsqrt(jnp.mean(o * o, axis=-1, keepdims=True) + LN_EPS)
                g = gg_ref[rows, cols]
                mix_ref[rows, ocols] = (n * gnw_ref[:, cols] * (g * _sigmoid(g))).astype(BF16)
            s_scr[...] = st_new
            return carry

        _for_tiles(tc // tr, ret_tile)
        _for_tiles(tc // tg, gla_tile)

    def col(width, off):
        return pl.BlockSpec((tc, width), lambda i, o=off // width: (i, o))

    fix = lambda i: (0, 0)
    fix3 = lambda i: (0, 0, 0)
    dec, qkd, rot_in, rot_tile = tables
    in_specs = [col(RET_W, OFF_RQ), col(RET_W, OFF_RK), col(RET_W, OFF_RV), col(RET_W, OFF_RG),
                col(GLA_KW, OFF_GQ), col(GLA_KW, OFF_GK), col(GLA_VW, OFF_GV), col(GLA_VW, OFF_GG),
                col(V7X_LANES, OFF_GLR),
                pl.BlockSpec(dec.shape, fix3), pl.BlockSpec(qkd.shape, fix3), pl.BlockSpec(rot_in.shape, fix3),
                pl.BlockSpec((tc // tr, 8, 2 * RET_D), lambda i: (i, 0, 0)),
                pl.BlockSpec((V7X_LANES, GLA_KW), fix), pl.BlockSpec((1, GLA_KW), fix),
                pl.BlockSpec((1, RET_W), fix), pl.BlockSpec((1, GLA_VW), fix)]
    half = pl.BlockSpec((tc, RET_W), lambda i: (i, 0))
    out_specs = (pl.BlockSpec((tc, D_MODEL), lambda i: (i, 0)), pl.BlockSpec((tc, D_MODEL), lambda i: (i, 0)),
                 half, half,
                 pl.BlockSpec((tc // tr, RET_W, RET_D), lambda i: (i, 0, 0)),
                 pl.BlockSpec((tc // tg, GLA_DV, GLA_KW), lambda i: (i, 0, 0)))
    out_shape = (jax.ShapeDtypeStruct((t, D_MODEL), BF16), jax.ShapeDtypeStruct((t, D_MODEL), F32),
                 jax.ShapeDtypeStruct((t, RET_W), BF16), jax.ShapeDtypeStruct((t, RET_W), BF16),
                 jax.ShapeDtypeStruct((t // tr, RET_W, RET_D), F32),
                 jax.ShapeDtypeStruct((t // tg, GLA_DV, GLA_KW), F32))
    return pl.pallas_call(
        body, name="mixer_fwd", grid=(nsteps,), in_specs=in_specs, out_specs=out_specs, out_shape=out_shape,
        scratch_shapes=[pltpu.VMEM((RET_W, RET_D), F32), pltpu.VMEM((GLA_DV, GLA_KW), F32)],
        compiler_params=pltpu.CompilerParams(dimension_semantics=("arbitrary",), vmem_limit_bytes=V7X_VMEM_LIMIT),
    )(*([proj] * 9), dec, qkd, rot_in, rot_tile, gw_pad, gb, rnw, gnw)


def _mid_fwd(mixed, x2d, target, vecs, w_out_b, w1_b, w2_b):
    t = x2d.shape[0]
    tm = min(ROW_TILE, t)

    def body(mix_ref, x_ref, tgt_ref, v_ref, wo_hbm, w1_hbm, w2_hbm,
             m_ref, x1n_ref, rstd_ref, u2_ref, a_ref, df_ref, dh2_ref, acc_ref, wo, w1, w2, sem):
        first = pl.program_id(0) == 0
        _load_resident(first, [(wo_hbm, wo), (w1_hbm, w1), (w2_hbm, w2)], sem)

        @pl.when(first)
        def _():
            acc_ref[...] = jnp.zeros_like(acc_ref)

        gate1, sc2p, sh2, gate2 = v_ref[0:1, :], v_ref[1:2, :], v_ref[2:3, :], v_ref[3:4, :]
        l1w, l1b, l2w, l2b = v_ref[4:5, :], v_ref[5:6, :], v_ref[6:7, :], v_ref[7:8, :]
        m = _dot(mix_ref[...], wo[...])
        m_ref[...] = m.astype(BF16)
        x1n, rstd1 = _ln_stats(ALPHA * x_ref[...] + gate1 * m)
        x1n_ref[...] = x1n
        rstd_ref[...] = rstd1
        x1 = x1n * l1w + l1b
        xh1, _ = _ln_stats(x1)
        u2 = (xh1 * sc2p + sh2).astype(BF16)
        u2_ref[...] = u2
        f = jnp.zeros((tm, D_MODEL), F32)
        for j in range(N_DEV):
            cols = slice(j * FF_COLS, (j + 1) * FF_COLS)
            a = _dot(u2, w1[j])
            a_ref[:, cols] = a.astype(BF16)
            r = jnp.maximum(a, 0.0)
            f = f + _dot((r * r).astype(BF16), w2[cols, :])
        yh, rstd2 = _ln_stats(ALPHA * x1 + gate2 * f)
        e = yh * l2w + l2b - tgt_ref[...]
        dy = e * (1.0 / D_MODEL)
        dh2 = _ln_bwd(dy * l2w, yh, rstd2)
        dh2_ref[...] = dh2
        df_ref[...] = (dh2 * gate2).astype(BF16)
        acc_ref[0:1, :] += jnp.sum(dy * yh, axis=0, keepdims=True)
        acc_ref[1:2, :] += jnp.sum(dy, axis=0, keepdims=True)
        acc_ref[2:3, :] += jnp.sum(dh2 * f, axis=0, keepdims=True)
        acc_ref[3:4, :] += jnp.sum(e * e, axis=0, keepdims=True) * (0.5 / D_MODEL)

    row = lambda i: (i, 0)
    fix = lambda i: (0, 0)
    hbm = pl.BlockSpec(memory_space=pl.ANY)
    return pl.pallas_call(
        body, name="mid_fwd", grid=(t // tm,),
        in_specs=[pl.BlockSpec((tm, D_MODEL), row), pl.BlockSpec((tm, D_MODEL), row), pl.BlockSpec((tm, D_MODEL), row),
                  pl.BlockSpec((8, D_MODEL), fix), hbm, hbm, hbm],
        out_specs=(pl.BlockSpec((tm, D_MODEL), row), pl.BlockSpec((tm, D_MODEL), row), pl.BlockSpec((tm, 1), row),
                   pl.BlockSpec((tm, D_MODEL), row), pl.BlockSpec((tm, D_FF), row), pl.BlockSpec((tm, D_MODEL), row),
                   pl.BlockSpec((tm, D_MODEL), row), pl.BlockSpec((8, D_MODEL), fix)),
        out_shape=(jax.ShapeDtypeStruct((t, D_MODEL), BF16), jax.ShapeDtypeStruct((t, D_MODEL), F32),
                   jax.ShapeDtypeStruct((t, 1), F32), jax.ShapeDtypeStruct((t, D_MODEL), BF16),
                   jax.ShapeDtypeStruct((t, D_FF), BF16), jax.ShapeDtypeStruct((t, D_MODEL), BF16),
                   jax.ShapeDtypeStruct((t, D_MODEL), F32), jax.ShapeDtypeStruct((8, D_MODEL), F32)),
        scratch_shapes=[pltpu.VMEM((D_MODEL, D_MODEL), BF16), pltpu.VMEM((N_DEV, D_MODEL, FF_COLS), BF16),
                        pltpu.VMEM((D_FF, D_MODEL), BF16), pltpu.SemaphoreType.DMA((3,))],
        compiler_params=pltpu.CompilerParams(dimension_semantics=("arbitrary",), vmem_limit_bytes=V7X_VMEM_LIMIT),
    )(mixed, x2d, target, vecs, w_out_b, w1_b, w2_b)


def _ffn_bwd(df, a, dh2, x1n, rstd1, m, vecs, w_out_b, w1_b, w2_b):
    t = x1n.shape[0]
    tm = min(ROW_TILE, t)

    def body(df_ref, a_ref, dh2_ref, x1n_ref, rstd_ref, m_ref, v_ref, wo_hbm, w1_hbm, w2_hbm,
             da_ref, dm_ref, dmix_ref, dxa_ref, acc_ref, wo, w1, w2, sem):
        first = pl.program_id(0) == 0
        _load_resident(first, [(wo_hbm, wo), (w1_hbm, w1), (w2_hbm, w2)], sem)

        @pl.when(first)
        def _():
            acc_ref[...] = jnp.zeros_like(acc_ref)

        gate1, sc2p, l1w, l1b = v_ref[0:1, :], v_ref[1:2, :], v_ref[2:3, :], v_ref[3:4, :]
        df = df_ref[...]
        du2 = jnp.zeros((tm, D_MODEL), F32)
        for j in range(N_DEV):
            cols = slice(j * FF_COLS, (j + 1) * FF_COLS)
            dr2 = _dot(df, w2[cols, :], NT)
            da = (dr2 * (2.0 * jnp.maximum(a_ref[:, cols].astype(F32), 0.0))).astype(BF16)
            da_ref[:, cols] = da
            du2 = du2 + _dot(da, w1[j], NT)
        x1n = x1n_ref[...]
        xh1, rstd0 = _ln_stats(x1n * l1w + l1b)
        dx1 = ALPHA * dh2_ref[...] + _ln_bwd(du2 * sc2p, xh1, rstd0)
        dh1 = _ln_bwd(dx1 * l1w, x1n, rstd_ref[...])
        dxa_ref[...] = ALPHA * dh1
        dm = (dh1 * gate1).astype(BF16)
        dm_ref[...] = dm
        dmix_ref[...] = _dot(dm, wo[...], NT)
        acc_ref[0:1, :] += jnp.sum(du2 * xh1, axis=0, keepdims=True)
        acc_ref[1:2, :] += jnp.sum(du2, axis=0, keepdims=True)
        acc_ref[2:3, :] += jnp.sum(dx1 * x1n, axis=0, keepdims=True)
        acc_ref[3:4, :] += jnp.sum(dx1, axis=0, keepdims=True)
        acc_ref[4:5, :] += jnp.sum(dh1 * m_ref[...].astype(F32), axis=0, keepdims=True)

    row = lambda i: (i, 0)
    fix = lambda i: (0, 0)
    hbm = pl.BlockSpec(memory_space=pl.ANY)
    return pl.pallas_call(
        body, name="ffn_bwd", grid=(t // tm,),
        in_specs=[pl.BlockSpec((tm, D_MODEL), row), pl.BlockSpec((tm, D_FF), row), pl.BlockSpec((tm, D_MODEL), row),
                  pl.BlockSpec((tm, D_MODEL), row), pl.BlockSpec((tm, 1), row), pl.BlockSpec((tm, D_MODEL), row),
                  pl.BlockSpec((8, D_MODEL), fix), hbm, hbm, hbm],
        out_specs=(pl.BlockSpec((tm, D_FF), row), pl.BlockSpec((tm, D_MODEL), row), pl.BlockSpec((tm, D_MODEL), row),
                   pl.BlockSpec((tm, D_MODEL), row), pl.BlockSpec((8, D_MODEL), fix)),
        out_shape=(jax.ShapeDtypeStruct((t, D_FF), BF16), jax.ShapeDtypeStruct((t, D_MODEL), BF16),
                   jax.ShapeDtypeStruct((t, D_MODEL), F32), jax.ShapeDtypeStruct((t, D_MODEL), F32),
                   jax.ShapeDtypeStruct((8, D_MODEL), F32)),
        scratch_shapes=[pltpu.VMEM((D_MODEL, D_MODEL), BF16), pltpu.VMEM((N_DEV, D_MODEL, FF_COLS), BF16),
                        pltpu.VMEM((D_FF, D_MODEL), BF16), pltpu.SemaphoreType.DMA((3,))],
        compiler_params=pltpu.CompilerParams(dimension_semantics=("arbitrary",), vmem_limit_bytes=V7X_VMEM_LIMIT),
    )(df, a, dh2, x1n, rstd1, m, vecs, w_out_b, w1_b, w2_b)


def _matmul_tn(lhs, rhs, tmm, tn, tk, name, relu_sq=False, col_slab=None, out_rows=None):
    t, mm = lhs.shape
    assert out_rows is None or (col_slab is None and tmm == mm)
    nn = rhs.shape[1]
    tk = min(tk, t)
    nk = t // tk

    def body(l_ref, r_ref, o_ref, acc):
        kk = pl.program_id(2)

        @pl.when(kk == 0)
        def _():
            acc[...] = jnp.zeros_like(acc)

        l = l_ref[...]
        if relu_sq:
            lf = jnp.maximum(l.astype(F32), 0.0)
            l = (lf * lf).astype(BF16)
        acc[...] += _dot(l, r_ref[...], TN)

        @pl.when(kk == nk - 1)
        def _():
            if out_rows is not None:
                for s in range(N_DEV):
                    o_ref[s] = acc[s * out_rows:(s + 1) * out_rows, :].astype(o_ref.dtype)
            elif col_slab is None:
                o_ref[...] = acc[...].astype(o_ref.dtype)
            else:
                for s in range(tn // col_slab):
                    o_ref[s] = acc[:, s * col_slab:(s + 1) * col_slab].astype(o_ref.dtype)

    if out_rows is not None:
        out_spec = pl.BlockSpec((N_DEV, out_rows, tn), lambda i, j, k: (0, 0, j))
        out_shape = jax.ShapeDtypeStruct((N_DEV, out_rows, nn), BF16)
    elif col_slab is None:
        out_spec = pl.BlockSpec((tmm, tn), lambda i, j, k: (i, j))
        out_shape = jax.ShapeDtypeStruct((mm, nn), BF16)
    else:
        out_spec = pl.BlockSpec((tn // col_slab, tmm, col_slab), lambda i, j, k: (j, i, 0))
        out_shape = jax.ShapeDtypeStruct((nn // col_slab, mm, col_slab), BF16)
    return pl.pallas_call(
        body, name=name, grid=(mm // tmm, nn // tn, nk),
        in_specs=[pl.BlockSpec((tk, tmm), lambda i, j, k: (k, i)), pl.BlockSpec((tk, tn), lambda i, j, k: (k, j))],
        out_specs=out_spec,
        out_shape=out_shape,
        scratch_shapes=[pltpu.VMEM((tmm, tn), F32)],
        compiler_params=pltpu.CompilerParams(dimension_semantics=("arbitrary", "arbitrary", "arbitrary"),
                                             vmem_limit_bytes=V7X_VMEM_LIMIT),
    )(lhs, rhs)


def _mixer_bwd(dmix, proj, qrb, krb, oraw, tables, rst, sst, gw_pad, gb, rnw, gnw):
    t = proj.shape[0]
    tc = min(MIX_TILE, t)
    tr, tg = min(RET_SUB, tc), min(GLA_SUB, tc)
    nsteps = t // tc
    scale_r = RET_D ** -0.5
    scale_g = GLA_DK ** -0.5
    gammas = _tile_gammas(tr)

    def body(dmix_ref, qrb_ref, krb_ref, rv_ref, rg_ref, gq_ref, gk_ref, gv_ref, gg_ref, glr_ref, oraw_ref,
             dec_ref, qkd_ref, rot_in_ref, rot_tile_ref, rst_ref, sst_ref, gw_ref, gb_ref, rnw_ref, gnw_ref,
             dproj_ref, dgw_ref, dvec_ref, dr_scr, ds_scr):
        @pl.when(pl.program_id(0) == 0)
        def _():
            dr_scr[...] = jnp.zeros_like(dr_scr)
            ds_scr[...] = jnp.zeros_like(ds_scr)
            dgw_ref[...] = jnp.zeros_like(dgw_ref)
            dvec_ref[...] = jnp.zeros_like(dvec_ref)

        gla_k = _gla_consts(tg)
        last_row = lax.broadcasted_iota(jnp.int32, (tg, GLA_KW), 0) == tg - 1

        def ret_tile(jj, carry):
            j = tc // tr - 1 - jj
            rows = _tile_rows(j, tr)
            cosv, sinv = _tile_rotary(rot_in_ref, rot_tile_ref, j)
            for h in range(RET_HEADS):
                cols = slice(h * RET_D, (h + 1) * RET_D)
                o = oraw_ref[rows, cols]
                g = rg_ref[rows, cols]
                w = rnw_ref[:, cols]
                dout = dmix_ref[rows, cols]
                oc = o - jnp.mean(o, axis=-1, keepdims=True)
                inv = lax.rsqrt(jnp.mean(oc * oc, axis=-1, keepdims=True) + LN_EPS)
                n = oc * inv
                sg = _sigmoid(g)
                sil = g * sg
                dn = dout * w * sil
                dvec_ref[0:1, cols] += jnp.sum(dout * n * sil, axis=0, keepdims=True)
                dproj_ref[rows, OFF_RG + h * RET_D:OFF_RG + (h + 1) * RET_D] = (
                    dout * n * w * (sg * (1.0 + g * (1.0 - sg)))).astype(BF16)
                doc = inv * (dn - n * jnp.mean(dn * n, axis=-1, keepdims=True))
                do = doc - jnp.mean(doc, axis=-1, keepdims=True)

                qb, kb = qrb_ref[rows, cols], krb_ref[rows, cols]
                qr, kr = qb.astype(F32), kb.astype(F32)
                vb = rv_ref[rows, cols].astype(BF16)
                dob = do.astype(BF16)
                qd, kd = qkd_ref[h], qkd_ref[RET_HEADS + h]
                p = _dot(qb, kb, NT) * dec_ref[h]
                rp = rst_ref[j, cols, :].astype(BF16)
                dr = dr_scr[cols, :]
                drb = dr.astype(BF16)
                dpb = (_dot(dob, vb, NT) * dec_ref[h]).astype(BF16)
                dqr = _dot(dpb, kb) + _dot(dob, rp, NT) * qd
                dkr = _dot(dpb, qb, TN) + _dot(vb, drb, NT) * kd
                dv = _dot(p.astype(BF16), dob, TN) + _dot((kr * kd).astype(BF16), drb)
                dr_scr[cols, :] = gammas[h] * dr + _dot((qr * qd).astype(BF16), dob, TN)
                dproj_ref[rows, OFF_RQ + h * RET_D:OFF_RQ + (h + 1) * RET_D] = (
                    _rotate_t(dqr, cosv, sinv) * scale_r).astype(BF16)
                dproj_ref[rows, OFF_RK + h * RET_D:OFF_RK + (h + 1) * RET_D] = _rotate_t(dkr, cosv, sinv).astype(BF16)
                dproj_ref[rows, OFF_RV + h * RET_D:OFF_RV + (h + 1) * RET_D] = dv.astype(BF16)
            return carry

        def gla_tile(jj, carry):
            k = gla_k
            tl = tg
            j = tc // tg - 1 - jj
            rows = _tile_rows(j, tg)
            glr = glr_ref[rows, :]
            z, b, bl, ep, em = _gla_gates(glr, gw_ref[...], gb_ref[...], k["ltri"], tl)
            qs = gq_ref[rows, :] * scale_g
            kk = gk_ref[rows, :]
            eb = jnp.exp(b)
            ekb = jnp.exp(bl - b)
            ebl = jnp.exp(bl)
            ql, qu, kl, ku = qs * ep, qs * em, kk * em, kk * ep
            qg, kg = qs * eb, kk * ekb
            qlm = _stack_heads(ql, k["hmask"]).astype(BF16)
            qum = _stack_heads(qu, k["hmask"]).astype(BF16)
            klb, kub = kl.astype(BF16), ku.astype(BF16)
            a_all = jnp.where(k["lower"], _dot(qlm, klb, NT),
                              jnp.where(k["upper"], _dot(qum, kub, NT), 0.0)).astype(BF16)
            st = sst_ref[j]
            stb = st.astype(BF16)
            ds = ds_scr[...]
            dsb = ds.astype(BF16)
            ds_new = ds * ebl
            da_parts = []
            dqg = jnp.zeros((tl, GLA_KW), F32)
            dkg = jnp.zeros((tl, GLA_KW), F32)
            for h in range(GLA_HEADS):
                cols = slice(h * GLA_DV, (h + 1) * GLA_DV)
                hr = slice(h * tl, (h + 1) * tl)
                ocols = slice(RET_W + h * GLA_DV, RET_W + (h + 1) * GLA_DV)
                o = oraw_ref[rows, ocols]
                g = gg_ref[rows, cols]
                w = gnw_ref[:, cols]
                dout = dmix_ref[rows, ocols]
                inv = lax.rsqrt(jnp.mean(o * o, axis=-1, keepdims=True) + LN_EPS)
                n = o * inv
                sg = _sigmoid(g)
                sil = g * sg
                dn = dout * w * sil
                dvec_ref[1:2, cols] += jnp.sum(dout * n * sil, axis=0, keepdims=True)
                dproj_ref[rows, OFF_GG + h * GLA_DV:OFF_GG + (h + 1) * GLA_DV] = (
                    dout * n * w * (sg * (1.0 + g * (1.0 - sg)))).astype(BF16)
                dob = (inv * (dn - n * jnp.mean(dn * n, axis=-1, keepdims=True))).astype(BF16)
                vb = gv_ref[rows, cols].astype(BF16)
                mh = k["hmask"][h]
                da_parts.append(_dot(dob, vb, NT))
                dv = _dot(a_all[hr, :], dob, TN) + _dot((kg * mh).astype(BF16), dsb, NT)
                dproj_ref[rows, OFF_GV + h * GLA_DV:OFF_GV + (h + 1) * GLA_DV] = dv.astype(BF16)
                dkg = dkg + mh * _dot(vb, dsb)
                dqg = dqg + mh * _dot(dob, stb)
                ds_new = ds_new + _dot(dob, (qg * mh).astype(BF16), TN)
            da_all = jnp.concatenate(da_parts, axis=0)
            dal = jnp.where(k["lower"], da_all, 0.0).astype(BF16)
            dau = jnp.where(k["upper"], da_all, 0.0).astype(BF16)
            dqlm = _dot(dal, klb)
            dqum = _dot(dau, kub)
            dql = jnp.zeros((tl, GLA_KW), F32)
            dqu = jnp.zeros((tl, GLA_KW), F32)
            for h in range(GLA_HEADS):
                hr = slice(h * tl, (h + 1) * tl)
                dql = dql + k["hmask"][h] * dqlm[hr, :]
                dqu = dqu + k["hmask"][h] * dqum[hr, :]
            dkl = _dot(dal, qlm, TN)
            dku = _dot(dau, qum, TN)
            dbl = (jnp.sum(dkg * kg, axis=0, keepdims=True)
                   + jnp.sum(ds * st, axis=0, keepdims=True) * ebl)
            ds_scr[...] = ds_new
            dqs = dql * ep + dqu * em + dqg * eb
            dk = dkl * em + dku * ep + dkg * ekb
            db = dql * ql - dkl * kl - dqu * qu + dku * ku + dqg * qg - dkg * kg
            db = db + jnp.where(last_row, dbl, 0.0)
            dla = _dot_split(k["utri"], db, NN, a_exact=True)
            dz = dla * (1.0 / GATE_TAU) * _sigmoid(-z)
            dvec_ref[2:3, 0:GLA_KW] += jnp.sum(dz, axis=0, keepdims=True)
            dgw_ref[...] += _dot_split(glr, dz, TN)
            dproj_ref[rows, OFF_GLR:D_IN_PAD] = _dot(dz.astype(BF16), gw_ref[...].astype(BF16), NT).astype(BF16)
            dproj_ref[rows, OFF_GQ:OFF_GQ + GLA_KW] = (dqs * scale_g).astype(BF16)
            dproj_ref[rows, OFF_GK:OFF_GK + GLA_KW] = dk.astype(BF16)
            return carry

        _for_tiles(tc // tr, ret_tile)
        _for_tiles(tc // tg, gla_tile)

    rev = lambda i: (nsteps - 1 - i, 0)

    def col(width, off):
        return pl.BlockSpec((tc, width), lambda i, o=off // width: (nsteps - 1 - i, o))

    fix = lambda i: (0, 0)
    fix3 = lambda i: (0, 0, 0)
    dec, qkd, rot_in, rot_tile = tables
    half = pl.BlockSpec((tc, RET_W), rev)
    in_specs = [pl.BlockSpec((tc, D_MODEL), rev), half, half, col(RET_W, OFF_RV), col(RET_W, OFF_RG),
                col(GLA_KW, OFF_GQ), col(GLA_KW, OFF_GK), col(GLA_VW, OFF_GV), col(GLA_VW, OFF_GG),
                col(V7X_LANES, OFF_GLR),
                pl.BlockSpec((tc, D_MODEL), rev),
                pl.BlockSpec(dec.shape, fix3), pl.BlockSpec(qkd.shape, fix3), pl.BlockSpec(rot_in.shape, fix3),
                pl.BlockSpec((tc // tr, 8, 2 * RET_D), lambda i: (nsteps - 1 - i, 0, 0)),
                pl.BlockSpec((tc // tr, RET_W, RET_D), lambda i: (nsteps - 1 - i, 0, 0)),
                pl.BlockSpec((tc // tg, GLA_DV, GLA_KW), lambda i: (nsteps - 1 - i, 0, 0)),
                pl.BlockSpec((V7X_LANES, GLA_KW), fix), pl.BlockSpec((1, GLA_KW), fix),
                pl.BlockSpec((1, RET_W), fix), pl.BlockSpec((1, GLA_VW), fix)]
    out_specs = (pl.BlockSpec((tc, D_IN_PAD), rev), pl.BlockSpec((V7X_LANES, GLA_KW), fix),
                 pl.BlockSpec((8, RET_W), fix))
    out_shape = (jax.ShapeDtypeStruct((t, D_IN_PAD), BF16), jax.ShapeDtypeStruct((V7X_LANES, GLA_KW), F32),
                 jax.ShapeDtypeStruct((8, RET_W), F32))
    return pl.pallas_call(
        body, name="mixer_bwd", grid=(nsteps,), in_specs=in_specs, out_specs=out_specs, out_shape=out_shape,
        scratch_shapes=[pltpu.VMEM((RET_W, RET_D), F32), pltpu.VMEM((GLA_DV, GLA_KW), F32)],
        compiler_params=pltpu.CompilerParams(dimension_semantics=("arbitrary",), vmem_limit_bytes=V7X_VMEM_LIMIT),
    )(dmix, qrb, krb, *([proj] * 7), oraw, dec, qkd, rot_in, rot_tile, rst, sst, gw_pad, gb, rnw, gnw)


def _inproj_bwd(dproj, x2d, dxa, sc1p, w_in_t):
    t = x2d.shape[0]
    tm = min(2 * PROJ_TILE, t)

    def body(dp_ref, x_ref, dxa_ref, sc_ref, w_hbm, gx_ref, acc_ref, w_vmem, sem):
        first = pl.program_id(0) == 0
        _load_w_in_t(first, w_hbm, w_vmem, sem)

        @pl.when(first)
        def _():
            acc_ref[...] = jnp.zeros_like(acc_ref)

        du = _dot(dp_ref[...], w_vmem[...])
        xh, rstd = _ln_stats(x_ref[...])
        gx_ref[...] = dxa_ref[...] + _ln_bwd(du * sc_ref[...], xh, rstd)
        acc_ref[0:1, :] += jnp.sum(du * xh, axis=0, keepdims=True)
        acc_ref[1:2, :] += jnp.sum(du, axis=0, keepdims=True)

    row = lambda i: (i, 0)
    fix = lambda i: (0, 0)
    return pl.pallas_call(
        body, name="inproj_bwd", grid=(t // tm,),
        in_specs=[pl.BlockSpec((tm, D_IN_PAD), row), pl.BlockSpec((tm, D_MODEL), row), pl.BlockSpec((tm, D_MODEL), row),
                  pl.BlockSpec((1, D_MODEL), fix), pl.BlockSpec(memory_space=pl.ANY)],
        out_specs=(pl.BlockSpec((tm, D_MODEL), row), pl.BlockSpec((8, D_MODEL), fix)),
        out_shape=(jax.ShapeDtypeStruct((t, D_MODEL), F32), jax.ShapeDtypeStruct((8, D_MODEL), F32)),
        scratch_shapes=[pltpu.VMEM((D_IN_PAD, D_MODEL), BF16), pltpu.SemaphoreType.DMA((1,))],
        compiler_params=pltpu.CompilerParams(dimension_semantics=("arbitrary",), vmem_limit_bytes=V7X_VMEM_LIMIT),
    )(dproj, x2d, dxa, sc1p, w_in_t)


def _adam_math(w, g, m, v):
    m = ADAM_B1 * m + (1.0 - ADAM_B1) * g
    v = ADAM_B2 * v + (1.0 - ADAM_B2) * (g * g)
    m_hat = m / (1.0 - ADAM_B1 ** ADAM_STEP)
    v_hat = v / (1.0 - ADAM_B2 ** ADAM_STEP)
    delta = -ADAM_LR * (m_hat / (jnp.sqrt(v_hat) + ADAM_EPS) + ADAM_WD * w)
    return delta, m, v


def _adamw(w, gparts, m, v, name, slabs_transposed=False):
    nparts = gparts.shape[0]
    rows, cols = w.shape
    tr = rows
    for cand in (512, 256, 128, 64, 32, 16, 8):
        if rows % cand == 0:
            tr = cand
            break
    cols_tile = cols // V7X_SUBLANES * V7X_SUBLANES
    cols_pad = -(-cols // V7X_LANES) * V7X_LANES

    def body(w_ref, g_ref, m_ref, v_ref, go_ref, d_ref, mo_ref, vo_ref, *gt_scr):
        g = g_ref[0].astype(F32)
        for p in range(1, nparts):
            g = g + g_ref[p].astype(F32)
        if slabs_transposed:
            gt_ref, = gt_scr
            gt_ref[pl.ds(cols_tile, cols_pad - cols_tile), :] = jnp.zeros((cols_pad - cols_tile, tr), F32)
            gt_ref[pl.ds(0, cols), :] = g
            g = gt_ref[...].T[:, :cols]
        delta, mn, vn = _adam_math(w_ref[...], g, m_ref[...], v_ref[...])
        go_ref[...] = g
        d_ref[...] = delta
        mo_ref[...] = mn
        vo_ref[...] = vn

    blk = pl.BlockSpec((tr, cols), lambda i: (i, 0))
    shp = jax.ShapeDtypeStruct((rows, cols), F32)
    if slabs_transposed:
        g_spec = pl.BlockSpec((nparts, cols, tr), lambda i: (0, 0, i))
    else:
        g_spec = pl.BlockSpec((nparts, tr, cols), lambda i: (0, i, 0))
    return pl.pallas_call(
        body, name=name, grid=(rows // tr,),
        in_specs=[blk, g_spec, blk, blk],
        out_specs=(blk, blk, blk, blk), out_shape=(shp, shp, shp, shp),
        scratch_shapes=[pltpu.VMEM((cols_pad, tr), F32)] if slabs_transposed else [],
        compiler_params=pltpu.CompilerParams(dimension_semantics=("arbitrary",), vmem_limit_bytes=V7X_VMEM_LIMIT),
    )(w, gparts, m, v)


def _small_reduce(gathered, gathered_gw, c_all, dmod_cols):
    def body(g_ref, gw_ref, c_ref, dm_ref, sum_ref, gwsum_ref, gb_ref, gwa_ref):
        s = g_ref[0]
        sw = gw_ref[0]
        for p in range(1, N_DEV):
            s = s + g_ref[p]
            sw = sw + gw_ref[p]
        sum_ref[...] = s
        gwsum_ref[...] = sw
        for i in range(6):
            gb_ref[:, i * D_MODEL:(i + 1) * D_MODEL] = s[i:i + 1, :]
        cc = c_ref[...]
        gwa_ref[...] = _dot(cc * _sigmoid(cc), dm_ref[...], TN, HIGHEST)

    vm = pl.BlockSpec(memory_space=pltpu.VMEM)
    return pl.pallas_call(
        body, name="small_reduce",
        out_shape=(jax.ShapeDtypeStruct(gathered.shape[1:], F32), jax.ShapeDtypeStruct(gathered_gw.shape[1:], F32),
                   jax.ShapeDtypeStruct((1, 6 * D_MODEL), F32), jax.ShapeDtypeStruct((D_MODEL, ADA_COLS), F32)),
        in_specs=[vm] * 4, out_specs=(vm, vm, vm, vm),
        compiler_params=pltpu.CompilerParams(vmem_limit_bytes=V7X_VMEM_LIMIT),
    )(gathered, gathered_gw, c_all, dmod_cols)


SMR_LN1W, SMR_LN1B, SMR_LN2W, SMR_LN2B, SMR_NORMS, SMR_MISC = 6, 7, 8, 9, 10, 11


def _adamw_small(gsum, g_b_ada, g_ggw, params, moms, vels):
    n = len(params)

    def body(*refs):
        gsum_ref, gb_ref, gw_ref = refs[:3]
        w_refs, m_refs, v_refs = refs[3:3 + n], refs[3 + n:3 + 2 * n], refs[3 + 2 * n:3 + 3 * n]
        outs = refs[3 + 3 * n:]
        g_refs, d_refs, mo_refs, vo_refs = outs[:n - 1], outs[n - 1:2 * n - 1], outs[2 * n - 1:3 * n - 1], outs[3 * n - 1:]
        grads = [gb_ref[...],
                 gsum_ref[SMR_NORMS:SMR_NORMS + 1, 0:RET_W],
                 gsum_ref[SMR_MISC:SMR_MISC + 1, 0:GLA_KW],
                 gsum_ref[SMR_NORMS:SMR_NORMS + 1, RET_W:RET_W + GLA_VW],
                 gsum_ref[SMR_LN1W:SMR_LN1W + 1, :], gsum_ref[SMR_LN1B:SMR_LN1B + 1, :],
                 gsum_ref[SMR_LN2W:SMR_LN2W + 1, :], gsum_ref[SMR_LN2B:SMR_LN2B + 1, :],
                 gw_ref[...]]
        for i in range(n):
            delta, mn, vn = _adam_math(w_refs[i][...], grads[i], m_refs[i][...], v_refs[i][...])
            if i < n - 1:
                g_refs[i][...] = grads[i]
            d_refs[i][...] = delta
            mo_refs[i][...] = mn
            vo_refs[i][...] = vn

    vm = pl.BlockSpec(memory_space=pltpu.VMEM)
    shapes = [jax.ShapeDtypeStruct(p.shape, F32) for p in params]
    n_in = 3 + 3 * n
    out_shape = tuple(shapes[:n - 1] + shapes * 3)
    return pl.pallas_call(
        body, name="adamw_small", out_shape=out_shape,
        in_specs=[vm] * n_in, out_specs=tuple([vm] * len(out_shape)),
        compiler_params=pltpu.CompilerParams(vmem_limit_bytes=V7X_VMEM_LIMIT),
    )(gsum, g_b_ada, g_ggw, *params, *moms, *vels)


def kernel(x, c, w_ada, b_ada, w_in, ret_norm_w, gla_gate_w, gla_gate_b, gla_norm_w, w_out, ln1_w, ln1_b, w_ff1, w_ff2, ln2_w, ln2_b, loss_target, m_w_ada, m_b_ada, m_w_in, m_ret_norm_w, m_gla_gate_w, m_gla_gate_b, m_gla_norm_w, m_w_out, m_ln1_w, m_ln1_b, m_w_ff1, m_w_ff2, m_ln2_w, m_ln2_b, v_w_ada, v_b_ada, v_w_in, v_ret_norm_w, v_gla_gate_w, v_gla_gate_b, v_gla_norm_w, v_w_out, v_ln1_w, v_ln1_b, v_w_ff1, v_w_ff2, v_ln2_w, v_ln2_b):
    t = x.shape[1]
    xi, yi, ci = _my_coords()
    me = 4 * xi + 2 * yi + ci
    x2d = x[0]
    tgt = loss_target[0]

    c_ext = jnp.concatenate([c, gla_gate_w[0].reshape(1, GATE_RANK * GLA_KW // N_DEV)], axis=1)
    b_l = lax.dynamic_slice(b_ada, (0, me * ADA_COLS), (1, ADA_COLS))
    c_all3, mod_all, wi_g, ada_token = _adaln_mod(c_ext, w_ada[0], b_l, w_in[0].T.astype(BF16))

    wg = _exchange_start([(w_out[0] + ada_token[0, 0]).astype(BF16), w_ff1[0].astype(BF16), w_ff2[0].astype(BF16)],
                         True, "wgather_start")

    c_all = c_all3[:, 0, :D_MODEL]
    gate_w = c_all3[:, 0, D_MODEL:].reshape(N_DEV, GATE_RANK, GLA_KW // N_DEV)
    gate_w = gate_w.transpose(1, 0, 2).reshape(GATE_RANK, GLA_KW)
    gw_pad = jnp.zeros((V7X_LANES, GLA_KW), F32).at[:GATE_RANK].set(gate_w)
    mod = lax.dynamic_slice(mod_all, (0, me, 0), (N_DEV, 1, ADA_COLS)).reshape(6, D_MODEL)
    shift1, scale1, gate1, shift2, scale2, gate2 = [mod[i:i + 1] for i in range(6)]

    w_in_t = wi_g.reshape(D_IN, D_MODEL)

    tables = _ret_tables(t, min(RET_SUB, t))

    sc1p = 1.0 + scale1
    proj, u = _inproj_fwd(x2d, sc1p, shift1 + wg[4][0, 0], w_in_t)
    mixed, oraw, qrb, krb, rst, sst = _mixer_fwd(proj, tables, gw_pad, gla_gate_b, ret_norm_w, gla_norm_w)
    wo_g, w1_b, w2_g = _exchange_wait(*wg[:4], mixed, True, "wgather_wait")
    w_out_b = wo_g.reshape(D_MODEL, D_MODEL)
    w2_b = w2_g.reshape(D_FF, D_MODEL)
    vec_f = jnp.concatenate([gate1, 1.0 + scale2, shift2, gate2, ln1_w, ln1_b, ln2_w, ln2_b], axis=0)
    m, x1n, rstd1, u2, a, df, dh2, acc_f = _mid_fwd(mixed, x2d, tgt, vec_f, w_out_b, w1_b, w2_b)

    vec_b = jnp.concatenate([gate1, 1.0 + scale2, ln1_w, ln1_b, jnp.zeros((4, D_MODEL), F32)], axis=0)
    da, dm, dmix, dxa, acc_b = _ffn_bwd(df, a, dh2, x1n, rstd1, m, vec_b, w_out_b, w1_b, w2_b)
    dw2 = _matmul_tn(a, df, 2048, 1024, 2048, "tn_dw2", relu_sq=True)
    dw1 = _matmul_tn(u2, da, 1024, 2048, 2048, "tn_dw1", col_slab=FF_COLS)
    dwo = _matmul_tn(mixed, dm, 1024, 1024, 2048, "tn_dwout")
    gx = _exchange_start([dwo.reshape(N_DEV, OUT_ROWS, D_MODEL), dw1, dw2.reshape(N_DEV, FF_COLS, D_MODEL)], False,
                         "gradx_start")
    dproj, dgw, dvec = _mixer_bwd(dmix, proj, qrb, krb, oraw, tables, rst, sst, gw_pad,
                                  gla_gate_b + gx[4][0, 0], ret_norm_w, gla_norm_w)
    dwi_s = _matmul_tn(dproj, u, D_IN_PAD, 1024, 1024, "tn_dwin", out_rows=IN_COLS)
    gi = _exchange_start([dwi_s], False, "gradin_start")
    grad_x, acc_i = _inproj_bwd(dproj, x2d, dxa, sc1p + gi[4][0, 0], w_in_t)

    loss_part = jnp.sum(acc_f[3])
    small = jnp.concatenate([
        acc_i[1:2], acc_i[0:1], acc_b[4:5], acc_b[1:2], acc_b[0:1], acc_f[2:3],
        acc_b[2:3], acc_b[3:4], acc_f[0:1], acc_f[1:2],
        jnp.concatenate([dvec[0:1], dvec[1:2]], axis=1),
        jnp.concatenate([dvec[2:3, :GLA_KW], jnp.full((1, 128), loss_part, F32),
                         jnp.zeros((1, D_MODEL - GLA_KW - 128), F32)], axis=1),
        jnp.zeros((4, D_MODEL), F32)], axis=0)
    sg = _exchange_start([small, dgw[:GATE_RANK]], True, "small_start")

    r_wo, r_w1, r_w2 = _exchange_wait(*gx[:4], sg[4], False, "gradx_wait")
    r_wi, = _exchange_wait(*gi[:4], sg[4], False, "gradin_wait")
    big = [_adamw(w[0], r, m_[0], v_[0], nm) for w, r, m_, v_, nm in (
        (w_out, r_wo, m_w_out, v_w_out, "adamw_out"),
        (w_ff1, r_w1, m_w_ff1, v_w_ff1, "adamw_ff1"), (w_ff2, r_w2, m_w_ff2, v_w_ff2, "adamw_ff2"))]
    big_in = _adamw(w_in[0], r_wi, m_w_in[0], v_w_in[0], "adamw_in", slabs_transposed=True)
    big = [big_in] + big
    g_big, d_big, m_big, v_big = [[b[i][None] for b in big] for i in range(4)]

    small_all, gw_all = _exchange_wait(*sg[:4], big_in[1], True, "small_wait")
    dmod_all = small_all[:, :6].reshape(N_DEV, 6 * D_MODEL)
    dmod_cols = lax.dynamic_slice(dmod_all, (0, me * ADA_COLS), (N_DEV, ADA_COLS))
    ssum, gw_sum, g_b_ada, g_w_ada = _small_reduce(small_all, gw_all, c_all, dmod_cols)
    loss = ssum[SMR_MISC, GLA_KW]
    g_ggw = lax.dynamic_slice(gw_sum, (0, me * (GLA_KW // N_DEV)), (GATE_RANK, GLA_KW // N_DEV))[None]

    small_w = [b_ada, ret_norm_w, gla_gate_b, gla_norm_w, ln1_w, ln1_b, ln2_w, ln2_b, gla_gate_w]
    small_m = [m_b_ada, m_ret_norm_w, m_gla_gate_b, m_gla_norm_w, m_ln1_w, m_ln1_b, m_ln2_w, m_ln2_b, m_gla_gate_w]
    small_v = [v_b_ada, v_ret_norm_w, v_gla_gate_b, v_gla_norm_w, v_ln1_w, v_ln1_b, v_ln2_w, v_ln2_b, v_gla_gate_w]
    res = _adamw_small(ssum, g_b_ada, g_ggw, small_w, small_m, small_v)
    small_g = list(res[:8]) + [g_ggw]
    d_small, m_small, v_small = list(res[8:17]), list(res[17:26]), list(res[26:35])

    _, d_w_ada, nm_w_ada, nv_w_ada = _adamw(w_ada[0], g_w_ada[None], m_w_ada[0], v_w_ada[0], "adamw_ada")

    def ordered(w_ada_v, small_vals, big_vals):
        b_ada_v, rnw_v, ggb_v, gnw_v, l1w_v, l1b_v, l2w_v, l2b_v, ggw_v = small_vals
        wi_v, wo_v, w1_v, w2_v = big_vals
        return [w_ada_v, b_ada_v, wi_v, rnw_v, ggw_v, ggb_v, gnw_v, wo_v, l1w_v, l1b_v, w1_v, w2_v, l2w_v, l2b_v]

    grads = ordered(g_w_ada[None], small_g, g_big)
    deltas = ordered(d_w_ada[None], d_small, d_big)
    new_m = ordered(nm_w_ada[None], m_small, m_big)
    new_v = ordered(nv_w_ada[None], v_small, v_big)
    return (loss, grad_x[None], *grads, *deltas, *new_m, *new_v)
```

```python
import numpy as np
import jax
import jax.numpy as jnp
from jax import lax
from jax.experimental import pallas as pl
from jax.experimental.pallas import tpu as pltpu

F32 = jnp.float32
BF16 = jnp.bfloat16
MESH = pl.DeviceIdType.MESH
HIGHEST = lax.Precision.HIGHEST

N_DEV = 8
D_MODEL = 1024
CHUNK = 64
RET_HEADS = 4
RET_D = 128
GLA_HEADS = 4
GLA_DK = 64
GLA_DV = 128
GLA_KW = GLA_HEADS * GLA_DK
RET_W = RET_HEADS * RET_D
GLA_VW = GLA_HEADS * GLA_DV
V7X_LANES = 128
GATE_RANK = 16
GATE_TAU = 16.0
D_FF = 4096
LN_EPS = 1e-5
ALPHA = (2.0 * 1) ** 0.25
D_IN = 3600
D_IN_PAD = 3712
ADA_COLS = 6 * D_MODEL // N_DEV
IN_COLS = D_IN // N_DEV
FF_COLS = D_FF // N_DEV
OUT_ROWS = D_MODEL // N_DEV

OFF_RQ, OFF_RK, OFF_RV, OFF_RG = 0, RET_W, 2 * RET_W, 3 * RET_W
OFF_GQ = 4 * RET_W
OFF_GK = OFF_GQ + GLA_KW
OFF_GV = OFF_GK + GLA_KW
OFF_GG = OFF_GV + GLA_VW
OFF_GLR = OFF_GG + GLA_VW

ADAM_LR, ADAM_B1, ADAM_B2, ADAM_EPS, ADAM_WD, ADAM_STEP = 0.001, 0.9, 0.999, 1e-08, 0.01, 10

V7X_VMEM_LIMIT = 62 * 1024 * 1024

ROW_TILE = 512
PROJ_TILE = 512
MIX_TILE = 512
RET_SUB = 256
GLA_SUB = 128


def _log_gamma(h):
    return float(np.log(np.float32(1.0) - np.float32(2.0) ** np.float32(-5.0 - h)))


def _my_coords():
    return lax.axis_index("x"), lax.axis_index("y"), lax.axis_index("c")


def _flip(v, bit):
    return 1 - v if bit else v


def _peer(k):
    x, y, c = _my_coords()
    px, py, pc = _flip(x, (k >> 2) & 1), _flip(y, (k >> 1) & 1), _flip(c, k & 1)
    return (px, py, pc), 4 * px + 2 * py + pc


def _dot(a, b, dims=(((1,), (0,)), ((), ())), precision=None):
    return lax.dot_general(a, b, dims, precision=precision, preferred_element_type=F32)


NN = (((1,), (0,)), ((), ()))
NT = (((1,), (1,)), ((), ()))
TN = (((0,), (0,)), ((), ()))


def _split_bf16(v, parts):
    out = []
    for _ in range(parts):
        p = v.astype(BF16)
        out.append(p)
        v = v - p.astype(F32)
    return out


def _dot_split(a, b, dims, a_exact=False):
    if a_exact:
        ab = a.astype(BF16)
        return sum(_dot(ab, p, dims) for p in _split_bf16(b, 2))
    a_hi, a_lo = _split_bf16(a, 2)
    b_hi, b_lo = _split_bf16(b, 2)
    return _dot(a_hi, b_hi, dims) + _dot(a_hi, b_lo, dims) + _dot(a_lo, b_hi, dims)


def _sigmoid(x):
    return 1.0 / (1.0 + jnp.exp(-x))


def _ln_stats(x):
    mu = jnp.mean(x, axis=-1, keepdims=True)
    xc = x - mu
    var = jnp.mean(xc * xc, axis=-1, keepdims=True)
    rstd = lax.rsqrt(var + LN_EPS)
    return xc * rstd, rstd


def _ln_bwd(dyh, xh, rstd):
    return rstd * (dyh - jnp.mean(dyh, axis=-1, keepdims=True) - xh * jnp.mean(dyh * xh, axis=-1, keepdims=True))


def _adaln_mod(c_ext, w_ada_l, b_l, w_in_l):
    width = c_ext.shape[1]

    def body(c_ref, w_ref, b_ref, wi_ref, call_ref, mod_ref, wig_ref, token_ref, s1, r1, s2, r2, gs, gr, gl):
        gather = _TwoLevelGather([wi_ref], [wig_ref], gs, gr, gl)
        gather.start()
        token_ref[...] = jnp.zeros_like(token_ref)
        x, y, c = _my_coords()
        me = 4 * x + 2 * y + c
        call_ref[me] = c_ref[...]
        sends = []
        for k in range(1, N_DEV):
            peer, _ = _peer(k)
            cp = pltpu.make_async_remote_copy(c_ref, call_ref.at[me], s1.at[k - 1], r1.at[k - 1],
                                              device_id=peer, device_id_type=MESH)
            cp.start()
            sends.append(cp)
        for k in range(1, N_DEV):
            peer, pid = _peer(k)
            pltpu.make_async_remote_copy(c_ref, call_ref.at[pid], s1.at[k - 1], r1.at[k - 1],
                                         device_id=peer, device_id_type=MESH).wait_recv()
        for cp in sends:
            cp.wait_send()
        row = lax.broadcasted_iota(jnp.int32, (N_DEV, D_MODEL), 0)
        call = jnp.zeros((N_DEV, D_MODEL), F32)
        for j in range(N_DEV):
            call = jnp.where(row == j, jnp.broadcast_to(call_ref[j][:, :D_MODEL], (N_DEV, D_MODEL)), call)
        sc = call * _sigmoid(call)
        mod = _dot(sc, w_ref[...], NN, HIGHEST) + b_ref[...]
        mod_ref[me] = mod
        sends = []
        for k in range(1, N_DEV):
            peer, _ = _peer(k)
            cp = pltpu.make_async_remote_copy(mod_ref.at[me], mod_ref.at[me], s2.at[k - 1], r2.at[k - 1],
                                              device_id=peer, device_id_type=MESH)
            cp.start()
            sends.append(cp)
        for k in range(1, N_DEV):
            peer, pid = _peer(k)
            pltpu.make_async_remote_copy(mod_ref.at[pid], mod_ref.at[pid], s2.at[k - 1], r2.at[k - 1],
                                         device_id=peer, device_id_type=MESH).wait_recv()
        for cp in sends:
            cp.wait_send()
        gather.forward()
        gather.finish()

    vm = pl.BlockSpec(memory_space=pltpu.VMEM)
    hbm = pl.BlockSpec(memory_space=pl.ANY)
    return pl.pallas_call(
        body, name="adaln_mod",
        out_shape=(jax.ShapeDtypeStruct((N_DEV, 1, width), F32),
                   jax.ShapeDtypeStruct((N_DEV, N_DEV, ADA_COLS), F32),
                   jax.ShapeDtypeStruct((N_DEV, *w_in_l.shape), w_in_l.dtype),
                   jax.ShapeDtypeStruct((8, 128), F32)),
        in_specs=[vm, vm, vm, hbm], out_specs=(vm, vm, hbm, vm),
        scratch_shapes=[pltpu.SemaphoreType.DMA((N_DEV - 1,))] * 4
        + [pltpu.SemaphoreType.DMA((7,)), pltpu.SemaphoreType.DMA((7,)), pltpu.SemaphoreType.DMA((1,))],
        compiler_params=pltpu.CompilerParams(vmem_limit_bytes=V7X_VMEM_LIMIT),
    )(c_ext, w_ada_l, b_l, w_in_l)


class _TwoLevelGather:
    def __init__(self, x_refs, out_refs, send_sems, recv_sems, local_sems):
        self.x_refs, self.out_refs = x_refs, out_refs
        self.send_sems, self.recv_sems, self.local_sems = send_sems, recv_sems, local_sems
        x, y, c = _my_coords()
        self.c = c
        self.me, self.sibling = (x, y, c), (x, y, 1 - c)
        self.chips = [(1 - x, y), (x, 1 - y), (1 - x, 1 - y)]

    def _copy(self, a, k, block, to, src=None):
        px, py, pc = block
        slab = self.out_refs[a].at[4 * px + 2 * py + pc]
        return pltpu.make_async_remote_copy(
            src_ref=slab if src is None else src, dst_ref=slab,
            send_sem=self.send_sems.at[7 * a + k], recv_sem=self.recv_sems.at[7 * a + k],
            device_id=to, device_id_type=MESH)

    def _mine(self, a):
        px, py, pc = self.me
        return pltpu.make_async_copy(self.x_refs[a], self.out_refs[a].at[4 * px + 2 * py + pc], self.local_sems.at[a])

    def _first(self, a):
        cps = [self._copy(a, 0, self.me, self.sibling, src=self.x_refs[a])]
        cps += [self._copy(a, 1 + j, self.me, (*chip, self.c), src=self.x_refs[a]) for j, chip in enumerate(self.chips)]
        return cps

    def _passed(self, a):
        return [self._copy(a, 4 + j, (*chip, self.c), self.sibling) for j, chip in enumerate(self.chips)]

    def start(self):
        for a in range(len(self.x_refs)):
            self._mine(a).start()
            for cp in self._first(a):
                cp.start()

    def forward(self):
        for a in range(len(self.x_refs)):
            passed = self._passed(a)
            for j, chip in enumerate(self.chips):
                self._copy(a, 1 + j, (*chip, self.c), self.me).wait_recv()
                passed[j].start()

    def finish(self):
        for a in range(len(self.x_refs)):
            self._copy(a, 0, self.sibling, self.me).wait_recv()
            for j, chip in enumerate(self.chips):
                self._copy(a, 4 + j, (*chip, 1 - self.c), self.me).wait_recv()
            for cp in self._first(a) + self._passed(a):
                cp.wait_send()
            self._mine(a).wait()


def _exchange_copy(src_refs, land_refs, send_sems, recv_sems, a, k, gather, receiving):
    x, y, c = _my_coords()
    me = 4 * x + 2 * y + c
    peer, pid = _peer(k)
    src = src_refs[a] if gather else src_refs[a].at[pid]
    dst = land_refs[a].at[pid if receiving else me]
    return pltpu.make_async_remote_copy(src, dst, send_sems.at[7 * a + k - 1], recv_sems.at[7 * a + k - 1],
                                        device_id=peer, device_id_type=MESH)


def _own_copy(src_refs, land_refs, send_sems, a, n, gather):
    x, y, c = _my_coords()
    me = 4 * x + 2 * y + c
    src = src_refs[a] if gather else src_refs[a].at[me]
    return pltpu.make_async_copy(src, land_refs[a].at[me], send_sems.at[7 * n + a])


def _exchange_start(srcs, gather, name):
    n = len(srcs)
    land_shapes = [(N_DEV, *s.shape) if gather else s.shape for s in srcs]

    def body(*refs):
        src_refs, send_sems, recv_sems, land_refs, token = refs[:n], refs[n], refs[n + 1], refs[2 * n + 2:3 * n + 2], refs[-1]
        for a in range(n):
            _own_copy(src_refs, land_refs, send_sems, a, n, gather).start()
            for k in range(1, N_DEV):
                _exchange_copy(src_refs, land_refs, send_sems, recv_sems, a, k, gather, receiving=False).start()
        token[...] = jnp.zeros_like(token)

    hbm = pl.BlockSpec(memory_space=pltpu.HBM)
    sem = pl.BlockSpec(memory_space=pltpu.SEMAPHORE)
    res = pl.pallas_call(
        body, name=name,
        out_shape=(pltpu.SemaphoreType.DMA((8 * n,)), pltpu.SemaphoreType.DMA((7 * n,)),
                   *[pltpu.HBM(v.shape, v.dtype) for v in srcs],
                   *[pltpu.HBM(shape, v.dtype) for shape, v in zip(land_shapes, srcs)],
                   jax.ShapeDtypeStruct((8, 128), F32)),
        in_specs=[hbm] * n,
        out_specs=(sem, sem, *([hbm] * (2 * n)), pl.BlockSpec(memory_space=pltpu.VMEM)),
        input_output_aliases={i: 2 + i for i in range(n)},
        compiler_params=pltpu.CompilerParams(has_side_effects=pltpu.SideEffectType.DATAFLOW_SIDE_EFFECTING),
    )(*[pltpu.with_memory_space_constraint(v, pltpu.HBM) for v in srcs])
    return res[0], res[1], list(res[2:2 + n]), list(res[2 + n:2 + 2 * n]), res[-1]


def _exchange_wait(send_sems, recv_sems, srcs, lands, after, gather, name):
    n = len(srcs)

    def body(*refs):
        src_refs, land_refs, s_sems, r_sems = refs[:n], refs[n:2 * n], refs[2 * n], refs[2 * n + 1]
        for a in range(n):
            _own_copy(src_refs, land_refs, s_sems, a, n, gather).wait()
            for k in range(1, N_DEV):
                _exchange_copy(src_refs, land_refs, s_sems, r_sems, a, k, gather, receiving=False).wait_send()
                _exchange_copy(src_refs, land_refs, s_sems, r_sems, a, k, gather, receiving=True).wait_recv()

    hbm = pl.BlockSpec(memory_space=pltpu.HBM)
    sem = pl.BlockSpec(memory_space=pltpu.SEMAPHORE)
    res = pl.pallas_call(
        body, name=name,
        out_shape=tuple(pltpu.HBM(v.shape, v.dtype) for v in srcs + lands),
        in_specs=[hbm] * (2 * n) + [sem, sem, pl.BlockSpec(memory_space=pl.ANY)],
        out_specs=tuple([hbm] * (2 * n)),
        input_output_aliases={i: i for i in range(2 * n)},
        compiler_params=pltpu.CompilerParams(has_side_effects=pltpu.SideEffectType.DATAFLOW_SIDE_EFFECTING),
    )(*srcs, *lands, send_sems, recv_sems, after)
    return list(res[n:])


def _load_resident(step_is_first, pairs, sem):
    @pl.when(step_is_first)
    def _():
        copies = [pltpu.make_async_copy(src, dst, sem.at[i]) for i, (src, dst) in enumerate(pairs)]
        for cp in copies:
            cp.start()
        for cp in copies:
            cp.wait()


def _load_w_in_t(step_is_first, w_hbm, w_vmem, sem):
    @pl.when(step_is_first)
    def _():
        w_vmem[D_IN:, :] = jnp.zeros((D_IN_PAD - D_IN, D_MODEL), BF16)
    _load_resident(step_is_first, [(w_hbm, w_vmem.at[pl.ds(0, D_IN)])], sem)


def _inproj_fwd(x2d, sc1p, sh1, w_in_t):
    t = x2d.shape[0]
    tm = min(PROJ_TILE, t)

    def body(x_ref, sc_ref, sh_ref, w_hbm, proj_ref, u_ref, w_vmem, sem):
        _load_w_in_t(pl.program_id(0) == 0, w_hbm, w_vmem, sem)
        xh, _ = _ln_stats(x_ref[...])
        ub = (xh * sc_ref[...] + sh_ref[...]).astype(BF16)
        u_ref[...] = ub
        proj_ref[...] = _dot(ub, w_vmem[...], NT)

    row = lambda i: (i, 0)
    fix = lambda i: (0, 0)
    return pl.pallas_call(
        body, name="inproj_fwd", grid=(t // tm,),
        in_specs=[pl.BlockSpec((tm, D_MODEL), row), pl.BlockSpec((1, D_MODEL), fix), pl.BlockSpec((1, D_MODEL), fix),
                  pl.BlockSpec(memory_space=pl.ANY)],
        out_specs=(pl.BlockSpec((tm, D_IN_PAD), row), pl.BlockSpec((tm, D_MODEL), row)),
        out_shape=(jax.ShapeDtypeStruct((t, D_IN_PAD), F32), jax.ShapeDtypeStruct((t, D_MODEL), BF16)),
        scratch_shapes=[pltpu.VMEM((D_IN_PAD, D_MODEL), BF16), pltpu.SemaphoreType.DMA((1,))],
        compiler_params=pltpu.CompilerParams(dimension_semantics=("arbitrary",), vmem_limit_bytes=V7X_VMEM_LIMIT),
    )(x2d, sc1p, sh1, w_in_t)


CHUNK_SHIFT = CHUNK.bit_length() - 1


def _ret_tables(t, tl):
    r = lax.broadcasted_iota(jnp.int32, (tl, tl), 0)
    c = lax.broadcasted_iota(jnp.int32, (tl, tl), 1)
    allowed = jnp.right_shift(c, CHUNK_SHIFT) <= jnp.right_shift(r, CHUNK_SHIFT)
    dist = jnp.abs(r - c).astype(F32)
    rowf = lax.broadcasted_iota(jnp.int32, (tl, RET_D), 0).astype(F32)
    lgs = [_log_gamma(h) for h in range(RET_HEADS)]
    dec = jnp.stack([jnp.where(allowed, jnp.exp(lg * dist), 0.0) for lg in lgs])
    qkd = jnp.stack([jnp.exp(lg * (rowf + 1.0)) for lg in lgs] + [jnp.exp(lg * (tl - 1.0 - rowf)) for lg in lgs])
    inv = 1.0 / (10000.0 ** jnp.linspace(0.0, 1.0, RET_D // 2, dtype=F32))
    off = jnp.arange(tl, dtype=F32)[:, None] * inv[None, :]
    start = (jnp.arange(t // tl, dtype=F32) * tl)[:, None] * inv[None, :]
    co, so = jnp.cos(off), jnp.sin(off)
    rot_in = jnp.stack([jnp.concatenate([co, co], 1), jnp.concatenate([so, so], 1),
                        jnp.concatenate([-co, co], 1), jnp.concatenate([-so, so], 1)])
    cs, ss = jnp.cos(start), jnp.sin(start)
    rot_tile = jnp.concatenate([cs, cs, ss, ss], axis=1)
    rot_tile = jnp.broadcast_to(rot_tile[:, None, :], (t // tl, 8, 2 * RET_D))
    return dec, qkd, rot_in, rot_tile


def _tile_gammas(tl):
    return [float(np.exp(np.float32(_log_gamma(h)) * np.float32(tl))) for h in range(RET_HEADS)]


def _tile_rotary(rot_in_ref, rot_tile_ref, j):
    ca, sa = rot_tile_ref[j, 0:1, 0:RET_D], rot_tile_ref[j, 0:1, RET_D:2 * RET_D]
    cosv = ca * rot_in_ref[0] - sa * rot_in_ref[1]
    sinv = sa * rot_in_ref[2] + ca * rot_in_ref[3]
    return cosv, sinv


def _gla_consts(tl):
    r = lax.broadcasted_iota(jnp.int32, (tl, tl), 0)
    c = lax.broadcasted_iota(jnp.int32, (tl, tl), 1)
    ltri = (c <= r).astype(F32)
    utri = (c >= r).astype(F32)
    lane = lax.broadcasted_iota(jnp.int32, (1, GLA_KW), 1)
    hmask = [((lane >= h * GLA_DK) & (lane < (h + 1) * GLA_DK)).astype(F32) for h in range(GLA_HEADS)]
    rs = lax.broadcasted_iota(jnp.int32, (GLA_HEADS * tl, tl), 0) & (tl - 1)
    cs = lax.broadcasted_iota(jnp.int32, (GLA_HEADS * tl, tl), 1)
    lower = cs <= rs
    same = jnp.right_shift(cs, CHUNK_SHIFT) == jnp.right_shift(rs, CHUNK_SHIFT)
    upper = jnp.logical_and(jnp.logical_not(lower), same)
    return dict(ltri=ltri, utri=utri, hmask=hmask, lower=lower, upper=upper)


def _tile_rows(j, tl):
    return pl.ds(j * tl, tl) if isinstance(j, int) else pl.ds(pl.multiple_of(j * tl, tl), tl)


def _for_tiles(cps, fn):
    for j in range(cps):
        fn(j, 0)


def _rotate(v, cosv, sinv):
    return v * cosv + pltpu.roll(v, RET_D // 2, 1) * sinv


def _rotate_t(d, cosv, sinv):
    return d * cosv + pltpu.roll(d * sinv, RET_D // 2, 1)


def _stack_heads(v, hmask):
    return jnp.concatenate([v * hmask[h] for h in range(GLA_HEADS)], axis=0)


def _gla_gates(glr, gw, gb, ltri, tl):
    z = _dot_split(glr, gw, NN) + gb
    la = (jnp.minimum(z, 0.0) - jnp.log(1.0 + jnp.exp(-jnp.abs(z)))) * (1.0 / GATE_TAU)
    b = _dot_split(ltri, la, NN, a_exact=True)
    level = b[tl // 2 - 1:tl // 2, :]
    ep = jnp.exp(jnp.clip(b - level, -80.0, 80.0))
    em = jnp.exp(jnp.clip(level - b, -80.0, 80.0))
    bl = b[tl - 1:tl, :]
    return z, b, bl, ep, em


def _mixer_fwd(proj, tables, gw_pad, gb, rnw, gnw):
    t = proj.shape[0]
    tc = min(MIX_TILE, t)
    tr, tg = min(RET_SUB, tc), min(GLA_SUB, tc)
    nsteps = t // tc
    scale_r = RET_D ** -0.5
    scale_g = GLA_DK ** -0.5
    gammas = _tile_gammas(tr)

    def body(rq_ref, rk_ref, rv_ref, rg_ref, gq_ref, gk_ref, gv_ref, gg_ref, glr_ref,
             dec_ref, qkd_ref, rot_in_ref, rot_tile_ref, gw_ref, gb_ref, rnw_ref, gnw_ref,
             mix_ref, oraw_ref, qrb_ref, krb_ref, rst_ref, sst_ref, r_scr, s_scr):
        @pl.when(pl.program_id(0) == 0)
        def _():
            r_scr[...] = jnp.zeros_like(r_scr)
            s_scr[...] = jnp.zeros_like(s_scr)

        gla_k = _gla_consts(tg)

        def ret_tile(j, carry):
            rows = _tile_rows(j, tr)
            cosv, sinv = _tile_rotary(rot_in_ref, rot_tile_ref, j)
            for h in range(RET_HEADS):
                cols = slice(h * RET_D, (h + 1) * RET_D)
                qr = _rotate(rq_ref[rows, cols], cosv, sinv) * scale_r
                kr = _rotate(rk_ref[rows, cols], cosv, sinv)
                vb = rv_ref[rows, cols].astype(BF16)
                qb, kb = qr.astype(BF16), kr.astype(BF16)
                qrb_ref[rows, cols] = qb
                krb_ref[rows, cols] = kb
                p = _dot(qb, kb, NT) * dec_ref[h]
                rp = r_scr[cols, :]
                o = _dot(p.astype(BF16), vb) + _dot((qr * qkd_ref[h]).astype(BF16), rp.astype(BF16))
                rst_ref[j, cols, :] = rp
                r_scr[cols, :] = gammas[h] * rp + _dot((kr * qkd_ref[RET_HEADS + h]).astype(BF16), vb, TN)
                oraw_ref[rows, cols] = o
                oc = o - jnp.mean(o, axis=-1, keepdims=True)
                n = oc * lax.rsqrt(jnp.mean(oc * oc, axis=-1, keepdims=True) + LN_EPS)
                g = rg_ref[rows, cols]
                mix_ref[rows, cols] = (n * rnw_ref[:, cols] * (g * _sigmoid(g))).astype(BF16)
            return carry

        def gla_tile(j, carry):
            k = gla_k
            tl = tg
            rows = _tile_rows(j, tg)
            _, b, bl, ep, em = _gla_gates(glr_ref[rows, :], gw_ref[...], gb_ref[...], k["ltri"], tl)
            qs = gq_ref[rows, :] * scale_g
            kk = gk_ref[rows, :]
            x_all = _dot(_stack_heads(qs * ep, k["hmask"]).astype(BF16), (kk * em).astype(BF16), NT)
            y_all = _dot(_stack_heads(qs * em, k["hmask"]).astype(BF16), (kk * ep).astype(BF16), NT)
            a_all = jnp.where(k["lower"], x_all, jnp.where(k["upper"], y_all, 0.0)).astype(BF16)
            st = s_scr[...]
            oq = _dot(_stack_heads(qs * jnp.exp(b), k["hmask"]).astype(BF16), st.astype(BF16), NT)
            kg = kk * jnp.exp(bl - b)
            sst_ref[j] = st
            st_new = st * jnp.exp(bl)
            for h in range(GLA_HEADS):
                cols = slice(h * GLA_DV, (h + 1) * GLA_DV)
                hr = slice(h * tl, (h + 1) * tl)
                vb = gv_ref[rows, cols].astype(BF16)
                o = _dot(a_all[hr, :], vb) + oq[hr, :]
                st_new = st_new + _dot(vb, (kg * k["hmask"][h]).astype(BF16), TN)
                ocols = slice(RET_W + h * GLA_DV, RET_W + (h + 1) * GLA_DV)
                oraw_ref[rows, ocols] = o
                n = o * lax.rsqrt(jnp.mean(o * o, axis=-1, keepdims=True) + LN_EPS)
                g = gg_ref[rows, cols]
                mix_ref[rows, ocols] = (n * gnw_ref[:, cols] * (g * _sigmoid(g))).astype(BF16)
            s_scr[...] = st_new
            return carry

        _for_tiles(tc // tr, ret_tile)
        _for_tiles(tc // tg, gla_tile)

    def col(width, off):
        return pl.BlockSpec((tc, width), lambda i, o=off // width: (i, o))

    fix = lambda i: (0, 0)
    fix3 = lambda i: (0, 0, 0)
    dec, qkd, rot_in, rot_tile = tables
    in_specs = [col(RET_W, OFF_RQ), col(RET_W, OFF_RK), col(RET_W, OFF_RV), col(RET_W, OFF_RG),
                col(GLA_KW, OFF_GQ), col(GLA_KW, OFF_GK), col(GLA_VW, OFF_GV), col(GLA_VW, OFF_GG),
                col(V7X_LANES, OFF_GLR),
                pl.BlockSpec(dec.shape, fix3), pl.BlockSpec(qkd.shape, fix3), pl.BlockSpec(rot_in.shape, fix3),
                pl.BlockSpec((tc // tr, 8, 2 * RET_D), lambda i: (i, 0, 0)),
                pl.BlockSpec((V7X_LANES, GLA_KW), fix), pl.BlockSpec((1, GLA_KW), fix),
                pl.BlockSpec((1, RET_W), fix), pl.BlockSpec((1, GLA_VW), fix)]
    half = pl.BlockSpec((tc, RET_W), lambda i: (i, 0))
    out_specs = (pl.BlockSpec((tc, D_MODEL), lambda i: (i, 0)), pl.BlockSpec((tc, D_MODEL), lambda i: (i, 0)),
                 half, half,
                 pl.BlockSpec((tc // tr, RET_W, RET_D), lambda i: (i, 0, 0)),
                 pl.BlockSpec((tc // tg, GLA_DV, GLA_KW), lambda i: (i, 0, 0)))
    out_shape = (jax.ShapeDtypeStruct((t, D_MODEL), BF16), jax.ShapeDtypeStruct((t, D_MODEL), F32),
                 jax.ShapeDtypeStruct((t, RET_W), BF16), jax.ShapeDtypeStruct((t, RET_W), BF16),
                 jax.ShapeDtypeStruct((t // tr, RET_W, RET_D), F32),
                 jax.ShapeDtypeStruct((t // tg, GLA_DV, GLA_KW), F32))
    return pl.pallas_call(
        body, name="mixer_fwd", grid=(nsteps,), in_specs=in_specs, out_specs=out_specs, out_shape=out_shape,
        scratch_shapes=[pltpu.VMEM((RET_W, RET_D), F32), pltpu.VMEM((GLA_DV, GLA_KW), F32)],
        compiler_params=pltpu.CompilerParams(dimension_semantics=("arbitrary",), vmem_limit_bytes=V7X_VMEM_LIMIT),
    )(*([proj] * 9), dec, qkd, rot_in, rot_tile, gw_pad, gb, rnw, gnw)


def _mid_fwd(mixed, x2d, target, vecs, w_out_b, w1_b, w2_b):
    t = x2d.shape[0]
    tm = min(ROW_TILE, t)

    def body(mix_ref, x_ref, tgt_ref, v_ref, wo_hbm, w1_hbm, w2_hbm,
             m_ref, x1n_ref, rstd_ref, u2_ref, a_ref, df_ref, dh2_ref, acc_ref, wo, w1, w2, sem):
        first = pl.program_id(0) == 0
        _load_resident(first, [(wo_hbm, wo), (w1_hbm, w1), (w2_hbm, w2)], sem)

        @pl.when(first)
        def _():
            acc_ref[...] = jnp.zeros_like(acc_ref)

        gate1, sc2p, sh2, gate2 = v_ref[0:1, :], v_ref[1:2, :], v_ref[2:3, :], v_ref[3:4, :]
        l1w, l1b, l2w, l2b = v_ref[4:5, :], v_ref[5:6, :], v_ref[6:7, :], v_ref[7:8, :]
        m = _dot(mix_ref[...], wo[...])
        m_ref[...] = m.astype(BF16)
        x1n, rstd1 = _ln_stats(ALPHA * x_ref[...] + gate1 * m)
        x1n_ref[...] = x1n
        rstd_ref[...] = rstd1
        x1 = x1n * l1w + l1b
        xh1, _ = _ln_stats(x1)
        u2 = (xh1 * sc2p + sh2).astype(BF16)
        u2_ref[...] = u2
        f = jnp.zeros((tm, D_MODEL), F32)
        for j in range(N_DEV):
            cols = slice(j * FF_COLS, (j + 1) * FF_COLS)
            a = _dot(u2, w1[j])
            a_ref[:, cols] = a.astype(BF16)
            r = jnp.maximum(a, 0.0)
            f = f + _dot((r * r).astype(BF16), w2[cols, :])
        yh, rstd2 = _ln_stats(ALPHA * x1 + gate2 * f)
        e = yh * l2w + l2b - tgt_ref[...]
        dy = e * (1.0 / D_MODEL)
        dh2 = _ln_bwd(dy * l2w, yh, rstd2)
        dh2_ref[...] = dh2
        df_ref[...] = (dh2 * gate2).astype(BF16)
        acc_ref[0:1, :] += jnp.sum(dy * yh, axis=0, keepdims=True)
        acc_ref[1:2, :] += jnp.sum(dy, axis=0, keepdims=True)
        acc_ref[2:3, :] += jnp.sum(dh2 * f, axis=0, keepdims=True)
        acc_ref[3:4, :] += jnp.sum(e * e, axis=0, keepdims=True) * (0.5 / D_MODEL)

    row = lambda i: (i, 0)
    fix = lambda i: (0, 0)
    hbm = pl.BlockSpec(memory_space=pl.ANY)
    return pl.pallas_call(
        body, name="mid_fwd", grid=(t // tm,),
        in_specs=[pl.BlockSpec((tm, D_MODEL), row), pl.BlockSpec((tm, D_MODEL), row), pl.BlockSpec((tm, D_MODEL), row),
                  pl.BlockSpec((8, D_MODEL), fix), hbm, hbm, hbm],
        out_specs=(pl.BlockSpec((tm, D_MODEL), row), pl.BlockSpec((tm, D_MODEL), row), pl.BlockSpec((tm, 1), row),
                   pl.BlockSpec((tm, D_MODEL), row), pl.BlockSpec((tm, D_FF), row), pl.BlockSpec((tm, D_MODEL), row),
                   pl.BlockSpec((tm, D_MODEL), row), pl.BlockSpec((8, D_MODEL), fix)),
        out_shape=(jax.ShapeDtypeStruct((t, D_MODEL), BF16), jax.ShapeDtypeStruct((t, D_MODEL), F32),
                   jax.ShapeDtypeStruct((t, 1), F32), jax.ShapeDtypeStruct((t, D_MODEL), BF16),
                   jax.ShapeDtypeStruct((t, D_FF), BF16), jax.ShapeDtypeStruct((t, D_MODEL), BF16),
                   jax.ShapeDtypeStruct((t, D_MODEL), F32), jax.ShapeDtypeStruct((8, D_MODEL), F32)),
        scratch_shapes=[pltpu.VMEM((D_MODEL, D_MODEL), BF16), pltpu.VMEM((N_DEV, D_MODEL, FF_COLS), BF16),
                        pltpu.VMEM((D_FF, D_MODEL), BF16), pltpu.SemaphoreType.DMA((3,))],
        compiler_params=pltpu.CompilerParams(dimension_semantics=("arbitrary",), vmem_limit_bytes=V7X_VMEM_LIMIT),
    )(mixed, x2d, target, vecs, w_out_b, w1_b, w2_b)


def _ffn_bwd(df, a, dh2, x1n, rstd1, m, vecs, w_out_b, w1_b, w2_b):
    t = x1n.shape[0]
    tm = min(ROW_TILE, t)

    def body(df_ref, a_ref, dh2_ref, x1n_ref, rstd_ref, m_ref, v_ref, wo_hbm, w1_hbm, w2_hbm,
             da_ref, dm_ref, dmix_ref, dxa_ref, acc_ref, wo, w1, w2, sem):
        first = pl.program_id(0) == 0
        _load_resident(first, [(wo_hbm, wo), (w1_hbm, w1), (w2_hbm, w2)], sem)

        @pl.when(first)
        def _():
            acc_ref[...] = jnp.zeros_like(acc_ref)

        gate1, sc2p, l1w, l1b = v_ref[0:1, :], v_ref[1:2, :], v_ref[2:3, :], v_ref[3:4, :]
        df = df_ref[...]
        du2 = jnp.zeros((tm, D_MODEL), F32)
        for j in range(N_DEV):
            cols = slice(j * FF_COLS, (j + 1) * FF_COLS)
            dr2 = _dot(df, w2[cols, :], NT)
            da = (dr2 * (2.0 * jnp.maximum(a_ref[:, cols].astype(F32), 0.0))).astype(BF16)
            da_ref[:, cols] = da
            du2 = du2 + _dot(da, w1[j], NT)
        x1n = x1n_ref[...]
        xh1, rstd0 = _ln_stats(x1n * l1w + l1b)
        dx1 = ALPHA * dh2_ref[...] + _ln_bwd(du2 * sc2p, xh1, rstd0)
        dh1 = _ln_bwd(dx1 * l1w, x1n, rstd_ref[...])
        dxa_ref[...] = ALPHA * dh1
        dm = (dh1 * gate1).astype(BF16)
        dm_ref[...] = dm
        dmix_ref[...] = _dot(dm, wo[...], NT)
        acc_ref[0:1, :] += jnp.sum(du2 * xh1, axis=0, keepdims=True)
        acc_ref[1:2, :] += jnp.sum(du2, axis=0, keepdims=True)
        acc_ref[2:3, :] += jnp.sum(dx1 * x1n, axis=0, keepdims=True)
        acc_ref[3:4, :] += jnp.sum(dx1, axis=0, keepdims=True)
        acc_ref[4:5, :] += jnp.sum(dh1 * m_ref[...].astype(F32), axis=0, keepdims=True)

    row = lambda i: (i, 0)
    fix = lambda i: (0, 0)
    hbm = pl.BlockSpec(memory_space=pl.ANY)
    return pl.pallas_call(
        body, name="ffn_bwd", grid=(t // tm,),
        in_specs=[pl.BlockSpec((tm, D_MODEL), row), pl.BlockSpec((tm, D_FF), row), pl.BlockSpec((tm, D_MODEL), row),
                  pl.BlockSpec((tm, D_MODEL), row), pl.BlockSpec((tm, 1), row), pl.BlockSpec((tm, D_MODEL), row),
                  pl.BlockSpec((8, D_MODEL), fix), hbm, hbm, hbm],
        out_specs=(pl.BlockSpec((tm, D_FF), row), pl.BlockSpec((tm, D_MODEL), row), pl.BlockSpec((tm, D_MODEL), row),
                   pl.BlockSpec((tm, D_MODEL), row), pl.BlockSpec((8, D_MODEL), fix)),
        out_shape=(jax.ShapeDtypeStruct((t, D_FF), BF16), jax.ShapeDtypeStruct((t, D_MODEL), BF16),
                   jax.ShapeDtypeStruct((t, D_MODEL), F32), jax.ShapeDtypeStruct((t, D_MODEL), F32),
                   jax.ShapeDtypeStruct((8, D_MODEL), F32)),
        scratch_shapes=[pltpu.VMEM((D_MODEL, D_MODEL), BF16), pltpu.VMEM((N_DEV, D_MODEL, FF_COLS), BF16),
                        pltpu.VMEM((D_FF, D_MODEL), BF16), pltpu.SemaphoreType.DMA((3,))],
        compiler_params=pltpu.CompilerParams(dimension_semantics=("arbitrary",), vmem_limit_bytes=V7X_VMEM_LIMIT),
    )(df, a, dh2, x1n, rstd1, m, vecs, w_out_b, w1_b, w2_b)


def _matmul_tn(lhs, rhs, tmm, tn, tk, name, relu_sq=False, col_slab=None, out_rows=None):
    t, mm = lhs.shape
    assert out_rows is None or (col_slab is None and tmm == mm)
    nn = rhs.shape[1]
    tk = min(tk, t)
    nk = t // tk

    def body(l_ref, r_ref, o_ref, acc):
        kk = pl.program_id(2)

        @pl.when(kk == 0)
        def _():
            acc[...] = jnp.zeros_like(acc)

        l = l_ref[...]
        if relu_sq:
            lf = jnp.maximum(l.astype(F32), 0.0)
            l = (lf * lf).astype(BF16)
        acc[...] += _dot(l, r_ref[...], TN)

        @pl.when(kk == nk - 1)
        def _():
            if out_rows is not None:
                for s in range(N_DEV):
                    o_ref[s] = acc[s * out_rows:(s + 1) * out_rows, :].astype(o_ref.dtype)
            elif col_slab is None:
                o_ref[...] = acc[...].astype(o_ref.dtype)
            else:
                for s in range(tn // col_slab):
                    o_ref[s] = acc[:, s * col_slab:(s + 1) * col_slab].astype(o_ref.dtype)

    if out_rows is not None:
        out_spec = pl.BlockSpec((N_DEV, out_rows, tn), lambda i, j, k: (0, 0, j))
        out_shape = jax.ShapeDtypeStruct((N_DEV, out_rows, nn), BF16)
    elif col_slab is None:
        out_spec = pl.BlockSpec((tmm, tn), lambda i, j, k: (i, j))
        out_shape = jax.ShapeDtypeStruct((mm, nn), BF16)
    else:
        out_spec = pl.BlockSpec((tn // col_slab, tmm, col_slab), lambda i, j, k: (j, i, 0))
        out_shape = jax.ShapeDtypeStruct((nn // col_slab, mm, col_slab), BF16)
    return pl.pallas_call(
        body, name=name, grid=(mm // tmm, nn // tn, nk),
        in_specs=[pl.BlockSpec((tk, tmm), lambda i, j, k: (k, i)), pl.BlockSpec((tk, tn), lambda i, j, k: (k, j))],
        out_specs=out_spec,
        out_shape=out_shape,
        scratch_shapes=[pltpu.VMEM((tmm, tn), F32)],
        compiler_params=pltpu.CompilerParams(dimension_semantics=("arbitrary", "arbitrary", "arbitrary"),
                                             vmem_limit_bytes=V7X_VMEM_LIMIT),
    )(lhs, rhs)


def _mixer_bwd(dmix, proj, qrb, krb, oraw, tables, rst, sst, gw_pad, gb, rnw, gnw):
    t = proj.shape[0]
    tc = min(MIX_TILE, t)
    tr, tg = min(RET_SUB, tc), min(GLA_SUB, tc)
    nsteps = t // tc
    scale_r = RET_D ** -0.5
    scale_g = GLA_DK ** -0.5
    gammas = _tile_gammas(tr)

    def body(dmix_ref, qrb_ref, krb_ref, rv_ref, rg_ref, gq_ref, gk_ref, gv_ref, gg_ref, glr_ref, oraw_ref,
             dec_ref, qkd_ref, rot_in_ref, rot_tile_ref, rst_ref, sst_ref, gw_ref, gb_ref, rnw_ref, gnw_ref,
             dproj_ref, dgw_ref, dvec_ref, dr_scr, ds_scr):
        @pl.when(pl.program_id(0) == 0)
        def _():
            dr_scr[...] = jnp.zeros_like(dr_scr)
            ds_scr[...] = jnp.zeros_like(ds_scr)
            dgw_ref[...] = jnp.zeros_like(dgw_ref)
            dvec_ref[...] = jnp.zeros_like(dvec_ref)

        gla_k = _gla_consts(tg)
        last_row = lax.broadcasted_iota(jnp.int32, (tg, GLA_KW), 0) == tg - 1

        def ret_tile(jj, carry):
            j = tc // tr - 1 - jj
            rows = _tile_rows(j, tr)
            cosv, sinv = _tile_rotary(rot_in_ref, rot_tile_ref, j)
            for h in range(RET_HEADS):
                cols = slice(h * RET_D, (h + 1) * RET_D)
                o = oraw_ref[rows, cols]
                g = rg_ref[rows, cols]
                w = rnw_ref[:, cols]
                dout = dmix_ref[rows, cols]
                oc = o - jnp.mean(o, axis=-1, keepdims=True)
                inv = lax.rsqrt(jnp.mean(oc * oc, axis=-1, keepdims=True) + LN_EPS)
                n = oc * inv
                sg = _sigmoid(g)
                sil = g * sg
                dn = dout * w * sil
                dvec_ref[0:1, cols] += jnp.sum(dout * n * sil, axis=0, keepdims=True)
                dproj_ref[rows, OFF_RG + h * RET_D:OFF_RG + (h + 1) * RET_D] = (
                    dout * n * w * (sg * (1.0 + g * (1.0 - sg)))).astype(BF16)
                doc = inv * (dn - n * jnp.mean(dn * n, axis=-1, keepdims=True))
                do = doc - jnp.mean(doc, axis=-1, keepdims=True)

                qb, kb = qrb_ref[rows, cols], krb_ref[rows, cols]
                qr, kr = qb.astype(F32), kb.astype(F32)
                vb = rv_ref[rows, cols].astype(BF16)
                dob = do.astype(BF16)
                qd, kd = qkd_ref[h], qkd_ref[RET_HEADS + h]
                p = _dot(qb, kb, NT) * dec_ref[h]
                rp = rst_ref[j, cols, :].astype(BF16)
                dr = dr_scr[cols, :]
                drb = dr.astype(BF16)
                dpb = (_dot(dob, vb, NT) * dec_ref[h]).astype(BF16)
                dqr = _dot(dpb, kb) + _dot(dob, rp, NT) * qd
                dkr = _dot(dpb, qb, TN) + _dot(vb, drb, NT) * kd
                dv = _dot(p.astype(BF16), dob, TN) + _dot((kr * kd).astype(BF16), drb)
                dr_scr[cols, :] = gammas[h] * dr + _dot((qr * qd).astype(BF16), dob, TN)
                dproj_ref[rows, OFF_RQ + h * RET_D:OFF_RQ + (h + 1) * RET_D] = (
                    _rotate_t(dqr, cosv, sinv) * scale_r).astype(BF16)
                dproj_ref[rows, OFF_RK + h * RET_D:OFF_RK + (h + 1) * RET_D] = _rotate_t(dkr, cosv, sinv).astype(BF16)
                dproj_ref[rows, OFF_RV + h * RET_D:OFF_RV + (h + 1) * RET_D] = dv.astype(BF16)
            return carry

        def gla_tile(jj, carry):
            k = gla_k
            tl = tg
            j = tc // tg - 1 - jj
            rows = _tile_rows(j, tg)
            glr = glr_ref[rows, :]
            z, b, bl, ep, em = _gla_gates(glr, gw_ref[...], gb_ref[...], k["ltri"], tl)
            qs = gq_ref[rows, :] * scale_g
            kk = gk_ref[rows, :]
            eb = jnp.exp(b)
            ekb = jnp.exp(bl - b)
            ebl = jnp.exp(bl)
            ql, qu, kl, ku = qs * ep, qs * em, kk * em, kk * ep
            qg, kg = qs * eb, kk * ekb
            qlm = _stack_heads(ql, k["hmask"]).astype(BF16)
            qum = _stack_heads(qu, k["hmask"]).astype(BF16)
            klb, kub = kl.astype(BF16), ku.astype(BF16)
            a_all = jnp.where(k["lower"], _dot(qlm, klb, NT),
                              jnp.where(k["upper"], _dot(qum, kub, NT), 0.0)).astype(BF16)
            st = sst_ref[j]
            stb = st.astype(BF16)
            ds = ds_scr[...]
            dsb = ds.astype(BF16)
            ds_new = ds * ebl
            da_parts = []
            dqg = jnp.zeros((tl, GLA_KW), F32)
            dkg = jnp.zeros((tl, GLA_KW), F32)
            for h in range(GLA_HEADS):
                cols = slice(h * GLA_DV, (h + 1) * GLA_DV)
                hr = slice(h * tl, (h + 1) * tl)
                ocols = slice(RET_W + h * GLA_DV, RET_W + (h + 1) * GLA_DV)
                o = oraw_ref[rows, ocols]
                g = gg_ref[rows, cols]
                w = gnw_ref[:, cols]
                dout = dmix_ref[rows, ocols]
                inv = lax.rsqrt(jnp.mean(o * o, axis=-1, keepdims=True) + LN_EPS)
                n = o * inv
                sg = _sigmoid(g)
                sil = g * sg
                dn = dout * w * sil
                dvec_ref[1:2, cols] += jnp.sum(dout * n * sil, axis=0, keepdims=True)
                dproj_ref[rows, OFF_GG + h * GLA_DV:OFF_GG + (h + 1) * GLA_DV] = (
                    dout * n * w * (sg * (1.0 + g * (1.0 - sg)))).astype(BF16)
                dob = (inv * (dn - n * jnp.mean(dn * n, axis=-1, keepdims=True))).astype(BF16)
                vb = gv_ref[rows, cols].astype(BF16)
                mh = k["hmask"][h]
                da_parts.append(_dot(dob, vb, NT))
                dv = _dot(a_all[hr, :], dob, TN) + _dot((kg * mh).astype(BF16), dsb, NT)
                dproj_ref[rows, OFF_GV + h * GLA_DV:OFF_GV + (h + 1) * GLA_DV] = dv.astype(BF16)
                dkg = dkg + mh * _dot(vb, dsb)
                dqg = dqg + mh * _dot(dob, stb)
                ds_new = ds_new + _dot(dob, (qg * mh).astype(BF16), TN)
            da_all = jnp.concatenate(da_parts, axis=0)
            dal = jnp.where(k["lower"], da_all, 0.0).astype(BF16)
            dau = jnp.where(k["upper"], da_all, 0.0).astype(BF16)
            dqlm = _dot(dal, klb)
            dqum = _dot(dau, kub)
            dql = jnp.zeros((tl, GLA_KW), F32)
            dqu = jnp.zeros((tl, GLA_KW), F32)
            for h in range(GLA_HEADS):
                hr = slice(h * tl, (h + 1) * tl)
                dql = dql + k["hmask"][h] * dqlm[hr, :]
                dqu = dqu + k["hmask"][h] * dqum[hr, :]
            dkl = _dot(dal, qlm, TN)
            dku = _dot(dau, qum, TN)
            dbl = (jnp.sum(dkg * kg, axis=0, keepdims=True)
                   + jnp.sum(ds * st, axis=0, keepdims=True) * ebl)
            ds_scr[...] = ds_new
            dqs = dql * ep + dqu * em + dqg * eb
            dk = dkl * em + dku * ep + dkg * ekb
            db = dql * ql - dkl * kl - dqu * qu + dku * ku + dqg * qg - dkg * kg
            db = db + jnp.where(last_row, dbl, 0.0)
            dla = _dot_split(k["utri"], db, NN, a_exact=True)
            dz = dla * (1.0 / GATE_TAU) * _sigmoid(-z)
            dvec_ref[2:3, 0:GLA_KW] += jnp.sum(dz, axis=0, keepdims=True)
            dgw_ref[...] += _dot_split(glr, dz, TN)
            dproj_ref[rows, OFF_GLR:D_IN_PAD] = _dot(dz.astype(BF16), gw_ref[...].astype(BF16), NT).astype(BF16)
            dproj_ref[rows, OFF_GQ:OFF_GQ + GLA_KW] = (dqs * scale_g).astype(BF16)
            dproj_ref[rows, OFF_GK:OFF_GK + GLA_KW] = dk.astype(BF16)
            return carry

        _for_tiles(tc // tr, ret_tile)
        _for_tiles(tc // tg, gla_tile)

    rev = lambda i: (nsteps - 1 - i, 0)

    def col(width, off):
        return pl.BlockSpec((tc, width), lambda i, o=off // width: (nsteps - 1 - i, o))

    fix = lambda i: (0, 0)
    fix3 = lambda i: (0, 0, 0)
    dec, qkd, rot_in, rot_tile = tables
    half = pl.BlockSpec((tc, RET_W), rev)
    in_specs = [pl.BlockSpec((tc, D_MODEL), rev), half, half, col(RET_W, OFF_RV), col(RET_W, OFF_RG),
                col(GLA_KW, OFF_GQ), col(GLA_KW, OFF_GK), col(GLA_VW, OFF_GV), col(GLA_VW, OFF_GG),
                col(V7X_LANES, OFF_GLR),
                pl.BlockSpec((tc, D_MODEL), rev),
                pl.BlockSpec(dec.shape, fix3), pl.BlockSpec(qkd.shape, fix3), pl.BlockSpec(rot_in.shape, fix3),
                pl.BlockSpec((tc // tr, 8, 2 * RET_D), lambda i: (nsteps - 1 - i, 0, 0)),
                pl.BlockSpec((tc // tr, RET_W, RET_D), lambda i: (nsteps - 1 - i, 0, 0)),
                pl.BlockSpec((tc // tg, GLA_DV, GLA_KW), lambda i: (nsteps - 1 - i, 0, 0)),
                pl.BlockSpec((V7X_LANES, GLA_KW), fix), pl.BlockSpec((1, GLA_KW), fix),
                pl.BlockSpec((1, RET_W), fix), pl.BlockSpec((1, GLA_VW), fix)]
    out_specs = (pl.BlockSpec((tc, D_IN_PAD), rev), pl.BlockSpec((V7X_LANES, GLA_KW), fix),
                 pl.BlockSpec((8, RET_W), fix))
    out_shape = (jax.ShapeDtypeStruct((t, D_IN_PAD), BF16), jax.ShapeDtypeStruct((V7X_LANES, GLA_KW), F32),
                 jax.ShapeDtypeStruct((8, RET_W), F32))
    return pl.pallas_call(
        body, name="mixer_bwd", grid=(nsteps,), in_specs=in_specs, out_specs=out_specs, out_shape=out_shape,
        scratch_shapes=[pltpu.VMEM((RET_W, RET_D), F32), pltpu.VMEM((GLA_DV, GLA_KW), F32)],
        compiler_params=pltpu.CompilerParams(dimension_semantics=("arbitrary",), vmem_limit_bytes=V7X_VMEM_LIMIT),
    )(dmix, qrb, krb, *([proj] * 7), oraw, dec, qkd, rot_in, rot_tile, rst, sst, gw_pad, gb, rnw, gnw)


def _inproj_bwd(dproj, x2d, dxa, sc1p, w_in_t):
    t = x2d.shape[0]
    tm = min(2 * PROJ_TILE, t)

    def body(dp_ref, x_ref, dxa_ref, sc_ref, w_hbm, gx_ref, acc_ref, w_vmem, sem):
        first = pl.program_id(0) == 0
        _load_w_in_t(first, w_hbm, w_vmem, sem)

        @pl.when(first)
        def _():
            acc_ref[...] = jnp.zeros_like(acc_ref)

        du = _dot(dp_ref[...], w_vmem[...])
        xh, rstd = _ln_stats(x_ref[...])
        gx_ref[...] = dxa_ref[...] + _ln_bwd(du * sc_ref[...], xh, rstd)
        acc_ref[0:1, :] += jnp.sum(du * xh, axis=0, keepdims=True)
        acc_ref[1:2, :] += jnp.sum(du, axis=0, keepdims=True)

    row = lambda i: (i, 0)
    fix = lambda i: (0, 0)
    return pl.pallas_call(
        body, name="inproj_bwd", grid=(t // tm,),
        in_specs=[pl.BlockSpec((tm, D_IN_PAD), row), pl.BlockSpec((tm, D_MODEL), row), pl.BlockSpec((tm, D_MODEL), row),
                  pl.BlockSpec((1, D_MODEL), fix), pl.BlockSpec(memory_space=pl.ANY)],
        out_specs=(pl.BlockSpec((tm, D_MODEL), row), pl.BlockSpec((8, D_MODEL), fix)),
        out_shape=(jax.ShapeDtypeStruct((t, D_MODEL), F32), jax.ShapeDtypeStruct((8, D_MODEL), F32)),
        scratch_shapes=[pltpu.VMEM((D_IN_PAD, D_MODEL), BF16), pltpu.SemaphoreType.DMA((1,))],
        compiler_params=pltpu.CompilerParams(dimension_semantics=("arbitrary",), vmem_limit_bytes=V7X_VMEM_LIMIT),
    )(dproj, x2d, dxa, sc1p, w_in_t)


def _adam_math(w, g, m, v):
    m = ADAM_B1 * m + (1.0 - ADAM_B1) * g
    v = ADAM_B2 * v + (1.0 - ADAM_B2) * (g * g)
    m_hat = m / (1.0 - ADAM_B1 ** ADAM_STEP)
    v_hat = v / (1.0 - ADAM_B2 ** ADAM_STEP)
    delta = -ADAM_LR * (m_hat / (jnp.sqrt(v_hat) + ADAM_EPS) + ADAM_WD * w)
    return delta, m, v


def _adamw(w, gparts, m, v, name):
    nparts, rows, cols = gparts.shape
    tr = rows
    for cand in (512, 256, 128, 64, 32, 16, 8):
        if rows % cand == 0:
            tr = cand
            break

    def body(w_ref, g_ref, m_ref, v_ref, go_ref, d_ref, mo_ref, vo_ref):
        g = g_ref[0].astype(F32)
        for p in range(1, nparts):
            g = g + g_ref[p].astype(F32)
        delta, mn, vn = _adam_math(w_ref[...], g, m_ref[...], v_ref[...])
        go_ref[...] = g
        d_ref[...] = delta
        mo_ref[...] = mn
        vo_ref[...] = vn

    blk = pl.BlockSpec((tr, cols), lambda i: (i, 0))
    shp = jax.ShapeDtypeStruct((rows, cols), F32)
    return pl.pallas_call(
        body, name=name, grid=(rows // tr,),
        in_specs=[blk, pl.BlockSpec((nparts, tr, cols), lambda i: (0, i, 0)), blk, blk],
        out_specs=(blk, blk, blk, blk), out_shape=(shp, shp, shp, shp),
        compiler_params=pltpu.CompilerParams(dimension_semantics=("arbitrary",), vmem_limit_bytes=V7X_VMEM_LIMIT),
    )(w, gparts, m, v)


def _small_reduce(gathered, gathered_gw, c_all, dmod_cols):
    def body(g_ref, gw_ref, c_ref, dm_ref, sum_ref, gwsum_ref, gb_ref, gwa_ref):
        s = g_ref[0]
        sw = gw_ref[0]
        for p in range(1, N_DEV):
            s = s + g_ref[p]
            sw = sw + gw_ref[p]
        sum_ref[...] = s
        gwsum_ref[...] = sw
        for i in range(6):
            gb_ref[:, i * D_MODEL:(i + 1) * D_MODEL] = s[i:i + 1, :]
        cc = c_ref[...]
        gwa_ref[...] = _dot(cc * _sigmoid(cc), dm_ref[...], TN, HIGHEST)

    vm = pl.BlockSpec(memory_space=pltpu.VMEM)
    return pl.pallas_call(
        body, name="small_reduce",
        out_shape=(jax.ShapeDtypeStruct(gathered.shape[1:], F32), jax.ShapeDtypeStruct(gathered_gw.shape[1:], F32),
                   jax.ShapeDtypeStruct((1, 6 * D_MODEL), F32), jax.ShapeDtypeStruct((D_MODEL, ADA_COLS), F32)),
        in_specs=[vm] * 4, out_specs=(vm, vm, vm, vm),
        compiler_params=pltpu.CompilerParams(vmem_limit_bytes=V7X_VMEM_LIMIT),
    )(gathered, gathered_gw, c_all, dmod_cols)


SMR_LN1W, SMR_LN1B, SMR_LN2W, SMR_LN2B, SMR_NORMS, SMR_MISC = 6, 7, 8, 9, 10, 11


def _adamw_small(gsum, g_b_ada, g_ggw, params, moms, vels):
    n = len(params)

    def body(*refs):
        gsum_ref, gb_ref, gw_ref = refs[:3]
        w_refs, m_refs, v_refs = refs[3:3 + n], refs[3 + n:3 + 2 * n], refs[3 + 2 * n:3 + 3 * n]
        outs = refs[3 + 3 * n:]
        g_refs, d_refs, mo_refs, vo_refs = outs[:n - 1], outs[n - 1:2 * n - 1], outs[2 * n - 1:3 * n - 1], outs[3 * n - 1:]
        grads = [gb_ref[...],
                 gsum_ref[SMR_NORMS:SMR_NORMS + 1, 0:RET_W],
                 gsum_ref[SMR_MISC:SMR_MISC + 1, 0:GLA_KW],
                 gsum_ref[SMR_NORMS:SMR_NORMS + 1, RET_W:RET_W + GLA_VW],
                 gsum_ref[SMR_LN1W:SMR_LN1W + 1, :], gsum_ref[SMR_LN1B:SMR_LN1B + 1, :],
                 gsum_ref[SMR_LN2W:SMR_LN2W + 1, :], gsum_ref[SMR_LN2B:SMR_LN2B + 1, :],
                 gw_ref[...]]
        for i in range(n):
            delta, mn, vn = _adam_math(w_refs[i][...], grads[i], m_refs[i][...], v_refs[i][...])
            if i < n - 1:
                g_refs[i][...] = grads[i]
            d_refs[i][...] = delta
            mo_refs[i][...] = mn
            vo_refs[i][...] = vn

    vm = pl.BlockSpec(memory_space=pltpu.VMEM)
    shapes = [jax.ShapeDtypeStruct(p.shape, F32) for p in params]
    n_in = 3 + 3 * n
    out_shape = tuple(shapes[:n - 1] + shapes * 3)
    return pl.pallas_call(
        body, name="adamw_small", out_shape=out_shape,
        in_specs=[vm] * n_in, out_specs=tuple([vm] * len(out_shape)),
        compiler_params=pltpu.CompilerParams(vmem_limit_bytes=V7X_VMEM_LIMIT),
    )(gsum, g_b_ada, g_ggw, *params, *moms, *vels)


def kernel(x, c, w_ada, b_ada, w_in, ret_norm_w, gla_gate_w, gla_gate_b, gla_norm_w, w_out, ln1_w, ln1_b, w_ff1, w_ff2, ln2_w, ln2_b, loss_target, m_w_ada, m_b_ada, m_w_in, m_ret_norm_w, m_gla_gate_w, m_gla_gate_b, m_gla_norm_w, m_w_out, m_ln1_w, m_ln1_b, m_w_ff1, m_w_ff2, m_ln2_w, m_ln2_b, v_w_ada, v_b_ada, v_w_in, v_ret_norm_w, v_gla_gate_w, v_gla_gate_b, v_gla_norm_w, v_w_out, v_ln1_w, v_ln1_b, v_w_ff1, v_w_ff2, v_ln2_w, v_ln2_b):
    t = x.shape[1]
    xi, yi, ci = _my_coords()
    me = 4 * xi + 2 * yi + ci
    x2d = x[0]
    tgt = loss_target[0]

    c_ext = jnp.concatenate([c, gla_gate_w[0].reshape(1, GATE_RANK * GLA_KW // N_DEV)], axis=1)
    b_l = lax.dynamic_slice(b_ada, (0, me * ADA_COLS), (1, ADA_COLS))
    c_all3, mod_all, wi_g, ada_token = _adaln_mod(c_ext, w_ada[0], b_l, w_in[0].T.astype(BF16))

    wg = _exchange_start([(w_out[0] + ada_token[0, 0]).astype(BF16), w_ff1[0].astype(BF16), w_ff2[0].astype(BF16)],
                         True, "wgather_start")

    c_all = c_all3[:, 0, :D_MODEL]
    gate_w = c_all3[:, 0, D_MODEL:].reshape(N_DEV, GATE_RANK, GLA_KW // N_DEV)
    gate_w = gate_w.transpose(1, 0, 2).reshape(GATE_RANK, GLA_KW)
    gw_pad = jnp.zeros((V7X_LANES, GLA_KW), F32).at[:GATE_RANK].set(gate_w)
    mod = lax.dynamic_slice(mod_all, (0, me, 0), (N_DEV, 1, ADA_COLS)).reshape(6, D_MODEL)
    shift1, scale1, gate1, shift2, scale2, gate2 = [mod[i:i + 1] for i in range(6)]

    w_in_t = wi_g.reshape(D_IN, D_MODEL)

    tables = _ret_tables(t, min(RET_SUB, t))

    sc1p = 1.0 + scale1
    proj, u = _inproj_fwd(x2d, sc1p, shift1 + wg[4][0, 0], w_in_t)
    mixed, oraw, qrb, krb, rst, sst = _mixer_fwd(proj, tables, gw_pad, gla_gate_b, ret_norm_w, gla_norm_w)
    wo_g, w1_b, w2_g = _exchange_wait(*wg[:4], mixed, True, "wgather_wait")
    w_out_b = wo_g.reshape(D_MODEL, D_MODEL)
    w2_b = w2_g.reshape(D_FF, D_MODEL)
    vec_f = jnp.concatenate([gate1, 1.0 + scale2, shift2, gate2, ln1_w, ln1_b, ln2_w, ln2_b], axis=0)
    m, x1n, rstd1, u2, a, df, dh2, acc_f = _mid_fwd(mixed, x2d, tgt, vec_f, w_out_b, w1_b, w2_b)

    vec_b = jnp.concatenate([gate1, 1.0 + scale2, ln1_w, ln1_b, jnp.zeros((4, D_MODEL), F32)], axis=0)
    da, dm, dmix, dxa, acc_b = _ffn_bwd(df, a, dh2, x1n, rstd1, m, vec_b, w_out_b, w1_b, w2_b)
    dw2 = _matmul_tn(a, df, 2048, 1024, 2048, "tn_dw2", relu_sq=True)
    dw1 = _matmul_tn(u2, da, 1024, 2048, 2048, "tn_dw1", col_slab=FF_COLS)
    dwo = _matmul_tn(mixed, dm, 1024, 1024, 2048, "tn_dwout")
    gx = _exchange_start([dwo.reshape(N_DEV, OUT_ROWS, D_MODEL), dw1, dw2.reshape(N_DEV, FF_COLS, D_MODEL)], False,
                         "gradx_start")
    dproj, dgw, dvec = _mixer_bwd(dmix, proj, qrb, krb, oraw, tables, rst, sst, gw_pad,
                                  gla_gate_b + gx[4][0, 0], ret_norm_w, gla_norm_w)
    dwi_s = _matmul_tn(dproj, u, D_IN_PAD, 1024, 1024, "tn_dwin", out_rows=IN_COLS)
    gi = _exchange_start([dwi_s], False, "gradin_start")
    grad_x, acc_i = _inproj_bwd(dproj, x2d, dxa, sc1p + gi[4][0, 0], w_in_t)

    loss_part = jnp.sum(acc_f[3])
    small = jnp.concatenate([
        acc_i[1:2], acc_i[0:1], acc_b[4:5], acc_b[1:2], acc_b[0:1], acc_f[2:3],
        acc_b[2:3], acc_b[3:4], acc_f[0:1], acc_f[1:2],
        jnp.concatenate([dvec[0:1], dvec[1:2]], axis=1),
        jnp.concatenate([dvec[2:3, :GLA_KW], jnp.full((1, 128), loss_part, F32),
                         jnp.zeros((1, D_MODEL - GLA_KW - 128), F32)], axis=1),
        jnp.zeros((4, D_MODEL), F32)], axis=0)
    sg = _exchange_start([small, dgw[:GATE_RANK]], True, "small_start")

    r_wo, r_w1, r_w2 = _exchange_wait(*gx[:4], sg[4], False, "gradx_wait")
    r_wi, = _exchange_wait(*gi[:4], sg[4], False, "gradin_wait")
    big = [_adamw(w[0], r, m_[0], v_[0], nm) for w, r, m_, v_, nm in (
        (w_out, r_wo, m_w_out, v_w_out, "adamw_out"),
        (w_ff1, r_w1, m_w_ff1, v_w_ff1, "adamw_ff1"), (w_ff2, r_w2, m_w_ff2, v_w_ff2, "adamw_ff2"))]
    big_in = _adamw(w_in[0].T, r_wi, m_w_in[0].T, v_w_in[0].T, "adamw_in")
    big = [tuple(b.T for b in big_in)] + big
    g_big, d_big, m_big, v_big = [[b[i][None] for b in big] for i in range(4)]

    small_all, gw_all = _exchange_wait(*sg[:4], big_in[1], True, "small_wait")
    dmod_all = small_all[:, :6].reshape(N_DEV, 6 * D_MODEL)
    dmod_cols = lax.dynamic_slice(dmod_all, (0, me * ADA_COLS), (N_DEV, ADA_COLS))
    ssum, gw_sum, g_b_ada, g_w_ada = _small_reduce(small_all, gw_all, c_all, dmod_cols)
    loss = ssum[SMR_MISC, GLA_KW]
    g_ggw = lax.dynamic_slice(gw_sum, (0, me * (GLA_KW // N_DEV)), (GATE_RANK, GLA_KW // N_DEV))[None]

    small_w = [b_ada, ret_norm_w, gla_gate_b, gla_norm_w, ln1_w, ln1_b, ln2_w, ln2_b, gla_gate_w]
    small_m = [m_b_ada, m_ret_norm_w, m_gla_gate_b, m_gla_norm_w, m_ln1_w, m_ln1_b, m_ln2_w, m_ln2_b, m_gla_gate_w]
    small_v = [v_b_ada, v_ret_norm_w, v_gla_gate_b, v_gla_norm_w, v_ln1_w, v_ln1_b, v_ln2_w, v_ln2_b, v_gla_gate_w]
    res = _adamw_small(ssum, g_b_ada, g_ggw, small_w, small_m, small_v)
    small_g = list(res[:8]) + [g_ggw]
    d_small, m_small, v_small = list(res[8:17]), list(res[17:26]), list(res[26:35])

    _, d_w_ada, nm_w_ada, nv_w_ada = _adamw(w_ada[0], g_w_ada[None], m_w_ada[0], v_w_ada[0], "adamw_ada")

    def ordered(w_ada_v, small_vals, big_vals):
        b_ada_v, rnw_v, ggb_v, gnw_v, l1w_v, l1b_v, l2w_v, l2b_v, ggw_v = small_vals
        wi_v, wo_v, w1_v, w2_v = big_vals
        return [w_ada_v, b_ada_v, wi_v, rnw_v, ggw_v, ggb_v, gnw_v, wo_v, l1w_v, l1b_v, w1_v, w2_v, l2w_v, l2b_v]

    grads = ordered(g_w_ada[None], small_g, g_big)
    deltas = ordered(d_w_ada[None], d_small, d_big)
    new_m = ordered(nm_w_ada[None], m_small, m_big)
    new_v = ordered(nv_w_ada[None], v_small, v_big)
    return (loss, grad_x[None], *grads, *deltas, *new_m, *new_v)
```

```python
import numpy as np
import jax
import jax.numpy as jnp
from jax import lax
from jax.experimental import pallas as pl
from jax.experimental.pallas import tpu as pltpu

F32 = jnp.float32
BF16 = jnp.bfloat16
MESH = pl.DeviceIdType.MESH
HIGHEST = lax.Precision.HIGHEST

N_DEV = 8
D_MODEL = 1024
CHUNK = 64
RET_HEADS = 4
RET_D = 128
GLA_HEADS = 4
GLA_DK = 64
GLA_DV = 128
GLA_KW = GLA_HEADS * GLA_DK
RET_W = RET_HEADS * RET_D
GLA_VW = GLA_HEADS * GLA_DV
V7X_LANES = 128
GATE_RANK = 16
GATE_TAU = 16.0
D_FF = 4096
LN_EPS = 1e-5
ALPHA = (2.0 * 1) ** 0.25
D_IN = 3600
D_IN_PAD = 3712
ADA_COLS = 6 * D_MODEL // N_DEV
IN_COLS = D_IN // N_DEV
FF_COLS = D_FF // N_DEV
OUT_ROWS = D_MODEL // N_DEV

OFF_RQ, OFF_RK, OFF_RV, OFF_RG = 0, RET_W, 2 * RET_W, 3 * RET_W
OFF_GQ = 4 * RET_W
OFF_GK = OFF_GQ + GLA_KW
OFF_GV = OFF_GK + GLA_KW
OFF_GG = OFF_GV + GLA_VW
OFF_GLR = OFF_GG + GLA_VW

ADAM_LR, ADAM_B1, ADAM_B2, ADAM_EPS, ADAM_WD, ADAM_STEP = 0.001, 0.9, 0.999, 1e-08, 0.01, 10

V7X_VMEM_LIMIT = 62 * 1024 * 1024

ROW_TILE = 512
PROJ_TILE = 512
MIX_TILE = 512
RET_SUB = 256
GLA_SUB = 128


def _log_gamma(h):
    return float(np.log(np.float32(1.0) - np.float32(2.0) ** np.float32(-5.0 - h)))


def _my_coords():
    return lax.axis_index("x"), lax.axis_index("y"), lax.axis_index("c")


def _flip(v, bit):
    return 1 - v if bit else v


def _peer(k):
    x, y, c = _my_coords()
    px, py, pc = _flip(x, (k >> 2) & 1), _flip(y, (k >> 1) & 1), _flip(c, k & 1)
    return (px, py, pc), 4 * px + 2 * py + pc


def _dot(a, b, dims=(((1,), (0,)), ((), ())), precision=None):
    return lax.dot_general(a, b, dims, precision=precision, preferred_element_type=F32)


NN = (((1,), (0,)), ((), ()))
NT = (((1,), (1,)), ((), ()))
TN = (((0,), (0,)), ((), ()))


def _split_bf16(v, parts):
    out = []
    for _ in range(parts):
        p = v.astype(BF16)
        out.append(p)
        v = v - p.astype(F32)
    return out


def _dot_split(a, b, dims, a_exact=False):
    if a_exact:
        ab = a.astype(BF16)
        return sum(_dot(ab, p, dims) for p in _split_bf16(b, 2))
    a_hi, a_lo = _split_bf16(a, 2)
    b_hi, b_lo = _split_bf16(b, 2)
    return _dot(a_hi, b_hi, dims) + _dot(a_hi, b_lo, dims) + _dot(a_lo, b_hi, dims)


def _sigmoid(x):
    return 1.0 / (1.0 + jnp.exp(-x))


def _ln_stats(x):
    mu = jnp.mean(x, axis=-1, keepdims=True)
    xc = x - mu
    var = jnp.mean(xc * xc, axis=-1, keepdims=True)
    rstd = lax.rsqrt(var + LN_EPS)
    return xc * rstd, rstd


def _ln_bwd(dyh, xh, rstd):
    return rstd * (dyh - jnp.mean(dyh, axis=-1, keepdims=True) - xh * jnp.mean(dyh * xh, axis=-1, keepdims=True))


def _adaln_mod(c_ext, w_ada_l, b_l, w_in_l):
    width = c_ext.shape[1]

    def body(c_ref, w_ref, b_ref, wi_ref, call_ref, mod_ref, wig_ref, token_ref, s1, r1, s2, r2, gs, gr, gl):
        gather = _TwoLevelGather([wi_ref], [wig_ref], gs, gr, gl)
        gather.start()
        token_ref[...] = jnp.zeros_like(token_ref)
        x, y, c = _my_coords()
        me = 4 * x + 2 * y + c
        call_ref[me] = c_ref[...]
        sends = []
        for k in range(1, N_DEV):
            peer, _ = _peer(k)
            cp = pltpu.make_async_remote_copy(c_ref, call_ref.at[me], s1.at[k - 1], r1.at[k - 1],
                                              device_id=peer, device_id_type=MESH)
            cp.start()
            sends.append(cp)
        for k in range(1, N_DEV):
            peer, pid = _peer(k)
            pltpu.make_async_remote_copy(c_ref, call_ref.at[pid], s1.at[k - 1], r1.at[k - 1],
                                         device_id=peer, device_id_type=MESH).wait_recv()
        for cp in sends:
            cp.wait_send()
        row = lax.broadcasted_iota(jnp.int32, (N_DEV, D_MODEL), 0)
        call = jnp.zeros((N_DEV, D_MODEL), F32)
        for j in range(N_DEV):
            call = jnp.where(row == j, jnp.broadcast_to(call_ref[j][:, :D_MODEL], (N_DEV, D_MODEL)), call)
        sc = call * _sigmoid(call)
        mod = _dot(sc, w_ref[...], NN, HIGHEST) + b_ref[...]
        mod_ref[me] = mod
        sends = []
        for k in range(1, N_DEV):
            peer, _ = _peer(k)
            cp = pltpu.make_async_remote_copy(mod_ref.at[me], mod_ref.at[me], s2.at[k - 1], r2.at[k - 1],
                                              device_id=peer, device_id_type=MESH)
            cp.start()
            sends.append(cp)
        for k in range(1, N_DEV):
            peer, pid = _peer(k)
            pltpu.make_async_remote_copy(mod_ref.at[pid], mod_ref.at[pid], s2.at[k - 1], r2.at[k - 1],
                                         device_id=peer, device_id_type=MESH).wait_recv()
        for cp in sends:
            cp.wait_send()
        gather.forward()
        gather.finish()

    vm = pl.BlockSpec(memory_space=pltpu.VMEM)
    hbm = pl.BlockSpec(memory_space=pl.ANY)
    return pl.pallas_call(
        body, name="adaln_mod",
        out_shape=(jax.ShapeDtypeStruct((N_DEV, 1, width), F32),
                   jax.ShapeDtypeStruct((N_DEV, N_DEV, ADA_COLS), F32),
                   jax.ShapeDtypeStruct((N_DEV, *w_in_l.shape), w_in_l.dtype),
                   jax.ShapeDtypeStruct((8, 128), F32)),
        in_specs=[vm, vm, vm, hbm], out_specs=(vm, vm, hbm, vm),
        scratch_shapes=[pltpu.SemaphoreType.DMA((N_DEV - 1,))] * 4
        + [pltpu.SemaphoreType.DMA((7,)), pltpu.SemaphoreType.DMA((7,)), pltpu.SemaphoreType.DMA((1,))],
        compiler_params=pltpu.CompilerParams(vmem_limit_bytes=V7X_VMEM_LIMIT),
    )(c_ext, w_ada_l, b_l, w_in_l)


class _TwoLevelGather:
    def __init__(self, x_refs, out_refs, send_sems, recv_sems, local_sems):
        self.x_refs, self.out_refs = x_refs, out_refs
        self.send_sems, self.recv_sems, self.local_sems = send_sems, recv_sems, local_sems
        x, y, c = _my_coords()
        self.c = c
        self.me, self.sibling = (x, y, c), (x, y, 1 - c)
        self.chips = [(1 - x, y), (x, 1 - y), (1 - x, 1 - y)]

    def _copy(self, a, k, block, to, src=None):
        px, py, pc = block
        slab = self.out_refs[a].at[4 * px + 2 * py + pc]
        return pltpu.make_async_remote_copy(
            src_ref=slab if src is None else src, dst_ref=slab,
            send_sem=self.send_sems.at[7 * a + k], recv_sem=self.recv_sems.at[7 * a + k],
            device_id=to, device_id_type=MESH)

    def _mine(self, a):
        px, py, pc = self.me
        return pltpu.make_async_copy(self.x_refs[a], self.out_refs[a].at[4 * px + 2 * py + pc], self.local_sems.at[a])

    def _first(self, a):
        cps = [self._copy(a, 0, self.me, self.sibling, src=self.x_refs[a])]
        cps += [self._copy(a, 1 + j, self.me, (*chip, self.c), src=self.x_refs[a]) for j, chip in enumerate(self.chips)]
        return cps

    def _passed(self, a):
        return [self._copy(a, 4 + j, (*chip, self.c), self.sibling) for j, chip in enumerate(self.chips)]

    def start(self):
        for a in range(len(self.x_refs)):
            self._mine(a).start()
            for cp in self._first(a):
                cp.start()

    def forward(self):
        for a in range(len(self.x_refs)):
            passed = self._passed(a)
            for j, chip in enumerate(self.chips):
                self._copy(a, 1 + j, (*chip, self.c), self.me).wait_recv()
                passed[j].start()

    def finish(self):
        for a in range(len(self.x_refs)):
            self._copy(a, 0, self.sibling, self.me).wait_recv()
            for j, chip in enumerate(self.chips):
                self._copy(a, 4 + j, (*chip, 1 - self.c), self.me).wait_recv()
            for cp in self._first(a) + self._passed(a):
                cp.wait_send()
            self._mine(a).wait()


def _exchange_copy(src_refs, land_refs, send_sems, recv_sems, a, k, gather, receiving):
    x, y, c = _my_coords()
    me = 4 * x + 2 * y + c
    peer, pid = _peer(k)
    src = src_refs[a] if gather else src_refs[a].at[pid]
    dst = land_refs[a].at[pid if receiving else me]
    return pltpu.make_async_remote_copy(src, dst, send_sems.at[7 * a + k - 1], recv_sems.at[7 * a + k - 1],
                                        device_id=peer, device_id_type=MESH)


def _own_copy(src_refs, land_refs, send_sems, a, n, gather):
    x, y, c = _my_coords()
    me = 4 * x + 2 * y + c
    src = src_refs[a] if gather else src_refs[a].at[me]
    return pltpu.make_async_copy(src, land_refs[a].at[me], send_sems.at[7 * n + a])


def _exchange_start(srcs, gather, name, after=None):
    n = len(srcs)
    land_shapes = [(N_DEV, *s.shape) if gather else s.shape for s in srcs]
    n_in = n if after is None else n + 1

    def body(*refs):
        src_refs, send_sems, recv_sems, token = refs[:n], refs[n_in], refs[n_in + 1], refs[-1]
        land_refs = refs[n_in + 2 + n:n_in + 2 + 2 * n]
        for a in range(n):
            _own_copy(src_refs, land_refs, send_sems, a, n, gather).start()
            for k in range(1, N_DEV):
                _exchange_copy(src_refs, land_refs, send_sems, recv_sems, a, k, gather, receiving=False).start()
        token[...] = jnp.zeros_like(token)

    hbm = pl.BlockSpec(memory_space=pltpu.HBM)
    sem = pl.BlockSpec(memory_space=pltpu.SEMAPHORE)
    res = pl.pallas_call(
        body, name=name,
        out_shape=(pltpu.SemaphoreType.DMA((8 * n,)), pltpu.SemaphoreType.DMA((7 * n,)),
                   *[pltpu.HBM(v.shape, v.dtype) for v in srcs],
                   *[pltpu.HBM(shape, v.dtype) for shape, v in zip(land_shapes, srcs)],
                   jax.ShapeDtypeStruct((8, 128), F32)),
        in_specs=[hbm] * n + [pl.BlockSpec(memory_space=pl.ANY)] * (n_in - n),
        out_specs=(sem, sem, *([hbm] * (2 * n)), pl.BlockSpec(memory_space=pltpu.VMEM)),
        input_output_aliases={i: 2 + i for i in range(n)},
        compiler_params=pltpu.CompilerParams(has_side_effects=pltpu.SideEffectType.DATAFLOW_SIDE_EFFECTING),
    )(*[pltpu.with_memory_space_constraint(v, pltpu.HBM) for v in srcs], *([] if after is None else [after]))
    return res[0], res[1], list(res[2:2 + n]), list(res[2 + n:2 + 2 * n]), res[-1]


def _exchange_wait(send_sems, recv_sems, srcs, lands, after, gather, name):
    n = len(srcs)

    def body(*refs):
        src_refs, land_refs, s_sems, r_sems = refs[:n], refs[n:2 * n], refs[2 * n], refs[2 * n + 1]
        for a in range(n):
            _own_copy(src_refs, land_refs, s_sems, a, n, gather).wait()
            for k in range(1, N_DEV):
                _exchange_copy(src_refs, land_refs, s_sems, r_sems, a, k, gather, receiving=False).wait_send()
                _exchange_copy(src_refs, land_refs, s_sems, r_sems, a, k, gather, receiving=True).wait_recv()

    hbm = pl.BlockSpec(memory_space=pltpu.HBM)
    sem = pl.BlockSpec(memory_space=pltpu.SEMAPHORE)
    res = pl.pallas_call(
        body, name=name,
        out_shape=tuple(pltpu.HBM(v.shape, v.dtype) for v in srcs + lands),
        in_specs=[hbm] * (2 * n) + [sem, sem, pl.BlockSpec(memory_space=pl.ANY)],
        out_specs=tuple([hbm] * (2 * n)),
        input_output_aliases={i: i for i in range(2 * n)},
        compiler_params=pltpu.CompilerParams(has_side_effects=pltpu.SideEffectType.DATAFLOW_SIDE_EFFECTING),
    )(*srcs, *lands, send_sems, recv_sems, after)
    return list(res[n:])


def _load_resident(step_is_first, pairs, sem):
    @pl.when(step_is_first)
    def _():
        copies = [pltpu.make_async_copy(src, dst, sem.at[i]) for i, (src, dst) in enumerate(pairs)]
        for cp in copies:
            cp.start()
        for cp in copies:
            cp.wait()


def _load_w_in_t(step_is_first, w_hbm, w_vmem, sem):
    @pl.when(step_is_first)
    def _():
        w_vmem[D_IN:, :] = jnp.zeros((D_IN_PAD - D_IN, D_MODEL), BF16)
    _load_resident(step_is_first, [(w_hbm, w_vmem.at[pl.ds(0, D_IN)])], sem)


def _inproj_fwd(x2d, sc1p, sh1, w_in_t, after):
    t = x2d.shape[0]
    tm = min(PROJ_TILE, t)

    def body(x_ref, sc_ref, sh_ref, w_hbm, after_ref, proj_ref, u_ref, w_vmem, sem):
        _load_w_in_t(pl.program_id(0) == 0, w_hbm, w_vmem, sem)
        xh, _ = _ln_stats(x_ref[...])
        ub = (xh * sc_ref[...] + sh_ref[...]).astype(BF16)
        u_ref[...] = ub
        proj_ref[...] = _dot(ub, w_vmem[...], NT)

    row = lambda i: (i, 0)
    fix = lambda i: (0, 0)
    return pl.pallas_call(
        body, name="inproj_fwd", grid=(t // tm,),
        in_specs=[pl.BlockSpec((tm, D_MODEL), row), pl.BlockSpec((1, D_MODEL), fix), pl.BlockSpec((1, D_MODEL), fix),
                  pl.BlockSpec(memory_space=pl.ANY), pl.BlockSpec(memory_space=pl.ANY)],
        out_specs=(pl.BlockSpec((tm, D_IN_PAD), row), pl.BlockSpec((tm, D_MODEL), row)),
        out_shape=(jax.ShapeDtypeStruct((t, D_IN_PAD), F32), jax.ShapeDtypeStruct((t, D_MODEL), BF16)),
        scratch_shapes=[pltpu.VMEM((D_IN_PAD, D_MODEL), BF16), pltpu.SemaphoreType.DMA((1,))],
        compiler_params=pltpu.CompilerParams(dimension_semantics=("arbitrary",), vmem_limit_bytes=V7X_VMEM_LIMIT),
    )(x2d, sc1p, sh1, w_in_t, after)


CHUNK_SHIFT = CHUNK.bit_length() - 1


def _ret_tables(t, tl):
    r = lax.broadcasted_iota(jnp.int32, (tl, tl), 0)
    c = lax.broadcasted_iota(jnp.int32, (tl, tl), 1)
    allowed = jnp.right_shift(c, CHUNK_SHIFT) <= jnp.right_shift(r, CHUNK_SHIFT)
    dist = jnp.abs(r - c).astype(F32)
    rowf = lax.broadcasted_iota(jnp.int32, (tl, RET_D), 0).astype(F32)
    lgs = [_log_gamma(h) for h in range(RET_HEADS)]
    dec = jnp.stack([jnp.where(allowed, jnp.exp(lg * dist), 0.0) for lg in lgs])
    qkd = jnp.stack([jnp.exp(lg * (rowf + 1.0)) for lg in lgs] + [jnp.exp(lg * (tl - 1.0 - rowf)) for lg in lgs])
    inv = 1.0 / (10000.0 ** jnp.linspace(0.0, 1.0, RET_D // 2, dtype=F32))
    off = jnp.arange(tl, dtype=F32)[:, None] * inv[None, :]
    start = (jnp.arange(t // tl, dtype=F32) * tl)[:, None] * inv[None, :]
    co, so = jnp.cos(off), jnp.sin(off)
    rot_in = jnp.stack([jnp.concatenate([co, co], 1), jnp.concatenate([so, so], 1),
                        jnp.concatenate([-co, co], 1), jnp.concatenate([-so, so], 1)])
    cs, ss = jnp.cos(start), jnp.sin(start)
    rot_tile = jnp.concatenate([cs, cs, ss, ss], axis=1)
    rot_tile = jnp.broadcast_to(rot_tile[:, None, :], (t // tl, 8, 2 * RET_D))
    return dec, qkd, rot_in, rot_tile


def _tile_gammas(tl):
    return [float(np.exp(np.float32(_log_gamma(h)) * np.float32(tl))) for h in range(RET_HEADS)]


def _tile_rotary(rot_in_ref, rot_tile_ref, j):
    ca, sa = rot_tile_ref[j, 0:1, 0:RET_D], rot_tile_ref[j, 0:1, RET_D:2 * RET_D]
    cosv = ca * rot_in_ref[0] - sa * rot_in_ref[1]
    sinv = sa * rot_in_ref[2] + ca * rot_in_ref[3]
    return cosv, sinv


def _gla_consts(tl):
    r = lax.broadcasted_iota(jnp.int32, (tl, tl), 0)
    c = lax.broadcasted_iota(jnp.int32, (tl, tl), 1)
    ltri = (c <= r).astype(F32)
    utri = (c >= r).astype(F32)
    lane = lax.broadcasted_iota(jnp.int32, (1, GLA_KW), 1)
    hmask = [((lane >= h * GLA_DK) & (lane < (h + 1) * GLA_DK)).astype(F32) for h in range(GLA_HEADS)]
    rs = lax.broadcasted_iota(jnp.int32, (GLA_HEADS * tl, tl), 0) & (tl - 1)
    cs = lax.broadcasted_iota(jnp.int32, (GLA_HEADS * tl, tl), 1)
    lower = cs <= rs
    same = jnp.right_shift(cs, CHUNK_SHIFT) == jnp.right_shift(rs, CHUNK_SHIFT)
    upper = jnp.logical_and(jnp.logical_not(lower), same)
    return dict(ltri=ltri, utri=utri, hmask=hmask, lower=lower, upper=upper)


def _tile_rows(j, tl):
    return pl.ds(j * tl, tl) if isinstance(j, int) else pl.ds(pl.multiple_of(j * tl, tl), tl)


def _for_tiles(cps, fn):
    for j in range(cps):
        fn(j, 0)


def _rotate(v, cosv, sinv):
    return v * cosv + pltpu.roll(v, RET_D // 2, 1) * sinv


def _rotate_t(d, cosv, sinv):
    return d * cosv + pltpu.roll(d * sinv, RET_D // 2, 1)


def _stack_heads(v, hmask):
    return jnp.concatenate([v * hmask[h] for h in range(GLA_HEADS)], axis=0)


def _gla_gates(glr, gw, gb, ltri, tl):
    z = _dot_split(glr, gw, NN) + gb
    la = (jnp.minimum(z, 0.0) - jnp.log(1.0 + jnp.exp(-jnp.abs(z)))) * (1.0 / GATE_TAU)
    b = _dot_split(ltri, la, NN, a_exact=True)
    level = b[tl // 2 - 1:tl // 2, :]
    ep = jnp.exp(jnp.clip(b - level, -80.0, 80.0))
    em = jnp.exp(jnp.clip(level - b, -80.0, 80.0))
    bl = b[tl - 1:tl, :]
    return z, b, bl, ep, em


def _mixer_fwd(proj, tables, gw_pad, gb, rnw, gnw):
    t = proj.shape[0]
    tc = min(MIX_TILE, t)
    tr, tg = min(RET_SUB, tc), min(GLA_SUB, tc)
    nsteps = t // tc
    scale_r = RET_D ** -0.5
    scale_g = GLA_DK ** -0.5
    gammas = _tile_gammas(tr)

    def body(rq_ref, rk_ref, rv_ref, rg_ref, gq_ref, gk_ref, gv_ref, gg_ref, glr_ref,
             dec_ref, qkd_ref, rot_in_ref, rot_tile_ref, gw_ref, gb_ref, rnw_ref, gnw_ref,
             mix_ref, oraw_ref, qrb_ref, krb_ref, rst_ref, sst_ref, r_scr, s_scr):
        @pl.when(pl.program_id(0) == 0)
        def _():
            r_scr[...] = jnp.zeros_like(r_scr)
            s_scr[...] = jnp.zeros_like(s_scr)

        gla_k = _gla_consts(tg)

        def ret_tile(j, carry):
            rows = _tile_rows(j, tr)
            cosv, sinv = _tile_rotary(rot_in_ref, rot_tile_ref, j)
            for h in range(RET_HEADS):
                cols = slice(h * RET_D, (h + 1) * RET_D)
                qr = _rotate(rq_ref[rows, cols], cosv, sinv) * scale_r
                kr = _rotate(rk_ref[rows, cols], cosv, sinv)
                vb = rv_ref[rows, cols].astype(BF16)
                qb, kb = qr.astype(BF16), kr.astype(BF16)
                qrb_ref[rows, cols] = qb
                krb_ref[rows, cols] = kb
                p = _dot(qb, kb, NT) * dec_ref[h]
                rp = r_scr[cols, :]
                o = _dot(p.astype(BF16), vb) + _dot((qr * qkd_ref[h]).astype(BF16), rp.astype(BF16))
                rst_ref[j, cols, :] = rp
                r_scr[cols, :] = gammas[h] * rp + _dot((kr * qkd_ref[RET_HEADS + h]).astype(BF16), vb, TN)
                oraw_ref[rows, cols] = o
                oc = o - jnp.mean(o, axis=-1, keepdims=True)
                n = oc * lax.rsqrt(jnp.mean(oc * oc, axis=-1, keepdims=True) + LN_EPS)
                g = rg_ref[rows, cols]
                mix_ref[rows, cols] = (n * rnw_ref[:, cols] * (g * _sigmoid(g))).astype(BF16)
            return carry

        def gla_tile(j, carry):
            k = gla_k
            tl = tg
            rows = _tile_rows(j, tg)
            _, b, bl, ep, em = _gla_gates(glr_ref[rows, :], gw_ref[...], gb_ref[...], k["ltri"], tl)
            qs = gq_ref[rows, :] * scale_g
            kk = gk_ref[rows, :]
            x_all = _dot(_stack_heads(qs * ep, k["hmask"]).astype(BF16), (kk * em).astype(BF16), NT)
            y_all = _dot(_stack_heads(qs * em, k["hmask"]).astype(BF16), (kk * ep).astype(BF16), NT)
            a_all = jnp.where(k["lower"], x_all, jnp.where(k["upper"], y_all, 0.0)).astype(BF16)
            st = s_scr[...]
            oq = _dot(_stack_heads(qs * jnp.exp(b), k["hmask"]).astype(BF16), st.astype(BF16), NT)
            kg = kk * jnp.exp(bl - b)
            sst_ref[j] = st
            st_new = st * jnp.exp(bl)
            for h in range(GLA_HEADS):
                cols = slice(h * GLA_DV, (h + 1) * GLA_DV)
                hr = slice(h * tl, (h + 1) * tl)
                vb = gv_ref[rows, cols].astype(BF16)
                o = _dot(a_all[hr, :], vb) + oq[hr, :]
                st_new = st_new + _dot(vb, (kg * k["hmask"][h]).astype(BF16), TN)
                ocols = slice(RET_W + h * GLA_DV, RET_W + (h + 1) * GLA_DV)
                oraw_ref[rows, ocols] = o
                n = o * lax.rsqrt(jnp.mean(o * o, axis=-1, keepdims=True) + LN_EPS)
                g = gg_ref[rows, cols]
                mix_ref[rows, ocols] = (n * gnw_ref[:, cols] * (g * _sigmoid(g))).astype(BF16)
            s_scr[...] = st_new
            return carry

        _for_tiles(tc // tr, ret_tile)
        _for_tiles(tc // tg, gla_tile)

    def col(width, off):
        return pl.BlockSpec((tc, width), lambda i, o=off // width: (i, o))

    fix = lambda i: (0, 0)
    fix3 = lambda i: (0, 0, 0)
    dec, qkd, rot_in, rot_tile = tables
    in_specs = [col(RET_W, OFF_RQ), col(RET_W, OFF_RK), col(RET_W, OFF_RV), col(RET_W, OFF_RG),
                col(GLA_KW, OFF_GQ), col(GLA_KW, OFF_GK), col(GLA_VW, OFF_GV), col(GLA_VW, OFF_GG),
                col(V7X_LANES, OFF_GLR),
                pl.BlockSpec(dec.shape, fix3), pl.BlockSpec(qkd.shape, fix3), pl.BlockSpec(rot_in.shape, fix3),
                pl.BlockSpec((tc // tr, 8, 2 * RET_D), lambda i: (i, 0, 0)),
                pl.BlockSpec((V7X_LANES, GLA_KW), fix), pl.BlockSpec((1, GLA_KW), fix),
                pl.BlockSpec((1, RET_W), fix), pl.BlockSpec((1, GLA_VW), fix)]
    half = pl.BlockSpec((tc, RET_W), lambda i: (i, 0))
    out_specs = (pl.BlockSpec((tc, D_MODEL), lambda i: (i, 0)), pl.BlockSpec((tc, D_MODEL), lambda i: (i, 0)),
                 half, half,
                 pl.BlockSpec((tc // tr, RET_W, RET_D), lambda i: (i, 0, 0)),
                 pl.BlockSpec((tc // tg, GLA_DV, GLA_KW), lambda i: (i, 0, 0)))
    out_shape = (jax.ShapeDtypeStruct((t, D_MODEL), BF16), jax.ShapeDtypeStruct((t, D_MODEL), F32),
                 jax.ShapeDtypeStruct((t, RET_W), BF16), jax.ShapeDtypeStruct((t, RET_W), BF16),
                 jax.ShapeDtypeStruct((t // tr, RET_W, RET_D), F32),
                 jax.ShapeDtypeStruct((t // tg, GLA_DV, GLA_KW), F32))
    return pl.pallas_call(
        body, name="mixer_fwd", grid=(nsteps,), in_specs=in_specs, out_specs=out_specs, out_shape=out_shape,
        scratch_shapes=[pltpu.VMEM((RET_W, RET_D), F32), pltpu.VMEM((GLA_DV, GLA_KW), F32)],
        compiler_params=pltpu.CompilerParams(dimension_semantics=("arbitrary",), vmem_limit_bytes=V7X_VMEM_LIMIT),
    )(*([proj] * 9), dec, qkd, rot_in, rot_tile, gw_pad, gb, rnw, gnw)


def _mid_fwd(mixed, x2d, target, vecs, w_out_b, w1_b, w2_b):
    t = x2d.shape[0]
    tm = min(ROW_TILE, t)

    def body(mix_ref, x_ref, tgt_ref, v_ref, wo_hbm, w1_hbm, w2_hbm,
             m_ref, x1n_ref, rstd_ref, u2_ref, a_ref, df_ref, dh2_ref, acc_ref, wo, w1, w2, sem):
        first = pl.program_id(0) == 0
        _load_resident(first, [(wo_hbm, wo), (w1_hbm, w1), (w2_hbm, w2)], sem)

        @pl.when(first)
        def _():
            acc_ref[...] = jnp.zeros_like(acc_ref)

        gate1, sc2p, sh2, gate2 = v_ref[0:1, :], v_ref[1:2, :], v_ref[2:3, :], v_ref[3:4, :]
        l1w, l1b, l2w, l2b = v_ref[4:5, :], v_ref[5:6, :], v_ref[6:7, :], v_ref[7:8, :]
        m = _dot(mix_ref[...], wo[...])
        m_ref[...] = m.astype(BF16)
        x1n, rstd1 = _ln_stats(ALPHA * x_ref[...] + gate1 * m)
        x1n_ref[...] = x1n
        rstd_ref[...] = rstd1
        x1 = x1n * l1w + l1b
        xh1, _ = _ln_stats(x1)
        u2 = (xh1 * sc2p + sh2).astype(BF16)
        u2_ref[...] = u2
        f = jnp.zeros((tm, D_MODEL), F32)
        for j in range(N_DEV):
            cols = slice(j * FF_COLS, (j + 1) * FF_COLS)
            a = _dot(u2, w1[j])
            a_ref[:, cols] = a.astype(BF16)
            r = jnp.maximum(a, 0.0)
            f = f + _dot((r * r).astype(BF16), w2[cols, :])
        yh, rstd2 = _ln_stats(ALPHA * x1 + gate2 * f)
        e = yh * l2w + l2b - tgt_ref[...]
        dy = e * (1.0 / D_MODEL)
        dh2 = _ln_bwd(dy * l2w, yh, rstd2)
        dh2_ref[...] = dh2
        df_ref[...] = (dh2 * gate2).astype(BF16)
        acc_ref[0:1, :] += jnp.sum(dy * yh, axis=0, keepdims=True)
        acc_ref[1:2, :] += jnp.sum(dy, axis=0, keepdims=True)
        acc_ref[2:3, :] += jnp.sum(dh2 * f, axis=0, keepdims=True)
        acc_ref[3:4, :] += jnp.sum(e * e, axis=0, keepdims=True) * (0.5 / D_MODEL)

    row = lambda i: (i, 0)
    fix = lambda i: (0, 0)
    hbm = pl.BlockSpec(memory_space=pl.ANY)
    return pl.pallas_call(
        body, name="mid_fwd", grid=(t // tm,),
        in_specs=[pl.BlockSpec((tm, D_MODEL), row), pl.BlockSpec((tm, D_MODEL), row), pl.BlockSpec((tm, D_MODEL), row),
                  pl.BlockSpec((8, D_MODEL), fix), hbm, hbm, hbm],
        out_specs=(pl.BlockSpec((tm, D_MODEL), row), pl.BlockSpec((tm, D_MODEL), row), pl.BlockSpec((tm, 1), row),
                   pl.BlockSpec((tm, D_MODEL), row), pl.BlockSpec((tm, D_FF), row), pl.BlockSpec((tm, D_MODEL), row),
                   pl.BlockSpec((tm, D_MODEL), row), pl.BlockSpec((8, D_MODEL), fix)),
        out_shape=(jax.ShapeDtypeStruct((t, D_MODEL), BF16), jax.ShapeDtypeStruct((t, D_MODEL), F32),
                   jax.ShapeDtypeStruct((t, 1), F32), jax.ShapeDtypeStruct((t, D_MODEL), BF16),
                   jax.ShapeDtypeStruct((t, D_FF), BF16), jax.ShapeDtypeStruct((t, D_MODEL), BF16),
                   jax.ShapeDtypeStruct((t, D_MODEL), F32), jax.ShapeDtypeStruct((8, D_MODEL), F32)),
        scratch_shapes=[pltpu.VMEM((D_MODEL, D_MODEL), BF16), pltpu.VMEM((N_DEV, D_MODEL, FF_COLS), BF16),
                        pltpu.VMEM((D_FF, D_MODEL), BF16), pltpu.SemaphoreType.DMA((3,))],
        compiler_params=pltpu.CompilerParams(dimension_semantics=("arbitrary",), vmem_limit_bytes=V7X_VMEM_LIMIT),
    )(mixed, x2d, target, vecs, w_out_b, w1_b, w2_b)


def _ffn_bwd(df, a, dh2, x1n, rstd1, m, vecs, w_out_b, w1_b, w2_b):
    t = x1n.shape[0]
    tm = min(ROW_TILE, t)

    def body(df_ref, a_ref, dh2_ref, x1n_ref, rstd_ref, m_ref, v_ref, wo_hbm, w1_hbm, w2_hbm,
             da_ref, dm_ref, dmix_ref, dxa_ref, acc_ref, wo, w1, w2, sem):
        first = pl.program_id(0) == 0
        _load_resident(first, [(wo_hbm, wo), (w1_hbm, w1), (w2_hbm, w2)], sem)

        @pl.when(first)
        def _():
            acc_ref[...] = jnp.zeros_like(acc_ref)

        gate1, sc2p, l1w, l1b = v_ref[0:1, :], v_ref[1:2, :], v_ref[2:3, :], v_ref[3:4, :]
        df = df_ref[...]
        du2 = jnp.zeros((tm, D_MODEL), F32)
        for j in range(N_DEV):
            cols = slice(j * FF_COLS, (j + 1) * FF_COLS)
            dr2 = _dot(df, w2[cols, :], NT)
            da = (dr2 * (2.0 * jnp.maximum(a_ref[:, cols].astype(F32), 0.0))).astype(BF16)
            da_ref[:, cols] = da
            du2 = du2 + _dot(da, w1[j], NT)
        x1n = x1n_ref[...]
        xh1, rstd0 = _ln_stats(x1n * l1w + l1b)
        dx1 = ALPHA * dh2_ref[...] + _ln_bwd(du2 * sc2p, xh1, rstd0)
        dh1 = _ln_bwd(dx1 * l1w, x1n, rstd_ref[...])
        dxa_ref[...] = ALPHA * dh1
        dm = (dh1 * gate1).astype(BF16)
        dm_ref[...] = dm
        dmix_ref[...] = _dot(dm, wo[...], NT)
        acc_ref[0:1, :] += jnp.sum(du2 * xh1, axis=0, keepdims=True)
        acc_ref[1:2, :] += jnp.sum(du2, axis=0, keepdims=True)
        acc_ref[2:3, :] += jnp.sum(dx1 * x1n, axis=0, keepdims=True)
        acc_ref[3:4, :] += jnp.sum(dx1, axis=0, keepdims=True)
        acc_ref[4:5, :] += jnp.sum(dh1 * m_ref[...].astype(F32), axis=0, keepdims=True)

    row = lambda i: (i, 0)
    fix = lambda i: (0, 0)
    hbm = pl.BlockSpec(memory_space=pl.ANY)
    return pl.pallas_call(
        body, name="ffn_bwd", grid=(t // tm,),
        in_specs=[pl.BlockSpec((tm, D_MODEL), row), pl.BlockSpec((tm, D_FF), row), pl.BlockSpec((tm, D_MODEL), row),
                  pl.BlockSpec((tm, D_MODEL), row), pl.BlockSpec((tm, 1), row), pl.BlockSpec((tm, D_MODEL), row),
                  pl.BlockSpec((8, D_MODEL), fix), hbm, hbm, hbm],
        out_specs=(pl.BlockSpec((tm, D_FF), row), pl.BlockSpec((tm, D_MODEL), row), pl.BlockSpec((tm, D_MODEL), row),
                   pl.BlockSpec((tm, D_MODEL), row), pl.BlockSpec((8, D_MODEL), fix)),
        out_shape=(jax.ShapeDtypeStruct((t, D_FF), BF16), jax.ShapeDtypeStruct((t, D_MODEL), BF16),
                   jax.ShapeDtypeStruct((t, D_MODEL), F32), jax.ShapeDtypeStruct((t, D_MODEL), F32),
                   jax.ShapeDtypeStruct((8, D_MODEL), F32)),
        scratch_shapes=[pltpu.VMEM((D_MODEL, D_MODEL), BF16), pltpu.VMEM((N_DEV, D_MODEL, FF_COLS), BF16),
                        pltpu.VMEM((D_FF, D_MODEL), BF16), pltpu.SemaphoreType.DMA((3,))],
        compiler_params=pltpu.CompilerParams(dimension_semantics=("arbitrary",), vmem_limit_bytes=V7X_VMEM_LIMIT),
    )(df, a, dh2, x1n, rstd1, m, vecs, w_out_b, w1_b, w2_b)


def _matmul_tn(lhs, rhs, tmm, tn, tk, name, relu_sq=False, col_slab=None, out_rows=None):
    t, mm = lhs.shape
    assert out_rows is None or (col_slab is None and tmm == mm)
    nn = rhs.shape[1]
    tk = min(tk, t)
    nk = t // tk

    def body(l_ref, r_ref, o_ref, acc):
        kk = pl.program_id(2)

        @pl.when(kk == 0)
        def _():
            acc[...] = jnp.zeros_like(acc)

        l = l_ref[...]
        if relu_sq:
            lf = jnp.maximum(l.astype(F32), 0.0)
            l = (lf * lf).astype(BF16)
        acc[...] += _dot(l, r_ref[...], TN)

        @pl.when(kk == nk - 1)
        def _():
            if out_rows is not None:
                for s in range(N_DEV):
                    o_ref[s] = acc[s * out_rows:(s + 1) * out_rows, :].astype(o_ref.dtype)
            elif col_slab is None:
                o_ref[...] = acc[...].astype(o_ref.dtype)
            else:
                for s in range(tn // col_slab):
                    o_ref[s] = acc[:, s * col_slab:(s + 1) * col_slab].astype(o_ref.dtype)

    if out_rows is not None:
        out_spec = pl.BlockSpec((N_DEV, out_rows, tn), lambda i, j, k: (0, 0, j))
        out_shape = jax.ShapeDtypeStruct((N_DEV, out_rows, nn), BF16)
    elif col_slab is None:
        out_spec = pl.BlockSpec((tmm, tn), lambda i, j, k: (i, j))
        out_shape = jax.ShapeDtypeStruct((mm, nn), BF16)
    else:
        out_spec = pl.BlockSpec((tn // col_slab, tmm, col_slab), lambda i, j, k: (j, i, 0))
        out_shape = jax.ShapeDtypeStruct((nn // col_slab, mm, col_slab), BF16)
    return pl.pallas_call(
        body, name=name, grid=(mm // tmm, nn // tn, nk),
        in_specs=[pl.BlockSpec((tk, tmm), lambda i, j, k: (k, i)), pl.BlockSpec((tk, tn), lambda i, j, k: (k, j))],
        out_specs=out_spec,
        out_shape=out_shape,
        scratch_shapes=[pltpu.VMEM((tmm, tn), F32)],
        compiler_params=pltpu.CompilerParams(dimension_semantics=("arbitrary", "arbitrary", "arbitrary"),
                                             vmem_limit_bytes=V7X_VMEM_LIMIT),
    )(lhs, rhs)


def _mixer_bwd(dmix, proj, qrb, krb, oraw, tables, rst, sst, gw_pad, gb, rnw, gnw, after):
    t = proj.shape[0]
    tc = min(MIX_TILE, t)
    tr, tg = min(RET_SUB, tc), min(GLA_SUB, tc)
    nsteps = t // tc
    scale_r = RET_D ** -0.5
    scale_g = GLA_DK ** -0.5
    gammas = _tile_gammas(tr)

    def body(dmix_ref, qrb_ref, krb_ref, rv_ref, rg_ref, gq_ref, gk_ref, gv_ref, gg_ref, glr_ref, oraw_ref,
             dec_ref, qkd_ref, rot_in_ref, rot_tile_ref, rst_ref, sst_ref, gw_ref, gb_ref, rnw_ref, gnw_ref, after_ref,
             dproj_ref, dgw_ref, dvec_ref, dr_scr, ds_scr):
        @pl.when(pl.program_id(0) == 0)
        def _():
            dr_scr[...] = jnp.zeros_like(dr_scr)
            ds_scr[...] = jnp.zeros_like(ds_scr)
            dgw_ref[...] = jnp.zeros_like(dgw_ref)
            dvec_ref[...] = jnp.zeros_like(dvec_ref)

        gla_k = _gla_consts(tg)
        last_row = lax.broadcasted_iota(jnp.int32, (tg, GLA_KW), 0) == tg - 1

        def ret_tile(jj, carry):
            j = tc // tr - 1 - jj
            rows = _tile_rows(j, tr)
            cosv, sinv = _tile_rotary(rot_in_ref, rot_tile_ref, j)
            for h in range(RET_HEADS):
                cols = slice(h * RET_D, (h + 1) * RET_D)
                o = oraw_ref[rows, cols]
                g = rg_ref[rows, cols]
                w = rnw_ref[:, cols]
                dout = dmix_ref[rows, cols]
                oc = o - jnp.mean(o, axis=-1, keepdims=True)
                inv = lax.rsqrt(jnp.mean(oc * oc, axis=-1, keepdims=True) + LN_EPS)
                n = oc * inv
                sg = _sigmoid(g)
                sil = g * sg
                dn = dout * w * sil
                dvec_ref[0:1, cols] += jnp.sum(dout * n * sil, axis=0, keepdims=True)
                dproj_ref[rows, OFF_RG + h * RET_D:OFF_RG + (h + 1) * RET_D] = (
                    dout * n * w * (sg * (1.0 + g * (1.0 - sg)))).astype(BF16)
                doc = inv * (dn - n * jnp.mean(dn * n, axis=-1, keepdims=True))
                do = doc - jnp.mean(doc, axis=-1, keepdims=True)

                qb, kb = qrb_ref[rows, cols], krb_ref[rows, cols]
                qr, kr = qb.astype(F32), kb.astype(F32)
                vb = rv_ref[rows, cols].astype(BF16)
                dob = do.astype(BF16)
                qd, kd = qkd_ref[h], qkd_ref[RET_HEADS + h]
                p = _dot(qb, kb, NT) * dec_ref[h]
                rp = rst_ref[j, cols, :].astype(BF16)
                dr = dr_scr[cols, :]
                drb = dr.astype(BF16)
                dpb = (_dot(dob, vb, NT) * dec_ref[h]).astype(BF16)
                dqr = _dot(dpb, kb) + _dot(dob, rp, NT) * qd
                dkr = _dot(dpb, qb, TN) + _dot(vb, drb, NT) * kd
                dv = _dot(p.astype(BF16), dob, TN) + _dot((kr * kd).astype(BF16), drb)
                dr_scr[cols, :] = gammas[h] * dr + _dot((qr * qd).astype(BF16), dob, TN)
                dproj_ref[rows, OFF_RQ + h * RET_D:OFF_RQ + (h + 1) * RET_D] = (
                    _rotate_t(dqr, cosv, sinv) * scale_r).astype(BF16)
                dproj_ref[rows, OFF_RK + h * RET_D:OFF_RK + (h + 1) * RET_D] = _rotate_t(dkr, cosv, sinv).astype(BF16)
                dproj_ref[rows, OFF_RV + h * RET_D:OFF_RV + (h + 1) * RET_D] = dv.astype(BF16)
            return carry

        def gla_tile(jj, carry):
            k = gla_k
            tl = tg
            j = tc // tg - 1 - jj
            rows = _tile_rows(j, tg)
            glr = glr_ref[rows, :]
            z, b, bl, ep, em = _gla_gates(glr, gw_ref[...], gb_ref[...], k["ltri"], tl)
            qs = gq_ref[rows, :] * scale_g
            kk = gk_ref[rows, :]
            eb = jnp.exp(b)
            ekb = jnp.exp(bl - b)
            ebl = jnp.exp(bl)
            ql, qu, kl, ku = qs * ep, qs * em, kk * em, kk * ep
            qg, kg = qs * eb, kk * ekb
            qlm = _stack_heads(ql, k["hmask"]).astype(BF16)
            qum = _stack_heads(qu, k["hmask"]).astype(BF16)
            klb, kub = kl.astype(BF16), ku.astype(BF16)
            a_all = jnp.where(k["lower"], _dot(qlm, klb, NT),
                              jnp.where(k["upper"], _dot(qum, kub, NT), 0.0)).astype(BF16)
            st = sst_ref[j]
            stb = st.astype(BF16)
            ds = ds_scr[...]
            dsb = ds.astype(BF16)
            ds_new = ds * ebl
            da_parts = []
            dqg = jnp.zeros((tl, GLA_KW), F32)
            dkg = jnp.zeros((tl, GLA_KW), F32)
            for h in range(GLA_HEADS):
                cols = slice(h * GLA_DV, (h + 1) * GLA_DV)
                hr = slice(h * tl, (h + 1) * tl)
                ocols = slice(RET_W + h * GLA_DV, RET_W + (h + 1) * GLA_DV)
                o = oraw_ref[rows, ocols]
                g = gg_ref[rows, cols]
                w = gnw_ref[:, cols]
                dout = dmix_ref[rows, ocols]
                inv = lax.rsqrt(jnp.mean(o * o, axis=-1, keepdims=True) + LN_EPS)
                n = o * inv
                sg = _sigmoid(g)
                sil = g * sg
                dn = dout * w * sil
                dvec_ref[1:2, cols] += jnp.sum(dout * n * sil, axis=0, keepdims=True)
                dproj_ref[rows, OFF_GG + h * GLA_DV:OFF_GG + (h + 1) * GLA_DV] = (
                    dout * n * w * (sg * (1.0 + g * (1.0 - sg)))).astype(BF16)
                dob = (inv * (dn - n * jnp.mean(dn * n, axis=-1, keepdims=True))).astype(BF16)
                vb = gv_ref[rows, cols].astype(BF16)
                mh = k["hmask"][h]
                da_parts.append(_dot(dob, vb, NT))
                dv = _dot(a_all[hr, :], dob, TN) + _dot((kg * mh).astype(BF16), dsb, NT)
                dproj_ref[rows, OFF_GV + h * GLA_DV:OFF_GV + (h + 1) * GLA_DV] = dv.astype(BF16)
                dkg = dkg + mh * _dot(vb, dsb)
                dqg = dqg + mh * _dot(dob, stb)
                ds_new = ds_new + _dot(dob, (qg * mh).astype(BF16), TN)
            da_all = jnp.concatenate(da_parts, axis=0)
            dal = jnp.where(k["lower"], da_all, 0.0).astype(BF16)
            dau = jnp.where(k["upper"], da_all, 0.0).astype(BF16)
            dqlm = _dot(dal, klb)
            dqum = _dot(dau, kub)
            dql = jnp.zeros((tl, GLA_KW), F32)
            dqu = jnp.zeros((tl, GLA_KW), F32)
            for h in range(GLA_HEADS):
                hr = slice(h * tl, (h + 1) * tl)
                dql = dql + k["hmask"][h] * dqlm[hr, :]
                dqu = dqu + k["hmask"][h] * dqum[hr, :]
            dkl = _dot(dal, qlm, TN)
            dku = _dot(dau, qum, TN)
            dbl = (jnp.sum(dkg * kg, axis=0, keepdims=True)
                   + jnp.sum(ds * st, axis=0, keepdims=True) * ebl)
            ds_scr[...] = ds_new
            dqs = dql * ep + dqu * em + dqg * eb
            dk = dkl * em + dku * ep + dkg * ekb
            db = dql * ql - dkl * kl - dqu * qu + dku * ku + dqg * qg - dkg * kg
            db = db + jnp.where(last_row, dbl, 0.0)
            dla = _dot_split(k["utri"], db, NN, a_exact=True)
            dz = dla * (1.0 / GATE_TAU) * _sigmoid(-z)
            dvec_ref[2:3, 0:GLA_KW] += jnp.sum(dz, axis=0, keepdims=True)
            dgw_ref[...] += _dot_split(glr, dz, TN)
            dproj_ref[rows, OFF_GLR:D_IN_PAD] = _dot(dz.astype(BF16), gw_ref[...].astype(BF16), NT).astype(BF16)
            dproj_ref[rows, OFF_GQ:OFF_GQ + GLA_KW] = (dqs * scale_g).astype(BF16)
            dproj_ref[rows, OFF_GK:OFF_GK + GLA_KW] = dk.astype(BF16)
            return carry

        _for_tiles(tc // tr, ret_tile)
        _for_tiles(tc // tg, gla_tile)

    rev = lambda i: (nsteps - 1 - i, 0)

    def col(width, off):
        return pl.BlockSpec((tc, width), lambda i, o=off // width: (nsteps - 1 - i, o))

    fix = lambda i: (0, 0)
    fix3 = lambda i: (0, 0, 0)
    dec, qkd, rot_in, rot_tile = tables
    half = pl.BlockSpec((tc, RET_W), rev)
    in_specs = [pl.BlockSpec((tc, D_MODEL), rev), half, half, col(RET_W, OFF_RV), col(RET_W, OFF_RG),
                col(GLA_KW, OFF_GQ), col(GLA_KW, OFF_GK), col(GLA_VW, OFF_GV), col(GLA_VW, OFF_GG),
                col(V7X_LANES, OFF_GLR),
                pl.BlockSpec((tc, D_MODEL), rev),
                pl.BlockSpec(dec.shape, fix3), pl.BlockSpec(qkd.shape, fix3), pl.BlockSpec(rot_in.shape, fix3),
                pl.BlockSpec((tc // tr, 8, 2 * RET_D), lambda i: (nsteps - 1 - i, 0, 0)),
                pl.BlockSpec((tc // tr, RET_W, RET_D), lambda i: (nsteps - 1 - i, 0, 0)),
                pl.BlockSpec((tc // tg, GLA_DV, GLA_KW), lambda i: (nsteps - 1 - i, 0, 0)),
                pl.BlockSpec((V7X_LANES, GLA_KW), fix), pl.BlockSpec((1, GLA_KW), fix),
                pl.BlockSpec((1, RET_W), fix), pl.BlockSpec((1, GLA_VW), fix), pl.BlockSpec(memory_space=pl.ANY)]
    out_specs = (pl.BlockSpec((tc, D_IN_PAD), rev), pl.BlockSpec((V7X_LANES, GLA_KW), fix),
                 pl.BlockSpec((8, RET_W), fix))
    out_shape = (jax.ShapeDtypeStruct((t, D_IN_PAD), BF16), jax.ShapeDtypeStruct((V7X_LANES, GLA_KW), F32),
                 jax.ShapeDtypeStruct((8, RET_W), F32))
    return pl.pallas_call(
        body, name="mixer_bwd", grid=(nsteps,), in_specs=in_specs, out_specs=out_specs, out_shape=out_shape,
        scratch_shapes=[pltpu.VMEM((RET_W, RET_D), F32), pltpu.VMEM((GLA_DV, GLA_KW), F32)],
        compiler_params=pltpu.CompilerParams(dimension_semantics=("arbitrary",), vmem_limit_bytes=V7X_VMEM_LIMIT),
    )(dmix, qrb, krb, *([proj] * 7), oraw, dec, qkd, rot_in, rot_tile, rst, sst, gw_pad, gb, rnw, gnw, after)


def _inproj_bwd(dproj, x2d, dxa, sc1p, w_in_t, after):
    t = x2d.shape[0]
    tm = min(2 * PROJ_TILE, t)

    def body(dp_ref, x_ref, dxa_ref, sc_ref, w_hbm, after_ref, gx_ref, acc_ref, w_vmem, sem):
        first = pl.program_id(0) == 0
        _load_w_in_t(first, w_hbm, w_vmem, sem)

        @pl.when(first)
        def _():
            acc_ref[...] = jnp.zeros_like(acc_ref)

        du = _dot(dp_ref[...], w_vmem[...])
        xh, rstd = _ln_stats(x_ref[...])
        gx_ref[...] = dxa_ref[...] + _ln_bwd(du * sc_ref[...], xh, rstd)
        acc_ref[0:1, :] += jnp.sum(du * xh, axis=0, keepdims=True)
        acc_ref[1:2, :] += jnp.sum(du, axis=0, keepdims=True)

    row = lambda i: (i, 0)
    fix = lambda i: (0, 0)
    return pl.pallas_call(
        body, name="inproj_bwd", grid=(t // tm,),
        in_specs=[pl.BlockSpec((tm, D_IN_PAD), row), pl.BlockSpec((tm, D_MODEL), row), pl.BlockSpec((tm, D_MODEL), row),
                  pl.BlockSpec((1, D_MODEL), fix), pl.BlockSpec(memory_space=pl.ANY), pl.BlockSpec(memory_space=pl.ANY)],
        out_specs=(pl.BlockSpec((tm, D_MODEL), row), pl.BlockSpec((8, D_MODEL), fix)),
        out_shape=(jax.ShapeDtypeStruct((t, D_MODEL), F32), jax.ShapeDtypeStruct((8, D_MODEL), F32)),
        scratch_shapes=[pltpu.VMEM((D_IN_PAD, D_MODEL), BF16), pltpu.SemaphoreType.DMA((1,))],
        compiler_params=pltpu.CompilerParams(dimension_semantics=("arbitrary",), vmem_limit_bytes=V7X_VMEM_LIMIT),
    )(dproj, x2d, dxa, sc1p, w_in_t, after)


def _adam_math(w, g, m, v):
    m = ADAM_B1 * m + (1.0 - ADAM_B1) * g
    v = ADAM_B2 * v + (1.0 - ADAM_B2) * (g * g)
    m_hat = m / (1.0 - ADAM_B1 ** ADAM_STEP)
    v_hat = v / (1.0 - ADAM_B2 ** ADAM_STEP)
    delta = -ADAM_LR * (m_hat / (jnp.sqrt(v_hat) + ADAM_EPS) + ADAM_WD * w)
    return delta, m, v


def _adamw(w, gparts, m, v, name):
    nparts, rows, cols = gparts.shape
    tr = rows
    for cand in (512, 256, 128, 64, 32, 16, 8):
        if rows % cand == 0:
            tr = cand
            break

    def body(w_ref, g_ref, m_ref, v_ref, go_ref, d_ref, mo_ref, vo_ref):
        g = g_ref[0].astype(F32)
        for p in range(1, nparts):
            g = g + g_ref[p].astype(F32)
        delta, mn, vn = _adam_math(w_ref[...], g, m_ref[...], v_ref[...])
        go_ref[...] = g
        d_ref[...] = delta
        mo_ref[...] = mn
        vo_ref[...] = vn

    blk = pl.BlockSpec((tr, cols), lambda i: (i, 0))
    shp = jax.ShapeDtypeStruct((rows, cols), F32)
    return pl.pallas_call(
        body, name=name, grid=(rows // tr,),
        in_specs=[blk, pl.BlockSpec((nparts, tr, cols), lambda i: (0, i, 0)), blk, blk],
        out_specs=(blk, blk, blk, blk), out_shape=(shp, shp, shp, shp),
        compiler_params=pltpu.CompilerParams(dimension_semantics=("arbitrary",), vmem_limit_bytes=V7X_VMEM_LIMIT),
    )(w, gparts, m, v)


def _small_reduce(gathered, gathered_gw, c_all, dmod_cols):
    def body(g_ref, gw_ref, c_ref, dm_ref, sum_ref, gwsum_ref, gb_ref, gwa_ref):
        s = g_ref[0]
        sw = gw_ref[0]
        for p in range(1, N_DEV):
            s = s + g_ref[p]
            sw = sw + gw_ref[p]
        sum_ref[...] = s
        gwsum_ref[...] = sw
        for i in range(6):
            gb_ref[:, i * D_MODEL:(i + 1) * D_MODEL] = s[i:i + 1, :]
        cc = c_ref[...]
        gwa_ref[...] = _dot(cc * _sigmoid(cc), dm_ref[...], TN, HIGHEST)

    vm = pl.BlockSpec(memory_space=pltpu.VMEM)
    return pl.pallas_call(
        body, name="small_reduce",
        out_shape=(jax.ShapeDtypeStruct(gathered.shape[1:], F32), jax.ShapeDtypeStruct(gathered_gw.shape[1:], F32),
                   jax.ShapeDtypeStruct((1, 6 * D_MODEL), F32), jax.ShapeDtypeStruct((D_MODEL, ADA_COLS), F32)),
        in_specs=[vm] * 4, out_specs=(vm, vm, vm, vm),
        compiler_params=pltpu.CompilerParams(vmem_limit_bytes=V7X_VMEM_LIMIT),
    )(gathered, gathered_gw, c_all, dmod_cols)


SMR_LN1W, SMR_LN1B, SMR_LN2W, SMR_LN2B, SMR_NORMS, SMR_MISC = 6, 7, 8, 9, 10, 11


def _adamw_small(gsum, g_b_ada, g_ggw, params, moms, vels):
    n = len(params)

    def body(*refs):
        gsum_ref, gb_ref, gw_ref = refs[:3]
        w_refs, m_refs, v_refs = refs[3:3 + n], refs[3 + n:3 + 2 * n], refs[3 + 2 * n:3 + 3 * n]
        outs = refs[3 + 3 * n:]
        g_refs, d_refs, mo_refs, vo_refs = outs[:n - 1], outs[n - 1:2 * n - 1], outs[2 * n - 1:3 * n - 1], outs[3 * n - 1:]
        grads = [gb_ref[...],
                 gsum_ref[SMR_NORMS:SMR_NORMS + 1, 0:RET_W],
                 gsum_ref[SMR_MISC:SMR_MISC + 1, 0:GLA_KW],
                 gsum_ref[SMR_NORMS:SMR_NORMS + 1, RET_W:RET_W + GLA_VW],
                 gsum_ref[SMR_LN1W:SMR_LN1W + 1, :], gsum_ref[SMR_LN1B:SMR_LN1B + 1, :],
                 gsum_ref[SMR_LN2W:SMR_LN2W + 1, :], gsum_ref[SMR_LN2B:SMR_LN2B + 1, :],
                 gw_ref[...]]
        for i in range(n):
            delta, mn, vn = _adam_math(w_refs[i][...], grads[i], m_refs[i][...], v_refs[i][...])
            if i < n - 1:
                g_refs[i][...] = grads[i]
            d_refs[i][...] = delta
            mo_refs[i][...] = mn
            vo_refs[i][...] = vn

    vm = pl.BlockSpec(memory_space=pltpu.VMEM)
    shapes = [jax.ShapeDtypeStruct(p.shape, F32) for p in params]
    n_in = 3 + 3 * n
    out_shape = tuple(shapes[:n - 1] + shapes * 3)
    return pl.pallas_call(
        body, name="adamw_small", out_shape=out_shape,
        in_specs=[vm] * n_in, out_specs=tuple([vm] * len(out_shape)),
        compiler_params=pltpu.CompilerParams(vmem_limit_bytes=V7X_VMEM_LIMIT),
    )(gsum, g_b_ada, g_ggw, *params, *moms, *vels)


def kernel(x, c, w_ada, b_ada, w_in, ret_norm_w, gla_gate_w, gla_gate_b, gla_norm_w, w_out, ln1_w, ln1_b, w_ff1, w_ff2, ln2_w, ln2_b, loss_target, m_w_ada, m_b_ada, m_w_in, m_ret_norm_w, m_gla_gate_w, m_gla_gate_b, m_gla_norm_w, m_w_out, m_ln1_w, m_ln1_b, m_w_ff1, m_w_ff2, m_ln2_w, m_ln2_b, v_w_ada, v_b_ada, v_w_in, v_ret_norm_w, v_gla_gate_w, v_gla_gate_b, v_gla_norm_w, v_w_out, v_ln1_w, v_ln1_b, v_w_ff1, v_w_ff2, v_ln2_w, v_ln2_b):
    t = x.shape[1]
    xi, yi, ci = _my_coords()
    me = 4 * xi + 2 * yi + ci
    x2d = x[0]
    tgt = loss_target[0]

    c_ext = jnp.concatenate([c, gla_gate_w[0].reshape(1, GATE_RANK * GLA_KW // N_DEV)], axis=1)
    b_l = lax.dynamic_slice(b_ada, (0, me * ADA_COLS), (1, ADA_COLS))
    c_all3, mod_all, wi_g, ada_token = _adaln_mod(c_ext, w_ada[0], b_l, w_in[0].T.astype(BF16))

    wg = _exchange_start([w_out[0].astype(BF16), w_ff1[0].astype(BF16), w_ff2[0].astype(BF16)],
                         True, "wgather_start", after=ada_token)

    c_all = c_all3[:, 0, :D_MODEL]
    gate_w = c_all3[:, 0, D_MODEL:].reshape(N_DEV, GATE_RANK, GLA_KW // N_DEV)
    gate_w = gate_w.transpose(1, 0, 2).reshape(GATE_RANK, GLA_KW)
    gw_pad = jnp.zeros((V7X_LANES, GLA_KW), F32).at[:GATE_RANK].set(gate_w)
    mod = lax.dynamic_slice(mod_all, (0, me, 0), (N_DEV, 1, ADA_COLS)).reshape(6, D_MODEL)
    shift1, scale1, gate1, shift2, scale2, gate2 = [mod[i:i + 1] for i in range(6)]

    w_in_t = wi_g.reshape(D_IN, D_MODEL)

    tables = _ret_tables(t, min(RET_SUB, t))

    sc1p = 1.0 + scale1
    proj, u = _inproj_fwd(x2d, sc1p, shift1, w_in_t, after=wg[4])
    mixed, oraw, qrb, krb, rst, sst = _mixer_fwd(proj, tables, gw_pad, gla_gate_b, ret_norm_w, gla_norm_w)
    wo_g, w1_b, w2_g = _exchange_wait(*wg[:4], mixed, True, "wgather_wait")
    w_out_b = wo_g.reshape(D_MODEL, D_MODEL)
    w2_b = w2_g.reshape(D_FF, D_MODEL)
    vec_f = jnp.concatenate([gate1, 1.0 + scale2, shift2, gate2, ln1_w, ln1_b, ln2_w, ln2_b], axis=0)
    m, x1n, rstd1, u2, a, df, dh2, acc_f = _mid_fwd(mixed, x2d, tgt, vec_f, w_out_b, w1_b, w2_b)

    vec_b = jnp.concatenate([gate1, 1.0 + scale2, ln1_w, ln1_b, jnp.zeros((4, D_MODEL), F32)], axis=0)
    da, dm, dmix, dxa, acc_b = _ffn_bwd(df, a, dh2, x1n, rstd1, m, vec_b, w_out_b, w1_b, w2_b)
    dw2 = _matmul_tn(a, df, 2048, 1024, 2048, "tn_dw2", relu_sq=True)
    dw1 = _matmul_tn(u2, da, 1024, 2048, 2048, "tn_dw1", col_slab=FF_COLS)
    dwo = _matmul_tn(mixed, dm, 1024, 1024, 2048, "tn_dwout")
    gx = _exchange_start([dwo.reshape(N_DEV, OUT_ROWS, D_MODEL), dw1, dw2.reshape(N_DEV, FF_COLS, D_MODEL)], False,
                         "gradx_start")
    dproj, dgw, dvec = _mixer_bwd(dmix, proj, qrb, krb, oraw, tables, rst, sst, gw_pad,
                                  gla_gate_b, ret_norm_w, gla_norm_w, after=gx[4])
    dwi_s = _matmul_tn(dproj, u, D_IN_PAD, 1024, 1024, "tn_dwin", out_rows=IN_COLS)
    gi = _exchange_start([dwi_s], False, "gradin_start")
    grad_x, acc_i = _inproj_bwd(dproj, x2d, dxa, sc1p, w_in_t, after=gi[4])

    loss_part = jnp.sum(acc_f[3])
    small = jnp.concatenate([
        acc_i[1:2], acc_i[0:1], acc_b[4:5], acc_b[1:2], acc_b[0:1], acc_f[2:3],
        acc_b[2:3], acc_b[3:4], acc_f[0:1], acc_f[1:2],
        jnp.concatenate([dvec[0:1], dvec[1:2]], axis=1),
        jnp.concatenate([dvec[2:3, :GLA_KW], jnp.full((1, 128), loss_part, F32),
                         jnp.zeros((1, D_MODEL - GLA_KW - 128), F32)], axis=1),
        jnp.zeros((4, D_MODEL), F32)], axis=0)
    sg = _exchange_start([small, dgw[:GATE_RANK]], True, "small_start")

    r_wo, r_w1, r_w2 = _exchange_wait(*gx[:4], sg[4], False, "gradx_wait")
    r_wi, = _exchange_wait(*gi[:4], sg[4], False, "gradin_wait")
    big = [_adamw(w[0], r, m_[0], v_[0], nm) for w, r, m_, v_, nm in (
        (w_out, r_wo, m_w_out, v_w_out, "adamw_out"),
        (w_ff1, r_w1, m_w_ff1, v_w_ff1, "adamw_ff1"), (w_ff2, r_w2, m_w_ff2, v_w_ff2, "adamw_ff2"))]
    big_in = _adamw(w_in[0].T, r_wi, m_w_in[0].T, v_w_in[0].T, "adamw_in")
    big = [tuple(b.T for b in big_in)] + big
    g_big, d_big, m_big, v_big = [[b[i][None] for b in big] for i in range(4)]

    small_all, gw_all = _exchange_wait(*sg[:4], big_in[1], True, "small_wait")
    dmod_all = small_all[:, :6].reshape(N_DEV, 6 * D_MODEL)
    dmod_cols = lax.dynamic_slice(dmod_all, (0, me * ADA_COLS), (N_DEV, ADA_COLS))
    ssum, gw_sum, g_b_ada, g_w_ada = _small_reduce(small_all, gw_all, c_all, dmod_cols)
    loss = ssum[SMR_MISC, GLA_KW]
    g_ggw = lax.dynamic_slice(gw_sum, (0, me * (GLA_KW // N_DEV)), (GATE_RANK, GLA_KW // N_DEV))[None]

    small_w = [b_ada, ret_norm_w, gla_gate_b, gla_norm_w, ln1_w, ln1_b, ln2_w, ln2_b, gla_gate_w]
    small_m = [m_b_ada, m_ret_norm_w, m_gla_gate_b, m_gla_norm_w, m_ln1_w, m_ln1_b, m_ln2_w, m_ln2_b, m_gla_gate_w]
    small_v = [v_b_ada, v_ret_norm_w, v_gla_gate_b, v_gla_norm_w, v_ln1_w, v_ln1_b, v_ln2_w, v_ln2_b, v_gla_gate_w]
    res = _adamw_small(ssum, g_b_ada, g_ggw, small_w, small_m, small_v)
    small_g = list(res[:8]) + [g_ggw]
    d_small, m_small, v_small = list(res[8:17]), list(res[17:26]), list(res[26:35])

    _, d_w_ada, nm_w_ada, nv_w_ada = _adamw(w_ada[0], g_w_ada[None], m_w_ada[0], v_w_ada[0], "adamw_ada")

    def ordered(w_ada_v, small_vals, big_vals):
        b_ada_v, rnw_v, ggb_v, gnw_v, l1w_v, l1b_v, l2w_v, l2b_v, ggw_v = small_vals
        wi_v, wo_v, w1_v, w2_v = big_vals
        return [w_ada_v, b_ada_v, wi_v, rnw_v, ggw_v, ggb_v, gnw_v, wo_v, l1w_v, l1b_v, w1_v, w2_v, l2w_v, l2b_v]

    grads = ordered(g_w_ada[None], small_g, g_big)
    deltas = ordered(d_w_ada[None], d_small, d_big)
    new_m = ordered(nm_w_ada[None], m_small, m_big)
    new_v = ordered(nv_w_ada[None], v_small, v_big)
    return (loss, grad_x[None], *grads, *deltas, *new_m, *new_v)
```

```python
import numpy as np
import jax
import jax.numpy as jnp
from jax import lax
from jax.experimental import pallas as pl
from jax.experimental.pallas import tpu as pltpu

F32 = jnp.float32
BF16 = jnp.bfloat16
MESH = pl.DeviceIdType.MESH
HIGHEST = lax.Precision.HIGHEST

N_DEV = 8
D_MODEL = 1024
CHUNK = 64
RET_HEADS = 4
RET_D = 128
GLA_HEADS = 4
GLA_DK = 64
GLA_DV = 128
GLA_KW = GLA_HEADS * GLA_DK
RET_W = RET_HEADS * RET_D
GLA_VW = GLA_HEADS * GLA_DV
V7X_LANES = 128
GATE_RANK = 16
GATE_TAU = 16.0
D_FF = 4096
LN_EPS = 1e-5
ALPHA = (2.0 * 1) ** 0.25
D_IN = 3600
D_IN_PAD = 3712
ADA_COLS = 6 * D_MODEL // N_DEV
IN_COLS = D_IN // N_DEV
FF_COLS = D_FF // N_DEV
OUT_ROWS = D_MODEL // N_DEV

OFF_RQ, OFF_RK, OFF_RV, OFF_RG = 0, RET_W, 2 * RET_W, 3 * RET_W
OFF_GQ = 4 * RET_W
OFF_GK = OFF_GQ + GLA_KW
OFF_GV = OFF_GK + GLA_KW
OFF_GG = OFF_GV + GLA_VW
OFF_GLR = OFF_GG + GLA_VW

ADAM_LR, ADAM_B1, ADAM_B2, ADAM_EPS, ADAM_WD, ADAM_STEP = 0.001, 0.9, 0.999, 1e-08, 0.01, 10

V7X_VMEM_LIMIT = 62 * 1024 * 1024

ROW_TILE = 512
PROJ_TILE = 512
MIX_TILE = 512
RET_SUB = 256
GLA_SUB = 128


def _log_gamma(h):
    return float(np.log(np.float32(1.0) - np.float32(2.0) ** np.float32(-5.0 - h)))


def _my_coords():
    return lax.axis_index("x"), lax.axis_index("y"), lax.axis_index("c")


def _flip(v, bit):
    return 1 - v if bit else v


def _peer(k):
    x, y, c = _my_coords()
    px, py, pc = _flip(x, (k >> 2) & 1), _flip(y, (k >> 1) & 1), _flip(c, k & 1)
    return (px, py, pc), 4 * px + 2 * py + pc


def _dot(a, b, dims=(((1,), (0,)), ((), ())), precision=None):
    return lax.dot_general(a, b, dims, precision=precision, preferred_element_type=F32)


NN = (((1,), (0,)), ((), ()))
NT = (((1,), (1,)), ((), ()))
TN = (((0,), (0,)), ((), ()))


def _split_bf16(v, parts):
    out = []
    for _ in range(parts):
        p = v.astype(BF16)
        out.append(p)
        v = v - p.astype(F32)
    return out


def _dot_split(a, b, dims, a_exact=False):
    if a_exact:
        ab = a.astype(BF16)
        return sum(_dot(ab, p, dims) for p in _split_bf16(b, 2))
    a_hi, a_lo = _split_bf16(a, 2)
    b_hi, b_lo = _split_bf16(b, 2)
    return _dot(a_hi, b_hi, dims) + _dot(a_hi, b_lo, dims) + _dot(a_lo, b_hi, dims)


def _sigmoid(x):
    return 1.0 / (1.0 + jnp.exp(-x))


def _ln_stats(x):
    mu = jnp.mean(x, axis=-1, keepdims=True)
    xc = x - mu
    var = jnp.mean(xc * xc, axis=-1, keepdims=True)
    rstd = lax.rsqrt(var + LN_EPS)
    return xc * rstd, rstd


def _ln_bwd(dyh, xh, rstd):
    return rstd * (dyh - jnp.mean(dyh, axis=-1, keepdims=True) - xh * jnp.mean(dyh * xh, axis=-1, keepdims=True))


def _adaln_mod(c_ext, w_ada_l, b_l, w_in_l):
    width = c_ext.shape[1]

    def body(c_ref, w_ref, b_ref, wi_ref, call_ref, mod_ref, wig_ref, token_ref, s1, r1, s2, r2, gs, gr, gl):
        gather = _TwoLevelGather([wi_ref], [wig_ref], gs, gr, gl)
        gather.start()
        token_ref[...] = jnp.zeros_like(token_ref)
        x, y, c = _my_coords()
        me = 4 * x + 2 * y + c
        call_ref[me] = c_ref[...]
        sends = []
        for k in range(1, N_DEV):
            peer, _ = _peer(k)
            cp = pltpu.make_async_remote_copy(c_ref, call_ref.at[me], s1.at[k - 1], r1.at[k - 1],
                                              device_id=peer, device_id_type=MESH)
            cp.start()
            sends.append(cp)
        for k in range(1, N_DEV):
            peer, pid = _peer(k)
            pltpu.make_async_remote_copy(c_ref, call_ref.at[pid], s1.at[k - 1], r1.at[k - 1],
                                         device_id=peer, device_id_type=MESH).wait_recv()
        for cp in sends:
            cp.wait_send()
        row = lax.broadcasted_iota(jnp.int32, (N_DEV, D_MODEL), 0)
        call = jnp.zeros((N_DEV, D_MODEL), F32)
        for j in range(N_DEV):
            call = jnp.where(row == j, jnp.broadcast_to(call_ref[j][:, :D_MODEL], (N_DEV, D_MODEL)), call)
        sc = call * _sigmoid(call)
        mod = _dot(sc, w_ref[...], NN, HIGHEST) + b_ref[...]
        mod_ref[me] = mod
        sends = []
        for k in range(1, N_DEV):
            peer, _ = _peer(k)
            cp = pltpu.make_async_remote_copy(mod_ref.at[me], mod_ref.at[me], s2.at[k - 1], r2.at[k - 1],
                                              device_id=peer, device_id_type=MESH)
            cp.start()
            sends.append(cp)
        for k in range(1, N_DEV):
            peer, pid = _peer(k)
            pltpu.make_async_remote_copy(mod_ref.at[pid], mod_ref.at[pid], s2.at[k - 1], r2.at[k - 1],
                                         device_id=peer, device_id_type=MESH).wait_recv()
        for cp in sends:
            cp.wait_send()
        gather.forward()
        gather.finish()

    vm = pl.BlockSpec(memory_space=pltpu.VMEM)
    hbm = pl.BlockSpec(memory_space=pl.ANY)
    return pl.pallas_call(
        body, name="adaln_mod",
        out_shape=(jax.ShapeDtypeStruct((N_DEV, 1, width), F32),
                   jax.ShapeDtypeStruct((N_DEV, N_DEV, ADA_COLS), F32),
                   jax.ShapeDtypeStruct((N_DEV, *w_in_l.shape), w_in_l.dtype),
                   jax.ShapeDtypeStruct((8, 128), F32)),
        in_specs=[vm, vm, vm, hbm], out_specs=(vm, vm, hbm, vm),
        scratch_shapes=[pltpu.SemaphoreType.DMA((N_DEV - 1,))] * 4
        + [pltpu.SemaphoreType.DMA((7,)), pltpu.SemaphoreType.DMA((7,)), pltpu.SemaphoreType.DMA((1,))],
        compiler_params=pltpu.CompilerParams(vmem_limit_bytes=V7X_VMEM_LIMIT),
    )(c_ext, w_ada_l, b_l, w_in_l)


class _TwoLevelGather:
    def __init__(self, x_refs, out_refs, send_sems, recv_sems, local_sems):
        self.x_refs, self.out_refs = x_refs, out_refs
        self.send_sems, self.recv_sems, self.local_sems = send_sems, recv_sems, local_sems
        x, y, c = _my_coords()
        self.c = c
        self.me, self.sibling = (x, y, c), (x, y, 1 - c)
        self.chips = [(1 - x, y), (x, 1 - y), (1 - x, 1 - y)]

    def _copy(self, a, k, block, to, src=None):
        px, py, pc = block
        slab = self.out_refs[a].at[4 * px + 2 * py + pc]
        return pltpu.make_async_remote_copy(
            src_ref=slab if src is None else src, dst_ref=slab,
            send_sem=self.send_sems.at[7 * a + k], recv_sem=self.recv_sems.at[7 * a + k],
            device_id=to, device_id_type=MESH)

    def _mine(self, a):
        px, py, pc = self.me
        return pltpu.make_async_copy(self.x_refs[a], self.out_refs[a].at[4 * px + 2 * py + pc], self.local_sems.at[a])

    def _first(self, a):
        cps = [self._copy(a, 0, self.me, self.sibling, src=self.x_refs[a])]
        cps += [self._copy(a, 1 + j, self.me, (*chip, self.c), src=self.x_refs[a]) for j, chip in enumerate(self.chips)]
        return cps

    def _passed(self, a):
        return [self._copy(a, 4 + j, (*chip, self.c), self.sibling) for j, chip in enumerate(self.chips)]

    def start(self):
        for a in range(len(self.x_refs)):
            self._mine(a).start()
            for cp in self._first(a):
                cp.start()

    def forward(self):
        for a in range(len(self.x_refs)):
            passed = self._passed(a)
            for j, chip in enumerate(self.chips):
                self._copy(a, 1 + j, (*chip, self.c), self.me).wait_recv()
                passed[j].start()

    def finish(self):
        for a in range(len(self.x_refs)):
            self._copy(a, 0, self.sibling, self.me).wait_recv()
            for j, chip in enumerate(self.chips):
                self._copy(a, 4 + j, (*chip, 1 - self.c), self.me).wait_recv()
            for cp in self._first(a) + self._passed(a):
                cp.wait_send()
            self._mine(a).wait()


def _exchange_copy(src_refs, land_refs, send_sems, recv_sems, a, k, gather, receiving):
    x, y, c = _my_coords()
    me = 4 * x + 2 * y + c
    peer, pid = _peer(k)
    src = src_refs[a] if gather else src_refs[a].at[pid]
    dst = land_refs[a].at[pid if receiving else me]
    return pltpu.make_async_remote_copy(src, dst, send_sems.at[7 * a + k - 1], recv_sems.at[7 * a + k - 1],
                                        device_id=peer, device_id_type=MESH)


def _own_copy(src_refs, land_refs, send_sems, a, n, gather):
    x, y, c = _my_coords()
    me = 4 * x + 2 * y + c
    src = src_refs[a] if gather else src_refs[a].at[me]
    return pltpu.make_async_copy(src, land_refs[a].at[me], send_sems.at[7 * n + a])


def _exchange_start(srcs, gather, name, after=None):
    n = len(srcs)
    land_shapes = [(N_DEV, *s.shape) if gather else s.shape for s in srcs]
    n_in = n if after is None else n + 1

    def body(*refs):
        src_refs, send_sems, recv_sems, token = refs[:n], refs[n_in], refs[n_in + 1], refs[-1]
        land_refs = refs[n_in + 2 + n:n_in + 2 + 2 * n]
        for a in range(n):
            _own_copy(src_refs, land_refs, send_sems, a, n, gather).start()
            for k in range(1, N_DEV):
                _exchange_copy(src_refs, land_refs, send_sems, recv_sems, a, k, gather, receiving=False).start()
        token[...] = jnp.zeros_like(token)

    hbm = pl.BlockSpec(memory_space=pltpu.HBM)
    sem = pl.BlockSpec(memory_space=pltpu.SEMAPHORE)
    res = pl.pallas_call(
        body, name=name,
        out_shape=(pltpu.SemaphoreType.DMA((8 * n,)), pltpu.SemaphoreType.DMA((7 * n,)),
                   *[pltpu.HBM(v.shape, v.dtype) for v in srcs],
                   *[pltpu.HBM(shape, v.dtype) for shape, v in zip(land_shapes, srcs)],
                   jax.ShapeDtypeStruct((8, 128), F32)),
        in_specs=[hbm] * n + [pl.BlockSpec(memory_space=pl.ANY)] * (n_in - n),
        out_specs=(sem, sem, *([hbm] * (2 * n)), pl.BlockSpec(memory_space=pltpu.VMEM)),
        input_output_aliases={i: 2 + i for i in range(n)},
        compiler_params=pltpu.CompilerParams(has_side_effects=pltpu.SideEffectType.DATAFLOW_SIDE_EFFECTING),
    )(*[pltpu.with_memory_space_constraint(v, pltpu.HBM) for v in srcs], *([] if after is None else [after]))
    return res[0], res[1], list(res[2:2 + n]), list(res[2 + n:2 + 2 * n]), res[-1]


def _exchange_wait(send_sems, recv_sems, srcs, lands, after, gather, name):
    n = len(srcs)

    def body(*refs):
        src_refs, land_refs, s_sems, r_sems = refs[:n], refs[n:2 * n], refs[2 * n], refs[2 * n + 1]
        for a in range(n):
            _own_copy(src_refs, land_refs, s_sems, a, n, gather).wait()
            for k in range(1, N_DEV):
                _exchange_copy(src_refs, land_refs, s_sems, r_sems, a, k, gather, receiving=False).wait_send()
                _exchange_copy(src_refs, land_refs, s_sems, r_sems, a, k, gather, receiving=True).wait_recv()

    hbm = pl.BlockSpec(memory_space=pltpu.HBM)
    sem = pl.BlockSpec(memory_space=pltpu.SEMAPHORE)
    res = pl.pallas_call(
        body, name=name,
        out_shape=tuple(pltpu.HBM(v.shape, v.dtype) for v in srcs + lands),
        in_specs=[hbm] * (2 * n) + [sem, sem, pl.BlockSpec(memory_space=pl.ANY)],
        out_specs=tuple([hbm] * (2 * n)),
        input_output_aliases={i: i for i in range(2 * n)},
        compiler_params=pltpu.CompilerParams(has_side_effects=pltpu.SideEffectType.DATAFLOW_SIDE_EFFECTING),
    )(*srcs, *lands, send_sems, recv_sems, after)
    return list(res[n:])


def _load_resident(step_is_first, pairs, sem):
    @pl.when(step_is_first)
    def _():
        copies = [pltpu.make_async_copy(src, dst, sem.at[i]) for i, (src, dst) in enumerate(pairs)]
        for cp in copies:
            cp.start()
        for cp in copies:
            cp.wait()


def _load_w_in_t(step_is_first, w_hbm, w_vmem, sem):
    @pl.when(step_is_first)
    def _():
        w_vmem[D_IN:, :] = jnp.zeros((D_IN_PAD - D_IN, D_MODEL), BF16)
    _load_resident(step_is_first, [(w_hbm, w_vmem.at[pl.ds(0, D_IN)])], sem)


def _inproj_fwd(x2d, sc1p, sh1, w_in_t, after):
    t = x2d.shape[0]
    tm = min(PROJ_TILE, t)

    def body(x_ref, sc_ref, sh_ref, w_hbm, after_ref, proj_ref, u_ref, w_vmem, sem):
        _load_w_in_t(pl.program_id(0) == 0, w_hbm, w_vmem, sem)
        xh, _ = _ln_stats(x_ref[...])
        ub = (xh * sc_ref[...] + sh_ref[...]).astype(BF16)
        u_ref[...] = ub
        proj_ref[...] = _dot(ub, w_vmem[...], NT)

    row = lambda i: (i, 0)
    fix = lambda i: (0, 0)
    return pl.pallas_call(
        body, name="inproj_fwd", grid=(t // tm,),
        in_specs=[pl.BlockSpec((tm, D_MODEL), row), pl.BlockSpec((1, D_MODEL), fix), pl.BlockSpec((1, D_MODEL), fix),
                  pl.BlockSpec(memory_space=pl.ANY), pl.BlockSpec(memory_space=pl.ANY)],
        out_specs=(pl.BlockSpec((tm, D_IN_PAD), row), pl.BlockSpec((tm, D_MODEL), row)),
        out_shape=(jax.ShapeDtypeStruct((t, D_IN_PAD), F32), jax.ShapeDtypeStruct((t, D_MODEL), BF16)),
        scratch_shapes=[pltpu.VMEM((D_IN_PAD, D_MODEL), BF16), pltpu.SemaphoreType.DMA((1,))],
        compiler_params=pltpu.CompilerParams(dimension_semantics=("arbitrary",), vmem_limit_bytes=V7X_VMEM_LIMIT),
    )(x2d, sc1p, sh1, w_in_t, after)


CHUNK_SHIFT = CHUNK.bit_length() - 1


def _ret_tables(t, tl):
    r = lax.broadcasted_iota(jnp.int32, (tl, tl), 0)
    c = lax.broadcasted_iota(jnp.int32, (tl, tl), 1)
    allowed = jnp.right_shift(c, CHUNK_SHIFT) <= jnp.right_shift(r, CHUNK_SHIFT)
    dist = jnp.abs(r - c).astype(F32)
    rowf = lax.broadcasted_iota(jnp.int32, (tl, RET_D), 0).astype(F32)
    lgs = [_log_gamma(h) for h in range(RET_HEADS)]
    dec = jnp.stack([jnp.where(allowed, jnp.exp(lg * dist), 0.0) for lg in lgs])
    qkd = jnp.stack([jnp.exp(lg * (rowf + 1.0)) for lg in lgs] + [jnp.exp(lg * (tl - 1.0 - rowf)) for lg in lgs])
    inv = 1.0 / (10000.0 ** jnp.linspace(0.0, 1.0, RET_D // 2, dtype=F32))
    off = jnp.arange(tl, dtype=F32)[:, None] * inv[None, :]
    start = (jnp.arange(t // tl, dtype=F32) * tl)[:, None] * inv[None, :]
    co, so = jnp.cos(off), jnp.sin(off)
    rot_in = jnp.stack([jnp.concatenate([co, co], 1), jnp.concatenate([so, so], 1),
                        jnp.concatenate([-co, co], 1), jnp.concatenate([-so, so], 1)])
    cs, ss = jnp.cos(start), jnp.sin(start)
    rot_tile = jnp.concatenate([cs, cs, ss, ss], axis=1)
    rot_tile = jnp.broadcast_to(rot_tile[:, None, :], (t // tl, 8, 2 * RET_D))
    return dec, qkd, rot_in, rot_tile


def _tile_gammas(tl):
    return [float(np.exp(np.float32(_log_gamma(h)) * np.float32(tl))) for h in range(RET_HEADS)]


def _tile_rotary(rot_in_ref, rot_tile_ref, j):
    ca, sa = rot_tile_ref[j, 0:1, 0:RET_D], rot_tile_ref[j, 0:1, RET_D:2 * RET_D]
    cosv = ca * rot_in_ref[0] - sa * rot_in_ref[1]
    sinv = sa * rot_in_ref[2] + ca * rot_in_ref[3]
    return cosv, sinv


def _gla_consts(tl):
    r = lax.broadcasted_iota(jnp.int32, (tl, tl), 0)
    c = lax.broadcasted_iota(jnp.int32, (tl, tl), 1)
    ltri = (c <= r).astype(F32)
    utri = (c >= r).astype(F32)
    lane = lax.broadcasted_iota(jnp.int32, (1, GLA_KW), 1)
    hmask = [((lane >= h * GLA_DK) & (lane < (h + 1) * GLA_DK)).astype(F32) for h in range(GLA_HEADS)]
    rs = lax.broadcasted_iota(jnp.int32, (GLA_HEADS * tl, tl), 0) & (tl - 1)
    cs = lax.broadcasted_iota(jnp.int32, (GLA_HEADS * tl, tl), 1)
    lower = cs <= rs
    same = jnp.right_shift(cs, CHUNK_SHIFT) == jnp.right_shift(rs, CHUNK_SHIFT)
    upper = jnp.logical_and(jnp.logical_not(lower), same)
    return dict(ltri=ltri, utri=utri, hmask=hmask, lower=lower, upper=upper)


def _tile_rows(j, tl):
    return pl.ds(j * tl, tl) if isinstance(j, int) else pl.ds(pl.multiple_of(j * tl, tl), tl)


def _for_tiles(cps, fn):
    for j in range(cps):
        fn(j, 0)


def _rotate(v, cosv, sinv):
    return v * cosv + pltpu.roll(v, RET_D // 2, 1) * sinv


def _rotate_t(d, cosv, sinv):
    return d * cosv + pltpu.roll(d * sinv, RET_D // 2, 1)


def _stack_heads(v, hmask):
    return jnp.concatenate([v * hmask[h] for h in range(GLA_HEADS)], axis=0)


def _gla_gates(glr, gw, gb, ltri, tl):
    z = _dot_split(glr, gw, NN) + gb
    la = (jnp.minimum(z, 0.0) - jnp.log(1.0 + jnp.exp(-jnp.abs(z)))) * (1.0 / GATE_TAU)
    b = _dot_split(ltri, la, NN, a_exact=True)
    level = b[tl // 2 - 1:tl // 2, :]
    ep = jnp.exp(jnp.clip(b - level, -80.0, 80.0))
    em = jnp.exp(jnp.clip(level - b, -80.0, 80.0))
    bl = b[tl - 1:tl, :]
    return z, b, bl, ep, em


def _mixer_fwd(proj, tables, gw_pad, gb, rnw, gnw):
    t = proj.shape[0]
    tc = min(MIX_TILE, t)
    tr, tg = min(RET_SUB, tc), min(GLA_SUB, tc)
    nsteps = t // tc
    scale_r = RET_D ** -0.5
    scale_g = GLA_DK ** -0.5
    gammas = _tile_gammas(tr)

    def body(rq_ref, rk_ref, rv_ref, rg_ref, gq_ref, gk_ref, gv_ref, gg_ref, glr_ref,
             dec_ref, qkd_ref, rot_in_ref, rot_tile_ref, gw_ref, gb_ref, rnw_ref, gnw_ref,
             mix_ref, oraw_ref, qrb_ref, krb_ref, rst_ref, sst_ref, r_scr, s_scr):
        @pl.when(pl.program_id(0) == 0)
        def _():
            r_scr[...] = jnp.zeros_like(r_scr)
            s_scr[...] = jnp.zeros_like(s_scr)

        gla_k = _gla_consts(tg)

        def ret_tile(j, carry):
            rows = _tile_rows(j, tr)
            cosv, sinv = _tile_rotary(rot_in_ref, rot_tile_ref, j)
            for h in range(RET_HEADS):
                cols = slice(h * RET_D, (h + 1) * RET_D)
                qr = _rotate(rq_ref[rows, cols], cosv, sinv) * scale_r
                kr = _rotate(rk_ref[rows, cols], cosv, sinv)
                vb = rv_ref[rows, cols].astype(BF16)
                qb, kb = qr.astype(BF16), kr.astype(BF16)
                qrb_ref[rows, cols] = qb
                krb_ref[rows, cols] = kb
                p = _dot(qb, kb, NT) * dec_ref[h]
                rp = r_scr[cols, :]
                o = _dot(p.astype(BF16), vb) + _dot((qr * qkd_ref[h]).astype(BF16), rp.astype(BF16))
                rst_ref[j, cols, :] = rp
                r_scr[cols, :] = gammas[h] * rp + _dot((kr * qkd_ref[RET_HEADS + h]).astype(BF16), vb, TN)
                oraw_ref[rows, cols] = o
                oc = o - jnp.mean(o, axis=-1, keepdims=True)
                n = oc * lax.rsqrt(jnp.mean(oc * oc, axis=-1, keepdims=True) + LN_EPS)
                g = rg_ref[rows, cols]
                mix_ref[rows, cols] = (n * rnw_ref[:, cols] * (g * _sigmoid(g))).astype(BF16)
            return carry

        def gla_tile(j, carry):
            k = gla_k
            tl = tg
            rows = _tile_rows(j, tg)
            _, b, bl, ep, em = _gla_gates(glr_ref[rows, :], gw_ref[...], gb_ref[...], k["ltri"], tl)
            qs = gq_ref[rows, :] * scale_g
            kk = gk_ref[rows, :]
            x_all = _dot(_stack_heads(qs * ep, k["hmask"]).astype(BF16), (kk * em).astype(BF16), NT)
            y_all = _dot(_stack_heads(qs * em, k["hmask"]).astype(BF16), (kk * ep).astype(BF16), NT)
            a_all = jnp.where(k["lower"], x_all, jnp.where(k["upper"], y_all, 0.0)).astype(BF16)
            st = s_scr[...]
            oq = _dot(_stack_heads(qs * jnp.exp(b), k["hmask"]).astype(BF16), st.astype(BF16), NT)
            kg = kk * jnp.exp(bl - b)
            sst_ref[j] = st
            st_new = st * jnp.exp(bl)
            for h in range(GLA_HEADS):
                cols = slice(h * GLA_DV, (h + 1) * GLA_DV)
                hr = slice(h * tl, (h + 1) * tl)
                vb = gv_ref[rows, cols].astype(BF16)
                o = _dot(a_all[hr, :], vb) + oq[hr, :]
                st_new = st_new + _dot(vb, (kg * k["hmask"][h]).astype(BF16), TN)
                ocols = slice(RET_W + h * GLA_DV, RET_W + (h + 1) * GLA_DV)
                oraw_ref[rows, ocols] = o
                n = o * lax.rsqrt(jnp.mean(o * o, axis=-1, keepdims=True) + LN_EPS)
                g = gg_ref[rows, cols]
                mix_ref[rows, ocols] = (n * gnw_ref[:, cols] * (g * _sigmoid(g))).astype(BF16)
            s_scr[...] = st_new
            return carry

        _for_tiles(tc // tr, ret_tile)
        _for_tiles(tc // tg, gla_tile)

    def col(width, off):
        return pl.BlockSpec((tc, width), lambda i, o=off // width: (i, o))

    fix = lambda i: (0, 0)
    fix3 = lambda i: (0, 0, 0)
    dec, qkd, rot_in, rot_tile = tables
    in_specs = [col(RET_W, OFF_RQ), col(RET_W, OFF_RK), col(RET_W, OFF_RV), col(RET_W, OFF_RG),
                col(GLA_KW, OFF_GQ), col(GLA_KW, OFF_GK), col(GLA_VW, OFF_GV), col(GLA_VW, OFF_GG),
                col(V7X_LANES, OFF_GLR),
                pl.BlockSpec(dec.shape, fix3), pl.BlockSpec(qkd.shape, fix3), pl.BlockSpec(rot_in.shape, fix3),
                pl.BlockSpec((tc // tr, 8, 2 * RET_D), lambda i: (i, 0, 0)),
                pl.BlockSpec((V7X_LANES, GLA_KW), fix), pl.BlockSpec((1, GLA_KW), fix),
                pl.BlockSpec((1, RET_W), fix), pl.BlockSpec((1, GLA_VW), fix)]
    half = pl.BlockSpec((tc, RET_W), lambda i: (i, 0))
    out_specs = (pl.BlockSpec((tc, D_MODEL), lambda i: (i, 0)), pl.BlockSpec((tc, D_MODEL), lambda i: (i, 0)),
                 half, half,
                 pl.BlockSpec((tc // tr, RET_W, RET_D), lambda i: (i, 0, 0)),
                 pl.BlockSpec((tc // tg, GLA_DV, GLA_KW), lambda i: (i, 0, 0)))
    out_shape = (jax.ShapeDtypeStruct((t, D_MODEL), BF16), jax.ShapeDtypeStruct((t, D_MODEL), F32),
                 jax.ShapeDtypeStruct((t, RET_W), BF16), jax.ShapeDtypeStruct((t, RET_W), BF16),
                 jax.ShapeDtypeStruct((t // tr, RET_W, RET_D), F32),
                 jax.ShapeDtypeStruct((t // tg, GLA_DV, GLA_KW), F32))
    return pl.pallas_call(
        body, name="mixer_fwd", grid=(nsteps,), in_specs=in_specs, out_specs=out_specs, out_shape=out_shape,
        scratch_shapes=[pltpu.VMEM((RET_W, RET_D), F32), pltpu.VMEM((GLA_DV, GLA_KW), F32)],
        compiler_params=pltpu.CompilerParams(dimension_semantics=("arbitrary",), vmem_limit_bytes=V7X_VMEM_LIMIT),
    )(*([proj] * 9), dec, qkd, rot_in, rot_tile, gw_pad, gb, rnw, gnw)


def _mid_fwd(mixed, x2d, target, vecs, w_out_b, w1_b, w2_b):
    t = x2d.shape[0]
    tm = min(ROW_TILE, t)

    def body(mix_ref, x_ref, tgt_ref, v_ref, wo_hbm, w1_hbm, w2_hbm,
             m_ref, x1n_ref, rstd_ref, u2_ref, a_ref, df_ref, dh2_ref, acc_ref, wo, w1, w2, sem):
        first = pl.program_id(0) == 0
        _load_resident(first, [(wo_hbm, wo), (w1_hbm, w1), (w2_hbm, w2)], sem)

        @pl.when(first)
        def _():
            acc_ref[...] = jnp.zeros_like(acc_ref)

        gate1, sc2p, sh2, gate2 = v_ref[0:1, :], v_ref[1:2, :], v_ref[2:3, :], v_ref[3:4, :]
        l1w, l1b, l2w, l2b = v_ref[4:5, :], v_ref[5:6, :], v_ref[6:7, :], v_ref[7:8, :]
        m = _dot(mix_ref[...], wo[...])
        m_ref[...] = m.astype(BF16)
        x1n, rstd1 = _ln_stats(ALPHA * x_ref[...] + gate1 * m)
        x1n_ref[...] = x1n
        rstd_ref[...] = rstd1
        x1 = x1n * l1w + l1b
        xh1, _ = _ln_stats(x1)
        u2 = (xh1 * sc2p + sh2).astype(BF16)
        u2_ref[...] = u2
        f = jnp.zeros((tm, D_MODEL), F32)
        for j in range(N_DEV):
            cols = slice(j * FF_COLS, (j + 1) * FF_COLS)
            a = _dot(u2, w1[j])
            a_ref[:, cols] = a.astype(BF16)
            r = jnp.maximum(a, 0.0)
            f = f + _dot((r * r).astype(BF16), w2[cols, :])
        yh, rstd2 = _ln_stats(ALPHA * x1 + gate2 * f)
        e = yh * l2w + l2b - tgt_ref[...]
        dy = e * (1.0 / D_MODEL)
        dh2 = _ln_bwd(dy * l2w, yh, rstd2)
        dh2_ref[...] = dh2
        df_ref[...] = (dh2 * gate2).astype(BF16)
        acc_ref[0:1, :] += jnp.sum(dy * yh, axis=0, keepdims=True)
        acc_ref[1:2, :] += jnp.sum(dy, axis=0, keepdims=True)
        acc_ref[2:3, :] += jnp.sum(dh2 * f, axis=0, keepdims=True)
        acc_ref[3:4, :] += jnp.sum(e * e, axis=0, keepdims=True) * (0.5 / D_MODEL)

    row = lambda i: (i, 0)
    fix = lambda i: (0, 0)
    hbm = pl.BlockSpec(memory_space=pl.ANY)
    return pl.pallas_call(
        body, name="mid_fwd", grid=(t // tm,),
        in_specs=[pl.BlockSpec((tm, D_MODEL), row), pl.BlockSpec((tm, D_MODEL), row), pl.BlockSpec((tm, D_MODEL), row),
                  pl.BlockSpec((8, D_MODEL), fix), hbm, hbm, hbm],
        out_specs=(pl.BlockSpec((tm, D_MODEL), row), pl.BlockSpec((tm, D_MODEL), row), pl.BlockSpec((tm, 1), row),
                   pl.BlockSpec((tm, D_MODEL), row), pl.BlockSpec((tm, D_FF), row), pl.BlockSpec((tm, D_MODEL), row),
                   pl.BlockSpec((tm, D_MODEL), row), pl.BlockSpec((8, D_MODEL), fix)),
        out_shape=(jax.ShapeDtypeStruct((t, D_MODEL), BF16), jax.ShapeDtypeStruct((t, D_MODEL), F32),
                   jax.ShapeDtypeStruct((t, 1), F32), jax.ShapeDtypeStruct((t, D_MODEL), BF16),
                   jax.ShapeDtypeStruct((t, D_FF), BF16), jax.ShapeDtypeStruct((t, D_MODEL), BF16),
                   jax.ShapeDtypeStruct((t, D_MODEL), F32), jax.ShapeDtypeStruct((8, D_MODEL), F32)),
        scratch_shapes=[pltpu.VMEM((D_MODEL, D_MODEL), BF16), pltpu.VMEM((N_DEV, D_MODEL, FF_COLS), BF16),
                        pltpu.VMEM((D_FF, D_MODEL), BF16), pltpu.SemaphoreType.DMA((3,))],
        compiler_params=pltpu.CompilerParams(dimension_semantics=("arbitrary",), vmem_limit_bytes=V7X_VMEM_LIMIT),
    )(mixed, x2d, target, vecs, w_out_b, w1_b, w2_b)


def _ffn_bwd(df, a, dh2, x1n, rstd1, m, vecs, w_out_b, w1_b, w2_b):
    t = x1n.shape[0]
    tm = min(ROW_TILE, t)

    def body(df_ref, a_ref, dh2_ref, x1n_ref, rstd_ref, m_ref, v_ref, wo_hbm, w1_hbm, w2_hbm,
             da_ref, dm_ref, dmix_ref, dxa_ref, acc_ref, wo, w1, w2, sem):
        first = pl.program_id(0) == 0
        _load_resident(first, [(wo_hbm, wo), (w1_hbm, w1), (w2_hbm, w2)], sem)

        @pl.when(first)
        def _():
            acc_ref[...] = jnp.zeros_like(acc_ref)

        gate1, sc2p, l1w, l1b = v_ref[0:1, :], v_ref[1:2, :], v_ref[2:3, :], v_ref[3:4, :]
        df = df_ref[...]
        du2 = jnp.zeros((tm, D_MODEL), F32)
        for j in range(N_DEV):
            cols = slice(j * FF_COLS, (j + 1) * FF_COLS)
            dr2 = _dot(df, w2[cols, :], NT)
            da = (dr2 * (2.0 * jnp.maximum(a_ref[:, cols].astype(F32), 0.0))).astype(BF16)
            da_ref[:, cols] = da
            du2 = du2 + _dot(da, w1[j], NT)
        x1n = x1n_ref[...]
        xh1, rstd0 = _ln_stats(x1n * l1w + l1b)
        dx1 = ALPHA * dh2_ref[...] + _ln_bwd(du2 * sc2p, xh1, rstd0)
        dh1 = _ln_bwd(dx1 * l1w, x1n, rstd_ref[...])
        dxa_ref[...] = ALPHA * dh1
        dm = (dh1 * gate1).astype(BF16)
        dm_ref[...] = dm
        dmix_ref[...] = _dot(dm, wo[...], NT)
        acc_ref[0:1, :] += jnp.sum(du2 * xh1, axis=0, keepdims=True)
        acc_ref[1:2, :] += jnp.sum(du2, axis=0, keepdims=True)
        acc_ref[2:3, :] += jnp.sum(dx1 * x1n, axis=0, keepdims=True)
        acc_ref[3:4, :] += jnp.sum(dx1, axis=0, keepdims=True)
        acc_ref[4:5, :] += jnp.sum(dh1 * m_ref[...].astype(F32), axis=0, keepdims=True)

    row = lambda i: (i, 0)
    fix = lambda i: (0, 0)
    hbm = pl.BlockSpec(memory_space=pl.ANY)
    return pl.pallas_call(
        body, name="ffn_bwd", grid=(t // tm,),
        in_specs=[pl.BlockSpec((tm, D_MODEL), row), pl.BlockSpec((tm, D_FF), row), pl.BlockSpec((tm, D_MODEL), row),
                  pl.BlockSpec((tm, D_MODEL), row), pl.BlockSpec((tm, 1), row), pl.BlockSpec((tm, D_MODEL), row),
                  pl.BlockSpec((8, D_MODEL), fix), hbm, hbm, hbm],
        out_specs=(pl.BlockSpec((tm, D_FF), row), pl.BlockSpec((tm, D_MODEL), row), pl.BlockSpec((tm, D_MODEL), row),
                   pl.BlockSpec((tm, D_MODEL), row), pl.BlockSpec((8, D_MODEL), fix)),
        out_shape=(jax.ShapeDtypeStruct((t, D_FF), BF16), jax.ShapeDtypeStruct((t, D_MODEL), BF16),
                   jax.ShapeDtypeStruct((t, D_MODEL), F32), jax.ShapeDtypeStruct((t, D_MODEL), F32),
                   jax.ShapeDtypeStruct((8, D_MODEL), F32)),
        scratch_shapes=[pltpu.VMEM((D_MODEL, D_MODEL), BF16), pltpu.VMEM((N_DEV, D_MODEL, FF_COLS), BF16),
                        pltpu.VMEM((D_FF, D_MODEL), BF16), pltpu.SemaphoreType.DMA((3,))],
        compiler_params=pltpu.CompilerParams(dimension_semantics=("arbitrary",), vmem_limit_bytes=V7X_VMEM_LIMIT),
    )(df, a, dh2, x1n, rstd1, m, vecs, w_out_b, w1_b, w2_b)


def _matmul_tn(lhs, rhs, tmm, tn, tk, name, relu_sq=False, col_slab=None, out_rows=None):
    t, mm = lhs.shape
    assert out_rows is None or (col_slab is None and tmm == mm)
    nn = rhs.shape[1]
    tk = min(tk, t)
    nk = t // tk

    def body(l_ref, r_ref, o_ref, acc):
        kk = pl.program_id(2)

        @pl.when(kk == 0)
        def _():
            acc[...] = jnp.zeros_like(acc)

        l = l_ref[...]
        if relu_sq:
            lf = jnp.maximum(l.astype(F32), 0.0)
            l = (lf * lf).astype(BF16)
        acc[...] += _dot(l, r_ref[...], TN)

        @pl.when(kk == nk - 1)
        def _():
            if out_rows is not None:
                for s in range(N_DEV):
                    o_ref[s] = acc[s * out_rows:(s + 1) * out_rows, :].astype(o_ref.dtype)
            elif col_slab is None:
                o_ref[...] = acc[...].astype(o_ref.dtype)
            else:
                for s in range(tn // col_slab):
                    o_ref[s] = acc[:, s * col_slab:(s + 1) * col_slab].astype(o_ref.dtype)

    if out_rows is not None:
        out_spec = pl.BlockSpec((N_DEV, out_rows, tn), lambda i, j, k: (0, 0, j))
        out_shape = jax.ShapeDtypeStruct((N_DEV, out_rows, nn), BF16)
    elif col_slab is None:
        out_spec = pl.BlockSpec((tmm, tn), lambda i, j, k: (i, j))
        out_shape = jax.ShapeDtypeStruct((mm, nn), BF16)
    else:
        out_spec = pl.BlockSpec((tn // col_slab, tmm, col_slab), lambda i, j, k: (j, i, 0))
        out_shape = jax.ShapeDtypeStruct((nn // col_slab, mm, col_slab), BF16)
    return pl.pallas_call(
        body, name=name, grid=(mm // tmm, nn // tn, nk),
        in_specs=[pl.BlockSpec((tk, tmm), lambda i, j, k: (k, i)), pl.BlockSpec((tk, tn), lambda i, j, k: (k, j))],
        out_specs=out_spec,
        out_shape=out_shape,
        scratch_shapes=[pltpu.VMEM((tmm, tn), F32)],
        compiler_params=pltpu.CompilerParams(dimension_semantics=("arbitrary", "arbitrary", "arbitrary"),
                                             vmem_limit_bytes=V7X_VMEM_LIMIT),
    )(lhs, rhs)


def _mixer_bwd(dmix, proj, qrb, krb, oraw, tables, rst, sst, gw_pad, gb, rnw, gnw, after):
    t = proj.shape[0]
    tc = min(MIX_TILE, t)
    tr, tg = min(RET_SUB, tc), min(GLA_SUB, tc)
    nsteps = t // tc
    scale_r = RET_D ** -0.5
    scale_g = GLA_DK ** -0.5
    gammas = _tile_gammas(tr)

    def body(dmix_ref, qrb_ref, krb_ref, rv_ref, rg_ref, gq_ref, gk_ref, gv_ref, gg_ref, glr_ref, oraw_ref,
             dec_ref, qkd_ref, rot_in_ref, rot_tile_ref, rst_ref, sst_ref, gw_ref, gb_ref, rnw_ref, gnw_ref, after_ref,
             dproj_ref, dgw_ref, dvec_ref, dr_scr, ds_scr):
        @pl.when(pl.program_id(0) == 0)
        def _():
            dr_scr[...] = jnp.zeros_like(dr_scr)
            ds_scr[...] = jnp.zeros_like(ds_scr)
            dgw_ref[...] = jnp.zeros_like(dgw_ref)
            dvec_ref[...] = jnp.zeros_like(dvec_ref)

        gla_k = _gla_consts(tg)
        last_row = lax.broadcasted_iota(jnp.int32, (tg, GLA_KW), 0) == tg - 1

        def ret_tile(jj, carry):
            j = tc // tr - 1 - jj
            rows = _tile_rows(j, tr)
            cosv, sinv = _tile_rotary(rot_in_ref, rot_tile_ref, j)
            for h in range(RET_HEADS):
                cols = slice(h * RET_D, (h + 1) * RET_D)
                o = oraw_ref[rows, cols]
                g = rg_ref[rows, cols]
                w = rnw_ref[:, cols]
                dout = dmix_ref[rows, cols]
                oc = o - jnp.mean(o, axis=-1, keepdims=True)
                inv = lax.rsqrt(jnp.mean(oc * oc, axis=-1, keepdims=True) + LN_EPS)
                n = oc * inv
                sg = _sigmoid(g)
                sil = g * sg
                dn = dout * w * sil
                dvec_ref[0:1, cols] += jnp.sum(dout * n * sil, axis=0, keepdims=True)
                dproj_ref[rows, OFF_RG + h * RET_D:OFF_RG + (h + 1) * RET_D] = (
                    dout * n * w * (sg * (1.0 + g * (1.0 - sg)))).astype(BF16)
                doc = inv * (dn - n * jnp.mean(dn * n, axis=-1, keepdims=True))
                do = doc - jnp.mean(doc, axis=-1, keepdims=True)

                qb, kb = qrb_ref[rows, cols], krb_ref[rows, cols]
                qr, kr = qb.astype(F32), kb.astype(F32)
                vb = rv_ref[rows, cols].astype(BF16)
                dob = do.astype(BF16)
                qd, kd = qkd_ref[h], qkd_ref[RET_HEADS + h]
                p = _dot(qb, kb, NT) * dec_ref[h]
                rp = rst_ref[j, cols, :].astype(BF16)
                dr = dr_scr[cols, :]
                drb = dr.astype(BF16)
                dpb = (_dot(dob, vb, NT) * dec_ref[h]).astype(BF16)
                dqr = _dot(dpb, kb) + _dot(dob, rp, NT) * qd
                dkr = _dot(dpb, qb, TN) + _dot(vb, drb, NT) * kd
                dv = _dot(p.astype(BF16), dob, TN) + _dot((kr * kd).astype(BF16), drb)
                dr_scr[cols, :] = gammas[h] * dr + _dot((qr * qd).astype(BF16), dob, TN)
                dproj_ref[rows, OFF_RQ + h * RET_D:OFF_RQ + (h + 1) * RET_D] = (
                    _rotate_t(dqr, cosv, sinv) * scale_r).astype(BF16)
                dproj_ref[rows, OFF_RK + h * RET_D:OFF_RK + (h + 1) * RET_D] = _rotate_t(dkr, cosv, sinv).astype(BF16)
                dproj_ref[rows, OFF_RV + h * RET_D:OFF_RV + (h + 1) * RET_D] = dv.astype(BF16)
            return carry

        def gla_tile(jj, carry):
            k = gla_k
            tl = tg
            j = tc // tg - 1 - jj
            rows = _tile_rows(j, tg)
            glr = glr_ref[rows, :]
            z, b, bl, ep, em = _gla_gates(glr, gw_ref[...], gb_ref[...], k["ltri"], tl)
            qs = gq_ref[rows, :] * scale_g
            kk = gk_ref[rows, :]
            eb = jnp.exp(b)
            ekb = jnp.exp(bl - b)
            ebl = jnp.exp(bl)
            ql, qu, kl, ku = qs * ep, qs * em, kk * em, kk * ep
            qg, kg = qs * eb, kk * ekb
            qlm = _stack_heads(ql, k["hmask"]).astype(BF16)
            qum = _stack_heads(qu, k["hmask"]).astype(BF16)
            klb, kub = kl.astype(BF16), ku.astype(BF16)
            a_all = jnp.where(k["lower"], _dot(qlm, klb, NT),
                              jnp.where(k["upper"], _dot(qum, kub, NT), 0.0)).astype(BF16)
            st = sst_ref[j]
            stb = st.astype(BF16)
            ds = ds_scr[...]
            dsb = ds.astype(BF16)
            ds_new = ds * ebl
            da_parts = []
            dqg = jnp.zeros((tl, GLA_KW), F32)
            dkg = jnp.zeros((tl, GLA_KW), F32)
            for h in range(GLA_HEADS):
                cols = slice(h * GLA_DV, (h + 1) * GLA_DV)
                hr = slice(h * tl, (h + 1) * tl)
                ocols = slice(RET_W + h * GLA_DV, RET_W + (h + 1) * GLA_DV)
                o = oraw_ref[rows, ocols]
                g = gg_ref[rows, cols]
                w = gnw_ref[:, cols]
                dout = dmix_ref[rows, ocols]
                inv = lax.rsqrt(jnp.mean(o * o, axis=-1, keepdims=True) + LN_EPS)
                n = o * inv
                sg = _sigmoid(g)
                sil = g * sg
                dn = dout * w * sil
                dvec_ref[1:2, cols] += jnp.sum(dout * n * sil, axis=0, keepdims=True)
                dproj_ref[rows, OFF_GG + h * GLA_DV:OFF_GG + (h + 1) * GLA_DV] = (
                    dout * n * w * (sg * (1.0 + g * (1.0 - sg)))).astype(BF16)
                dob = (inv * (dn - n * jnp.mean(dn * n, axis=-1, keepdims=True))).astype(BF16)
                vb = gv_ref[rows, cols].astype(BF16)
                mh = k["hmask"][h]
                da_parts.append(_dot(dob, vb, NT))
                dv = _dot(a_all[hr, :], dob, TN) + _dot((kg * mh).astype(BF16), dsb, NT)
                dproj_ref[rows, OFF_GV + h * GLA_DV:OFF_GV + (h + 1) * GLA_DV] = dv.astype(BF16)
                dkg = dkg + mh * _dot(vb, dsb)
                dqg = dqg + mh * _dot(dob, stb)
                ds_new = ds_new + _dot(dob, (qg * mh).astype(BF16), TN)
            da_all = jnp.concatenate(da_parts, axis=0)
            dal = jnp.where(k["lower"], da_all, 0.0).astype(BF16)
            dau = jnp.where(k["upper"], da_all, 0.0).astype(BF16)
            dqlm = _dot(dal, klb)
            dqum = _dot(dau, kub)
            dql = jnp.zeros((tl, GLA_KW), F32)
            dqu = jnp.zeros((tl, GLA_KW), F32)
            for h in range(GLA_HEADS):
                hr = slice(h * tl, (h + 1) * tl)
                dql = dql + k["hmask"][h] * dqlm[hr, :]
                dqu = dqu + k["hmask"][h] * dqum[hr, :]
            dkl = _dot(dal, qlm, TN)
            dku = _dot(dau, qum, TN)
            dbl = (jnp.sum(dkg * kg, axis=0, keepdims=True)
                   + jnp.sum(ds * st, axis=0, keepdims=True) * ebl)
            ds_scr[...] = ds_new
            dqs = dql * ep + dqu * em + dqg * eb
            dk = dkl * em + dku * ep + dkg * ekb
            db = dql * ql - dkl * kl - dqu * qu + dku * ku + dqg * qg - dkg * kg
            db = db + jnp.where(last_row, dbl, 0.0)
            dla = _dot_split(k["utri"], db, NN, a_exact=True)
            dz = dla * (1.0 / GATE_TAU) * _sigmoid(-z)
            dvec_ref[2:3, 0:GLA_KW] += jnp.sum(dz, axis=0, keepdims=True)
            dgw_ref[...] += _dot_split(glr, dz, TN)
            dproj_ref[rows, OFF_GLR:D_IN_PAD] = _dot(dz.astype(BF16), gw_ref[...].astype(BF16), NT).astype(BF16)
            dproj_ref[rows, OFF_GQ:OFF_GQ + GLA_KW] = (dqs * scale_g).astype(BF16)
            dproj_ref[rows, OFF_GK:OFF_GK + GLA_KW] = dk.astype(BF16)
            return carry

        _for_tiles(tc // tr, ret_tile)
        _for_tiles(tc // tg, gla_tile)

    rev = lambda i: (nsteps - 1 - i, 0)

    def col(width, off):
        return pl.BlockSpec((tc, width), lambda i, o=off // width: (nsteps - 1 - i, o))

    fix = lambda i: (0, 0)
    fix3 = lambda i: (0, 0, 0)
    dec, qkd, rot_in, rot_tile = tables
    half = pl.BlockSpec((tc, RET_W), rev)
    in_specs = [pl.BlockSpec((tc, D_MODEL), rev), half, half, col(RET_W, OFF_RV), col(RET_W, OFF_RG),
                col(GLA_KW, OFF_GQ), col(GLA_KW, OFF_GK), col(GLA_VW, OFF_GV), col(GLA_VW, OFF_GG),
                col(V7X_LANES, OFF_GLR),
                pl.BlockSpec((tc, D_MODEL), rev),
                pl.BlockSpec(dec.shape, fix3), pl.BlockSpec(qkd.shape, fix3), pl.BlockSpec(rot_in.shape, fix3),
                pl.BlockSpec((tc // tr, 8, 2 * RET_D), lambda i: (nsteps - 1 - i, 0, 0)),
                pl.BlockSpec((tc // tr, RET_W, RET_D), lambda i: (nsteps - 1 - i, 0, 0)),
                pl.BlockSpec((tc // tg, GLA_DV, GLA_KW), lambda i: (nsteps - 1 - i, 0, 0)),
                pl.BlockSpec((V7X_LANES, GLA_KW), fix), pl.BlockSpec((1, GLA_KW), fix),
                pl.BlockSpec((1, RET_W), fix), pl.BlockSpec((1, GLA_VW), fix), pl.BlockSpec(memory_space=pl.ANY)]
    out_specs = (pl.BlockSpec((tc, D_IN_PAD), rev), pl.BlockSpec((V7X_LANES, GLA_KW), fix),
                 pl.BlockSpec((8, RET_W), fix))
    out_shape = (jax.ShapeDtypeStruct((t, D_IN_PAD), BF16), jax.ShapeDtypeStruct((V7X_LANES, GLA_KW), F32),
                 jax.ShapeDtypeStruct((8, RET_W), F32))
    return pl.pallas_call(
        body, name="mixer_bwd", grid=(nsteps,), in_specs=in_specs, out_specs=out_specs, out_shape=out_shape,
        scratch_shapes=[pltpu.VMEM((RET_W, RET_D), F32), pltpu.VMEM((GLA_DV, GLA_KW), F32)],
        compiler_params=pltpu.CompilerParams(dimension_semantics=("arbitrary",), vmem_limit_bytes=V7X_VMEM_LIMIT),
    )(dmix, qrb, krb, *([proj] * 7), oraw, dec, qkd, rot_in, rot_tile, rst, sst, gw_pad, gb, rnw, gnw, after)


def _inproj_bwd(dproj, x2d, dxa, sc1p, w_in_t, after):
    t = x2d.shape[0]
    tm = min(2 * PROJ_TILE, t)

    def body(dp_ref, x_ref, dxa_ref, sc_ref, w_hbm, after_ref, gx_ref, acc_ref, w_vmem, sem):
        first = pl.program_id(0) == 0
        _load_w_in_t(first, w_hbm, w_vmem, sem)

        @pl.when(first)
        def _():
            acc_ref[...] = jnp.zeros_like(acc_ref)

        du = _dot(dp_ref[...], w_vmem[...])
        xh, rstd = _ln_stats(x_ref[...])
        gx_ref[...] = dxa_ref[...] + _ln_bwd(du * sc_ref[...], xh, rstd)
        acc_ref[0:1, :] += jnp.sum(du * xh, axis=0, keepdims=True)
        acc_ref[1:2, :] += jnp.sum(du, axis=0, keepdims=True)

    row = lambda i: (i, 0)
    fix = lambda i: (0, 0)
    return pl.pallas_call(
        body, name="inproj_bwd", grid=(t // tm,),
        in_specs=[pl.BlockSpec((tm, D_IN_PAD), row), pl.BlockSpec((tm, D_MODEL), row), pl.BlockSpec((tm, D_MODEL), row),
                  pl.BlockSpec((1, D_MODEL), fix), pl.BlockSpec(memory_space=pl.ANY), pl.BlockSpec(memory_space=pl.ANY)],
        out_specs=(pl.BlockSpec((tm, D_MODEL), row), pl.BlockSpec((8, D_MODEL), fix)),
        out_shape=(jax.ShapeDtypeStruct((t, D_MODEL), F32), jax.ShapeDtypeStruct((8, D_MODEL), F32)),
        scratch_shapes=[pltpu.VMEM((D_IN_PAD, D_MODEL), BF16), pltpu.SemaphoreType.DMA((1,))],
        compiler_params=pltpu.CompilerParams(dimension_semantics=("arbitrary",), vmem_limit_bytes=V7X_VMEM_LIMIT),
    )(dproj, x2d, dxa, sc1p, w_in_t, after)


def _adam_math(w, g, m, v):
    m = ADAM_B1 * m + (1.0 - ADAM_B1) * g
    v = ADAM_B2 * v + (1.0 - ADAM_B2) * (g * g)
    m_hat = m / (1.0 - ADAM_B1 ** ADAM_STEP)
    v_hat = v / (1.0 - ADAM_B2 ** ADAM_STEP)
    delta = -ADAM_LR * (m_hat / (jnp.sqrt(v_hat) + ADAM_EPS) + ADAM_WD * w)
    return delta, m, v


def _adamw(w, gparts, m, v, name, row_tiles=False):
    nparts, rows, cols = gparts.shape
    tr = rows
    for cand in (512, 256, 128, 64, 32, 16, 8):
        if rows % cand == 0:
            tr = cand
            break

    def body(w_ref, g_ref, m_ref, v_ref, go_ref, d_ref, mo_ref, vo_ref):
        g = g_ref[0].astype(F32)
        for p in range(1, nparts):
            g = g + g_ref[p].astype(F32)
        if row_tiles:
            g = g.reshape(tr, cols // V7X_LANES, V7X_LANES)
        delta, mn, vn = _adam_math(w_ref[...], g, m_ref[...], v_ref[...])
        go_ref[...] = g
        d_ref[...] = delta
        mo_ref[...] = mn
        vo_ref[...] = vn

    blk = pl.BlockSpec((tr, cols), lambda i: (i, 0))
    shp = jax.ShapeDtypeStruct((rows, cols), F32)
    if row_tiles:
        blk = pl.BlockSpec((tr, cols // V7X_LANES, V7X_LANES), lambda i: (i, 0, 0))
        shp = jax.ShapeDtypeStruct((rows, cols // V7X_LANES, V7X_LANES), F32)
    return pl.pallas_call(
        body, name=name, grid=(rows // tr,),
        in_specs=[blk, pl.BlockSpec((nparts, tr, cols), lambda i: (0, i, 0)), blk, blk],
        out_specs=(blk, blk, blk, blk), out_shape=(shp, shp, shp, shp),
        compiler_params=pltpu.CompilerParams(dimension_semantics=("arbitrary",), vmem_limit_bytes=V7X_VMEM_LIMIT),
    )(w, gparts, m, v)


def _small_reduce(gathered, gathered_gw, c_all, dmod_cols):
    def body(g_ref, gw_ref, c_ref, dm_ref, sum_ref, gwsum_ref, gb_ref, gwa_ref):
        s = g_ref[0]
        sw = gw_ref[0]
        for p in range(1, N_DEV):
            s = s + g_ref[p]
            sw = sw + gw_ref[p]
        sum_ref[...] = s
        gwsum_ref[...] = sw
        for i in range(6):
            gb_ref[:, i * D_MODEL:(i + 1) * D_MODEL] = s[i:i + 1, :]
        cc = c_ref[...]
        gwa_ref[...] = _dot(cc * _sigmoid(cc), dm_ref[...], TN, HIGHEST)

    vm = pl.BlockSpec(memory_space=pltpu.VMEM)
    return pl.pallas_call(
        body, name="small_reduce",
        out_shape=(jax.ShapeDtypeStruct(gathered.shape[1:], F32), jax.ShapeDtypeStruct(gathered_gw.shape[1:], F32),
                   jax.ShapeDtypeStruct((1, 6 * D_MODEL), F32), jax.ShapeDtypeStruct((D_MODEL, ADA_COLS), F32)),
        in_specs=[vm] * 4, out_specs=(vm, vm, vm, vm),
        compiler_params=pltpu.CompilerParams(vmem_limit_bytes=V7X_VMEM_LIMIT),
    )(gathered, gathered_gw, c_all, dmod_cols)


SMR_LN1W, SMR_LN1B, SMR_LN2W, SMR_LN2B, SMR_NORMS, SMR_MISC = 6, 7, 8, 9, 10, 11


def _adamw_small(gsum, g_b_ada, g_ggw, params, moms, vels):
    n = len(params)

    def body(*refs):
        gsum_ref, gb_ref, gw_ref = refs[:3]
        w_refs, m_refs, v_refs = refs[3:3 + n], refs[3 + n:3 + 2 * n], refs[3 + 2 * n:3 + 3 * n]
        outs = refs[3 + 3 * n:]
        g_refs, d_refs, mo_refs, vo_refs = outs[:n - 1], outs[n - 1:2 * n - 1], outs[2 * n - 1:3 * n - 1], outs[3 * n - 1:]
        grads = [gb_ref[...],
                 gsum_ref[SMR_NORMS:SMR_NORMS + 1, 0:RET_W],
                 gsum_ref[SMR_MISC:SMR_MISC + 1, 0:GLA_KW],
                 gsum_ref[SMR_NORMS:SMR_NORMS + 1, RET_W:RET_W + GLA_VW],
                 gsum_ref[SMR_LN1W:SMR_LN1W + 1, :], gsum_ref[SMR_LN1B:SMR_LN1B + 1, :],
                 gsum_ref[SMR_LN2W:SMR_LN2W + 1, :], gsum_ref[SMR_LN2B:SMR_LN2B + 1, :],
                 gw_ref[...]]
        for i in range(n):
            delta, mn, vn = _adam_math(w_refs[i][...], grads[i], m_refs[i][...], v_refs[i][...])
            if i < n - 1:
                g_refs[i][...] = grads[i]
            d_refs[i][...] = delta
            mo_refs[i][...] = mn
            vo_refs[i][...] = vn

    vm = pl.BlockSpec(memory_space=pltpu.VMEM)
    shapes = [jax.ShapeDtypeStruct(p.shape, F32) for p in params]
    n_in = 3 + 3 * n
    out_shape = tuple(shapes[:n - 1] + shapes * 3)
    return pl.pallas_call(
        body, name="adamw_small", out_shape=out_shape,
        in_specs=[vm] * n_in, out_specs=tuple([vm] * len(out_shape)),
        compiler_params=pltpu.CompilerParams(vmem_limit_bytes=V7X_VMEM_LIMIT),
    )(gsum, g_b_ada, g_ggw, *params, *moms, *vels)


def kernel(x, c, w_ada, b_ada, w_in, ret_norm_w, gla_gate_w, gla_gate_b, gla_norm_w, w_out, ln1_w, ln1_b, w_ff1, w_ff2, ln2_w, ln2_b, loss_target, m_w_ada, m_b_ada, m_w_in, m_ret_norm_w, m_gla_gate_w, m_gla_gate_b, m_gla_norm_w, m_w_out, m_ln1_w, m_ln1_b, m_w_ff1, m_w_ff2, m_ln2_w, m_ln2_b, v_w_ada, v_b_ada, v_w_in, v_ret_norm_w, v_gla_gate_w, v_gla_gate_b, v_gla_norm_w, v_w_out, v_ln1_w, v_ln1_b, v_w_ff1, v_w_ff2, v_ln2_w, v_ln2_b):
    t = x.shape[1]
    xi, yi, ci = _my_coords()
    me = 4 * xi + 2 * yi + ci
    x2d = x[0]
    tgt = loss_target[0]

    c_ext = jnp.concatenate([c, gla_gate_w[0].reshape(1, GATE_RANK * GLA_KW // N_DEV)], axis=1)
    b_l = lax.dynamic_slice(b_ada, (0, me * ADA_COLS), (1, ADA_COLS))
    c_all3, mod_all, wi_g, ada_token = _adaln_mod(c_ext, w_ada[0], b_l, w_in[0].T.astype(BF16))

    wg = _exchange_start([w_out[0].astype(BF16), w_ff1[0].astype(BF16), w_ff2[0].astype(BF16)],
                         True, "wgather_start", after=ada_token)

    c_all = c_all3[:, 0, :D_MODEL]
    gate_w = c_all3[:, 0, D_MODEL:].reshape(N_DEV, GATE_RANK, GLA_KW // N_DEV)
    gate_w = gate_w.transpose(1, 0, 2).reshape(GATE_RANK, GLA_KW)
    gw_pad = jnp.zeros((V7X_LANES, GLA_KW), F32).at[:GATE_RANK].set(gate_w)
    mod = lax.dynamic_slice(mod_all, (0, me, 0), (N_DEV, 1, ADA_COLS)).reshape(6, D_MODEL)
    shift1, scale1, gate1, shift2, scale2, gate2 = [mod[i:i + 1] for i in range(6)]

    w_in_t = wi_g.reshape(D_IN, D_MODEL)

    tables = _ret_tables(t, min(RET_SUB, t))

    sc1p = 1.0 + scale1
    proj, u = _inproj_fwd(x2d, sc1p, shift1, w_in_t, after=wg[4])
    mixed, oraw, qrb, krb, rst, sst = _mixer_fwd(proj, tables, gw_pad, gla_gate_b, ret_norm_w, gla_norm_w)
    wo_g, w1_b, w2_g = _exchange_wait(*wg[:4], mixed, True, "wgather_wait")
    w_out_b = wo_g.reshape(D_MODEL, D_MODEL)
    w2_b = w2_g.reshape(D_FF, D_MODEL)
    vec_f = jnp.concatenate([gate1, 1.0 + scale2, shift2, gate2, ln1_w, ln1_b, ln2_w, ln2_b], axis=0)
    m, x1n, rstd1, u2, a, df, dh2, acc_f = _mid_fwd(mixed, x2d, tgt, vec_f, w_out_b, w1_b, w2_b)

    vec_b = jnp.concatenate([gate1, 1.0 + scale2, ln1_w, ln1_b, jnp.zeros((4, D_MODEL), F32)], axis=0)
    da, dm, dmix, dxa, acc_b = _ffn_bwd(df, a, dh2, x1n, rstd1, m, vec_b, w_out_b, w1_b, w2_b)
    dw2 = _matmul_tn(a, df, 2048, 1024, 2048, "tn_dw2", relu_sq=True)
    dw1 = _matmul_tn(u2, da, 1024, 2048, 2048, "tn_dw1", col_slab=FF_COLS)
    dwo = _matmul_tn(mixed, dm, 1024, 1024, 2048, "tn_dwout")
    gx = _exchange_start([dwo.reshape(N_DEV, OUT_ROWS, D_MODEL), dw1, dw2.reshape(N_DEV, FF_COLS, D_MODEL)], False,
                         "gradx_start")
    dproj, dgw, dvec = _mixer_bwd(dmix, proj, qrb, krb, oraw, tables, rst, sst, gw_pad,
                                  gla_gate_b, ret_norm_w, gla_norm_w, after=gx[4])
    dwi_s = _matmul_tn(dproj, u, D_IN_PAD, 1024, 1024, "tn_dwin", out_rows=IN_COLS)
    gi = _exchange_start([dwi_s], False, "gradin_start")
    grad_x, acc_i = _inproj_bwd(dproj, x2d, dxa, sc1p, w_in_t, after=gi[4])

    loss_part = jnp.sum(acc_f[3])
    small = jnp.concatenate([
        acc_i[1:2], acc_i[0:1], acc_b[4:5], acc_b[1:2], acc_b[0:1], acc_f[2:3],
        acc_b[2:3], acc_b[3:4], acc_f[0:1], acc_f[1:2],
        jnp.concatenate([dvec[0:1], dvec[1:2]], axis=1),
        jnp.concatenate([dvec[2:3, :GLA_KW], jnp.full((1, 128), loss_part, F32),
                         jnp.zeros((1, D_MODEL - GLA_KW - 128), F32)], axis=1),
        jnp.zeros((4, D_MODEL), F32)], axis=0)
    sg = _exchange_start([small, dgw[:GATE_RANK]], True, "small_start")

    r_wo, r_w1, r_w2 = _exchange_wait(*gx[:4], sg[4], False, "gradx_wait")
    r_wi, = _exchange_wait(*gi[:4], sg[4], False, "gradin_wait")
    big = [_adamw(w[0], r, m_[0], v_[0], nm) for w, r, m_, v_, nm in (
        (w_out, r_wo, m_w_out, v_w_out, "adamw_out"),
        (w_ff1, r_w1, m_w_ff1, v_w_ff1, "adamw_ff1"), (w_ff2, r_w2, m_w_ff2, v_w_ff2, "adamw_ff2"))]
    tiles = lambda a: a.T.reshape(IN_COLS, D_MODEL // V7X_LANES, V7X_LANES)
    big_in = _adamw(tiles(w_in[0]), r_wi, tiles(m_w_in[0]), tiles(v_w_in[0]), "adamw_in", row_tiles=True)
    big = [tuple(b.reshape(IN_COLS, D_MODEL).T for b in big_in)] + big
    g_big, d_big, m_big, v_big = [[b[i][None] for b in big] for i in range(4)]

    small_all, gw_all = _exchange_wait(*sg[:4], big_in[1], True, "small_wait")
    dmod_all = small_all[:, :6].reshape(N_DEV, 6 * D_MODEL)
    dmod_cols = lax.dynamic_slice(dmod_all, (0, me * ADA_COLS), (N_DEV, ADA_COLS))
    ssum, gw_sum, g_b_ada, g_w_ada = _small_reduce(small_all, gw_all, c_all, dmod_cols)
    loss = ssum[SMR_MISC, GLA_KW]
    g_ggw = lax.dynamic_slice(gw_sum, (0, me * (GLA_KW // N_DEV)), (GATE_RANK, GLA_KW // N_DEV))[None]

    small_w = [b_ada, ret_norm_w, gla_gate_b, gla_norm_w, ln1_w, ln1_b, ln2_w, ln2_b, gla_gate_w]
    small_m = [m_b_ada, m_ret_norm_w, m_gla_gate_b, m_gla_norm_w, m_ln1_w, m_ln1_b, m_ln2_w, m_ln2_b, m_gla_gate_w]
    small_v = [v_b_ada, v_ret_norm_w, v_gla_gate_b, v_gla_norm_w, v_ln1_w, v_ln1_b, v_ln2_w, v_ln2_b, v_gla_gate_w]
    res = _adamw_small(ssum, g_b_ada, g_ggw, small_w, small_m, small_v)
    small_g = list(res[:8]) + [g_ggw]
    d_small, m_small, v_small = list(res[8:17]), list(res[17:26]), list(res[26:35])

    _, d_w_ada, nm_w_ada, nv_w_ada = _adamw(w_ada[0], g_w_ada[None], m_w_ada[0], v_w_ada[0], "adamw_ada")

    def ordered(w_ada_v, small_vals, big_vals):
        b_ada_v, rnw_v, ggb_v, gnw_v, l1w_v, l1b_v, l2w_v, l2b_v, ggw_v = small_vals
        wi_v, wo_v, w1_v, w2_v = big_vals
        return [w_ada_v, b_ada_v, wi_v, rnw_v, ggw_v, ggb_v, gnw_v, wo_v, l1w_v, l1b_v, w1_v, w2_v, l2w_v, l2b_v]

    grads = ordered(g_w_ada[None], small_g, g_big)
    deltas = ordered(d_w_ada[None], d_small, d_big)
    new_m = ordered(nm_w_ada[None], m_small, m_big)
    new_v = ordered(nv_w_ada[None], v_small, v_big)
    return (loss, grad_x[None], *grads, *deltas, *new_m, *new_v)
```

```python
import numpy as np
import jax
import jax.numpy as jnp
from jax import lax
from jax.experimental import pallas as pl
from jax.experimental.pallas import tpu as pltpu

F32 = jnp.float32
BF16 = jnp.bfloat16
MESH = pl.DeviceIdType.MESH
HIGHEST = lax.Precision.HIGHEST

N_DEV = 8
D_MODEL = 1024
CHUNK = 64
RET_HEADS = 4
RET_D = 128
GLA_HEADS = 4
GLA_DK = 64
GLA_DV = 128
GLA_KW = GLA_HEADS * GLA_DK
RET_W = RET_HEADS * RET_D
GLA_VW = GLA_HEADS * GLA_DV
V7X_LANES = 128
GATE_RANK = 16
GATE_TAU = 16.0
D_FF = 4096
LN_EPS = 1e-5
ALPHA = (2.0 * 1) ** 0.25
D_IN = 3600
D_IN_PAD = 3712
ADA_COLS = 6 * D_MODEL // N_DEV
IN_COLS = D_IN // N_DEV
FF_COLS = D_FF // N_DEV
OUT_ROWS = D_MODEL // N_DEV

OFF_RQ, OFF_RK, OFF_RV, OFF_RG = 0, RET_W, 2 * RET_W, 3 * RET_W
OFF_GQ = 4 * RET_W
OFF_GK = OFF_GQ + GLA_KW
OFF_GV = OFF_GK + GLA_KW
OFF_GG = OFF_GV + GLA_VW
OFF_GLR = OFF_GG + GLA_VW

ADAM_LR, ADAM_B1, ADAM_B2, ADAM_EPS, ADAM_WD, ADAM_STEP = 0.001, 0.9, 0.999, 1e-08, 0.01, 10

V7X_VMEM_LIMIT = 62 * 1024 * 1024

ROW_TILE = 512
PROJ_TILE = 512
MIX_TILE = 512
RET_SUB = 256
GLA_SUB = 128


def _log_gamma(h):
    return float(np.log(np.float32(1.0) - np.float32(2.0) ** np.float32(-5.0 - h)))


def _my_coords():
    return lax.axis_index("x"), lax.axis_index("y"), lax.axis_index("c")


def _flip(v, bit):
    return 1 - v if bit else v


def _peer(k):
    x, y, c = _my_coords()
    px, py, pc = _flip(x, (k >> 2) & 1), _flip(y, (k >> 1) & 1), _flip(c, k & 1)
    return (px, py, pc), 4 * px + 2 * py + pc


def _dot(a, b, dims=(((1,), (0,)), ((), ())), precision=None):
    return lax.dot_general(a, b, dims, precision=precision, preferred_element_type=F32)


NN = (((1,), (0,)), ((), ()))
NT = (((1,), (1,)), ((), ()))
TN = (((0,), (0,)), ((), ()))


def _split_bf16(v, parts):
    out = []
    for _ in range(parts):
        p = v.astype(BF16)
        out.append(p)
        v = v - p.astype(F32)
    return out


def _dot_split(a, b, dims, a_exact=False):
    if a_exact:
        ab = a.astype(BF16)
        return sum(_dot(ab, p, dims) for p in _split_bf16(b, 2))
    a_hi, a_lo = _split_bf16(a, 2)
    b_hi, b_lo = _split_bf16(b, 2)
    return _dot(a_hi, b_hi, dims) + _dot(a_hi, b_lo, dims) + _dot(a_lo, b_hi, dims)


def _sigmoid(x):
    return 1.0 / (1.0 + jnp.exp(-x))


def _ln_stats(x):
    mu = jnp.mean(x, axis=-1, keepdims=True)
    xc = x - mu
    var = jnp.mean(xc * xc, axis=-1, keepdims=True)
    rstd = lax.rsqrt(var + LN_EPS)
    return xc * rstd, rstd


def _ln_bwd(dyh, xh, rstd):
    return rstd * (dyh - jnp.mean(dyh, axis=-1, keepdims=True) - xh * jnp.mean(dyh * xh, axis=-1, keepdims=True))


def _adaln_mod(c_ext, w_ada_l, b_l, w_in_l):
    width = c_ext.shape[1]

    def body(c_ref, w_ref, b_ref, wi_ref, call_ref, mod_ref, wig_ref, token_ref, s1, r1, s2, r2, gs, gr, gl):
        gather = _TwoLevelGather([wi_ref], [wig_ref], gs, gr, gl)
        gather.start()
        token_ref[...] = jnp.zeros_like(token_ref)
        x, y, c = _my_coords()
        me = 4 * x + 2 * y + c
        call_ref[me] = c_ref[...]
        sends = []
        for k in range(1, N_DEV):
            peer, _ = _peer(k)
            cp = pltpu.make_async_remote_copy(c_ref, call_ref.at[me], s1.at[k - 1], r1.at[k - 1],
                                              device_id=peer, device_id_type=MESH)
            cp.start()
            sends.append(cp)
        for k in range(1, N_DEV):
            peer, pid = _peer(k)
            pltpu.make_async_remote_copy(c_ref, call_ref.at[pid], s1.at[k - 1], r1.at[k - 1],
                                         device_id=peer, device_id_type=MESH).wait_recv()
        for cp in sends:
            cp.wait_send()
        row = lax.broadcasted_iota(jnp.int32, (N_DEV, D_MODEL), 0)
        call = jnp.zeros((N_DEV, D_MODEL), F32)
        for j in range(N_DEV):
            call = jnp.where(row == j, jnp.broadcast_to(call_ref[j][:, :D_MODEL], (N_DEV, D_MODEL)), call)
        sc = call * _sigmoid(call)
        mod = _dot(sc, w_ref[...], NN, HIGHEST) + b_ref[...]
        mod_ref[me] = mod
        sends = []
        for k in range(1, N_DEV):
            peer, _ = _peer(k)
            cp = pltpu.make_async_remote_copy(mod_ref.at[me], mod_ref.at[me], s2.at[k - 1], r2.at[k - 1],
                                              device_id=peer, device_id_type=MESH)
            cp.start()
            sends.append(cp)
        for k in range(1, N_DEV):
            peer, pid = _peer(k)
            pltpu.make_async_remote_copy(mod_ref.at[pid], mod_ref.at[pid], s2.at[k - 1], r2.at[k - 1],
                                         device_id=peer, device_id_type=MESH).wait_recv()
        for cp in sends:
            cp.wait_send()
        gather.forward()
        gather.finish()

    vm = pl.BlockSpec(memory_space=pltpu.VMEM)
    hbm = pl.BlockSpec(memory_space=pl.ANY)
    return pl.pallas_call(
        body, name="adaln_mod",
        out_shape=(jax.ShapeDtypeStruct((N_DEV, 1, width), F32),
                   jax.ShapeDtypeStruct((N_DEV, N_DEV, ADA_COLS), F32),
                   jax.ShapeDtypeStruct((N_DEV, *w_in_l.shape), w_in_l.dtype),
                   jax.ShapeDtypeStruct((8, 128), F32)),
        in_specs=[vm, vm, vm, hbm], out_specs=(vm, vm, hbm, vm),
        scratch_shapes=[pltpu.SemaphoreType.DMA((N_DEV - 1,))] * 4
        + [pltpu.SemaphoreType.DMA((7,)), pltpu.SemaphoreType.DMA((7,)), pltpu.SemaphoreType.DMA((1,))],
        compiler_params=pltpu.CompilerParams(vmem_limit_bytes=V7X_VMEM_LIMIT),
    )(c_ext, w_ada_l, b_l, w_in_l)


class _TwoLevelGather:
    def __init__(self, x_refs, out_refs, send_sems, recv_sems, local_sems):
        self.x_refs, self.out_refs = x_refs, out_refs
        self.send_sems, self.recv_sems, self.local_sems = send_sems, recv_sems, local_sems
        x, y, c = _my_coords()
        self.c = c
        self.me, self.sibling = (x, y, c), (x, y, 1 - c)
        self.chips = [(1 - x, y), (x, 1 - y), (1 - x, 1 - y)]

    def _copy(self, a, k, block, to, src=None):
        px, py, pc = block
        slab = self.out_refs[a].at[4 * px + 2 * py + pc]
        return pltpu.make_async_remote_copy(
            src_ref=slab if src is None else src, dst_ref=slab,
            send_sem=self.send_sems.at[7 * a + k], recv_sem=self.recv_sems.at[7 * a + k],
            device_id=to, device_id_type=MESH)

    def _mine(self, a):
        px, py, pc = self.me
        return pltpu.make_async_copy(self.x_refs[a], self.out_refs[a].at[4 * px + 2 * py + pc], self.local_sems.at[a])

    def _first(self, a):
        cps = [self._copy(a, 0, self.me, self.sibling, src=self.x_refs[a])]
        cps += [self._copy(a, 1 + j, self.me, (*chip, self.c), src=self.x_refs[a]) for j, chip in enumerate(self.chips)]
        return cps

    def _passed(self, a):
        return [self._copy(a, 4 + j, (*chip, self.c), self.sibling) for j, chip in enumerate(self.chips)]

    def start(self):
        for a in range(len(self.x_refs)):
            self._mine(a).start()
            for cp in self._first(a):
                cp.start()

    def forward(self):
        for a in range(len(self.x_refs)):
            passed = self._passed(a)
            for j, chip in enumerate(self.chips):
                self._copy(a, 1 + j, (*chip, self.c), self.me).wait_recv()
                passed[j].start()

    def arrive(self):
        for a in range(len(self.x_refs)):
            for j, chip in enumerate(self.chips):
                self._copy(a, 1 + j, (*chip, self.c), self.me).wait_recv()

    def pass_on(self):
        for a in range(len(self.x_refs)):
            for cp in self._passed(a):
                cp.start()

    def finish(self):
        for a in range(len(self.x_refs)):
            self._copy(a, 0, self.sibling, self.me).wait_recv()
            for j, chip in enumerate(self.chips):
                self._copy(a, 4 + j, (*chip, 1 - self.c), self.me).wait_recv()
            for cp in self._first(a) + self._passed(a):
                cp.wait_send()
            self._mine(a).wait()


def _exchange_copy(src_refs, land_refs, send_sems, recv_sems, a, k, gather, receiving):
    x, y, c = _my_coords()
    me = 4 * x + 2 * y + c
    peer, pid = _peer(k)
    src = src_refs[a] if gather else src_refs[a].at[pid]
    dst = land_refs[a].at[pid if receiving else me]
    return pltpu.make_async_remote_copy(src, dst, send_sems.at[7 * a + k - 1], recv_sems.at[7 * a + k - 1],
                                        device_id=peer, device_id_type=MESH)


def _own_copy(src_refs, land_refs, send_sems, a, n, gather):
    x, y, c = _my_coords()
    me = 4 * x + 2 * y + c
    src = src_refs[a] if gather else src_refs[a].at[me]
    return pltpu.make_async_copy(src, land_refs[a].at[me], send_sems.at[7 * n + a])


def _exchange_start(srcs, gather, name, after=None):
    n = len(srcs)
    land_shapes = [(N_DEV, *s.shape) if gather else s.shape for s in srcs]
    n_in = n if after is None else n + 1

    def body(*refs):
        src_refs, send_sems, recv_sems, token = refs[:n], refs[n_in], refs[n_in + 1], refs[-1]
        land_refs = refs[n_in + 2 + n:n_in + 2 + 2 * n]
        for a in range(n):
            _own_copy(src_refs, land_refs, send_sems, a, n, gather).start()
            for k in range(1, N_DEV):
                _exchange_copy(src_refs, land_refs, send_sems, recv_sems, a, k, gather, receiving=False).start()
        token[...] = jnp.zeros_like(token)

    hbm = pl.BlockSpec(memory_space=pltpu.HBM)
    sem = pl.BlockSpec(memory_space=pltpu.SEMAPHORE)
    res = pl.pallas_call(
        body, name=name,
        out_shape=(pltpu.SemaphoreType.DMA((8 * n,)), pltpu.SemaphoreType.DMA((7 * n,)),
                   *[pltpu.HBM(v.shape, v.dtype) for v in srcs],
                   *[pltpu.HBM(shape, v.dtype) for shape, v in zip(land_shapes, srcs)],
                   jax.ShapeDtypeStruct((8, 128), F32)),
        in_specs=[hbm] * n + [pl.BlockSpec(memory_space=pl.ANY)] * (n_in - n),
        out_specs=(sem, sem, *([hbm] * (2 * n)), pl.BlockSpec(memory_space=pltpu.VMEM)),
        input_output_aliases={i: 2 + i for i in range(n)},
        compiler_params=pltpu.CompilerParams(has_side_effects=pltpu.SideEffectType.DATAFLOW_SIDE_EFFECTING),
    )(*[pltpu.with_memory_space_constraint(v, pltpu.HBM) for v in srcs], *([] if after is None else [after]))
    return res[0], res[1], list(res[2:2 + n]), list(res[2 + n:2 + 2 * n]), res[-1]


def _exchange_wait(send_sems, recv_sems, srcs, lands, after, gather, name):
    n = len(srcs)

    def body(*refs):
        src_refs, land_refs, s_sems, r_sems = refs[:n], refs[n:2 * n], refs[2 * n], refs[2 * n + 1]
        for a in range(n):
            _own_copy(src_refs, land_refs, s_sems, a, n, gather).wait()
            for k in range(1, N_DEV):
                _exchange_copy(src_refs, land_refs, s_sems, r_sems, a, k, gather, receiving=False).wait_send()
                _exchange_copy(src_refs, land_refs, s_sems, r_sems, a, k, gather, receiving=True).wait_recv()

    hbm = pl.BlockSpec(memory_space=pltpu.HBM)
    sem = pl.BlockSpec(memory_space=pltpu.SEMAPHORE)
    res = pl.pallas_call(
        body, name=name,
        out_shape=tuple(pltpu.HBM(v.shape, v.dtype) for v in srcs + lands),
        in_specs=[hbm] * (2 * n) + [sem, sem, pl.BlockSpec(memory_space=pl.ANY)],
        out_specs=tuple([hbm] * (2 * n)),
        input_output_aliases={i: i for i in range(2 * n)},
        compiler_params=pltpu.CompilerParams(has_side_effects=pltpu.SideEffectType.DATAFLOW_SIDE_EFFECTING),
    )(*srcs, *lands, send_sems, recv_sems, after)
    return list(res[n:])


def _gather_phase(phase, srcs, lands, sems, after, name):
    n = len(srcs)
    first = phase == "start"

    def body(*refs):
        if first:
            src_refs, sem_refs, land_refs, token = refs[:n], refs[n + 1:n + 4], refs[2 * n + 4:3 * n + 4], refs[-1]
        else:
            src_refs, land_refs, sem_refs, token = refs[:n], refs[n:2 * n], refs[2 * n:2 * n + 3], refs[-1]
        gather = _TwoLevelGather(list(src_refs), list(land_refs), *sem_refs)
        {"start": gather.start, "arrive": gather.arrive, "pass_on": gather.pass_on, "finish": gather.finish}[phase]()
        token[...] = jnp.zeros_like(token)

    hbm = pl.BlockSpec(memory_space=pltpu.HBM)
    sem = pl.BlockSpec(memory_space=pltpu.SEMAPHORE)
    tok_spec, tok_shape = pl.BlockSpec(memory_space=pltpu.VMEM), jax.ShapeDtypeStruct((8, 128), F32)
    params = pltpu.CompilerParams(has_side_effects=pltpu.SideEffectType.DATAFLOW_SIDE_EFFECTING)
    if first:
        res = pl.pallas_call(
            body, name=name,
            out_shape=(pltpu.SemaphoreType.DMA((7 * n,)), pltpu.SemaphoreType.DMA((7 * n,)), pltpu.SemaphoreType.DMA((n,)),
                       *[pltpu.HBM(v.shape, v.dtype) for v in srcs],
                       *[pltpu.HBM((N_DEV, *v.shape), v.dtype) for v in srcs], tok_shape),
            in_specs=[hbm] * n + [pl.BlockSpec(memory_space=pl.ANY)],
            out_specs=(sem, sem, sem, *([hbm] * (2 * n)), tok_spec),
            input_output_aliases={i: 3 + i for i in range(n)},
            compiler_params=params,
        )(*[pltpu.with_memory_space_constraint(v, pltpu.HBM) for v in srcs], after)
        return list(res[3:3 + n]), list(res[3 + n:3 + 2 * n]), list(res[:3]), res[-1]
    res = pl.pallas_call(
        body, name=name,
        out_shape=(*[pltpu.HBM(v.shape, v.dtype) for v in srcs + lands], tok_shape),
        in_specs=[hbm] * (2 * n) + [sem] * 3 + [pl.BlockSpec(memory_space=pl.ANY)],
        out_specs=(*([hbm] * (2 * n)), tok_spec),
        input_output_aliases={i: i for i in range(2 * n)},
        compiler_params=params,
    )(*srcs, *lands, *sems, after)
    return list(res[:n]), list(res[n:2 * n]), sems, res[-1]


def _load_resident(step_is_first, pairs, sem):
    @pl.when(step_is_first)
    def _():
        copies = [pltpu.make_async_copy(src, dst, sem.at[i]) for i, (src, dst) in enumerate(pairs)]
        for cp in copies:
            cp.start()
        for cp in copies:
            cp.wait()


def _load_w_in_t(step_is_first, w_hbm, w_vmem, sem):
    @pl.when(step_is_first)
    def _():
        w_vmem[D_IN:, :] = jnp.zeros((D_IN_PAD - D_IN, D_MODEL), BF16)
    _load_resident(step_is_first, [(w_hbm, w_vmem.at[pl.ds(0, D_IN)])], sem)


def _inproj_fwd(x2d, sc1p, sh1, w_in_t, after):
    t = x2d.shape[0]
    tm = min(PROJ_TILE, t)

    def body(x_ref, sc_ref, sh_ref, w_hbm, after_ref, proj_ref, u_ref, w_vmem, sem):
        _load_w_in_t(pl.program_id(0) == 0, w_hbm, w_vmem, sem)
        xh, _ = _ln_stats(x_ref[...])
        ub = (xh * sc_ref[...] + sh_ref[...]).astype(BF16)
        u_ref[...] = ub
        proj_ref[...] = _dot(ub, w_vmem[...], NT)

    row = lambda i: (i, 0)
    fix = lambda i: (0, 0)
    return pl.pallas_call(
        body, name="inproj_fwd", grid=(t // tm,),
        in_specs=[pl.BlockSpec((tm, D_MODEL), row), pl.BlockSpec((1, D_MODEL), fix), pl.BlockSpec((1, D_MODEL), fix),
                  pl.BlockSpec(memory_space=pl.ANY), pl.BlockSpec(memory_space=pl.ANY)],
        out_specs=(pl.BlockSpec((tm, D_IN_PAD), row), pl.BlockSpec((tm, D_MODEL), row)),
        out_shape=(jax.ShapeDtypeStruct((t, D_IN_PAD), F32), jax.ShapeDtypeStruct((t, D_MODEL), BF16)),
        scratch_shapes=[pltpu.VMEM((D_IN_PAD, D_MODEL), BF16), pltpu.SemaphoreType.DMA((1,))],
        compiler_params=pltpu.CompilerParams(dimension_semantics=("arbitrary",), vmem_limit_bytes=V7X_VMEM_LIMIT),
    )(x2d, sc1p, sh1, w_in_t, after)


CHUNK_SHIFT = CHUNK.bit_length() - 1


def _ret_tables(t, tl):
    r = lax.broadcasted_iota(jnp.int32, (tl, tl), 0)
    c = lax.broadcasted_iota(jnp.int32, (tl, tl), 1)
    allowed = jnp.right_shift(c, CHUNK_SHIFT) <= jnp.right_shift(r, CHUNK_SHIFT)
    dist = jnp.abs(r - c).astype(F32)
    rowf = lax.broadcasted_iota(jnp.int32, (tl, RET_D), 0).astype(F32)
    lgs = [_log_gamma(h) for h in range(RET_HEADS)]
    dec = jnp.stack([jnp.where(allowed, jnp.exp(lg * dist), 0.0) for lg in lgs])
    qkd = jnp.stack([jnp.exp(lg * (rowf + 1.0)) for lg in lgs] + [jnp.exp(lg * (tl - 1.0 - rowf)) for lg in lgs])
    inv = 1.0 / (10000.0 ** jnp.linspace(0.0, 1.0, RET_D // 2, dtype=F32))
    off = jnp.arange(tl, dtype=F32)[:, None] * inv[None, :]
    start = (jnp.arange(t // tl, dtype=F32) * tl)[:, None] * inv[None, :]
    co, so = jnp.cos(off), jnp.sin(off)
    rot_in = jnp.stack([jnp.concatenate([co, co], 1), jnp.concatenate([so, so], 1),
                        jnp.concatenate([-co, co], 1), jnp.concatenate([-so, so], 1)])
    cs, ss = jnp.cos(start), jnp.sin(start)
    rot_tile = jnp.concatenate([cs, cs, ss, ss], axis=1)
    rot_tile = jnp.broadcast_to(rot_tile[:, None, :], (t // tl, 8, 2 * RET_D))
    return dec, qkd, rot_in, rot_tile


def _tile_gammas(tl):
    return [float(np.exp(np.float32(_log_gamma(h)) * np.float32(tl))) for h in range(RET_HEADS)]


def _tile_rotary(rot_in_ref, rot_tile_ref, j):
    ca, sa = rot_tile_ref[j, 0:1, 0:RET_D], rot_tile_ref[j, 0:1, RET_D:2 * RET_D]
    cosv = ca * rot_in_ref[0] - sa * rot_in_ref[1]
    sinv = sa * rot_in_ref[2] + ca * rot_in_ref[3]
    return cosv, sinv


def _gla_consts(tl):
    r = lax.broadcasted_iota(jnp.int32, (tl, tl), 0)
    c = lax.broadcasted_iota(jnp.int32, (tl, tl), 1)
    ltri = (c <= r).astype(F32)
    utri = (c >= r).astype(F32)
    lane = lax.broadcasted_iota(jnp.int32, (1, GLA_KW), 1)
    hmask = [((lane >= h * GLA_DK) & (lane < (h + 1) * GLA_DK)).astype(F32) for h in range(GLA_HEADS)]
    rs = lax.broadcasted_iota(jnp.int32, (GLA_HEADS * tl, tl), 0) & (tl - 1)
    cs = lax.broadcasted_iota(jnp.int32, (GLA_HEADS * tl, tl), 1)
    lower = cs <= rs
    same = jnp.right_shift(cs, CHUNK_SHIFT) == jnp.right_shift(rs, CHUNK_SHIFT)
    upper = jnp.logical_and(jnp.logical_not(lower), same)
    return dict(ltri=ltri, utri=utri, hmask=hmask, lower=lower, upper=upper)


def _tile_rows(j, tl):
    return pl.ds(j * tl, tl) if isinstance(j, int) else pl.ds(pl.multiple_of(j * tl, tl), tl)


def _for_tiles(cps, fn):
    for j in range(cps):
        fn(j, 0)


def _rotate(v, cosv, sinv):
    return v * cosv + pltpu.roll(v, RET_D // 2, 1) * sinv


def _rotate_t(d, cosv, sinv):
    return d * cosv + pltpu.roll(d * sinv, RET_D // 2, 1)


def _stack_heads(v, hmask):
    return jnp.concatenate([v * hmask[h] for h in range(GLA_HEADS)], axis=0)


def _gla_gates(glr, gw, gb, ltri, tl):
    z = _dot_split(glr, gw, NN) + gb
    la = (jnp.minimum(z, 0.0) - jnp.log(1.0 + jnp.exp(-jnp.abs(z)))) * (1.0 / GATE_TAU)
    b = _dot_split(ltri, la, NN, a_exact=True)
    level = b[tl // 2 - 1:tl // 2, :]
    ep = jnp.exp(jnp.clip(b - level, -80.0, 80.0))
    em = jnp.exp(jnp.clip(level - b, -80.0, 80.0))
    bl = b[tl - 1:tl, :]
    return z, b, bl, ep, em


def _mixer_fwd(proj, tables, gw_pad, gb, rnw, gnw, after):
    t = proj.shape[0]
    tc = min(MIX_TILE, t)
    tr, tg = min(RET_SUB, tc), min(GLA_SUB, tc)
    nsteps = t // tc
    scale_r = RET_D ** -0.5
    scale_g = GLA_DK ** -0.5
    gammas = _tile_gammas(tr)

    def body(rq_ref, rk_ref, rv_ref, rg_ref, gq_ref, gk_ref, gv_ref, gg_ref, glr_ref,
             dec_ref, qkd_ref, rot_in_ref, rot_tile_ref, gw_ref, gb_ref, rnw_ref, gnw_ref, after_ref,
             mix_ref, oraw_ref, qrb_ref, krb_ref, rst_ref, sst_ref, r_scr, s_scr):
        @pl.when(pl.program_id(0) == 0)
        def _():
            r_scr[...] = jnp.zeros_like(r_scr)
            s_scr[...] = jnp.zeros_like(s_scr)

        gla_k = _gla_consts(tg)

        def ret_tile(j, carry):
            rows = _tile_rows(j, tr)
            cosv, sinv = _tile_rotary(rot_in_ref, rot_tile_ref, j)
            for h in range(RET_HEADS):
                cols = slice(h * RET_D, (h + 1) * RET_D)
                qr = _rotate(rq_ref[rows, cols], cosv, sinv) * scale_r
                kr = _rotate(rk_ref[rows, cols], cosv, sinv)
                vb = rv_ref[rows, cols].astype(BF16)
                qb, kb = qr.astype(BF16), kr.astype(BF16)
                qrb_ref[rows, cols] = qb
                krb_ref[rows, cols] = kb
                p = _dot(qb, kb, NT) * dec_ref[h]
                rp = r_scr[cols, :]
                o = _dot(p.astype(BF16), vb) + _dot((qr * qkd_ref[h]).astype(BF16), rp.astype(BF16))
                rst_ref[j, cols, :] = rp
                r_scr[cols, :] = gammas[h] * rp + _dot((kr * qkd_ref[RET_HEADS + h]).astype(BF16), vb, TN)
                oraw_ref[rows, cols] = o
                oc = o - jnp.mean(o, axis=-1, keepdims=True)
                n = oc * lax.rsqrt(jnp.mean(oc * oc, axis=-1, keepdims=True) + LN_EPS)
                g = rg_ref[rows, cols]
                mix_ref[rows, cols] = (n * rnw_ref[:, cols] * (g * _sigmoid(g))).astype(BF16)
            return carry

        def gla_tile(j, carry):
            k = gla_k
            tl = tg
            rows = _tile_rows(j, tg)
            _, b, bl, ep, em = _gla_gates(glr_ref[rows, :], gw_ref[...], gb_ref[...], k["ltri"], tl)
            qs = gq_ref[rows, :] * scale_g
            kk = gk_ref[rows, :]
            x_all = _dot(_stack_heads(qs * ep, k["hmask"]).astype(BF16), (kk * em).astype(BF16), NT)
            y_all = _dot(_stack_heads(qs * em, k["hmask"]).astype(BF16), (kk * ep).astype(BF16), NT)
            a_all = jnp.where(k["lower"], x_all, jnp.where(k["upper"], y_all, 0.0)).astype(BF16)
            st = s_scr[...]
            oq = _dot(_stack_heads(qs * jnp.exp(b), k["hmask"]).astype(BF16), st.astype(BF16), NT)
            kg = kk * jnp.exp(bl - b)
            sst_ref[j] = st
            st_new = st * jnp.exp(bl)
            for h in range(GLA_HEADS):
                cols = slice(h * GLA_DV, (h + 1) * GLA_DV)
                hr = slice(h * tl, (h + 1) * tl)
                vb = gv_ref[rows, cols].astype(BF16)
                o = _dot(a_all[hr, :], vb) + oq[hr, :]
                st_new = st_new + _dot(vb, (kg * k["hmask"][h]).astype(BF16), TN)
                ocols = slice(RET_W + h * GLA_DV, RET_W + (h + 1) * GLA_DV)
                oraw_ref[rows, ocols] = o
                n = o * lax.rsqrt(jnp.mean(o * o, axis=-1, keepdims=True) + LN_EPS)
                g = gg_ref[rows, cols]
                mix_ref[rows, ocols] = (n * gnw_ref[:, cols] * (g * _sigmoid(g))).astype(BF16)
            s_scr[...] = st_new
            return carry

        _for_tiles(tc // tr, ret_tile)
        _for_tiles(tc // tg, gla_tile)

    def col(width, off):
        return pl.BlockSpec((tc, width), lambda i, o=off // width: (i, o))

    fix = lambda i: (0, 0)
    fix3 = lambda i: (0, 0, 0)
    dec, qkd, rot_in, rot_tile = tables
    in_specs = [col(RET_W, OFF_RQ), col(RET_W, OFF_RK), col(RET_W, OFF_RV), col(RET_W, OFF_RG),
                col(GLA_KW, OFF_GQ), col(GLA_KW, OFF_GK), col(GLA_VW, OFF_GV), col(GLA_VW, OFF_GG),
                col(V7X_LANES, OFF_GLR),
                pl.BlockSpec(dec.shape, fix3), pl.BlockSpec(qkd.shape, fix3), pl.BlockSpec(rot_in.shape, fix3),
                pl.BlockSpec((tc // tr, 8, 2 * RET_D), lambda i: (i, 0, 0)),
                pl.BlockSpec((V7X_LANES, GLA_KW), fix), pl.BlockSpec((1, GLA_KW), fix),
                pl.BlockSpec((1, RET_W), fix), pl.BlockSpec((1, GLA_VW), fix), pl.BlockSpec(memory_space=pl.ANY)]
    half = pl.BlockSpec((tc, RET_W), lambda i: (i, 0))
    out_specs = (pl.BlockSpec((tc, D_MODEL), lambda i: (i, 0)), pl.BlockSpec((tc, D_MODEL), lambda i: (i, 0)),
                 half, half,
                 pl.BlockSpec((tc // tr, RET_W, RET_D), lambda i: (i, 0, 0)),
                 pl.BlockSpec((tc // tg, GLA_DV, GLA_KW), lambda i: (i, 0, 0)))
    out_shape = (jax.ShapeDtypeStruct((t, D_MODEL), BF16), jax.ShapeDtypeStruct((t, D_MODEL), F32),
                 jax.ShapeDtypeStruct((t, RET_W), BF16), jax.ShapeDtypeStruct((t, RET_W), BF16),
                 jax.ShapeDtypeStruct((t // tr, RET_W, RET_D), F32),
                 jax.ShapeDtypeStruct((t // tg, GLA_DV, GLA_KW), F32))
    return pl.pallas_call(
        body, name="mixer_fwd", grid=(nsteps,), in_specs=in_specs, out_specs=out_specs, out_shape=out_shape,
        scratch_shapes=[pltpu.VMEM((RET_W, RET_D), F32), pltpu.VMEM((GLA_DV, GLA_KW), F32)],
        compiler_params=pltpu.CompilerParams(dimension_semantics=("arbitrary",), vmem_limit_bytes=V7X_VMEM_LIMIT),
    )(*([proj] * 9), dec, qkd, rot_in, rot_tile, gw_pad, gb, rnw, gnw, after)


def _mid_fwd(mixed, x2d, target, vecs, w_out_b, w1_b, w2_b):
    t = x2d.shape[0]
    tm = min(ROW_TILE, t)

    def body(mix_ref, x_ref, tgt_ref, v_ref, wo_hbm, w1_hbm, w2_hbm,
             m_ref, x1n_ref, rstd_ref, u2_ref, a_ref, df_ref, dh2_ref, acc_ref, wo, w1, w2, sem):
        first = pl.program_id(0) == 0
        _load_resident(first, [(wo_hbm, wo), (w1_hbm, w1), (w2_hbm, w2)], sem)

        @pl.when(first)
        def _():
            acc_ref[...] = jnp.zeros_like(acc_ref)

        gate1, sc2p, sh2, gate2 = v_ref[0:1, :], v_ref[1:2, :], v_ref[2:3, :], v_ref[3:4, :]
        l1w, l1b, l2w, l2b = v_ref[4:5, :], v_ref[5:6, :], v_ref[6:7, :], v_ref[7:8, :]
        m = _dot(mix_ref[...], wo[...])
        m_ref[...] = m.astype(BF16)
        x1n, rstd1 = _ln_stats(ALPHA * x_ref[...] + gate1 * m)
        x1n_ref[...] = x1n
        rstd_ref[...] = rstd1
        x1 = x1n * l1w + l1b
        xh1, _ = _ln_stats(x1)
        u2 = (xh1 * sc2p + sh2).astype(BF16)
        u2_ref[...] = u2
        f = jnp.zeros((tm, D_MODEL), F32)
        for j in range(N_DEV):
            cols = slice(j * FF_COLS, (j + 1) * FF_COLS)
            a = _dot(u2, w1[j])
            a_ref[:, cols] = a.astype(BF16)
            r = jnp.maximum(a, 0.0)
            f = f + _dot((r * r).astype(BF16), w2[cols, :])
        yh, rstd2 = _ln_stats(ALPHA * x1 + gate2 * f)
        e = yh * l2w + l2b - tgt_ref[...]
        dy = e * (1.0 / D_MODEL)
        dh2 = _ln_bwd(dy * l2w, yh, rstd2)
        dh2_ref[...] = dh2
        df_ref[...] = (dh2 * gate2).astype(BF16)
        acc_ref[0:1, :] += jnp.sum(dy * yh, axis=0, keepdims=True)
        acc_ref[1:2, :] += jnp.sum(dy, axis=0, keepdims=True)
        acc_ref[2:3, :] += jnp.sum(dh2 * f, axis=0, keepdims=True)
        acc_ref[3:4, :] += jnp.sum(e * e, axis=0, keepdims=True) * (0.5 / D_MODEL)

    row = lambda i: (i, 0)
    fix = lambda i: (0, 0)
    hbm = pl.BlockSpec(memory_space=pl.ANY)
    return pl.pallas_call(
        body, name="mid_fwd", grid=(t // tm,),
        in_specs=[pl.BlockSpec((tm, D_MODEL), row), pl.BlockSpec((tm, D_MODEL), row), pl.BlockSpec((tm, D_MODEL), row),
                  pl.BlockSpec((8, D_MODEL), fix), hbm, hbm, hbm],
        out_specs=(pl.BlockSpec((tm, D_MODEL), row), pl.BlockSpec((tm, D_MODEL), row), pl.BlockSpec((tm, 1), row),
                   pl.BlockSpec((tm, D_MODEL), row), pl.BlockSpec((tm, D_FF), row), pl.BlockSpec((tm, D_MODEL), row),
                   pl.BlockSpec((tm, D_MODEL), row), pl.BlockSpec((8, D_MODEL), fix)),
        out_shape=(jax.ShapeDtypeStruct((t, D_MODEL), BF16), jax.ShapeDtypeStruct((t, D_MODEL), F32),
                   jax.ShapeDtypeStruct((t, 1), F32), jax.ShapeDtypeStruct((t, D_MODEL), BF16),
                   jax.ShapeDtypeStruct((t, D_FF), BF16), jax.ShapeDtypeStruct((t, D_MODEL), BF16),
                   jax.ShapeDtypeStruct((t, D_MODEL), F32), jax.ShapeDtypeStruct((8, D_MODEL), F32)),
        scratch_shapes=[pltpu.VMEM((D_MODEL, D_MODEL), BF16), pltpu.VMEM((N_DEV, D_MODEL, FF_COLS), BF16),
                        pltpu.VMEM((D_FF, D_MODEL), BF16), pltpu.SemaphoreType.DMA((3,))],
        compiler_params=pltpu.CompilerParams(dimension_semantics=("arbitrary",), vmem_limit_bytes=V7X_VMEM_LIMIT),
    )(mixed, x2d, target, vecs, w_out_b, w1_b, w2_b)


def _ffn_bwd(df, a, dh2, x1n, rstd1, m, vecs, w_out_b, w1_b, w2_b):
    t = x1n.shape[0]
    tm = min(ROW_TILE, t)

    def body(df_ref, a_ref, dh2_ref, x1n_ref, rstd_ref, m_ref, v_ref, wo_hbm, w1_hbm, w2_hbm,
             da_ref, dm_ref, dmix_ref, dxa_ref, acc_ref, wo, w1, w2, sem):
        first = pl.program_id(0) == 0
        _load_resident(first, [(wo_hbm, wo), (w1_hbm, w1), (w2_hbm, w2)], sem)

        @pl.when(first)
        def _():
            acc_ref[...] = jnp.zeros_like(acc_ref)

        gate1, sc2p, l1w, l1b = v_ref[0:1, :], v_ref[1:2, :], v_ref[2:3, :], v_ref[3:4, :]
        df = df_ref[...]
        du2 = jnp.zeros((tm, D_MODEL), F32)
        for j in range(N_DEV):
            cols = slice(j * FF_COLS, (j + 1) * FF_COLS)
            dr2 = _dot(df, w2[cols, :], NT)
            da = (dr2 * (2.0 * jnp.maximum(a_ref[:, cols].astype(F32), 0.0))).astype(BF16)
            da_ref[:, cols] = da
            du2 = du2 + _dot(da, w1[j], NT)
        x1n = x1n_ref[...]
        xh1, rstd0 = _ln_stats(x1n * l1w + l1b)
        dx1 = ALPHA * dh2_ref[...] + _ln_bwd(du2 * sc2p, xh1, rstd0)
        dh1 = _ln_bwd(dx1 * l1w, x1n, rstd_ref[...])
        dxa_ref[...] = ALPHA * dh1
        dm = (dh1 * gate1).astype(BF16)
        dm_ref[...] = dm
        dmix_ref[...] = _dot(dm, wo[...], NT)
        acc_ref[0:1, :] += jnp.sum(du2 * xh1, axis=0, keepdims=True)
        acc_ref[1:2, :] += jnp.sum(du2, axis=0, keepdims=True)
        acc_ref[2:3, :] += jnp.sum(dx1 * x1n, axis=0, keepdims=True)
        acc_ref[3:4, :] += jnp.sum(dx1, axis=0, keepdims=True)
        acc_ref[4:5, :] += jnp.sum(dh1 * m_ref[...].astype(F32), axis=0, keepdims=True)

    row = lambda i: (i, 0)
    fix = lambda i: (0, 0)
    hbm = pl.BlockSpec(memory_space=pl.ANY)
    return pl.pallas_call(
        body, name="ffn_bwd", grid=(t // tm,),
        in_specs=[pl.BlockSpec((tm, D_MODEL), row), pl.BlockSpec((tm, D_FF), row), pl.BlockSpec((tm, D_MODEL), row),
                  pl.BlockSpec((tm, D_MODEL), row), pl.BlockSpec((tm, 1), row), pl.BlockSpec((tm, D_MODEL), row),
                  pl.BlockSpec((8, D_MODEL), fix), hbm, hbm, hbm],
        out_specs=(pl.BlockSpec((tm, D_FF), row), pl.BlockSpec((tm, D_MODEL), row), pl.BlockSpec((tm, D_MODEL), row),
                   pl.BlockSpec((tm, D_MODEL), row), pl.BlockSpec((8, D_MODEL), fix)),
        out_shape=(jax.ShapeDtypeStruct((t, D_FF), BF16), jax.ShapeDtypeStruct((t, D_MODEL), BF16),
                   jax.ShapeDtypeStruct((t, D_MODEL), F32), jax.ShapeDtypeStruct((t, D_MODEL), F32),
                   jax.ShapeDtypeStruct((8, D_MODEL), F32)),
        scratch_shapes=[pltpu.VMEM((D_MODEL, D_MODEL), BF16), pltpu.VMEM((N_DEV, D_MODEL, FF_COLS), BF16),
                        pltpu.VMEM((D_FF, D_MODEL), BF16), pltpu.SemaphoreType.DMA((3,))],
        compiler_params=pltpu.CompilerParams(dimension_semantics=("arbitrary",), vmem_limit_bytes=V7X_VMEM_LIMIT),
    )(df, a, dh2, x1n, rstd1, m, vecs, w_out_b, w1_b, w2_b)


def _matmul_tn(lhs, rhs, tmm, tn, tk, name, relu_sq=False, col_slab=None, out_rows=None):
    t, mm = lhs.shape
    assert out_rows is None or (col_slab is None and tmm == mm)
    nn = rhs.shape[1]
    tk = min(tk, t)
    nk = t // tk

    def body(l_ref, r_ref, o_ref, acc):
        kk = pl.program_id(2)

        @pl.when(kk == 0)
        def _():
            acc[...] = jnp.zeros_like(acc)

        l = l_ref[...]
        if relu_sq:
            lf = jnp.maximum(l.astype(F32), 0.0)
            l = (lf * lf).astype(BF16)
        acc[...] += _dot(l, r_ref[...], TN)

        @pl.when(kk == nk - 1)
        def _():
            if out_rows is not None:
                for s in range(N_DEV):
                    o_ref[s] = acc[s * out_rows:(s + 1) * out_rows, :].astype(o_ref.dtype)
            elif col_slab is None:
                o_ref[...] = acc[...].astype(o_ref.dtype)
            else:
                for s in range(tn // col_slab):
                    o_ref[s] = acc[:, s * col_slab:(s + 1) * col_slab].astype(o_ref.dtype)

    if out_rows is not None:
        out_spec = pl.BlockSpec((N_DEV, out_rows, tn), lambda i, j, k: (0, 0, j))
        out_shape = jax.ShapeDtypeStruct((N_DEV, out_rows, nn), BF16)
    elif col_slab is None:
        out_spec = pl.BlockSpec((tmm, tn), lambda i, j, k: (i, j))
        out_shape = jax.ShapeDtypeStruct((mm, nn), BF16)
    else:
        out_spec = pl.BlockSpec((tn // col_slab, tmm, col_slab), lambda i, j, k: (j, i, 0))
        out_shape = jax.ShapeDtypeStruct((nn // col_slab, mm, col_slab), BF16)
    return pl.pallas_call(
        body, name=name, grid=(mm // tmm, nn // tn, nk),
        in_specs=[pl.BlockSpec((tk, tmm), lambda i, j, k: (k, i)), pl.BlockSpec((tk, tn), lambda i, j, k: (k, j))],
        out_specs=out_spec,
        out_shape=out_shape,
        scratch_shapes=[pltpu.VMEM((tmm, tn), F32)],
        compiler_params=pltpu.CompilerParams(dimension_semantics=("arbitrary", "arbitrary", "arbitrary"),
                                             vmem_limit_bytes=V7X_VMEM_LIMIT),
    )(lhs, rhs)


def _mixer_bwd(dmix, proj, qrb, krb, oraw, tables, rst, sst, gw_pad, gb, rnw, gnw, after):
    t = proj.shape[0]
    tc = min(MIX_TILE, t)
    tr, tg = min(RET_SUB, tc), min(GLA_SUB, tc)
    nsteps = t // tc
    scale_r = RET_D ** -0.5
    scale_g = GLA_DK ** -0.5
    gammas = _tile_gammas(tr)

    def body(dmix_ref, qrb_ref, krb_ref, rv_ref, rg_ref, gq_ref, gk_ref, gv_ref, gg_ref, glr_ref, oraw_ref,
             dec_ref, qkd_ref, rot_in_ref, rot_tile_ref, rst_ref, sst_ref, gw_ref, gb_ref, rnw_ref, gnw_ref, after_ref,
             dproj_ref, dgw_ref, dvec_ref, dr_scr, ds_scr):
        @pl.when(pl.program_id(0) == 0)
        def _():
            dr_scr[...] = jnp.zeros_like(dr_scr)
            ds_scr[...] = jnp.zeros_like(ds_scr)
            dgw_ref[...] = jnp.zeros_like(dgw_ref)
            dvec_ref[...] = jnp.zeros_like(dvec_ref)

        gla_k = _gla_consts(tg)
        last_row = lax.broadcasted_iota(jnp.int32, (tg, GLA_KW), 0) == tg - 1

        def ret_tile(jj, carry):
            j = tc // tr - 1 - jj
            rows = _tile_rows(j, tr)
            cosv, sinv = _tile_rotary(rot_in_ref, rot_tile_ref, j)
            for h in range(RET_HEADS):
                cols = slice(h * RET_D, (h + 1) * RET_D)
                o = oraw_ref[rows, cols]
                g = rg_ref[rows, cols]
                w = rnw_ref[:, cols]
                dout = dmix_ref[rows, cols]
                oc = o - jnp.mean(o, axis=-1, keepdims=True)
                inv = lax.rsqrt(jnp.mean(oc * oc, axis=-1, keepdims=True) + LN_EPS)
                n = oc * inv
                sg = _sigmoid(g)
                sil = g * sg
                dn = dout * w * sil
                dvec_ref[0:1, cols] += jnp.sum(dout * n * sil, axis=0, keepdims=True)
                dproj_ref[rows, OFF_RG + h * RET_D:OFF_RG + (h + 1) * RET_D] = (
                    dout * n * w * (sg * (1.0 + g * (1.0 - sg)))).astype(BF16)
                doc = inv * (dn - n * jnp.mean(dn * n, axis=-1, keepdims=True))
                do = doc - jnp.mean(doc, axis=-1, keepdims=True)

                qb, kb = qrb_ref[rows, cols], krb_ref[rows, cols]
                qr, kr = qb.astype(F32), kb.astype(F32)
                vb = rv_ref[rows, cols].astype(BF16)
                dob = do.astype(BF16)
                qd, kd = qkd_ref[h], qkd_ref[RET_HEADS + h]
                p = _dot(qb, kb, NT) * dec_ref[h]
                rp = rst_ref[j, cols, :].astype(BF16)
                dr = dr_scr[cols, :]
                drb = dr.astype(BF16)
                dpb = (_dot(dob, vb, NT) * dec_ref[h]).astype(BF16)
                dqr = _dot(dpb, kb) + _dot(dob, rp, NT) * qd
                dkr = _dot(dpb, qb, TN) + _dot(vb, drb, NT) * kd
                dv = _dot(p.astype(BF16), dob, TN) + _dot((kr * kd).astype(BF16), drb)
                dr_scr[cols, :] = gammas[h] * dr + _dot((qr * qd).astype(BF16), dob, TN)
                dproj_ref[rows, OFF_RQ + h * RET_D:OFF_RQ + (h + 1) * RET_D] = (
                    _rotate_t(dqr, cosv, sinv) * scale_r).astype(BF16)
                dproj_ref[rows, OFF_RK + h * RET_D:OFF_RK + (h + 1) * RET_D] = _rotate_t(dkr, cosv, sinv).astype(BF16)
                dproj_ref[rows, OFF_RV + h * RET_D:OFF_RV + (h + 1) * RET_D] = dv.astype(BF16)
            return carry

        def gla_tile(jj, carry):
            k = gla_k
            tl = tg
            j = tc // tg - 1 - jj
            rows = _tile_rows(j, tg)
            glr = glr_ref[rows, :]
            z, b, bl, ep, em = _gla_gates(glr, gw_ref[...], gb_ref[...], k["ltri"], tl)
            qs = gq_ref[rows, :] * scale_g
            kk = gk_ref[rows, :]
            eb = jnp.exp(b)
            ekb = jnp.exp(bl - b)
            ebl = jnp.exp(bl)
            ql, qu, kl, ku = qs * ep, qs * em, kk * em, kk * ep
            qg, kg = qs * eb, kk * ekb
            qlm = _stack_heads(ql, k["hmask"]).astype(BF16)
            qum = _stack_heads(qu, k["hmask"]).astype(BF16)
            klb, kub = kl.astype(BF16), ku.astype(BF16)
            a_all = jnp.where(k["lower"], _dot(qlm, klb, NT),
                              jnp.where(k["upper"], _dot(qum, kub, NT), 0.0)).astype(BF16)
            st = sst_ref[j]
            stb = st.astype(BF16)
            ds = ds_scr[...]
            dsb = ds.astype(BF16)
            ds_new = ds * ebl
            da_parts = []
            dqg = jnp.zeros((tl, GLA_KW), F32)
            dkg = jnp.zeros((tl, GLA_KW), F32)
            for h in range(GLA_HEADS):
                cols = slice(h * GLA_DV, (h + 1) * GLA_DV)
                hr = slice(h * tl, (h + 1) * tl)
                ocols = slice(RET_W + h * GLA_DV, RET_W + (h + 1) * GLA_DV)
                o = oraw_ref[rows, ocols]
                g = gg_ref[rows, cols]
                w = gnw_ref[:, cols]
                dout = dmix_ref[rows, ocols]
                inv = lax.rsqrt(jnp.mean(o * o, axis=-1, keepdims=True) + LN_EPS)
                n = o * inv
                sg = _sigmoid(g)
                sil = g * sg
                dn = dout * w * sil
                dvec_ref[1:2, cols] += jnp.sum(dout * n * sil, axis=0, keepdims=True)
                dproj_ref[rows, OFF_GG + h * GLA_DV:OFF_GG + (h + 1) * GLA_DV] = (
                    dout * n * w * (sg * (1.0 + g * (1.0 - sg)))).astype(BF16)
                dob = (inv * (dn - n * jnp.mean(dn * n, axis=-1, keepdims=True))).astype(BF16)
                vb = gv_ref[rows, cols].astype(BF16)
                mh = k["hmask"][h]
                da_parts.append(_dot(dob, vb, NT))
                dv = _dot(a_all[hr, :], dob, TN) + _dot((kg * mh).astype(BF16), dsb, NT)
                dproj_ref[rows, OFF_GV + h * GLA_DV:OFF_GV + (h + 1) * GLA_DV] = dv.astype(BF16)
                dkg = dkg + mh * _dot(vb, dsb)
                dqg = dqg + mh * _dot(dob, stb)
                ds_new = ds_new + _dot(dob, (qg * mh).astype(BF16), TN)
            da_all = jnp.concatenate(da_parts, axis=0)
            dal = jnp.where(k["lower"], da_all, 0.0).astype(BF16)
            dau = jnp.where(k["upper"], da_all, 0.0).astype(BF16)
            dqlm = _dot(dal, klb)
            dqum = _dot(dau, kub)
            dql = jnp.zeros((tl, GLA_KW), F32)
            dqu = jnp.zeros((tl, GLA_KW), F32)
            for h in range(GLA_HEADS):
                hr = slice(h * tl, (h + 1) * tl)
                dql = dql + k["hmask"][h] * dqlm[hr, :]
                dqu = dqu + k["hmask"][h] * dqum[hr, :]
            dkl = _dot(dal, qlm, TN)
            dku = _dot(dau, qum, TN)
            dbl = (jnp.sum(dkg * kg, axis=0, keepdims=True)
                   + jnp.sum(ds * st, axis=0, keepdims=True) * ebl)
            ds_scr[...] = ds_new
            dqs = dql * ep + dqu * em + dqg * eb
            dk = dkl * em + dku * ep + dkg * ekb
            db = dql * ql - dkl * kl - dqu * qu + dku * ku + dqg * qg - dkg * kg
            db = db + jnp.where(last_row, dbl, 0.0)
            dla = _dot_split(k["utri"], db, NN, a_exact=True)
            dz = dla * (1.0 / GATE_TAU) * _sigmoid(-z)
            dvec_ref[2:3, 0:GLA_KW] += jnp.sum(dz, axis=0, keepdims=True)
            dgw_ref[...] += _dot_split(glr, dz, TN)
            dproj_ref[rows, OFF_GLR:D_IN_PAD] = _dot(dz.astype(BF16), gw_ref[...].astype(BF16), NT).astype(BF16)
            dproj_ref[rows, OFF_GQ:OFF_GQ + GLA_KW] = (dqs * scale_g).astype(BF16)
            dproj_ref[rows, OFF_GK:OFF_GK + GLA_KW] = dk.astype(BF16)
            return carry

        _for_tiles(tc // tr, ret_tile)
        _for_tiles(tc // tg, gla_tile)

    rev = lambda i: (nsteps - 1 - i, 0)

    def col(width, off):
        return pl.BlockSpec((tc, width), lambda i, o=off // width: (nsteps - 1 - i, o))

    fix = lambda i: (0, 0)
    fix3 = lambda i: (0, 0, 0)
    dec, qkd, rot_in, rot_tile = tables
    half = pl.BlockSpec((tc, RET_W), rev)
    in_specs = [pl.BlockSpec((tc, D_MODEL), rev), half, half, col(RET_W, OFF_RV), col(RET_W, OFF_RG),
                col(GLA_KW, OFF_GQ), col(GLA_KW, OFF_GK), col(GLA_VW, OFF_GV), col(GLA_VW, OFF_GG),
                col(V7X_LANES, OFF_GLR),
                pl.BlockSpec((tc, D_MODEL), rev),
                pl.BlockSpec(dec.shape, fix3), pl.BlockSpec(qkd.shape, fix3), pl.BlockSpec(rot_in.shape, fix3),
                pl.BlockSpec((tc // tr, 8, 2 * RET_D), lambda i: (nsteps - 1 - i, 0, 0)),
                pl.BlockSpec((tc // tr, RET_W, RET_D), lambda i: (nsteps - 1 - i, 0, 0)),
                pl.BlockSpec((tc // tg, GLA_DV, GLA_KW), lambda i: (nsteps - 1 - i, 0, 0)),
                pl.BlockSpec((V7X_LANES, GLA_KW), fix), pl.BlockSpec((1, GLA_KW), fix),
                pl.BlockSpec((1, RET_W), fix), pl.BlockSpec((1, GLA_VW), fix), pl.BlockSpec(memory_space=pl.ANY)]
    out_specs = (pl.BlockSpec((tc, D_IN_PAD), rev), pl.BlockSpec((V7X_LANES, GLA_KW), fix),
                 pl.BlockSpec((8, RET_W), fix))
    out_shape = (jax.ShapeDtypeStruct((t, D_IN_PAD), BF16), jax.ShapeDtypeStruct((V7X_LANES, GLA_KW), F32),
                 jax.ShapeDtypeStruct((8, RET_W), F32))
    return pl.pallas_call(
        body, name="mixer_bwd", grid=(nsteps,), in_specs=in_specs, out_specs=out_specs, out_shape=out_shape,
        scratch_shapes=[pltpu.VMEM((RET_W, RET_D), F32), pltpu.VMEM((GLA_DV, GLA_KW), F32)],
        compiler_params=pltpu.CompilerParams(dimension_semantics=("arbitrary",), vmem_limit_bytes=V7X_VMEM_LIMIT),
    )(dmix, qrb, krb, *([proj] * 7), oraw, dec, qkd, rot_in, rot_tile, rst, sst, gw_pad, gb, rnw, gnw, after)


def _inproj_bwd(dproj, x2d, dxa, sc1p, w_in_t, after):
    t = x2d.shape[0]
    tm = min(2 * PROJ_TILE, t)

    def body(dp_ref, x_ref, dxa_ref, sc_ref, w_hbm, after_ref, gx_ref, acc_ref, w_vmem, sem):
        first = pl.program_id(0) == 0
        _load_w_in_t(first, w_hbm, w_vmem, sem)

        @pl.when(first)
        def _():
            acc_ref[...] = jnp.zeros_like(acc_ref)

        du = _dot(dp_ref[...], w_vmem[...])
        xh, rstd = _ln_stats(x_ref[...])
        gx_ref[...] = dxa_ref[...] + _ln_bwd(du * sc_ref[...], xh, rstd)
        acc_ref[0:1, :] += jnp.sum(du * xh, axis=0, keepdims=True)
        acc_ref[1:2, :] += jnp.sum(du, axis=0, keepdims=True)

    row = lambda i: (i, 0)
    fix = lambda i: (0, 0)
    return pl.pallas_call(
        body, name="inproj_bwd", grid=(t // tm,),
        in_specs=[pl.BlockSpec((tm, D_IN_PAD), row), pl.BlockSpec((tm, D_MODEL), row), pl.BlockSpec((tm, D_MODEL), row),
                  pl.BlockSpec((1, D_MODEL), fix), pl.BlockSpec(memory_space=pl.ANY), pl.BlockSpec(memory_space=pl.ANY)],
        out_specs=(pl.BlockSpec((tm, D_MODEL), row), pl.BlockSpec((8, D_MODEL), fix)),
        out_shape=(jax.ShapeDtypeStruct((t, D_MODEL), F32), jax.ShapeDtypeStruct((8, D_MODEL), F32)),
        scratch_shapes=[pltpu.VMEM((D_IN_PAD, D_MODEL), BF16), pltpu.SemaphoreType.DMA((1,))],
        compiler_params=pltpu.CompilerParams(dimension_semantics=("arbitrary",), vmem_limit_bytes=V7X_VMEM_LIMIT),
    )(dproj, x2d, dxa, sc1p, w_in_t, after)


def _adam_math(w, g, m, v):
    m = ADAM_B1 * m + (1.0 - ADAM_B1) * g
    v = ADAM_B2 * v + (1.0 - ADAM_B2) * (g * g)
    m_hat = m / (1.0 - ADAM_B1 ** ADAM_STEP)
    v_hat = v / (1.0 - ADAM_B2 ** ADAM_STEP)
    delta = -ADAM_LR * (m_hat / (jnp.sqrt(v_hat) + ADAM_EPS) + ADAM_WD * w)
    return delta, m, v


def _adamw(w, gparts, m, v, name, row_tiles=False):
    nparts, rows, cols = gparts.shape
    tr = rows
    for cand in (512, 256, 128, 64, 32, 16, 8):
        if rows % cand == 0:
            tr = cand
            break

    def body(w_ref, g_ref, m_ref, v_ref, go_ref, d_ref, mo_ref, vo_ref):
        g = g_ref[0].astype(F32)
        for p in range(1, nparts):
            g = g + g_ref[p].astype(F32)
        if row_tiles:
            g = g.reshape(tr, cols // V7X_LANES, V7X_LANES)
        delta, mn, vn = _adam_math(w_ref[...], g, m_ref[...], v_ref[...])
        go_ref[...] = g
        d_ref[...] = delta
        mo_ref[...] = mn
        vo_ref[...] = vn

    blk = pl.BlockSpec((tr, cols), lambda i: (i, 0))
    shp = jax.ShapeDtypeStruct((rows, cols), F32)
    if row_tiles:
        blk = pl.BlockSpec((tr, cols // V7X_LANES, V7X_LANES), lambda i: (i, 0, 0))
        shp = jax.ShapeDtypeStruct((rows, cols // V7X_LANES, V7X_LANES), F32)
    return pl.pallas_call(
        body, name=name, grid=(rows // tr,),
        in_specs=[blk, pl.BlockSpec((nparts, tr, cols), lambda i: (0, i, 0)), blk, blk],
        out_specs=(blk, blk, blk, blk), out_shape=(shp, shp, shp, shp),
        compiler_params=pltpu.CompilerParams(dimension_semantics=("arbitrary",), vmem_limit_bytes=V7X_VMEM_LIMIT),
    )(w, gparts, m, v)


def _small_reduce(gathered, gathered_gw, c_all, dmod_cols):
    def body(g_ref, gw_ref, c_ref, dm_ref, sum_ref, gwsum_ref, gb_ref, gwa_ref):
        s = g_ref[0]
        sw = gw_ref[0]
        for p in range(1, N_DEV):
            s = s + g_ref[p]
            sw = sw + gw_ref[p]
        sum_ref[...] = s
        gwsum_ref[...] = sw
        for i in range(6):
            gb_ref[:, i * D_MODEL:(i + 1) * D_MODEL] = s[i:i + 1, :]
        cc = c_ref[...]
        gwa_ref[...] = _dot(cc * _sigmoid(cc), dm_ref[...], TN, HIGHEST)

    vm = pl.BlockSpec(memory_space=pltpu.VMEM)
    return pl.pallas_call(
        body, name="small_reduce",
        out_shape=(jax.ShapeDtypeStruct(gathered.shape[1:], F32), jax.ShapeDtypeStruct(gathered_gw.shape[1:], F32),
                   jax.ShapeDtypeStruct((1, 6 * D_MODEL), F32), jax.ShapeDtypeStruct((D_MODEL, ADA_COLS), F32)),
        in_specs=[vm] * 4, out_specs=(vm, vm, vm, vm),
        compiler_params=pltpu.CompilerParams(vmem_limit_bytes=V7X_VMEM_LIMIT),
    )(gathered, gathered_gw, c_all, dmod_cols)


SMR_LN1W, SMR_LN1B, SMR_LN2W, SMR_LN2B, SMR_NORMS, SMR_MISC = 6, 7, 8, 9, 10, 11


def _adamw_small(gsum, g_b_ada, g_ggw, params, moms, vels):
    n = len(params)

    def body(*refs):
        gsum_ref, gb_ref, gw_ref = refs[:3]
        w_refs, m_refs, v_refs = refs[3:3 + n], refs[3 + n:3 + 2 * n], refs[3 + 2 * n:3 + 3 * n]
        outs = refs[3 + 3 * n:]
        g_refs, d_refs, mo_refs, vo_refs = outs[:n - 1], outs[n - 1:2 * n - 1], outs[2 * n - 1:3 * n - 1], outs[3 * n - 1:]
        grads = [gb_ref[...],
                 gsum_ref[SMR_NORMS:SMR_NORMS + 1, 0:RET_W],
                 gsum_ref[SMR_MISC:SMR_MISC + 1, 0:GLA_KW],
                 gsum_ref[SMR_NORMS:SMR_NORMS + 1, RET_W:RET_W + GLA_VW],
                 gsum_ref[SMR_LN1W:SMR_LN1W + 1, :], gsum_ref[SMR_LN1B:SMR_LN1B + 1, :],
                 gsum_ref[SMR_LN2W:SMR_LN2W + 1, :], gsum_ref[SMR_LN2B:SMR_LN2B + 1, :],
                 gw_ref[...]]
        for i in range(n):
            delta, mn, vn = _adam_math(w_refs[i][...], grads[i], m_refs[i][...], v_refs[i][...])
            if i < n - 1:
                g_refs[i][...] = grads[i]
            d_refs[i][...] = delta
            mo_refs[i][...] = mn
            vo_refs[i][...] = vn

    vm = pl.BlockSpec(memory_space=pltpu.VMEM)
    shapes = [jax.ShapeDtypeStruct(p.shape, F32) for p in params]
    n_in = 3 + 3 * n
    out_shape = tuple(shapes[:n - 1] + shapes * 3)
    return pl.pallas_call(
        body, name="adamw_small", out_shape=out_shape,
        in_specs=[vm] * n_in, out_specs=tuple([vm] * len(out_shape)),
        compiler_params=pltpu.CompilerParams(vmem_limit_bytes=V7X_VMEM_LIMIT),
    )(gsum, g_b_ada, g_ggw, *params, *moms, *vels)


def kernel(x, c, w_ada, b_ada, w_in, ret_norm_w, gla_gate_w, gla_gate_b, gla_norm_w, w_out, ln1_w, ln1_b, w_ff1, w_ff2, ln2_w, ln2_b, loss_target, m_w_ada, m_b_ada, m_w_in, m_ret_norm_w, m_gla_gate_w, m_gla_gate_b, m_gla_norm_w, m_w_out, m_ln1_w, m_ln1_b, m_w_ff1, m_w_ff2, m_ln2_w, m_ln2_b, v_w_ada, v_b_ada, v_w_in, v_ret_norm_w, v_gla_gate_w, v_gla_gate_b, v_gla_norm_w, v_w_out, v_ln1_w, v_ln1_b, v_w_ff1, v_w_ff2, v_ln2_w, v_ln2_b):
    t = x.shape[1]
    xi, yi, ci = _my_coords()
    me = 4 * xi + 2 * yi + ci
    x2d = x[0]
    tgt = loss_target[0]

    c_ext = jnp.concatenate([c, gla_gate_w[0].reshape(1, GATE_RANK * GLA_KW // N_DEV)], axis=1)
    b_l = lax.dynamic_slice(b_ada, (0, me * ADA_COLS), (1, ADA_COLS))
    c_all3, mod_all, wi_g, ada_token = _adaln_mod(c_ext, w_ada[0], b_l, w_in[0].T.astype(BF16))

    wg = _gather_phase("start", [w_out[0].astype(BF16), w_ff1[0].astype(BF16), w_ff2[0].astype(BF16)], None, None,
                       ada_token, "wgather_start")

    c_all = c_all3[:, 0, :D_MODEL]
    gate_w = c_all3[:, 0, D_MODEL:].reshape(N_DEV, GATE_RANK, GLA_KW // N_DEV)
    gate_w = gate_w.transpose(1, 0, 2).reshape(GATE_RANK, GLA_KW)
    gw_pad = jnp.zeros((V7X_LANES, GLA_KW), F32).at[:GATE_RANK].set(gate_w)
    mod = lax.dynamic_slice(mod_all, (0, me, 0), (N_DEV, 1, ADA_COLS)).reshape(6, D_MODEL)
    shift1, scale1, gate1, shift2, scale2, gate2 = [mod[i:i + 1] for i in range(6)]

    w_in_t = wi_g.reshape(D_IN, D_MODEL)

    tables = _ret_tables(t, min(RET_SUB, t))

    sc1p = 1.0 + scale1
    proj, u = _inproj_fwd(x2d, sc1p, shift1, w_in_t, after=wg[3])
    wg = _gather_phase("arrive", *wg[:3], u, "wgather_arrive")
    wg = _gather_phase("pass_on", *wg[:3], wg[3], "wgather_pass")
    mixed, oraw, qrb, krb, rst, sst = _mixer_fwd(proj, tables, gw_pad, gla_gate_b, ret_norm_w, gla_norm_w, after=wg[3])
    _, (wo_g, w1_b, w2_g), _, _ = _gather_phase("finish", *wg[:3], mixed, "wgather_wait")
    w_out_b = wo_g.reshape(D_MODEL, D_MODEL)
    w2_b = w2_g.reshape(D_FF, D_MODEL)
    vec_f = jnp.concatenate([gate1, 1.0 + scale2, shift2, gate2, ln1_w, ln1_b, ln2_w, ln2_b], axis=0)
    m, x1n, rstd1, u2, a, df, dh2, acc_f = _mid_fwd(mixed, x2d, tgt, vec_f, w_out_b, w1_b, w2_b)

    vec_b = jnp.concatenate([gate1, 1.0 + scale2, ln1_w, ln1_b, jnp.zeros((4, D_MODEL), F32)], axis=0)
    da, dm, dmix, dxa, acc_b = _ffn_bwd(df, a, dh2, x1n, rstd1, m, vec_b, w_out_b, w1_b, w2_b)
    dw2 = _matmul_tn(a, df, 2048, 1024, 2048, "tn_dw2", relu_sq=True)
    dw1 = _matmul_tn(u2, da, 1024, 2048, 2048, "tn_dw1", col_slab=FF_COLS)
    dwo = _matmul_tn(mixed, dm, 1024, 1024, 2048, "tn_dwout")
    gx = _exchange_start([dwo.reshape(N_DEV, OUT_ROWS, D_MODEL), dw1, dw2.reshape(N_DEV, FF_COLS, D_MODEL)], False,
                         "gradx_start")
    dproj, dgw, dvec = _mixer_bwd(dmix, proj, qrb, krb, oraw, tables, rst, sst, gw_pad,
                                  gla_gate_b, ret_norm_w, gla_norm_w, after=gx[4])
    dwi_s = _matmul_tn(dproj, u, D_IN_PAD, 1024, 1024, "tn_dwin", out_rows=IN_COLS)
    gi = _exchange_start([dwi_s], False, "gradin_start")
    grad_x, acc_i = _inproj_bwd(dproj, x2d, dxa, sc1p, w_in_t, after=gi[4])

    loss_part = jnp.sum(acc_f[3])
    small = jnp.concatenate([
        acc_i[1:2], acc_i[0:1], acc_b[4:5], acc_b[1:2], acc_b[0:1], acc_f[2:3],
        acc_b[2:3], acc_b[3:4], acc_f[0:1], acc_f[1:2],
        jnp.concatenate([dvec[0:1], dvec[1:2]], axis=1),
        jnp.concatenate([dvec[2:3, :GLA_KW], jnp.full((1, 128), loss_part, F32),
                         jnp.zeros((1, D_MODEL - GLA_KW - 128), F32)], axis=1),
        jnp.zeros((4, D_MODEL), F32)], axis=0)
    sg = _exchange_start([small, dgw[:GATE_RANK]], True, "small_start")

    r_wo, r_w1, r_w2 = _exchange_wait(*gx[:4], sg[4], False, "gradx_wait")
    r_wi, = _exchange_wait(*gi[:4], sg[4], False, "gradin_wait")
    big = [_adamw(w[0], r, m_[0], v_[0], nm) for w, r, m_, v_, nm in (
        (w_out, r_wo, m_w_out, v_w_out, "adamw_out"),
        (w_ff1, r_w1, m_w_ff1, v_w_ff1, "adamw_ff1"), (w_ff2, r_w2, m_w_ff2, v_w_ff2, "adamw_ff2"))]
    tiles = lambda a: a.T.reshape(IN_COLS, D_MODEL // V7X_LANES, V7X_LANES)
    big_in = _adamw(tiles(w_in[0]), r_wi, tiles(m_w_in[0]), tiles(v_w_in[0]), "adamw_in", row_tiles=True)
    big = [tuple(b.reshape(IN_COLS, D_MODEL).T for b in big_in)] + big
    g_big, d_big, m_big, v_big = [[b[i][None] for b in big] for i in range(4)]

    small_all, gw_all = _exchange_wait(*sg[:4], big_in[1], True, "small_wait")
    dmod_all = small_all[:, :6].reshape(N_DEV, 6 * D_MODEL)
    dmod_cols = lax.dynamic_slice(dmod_all, (0, me * ADA_COLS), (N_DEV, ADA_COLS))
    ssum, gw_sum, g_b_ada, g_w_ada = _small_reduce(small_all, gw_all, c_all, dmod_cols)
    loss = ssum[SMR_MISC, GLA_KW]
    g_ggw = lax.dynamic_slice(gw_sum, (0, me * (GLA_KW // N_DEV)), (GATE_RANK, GLA_KW // N_DEV))[None]

    small_w = [b_ada, ret_norm_w, gla_gate_b, gla_norm_w, ln1_w, ln1_b, ln2_w, ln2_b, gla_gate_w]
    small_m = [m_b_ada, m_ret_norm_w, m_gla_gate_b, m_gla_norm_w, m_ln1_w, m_ln1_b, m_ln2_w, m_ln2_b, m_gla_gate_w]
    small_v = [v_b_ada, v_ret_norm_w, v_gla_gate_b, v_gla_norm_w, v_ln1_w, v_ln1_b, v_ln2_w, v_ln2_b, v_gla_gate_w]
    res = _adamw_small(ssum, g_b_ada, g_ggw, small_w, small_m, small_v)
    small_g = list(res[:8]) + [g_ggw]
    d_small, m_small, v_small = list(res[8:17]), list(res[17:26]), list(res[26:35])

    _, d_w_ada, nm_w_ada, nv_w_ada = _adamw(w_ada[0], g_w_ada[None], m_w_ada[0], v_w_ada[0], "adamw_ada")

    def ordered(w_ada_v, small_vals, big_vals):
        b_ada_v, rnw_v, ggb_v, gnw_v, l1w_v, l1b_v, l2w_v, l2b_v, ggw_v = small_vals
        wi_v, wo_v, w1_v, w2_v = big_vals
        return [w_ada_v, b_ada_v, wi_v, rnw_v, ggw_v, ggb_v, gnw_v, wo_v, l1w_v, l1b_v, w1_v, w2_v, l2w_v, l2b_v]

    grads = ordered(g_w_ada[None], small_g, g_big)
    deltas = ordered(d_w_ada[None], d_small, d_big)
    new_m = ordered(nm_w_ada[None], m_small, m_big)
    new_v = ordered(nv_w_ada[None], v_small, v_big)
    return (loss, grad_x[None], *grads, *deltas, *new_m, *new_v)
```

```python
import numpy as np
import jax
import jax.numpy as jnp
from jax import lax
from jax.experimental import pallas as pl
from jax.experimental.pallas import tpu as pltpu

F32 = jnp.float32
BF16 = jnp.bfloat16
MESH = pl.DeviceIdType.MESH
HIGHEST = lax.Precision.HIGHEST

N_DEV = 8
D_MODEL = 1024
CHUNK = 64
RET_HEADS = 4
RET_D = 128
GLA_HEADS = 4
GLA_DK = 64
GLA_DV = 128
GLA_KW = GLA_HEADS * GLA_DK
RET_W = RET_HEADS * RET_D
GLA_VW = GLA_HEADS * GLA_DV
V7X_LANES = 128
GATE_RANK = 16
GATE_TAU = 16.0
D_FF = 4096
LN_EPS = 1e-5
ALPHA = (2.0 * 1) ** 0.25
D_IN = 3600
D_IN_PAD = 3712
ADA_COLS = 6 * D_MODEL // N_DEV
IN_COLS = D_IN // N_DEV
FF_COLS = D_FF // N_DEV
OUT_ROWS = D_MODEL // N_DEV

OFF_RQ, OFF_RK, OFF_RV, OFF_RG = 0, RET_W, 2 * RET_W, 3 * RET_W
OFF_GQ = 4 * RET_W
OFF_GK = OFF_GQ + GLA_KW
OFF_GV = OFF_GK + GLA_KW
OFF_GG = OFF_GV + GLA_VW
OFF_GLR = OFF_GG + GLA_VW

ADAM_LR, ADAM_B1, ADAM_B2, ADAM_EPS, ADAM_WD, ADAM_STEP = 0.001, 0.9, 0.999, 1e-08, 0.01, 10

V7X_VMEM_LIMIT = 62 * 1024 * 1024

ROW_TILE = 512
PROJ_TILE = 512
MIX_TILE = 512
RET_SUB = 256
GLA_SUB = 128


def _log_gamma(h):
    return float(np.log(np.float32(1.0) - np.float32(2.0) ** np.float32(-5.0 - h)))


def _my_coords():
    return lax.axis_index("x"), lax.axis_index("y"), lax.axis_index("c")


def _flip(v, bit):
    return 1 - v if bit else v


def _peer(k):
    x, y, c = _my_coords()
    px, py, pc = _flip(x, (k >> 2) & 1), _flip(y, (k >> 1) & 1), _flip(c, k & 1)
    return (px, py, pc), 4 * px + 2 * py + pc


def _dot(a, b, dims=(((1,), (0,)), ((), ())), precision=None):
    return lax.dot_general(a, b, dims, precision=precision, preferred_element_type=F32)


NN = (((1,), (0,)), ((), ()))
NT = (((1,), (1,)), ((), ()))
TN = (((0,), (0,)), ((), ()))


def _split_bf16(v, parts):
    out = []
    for _ in range(parts):
        p = v.astype(BF16)
        out.append(p)
        v = v - p.astype(F32)
    return out


def _dot_split(a, b, dims, a_exact=False):
    if a_exact:
        ab = a.astype(BF16)
        return sum(_dot(ab, p, dims) for p in _split_bf16(b, 2))
    a_hi, a_lo = _split_bf16(a, 2)
    b_hi, b_lo = _split_bf16(b, 2)
    return _dot(a_hi, b_hi, dims) + _dot(a_hi, b_lo, dims) + _dot(a_lo, b_hi, dims)


def _sigmoid(x):
    return 1.0 / (1.0 + jnp.exp(-x))


def _ln_stats(x):
    mu = jnp.mean(x, axis=-1, keepdims=True)
    xc = x - mu
    var = jnp.mean(xc * xc, axis=-1, keepdims=True)
    rstd = lax.rsqrt(var + LN_EPS)
    return xc * rstd, rstd


def _ln_bwd(dyh, xh, rstd):
    return rstd * (dyh - jnp.mean(dyh, axis=-1, keepdims=True) - xh * jnp.mean(dyh * xh, axis=-1, keepdims=True))


def _adaln_mod(c_ext, w_ada_l, b_l, w_in_l):
    width = c_ext.shape[1]

    def body(c_ref, w_ref, b_ref, wi_ref, call_ref, mod_ref, wig_ref, s1, r1, s2, r2, gs, gr, gl):
        gather = _TwoLevelGather([wi_ref], [wig_ref], gs, gr, gl)
        gather.start()
        x, y, c = _my_coords()
        me = 4 * x + 2 * y + c
        call_ref[me] = c_ref[...]
        sends = []
        for k in range(1, N_DEV):
            peer, _ = _peer(k)
            cp = pltpu.make_async_remote_copy(c_ref, call_ref.at[me], s1.at[k - 1], r1.at[k - 1],
                                              device_id=peer, device_id_type=MESH)
            cp.start()
            sends.append(cp)
        for k in range(1, N_DEV):
            peer, pid = _peer(k)
            pltpu.make_async_remote_copy(c_ref, call_ref.at[pid], s1.at[k - 1], r1.at[k - 1],
                                         device_id=peer, device_id_type=MESH).wait_recv()
        for cp in sends:
            cp.wait_send()
        row = lax.broadcasted_iota(jnp.int32, (N_DEV, D_MODEL), 0)
        call = jnp.zeros((N_DEV, D_MODEL), F32)
        for j in range(N_DEV):
            call = jnp.where(row == j, jnp.broadcast_to(call_ref[j][:, :D_MODEL], (N_DEV, D_MODEL)), call)
        sc = call * _sigmoid(call)
        mod = _dot(sc, w_ref[...], NN, HIGHEST) + b_ref[...]
        mod_ref[me] = mod
        sends = []
        for k in range(1, N_DEV):
            peer, _ = _peer(k)
            cp = pltpu.make_async_remote_copy(mod_ref.at[me], mod_ref.at[me], s2.at[k - 1], r2.at[k - 1],
                                              device_id=peer, device_id_type=MESH)
            cp.start()
            sends.append(cp)
        for k in range(1, N_DEV):
            peer, pid = _peer(k)
            pltpu.make_async_remote_copy(mod_ref.at[pid], mod_ref.at[pid], s2.at[k - 1], r2.at[k - 1],
                                         device_id=peer, device_id_type=MESH).wait_recv()
        for cp in sends:
            cp.wait_send()
        gather.forward()
        gather.finish()

    vm = pl.BlockSpec(memory_space=pltpu.VMEM)
    hbm = pl.BlockSpec(memory_space=pl.ANY)
    return pl.pallas_call(
        body, name="adaln_mod",
        out_shape=(jax.ShapeDtypeStruct((N_DEV, 1, width), F32),
                   jax.ShapeDtypeStruct((N_DEV, N_DEV, ADA_COLS), F32),
                   jax.ShapeDtypeStruct((N_DEV, *w_in_l.shape), w_in_l.dtype)),
        in_specs=[vm, vm, vm, hbm], out_specs=(vm, vm, hbm),
        scratch_shapes=[pltpu.SemaphoreType.DMA((N_DEV - 1,))] * 4
        + [pltpu.SemaphoreType.DMA((7,)), pltpu.SemaphoreType.DMA((7,)), pltpu.SemaphoreType.DMA((1,))],
        compiler_params=pltpu.CompilerParams(vmem_limit_bytes=V7X_VMEM_LIMIT),
    )(c_ext, w_ada_l, b_l, w_in_l)


class _TwoLevelGather:
    def __init__(self, x_refs, out_refs, send_sems, recv_sems, local_sems):
        self.x_refs, self.out_refs = x_refs, out_refs
        self.send_sems, self.recv_sems, self.local_sems = send_sems, recv_sems, local_sems
        x, y, c = _my_coords()
        self.c = c
        self.me, self.sibling = (x, y, c), (x, y, 1 - c)
        self.chips = [(1 - x, y), (x, 1 - y), (1 - x, 1 - y)]

    def _copy(self, a, k, block, to, src=None):
        px, py, pc = block
        slab = self.out_refs[a].at[4 * px + 2 * py + pc]
        return pltpu.make_async_remote_copy(
            src_ref=slab if src is None else src, dst_ref=slab,
            send_sem=self.send_sems.at[7 * a + k], recv_sem=self.recv_sems.at[7 * a + k],
            device_id=to, device_id_type=MESH)

    def _mine(self, a):
        px, py, pc = self.me
        return pltpu.make_async_copy(self.x_refs[a], self.out_refs[a].at[4 * px + 2 * py + pc], self.local_sems.at[a])

    def _first(self, a):
        cps = [self._copy(a, 0, self.me, self.sibling, src=self.x_refs[a])]
        cps += [self._copy(a, 1 + j, self.me, (*chip, self.c), src=self.x_refs[a]) for j, chip in enumerate(self.chips)]
        return cps

    def _passed(self, a):
        return [self._copy(a, 4 + j, (*chip, self.c), self.sibling) for j, chip in enumerate(self.chips)]

    def start(self):
        for a in range(len(self.x_refs)):
            self._mine(a).start()
            for cp in self._first(a):
                cp.start()

    def forward(self):
        for a in range(len(self.x_refs)):
            passed = self._passed(a)
            for j, chip in enumerate(self.chips):
                self._copy(a, 1 + j, (*chip, self.c), self.me).wait_recv()
                passed[j].start()

    def finish(self):
        for a in range(len(self.x_refs)):
            self._copy(a, 0, self.sibling, self.me).wait_recv()
            for j, chip in enumerate(self.chips):
                self._copy(a, 4 + j, (*chip, 1 - self.c), self.me).wait_recv()
            for cp in self._first(a) + self._passed(a):
                cp.wait_send()
            self._mine(a).wait()


def _exchange_copy(src_refs, land_refs, send_sems, recv_sems, a, k, gather, receiving):
    x, y, c = _my_coords()
    me = 4 * x + 2 * y + c
    peer, pid = _peer(k)
    src = src_refs[a] if gather else src_refs[a].at[pid]
    dst = land_refs[a].at[pid if receiving else me]
    return pltpu.make_async_remote_copy(src, dst, send_sems.at[7 * a + k - 1], recv_sems.at[7 * a + k - 1],
                                        device_id=peer, device_id_type=MESH)


def _own_copy(src_refs, land_refs, send_sems, a, n, gather):
    x, y, c = _my_coords()
    me = 4 * x + 2 * y + c
    src = src_refs[a] if gather else src_refs[a].at[me]
    return pltpu.make_async_copy(src, land_refs[a].at[me], send_sems.at[7 * n + a])


def _exchange_start(srcs, gather, name, after=None):
    n = len(srcs)
    land_shapes = [(N_DEV, *s.shape) if gather else s.shape for s in srcs]
    n_in = n if after is None else n + 1

    def body(*refs):
        src_refs, send_sems, recv_sems, token = refs[:n], refs[n_in], refs[n_in + 1], refs[n_in + 2 + 2 * n]
        land_refs = refs[n_in + 2 + n:n_in + 2 + 2 * n]
        for a in range(n):
            _own_copy(src_refs, land_refs, send_sems, a, n, gather).start()
            for k in range(1, N_DEV):
                _exchange_copy(src_refs, land_refs, send_sems, recv_sems, a, k, gather, receiving=False).start()
        token[...] = jnp.zeros_like(token)

    hbm = pl.BlockSpec(memory_space=pltpu.HBM)
    sem = pl.BlockSpec(memory_space=pltpu.SEMAPHORE)
    res = pl.pallas_call(
        body, name=name,
        out_shape=(pltpu.SemaphoreType.DMA((8 * n,)), pltpu.SemaphoreType.DMA((7 * n,)),
                   *[pltpu.HBM(v.shape, v.dtype) for v in srcs],
                   *[pltpu.HBM(shape, v.dtype) for shape, v in zip(land_shapes, srcs)],
                   jax.ShapeDtypeStruct((8, 128), F32),
                   *([] if after is None else [jax.ShapeDtypeStruct(after.shape, after.dtype)])),
        in_specs=[hbm] * n + [pl.BlockSpec(memory_space=pl.ANY)] * (n_in - n),
        out_specs=(sem, sem, *([hbm] * (2 * n)), pl.BlockSpec(memory_space=pltpu.VMEM),
                   *([pl.BlockSpec(memory_space=pl.ANY)] * (n_in - n))),
        input_output_aliases={**{i: 2 + i for i in range(n)}, **({} if after is None else {n: 3 + 2 * n})},
        compiler_params=pltpu.CompilerParams(has_side_effects=pltpu.SideEffectType.DATAFLOW_SIDE_EFFECTING),
    )(*[pltpu.with_memory_space_constraint(v, pltpu.HBM) for v in srcs], *([] if after is None else [after]))
    return (res[0], res[1], list(res[2:2 + n]), list(res[2 + n:2 + 2 * n]), res[2 + 2 * n],
            None if after is None else res[3 + 2 * n])


def _exchange_wait(send_sems, recv_sems, srcs, lands, after, gather, name):
    n = len(srcs)

    def body(*refs):
        src_refs, land_refs, s_sems, r_sems = refs[:n], refs[n:2 * n], refs[2 * n], refs[2 * n + 1]
        for a in range(n):
            _own_copy(src_refs, land_refs, s_sems, a, n, gather).wait()
            for k in range(1, N_DEV):
                _exchange_copy(src_refs, land_refs, s_sems, r_sems, a, k, gather, receiving=False).wait_send()
                _exchange_copy(src_refs, land_refs, s_sems, r_sems, a, k, gather, receiving=True).wait_recv()

    hbm = pl.BlockSpec(memory_space=pltpu.HBM)
    sem = pl.BlockSpec(memory_space=pltpu.SEMAPHORE)
    res = pl.pallas_call(
        body, name=name,
        out_shape=tuple(pltpu.HBM(v.shape, v.dtype) for v in srcs + lands),
        in_specs=[hbm] * (2 * n) + [sem, sem, pl.BlockSpec(memory_space=pl.ANY)],
        out_specs=tuple([hbm] * (2 * n)),
        input_output_aliases={i: i for i in range(2 * n)},
        compiler_params=pltpu.CompilerParams(has_side_effects=pltpu.SideEffectType.DATAFLOW_SIDE_EFFECTING),
    )(*srcs, *lands, send_sems, recv_sems, after)
    return list(res[n:])


def _load_resident(step_is_first, pairs, sem):
    @pl.when(step_is_first)
    def _():
        copies = [pltpu.make_async_copy(src, dst, sem.at[i]) for i, (src, dst) in enumerate(pairs)]
        for cp in copies:
            cp.start()
        for cp in copies:
            cp.wait()


def _load_w_in_t(step_is_first, w_hbm, w_vmem, sem):
    @pl.when(step_is_first)
    def _():
        w_vmem[D_IN:, :] = jnp.zeros((D_IN_PAD - D_IN, D_MODEL), BF16)
    _load_resident(step_is_first, [(w_hbm, w_vmem.at[pl.ds(0, D_IN)])], sem)


def _inproj_fwd(x2d, sc1p, sh1, w_in_t, after):
    t = x2d.shape[0]
    tm = min(PROJ_TILE, t)

    def body(x_ref, sc_ref, sh_ref, w_hbm, after_ref, proj_ref, u_ref, w_vmem, sem):
        _load_w_in_t(pl.program_id(0) == 0, w_hbm, w_vmem, sem)
        xh, _ = _ln_stats(x_ref[...])
        ub = (xh * sc_ref[...] + sh_ref[...]).astype(BF16)
        u_ref[...] = ub
        proj_ref[...] = _dot(ub, w_vmem[...], NT)

    row = lambda i: (i, 0)
    fix = lambda i: (0, 0)
    return pl.pallas_call(
        body, name="inproj_fwd", grid=(t // tm,),
        in_specs=[pl.BlockSpec((tm, D_MODEL), row), pl.BlockSpec((1, D_MODEL), fix), pl.BlockSpec((1, D_MODEL), fix),
                  pl.BlockSpec(memory_space=pl.ANY), pl.BlockSpec(memory_space=pl.ANY)],
        out_specs=(pl.BlockSpec((tm, D_IN_PAD), row), pl.BlockSpec((tm, D_MODEL), row)),
        out_shape=(jax.ShapeDtypeStruct((t, D_IN_PAD), F32), jax.ShapeDtypeStruct((t, D_MODEL), BF16)),
        scratch_shapes=[pltpu.VMEM((D_IN_PAD, D_MODEL), BF16), pltpu.SemaphoreType.DMA((1,))],
        compiler_params=pltpu.CompilerParams(dimension_semantics=("arbitrary",), vmem_limit_bytes=V7X_VMEM_LIMIT),
    )(x2d, sc1p, sh1, w_in_t, after)


CHUNK_SHIFT = CHUNK.bit_length() - 1


def _ret_tables(t, tl):
    r = lax.broadcasted_iota(jnp.int32, (tl, tl), 0)
    c = lax.broadcasted_iota(jnp.int32, (tl, tl), 1)
    allowed = jnp.right_shift(c, CHUNK_SHIFT) <= jnp.right_shift(r, CHUNK_SHIFT)
    dist = jnp.abs(r - c).astype(F32)
    rowf = lax.broadcasted_iota(jnp.int32, (tl, RET_D), 0).astype(F32)
    lgs = [_log_gamma(h) for h in range(RET_HEADS)]
    dec = jnp.stack([jnp.where(allowed, jnp.exp(lg * dist), 0.0) for lg in lgs])
    qkd = jnp.stack([jnp.exp(lg * (rowf + 1.0)) for lg in lgs] + [jnp.exp(lg * (tl - 1.0 - rowf)) for lg in lgs])
    inv = 1.0 / (10000.0 ** jnp.linspace(0.0, 1.0, RET_D // 2, dtype=F32))
    off = jnp.arange(tl, dtype=F32)[:, None] * inv[None, :]
    start = (jnp.arange(t // tl, dtype=F32) * tl)[:, None] * inv[None, :]
    co, so = jnp.cos(off), jnp.sin(off)
    rot_in = jnp.stack([jnp.concatenate([co, co], 1), jnp.concatenate([so, so], 1),
                        jnp.concatenate([-co, co], 1), jnp.concatenate([-so, so], 1)])
    cs, ss = jnp.cos(start), jnp.sin(start)
    rot_tile = jnp.concatenate([cs, cs, ss, ss], axis=1)
    rot_tile = jnp.broadcast_to(rot_tile[:, None, :], (t // tl, 8, 2 * RET_D))
    return dec, qkd, rot_in, rot_tile


def _tile_gammas(tl):
    return [float(np.exp(np.float32(_log_gamma(h)) * np.float32(tl))) for h in range(RET_HEADS)]


def _tile_rotary(rot_in_ref, rot_tile_ref, j):
    ca, sa = rot_tile_ref[j, 0:1, 0:RET_D], rot_tile_ref[j, 0:1, RET_D:2 * RET_D]
    cosv = ca * rot_in_ref[0] - sa * rot_in_ref[1]
    sinv = sa * rot_in_ref[2] + ca * rot_in_ref[3]
    return cosv, sinv


def _gla_consts(tl):
    r = lax.broadcasted_iota(jnp.int32, (tl, tl), 0)
    c = lax.broadcasted_iota(jnp.int32, (tl, tl), 1)
    ltri = (c <= r).astype(F32)
    utri = (c >= r).astype(F32)
    lane = lax.broadcasted_iota(jnp.int32, (1, GLA_KW), 1)
    hmask = [((lane >= h * GLA_DK) & (lane < (h + 1) * GLA_DK)).astype(F32) for h in range(GLA_HEADS)]
    rs = lax.broadcasted_iota(jnp.int32, (GLA_HEADS * tl, tl), 0) & (tl - 1)
    cs = lax.broadcasted_iota(jnp.int32, (GLA_HEADS * tl, tl), 1)
    lower = cs <= rs
    same = jnp.right_shift(cs, CHUNK_SHIFT) == jnp.right_shift(rs, CHUNK_SHIFT)
    upper = jnp.logical_and(jnp.logical_not(lower), same)
    return dict(ltri=ltri, utri=utri, hmask=hmask, lower=lower, upper=upper)


def _tile_rows(j, tl):
    return pl.ds(j * tl, tl) if isinstance(j, int) else pl.ds(pl.multiple_of(j * tl, tl), tl)


def _for_tiles(cps, fn):
    for j in range(cps):
        fn(j, 0)


def _rotate(v, cosv, sinv):
    return v * cosv + pltpu.roll(v, RET_D // 2, 1) * sinv


def _rotate_t(d, cosv, sinv):
    return d * cosv + pltpu.roll(d * sinv, RET_D // 2, 1)


def _stack_heads(v, hmask):
    return jnp.concatenate([v * hmask[h] for h in range(GLA_HEADS)], axis=0)


def _gla_gates(glr, gw, gb, ltri, tl):
    z = _dot_split(glr, gw, NN) + gb
    la = (jnp.minimum(z, 0.0) - jnp.log(1.0 + jnp.exp(-jnp.abs(z)))) * (1.0 / GATE_TAU)
    b = _dot_split(ltri, la, NN, a_exact=True)
    level = b[tl // 2 - 1:tl // 2, :]
    ep = jnp.exp(jnp.clip(b - level, -80.0, 80.0))
    em = jnp.exp(jnp.clip(level - b, -80.0, 80.0))
    bl = b[tl - 1:tl, :]
    return z, b, bl, ep, em


def _mixer_fwd(proj, tables, gw_pad, gb, rnw, gnw):
    t = proj.shape[0]
    tc = min(MIX_TILE, t)
    tr, tg = min(RET_SUB, tc), min(GLA_SUB, tc)
    nsteps = t // tc
    scale_r = RET_D ** -0.5
    scale_g = GLA_DK ** -0.5
    gammas = _tile_gammas(tr)

    def body(rq_ref, rk_ref, rv_ref, rg_ref, gq_ref, gk_ref, gv_ref, gg_ref, glr_ref,
             dec_ref, qkd_ref, rot_in_ref, rot_tile_ref, gw_ref, gb_ref, rnw_ref, gnw_ref,
             mix_ref, oraw_ref, qrb_ref, krb_ref, rst_ref, sst_ref, r_scr, s_scr):
        @pl.when(pl.program_id(0) == 0)
        def _():
            r_scr[...] = jnp.zeros_like(r_scr)
            s_scr[...] = jnp.zeros_like(s_scr)

        gla_k = _gla_consts(tg)

        def ret_tile(j, carry):
            rows = _tile_rows(j, tr)
            cosv, sinv = _tile_rotary(rot_in_ref, rot_tile_ref, j)
            for h in range(RET_HEADS):
                cols = slice(h * RET_D, (h + 1) * RET_D)
                qr = _rotate(rq_ref[rows, cols], cosv, sinv) * scale_r
                kr = _rotate(rk_ref[rows, cols], cosv, sinv)
                vb = rv_ref[rows, cols].astype(BF16)
                qb, kb = qr.astype(BF16), kr.astype(BF16)
                qrb_ref[rows, cols] = qb
                krb_ref[rows, cols] = kb
                p = _dot(qb, kb, NT) * dec_ref[h]
                rp = r_scr[cols, :]
                o = _dot(p.astype(BF16), vb) + _dot((qr * qkd_ref[h]).astype(BF16), rp.astype(BF16))
                rst_ref[j, cols, :] = rp
                r_scr[cols, :] = gammas[h] * rp + _dot((kr * qkd_ref[RET_HEADS + h]).astype(BF16), vb, TN)
                oraw_ref[rows, cols] = o
                oc = o - jnp.mean(o, axis=-1, keepdims=True)
                n = oc * lax.rsqrt(jnp.mean(oc * oc, axis=-1, keepdims=True) + LN_EPS)
                g = rg_ref[rows, cols]
                mix_ref[rows, cols] = (n * rnw_ref[:, cols] * (g * _sigmoid(g))).astype(BF16)
            return carry

        def gla_tile(j, carry):
            k = gla_k
            tl = tg
            rows = _tile_rows(j, tg)
            _, b, bl, ep, em = _gla_gates(glr_ref[rows, :], gw_ref[...], gb_ref[...], k["ltri"], tl)
            qs = gq_ref[rows, :] * scale_g
            kk = gk_ref[rows, :]
            x_all = _dot(_stack_heads(qs * ep, k["hmask"]).astype(BF16), (kk * em).astype(BF16), NT)
            y_all = _dot(_stack_heads(qs * em, k["hmask"]).astype(BF16), (kk * ep).astype(BF16), NT)
            a_all = jnp.where(k["lower"], x_all, jnp.where(k["upper"], y_all, 0.0)).astype(BF16)
            st = s_scr[...]
            oq = _dot(_stack_heads(qs * jnp.exp(b), k["hmask"]).astype(BF16), st.astype(BF16), NT)
            kg = kk * jnp.exp(bl - b)
            sst_ref[j] = st
            st_new = st * jnp.exp(bl)
            for h in range(GLA_HEADS):
                cols = slice(h * GLA_DV, (h + 1) * GLA_DV)
                hr = slice(h * tl, (h + 1) * tl)
                vb = gv_ref[rows, cols].astype(BF16)
                o = _dot(a_all[hr, :], vb) + oq[hr, :]
                st_new = st_new + _dot(vb, (kg * k["hmask"][h]).astype(BF16), TN)
                ocols = slice(RET_W + h * GLA_DV, RET_W + (h + 1) * GLA_DV)
                oraw_ref[rows, ocols] = o
                n = o * lax.rsqrt(jnp.mean(o * o, axis=-1, keepdims=True) + LN_EPS)
                g = gg_ref[rows, cols]
                mix_ref[rows, ocols] = (n * gnw_ref[:, cols] * (g * _sigmoid(g))).astype(BF16)
            s_scr[...] = st_new
            return carry

        _for_tiles(tc // tr, ret_tile)
        _for_tiles(tc // tg, gla_tile)

    def col(width, off):
        return pl.BlockSpec((tc, width), lambda i, o=off // width: (i, o))

    fix = lambda i: (0, 0)
    fix3 = lambda i: (0, 0, 0)
    dec, qkd, rot_in, rot_tile = tables
    in_specs = [col(RET_W, OFF_RQ), col(RET_W, OFF_RK), col(RET_W, OFF_RV), col(RET_W, OFF_RG),
                col(GLA_KW, OFF_GQ), col(GLA_KW, OFF_GK), col(GLA_VW, OFF_GV), col(GLA_VW, OFF_GG),
                col(V7X_LANES, OFF_GLR),
                pl.BlockSpec(dec.shape, fix3), pl.BlockSpec(qkd.shape, fix3), pl.BlockSpec(rot_in.shape, fix3),
                pl.BlockSpec((tc // tr, 8, 2 * RET_D), lambda i: (i, 0, 0)),
                pl.BlockSpec((V7X_LANES, GLA_KW), fix), pl.BlockSpec((1, GLA_KW), fix),
                pl.BlockSpec((1, RET_W), fix), pl.BlockSpec((1, GLA_VW), fix)]
    half = pl.BlockSpec((tc, RET_W), lambda i: (i, 0))
    out_specs = (pl.BlockSpec((tc, D_MODEL), lambda i: (i, 0)), pl.BlockSpec((tc, D_MODEL), lambda i: (i, 0)),
                 half, half,
                 pl.BlockSpec((tc // tr, RET_W, RET_D), lambda i: (i, 0, 0)),
                 pl.BlockSpec((tc // tg, GLA_DV, GLA_KW), lambda i: (i, 0, 0)))
    out_shape = (jax.ShapeDtypeStruct((t, D_MODEL), BF16), jax.ShapeDtypeStruct((t, D_MODEL), F32),
                 jax.ShapeDtypeStruct((t, RET_W), BF16), jax.ShapeDtypeStruct((t, RET_W), BF16),
                 jax.ShapeDtypeStruct((t // tr, RET_W, RET_D), F32),
                 jax.ShapeDtypeStruct((t // tg, GLA_DV, GLA_KW), F32))
    return pl.pallas_call(
        body, name="mixer_fwd", grid=(nsteps,), in_specs=in_specs, out_specs=out_specs, out_shape=out_shape,
        scratch_shapes=[pltpu.VMEM((RET_W, RET_D), F32), pltpu.VMEM((GLA_DV, GLA_KW), F32)],
        compiler_params=pltpu.CompilerParams(dimension_semantics=("arbitrary",), vmem_limit_bytes=V7X_VMEM_LIMIT),
    )(*([proj] * 9), dec, qkd, rot_in, rot_tile, gw_pad, gb, rnw, gnw)


def _mid_fwd(mixed, x2d, target, vecs, w_out_b, w1_b, w2_b):
    t = x2d.shape[0]
    tm = min(ROW_TILE, t)

    def body(mix_ref, x_ref, tgt_ref, v_ref, wo_hbm, w1_hbm, w2_hbm,
             m_ref, x1n_ref, rstd_ref, u2_ref, a_ref, df_ref, dh2_ref, acc_ref, wo, w1, w2, sem):
        first = pl.program_id(0) == 0
        _load_resident(first, [(wo_hbm, wo), (w1_hbm, w1), (w2_hbm, w2)], sem)

        @pl.when(first)
        def _():
            acc_ref[...] = jnp.zeros_like(acc_ref)

        gate1, sc2p, sh2, gate2 = v_ref[0:1, :], v_ref[1:2, :], v_ref[2:3, :], v_ref[3:4, :]
        l1w, l1b, l2w, l2b = v_ref[4:5, :], v_ref[5:6, :], v_ref[6:7, :], v_ref[7:8, :]
        m = _dot(mix_ref[...], wo[...])
        m_ref[...] = m.astype(BF16)
        x1n, rstd1 = _ln_stats(ALPHA * x_ref[...] + gate1 * m)
        x1n_ref[...] = x1n
        rstd_ref[...] = rstd1
        x1 = x1n * l1w + l1b
        xh1, _ = _ln_stats(x1)
        u2 = (xh1 * sc2p + sh2).astype(BF16)
        u2_ref[...] = u2
        f = jnp.zeros((tm, D_MODEL), F32)
        for j in range(N_DEV):
            cols = slice(j * FF_COLS, (j + 1) * FF_COLS)
            a = _dot(u2, w1[j])
            a_ref[:, cols] = a.astype(BF16)
            r = jnp.maximum(a, 0.0)
            f = f + _dot((r * r).astype(BF16), w2[cols, :])
        yh, rstd2 = _ln_stats(ALPHA * x1 + gate2 * f)
        e = yh * l2w + l2b - tgt_ref[...]
        dy = e * (1.0 / D_MODEL)
        dh2 = _ln_bwd(dy * l2w, yh, rstd2)
        dh2_ref[...] = dh2
        df_ref[...] = (dh2 * gate2).astype(BF16)
        acc_ref[0:1, :] += jnp.sum(dy * yh, axis=0, keepdims=True)
        acc_ref[1:2, :] += jnp.sum(dy, axis=0, keepdims=True)
        acc_ref[2:3, :] += jnp.sum(dh2 * f, axis=0, keepdims=True)
        acc_ref[3:4, :] += jnp.sum(e * e, axis=0, keepdims=True) * (0.5 / D_MODEL)

    row = lambda i: (i, 0)
    fix = lambda i: (0, 0)
    hbm = pl.BlockSpec(memory_space=pl.ANY)
    return pl.pallas_call(
        body, name="mid_fwd", grid=(t // tm,),
        in_specs=[pl.BlockSpec((tm, D_MODEL), row), pl.BlockSpec((tm, D_MODEL), row), pl.BlockSpec((tm, D_MODEL), row),
                  pl.BlockSpec((8, D_MODEL), fix), hbm, hbm, hbm],
        out_specs=(pl.BlockSpec((tm, D_MODEL), row), pl.BlockSpec((tm, D_MODEL), row), pl.BlockSpec((tm, 1), row),
                   pl.BlockSpec((tm, D_MODEL), row), pl.BlockSpec((tm, D_FF), row), pl.BlockSpec((tm, D_MODEL), row),
                   pl.BlockSpec((tm, D_MODEL), row), pl.BlockSpec((8, D_MODEL), fix)),
        out_shape=(jax.ShapeDtypeStruct((t, D_MODEL), BF16), jax.ShapeDtypeStruct((t, D_MODEL), F32),
                   jax.ShapeDtypeStruct((t, 1), F32), jax.ShapeDtypeStruct((t, D_MODEL), BF16),
                   jax.ShapeDtypeStruct((t, D_FF), BF16), jax.ShapeDtypeStruct((t, D_MODEL), BF16),
                   jax.ShapeDtypeStruct((t, D_MODEL), F32), jax.ShapeDtypeStruct((8, D_MODEL), F32)),
        scratch_shapes=[pltpu.VMEM((D_MODEL, D_MODEL), BF16), pltpu.VMEM((N_DEV, D_MODEL, FF_COLS), BF16),
                        pltpu.VMEM((D_FF, D_MODEL), BF16), pltpu.SemaphoreType.DMA((3,))],
        compiler_params=pltpu.CompilerParams(dimension_semantics=("arbitrary",), vmem_limit_bytes=V7X_VMEM_LIMIT),
    )(mixed, x2d, target, vecs, w_out_b, w1_b, w2_b)


def _ffn_bwd(df, a, dh2, x1n, rstd1, m, vecs, w_out_b, w1_b, w2_b):
    t = x1n.shape[0]
    tm = min(ROW_TILE, t)

    def body(df_ref, a_ref, dh2_ref, x1n_ref, rstd_ref, m_ref, v_ref, wo_hbm, w1_hbm, w2_hbm,
             da_ref, dm_ref, dmix_ref, dxa_ref, acc_ref, wo, w1, w2, sem):
        first = pl.program_id(0) == 0
        _load_resident(first, [(wo_hbm, wo), (w1_hbm, w1), (w2_hbm, w2)], sem)

        @pl.when(first)
        def _():
            acc_ref[...] = jnp.zeros_like(acc_ref)

        gate1, sc2p, l1w, l1b = v_ref[0:1, :], v_ref[1:2, :], v_ref[2:3, :], v_ref[3:4, :]
        df = df_ref[...]
        du2 = jnp.zeros((tm, D_MODEL), F32)
        for j in range(N_DEV):
            cols = slice(j * FF_COLS, (j + 1) * FF_COLS)
            dr2 = _dot(df, w2[cols, :], NT)
            da = (dr2 * (2.0 * jnp.maximum(a_ref[:, cols].astype(F32), 0.0))).astype(BF16)
            da_ref[:, cols] = da
            du2 = du2 + _dot(da, w1[j], NT)
        x1n = x1n_ref[...]
        xh1, rstd0 = _ln_stats(x1n * l1w + l1b)
        dx1 = ALPHA * dh2_ref[...] + _ln_bwd(du2 * sc2p, xh1, rstd0)
        dh1 = _ln_bwd(dx1 * l1w, x1n, rstd_ref[...])
        dxa_ref[...] = ALPHA * dh1
        dm = (dh1 * gate1).astype(BF16)
        dm_ref[...] = dm
        dmix_ref[...] = _dot(dm, wo[...], NT)
        acc_ref[0:1, :] += jnp.sum(du2 * xh1, axis=0, keepdims=True)
        acc_ref[1:2, :] += jnp.sum(du2, axis=0, keepdims=True)
        acc_ref[2:3, :] += jnp.sum(dx1 * x1n, axis=0, keepdims=True)
        acc_ref[3:4, :] += jnp.sum(dx1, axis=0, keepdims=True)
        acc_ref[4:5, :] += jnp.sum(dh1 * m_ref[...].astype(F32), axis=0, keepdims=True)

    row = lambda i: (i, 0)
    fix = lambda i: (0, 0)
    hbm = pl.BlockSpec(memory_space=pl.ANY)
    return pl.pallas_call(
        body, name="ffn_bwd", grid=(t // tm,),
        in_specs=[pl.BlockSpec((tm, D_MODEL), row), pl.BlockSpec((tm, D_FF), row), pl.BlockSpec((tm, D_MODEL), row),
                  pl.BlockSpec((tm, D_MODEL), row), pl.BlockSpec((tm, 1), row), pl.BlockSpec((tm, D_MODEL), row),
                  pl.BlockSpec((8, D_MODEL), fix), hbm, hbm, hbm],
        out_specs=(pl.BlockSpec((tm, D_FF), row), pl.BlockSpec((tm, D_MODEL), row), pl.BlockSpec((tm, D_MODEL), row),
                   pl.BlockSpec((tm, D_MODEL), row), pl.BlockSpec((8, D_MODEL), fix)),
        out_shape=(jax.ShapeDtypeStruct((t, D_FF), BF16), jax.ShapeDtypeStruct((t, D_MODEL), BF16),
                   jax.ShapeDtypeStruct((t, D_MODEL), F32), jax.ShapeDtypeStruct((t, D_MODEL), F32),
                   jax.ShapeDtypeStruct((8, D_MODEL), F32)),
        scratch_shapes=[pltpu.VMEM((D_MODEL, D_MODEL), BF16), pltpu.VMEM((N_DEV, D_MODEL, FF_COLS), BF16),
                        pltpu.VMEM((D_FF, D_MODEL), BF16), pltpu.SemaphoreType.DMA((3,))],
        compiler_params=pltpu.CompilerParams(dimension_semantics=("arbitrary",), vmem_limit_bytes=V7X_VMEM_LIMIT),
    )(df, a, dh2, x1n, rstd1, m, vecs, w_out_b, w1_b, w2_b)


def _matmul_tn(lhs, rhs, tmm, tn, tk, name, relu_sq=False, col_slab=None, out_rows=None):
    t, mm = lhs.shape
    assert out_rows is None or (col_slab is None and tmm == mm)
    nn = rhs.shape[1]
    tk = min(tk, t)
    nk = t // tk

    def body(l_ref, r_ref, o_ref, acc):
        kk = pl.program_id(2)

        @pl.when(kk == 0)
        def _():
            acc[...] = jnp.zeros_like(acc)

        l = l_ref[...]
        if relu_sq:
            lf = jnp.maximum(l.astype(F32), 0.0)
            l = (lf * lf).astype(BF16)
        acc[...] += _dot(l, r_ref[...], TN)

        @pl.when(kk == nk - 1)
        def _():
            if out_rows is not None:
                for s in range(N_DEV):
                    o_ref[s] = acc[s * out_rows:(s + 1) * out_rows, :].astype(o_ref.dtype)
            elif col_slab is None:
                o_ref[...] = acc[...].astype(o_ref.dtype)
            else:
                for s in range(tn // col_slab):
                    o_ref[s] = acc[:, s * col_slab:(s + 1) * col_slab].astype(o_ref.dtype)

    if out_rows is not None:
        out_spec = pl.BlockSpec((N_DEV, out_rows, tn), lambda i, j, k: (0, 0, j))
        out_shape = jax.ShapeDtypeStruct((N_DEV, out_rows, nn), BF16)
    elif col_slab is None:
        out_spec = pl.BlockSpec((tmm, tn), lambda i, j, k: (i, j))
        out_shape = jax.ShapeDtypeStruct((mm, nn), BF16)
    else:
        out_spec = pl.BlockSpec((tn // col_slab, tmm, col_slab), lambda i, j, k: (j, i, 0))
        out_shape = jax.ShapeDtypeStruct((nn // col_slab, mm, col_slab), BF16)
    return pl.pallas_call(
        body, name=name, grid=(mm // tmm, nn // tn, nk),
        in_specs=[pl.BlockSpec((tk, tmm), lambda i, j, k: (k, i)), pl.BlockSpec((tk, tn), lambda i, j, k: (k, j))],
        out_specs=out_spec,
        out_shape=out_shape,
        scratch_shapes=[pltpu.VMEM((tmm, tn), F32)],
        compiler_params=pltpu.CompilerParams(dimension_semantics=("arbitrary", "arbitrary", "arbitrary"),
                                             vmem_limit_bytes=V7X_VMEM_LIMIT),
    )(lhs, rhs)


def _mixer_bwd(dmix, proj, qrb, krb, oraw, tables, rst, sst, gw_pad, gb, rnw, gnw, after):
    t = proj.shape[0]
    tc = min(MIX_TILE, t)
    tr, tg = min(RET_SUB, tc), min(GLA_SUB, tc)
    nsteps = t // tc
    scale_r = RET_D ** -0.5
    scale_g = GLA_DK ** -0.5
    gammas = _tile_gammas(tr)

    def body(dmix_ref, qrb_ref, krb_ref, rv_ref, rg_ref, gq_ref, gk_ref, gv_ref, gg_ref, glr_ref, oraw_ref,
             dec_ref, qkd_ref, rot_in_ref, rot_tile_ref, rst_ref, sst_ref, gw_ref, gb_ref, rnw_ref, gnw_ref, after_ref,
             dproj_ref, dgw_ref, dvec_ref, dr_scr, ds_scr):
        @pl.when(pl.program_id(0) == 0)
        def _():
            dr_scr[...] = jnp.zeros_like(dr_scr)
            ds_scr[...] = jnp.zeros_like(ds_scr)
            dgw_ref[...] = jnp.zeros_like(dgw_ref)
            dvec_ref[...] = jnp.zeros_like(dvec_ref)

        gla_k = _gla_consts(tg)
        last_row = lax.broadcasted_iota(jnp.int32, (tg, GLA_KW), 0) == tg - 1

        def ret_tile(jj, carry):
            j = tc // tr - 1 - jj
            rows = _tile_rows(j, tr)
            cosv, sinv = _tile_rotary(rot_in_ref, rot_tile_ref, j)
            for h in range(RET_HEADS):
                cols = slice(h * RET_D, (h + 1) * RET_D)
                o = oraw_ref[rows, cols]
                g = rg_ref[rows, cols]
                w = rnw_ref[:, cols]
                dout = dmix_ref[rows, cols]
                oc = o - jnp.mean(o, axis=-1, keepdims=True)
                inv = lax.rsqrt(jnp.mean(oc * oc, axis=-1, keepdims=True) + LN_EPS)
                n = oc * inv
                sg = _sigmoid(g)
                sil = g * sg
                dn = dout * w * sil
                dvec_ref[0:1, cols] += jnp.sum(dout * n * sil, axis=0, keepdims=True)
                dproj_ref[rows, OFF_RG + h * RET_D:OFF_RG + (h + 1) * RET_D] = (
                    dout * n * w * (sg * (1.0 + g * (1.0 - sg)))).astype(BF16)
                doc = inv * (dn - n * jnp.mean(dn * n, axis=-1, keepdims=True))
                do = doc - jnp.mean(doc, axis=-1, keepdims=True)

                qb, kb = qrb_ref[rows, cols], krb_ref[rows, cols]
                qr, kr = qb.astype(F32), kb.astype(F32)
                vb = rv_ref[rows, cols].astype(BF16)
                dob = do.astype(BF16)
                qd, kd = qkd_ref[h], qkd_ref[RET_HEADS + h]
                p = _dot(qb, kb, NT) * dec_ref[h]
                rp = rst_ref[j, cols, :].astype(BF16)
                dr = dr_scr[cols, :]
                drb = dr.astype(BF16)
                dpb = (_dot(dob, vb, NT) * dec_ref[h]).astype(BF16)
                dqr = _dot(dpb, kb) + _dot(dob, rp, NT) * qd
                dkr = _dot(dpb, qb, TN) + _dot(vb, drb, NT) * kd
                dv = _dot(p.astype(BF16), dob, TN) + _dot((kr * kd).astype(BF16), drb)
                dr_scr[cols, :] = gammas[h] * dr + _dot((qr * qd).astype(BF16), dob, TN)
                dproj_ref[rows, OFF_RQ + h * RET_D:OFF_RQ + (h + 1) * RET_D] = (
                    _rotate_t(dqr, cosv, sinv) * scale_r).astype(BF16)
                dproj_ref[rows, OFF_RK + h * RET_D:OFF_RK + (h + 1) * RET_D] = _rotate_t(dkr, cosv, sinv).astype(BF16)
                dproj_ref[rows, OFF_RV + h * RET_D:OFF_RV + (h + 1) * RET_D] = dv.astype(BF16)
            return carry

        def gla_tile(jj, carry):
            k = gla_k
            tl = tg
            j = tc // tg - 1 - jj
            rows = _tile_rows(j, tg)
            glr = glr_ref[rows, :]
            z, b, bl, ep, em = _gla_gates(glr, gw_ref[...], gb_ref[...], k["ltri"], tl)
            qs = gq_ref[rows, :] * scale_g
            kk = gk_ref[rows, :]
            eb = jnp.exp(b)
            ekb = jnp.exp(bl - b)
            ebl = jnp.exp(bl)
            ql, qu, kl, ku = qs * ep, qs * em, kk * em, kk * ep
            qg, kg = qs * eb, kk * ekb
            qlm = _stack_heads(ql, k["hmask"]).astype(BF16)
            qum = _stack_heads(qu, k["hmask"]).astype(BF16)
            klb, kub = kl.astype(BF16), ku.astype(BF16)
            a_all = jnp.where(k["lower"], _dot(qlm, klb, NT),
                              jnp.where(k["upper"], _dot(qum, kub, NT), 0.0)).astype(BF16)
            st = sst_ref[j]
            stb = st.astype(BF16)
            ds = ds_scr[...]
            dsb = ds.astype(BF16)
            ds_new = ds * ebl
            da_parts = []
            dqg = jnp.zeros((tl, GLA_KW), F32)
            dkg = jnp.zeros((tl, GLA_KW), F32)
            for h in range(GLA_HEADS):
                cols = slice(h * GLA_DV, (h + 1) * GLA_DV)
                hr = slice(h * tl, (h + 1) * tl)
                ocols = slice(RET_W + h * GLA_DV, RET_W + (h + 1) * GLA_DV)
                o = oraw_ref[rows, ocols]
                g = gg_ref[rows, cols]
                w = gnw_ref[:, cols]
                dout = dmix_ref[rows, ocols]
                inv = lax.rsqrt(jnp.mean(o * o, axis=-1, keepdims=True) + LN_EPS)
                n = o * inv
                sg = _sigmoid(g)
                sil = g * sg
                dn = dout * w * sil
                dvec_ref[1:2, cols] += jnp.sum(dout * n * sil, axis=0, keepdims=True)
                dproj_ref[rows, OFF_GG + h * GLA_DV:OFF_GG + (h + 1) * GLA_DV] = (
                    dout * n * w * (sg * (1.0 + g * (1.0 - sg)))).astype(BF16)
                dob = (inv * (dn - n * jnp.mean(dn * n, axis=-1, keepdims=True))).astype(BF16)
                vb = gv_ref[rows, cols].astype(BF16)
                mh = k["hmask"][h]
                da_parts.append(_dot(dob, vb, NT))
                dv = _dot(a_all[hr, :], dob, TN) + _dot((kg * mh).astype(BF16), dsb, NT)
                dproj_ref[rows, OFF_GV + h * GLA_DV:OFF_GV + (h + 1) * GLA_DV] = dv.astype(BF16)
                dkg = dkg + mh * _dot(vb, dsb)
                dqg = dqg + mh * _dot(dob, stb)
                ds_new = ds_new + _dot(dob, (qg * mh).astype(BF16), TN)
            da_all = jnp.concatenate(da_parts, axis=0)
            dal = jnp.where(k["lower"], da_all, 0.0).astype(BF16)
            dau = jnp.where(k["upper"], da_all, 0.0).astype(BF16)
            dqlm = _dot(dal, klb)
            dqum = _dot(dau, kub)
            dql = jnp.zeros((tl, GLA_KW), F32)
            dqu = jnp.zeros((tl, GLA_KW), F32)
            for h in range(GLA_HEADS):
                hr = slice(h * tl, (h + 1) * tl)
                dql = dql + k["hmask"][h] * dqlm[hr, :]
                dqu = dqu + k["hmask"][h] * dqum[hr, :]
            dkl = _dot(dal, qlm, TN)
            dku = _dot(dau, qum, TN)
            dbl = (jnp.sum(dkg * kg, axis=0, keepdims=True)
                   + jnp.sum(ds * st, axis=0, keepdims=True) * ebl)
            ds_scr[...] = ds_new
            dqs = dql * ep + dqu * em + dqg * eb
            dk = dkl * em + dku * ep + dkg * ekb
            db = dql * ql - dkl * kl - dqu * qu + dku * ku + dqg * qg - dkg * kg
            db = db + jnp.where(last_row, dbl, 0.0)
            dla = _dot_split(k["utri"], db, NN, a_exact=True)
            dz = dla * (1.0 / GATE_TAU) * _sigmoid(-z)
            dvec_ref[2:3, 0:GLA_KW] += jnp.sum(dz, axis=0, keepdims=True)
            dgw_ref[...] += _dot_split(glr, dz, TN)
            dproj_ref[rows, OFF_GLR:D_IN_PAD] = _dot(dz.astype(BF16), gw_ref[...].astype(BF16), NT).astype(BF16)
            dproj_ref[rows, OFF_GQ:OFF_GQ + GLA_KW] = (dqs * scale_g).astype(BF16)
            dproj_ref[rows, OFF_GK:OFF_GK + GLA_KW] = dk.astype(BF16)
            return carry

        _for_tiles(tc // tr, ret_tile)
        _for_tiles(tc // tg, gla_tile)

    rev = lambda i: (nsteps - 1 - i, 0)

    def col(width, off):
        return pl.BlockSpec((tc, width), lambda i, o=off // width: (nsteps - 1 - i, o))

    fix = lambda i: (0, 0)
    fix3 = lambda i: (0, 0, 0)
    dec, qkd, rot_in, rot_tile = tables
    half = pl.BlockSpec((tc, RET_W), rev)
    in_specs = [pl.BlockSpec((tc, D_MODEL), rev), half, half, col(RET_W, OFF_RV), col(RET_W, OFF_RG),
                col(GLA_KW, OFF_GQ), col(GLA_KW, OFF_GK), col(GLA_VW, OFF_GV), col(GLA_VW, OFF_GG),
                col(V7X_LANES, OFF_GLR),
                pl.BlockSpec((tc, D_MODEL), rev),
                pl.BlockSpec(dec.shape, fix3), pl.BlockSpec(qkd.shape, fix3), pl.BlockSpec(rot_in.shape, fix3),
                pl.BlockSpec((tc // tr, 8, 2 * RET_D), lambda i: (nsteps - 1 - i, 0, 0)),
                pl.BlockSpec((tc // tr, RET_W, RET_D), lambda i: (nsteps - 1 - i, 0, 0)),
                pl.BlockSpec((tc // tg, GLA_DV, GLA_KW), lambda i: (nsteps - 1 - i, 0, 0)),
                pl.BlockSpec((V7X_LANES, GLA_KW), fix), pl.BlockSpec((1, GLA_KW), fix),
                pl.BlockSpec((1, RET_W), fix), pl.BlockSpec((1, GLA_VW), fix), pl.BlockSpec(memory_space=pl.ANY)]
    out_specs = (pl.BlockSpec((tc, D_IN_PAD), rev), pl.BlockSpec((V7X_LANES, GLA_KW), fix),
                 pl.BlockSpec((8, RET_W), fix))
    out_shape = (jax.ShapeDtypeStruct((t, D_IN_PAD), BF16), jax.ShapeDtypeStruct((V7X_LANES, GLA_KW), F32),
                 jax.ShapeDtypeStruct((8, RET_W), F32))
    return pl.pallas_call(
        body, name="mixer_bwd", grid=(nsteps,), in_specs=in_specs, out_specs=out_specs, out_shape=out_shape,
        scratch_shapes=[pltpu.VMEM((RET_W, RET_D), F32), pltpu.VMEM((GLA_DV, GLA_KW), F32)],
        compiler_params=pltpu.CompilerParams(dimension_semantics=("arbitrary",), vmem_limit_bytes=V7X_VMEM_LIMIT),
    )(dmix, qrb, krb, *([proj] * 7), oraw, dec, qkd, rot_in, rot_tile, rst, sst, gw_pad, gb, rnw, gnw, after)


def _inproj_bwd(dproj, x2d, dxa, sc1p, w_in_t, after):
    t = x2d.shape[0]
    tm = min(2 * PROJ_TILE, t)

    def body(dp_ref, x_ref, dxa_ref, sc_ref, w_hbm, after_ref, gx_ref, acc_ref, w_vmem, sem):
        first = pl.program_id(0) == 0
        _load_w_in_t(first, w_hbm, w_vmem, sem)

        @pl.when(first)
        def _():
            acc_ref[...] = jnp.zeros_like(acc_ref)

        du = _dot(dp_ref[...], w_vmem[...])
        xh, rstd = _ln_stats(x_ref[...])
        gx_ref[...] = dxa_ref[...] + _ln_bwd(du * sc_ref[...], xh, rstd)
        acc_ref[0:1, :] += jnp.sum(du * xh, axis=0, keepdims=True)
        acc_ref[1:2, :] += jnp.sum(du, axis=0, keepdims=True)

    row = lambda i: (i, 0)
    fix = lambda i: (0, 0)
    return pl.pallas_call(
        body, name="inproj_bwd", grid=(t // tm,),
        in_specs=[pl.BlockSpec((tm, D_IN_PAD), row), pl.BlockSpec((tm, D_MODEL), row), pl.BlockSpec((tm, D_MODEL), row),
                  pl.BlockSpec((1, D_MODEL), fix), pl.BlockSpec(memory_space=pl.ANY), pl.BlockSpec(memory_space=pl.ANY)],
        out_specs=(pl.BlockSpec((tm, D_MODEL), row), pl.BlockSpec((8, D_MODEL), fix)),
        out_shape=(jax.ShapeDtypeStruct((t, D_MODEL), F32), jax.ShapeDtypeStruct((8, D_MODEL), F32)),
        scratch_shapes=[pltpu.VMEM((D_IN_PAD, D_MODEL), BF16), pltpu.SemaphoreType.DMA((1,))],
        compiler_params=pltpu.CompilerParams(dimension_semantics=("arbitrary",), vmem_limit_bytes=V7X_VMEM_LIMIT),
    )(dproj, x2d, dxa, sc1p, w_in_t, after)


def _adam_math(w, g, m, v):
    m = ADAM_B1 * m + (1.0 - ADAM_B1) * g
    v = ADAM_B2 * v + (1.0 - ADAM_B2) * (g * g)
    m_hat = m / (1.0 - ADAM_B1 ** ADAM_STEP)
    v_hat = v / (1.0 - ADAM_B2 ** ADAM_STEP)
    delta = -ADAM_LR * (m_hat / (jnp.sqrt(v_hat) + ADAM_EPS) + ADAM_WD * w)
    return delta, m, v


def _adamw(w, gparts, m, v, name, row_tiles=False):
    nparts, rows, cols = gparts.shape
    tr = rows
    for cand in (512, 256, 128, 64, 32, 16, 8):
        if rows % cand == 0:
            tr = cand
            break

    def body(w_ref, g_ref, m_ref, v_ref, go_ref, d_ref, mo_ref, vo_ref):
        g = g_ref[0].astype(F32)
        for p in range(1, nparts):
            g = g + g_ref[p].astype(F32)
        if row_tiles:
            g = g.reshape(tr, cols // V7X_LANES, V7X_LANES)
        delta, mn, vn = _adam_math(w_ref[...], g, m_ref[...], v_ref[...])
        go_ref[...] = g
        d_ref[...] = delta
        mo_ref[...] = mn
        vo_ref[...] = vn

    blk = pl.BlockSpec((tr, cols), lambda i: (i, 0))
    shp = jax.ShapeDtypeStruct((rows, cols), F32)
    if row_tiles:
        blk = pl.BlockSpec((tr, cols // V7X_LANES, V7X_LANES), lambda i: (i, 0, 0))
        shp = jax.ShapeDtypeStruct((rows, cols // V7X_LANES, V7X_LANES), F32)
    return pl.pallas_call(
        body, name=name, grid=(rows // tr,),
        in_specs=[blk, pl.BlockSpec((nparts, tr, cols), lambda i: (0, i, 0)), blk, blk],
        out_specs=(blk, blk, blk, blk), out_shape=(shp, shp, shp, shp),
        compiler_params=pltpu.CompilerParams(dimension_semantics=("arbitrary",), vmem_limit_bytes=V7X_VMEM_LIMIT),
    )(w, gparts, m, v)


def _small_reduce(gathered, gathered_gw, c_all, dmod_cols):
    def body(g_ref, gw_ref, c_ref, dm_ref, sum_ref, gwsum_ref, gb_ref, gwa_ref):
        s = g_ref[0]
        sw = gw_ref[0]
        for p in range(1, N_DEV):
            s = s + g_ref[p]
            sw = sw + gw_ref[p]
        sum_ref[...] = s
        gwsum_ref[...] = sw
        for i in range(6):
            gb_ref[:, i * D_MODEL:(i + 1) * D_MODEL] = s[i:i + 1, :]
        cc = c_ref[...]
        gwa_ref[...] = _dot(cc * _sigmoid(cc), dm_ref[...], TN, HIGHEST)

    vm = pl.BlockSpec(memory_space=pltpu.VMEM)
    return pl.pallas_call(
        body, name="small_reduce",
        out_shape=(jax.ShapeDtypeStruct(gathered.shape[1:], F32), jax.ShapeDtypeStruct(gathered_gw.shape[1:], F32),
                   jax.ShapeDtypeStruct((1, 6 * D_MODEL), F32), jax.ShapeDtypeStruct((D_MODEL, ADA_COLS), F32)),
        in_specs=[vm] * 4, out_specs=(vm, vm, vm, vm),
        compiler_params=pltpu.CompilerParams(vmem_limit_bytes=V7X_VMEM_LIMIT),
    )(gathered, gathered_gw, c_all, dmod_cols)


SMR_LN1W, SMR_LN1B, SMR_LN2W, SMR_LN2B, SMR_NORMS, SMR_MISC = 6, 7, 8, 9, 10, 11


def _adamw_small(gsum, g_b_ada, g_ggw, params, moms, vels):
    n = len(params)

    def body(*refs):
        gsum_ref, gb_ref, gw_ref = refs[:3]
        w_refs, m_refs, v_refs = refs[3:3 + n], refs[3 + n:3 + 2 * n], refs[3 + 2 * n:3 + 3 * n]
        outs = refs[3 + 3 * n:]
        g_refs, d_refs, mo_refs, vo_refs = outs[:n - 1], outs[n - 1:2 * n - 1], outs[2 * n - 1:3 * n - 1], outs[3 * n - 1:]
        grads = [gb_ref[...],
                 gsum_ref[SMR_NORMS:SMR_NORMS + 1, 0:RET_W],
                 gsum_ref[SMR_MISC:SMR_MISC + 1, 0:GLA_KW],
                 gsum_ref[SMR_NORMS:SMR_NORMS + 1, RET_W:RET_W + GLA_VW],
                 gsum_ref[SMR_LN1W:SMR_LN1W + 1, :], gsum_ref[SMR_LN1B:SMR_LN1B + 1, :],
                 gsum_ref[SMR_LN2W:SMR_LN2W + 1, :], gsum_ref[SMR_LN2B:SMR_LN2B + 1, :],
                 gw_ref[...]]
        for i in range(n):
            delta, mn, vn = _adam_math(w_refs[i][...], grads[i], m_refs[i][...], v_refs[i][...])
            if i < n - 1:
                g_refs[i][...] = grads[i]
            d_refs[i][...] = delta
            mo_refs[i][...] = mn
            vo_refs[i][...] = vn

    vm = pl.BlockSpec(memory_space=pltpu.VMEM)
    shapes = [jax.ShapeDtypeStruct(p.shape, F32) for p in params]
    n_in = 3 + 3 * n
    out_shape = tuple(shapes[:n - 1] + shapes * 3)
    return pl.pallas_call(
        body, name="adamw_small", out_shape=out_shape,
        in_specs=[vm] * n_in, out_specs=tuple([vm] * len(out_shape)),
        compiler_params=pltpu.CompilerParams(vmem_limit_bytes=V7X_VMEM_LIMIT),
    )(gsum, g_b_ada, g_ggw, *params, *moms, *vels)


def kernel(x, c, w_ada, b_ada, w_in, ret_norm_w, gla_gate_w, gla_gate_b, gla_norm_w, w_out, ln1_w, ln1_b, w_ff1, w_ff2, ln2_w, ln2_b, loss_target, m_w_ada, m_b_ada, m_w_in, m_ret_norm_w, m_gla_gate_w, m_gla_gate_b, m_gla_norm_w, m_w_out, m_ln1_w, m_ln1_b, m_w_ff1, m_w_ff2, m_ln2_w, m_ln2_b, v_w_ada, v_b_ada, v_w_in, v_ret_norm_w, v_gla_gate_w, v_gla_gate_b, v_gla_norm_w, v_w_out, v_ln1_w, v_ln1_b, v_w_ff1, v_w_ff2, v_ln2_w, v_ln2_b):
    t = x.shape[1]
    xi, yi, ci = _my_coords()
    me = 4 * xi + 2 * yi + ci
    x2d = x[0]
    tgt = loss_target[0]

    c_ext = jnp.concatenate([c, gla_gate_w[0].reshape(1, GATE_RANK * GLA_KW // N_DEV)], axis=1)
    b_l = lax.dynamic_slice(b_ada, (0, me * ADA_COLS), (1, ADA_COLS))
    c_all3, mod_all, wi_g = _adaln_mod(c_ext, w_ada[0], b_l, w_in[0].T.astype(BF16))

    wg = _exchange_start([w_out[0].astype(BF16), w_ff1[0].astype(BF16), w_ff2[0].astype(BF16)],
                         True, "wgather_start", after=wi_g)

    c_all = c_all3[:, 0, :D_MODEL]
    gate_w = c_all3[:, 0, D_MODEL:].reshape(N_DEV, GATE_RANK, GLA_KW // N_DEV)
    gate_w = gate_w.transpose(1, 0, 2).reshape(GATE_RANK, GLA_KW)
    gw_pad = jnp.zeros((V7X_LANES, GLA_KW), F32).at[:GATE_RANK].set(gate_w)
    mod = lax.dynamic_slice(mod_all, (0, me, 0), (N_DEV, 1, ADA_COLS)).reshape(6, D_MODEL)
    shift1, scale1, gate1, shift2, scale2, gate2 = [mod[i:i + 1] for i in range(6)]

    w_in_t = wg[5].reshape(D_IN, D_MODEL)

    tables = _ret_tables(t, min(RET_SUB, t))

    sc1p = 1.0 + scale1
    proj, u = _inproj_fwd(x2d, sc1p, shift1, w_in_t, after=wg[4])
    mixed, oraw, qrb, krb, rst, sst = _mixer_fwd(proj, tables, gw_pad, gla_gate_b, ret_norm_w, gla_norm_w)
    wo_g, w1_b, w2_g = _exchange_wait(*wg[:4], mixed, True, "wgather_wait")
    w_out_b = wo_g.reshape(D_MODEL, D_MODEL)
    w2_b = w2_g.reshape(D_FF, D_MODEL)
    vec_f = jnp.concatenate([gate1, 1.0 + scale2, shift2, gate2, ln1_w, ln1_b, ln2_w, ln2_b], axis=0)
    m, x1n, rstd1, u2, a, df, dh2, acc_f = _mid_fwd(mixed, x2d, tgt, vec_f, w_out_b, w1_b, w2_b)

    vec_b = jnp.concatenate([gate1, 1.0 + scale2, ln1_w, ln1_b, jnp.zeros((4, D_MODEL), F32)], axis=0)
    da, dm, dmix, dxa, acc_b = _ffn_bwd(df, a, dh2, x1n, rstd1, m, vec_b, w_out_b, w1_b, w2_b)
    dw2 = _matmul_tn(a, df, 2048, 1024, 2048, "tn_dw2", relu_sq=True)
    dw1 = _matmul_tn(u2, da, 1024, 2048, 2048, "tn_dw1", col_slab=FF_COLS)
    dwo = _matmul_tn(mixed, dm, 1024, 1024, 2048, "tn_dwout")
    gx = _exchange_start([dwo.reshape(N_DEV, OUT_ROWS, D_MODEL), dw1, dw2.reshape(N_DEV, FF_COLS, D_MODEL)], False,
                         "gradx_start")
    dproj, dgw, dvec = _mixer_bwd(dmix, proj, qrb, krb, oraw, tables, rst, sst, gw_pad,
                                  gla_gate_b, ret_norm_w, gla_norm_w, after=gx[4])
    dwi_s = _matmul_tn(dproj, u, D_IN_PAD, 1024, 1024, "tn_dwin", out_rows=IN_COLS)
    gi = _exchange_start([dwi_s], False, "gradin_start")
    grad_x, acc_i = _inproj_bwd(dproj, x2d, dxa, sc1p, w_in_t, after=gi[4])

    loss_part = jnp.sum(acc_f[3])
    small = jnp.concatenate([
        acc_i[1:2], acc_i[0:1], acc_b[4:5], acc_b[1:2], acc_b[0:1], acc_f[2:3],
        acc_b[2:3], acc_b[3:4], acc_f[0:1], acc_f[1:2],
        jnp.concatenate([dvec[0:1], dvec[1:2]], axis=1),
        jnp.concatenate([dvec[2:3, :GLA_KW], jnp.full((1, 128), loss_part, F32),
                         jnp.zeros((1, D_MODEL - GLA_KW - 128), F32)], axis=1),
        jnp.zeros((4, D_MODEL), F32)], axis=0)
    sg = _exchange_start([small, dgw[:GATE_RANK]], True, "small_start")

    r_wo, r_w1, r_w2 = _exchange_wait(*gx[:4], sg[4], False, "gradx_wait")
    r_wi, = _exchange_wait(*gi[:4], sg[4], False, "gradin_wait")
    big = [_adamw(w[0], r, m_[0], v_[0], nm) for w, r, m_, v_, nm in (
        (w_out, r_wo, m_w_out, v_w_out, "adamw_out"),
        (w_ff1, r_w1, m_w_ff1, v_w_ff1, "adamw_ff1"), (w_ff2, r_w2, m_w_ff2, v_w_ff2, "adamw_ff2"))]
    tiles = lambda a: a.T.reshape(IN_COLS, D_MODEL // V7X_LANES, V7X_LANES)
    big_in = _adamw(tiles(w_in[0]), r_wi, tiles(m_w_in[0]), tiles(v_w_in[0]), "adamw_in", row_tiles=True)
    big = [tuple(b.reshape(IN_COLS, D_MODEL).T for b in big_in)] + big
    g_big, d_big, m_big, v_big = [[b[i][None] for b in big] for i in range(4)]

    small_all, gw_all = _exchange_wait(*sg[:4], big_in[1], True, "small_wait")
    dmod_all = small_all[:, :6].reshape(N_DEV, 6 * D_MODEL)
    dmod_cols = lax.dynamic_slice(dmod_all, (0, me * ADA_COLS), (N_DEV, ADA_COLS))
    ssum, gw_sum, g_b_ada, g_w_ada = _small_reduce(small_all, gw_all, c_all, dmod_cols)
    loss = ssum[SMR_MISC, GLA_KW]
    g_ggw = lax.dynamic_slice(gw_sum, (0, me * (GLA_KW // N_DEV)), (GATE_RANK, GLA_KW // N_DEV))[None]

    small_w = [b_ada, ret_norm_w, gla_gate_b, gla_norm_w, ln1_w, ln1_b, ln2_w, ln2_b, gla_gate_w]
    small_m = [m_b_ada, m_ret_norm_w, m_gla_gate_b, m_gla_norm_w, m_ln1_w, m_ln1_b, m_ln2_w, m_ln2_b, m_gla_gate_w]
    small_v = [v_b_ada, v_ret_norm_w, v_gla_gate_b, v_gla_norm_w, v_ln1_w, v_ln1_b, v_ln2_w, v_ln2_b, v_gla_gate_w]
    res = _adamw_small(ssum, g_b_ada, g_ggw, small_w, small_m, small_v)
    small_g = list(res[:8]) + [g_ggw]
    d_small, m_small, v_small = list(res[8:17]), list(res[17:26]), list(res[26:35])

    _, d_w_ada, nm_w_ada, nv_w_ada = _adamw(w_ada[0], g_w_ada[None], m_w_ada[0], v_w_ada[0], "adamw_ada")

    def ordered(w_ada_v, small_vals, big_vals):
        b_ada_v, rnw_v, ggb_v, gnw_v, l1w_v, l1b_v, l2w_v, l2b_v, ggw_v = small_vals
        wi_v, wo_v, w1_v, w2_v = big_vals
        return [w_ada_v, b_ada_v, wi_v, rnw_v, ggw_v, ggb_v, gnw_v, wo_v, l1w_v, l1b_v, w1_v, w2_v, l2w_v, l2b_v]

    grads = ordered(g_w_ada[None], small_g, g_big)
    deltas = ordered(d_w_ada[None], d_small, d_big)
    new_m = ordered(nm_w_ada[None], m_small, m_big)
    new_v = ordered(nv_w_ada[None], v_small, v_big)
    return (loss, grad_x[None], *grads, *deltas, *new_m, *new_v)
```

```python
import numpy as np
import jax
import jax.numpy as jnp
from jax import lax
from jax.experimental import pallas as pl
from jax.experimental.pallas import tpu as pltpu

F32 = jnp.float32
BF16 = jnp.bfloat16
MESH = pl.DeviceIdType.MESH
HIGHEST = lax.Precision.HIGHEST

N_DEV = 8
D_MODEL = 1024
CHUNK = 64
RET_HEADS = 4
RET_D = 128
GLA_HEADS = 4
GLA_DK = 64
GLA_DV = 128
GLA_KW = GLA_HEADS * GLA_DK
RET_W = RET_HEADS * RET_D
GLA_VW = GLA_HEADS * GLA_DV
V7X_LANES = 128
GATE_RANK = 16
GATE_TAU = 16.0
D_FF = 4096
LN_EPS = 1e-5
ALPHA = (2.0 * 1) ** 0.25
D_IN = 3600
D_IN_PAD = 3712
ADA_COLS = 6 * D_MODEL // N_DEV
IN_COLS = D_IN // N_DEV
FF_COLS = D_FF // N_DEV
OUT_ROWS = D_MODEL // N_DEV

OFF_RQ, OFF_RK, OFF_RV, OFF_RG = 0, RET_W, 2 * RET_W, 3 * RET_W
OFF_GQ = 4 * RET_W
OFF_GK = OFF_GQ + GLA_KW
OFF_GV = OFF_GK + GLA_KW
OFF_GG = OFF_GV + GLA_VW
OFF_GLR = OFF_GG + GLA_VW

ADAM_LR, ADAM_B1, ADAM_B2, ADAM_EPS, ADAM_WD, ADAM_STEP = 0.001, 0.9, 0.999, 1e-08, 0.01, 10

V7X_VMEM_LIMIT = 62 * 1024 * 1024

ROW_TILE = 512
PROJ_TILE = 512
MIX_TILE = 512
RET_SUB = 256
GLA_SUB = 128


def _log_gamma(h):
    return float(np.log(np.float32(1.0) - np.float32(2.0) ** np.float32(-5.0 - h)))


def _my_coords():
    return lax.axis_index("x"), lax.axis_index("y"), lax.axis_index("c")


def _flip(v, bit):
    return 1 - v if bit else v


def _peer(k):
    x, y, c = _my_coords()
    px, py, pc = _flip(x, (k >> 2) & 1), _flip(y, (k >> 1) & 1), _flip(c, k & 1)
    return (px, py, pc), 4 * px + 2 * py + pc


def _dot(a, b, dims=(((1,), (0,)), ((), ())), precision=None):
    return lax.dot_general(a, b, dims, precision=precision, preferred_element_type=F32)


NN = (((1,), (0,)), ((), ()))
NT = (((1,), (1,)), ((), ()))
TN = (((0,), (0,)), ((), ()))


def _split_bf16(v, parts):
    out = []
    for _ in range(parts):
        p = v.astype(BF16)
        out.append(p)
        v = v - p.astype(F32)
    return out


def _dot_split(a, b, dims, a_exact=False):
    if a_exact:
        ab = a.astype(BF16)
        return sum(_dot(ab, p, dims) for p in _split_bf16(b, 2))
    a_hi, a_lo = _split_bf16(a, 2)
    b_hi, b_lo = _split_bf16(b, 2)
    return _dot(a_hi, b_hi, dims) + _dot(a_hi, b_lo, dims) + _dot(a_lo, b_hi, dims)


def _sigmoid(x):
    return 1.0 / (1.0 + jnp.exp(-x))


def _ln_stats(x):
    mu = jnp.mean(x, axis=-1, keepdims=True)
    xc = x - mu
    var = jnp.mean(xc * xc, axis=-1, keepdims=True)
    rstd = lax.rsqrt(var + LN_EPS)
    return xc * rstd, rstd


def _ln_bwd(dyh, xh, rstd):
    return rstd * (dyh - jnp.mean(dyh, axis=-1, keepdims=True) - xh * jnp.mean(dyh * xh, axis=-1, keepdims=True))


def _adaln_mod(c_ext, w_ada_l, b_l, w_in_l):
    width = c_ext.shape[1]

    def body(c_ref, w_ref, b_ref, wi_ref, call_ref, mod_ref, wig_ref, s1, r1, s2, r2, gs, gr, gl):
        gather = _TwoLevelGather([wi_ref], [wig_ref], gs, gr, gl)
        gather.start()
        x, y, c = _my_coords()
        me = 4 * x + 2 * y + c
        call_ref[me] = c_ref[...]
        sends = []
        for k in range(1, N_DEV):
            peer, _ = _peer(k)
            cp = pltpu.make_async_remote_copy(c_ref, call_ref.at[me], s1.at[k - 1], r1.at[k - 1],
                                              device_id=peer, device_id_type=MESH)
            cp.start()
            sends.append(cp)
        for k in range(1, N_DEV):
            peer, pid = _peer(k)
            pltpu.make_async_remote_copy(c_ref, call_ref.at[pid], s1.at[k - 1], r1.at[k - 1],
                                         device_id=peer, device_id_type=MESH).wait_recv()
        for cp in sends:
            cp.wait_send()
        row = lax.broadcasted_iota(jnp.int32, (N_DEV, D_MODEL), 0)
        call = jnp.zeros((N_DEV, D_MODEL), F32)
        for j in range(N_DEV):
            call = jnp.where(row == j, jnp.broadcast_to(call_ref[j][:, :D_MODEL], (N_DEV, D_MODEL)), call)
        sc = call * _sigmoid(call)
        mod = _dot(sc, w_ref[...], NN, HIGHEST) + b_ref[...]
        mod_ref[me] = mod
        sends = []
        for k in range(1, N_DEV):
            peer, _ = _peer(k)
            cp = pltpu.make_async_remote_copy(mod_ref.at[me], mod_ref.at[me], s2.at[k - 1], r2.at[k - 1],
                                              device_id=peer, device_id_type=MESH)
            cp.start()
            sends.append(cp)
        for k in range(1, N_DEV):
            peer, pid = _peer(k)
            pltpu.make_async_remote_copy(mod_ref.at[pid], mod_ref.at[pid], s2.at[k - 1], r2.at[k - 1],
                                         device_id=peer, device_id_type=MESH).wait_recv()
        for cp in sends:
            cp.wait_send()
        gather.forward()
        gather.finish()

    vm = pl.BlockSpec(memory_space=pltpu.VMEM)
    hbm = pl.BlockSpec(memory_space=pl.ANY)
    return pl.pallas_call(
        body, name="adaln_mod",
        out_shape=(jax.ShapeDtypeStruct((N_DEV, 1, width), F32),
                   jax.ShapeDtypeStruct((N_DEV, N_DEV, ADA_COLS), F32),
                   jax.ShapeDtypeStruct((N_DEV, *w_in_l.shape), w_in_l.dtype)),
        in_specs=[vm, vm, vm, hbm], out_specs=(vm, vm, hbm),
        scratch_shapes=[pltpu.SemaphoreType.DMA((N_DEV - 1,))] * 4
        + [pltpu.SemaphoreType.DMA((7,)), pltpu.SemaphoreType.DMA((7,)), pltpu.SemaphoreType.DMA((1,))],
        compiler_params=pltpu.CompilerParams(vmem_limit_bytes=V7X_VMEM_LIMIT),
    )(c_ext, w_ada_l, b_l, w_in_l)


class _TwoLevelGather:
    def __init__(self, x_refs, out_refs, send_sems, recv_sems, local_sems):
        self.x_refs, self.out_refs = x_refs, out_refs
        self.send_sems, self.recv_sems, self.local_sems = send_sems, recv_sems, local_sems
        x, y, c = _my_coords()
        self.c = c
        self.me, self.sibling = (x, y, c), (x, y, 1 - c)
        self.chips = [(1 - x, y), (x, 1 - y), (1 - x, 1 - y)]

    def _copy(self, a, k, block, to, src=None):
        px, py, pc = block
        slab = self.out_refs[a].at[4 * px + 2 * py + pc]
        return pltpu.make_async_remote_copy(
            src_ref=slab if src is None else src, dst_ref=slab,
            send_sem=self.send_sems.at[7 * a + k], recv_sem=self.recv_sems.at[7 * a + k],
            device_id=to, device_id_type=MESH)

    def _mine(self, a):
        px, py, pc = self.me
        return pltpu.make_async_copy(self.x_refs[a], self.out_refs[a].at[4 * px + 2 * py + pc], self.local_sems.at[a])

    def _first(self, a):
        cps = [self._copy(a, 0, self.me, self.sibling, src=self.x_refs[a])]
        cps += [self._copy(a, 1 + j, self.me, (*chip, self.c), src=self.x_refs[a]) for j, chip in enumerate(self.chips)]
        return cps

    def _passed(self, a):
        return [self._copy(a, 4 + j, (*chip, self.c), self.sibling) for j, chip in enumerate(self.chips)]

    def start(self):
        for a in range(len(self.x_refs)):
            self._mine(a).start()
            for cp in self._first(a):
                cp.start()

    def forward(self):
        for a in range(len(self.x_refs)):
            passed = self._passed(a)
            for j, chip in enumerate(self.chips):
                self._copy(a, 1 + j, (*chip, self.c), self.me).wait_recv()
                passed[j].start()

    def finish(self):
        for a in range(len(self.x_refs)):
            self._copy(a, 0, self.sibling, self.me).wait_recv()
            for j, chip in enumerate(self.chips):
                self._copy(a, 4 + j, (*chip, 1 - self.c), self.me).wait_recv()
            for cp in self._first(a) + self._passed(a):
                cp.wait_send()
            self._mine(a).wait()


def _exchange_copy(src_refs, land_refs, send_sems, recv_sems, a, k, gather, receiving):
    x, y, c = _my_coords()
    me = 4 * x + 2 * y + c
    peer, pid = _peer(k)
    src = src_refs[a] if gather else src_refs[a].at[pid]
    dst = land_refs[a].at[pid if receiving else me]
    return pltpu.make_async_remote_copy(src, dst, send_sems.at[7 * a + k - 1], recv_sems.at[7 * a + k - 1],
                                        device_id=peer, device_id_type=MESH)


def _own_copy(src_refs, land_refs, send_sems, a, n, gather):
    x, y, c = _my_coords()
    me = 4 * x + 2 * y + c
    src = src_refs[a] if gather else src_refs[a].at[me]
    return pltpu.make_async_copy(src, land_refs[a].at[me], send_sems.at[7 * n + a])


def _exchange_start(srcs, gather, name, after=None):
    n = len(srcs)
    land_shapes = [(N_DEV, *s.shape) if gather else s.shape for s in srcs]
    n_in = n if after is None else n + 1

    def body(*refs):
        src_refs, send_sems, recv_sems, token = refs[:n], refs[n_in], refs[n_in + 1], refs[n_in + 2 + 2 * n]
        land_refs = refs[n_in + 2 + n:n_in + 2 + 2 * n]
        for a in range(n):
            _own_copy(src_refs, land_refs, send_sems, a, n, gather).start()
            for k in range(1, N_DEV):
                _exchange_copy(src_refs, land_refs, send_sems, recv_sems, a, k, gather, receiving=False).start()
        token[...] = jnp.zeros_like(token)

    hbm = pl.BlockSpec(memory_space=pltpu.HBM)
    sem = pl.BlockSpec(memory_space=pltpu.SEMAPHORE)
    res = pl.pallas_call(
        body, name=name,
        out_shape=(pltpu.SemaphoreType.DMA((8 * n,)), pltpu.SemaphoreType.DMA((7 * n,)),
                   *[pltpu.HBM(v.shape, v.dtype) for v in srcs],
                   *[pltpu.HBM(shape, v.dtype) for shape, v in zip(land_shapes, srcs)],
                   jax.ShapeDtypeStruct((8, 128), F32),
                   *([] if after is None else [jax.ShapeDtypeStruct(after.shape, after.dtype)])),
        in_specs=[hbm] * n + [pl.BlockSpec(memory_space=pl.ANY)] * (n_in - n),
        out_specs=(sem, sem, *([hbm] * (2 * n)), pl.BlockSpec(memory_space=pltpu.VMEM),
                   *([pl.BlockSpec(memory_space=pl.ANY)] * (n_in - n))),
        input_output_aliases={**{i: 2 + i for i in range(n)}, **({} if after is None else {n: 3 + 2 * n})},
        compiler_params=pltpu.CompilerParams(has_side_effects=pltpu.SideEffectType.DATAFLOW_SIDE_EFFECTING),
    )(*[pltpu.with_memory_space_constraint(v, pltpu.HBM) for v in srcs], *([] if after is None else [after]))
    return (res[0], res[1], list(res[2:2 + n]), list(res[2 + n:2 + 2 * n]), res[2 + 2 * n],
            None if after is None else res[3 + 2 * n])


def _exchange_wait(send_sems, recv_sems, srcs, lands, after, gather, name):
    n = len(srcs)

    def body(*refs):
        src_refs, land_refs, s_sems, r_sems = refs[:n], refs[n:2 * n], refs[2 * n], refs[2 * n + 1]
        for a in range(n):
            _own_copy(src_refs, land_refs, s_sems, a, n, gather).wait()
            for k in range(1, N_DEV):
                _exchange_copy(src_refs, land_refs, s_sems, r_sems, a, k, gather, receiving=False).wait_send()
                _exchange_copy(src_refs, land_refs, s_sems, r_sems, a, k, gather, receiving=True).wait_recv()

    hbm = pl.BlockSpec(memory_space=pltpu.HBM)
    sem = pl.BlockSpec(memory_space=pltpu.SEMAPHORE)
    res = pl.pallas_call(
        body, name=name,
        out_shape=tuple(pltpu.HBM(v.shape, v.dtype) for v in srcs + lands),
        in_specs=[hbm] * (2 * n) + [sem, sem, pl.BlockSpec(memory_space=pl.ANY)],
        out_specs=tuple([hbm] * (2 * n)),
        input_output_aliases={i: i for i in range(2 * n)},
        compiler_params=pltpu.CompilerParams(has_side_effects=pltpu.SideEffectType.DATAFLOW_SIDE_EFFECTING),
    )(*srcs, *lands, send_sems, recv_sems, after)
    return list(res[n:])


def _load_resident(step_is_first, pairs, sem):
    @pl.when(step_is_first)
    def _():
        copies = [pltpu.make_async_copy(src, dst, sem.at[i]) for i, (src, dst) in enumerate(pairs)]
        for cp in copies:
            cp.start()
        for cp in copies:
            cp.wait()


def _load_w_in_t(step_is_first, w_hbm, w_vmem, sem):
    @pl.when(step_is_first)
    def _():
        w_vmem[D_IN:, :] = jnp.zeros((D_IN_PAD - D_IN, D_MODEL), BF16)
    _load_resident(step_is_first, [(w_hbm, w_vmem.at[pl.ds(0, D_IN)])], sem)


def _inproj_fwd(x2d, sc1p, sh1, w_in_t, after):
    t = x2d.shape[0]
    tm = min(PROJ_TILE, t)

    def body(x_ref, sc_ref, sh_ref, w_hbm, after_ref, proj_ref, u_ref, w_vmem, sem):
        _load_w_in_t(pl.program_id(0) == 0, w_hbm, w_vmem, sem)
        xh, _ = _ln_stats(x_ref[...])
        ub = (xh * sc_ref[...] + sh_ref[...]).astype(BF16)
        u_ref[...] = ub
        proj_ref[...] = _dot(ub, w_vmem[...], NT)

    row = lambda i: (i, 0)
    fix = lambda i: (0, 0)
    return pl.pallas_call(
        body, name="inproj_fwd", grid=(t // tm,),
        in_specs=[pl.BlockSpec((tm, D_MODEL), row), pl.BlockSpec((1, D_MODEL), fix), pl.BlockSpec((1, D_MODEL), fix),
                  pl.BlockSpec(memory_space=pl.ANY), pl.BlockSpec(memory_space=pl.ANY)],
        out_specs=(pl.BlockSpec((tm, D_IN_PAD), row), pl.BlockSpec((tm, D_MODEL), row)),
        out_shape=(jax.ShapeDtypeStruct((t, D_IN_PAD), F32), jax.ShapeDtypeStruct((t, D_MODEL), BF16)),
        scratch_shapes=[pltpu.VMEM((D_IN_PAD, D_MODEL), BF16), pltpu.SemaphoreType.DMA((1,))],
        compiler_params=pltpu.CompilerParams(dimension_semantics=("arbitrary",), vmem_limit_bytes=V7X_VMEM_LIMIT),
    )(x2d, sc1p, sh1, w_in_t, after)


CHUNK_SHIFT = CHUNK.bit_length() - 1


def _ret_tables(t, tl):
    r = lax.broadcasted_iota(jnp.int32, (tl, tl), 0)
    c = lax.broadcasted_iota(jnp.int32, (tl, tl), 1)
    allowed = jnp.right_shift(c, CHUNK_SHIFT) <= jnp.right_shift(r, CHUNK_SHIFT)
    dist = jnp.abs(r - c).astype(F32)
    rowf = lax.broadcasted_iota(jnp.int32, (tl, RET_D), 0).astype(F32)
    lgs = [_log_gamma(h) for h in range(RET_HEADS)]
    dec = jnp.stack([jnp.where(allowed, jnp.exp(lg * dist), 0.0) for lg in lgs])
    qkd = jnp.stack([jnp.exp(lg * (rowf + 1.0)) for lg in lgs] + [jnp.exp(lg * (tl - 1.0 - rowf)) for lg in lgs])
    inv = 1.0 / (10000.0 ** jnp.linspace(0.0, 1.0, RET_D // 2, dtype=F32))
    off = jnp.arange(tl, dtype=F32)[:, None] * inv[None, :]
    start = (jnp.arange(t // tl, dtype=F32) * tl)[:, None] * inv[None, :]
    co, so = jnp.cos(off), jnp.sin(off)
    rot_in = jnp.stack([jnp.concatenate([co, co], 1), jnp.concatenate([so, so], 1),
                        jnp.concatenate([-co, co], 1), jnp.concatenate([-so, so], 1)])
    cs, ss = jnp.cos(start), jnp.sin(start)
    rot_tile = jnp.concatenate([cs, cs, ss, ss], axis=1)
    rot_tile = jnp.broadcast_to(rot_tile[:, None, :], (t // tl, 8, 2 * RET_D))
    return dec, qkd, rot_in, rot_tile


def _tile_gammas(tl):
    return [float(np.exp(np.float32(_log_gamma(h)) * np.float32(tl))) for h in range(RET_HEADS)]


def _tile_rotary(rot_in_ref, rot_tile_ref, j):
    ca, sa = rot_tile_ref[j, 0:1, 0:RET_D], rot_tile_ref[j, 0:1, RET_D:2 * RET_D]
    cosv = ca * rot_in_ref[0] - sa * rot_in_ref[1]
    sinv = sa * rot_in_ref[2] + ca * rot_in_ref[3]
    return cosv, sinv


def _gla_consts(tl):
    r = lax.broadcasted_iota(jnp.int32, (tl, tl), 0)
    c = lax.broadcasted_iota(jnp.int32, (tl, tl), 1)
    ltri = (c <= r).astype(F32)
    utri = (c >= r).astype(F32)
    lane = lax.broadcasted_iota(jnp.int32, (1, GLA_KW), 1)
    hmask = [((lane >= h * GLA_DK) & (lane < (h + 1) * GLA_DK)).astype(F32) for h in range(GLA_HEADS)]
    rs = lax.broadcasted_iota(jnp.int32, (GLA_HEADS * tl, tl), 0) & (tl - 1)
    cs = lax.broadcasted_iota(jnp.int32, (GLA_HEADS * tl, tl), 1)
    lower = cs <= rs
    same = jnp.right_shift(cs, CHUNK_SHIFT) == jnp.right_shift(rs, CHUNK_SHIFT)
    upper = jnp.logical_and(jnp.logical_not(lower), same)
    return dict(ltri=ltri, utri=utri, hmask=hmask, lower=lower, upper=upper)


def _tile_rows(j, tl):
    return pl.ds(j * tl, tl) if isinstance(j, int) else pl.ds(pl.multiple_of(j * tl, tl), tl)


def _for_tiles(cps, fn):
    for j in range(cps):
        fn(j, 0)


def _rotate(v, cosv, sinv):
    return v * cosv + pltpu.roll(v, RET_D // 2, 1) * sinv


def _rotate_t(d, cosv, sinv):
    return d * cosv + pltpu.roll(d * sinv, RET_D // 2, 1)


def _stack_heads(v, hmask):
    return jnp.concatenate([v * hmask[h] for h in range(GLA_HEADS)], axis=0)


def _gla_gates(glr, gw, gb, ltri, tl):
    z = _dot_split(glr, gw, NN) + gb
    la = (jnp.minimum(z, 0.0) - jnp.log(1.0 + jnp.exp(-jnp.abs(z)))) * (1.0 / GATE_TAU)
    b = _dot_split(ltri, la, NN, a_exact=True)
    level = b[tl // 2 - 1:tl // 2, :]
    ep = jnp.exp(jnp.clip(b - level, -80.0, 80.0))
    em = jnp.exp(jnp.clip(level - b, -80.0, 80.0))
    bl = b[tl - 1:tl, :]
    return z, b, bl, ep, em


def _mixer_fwd(proj, tables, gw_pad, gb, rnw, gnw):
    t = proj.shape[0]
    tc = min(MIX_TILE, t)
    tr, tg = min(RET_SUB, tc), min(GLA_SUB, tc)
    nsteps = t // tc
    scale_r = RET_D ** -0.5
    scale_g = GLA_DK ** -0.5
    gammas = _tile_gammas(tr)

    def body(rq_ref, rk_ref, rv_ref, rg_ref, gq_ref, gk_ref, gv_ref, gg_ref, glr_ref,
             dec_ref, qkd_ref, rot_in_ref, rot_tile_ref, gw_ref, gb_ref, rnw_ref, gnw_ref,
             mix_ref, oraw_ref, qrb_ref, krb_ref, rst_ref, sst_ref, r_scr, s_scr):
        @pl.when(pl.program_id(0) == 0)
        def _():
            r_scr[...] = jnp.zeros_like(r_scr)
            s_scr[...] = jnp.zeros_like(s_scr)

        gla_k = _gla_consts(tg)

        def ret_tile(j, carry):
            rows = _tile_rows(j, tr)
            cosv, sinv = _tile_rotary(rot_in_ref, rot_tile_ref, j)
            for h in range(RET_HEADS):
                cols = slice(h * RET_D, (h + 1) * RET_D)
                qr = _rotate(rq_ref[rows, cols], cosv, sinv) * scale_r
                kr = _rotate(rk_ref[rows, cols], cosv, sinv)
                vb = rv_ref[rows, cols].astype(BF16)
                qb, kb = qr.astype(BF16), kr.astype(BF16)
                qrb_ref[rows, cols] = qb
                krb_ref[rows, cols] = kb
                p = _dot(qb, kb, NT) * dec_ref[h]
                rp = r_scr[cols, :]
                o = _dot(p.astype(BF16), vb) + _dot((qr * qkd_ref[h]).astype(BF16), rp.astype(BF16))
                rst_ref[j, cols, :] = rp
                r_scr[cols, :] = gammas[h] * rp + _dot((kr * qkd_ref[RET_HEADS + h]).astype(BF16), vb, TN)
                oraw_ref[rows, cols] = o
                oc = o - jnp.mean(o, axis=-1, keepdims=True)
                n = oc * lax.rsqrt(jnp.mean(oc * oc, axis=-1, keepdims=True) + LN_EPS)
                g = rg_ref[rows, cols]
                mix_ref[rows, cols] = (n * rnw_ref[:, cols] * (g * _sigmoid(g))).astype(BF16)
            return carry

        def gla_tile(j, carry):
            k = gla_k
            tl = tg
            rows = _tile_rows(j, tg)
            _, b, bl, ep, em = _gla_gates(glr_ref[rows, :], gw_ref[...], gb_ref[...], k["ltri"], tl)
            qs = gq_ref[rows, :] * scale_g
            kk = gk_ref[rows, :]
            x_all = _dot(_stack_heads(qs * ep, k["hmask"]).astype(BF16), (kk * em).astype(BF16), NT)
            y_all = _dot(_stack_heads(qs * em, k["hmask"]).astype(BF16), (kk * ep).astype(BF16), NT)
            a_all = jnp.where(k["lower"], x_all, jnp.where(k["upper"], y_all, 0.0)).astype(BF16)
            st = s_scr[...]
            oq = _dot(_stack_heads(qs * jnp.exp(b), k["hmask"]).astype(BF16), st.astype(BF16), NT)
            kg = kk * jnp.exp(bl - b)
            sst_ref[j] = st
            st_new = st * jnp.exp(bl)
            for h in range(GLA_HEADS):
                cols = slice(h * GLA_DV, (h + 1) * GLA_DV)
                hr = slice(h * tl, (h + 1) * tl)
                vb = gv_ref[rows, cols].astype(BF16)
                o = _dot(a_all[hr, :], vb) + oq[hr, :]
                st_new = st_new + _dot(vb, (kg * k["hmask"][h]).astype(BF16), TN)
                ocols = slice(RET_W + h * GLA_DV, RET_W + (h + 1) * GLA_DV)
                oraw_ref[rows, ocols] = o
                n = o * lax.rsqrt(jnp.mean(o * o, axis=-1, keepdims=True) + LN_EPS)
                g = gg_ref[rows, cols]
                mix_ref[rows, ocols] = (n * gnw_ref[:, cols] * (g * _sigmoid(g))).astype(BF16)
            s_scr[...] = st_new
            return carry

        _for_tiles(tc // tr, ret_tile)
        _for_tiles(tc // tg, gla_tile)

    def col(width, off):
        return pl.BlockSpec((tc, width), lambda i, o=off // width: (i, o))

    fix = lambda i: (0, 0)
    fix3 = lambda i: (0, 0, 0)
    dec, qkd, rot_in, rot_tile = tables
    in_specs = [col(RET_W, OFF_RQ), col(RET_W, OFF_RK), col(RET_W, OFF_RV), col(RET_W, OFF_RG),
                col(GLA_KW, OFF_GQ), col(GLA_KW, OFF_GK), col(GLA_VW, OFF_GV), col(GLA_VW, OFF_GG),
                col(V7X_LANES, OFF_GLR),
                pl.BlockSpec(dec.shape, fix3), pl.BlockSpec(qkd.shape, fix3), pl.BlockSpec(rot_in.shape, fix3),
                pl.BlockSpec((tc // tr, 8, 2 * RET_D), lambda i: (i, 0, 0)),
                pl.BlockSpec((V7X_LANES, GLA_KW), fix), pl.BlockSpec((1, GLA_KW), fix),
                pl.BlockSpec((1, RET_W), fix), pl.BlockSpec((1, GLA_VW), fix)]
    half = pl.BlockSpec((tc, RET_W), lambda i: (i, 0))
    out_specs = (pl.BlockSpec((tc, D_MODEL), lambda i: (i, 0)), pl.BlockSpec((tc, D_MODEL), lambda i: (i, 0)),
                 half, half,
                 pl.BlockSpec((tc // tr, RET_W, RET_D), lambda i: (i, 0, 0)),
                 pl.BlockSpec((tc // tg, GLA_DV, GLA_KW), lambda i: (i, 0, 0)))
    out_shape = (jax.ShapeDtypeStruct((t, D_MODEL), BF16), jax.ShapeDtypeStruct((t, D_MODEL), F32),
                 jax.ShapeDtypeStruct((t, RET_W), BF16), jax.ShapeDtypeStruct((t, RET_W), BF16),
                 jax.ShapeDtypeStruct((t // tr, RET_W, RET_D), F32),
                 jax.ShapeDtypeStruct((t // tg, GLA_DV, GLA_KW), F32))
    return pl.pallas_call(
        body, name="mixer_fwd", grid=(nsteps,), in_specs=in_specs, out_specs=out_specs, out_shape=out_shape,
        scratch_shapes=[pltpu.VMEM((RET_W, RET_D), F32), pltpu.VMEM((GLA_DV, GLA_KW), F32)],
        compiler_params=pltpu.CompilerParams(dimension_semantics=("arbitrary",), vmem_limit_bytes=V7X_VMEM_LIMIT),
    )(*([proj] * 9), dec, qkd, rot_in, rot_tile, gw_pad, gb, rnw, gnw)


def _mid_fwd(mixed, x2d, target, vecs, w_out_b, w1_b, w2_b):
    t = x2d.shape[0]
    tm = min(ROW_TILE, t)

    def body(mix_ref, x_ref, tgt_ref, v_ref, wo_hbm, w1_hbm, w2_hbm,
             m_ref, x1n_ref, rstd_ref, u2_ref, a_ref, df_ref, dh2_ref, acc_ref, wo, w1, w2, sem):
        first = pl.program_id(0) == 0
        _load_resident(first, [(wo_hbm, wo), (w1_hbm, w1), (w2_hbm, w2)], sem)

        @pl.when(first)
        def _():
            acc_ref[...] = jnp.zeros_like(acc_ref)

        gate1, sc2p, sh2, gate2 = v_ref[0:1, :], v_ref[1:2, :], v_ref[2:3, :], v_ref[3:4, :]
        l1w, l1b, l2w, l2b = v_ref[4:5, :], v_ref[5:6, :], v_ref[6:7, :], v_ref[7:8, :]
        m = _dot(mix_ref[...], wo[...])
        m_ref[...] = m.astype(BF16)
        x1n, rstd1 = _ln_stats(ALPHA * x_ref[...] + gate1 * m)
        x1n_ref[...] = x1n
        rstd_ref[...] = rstd1
        x1 = x1n * l1w + l1b
        xh1, _ = _ln_stats(x1)
        u2 = (xh1 * sc2p + sh2).astype(BF16)
        u2_ref[...] = u2
        f = jnp.zeros((tm, D_MODEL), F32)
        for j in range(N_DEV):
            cols = slice(j * FF_COLS, (j + 1) * FF_COLS)
            a = _dot(u2, w1[j])
            a_ref[:, cols] = a.astype(BF16)
            r = jnp.maximum(a, 0.0)
            f = f + _dot((r * r).astype(BF16), w2[cols, :])
        yh, rstd2 = _ln_stats(ALPHA * x1 + gate2 * f)
        e = yh * l2w + l2b - tgt_ref[...]
        dy = e * (1.0 / D_MODEL)
        dh2 = _ln_bwd(dy * l2w, yh, rstd2)
        dh2_ref[...] = dh2
        df_ref[...] = (dh2 * gate2).astype(BF16)
        acc_ref[0:1, :] += jnp.sum(dy * yh, axis=0, keepdims=True)
        acc_ref[1:2, :] += jnp.sum(dy, axis=0, keepdims=True)
        acc_ref[2:3, :] += jnp.sum(dh2 * f, axis=0, keepdims=True)
        acc_ref[3:4, :] += jnp.sum(e * e, axis=0, keepdims=True) * (0.5 / D_MODEL)

    row = lambda i: (i, 0)
    fix = lambda i: (0, 0)
    hbm = pl.BlockSpec(memory_space=pl.ANY)
    return pl.pallas_call(
        body, name="mid_fwd", grid=(t // tm,),
        in_specs=[pl.BlockSpec((tm, D_MODEL), row), pl.BlockSpec((tm, D_MODEL), row), pl.BlockSpec((tm, D_MODEL), row),
                  pl.BlockSpec((8, D_MODEL), fix), hbm, hbm, hbm],
        out_specs=(pl.BlockSpec((tm, D_MODEL), row), pl.BlockSpec((tm, D_MODEL), row), pl.BlockSpec((tm, 1), row),
                   pl.BlockSpec((tm, D_MODEL), row), pl.BlockSpec((tm, D_FF), row), pl.BlockSpec((tm, D_MODEL), row),
                   pl.BlockSpec((tm, D_MODEL), row), pl.BlockSpec((8, D_MODEL), fix)),
        out_shape=(jax.ShapeDtypeStruct((t, D_MODEL), BF16), jax.ShapeDtypeStruct((t, D_MODEL), F32),
                   jax.ShapeDtypeStruct((t, 1), F32), jax.ShapeDtypeStruct((t, D_MODEL), BF16),
                   jax.ShapeDtypeStruct((t, D_FF), BF16), jax.ShapeDtypeStruct((t, D_MODEL), BF16),
                   jax.ShapeDtypeStruct((t, D_MODEL), F32), jax.ShapeDtypeStruct((8, D_MODEL), F32)),
        scratch_shapes=[pltpu.VMEM((D_MODEL, D_MODEL), BF16), pltpu.VMEM((N_DEV, D_MODEL, FF_COLS), BF16),
                        pltpu.VMEM((D_FF, D_MODEL), BF16), pltpu.SemaphoreType.DMA((3,))],
        compiler_params=pltpu.CompilerParams(dimension_semantics=("arbitrary",), vmem_limit_bytes=V7X_VMEM_LIMIT),
    )(mixed, x2d, target, vecs, w_out_b, w1_b, w2_b)


def _ffn_bwd(df, a, dh2, x1n, rstd1, m, vecs, w_out_b, w1_b, w2_b):
    t = x1n.shape[0]
    tm = min(ROW_TILE, t)

    def body(df_ref, a_ref, dh2_ref, x1n_ref, rstd_ref, m_ref, v_ref, wo_hbm, w1_hbm, w2_hbm,
             da_ref, dm_ref, dmix_ref, dxa_ref, acc_ref, wo, w1, w2, sem):
        first = pl.program_id(0) == 0
        _load_resident(first, [(wo_hbm, wo), (w1_hbm, w1), (w2_hbm, w2)], sem)

        @pl.when(first)
        def _():
            acc_ref[...] = jnp.zeros_like(acc_ref)

        gate1, sc2p, l1w, l1b = v_ref[0:1, :], v_ref[1:2, :], v_ref[2:3, :], v_ref[3:4, :]
        df = df_ref[...]
        du2 = jnp.zeros((tm, D_MODEL), F32)
        for j in range(N_DEV):
            cols = slice(j * FF_COLS, (j + 1) * FF_COLS)
            dr2 = _dot(df, w2[cols, :], NT)
            da = (dr2 * (2.0 * jnp.maximum(a_ref[:, cols].astype(F32), 0.0))).astype(BF16)
            da_ref[:, cols] = da
            du2 = du2 + _dot(da, w1[j], NT)
        x1n = x1n_ref[...]
        xh1, rstd0 = _ln_stats(x1n * l1w + l1b)
        dx1 = ALPHA * dh2_ref[...] + _ln_bwd(du2 * sc2p, xh1, rstd0)
        dh1 = _ln_bwd(dx1 * l1w, x1n, rstd_ref[...])
        dxa_ref[...] = ALPHA * dh1
        dm = (dh1 * gate1).astype(BF16)
        dm_ref[...] = dm
        dmix_ref[...] = _dot(dm, wo[...], NT)
        acc_ref[0:1, :] += jnp.sum(du2 * xh1, axis=0, keepdims=True)
        acc_ref[1:2, :] += jnp.sum(du2, axis=0, keepdims=True)
        acc_ref[2:3, :] += jnp.sum(dx1 * x1n, axis=0, keepdims=True)
        acc_ref[3:4, :] += jnp.sum(dx1, axis=0, keepdims=True)
        acc_ref[4:5, :] += jnp.sum(dh1 * m_ref[...].astype(F32), axis=0, keepdims=True)

    row = lambda i: (i, 0)
    fix = lambda i: (0, 0)
    hbm = pl.BlockSpec(memory_space=pl.ANY)
    return pl.pallas_call(
        body, name="ffn_bwd", grid=(t // tm,),
        in_specs=[pl.BlockSpec((tm, D_MODEL), row), pl.BlockSpec((tm, D_FF), row), pl.BlockSpec((tm, D_MODEL), row),
                  pl.BlockSpec((tm, D_MODEL), row), pl.BlockSpec((tm, 1), row), pl.BlockSpec((tm, D_MODEL), row),
                  pl.BlockSpec((8, D_MODEL), fix), hbm, hbm, hbm],
        out_specs=(pl.BlockSpec((tm, D_FF), row), pl.BlockSpec((tm, D_MODEL), row), pl.BlockSpec((tm, D_MODEL), row),
                   pl.BlockSpec((tm, D_MODEL), row), pl.BlockSpec((8, D_MODEL), fix)),
        out_shape=(jax.ShapeDtypeStruct((t, D_FF), BF16), jax.ShapeDtypeStruct((t, D_MODEL), BF16),
                   jax.ShapeDtypeStruct((t, D_MODEL), F32), jax.ShapeDtypeStruct((t, D_MODEL), F32),
                   jax.ShapeDtypeStruct((8, D_MODEL), F32)),
        scratch_shapes=[pltpu.VMEM((D_MODEL, D_MODEL), BF16), pltpu.VMEM((N_DEV, D_MODEL, FF_COLS), BF16),
                        pltpu.VMEM((D_FF, D_MODEL), BF16), pltpu.SemaphoreType.DMA((3,))],
        compiler_params=pltpu.CompilerParams(dimension_semantics=("arbitrary",), vmem_limit_bytes=V7X_VMEM_LIMIT),
    )(df, a, dh2, x1n, rstd1, m, vecs, w_out_b, w1_b, w2_b)


def _matmul_tn(lhs, rhs, tmm, tn, tk, name, relu_sq=False, col_slab=None, out_rows=None):
    t, mm = lhs.shape
    assert out_rows is None or (col_slab is None and tmm == mm)
    nn = rhs.shape[1]
    tk = min(tk, t)
    nk = t // tk

    def body(l_ref, r_ref, o_ref, acc):
        kk = pl.program_id(2)

        @pl.when(kk == 0)
        def _():
            acc[...] = jnp.zeros_like(acc)

        l = l_ref[...]
        if relu_sq:
            lf = jnp.maximum(l.astype(F32), 0.0)
            l = (lf * lf).astype(BF16)
        acc[...] += _dot(l, r_ref[...], TN)

        @pl.when(kk == nk - 1)
        def _():
            if out_rows is not None:
                for s in range(N_DEV):
                    o_ref[s] = acc[s * out_rows:(s + 1) * out_rows, :].astype(o_ref.dtype)
            elif col_slab is None:
                o_ref[...] = acc[...].astype(o_ref.dtype)
            else:
                for s in range(tn // col_slab):
                    o_ref[s] = acc[:, s * col_slab:(s + 1) * col_slab].astype(o_ref.dtype)

    if out_rows is not None:
        out_spec = pl.BlockSpec((N_DEV, out_rows, tn), lambda i, j, k: (0, 0, j))
        out_shape = jax.ShapeDtypeStruct((N_DEV, out_rows, nn), BF16)
    elif col_slab is None:
        out_spec = pl.BlockSpec((tmm, tn), lambda i, j, k: (i, j))
        out_shape = jax.ShapeDtypeStruct((mm, nn), BF16)
    else:
        out_spec = pl.BlockSpec((tn // col_slab, tmm, col_slab), lambda i, j, k: (j, i, 0))
        out_shape = jax.ShapeDtypeStruct((nn // col_slab, mm, col_slab), BF16)
    return pl.pallas_call(
        body, name=name, grid=(mm // tmm, nn // tn, nk),
        in_specs=[pl.BlockSpec((tk, tmm), lambda i, j, k: (k, i)), pl.BlockSpec((tk, tn), lambda i, j, k: (k, j))],
        out_specs=out_spec,
        out_shape=out_shape,
        scratch_shapes=[pltpu.VMEM((tmm, tn), F32)],
        compiler_params=pltpu.CompilerParams(dimension_semantics=("arbitrary", "arbitrary", "arbitrary"),
                                             vmem_limit_bytes=V7X_VMEM_LIMIT),
    )(lhs, rhs)


def _mixer_bwd(dmix, proj, qrb, krb, oraw, tables, rst, sst, gw_pad, gb, rnw, gnw, after):
    t = proj.shape[0]
    tc = min(MIX_TILE, t)
    tr, tg = min(RET_SUB, tc), min(GLA_SUB, tc)
    nsteps = t // tc
    scale_r = RET_D ** -0.5
    scale_g = GLA_DK ** -0.5
    gammas = _tile_gammas(tr)

    def body(dmix_ref, qrb_ref, krb_ref, rv_ref, rg_ref, gq_ref, gk_ref, gv_ref, gg_ref, glr_ref, oraw_ref,
             dec_ref, qkd_ref, rot_in_ref, rot_tile_ref, rst_ref, sst_ref, gw_ref, gb_ref, rnw_ref, gnw_ref, after_ref,
             dproj_ref, dgw_ref, dvec_ref, dr_scr, ds_scr):
        @pl.when(pl.program_id(0) == 0)
        def _():
            dr_scr[...] = jnp.zeros_like(dr_scr)
            ds_scr[...] = jnp.zeros_like(ds_scr)
            dgw_ref[...] = jnp.zeros_like(dgw_ref)
            dvec_ref[...] = jnp.zeros_like(dvec_ref)

        gla_k = _gla_consts(tg)
        last_row = lax.broadcasted_iota(jnp.int32, (tg, GLA_KW), 0) == tg - 1

        def ret_tile(jj, carry):
            j = tc // tr - 1 - jj
            rows = _tile_rows(j, tr)
            cosv, sinv = _tile_rotary(rot_in_ref, rot_tile_ref, j)
            for h in range(RET_HEADS):
                cols = slice(h * RET_D, (h + 1) * RET_D)
                o = oraw_ref[rows, cols]
                g = rg_ref[rows, cols]
                w = rnw_ref[:, cols]
                dout = dmix_ref[rows, cols]
                oc = o - jnp.mean(o, axis=-1, keepdims=True)
                inv = lax.rsqrt(jnp.mean(oc * oc, axis=-1, keepdims=True) + LN_EPS)
                n = oc * inv
                sg = _sigmoid(g)
                sil = g * sg
                dn = dout * w * sil
                dvec_ref[0:1, cols] += jnp.sum(dout * n * sil, axis=0, keepdims=True)
                dproj_ref[rows, OFF_RG + h * RET_D:OFF_RG + (h + 1) * RET_D] = (
                    dout * n * w * (sg * (1.0 + g * (1.0 - sg)))).astype(BF16)
                doc = inv * (dn - n * jnp.mean(dn * n, axis=-1, keepdims=True))
                do = doc - jnp.mean(doc, axis=-1, keepdims=True)

                qb, kb = qrb_ref[rows, cols], krb_ref[rows, cols]
                qr, kr = qb.astype(F32), kb.astype(F32)
                vb = rv_ref[rows, cols].astype(BF16)
                dob = do.astype(BF16)
                qd, kd = qkd_ref[h], qkd_ref[RET_HEADS + h]
                p = _dot(qb, kb, NT) * dec_ref[h]
                rp = rst_ref[j, cols, :].astype(BF16)
                dr = dr_scr[cols, :]
                drb = dr.astype(BF16)
                dpb = (_dot(dob, vb, NT) * dec_ref[h]).astype(BF16)
                dqr = _dot(dpb, kb) + _dot(dob, rp, NT) * qd
                dkr = _dot(dpb, qb, TN) + _dot(vb, drb, NT) * kd
                dv = _dot(p.astype(BF16), dob, TN) + _dot((kr * kd).astype(BF16), drb)
                dr_scr[cols, :] = gammas[h] * dr + _dot((qr * qd).astype(BF16), dob, TN)
                dproj_ref[rows, OFF_RQ + h * RET_D:OFF_RQ + (h + 1) * RET_D] = (
                    _rotate_t(dqr, cosv, sinv) * scale_r).astype(BF16)
                dproj_ref[rows, OFF_RK + h * RET_D:OFF_RK + (h + 1) * RET_D] = _rotate_t(dkr, cosv, sinv).astype(BF16)
                dproj_ref[rows, OFF_RV + h * RET_D:OFF_RV + (h + 1) * RET_D] = dv.astype(BF16)
            return carry

        def gla_tile(jj, carry):
            k = gla_k
            tl = tg
            j = tc // tg - 1 - jj
            rows = _tile_rows(j, tg)
            glr = glr_ref[rows, :]
            z, b, bl, ep, em = _gla_gates(glr, gw_ref[...], gb_ref[...], k["ltri"], tl)
            qs = gq_ref[rows, :] * scale_g
            kk = gk_ref[rows, :]
            eb = jnp.exp(b)
            ekb = jnp.exp(bl - b)
            ebl = jnp.exp(bl)
            ql, qu, kl, ku = qs * ep, qs * em, kk * em, kk * ep
            qg, kg = qs * eb, kk * ekb
            qlm = _stack_heads(ql, k["hmask"]).astype(BF16)
            qum = _stack_heads(qu, k["hmask"]).astype(BF16)
            klb, kub = kl.astype(BF16), ku.astype(BF16)
            a_all = jnp.where(k["lower"], _dot(qlm, klb, NT),
                              jnp.where(k["upper"], _dot(qum, kub, NT), 0.0)).astype(BF16)
            st = sst_ref[j]
            stb = st.astype(BF16)
            ds = ds_scr[...]
            dsb = ds.astype(BF16)
            ds_new = ds * ebl
            da_parts = []
            dqg = jnp.zeros((tl, GLA_KW), F32)
            dkg = jnp.zeros((tl, GLA_KW), F32)
            for h in range(GLA_HEADS):
                cols = slice(h * GLA_DV, (h + 1) * GLA_DV)
                hr = slice(h * tl, (h + 1) * tl)
                ocols = slice(RET_W + h * GLA_DV, RET_W + (h + 1) * GLA_DV)
                o = oraw_ref[rows, ocols]
                g = gg_ref[rows, cols]
                w = gnw_ref[:, cols]
                dout = dmix_ref[rows, ocols]
                inv = lax.rsqrt(jnp.mean(o * o, axis=-1, keepdims=True) + LN_EPS)
                n = o * inv
                sg = _sigmoid(g)
                sil = g * sg
                dn = dout * w * sil
                dvec_ref[1:2, cols] += jnp.sum(dout * n * sil, axis=0, keepdims=True)
                dproj_ref[rows, OFF_GG + h * GLA_DV:OFF_GG + (h + 1) * GLA_DV] = (
                    dout * n * w * (sg * (1.0 + g * (1.0 - sg)))).astype(BF16)
                dob = (inv * (dn - n * jnp.mean(dn * n, axis=-1, keepdims=True))).astype(BF16)
                vb = gv_ref[rows, cols].astype(BF16)
                mh = k["hmask"][h]
                da_parts.append(_dot(dob, vb, NT))
                dv = _dot(a_all[hr, :], dob, TN) + _dot((kg * mh).astype(BF16), dsb, NT)
                dproj_ref[rows, OFF_GV + h * GLA_DV:OFF_GV + (h + 1) * GLA_DV] = dv.astype(BF16)
                dkg = dkg + mh * _dot(vb, dsb)
                dqg = dqg + mh * _dot(dob, stb)
                ds_new = ds_new + _dot(dob, (qg * mh).astype(BF16), TN)
            da_all = jnp.concatenate(da_parts, axis=0)
            dal = jnp.where(k["lower"], da_all, 0.0).astype(BF16)
            dau = jnp.where(k["upper"], da_all, 0.0).astype(BF16)
            dqlm = _dot(dal, klb)
            dqum = _dot(dau, kub)
            dql = jnp.zeros((tl, GLA_KW), F32)
            dqu = jnp.zeros((tl, GLA_KW), F32)
            for h in range(GLA_HEADS):
                hr = slice(h * tl, (h + 1) * tl)
                dql = dql + k["hmask"][h] * dqlm[hr, :]
                dqu = dqu + k["hmask"][h] * dqum[hr, :]
            dkl = _dot(dal, qlm, TN)
            dku = _dot(dau, qum, TN)
            dbl = (jnp.sum(dkg * kg, axis=0, keepdims=True)
                   + jnp.sum(ds * st, axis=0, keepdims=True) * ebl)
            ds_scr[...] = ds_new
            dqs = dql * ep + dqu * em + dqg * eb
            dk = dkl * em + dku * ep + dkg * ekb
            db = dql * ql - dkl * kl - dqu * qu + dku * ku + dqg * qg - dkg * kg
            db = db + jnp.where(last_row, dbl, 0.0)
            dla = _dot_split(k["utri"], db, NN, a_exact=True)
            dz = dla * (1.0 / GATE_TAU) * _sigmoid(-z)
            dvec_ref[2:3, 0:GLA_KW] += jnp.sum(dz, axis=0, keepdims=True)
            dgw_ref[...] += _dot_split(glr, dz, TN)
            dproj_ref[rows, OFF_GLR:D_IN_PAD] = _dot(dz.astype(BF16), gw_ref[...].astype(BF16), NT).astype(BF16)
            dproj_ref[rows, OFF_GQ:OFF_GQ + GLA_KW] = (dqs * scale_g).astype(BF16)
            dproj_ref[rows, OFF_GK:OFF_GK + GLA_KW] = dk.astype(BF16)
            return carry

        _for_tiles(tc // tr, ret_tile)
        _for_tiles(tc // tg, gla_tile)

    rev = lambda i: (nsteps - 1 - i, 0)

    def col(width, off):
        return pl.BlockSpec((tc, width), lambda i, o=off // width: (nsteps - 1 - i, o))

    fix = lambda i: (0, 0)
    fix3 = lambda i: (0, 0, 0)
    dec, qkd, rot_in, rot_tile = tables
    half = pl.BlockSpec((tc, RET_W), rev)
    in_specs = [pl.BlockSpec((tc, D_MODEL), rev), half, half, col(RET_W, OFF_RV), col(RET_W, OFF_RG),
                col(GLA_KW, OFF_GQ), col(GLA_KW, OFF_GK), col(GLA_VW, OFF_GV), col(GLA_VW, OFF_GG),
                col(V7X_LANES, OFF_GLR),
                pl.BlockSpec((tc, D_MODEL), rev),
                pl.BlockSpec(dec.shape, fix3), pl.BlockSpec(qkd.shape, fix3), pl.BlockSpec(rot_in.shape, fix3),
                pl.BlockSpec((tc // tr, 8, 2 * RET_D), lambda i: (nsteps - 1 - i, 0, 0)),
                pl.BlockSpec((tc // tr, RET_W, RET_D), lambda i: (nsteps - 1 - i, 0, 0)),
                pl.BlockSpec((tc // tg, GLA_DV, GLA_KW), lambda i: (nsteps - 1 - i, 0, 0)),
                pl.BlockSpec((V7X_LANES, GLA_KW), fix), pl.BlockSpec((1, GLA_KW), fix),
                pl.BlockSpec((1, RET_W), fix), pl.BlockSpec((1, GLA_VW), fix), pl.BlockSpec(memory_space=pl.ANY)]
    out_specs = (pl.BlockSpec((tc, D_IN_PAD), rev), pl.BlockSpec((V7X_LANES, GLA_KW), fix),
                 pl.BlockSpec((8, RET_W), fix))
    out_shape = (jax.ShapeDtypeStruct((t, D_IN_PAD), BF16), jax.ShapeDtypeStruct((V7X_LANES, GLA_KW), F32),
                 jax.ShapeDtypeStruct((8, RET_W), F32))
    return pl.pallas_call(
        body, name="mixer_bwd", grid=(nsteps,), in_specs=in_specs, out_specs=out_specs, out_shape=out_shape,
        scratch_shapes=[pltpu.VMEM((RET_W, RET_D), F32), pltpu.VMEM((GLA_DV, GLA_KW), F32)],
        compiler_params=pltpu.CompilerParams(dimension_semantics=("arbitrary",), vmem_limit_bytes=V7X_VMEM_LIMIT),
    )(dmix, qrb, krb, *([proj] * 7), oraw, dec, qkd, rot_in, rot_tile, rst, sst, gw_pad, gb, rnw, gnw, after)


def _inproj_bwd(dproj, x2d, dxa, sc1p, w_in_t, after):
    t = x2d.shape[0]
    tm = min(2 * PROJ_TILE, t)

    def body(dp_ref, x_ref, dxa_ref, sc_ref, w_hbm, after_ref, gx_ref, acc_ref, w_vmem, sem):
        first = pl.program_id(0) == 0
        _load_w_in_t(first, w_hbm, w_vmem, sem)

        @pl.when(first)
        def _():
            acc_ref[...] = jnp.zeros_like(acc_ref)

        du = _dot(dp_ref[...], w_vmem[...])
        xh, rstd = _ln_stats(x_ref[...])
        gx_ref[...] = dxa_ref[...] + _ln_bwd(du * sc_ref[...], xh, rstd)
        acc_ref[0:1, :] += jnp.sum(du * xh, axis=0, keepdims=True)
        acc_ref[1:2, :] += jnp.sum(du, axis=0, keepdims=True)

    row = lambda i: (i, 0)
    fix = lambda i: (0, 0)
    return pl.pallas_call(
        body, name="inproj_bwd", grid=(t // tm,),
        in_specs=[pl.BlockSpec((tm, D_IN_PAD), row), pl.BlockSpec((tm, D_MODEL), row), pl.BlockSpec((tm, D_MODEL), row),
                  pl.BlockSpec((1, D_MODEL), fix), pl.BlockSpec(memory_space=pl.ANY), pl.BlockSpec(memory_space=pl.ANY)],
        out_specs=(pl.BlockSpec((tm, D_MODEL), row), pl.BlockSpec((8, D_MODEL), fix)),
        out_shape=(jax.ShapeDtypeStruct((t, D_MODEL), F32), jax.ShapeDtypeStruct((8, D_MODEL), F32)),
        scratch_shapes=[pltpu.VMEM((D_IN_PAD, D_MODEL), BF16), pltpu.SemaphoreType.DMA((1,))],
        compiler_params=pltpu.CompilerParams(dimension_semantics=("arbitrary",), vmem_limit_bytes=V7X_VMEM_LIMIT),
    )(dproj, x2d, dxa, sc1p, w_in_t, after)


def _adam_math(w, g, m, v):
    m = ADAM_B1 * m + (1.0 - ADAM_B1) * g
    v = ADAM_B2 * v + (1.0 - ADAM_B2) * (g * g)
    m_hat = m / (1.0 - ADAM_B1 ** ADAM_STEP)
    v_hat = v / (1.0 - ADAM_B2 ** ADAM_STEP)
    delta = -ADAM_LR * (m_hat / (jnp.sqrt(v_hat) + ADAM_EPS) + ADAM_WD * w)
    return delta, m, v


def _adamw(w, gparts, m, v, name, row_tiles=False):
    nparts, rows, cols = gparts.shape
    tr = rows
    for cand in (512, 256, 128, 64, 32, 16, 8):
        if rows % cand == 0:
            tr = cand
            break

    def body(w_ref, g_ref, m_ref, v_ref, go_ref, d_ref, mo_ref, vo_ref):
        g = g_ref[0].astype(F32)
        for p in range(1, nparts):
            g = g + g_ref[p].astype(F32)
        if row_tiles:
            g = g.reshape(tr, cols // V7X_LANES, V7X_LANES).reshape(tr * cols // V7X_LANES, V7X_LANES)
        delta, mn, vn = _adam_math(w_ref[...], g, m_ref[...], v_ref[...])
        go_ref[...] = g
        d_ref[...] = delta
        mo_ref[...] = mn
        vo_ref[...] = vn

    blk = pl.BlockSpec((tr, cols), lambda i: (i, 0))
    shp = jax.ShapeDtypeStruct((rows, cols), F32)
    if row_tiles:
        blk = pl.BlockSpec((tr * cols // V7X_LANES, V7X_LANES), lambda i: (i, 0))
        shp = jax.ShapeDtypeStruct((rows * cols // V7X_LANES, V7X_LANES), F32)
    return pl.pallas_call(
        body, name=name, grid=(rows // tr,),
        in_specs=[blk, pl.BlockSpec((nparts, tr, cols), lambda i: (0, i, 0)), blk, blk],
        out_specs=(blk, blk, blk, blk), out_shape=(shp, shp, shp, shp),
        compiler_params=pltpu.CompilerParams(dimension_semantics=("arbitrary",), vmem_limit_bytes=V7X_VMEM_LIMIT),
    )(w, gparts, m, v)


def _small_reduce(gathered, gathered_gw, c_all, dmod_cols):
    def body(g_ref, gw_ref, c_ref, dm_ref, sum_ref, gwsum_ref, gb_ref, gwa_ref):
        s = g_ref[0]
        sw = gw_ref[0]
        for p in range(1, N_DEV):
            s = s + g_ref[p]
            sw = sw + gw_ref[p]
        sum_ref[...] = s
        gwsum_ref[...] = sw
        for i in range(6):
            gb_ref[:, i * D_MODEL:(i + 1) * D_MODEL] = s[i:i + 1, :]
        cc = c_ref[...]
        gwa_ref[...] = _dot(cc * _sigmoid(cc), dm_ref[...], TN, HIGHEST)

    vm = pl.BlockSpec(memory_space=pltpu.VMEM)
    return pl.pallas_call(
        body, name="small_reduce",
        out_shape=(jax.ShapeDtypeStruct(gathered.shape[1:], F32), jax.ShapeDtypeStruct(gathered_gw.shape[1:], F32),
                   jax.ShapeDtypeStruct((1, 6 * D_MODEL), F32), jax.ShapeDtypeStruct((D_MODEL, ADA_COLS), F32)),
        in_specs=[vm] * 4, out_specs=(vm, vm, vm, vm),
        compiler_params=pltpu.CompilerParams(vmem_limit_bytes=V7X_VMEM_LIMIT),
    )(gathered, gathered_gw, c_all, dmod_cols)


SMR_LN1W, SMR_LN1B, SMR_LN2W, SMR_LN2B, SMR_NORMS, SMR_MISC = 6, 7, 8, 9, 10, 11


def _adamw_small(gsum, g_b_ada, g_ggw, params, moms, vels):
    n = len(params)

    def body(*refs):
        gsum_ref, gb_ref, gw_ref = refs[:3]
        w_refs, m_refs, v_refs = refs[3:3 + n], refs[3 + n:3 + 2 * n], refs[3 + 2 * n:3 + 3 * n]
        outs = refs[3 + 3 * n:]
        g_refs, d_refs, mo_refs, vo_refs = outs[:n - 1], outs[n - 1:2 * n - 1], outs[2 * n - 1:3 * n - 1], outs[3 * n - 1:]
        grads = [gb_ref[...],
                 gsum_ref[SMR_NORMS:SMR_NORMS + 1, 0:RET_W],
                 gsum_ref[SMR_MISC:SMR_MISC + 1, 0:GLA_KW],
                 gsum_ref[SMR_NORMS:SMR_NORMS + 1, RET_W:RET_W + GLA_VW],
                 gsum_ref[SMR_LN1W:SMR_LN1W + 1, :], gsum_ref[SMR_LN1B:SMR_LN1B + 1, :],
                 gsum_ref[SMR_LN2W:SMR_LN2W + 1, :], gsum_ref[SMR_LN2B:SMR_LN2B + 1, :],
                 gw_ref[...]]
        for i in range(n):
            delta, mn, vn = _adam_math(w_refs[i][...], grads[i], m_refs[i][...], v_refs[i][...])
            if i < n - 1:
                g_refs[i][...] = grads[i]
            d_refs[i][...] = delta
            mo_refs[i][...] = mn
            vo_refs[i][...] = vn

    vm = pl.BlockSpec(memory_space=pltpu.VMEM)
    shapes = [jax.ShapeDtypeStruct(p.shape, F32) for p in params]
    n_in = 3 + 3 * n
    out_shape = tuple(shapes[:n - 1] + shapes * 3)
    return pl.pallas_call(
        body, name="adamw_small", out_shape=out_shape,
        in_specs=[vm] * n_in, out_specs=tuple([vm] * len(out_shape)),
        compiler_params=pltpu.CompilerParams(vmem_limit_bytes=V7X_VMEM_LIMIT),
    )(gsum, g_b_ada, g_ggw, *params, *moms, *vels)


def kernel(x, c, w_ada, b_ada, w_in, ret_norm_w, gla_gate_w, gla_gate_b, gla_norm_w, w_out, ln1_w, ln1_b, w_ff1, w_ff2, ln2_w, ln2_b, loss_target, m_w_ada, m_b_ada, m_w_in, m_ret_norm_w, m_gla_gate_w, m_gla_gate_b, m_gla_norm_w, m_w_out, m_ln1_w, m_ln1_b, m_w_ff1, m_w_ff2, m_ln2_w, m_ln2_b, v_w_ada, v_b_ada, v_w_in, v_ret_norm_w, v_gla_gate_w, v_gla_gate_b, v_gla_norm_w, v_w_out, v_ln1_w, v_ln1_b, v_w_ff1, v_w_ff2, v_ln2_w, v_ln2_b):
    t = x.shape[1]
    xi, yi, ci = _my_coords()
    me = 4 * xi + 2 * yi + ci
    x2d = x[0]
    tgt = loss_target[0]

    c_ext = jnp.concatenate([c, gla_gate_w[0].reshape(1, GATE_RANK * GLA_KW // N_DEV)], axis=1)
    b_l = lax.dynamic_slice(b_ada, (0, me * ADA_COLS), (1, ADA_COLS))
    c_all3, mod_all, wi_g = _adaln_mod(c_ext, w_ada[0], b_l, w_in[0].T.astype(BF16))

    wg = _exchange_start([w_out[0].astype(BF16), w_ff1[0].astype(BF16), w_ff2[0].astype(BF16)],
                         True, "wgather_start", after=wi_g)

    c_all = c_all3[:, 0, :D_MODEL]
    gate_w = c_all3[:, 0, D_MODEL:].reshape(N_DEV, GATE_RANK, GLA_KW // N_DEV)
    gate_w = gate_w.transpose(1, 0, 2).reshape(GATE_RANK, GLA_KW)
    gw_pad = jnp.zeros((V7X_LANES, GLA_KW), F32).at[:GATE_RANK].set(gate_w)
    mod = lax.dynamic_slice(mod_all, (0, me, 0), (N_DEV, 1, ADA_COLS)).reshape(6, D_MODEL)
    shift1, scale1, gate1, shift2, scale2, gate2 = [mod[i:i + 1] for i in range(6)]

    w_in_t = wg[5].reshape(D_IN, D_MODEL)

    tables = _ret_tables(t, min(RET_SUB, t))

    sc1p = 1.0 + scale1
    proj, u = _inproj_fwd(x2d, sc1p, shift1, w_in_t, after=wg[4])
    mixed, oraw, qrb, krb, rst, sst = _mixer_fwd(proj, tables, gw_pad, gla_gate_b, ret_norm_w, gla_norm_w)
    wo_g, w1_b, w2_g = _exchange_wait(*wg[:4], mixed, True, "wgather_wait")
    w_out_b = wo_g.reshape(D_MODEL, D_MODEL)
    w2_b = w2_g.reshape(D_FF, D_MODEL)
    vec_f = jnp.concatenate([gate1, 1.0 + scale2, shift2, gate2, ln1_w, ln1_b, ln2_w, ln2_b], axis=0)
    m, x1n, rstd1, u2, a, df, dh2, acc_f = _mid_fwd(mixed, x2d, tgt, vec_f, w_out_b, w1_b, w2_b)

    vec_b = jnp.concatenate([gate1, 1.0 + scale2, ln1_w, ln1_b, jnp.zeros((4, D_MODEL), F32)], axis=0)
    da, dm, dmix, dxa, acc_b = _ffn_bwd(df, a, dh2, x1n, rstd1, m, vec_b, w_out_b, w1_b, w2_b)
    dw2 = _matmul_tn(a, df, 2048, 1024, 2048, "tn_dw2", relu_sq=True)
    dw1 = _matmul_tn(u2, da, 1024, 2048, 2048, "tn_dw1", col_slab=FF_COLS)
    dwo = _matmul_tn(mixed, dm, 1024, 1024, 2048, "tn_dwout")
    gx = _exchange_start([dwo.reshape(N_DEV, OUT_ROWS, D_MODEL), dw1, dw2.reshape(N_DEV, FF_COLS, D_MODEL)], False,
                         "gradx_start")
    dproj, dgw, dvec = _mixer_bwd(dmix, proj, qrb, krb, oraw, tables, rst, sst, gw_pad,
                                  gla_gate_b, ret_norm_w, gla_norm_w, after=gx[4])
    dwi_s = _matmul_tn(dproj, u, D_IN_PAD, 1024, 1024, "tn_dwin", out_rows=IN_COLS)
    gi = _exchange_start([dwi_s], False, "gradin_start")
    grad_x, acc_i = _inproj_bwd(dproj, x2d, dxa, sc1p, w_in_t, after=gi[4])

    loss_part = jnp.sum(acc_f[3])
    small = jnp.concatenate([
        acc_i[1:2], acc_i[0:1], acc_b[4:5], acc_b[1:2], acc_b[0:1], acc_f[2:3],
        acc_b[2:3], acc_b[3:4], acc_f[0:1], acc_f[1:2],
        jnp.concatenate([dvec[0:1], dvec[1:2]], axis=1),
        jnp.concatenate([dvec[2:3, :GLA_KW], jnp.full((1, 128), loss_part, F32),
                         jnp.zeros((1, D_MODEL - GLA_KW - 128), F32)], axis=1),
        jnp.zeros((4, D_MODEL), F32)], axis=0)
    sg = _exchange_start([small, dgw[:GATE_RANK]], True, "small_start")

    r_wo, r_w1, r_w2 = _exchange_wait(*gx[:4], sg[4], False, "gradx_wait")
    r_wi, = _exchange_wait(*gi[:4], sg[4], False, "gradin_wait")
    big = [_adamw(w[0], r, m_[0], v_[0], nm) for w, r, m_, v_, nm in (
        (w_out, r_wo, m_w_out, v_w_out, "adamw_out"),
        (w_ff1, r_w1, m_w_ff1, v_w_ff1, "adamw_ff1"), (w_ff2, r_w2, m_w_ff2, v_w_ff2, "adamw_ff2"))]
    tiles = lambda a: a.T.reshape(IN_COLS * D_MODEL // V7X_LANES, V7X_LANES)
    big_in = _adamw(tiles(w_in[0]), r_wi, tiles(m_w_in[0]), tiles(v_w_in[0]), "adamw_in", row_tiles=True)
    big = [tuple(b.reshape(IN_COLS, D_MODEL).T for b in big_in)] + big
    g_big, d_big, m_big, v_big = [[b[i][None] for b in big] for i in range(4)]

    small_all, gw_all = _exchange_wait(*sg[:4], big_in[1], True, "small_wait")
    dmod_all = small_all[:, :6].reshape(N_DEV, 6 * D_MODEL)
    dmod_cols = lax.dynamic_slice(dmod_all, (0, me * ADA_COLS), (N_DEV, ADA_COLS))
    ssum, gw_sum, g_b_ada, g_w_ada = _small_reduce(small_all, gw_all, c_all, dmod_cols)
    loss = ssum[SMR_MISC, GLA_KW]
    g_ggw = lax.dynamic_slice(gw_sum, (0, me * (GLA_KW // N_DEV)), (GATE_RANK, GLA_KW // N_DEV))[None]

    small_w = [b_ada, ret_norm_w, gla_gate_b, gla_norm_w, ln1_w, ln1_b, ln2_w, ln2_b, gla_gate_w]
    small_m = [m_b_ada, m_ret_norm_w, m_gla_gate_b, m_gla_norm_w, m_ln1_w, m_ln1_b, m_ln2_w, m_ln2_b, m_gla_gate_w]
    small_v = [v_b_ada, v_ret_norm_w, v_gla_gate_b, v_gla_norm_w, v_ln1_w, v_ln1_b, v_ln2_w, v_ln2_b, v_gla_gate_w]
    res = _adamw_small(ssum, g_b_ada, g_ggw, small_w, small_m, small_v)
    small_g = list(res[:8]) + [g_ggw]
    d_small, m_small, v_small = list(res[8:17]), list(res[17:26]), list(res[26:35])

    _, d_w_ada, nm_w_ada, nv_w_ada = _adamw(w_ada[0], g_w_ada[None], m_w_ada[0], v_w_ada[0], "adamw_ada")

    def ordered(w_ada_v, small_vals, big_vals):
        b_ada_v, rnw_v, ggb_v, gnw_v, l1w_v, l1b_v, l2w_v, l2b_v, ggw_v = small_vals
        wi_v, wo_v, w1_v, w2_v = big_vals
        return [w_ada_v, b_ada_v, wi_v, rnw_v, ggw_v, ggb_v, gnw_v, wo_v, l1w_v, l1b_v, w1_v, w2_v, l2w_v, l2b_v]

    grads = ordered(g_w_ada[None], small_g, g_big)
    deltas = ordered(d_w_ada[None], d_small, d_big)
    new_m = ordered(nm_w_ada[None], m_small, m_big)
    new_v = ordered(nv_w_ada[None], v_small, v_big)
    return (loss, grad_x[None], *grads, *deltas, *new_m, *new_v)
```

```python
import numpy as np
import jax
import jax.numpy as jnp
from jax import lax
from jax.experimental import pallas as pl
from jax.experimental.pallas import tpu as pltpu

F32 = jnp.float32
BF16 = jnp.bfloat16
MESH = pl.DeviceIdType.MESH
HIGHEST = lax.Precision.HIGHEST

N_DEV = 8
D_MODEL = 1024
CHUNK = 64
RET_HEADS = 4
RET_D = 128
GLA_HEADS = 4
GLA_DK = 64
GLA_DV = 128
GLA_KW = GLA_HEADS * GLA_DK
RET_W = RET_HEADS * RET_D
GLA_VW = GLA_HEADS * GLA_DV
V7X_LANES = 128
GATE_RANK = 16
GATE_TAU = 16.0
D_FF = 4096
LN_EPS = 1e-5
ALPHA = (2.0 * 1) ** 0.25
D_IN = 3600
D_IN_PAD = 3712
ADA_COLS = 6 * D_MODEL // N_DEV
IN_COLS = D_IN // N_DEV
FF_COLS = D_FF // N_DEV
OUT_ROWS = D_MODEL // N_DEV

OFF_RQ, OFF_RK, OFF_RV, OFF_RG = 0, RET_W, 2 * RET_W, 3 * RET_W
OFF_GQ = 4 * RET_W
OFF_GK = OFF_GQ + GLA_KW
OFF_GV = OFF_GK + GLA_KW
OFF_GG = OFF_GV + GLA_VW
OFF_GLR = OFF_GG + GLA_VW

ADAM_LR, ADAM_B1, ADAM_B2, ADAM_EPS, ADAM_WD, ADAM_STEP = 0.001, 0.9, 0.999, 1e-08, 0.01, 10

V7X_VMEM_LIMIT = 62 * 1024 * 1024

ROW_TILE = 512
PROJ_TILE = 512
MIX_TILE = 512
RET_SUB = 256
GLA_SUB = 128
ADAMW_BLOCK_BYTES = 512 * 1024


def _log_gamma(h):
    return float(np.log(np.float32(1.0) - np.float32(2.0) ** np.float32(-5.0 - h)))


def _my_coords():
    return lax.axis_index("x"), lax.axis_index("y"), lax.axis_index("c")


def _flip(v, bit):
    return 1 - v if bit else v


def _peer(k):
    x, y, c = _my_coords()
    px, py, pc = _flip(x, (k >> 2) & 1), _flip(y, (k >> 1) & 1), _flip(c, k & 1)
    return (px, py, pc), 4 * px + 2 * py + pc


def _dot(a, b, dims=(((1,), (0,)), ((), ())), precision=None):
    return lax.dot_general(a, b, dims, precision=precision, preferred_element_type=F32)


NN = (((1,), (0,)), ((), ()))
NT = (((1,), (1,)), ((), ()))
TN = (((0,), (0,)), ((), ()))


def _split_bf16(v, parts):
    out = []
    for _ in range(parts):
        p = v.astype(BF16)
        out.append(p)
        v = v - p.astype(F32)
    return out


def _dot_split(a, b, dims, a_exact=False):
    if a_exact:
        ab = a.astype(BF16)
        return sum(_dot(ab, p, dims) for p in _split_bf16(b, 2))
    a_hi, a_lo = _split_bf16(a, 2)
    b_hi, b_lo = _split_bf16(b, 2)
    return _dot(a_hi, b_hi, dims) + _dot(a_hi, b_lo, dims) + _dot(a_lo, b_hi, dims)


def _sigmoid(x):
    return 1.0 / (1.0 + jnp.exp(-x))


def _ln_stats(x):
    mu = jnp.mean(x, axis=-1, keepdims=True)
    xc = x - mu
    var = jnp.mean(xc * xc, axis=-1, keepdims=True)
    rstd = lax.rsqrt(var + LN_EPS)
    return xc * rstd, rstd


def _ln_bwd(dyh, xh, rstd):
    return rstd * (dyh - jnp.mean(dyh, axis=-1, keepdims=True) - xh * jnp.mean(dyh * xh, axis=-1, keepdims=True))


def _adaln_mod(c_ext, w_ada_l, b_l, w_in_l):
    width = c_ext.shape[1]

    def body(c_ref, w_ref, b_ref, wi_ref, call_ref, mod_ref, wig_ref, s1, r1, s2, r2, gs, gr, gl):
        gather = _TwoLevelGather([wi_ref], [wig_ref], gs, gr, gl)
        gather.start()
        x, y, c = _my_coords()
        me = 4 * x + 2 * y + c
        call_ref[me] = c_ref[...]
        sends = []
        for k in range(1, N_DEV):
            peer, _ = _peer(k)
            cp = pltpu.make_async_remote_copy(c_ref, call_ref.at[me], s1.at[k - 1], r1.at[k - 1],
                                              device_id=peer, device_id_type=MESH)
            cp.start()
            sends.append(cp)
        for k in range(1, N_DEV):
            peer, pid = _peer(k)
            pltpu.make_async_remote_copy(c_ref, call_ref.at[pid], s1.at[k - 1], r1.at[k - 1],
                                         device_id=peer, device_id_type=MESH).wait_recv()
        for cp in sends:
            cp.wait_send()
        row = lax.broadcasted_iota(jnp.int32, (N_DEV, D_MODEL), 0)
        call = jnp.zeros((N_DEV, D_MODEL), F32)
        for j in range(N_DEV):
            call = jnp.where(row == j, jnp.broadcast_to(call_ref[j][:, :D_MODEL], (N_DEV, D_MODEL)), call)
        sc = call * _sigmoid(call)
        mod = _dot(sc, w_ref[...], NN, HIGHEST) + b_ref[...]
        mod_ref[me] = mod
        sends = []
        for k in range(1, N_DEV):
            peer, _ = _peer(k)
            cp = pltpu.make_async_remote_copy(mod_ref.at[me], mod_ref.at[me], s2.at[k - 1], r2.at[k - 1],
                                              device_id=peer, device_id_type=MESH)
            cp.start()
            sends.append(cp)
        for k in range(1, N_DEV):
            peer, pid = _peer(k)
            pltpu.make_async_remote_copy(mod_ref.at[pid], mod_ref.at[pid], s2.at[k - 1], r2.at[k - 1],
                                         device_id=peer, device_id_type=MESH).wait_recv()
        for cp in sends:
            cp.wait_send()
        gather.forward()
        gather.finish()

    vm = pl.BlockSpec(memory_space=pltpu.VMEM)
    hbm = pl.BlockSpec(memory_space=pl.ANY)
    return pl.pallas_call(
        body, name="adaln_mod",
        out_shape=(jax.ShapeDtypeStruct((N_DEV, 1, width), F32),
                   jax.ShapeDtypeStruct((N_DEV, N_DEV, ADA_COLS), F32),
                   jax.ShapeDtypeStruct((N_DEV, *w_in_l.shape), w_in_l.dtype)),
        in_specs=[vm, vm, vm, hbm], out_specs=(vm, vm, hbm),
        scratch_shapes=[pltpu.SemaphoreType.DMA((N_DEV - 1,))] * 4
        + [pltpu.SemaphoreType.DMA((7,)), pltpu.SemaphoreType.DMA((7,)), pltpu.SemaphoreType.DMA((1,))],
        compiler_params=pltpu.CompilerParams(vmem_limit_bytes=V7X_VMEM_LIMIT),
    )(c_ext, w_ada_l, b_l, w_in_l)


class _TwoLevelGather:
    def __init__(self, x_refs, out_refs, send_sems, recv_sems, local_sems):
        self.x_refs, self.out_refs = x_refs, out_refs
        self.send_sems, self.recv_sems, self.local_sems = send_sems, recv_sems, local_sems
        x, y, c = _my_coords()
        self.c = c
        self.me, self.sibling = (x, y, c), (x, y, 1 - c)
        self.chips = [(1 - x, y), (x, 1 - y), (1 - x, 1 - y)]

    def _copy(self, a, k, block, to, src=None):
        px, py, pc = block
        slab = self.out_refs[a].at[4 * px + 2 * py + pc]
        return pltpu.make_async_remote_copy(
            src_ref=slab if src is None else src, dst_ref=slab,
            send_sem=self.send_sems.at[7 * a + k], recv_sem=self.recv_sems.at[7 * a + k],
            device_id=to, device_id_type=MESH)

    def _mine(self, a):
        px, py, pc = self.me
        return pltpu.make_async_copy(self.x_refs[a], self.out_refs[a].at[4 * px + 2 * py + pc], self.local_sems.at[a])

    def _first(self, a):
        cps = [self._copy(a, 0, self.me, self.sibling, src=self.x_refs[a])]
        cps += [self._copy(a, 1 + j, self.me, (*chip, self.c), src=self.x_refs[a]) for j, chip in enumerate(self.chips)]
        return cps

    def _passed(self, a):
        return [self._copy(a, 4 + j, (*chip, self.c), self.sibling) for j, chip in enumerate(self.chips)]

    def start(self):
        for a in range(len(self.x_refs)):
            self._mine(a).start()
            for cp in self._first(a):
                cp.start()

    def forward(self):
        for a in range(len(self.x_refs)):
            passed = self._passed(a)
            for j, chip in enumerate(self.chips):
                self._copy(a, 1 + j, (*chip, self.c), self.me).wait_recv()
                passed[j].start()

    def finish(self):
        for a in range(len(self.x_refs)):
            self._copy(a, 0, self.sibling, self.me).wait_recv()
            for j, chip in enumerate(self.chips):
                self._copy(a, 4 + j, (*chip, 1 - self.c), self.me).wait_recv()
            for cp in self._first(a) + self._passed(a):
                cp.wait_send()
            self._mine(a).wait()


def _exchange_copy(src_refs, land_refs, send_sems, recv_sems, a, k, gather, receiving):
    x, y, c = _my_coords()
    me = 4 * x + 2 * y + c
    peer, pid = _peer(k)
    src = src_refs[a] if gather else src_refs[a].at[pid]
    dst = land_refs[a].at[pid if receiving else me]
    return pltpu.make_async_remote_copy(src, dst, send_sems.at[7 * a + k - 1], recv_sems.at[7 * a + k - 1],
                                        device_id=peer, device_id_type=MESH)


def _own_copy(src_refs, land_refs, send_sems, a, n, gather):
    x, y, c = _my_coords()
    me = 4 * x + 2 * y + c
    src = src_refs[a] if gather else src_refs[a].at[me]
    return pltpu.make_async_copy(src, land_refs[a].at[me], send_sems.at[7 * n + a])


def _exchange_start(srcs, gather, name, after=None):
    n = len(srcs)
    land_shapes = [(N_DEV, *s.shape) if gather else s.shape for s in srcs]
    n_in = n if after is None else n + 1

    def body(*refs):
        src_refs, send_sems, recv_sems, token = refs[:n], refs[n_in], refs[n_in + 1], refs[n_in + 2 + 2 * n]
        land_refs = refs[n_in + 2 + n:n_in + 2 + 2 * n]
        for a in range(n):
            _own_copy(src_refs, land_refs, send_sems, a, n, gather).start()
            for k in range(1, N_DEV):
                _exchange_copy(src_refs, land_refs, send_sems, recv_sems, a, k, gather, receiving=False).start()
        token[...] = jnp.zeros_like(token)

    hbm = pl.BlockSpec(memory_space=pltpu.HBM)
    sem = pl.BlockSpec(memory_space=pltpu.SEMAPHORE)
    res = pl.pallas_call(
        body, name=name,
        out_shape=(pltpu.SemaphoreType.DMA((8 * n,)), pltpu.SemaphoreType.DMA((7 * n,)),
                   *[pltpu.HBM(v.shape, v.dtype) for v in srcs],
                   *[pltpu.HBM(shape, v.dtype) for shape, v in zip(land_shapes, srcs)],
                   jax.ShapeDtypeStruct((8, 128), F32),
                   *([] if after is None else [jax.ShapeDtypeStruct(after.shape, after.dtype)])),
        in_specs=[hbm] * n + [pl.BlockSpec(memory_space=pl.ANY)] * (n_in - n),
        out_specs=(sem, sem, *([hbm] * (2 * n)), pl.BlockSpec(memory_space=pltpu.VMEM),
                   *([pl.BlockSpec(memory_space=pl.ANY)] * (n_in - n))),
        input_output_aliases={**{i: 2 + i for i in range(n)}, **({} if after is None else {n: 3 + 2 * n})},
        compiler_params=pltpu.CompilerParams(has_side_effects=pltpu.SideEffectType.DATAFLOW_SIDE_EFFECTING),
    )(*[pltpu.with_memory_space_constraint(v, pltpu.HBM) for v in srcs], *([] if after is None else [after]))
    return (res[0], res[1], list(res[2:2 + n]), list(res[2 + n:2 + 2 * n]), res[2 + 2 * n],
            None if after is None else res[3 + 2 * n])


def _exchange_wait(send_sems, recv_sems, srcs, lands, after, gather, name):
    n = len(srcs)

    def body(*refs):
        src_refs, land_refs, s_sems, r_sems = refs[:n], refs[n:2 * n], refs[2 * n], refs[2 * n + 1]
        for a in range(n):
            _own_copy(src_refs, land_refs, s_sems, a, n, gather).wait()
            for k in range(1, N_DEV):
                _exchange_copy(src_refs, land_refs, s_sems, r_sems, a, k, gather, receiving=False).wait_send()
                _exchange_copy(src_refs, land_refs, s_sems, r_sems, a, k, gather, receiving=True).wait_recv()

    hbm = pl.BlockSpec(memory_space=pltpu.HBM)
    sem = pl.BlockSpec(memory_space=pltpu.SEMAPHORE)
    res = pl.pallas_call(
        body, name=name,
        out_shape=tuple(pltpu.HBM(v.shape, v.dtype) for v in srcs + lands),
        in_specs=[hbm] * (2 * n) + [sem, sem, pl.BlockSpec(memory_space=pl.ANY)],
        out_specs=tuple([hbm] * (2 * n)),
        input_output_aliases={i: i for i in range(2 * n)},
        compiler_params=pltpu.CompilerParams(has_side_effects=pltpu.SideEffectType.DATAFLOW_SIDE_EFFECTING),
    )(*srcs, *lands, send_sems, recv_sems, after)
    return list(res[n:])


def _load_resident(step_is_first, pairs, sem):
    @pl.when(step_is_first)
    def _():
        copies = [pltpu.make_async_copy(src, dst, sem.at[i]) for i, (src, dst) in enumerate(pairs)]
        for cp in copies:
            cp.start()
        for cp in copies:
            cp.wait()


def _load_w_in_t(step_is_first, w_hbm, w_vmem, sem):
    @pl.when(step_is_first)
    def _():
        w_vmem[D_IN:, :] = jnp.zeros((D_IN_PAD - D_IN, D_MODEL), BF16)
    _load_resident(step_is_first, [(w_hbm, w_vmem.at[pl.ds(0, D_IN)])], sem)


def _inproj_fwd(x2d, sc1p, sh1, w_in_t, after):
    t = x2d.shape[0]
    tm = min(PROJ_TILE, t)

    def body(x_ref, sc_ref, sh_ref, w_hbm, after_ref, proj_ref, u_ref, w_vmem, sem):
        _load_w_in_t(pl.program_id(0) == 0, w_hbm, w_vmem, sem)
        xh, _ = _ln_stats(x_ref[...])
        ub = (xh * sc_ref[...] + sh_ref[...]).astype(BF16)
        u_ref[...] = ub
        proj_ref[...] = _dot(ub, w_vmem[...], NT)

    row = lambda i: (i, 0)
    fix = lambda i: (0, 0)
    return pl.pallas_call(
        body, name="inproj_fwd", grid=(t // tm,),
        in_specs=[pl.BlockSpec((tm, D_MODEL), row), pl.BlockSpec((1, D_MODEL), fix), pl.BlockSpec((1, D_MODEL), fix),
                  pl.BlockSpec(memory_space=pl.ANY), pl.BlockSpec(memory_space=pl.ANY)],
        out_specs=(pl.BlockSpec((tm, D_IN_PAD), row), pl.BlockSpec((tm, D_MODEL), row)),
        out_shape=(jax.ShapeDtypeStruct((t, D_IN_PAD), F32), jax.ShapeDtypeStruct((t, D_MODEL), BF16)),
        scratch_shapes=[pltpu.VMEM((D_IN_PAD, D_MODEL), BF16), pltpu.SemaphoreType.DMA((1,))],
        compiler_params=pltpu.CompilerParams(dimension_semantics=("arbitrary",), vmem_limit_bytes=V7X_VMEM_LIMIT),
    )(x2d, sc1p, sh1, w_in_t, after)


CHUNK_SHIFT = CHUNK.bit_length() - 1


def _ret_tables(t, tl):
    r = lax.broadcasted_iota(jnp.int32, (tl, tl), 0)
    c = lax.broadcasted_iota(jnp.int32, (tl, tl), 1)
    allowed = jnp.right_shift(c, CHUNK_SHIFT) <= jnp.right_shift(r, CHUNK_SHIFT)
    dist = jnp.abs(r - c).astype(F32)
    rowf = lax.broadcasted_iota(jnp.int32, (tl, RET_D), 0).astype(F32)
    lgs = [_log_gamma(h) for h in range(RET_HEADS)]
    dec = jnp.stack([jnp.where(allowed, jnp.exp(lg * dist), 0.0) for lg in lgs])
    qkd = jnp.stack([jnp.exp(lg * (rowf + 1.0)) for lg in lgs] + [jnp.exp(lg * (tl - 1.0 - rowf)) for lg in lgs])
    inv = 1.0 / (10000.0 ** jnp.linspace(0.0, 1.0, RET_D // 2, dtype=F32))
    off = jnp.arange(tl, dtype=F32)[:, None] * inv[None, :]
    start = (jnp.arange(t // tl, dtype=F32) * tl)[:, None] * inv[None, :]
    co, so = jnp.cos(off), jnp.sin(off)
    rot_in = jnp.stack([jnp.concatenate([co, co], 1), jnp.concatenate([so, so], 1),
                        jnp.concatenate([-co, co], 1), jnp.concatenate([-so, so], 1)])
    cs, ss = jnp.cos(start), jnp.sin(start)
    rot_tile = jnp.concatenate([cs, cs, ss, ss], axis=1)
    rot_tile = jnp.broadcast_to(rot_tile[:, None, :], (t // tl, 8, 2 * RET_D))
    return dec, qkd, rot_in, rot_tile


def _tile_gammas(tl):
    return [float(np.exp(np.float32(_log_gamma(h)) * np.float32(tl))) for h in range(RET_HEADS)]


def _tile_rotary(rot_in_ref, rot_tile_ref, j):
    ca, sa = rot_tile_ref[j, 0:1, 0:RET_D], rot_tile_ref[j, 0:1, RET_D:2 * RET_D]
    cosv = ca * rot_in_ref[0] - sa * rot_in_ref[1]
    sinv = sa * rot_in_ref[2] + ca * rot_in_ref[3]
    return cosv, sinv


def _gla_consts(tl):
    r = lax.broadcasted_iota(jnp.int32, (tl, tl), 0)
    c = lax.broadcasted_iota(jnp.int32, (tl, tl), 1)
    ltri = (c <= r).astype(F32)
    utri = (c >= r).astype(F32)
    lane = lax.broadcasted_iota(jnp.int32, (1, GLA_KW), 1)
    hmask = [((lane >= h * GLA_DK) & (lane < (h + 1) * GLA_DK)).astype(F32) for h in range(GLA_HEADS)]
    rs = lax.broadcasted_iota(jnp.int32, (GLA_HEADS * tl, tl), 0) & (tl - 1)
    cs = lax.broadcasted_iota(jnp.int32, (GLA_HEADS * tl, tl), 1)
    lower = cs <= rs
    same = jnp.right_shift(cs, CHUNK_SHIFT) == jnp.right_shift(rs, CHUNK_SHIFT)
    upper = jnp.logical_and(jnp.logical_not(lower), same)
    return dict(ltri=ltri, utri=utri, hmask=hmask, lower=lower, upper=upper)


def _tile_rows(j, tl):
    return pl.ds(j * tl, tl) if isinstance(j, int) else pl.ds(pl.multiple_of(j * tl, tl), tl)


def _for_tiles(cps, fn):
    for j in range(cps):
        fn(j, 0)


def _rotate(v, cosv, sinv):
    return v * cosv + pltpu.roll(v, RET_D // 2, 1) * sinv


def _rotate_t(d, cosv, sinv):
    return d * cosv + pltpu.roll(d * sinv, RET_D // 2, 1)


def _stack_heads(v, hmask):
    return jnp.concatenate([v * hmask[h] for h in range(GLA_HEADS)], axis=0)


def _gla_gates(glr, gw, gb, ltri, tl):
    z = _dot_split(glr, gw, NN) + gb
    la = (jnp.minimum(z, 0.0) - jnp.log(1.0 + jnp.exp(-jnp.abs(z)))) * (1.0 / GATE_TAU)
    b = _dot_split(ltri, la, NN, a_exact=True)
    level = b[tl // 2 - 1:tl // 2, :]
    ep = jnp.exp(jnp.clip(b - level, -80.0, 80.0))
    em = jnp.exp(jnp.clip(level - b, -80.0, 80.0))
    bl = b[tl - 1:tl, :]
    return z, b, bl, ep, em


def _mixer_fwd(proj, tables, gw_pad, gb, rnw, gnw):
    t = proj.shape[0]
    tc = min(MIX_TILE, t)
    tr, tg = min(RET_SUB, tc), min(GLA_SUB, tc)
    nsteps = t // tc
    scale_r = RET_D ** -0.5
    scale_g = GLA_DK ** -0.5
    gammas = _tile_gammas(tr)

    def body(rq_ref, rk_ref, rv_ref, rg_ref, gq_ref, gk_ref, gv_ref, gg_ref, glr_ref,
             dec_ref, qkd_ref, rot_in_ref, rot_tile_ref, gw_ref, gb_ref, rnw_ref, gnw_ref,
             mix_ref, oraw_ref, qrb_ref, krb_ref, rst_ref, sst_ref, r_scr, s_scr):
        @pl.when(pl.program_id(0) == 0)
        def _():
            r_scr[...] = jnp.zeros_like(r_scr)
            s_scr[...] = jnp.zeros_like(s_scr)

        gla_k = _gla_consts(tg)

        def ret_tile(j, carry):
            rows = _tile_rows(j, tr)
            cosv, sinv = _tile_rotary(rot_in_ref, rot_tile_ref, j)
            for h in range(RET_HEADS):
                cols = slice(h * RET_D, (h + 1) * RET_D)
                qr = _rotate(rq_ref[rows, cols], cosv, sinv) * scale_r
                kr = _rotate(rk_ref[rows, cols], cosv, sinv)
                vb = rv_ref[rows, cols].astype(BF16)
                qb, kb = qr.astype(BF16), kr.astype(BF16)
                qrb_ref[rows, cols] = qb
                krb_ref[rows, cols] = kb
                p = _dot(qb, kb, NT) * dec_ref[h]
                rp = r_scr[cols, :]
                o = _dot(p.astype(BF16), vb) + _dot((qr * qkd_ref[h]).astype(BF16), rp.astype(BF16))
                rst_ref[j, cols, :] = rp
                r_scr[cols, :] = gammas[h] * rp + _dot((kr * qkd_ref[RET_HEADS + h]).astype(BF16), vb, TN)
                oraw_ref[rows, cols] = o
                oc = o - jnp.mean(o, axis=-1, keepdims=True)
                n = oc * lax.rsqrt(jnp.mean(oc * oc, axis=-1, keepdims=True) + LN_EPS)
                g = rg_ref[rows, cols]
                mix_ref[rows, cols] = (n * rnw_ref[:, cols] * (g * _sigmoid(g))).astype(BF16)
            return carry

        def gla_tile(j, carry):
            k = gla_k
            tl = tg
            rows = _tile_rows(j, tg)
            _, b, bl, ep, em = _gla_gates(glr_ref[rows, :], gw_ref[...], gb_ref[...], k["ltri"], tl)
            qs = gq_ref[rows, :] * scale_g
            kk = gk_ref[rows, :]
            x_all = _dot(_stack_heads(qs * ep, k["hmask"]).astype(BF16), (kk * em).astype(BF16), NT)
            y_all = _dot(_stack_heads(qs * em, k["hmask"]).astype(BF16), (kk * ep).astype(BF16), NT)
            a_all = jnp.where(k["lower"], x_all, jnp.where(k["upper"], y_all, 0.0)).astype(BF16)
            st = s_scr[...]
            oq = _dot(_stack_heads(qs * jnp.exp(b), k["hmask"]).astype(BF16), st.astype(BF16), NT)
            kg = kk * jnp.exp(bl - b)
            sst_ref[j] = st
            st_new = st * jnp.exp(bl)
            for h in range(GLA_HEADS):
                cols = slice(h * GLA_DV, (h + 1) * GLA_DV)
                hr = slice(h * tl, (h + 1) * tl)
                vb = gv_ref[rows, cols].astype(BF16)
                o = _dot(a_all[hr, :], vb) + oq[hr, :]
                st_new = st_new + _dot(vb, (kg * k["hmask"][h]).astype(BF16), TN)
                ocols = slice(RET_W + h * GLA_DV, RET_W + (h + 1) * GLA_DV)
                oraw_ref[rows, ocols] = o
                n = o * lax.rsqrt(jnp.mean(o * o, axis=-1, keepdims=True) + LN_EPS)
                g = gg_ref[rows, cols]
                mix_ref[rows, ocols] = (n * gnw_ref[:, cols] * (g * _sigmoid(g))).astype(BF16)
            s_scr[...] = st_new
            return carry

        _for_tiles(tc // tr, ret_tile)
        _for_tiles(tc // tg, gla_tile)

    def col(width, off):
        return pl.BlockSpec((tc, width), lambda i, o=off // width: (i, o))

    fix = lambda i: (0, 0)
    fix3 = lambda i: (0, 0, 0)
    dec, qkd, rot_in, rot_tile = tables
    in_specs = [col(RET_W, OFF_RQ), col(RET_W, OFF_RK), col(RET_W, OFF_RV), col(RET_W, OFF_RG),
                col(GLA_KW, OFF_GQ), col(GLA_KW, OFF_GK), col(GLA_VW, OFF_GV), col(GLA_VW, OFF_GG),
                col(V7X_LANES, OFF_GLR),
                pl.BlockSpec(dec.shape, fix3), pl.BlockSpec(qkd.shape, fix3), pl.BlockSpec(rot_in.shape, fix3),
                pl.BlockSpec((tc // tr, 8, 2 * RET_D), lambda i: (i, 0, 0)),
                pl.BlockSpec((V7X_LANES, GLA_KW), fix), pl.BlockSpec((1, GLA_KW), fix),
                pl.BlockSpec((1, RET_W), fix), pl.BlockSpec((1, GLA_VW), fix)]
    half = pl.BlockSpec((tc, RET_W), lambda i: (i, 0))
    out_specs = (pl.BlockSpec((tc, D_MODEL), lambda i: (i, 0)), pl.BlockSpec((tc, D_MODEL), lambda i: (i, 0)),
                 half, half,
                 pl.BlockSpec((tc // tr, RET_W, RET_D), lambda i: (i, 0, 0)),
                 pl.BlockSpec((tc // tg, GLA_DV, GLA_KW), lambda i: (i, 0, 0)))
    out_shape = (jax.ShapeDtypeStruct((t, D_MODEL), BF16), jax.ShapeDtypeStruct((t, D_MODEL), F32),
                 jax.ShapeDtypeStruct((t, RET_W), BF16), jax.ShapeDtypeStruct((t, RET_W), BF16),
                 jax.ShapeDtypeStruct((t // tr, RET_W, RET_D), F32),
                 jax.ShapeDtypeStruct((t // tg, GLA_DV, GLA_KW), F32))
    return pl.pallas_call(
        body, name="mixer_fwd", grid=(nsteps,), in_specs=in_specs, out_specs=out_specs, out_shape=out_shape,
        scratch_shapes=[pltpu.VMEM((RET_W, RET_D), F32), pltpu.VMEM((GLA_DV, GLA_KW), F32)],
        compiler_params=pltpu.CompilerParams(dimension_semantics=("arbitrary",), vmem_limit_bytes=V7X_VMEM_LIMIT),
    )(*([proj] * 9), dec, qkd, rot_in, rot_tile, gw_pad, gb, rnw, gnw)


def _mid_fwd(mixed, x2d, target, vecs, w_out_b, w1_b, w2_b):
    t = x2d.shape[0]
    tm = min(ROW_TILE, t)

    def body(mix_ref, x_ref, tgt_ref, v_ref, wo_hbm, w1_hbm, w2_hbm,
             m_ref, x1n_ref, rstd_ref, u2_ref, a_ref, df_ref, dh2_ref, acc_ref, wo, w1, w2, sem):
        first = pl.program_id(0) == 0
        _load_resident(first, [(wo_hbm, wo), (w1_hbm, w1), (w2_hbm, w2)], sem)

        @pl.when(first)
        def _():
            acc_ref[...] = jnp.zeros_like(acc_ref)

        gate1, sc2p, sh2, gate2 = v_ref[0:1, :], v_ref[1:2, :], v_ref[2:3, :], v_ref[3:4, :]
        l1w, l1b, l2w, l2b = v_ref[4:5, :], v_ref[5:6, :], v_ref[6:7, :], v_ref[7:8, :]
        m = _dot(mix_ref[...], wo[...])
        m_ref[...] = m.astype(BF16)
        x1n, rstd1 = _ln_stats(ALPHA * x_ref[...] + gate1 * m)
        x1n_ref[...] = x1n
        rstd_ref[...] = rstd1
        x1 = x1n * l1w + l1b
        xh1, _ = _ln_stats(x1)
        u2 = (xh1 * sc2p + sh2).astype(BF16)
        u2_ref[...] = u2
        f = jnp.zeros((tm, D_MODEL), F32)
        for j in range(N_DEV):
            cols = slice(j * FF_COLS, (j + 1) * FF_COLS)
            a = _dot(u2, w1[j])
            a_ref[:, cols] = a.astype(BF16)
            r = jnp.maximum(a, 0.0)
            f = f + _dot((r * r).astype(BF16), w2[cols, :])
        yh, rstd2 = _ln_stats(ALPHA * x1 + gate2 * f)
        e = yh * l2w + l2b - tgt_ref[...]
        dy = e * (1.0 / D_MODEL)
        dh2 = _ln_bwd(dy * l2w, yh, rstd2)
        dh2_ref[...] = dh2
        df_ref[...] = (dh2 * gate2).astype(BF16)
        acc_ref[0:1, :] += jnp.sum(dy * yh, axis=0, keepdims=True)
        acc_ref[1:2, :] += jnp.sum(dy, axis=0, keepdims=True)
        acc_ref[2:3, :] += jnp.sum(dh2 * f, axis=0, keepdims=True)
        acc_ref[3:4, :] += jnp.sum(e * e, axis=0, keepdims=True) * (0.5 / D_MODEL)

    row = lambda i: (i, 0)
    fix = lambda i: (0, 0)
    hbm = pl.BlockSpec(memory_space=pl.ANY)
    return pl.pallas_call(
        body, name="mid_fwd", grid=(t // tm,),
        in_specs=[pl.BlockSpec((tm, D_MODEL), row), pl.BlockSpec((tm, D_MODEL), row), pl.BlockSpec((tm, D_MODEL), row),
                  pl.BlockSpec((8, D_MODEL), fix), hbm, hbm, hbm],
        out_specs=(pl.BlockSpec((tm, D_MODEL), row), pl.BlockSpec((tm, D_MODEL), row), pl.BlockSpec((tm, 1), row),
                   pl.BlockSpec((tm, D_MODEL), row), pl.BlockSpec((tm, D_FF), row), pl.BlockSpec((tm, D_MODEL), row),
                   pl.BlockSpec((tm, D_MODEL), row), pl.BlockSpec((8, D_MODEL), fix)),
        out_shape=(jax.ShapeDtypeStruct((t, D_MODEL), BF16), jax.ShapeDtypeStruct((t, D_MODEL), F32),
                   jax.ShapeDtypeStruct((t, 1), F32), jax.ShapeDtypeStruct((t, D_MODEL), BF16),
                   jax.ShapeDtypeStruct((t, D_FF), BF16), jax.ShapeDtypeStruct((t, D_MODEL), BF16),
                   jax.ShapeDtypeStruct((t, D_MODEL), F32), jax.ShapeDtypeStruct((8, D_MODEL), F32)),
        scratch_shapes=[pltpu.VMEM((D_MODEL, D_MODEL), BF16), pltpu.VMEM((N_DEV, D_MODEL, FF_COLS), BF16),
                        pltpu.VMEM((D_FF, D_MODEL), BF16), pltpu.SemaphoreType.DMA((3,))],
        compiler_params=pltpu.CompilerParams(dimension_semantics=("arbitrary",), vmem_limit_bytes=V7X_VMEM_LIMIT),
    )(mixed, x2d, target, vecs, w_out_b, w1_b, w2_b)


def _ffn_bwd(df, a, dh2, x1n, rstd1, m, vecs, w_out_b, w1_b, w2_b):
    t = x1n.shape[0]
    tm = min(ROW_TILE, t)

    def body(df_ref, a_ref, dh2_ref, x1n_ref, rstd_ref, m_ref, v_ref, wo_hbm, w1_hbm, w2_hbm,
             da_ref, dm_ref, dmix_ref, dxa_ref, acc_ref, wo, w1, w2, sem):
        first = pl.program_id(0) == 0
        _load_resident(first, [(wo_hbm, wo), (w1_hbm, w1), (w2_hbm, w2)], sem)

        @pl.when(first)
        def _():
            acc_ref[...] = jnp.zeros_like(acc_ref)

        gate1, sc2p, l1w, l1b = v_ref[0:1, :], v_ref[1:2, :], v_ref[2:3, :], v_ref[3:4, :]
        df = df_ref[...]
        du2 = jnp.zeros((tm, D_MODEL), F32)
        for j in range(N_DEV):
            cols = slice(j * FF_COLS, (j + 1) * FF_COLS)
            dr2 = _dot(df, w2[cols, :], NT)
            da = (dr2 * (2.0 * jnp.maximum(a_ref[:, cols].astype(F32), 0.0))).astype(BF16)
            da_ref[:, cols] = da
            du2 = du2 + _dot(da, w1[j], NT)
        x1n = x1n_ref[...]
        xh1, rstd0 = _ln_stats(x1n * l1w + l1b)
        dx1 = ALPHA * dh2_ref[...] + _ln_bwd(du2 * sc2p, xh1, rstd0)
        dh1 = _ln_bwd(dx1 * l1w, x1n, rstd_ref[...])
        dxa_ref[...] = ALPHA * dh1
        dm = (dh1 * gate1).astype(BF16)
        dm_ref[...] = dm
        dmix_ref[...] = _dot(dm, wo[...], NT)
        acc_ref[0:1, :] += jnp.sum(du2 * xh1, axis=0, keepdims=True)
        acc_ref[1:2, :] += jnp.sum(du2, axis=0, keepdims=True)
        acc_ref[2:3, :] += jnp.sum(dx1 * x1n, axis=0, keepdims=True)
        acc_ref[3:4, :] += jnp.sum(dx1, axis=0, keepdims=True)
        acc_ref[4:5, :] += jnp.sum(dh1 * m_ref[...].astype(F32), axis=0, keepdims=True)

    row = lambda i: (i, 0)
    fix = lambda i: (0, 0)
    hbm = pl.BlockSpec(memory_space=pl.ANY)
    return pl.pallas_call(
        body, name="ffn_bwd", grid=(t // tm,),
        in_specs=[pl.BlockSpec((tm, D_MODEL), row), pl.BlockSpec((tm, D_FF), row), pl.BlockSpec((tm, D_MODEL), row),
                  pl.BlockSpec((tm, D_MODEL), row), pl.BlockSpec((tm, 1), row), pl.BlockSpec((tm, D_MODEL), row),
                  pl.BlockSpec((8, D_MODEL), fix), hbm, hbm, hbm],
        out_specs=(pl.BlockSpec((tm, D_FF), row), pl.BlockSpec((tm, D_MODEL), row), pl.BlockSpec((tm, D_MODEL), row),
                   pl.BlockSpec((tm, D_MODEL), row), pl.BlockSpec((8, D_MODEL), fix)),
        out_shape=(jax.ShapeDtypeStruct((t, D_FF), BF16), jax.ShapeDtypeStruct((t, D_MODEL), BF16),
                   jax.ShapeDtypeStruct((t, D_MODEL), F32), jax.ShapeDtypeStruct((t, D_MODEL), F32),
                   jax.ShapeDtypeStruct((8, D_MODEL), F32)),
        scratch_shapes=[pltpu.VMEM((D_MODEL, D_MODEL), BF16), pltpu.VMEM((N_DEV, D_MODEL, FF_COLS), BF16),
                        pltpu.VMEM((D_FF, D_MODEL), BF16), pltpu.SemaphoreType.DMA((3,))],
        compiler_params=pltpu.CompilerParams(dimension_semantics=("arbitrary",), vmem_limit_bytes=V7X_VMEM_LIMIT),
    )(df, a, dh2, x1n, rstd1, m, vecs, w_out_b, w1_b, w2_b)


def _matmul_tn(lhs, rhs, tmm, tn, tk, name, relu_sq=False, col_slab=None, out_rows=None):
    t, mm = lhs.shape
    assert out_rows is None or (col_slab is None and tmm == mm)
    nn = rhs.shape[1]
    tk = min(tk, t)
    nk = t // tk

    def body(l_ref, r_ref, o_ref, acc):
        kk = pl.program_id(2)

        @pl.when(kk == 0)
        def _():
            acc[...] = jnp.zeros_like(acc)

        l = l_ref[...]
        if relu_sq:
            lf = jnp.maximum(l.astype(F32), 0.0)
            l = (lf * lf).astype(BF16)
        acc[...] += _dot(l, r_ref[...], TN)

        @pl.when(kk == nk - 1)
        def _():
            if out_rows is not None:
                for s in range(N_DEV):
                    o_ref[s] = acc[s * out_rows:(s + 1) * out_rows, :].astype(o_ref.dtype)
            elif col_slab is None:
                o_ref[...] = acc[...].astype(o_ref.dtype)
            else:
                for s in range(tn // col_slab):
                    o_ref[s] = acc[:, s * col_slab:(s + 1) * col_slab].astype(o_ref.dtype)

    if out_rows is not None:
        out_spec = pl.BlockSpec((N_DEV, out_rows, tn), lambda i, j, k: (0, 0, j))
        out_shape = jax.ShapeDtypeStruct((N_DEV, out_rows, nn), BF16)
    elif col_slab is None:
        out_spec = pl.BlockSpec((tmm, tn), lambda i, j, k: (i, j))
        out_shape = jax.ShapeDtypeStruct((mm, nn), BF16)
    else:
        out_spec = pl.BlockSpec((tn // col_slab, tmm, col_slab), lambda i, j, k: (j, i, 0))
        out_shape = jax.ShapeDtypeStruct((nn // col_slab, mm, col_slab), BF16)
    return pl.pallas_call(
        body, name=name, grid=(mm // tmm, nn // tn, nk),
        in_specs=[pl.BlockSpec((tk, tmm), lambda i, j, k: (k, i)), pl.BlockSpec((tk, tn), lambda i, j, k: (k, j))],
        out_specs=out_spec,
        out_shape=out_shape,
        scratch_shapes=[pltpu.VMEM((tmm, tn), F32)],
        compiler_params=pltpu.CompilerParams(dimension_semantics=("arbitrary", "arbitrary", "arbitrary"),
                                             vmem_limit_bytes=V7X_VMEM_LIMIT),
    )(lhs, rhs)


def _mixer_bwd(dmix, proj, qrb, krb, oraw, tables, rst, sst, gw_pad, gb, rnw, gnw, after):
    t = proj.shape[0]
    tc = min(MIX_TILE, t)
    tr, tg = min(RET_SUB, tc), min(GLA_SUB, tc)
    nsteps = t // tc
    scale_r = RET_D ** -0.5
    scale_g = GLA_DK ** -0.5
    gammas = _tile_gammas(tr)

    def body(dmix_ref, qrb_ref, krb_ref, rv_ref, rg_ref, gq_ref, gk_ref, gv_ref, gg_ref, glr_ref, oraw_ref,
             dec_ref, qkd_ref, rot_in_ref, rot_tile_ref, rst_ref, sst_ref, gw_ref, gb_ref, rnw_ref, gnw_ref, after_ref,
             dproj_ref, dgw_ref, dvec_ref, dr_scr, ds_scr):
        @pl.when(pl.program_id(0) == 0)
        def _():
            dr_scr[...] = jnp.zeros_like(dr_scr)
            ds_scr[...] = jnp.zeros_like(ds_scr)
            dgw_ref[...] = jnp.zeros_like(dgw_ref)
            dvec_ref[...] = jnp.zeros_like(dvec_ref)

        gla_k = _gla_consts(tg)
        last_row = lax.broadcasted_iota(jnp.int32, (tg, GLA_KW), 0) == tg - 1

        def ret_tile(jj, carry):
            j = tc // tr - 1 - jj
            rows = _tile_rows(j, tr)
            cosv, sinv = _tile_rotary(rot_in_ref, rot_tile_ref, j)
            for h in range(RET_HEADS):
                cols = slice(h * RET_D, (h + 1) * RET_D)
                o = oraw_ref[rows, cols]
                g = rg_ref[rows, cols]
                w = rnw_ref[:, cols]
                dout = dmix_ref[rows, cols]
                oc = o - jnp.mean(o, axis=-1, keepdims=True)
                inv = lax.rsqrt(jnp.mean(oc * oc, axis=-1, keepdims=True) + LN_EPS)
                n = oc * inv
                sg = _sigmoid(g)
                sil = g * sg
                dn = dout * w * sil
                dvec_ref[0:1, cols] += jnp.sum(dout * n * sil, axis=0, keepdims=True)
                dproj_ref[rows, OFF_RG + h * RET_D:OFF_RG + (h + 1) * RET_D] = (
                    dout * n * w * (sg * (1.0 + g * (1.0 - sg)))).astype(BF16)
                doc = inv * (dn - n * jnp.mean(dn * n, axis=-1, keepdims=True))
                do = doc - jnp.mean(doc, axis=-1, keepdims=True)

                qb, kb = qrb_ref[rows, cols], krb_ref[rows, cols]
                qr, kr = qb.astype(F32), kb.astype(F32)
                vb = rv_ref[rows, cols].astype(BF16)
                dob = do.astype(BF16)
                qd, kd = qkd_ref[h], qkd_ref[RET_HEADS + h]
                p = _dot(qb, kb, NT) * dec_ref[h]
                rp = rst_ref[j, cols, :].astype(BF16)
                dr = dr_scr[cols, :]
                drb = dr.astype(BF16)
                dpb = (_dot(dob, vb, NT) * dec_ref[h]).astype(BF16)
                dqr = _dot(dpb, kb) + _dot(dob, rp, NT) * qd
                dkr = _dot(dpb, qb, TN) + _dot(vb, drb, NT) * kd
                dv = _dot(p.astype(BF16), dob, TN) + _dot((kr * kd).astype(BF16), drb)
                dr_scr[cols, :] = gammas[h] * dr + _dot((qr * qd).astype(BF16), dob, TN)
                dproj_ref[rows, OFF_RQ + h * RET_D:OFF_RQ + (h + 1) * RET_D] = (
                    _rotate_t(dqr, cosv, sinv) * scale_r).astype(BF16)
                dproj_ref[rows, OFF_RK + h * RET_D:OFF_RK + (h + 1) * RET_D] = _rotate_t(dkr, cosv, sinv).astype(BF16)
                dproj_ref[rows, OFF_RV + h * RET_D:OFF_RV + (h + 1) * RET_D] = dv.astype(BF16)
            return carry

        def gla_tile(jj, carry):
            k = gla_k
            tl = tg
            j = tc // tg - 1 - jj
            rows = _tile_rows(j, tg)
            glr = glr_ref[rows, :]
            z, b, bl, ep, em = _gla_gates(glr, gw_ref[...], gb_ref[...], k["ltri"], tl)
            qs = gq_ref[rows, :] * scale_g
            kk = gk_ref[rows, :]
            eb = jnp.exp(b)
            ekb = jnp.exp(bl - b)
            ebl = jnp.exp(bl)
            ql, qu, kl, ku = qs * ep, qs * em, kk * em, kk * ep
            qg, kg = qs * eb, kk * ekb
            qlm = _stack_heads(ql, k["hmask"]).astype(BF16)
            qum = _stack_heads(qu, k["hmask"]).astype(BF16)
            klb, kub = kl.astype(BF16), ku.astype(BF16)
            a_all = jnp.where(k["lower"], _dot(qlm, klb, NT),
                              jnp.where(k["upper"], _dot(qum, kub, NT), 0.0)).astype(BF16)
            st = sst_ref[j]
            stb = st.astype(BF16)
            ds = ds_scr[...]
            dsb = ds.astype(BF16)
            ds_new = ds * ebl
            da_parts = []
            dqg = jnp.zeros((tl, GLA_KW), F32)
            dkg = jnp.zeros((tl, GLA_KW), F32)
            for h in range(GLA_HEADS):
                cols = slice(h * GLA_DV, (h + 1) * GLA_DV)
                hr = slice(h * tl, (h + 1) * tl)
                ocols = slice(RET_W + h * GLA_DV, RET_W + (h + 1) * GLA_DV)
                o = oraw_ref[rows, ocols]
                g = gg_ref[rows, cols]
                w = gnw_ref[:, cols]
                dout = dmix_ref[rows, ocols]
                inv = lax.rsqrt(jnp.mean(o * o, axis=-1, keepdims=True) + LN_EPS)
                n = o * inv
                sg = _sigmoid(g)
                sil = g * sg
                dn = dout * w * sil
                dvec_ref[1:2, cols] += jnp.sum(dout * n * sil, axis=0, keepdims=True)
                dproj_ref[rows, OFF_GG + h * GLA_DV:OFF_GG + (h + 1) * GLA_DV] = (
                    dout * n * w * (sg * (1.0 + g * (1.0 - sg)))).astype(BF16)
                dob = (inv * (dn - n * jnp.mean(dn * n, axis=-1, keepdims=True))).astype(BF16)
                vb = gv_ref[rows, cols].astype(BF16)
                mh = k["hmask"][h]
                da_parts.append(_dot(dob, vb, NT))
                dv = _dot(a_all[hr, :], dob, TN) + _dot((kg * mh).astype(BF16), dsb, NT)
                dproj_ref[rows, OFF_GV + h * GLA_DV:OFF_GV + (h + 1) * GLA_DV] = dv.astype(BF16)
                dkg = dkg + mh * _dot(vb, dsb)
                dqg = dqg + mh * _dot(dob, stb)
                ds_new = ds_new + _dot(dob, (qg * mh).astype(BF16), TN)
            da_all = jnp.concatenate(da_parts, axis=0)
            dal = jnp.where(k["lower"], da_all, 0.0).astype(BF16)
            dau = jnp.where(k["upper"], da_all, 0.0).astype(BF16)
            dqlm = _dot(dal, klb)
            dqum = _dot(dau, kub)
            dql = jnp.zeros((tl, GLA_KW), F32)
            dqu = jnp.zeros((tl, GLA_KW), F32)
            for h in range(GLA_HEADS):
                hr = slice(h * tl, (h + 1) * tl)
                dql = dql + k["hmask"][h] * dqlm[hr, :]
                dqu = dqu + k["hmask"][h] * dqum[hr, :]
            dkl = _dot(dal, qlm, TN)
            dku = _dot(dau, qum, TN)
            dbl = (jnp.sum(dkg * kg, axis=0, keepdims=True)
                   + jnp.sum(ds * st, axis=0, keepdims=True) * ebl)
            ds_scr[...] = ds_new
            dqs = dql * ep + dqu * em + dqg * eb
            dk = dkl * em + dku * ep + dkg * ekb
            db = dql * ql - dkl * kl - dqu * qu + dku * ku + dqg * qg - dkg * kg
            db = db + jnp.where(last_row, dbl, 0.0)
            dla = _dot_split(k["utri"], db, NN, a_exact=True)
            dz = dla * (1.0 / GATE_TAU) * _sigmoid(-z)
            dvec_ref[2:3, 0:GLA_KW] += jnp.sum(dz, axis=0, keepdims=True)
            dgw_ref[...] += _dot_split(glr, dz, TN)
            dproj_ref[rows, OFF_GLR:D_IN_PAD] = _dot(dz.astype(BF16), gw_ref[...].astype(BF16), NT).astype(BF16)
            dproj_ref[rows, OFF_GQ:OFF_GQ + GLA_KW] = (dqs * scale_g).astype(BF16)
            dproj_ref[rows, OFF_GK:OFF_GK + GLA_KW] = dk.astype(BF16)
            return carry

        _for_tiles(tc // tr, ret_tile)
        _for_tiles(tc // tg, gla_tile)

    rev = lambda i: (nsteps - 1 - i, 0)

    def col(width, off):
        return pl.BlockSpec((tc, width), lambda i, o=off // width: (nsteps - 1 - i, o))

    fix = lambda i: (0, 0)
    fix3 = lambda i: (0, 0, 0)
    dec, qkd, rot_in, rot_tile = tables
    half = pl.BlockSpec((tc, RET_W), rev)
    in_specs = [pl.BlockSpec((tc, D_MODEL), rev), half, half, col(RET_W, OFF_RV), col(RET_W, OFF_RG),
                col(GLA_KW, OFF_GQ), col(GLA_KW, OFF_GK), col(GLA_VW, OFF_GV), col(GLA_VW, OFF_GG),
                col(V7X_LANES, OFF_GLR),
                pl.BlockSpec((tc, D_MODEL), rev),
                pl.BlockSpec(dec.shape, fix3), pl.BlockSpec(qkd.shape, fix3), pl.BlockSpec(rot_in.shape, fix3),
                pl.BlockSpec((tc // tr, 8, 2 * RET_D), lambda i: (nsteps - 1 - i, 0, 0)),
                pl.BlockSpec((tc // tr, RET_W, RET_D), lambda i: (nsteps - 1 - i, 0, 0)),
                pl.BlockSpec((tc // tg, GLA_DV, GLA_KW), lambda i: (nsteps - 1 - i, 0, 0)),
                pl.BlockSpec((V7X_LANES, GLA_KW), fix), pl.BlockSpec((1, GLA_KW), fix),
                pl.BlockSpec((1, RET_W), fix), pl.BlockSpec((1, GLA_VW), fix), pl.BlockSpec(memory_space=pl.ANY)]
    out_specs = (pl.BlockSpec((tc, D_IN_PAD), rev), pl.BlockSpec((V7X_LANES, GLA_KW), fix),
                 pl.BlockSpec((8, RET_W), fix))
    out_shape = (jax.ShapeDtypeStruct((t, D_IN_PAD), BF16), jax.ShapeDtypeStruct((V7X_LANES, GLA_KW), F32),
                 jax.ShapeDtypeStruct((8, RET_W), F32))
    return pl.pallas_call(
        body, name="mixer_bwd", grid=(nsteps,), in_specs=in_specs, out_specs=out_specs, out_shape=out_shape,
        scratch_shapes=[pltpu.VMEM((RET_W, RET_D), F32), pltpu.VMEM((GLA_DV, GLA_KW), F32)],
        compiler_params=pltpu.CompilerParams(dimension_semantics=("arbitrary",), vmem_limit_bytes=V7X_VMEM_LIMIT),
    )(dmix, qrb, krb, *([proj] * 7), oraw, dec, qkd, rot_in, rot_tile, rst, sst, gw_pad, gb, rnw, gnw, after)


def _inproj_bwd(dproj, x2d, dxa, sc1p, w_in_t, after):
    t = x2d.shape[0]
    tm = min(2 * PROJ_TILE, t)

    def body(dp_ref, x_ref, dxa_ref, sc_ref, w_hbm, after_ref, gx_ref, acc_ref, w_vmem, sem):
        first = pl.program_id(0) == 0
        _load_w_in_t(first, w_hbm, w_vmem, sem)

        @pl.when(first)
        def _():
            acc_ref[...] = jnp.zeros_like(acc_ref)

        du = _dot(dp_ref[...], w_vmem[...])
        xh, rstd = _ln_stats(x_ref[...])
        gx_ref[...] = dxa_ref[...] + _ln_bwd(du * sc_ref[...], xh, rstd)
        acc_ref[0:1, :] += jnp.sum(du * xh, axis=0, keepdims=True)
        acc_ref[1:2, :] += jnp.sum(du, axis=0, keepdims=True)

    row = lambda i: (i, 0)
    fix = lambda i: (0, 0)
    return pl.pallas_call(
        body, name="inproj_bwd", grid=(t // tm,),
        in_specs=[pl.BlockSpec((tm, D_IN_PAD), row), pl.BlockSpec((tm, D_MODEL), row), pl.BlockSpec((tm, D_MODEL), row),
                  pl.BlockSpec((1, D_MODEL), fix), pl.BlockSpec(memory_space=pl.ANY), pl.BlockSpec(memory_space=pl.ANY)],
        out_specs=(pl.BlockSpec((tm, D_MODEL), row), pl.BlockSpec((8, D_MODEL), fix)),
        out_shape=(jax.ShapeDtypeStruct((t, D_MODEL), F32), jax.ShapeDtypeStruct((8, D_MODEL), F32)),
        scratch_shapes=[pltpu.VMEM((D_IN_PAD, D_MODEL), BF16), pltpu.SemaphoreType.DMA((1,))],
        compiler_params=pltpu.CompilerParams(dimension_semantics=("arbitrary",), vmem_limit_bytes=V7X_VMEM_LIMIT),
    )(dproj, x2d, dxa, sc1p, w_in_t, after)


def _adam_math(w, g, m, v):
    m = ADAM_B1 * m + (1.0 - ADAM_B1) * g
    v = ADAM_B2 * v + (1.0 - ADAM_B2) * (g * g)
    m_hat = m / (1.0 - ADAM_B1 ** ADAM_STEP)
    v_hat = v / (1.0 - ADAM_B2 ** ADAM_STEP)
    delta = -ADAM_LR * (m_hat / (jnp.sqrt(v_hat) + ADAM_EPS) + ADAM_WD * w)
    return delta, m, v


def _adamw(w, gparts, m, v, name, row_tiles=False):
    nparts, rows, cols = gparts.shape
    tr = rows
    for cand in (512, 256, 128, 64, 32, 16):
        if rows % cand == 0 and cand * cols * 4 <= ADAMW_BLOCK_BYTES:
            tr = cand
            break
    deep = dict(pipeline_mode=pl.Buffered(3)) if rows // tr >= 3 else {}

    def body(w_ref, g_ref, m_ref, v_ref, go_ref, d_ref, mo_ref, vo_ref):
        g = g_ref[0].astype(F32)
        for p in range(1, nparts):
            g = g + g_ref[p].astype(F32)
        if row_tiles:
            g = g.reshape(tr, cols // V7X_LANES, V7X_LANES).reshape(tr * cols // V7X_LANES, V7X_LANES)
        delta, mn, vn = _adam_math(w_ref[...], g, m_ref[...], v_ref[...])
        go_ref[...] = g
        d_ref[...] = delta
        mo_ref[...] = mn
        vo_ref[...] = vn

    blk_shape = (tr * cols // V7X_LANES, V7X_LANES) if row_tiles else (tr, cols)
    blk = pl.BlockSpec(blk_shape, lambda i: (i, 0))
    blk_in = pl.BlockSpec(blk_shape, lambda i: (i, 0), **deep)
    shp = jax.ShapeDtypeStruct((rows * blk_shape[0] // tr, blk_shape[1]), F32)
    in_specs = [blk_in, pl.BlockSpec((nparts, tr, cols), lambda i: (0, i, 0), **deep), blk_in, blk_in]

    def streamed(*hbm_refs):
        pltpu.emit_pipeline(body, grid=(rows // tr,), in_specs=in_specs, out_specs=[blk, blk, blk, blk])(*hbm_refs)

    hbm = pl.BlockSpec(memory_space=pl.ANY)
    return pl.pallas_call(
        streamed, name=name,
        in_specs=[hbm] * 4, out_specs=(hbm, hbm, hbm, hbm), out_shape=(shp, shp, shp, shp),
        compiler_params=pltpu.CompilerParams(vmem_limit_bytes=V7X_VMEM_LIMIT),
    )(w, gparts, m, v)


def _small_reduce(gathered, gathered_gw, c_all, dmod_cols):
    def body(g_ref, gw_ref, c_ref, dm_ref, sum_ref, gwsum_ref, gb_ref, gwa_ref):
        s = g_ref[0]
        sw = gw_ref[0]
        for p in range(1, N_DEV):
            s = s + g_ref[p]
            sw = sw + gw_ref[p]
        sum_ref[...] = s
        gwsum_ref[...] = sw
        for i in range(6):
            gb_ref[:, i * D_MODEL:(i + 1) * D_MODEL] = s[i:i + 1, :]
        cc = c_ref[...]
        gwa_ref[...] = _dot(cc * _sigmoid(cc), dm_ref[...], TN, HIGHEST)

    vm = pl.BlockSpec(memory_space=pltpu.VMEM)
    return pl.pallas_call(
        body, name="small_reduce",
        out_shape=(jax.ShapeDtypeStruct(gathered.shape[1:], F32), jax.ShapeDtypeStruct(gathered_gw.shape[1:], F32),
                   jax.ShapeDtypeStruct((1, 6 * D_MODEL), F32), jax.ShapeDtypeStruct((D_MODEL, ADA_COLS), F32)),
        in_specs=[vm] * 4, out_specs=(vm, vm, vm, vm),
        compiler_params=pltpu.CompilerParams(vmem_limit_bytes=V7X_VMEM_LIMIT),
    )(gathered, gathered_gw, c_all, dmod_cols)


SMR_LN1W, SMR_LN1B, SMR_LN2W, SMR_LN2B, SMR_NORMS, SMR_MISC = 6, 7, 8, 9, 10, 11


def _adamw_small(gsum, g_b_ada, g_ggw, params, moms, vels):
    n = len(params)

    def body(*refs):
        gsum_ref, gb_ref, gw_ref = refs[:3]
        w_refs, m_refs, v_refs = refs[3:3 + n], refs[3 + n:3 + 2 * n], refs[3 + 2 * n:3 + 3 * n]
        outs = refs[3 + 3 * n:]
        g_refs, d_refs, mo_refs, vo_refs = outs[:n - 1], outs[n - 1:2 * n - 1], outs[2 * n - 1:3 * n - 1], outs[3 * n - 1:]
        grads = [gb_ref[...],
                 gsum_ref[SMR_NORMS:SMR_NORMS + 1, 0:RET_W],
                 gsum_ref[SMR_MISC:SMR_MISC + 1, 0:GLA_KW],
                 gsum_ref[SMR_NORMS:SMR_NORMS + 1, RET_W:RET_W + GLA_VW],
                 gsum_ref[SMR_LN1W:SMR_LN1W + 1, :], gsum_ref[SMR_LN1B:SMR_LN1B + 1, :],
                 gsum_ref[SMR_LN2W:SMR_LN2W + 1, :], gsum_ref[SMR_LN2B:SMR_LN2B + 1, :],
                 gw_ref[...]]
        for i in range(n):
            delta, mn, vn = _adam_math(w_refs[i][...], grads[i], m_refs[i][...], v_refs[i][...])
            if i < n - 1:
                g_refs[i][...] = grads[i]
            d_refs[i][...] = delta
            mo_refs[i][...] = mn
            vo_refs[i][...] = vn

    vm = pl.BlockSpec(memory_space=pltpu.VMEM)
    shapes = [jax.ShapeDtypeStruct(p.shape, F32) for p in params]
    n_in = 3 + 3 * n
    out_shape = tuple(shapes[:n - 1] + shapes * 3)
    return pl.pallas_call(
        body, name="adamw_small", out_shape=out_shape,
        in_specs=[vm] * n_in, out_specs=tuple([vm] * len(out_shape)),
        compiler_params=pltpu.CompilerParams(vmem_limit_bytes=V7X_VMEM_LIMIT),
    )(gsum, g_b_ada, g_ggw, *params, *moms, *vels)


def kernel(x, c, w_ada, b_ada, w_in, ret_norm_w, gla_gate_w, gla_gate_b, gla_norm_w, w_out, ln1_w, ln1_b, w_ff1, w_ff2, ln2_w, ln2_b, loss_target, m_w_ada, m_b_ada, m_w_in, m_ret_norm_w, m_gla_gate_w, m_gla_gate_b, m_gla_norm_w, m_w_out, m_ln1_w, m_ln1_b, m_w_ff1, m_w_ff2, m_ln2_w, m_ln2_b, v_w_ada, v_b_ada, v_w_in, v_ret_norm_w, v_gla_gate_w, v_gla_gate_b, v_gla_norm_w, v_w_out, v_ln1_w, v_ln1_b, v_w_ff1, v_w_ff2, v_ln2_w, v_ln2_b):
    t = x.shape[1]
    xi, yi, ci = _my_coords()
    me = 4 * xi + 2 * yi + ci
    x2d = x[0]
    tgt = loss_target[0]

    c_ext = jnp.concatenate([c, gla_gate_w[0].reshape(1, GATE_RANK * GLA_KW // N_DEV)], axis=1)
    b_l = lax.dynamic_slice(b_ada, (0, me * ADA_COLS), (1, ADA_COLS))
    c_all3, mod_all, wi_g = _adaln_mod(c_ext, w_ada[0], b_l, w_in[0].T.astype(BF16))

    wg = _exchange_start([w_out[0].astype(BF16), w_ff1[0].astype(BF16), w_ff2[0].astype(BF16)],
                         True, "wgather_start", after=wi_g)

    c_all = c_all3[:, 0, :D_MODEL]
    gate_w = c_all3[:, 0, D_MODEL:].reshape(N_DEV, GATE_RANK, GLA_KW // N_DEV)
    gate_w = gate_w.transpose(1, 0, 2).reshape(GATE_RANK, GLA_KW)
    gw_pad = jnp.zeros((V7X_LANES, GLA_KW), F32).at[:GATE_RANK].set(gate_w)
    mod = lax.dynamic_slice(mod_all, (0, me, 0), (N_DEV, 1, ADA_COLS)).reshape(6, D_MODEL)
    shift1, scale1, gate1, shift2, scale2, gate2 = [mod[i:i + 1] for i in range(6)]

    w_in_t = wg[5].reshape(D_IN, D_MODEL)

    tables = _ret_tables(t, min(RET_SUB, t))

    sc1p = 1.0 + scale1
    proj, u = _inproj_fwd(x2d, sc1p, shift1, w_in_t, after=wg[4])
    mixed, oraw, qrb, krb, rst, sst = _mixer_fwd(proj, tables, gw_pad, gla_gate_b, ret_norm_w, gla_norm_w)
    wo_g, w1_b, w2_g = _exchange_wait(*wg[:4], mixed, True, "wgather_wait")
    w_out_b = wo_g.reshape(D_MODEL, D_MODEL)
    w2_b = w2_g.reshape(D_FF, D_MODEL)
    vec_f = jnp.concatenate([gate1, 1.0 + scale2, shift2, gate2, ln1_w, ln1_b, ln2_w, ln2_b], axis=0)
    m, x1n, rstd1, u2, a, df, dh2, acc_f = _mid_fwd(mixed, x2d, tgt, vec_f, w_out_b, w1_b, w2_b)

    vec_b = jnp.concatenate([gate1, 1.0 + scale2, ln1_w, ln1_b, jnp.zeros((4, D_MODEL), F32)], axis=0)
    da, dm, dmix, dxa, acc_b = _ffn_bwd(df, a, dh2, x1n, rstd1, m, vec_b, w_out_b, w1_b, w2_b)
    dw2 = _matmul_tn(a, df, 2048, 1024, 2048, "tn_dw2", relu_sq=True)
    dw1 = _matmul_tn(u2, da, 1024, 2048, 2048, "tn_dw1", col_slab=FF_COLS)
    dwo = _matmul_tn(mixed, dm, 1024, 1024, 2048, "tn_dwout")
    gx = _exchange_start([dwo.reshape(N_DEV, OUT_ROWS, D_MODEL), dw1, dw2.reshape(N_DEV, FF_COLS, D_MODEL)], False,
                         "gradx_start")
    dproj, dgw, dvec = _mixer_bwd(dmix, proj, qrb, krb, oraw, tables, rst, sst, gw_pad,
                                  gla_gate_b, ret_norm_w, gla_norm_w, after=gx[4])
    dwi_s = _matmul_tn(dproj, u, D_IN_PAD, 1024, 1024, "tn_dwin", out_rows=IN_COLS)
    gi = _exchange_start([dwi_s], False, "gradin_start")
    grad_x, acc_i = _inproj_bwd(dproj, x2d, dxa, sc1p, w_in_t, after=gi[4])

    loss_part = jnp.sum(acc_f[3])
    small = jnp.concatenate([
        acc_i[1:2], acc_i[0:1], acc_b[4:5], acc_b[1:2], acc_b[0:1], acc_f[2:3],
        acc_b[2:3], acc_b[3:4], acc_f[0:1], acc_f[1:2],
        jnp.concatenate([dvec[0:1], dvec[1:2]], axis=1),
        jnp.concatenate([dvec[2:3, :GLA_KW], jnp.full((1, 128), loss_part, F32),
                         jnp.zeros((1, D_MODEL - GLA_KW - 128), F32)], axis=1),
        jnp.zeros((4, D_MODEL), F32)], axis=0)
    sg = _exchange_start([small, dgw[:GATE_RANK]], True, "small_start")

    r_wo, r_w1, r_w2 = _exchange_wait(*gx[:4], sg[4], False, "gradx_wait")
    r_wi, = _exchange_wait(*gi[:4], sg[4], False, "gradin_wait")
    big = [_adamw(w[0], r, m_[0], v_[0], nm) for w, r, m_, v_, nm in (
        (w_out, r_wo, m_w_out, v_w_out, "adamw_out"),
        (w_ff1, r_w1, m_w_ff1, v_w_ff1, "adamw_ff1"), (w_ff2, r_w2, m_w_ff2, v_w_ff2, "adamw_ff2"))]
    tiles = lambda a: a.T.reshape(IN_COLS * D_MODEL // V7X_LANES, V7X_LANES)
    big_in = _adamw(tiles(w_in[0]), r_wi, tiles(m_w_in[0]), tiles(v_w_in[0]), "adamw_in", row_tiles=True)
    big = [tuple(b.reshape(IN_COLS, D_MODEL).T for b in big_in)] + big
    g_big, d_big, m_big, v_big = [[b[i][None] for b in big] for i in range(4)]

    small_all, gw_all = _exchange_wait(*sg[:4], big_in[1], True, "small_wait")
    dmod_all = small_all[:, :6].reshape(N_DEV, 6 * D_MODEL)
    dmod_cols = lax.dynamic_slice(dmod_all, (0, me * ADA_COLS), (N_DEV, ADA_COLS))
    ssum, gw_sum, g_b_ada, g_w_ada = _small_reduce(small_all, gw_all, c_all, dmod_cols)
    loss = ssum[SMR_MISC, GLA_KW]
    g_ggw = lax.dynamic_slice(gw_sum, (0, me * (GLA_KW // N_DEV)), (GATE_RANK, GLA_KW // N_DEV))[None]

    small_w = [b_ada, ret_norm_w, gla_gate_b, gla_norm_w, ln1_w, ln1_b, ln2_w, ln2_b, gla_gate_w]
    small_m = [m_b_ada, m_ret_norm_w, m_gla_gate_b, m_gla_norm_w, m_ln1_w, m_ln1_b, m_ln2_w, m_ln2_b, m_gla_gate_w]
    small_v = [v_b_ada, v_ret_norm_w, v_gla_gate_b, v_gla_norm_w, v_ln1_w, v_ln1_b, v_ln2_w, v_ln2_b, v_gla_gate_w]
    res = _adamw_small(ssum, g_b_ada, g_ggw, small_w, small_m, small_v)
    small_g = list(res[:8]) + [g_ggw]
    d_small, m_small, v_small = list(res[8:17]), list(res[17:26]), list(res[26:35])

    _, d_w_ada, nm_w_ada, nv_w_ada = _adamw(w_ada[0], g_w_ada[None], m_w_ada[0], v_w_ada[0], "adamw_ada")

    def ordered(w_ada_v, small_vals, big_vals):
        b_ada_v, rnw_v, ggb_v, gnw_v, l1w_v, l1b_v, l2w_v, l2b_v, ggw_v = small_vals
        wi_v, wo_v, w1_v, w2_v = big_vals
        return [w_ada_v, b_ada_v, wi_v, rnw_v, ggw_v, ggb_v, gnw_v, wo_v, l1w_v, l1b_v, w1_v, w2_v, l2w_v, l2b_v]

    grads = ordered(g_w_ada[None], small_g, g_big)
    deltas = ordered(d_w_ada[None], d_small, d_big)
    new_m = ordered(nm_w_ada[None], m_small, m_big)
    new_v = ordered(nv_w_ada[None], v_small, v_big)
    return (loss, grad_x[None], *grads, *deltas, *new_m, *new_v)
```

```python
import numpy as np
import jax
import jax.numpy as jnp
from jax import lax
from jax.experimental import pallas as pl
from jax.experimental.pallas import tpu as pltpu

F32 = jnp.float32
BF16 = jnp.bfloat16
MESH = pl.DeviceIdType.MESH
HIGHEST = lax.Precision.HIGHEST

N_DEV = 8
D_MODEL = 1024
CHUNK = 64
RET_HEADS = 4
RET_D = 128
GLA_HEADS = 4
GLA_DK = 64
GLA_DV = 128
GLA_KW = GLA_HEADS * GLA_DK
RET_W = RET_HEADS * RET_D
GLA_VW = GLA_HEADS * GLA_DV
V7X_LANES = 128
GATE_RANK = 16
GATE_TAU = 16.0
D_FF = 4096
LN_EPS = 1e-5
ALPHA = (2.0 * 1) ** 0.25
D_IN = 3600
D_IN_PAD = 3712
ADA_COLS = 6 * D_MODEL // N_DEV
IN_COLS = D_IN // N_DEV
FF_COLS = D_FF // N_DEV
OUT_ROWS = D_MODEL // N_DEV

OFF_RQ, OFF_RK, OFF_RV, OFF_RG = 0, RET_W, 2 * RET_W, 3 * RET_W
OFF_GQ = 4 * RET_W
OFF_GK = OFF_GQ + GLA_KW
OFF_GV = OFF_GK + GLA_KW
OFF_GG = OFF_GV + GLA_VW
OFF_GLR = OFF_GG + GLA_VW

ADAM_LR, ADAM_B1, ADAM_B2, ADAM_EPS, ADAM_WD, ADAM_STEP = 0.001, 0.9, 0.999, 1e-08, 0.01, 10

V7X_VMEM_LIMIT = 62 * 1024 * 1024

ROW_TILE = 512
PROJ_TILE = 512
MIX_TILE = 512
RET_SUB = 256
GLA_SUB = 128
ADAMW_BLOCK_BYTES = 512 * 1024


def _log_gamma(h):
    return float(np.log(np.float32(1.0) - np.float32(2.0) ** np.float32(-5.0 - h)))


def _my_coords():
    return lax.axis_index("x"), lax.axis_index("y"), lax.axis_index("c")


def _flip(v, bit):
    return 1 - v if bit else v


def _peer(k):
    x, y, c = _my_coords()
    px, py, pc = _flip(x, (k >> 2) & 1), _flip(y, (k >> 1) & 1), _flip(c, k & 1)
    return (px, py, pc), 4 * px + 2 * py + pc


def _dot(a, b, dims=(((1,), (0,)), ((), ())), precision=None):
    return lax.dot_general(a, b, dims, precision=precision, preferred_element_type=F32)


NN = (((1,), (0,)), ((), ()))
NT = (((1,), (1,)), ((), ()))
TN = (((0,), (0,)), ((), ()))


def _split_bf16(v, parts):
    out = []
    for _ in range(parts):
        p = v.astype(BF16)
        out.append(p)
        v = v - p.astype(F32)
    return out


def _dot_split(a, b, dims, a_exact=False):
    if a_exact:
        ab = a.astype(BF16)
        return sum(_dot(ab, p, dims) for p in _split_bf16(b, 2))
    a_hi, a_lo = _split_bf16(a, 2)
    b_hi, b_lo = _split_bf16(b, 2)
    return _dot(a_hi, b_hi, dims) + _dot(a_hi, b_lo, dims) + _dot(a_lo, b_hi, dims)


def _sigmoid(x):
    return 1.0 / (1.0 + jnp.exp(-x))


def _ln_stats(x):
    mu = jnp.mean(x, axis=-1, keepdims=True)
    xc = x - mu
    var = jnp.mean(xc * xc, axis=-1, keepdims=True)
    rstd = lax.rsqrt(var + LN_EPS)
    return xc * rstd, rstd


def _ln_bwd(dyh, xh, rstd):
    return rstd * (dyh - jnp.mean(dyh, axis=-1, keepdims=True) - xh * jnp.mean(dyh * xh, axis=-1, keepdims=True))


def _adaln_mod(c_ext, w_ada_l, b_l, w_in_l):
    width = c_ext.shape[1]

    def body(c_ref, w_ref, b_ref, wi_ref, call_ref, mod_ref, wig_ref, s1, r1, s2, r2, gs, gr, gl):
        gather = _TwoLevelGather([wi_ref], [wig_ref], gs, gr, gl)
        gather.start()
        x, y, c = _my_coords()
        me = 4 * x + 2 * y + c
        call_ref[me] = c_ref[...]
        sends = []
        for k in range(1, N_DEV):
            peer, _ = _peer(k)
            cp = pltpu.make_async_remote_copy(c_ref, call_ref.at[me], s1.at[k - 1], r1.at[k - 1],
                                              device_id=peer, device_id_type=MESH)
            cp.start()
            sends.append(cp)
        for k in range(1, N_DEV):
            peer, pid = _peer(k)
            pltpu.make_async_remote_copy(c_ref, call_ref.at[pid], s1.at[k - 1], r1.at[k - 1],
                                         device_id=peer, device_id_type=MESH).wait_recv()
        for cp in sends:
            cp.wait_send()
        row = lax.broadcasted_iota(jnp.int32, (N_DEV, D_MODEL), 0)
        call = jnp.zeros((N_DEV, D_MODEL), F32)
        for j in range(N_DEV):
            call = jnp.where(row == j, jnp.broadcast_to(call_ref[j][:, :D_MODEL], (N_DEV, D_MODEL)), call)
        sc = call * _sigmoid(call)
        mod = _dot(sc, w_ref[...], NN, HIGHEST) + b_ref[...]
        mod_ref[me] = mod
        sends = []
        for k in range(1, N_DEV):
            peer, _ = _peer(k)
            cp = pltpu.make_async_remote_copy(mod_ref.at[me], mod_ref.at[me], s2.at[k - 1], r2.at[k - 1],
                                              device_id=peer, device_id_type=MESH)
            cp.start()
            sends.append(cp)
        for k in range(1, N_DEV):
            peer, pid = _peer(k)
            pltpu.make_async_remote_copy(mod_ref.at[pid], mod_ref.at[pid], s2.at[k - 1], r2.at[k - 1],
                                         device_id=peer, device_id_type=MESH).wait_recv()
        for cp in sends:
            cp.wait_send()
        gather.forward()
        gather.finish()

    vm = pl.BlockSpec(memory_space=pltpu.VMEM)
    hbm = pl.BlockSpec(memory_space=pl.ANY)
    return pl.pallas_call(
        body, name="adaln_mod",
        out_shape=(jax.ShapeDtypeStruct((N_DEV, 1, width), F32),
                   jax.ShapeDtypeStruct((N_DEV, N_DEV, ADA_COLS), F32),
                   jax.ShapeDtypeStruct((N_DEV, *w_in_l.shape), w_in_l.dtype)),
        in_specs=[vm, vm, vm, hbm], out_specs=(vm, vm, hbm),
        scratch_shapes=[pltpu.SemaphoreType.DMA((N_DEV - 1,))] * 4
        + [pltpu.SemaphoreType.DMA((7,)), pltpu.SemaphoreType.DMA((7,)), pltpu.SemaphoreType.DMA((1,))],
        compiler_params=pltpu.CompilerParams(vmem_limit_bytes=V7X_VMEM_LIMIT),
    )(c_ext, w_ada_l, b_l, w_in_l)


class _TwoLevelGather:
    def __init__(self, x_refs, out_refs, send_sems, recv_sems, local_sems):
        self.x_refs, self.out_refs = x_refs, out_refs
        self.send_sems, self.recv_sems, self.local_sems = send_sems, recv_sems, local_sems
        x, y, c = _my_coords()
        self.c = c
        self.me, self.sibling = (x, y, c), (x, y, 1 - c)
        self.chips = [(1 - x, y), (x, 1 - y), (1 - x, 1 - y)]

    def _copy(self, a, k, block, to, src=None):
        px, py, pc = block
        slab = self.out_refs[a].at[4 * px + 2 * py + pc]
        return pltpu.make_async_remote_copy(
            src_ref=slab if src is None else src, dst_ref=slab,
            send_sem=self.send_sems.at[7 * a + k], recv_sem=self.recv_sems.at[7 * a + k],
            device_id=to, device_id_type=MESH)

    def _mine(self, a):
        px, py, pc = self.me
        return pltpu.make_async_copy(self.x_refs[a], self.out_refs[a].at[4 * px + 2 * py + pc], self.local_sems.at[a])

    def _first(self, a):
        cps = [self._copy(a, 0, self.me, self.sibling, src=self.x_refs[a])]
        cps += [self._copy(a, 1 + j, self.me, (*chip, self.c), src=self.x_refs[a]) for j, chip in enumerate(self.chips)]
        return cps

    def _passed(self, a):
        return [self._copy(a, 4 + j, (*chip, self.c), self.sibling) for j, chip in enumerate(self.chips)]

    def start(self):
        for a in range(len(self.x_refs)):
            self._mine(a).start()
            for cp in self._first(a):
                cp.start()

    def forward(self):
        for a in range(len(self.x_refs)):
            passed = self._passed(a)
            for j, chip in enumerate(self.chips):
                self._copy(a, 1 + j, (*chip, self.c), self.me).wait_recv()
                passed[j].start()

    def finish(self):
        for a in range(len(self.x_refs)):
            self._copy(a, 0, self.sibling, self.me).wait_recv()
            for j, chip in enumerate(self.chips):
                self._copy(a, 4 + j, (*chip, 1 - self.c), self.me).wait_recv()
            for cp in self._first(a) + self._passed(a):
                cp.wait_send()
            self._mine(a).wait()


def _exchange_copy(src_refs, land_refs, send_sems, recv_sems, a, k, gather, receiving):
    x, y, c = _my_coords()
    me = 4 * x + 2 * y + c
    peer, pid = _peer(k)
    src = src_refs[a] if gather else src_refs[a].at[pid]
    dst = land_refs[a].at[pid if receiving else me]
    return pltpu.make_async_remote_copy(src, dst, send_sems.at[7 * a + k - 1], recv_sems.at[7 * a + k - 1],
                                        device_id=peer, device_id_type=MESH)


def _own_copy(src_refs, land_refs, send_sems, a, n, gather):
    x, y, c = _my_coords()
    me = 4 * x + 2 * y + c
    src = src_refs[a] if gather else src_refs[a].at[me]
    return pltpu.make_async_copy(src, land_refs[a].at[me], send_sems.at[7 * n + a])


def _exchange_start(srcs, gather, name, after=None):
    n = len(srcs)
    land_shapes = [(N_DEV, *s.shape) if gather else s.shape for s in srcs]
    n_in = n if after is None else n + 1

    def body(*refs):
        src_refs, send_sems, recv_sems, token = refs[:n], refs[n_in], refs[n_in + 1], refs[n_in + 2 + 2 * n]
        land_refs = refs[n_in + 2 + n:n_in + 2 + 2 * n]
        for a in range(n):
            _own_copy(src_refs, land_refs, send_sems, a, n, gather).start()
            for k in range(1, N_DEV):
                _exchange_copy(src_refs, land_refs, send_sems, recv_sems, a, k, gather, receiving=False).start()
        token[...] = jnp.zeros_like(token)

    hbm = pl.BlockSpec(memory_space=pltpu.HBM)
    sem = pl.BlockSpec(memory_space=pltpu.SEMAPHORE)
    res = pl.pallas_call(
        body, name=name,
        out_shape=(pltpu.SemaphoreType.DMA((8 * n,)), pltpu.SemaphoreType.DMA((7 * n,)),
                   *[pltpu.HBM(v.shape, v.dtype) for v in srcs],
                   *[pltpu.HBM(shape, v.dtype) for shape, v in zip(land_shapes, srcs)],
                   jax.ShapeDtypeStruct((8, 128), F32),
                   *([] if after is None else [jax.ShapeDtypeStruct(after.shape, after.dtype)])),
        in_specs=[hbm] * n + [pl.BlockSpec(memory_space=pl.ANY)] * (n_in - n),
        out_specs=(sem, sem, *([hbm] * (2 * n)), pl.BlockSpec(memory_space=pltpu.VMEM),
                   *([pl.BlockSpec(memory_space=pl.ANY)] * (n_in - n))),
        input_output_aliases={**{i: 2 + i for i in range(n)}, **({} if after is None else {n: 3 + 2 * n})},
        compiler_params=pltpu.CompilerParams(has_side_effects=pltpu.SideEffectType.DATAFLOW_SIDE_EFFECTING),
    )(*[pltpu.with_memory_space_constraint(v, pltpu.HBM) for v in srcs], *([] if after is None else [after]))
    return (res[0], res[1], list(res[2:2 + n]), list(res[2 + n:2 + 2 * n]), res[2 + 2 * n],
            None if after is None else res[3 + 2 * n])


def _exchange_wait(send_sems, recv_sems, srcs, lands, after, gather, name):
    n = len(srcs)

    def body(*refs):
        src_refs, land_refs, s_sems, r_sems = refs[:n], refs[n:2 * n], refs[2 * n], refs[2 * n + 1]
        for a in range(n):
            _own_copy(src_refs, land_refs, s_sems, a, n, gather).wait()
            for k in range(1, N_DEV):
                _exchange_copy(src_refs, land_refs, s_sems, r_sems, a, k, gather, receiving=False).wait_send()
                _exchange_copy(src_refs, land_refs, s_sems, r_sems, a, k, gather, receiving=True).wait_recv()

    hbm = pl.BlockSpec(memory_space=pltpu.HBM)
    sem = pl.BlockSpec(memory_space=pltpu.SEMAPHORE)
    res = pl.pallas_call(
        body, name=name,
        out_shape=tuple(pltpu.HBM(v.shape, v.dtype) for v in srcs + lands),
        in_specs=[hbm] * (2 * n) + [sem, sem, pl.BlockSpec(memory_space=pl.ANY)],
        out_specs=tuple([hbm] * (2 * n)),
        input_output_aliases={i: i for i in range(2 * n)},
        compiler_params=pltpu.CompilerParams(has_side_effects=pltpu.SideEffectType.DATAFLOW_SIDE_EFFECTING),
    )(*srcs, *lands, send_sems, recv_sems, after)
    return list(res[n:])


def _load_resident(step_is_first, pairs, sem):
    @pl.when(step_is_first)
    def _():
        copies = [pltpu.make_async_copy(src, dst, sem.at[i]) for i, (src, dst) in enumerate(pairs)]
        for cp in copies:
            cp.start()
        for cp in copies:
            cp.wait()


def _load_w_in_t(step_is_first, w_hbm, w_vmem, sem):
    @pl.when(step_is_first)
    def _():
        w_vmem[D_IN:, :] = jnp.zeros((D_IN_PAD - D_IN, D_MODEL), BF16)
    _load_resident(step_is_first, [(w_hbm, w_vmem.at[pl.ds(0, D_IN)])], sem)


def _inproj_fwd(x2d, sc1p, sh1, w_in_t, after):
    t = x2d.shape[0]
    tm = min(PROJ_TILE, t)

    def body(x_ref, sc_ref, sh_ref, w_hbm, after_ref, proj_ref, u_ref, w_vmem, sem):
        _load_w_in_t(pl.program_id(0) == 0, w_hbm, w_vmem, sem)
        xh, _ = _ln_stats(x_ref[...])
        ub = (xh * sc_ref[...] + sh_ref[...]).astype(BF16)
        u_ref[...] = ub
        proj_ref[...] = _dot(ub, w_vmem[...], NT)

    row = lambda i: (i, 0)
    fix = lambda i: (0, 0)
    return pl.pallas_call(
        body, name="inproj_fwd", grid=(t // tm,),
        in_specs=[pl.BlockSpec((tm, D_MODEL), row), pl.BlockSpec((1, D_MODEL), fix), pl.BlockSpec((1, D_MODEL), fix),
                  pl.BlockSpec(memory_space=pl.ANY), pl.BlockSpec(memory_space=pl.ANY)],
        out_specs=(pl.BlockSpec((tm, D_IN_PAD), row), pl.BlockSpec((tm, D_MODEL), row)),
        out_shape=(jax.ShapeDtypeStruct((t, D_IN_PAD), F32), jax.ShapeDtypeStruct((t, D_MODEL), BF16)),
        scratch_shapes=[pltpu.VMEM((D_IN_PAD, D_MODEL), BF16), pltpu.SemaphoreType.DMA((1,))],
        compiler_params=pltpu.CompilerParams(dimension_semantics=("arbitrary",), vmem_limit_bytes=V7X_VMEM_LIMIT),
    )(x2d, sc1p, sh1, w_in_t, after)


CHUNK_SHIFT = CHUNK.bit_length() - 1


def _ret_tables(t, tl):
    r = lax.broadcasted_iota(jnp.int32, (tl, tl), 0)
    c = lax.broadcasted_iota(jnp.int32, (tl, tl), 1)
    allowed = jnp.right_shift(c, CHUNK_SHIFT) <= jnp.right_shift(r, CHUNK_SHIFT)
    dist = jnp.abs(r - c).astype(F32)
    rowf = lax.broadcasted_iota(jnp.int32, (tl, RET_D), 0).astype(F32)
    lgs = [_log_gamma(h) for h in range(RET_HEADS)]
    dec = jnp.stack([jnp.where(allowed, jnp.exp(lg * dist), 0.0) for lg in lgs])
    qkd = jnp.stack([jnp.exp(lg * (rowf + 1.0)) for lg in lgs] + [jnp.exp(lg * (tl - 1.0 - rowf)) for lg in lgs])
    inv = 1.0 / (10000.0 ** jnp.linspace(0.0, 1.0, RET_D // 2, dtype=F32))
    off = jnp.arange(tl, dtype=F32)[:, None] * inv[None, :]
    start = (jnp.arange(t // tl, dtype=F32) * tl)[:, None] * inv[None, :]
    co, so = jnp.cos(off), jnp.sin(off)
    rot_in = jnp.stack([jnp.concatenate([co, co], 1), jnp.concatenate([so, so], 1),
                        jnp.concatenate([-co, co], 1), jnp.concatenate([-so, so], 1)])
    cs, ss = jnp.cos(start), jnp.sin(start)
    rot_tile = jnp.concatenate([cs, cs, ss, ss], axis=1)
    rot_tile = jnp.broadcast_to(rot_tile[:, None, :], (t // tl, 8, 2 * RET_D))
    return dec, qkd, rot_in, rot_tile


def _tile_gammas(tl):
    return [float(np.exp(np.float32(_log_gamma(h)) * np.float32(tl))) for h in range(RET_HEADS)]


def _tile_rotary(rot_in_ref, rot_tile_ref, j):
    ca, sa = rot_tile_ref[j, 0:1, 0:RET_D], rot_tile_ref[j, 0:1, RET_D:2 * RET_D]
    cosv = ca * rot_in_ref[0] - sa * rot_in_ref[1]
    sinv = sa * rot_in_ref[2] + ca * rot_in_ref[3]
    return cosv, sinv


def _gla_consts(tl):
    r = lax.broadcasted_iota(jnp.int32, (tl, tl), 0)
    c = lax.broadcasted_iota(jnp.int32, (tl, tl), 1)
    ltri = (c <= r).astype(F32)
    utri = (c >= r).astype(F32)
    lane = lax.broadcasted_iota(jnp.int32, (1, GLA_KW), 1)
    hmask = [((lane >= h * GLA_DK) & (lane < (h + 1) * GLA_DK)).astype(F32) for h in range(GLA_HEADS)]
    rs = lax.broadcasted_iota(jnp.int32, (GLA_HEADS * tl, tl), 0) & (tl - 1)
    cs = lax.broadcasted_iota(jnp.int32, (GLA_HEADS * tl, tl), 1)
    lower = cs <= rs
    same = jnp.right_shift(cs, CHUNK_SHIFT) == jnp.right_shift(rs, CHUNK_SHIFT)
    upper = jnp.logical_and(jnp.logical_not(lower), same)
    return dict(ltri=ltri, utri=utri, hmask=hmask, lower=lower, upper=upper)


def _tile_rows(j, tl):
    return pl.ds(j * tl, tl) if isinstance(j, int) else pl.ds(pl.multiple_of(j * tl, tl), tl)


def _for_tiles(cps, fn):
    for j in range(cps):
        fn(j, 0)


def _rotate(v, cosv, sinv):
    return v * cosv + pltpu.roll(v, RET_D // 2, 1) * sinv


def _rotate_t(d, cosv, sinv):
    return d * cosv + pltpu.roll(d * sinv, RET_D // 2, 1)


def _stack_heads(v, hmask):
    return jnp.concatenate([v * hmask[h] for h in range(GLA_HEADS)], axis=0)


def _gla_gates(glr, gw, gb, ltri, tl):
    z = _dot_split(glr, gw, NN) + gb
    la = (jnp.minimum(z, 0.0) - jnp.log(1.0 + jnp.exp(-jnp.abs(z)))) * (1.0 / GATE_TAU)
    b = _dot_split(ltri, la, NN, a_exact=True)
    level = b[tl // 2 - 1:tl // 2, :]
    ep = jnp.exp(jnp.clip(b - level, -80.0, 80.0))
    em = jnp.exp(jnp.clip(level - b, -80.0, 80.0))
    bl = b[tl - 1:tl, :]
    return z, b, bl, ep, em


def _mixer_fwd(proj, tables, gw_pad, gb, rnw, gnw):
    t = proj.shape[0]
    tc = min(MIX_TILE, t)
    tr, tg = min(RET_SUB, tc), min(GLA_SUB, tc)
    nsteps = t // tc
    scale_r = RET_D ** -0.5
    scale_g = GLA_DK ** -0.5
    gammas = _tile_gammas(tr)

    def body(rq_ref, rk_ref, rv_ref, rg_ref, gq_ref, gk_ref, gv_ref, gg_ref, glr_ref,
             dec_ref, qkd_ref, rot_in_ref, rot_tile_ref, gw_ref, gb_ref, rnw_ref, gnw_ref,
             mix_ref, oraw_ref, qrb_ref, krb_ref, rst_ref, sst_ref, r_scr, s_scr):
        @pl.when(pl.program_id(0) == 0)
        def _():
            r_scr[...] = jnp.zeros_like(r_scr)
            s_scr[...] = jnp.zeros_like(s_scr)

        gla_k = _gla_consts(tg)

        def ret_tile(j, carry):
            rows = _tile_rows(j, tr)
            cosv, sinv = _tile_rotary(rot_in_ref, rot_tile_ref, j)
            for h in range(RET_HEADS):
                cols = slice(h * RET_D, (h + 1) * RET_D)
                qr = _rotate(rq_ref[rows, cols], cosv, sinv) * scale_r
                kr = _rotate(rk_ref[rows, cols], cosv, sinv)
                vb = rv_ref[rows, cols].astype(BF16)
                qb, kb = qr.astype(BF16), kr.astype(BF16)
                qrb_ref[rows, cols] = qb
                krb_ref[rows, cols] = kb
                p = _dot(qb, kb, NT) * dec_ref[h]
                rp = r_scr[cols, :]
                o = _dot(p.astype(BF16), vb) + _dot((qr * qkd_ref[h]).astype(BF16), rp.astype(BF16))
                rst_ref[j, cols, :] = rp
                r_scr[cols, :] = gammas[h] * rp + _dot((kr * qkd_ref[RET_HEADS + h]).astype(BF16), vb, TN)
                oraw_ref[rows, cols] = o
                oc = o - jnp.mean(o, axis=-1, keepdims=True)
                n = oc * lax.rsqrt(jnp.mean(oc * oc, axis=-1, keepdims=True) + LN_EPS)
                g = rg_ref[rows, cols]
                mix_ref[rows, cols] = (n * rnw_ref[:, cols] * (g * _sigmoid(g))).astype(BF16)
            return carry

        def gla_tile(j, carry):
            k = gla_k
            tl = tg
            rows = _tile_rows(j, tg)
            _, b, bl, ep, em = _gla_gates(glr_ref[rows, :], gw_ref[...], gb_ref[...], k["ltri"], tl)
            qs = gq_ref[rows, :] * scale_g
            kk = gk_ref[rows, :]
            x_all = _dot(_stack_heads(qs * ep, k["hmask"]).astype(BF16), (kk * em).astype(BF16), NT)
            y_all = _dot(_stack_heads(qs * em, k["hmask"]).astype(BF16), (kk * ep).astype(BF16), NT)
            a_all = jnp.where(k["lower"], x_all, jnp.where(k["upper"], y_all, 0.0)).astype(BF16)
            st = s_scr[...]
            oq = _dot(_stack_heads(qs * jnp.exp(b), k["hmask"]).astype(BF16), st.astype(BF16), NT)
            kg = kk * jnp.exp(bl - b)
            sst_ref[j] = st
            st_new = st * jnp.exp(bl)
            for h in range(GLA_HEADS):
                cols = slice(h * GLA_DV, (h + 1) * GLA_DV)
                hr = slice(h * tl, (h + 1) * tl)
                vb = gv_ref[rows, cols].astype(BF16)
                o = _dot(a_all[hr, :], vb) + oq[hr, :]
                st_new = st_new + _dot(vb, (kg * k["hmask"][h]).astype(BF16), TN)
                ocols = slice(RET_W + h * GLA_DV, RET_W + (h + 1) * GLA_DV)
                oraw_ref[rows, ocols] = o
                n = o * lax.rsqrt(jnp.mean(o * o, axis=-1, keepdims=True) + LN_EPS)
                g = gg_ref[rows, cols]
                mix_ref[rows, ocols] = (n * gnw_ref[:, cols] * (g * _sigmoid(g))).astype(BF16)
            s_scr[...] = st_new
            return carry

        _for_tiles(tc // tr, ret_tile)
        _for_tiles(tc // tg, gla_tile)

    def col(width, off):
        return pl.BlockSpec((tc, width), lambda i, o=off // width: (i, o))

    fix = lambda i: (0, 0)
    fix3 = lambda i: (0, 0, 0)
    dec, qkd, rot_in, rot_tile = tables
    in_specs = [col(RET_W, OFF_RQ), col(RET_W, OFF_RK), col(RET_W, OFF_RV), col(RET_W, OFF_RG),
                col(GLA_KW, OFF_GQ), col(GLA_KW, OFF_GK), col(GLA_VW, OFF_GV), col(GLA_VW, OFF_GG),
                col(V7X_LANES, OFF_GLR),
                pl.BlockSpec(dec.shape, fix3), pl.BlockSpec(qkd.shape, fix3), pl.BlockSpec(rot_in.shape, fix3),
                pl.BlockSpec((tc // tr, 8, 2 * RET_D), lambda i: (i, 0, 0)),
                pl.BlockSpec((V7X_LANES, GLA_KW), fix), pl.BlockSpec((1, GLA_KW), fix),
                pl.BlockSpec((1, RET_W), fix), pl.BlockSpec((1, GLA_VW), fix)]
    half = pl.BlockSpec((tc, RET_W), lambda i: (i, 0))
    out_specs = (pl.BlockSpec((tc, D_MODEL), lambda i: (i, 0)), pl.BlockSpec((tc, D_MODEL), lambda i: (i, 0)),
                 half, half,
                 pl.BlockSpec((tc // tr, RET_W, RET_D), lambda i: (i, 0, 0)),
                 pl.BlockSpec((tc // tg, GLA_DV, GLA_KW), lambda i: (i, 0, 0)))
    out_shape = (jax.ShapeDtypeStruct((t, D_MODEL), BF16), jax.ShapeDtypeStruct((t, D_MODEL), F32),
                 jax.ShapeDtypeStruct((t, RET_W), BF16), jax.ShapeDtypeStruct((t, RET_W), BF16),
                 jax.ShapeDtypeStruct((t // tr, RET_W, RET_D), F32),
                 jax.ShapeDtypeStruct((t // tg, GLA_DV, GLA_KW), F32))
    return pl.pallas_call(
        body, name="mixer_fwd", grid=(nsteps,), in_specs=in_specs, out_specs=out_specs, out_shape=out_shape,
        scratch_shapes=[pltpu.VMEM((RET_W, RET_D), F32), pltpu.VMEM((GLA_DV, GLA_KW), F32)],
        compiler_params=pltpu.CompilerParams(dimension_semantics=("arbitrary",), vmem_limit_bytes=V7X_VMEM_LIMIT),
    )(*([proj] * 9), dec, qkd, rot_in, rot_tile, gw_pad, gb, rnw, gnw)


def _mid_fwd(mixed, x2d, target, vecs, w_out_b, w1_b, w2_b):
    t = x2d.shape[0]
    tm = min(ROW_TILE, t)

    def body(mix_ref, x_ref, tgt_ref, v_ref, wo_hbm, w1_hbm, w2_hbm,
             m_ref, x1n_ref, rstd_ref, u2_ref, a_ref, df_ref, dh2_ref, acc_ref, wo, w1, w2, sem):
        first = pl.program_id(0) == 0
        _load_resident(first, [(wo_hbm, wo), (w1_hbm, w1), (w2_hbm, w2)], sem)

        @pl.when(first)
        def _():
            acc_ref[...] = jnp.zeros_like(acc_ref)

        gate1, sc2p, sh2, gate2 = v_ref[0:1, :], v_ref[1:2, :], v_ref[2:3, :], v_ref[3:4, :]
        l1w, l1b, l2w, l2b = v_ref[4:5, :], v_ref[5:6, :], v_ref[6:7, :], v_ref[7:8, :]
        m = _dot(mix_ref[...], wo[...])
        m_ref[...] = m.astype(BF16)
        x1n, rstd1 = _ln_stats(ALPHA * x_ref[...] + gate1 * m)
        x1n_ref[...] = x1n
        rstd_ref[...] = rstd1
        x1 = x1n * l1w + l1b
        xh1, _ = _ln_stats(x1)
        u2 = (xh1 * sc2p + sh2).astype(BF16)
        u2_ref[...] = u2
        f = jnp.zeros((tm, D_MODEL), F32)
        for j in range(N_DEV):
            cols = slice(j * FF_COLS, (j + 1) * FF_COLS)
            a = _dot(u2, w1[j])
            a_ref[:, cols] = a.astype(BF16)
            r = jnp.maximum(a, 0.0)
            f = f + _dot((r * r).astype(BF16), w2[cols, :])
        yh, rstd2 = _ln_stats(ALPHA * x1 + gate2 * f)
        e = yh * l2w + l2b - tgt_ref[...]
        dy = e * (1.0 / D_MODEL)
        dh2 = _ln_bwd(dy * l2w, yh, rstd2)
        dh2_ref[...] = dh2
        df_ref[...] = (dh2 * gate2).astype(BF16)
        acc_ref[0:1, :] += jnp.sum(dy * yh, axis=0, keepdims=True)
        acc_ref[1:2, :] += jnp.sum(dy, axis=0, keepdims=True)
        acc_ref[2:3, :] += jnp.sum(dh2 * f, axis=0, keepdims=True)
        acc_ref[3:4, :] += jnp.sum(e * e, axis=0, keepdims=True) * (0.5 / D_MODEL)

    row = lambda i: (i, 0)
    fix = lambda i: (0, 0)
    hbm = pl.BlockSpec(memory_space=pl.ANY)
    return pl.pallas_call(
        body, name="mid_fwd", grid=(t // tm,),
        in_specs=[pl.BlockSpec((tm, D_MODEL), row), pl.BlockSpec((tm, D_MODEL), row), pl.BlockSpec((tm, D_MODEL), row),
                  pl.BlockSpec((8, D_MODEL), fix), hbm, hbm, hbm],
        out_specs=(pl.BlockSpec((tm, D_MODEL), row), pl.BlockSpec((tm, D_MODEL), row), pl.BlockSpec((tm, 1), row),
                   pl.BlockSpec((tm, D_MODEL), row), pl.BlockSpec((tm, D_FF), row), pl.BlockSpec((tm, D_MODEL), row),
                   pl.BlockSpec((tm, D_MODEL), row), pl.BlockSpec((8, D_MODEL), fix)),
        out_shape=(jax.ShapeDtypeStruct((t, D_MODEL), BF16), jax.ShapeDtypeStruct((t, D_MODEL), F32),
                   jax.ShapeDtypeStruct((t, 1), F32), jax.ShapeDtypeStruct((t, D_MODEL), BF16),
                   jax.ShapeDtypeStruct((t, D_FF), BF16), jax.ShapeDtypeStruct((t, D_MODEL), BF16),
                   jax.ShapeDtypeStruct((t, D_MODEL), F32), jax.ShapeDtypeStruct((8, D_MODEL), F32)),
        scratch_shapes=[pltpu.VMEM((D_MODEL, D_MODEL), BF16), pltpu.VMEM((N_DEV, D_MODEL, FF_COLS), BF16),
                        pltpu.VMEM((D_FF, D_MODEL), BF16), pltpu.SemaphoreType.DMA((3,))],
        compiler_params=pltpu.CompilerParams(dimension_semantics=("arbitrary",), vmem_limit_bytes=V7X_VMEM_LIMIT),
    )(mixed, x2d, target, vecs, w_out_b, w1_b, w2_b)


def _ffn_bwd(df, a, dh2, x1n, rstd1, m, vecs, w_out_b, w1_b, w2_b):
    t = x1n.shape[0]
    tm = min(ROW_TILE, t)

    def body(df_ref, a_ref, dh2_ref, x1n_ref, rstd_ref, m_ref, v_ref, wo_hbm, w1_hbm, w2_hbm,
             da_ref, dm_ref, dmix_ref, dxa_ref, acc_ref, wo, w1, w2, sem):
        first = pl.program_id(0) == 0
        _load_resident(first, [(wo_hbm, wo), (w1_hbm, w1), (w2_hbm, w2)], sem)

        @pl.when(first)
        def _():
            acc_ref[...] = jnp.zeros_like(acc_ref)

        gate1, sc2p, l1w, l1b = v_ref[0:1, :], v_ref[1:2, :], v_ref[2:3, :], v_ref[3:4, :]
        df = df_ref[...]
        du2 = jnp.zeros((tm, D_MODEL), F32)
        for j in range(N_DEV):
            cols = slice(j * FF_COLS, (j + 1) * FF_COLS)
            dr2 = _dot(df, w2[cols, :], NT)
            da = (dr2 * (2.0 * jnp.maximum(a_ref[:, cols].astype(F32), 0.0))).astype(BF16)
            da_ref[:, cols] = da
            du2 = du2 + _dot(da, w1[j], NT)
        x1n = x1n_ref[...]
        xh1, rstd0 = _ln_stats(x1n * l1w + l1b)
        dx1 = ALPHA * dh2_ref[...] + _ln_bwd(du2 * sc2p, xh1, rstd0)
        dh1 = _ln_bwd(dx1 * l1w, x1n, rstd_ref[...])
        dxa_ref[...] = ALPHA * dh1
        dm = (dh1 * gate1).astype(BF16)
        dm_ref[...] = dm
        dmix_ref[...] = _dot(dm, wo[...], NT)
        acc_ref[0:1, :] += jnp.sum(du2 * xh1, axis=0, keepdims=True)
        acc_ref[1:2, :] += jnp.sum(du2, axis=0, keepdims=True)
        acc_ref[2:3, :] += jnp.sum(dx1 * x1n, axis=0, keepdims=True)
        acc_ref[3:4, :] += jnp.sum(dx1, axis=0, keepdims=True)
        acc_ref[4:5, :] += jnp.sum(dh1 * m_ref[...].astype(F32), axis=0, keepdims=True)

    row = lambda i: (i, 0)
    fix = lambda i: (0, 0)
    hbm = pl.BlockSpec(memory_space=pl.ANY)
    return pl.pallas_call(
        body, name="ffn_bwd", grid=(t // tm,),
        in_specs=[pl.BlockSpec((tm, D_MODEL), row), pl.BlockSpec((tm, D_FF), row), pl.BlockSpec((tm, D_MODEL), row),
                  pl.BlockSpec((tm, D_MODEL), row), pl.BlockSpec((tm, 1), row), pl.BlockSpec((tm, D_MODEL), row),
                  pl.BlockSpec((8, D_MODEL), fix), hbm, hbm, hbm],
        out_specs=(pl.BlockSpec((tm, D_FF), row), pl.BlockSpec((tm, D_MODEL), row), pl.BlockSpec((tm, D_MODEL), row),
                   pl.BlockSpec((tm, D_MODEL), row), pl.BlockSpec((8, D_MODEL), fix)),
        out_shape=(jax.ShapeDtypeStruct((t, D_FF), BF16), jax.ShapeDtypeStruct((t, D_MODEL), BF16),
                   jax.ShapeDtypeStruct((t, D_MODEL), F32), jax.ShapeDtypeStruct((t, D_MODEL), F32),
                   jax.ShapeDtypeStruct((8, D_MODEL), F32)),
        scratch_shapes=[pltpu.VMEM((D_MODEL, D_MODEL), BF16), pltpu.VMEM((N_DEV, D_MODEL, FF_COLS), BF16),
                        pltpu.VMEM((D_FF, D_MODEL), BF16), pltpu.SemaphoreType.DMA((3,))],
        compiler_params=pltpu.CompilerParams(dimension_semantics=("arbitrary",), vmem_limit_bytes=V7X_VMEM_LIMIT),
    )(df, a, dh2, x1n, rstd1, m, vecs, w_out_b, w1_b, w2_b)


def _matmul_tn(lhs, rhs, tmm, tn, tk, name, relu_sq=False, col_slab=None, out_rows=None):
    t, mm = lhs.shape
    assert out_rows is None or (col_slab is None and tmm == mm)
    nn = rhs.shape[1]
    tk = min(tk, t)
    nk = t // tk

    def body(l_ref, r_ref, o_ref, acc):
        kk = pl.program_id(2)

        @pl.when(kk == 0)
        def _():
            acc[...] = jnp.zeros_like(acc)

        l = l_ref[...]
        if relu_sq:
            lf = jnp.maximum(l.astype(F32), 0.0)
            l = (lf * lf).astype(BF16)
        acc[...] += _dot(l, r_ref[...], TN)

        @pl.when(kk == nk - 1)
        def _():
            if out_rows is not None:
                for s in range(N_DEV):
                    o_ref[s] = acc[s * out_rows:(s + 1) * out_rows, :].astype(o_ref.dtype)
            elif col_slab is None:
                o_ref[...] = acc[...].astype(o_ref.dtype)
            else:
                for s in range(tn // col_slab):
                    o_ref[s] = acc[:, s * col_slab:(s + 1) * col_slab].astype(o_ref.dtype)

    if out_rows is not None:
        out_spec = pl.BlockSpec((N_DEV, out_rows, tn), lambda i, j, k: (0, 0, j))
        out_shape = jax.ShapeDtypeStruct((N_DEV, out_rows, nn), BF16)
    elif col_slab is None:
        out_spec = pl.BlockSpec((tmm, tn), lambda i, j, k: (i, j))
        out_shape = jax.ShapeDtypeStruct((mm, nn), BF16)
    else:
        out_spec = pl.BlockSpec((tn // col_slab, tmm, col_slab), lambda i, j, k: (j, i, 0))
        out_shape = jax.ShapeDtypeStruct((nn // col_slab, mm, col_slab), BF16)
    return pl.pallas_call(
        body, name=name, grid=(mm // tmm, nn // tn, nk),
        in_specs=[pl.BlockSpec((tk, tmm), lambda i, j, k: (k, i)), pl.BlockSpec((tk, tn), lambda i, j, k: (k, j))],
        out_specs=out_spec,
        out_shape=out_shape,
        scratch_shapes=[pltpu.VMEM((tmm, tn), F32)],
        compiler_params=pltpu.CompilerParams(dimension_semantics=("arbitrary", "arbitrary", "arbitrary"),
                                             vmem_limit_bytes=V7X_VMEM_LIMIT),
    )(lhs, rhs)


def _mixer_bwd(dmix, proj, qrb, krb, oraw, tables, rst, sst, gw_pad, gb, rnw, gnw, after):
    t = proj.shape[0]
    tc = min(MIX_TILE, t)
    tr, tg = min(RET_SUB, tc), min(GLA_SUB, tc)
    nsteps = t // tc
    scale_r = RET_D ** -0.5
    scale_g = GLA_DK ** -0.5
    gammas = _tile_gammas(tr)

    def body(dmix_ref, qrb_ref, krb_ref, rv_ref, rg_ref, gq_ref, gk_ref, gv_ref, gg_ref, glr_ref, oraw_ref,
             dec_ref, qkd_ref, rot_in_ref, rot_tile_ref, rst_ref, sst_ref, gw_ref, gb_ref, rnw_ref, gnw_ref, after_ref,
             dproj_ref, dgw_ref, dvec_ref, dr_scr, ds_scr):
        @pl.when(pl.program_id(0) == 0)
        def _():
            dr_scr[...] = jnp.zeros_like(dr_scr)
            ds_scr[...] = jnp.zeros_like(ds_scr)
            dgw_ref[...] = jnp.zeros_like(dgw_ref)
            dvec_ref[...] = jnp.zeros_like(dvec_ref)

        gla_k = _gla_consts(tg)
        last_row = lax.broadcasted_iota(jnp.int32, (tg, GLA_KW), 0) == tg - 1

        def ret_tile(jj, carry):
            j = tc // tr - 1 - jj
            rows = _tile_rows(j, tr)
            cosv, sinv = _tile_rotary(rot_in_ref, rot_tile_ref, j)
            for h in range(RET_HEADS):
                cols = slice(h * RET_D, (h + 1) * RET_D)
                o = oraw_ref[rows, cols]
                g = rg_ref[rows, cols]
                w = rnw_ref[:, cols]
                dout = dmix_ref[rows, cols]
                oc = o - jnp.mean(o, axis=-1, keepdims=True)
                inv = lax.rsqrt(jnp.mean(oc * oc, axis=-1, keepdims=True) + LN_EPS)
                n = oc * inv
                sg = _sigmoid(g)
                sil = g * sg
                dn = dout * w * sil
                dvec_ref[0:1, cols] += jnp.sum(dout * n * sil, axis=0, keepdims=True)
                dproj_ref[rows, OFF_RG + h * RET_D:OFF_RG + (h + 1) * RET_D] = (
                    dout * n * w * (sg * (1.0 + g * (1.0 - sg)))).astype(BF16)
                doc = inv * (dn - n * jnp.mean(dn * n, axis=-1, keepdims=True))
                do = doc - jnp.mean(doc, axis=-1, keepdims=True)

                qb, kb = qrb_ref[rows, cols], krb_ref[rows, cols]
                qr, kr = qb.astype(F32), kb.astype(F32)
                vb = rv_ref[rows, cols].astype(BF16)
                dob = do.astype(BF16)
                qd, kd = qkd_ref[h], qkd_ref[RET_HEADS + h]
                p = _dot(qb, kb, NT) * dec_ref[h]
                rp = rst_ref[j, cols, :].astype(BF16)
                dr = dr_scr[cols, :]
                drb = dr.astype(BF16)
                dpb = (_dot(dob, vb, NT) * dec_ref[h]).astype(BF16)
                dqr = _dot(dpb, kb) + _dot(dob, rp, NT) * qd
                dkr = _dot(dpb, qb, TN) + _dot(vb, drb, NT) * kd
                dv = _dot(p.astype(BF16), dob, TN) + _dot((kr * kd).astype(BF16), drb)
                dr_scr[cols, :] = gammas[h] * dr + _dot((qr * qd).astype(BF16), dob, TN)
                dproj_ref[rows, OFF_RQ + h * RET_D:OFF_RQ + (h + 1) * RET_D] = (
                    _rotate_t(dqr, cosv, sinv) * scale_r).astype(BF16)
                dproj_ref[rows, OFF_RK + h * RET_D:OFF_RK + (h + 1) * RET_D] = _rotate_t(dkr, cosv, sinv).astype(BF16)
                dproj_ref[rows, OFF_RV + h * RET_D:OFF_RV + (h + 1) * RET_D] = dv.astype(BF16)
            return carry

        def gla_tile(jj, carry):
            k = gla_k
            tl = tg
            j = tc // tg - 1 - jj
            rows = _tile_rows(j, tg)
            glr = glr_ref[rows, :]
            z, b, bl, ep, em = _gla_gates(glr, gw_ref[...], gb_ref[...], k["ltri"], tl)
            qs = gq_ref[rows, :] * scale_g
            kk = gk_ref[rows, :]
            eb = jnp.exp(b)
            ekb = jnp.exp(bl - b)
            ebl = jnp.exp(bl)
            ql, qu, kl, ku = qs * ep, qs * em, kk * em, kk * ep
            qg, kg = qs * eb, kk * ekb
            qlm = _stack_heads(ql, k["hmask"]).astype(BF16)
            qum = _stack_heads(qu, k["hmask"]).astype(BF16)
            klb, kub = kl.astype(BF16), ku.astype(BF16)
            a_all = jnp.where(k["lower"], _dot(qlm, klb, NT),
                              jnp.where(k["upper"], _dot(qum, kub, NT), 0.0)).astype(BF16)
            st = sst_ref[j]
            stb = st.astype(BF16)
            ds = ds_scr[...]
            dsb = ds.astype(BF16)
            ds_new = ds * ebl
            da_parts = []
            dqg = jnp.zeros((tl, GLA_KW), F32)
            dkg = jnp.zeros((tl, GLA_KW), F32)
            for h in range(GLA_HEADS):
                cols = slice(h * GLA_DV, (h + 1) * GLA_DV)
                hr = slice(h * tl, (h + 1) * tl)
                ocols = slice(RET_W + h * GLA_DV, RET_W + (h + 1) * GLA_DV)
                o = oraw_ref[rows, ocols]
                g = gg_ref[rows, cols]
                w = gnw_ref[:, cols]
                dout = dmix_ref[rows, ocols]
                inv = lax.rsqrt(jnp.mean(o * o, axis=-1, keepdims=True) + LN_EPS)
                n = o * inv
                sg = _sigmoid(g)
                sil = g * sg
                dn = dout * w * sil
                dvec_ref[1:2, cols] += jnp.sum(dout * n * sil, axis=0, keepdims=True)
                dproj_ref[rows, OFF_GG + h * GLA_DV:OFF_GG + (h + 1) * GLA_DV] = (
                    dout * n * w * (sg * (1.0 + g * (1.0 - sg)))).astype(BF16)
                dob = (inv * (dn - n * jnp.mean(dn * n, axis=-1, keepdims=True))).astype(BF16)
                vb = gv_ref[rows, cols].astype(BF16)
                mh = k["hmask"][h]
                da_parts.append(_dot(dob, vb, NT))
                dv = _dot(a_all[hr, :], dob, TN) + _dot((kg * mh).astype(BF16), dsb, NT)
                dproj_ref[rows, OFF_GV + h * GLA_DV:OFF_GV + (h + 1) * GLA_DV] = dv.astype(BF16)
                dkg = dkg + mh * _dot(vb, dsb)
                dqg = dqg + mh * _dot(dob, stb)
                ds_new = ds_new + _dot(dob, (qg * mh).astype(BF16), TN)
            da_all = jnp.concatenate(da_parts, axis=0)
            dal = jnp.where(k["lower"], da_all, 0.0).astype(BF16)
            dau = jnp.where(k["upper"], da_all, 0.0).astype(BF16)
            dqlm = _dot(dal, klb)
            dqum = _dot(dau, kub)
            dql = jnp.zeros((tl, GLA_KW), F32)
            dqu = jnp.zeros((tl, GLA_KW), F32)
            for h in range(GLA_HEADS):
                hr = slice(h * tl, (h + 1) * tl)
                dql = dql + k["hmask"][h] * dqlm[hr, :]
                dqu = dqu + k["hmask"][h] * dqum[hr, :]
            dkl = _dot(dal, qlm, TN)
            dku = _dot(dau, qum, TN)
            dbl = (jnp.sum(dkg * kg, axis=0, keepdims=True)
                   + jnp.sum(ds * st, axis=0, keepdims=True) * ebl)
            ds_scr[...] = ds_new
            dqs = dql * ep + dqu * em + dqg * eb
            dk = dkl * em + dku * ep + dkg * ekb
            db = dql * ql - dkl * kl - dqu * qu + dku * ku + dqg * qg - dkg * kg
            db = db + jnp.where(last_row, dbl, 0.0)
            dla = _dot_split(k["utri"], db, NN, a_exact=True)
            dz = dla * (1.0 / GATE_TAU) * _sigmoid(-z)
            dvec_ref[2:3, 0:GLA_KW] += jnp.sum(dz, axis=0, keepdims=True)
            dgw_ref[...] += _dot_split(glr, dz, TN)
            dproj_ref[rows, OFF_GLR:D_IN_PAD] = _dot(dz.astype(BF16), gw_ref[...].astype(BF16), NT).astype(BF16)
            dproj_ref[rows, OFF_GQ:OFF_GQ + GLA_KW] = (dqs * scale_g).astype(BF16)
            dproj_ref[rows, OFF_GK:OFF_GK + GLA_KW] = dk.astype(BF16)
            return carry

        _for_tiles(tc // tr, ret_tile)
        _for_tiles(tc // tg, gla_tile)

    rev = lambda i: (nsteps - 1 - i, 0)

    def col(width, off):
        return pl.BlockSpec((tc, width), lambda i, o=off // width: (nsteps - 1 - i, o))

    fix = lambda i: (0, 0)
    fix3 = lambda i: (0, 0, 0)
    dec, qkd, rot_in, rot_tile = tables
    half = pl.BlockSpec((tc, RET_W), rev)
    in_specs = [pl.BlockSpec((tc, D_MODEL), rev), half, half, col(RET_W, OFF_RV), col(RET_W, OFF_RG),
                col(GLA_KW, OFF_GQ), col(GLA_KW, OFF_GK), col(GLA_VW, OFF_GV), col(GLA_VW, OFF_GG),
                col(V7X_LANES, OFF_GLR),
                pl.BlockSpec((tc, D_MODEL), rev),
                pl.BlockSpec(dec.shape, fix3), pl.BlockSpec(qkd.shape, fix3), pl.BlockSpec(rot_in.shape, fix3),
                pl.BlockSpec((tc // tr, 8, 2 * RET_D), lambda i: (nsteps - 1 - i, 0, 0)),
                pl.BlockSpec((tc // tr, RET_W, RET_D), lambda i: (nsteps - 1 - i, 0, 0)),
                pl.BlockSpec((tc // tg, GLA_DV, GLA_KW), lambda i: (nsteps - 1 - i, 0, 0)),
                pl.BlockSpec((V7X_LANES, GLA_KW), fix), pl.BlockSpec((1, GLA_KW), fix),
                pl.BlockSpec((1, RET_W), fix), pl.BlockSpec((1, GLA_VW), fix), pl.BlockSpec(memory_space=pl.ANY)]
    out_specs = (pl.BlockSpec((tc, D_IN_PAD), rev), pl.BlockSpec((V7X_LANES, GLA_KW), fix),
                 pl.BlockSpec((8, RET_W), fix))
    out_shape = (jax.ShapeDtypeStruct((t, D_IN_PAD), BF16), jax.ShapeDtypeStruct((V7X_LANES, GLA_KW), F32),
                 jax.ShapeDtypeStruct((8, RET_W), F32))
    return pl.pallas_call(
        body, name="mixer_bwd", grid=(nsteps,), in_specs=in_specs, out_specs=out_specs, out_shape=out_shape,
        scratch_shapes=[pltpu.VMEM((RET_W, RET_D), F32), pltpu.VMEM((GLA_DV, GLA_KW), F32)],
        compiler_params=pltpu.CompilerParams(dimension_semantics=("arbitrary",), vmem_limit_bytes=V7X_VMEM_LIMIT),
    )(dmix, qrb, krb, *([proj] * 7), oraw, dec, qkd, rot_in, rot_tile, rst, sst, gw_pad, gb, rnw, gnw, after)


def _inproj_bwd(dproj, x2d, dxa, sc1p, w_in_t, after):
    t = x2d.shape[0]
    tm = min(2 * PROJ_TILE, t)

    def body(dp_ref, x_ref, dxa_ref, sc_ref, w_hbm, after_ref, gx_ref, acc_ref, w_vmem, sem):
        first = pl.program_id(0) == 0
        _load_w_in_t(first, w_hbm, w_vmem, sem)

        @pl.when(first)
        def _():
            acc_ref[...] = jnp.zeros_like(acc_ref)

        du = _dot(dp_ref[...], w_vmem[...])
        xh, rstd = _ln_stats(x_ref[...])
        gx_ref[...] = dxa_ref[...] + _ln_bwd(du * sc_ref[...], xh, rstd)
        acc_ref[0:1, :] += jnp.sum(du * xh, axis=0, keepdims=True)
        acc_ref[1:2, :] += jnp.sum(du, axis=0, keepdims=True)

    row = lambda i: (i, 0)
    fix = lambda i: (0, 0)
    return pl.pallas_call(
        body, name="inproj_bwd", grid=(t // tm,),
        in_specs=[pl.BlockSpec((tm, D_IN_PAD), row), pl.BlockSpec((tm, D_MODEL), row), pl.BlockSpec((tm, D_MODEL), row),
                  pl.BlockSpec((1, D_MODEL), fix), pl.BlockSpec(memory_space=pl.ANY), pl.BlockSpec(memory_space=pl.ANY)],
        out_specs=(pl.BlockSpec((tm, D_MODEL), row), pl.BlockSpec((8, D_MODEL), fix)),
        out_shape=(jax.ShapeDtypeStruct((t, D_MODEL), F32), jax.ShapeDtypeStruct((8, D_MODEL), F32)),
        scratch_shapes=[pltpu.VMEM((D_IN_PAD, D_MODEL), BF16), pltpu.SemaphoreType.DMA((1,))],
        compiler_params=pltpu.CompilerParams(dimension_semantics=("arbitrary",), vmem_limit_bytes=V7X_VMEM_LIMIT),
    )(dproj, x2d, dxa, sc1p, w_in_t, after)


def _adam_math(w, g, m, v):
    m = ADAM_B1 * m + (1.0 - ADAM_B1) * g
    v = ADAM_B2 * v + (1.0 - ADAM_B2) * (g * g)
    m_hat = m / (1.0 - ADAM_B1 ** ADAM_STEP)
    v_hat = v / (1.0 - ADAM_B2 ** ADAM_STEP)
    delta = -ADAM_LR * (m_hat / (jnp.sqrt(v_hat) + ADAM_EPS) + ADAM_WD * w)
    return delta, m, v


def _adamw(w, gparts, m, v, name, row_tiles=False):
    nparts, rows, cols = gparts.shape
    tr = rows
    for cand in (512, 256, 128, 64, 32, 16, 8):
        if rows % cand == 0 and (cand * cols * 4 <= ADAMW_BLOCK_BYTES or rows * cols * 4 <= ADAMW_BLOCK_BYTES):
            tr = cand
            break
    nsteps = rows // tr

    def body(w_ref, g_ref, m_ref, v_ref, go_ref, d_ref, mo_ref, vo_ref):
        g = g_ref[0].astype(F32)
        for p in range(1, nparts):
            g = g + g_ref[p].astype(F32)
        if row_tiles:
            g = g.reshape(tr, cols // V7X_LANES, V7X_LANES).reshape(tr * cols // V7X_LANES, V7X_LANES)
        delta, mn, vn = _adam_math(w_ref[...], g, m_ref[...], v_ref[...])
        go_ref[...] = g
        d_ref[...] = delta
        mo_ref[...] = mn
        vo_ref[...] = vn

    blk = pl.BlockSpec((tr, cols), lambda i: (i, 0))
    shp = jax.ShapeDtypeStruct((rows, cols), F32)
    if row_tiles:
        blk = pl.BlockSpec((tr * cols // V7X_LANES, V7X_LANES), lambda i: (i, 0))
        shp = jax.ShapeDtypeStruct((rows * cols // V7X_LANES, V7X_LANES), F32)
    if nsteps >= 3:
        deep = pl.Buffered(3)
        in_specs = [pl.BlockSpec((tr, cols), lambda i: (i, 0), pipeline_mode=deep),
                    pl.BlockSpec((nparts, tr, cols), lambda i: (0, i, 0), pipeline_mode=deep),
                    pl.BlockSpec((tr, cols), lambda i: (i, 0), pipeline_mode=deep),
                    pl.BlockSpec((tr, cols), lambda i: (i, 0), pipeline_mode=deep)]

        def streamed(*hbm_refs):
            pltpu.emit_pipeline(body, grid=(nsteps,), in_specs=in_specs, out_specs=[blk, blk, blk, blk])(*hbm_refs)

        hbm = pl.BlockSpec(memory_space=pl.ANY)
        return pl.pallas_call(
            streamed, name=name,
            in_specs=[hbm] * 4, out_specs=(hbm, hbm, hbm, hbm), out_shape=(shp, shp, shp, shp),
            compiler_params=pltpu.CompilerParams(vmem_limit_bytes=V7X_VMEM_LIMIT),
        )(w, gparts, m, v)
    return pl.pallas_call(
        body, name=name, grid=(nsteps,),
        in_specs=[blk, pl.BlockSpec((nparts, tr, cols), lambda i: (0, i, 0)), blk, blk],
        out_specs=(blk, blk, blk, blk), out_shape=(shp, shp, shp, shp),
        compiler_params=pltpu.CompilerParams(dimension_semantics=("arbitrary",), vmem_limit_bytes=V7X_VMEM_LIMIT),
    )(w, gparts, m, v)


def _small_reduce(gathered, gathered_gw, c_all, dmod_cols):
    def body(g_ref, gw_ref, c_ref, dm_ref, sum_ref, gwsum_ref, gb_ref, gwa_ref):
        s = g_ref[0]
        sw = gw_ref[0]
        for p in range(1, N_DEV):
            s = s + g_ref[p]
            sw = sw + gw_ref[p]
        sum_ref[...] = s
        gwsum_ref[...] = sw
        for i in range(6):
            gb_ref[:, i * D_MODEL:(i + 1) * D_MODEL] = s[i:i + 1, :]
        cc = c_ref[...]
        gwa_ref[...] = _dot(cc * _sigmoid(cc), dm_ref[...], TN, HIGHEST)

    vm = pl.BlockSpec(memory_space=pltpu.VMEM)
    return pl.pallas_call(
        body, name="small_reduce",
        out_shape=(jax.ShapeDtypeStruct(gathered.shape[1:], F32), jax.ShapeDtypeStruct(gathered_gw.shape[1:], F32),
                   jax.ShapeDtypeStruct((1, 6 * D_MODEL), F32), jax.ShapeDtypeStruct((D_MODEL, ADA_COLS), F32)),
        in_specs=[vm] * 4, out_specs=(vm, vm, vm, vm),
        compiler_params=pltpu.CompilerParams(vmem_limit_bytes=V7X_VMEM_LIMIT),
    )(gathered, gathered_gw, c_all, dmod_cols)


SMR_LN1W, SMR_LN1B, SMR_LN2W, SMR_LN2B, SMR_NORMS, SMR_MISC = 6, 7, 8, 9, 10, 11


def _adamw_small(gsum, g_b_ada, g_ggw, params, moms, vels):
    n = len(params)

    def body(*refs):
        gsum_ref, gb_ref, gw_ref = refs[:3]
        w_refs, m_refs, v_refs = refs[3:3 + n], refs[3 + n:3 + 2 * n], refs[3 + 2 * n:3 + 3 * n]
        outs = refs[3 + 3 * n:]
        g_refs, d_refs, mo_refs, vo_refs = outs[:n - 1], outs[n - 1:2 * n - 1], outs[2 * n - 1:3 * n - 1], outs[3 * n - 1:]
        grads = [gb_ref[...],
                 gsum_ref[SMR_NORMS:SMR_NORMS + 1, 0:RET_W],
                 gsum_ref[SMR_MISC:SMR_MISC + 1, 0:GLA_KW],
                 gsum_ref[SMR_NORMS:SMR_NORMS + 1, RET_W:RET_W + GLA_VW],
                 gsum_ref[SMR_LN1W:SMR_LN1W + 1, :], gsum_ref[SMR_LN1B:SMR_LN1B + 1, :],
                 gsum_ref[SMR_LN2W:SMR_LN2W + 1, :], gsum_ref[SMR_LN2B:SMR_LN2B + 1, :],
                 gw_ref[...]]
        for i in range(n):
            delta, mn, vn = _adam_math(w_refs[i][...], grads[i], m_refs[i][...], v_refs[i][...])
            if i < n - 1:
                g_refs[i][...] = grads[i]
            d_refs[i][...] = delta
            mo_refs[i][...] = mn
            vo_refs[i][...] = vn

    vm = pl.BlockSpec(memory_space=pltpu.VMEM)
    shapes = [jax.ShapeDtypeStruct(p.shape, F32) for p in params]
    n_in = 3 + 3 * n
    out_shape = tuple(shapes[:n - 1] + shapes * 3)
    return pl.pallas_call(
        body, name="adamw_small", out_shape=out_shape,
        in_specs=[vm] * n_in, out_specs=tuple([vm] * len(out_shape)),
        compiler_params=pltpu.CompilerParams(vmem_limit_bytes=V7X_VMEM_LIMIT),
    )(gsum, g_b_ada, g_ggw, *params, *moms, *vels)


def kernel(x, c, w_ada, b_ada, w_in, ret_norm_w, gla_gate_w, gla_gate_b, gla_norm_w, w_out, ln1_w, ln1_b, w_ff1, w_ff2, ln2_w, ln2_b, loss_target, m_w_ada, m_b_ada, m_w_in, m_ret_norm_w, m_gla_gate_w, m_gla_gate_b, m_gla_norm_w, m_w_out, m_ln1_w, m_ln1_b, m_w_ff1, m_w_ff2, m_ln2_w, m_ln2_b, v_w_ada, v_b_ada, v_w_in, v_ret_norm_w, v_gla_gate_w, v_gla_gate_b, v_gla_norm_w, v_w_out, v_ln1_w, v_ln1_b, v_w_ff1, v_w_ff2, v_ln2_w, v_ln2_b):
    t = x.shape[1]
    xi, yi, ci = _my_coords()
    me = 4 * xi + 2 * yi + ci
    x2d = x[0]
    tgt = loss_target[0]

    c_ext = jnp.concatenate([c, gla_gate_w[0].reshape(1, GATE_RANK * GLA_KW // N_DEV)], axis=1)
    b_l = lax.dynamic_slice(b_ada, (0, me * ADA_COLS), (1, ADA_COLS))
    c_all3, mod_all, wi_g = _adaln_mod(c_ext, w_ada[0], b_l, w_in[0].T.astype(BF16))

    wg = _exchange_start([w_out[0].astype(BF16), w_ff1[0].astype(BF16), w_ff2[0].astype(BF16)],
                         True, "wgather_start", after=wi_g)

    c_all = c_all3[:, 0, :D_MODEL]
    gate_w = c_all3[:, 0, D_MODEL:].reshape(N_DEV, GATE_RANK, GLA_KW // N_DEV)
    gate_w = gate_w.transpose(1, 0, 2).reshape(GATE_RANK, GLA_KW)
    gw_pad = jnp.zeros((V7X_LANES, GLA_KW), F32).at[:GATE_RANK].set(gate_w)
    mod = lax.dynamic_slice(mod_all, (0, me, 0), (N_DEV, 1, ADA_COLS)).reshape(6, D_MODEL)
    shift1, scale1, gate1, shift2, scale2, gate2 = [mod[i:i + 1] for i in range(6)]

    w_in_t = wg[5].reshape(D_IN, D_MODEL)

    tables = _ret_tables(t, min(RET_SUB, t))

    sc1p = 1.0 + scale1
    proj, u = _inproj_fwd(x2d, sc1p, shift1, w_in_t, after=wg[4])
    mixed, oraw, qrb, krb, rst, sst = _mixer_fwd(proj, tables, gw_pad, gla_gate_b, ret_norm_w, gla_norm_w)
    wo_g, w1_b, w2_g = _exchange_wait(*wg[:4], mixed, True, "wgather_wait")
    w_out_b = wo_g.reshape(D_MODEL, D_MODEL)
    w2_b = w2_g.reshape(D_FF, D_MODEL)
    vec_f = jnp.concatenate([gate1, 1.0 + scale2, shift2, gate2, ln1_w, ln1_b, ln2_w, ln2_b], axis=0)
    m, x1n, rstd1, u2, a, df, dh2, acc_f = _mid_fwd(mixed, x2d, tgt, vec_f, w_out_b, w1_b, w2_b)

    vec_b = jnp.concatenate([gate1, 1.0 + scale2, ln1_w, ln1_b, jnp.zeros((4, D_MODEL), F32)], axis=0)
    da, dm, dmix, dxa, acc_b = _ffn_bwd(df, a, dh2, x1n, rstd1, m, vec_b, w_out_b, w1_b, w2_b)
    dw2 = _matmul_tn(a, df, 2048, 1024, 2048, "tn_dw2", relu_sq=True)
    dw1 = _matmul_tn(u2, da, 1024, 2048, 2048, "tn_dw1", col_slab=FF_COLS)
    dwo = _matmul_tn(mixed, dm, 1024, 1024, 2048, "tn_dwout")
    gx = _exchange_start([dwo.reshape(N_DEV, OUT_ROWS, D_MODEL), dw1, dw2.reshape(N_DEV, FF_COLS, D_MODEL)], False,
                         "gradx_start")
    dproj, dgw, dvec = _mixer_bwd(dmix, proj, qrb, krb, oraw, tables, rst, sst, gw_pad,
                                  gla_gate_b, ret_norm_w, gla_norm_w, after=gx[4])
    dwi_s = _matmul_tn(dproj, u, D_IN_PAD, 1024, 1024, "tn_dwin", out_rows=IN_COLS)
    gi = _exchange_start([dwi_s], False, "gradin_start")
    grad_x, acc_i = _inproj_bwd(dproj, x2d, dxa, sc1p, w_in_t, after=gi[4])

    loss_part = jnp.sum(acc_f[3])
    small = jnp.concatenate([
        acc_i[1:2], acc_i[0:1], acc_b[4:5], acc_b[1:2], acc_b[0:1], acc_f[2:3],
        acc_b[2:3], acc_b[3:4], acc_f[0:1], acc_f[1:2],
        jnp.concatenate([dvec[0:1], dvec[1:2]], axis=1),
        jnp.concatenate([dvec[2:3, :GLA_KW], jnp.full((1, 128), loss_part, F32),
                         jnp.zeros((1, D_MODEL - GLA_KW - 128), F32)], axis=1),
        jnp.zeros((4, D_MODEL), F32)], axis=0)
    sg = _exchange_start([small, dgw[:GATE_RANK]], True, "small_start")

    r_wo, r_w1, r_w2 = _exchange_wait(*gx[:4], sg[4], False, "gradx_wait")
    r_wi, = _exchange_wait(*gi[:4], sg[4], False, "gradin_wait")
    big = [_adamw(w[0], r, m_[0], v_[0], nm) for w, r, m_, v_, nm in (
        (w_out, r_wo, m_w_out, v_w_out, "adamw_out"),
        (w_ff1, r_w1, m_w_ff1, v_w_ff1, "adamw_ff1"), (w_ff2, r_w2, m_w_ff2, v_w_ff2, "adamw_ff2"))]
    tiles = lambda a: a.T.reshape(IN_COLS * D_MODEL // V7X_LANES, V7X_LANES)
    big_in = _adamw(tiles(w_in[0]), r_wi, tiles(m_w_in[0]), tiles(v_w_in[0]), "adamw_in", row_tiles=True)
    big = [tuple(b.reshape(IN_COLS, D_MODEL).T for b in big_in)] + big
    g_big, d_big, m_big, v_big = [[b[i][None] for b in big] for i in range(4)]

    small_all, gw_all = _exchange_wait(*sg[:4], big_in[1], True, "small_wait")
    dmod_all = small_all[:, :6].reshape(N_DEV, 6 * D_MODEL)
    dmod_cols = lax.dynamic_slice(dmod_all, (0, me * ADA_COLS), (N_DEV, ADA_COLS))
    ssum, gw_sum, g_b_ada, g_w_ada = _small_reduce(small_all, gw_all, c_all, dmod_cols)
    loss = ssum[SMR_MISC, GLA_KW]
    g_ggw = lax.dynamic_slice(gw_sum, (0, me * (GLA_KW // N_DEV)), (GATE_RANK, GLA_KW // N_DEV))[None]

    small_w = [b_ada, ret_norm_w, gla_gate_b, gla_norm_w, ln1_w, ln1_b, ln2_w, ln2_b, gla_gate_w]
    small_m = [m_b_ada, m_ret_norm_w, m_gla_gate_b, m_gla_norm_w, m_ln1_w, m_ln1_b, m_ln2_w, m_ln2_b, m_gla_gate_w]
    small_v = [v_b_ada, v_ret_norm_w, v_gla_gate_b, v_gla_norm_w, v_ln1_w, v_ln1_b, v_ln2_w, v_ln2_b, v_gla_gate_w]
    res = _adamw_small(ssum, g_b_ada, g_ggw, small_w, small_m, small_v)
    small_g = list(res[:8]) + [g_ggw]
    d_small, m_small, v_small = list(res[8:17]), list(res[17:26]), list(res[26:35])

    _, d_w_ada, nm_w_ada, nv_w_ada = _adamw(w_ada[0], g_w_ada[None], m_w_ada[0], v_w_ada[0], "adamw_ada")

    def ordered(w_ada_v, small_vals, big_vals):
        b_ada_v, rnw_v, ggb_v, gnw_v, l1w_v, l1b_v, l2w_v, l2b_v, ggw_v = small_vals
        wi_v, wo_v, w1_v, w2_v = big_vals
        return [w_ada_v, b_ada_v, wi_v, rnw_v, ggw_v, ggb_v, gnw_v, wo_v, l1w_v, l1b_v, w1_v, w2_v, l2w_v, l2b_v]

    grads = ordered(g_w_ada[None], small_g, g_big)
    deltas = ordered(d_w_ada[None], d_small, d_big)
    new_m = ordered(nm_w_ada[None], m_small, m_big)
    new_v = ordered(nv_w_ada[None], v_small, v_big)
    return (loss, grad_x[None], *grads, *deltas, *new_m, *new_v)
```
